```python
import jax, jax.numpy as jnp
from jax import lax
import numpy as np

D_MODEL = 1024
BATCH = 8
SEQ = 4096
DEPTH = 2

MEM_LEN = 256
HEAD_DIM = 64
D_CONV = 256
CONV_K = 3
N_SWA_HEADS = 8
N_SWA_KV = 2
WINDOW = 128
BLOCK = 128
N_MEM_HEADS = 4
D_SWA = N_SWA_HEADS * HEAD_DIM
D_KV = N_SWA_KV * HEAD_DIM
D_MEMQ = N_MEM_HEADS * HEAD_DIM
D_MIX = D_CONV + D_SWA + D_MEMQ
D_IN = 3 * D_CONV + D_SWA + 2 * D_KV + D_MEMQ
D_FF = 2816
EPS = 1e-6

kernel_name = "hymba_style_conv_swa_memory_macaron"


def _split_points():
    widths = [D_CONV, D_CONV, D_CONV, D_SWA, D_KV, D_KV]
    return [int(v) for v in np.cumsum(widths)]


def _rms(x):
    xf = x.astype(jnp.float32)
    return (xf * lax.rsqrt(jnp.mean(xf * xf, axis=-1, keepdims=True) + EPS)).astype(x.dtype)


def rmsnorm(x, g):
    xf = x.astype(jnp.float32)
    y = xf * lax.rsqrt(jnp.mean(xf * xf, axis=-1, keepdims=True) + EPS)
    return (y * g.astype(jnp.float32)).astype(x.dtype)


def swiglu(x, w_up, w_down):
    gate, up = jnp.split(x @ w_up, 2, axis=-1)
    return (jax.nn.silu(gate) * up) @ w_down


def alibi_slopes(n):
    return jnp.asarray([2.0 ** (-8.0 * (i + 1) / n) for i in range(n)], dtype=jnp.float32)


def short_conv_mixer(b_gate, c_gate, u, conv_w):
    seq = u.shape[1]
    v = c_gate * u
    vp = jnp.pad(v, ((0, 0), (CONV_K - 1, 0), (0, 0)))
    y = conv_w[0] * vp[:, 0:seq]
    for j in range(1, CONV_K):
        y = y + conv_w[j] * vp[:, j:j + seq]
    return b_gate * y


def sliding_window_attention(q, k, v, sinks, slopes):
    bsz, seq = q.shape[0], q.shape[1]
    nb = seq // BLOCK
    grp = N_SWA_HEADS // N_SWA_KV
    qb = q.reshape(bsz, nb, BLOCK, N_SWA_KV, grp, HEAD_DIM)

    def band(t):
        tb = t.reshape(bsz, nb, BLOCK, N_SWA_KV, HEAD_DIM)
        prev = jnp.pad(tb[:, :-1], ((0, 0), (1, 0), (0, 0), (0, 0), (0, 0)))
        return jnp.concatenate([prev, tb], axis=2)

    kb, vb = band(k), band(v)
    scores = jnp.einsum('bnqkgd,bnskd->bnkgqs', qb, kb).astype(jnp.float32) * (HEAD_DIM ** -0.5)
    qi = jnp.arange(BLOCK)[:, None]
    sj = jnp.arange(2 * BLOCK)[None, :]
    dist = qi + BLOCK - sj
    blk = jnp.arange(nb)[:, None, None]
    valid = ((dist >= 0) & (dist < WINDOW))[None] & ((blk > 0) | (sj[None] >= BLOCK))
    bias = -slopes.reshape(N_SWA_KV, grp)[:, :, None, None] * dist.astype(jnp.float32)
    scores = jnp.where(valid[None, :, None, None], scores + bias, -jnp.inf)
    sink = jnp.broadcast_to(sinks.astype(jnp.float32).reshape(N_SWA_KV, grp)[None, None, :, :, None, None],
                            scores.shape[:-1] + (1,))
    probs = jax.nn.softmax(jnp.concatenate([scores, sink], axis=-1), axis=-1)[..., :-1]
    out = jnp.einsum('bnkgqs,bnskd->bnqkgd', probs.astype(v.dtype), vb)
    return out.reshape(bsz, seq, D_SWA)


def memory_cross_attention(q, mk, mv):
    s = jnp.einsum('bqhd,bmhd->bhqm', q, mk).astype(jnp.float32) * (HEAD_DIM ** -0.5)
    p = jax.nn.softmax(s, axis=-1)
    out = jnp.einsum('bhqm,bmhd->bqhd', p.astype(mv.dtype), mv)
    return out.reshape(q.shape[0], q.shape[1], D_MEMQ)


def _fwd_setup_inputs(seed: int = 0) -> dict:
    key = jax.random.key(seed)
    ks = jax.random.split(key, 20)
    f32 = jnp.float32

    def nrm(k, shape, scale):
        return jax.random.normal(k, shape, f32) * scale

    def gain(k, shape):
        return 1.0 + 0.02 * jax.random.normal(k, shape, f32)

    return {
        "x": jax.random.normal(ks[0], (BATCH, SEQ, D_MODEL), f32),
        "mem": jax.random.normal(ks[1], (BATCH, MEM_LEN, D_MODEL), f32),
        "g_ffn1": gain(ks[2], (DEPTH, D_MODEL)),
        "w_ffn1_up": nrm(ks[3], (DEPTH, D_MODEL, 2 * D_FF), D_MODEL ** -0.5),
        "w_ffn1_down": nrm(ks[4], (DEPTH, D_FF, D_MODEL), D_FF ** -0.5),
        "g_mix": gain(ks[5], (DEPTH, D_MODEL)),
        "w_in": nrm(ks[6], (DEPTH, D_MODEL, D_IN), D_MODEL ** -0.5),
        "conv_w": nrm(ks[7], (DEPTH, CONV_K, D_CONV), CONV_K ** -0.5),
        "sinks": nrm(ks[8], (DEPTH, N_SWA_HEADS), 1.0),
        "g_mem": gain(ks[9], (DEPTH, D_MODEL)),
        "w_mem_kv": nrm(ks[10], (DEPTH, D_MODEL, 2 * D_MEMQ), D_MODEL ** -0.5),
        "g_grp": gain(ks[11], (DEPTH, D_MIX)),
        "w_out": nrm(ks[12], (DEPTH, D_MIX, D_MODEL), D_MIX ** -0.5),
        "g_ffn2": gain(ks[13], (DEPTH, D_MODEL)),
        "w_ffn2_up": nrm(ks[14], (DEPTH, D_MODEL, 2 * D_FF), D_MODEL ** -0.5),
        "w_ffn2_down": nrm(ks[15], (DEPTH, D_FF, D_MODEL), D_FF ** -0.5),
        "g_final": gain(ks[16], (D_MODEL,)),
    }


def _fwd_reference(x, mem, g_ffn1, w_ffn1_up, w_ffn1_down, g_mix, w_in, conv_w, sinks, g_mem,
              w_mem_kv, g_grp, w_out, g_ffn2, w_ffn2_up, w_ffn2_down, g_final):
    bsz, seq = x.shape[0], x.shape[1]
    mem_len = mem.shape[1]
    slopes = alibi_slopes(N_SWA_HEADS)
    cuts = _split_points()
    h = x
    for l in range(DEPTH):
        h = h + 0.5 * swiglu(rmsnorm(h, g_ffn1[l]), w_ffn1_up[l], w_ffn1_down[l])

        n = rmsnorm(h, g_mix[l])
        proj = n @ w_in[l]
        b_g, c_g, u, q, k, v, qm = jnp.split(proj, cuts, axis=-1)

        y_conv = short_conv_mixer(b_g, c_g, u, conv_w[l])

        y_swa = sliding_window_attention(
            q.reshape(bsz, seq, N_SWA_HEADS, HEAD_DIM),
            k.reshape(bsz, seq, N_SWA_KV, HEAD_DIM),
            v.reshape(bsz, seq, N_SWA_KV, HEAD_DIM),
            sinks[l], slopes)

        mkv = rmsnorm(mem, g_mem[l]) @ w_mem_kv[l]
        mk, mv = jnp.split(mkv, 2, axis=-1)
        y_mem = memory_cross_attention(
            qm.reshape(bsz, seq, N_MEM_HEADS, HEAD_DIM),
            mk.reshape(bsz, mem_len, N_MEM_HEADS, HEAD_DIM),
            mv.reshape(bsz, mem_len, N_MEM_HEADS, HEAD_DIM))

        mixed = jnp.concatenate([_rms(y_conv), _rms(y_swa), _rms(y_mem)], axis=-1) * g_grp[l]
        h = h + mixed @ w_out[l]

        h = h + 0.5 * swiglu(rmsnorm(h, g_ffn2[l]), w_ffn2_up[l], w_ffn2_down[l])
    return rmsnorm(h, g_final)


import jax as _jax
import jax.numpy as _jnp

TWIN_FORMAT = 'train_step'
FWD_PARAMS = ['x', 'mem', 'g_ffn1', 'w_ffn1_up', 'w_ffn1_down', 'g_mix', 'w_in', 'conv_w', 'sinks', 'g_mem', 'w_mem_kv', 'g_grp', 'w_out', 'g_ffn2', 'w_ffn2_up', 'w_ffn2_down', 'g_final']
TWIN_WEIGHTS = ['g_ffn1', 'w_ffn1_up', 'w_ffn1_down', 'g_mix', 'w_in', 'conv_w', 'sinks', 'g_mem', 'w_mem_kv', 'g_grp', 'w_out', 'g_ffn2', 'w_ffn2_up', 'w_ffn2_down', 'g_final']
TWIN_DIFF_INPUT = 'x'
TWIN_INPUTS = ['x', 'mem', 'g_ffn1', 'w_ffn1_up', 'w_ffn1_down', 'g_mix', 'w_in', 'conv_w', 'sinks', 'g_mem', 'w_mem_kv', 'g_grp', 'w_out', 'g_ffn2', 'w_ffn2_up', 'w_ffn2_down', 'g_final', 'loss_target', 'm_g_ffn1', 'm_w_ffn1_up', 'm_w_ffn1_down', 'm_g_mix', 'm_w_in', 'm_conv_w', 'm_sinks', 'm_g_mem', 'm_w_mem_kv', 'm_g_grp', 'm_w_out', 'm_g_ffn2', 'm_w_ffn2_up', 'm_w_ffn2_down', 'm_g_final', 'v_g_ffn1', 'v_w_ffn1_up', 'v_w_ffn1_down', 'v_g_mix', 'v_w_in', 'v_conv_w', 'v_sinks', 'v_g_mem', 'v_w_mem_kv', 'v_g_grp', 'v_w_out', 'v_g_ffn2', 'v_w_ffn2_up', 'v_w_ffn2_down', 'v_g_final']
TWIN_OUTPUTS = ['loss', 'grad_x', 'grad_g_ffn1', 'grad_w_ffn1_up', 'grad_w_ffn1_down', 'grad_g_mix', 'grad_w_in', 'grad_conv_w', 'grad_sinks', 'grad_g_mem', 'grad_w_mem_kv', 'grad_g_grp', 'grad_w_out', 'grad_g_ffn2', 'grad_w_ffn2_up', 'grad_w_ffn2_down', 'grad_g_final', 'delta_g_ffn1', 'delta_w_ffn1_up', 'delta_w_ffn1_down', 'delta_g_mix', 'delta_w_in', 'delta_conv_w', 'delta_sinks', 'delta_g_mem', 'delta_w_mem_kv', 'delta_g_grp', 'delta_w_out', 'delta_g_ffn2', 'delta_w_ffn2_up', 'delta_w_ffn2_down', 'delta_g_final', 'new_m_g_ffn1', 'new_m_w_ffn1_up', 'new_m_w_ffn1_down', 'new_m_g_mix', 'new_m_w_in', 'new_m_conv_w', 'new_m_sinks', 'new_m_g_mem', 'new_m_w_mem_kv', 'new_m_g_grp', 'new_m_w_out', 'new_m_g_ffn2', 'new_m_w_ffn2_up', 'new_m_w_ffn2_down', 'new_m_g_final', 'new_v_g_ffn1', 'new_v_w_ffn1_up', 'new_v_w_ffn1_down', 'new_v_g_mix', 'new_v_w_in', 'new_v_conv_w', 'new_v_sinks', 'new_v_g_mem', 'new_v_w_mem_kv', 'new_v_g_grp', 'new_v_w_out', 'new_v_g_ffn2', 'new_v_w_ffn2_up', 'new_v_w_ffn2_down', 'new_v_g_final']
TWIN_LEAF_KINDS = {'loss': 'loss', 'grad_x': 'grad_x', 'grad_g_ffn1': 'grad_w', 'grad_w_ffn1_up': 'grad_w', 'grad_w_ffn1_down': 'grad_w', 'grad_g_mix': 'grad_w', 'grad_w_in': 'grad_w', 'grad_conv_w': 'grad_w', 'grad_sinks': 'grad_w', 'grad_g_mem': 'grad_w', 'grad_w_mem_kv': 'grad_w', 'grad_g_grp': 'grad_w', 'grad_w_out': 'grad_w', 'grad_g_ffn2': 'grad_w', 'grad_w_ffn2_up': 'grad_w', 'grad_w_ffn2_down': 'grad_w', 'grad_g_final': 'grad_w', 'delta_g_ffn1': 'delta_w', 'delta_w_ffn1_up': 'delta_w', 'delta_w_ffn1_down': 'delta_w', 'delta_g_mix': 'delta_w', 'delta_w_in': 'delta_w', 'delta_conv_w': 'delta_w', 'delta_sinks': 'delta_w', 'delta_g_mem': 'delta_w', 'delta_w_mem_kv': 'delta_w', 'delta_g_grp': 'delta_w', 'delta_w_out': 'delta_w', 'delta_g_ffn2': 'delta_w', 'delta_w_ffn2_up': 'delta_w', 'delta_w_ffn2_down': 'delta_w', 'delta_g_final': 'delta_w', 'new_m_g_ffn1': 'new_m', 'new_m_w_ffn1_up': 'new_m', 'new_m_w_ffn1_down': 'new_m', 'new_m_g_mix': 'new_m', 'new_m_w_in': 'new_m', 'new_m_conv_w': 'new_m', 'new_m_sinks': 'new_m', 'new_m_g_mem': 'new_m', 'new_m_w_mem_kv': 'new_m', 'new_m_g_grp': 'new_m', 'new_m_w_out': 'new_m', 'new_m_g_ffn2': 'new_m', 'new_m_w_ffn2_up': 'new_m', 'new_m_w_ffn2_down': 'new_m', 'new_m_g_final': 'new_m', 'new_v_g_ffn1': 'new_v', 'new_v_w_ffn1_up': 'new_v', 'new_v_w_ffn1_down': 'new_v', 'new_v_g_mix': 'new_v', 'new_v_w_in': 'new_v', 'new_v_conv_w': 'new_v', 'new_v_sinks': 'new_v', 'new_v_g_mem': 'new_v', 'new_v_w_mem_kv': 'new_v', 'new_v_g_grp': 'new_v', 'new_v_w_out': 'new_v', 'new_v_g_ffn2': 'new_v', 'new_v_w_ffn2_up': 'new_v', 'new_v_w_ffn2_down': 'new_v', 'new_v_g_final': 'new_v'}


def _forward(args):
    return _fwd_reference(*[args[k] for k in FWD_PARAMS])


def _output_shape():
    out = _jax.eval_shape(lambda: _forward(_fwd_setup_inputs(0)))
    return out.shape, out.dtype

N_MICROBATCH = 1
ADAM_LR = 0.001
ADAM_B1 = 0.9
ADAM_B2 = 0.999
ADAM_EPS = 1e-08
ADAM_WD = 0.01
ADAM_STEP = 10
PER_EXAMPLE_BATCH_AXIS = {'x': 0, 'mem': 0, 'loss_target': 0}
SHARED_INPUTS = []
_WEIGHT_DTYPES = {'g_ffn1': _jnp.float32, 'w_ffn1_up': _jnp.float32, 'w_ffn1_down': _jnp.float32, 'g_mix': _jnp.float32, 'w_in': _jnp.float32, 'conv_w': _jnp.float32, 'sinks': _jnp.float32, 'g_mem': _jnp.float32, 'w_mem_kv': _jnp.float32, 'g_grp': _jnp.float32, 'w_out': _jnp.float32, 'g_ffn2': _jnp.float32, 'w_ffn2_up': _jnp.float32, 'w_ffn2_down': _jnp.float32, 'g_final': _jnp.float32}
MOMENT_SCALE = {'g_ffn1': 8.791676e-02, 'w_ffn1_up': 3.612003e-02, 'w_ffn1_down': 5.902572e-02, 'g_mix': 1.840328e-01, 'w_in': 1.333798e-01, 'conv_w': 1.284639e-01, 'sinks': 1.171809e-01, 'g_mem': 9.391498e-02, 'w_mem_kv': 1.304955e-01, 'g_grp': 1.285047e-01, 'w_out': 1.281478e-01, 'g_ffn2': 5.174139e-02, 'w_ffn2_up': 2.171789e-02, 'w_ffn2_down': 3.546072e-02, 'g_final': 3.203518e+01}


def _to_microbatches(a, axis):
    t = _jnp.moveaxis(a, axis, 0)
    t = t.reshape((N_MICROBATCH, t.shape[0] // N_MICROBATCH) + t.shape[1:])
    return _jnp.moveaxis(t, 1, axis + 1)


def setup_inputs(seed: int = 0) -> dict:
    inp = _fwd_setup_inputs(seed)
    key = _jax.random.fold_in(_jax.random.key(seed), 7919)
    shape, _ = _output_shape()
    out = dict(inp)
    out["loss_target"] = _jax.random.normal(_jax.random.fold_in(key, 0), shape, _jnp.float32)
    for i, name in enumerate(TWIN_WEIGHTS):
        w = inp[name].astype(_jnp.float32)
        if MOMENT_SCALE is None:
            s = _jnp.sqrt(_jnp.mean(_jnp.square(w)) + 1e-30)
        else:
            s = MOMENT_SCALE[name]
        km, kv = _jax.random.split(_jax.random.fold_in(key, i + 1))
        out[name] = w
        out["m_" + name] = s * _jax.random.normal(km, w.shape, _jnp.float32)
        out["v_" + name] = (s * s) * _jax.random.uniform(kv, w.shape, _jnp.float32, 0.5, 1.5)
    if N_MICROBATCH > 1:
        for name, axis in PER_EXAMPLE_BATCH_AXIS.items():
            out[name] = _to_microbatches(out[name], axis)
    return {'x': out['x'], 'mem': out['mem'], 'g_ffn1': out['g_ffn1'], 'w_ffn1_up': out['w_ffn1_up'], 'w_ffn1_down': out['w_ffn1_down'], 'g_mix': out['g_mix'], 'w_in': out['w_in'], 'conv_w': out['conv_w'], 'sinks': out['sinks'], 'g_mem': out['g_mem'], 'w_mem_kv': out['w_mem_kv'], 'g_grp': out['g_grp'], 'w_out': out['w_out'], 'g_ffn2': out['g_ffn2'], 'w_ffn2_up': out['w_ffn2_up'], 'w_ffn2_down': out['w_ffn2_down'], 'g_final': out['g_final'], 'loss_target': out['loss_target'], 'm_g_ffn1': out['m_g_ffn1'], 'm_w_ffn1_up': out['m_w_ffn1_up'], 'm_w_ffn1_down': out['m_w_ffn1_down'], 'm_g_mix': out['m_g_mix'], 'm_w_in': out['m_w_in'], 'm_conv_w': out['m_conv_w'], 'm_sinks': out['m_sinks'], 'm_g_mem': out['m_g_mem'], 'm_w_mem_kv': out['m_w_mem_kv'], 'm_g_grp': out['m_g_grp'], 'm_w_out': out['m_w_out'], 'm_g_ffn2': out['m_g_ffn2'], 'm_w_ffn2_up': out['m_w_ffn2_up'], 'm_w_ffn2_down': out['m_w_ffn2_down'], 'm_g_final': out['m_g_final'], 'v_g_ffn1': out['v_g_ffn1'], 'v_w_ffn1_up': out['v_w_ffn1_up'], 'v_w_ffn1_down': out['v_w_ffn1_down'], 'v_g_mix': out['v_g_mix'], 'v_w_in': out['v_w_in'], 'v_conv_w': out['v_conv_w'], 'v_sinks': out['v_sinks'], 'v_g_mem': out['v_g_mem'], 'v_w_mem_kv': out['v_w_mem_kv'], 'v_g_grp': out['v_g_grp'], 'v_w_out': out['v_w_out'], 'v_g_ffn2': out['v_g_ffn2'], 'v_w_ffn2_up': out['v_w_ffn2_up'], 'v_w_ffn2_down': out['v_w_ffn2_down'], 'v_g_final': out['v_g_final']}


def _loss(weights, diff, rest, loss_target):
    with _jax.named_scope("forward"):
        args = {**rest, TWIN_DIFF_INPUT: diff, **{k: w.astype(_WEIGHT_DTYPES[k]) for k, w in weights.items()}}
        y = _forward(args)
    with _jax.named_scope("loss_head"):
        err = _jnp.square(y.astype(_jnp.float32) - loss_target)
        return 0.5 * _jnp.sum(_jnp.mean(err, axis=-1)) if err.ndim else 0.5 * err


def _adamw(w, g, m, v):
    m = ADAM_B1 * m + (1.0 - ADAM_B1) * g
    v = ADAM_B2 * v + (1.0 - ADAM_B2) * _jnp.square(g)
    m_hat = m / (1.0 - ADAM_B1 ** ADAM_STEP)
    v_hat = v / (1.0 - ADAM_B2 ** ADAM_STEP)
    delta = -ADAM_LR * (m_hat / (_jnp.sqrt(v_hat) + ADAM_EPS) + ADAM_WD * w)
    return delta, m, v


def reference(x, mem, g_ffn1, w_ffn1_up, w_ffn1_down, g_mix, w_in, conv_w, sinks, g_mem, w_mem_kv, g_grp, w_out, g_ffn2, w_ffn2_up, w_ffn2_down, g_final, loss_target, m_g_ffn1, m_w_ffn1_up, m_w_ffn1_down, m_g_mix, m_w_in, m_conv_w, m_sinks, m_g_mem, m_w_mem_kv, m_g_grp, m_w_out, m_g_ffn2, m_w_ffn2_up, m_w_ffn2_down, m_g_final, v_g_ffn1, v_w_ffn1_up, v_w_ffn1_down, v_g_mix, v_w_in, v_conv_w, v_sinks, v_g_mem, v_w_mem_kv, v_g_grp, v_w_out, v_g_ffn2, v_w_ffn2_up, v_w_ffn2_down, v_g_final):
    given = dict(x=x, mem=mem, g_ffn1=g_ffn1, w_ffn1_up=w_ffn1_up, w_ffn1_down=w_ffn1_down, g_mix=g_mix, w_in=w_in, conv_w=conv_w, sinks=sinks, g_mem=g_mem, w_mem_kv=w_mem_kv, g_grp=g_grp, w_out=w_out, g_ffn2=g_ffn2, w_ffn2_up=w_ffn2_up, w_ffn2_down=w_ffn2_down, g_final=g_final, loss_target=loss_target, m_g_ffn1=m_g_ffn1, m_w_ffn1_up=m_w_ffn1_up, m_w_ffn1_down=m_w_ffn1_down, m_g_mix=m_g_mix, m_w_in=m_w_in, m_conv_w=m_conv_w, m_sinks=m_sinks, m_g_mem=m_g_mem, m_w_mem_kv=m_w_mem_kv, m_g_grp=m_g_grp, m_w_out=m_w_out, m_g_ffn2=m_g_ffn2, m_w_ffn2_up=m_w_ffn2_up, m_w_ffn2_down=m_w_ffn2_down, m_g_final=m_g_final, v_g_ffn1=v_g_ffn1, v_w_ffn1_up=v_w_ffn1_up, v_w_ffn1_down=v_w_ffn1_down, v_g_mix=v_g_mix, v_w_in=v_w_in, v_conv_w=v_conv_w, v_sinks=v_sinks, v_g_mem=v_g_mem, v_w_mem_kv=v_w_mem_kv, v_g_grp=v_g_grp, v_w_out=v_w_out, v_g_ffn2=v_g_ffn2, v_w_ffn2_up=v_w_ffn2_up, v_w_ffn2_down=v_w_ffn2_down, v_g_final=v_g_final)
    weights = {n: given[n] for n in TWIN_WEIGHTS}
    shared = {n: given[n] for n in SHARED_INPUTS}
    per_example = {n: given[n] for n in ['x', 'mem']}
    grad_fn = _jax.value_and_grad(_loss, argnums=(0, 1))

    def one_microbatch(ex, loss_target):
        ex = dict(ex)
        diff = ex.pop(TWIN_DIFF_INPUT)
        return grad_fn(weights, diff, {**shared, **ex}, loss_target)

    if N_MICROBATCH == 1:
        loss, (grad_w, grad_x) = one_microbatch(per_example, given["loss_target"])
    else:
        def body(carry, xs):
            loss_sum, grad_sum = carry
            l_k, (gw_k, gx_k) = one_microbatch(xs[0], xs[1])
            with _jax.named_scope("update"):
                return (loss_sum + l_k, _jax.tree.map(_jnp.add, grad_sum, gw_k)), gx_k

        init = (_jnp.zeros((), _jnp.float32), _jax.tree.map(_jnp.zeros_like, weights))
        (loss, grad_w), grad_x = _jax.lax.scan(body, init, (per_example, given["loss_target"]))
    with _jax.named_scope("update"):
        delta_w, new_m, new_v = {}, {}, {}
        for n in TWIN_WEIGHTS:
            delta_w[n], new_m[n], new_v[n] = _adamw(weights[n], grad_w[n], given["m_" + n], given["v_" + n])
    return (loss, grad_x, *[grad_w[n] for n in TWIN_WEIGHTS], *[delta_w[n] for n in TWIN_WEIGHTS],
            *[new_m[n] for n in TWIN_WEIGHTS], *[new_v[n] for n in TWIN_WEIGHTS])
```

```python
import functools

import jax
import jax.numpy as jnp
from jax import lax
from jax.experimental import pallas as pl
from jax.experimental.pallas import tpu as pltpu

F32 = jnp.float32
BF16 = jnp.bfloat16

N_DEV = 8
EPS = 1e-6
N_SWA_HEADS = 8
N_SWA_KV = 2
SWA_GROUP = N_SWA_HEADS // N_SWA_KV
HEAD_DIM = 64
N_MEM_HEADS = 4
D_CONV = 256
BLOCK = 128
D_SWA = N_SWA_HEADS * HEAD_DIM
D_KV = N_SWA_KV * HEAD_DIM
D_MEMQ = N_MEM_HEADS * HEAD_DIM
D_MIX = D_CONV + D_SWA + D_MEMQ
D_IN = 3 * D_CONV + D_SWA + 2 * D_KV + D_MEMQ
COL_BG, COL_CG, COL_U = 0, D_CONV, 2 * D_CONV
COL_Q = 3 * D_CONV
COL_K = COL_Q + D_SWA
COL_V = COL_K + D_KV
COL_QM = COL_V + D_KV
MIX_GROUPS = ((0, D_CONV), (D_CONV, D_CONV + D_SWA), (D_CONV + D_SWA, D_MIX))
SLOPES = tuple(2.0 ** (-8.0 * (i + 1) / N_SWA_HEADS) for i in range(N_SWA_HEADS))
SCALE = HEAD_DIM ** -0.5
NEG = -1e30

ADAM_LR = 0.001
ADAM_B1 = 0.9
ADAM_B2 = 0.999
ADAM_EPS = 1e-08
ADAM_WD = 0.01
ADAM_STEP = 10

V7X_VMEM_BYTES = 64 * 1024 * 1024
VMEM_LIMIT = (V7X_VMEM_BYTES * 3) // 4
MESH = pl.DeviceIdType.MESH


def _pcall(body, **kw):
    return pl.pallas_call(body, **kw)


def _params(sem=None, vmem=VMEM_LIMIT):
    return pltpu.CompilerParams(dimension_semantics=sem, vmem_limit_bytes=vmem)


def _dot(a, b):
    return lax.dot_general(a, b, (((1,), (0,)), ((), ())), preferred_element_type=F32)


def _dot_nt(a, b):
    return lax.dot_general(a, b, (((1,), (1,)), ((), ())), preferred_element_type=F32)


def _dot_tn(a, b):
    return lax.dot_general(a, b, (((0,), (0,)), ((), ())), preferred_element_type=F32)


def _rstd(x):
    return lax.rsqrt(jnp.mean(x * x, axis=-1, keepdims=True) + EPS)


def _sigmoid(x):
    return 1.0 / (1.0 + jnp.exp(-x))


def _sum8(x):
    r, w = x.shape
    return jnp.sum(x.reshape(r // 8, 8, w), axis=0)


def _tok_block(t):
    return min(512, t)


def _ffn_fwd(h, g, wup, wdn):
    t, d = h.shape
    _, nj, _, tf = wup.shape
    tm = _tok_block(t)
    ni = t // tm

    def body(h_ref, g_ref, wup_ref, wdn_ref, ho_ref, gu_ref, n_ref, nt_ref, acc_ref):
        j = pl.program_id(1)

        @pl.when(j == 0)
        def _():
            hh = h_ref[...]
            n = hh * _rstd(hh) * g_ref[...]
            n_ref[...] = n.astype(BF16)
            nt_ref[...] = n.T.astype(BF16)
            acc_ref[...] = jnp.zeros_like(acc_ref)

        n = n_ref[...]
        gate = _dot(n, wup_ref[0, 0])
        up = _dot(n, wup_ref[1, 0])
        gu_ref[0, 0] = gate.astype(BF16)
        gu_ref[1, 0] = up.astype(BF16)
        a = gate * _sigmoid(gate) * up
        acc_ref[...] += _dot(a.astype(BF16), wdn_ref[...])

        @pl.when(j == nj - 1)
        def _():
            ho_ref[...] = h_ref[...] + 0.5 * acc_ref[...]

    return _pcall(
        body, name="ffn_fwd", grid=(ni, nj),
        in_specs=[pl.BlockSpec((tm, d), lambda i, j: (i, 0)),
                  pl.BlockSpec((1, d), lambda i, j: (0, 0)),
                  pl.BlockSpec((2, 1, d, tf), lambda i, j: (0, j, 0, 0)),
                  pl.BlockSpec((tf, d), lambda i, j: (j, 0))],
        out_specs=[pl.BlockSpec((tm, d), lambda i, j: (i, 0)),
                   pl.BlockSpec((2, 1, tm, tf), lambda i, j: (0, j, i, 0)),
                   pl.BlockSpec((tm, d), lambda i, j: (i, 0)),
                   pl.BlockSpec((d, tm), lambda i, j: (0, i))],
        out_shape=[jax.ShapeDtypeStruct((t, d), F32),
                   jax.ShapeDtypeStruct((2, nj, t, tf), BF16),
                   jax.ShapeDtypeStruct((t, d), BF16),
                   jax.ShapeDtypeStruct((d, t), BF16)],
        scratch_shapes=[pltpu.VMEM((tm, d), F32)],
        compiler_params=_params(("parallel", "arbitrary")),
    )(h, g, wup, wdn)


def _ffn_bwd_act(dho, h, g, gu, wup, wdn):
    t, d = h.shape
    _, nj, _, tf = wup.shape
    tm = _tok_block(t)
    ni = t // tm

    def body(dho_ref, h_ref, g_ref, gu_ref, wup_ref, wdn_ref, dh_ref, agu_ref, dyb_ref, dg_ref, acc_ref):
        i = pl.program_id(0)
        j = pl.program_id(1)

        @pl.when(j == 0)
        def _():
            dyb_ref[...] = (0.5 * dho_ref[...]).astype(BF16)

        dy = dyb_ref[...]
        da = _dot_nt(dy, wdn_ref[...])
        gate = gu_ref[0, 0].astype(F32)
        up = gu_ref[1, 0].astype(F32)
        sg = _sigmoid(gate)
        silu = gate * sg
        dgate = (da * up * (sg * (1.0 + gate * (1.0 - sg)))).astype(BF16)
        dup = (da * silu).astype(BF16)
        agu_ref[0, 0] = (silu * up).astype(BF16)
        agu_ref[1, 0] = dgate
        agu_ref[2, 0] = dup
        dn = _dot_nt(dgate, wup_ref[0, 0]) + _dot_nt(dup, wup_ref[1, 0])

        @pl.when(j == 0)
        def _():
            acc_ref[...] = dn

        @pl.when(j > 0)
        def _():
            acc_ref[...] += dn

        @pl.when(j == nj - 1)
        def _():
            hh = h_ref[...]
            r = _rstd(hh)
            xhat = hh * r
            dnf = acc_ref[...]
            dxh = dnf * g_ref[...]
            dh_ref[...] = dho_ref[...] + r * (dxh - xhat * jnp.mean(dxh * xhat, axis=-1, keepdims=True))
            part = _sum8(dnf * xhat)

            @pl.when(i == 0)
            def _():
                dg_ref[...] = part

            @pl.when(i > 0)
            def _():
                dg_ref[...] += part

    return _pcall(
        body, name="ffn_bwd_act", grid=(ni, nj),
        in_specs=[pl.BlockSpec((tm, d), lambda i, j: (i, 0)),
                  pl.BlockSpec((tm, d), lambda i, j: (i, 0)),
                  pl.BlockSpec((1, d), lambda i, j: (0, 0)),
                  pl.BlockSpec((2, 1, tm, tf), lambda i, j: (0, j, i, 0)),
                  pl.BlockSpec((2, 1, d, tf), lambda i, j: (0, j, 0, 0)),
                  pl.BlockSpec((tf, d), lambda i, j: (j, 0))],
        out_specs=[pl.BlockSpec((tm, d), lambda i, j: (i, 0)),
                   pl.BlockSpec((3, 1, tm, tf), lambda i, j: (0, j, i, 0)),
                   pl.BlockSpec((tm, d), lambda i, j: (i, 0)),
                   pl.BlockSpec((8, d), lambda i, j: (0, 0))],
        out_shape=[jax.ShapeDtypeStruct((t, d), F32),
                   jax.ShapeDtypeStruct((3, nj, t, tf), BF16),
                   jax.ShapeDtypeStruct((t, d), BF16),
                   jax.ShapeDtypeStruct((8, d), F32)],
        scratch_shapes=[pltpu.VMEM((tm, d), F32)],
        compiler_params=_params(("arbitrary", "arbitrary")),
    )(dho, h, g, gu, wup, wdn)


def _ffn_bwd_w(agu, dyb, nt):
    _, nj, t, tf = agu.shape
    d = dyb.shape[1]
    tm = _tok_block(t)
    ni = t // tm

    def body(agu_ref, dyb_ref, nt_ref, dwup_ref, dwdn_ref, accg_ref, accu_ref, accd_ref):
        i = pl.program_id(1)
        ntb = nt_ref[...]
        pg = _dot(ntb, agu_ref[1, 0])
        pu = _dot(ntb, agu_ref[2, 0])
        pd = _dot_tn(agu_ref[0, 0], dyb_ref[...])

        @pl.when(i == 0)
        def _():
            accg_ref[...] = pg
            accu_ref[...] = pu
            accd_ref[...] = pd

        @pl.when(i > 0)
        def _():
            accg_ref[...] += pg
            accu_ref[...] += pu
            accd_ref[...] += pd

        @pl.when(i == ni - 1)
        def _():
            dwup_ref[0, 0] = accg_ref[...].astype(BF16)
            dwup_ref[1, 0] = accu_ref[...].astype(BF16)
            dwdn_ref[...] = accd_ref[...].astype(BF16)

    return _pcall(
        body, name="ffn_bwd_w", grid=(nj, ni),
        in_specs=[pl.BlockSpec((3, 1, tm, tf), lambda j, i: (0, j, i, 0)),
                  pl.BlockSpec((tm, d), lambda j, i: (i, 0)),
                  pl.BlockSpec((d, tm), lambda j, i: (0, i))],
        out_specs=[pl.BlockSpec((2, 1, d, tf), lambda j, i: (0, j, 0, 0)),
                   pl.BlockSpec((tf, d), lambda j, i: (j, 0))],
        out_shape=[jax.ShapeDtypeStruct((2, nj, d, tf), BF16),
                   jax.ShapeDtypeStruct((nj * tf, d), BF16)],
        scratch_shapes=[pltpu.VMEM((d, tf), F32), pltpu.VMEM((d, tf), F32), pltpu.VMEM((tf, d), F32)],
        compiler_params=_params(("parallel", "arbitrary")),
    )(agu, dyb, nt)


def _mix_proj_fwd(h, g, win):
    t, d = h.shape
    tm = _tok_block(t)

    def body(h_ref, g_ref, win_ref, p_ref, nt_ref):
        hh = h_ref[...]
        n = hh * _rstd(hh) * g_ref[...]
        nt_ref[...] = n.T.astype(BF16)
        p_ref[...] = _dot(n.astype(BF16), win_ref[...]).astype(BF16)

    return _pcall(
        body, name="mix_proj_fwd", grid=(t // tm,),
        in_specs=[pl.BlockSpec((tm, d), lambda i: (i, 0)),
                  pl.BlockSpec((1, d), lambda i: (0, 0)),
                  pl.BlockSpec((d, D_IN), lambda i: (0, 0))],
        out_specs=[pl.BlockSpec((tm, D_IN), lambda i: (i, 0)),
                   pl.BlockSpec((d, tm), lambda i: (0, i))],
        out_shape=[jax.ShapeDtypeStruct((t, D_IN), BF16), jax.ShapeDtypeStruct((d, t), BF16)],
        compiler_params=_params(("parallel",)),
    )(h, g, win)


def _memkv_fwd(mem, g, wkv):
    m, d = mem.shape

    def body(mem_ref, g_ref, w_ref, mkv_ref, nt_ref):
        mm = mem_ref[...]
        n = mm * _rstd(mm) * g_ref[...]
        nt_ref[...] = n.T.astype(BF16)
        mkv_ref[...] = _dot(n.astype(BF16), w_ref[...]).astype(BF16)

    return _pcall(
        body, name="memkv_fwd", grid=(1,),
        in_specs=[pl.BlockSpec((m, d), lambda i: (0, 0)),
                  pl.BlockSpec((1, d), lambda i: (0, 0)),
                  pl.BlockSpec((d, 2 * D_MEMQ), lambda i: (0, 0))],
        out_specs=[pl.BlockSpec((m, 2 * D_MEMQ), lambda i: (0, 0)),
                   pl.BlockSpec((d, m), lambda i: (0, 0))],
        out_shape=[jax.ShapeDtypeStruct((m, 2 * D_MEMQ), BF16), jax.ShapeDtypeStruct((d, m), BF16)],
        compiler_params=_params(("arbitrary",)),
    )(mem, g, wkv)


def _memkv_bwd(dmkv, mem, g, wkv, nt):
    m, d = mem.shape

    def body(dmkv_ref, mem_ref, g_ref, w_ref, nt_ref, dw_ref, dg_ref):
        db = dmkv_ref[...].astype(BF16)
        dw_ref[...] = _dot(nt_ref[...], db).astype(BF16)
        dn = _dot_nt(db, w_ref[...])
        mm = mem_ref[...]
        dg_ref[...] = _sum8(dn * (mm * _rstd(mm)))

    return _pcall(
        body, name="memkv_bwd", grid=(1,),
        in_specs=[pl.BlockSpec((m, 2 * D_MEMQ), lambda i: (0, 0)),
                  pl.BlockSpec((m, d), lambda i: (0, 0)),
                  pl.BlockSpec((1, d), lambda i: (0, 0)),
                  pl.BlockSpec((d, 2 * D_MEMQ), lambda i: (0, 0)),
                  pl.BlockSpec((d, m), lambda i: (0, 0))],
        out_specs=[pl.BlockSpec((d, 2 * D_MEMQ), lambda i: (0, 0)),
                   pl.BlockSpec((8, d), lambda i: (0, 0))],
        out_shape=[jax.ShapeDtypeStruct((d, 2 * D_MEMQ), BF16), jax.ShapeDtypeStruct((8, d), F32)],
        compiler_params=_params(("arbitrary",)),
    )(dmkv, mem, g, wkv, nt)


def _shift_rows(v, k, edge_rows, row):
    out = pltpu.roll(v, k, 0)
    for r in range(k):
        out = jnp.where(row == r, edge_rows[r], out)
    return out


def _shift_rows_up(v, k, edge_rows, row):
    n = v.shape[0]
    out = pltpu.roll(v, n - k, 0)
    for r in range(k):
        out = jnp.where(row == n - k + r, edge_rows[r], out)
    return out


def _swa_scores(q, k, slope, dist, mask):
    s = _dot_nt(q, k) * SCALE - slope * dist
    return jnp.where(mask, s, NEG)


def _band_geometry():
    ti = lax.broadcasted_iota(jnp.int32, (BLOCK, BLOCK), 0)
    si = lax.broadcasted_iota(jnp.int32, (BLOCK, BLOCK), 1)
    dist_cur = (ti - si).astype(F32)
    return dist_cur, dist_cur + float(BLOCK), ti >= si, si > ti


def _mix_core_fwd(p, mkv, convw, sinks):
    t = p.shape[0]
    m = mkv.shape[0]
    nb = t // BLOCK

    def body(sk_ref, pc_ref, pkv_ref, ppc_ref, ppu_ref, mkv_ref, cw_ref, y_ref, l_ref):
        i = pl.program_id(0)
        has_prev = i > 0
        prevf = has_prev.astype(F32)
        row = lax.broadcasted_iota(jnp.int32, (BLOCK, D_CONV), 0)

        bg = pc_ref[:, COL_BG:COL_BG + D_CONV].astype(F32)
        cg = pc_ref[:, COL_CG:COL_CG + D_CONV].astype(F32)
        u = pc_ref[:, COL_U:COL_U + D_CONV].astype(F32)
        vv = cg * u
        pvv = ppc_ref[...].astype(F32) * ppu_ref[...].astype(F32) * prevf
        vv1 = _shift_rows(vv, 1, [pvv[15:16]], row)
        vv2 = _shift_rows(vv, 2, [pvv[14:15], pvv[15:16]], row)
        w = cw_ref[...]
        y_ref[:, 0:D_CONV] = bg * (w[0:1] * vv2 + w[1:2] * vv1 + w[2:3] * vv)

        dist_cur, dist_prev, mask_cur, mask_up = _band_geometry()
        mask_prev = jnp.logical_and(mask_up, has_prev)
        lane = lax.broadcasted_iota(jnp.int32, (BLOCK, 128), 1)
        lse_all = jnp.zeros((BLOCK, 128), F32)
        for kv in range(N_SWA_KV):
            kc = pc_ref[:, COL_K + HEAD_DIM * kv:COL_K + HEAD_DIM * (kv + 1)]
            vc = pc_ref[:, COL_V + HEAD_DIM * kv:COL_V + HEAD_DIM * (kv + 1)]
            kp = pkv_ref[:, HEAD_DIM * kv:HEAD_DIM * (kv + 1)]
            vp = pkv_ref[:, D_KV + HEAD_DIM * kv:D_KV + HEAD_DIM * (kv + 1)]
            for gi in range(SWA_GROUP):
                hd = kv * SWA_GROUP + gi
                q = pc_ref[:, COL_Q + HEAD_DIM * hd:COL_Q + HEAD_DIM * (hd + 1)]
                sc = _swa_scores(q, kc, SLOPES[hd], dist_cur, mask_cur)
                sp = _swa_scores(q, kp, SLOPES[hd], dist_prev, mask_prev)
                sink = sk_ref[0, hd]
                mx = jnp.maximum(jnp.maximum(jnp.max(sc, axis=-1, keepdims=True),
                                             jnp.max(sp, axis=-1, keepdims=True)), sink)
                ec = jnp.exp(sc - mx)
                ep = jnp.exp(sp - mx)
                den = (jnp.sum(ec, axis=-1, keepdims=True) + jnp.sum(ep, axis=-1, keepdims=True)
                       + jnp.exp(sink - mx))
                o = _dot(ec.astype(BF16), vc) + _dot(ep.astype(BF16), vp)
                c0 = D_CONV + HEAD_DIM * hd
                y_ref[:, c0:c0 + HEAD_DIM] = o / den
                lse_all = jnp.where(lane == hd, mx + jnp.log(den), lse_all)

        for hm in range(N_MEM_HEADS):
            qm = pc_ref[:, COL_QM + HEAD_DIM * hm:COL_QM + HEAD_DIM * (hm + 1)]
            mk = mkv_ref[:, HEAD_DIM * hm:HEAD_DIM * (hm + 1)]
            mv = mkv_ref[:, D_MEMQ + HEAD_DIM * hm:D_MEMQ + HEAD_DIM * (hm + 1)]
            s = _dot_nt(qm, mk) * SCALE
            mx = jnp.max(s, axis=-1, keepdims=True)
            e = jnp.exp(s - mx)
            den = jnp.sum(e, axis=-1, keepdims=True)
            c0 = D_CONV + D_SWA + HEAD_DIM * hm
            y_ref[:, c0:c0 + HEAD_DIM] = _dot(e.astype(BF16), mv) / den
            lse_all = jnp.where(lane == N_SWA_HEADS + hm, mx + jnp.log(den), lse_all)
        l_ref[...] = lse_all

    kv_col = COL_K // (2 * D_KV)
    return _pcall(
        body, name="mix_core_fwd", grid=(nb,),
        in_specs=[pl.BlockSpec(memory_space=pltpu.SMEM),
                  pl.BlockSpec((BLOCK, D_IN), lambda i: (i, 0)),
                  pl.BlockSpec((BLOCK, 2 * D_KV), lambda i: (jnp.maximum(i - 1, 0), kv_col)),
                  pl.BlockSpec((16, D_CONV), lambda i: (jnp.maximum(i * (BLOCK // 16) - 1, 0), COL_CG // D_CONV)),
                  pl.BlockSpec((16, D_CONV), lambda i: (jnp.maximum(i * (BLOCK // 16) - 1, 0), COL_U // D_CONV)),
                  pl.BlockSpec((m, 2 * D_MEMQ), lambda i: (0, 0)),
                  pl.BlockSpec((3, D_CONV), lambda i: (0, 0))],
        out_specs=[pl.BlockSpec((BLOCK, D_MIX), lambda i: (i, 0)),
                   pl.BlockSpec((BLOCK, 128), lambda i: (i, 0))],
        out_shape=[jax.ShapeDtypeStruct((t, D_MIX), F32), jax.ShapeDtypeStruct((t, 128), F32)],
        compiler_params=_params(("parallel",)),
    )(sinks, p, p, p, p, mkv, convw)


def _mix_core_bwd(p, dy, y, lse, mkv, convw, sinks):
    t = p.shape[0]
    m = mkv.shape[0]
    nb = t // BLOCK

    def body(sk_ref, pc_ref, pp_ref, pn_ref, dyc_ref, dyn_ref, yc_ref, yn_ref, lc_ref, ln_ref, mkv_ref, cw_ref,
             dp_ref, dmkv_ref, dcw_ref, dsk_ref):
        i = pl.program_id(0)
        has_prev = i > 0
        has_next = i < nb - 1
        prevf = has_prev.astype(F32)
        nextf = has_next.astype(F32)
        row = lax.broadcasted_iota(jnp.int32, (BLOCK, D_CONV), 0)

        @pl.when(i == 0)
        def _():
            dmkv_ref[...] = jnp.zeros_like(dmkv_ref)
            dcw_ref[...] = jnp.zeros_like(dcw_ref)
            dsk_ref[...] = jnp.zeros_like(dsk_ref)

        bg = pc_ref[:, COL_BG:COL_BG + D_CONV].astype(F32)
        cg = pc_ref[:, COL_CG:COL_CG + D_CONV].astype(F32)
        u = pc_ref[:, COL_U:COL_U + D_CONV].astype(F32)
        vv = cg * u
        pvv = (pp_ref[BLOCK - 16:BLOCK, COL_CG:COL_CG + D_CONV].astype(F32)
               * pp_ref[BLOCK - 16:BLOCK, COL_U:COL_U + D_CONV].astype(F32) * prevf)
        vv1 = _shift_rows(vv, 1, [pvv[15:16]], row)
        vv2 = _shift_rows(vv, 2, [pvv[14:15], pvv[15:16]], row)
        w = cw_ref[...]
        yconv = w[0:1] * vv2 + w[1:2] * vv1 + w[2:3] * vv
        dyo = dyc_ref[:, 0:D_CONV]
        dyc = dyo * bg
        nxt = dyn_ref[0:16, 0:D_CONV] * pn_ref[0:16, COL_BG:COL_BG + D_CONV].astype(F32) * nextf
        d1 = _shift_rows_up(dyc, 1, [nxt[0:1]], row)
        d2 = _shift_rows_up(dyc, 2, [nxt[0:1], nxt[1:2]], row)
        dvv = w[2:3] * dyc + w[1:2] * d1 + w[0:1] * d2
        dp_ref[:, COL_BG:COL_BG + D_CONV] = (dyo * yconv).astype(BF16)
        dp_ref[:, COL_CG:COL_CG + D_CONV] = (dvv * u).astype(BF16)
        dp_ref[:, COL_U:COL_U + D_CONV] = (dvv * cg).astype(BF16)
        dcw_ref[0:1, :] += jnp.sum(dyc * vv2, axis=0, keepdims=True)
        dcw_ref[1:2, :] += jnp.sum(dyc * vv1, axis=0, keepdims=True)
        dcw_ref[2:3, :] += jnp.sum(dyc * vv, axis=0, keepdims=True)

        dist_cur, dist_prev, mask_cur, mask_up = _band_geometry()
        mask_prev = jnp.logical_and(mask_up, has_prev)
        mask_next = jnp.logical_and(mask_up, has_next)
        lane8 = jnp.where(lax.broadcasted_iota(jnp.int32, (8, 128), 0) == 0,
                          lax.broadcasted_iota(jnp.int32, (8, 128), 1), -1)
        dsk = jnp.zeros((8, 128), F32)
        for kv in range(N_SWA_KV):
            kc = pc_ref[:, COL_K + HEAD_DIM * kv:COL_K + HEAD_DIM * (kv + 1)]
            vc = pc_ref[:, COL_V + HEAD_DIM * kv:COL_V + HEAD_DIM * (kv + 1)]
            kp = pp_ref[:, COL_K + HEAD_DIM * kv:COL_K + HEAD_DIM * (kv + 1)]
            vp = pp_ref[:, COL_V + HEAD_DIM * kv:COL_V + HEAD_DIM * (kv + 1)]
            dk = jnp.zeros((BLOCK, HEAD_DIM), F32)
            dv = jnp.zeros((BLOCK, HEAD_DIM), F32)
            for gi in range(SWA_GROUP):
                hd = kv * SWA_GROUP + gi
                cq = COL_Q + HEAD_DIM * hd
                cy = D_CONV + HEAD_DIM * hd
                sink = sk_ref[0, hd]
                q = pc_ref[:, cq:cq + HEAD_DIM]
                do = dyc_ref[:, cy:cy + HEAD_DIM]
                delta = jnp.sum(do * yc_ref[:, cy:cy + HEAD_DIM], axis=-1, keepdims=True)
                lse_h = lc_ref[:, hd:hd + 1]
                dob = do.astype(BF16)
                pc_ = jnp.exp(_swa_scores(q, kc, SLOPES[hd], dist_cur, mask_cur) - lse_h)
                pp_ = jnp.exp(_swa_scores(q, kp, SLOPES[hd], dist_prev, mask_prev) - lse_h)
                dsc = (pc_ * (_dot_nt(dob, vc) - delta)).astype(BF16)
                dsp = (pp_ * (_dot_nt(dob, vp) - delta)).astype(BF16)
                dq = (_dot(dsc, kc) + _dot(dsp, kp)) * SCALE
                dp_ref[:, cq:cq + HEAD_DIM] = dq.astype(BF16)
                dsink = -jnp.sum(jnp.exp(sink - lse_h) * delta, axis=0, keepdims=True)
                dsk = dsk + jnp.where(lane8 == hd, dsink, 0.0)
                dv = dv + _dot_tn(pc_.astype(BF16), dob)
                dk = dk + _dot_tn(dsc, q) * SCALE
                qn = pn_ref[:, cq:cq + HEAD_DIM]
                don = dyn_ref[:, cy:cy + HEAD_DIM]
                deltan = jnp.sum(don * yn_ref[:, cy:cy + HEAD_DIM], axis=-1, keepdims=True)
                donb = don.astype(BF16)
                pn_ = jnp.exp(_swa_scores(qn, kc, SLOPES[hd], dist_prev, mask_next) - ln_ref[:, hd:hd + 1])
                dsn = (pn_ * (_dot_nt(donb, vc) - deltan)).astype(BF16)
                dv = dv + _dot_tn(pn_.astype(BF16), donb)
                dk = dk + _dot_tn(dsn, qn) * SCALE
            dp_ref[:, COL_K + HEAD_DIM * kv:COL_K + HEAD_DIM * (kv + 1)] = dk.astype(BF16)
            dp_ref[:, COL_V + HEAD_DIM * kv:COL_V + HEAD_DIM * (kv + 1)] = dv.astype(BF16)
        dsk_ref[...] += dsk

        for hm in range(N_MEM_HEADS):
            cq = COL_QM + HEAD_DIM * hm
            cy = D_CONV + D_SWA + HEAD_DIM * hm
            qm = pc_ref[:, cq:cq + HEAD_DIM]
            mk = mkv_ref[:, HEAD_DIM * hm:HEAD_DIM * (hm + 1)]
            mv = mkv_ref[:, D_MEMQ + HEAD_DIM * hm:D_MEMQ + HEAD_DIM * (hm + 1)]
            do = dyc_ref[:, cy:cy + HEAD_DIM]
            delta = jnp.sum(do * yc_ref[:, cy:cy + HEAD_DIM], axis=-1, keepdims=True)
            dob = do.astype(BF16)
            pr = jnp.exp(_dot_nt(qm, mk) * SCALE - lc_ref[:, N_SWA_HEADS + hm:N_SWA_HEADS + hm + 1])
            ds = (pr * (_dot_nt(dob, mv) - delta)).astype(BF16)
            dp_ref[:, cq:cq + HEAD_DIM] = (_dot(ds, mk) * SCALE).astype(BF16)
            dmkv_ref[:, HEAD_DIM * hm:HEAD_DIM * (hm + 1)] += _dot_tn(ds, qm) * SCALE
            dmkv_ref[:, D_MEMQ + HEAD_DIM * hm:D_MEMQ + HEAD_DIM * (hm + 1)] += _dot_tn(pr.astype(BF16), dob)

    cur = lambda i: (i, 0)
    prev = lambda i: (jnp.maximum(i - 1, 0), 0)
    nxt = lambda i: (jnp.minimum(i + 1, nb - 1), 0)
    const = lambda i: (0, 0)
    return _pcall(
        body, name="mix_core_bwd", grid=(nb,),
        in_specs=[pl.BlockSpec(memory_space=pltpu.SMEM),
                  pl.BlockSpec((BLOCK, D_IN), cur), pl.BlockSpec((BLOCK, D_IN), prev), pl.BlockSpec((BLOCK, D_IN), nxt),
                  pl.BlockSpec((BLOCK, D_MIX), cur), pl.BlockSpec((BLOCK, D_MIX), nxt),
                  pl.BlockSpec((BLOCK, D_MIX), cur), pl.BlockSpec((BLOCK, D_MIX), nxt),
                  pl.BlockSpec((BLOCK, 128), cur), pl.BlockSpec((BLOCK, 128), nxt),
                  pl.BlockSpec((m, 2 * D_MEMQ), const),
                  pl.BlockSpec((3, D_CONV), const)],
        out_specs=[pl.BlockSpec((BLOCK, D_IN), cur),
                   pl.BlockSpec((m, 2 * D_MEMQ), const),
                   pl.BlockSpec((8, D_CONV), const),
                   pl.BlockSpec((8, 128), const)],
        out_shape=[jax.ShapeDtypeStruct((t, D_IN), BF16),
                   jax.ShapeDtypeStruct((m, 2 * D_MEMQ), F32),
                   jax.ShapeDtypeStruct((8, D_CONV), F32),
                   jax.ShapeDtypeStruct((8, 128), F32)],
        compiler_params=_params(("arbitrary",)),
    )(sinks, p, p, p, dy, dy, y, y, lse, lse, mkv, convw)


def _group_norms(y):
    out = []
    for a, b in MIX_GROUPS:
        ys = y[:, a:b]
        r = _rstd(ys)
        out.append((ys * r, r))
    return out


def _mix_out_fwd(y, h, g, wout):
    t, d = h.shape
    tm = _tok_block(t)

    def body(y_ref, h_ref, g_ref, w_ref, ho_ref, mt_ref):
        yhat = jnp.concatenate([yh for yh, _ in _group_norms(y_ref[...])], axis=-1)
        mixed = yhat * g_ref[...]
        mt_ref[...] = mixed.T.astype(BF16)
        ho_ref[...] = h_ref[...] + _dot(mixed.astype(BF16), w_ref[...])

    return _pcall(
        body, name="mix_out_fwd", grid=(t // tm,),
        in_specs=[pl.BlockSpec((tm, D_MIX), lambda i: (i, 0)),
                  pl.BlockSpec((tm, d), lambda i: (i, 0)),
                  pl.BlockSpec((1, D_MIX), lambda i: (0, 0)),
                  pl.BlockSpec((D_MIX, d), lambda i: (0, 0))],
        out_specs=[pl.BlockSpec((tm, d), lambda i: (i, 0)),
                   pl.BlockSpec((D_MIX, tm), lambda i: (0, i))],
        out_shape=[jax.ShapeDtypeStruct((t, d), F32), jax.ShapeDtypeStruct((D_MIX, t), BF16)],
        compiler_params=_params(("parallel",)),
    )(y, h, g, wout)


def _mix_out_bwd(dho, y, g, wout, mt):
    t, d = dho.shape
    tm = _tok_block(t)
    ni = t // tm

    def body(dho_ref, y_ref, g_ref, w_ref, mt_ref, dy_ref, dw_ref, dg_ref, acc_ref):
        i = pl.program_id(0)
        dhb = dho_ref[...].astype(BF16)
        dm = _dot_nt(dhb, w_ref[...])
        pw = _dot(mt_ref[...], dhb)
        gg = g_ref[...]
        dys = []
        dgs = []
        for (a, b), (yhat, r) in zip(MIX_GROUPS, _group_norms(y_ref[...])):
            dmg = dm[:, a:b]
            dgs.append(_sum8(dmg * yhat))
            dyh = dmg * gg[:, a:b]
            dys.append(r * (dyh - yhat * jnp.mean(dyh * yhat, axis=-1, keepdims=True)))
        dy_ref[...] = jnp.concatenate(dys, axis=-1)
        part = jnp.concatenate(dgs, axis=-1)

        @pl.when(i == 0)
        def _():
            acc_ref[...] = pw
            dg_ref[...] = part

        @pl.when(i > 0)
        def _():
            acc_ref[...] += pw
            dg_ref[...] += part

        @pl.when(i == ni - 1)
        def _():
            dw_ref[...] = acc_ref[...].astype(BF16)

    return _pcall(
        body, name="mix_out_bwd", grid=(ni,),
        in_specs=[pl.BlockSpec((tm, d), lambda i: (i, 0)),
                  pl.BlockSpec((tm, D_MIX), lambda i: (i, 0)),
                  pl.BlockSpec((1, D_MIX), lambda i: (0, 0)),
                  pl.BlockSpec((D_MIX, d), lambda i: (0, 0)),
                  pl.BlockSpec((D_MIX, tm), lambda i: (0, i))],
        out_specs=[pl.BlockSpec((tm, D_MIX), lambda i: (i, 0)),
                   pl.BlockSpec((D_MIX, d), lambda i: (0, 0)),
                   pl.BlockSpec((8, D_MIX), lambda i: (0, 0))],
        out_shape=[jax.ShapeDtypeStruct((t, D_MIX), F32),
                   jax.ShapeDtypeStruct((D_MIX, d), BF16),
                   jax.ShapeDtypeStruct((8, D_MIX), F32)],
        scratch_shapes=[pltpu.VMEM((D_MIX, d), F32)],
        compiler_params=_params(("arbitrary",)),
    )(dho, y, g, wout, mt)


def _mix_proj_bwd(dp, dho, h, g, win, nt):
    t, d = h.shape
    tm = _tok_block(t)
    ni = t // tm

    def body(dp_ref, dho_ref, h_ref, g_ref, w_ref, nt_ref, dh_ref, dw_ref, dg_ref, acc_ref):
        i = pl.program_id(0)
        dpb = dp_ref[...]
        dn = _dot_nt(dpb, w_ref[...])
        pw = _dot(nt_ref[...], dpb)
        hh = h_ref[...]
        r = _rstd(hh)
        xhat = hh * r
        dxh = dn * g_ref[...]
        dh_ref[...] = dho_ref[...] + r * (dxh - xhat * jnp.mean(dxh * xhat, axis=-1, keepdims=True))
        part = _sum8(dn * xhat)

        @pl.when(i == 0)
        def _():
            acc_ref[...] = pw
            dg_ref[...] = part

        @pl.when(i > 0)
        def _():
            acc_ref[...] += pw
            dg_ref[...] += part

        @pl.when(i == ni - 1)
        def _():
            dw_ref[...] = acc_ref[...].astype(BF16)

    return _pcall(
        body, name="mix_proj_bwd", grid=(ni,),
        in_specs=[pl.BlockSpec((tm, D_IN), lambda i: (i, 0)),
                  pl.BlockSpec((tm, d), lambda i: (i, 0)),
                  pl.BlockSpec((tm, d), lambda i: (i, 0)),
                  pl.BlockSpec((1, d), lambda i: (0, 0)),
                  pl.BlockSpec((d, D_IN), lambda i: (0, 0)),
                  pl.BlockSpec((d, tm), lambda i: (0, i))],
        out_specs=[pl.BlockSpec((tm, d), lambda i: (i, 0)),
                   pl.BlockSpec((d, D_IN), lambda i: (0, 0)),
                   pl.BlockSpec((8, d), lambda i: (0, 0))],
        out_shape=[jax.ShapeDtypeStruct((t, d), F32),
                   jax.ShapeDtypeStruct((d, D_IN), BF16),
                   jax.ShapeDtypeStruct((8, d), F32)],
        scratch_shapes=[pltpu.VMEM((d, D_IN), F32)],
        compiler_params=_params(("arbitrary",)),
    )(dp, dho, h, g, win, nt)


def _final_loss(h, g, tgt):
    t, d = h.shape
    tm = _tok_block(t)

    def body(h_ref, g_ref, t_ref, dh_ref, ls_ref, dg_ref):
        i = pl.program_id(0)
        hh = h_ref[...]
        r = _rstd(hh)
        xhat = hh * r
        gg = g_ref[...]
        err = xhat * gg - t_ref[...]
        dy = err * (1.0 / d)
        dxh = dy * gg
        dh_ref[...] = r * (dxh - xhat * jnp.mean(dxh * xhat, axis=-1, keepdims=True))
        lpart = _sum8(err * err)
        gpart = _sum8(dy * xhat)

        @pl.when(i == 0)
        def _():
            ls_ref[...] = lpart
            dg_ref[...] = gpart

        @pl.when(i > 0)
        def _():
            ls_ref[...] += lpart
            dg_ref[...] += gpart

    return _pcall(
        body, name="final_loss", grid=(t // tm,),
        in_specs=[pl.BlockSpec((tm, d), lambda i: (i, 0)),
                  pl.BlockSpec((1, d), lambda i: (0, 0)),
                  pl.BlockSpec((tm, d), lambda i: (i, 0))],
        out_specs=[pl.BlockSpec((tm, d), lambda i: (i, 0)),
                   pl.BlockSpec((8, d), lambda i: (0, 0)),
                   pl.BlockSpec((8, d), lambda i: (0, 0))],
        out_shape=[jax.ShapeDtypeStruct((t, d), F32),
                   jax.ShapeDtypeStruct((8, d), F32),
                   jax.ShapeDtypeStruct((8, d), F32)],
        compiler_params=_params(("arbitrary",)),
    )(h, g, tgt)


def _position():
    return lax.axis_index("x"), lax.axis_index("y"), lax.axis_index("c")


def _flip(v, bit):
    return 1 - v if bit else v


def _peer(k):
    x, y, c = _position()
    return _flip(x, k & 4), _flip(y, k & 2), _flip(c, k & 1)


def _slot(px, py, pc):
    return 4 * px + 2 * py + pc


def _all_gather(shards, name):
    nt = len(shards)

    def body(*refs):
        xs = refs[:nt]
        outs = refs[nt:2 * nt]
        send_sems, recv_sems, local_sems = refs[2 * nt:]
        x, y, c = _position()
        me, sibling = (x, y, c), (x, y, 1 - c)
        chips = [(1 - x, y), (x, 1 - y), (1 - x, 1 - y)]

        def copy(t, k, block, to, src=None):
            dst = outs[t].at[_slot(*block)]
            return pltpu.make_async_remote_copy(
                src_ref=dst if src is None else src, dst_ref=dst,
                send_sem=send_sems.at[t, k], recv_sem=recv_sems.at[t, k],
                device_id=to, device_id_type=MESH)

        mine = [pltpu.make_async_copy(xs[t], outs[t].at[_slot(*me)], local_sems.at[t]) for t in range(nt)]
        for cp in mine:
            cp.start()
        first = []
        for t in range(nt):
            first.append(copy(t, 0, me, sibling, src=xs[t]))
            first += [copy(t, 1 + j, me, (*chip, c), src=xs[t]) for j, chip in enumerate(chips)]
        for cp in first:
            cp.start()
        passed = []
        for j, chip in enumerate(chips):
            for t in range(nt):
                copy(t, 1 + j, (*chip, c), me).wait_recv()
                fwd = copy(t, 4 + j, (*chip, c), sibling)
                fwd.start()
                passed.append(fwd)
        for t in range(nt):
            copy(t, 0, sibling, me).wait_recv()
            for j, chip in enumerate(chips):
                copy(t, 4 + j, (*chip, 1 - c), me).wait_recv()
        for cp in first + passed:
            cp.wait_send()
        for cp in mine:
            cp.wait()

    any_spec = pl.BlockSpec(memory_space=pl.ANY)
    return _pcall(
        body, name=name,
        in_specs=[any_spec] * nt, out_specs=[any_spec] * nt,
        out_shape=[jax.ShapeDtypeStruct((N_DEV,) + s.shape, s.dtype) for s in shards],
        scratch_shapes=[pltpu.SemaphoreType.DMA((nt, 7)), pltpu.SemaphoreType.DMA((nt, 7)),
                        pltpu.SemaphoreType.DMA((nt,))],
    )(*shards)


def _scatter_partials(partials, name):
    nt = len(partials)

    def body(*refs):
        srcs = refs[:nt]
        outs = refs[nt:2 * nt]
        send_sems, recv_sems, local_sems = refs[2 * nt:]
        x, y, c = _position()

        def copy(t, k):
            peer = _peer(k)
            return pltpu.make_async_remote_copy(
                src_ref=srcs[t].at[_slot(*peer)], dst_ref=outs[t].at[k],
                send_sem=send_sems.at[t, k - 1], recv_sem=recv_sems.at[t, k - 1],
                device_id=peer, device_id_type=MESH)

        mine = [pltpu.make_async_copy(srcs[t].at[_slot(x, y, c)], outs[t].at[0], local_sems.at[t]) for t in range(nt)]
        for cp in mine:
            cp.start()
        sent = [copy(t, k) for k in range(1, N_DEV) for t in range(nt)]
        for cp in sent:
            cp.start()
        for cp in sent:
            cp.wait_recv()
        for cp in sent:
            cp.wait_send()
        for cp in mine:
            cp.wait()

    any_spec = pl.BlockSpec(memory_space=pl.ANY)
    return _pcall(
        body, name=name,
        in_specs=[any_spec] * nt, out_specs=[any_spec] * nt,
        out_shape=[jax.ShapeDtypeStruct(p.shape, p.dtype) for p in partials],
        scratch_shapes=[pltpu.SemaphoreType.DMA((nt, 7)), pltpu.SemaphoreType.DMA((nt, 7)),
                        pltpu.SemaphoreType.DMA((nt,))],
    )(*partials)


def _all_reduce_rows(v):
    nv, _, w = v.shape

    def body(v_ref, out_ref, gath_ref, send_sems, recv_sems):
        x, y, c = _position()
        me = _slot(x, y, c)

        def copy(k):
            return pltpu.make_async_remote_copy(
                src_ref=v_ref, dst_ref=gath_ref.at[me],
                send_sem=send_sems.at[k - 1], recv_sem=recv_sems.at[k - 1],
                device_id=_peer(k), device_id_type=MESH)

        def arrival(k):
            return pltpu.make_async_remote_copy(
                src_ref=v_ref, dst_ref=gath_ref.at[_slot(*_peer(k))],
                send_sem=send_sems.at[k - 1], recv_sem=recv_sems.at[k - 1],
                device_id=_peer(k), device_id_type=MESH)

        sent = [copy(k) for k in range(1, N_DEV)]
        for cp in sent:
            cp.start()
        gath_ref[me] = v_ref[...]
        for k in range(1, N_DEV):
            arrival(k).wait_recv()
        for cp in sent:
            cp.wait_send()
        total = gath_ref[0]
        for s in range(1, N_DEV):
            total = total + gath_ref[s]
        out_ref[...] = jnp.sum(total, axis=1)

    vmem = pl.BlockSpec(memory_space=pltpu.VMEM)
    return _pcall(
        body, name="all_reduce_rows",
        in_specs=[vmem], out_specs=vmem,
        out_shape=jax.ShapeDtypeStruct((nv, w), F32),
        scratch_shapes=[pltpu.VMEM((N_DEV, nv, 8, w), F32),
                        pltpu.SemaphoreType.DMA((7,)), pltpu.SemaphoreType.DMA((7,))],
    )(v)


def _adamw_math(w, g, m, v):
    m2 = ADAM_B1 * m + (1.0 - ADAM_B1) * g
    v2 = ADAM_B2 * v + (1.0 - ADAM_B2) * (g * g)
    m_hat = m2 / (1.0 - ADAM_B1 ** ADAM_STEP)
    v_hat = v2 / (1.0 - ADAM_B2 ** ADAM_STEP)
    delta = -ADAM_LR * (m_hat / (jnp.sqrt(v_hat) + ADAM_EPS) + ADAM_WD * w)
    return delta, m2, v2


def _row_block(r):
    for cand in (256, 176, 128):
        if r % cand == 0:
            return cand
    return r


def _adamw_sharded(recv0, recv1, w, m, v):
    _, r, c = recv0.shape
    tr = _row_block(r)
    nr = r // tr

    def body(r0_ref, r1_ref, w_ref, m_ref, v_ref, g_ref, d_ref, m2_ref, v2_ref):
        layer = pl.program_id(0)

        def total(ref):
            acc = ref[0].astype(F32)
            for k in range(1, N_DEV):
                acc = acc + ref[k].astype(F32)
            return acc

        g = jnp.where(layer == 0, total(r0_ref), total(r1_ref))
        delta, m2, v2 = _adamw_math(w_ref[0], g, m_ref[0], v_ref[0])
        g_ref[0] = g
        d_ref[0] = delta
        m2_ref[0] = m2
        v2_ref[0] = v2

    shard = pl.BlockSpec((1, tr, c), lambda l, i: (l, i, 0))
    out = jax.ShapeDtypeStruct((2, r, c), F32)
    return _pcall(
        body, name="adamw_sharded", grid=(2, nr),
        in_specs=[pl.BlockSpec((N_DEV, tr, c), lambda l, i: (0, jnp.where(l == 0, i, nr - 1), 0)),
                  pl.BlockSpec((N_DEV, tr, c), lambda l, i: (0, jnp.where(l == 1, i, 0), 0)),
                  shard, shard, shard],
        out_specs=[shard, shard, shard, shard],
        out_shape=[out, out, out, out],
        compiler_params=_params(("arbitrary", "arbitrary")),
    )(recv0, recv1, w, m, v)


def _adamw_small(w, g, m, v):
    def body(w_ref, g_ref, m_ref, v_ref, d_ref, m2_ref, v2_ref):
        delta, m2, v2 = _adamw_math(w_ref[...], g_ref[...], m_ref[...], v_ref[...])
        d_ref[...] = delta
        m2_ref[...] = m2
        v2_ref[...] = v2

    spec = pl.BlockSpec(w.shape, lambda i: (0, 0))
    out = jax.ShapeDtypeStruct(w.shape, F32)
    return _pcall(
        body, name="adamw_small", grid=(1,),
        in_specs=[spec] * 4, out_specs=[spec] * 3, out_shape=[out] * 3,
        compiler_params=_params(("arbitrary",)),
    )(w, g, m, v)


def _pack(arrs):
    flat = jnp.concatenate([a.reshape(-1) for a in arrs])
    n = flat.shape[0]
    rows = -(-n // 1024) * 8
    return jnp.pad(flat, (0, rows * 128 - n)).reshape(rows, 128)


def _unpack(packed, like):
    flat = packed.reshape(-1)
    out, off = [], 0
    for a in like:
        out.append(flat[off:off + a.size].reshape(a.shape))
        off += a.size
    return out


def kernel(x, mem, g_ffn1, w_ffn1_up, w_ffn1_down, g_mix, w_in, conv_w, sinks, g_mem, w_mem_kv, g_grp, w_out, g_ffn2, w_ffn2_up, w_ffn2_down, g_final, loss_target, m_g_ffn1, m_w_ffn1_up, m_w_ffn1_down, m_g_mix, m_w_in, m_conv_w, m_sinks, m_g_mem, m_w_mem_kv, m_g_grp, m_w_out, m_g_ffn2, m_w_ffn2_up, m_w_ffn2_down, m_g_final, v_g_ffn1, v_w_ffn1_up, v_w_ffn1_down, v_g_mix, v_w_in, v_conv_w, v_sinks, v_g_mem, v_w_mem_kv, v_g_grp, v_w_out, v_g_ffn2, v_w_ffn2_up, v_w_ffn2_down, v_g_final):
    depth = g_ffn1.shape[0]
    t, d = x.shape[1], x.shape[2]
    width = max(d, D_MIX)
    nj = N_DEV // 2
    me = _slot(*_position())
    conv_shard = conv_w.shape[2]

    xin, memin, tgt = x[0], mem[0], loss_target[0]

    conv_tile = jnp.zeros((depth * 8, 128), F32).at[:, :conv_shard].set(
        jnp.pad(conv_w, ((0, 0), (0, 8 - conv_w.shape[1]), (0, 0))).reshape(depth * 8, conv_shard))
    weights = []
    for l in range(depth):
        shards = [w_ffn1_up[l].astype(BF16), w_ffn1_down[l].astype(BF16), w_in[l].astype(BF16),
                  w_mem_kv[l].astype(BF16), w_out[l].astype(BF16), w_ffn2_up[l].astype(BF16),
                  w_ffn2_down[l].astype(BF16)]
        if l == 0:
            shards.append(conv_tile)
        full = _all_gather(shards, f"all_gather_l{l}")
        if l == 0:
            conv_full = full[7].reshape(N_DEV, depth, 8, 128)[:, :, :3, :conv_shard]
            conv_full = conv_full.transpose(1, 2, 0, 3).reshape(depth, 3, N_DEV * conv_shard)
        up1, dn1, win, wkv, wout, up2, dn2 = full[:7]
        weights.append(dict(
            up1=up1.reshape(2, nj, d, up1.shape[2]), dn1=dn1.reshape(-1, d),
            win=win.transpose(1, 0, 2).reshape(d, D_IN),
            wkv=wkv.reshape(d, 2 * D_MEMQ), wout=wout.reshape(D_MIX, d),
            up2=up2.reshape(2, nj, d, up2.shape[2]), dn2=dn2.reshape(-1, d)))

    row = lambda a: a.reshape(1, -1)

    h = xin
    saved = []
    for l in range(depth):
        wl = weights[l]
        s = dict(h0=h)
        h, s["gu1"], _, s["nt1"] = _ffn_fwd(h, row(g_ffn1[l]), wl["up1"], wl["dn1"])
        s["h1"] = h
        s["p"], s["nt_mix"] = _mix_proj_fwd(h, row(g_mix[l]), wl["win"])
        s["mkv"], s["nt_mem"] = _memkv_fwd(memin, row(g_mem[l]), wl["wkv"])
        s["y"], s["lse"] = _mix_core_fwd(s["p"], s["mkv"], conv_full[l], row(sinks[l]))
        h, s["mt"] = _mix_out_fwd(s["y"], h, row(g_grp[l]), wl["wout"])
        s["h2"] = h
        h, s["gu2"], _, s["nt2"] = _ffn_fwd(h, row(g_ffn2[l]), wl["up2"], wl["dn2"])
        saved.append(s)

    dh, loss_part, dg_final = _final_loss(h, row(g_final), tgt)
    loss = lax.psum(0.5 * jnp.sum(loss_part) / d, ("x", "y", "c"))

    small = {}
    recv = []
    for l in reversed(range(depth)):
        wl, s = weights[l], saved[l]
        dh, agu, dyb, small["g_ffn2", l] = _ffn_bwd_act(dh, s["h2"], row(g_ffn2[l]), s["gu2"], wl["up2"], wl["dn2"])
        dup2, ddn2 = _ffn_bwd_w(agu, dyb, s["nt2"])
        dy, dwout, small["g_grp", l] = _mix_out_bwd(dh, s["y"], row(g_grp[l]), wl["wout"], s["mt"])
        dp, dmkv, small["conv_w", l], small["sinks", l] = _mix_core_bwd(
            s["p"], dy, s["y"], s["lse"], s["mkv"], conv_full[l], row(sinks[l]))
        dwkv, small["g_mem", l] = _memkv_bwd(dmkv, memin, row(g_mem[l]), wl["wkv"], s["nt_mem"])
        dh, dwin, small["g_mix", l] = _mix_proj_bwd(dp, dh, s["h1"], row(g_mix[l]), wl["win"], s["nt_mix"])
        dh, agu, dyb, small["g_ffn1", l] = _ffn_bwd_act(dh, s["h0"], row(g_ffn1[l]), s["gu1"], wl["up1"], wl["dn1"])
        dup1, ddn1 = _ffn_bwd_w(agu, dyb, s["nt1"])
        partials = [dup1.reshape(N_DEV, d, -1), ddn1.reshape(N_DEV, -1, d),
                    dwin.reshape(d, N_DEV, -1).transpose(1, 0, 2),
                    dwkv.reshape(N_DEV, -1, 2 * D_MEMQ), dwout.reshape(N_DEV, -1, d),
                    dup2.reshape(N_DEV, d, -1), ddn2.reshape(N_DEV, -1, d)]
        recv.append(_scatter_partials(partials, f"scatter_grads_l{l}"))
    recv = recv[::-1]
    grad_x = dh[None]

    sharded = {}
    big = [("w_ffn1_up", w_ffn1_up, m_w_ffn1_up, v_w_ffn1_up), ("w_ffn1_down", w_ffn1_down, m_w_ffn1_down, v_w_ffn1_down),
           ("w_in", w_in, m_w_in, v_w_in), ("w_mem_kv", w_mem_kv, m_w_mem_kv, v_w_mem_kv),
           ("w_out", w_out, m_w_out, v_w_out), ("w_ffn2_up", w_ffn2_up, m_w_ffn2_up, v_w_ffn2_up),
           ("w_ffn2_down", w_ffn2_down, m_w_ffn2_down, v_w_ffn2_down)]
    for idx, (name, w, m, v) in enumerate(big):
        sharded[name] = _adamw_sharded(recv[0][idx], recv[depth - 1][idx], w, m, v)

    def lanes(a):
        return jnp.pad(a, ((0, 0), (0, width - a.shape[1])))

    def first_row(a):
        return lanes(jnp.pad(a, ((0, 8 - a.shape[0]), (0, 0))))

    vec_names = ["g_ffn1", "g_mix", "g_mem", "g_grp", "g_ffn2", "sinks"]
    tiles = [lanes(small[n, l]) for n in vec_names for l in range(depth)]
    tiles += [first_row(small["conv_w", l][k:k + 1]) for l in range(depth) for k in range(3)]
    tiles.append(lanes(dg_final))
    n_real = len(tiles)
    tiles += [jnp.zeros((8, width), F32)] * (-n_real % 8)
    summed = _all_reduce_rows(jnp.stack(tiles))

    def vec(n, wd):
        return jnp.stack([summed[vec_names.index(n) * depth + l, :wd] for l in range(depth)])

    conv_base = len(vec_names) * depth
    conv_grad = jnp.stack([jnp.stack([summed[conv_base + 3 * l + k, :D_CONV] for k in range(3)]) for l in range(depth)])
    grads_small = {
        "g_ffn1": vec("g_ffn1", d), "g_mix": vec("g_mix", d), "g_mem": vec("g_mem", d),
        "g_grp": vec("g_grp", D_MIX), "g_ffn2": vec("g_ffn2", d), "sinks": vec("sinks", N_SWA_HEADS),
        "conv_w": lax.dynamic_slice_in_dim(conv_grad, me * conv_shard, conv_shard, axis=2),
        "g_final": summed[n_real - 1, :d],
    }
    small_w = [("g_ffn1", g_ffn1, m_g_ffn1, v_g_ffn1), ("g_mix", g_mix, m_g_mix, v_g_mix),
               ("conv_w", conv_w, m_conv_w, v_conv_w), ("sinks", sinks, m_sinks, v_sinks),
               ("g_mem", g_mem, m_g_mem, v_g_mem), ("g_grp", g_grp, m_g_grp, v_g_grp),
               ("g_ffn2", g_ffn2, m_g_ffn2, v_g_ffn2), ("g_final", g_final, m_g_final, v_g_final)]
    like = [w for _, w, _, _ in small_w]
    packed = _adamw_small(_pack(like), _pack([grads_small[n] for n, _, _, _ in small_w]),
                          _pack([m for _, _, m, _ in small_w]), _pack([v for _, _, _, v in small_w]))
    small_out = {n: (grads_small[n], dl, m2, v2)
                 for (n, _, _, _), dl, m2, v2 in zip(small_w, *[_unpack(pk, like) for pk in packed])}

    order = ["g_ffn1", "w_ffn1_up", "w_ffn1_down", "g_mix", "w_in", "conv_w", "sinks", "g_mem", "w_mem_kv", "g_grp",
             "w_out", "g_ffn2", "w_ffn2_up", "w_ffn2_down", "g_final"]
    results = {**sharded, **small_out}
    outs = [loss, grad_x]
    for part in range(4):
        outs += [results[n][part] for n in order]
    return tuple(outs)
```

```python
import functools

import jax
import jax.numpy as jnp
from jax import lax
from jax.experimental import pallas as pl
from jax.experimental.pallas import tpu as pltpu
from jax.experimental.pallas import tpu_sc as plsc

F32 = jnp.float32
BF16 = jnp.bfloat16

N_DEV = 8
EPS = 1e-6
N_SWA_HEADS = 8
N_SWA_KV = 2
SWA_GROUP = N_SWA_HEADS // N_SWA_KV
HEAD_DIM = 64
N_MEM_HEADS = 4
D_CONV = 256
BLOCK = 128
D_SWA = N_SWA_HEADS * HEAD_DIM
D_KV = N_SWA_KV * HEAD_DIM
D_MEMQ = N_MEM_HEADS * HEAD_DIM
D_MIX = D_CONV + D_SWA + D_MEMQ
D_IN = 3 * D_CONV + D_SWA + 2 * D_KV + D_MEMQ
COL_BG, COL_CG, COL_U = 0, D_CONV, 2 * D_CONV
COL_Q = 3 * D_CONV
COL_K = COL_Q + D_SWA
COL_V = COL_K + D_KV
COL_QM = COL_V + D_KV
MIX_GROUPS = ((0, D_CONV), (D_CONV, D_CONV + D_SWA), (D_CONV + D_SWA, D_MIX))
SLOPES = tuple(2.0 ** (-8.0 * (i + 1) / N_SWA_HEADS) for i in range(N_SWA_HEADS))
SCALE = HEAD_DIM ** -0.5
NEG = -1e30

ADAM_LR = 0.001
ADAM_B1 = 0.9
ADAM_B2 = 0.999
ADAM_EPS = 1e-08
ADAM_WD = 0.01
ADAM_STEP = 10

V7X_VMEM_BYTES = 64 * 1024 * 1024
VMEM_LIMIT = (V7X_VMEM_BYTES * 3) // 4
MESH = pl.DeviceIdType.MESH


def _pcall(body, **kw):
    return pl.pallas_call(body, **kw)


def _params(sem=None, vmem=VMEM_LIMIT):
    return pltpu.CompilerParams(dimension_semantics=sem, vmem_limit_bytes=vmem)


def _dot(a, b):
    return lax.dot_general(a, b, (((1,), (0,)), ((), ())), preferred_element_type=F32)


def _dot_nt(a, b):
    return lax.dot_general(a, b, (((1,), (1,)), ((), ())), preferred_element_type=F32)


def _dot_tn(a, b):
    return lax.dot_general(a, b, (((0,), (0,)), ((), ())), preferred_element_type=F32)


def _rstd(x):
    return lax.rsqrt(jnp.mean(x * x, axis=-1, keepdims=True) + EPS)


def _sigmoid(x):
    return 1.0 / (1.0 + jnp.exp(-x))


def _sum8(x):
    r, w = x.shape
    return jnp.sum(x.reshape(r // 8, 8, w), axis=0)


def _tok_block(t):
    return min(512, t)


def _feat_block(f):
    return f // (N_DEV // 2)


def _ffn_fwd(h, g, wup_t, wdn):
    t, d = h.shape
    f = wdn.shape[0]
    tm, tf = _tok_block(t), _feat_block(f)
    ni, nj = t // tm, f // tf

    def body(h_ref, g_ref, wup_ref, wdn_ref, ho_ref, gu_ref, n_ref, nt_ref, acc_ref):
        j = pl.program_id(1)

        @pl.when(j == 0)
        def _():
            hh = h_ref[...]
            n = hh * _rstd(hh) * g_ref[...]
            n_ref[...] = n.astype(BF16)
            nt_ref[...] = n.T.astype(BF16)
            acc_ref[...] = jnp.zeros_like(acc_ref)

        nt = nt_ref[...]
        gate = _dot(wup_ref[0], nt)
        up = _dot(wup_ref[1], nt)
        gu_ref[0] = gate.astype(BF16)
        gu_ref[1] = up.astype(BF16)
        a = gate * _sigmoid(gate) * up
        acc_ref[...] += _dot_tn(a.astype(BF16), wdn_ref[...])

        @pl.when(j == nj - 1)
        def _():
            ho_ref[...] = h_ref[...] + 0.5 * acc_ref[...]

    return _pcall(
        body, name="ffn_fwd", grid=(ni, nj),
        in_specs=[pl.BlockSpec((tm, d), lambda i, j: (i, 0)),
                  pl.BlockSpec((1, d), lambda i, j: (0, 0)),
                  pl.BlockSpec((2, tf, d), lambda i, j: (0, j, 0)),
                  pl.BlockSpec((tf, d), lambda i, j: (j, 0))],
        out_specs=[pl.BlockSpec((tm, d), lambda i, j: (i, 0)),
                   pl.BlockSpec((2, tf, tm), lambda i, j: (0, j, i)),
                   pl.BlockSpec((tm, d), lambda i, j: (i, 0))],
        out_shape=[jax.ShapeDtypeStruct((t, d), F32),
                   jax.ShapeDtypeStruct((2, f, t), BF16),
                   jax.ShapeDtypeStruct((t, d), BF16)],
        scratch_shapes=[pltpu.VMEM((d, tm), BF16), pltpu.VMEM((tm, d), F32)],
        compiler_params=_params(("parallel", "arbitrary")),
    )(h, g, wup_t, wdn)


def _ffn_bwd_act(dho, h, g, gu, wup_t, wdn, dep):
    t, d = h.shape
    f = wdn.shape[0]
    tm, tf = _tok_block(t), _feat_block(f)
    ni, nj = t // tm, f // tf

    def body(dho_ref, h_ref, g_ref, gu_ref, wup_ref, wdn_ref, dep_ref, dh_ref, agu_ref, dyb_ref, dg_ref, dyt_ref, acc_ref):
        i = pl.program_id(0)
        j = pl.program_id(1)

        @pl.when(j == 0)
        def _():
            dy0 = 0.5 * dho_ref[...]
            dyb_ref[...] = dy0.astype(BF16)
            dyt_ref[...] = dy0.T.astype(BF16)

        da = _dot(wdn_ref[...], dyt_ref[...])
        gate = gu_ref[0].astype(F32)
        up = gu_ref[1].astype(F32)
        sg = _sigmoid(gate)
        silu = gate * sg
        dgate = (da * up * (sg * (1.0 + gate * (1.0 - sg)))).astype(BF16)
        dup = (da * silu).astype(BF16)
        agu_ref[0] = (silu * up).astype(BF16)
        agu_ref[1] = dgate
        agu_ref[2] = dup
        dn = _dot_tn(dgate, wup_ref[0]) + _dot_tn(dup, wup_ref[1])

        @pl.when(j == 0)
        def _():
            acc_ref[...] = dn

        @pl.when(j > 0)
        def _():
            acc_ref[...] += dn

        @pl.when(j == nj - 1)
        def _():
            hh = h_ref[...]
            r = _rstd(hh)
            xhat = hh * r
            dnf = acc_ref[...]
            dxh = dnf * g_ref[...]
            dh_ref[...] = dho_ref[...] + r * (dxh - xhat * jnp.mean(dxh * xhat, axis=-1, keepdims=True))
            part = _sum8(dnf * xhat)

            @pl.when(i == 0)
            def _():
                dg_ref[...] = part

            @pl.when(i > 0)
            def _():
                dg_ref[...] += part

    return _pcall(
        body, name="ffn_bwd_act", grid=(ni, nj),
        in_specs=[pl.BlockSpec((tm, d), lambda i, j: (i, 0)),
                  pl.BlockSpec((tm, d), lambda i, j: (i, 0)),
                  pl.BlockSpec((1, d), lambda i, j: (0, 0)),
                  pl.BlockSpec((2, tf, tm), lambda i, j: (0, j, i)),
                  pl.BlockSpec((2, tf, d), lambda i, j: (0, j, 0)),
                  pl.BlockSpec((tf, d), lambda i, j: (j, 0)),
                  pl.BlockSpec(memory_space=pl.ANY)],
        out_specs=[pl.BlockSpec((tm, d), lambda i, j: (i, 0)),
                   pl.BlockSpec((3, tf, tm), lambda i, j: (0, j, i)),
                   pl.BlockSpec((tm, d), lambda i, j: (i, 0)),
                   pl.BlockSpec((8, d), lambda i, j: (0, 0))],
        out_shape=[jax.ShapeDtypeStruct((t, d), F32),
                   jax.ShapeDtypeStruct((3, f, t), BF16),
                   jax.ShapeDtypeStruct((t, d), BF16),
                   jax.ShapeDtypeStruct((8, d), F32)],
        scratch_shapes=[pltpu.VMEM((d, tm), BF16), pltpu.VMEM((tm, d), F32)],
        compiler_params=_params(("arbitrary", "arbitrary")),
    )(dho, h, g, gu, wup_t, wdn, dep)


def _ffn_bwd_w(agu, dyb, n):
    _, f, t = agu.shape
    d = dyb.shape[1]
    tm, tf = _tok_block(t), _feat_block(f)
    ni, nj = t // tm, f // tf

    def body(agu_ref, dyb_ref, n_ref, dwup_ref, dwdn_ref, accg_ref, accu_ref, accd_ref):
        i = pl.program_id(1)
        nb = n_ref[...]
        pg = _dot(agu_ref[1], nb)
        pu = _dot(agu_ref[2], nb)
        pd = _dot(agu_ref[0], dyb_ref[...])

        @pl.when(i == 0)
        def _():
            accg_ref[...] = pg
            accu_ref[...] = pu
            accd_ref[...] = pd

        @pl.when(i > 0)
        def _():
            accg_ref[...] += pg
            accu_ref[...] += pu
            accd_ref[...] += pd

        @pl.when(i == ni - 1)
        def _():
            dwup_ref[0] = accg_ref[...].astype(BF16)
            dwup_ref[1] = accu_ref[...].astype(BF16)
            dwdn_ref[...] = accd_ref[...].astype(BF16)

    return _pcall(
        body, name="ffn_bwd_w", grid=(nj, ni),
        in_specs=[pl.BlockSpec((3, tf, tm), lambda j, i: (0, j, i)),
                  pl.BlockSpec((tm, d), lambda j, i: (i, 0)),
                  pl.BlockSpec((tm, d), lambda j, i: (i, 0))],
        out_specs=[pl.BlockSpec((2, tf, d), lambda j, i: (0, j, 0)),
                   pl.BlockSpec((tf, d), lambda j, i: (j, 0))],
        out_shape=[jax.ShapeDtypeStruct((2, f, d), BF16),
                   jax.ShapeDtypeStruct((f, d), BF16)],
        scratch_shapes=[pltpu.VMEM((tf, d), F32), pltpu.VMEM((tf, d), F32), pltpu.VMEM((tf, d), F32)],
        compiler_params=_params(("parallel", "arbitrary")),
    )(agu, dyb, n)


def _mix_proj_fwd(h, g, win_t):
    t, d = h.shape
    tm = _tok_block(t)

    def body(h_ref, g_ref, win_ref, p_ref, n_ref):
        hh = h_ref[...]
        n = (hh * _rstd(hh) * g_ref[...]).astype(BF16)
        n_ref[...] = n
        p_ref[...] = _dot_nt(n, win_ref[...]).astype(BF16)

    return _pcall(
        body, name="mix_proj_fwd", grid=(t // tm,),
        in_specs=[pl.BlockSpec((tm, d), lambda i: (i, 0)),
                  pl.BlockSpec((1, d), lambda i: (0, 0)),
                  pl.BlockSpec((D_IN, d), lambda i: (0, 0))],
        out_specs=[pl.BlockSpec((tm, D_IN), lambda i: (i, 0)),
                   pl.BlockSpec((tm, d), lambda i: (i, 0))],
        out_shape=[jax.ShapeDtypeStruct((t, D_IN), BF16), jax.ShapeDtypeStruct((t, d), BF16)],
        compiler_params=_params(("parallel",)),
    )(h, g, win_t)


def _memkv_fwd(mem, g, wkv, dep):
    m, d = mem.shape

    def body(mem_ref, g_ref, w_ref, dep_ref, mkv_ref, nt_ref):
        mm = mem_ref[...]
        n = mm * _rstd(mm) * g_ref[...]
        nt_ref[...] = n.T.astype(BF16)
        mkv_ref[...] = _dot(n.astype(BF16), w_ref[...]).astype(BF16)

    return _pcall(
        body, name="memkv_fwd", grid=(1,),
        in_specs=[pl.BlockSpec((m, d), lambda i: (0, 0)),
                  pl.BlockSpec((1, d), lambda i: (0, 0)),
                  pl.BlockSpec((d, 2 * D_MEMQ), lambda i: (0, 0)),
                  pl.BlockSpec(memory_space=pl.ANY)],
        out_specs=[pl.BlockSpec((m, 2 * D_MEMQ), lambda i: (0, 0)),
                   pl.BlockSpec((d, m), lambda i: (0, 0))],
        out_shape=[jax.ShapeDtypeStruct((m, 2 * D_MEMQ), BF16), jax.ShapeDtypeStruct((d, m), BF16)],
        compiler_params=_params(("arbitrary",)),
    )(mem, g, wkv, dep)


def _memkv_bwd(dmkv, mem, g, wkv, nt):
    m, d = mem.shape

    def body(dmkv_ref, mem_ref, g_ref, w_ref, nt_ref, dw_ref, dg_ref):
        db = dmkv_ref[...].astype(BF16)
        dw_ref[...] = _dot(nt_ref[...], db).astype(BF16)
        dn = _dot_nt(db, w_ref[...])
        mm = mem_ref[...]
        dg_ref[...] = _sum8(dn * (mm * _rstd(mm)))

    return _pcall(
        body, name="memkv_bwd", grid=(1,),
        in_specs=[pl.BlockSpec((m, 2 * D_MEMQ), lambda i: (0, 0)),
                  pl.BlockSpec((m, d), lambda i: (0, 0)),
                  pl.BlockSpec((1, d), lambda i: (0, 0)),
                  pl.BlockSpec((d, 2 * D_MEMQ), lambda i: (0, 0)),
                  pl.BlockSpec((d, m), lambda i: (0, 0))],
        out_specs=[pl.BlockSpec((d, 2 * D_MEMQ), lambda i: (0, 0)),
                   pl.BlockSpec((8, d), lambda i: (0, 0))],
        out_shape=[jax.ShapeDtypeStruct((d, 2 * D_MEMQ), BF16), jax.ShapeDtypeStruct((8, d), F32)],
        compiler_params=_params(("arbitrary",)),
    )(dmkv, mem, g, wkv, nt)


def _shift_rows(v, k, edge_rows, row):
    out = pltpu.roll(v, k, 0)
    for r in range(k):
        out = jnp.where(row == r, edge_rows[r], out)
    return out


def _shift_rows_up(v, k, edge_rows, row):
    n = v.shape[0]
    out = pltpu.roll(v, n - k, 0)
    for r in range(k):
        out = jnp.where(row == n - k + r, edge_rows[r], out)
    return out


def _swa_scores(q, k, slope, dist, mask):
    s = _dot_nt(q, k) * SCALE - slope * dist
    return jnp.where(mask, s, NEG)


def _band_geometry():
    ti = lax.broadcasted_iota(jnp.int32, (BLOCK, BLOCK), 0)
    si = lax.broadcasted_iota(jnp.int32, (BLOCK, BLOCK), 1)
    dist_cur = (ti - si).astype(F32)
    return dist_cur, dist_cur + float(BLOCK), ti >= si, si > ti


def _mix_core_fwd(p, mkv, convw, sinks):
    t = p.shape[0]
    m = mkv.shape[0]
    nb = t // BLOCK

    def body(sk_ref, pc_ref, pkv_ref, ppc_ref, ppu_ref, mkv_ref, cw_ref, y_ref, l_ref):
        i = pl.program_id(0)
        has_prev = i > 0
        prevf = has_prev.astype(F32)
        row = lax.broadcasted_iota(jnp.int32, (BLOCK, D_CONV), 0)

        bg = pc_ref[:, COL_BG:COL_BG + D_CONV].astype(F32)
        cg = pc_ref[:, COL_CG:COL_CG + D_CONV].astype(F32)
        u = pc_ref[:, COL_U:COL_U + D_CONV].astype(F32)
        vv = cg * u
        pvv = ppc_ref[...].astype(F32) * ppu_ref[...].astype(F32) * prevf
        vv1 = _shift_rows(vv, 1, [pvv[15:16]], row)
        vv2 = _shift_rows(vv, 2, [pvv[14:15], pvv[15:16]], row)
        w = cw_ref[...]
        y_ref[:, 0:D_CONV] = bg * (w[0:1] * vv2 + w[1:2] * vv1 + w[2:3] * vv)

        dist_cur, dist_prev, mask_cur, mask_up = _band_geometry()
        mask_prev = jnp.logical_and(mask_up, has_prev)
        lane = lax.broadcasted_iota(jnp.int32, (BLOCK, 128), 1)
        lse_all = jnp.zeros((BLOCK, 128), F32)
        for kv in range(N_SWA_KV):
            kc = pc_ref[:, COL_K + HEAD_DIM * kv:COL_K + HEAD_DIM * (kv + 1)]
            vc = pc_ref[:, COL_V + HEAD_DIM * kv:COL_V + HEAD_DIM * (kv + 1)]
            kp = pkv_ref[:, HEAD_DIM * kv:HEAD_DIM * (kv + 1)]
            vp = pkv_ref[:, D_KV + HEAD_DIM * kv:D_KV + HEAD_DIM * (kv + 1)]
            for gi in range(SWA_GROUP):
                hd = kv * SWA_GROUP + gi
                q = pc_ref[:, COL_Q + HEAD_DIM * hd:COL_Q + HEAD_DIM * (hd + 1)]
                sc = _swa_scores(q, kc, SLOPES[hd], dist_cur, mask_cur)
                sp = _swa_scores(q, kp, SLOPES[hd], dist_prev, mask_prev)
                sink = sk_ref[0, hd]
                mx = jnp.maximum(jnp.maximum(jnp.max(sc, axis=-1, keepdims=True),
                                             jnp.max(sp, axis=-1, keepdims=True)), sink)
                ec = jnp.exp(sc - mx)
                ep = jnp.exp(sp - mx)
                den = (jnp.sum(ec, axis=-1, keepdims=True) + jnp.sum(ep, axis=-1, keepdims=True)
                       + jnp.exp(sink - mx))
                o = _dot(ec.astype(BF16), vc) + _dot(ep.astype(BF16), vp)
                c0 = D_CONV + HEAD_DIM * hd
                y_ref[:, c0:c0 + HEAD_DIM] = o / den
                lse_all = jnp.where(lane == hd, mx + jnp.log(den), lse_all)

        for hm in range(N_MEM_HEADS):
            qm = pc_ref[:, COL_QM + HEAD_DIM * hm:COL_QM + HEAD_DIM * (hm + 1)]
            mk = mkv_ref[:, HEAD_DIM * hm:HEAD_DIM * (hm + 1)]
            mv = mkv_ref[:, D_MEMQ + HEAD_DIM * hm:D_MEMQ + HEAD_DIM * (hm + 1)]
            s = _dot_nt(qm, mk) * SCALE
            mx = jnp.max(s, axis=-1, keepdims=True)
            e = jnp.exp(s - mx)
            den = jnp.sum(e, axis=-1, keepdims=True)
            c0 = D_CONV + D_SWA + HEAD_DIM * hm
            y_ref[:, c0:c0 + HEAD_DIM] = _dot(e.astype(BF16), mv) / den
            lse_all = jnp.where(lane == N_SWA_HEADS + hm, mx + jnp.log(den), lse_all)
        l_ref[...] = lse_all

    kv_col = COL_K // (2 * D_KV)
    return _pcall(
        body, name="mix_core_fwd", grid=(nb,),
        in_specs=[pl.BlockSpec(memory_space=pltpu.SMEM),
                  pl.BlockSpec((BLOCK, D_IN), lambda i: (i, 0)),
                  pl.BlockSpec((BLOCK, 2 * D_KV), lambda i: (jnp.maximum(i - 1, 0), kv_col)),
                  pl.BlockSpec((16, D_CONV), lambda i: (jnp.maximum(i * (BLOCK // 16) - 1, 0), COL_CG // D_CONV)),
                  pl.BlockSpec((16, D_CONV), lambda i: (jnp.maximum(i * (BLOCK // 16) - 1, 0), COL_U // D_CONV)),
                  pl.BlockSpec((m, 2 * D_MEMQ), lambda i: (0, 0)),
                  pl.BlockSpec((3, D_CONV), lambda i: (0, 0))],
        out_specs=[pl.BlockSpec((BLOCK, D_MIX), lambda i: (i, 0)),
                   pl.BlockSpec((BLOCK, 128), lambda i: (i, 0))],
        out_shape=[jax.ShapeDtypeStruct((t, D_MIX), F32), jax.ShapeDtypeStruct((t, 128), F32)],
        compiler_params=_params(("parallel",)),
    )(sinks, p, p, p, p, mkv, convw)


def _mix_core_bwd(p, dy, y, lse, mkv, convw, sinks):
    t = p.shape[0]
    m = mkv.shape[0]
    nb = t // BLOCK

    def body(sk_ref, pc_ref, pp_ref, pn_ref, dyc_ref, dyn_ref, yc_ref, yn_ref, lc_ref, ln_ref, mkv_ref, cw_ref,
             dp_ref, dmkv_ref, dcw_ref, dsk_ref):
        i = pl.program_id(0)
        has_prev = i > 0
        has_next = i < nb - 1
        prevf = has_prev.astype(F32)
        nextf = has_next.astype(F32)
        row = lax.broadcasted_iota(jnp.int32, (BLOCK, D_CONV), 0)

        @pl.when(i == 0)
        def _():
            dmkv_ref[...] = jnp.zeros_like(dmkv_ref)
            dcw_ref[...] = jnp.zeros_like(dcw_ref)
            dsk_ref[...] = jnp.zeros_like(dsk_ref)

        bg = pc_ref[:, COL_BG:COL_BG + D_CONV].astype(F32)
        cg = pc_ref[:, COL_CG:COL_CG + D_CONV].astype(F32)
        u = pc_ref[:, COL_U:COL_U + D_CONV].astype(F32)
        vv = cg * u
        pvv = (pp_ref[BLOCK - 16:BLOCK, COL_CG:COL_CG + D_CONV].astype(F32)
               * pp_ref[BLOCK - 16:BLOCK, COL_U:COL_U + D_CONV].astype(F32) * prevf)
        vv1 = _shift_rows(vv, 1, [pvv[15:16]], row)
        vv2 = _shift_rows(vv, 2, [pvv[14:15], pvv[15:16]], row)
        w = cw_ref[...]
        yconv = w[0:1] * vv2 + w[1:2] * vv1 + w[2:3] * vv
        dyo = dyc_ref[:, 0:D_CONV]
        dyc = dyo * bg
        nxt = dyn_ref[0:16, 0:D_CONV] * pn_ref[0:16, COL_BG:COL_BG + D_CONV].astype(F32) * nextf
        d1 = _shift_rows_up(dyc, 1, [nxt[0:1]], row)
        d2 = _shift_rows_up(dyc, 2, [nxt[0:1], nxt[1:2]], row)
        dvv = w[2:3] * dyc + w[1:2] * d1 + w[0:1] * d2
        dp_ref[:, COL_BG:COL_BG + D_CONV] = (dyo * yconv).astype(BF16)
        dp_ref[:, COL_CG:COL_CG + D_CONV] = (dvv * u).astype(BF16)
        dp_ref[:, COL_U:COL_U + D_CONV] = (dvv * cg).astype(BF16)
        dcw_ref[0:1, :] += jnp.sum(dyc * vv2, axis=0, keepdims=True)
        dcw_ref[1:2, :] += jnp.sum(dyc * vv1, axis=0, keepdims=True)
        dcw_ref[2:3, :] += jnp.sum(dyc * vv, axis=0, keepdims=True)

        dist_cur, dist_prev, mask_cur, mask_up = _band_geometry()
        mask_prev = jnp.logical_and(mask_up, has_prev)
        mask_next = jnp.logical_and(mask_up, has_next)
        lane8 = jnp.where(lax.broadcasted_iota(jnp.int32, (8, 128), 0) == 0,
                          lax.broadcasted_iota(jnp.int32, (8, 128), 1), -1)
        dsk = jnp.zeros((8, 128), F32)
        for kv in range(N_SWA_KV):
            kc = pc_ref[:, COL_K + HEAD_DIM * kv:COL_K + HEAD_DIM * (kv + 1)]
            vc = pc_ref[:, COL_V + HEAD_DIM * kv:COL_V + HEAD_DIM * (kv + 1)]
            kp = pp_ref[:, COL_K + HEAD_DIM * kv:COL_K + HEAD_DIM * (kv + 1)]
            vp = pp_ref[:, COL_V + HEAD_DIM * kv:COL_V + HEAD_DIM * (kv + 1)]
            dk = jnp.zeros((BLOCK, HEAD_DIM), F32)
            dv = jnp.zeros((BLOCK, HEAD_DIM), F32)
            for gi in range(SWA_GROUP):
                hd = kv * SWA_GROUP + gi
                cq = COL_Q + HEAD_DIM * hd
                cy = D_CONV + HEAD_DIM * hd
                sink = sk_ref[0, hd]
                q = pc_ref[:, cq:cq + HEAD_DIM]
                do = dyc_ref[:, cy:cy + HEAD_DIM]
                delta = jnp.sum(do * yc_ref[:, cy:cy + HEAD_DIM], axis=-1, keepdims=True)
                lse_h = lc_ref[:, hd:hd + 1]
                dob = do.astype(BF16)
                pc_ = jnp.exp(_swa_scores(q, kc, SLOPES[hd], dist_cur, mask_cur) - lse_h)
                pp_ = jnp.exp(_swa_scores(q, kp, SLOPES[hd], dist_prev, mask_prev) - lse_h)
                dsc = (pc_ * (_dot_nt(dob, vc) - delta)).astype(BF16)
                dsp = (pp_ * (_dot_nt(dob, vp) - delta)).astype(BF16)
                dq = (_dot(dsc, kc) + _dot(dsp, kp)) * SCALE
                dp_ref[:, cq:cq + HEAD_DIM] = dq.astype(BF16)
                dsink = -jnp.sum(jnp.exp(sink - lse_h) * delta, axis=0, keepdims=True)
                dsk = dsk + jnp.where(lane8 == hd, dsink, 0.0)
                dv = dv + _dot_tn(pc_.astype(BF16), dob)
                dk = dk + _dot_tn(dsc, q) * SCALE
                qn = pn_ref[:, cq:cq + HEAD_DIM]
                don = dyn_ref[:, cy:cy + HEAD_DIM]
                deltan = jnp.sum(don * yn_ref[:, cy:cy + HEAD_DIM], axis=-1, keepdims=True)
                donb = don.astype(BF16)
                pn_ = jnp.exp(_swa_scores(qn, kc, SLOPES[hd], dist_prev, mask_next) - ln_ref[:, hd:hd + 1])
                dsn = (pn_ * (_dot_nt(donb, vc) - deltan)).astype(BF16)
                dv = dv + _dot_tn(pn_.astype(BF16), donb)
                dk = dk + _dot_tn(dsn, qn) * SCALE
            dp_ref[:, COL_K + HEAD_DIM * kv:COL_K + HEAD_DIM * (kv + 1)] = dk.astype(BF16)
            dp_ref[:, COL_V + HEAD_DIM * kv:COL_V + HEAD_DIM * (kv + 1)] = dv.astype(BF16)
        dsk_ref[...] += dsk

        for hm in range(N_MEM_HEADS):
            cq = COL_QM + HEAD_DIM * hm
            cy = D_CONV + D_SWA + HEAD_DIM * hm
            qm = pc_ref[:, cq:cq + HEAD_DIM]
            mk = mkv_ref[:, HEAD_DIM * hm:HEAD_DIM * (hm + 1)]
            mv = mkv_ref[:, D_MEMQ + HEAD_DIM * hm:D_MEMQ + HEAD_DIM * (hm + 1)]
            do = dyc_ref[:, cy:cy + HEAD_DIM]
            delta = jnp.sum(do * yc_ref[:, cy:cy + HEAD_DIM], axis=-1, keepdims=True)
            dob = do.astype(BF16)
            pr = jnp.exp(_dot_nt(qm, mk) * SCALE - lc_ref[:, N_SWA_HEADS + hm:N_SWA_HEADS + hm + 1])
            ds = (pr * (_dot_nt(dob, mv) - delta)).astype(BF16)
            dp_ref[:, cq:cq + HEAD_DIM] = (_dot(ds, mk) * SCALE).astype(BF16)
            dmkv_ref[:, HEAD_DIM * hm:HEAD_DIM * (hm + 1)] += _dot_tn(ds, qm) * SCALE
            dmkv_ref[:, D_MEMQ + HEAD_DIM * hm:D_MEMQ + HEAD_DIM * (hm + 1)] += _dot_tn(pr.astype(BF16), dob)

    cur = lambda i: (i, 0)
    prev = lambda i: (jnp.maximum(i - 1, 0), 0)
    nxt = lambda i: (jnp.minimum(i + 1, nb - 1), 0)
    const = lambda i: (0, 0)
    return _pcall(
        body, name="mix_core_bwd", grid=(nb,),
        in_specs=[pl.BlockSpec(memory_space=pltpu.SMEM),
                  pl.BlockSpec((BLOCK, D_IN), cur), pl.BlockSpec((BLOCK, D_IN), prev), pl.BlockSpec((BLOCK, D_IN), nxt),
                  pl.BlockSpec((BLOCK, D_MIX), cur), pl.BlockSpec((BLOCK, D_MIX), nxt),
                  pl.BlockSpec((BLOCK, D_MIX), cur), pl.BlockSpec((BLOCK, D_MIX), nxt),
                  pl.BlockSpec((BLOCK, 128), cur), pl.BlockSpec((BLOCK, 128), nxt),
                  pl.BlockSpec((m, 2 * D_MEMQ), const),
                  pl.BlockSpec((3, D_CONV), const)],
        out_specs=[pl.BlockSpec((BLOCK, D_IN), cur),
                   pl.BlockSpec((m, 2 * D_MEMQ), const),
                   pl.BlockSpec((8, D_CONV), const),
                   pl.BlockSpec((8, 128), const)],
        out_shape=[jax.ShapeDtypeStruct((t, D_IN), BF16),
                   jax.ShapeDtypeStruct((m, 2 * D_MEMQ), F32),
                   jax.ShapeDtypeStruct((8, D_CONV), F32),
                   jax.ShapeDtypeStruct((8, 128), F32)],
        compiler_params=_params(("arbitrary",)),
    )(sinks, p, p, p, dy, dy, y, y, lse, lse, mkv, convw)


def _group_norms(y):
    out = []
    for a, b in MIX_GROUPS:
        ys = y[:, a:b]
        r = _rstd(ys)
        out.append((ys * r, r))
    return out


def _mix_out_fwd(y, h, g, wout):
    t, d = h.shape
    tm = _tok_block(t)

    def body(y_ref, h_ref, g_ref, w_ref, ho_ref, mt_ref):
        yhat = jnp.concatenate([yh for yh, _ in _group_norms(y_ref[...])], axis=-1)
        mixed = yhat * g_ref[...]
        mt_ref[...] = mixed.T.astype(BF16)
        ho_ref[...] = h_ref[...] + _dot(mixed.astype(BF16), w_ref[...])

    return _pcall(
        body, name="mix_out_fwd", grid=(t // tm,),
        in_specs=[pl.BlockSpec((tm, D_MIX), lambda i: (i, 0)),
                  pl.BlockSpec((tm, d), lambda i: (i, 0)),
                  pl.BlockSpec((1, D_MIX), lambda i: (0, 0)),
                  pl.BlockSpec((D_MIX, d), lambda i: (0, 0))],
        out_specs=[pl.BlockSpec((tm, d), lambda i: (i, 0)),
                   pl.BlockSpec((D_MIX, tm), lambda i: (0, i))],
        out_shape=[jax.ShapeDtypeStruct((t, d), F32), jax.ShapeDtypeStruct((D_MIX, t), BF16)],
        compiler_params=_params(("parallel",)),
    )(y, h, g, wout)


def _mix_out_bwd(dho, y, g, wout, mt, dep):
    t, d = dho.shape
    tm = _tok_block(t)
    ni = t // tm

    def body(dho_ref, y_ref, g_ref, w_ref, mt_ref, dep_ref, dy_ref, dw_ref, dg_ref, acc_ref):
        i = pl.program_id(0)
        dhb = dho_ref[...].astype(BF16)
        dm = _dot_nt(dhb, w_ref[...])
        pw = _dot(mt_ref[...], dhb)
        gg = g_ref[...]
        dys = []
        dgs = []
        for (a, b), (yhat, r) in zip(MIX_GROUPS, _group_norms(y_ref[...])):
            dmg = dm[:, a:b]
            dgs.append(_sum8(dmg * yhat))
            dyh = dmg * gg[:, a:b]
            dys.append(r * (dyh - yhat * jnp.mean(dyh * yhat, axis=-1, keepdims=True)))
        dy_ref[...] = jnp.concatenate(dys, axis=-1)
        part = jnp.concatenate(dgs, axis=-1)

        @pl.when(i == 0)
        def _():
            acc_ref[...] = pw
            dg_ref[...] = part

        @pl.when(i > 0)
        def _():
            acc_ref[...] += pw
            dg_ref[...] += part

        @pl.when(i == ni - 1)
        def _():
            dw_ref[...] = acc_ref[...].astype(BF16)

    return _pcall(
        body, name="mix_out_bwd", grid=(ni,),
        in_specs=[pl.BlockSpec((tm, d), lambda i: (i, 0)),
                  pl.BlockSpec((tm, D_MIX), lambda i: (i, 0)),
                  pl.BlockSpec((1, D_MIX), lambda i: (0, 0)),
                  pl.BlockSpec((D_MIX, d), lambda i: (0, 0)),
                  pl.BlockSpec((D_MIX, tm), lambda i: (0, i)),
                  pl.BlockSpec(memory_space=pl.ANY)],
        out_specs=[pl.BlockSpec((tm, D_MIX), lambda i: (i, 0)),
                   pl.BlockSpec((D_MIX, d), lambda i: (0, 0)),
                   pl.BlockSpec((8, D_MIX), lambda i: (0, 0))],
        out_shape=[jax.ShapeDtypeStruct((t, D_MIX), F32),
                   jax.ShapeDtypeStruct((D_MIX, d), BF16),
                   jax.ShapeDtypeStruct((8, D_MIX), F32)],
        scratch_shapes=[pltpu.VMEM((D_MIX, d), F32)],
        compiler_params=_params(("arbitrary",)),
    )(dho, y, g, wout, mt, dep)


def _mix_proj_bwd(dp, dho, h, g, win_t, n):
    t, d = h.shape
    tm = _tok_block(t)
    ni = t // tm

    def body(dp_ref, dho_ref, h_ref, g_ref, w_ref, n_ref, dh_ref, dw_ref, dg_ref, acc_ref):
        i = pl.program_id(0)
        dpb = dp_ref[...]
        dn = _dot(dpb, w_ref[...])
        pw = _dot_tn(dpb, n_ref[...])
        hh = h_ref[...]
        r = _rstd(hh)
        xhat = hh * r
        dxh = dn * g_ref[...]
        dh_ref[...] = dho_ref[...] + r * (dxh - xhat * jnp.mean(dxh * xhat, axis=-1, keepdims=True))
        part = _sum8(dn * xhat)

        @pl.when(i == 0)
        def _():
            acc_ref[...] = pw
            dg_ref[...] = part

        @pl.when(i > 0)
        def _():
            acc_ref[...] += pw
            dg_ref[...] += part

        @pl.when(i == ni - 1)
        def _():
            dw_ref[...] = acc_ref[...].astype(BF16)

    return _pcall(
        body, name="mix_proj_bwd", grid=(ni,),
        in_specs=[pl.BlockSpec((tm, D_IN), lambda i: (i, 0)),
                  pl.BlockSpec((tm, d), lambda i: (i, 0)),
                  pl.BlockSpec((tm, d), lambda i: (i, 0)),
                  pl.BlockSpec((1, d), lambda i: (0, 0)),
                  pl.BlockSpec((D_IN, d), lambda i: (0, 0)),
                  pl.BlockSpec((tm, d), lambda i: (i, 0))],
        out_specs=[pl.BlockSpec((tm, d), lambda i: (i, 0)),
                   pl.BlockSpec((D_IN, d), lambda i: (0, 0)),
                   pl.BlockSpec((8, d), lambda i: (0, 0))],
        out_shape=[jax.ShapeDtypeStruct((t, d), F32),
                   jax.ShapeDtypeStruct((D_IN, d), BF16),
                   jax.ShapeDtypeStruct((8, d), F32)],
        scratch_shapes=[pltpu.VMEM((D_IN, d), F32)],
        compiler_params=_params(("arbitrary",)),
    )(dp, dho, h, g, win_t, n)


def _final_loss(h, g, tgt):
    t, d = h.shape
    tm = _tok_block(t)

    def body(h_ref, g_ref, t_ref, dh_ref, ls_ref, dg_ref):
        i = pl.program_id(0)
        hh = h_ref[...]
        r = _rstd(hh)
        xhat = hh * r
        gg = g_ref[...]
        err = xhat * gg - t_ref[...]
        dy = err * (1.0 / d)
        dxh = dy * gg
        dh_ref[...] = r * (dxh - xhat * jnp.mean(dxh * xhat, axis=-1, keepdims=True))
        lpart = _sum8(err * err)
        gpart = _sum8(dy * xhat)

        @pl.when(i == 0)
        def _():
            ls_ref[...] = lpart
            dg_ref[...] = gpart

        @pl.when(i > 0)
        def _():
            ls_ref[...] += lpart
            dg_ref[...] += gpart

    return _pcall(
        body, name="final_loss", grid=(t // tm,),
        in_specs=[pl.BlockSpec((tm, d), lambda i: (i, 0)),
                  pl.BlockSpec((1, d), lambda i: (0, 0)),
                  pl.BlockSpec((tm, d), lambda i: (i, 0))],
        out_specs=[pl.BlockSpec((tm, d), lambda i: (i, 0)),
                   pl.BlockSpec((8, d), lambda i: (0, 0)),
                   pl.BlockSpec((8, d), lambda i: (0, 0))],
        out_shape=[jax.ShapeDtypeStruct((t, d), F32),
                   jax.ShapeDtypeStruct((8, d), F32),
                   jax.ShapeDtypeStruct((8, d), F32)],
        compiler_params=_params(("arbitrary",)),
    )(h, g, tgt)


def _position():
    return lax.axis_index("x"), lax.axis_index("y"), lax.axis_index("c")


def _flip(v, bit):
    return 1 - v if bit else v


def _peer(k):
    x, y, c = _position()
    return _flip(x, k & 4), _flip(y, k & 2), _flip(c, k & 1)


def _slot(px, py, pc):
    return 4 * px + 2 * py + pc


def _handshake(peers):
    barrier = pltpu.get_barrier_semaphore()
    for peer in peers:
        pl.semaphore_signal(barrier, inc=1, device_id=peer, device_id_type=MESH)
    pl.semaphore_wait(barrier, len(peers))


def _sequencer_call(body, name, collective_id, out_type, scratch_types, operands):
    return pl.kernel(
        body, out_type=out_type, mesh=plsc.ScalarSubcoreMesh(axis_name="sequencer", num_cores=1), name=name,
        scratch_types=scratch_types, compiler_params=pltpu.CompilerParams(collective_id=collective_id),
    )(*operands)


def _all_gather(shards, name, collective_id):
    nt = len(shards)

    def body(*refs):
        xs = refs[:nt]
        outs = refs[nt:2 * nt]
        send_sems, recv_sems, local_sems = refs[2 * nt:]
        x, y, c = _position()
        me, sibling = (x, y, c), (x, y, 1 - c)
        chips = [(1 - x, y), (x, 1 - y), (1 - x, 1 - y)]
        _handshake([sibling] + [(*chip, c) for chip in chips])

        def copy(t, k, block, to, src=None):
            dst = outs[t].at[_slot(*block)]
            return pltpu.make_async_remote_copy(
                src_ref=dst if src is None else src, dst_ref=dst,
                send_sem=send_sems.at[t, k], recv_sem=recv_sems.at[t, k],
                device_id=to, device_id_type=MESH)

        mine = [pltpu.make_async_copy(xs[t], outs[t].at[_slot(*me)], local_sems.at[t]) for t in range(nt)]
        for cp in mine:
            cp.start()
        first = []
        for t in range(nt):
            first.append(copy(t, 0, me, sibling, src=xs[t]))
            first += [copy(t, 1 + j, me, (*chip, c), src=xs[t]) for j, chip in enumerate(chips)]
        for cp in first:
            cp.start()
        passed = []
        for j, chip in enumerate(chips):
            for t in range(nt):
                copy(t, 1 + j, (*chip, c), me).wait_recv()
                fwd = copy(t, 4 + j, (*chip, c), sibling)
                fwd.start()
                passed.append(fwd)
        for t in range(nt):
            copy(t, 0, sibling, me).wait_recv()
            for j, chip in enumerate(chips):
                copy(t, 4 + j, (*chip, 1 - c), me).wait_recv()
        for cp in first + passed:
            cp.wait_send()
        for cp in mine:
            cp.wait()

    return _sequencer_call(
        body, name, collective_id,
        out_type=[jax.ShapeDtypeStruct((N_DEV,) + s.shape, s.dtype) for s in shards],
        scratch_types=[pltpu.SemaphoreType.DMA((nt, 7)), pltpu.SemaphoreType.DMA((nt, 7)),
                       pltpu.SemaphoreType.DMA((nt,))],
        operands=shards)


def _scatter_partials(partials, name, collective_id):
    nt = len(partials)

    def body(*refs):
        srcs = refs[:nt]
        outs = refs[nt:2 * nt]
        send_sems, recv_sems, local_sems = refs[2 * nt:]
        x, y, c = _position()
        _handshake([_peer(k) for k in range(1, N_DEV)])

        def copy(t, k):
            peer = _peer(k)
            return pltpu.make_async_remote_copy(
                src_ref=srcs[t].at[_slot(*peer)], dst_ref=outs[t].at[k],
                send_sem=send_sems.at[t, k - 1], recv_sem=recv_sems.at[t, k - 1],
                device_id=peer, device_id_type=MESH)

        mine = [pltpu.make_async_copy(srcs[t].at[_slot(x, y, c)], outs[t].at[0], local_sems.at[t]) for t in range(nt)]
        for cp in mine:
            cp.start()
        sent = [copy(t, k) for k in range(1, N_DEV) for t in range(nt)]
        for cp in sent:
            cp.start()
        for cp in sent:
            cp.wait_recv()
        for cp in sent:
            cp.wait_send()
        for cp in mine:
            cp.wait()

    return _sequencer_call(
        body, name, collective_id,
        out_type=[jax.ShapeDtypeStruct(p.shape, p.dtype) for p in partials],
        scratch_types=[pltpu.SemaphoreType.DMA((nt, 7)), pltpu.SemaphoreType.DMA((nt, 7)),
                       pltpu.SemaphoreType.DMA((nt,))],
        operands=partials)


def _all_reduce_rows(v):
    nv, _, w = v.shape

    def body(v_ref, out_ref, gath_ref, send_sems, recv_sems):
        x, y, c = _position()
        me = _slot(x, y, c)

        def copy(k):
            return pltpu.make_async_remote_copy(
                src_ref=v_ref, dst_ref=gath_ref.at[me],
                send_sem=send_sems.at[k - 1], recv_sem=recv_sems.at[k - 1],
                device_id=_peer(k), device_id_type=MESH)

        def arrival(k):
            return pltpu.make_async_remote_copy(
                src_ref=v_ref, dst_ref=gath_ref.at[_slot(*_peer(k))],
                send_sem=send_sems.at[k - 1], recv_sem=recv_sems.at[k - 1],
                device_id=_peer(k), device_id_type=MESH)

        sent = [copy(k) for k in range(1, N_DEV)]
        for cp in sent:
            cp.start()
        gath_ref[me] = v_ref[...]
        for k in range(1, N_DEV):
            arrival(k).wait_recv()
        for cp in sent:
            cp.wait_send()
        total = gath_ref[0]
        for s in range(1, N_DEV):
            total = total + gath_ref[s]
        out_ref[...] = jnp.sum(total, axis=1)

    vmem = pl.BlockSpec(memory_space=pltpu.VMEM)
    return _pcall(
        body, name="all_reduce_rows",
        in_specs=[vmem], out_specs=vmem,
        out_shape=jax.ShapeDtypeStruct((nv, w), F32),
        scratch_shapes=[pltpu.VMEM((N_DEV, nv, 8, w), F32),
                        pltpu.SemaphoreType.DMA((7,)), pltpu.SemaphoreType.DMA((7,))],
    )(v)


def _adamw_math(w, g, m, v):
    m2 = ADAM_B1 * m + (1.0 - ADAM_B1) * g
    v2 = ADAM_B2 * v + (1.0 - ADAM_B2) * (g * g)
    m_hat = m2 / (1.0 - ADAM_B1 ** ADAM_STEP)
    v_hat = v2 / (1.0 - ADAM_B2 ** ADAM_STEP)
    delta = -ADAM_LR * (m_hat / (jnp.sqrt(v_hat) + ADAM_EPS) + ADAM_WD * w)
    return delta, m2, v2


def _row_block(r):
    for cand in (256, 176, 128):
        if r % cand == 0:
            return cand
    return r


def _adamw_sharded(recv0, recv1, w, m, v):
    _, r, c = recv0.shape
    tr = _row_block(r)
    nr = r // tr

    def body(r0_ref, r1_ref, w_ref, m_ref, v_ref, g_ref, d_ref, m2_ref, v2_ref):
        layer = pl.program_id(0)

        def total(ref):
            acc = ref[0].astype(F32)
            for k in range(1, N_DEV):
                acc = acc + ref[k].astype(F32)
            return acc

        g = jnp.where(layer == 0, total(r0_ref), total(r1_ref))
        delta, m2, v2 = _adamw_math(w_ref[0], g, m_ref[0], v_ref[0])
        g_ref[0] = g
        d_ref[0] = delta
        m2_ref[0] = m2
        v2_ref[0] = v2

    shard = pl.BlockSpec((1, tr, c), lambda l, i: (l, i, 0))
    out = jax.ShapeDtypeStruct((2, r, c), F32)
    return _pcall(
        body, name="adamw_sharded", grid=(2, nr),
        in_specs=[pl.BlockSpec((N_DEV, tr, c), lambda l, i: (0, jnp.where(l == 0, i, nr - 1), 0)),
                  pl.BlockSpec((N_DEV, tr, c), lambda l, i: (0, jnp.where(l == 1, i, 0), 0)),
                  shard, shard, shard],
        out_specs=[shard, shard, shard, shard],
        out_shape=[out, out, out, out],
        compiler_params=_params(("arbitrary", "arbitrary")),
    )(recv0, recv1, w, m, v)


def _adamw_small(w, g, m, v):
    def body(w_ref, g_ref, m_ref, v_ref, d_ref, m2_ref, v2_ref):
        delta, m2, v2 = _adamw_math(w_ref[...], g_ref[...], m_ref[...], v_ref[...])
        d_ref[...] = delta
        m2_ref[...] = m2
        v2_ref[...] = v2

    spec = pl.BlockSpec(w.shape, lambda i: (0, 0))
    out = jax.ShapeDtypeStruct(w.shape, F32)
    return _pcall(
        body, name="adamw_small", grid=(1,),
        in_specs=[spec] * 4, out_specs=[spec] * 3, out_shape=[out] * 3,
        compiler_params=_params(("arbitrary",)),
    )(w, g, m, v)


def _pack(arrs):
    flat = jnp.concatenate([a.reshape(-1) for a in arrs])
    n = flat.shape[0]
    rows = -(-n // 1024) * 8
    return jnp.pad(flat, (0, rows * 128 - n)).reshape(rows, 128)


def _unpack(packed, like):
    flat = packed.reshape(-1)
    out, off = [], 0
    for a in like:
        out.append(flat[off:off + a.size].reshape(a.shape))
        off += a.size
    return out


def kernel(x, mem, g_ffn1, w_ffn1_up, w_ffn1_down, g_mix, w_in, conv_w, sinks, g_mem, w_mem_kv, g_grp, w_out, g_ffn2, w_ffn2_up, w_ffn2_down, g_final, loss_target, m_g_ffn1, m_w_ffn1_up, m_w_ffn1_down, m_g_mix, m_w_in, m_conv_w, m_sinks, m_g_mem, m_w_mem_kv, m_g_grp, m_w_out, m_g_ffn2, m_w_ffn2_up, m_w_ffn2_down, m_g_final, v_g_ffn1, v_w_ffn1_up, v_w_ffn1_down, v_g_mix, v_w_in, v_conv_w, v_sinks, v_g_mem, v_w_mem_kv, v_g_grp, v_w_out, v_g_ffn2, v_w_ffn2_up, v_w_ffn2_down, v_g_final):
    depth = g_ffn1.shape[0]
    t, d = x.shape[1], x.shape[2]
    width = max(d, D_MIX)
    nj = N_DEV // 2
    me = _slot(*_position())
    conv_shard = conv_w.shape[2]

    xin, memin, tgt = x[0], mem[0], loss_target[0]

    conv_tile = jnp.zeros((depth * 8, 128), F32).at[:, :conv_shard].set(
        jnp.pad(conv_w, ((0, 0), (0, 8 - conv_w.shape[1]), (0, 0))).reshape(depth * 8, conv_shard))
    tr = lambda a: jnp.swapaxes(a, -1, -2)
    bf = lambda a: a.astype(BF16)
    weights = []
    collective_id = 0
    for l in range(depth):
        groups = [[bf(tr(w_ffn1_up[l])), bf(w_ffn1_down[l])] + ([conv_tile] if l == 0 else []),
                  [bf(tr(w_in[l])), bf(w_mem_kv[l]), bf(w_out[l])],
                  [bf(tr(w_ffn2_up[l])), bf(w_ffn2_down[l])]]
        full = []
        for gi, shards in enumerate(groups):
            full.append(_all_gather(shards, f"all_gather_l{l}_g{gi}", collective_id))
            collective_id += 1
        if l == 0:
            conv_full = full[0][2].reshape(N_DEV, depth, 8, 128)[:, :, :3, :conv_shard]
            conv_full = conv_full.transpose(1, 2, 0, 3).reshape(depth, 3, N_DEV * conv_shard)
        weights.append(dict(
            up1=full[0][0].reshape(2, -1, d), dn1=full[0][1].reshape(-1, d),
            win=full[1][0].reshape(D_IN, d), wkv=full[1][1].reshape(d, 2 * D_MEMQ), wout=full[1][2].reshape(D_MIX, d),
            up2=full[2][0].reshape(2, -1, d), dn2=full[2][1].reshape(-1, d)))

    row = lambda a: a.reshape(1, -1)

    h = xin
    saved = []
    for l in range(depth):
        wl = weights[l]
        s = dict(h0=h)
        h, s["gu1"], s["n1"] = _ffn_fwd(h, row(g_ffn1[l]), wl["up1"], wl["dn1"])
        s["h1"] = h
        s["p"], s["n_mix"] = _mix_proj_fwd(h, row(g_mix[l]), wl["win"])
        s["mkv"], s["nt_mem"] = _memkv_fwd(memin, row(g_mem[l]), wl["wkv"], s["p"])
        s["y"], s["lse"] = _mix_core_fwd(s["p"], s["mkv"], conv_full[l], row(sinks[l]))
        h, s["mt"] = _mix_out_fwd(s["y"], h, row(g_grp[l]), wl["wout"])
        s["h2"] = h
        h, s["gu2"], s["n2"] = _ffn_fwd(h, row(g_ffn2[l]), wl["up2"], wl["dn2"])
        saved.append(s)

    dh, loss_part, dg_final = _final_loss(h, row(g_final), tgt)
    loss = lax.psum(0.5 * jnp.sum(loss_part) / d, ("x", "y", "c"))

    small = {}
    recv = [None] * depth
    dep = loss_part
    for l in reversed(range(depth)):
        wl, s = weights[l], saved[l]
        dh, agu, dyb, small["g_ffn2", l] = _ffn_bwd_act(dh, s["h2"], row(g_ffn2[l]), s["gu2"], wl["up2"], wl["dn2"], dep)
        dup2, ddn2 = _ffn_bwd_w(agu, dyb, s["n2"])
        got_ffn2 = _scatter_partials([dup2.reshape(N_DEV, -1, d), ddn2.reshape(N_DEV, -1, d)],
                                     f"scatter_grads_l{l}_ffn2", collective_id)
        dy, dwout, small["g_grp", l] = _mix_out_bwd(dh, s["y"], row(g_grp[l]), wl["wout"], s["mt"], ddn2)
        dp, dmkv, small["conv_w", l], small["sinks", l] = _mix_core_bwd(
            s["p"], dy, s["y"], s["lse"], s["mkv"], conv_full[l], row(sinks[l]))
        dwkv, small["g_mem", l] = _memkv_bwd(dmkv, memin, row(g_mem[l]), wl["wkv"], s["nt_mem"])
        dh, dwin, small["g_mix", l] = _mix_proj_bwd(dp, dh, s["h1"], row(g_mix[l]), wl["win"], s["n_mix"])
        got_mix = _scatter_partials([dwin.reshape(N_DEV, -1, d), dwkv.reshape(N_DEV, -1, 2 * D_MEMQ),
                                     dwout.reshape(N_DEV, -1, d)], f"scatter_grads_l{l}_mix", collective_id + 1)
        dh, agu, dyb, small["g_ffn1", l] = _ffn_bwd_act(dh, s["h0"], row(g_ffn1[l]), s["gu1"], wl["up1"], wl["dn1"], dwkv)
        dup1, ddn1 = _ffn_bwd_w(agu, dyb, s["n1"])
        got_ffn1 = _scatter_partials([dup1.reshape(N_DEV, -1, d), ddn1.reshape(N_DEV, -1, d)],
                                     f"scatter_grads_l{l}_ffn1", collective_id + 2)
        collective_id += 3
        dep = ddn1
        recv[l] = dict(w_ffn2_up=got_ffn2[0], w_ffn2_down=got_ffn2[1], w_in=got_mix[0], w_mem_kv=got_mix[1],
                       w_out=got_mix[2], w_ffn1_up=got_ffn1[0], w_ffn1_down=got_ffn1[1])
    grad_x = dh[None]

    sharded = {}
    big = [("w_ffn2_up", w_ffn2_up, m_w_ffn2_up, v_w_ffn2_up, True), ("w_ffn2_down", w_ffn2_down, m_w_ffn2_down, v_w_ffn2_down, False),
           ("w_in", w_in, m_w_in, v_w_in, True), ("w_mem_kv", w_mem_kv, m_w_mem_kv, v_w_mem_kv, False),
           ("w_out", w_out, m_w_out, v_w_out, False), ("w_ffn1_up", w_ffn1_up, m_w_ffn1_up, v_w_ffn1_up, True),
           ("w_ffn1_down", w_ffn1_down, m_w_ffn1_down, v_w_ffn1_down, False)]
    for name, w, m, v, transposed in big:
        if transposed:
            res = _adamw_sharded(recv[0][name], recv[depth - 1][name], tr(w), tr(m), tr(v))
            sharded[name] = tuple(tr(r) for r in res)
        else:
            sharded[name] = tuple(_adamw_sharded(recv[0][name], recv[depth - 1][name], w, m, v))

    def lanes(a):
        return jnp.pad(a, ((0, 0), (0, width - a.shape[1])))

    def first_row(a):
        return lanes(jnp.pad(a, ((0, 8 - a.shape[0]), (0, 0))))

    vec_names = ["g_ffn1", "g_mix", "g_mem", "g_grp", "g_ffn2", "sinks"]
    tiles = [lanes(small[n, l]) for n in vec_names for l in range(depth)]
    tiles += [first_row(small["conv_w", l][k:k + 1]) for l in range(depth) for k in range(3)]
    tiles.append(lanes(dg_final))
    n_real = len(tiles)
    tiles += [jnp.zeros((8, width), F32)] * (-n_real % 8)
    summed = _all_reduce_rows(jnp.stack(tiles))

    def vec(n, wd):
        return jnp.stack([summed[vec_names.index(n) * depth + l, :wd] for l in range(depth)])

    conv_base = len(vec_names) * depth
    conv_grad = jnp.stack([jnp.stack([summed[conv_base + 3 * l + k, :D_CONV] for k in range(3)]) for l in range(depth)])
    grads_small = {
        "g_ffn1": vec("g_ffn1", d), "g_mix": vec("g_mix", d), "g_mem": vec("g_mem", d),
        "g_grp": vec("g_grp", D_MIX), "g_ffn2": vec("g_ffn2", d), "sinks": vec("sinks", N_SWA_HEADS),
        "conv_w": lax.dynamic_slice_in_dim(conv_grad, me * conv_shard, conv_shard, axis=2),
        "g_final": summed[n_real - 1, :d],
    }
    small_w = [("g_ffn1", g_ffn1, m_g_ffn1, v_g_ffn1), ("g_mix", g_mix, m_g_mix, v_g_mix),
               ("conv_w", conv_w, m_conv_w, v_conv_w), ("sinks", sinks, m_sinks, v_sinks),
               ("g_mem", g_mem, m_g_mem, v_g_mem), ("g_grp", g_grp, m_g_grp, v_g_grp),
               ("g_ffn2", g_ffn2, m_g_ffn2, v_g_ffn2), ("g_final", g_final, m_g_final, v_g_final)]
    like = [w for _, w, _, _ in small_w]
    packed = _adamw_small(_pack(like), _pack([grads_small[n] for n, _, _, _ in small_w]),
                          _pack([m for _, _, m, _ in small_w]), _pack([v for _, _, _, v in small_w]))
    small_out = {n: (grads_small[n], dl, m2, v2)
                 for (n, _, _, _), dl, m2, v2 in zip(small_w, *[_unpack(pk, like) for pk in packed])}

    order = ["g_ffn1", "w_ffn1_up", "w_ffn1_down", "g_mix", "w_in", "conv_w", "sinks", "g_mem", "w_mem_kv", "g_grp",
             "w_out", "g_ffn2", "w_ffn2_up", "w_ffn2_down", "g_final"]
    results = {**sharded, **small_out}
    outs = [loss, grad_x]
    for part in range(4):
        outs += [results[n][part] for n in order]
    return tuple(outs)
```

```python
import numpy as np
import jax
import jax.numpy as jnp
from jax import lax
from jax.experimental import pallas as pl
from jax.experimental.pallas import tpu as pltpu
from jax.experimental.pallas import tpu_sc as plsc

F32 = jnp.float32
BF16 = jnp.bfloat16

N_DEV = 8
EPS = 1e-6
N_SWA_HEADS = 8
N_SWA_KV = 2
SWA_GROUP = N_SWA_HEADS // N_SWA_KV
HEAD_DIM = 64
N_MEM_HEADS = 4
D_CONV = 256
BLOCK = 128
D_SWA = N_SWA_HEADS * HEAD_DIM
D_KV = N_SWA_KV * HEAD_DIM
D_MEMQ = N_MEM_HEADS * HEAD_DIM
D_MIX = D_CONV + D_SWA + D_MEMQ
D_IN = 3 * D_CONV + D_SWA + 2 * D_KV + D_MEMQ
COL_BG, COL_CG, COL_U = 0, D_CONV, 2 * D_CONV
COL_Q = 3 * D_CONV
COL_K = COL_Q + D_SWA
COL_V = COL_K + D_KV
COL_QM = COL_V + D_KV
MIX_GROUPS = ((0, D_CONV), (D_CONV, D_CONV + D_SWA), (D_CONV + D_SWA, D_MIX))
SLOPES = tuple(2.0 ** (-8.0 * (i + 1) / N_SWA_HEADS) for i in range(N_SWA_HEADS))
SCALE = HEAD_DIM ** -0.5
NEG = -1e30

ADAM_LR = 0.001
ADAM_B1 = 0.9
ADAM_B2 = 0.999
ADAM_EPS = 1e-08
ADAM_WD = 0.01
ADAM_STEP = 10

V7X_VMEM_BYTES = 64 * 1024 * 1024
VMEM_LIMIT = (V7X_VMEM_BYTES * 3) // 4
MESH = pl.DeviceIdType.MESH


def _pcall(body, **kw):
    return pl.pallas_call(body, **kw)


def _params(sem=None, vmem=VMEM_LIMIT):
    return pltpu.CompilerParams(dimension_semantics=sem, vmem_limit_bytes=vmem)


def _dot(a, b):
    return lax.dot_general(a, b, (((1,), (0,)), ((), ())), preferred_element_type=F32)


def _dot_nt(a, b):
    return lax.dot_general(a, b, (((1,), (1,)), ((), ())), preferred_element_type=F32)


def _dot_tn(a, b):
    return lax.dot_general(a, b, (((0,), (0,)), ((), ())), preferred_element_type=F32)


def _rstd(x):
    return lax.rsqrt(jnp.mean(x * x, axis=-1, keepdims=True) + EPS)


def _sigmoid(x):
    return 1.0 / (1.0 + jnp.exp(-x))


def _sum8(x):
    r, w = x.shape
    return jnp.sum(x.reshape(r // 8, 8, w), axis=0)


def _tok_block(t, rows=512):
    return min(rows, t)


def _feat_block(f):
    return f // (N_DEV // 2)


def _ffn_fwd(h, g, wup_t, wdn):
    t, d = h.shape
    f = wdn.shape[0]
    tm, tf = _tok_block(t), _feat_block(f)
    ni, nj = t // tm, f // tf

    def body(h_ref, g_ref, wup_ref, wdn_ref, ho_ref, gu_ref, n_ref, nt_ref, acc_ref):
        j = pl.program_id(1)

        @pl.when(j == 0)
        def _():
            hh = h_ref[...]
            n = hh * _rstd(hh) * g_ref[...]
            n_ref[...] = n.astype(BF16)
            nt_ref[...] = n.T.astype(BF16)
            acc_ref[...] = jnp.zeros_like(acc_ref)

        nt = nt_ref[...]
        gate = _dot(wup_ref[0], nt)
        up = _dot(wup_ref[1], nt)
        gu_ref[0] = gate.astype(BF16)
        gu_ref[1] = up.astype(BF16)
        a = gate * _sigmoid(gate) * up
        acc_ref[...] += _dot_tn(a.astype(BF16), wdn_ref[...])

        @pl.when(j == nj - 1)
        def _():
            ho_ref[...] = h_ref[...] + 0.5 * acc_ref[...]

    return _pcall(
        body, name="ffn_fwd", grid=(ni, nj),
        in_specs=[pl.BlockSpec((tm, d), lambda i, j: (i, 0)),
                  pl.BlockSpec((1, d), lambda i, j: (0, 0)),
                  pl.BlockSpec((2, tf, d), lambda i, j: (0, j, 0)),
                  pl.BlockSpec((tf, d), lambda i, j: (j, 0))],
        out_specs=[pl.BlockSpec((tm, d), lambda i, j: (i, 0)),
                   pl.BlockSpec((2, tf, tm), lambda i, j: (0, j, i)),
                   pl.BlockSpec((tm, d), lambda i, j: (i, 0))],
        out_shape=[jax.ShapeDtypeStruct((t, d), F32),
                   jax.ShapeDtypeStruct((2, f, t), BF16),
                   jax.ShapeDtypeStruct((t, d), BF16)],
        scratch_shapes=[pltpu.VMEM((d, tm), BF16), pltpu.VMEM((tm, d), F32)],
        compiler_params=_params(("parallel", "arbitrary")),
    )(h, g, wup_t, wdn)


def _ffn_bwd_act(dho, h, g, gu, wup_t, wdn, dep):
    t, d = h.shape
    f = wdn.shape[0]
    tm, tf = _tok_block(t), _feat_block(f)
    ni, nj = t // tm, f // tf

    def body(dho_ref, h_ref, g_ref, gu_ref, wup_ref, wdn_ref, dep_ref, dh_ref, agu_ref, dyb_ref, dg_ref, dyt_ref, acc_ref):
        i = pl.program_id(0)
        j = pl.program_id(1)

        @pl.when(j == 0)
        def _():
            dy0 = 0.5 * dho_ref[...]
            dyb_ref[...] = dy0.astype(BF16)
            dyt_ref[...] = dy0.T.astype(BF16)

        da = _dot(wdn_ref[...], dyt_ref[...])
        gate = gu_ref[0].astype(F32)
        up = gu_ref[1].astype(F32)
        sg = _sigmoid(gate)
        silu = gate * sg
        dgate = (da * up * (sg * (1.0 + gate * (1.0 - sg)))).astype(BF16)
        dup = (da * silu).astype(BF16)
        agu_ref[0] = (silu * up).astype(BF16)
        agu_ref[1] = dgate
        agu_ref[2] = dup
        dn = _dot_tn(dgate, wup_ref[0]) + _dot_tn(dup, wup_ref[1])

        @pl.when(j == 0)
        def _():
            acc_ref[...] = dn

        @pl.when(j > 0)
        def _():
            acc_ref[...] += dn

        @pl.when(j == nj - 1)
        def _():
            hh = h_ref[...]
            r = _rstd(hh)
            xhat = hh * r
            dnf = acc_ref[...]
            dxh = dnf * g_ref[...]
            dh_ref[...] = dho_ref[...] + r * (dxh - xhat * jnp.mean(dxh * xhat, axis=-1, keepdims=True))
            part = _sum8(dnf * xhat)

            @pl.when(i == 0)
            def _():
                dg_ref[...] = part

            @pl.when(i > 0)
            def _():
                dg_ref[...] += part

    return _pcall(
        body, name="ffn_bwd_act", grid=(ni, nj),
        in_specs=[pl.BlockSpec((tm, d), lambda i, j: (i, 0)),
                  pl.BlockSpec((tm, d), lambda i, j: (i, 0)),
                  pl.BlockSpec((1, d), lambda i, j: (0, 0)),
                  pl.BlockSpec((2, tf, tm), lambda i, j: (0, j, i)),
                  pl.BlockSpec((2, tf, d), lambda i, j: (0, j, 0)),
                  pl.BlockSpec((tf, d), lambda i, j: (j, 0)),
                  pl.BlockSpec(memory_space=pl.ANY)],
        out_specs=[pl.BlockSpec((tm, d), lambda i, j: (i, 0)),
                   pl.BlockSpec((3, tf, tm), lambda i, j: (0, j, i)),
                   pl.BlockSpec((tm, d), lambda i, j: (i, 0)),
                   pl.BlockSpec((8, d), lambda i, j: (0, 0))],
        out_shape=[jax.ShapeDtypeStruct((t, d), F32),
                   jax.ShapeDtypeStruct((3, f, t), BF16),
                   jax.ShapeDtypeStruct((t, d), BF16),
                   jax.ShapeDtypeStruct((8, d), F32)],
        scratch_shapes=[pltpu.VMEM((d, tm), BF16), pltpu.VMEM((tm, d), F32)],
        compiler_params=_params(("arbitrary", "arbitrary")),
    )(dho, h, g, gu, wup_t, wdn, dep)


def _ffn_bwd_w(agu, dyb, n):
    _, f, t = agu.shape
    d = dyb.shape[1]
    tm, tf = _tok_block(t, 1024), _feat_block(f)
    ni, nj = t // tm, f // tf

    def body(agu_ref, dyb_ref, n_ref, dwup_ref, dwdn_ref, accg_ref, accu_ref, accd_ref):
        i = pl.program_id(1)
        nb = n_ref[...]
        pg = _dot(agu_ref[1], nb)
        pu = _dot(agu_ref[2], nb)
        pd = _dot(agu_ref[0], dyb_ref[...])

        @pl.when(i == 0)
        def _():
            accg_ref[...] = pg
            accu_ref[...] = pu
            accd_ref[...] = pd

        @pl.when(i > 0)
        def _():
            accg_ref[...] += pg
            accu_ref[...] += pu
            accd_ref[...] += pd

        @pl.when(i == ni - 1)
        def _():
            dwup_ref[0] = accg_ref[...].astype(BF16)
            dwup_ref[1] = accu_ref[...].astype(BF16)
            dwdn_ref[...] = accd_ref[...].astype(BF16)

    return _pcall(
        body, name="ffn_bwd_w", grid=(nj, ni),
        in_specs=[pl.BlockSpec((3, tf, tm), lambda j, i: (0, j, i)),
                  pl.BlockSpec((tm, d), lambda j, i: (i, 0)),
                  pl.BlockSpec((tm, d), lambda j, i: (i, 0))],
        out_specs=[pl.BlockSpec((2, tf, d), lambda j, i: (0, j, 0)),
                   pl.BlockSpec((tf, d), lambda j, i: (j, 0))],
        out_shape=[jax.ShapeDtypeStruct((2, f, d), BF16),
                   jax.ShapeDtypeStruct((f, d), BF16)],
        scratch_shapes=[pltpu.VMEM((tf, d), F32), pltpu.VMEM((tf, d), F32), pltpu.VMEM((tf, d), F32)],
        compiler_params=_params(("parallel", "arbitrary")),
    )(agu, dyb, n)


N_HEADS = N_SWA_HEADS + N_MEM_HEADS


def _q_col(hd):
    return COL_Q + HEAD_DIM * hd if hd < N_SWA_HEADS else COL_QM + HEAD_DIM * (hd - N_SWA_HEADS)


def _mix_proj_fwd(h, g, win_t):
    t, d = h.shape
    tm = _tok_block(t)

    def body(h_ref, g_ref, win_ref, p_ref, n_ref, qh_ref):
        hh = h_ref[...]
        n = (hh * _rstd(hh) * g_ref[...]).astype(BF16)
        n_ref[...] = n
        proj = _dot_nt(n, win_ref[...])
        p_ref[...] = proj.astype(BF16)
        for hd in range(N_HEADS):
            c0 = _q_col(hd)
            qh_ref[hd] = (proj[:, c0:c0 + HEAD_DIM] * SCALE).astype(BF16)

    return _pcall(
        body, name="mix_proj_fwd", grid=(t // tm,),
        in_specs=[pl.BlockSpec((tm, d), lambda i: (i, 0)),
                  pl.BlockSpec((1, d), lambda i: (0, 0)),
                  pl.BlockSpec((D_IN, d), lambda i: (0, 0))],
        out_specs=[pl.BlockSpec((tm, D_IN), lambda i: (i, 0)),
                   pl.BlockSpec((tm, d), lambda i: (i, 0)),
                   pl.BlockSpec((N_HEADS, tm, HEAD_DIM), lambda i: (0, i, 0))],
        out_shape=[jax.ShapeDtypeStruct((t, D_IN), BF16), jax.ShapeDtypeStruct((t, d), BF16),
                   jax.ShapeDtypeStruct((N_HEADS, t, HEAD_DIM), BF16)],
        compiler_params=_params(("parallel",)),
    )(h, g, win_t)


def _memkv_fwd(mem, g, wkv, dep):
    m, d = mem.shape

    def body(mem_ref, g_ref, w_ref, dep_ref, mkv_ref, nt_ref):
        mm = mem_ref[...]
        n = mm * _rstd(mm) * g_ref[...]
        nt_ref[...] = n.T.astype(BF16)
        mkv_ref[...] = _dot(n.astype(BF16), w_ref[...]).astype(BF16)

    return _pcall(
        body, name="memkv_fwd", grid=(1,),
        in_specs=[pl.BlockSpec((m, d), lambda i: (0, 0)),
                  pl.BlockSpec((1, d), lambda i: (0, 0)),
                  pl.BlockSpec((d, 2 * D_MEMQ), lambda i: (0, 0)),
                  pl.BlockSpec(memory_space=pl.ANY)],
        out_specs=[pl.BlockSpec((m, 2 * D_MEMQ), lambda i: (0, 0)),
                   pl.BlockSpec((d, m), lambda i: (0, 0))],
        out_shape=[jax.ShapeDtypeStruct((m, 2 * D_MEMQ), BF16), jax.ShapeDtypeStruct((d, m), BF16)],
        compiler_params=_params(("arbitrary",)),
    )(mem, g, wkv, dep)


def _memkv_bwd(dmkv, mem, g, wkv, nt):
    m, d = mem.shape

    def body(dmkv_ref, mem_ref, g_ref, w_ref, nt_ref, dw_ref, dg_ref):
        db = dmkv_ref[...].astype(BF16)
        dw_ref[...] = _dot(nt_ref[...], db).astype(BF16)
        dn = _dot_nt(db, w_ref[...])
        mm = mem_ref[...]
        dg_ref[...] = _sum8(dn * (mm * _rstd(mm)))

    return _pcall(
        body, name="memkv_bwd", grid=(1,),
        in_specs=[pl.BlockSpec((m, 2 * D_MEMQ), lambda i: (0, 0)),
                  pl.BlockSpec((m, d), lambda i: (0, 0)),
                  pl.BlockSpec((1, d), lambda i: (0, 0)),
                  pl.BlockSpec((d, 2 * D_MEMQ), lambda i: (0, 0)),
                  pl.BlockSpec((d, m), lambda i: (0, 0))],
        out_specs=[pl.BlockSpec((d, 2 * D_MEMQ), lambda i: (0, 0)),
                   pl.BlockSpec((8, d), lambda i: (0, 0))],
        out_shape=[jax.ShapeDtypeStruct((d, 2 * D_MEMQ), BF16), jax.ShapeDtypeStruct((8, d), F32)],
        compiler_params=_params(("arbitrary",)),
    )(dmkv, mem, g, wkv, nt)


def _shift_rows(v, k, edge_rows, row):
    out = pltpu.roll(v, k, 0)
    for r in range(k):
        out = jnp.where(row == r, edge_rows[r], out)
    return out


def _shift_rows_up(v, k, edge_rows, row):
    n = v.shape[0]
    out = pltpu.roll(v, n - k, 0)
    for r in range(k):
        out = jnp.where(row == n - k + r, edge_rows[r], out)
    return out


GROUP_ROWS = SWA_GROUP * BLOCK
BIAS_CUR, BIAS_PREV, BIAS_NONE = 0, 1, 2


def _bias_tables():
    tq = np.arange(BLOCK)[:, None]
    sk = np.arange(BLOCK)[None, :]
    slopes = np.asarray(SLOPES, np.float32)[:, None, None]
    cur = np.where(tq >= sk, -slopes * (tq - sk).astype(np.float32), NEG)
    prev = np.where(sk > tq, -slopes * (tq + BLOCK - sk).astype(np.float32), NEG)
    none = np.full_like(cur, NEG)
    tok = np.stack([cur, prev, none]).astype(np.float32).reshape(3, N_SWA_KV, GROUP_ROWS, BLOCK)
    return jnp.asarray(tok), jnp.asarray(np.ascontiguousarray(tok.transpose(0, 1, 3, 2)))


def _head_cols(hd):
    return D_CONV + HEAD_DIM * hd


def _stack_cols(ref, heads):
    return jnp.concatenate([ref[:, hd:hd + 1] for hd in heads], axis=0)


def _mix_core_fwd(p, qh, mkv, convw, sinks, bias_tok):
    t = p.shape[0]
    m = mkv.shape[0]
    nb = t // BLOCK

    def body(sk_ref, pc_ref, pkv_ref, ppc_ref, ppu_ref, qh_ref, mkv_ref, cw_ref, bc_ref, bp_ref, y_ref, l_ref):
        i = pl.program_id(0)
        prevf = (i > 0).astype(F32)
        row = lax.broadcasted_iota(jnp.int32, (BLOCK, D_CONV), 0)

        bg = pc_ref[:, COL_BG:COL_BG + D_CONV].astype(F32)
        cg = pc_ref[:, COL_CG:COL_CG + D_CONV].astype(F32)
        u = pc_ref[:, COL_U:COL_U + D_CONV].astype(F32)
        vv = cg * u
        pvv = ppc_ref[...].astype(F32) * ppu_ref[...].astype(F32) * prevf
        vv1 = _shift_rows(vv, 1, [pvv[15:16]], row)
        vv2 = _shift_rows(vv, 2, [pvv[14:15], pvv[15:16]], row)
        w = cw_ref[...]
        y_ref[:, 0:D_CONV] = bg * (w[0:1] * vv2 + w[1:2] * vv1 + w[2:3] * vv)

        lane = lax.broadcasted_iota(jnp.int32, (BLOCK, 128), 1)
        lse_all = jnp.zeros((BLOCK, 128), F32)
        for kv in range(N_SWA_KV):
            heads = range(kv * SWA_GROUP, (kv + 1) * SWA_GROUP)
            kc = pc_ref[:, COL_K + HEAD_DIM * kv:COL_K + HEAD_DIM * (kv + 1)]
            vc = pc_ref[:, COL_V + HEAD_DIM * kv:COL_V + HEAD_DIM * (kv + 1)]
            kp = pkv_ref[:, HEAD_DIM * kv:HEAD_DIM * (kv + 1)]
            vp = pkv_ref[:, D_KV + HEAD_DIM * kv:D_KV + HEAD_DIM * (kv + 1)]
            qg = qh_ref[kv * SWA_GROUP:(kv + 1) * SWA_GROUP].reshape(GROUP_ROWS, HEAD_DIM)
            sc = _dot_nt(qg, kc) + bc_ref[0, kv]
            sp = _dot_nt(qg, kp) + bp_ref[0, kv]
            sink = jnp.concatenate([jnp.full((BLOCK, 1), sk_ref[0, hd], F32) for hd in heads], axis=0)
            mx = jnp.maximum(jnp.max(jnp.maximum(sc, sp), axis=-1, keepdims=True), sink)
            ec = jnp.exp(sc - mx)
            ep = jnp.exp(sp - mx)
            den = jnp.sum(ec + ep, axis=-1, keepdims=True) + jnp.exp(sink - mx)
            o = (_dot(ec.astype(BF16), vc) + _dot(ep.astype(BF16), vp)) / den
            lse = mx + jnp.log(den)
            for gi, hd in enumerate(heads):
                rows = slice(gi * BLOCK, (gi + 1) * BLOCK)
                y_ref[:, _head_cols(hd):_head_cols(hd) + HEAD_DIM] = o[rows]
                lse_all = jnp.where(lane == hd, lse[rows], lse_all)

        for hm in range(N_MEM_HEADS):
            hd = N_SWA_HEADS + hm
            mk = mkv_ref[:, HEAD_DIM * hm:HEAD_DIM * (hm + 1)]
            mv = mkv_ref[:, D_MEMQ + HEAD_DIM * hm:D_MEMQ + HEAD_DIM * (hm + 1)]
            s = _dot_nt(qh_ref[hd], mk)
            mx = jnp.max(s, axis=-1, keepdims=True)
            e = jnp.exp(s - mx)
            den = jnp.sum(e, axis=-1, keepdims=True)
            y_ref[:, _head_cols(hd):_head_cols(hd) + HEAD_DIM] = _dot(e.astype(BF16), mv) / den
            lse_all = jnp.where(lane == hd, mx + jnp.log(den), lse_all)
        l_ref[...] = lse_all

    kv_col = COL_K // (2 * D_KV)
    bias_block = (1, N_SWA_KV, GROUP_ROWS, BLOCK)
    return _pcall(
        body, name="mix_core_fwd", grid=(nb,),
        in_specs=[pl.BlockSpec(memory_space=pltpu.SMEM),
                  pl.BlockSpec((BLOCK, D_IN), lambda i: (i, 0)),
                  pl.BlockSpec((BLOCK, 2 * D_KV), lambda i: (jnp.maximum(i - 1, 0), kv_col)),
                  pl.BlockSpec((16, D_CONV), lambda i: (jnp.maximum(i * (BLOCK // 16) - 1, 0), COL_CG // D_CONV)),
                  pl.BlockSpec((16, D_CONV), lambda i: (jnp.maximum(i * (BLOCK // 16) - 1, 0), COL_U // D_CONV)),
                  pl.BlockSpec((N_HEADS, BLOCK, HEAD_DIM), lambda i: (0, i, 0)),
                  pl.BlockSpec((m, 2 * D_MEMQ), lambda i: (0, 0)),
                  pl.BlockSpec((3, D_CONV), lambda i: (0, 0)),
                  pl.BlockSpec(bias_block, lambda i: (BIAS_CUR, 0, 0, 0)),
                  pl.BlockSpec(bias_block, lambda i: (jnp.where(i == 0, BIAS_NONE, BIAS_PREV), 0, 0, 0))],
        out_specs=[pl.BlockSpec((BLOCK, D_MIX), lambda i: (i, 0)),
                   pl.BlockSpec((BLOCK, 128), lambda i: (i, 0))],
        out_shape=[jax.ShapeDtypeStruct((t, D_MIX), F32), jax.ShapeDtypeStruct((t, 128), F32)],
        compiler_params=_params(("parallel",)),
    )(sinks, p, p, p, p, qh, mkv, convw, bias_tok, bias_tok)


def _mix_core_bwd(p, qh, dyconv, doh, delta, lse, mkv, convw, sinks, bias_tok, bias_key):
    t = p.shape[0]
    m = mkv.shape[0]
    nb = t // BLOCK

    def body(sk_ref, pc_ref, pkv_ref, ppc_ref, ppu_ref, pnb_ref, dyc_ref, dyn_ref, qc_ref, qn_ref, doc_ref, don_ref,
             dlc_ref, dln_ref, lc_ref, ln_ref, mkv_ref, cw_ref, bp_ref, bct_ref, bnt_ref,
             dp_ref, dmkv_ref, dcw_ref, dsk_ref):
        i = pl.program_id(0)
        prevf = (i > 0).astype(F32)
        nextf = (i < nb - 1).astype(F32)
        row = lax.broadcasted_iota(jnp.int32, (BLOCK, D_CONV), 0)

        @pl.when(i == 0)
        def _():
            dmkv_ref[...] = jnp.zeros_like(dmkv_ref)
            dcw_ref[...] = jnp.zeros_like(dcw_ref)
            dsk_ref[...] = jnp.zeros_like(dsk_ref)

        bg = pc_ref[:, COL_BG:COL_BG + D_CONV].astype(F32)
        cg = pc_ref[:, COL_CG:COL_CG + D_CONV].astype(F32)
        u = pc_ref[:, COL_U:COL_U + D_CONV].astype(F32)
        vv = cg * u
        pvv = ppc_ref[...].astype(F32) * ppu_ref[...].astype(F32) * prevf
        vv1 = _shift_rows(vv, 1, [pvv[15:16]], row)
        vv2 = _shift_rows(vv, 2, [pvv[14:15], pvv[15:16]], row)
        w = cw_ref[...]
        yconv = w[0:1] * vv2 + w[1:2] * vv1 + w[2:3] * vv
        dyo = dyc_ref[...]
        dyc = dyo * bg
        nxt = dyn_ref[...] * pnb_ref[...].astype(F32) * nextf
        d1 = _shift_rows_up(dyc, 1, [nxt[0:1]], row)
        d2 = _shift_rows_up(dyc, 2, [nxt[0:1], nxt[1:2]], row)
        dvv = w[2:3] * dyc + w[1:2] * d1 + w[0:1] * d2
        dp_ref[:, COL_BG:COL_BG + D_CONV] = (dyo * yconv).astype(BF16)
        dp_ref[:, COL_CG:COL_CG + D_CONV] = (dvv * u).astype(BF16)
        dp_ref[:, COL_U:COL_U + D_CONV] = (dvv * cg).astype(BF16)
        dcw_ref[0:1, :] += jnp.sum(dyc * vv2, axis=0, keepdims=True)
        dcw_ref[1:2, :] += jnp.sum(dyc * vv1, axis=0, keepdims=True)
        dcw_ref[2:3, :] += jnp.sum(dyc * vv, axis=0, keepdims=True)

        lse_t, dl_t = lc_ref[...].T, dlc_ref[...].T
        lse_nt, dl_nt = ln_ref[...].T, dln_ref[...].T

        def stack_rows(tile_t, heads):
            return jnp.concatenate([tile_t[hd:hd + 1, :] for hd in heads], axis=1)

        lane8 = jnp.where(lax.broadcasted_iota(jnp.int32, (8, 128), 0) == 0,
                          lax.broadcasted_iota(jnp.int32, (8, 128), 1), -1)
        dsk = jnp.zeros((8, 128), F32)
        for kv in range(N_SWA_KV):
            heads = range(kv * SWA_GROUP, (kv + 1) * SWA_GROUP)
            kc = pc_ref[:, COL_K + HEAD_DIM * kv:COL_K + HEAD_DIM * (kv + 1)]
            vc = pc_ref[:, COL_V + HEAD_DIM * kv:COL_V + HEAD_DIM * (kv + 1)]
            kp = pkv_ref[:, HEAD_DIM * kv:HEAD_DIM * (kv + 1)]
            vp = pkv_ref[:, D_KV + HEAD_DIM * kv:D_KV + HEAD_DIM * (kv + 1)]
            qg = qc_ref[kv * SWA_GROUP:(kv + 1) * SWA_GROUP].reshape(GROUP_ROWS, HEAD_DIM)
            dog = doc_ref[kv * SWA_GROUP:(kv + 1) * SWA_GROUP].reshape(GROUP_ROWS, HEAD_DIM)
            qn = qn_ref[kv * SWA_GROUP:(kv + 1) * SWA_GROUP].reshape(GROUP_ROWS, HEAD_DIM)
            don = don_ref[kv * SWA_GROUP:(kv + 1) * SWA_GROUP].reshape(GROUP_ROWS, HEAD_DIM)
            lse_col, dl_col = _stack_cols(lc_ref, heads), _stack_cols(dlc_ref, heads)
            pp_ = jnp.exp(_dot_nt(qg, kp) + bp_ref[0, kv] - lse_col)
            dsp = (pp_ * (_dot_nt(dog, vp) - dl_col)).astype(BF16)
            dq = _dot(dsp, kp)
            pt = jnp.exp(_dot_nt(kc, qg) + bct_ref[0, kv] - stack_rows(lse_t, heads))
            dst = (pt * (_dot_nt(vc, dog) - stack_rows(dl_t, heads))).astype(BF16)
            dv = _dot(pt.astype(BF16), dog)
            dk = _dot(dst, qg)
            dq = dq + _dot_tn(dst, kc)
            ptn = jnp.exp(_dot_nt(kc, qn) + bnt_ref[0, kv] - stack_rows(lse_nt, heads))
            dstn = (ptn * (_dot_nt(vc, don) - stack_rows(dl_nt, heads))).astype(BF16)
            dv = dv + _dot(ptn.astype(BF16), don)
            dk = dk + _dot(dstn, qn)
            dp_ref[:, COL_K + HEAD_DIM * kv:COL_K + HEAD_DIM * (kv + 1)] = dk.astype(BF16)
            dp_ref[:, COL_V + HEAD_DIM * kv:COL_V + HEAD_DIM * (kv + 1)] = dv.astype(BF16)
            sink = jnp.concatenate([jnp.full((BLOCK, 1), sk_ref[0, hd], F32) for hd in heads], axis=0)
            sink_term = jnp.exp(sink - lse_col) * dl_col
            for gi, hd in enumerate(heads):
                rows = slice(gi * BLOCK, (gi + 1) * BLOCK)
                dp_ref[:, _q_col(hd):_q_col(hd) + HEAD_DIM] = (dq[rows] * SCALE).astype(BF16)
                dsk = dsk + jnp.where(lane8 == hd, -jnp.sum(sink_term[rows], axis=0, keepdims=True), 0.0)
        dsk_ref[...] += dsk

        for hm in range(N_MEM_HEADS):
            hd = N_SWA_HEADS + hm
            qm, dom = qc_ref[hd], doc_ref[hd]
            mk = mkv_ref[:, HEAD_DIM * hm:HEAD_DIM * (hm + 1)]
            mv = mkv_ref[:, D_MEMQ + HEAD_DIM * hm:D_MEMQ + HEAD_DIM * (hm + 1)]
            pt = jnp.exp(_dot_nt(mk, qm) - lse_t[hd:hd + 1, :])
            dst = (pt * (_dot_nt(mv, dom) - dl_t[hd:hd + 1, :])).astype(BF16)
            dp_ref[:, _q_col(hd):_q_col(hd) + HEAD_DIM] = (_dot_tn(dst, mk) * SCALE).astype(BF16)
            dmkv_ref[:, HEAD_DIM * hm:HEAD_DIM * (hm + 1)] += _dot(dst, qm)
            dmkv_ref[:, D_MEMQ + HEAD_DIM * hm:D_MEMQ + HEAD_DIM * (hm + 1)] += _dot(pt.astype(BF16), dom)

    cur = lambda i: (i, 0)
    const = lambda i: (0, 0)
    rows16 = BLOCK // 16
    last16 = t // 16 - 1
    before = lambda col: (lambda i: (jnp.maximum(i * rows16 - 1, 0), col))
    after = lambda i: (jnp.minimum((i + 1) * rows16, last16), 0)
    heads_cur = lambda i: (0, i, 0)
    heads_next = lambda i: (0, jnp.minimum(i + 1, nb - 1), 0)
    stat_next = lambda i: (jnp.minimum(i + 1, nb - 1), 0)
    tok_block = (1, N_SWA_KV, GROUP_ROWS, BLOCK)
    key_block = (1, N_SWA_KV, BLOCK, GROUP_ROWS)
    head_block = (N_HEADS, BLOCK, HEAD_DIM)
    return _pcall(
        body, name="mix_core_bwd", grid=(nb,),
        in_specs=[pl.BlockSpec(memory_space=pltpu.SMEM),
                  pl.BlockSpec((BLOCK, D_IN), cur),
                  pl.BlockSpec((BLOCK, 2 * D_KV), lambda i: (jnp.maximum(i - 1, 0), COL_K // (2 * D_KV))),
                  pl.BlockSpec((16, D_CONV), before(COL_CG // D_CONV)),
                  pl.BlockSpec((16, D_CONV), before(COL_U // D_CONV)),
                  pl.BlockSpec((16, D_CONV), after),
                  pl.BlockSpec((BLOCK, D_CONV), cur),
                  pl.BlockSpec((16, D_CONV), after),
                  pl.BlockSpec(head_block, heads_cur), pl.BlockSpec(head_block, heads_next),
                  pl.BlockSpec(head_block, heads_cur), pl.BlockSpec(head_block, heads_next),
                  pl.BlockSpec((BLOCK, 128), cur), pl.BlockSpec((BLOCK, 128), stat_next),
                  pl.BlockSpec((BLOCK, 128), cur), pl.BlockSpec((BLOCK, 128), stat_next),
                  pl.BlockSpec((m, 2 * D_MEMQ), const),
                  pl.BlockSpec((3, D_CONV), const),
                  pl.BlockSpec(tok_block, lambda i: (jnp.where(i == 0, BIAS_NONE, BIAS_PREV), 0, 0, 0)),
                  pl.BlockSpec(key_block, lambda i: (BIAS_CUR, 0, 0, 0)),
                  pl.BlockSpec(key_block, lambda i: (jnp.where(i == nb - 1, BIAS_NONE, BIAS_PREV), 0, 0, 0))],
        out_specs=[pl.BlockSpec((BLOCK, D_IN), cur),
                   pl.BlockSpec((m, 2 * D_MEMQ), const),
                   pl.BlockSpec((8, D_CONV), const),
                   pl.BlockSpec((8, 128), const)],
        out_shape=[jax.ShapeDtypeStruct((t, D_IN), BF16),
                   jax.ShapeDtypeStruct((m, 2 * D_MEMQ), F32),
                   jax.ShapeDtypeStruct((8, D_CONV), F32),
                   jax.ShapeDtypeStruct((8, 128), F32)],
        compiler_params=_params(("arbitrary",)),
    )(sinks, p, p, p, p, p, dyconv, dyconv, qh, qh, doh, doh, delta, delta, lse, lse, mkv, convw,
      bias_tok, bias_key, bias_key)


def _group_norms(y):
    out = []
    for a, b in MIX_GROUPS:
        ys = y[:, a:b]
        r = _rstd(ys)
        out.append((ys * r, r))
    return out


def _mix_out_fwd(y, h, g, wout):
    t, d = h.shape
    tm = _tok_block(t)

    def body(y_ref, h_ref, g_ref, w_ref, ho_ref, mt_ref):
        yhat = jnp.concatenate([yh for yh, _ in _group_norms(y_ref[...])], axis=-1)
        mixed = yhat * g_ref[...]
        mt_ref[...] = mixed.T.astype(BF16)
        ho_ref[...] = h_ref[...] + _dot(mixed.astype(BF16), w_ref[...])

    return _pcall(
        body, name="mix_out_fwd", grid=(t // tm,),
        in_specs=[pl.BlockSpec((tm, D_MIX), lambda i: (i, 0)),
                  pl.BlockSpec((tm, d), lambda i: (i, 0)),
                  pl.BlockSpec((1, D_MIX), lambda i: (0, 0)),
                  pl.BlockSpec((D_MIX, d), lambda i: (0, 0))],
        out_specs=[pl.BlockSpec((tm, d), lambda i: (i, 0)),
                   pl.BlockSpec((D_MIX, tm), lambda i: (0, i))],
        out_shape=[jax.ShapeDtypeStruct((t, d), F32), jax.ShapeDtypeStruct((D_MIX, t), BF16)],
        compiler_params=_params(("parallel",)),
    )(y, h, g, wout)


def _head_indicator():
    ind = np.zeros((D_MIX, 128), np.float32)
    for hd in range(N_HEADS):
        ind[_head_cols(hd):_head_cols(hd) + HEAD_DIM, hd] = 1.0
    return jnp.asarray(ind, BF16)


def _mix_out_bwd(dho, y, g, wout, mt, dep):
    t, d = dho.shape
    tm = _tok_block(t)
    ni = t // tm

    def body(dho_ref, y_ref, g_ref, w_ref, mt_ref, ind_ref, dep_ref, dyc_ref, doh_ref, dl_ref, dw_ref, dg_ref, acc_ref):
        i = pl.program_id(0)
        dhb = dho_ref[...].astype(BF16)
        dm = _dot_nt(dhb, w_ref[...])
        pw = _dot(mt_ref[...], dhb)
        gg = g_ref[...]
        yy = y_ref[...]
        dys = []
        dgs = []
        for (a, b), (yhat, r) in zip(MIX_GROUPS, _group_norms(yy)):
            dmg = dm[:, a:b]
            dgs.append(_sum8(dmg * yhat))
            dyh = dmg * gg[:, a:b]
            dys.append(r * (dyh - yhat * jnp.mean(dyh * yhat, axis=-1, keepdims=True)))
        dy = jnp.concatenate(dys, axis=-1)
        dyc_ref[...] = dy[:, 0:D_CONV]
        for hd in range(N_HEADS):
            doh_ref[hd] = dy[:, _head_cols(hd):_head_cols(hd) + HEAD_DIM].astype(BF16)
        prod = dy * yy
        hi = prod.astype(BF16)
        lo = (prod - hi.astype(F32)).astype(BF16)
        dl_ref[...] = _dot(hi, ind_ref[...]) + _dot(lo, ind_ref[...])
        part = jnp.concatenate(dgs, axis=-1)

        @pl.when(i == 0)
        def _():
            acc_ref[...] = pw
            dg_ref[...] = part

        @pl.when(i > 0)
        def _():
            acc_ref[...] += pw
            dg_ref[...] += part

        @pl.when(i == ni - 1)
        def _():
            dw_ref[...] = acc_ref[...].astype(BF16)

    return _pcall(
        body, name="mix_out_bwd", grid=(ni,),
        in_specs=[pl.BlockSpec((tm, d), lambda i: (i, 0)),
                  pl.BlockSpec((tm, D_MIX), lambda i: (i, 0)),
                  pl.BlockSpec((1, D_MIX), lambda i: (0, 0)),
                  pl.BlockSpec((D_MIX, d), lambda i: (0, 0)),
                  pl.BlockSpec((D_MIX, tm), lambda i: (0, i)),
                  pl.BlockSpec((D_MIX, 128), lambda i: (0, 0)),
                  pl.BlockSpec(memory_space=pl.ANY)],
        out_specs=[pl.BlockSpec((tm, D_CONV), lambda i: (i, 0)),
                   pl.BlockSpec((N_HEADS, tm, HEAD_DIM), lambda i: (0, i, 0)),
                   pl.BlockSpec((tm, 128), lambda i: (i, 0)),
                   pl.BlockSpec((D_MIX, d), lambda i: (0, 0)),
                   pl.BlockSpec((8, D_MIX), lambda i: (0, 0))],
        out_shape=[jax.ShapeDtypeStruct((t, D_CONV), F32),
                   jax.ShapeDtypeStruct((N_HEADS, t, HEAD_DIM), BF16),
                   jax.ShapeDtypeStruct((t, 128), F32),
                   jax.ShapeDtypeStruct((D_MIX, d), BF16),
                   jax.ShapeDtypeStruct((8, D_MIX), F32)],
        scratch_shapes=[pltpu.VMEM((D_MIX, d), F32)],
        compiler_params=_params(("arbitrary",)),
    )(dho, y, g, wout, mt, _head_indicator(), dep)


def _mix_proj_bwd(dp, dho, h, g, win_t, n):
    t, d = h.shape
    tm = _tok_block(t)
    ni = t // tm

    def body(dp_ref, dho_ref, h_ref, g_ref, w_ref, n_ref, dh_ref, dw_ref, dg_ref, acc_ref):
        i = pl.program_id(0)
        dpb = dp_ref[...]
        dn = _dot(dpb, w_ref[...])
        pw = _dot_tn(dpb, n_ref[...])
        hh = h_ref[...]
        r = _rstd(hh)
        xhat = hh * r
        dxh = dn * g_ref[...]
        dh_ref[...] = dho_ref[...] + r * (dxh - xhat * jnp.mean(dxh * xhat, axis=-1, keepdims=True))
        part = _sum8(dn * xhat)

        @pl.when(i == 0)
        def _():
            acc_ref[...] = pw
            dg_ref[...] = part

        @pl.when(i > 0)
        def _():
            acc_ref[...] += pw
            dg_ref[...] += part

        @pl.when(i == ni - 1)
        def _():
            dw_ref[...] = acc_ref[...].astype(BF16)

    return _pcall(
        body, name="mix_proj_bwd", grid=(ni,),
        in_specs=[pl.BlockSpec((tm, D_IN), lambda i: (i, 0)),
                  pl.BlockSpec((tm, d), lambda i: (i, 0)),
                  pl.BlockSpec((tm, d), lambda i: (i, 0)),
                  pl.BlockSpec((1, d), lambda i: (0, 0)),
                  pl.BlockSpec((D_IN, d), lambda i: (0, 0)),
                  pl.BlockSpec((tm, d), lambda i: (i, 0))],
        out_specs=[pl.BlockSpec((tm, d), lambda i: (i, 0)),
                   pl.BlockSpec((D_IN, d), lambda i: (0, 0)),
                   pl.BlockSpec((8, d), lambda i: (0, 0))],
        out_shape=[jax.ShapeDtypeStruct((t, d), F32),
                   jax.ShapeDtypeStruct((D_IN, d), BF16),
                   jax.ShapeDtypeStruct((8, d), F32)],
        scratch_shapes=[pltpu.VMEM((D_IN, d), F32)],
        compiler_params=_params(("arbitrary",)),
    )(dp, dho, h, g, win_t, n)


def _final_loss(h, g, tgt):
    t, d = h.shape
    tm = _tok_block(t)

    def body(h_ref, g_ref, t_ref, dh_ref, ls_ref, dg_ref):
        i = pl.program_id(0)
        hh = h_ref[...]
        r = _rstd(hh)
        xhat = hh * r
        gg = g_ref[...]
        err = xhat * gg - t_ref[...]
        dy = err * (1.0 / d)
        dxh = dy * gg
        dh_ref[...] = r * (dxh - xhat * jnp.mean(dxh * xhat, axis=-1, keepdims=True))
        lpart = _sum8(err * err)
        gpart = _sum8(dy * xhat)

        @pl.when(i == 0)
        def _():
            ls_ref[...] = lpart
            dg_ref[...] = gpart

        @pl.when(i > 0)
        def _():
            ls_ref[...] += lpart
            dg_ref[...] += gpart

    return _pcall(
        body, name="final_loss", grid=(t // tm,),
        in_specs=[pl.BlockSpec((tm, d), lambda i: (i, 0)),
                  pl.BlockSpec((1, d), lambda i: (0, 0)),
                  pl.BlockSpec((tm, d), lambda i: (i, 0))],
        out_specs=[pl.BlockSpec((tm, d), lambda i: (i, 0)),
                   pl.BlockSpec((8, d), lambda i: (0, 0)),
                   pl.BlockSpec((8, d), lambda i: (0, 0))],
        out_shape=[jax.ShapeDtypeStruct((t, d), F32),
                   jax.ShapeDtypeStruct((8, d), F32),
                   jax.ShapeDtypeStruct((8, d), F32)],
        compiler_params=_params(("arbitrary",)),
    )(h, g, tgt)


def _position():
    return lax.axis_index("x"), lax.axis_index("y"), lax.axis_index("c")


def _flip(v, bit):
    return 1 - v if bit else v


def _peer(k):
    x, y, c = _position()
    return _flip(x, k & 4), _flip(y, k & 2), _flip(c, k & 1)


def _slot(px, py, pc):
    return 4 * px + 2 * py + pc


def _handshake(peers):
    barrier = pltpu.get_barrier_semaphore()
    for peer in peers:
        pl.semaphore_signal(barrier, inc=1, device_id=peer, device_id_type=MESH)
    pl.semaphore_wait(barrier, len(peers))


def _sequencer_call(body, name, collective_id, out_type, scratch_types, operands):
    return pl.kernel(
        body, out_type=out_type, mesh=plsc.ScalarSubcoreMesh(axis_name="sequencer", num_cores=1), name=name,
        scratch_types=scratch_types, compiler_params=pltpu.CompilerParams(collective_id=collective_id),
    )(*operands)


def _all_gather(shards, name, collective_id):
    nt = len(shards)

    def body(*refs):
        xs = refs[:nt]
        outs = refs[nt:2 * nt]
        send_sems, recv_sems, local_sems = refs[2 * nt:]
        x, y, c = _position()
        me, sibling = (x, y, c), (x, y, 1 - c)
        chips = [(1 - x, y), (x, 1 - y), (1 - x, 1 - y)]
        _handshake([sibling] + [(*chip, c) for chip in chips])

        def copy(t, k, block, to, src=None):
            dst = outs[t].at[_slot(*block)]
            return pltpu.make_async_remote_copy(
                src_ref=dst if src is None else src, dst_ref=dst,
                send_sem=send_sems.at[t, k], recv_sem=recv_sems.at[t, k],
                device_id=to, device_id_type=MESH)

        mine = [pltpu.make_async_copy(xs[t], outs[t].at[_slot(*me)], local_sems.at[t]) for t in range(nt)]
        for cp in mine:
            cp.start()
        first = []
        for t in range(nt):
            first.append(copy(t, 0, me, sibling, src=xs[t]))
            first += [copy(t, 1 + j, me, (*chip, c), src=xs[t]) for j, chip in enumerate(chips)]
        for cp in first:
            cp.start()
        passed = []
        for j, chip in enumerate(chips):
            for t in range(nt):
                copy(t, 1 + j, (*chip, c), me).wait_recv()
                fwd = copy(t, 4 + j, (*chip, c), sibling)
                fwd.start()
                passed.append(fwd)
        for t in range(nt):
            copy(t, 0, sibling, me).wait_recv()
            for j, chip in enumerate(chips):
                copy(t, 4 + j, (*chip, 1 - c), me).wait_recv()
        for cp in first + passed:
            cp.wait_send()
        for cp in mine:
            cp.wait()

    return _sequencer_call(
        body, name, collective_id,
        out_type=[jax.ShapeDtypeStruct((N_DEV,) + s.shape, s.dtype) for s in shards],
        scratch_types=[pltpu.SemaphoreType.DMA((nt, 7)), pltpu.SemaphoreType.DMA((nt, 7)),
                       pltpu.SemaphoreType.DMA((nt,))],
        operands=shards)


def _scatter_partials(partials, name, collective_id):
    nt = len(partials)

    def body(*refs):
        srcs = refs[:nt]
        outs = refs[nt:2 * nt]
        send_sems, recv_sems, local_sems = refs[2 * nt:]
        x, y, c = _position()
        _handshake([_peer(k) for k in range(1, N_DEV)])

        def copy(t, k):
            peer = _peer(k)
            return pltpu.make_async_remote_copy(
                src_ref=srcs[t].at[_slot(*peer)], dst_ref=outs[t].at[k],
                send_sem=send_sems.at[t, k - 1], recv_sem=recv_sems.at[t, k - 1],
                device_id=peer, device_id_type=MESH)

        mine = [pltpu.make_async_copy(srcs[t].at[_slot(x, y, c)], outs[t].at[0], local_sems.at[t]) for t in range(nt)]
        for cp in mine:
            cp.start()
        sent = [copy(t, k) for k in range(1, N_DEV) for t in range(nt)]
        for cp in sent:
            cp.start()
        for cp in sent:
            cp.wait_recv()
        for cp in sent:
            cp.wait_send()
        for cp in mine:
            cp.wait()

    return _sequencer_call(
        body, name, collective_id,
        out_type=[jax.ShapeDtypeStruct(p.shape, p.dtype) for p in partials],
        scratch_types=[pltpu.SemaphoreType.DMA((nt, 7)), pltpu.SemaphoreType.DMA((nt, 7)),
                       pltpu.SemaphoreType.DMA((nt,))],
        operands=partials)


def _all_reduce_rows(v):
    nv, _, w = v.shape

    def body(v_ref, out_ref, gath_ref, send_sems, recv_sems):
        x, y, c = _position()
        me = _slot(x, y, c)

        def copy(k):
            return pltpu.make_async_remote_copy(
                src_ref=v_ref, dst_ref=gath_ref.at[me],
                send_sem=send_sems.at[k - 1], recv_sem=recv_sems.at[k - 1],
                device_id=_peer(k), device_id_type=MESH)

        def arrival(k):
            return pltpu.make_async_remote_copy(
                src_ref=v_ref, dst_ref=gath_ref.at[_slot(*_peer(k))],
                send_sem=send_sems.at[k - 1], recv_sem=recv_sems.at[k - 1],
                device_id=_peer(k), device_id_type=MESH)

        sent = [copy(k) for k in range(1, N_DEV)]
        for cp in sent:
            cp.start()
        gath_ref[me] = v_ref[...]
        for k in range(1, N_DEV):
            arrival(k).wait_recv()
        for cp in sent:
            cp.wait_send()
        total = gath_ref[0]
        for s in range(1, N_DEV):
            total = total + gath_ref[s]
        out_ref[...] = jnp.sum(total, axis=1)

    vmem = pl.BlockSpec(memory_space=pltpu.VMEM)
    return _pcall(
        body, name="all_reduce_rows",
        in_specs=[vmem], out_specs=vmem,
        out_shape=jax.ShapeDtypeStruct((nv, w), F32),
        scratch_shapes=[pltpu.VMEM((N_DEV, nv, 8, w), F32),
                        pltpu.SemaphoreType.DMA((7,)), pltpu.SemaphoreType.DMA((7,))],
    )(v)


def _adamw_math(w, g, m, v):
    m2 = ADAM_B1 * m + (1.0 - ADAM_B1) * g
    v2 = ADAM_B2 * v + (1.0 - ADAM_B2) * (g * g)
    m_hat = m2 / (1.0 - ADAM_B1 ** ADAM_STEP)
    v_hat = v2 / (1.0 - ADAM_B2 ** ADAM_STEP)
    delta = -ADAM_LR * (m_hat / (jnp.sqrt(v_hat) + ADAM_EPS) + ADAM_WD * w)
    return delta, m2, v2


def _row_block(r):
    for cand in (256, 176, 128):
        if r % cand == 0:
            return cand
    return r


def _adamw_sharded(recv0, recv1, w, m, v):
    _, r, c = recv0.shape
    tr = _row_block(r)
    nr = r // tr

    def body(r0_ref, r1_ref, w_ref, m_ref, v_ref, g_ref, d_ref, m2_ref, v2_ref):
        layer = pl.program_id(0)

        def total(ref):
            acc = ref[0].astype(F32)
            for k in range(1, N_DEV):
                acc = acc + ref[k].astype(F32)
            return acc

        g = jnp.where(layer == 0, total(r0_ref), total(r1_ref))
        delta, m2, v2 = _adamw_math(w_ref[0], g, m_ref[0], v_ref[0])
        g_ref[0] = g
        d_ref[0] = delta
        m2_ref[0] = m2
        v2_ref[0] = v2

    shard = pl.BlockSpec((1, tr, c), lambda l, i: (l, i, 0))
    out = jax.ShapeDtypeStruct((2, r, c), F32)
    return _pcall(
        body, name="adamw_sharded", grid=(2, nr),
        in_specs=[pl.BlockSpec((N_DEV, tr, c), lambda l, i: (0, jnp.where(l == 0, i, nr - 1), 0)),
                  pl.BlockSpec((N_DEV, tr, c), lambda l, i: (0, jnp.where(l == 1, i, 0), 0)),
                  shard, shard, shard],
        out_specs=[shard, shard, shard, shard],
        out_shape=[out, out, out, out],
        compiler_params=_params(("arbitrary", "arbitrary")),
    )(recv0, recv1, w, m, v)


def _adamw_small(w, g, m, v):
    def body(w_ref, g_ref, m_ref, v_ref, d_ref, m2_ref, v2_ref):
        delta, m2, v2 = _adamw_math(w_ref[...], g_ref[...], m_ref[...], v_ref[...])
        d_ref[...] = delta
        m2_ref[...] = m2
        v2_ref[...] = v2

    spec = pl.BlockSpec(w.shape, lambda i: (0, 0))
    out = jax.ShapeDtypeStruct(w.shape, F32)
    return _pcall(
        body, name="adamw_small", grid=(1,),
        in_specs=[spec] * 4, out_specs=[spec] * 3, out_shape=[out] * 3,
        compiler_params=_params(("arbitrary",)),
    )(w, g, m, v)


def _pack(arrs):
    flat = jnp.concatenate([a.reshape(-1) for a in arrs])
    n = flat.shape[0]
    rows = -(-n // 1024) * 8
    return jnp.pad(flat, (0, rows * 128 - n)).reshape(rows, 128)


def _unpack(packed, like):
    flat = packed.reshape(-1)
    out, off = [], 0
    for a in like:
        out.append(flat[off:off + a.size].reshape(a.shape))
        off += a.size
    return out


def kernel(x, mem, g_ffn1, w_ffn1_up, w_ffn1_down, g_mix, w_in, conv_w, sinks, g_mem, w_mem_kv, g_grp, w_out, g_ffn2, w_ffn2_up, w_ffn2_down, g_final, loss_target, m_g_ffn1, m_w_ffn1_up, m_w_ffn1_down, m_g_mix, m_w_in, m_conv_w, m_sinks, m_g_mem, m_w_mem_kv, m_g_grp, m_w_out, m_g_ffn2, m_w_ffn2_up, m_w_ffn2_down, m_g_final, v_g_ffn1, v_w_ffn1_up, v_w_ffn1_down, v_g_mix, v_w_in, v_conv_w, v_sinks, v_g_mem, v_w_mem_kv, v_g_grp, v_w_out, v_g_ffn2, v_w_ffn2_up, v_w_ffn2_down, v_g_final):
    depth = g_ffn1.shape[0]
    t, d = x.shape[1], x.shape[2]
    width = max(d, D_MIX)
    me = _slot(*_position())
    conv_shard = conv_w.shape[2]

    xin, memin, tgt = x[0], mem[0], loss_target[0]

    conv_tile = jnp.zeros((depth * 8, 128), F32).at[:, :conv_shard].set(
        jnp.pad(conv_w, ((0, 0), (0, 8 - conv_w.shape[1]), (0, 0))).reshape(depth * 8, conv_shard))
    tr = lambda a: jnp.swapaxes(a, -1, -2)
    bf = lambda a: a.astype(BF16)
    weights = []
    collective_id = 0
    for l in range(depth):
        groups = [[bf(tr(w_ffn1_up[l])), bf(w_ffn1_down[l])] + ([conv_tile] if l == 0 else []),
                  [bf(tr(w_in[l])), bf(w_mem_kv[l]), bf(w_out[l])],
                  [bf(tr(w_ffn2_up[l])), bf(w_ffn2_down[l])]]
        full = []
        for gi, shards in enumerate(groups):
            full.append(_all_gather(shards, f"all_gather_l{l}_g{gi}", collective_id))
            collective_id += 1
        if l == 0:
            conv_full = full[0][2].reshape(N_DEV, depth, 8, 128)[:, :, :3, :conv_shard]
            conv_full = conv_full.transpose(1, 2, 0, 3).reshape(depth, 3, N_DEV * conv_shard)
        weights.append(dict(
            up1=full[0][0].reshape(2, -1, d), dn1=full[0][1].reshape(-1, d),
            win=full[1][0].reshape(D_IN, d), wkv=full[1][1].reshape(d, 2 * D_MEMQ), wout=full[1][2].reshape(D_MIX, d),
            up2=full[2][0].reshape(2, -1, d), dn2=full[2][1].reshape(-1, d)))

    row = lambda a: a.reshape(1, -1)
    bias_tok, bias_key = _bias_tables()

    h = xin
    saved = []
    for l in range(depth):
        wl = weights[l]
        s = dict(h0=h)
        h, s["gu1"], s["n1"] = _ffn_fwd(h, row(g_ffn1[l]), wl["up1"], wl["dn1"])
        s["h1"] = h
        s["p"], s["n_mix"], s["qh"] = _mix_proj_fwd(h, row(g_mix[l]), wl["win"])
        s["mkv"], s["nt_mem"] = _memkv_fwd(memin, row(g_mem[l]), wl["wkv"], s["p"])
        s["y"], s["lse"] = _mix_core_fwd(s["p"], s["qh"], s["mkv"], conv_full[l], row(sinks[l]), bias_tok)
        h, s["mt"] = _mix_out_fwd(s["y"], h, row(g_grp[l]), wl["wout"])
        s["h2"] = h
        h, s["gu2"], s["n2"] = _ffn_fwd(h, row(g_ffn2[l]), wl["up2"], wl["dn2"])
        saved.append(s)

    dh, loss_part, dg_final = _final_loss(h, row(g_final), tgt)
    loss = lax.psum(0.5 * jnp.sum(loss_part) / d, ("x", "y", "c"))

    small = {}
    recv = [None] * depth
    dep = loss_part
    for l in reversed(range(depth)):
        wl, s = weights[l], saved[l]
        dh, agu, dyb, small["g_ffn2", l] = _ffn_bwd_act(dh, s["h2"], row(g_ffn2[l]), s["gu2"], wl["up2"], wl["dn2"], dep)
        dup2, ddn2 = _ffn_bwd_w(agu, dyb, s["n2"])
        got_ffn2 = _scatter_partials([dup2.reshape(N_DEV, -1, d), ddn2.reshape(N_DEV, -1, d)],
                                     f"scatter_grads_l{l}_ffn2", collective_id)
        dyconv, doh, delta, dwout, small["g_grp", l] = _mix_out_bwd(dh, s["y"], row(g_grp[l]), wl["wout"], s["mt"], ddn2)
        dp, dmkv, small["conv_w", l], small["sinks", l] = _mix_core_bwd(
            s["p"], s["qh"], dyconv, doh, delta, s["lse"], s["mkv"], conv_full[l], row(sinks[l]), bias_tok, bias_key)
        dwkv, small["g_mem", l] = _memkv_bwd(dmkv, memin, row(g_mem[l]), wl["wkv"], s["nt_mem"])
        dh, dwin, small["g_mix", l] = _mix_proj_bwd(dp, dh, s["h1"], row(g_mix[l]), wl["win"], s["n_mix"])
        got_mix = _scatter_partials([dwin.reshape(N_DEV, -1, d), dwkv.reshape(N_DEV, -1, 2 * D_MEMQ),
                                     dwout.reshape(N_DEV, -1, d)], f"scatter_grads_l{l}_mix", collective_id + 1)
        dh, agu, dyb, small["g_ffn1", l] = _ffn_bwd_act(dh, s["h0"], row(g_ffn1[l]), s["gu1"], wl["up1"], wl["dn1"], dwkv)
        dup1, ddn1 = _ffn_bwd_w(agu, dyb, s["n1"])
        got_ffn1 = _scatter_partials([dup1.reshape(N_DEV, -1, d), ddn1.reshape(N_DEV, -1, d)],
                                     f"scatter_grads_l{l}_ffn1", collective_id + 2)
        collective_id += 3
        dep = ddn1
        recv[l] = dict(w_ffn2_up=got_ffn2[0], w_ffn2_down=got_ffn2[1], w_in=got_mix[0], w_mem_kv=got_mix[1],
                       w_out=got_mix[2], w_ffn1_up=got_ffn1[0], w_ffn1_down=got_ffn1[1])
    grad_x = dh[None]

    sharded = {}
    big = [("w_ffn2_up", w_ffn2_up, m_w_ffn2_up, v_w_ffn2_up, True), ("w_ffn2_down", w_ffn2_down, m_w_ffn2_down, v_w_ffn2_down, False),
           ("w_in", w_in, m_w_in, v_w_in, True), ("w_mem_kv", w_mem_kv, m_w_mem_kv, v_w_mem_kv, False),
           ("w_out", w_out, m_w_out, v_w_out, False), ("w_ffn1_up", w_ffn1_up, m_w_ffn1_up, v_w_ffn1_up, True),
           ("w_ffn1_down", w_ffn1_down, m_w_ffn1_down, v_w_ffn1_down, False)]
    for name, w, m, v, transposed in big:
        if transposed:
            res = _adamw_sharded(recv[0][name], recv[depth - 1][name], tr(w), tr(m), tr(v))
            sharded[name] = tuple(tr(r) for r in res)
        else:
            sharded[name] = tuple(_adamw_sharded(recv[0][name], recv[depth - 1][name], w, m, v))

    def lanes(a):
        return jnp.pad(a, ((0, 0), (0, width - a.shape[1])))

    def first_row(a):
        return lanes(jnp.pad(a, ((0, 8 - a.shape[0]), (0, 0))))

    vec_names = ["g_ffn1", "g_mix", "g_mem", "g_grp", "g_ffn2", "sinks"]
    tiles = [lanes(small[n, l]) for n in vec_names for l in range(depth)]
    tiles += [first_row(small["conv_w", l][k:k + 1]) for l in range(depth) for k in range(3)]
    tiles.append(lanes(dg_final))
    n_real = len(tiles)
    tiles += [jnp.zeros((8, width), F32)] * (-n_real % 8)
    summed = _all_reduce_rows(jnp.stack(tiles))

    def vec(n, wd):
        return jnp.stack([summed[vec_names.index(n) * depth + l, :wd] for l in range(depth)])

    conv_base = len(vec_names) * depth
    conv_grad = jnp.stack([jnp.stack([summed[conv_base + 3 * l + k, :D_CONV] for k in range(3)]) for l in range(depth)])
    grads_small = {
        "g_ffn1": vec("g_ffn1", d), "g_mix": vec("g_mix", d), "g_mem": vec("g_mem", d),
        "g_grp": vec("g_grp", D_MIX), "g_ffn2": vec("g_ffn2", d), "sinks": vec("sinks", N_SWA_HEADS),
        "conv_w": lax.dynamic_slice_in_dim(conv_grad, me * conv_shard, conv_shard, axis=2),
        "g_final": summed[n_real - 1, :d],
    }
    small_w = [("g_ffn1", g_ffn1, m_g_ffn1, v_g_ffn1), ("g_mix", g_mix, m_g_mix, v_g_mix),
               ("conv_w", conv_w, m_conv_w, v_conv_w), ("sinks", sinks, m_sinks, v_sinks),
               ("g_mem", g_mem, m_g_mem, v_g_mem), ("g_grp", g_grp, m_g_grp, v_g_grp),
               ("g_ffn2", g_ffn2, m_g_ffn2, v_g_ffn2), ("g_final", g_final, m_g_final, v_g_final)]
    like = [w for _, w, _, _ in small_w]
    packed = _adamw_small(_pack(like), _pack([grads_small[n] for n, _, _, _ in small_w]),
                          _pack([m for _, _, m, _ in small_w]), _pack([v for _, _, _, v in small_w]))
    small_out = {n: (grads_small[n], dl, m2, v2)
                 for (n, _, _, _), dl, m2, v2 in zip(small_w, *[_unpack(pk, like) for pk in packed])}

    order = ["g_ffn1", "w_ffn1_up", "w_ffn1_down", "g_mix", "w_in", "conv_w", "sinks", "g_mem", "w_mem_kv", "g_grp",
             "w_out", "g_ffn2", "w_ffn2_up", "w_ffn2_down", "g_final"]
    results = {**sharded, **small_out}
    outs = [loss, grad_x]
    for part in range(4):
        outs += [results[n][part] for n in order]
    return tuple(outs)
```

```python
import numpy as np
import jax
import jax.numpy as jnp
from jax import lax
from jax.experimental import pallas as pl
from jax.experimental.pallas import tpu as pltpu
from jax.experimental.pallas import tpu_sc as plsc

F32 = jnp.float32
BF16 = jnp.bfloat16

N_DEV = 8
EPS = 1e-6
N_SWA_HEADS = 8
N_SWA_KV = 2
SWA_GROUP = N_SWA_HEADS // N_SWA_KV
HEAD_DIM = 64
N_MEM_HEADS = 4
D_CONV = 256
BLOCK = 128
D_SWA = N_SWA_HEADS * HEAD_DIM
D_KV = N_SWA_KV * HEAD_DIM
D_MEMQ = N_MEM_HEADS * HEAD_DIM
D_MIX = D_CONV + D_SWA + D_MEMQ
D_IN = 3 * D_CONV + D_SWA + 2 * D_KV + D_MEMQ
COL_BG, COL_CG, COL_U = 0, D_CONV, 2 * D_CONV
COL_Q = 3 * D_CONV
COL_K = COL_Q + D_SWA
COL_V = COL_K + D_KV
COL_QM = COL_V + D_KV
MIX_GROUPS = ((0, D_CONV), (D_CONV, D_CONV + D_SWA), (D_CONV + D_SWA, D_MIX))
SLOPES = tuple(2.0 ** (-8.0 * (i + 1) / N_SWA_HEADS) for i in range(N_SWA_HEADS))
SCALE = HEAD_DIM ** -0.5
NEG = -1e30

ADAM_LR = 0.001
ADAM_B1 = 0.9
ADAM_B2 = 0.999
ADAM_EPS = 1e-08
ADAM_WD = 0.01
ADAM_STEP = 10

V7X_VMEM_BYTES = 64 * 1024 * 1024
VMEM_LIMIT = (V7X_VMEM_BYTES * 3) // 4
MESH = pl.DeviceIdType.MESH


def _pcall(body, **kw):
    return pl.pallas_call(body, **kw)


def _params(sem=None, vmem=VMEM_LIMIT):
    return pltpu.CompilerParams(dimension_semantics=sem, vmem_limit_bytes=vmem)


def _dot(a, b):
    return lax.dot_general(a, b, (((1,), (0,)), ((), ())), preferred_element_type=F32)


def _dot_nt(a, b):
    return lax.dot_general(a, b, (((1,), (1,)), ((), ())), preferred_element_type=F32)


def _dot_tn(a, b):
    return lax.dot_general(a, b, (((0,), (0,)), ((), ())), preferred_element_type=F32)


def _rstd(x):
    return lax.rsqrt(jnp.mean(x * x, axis=-1, keepdims=True) + EPS)


def _sigmoid(x):
    return 1.0 / (1.0 + jnp.exp(-x))


def _sum8(x):
    r, w = x.shape
    return jnp.sum(x.reshape(r // 8, 8, w), axis=0)


def _tok_block(t, rows=512):
    return min(rows, t)


def _feat_block(f):
    return f // (N_DEV // 2)


def _ffn_fwd(h, g, wup_t, wdn):
    t, d = h.shape
    f = wdn.shape[0]
    tm, tf = _tok_block(t), _feat_block(f)
    ni, nj = t // tm, f // tf

    def body(h_ref, g_ref, wup_ref, wdn_ref, ho_ref, gu_ref, n_ref, nt_ref, acc_ref):
        j = pl.program_id(1)

        @pl.when(j == 0)
        def _():
            hh = h_ref[...]
            n = hh * _rstd(hh) * g_ref[...]
            n_ref[...] = n.astype(BF16)
            nt_ref[...] = n.T.astype(BF16)
            acc_ref[...] = jnp.zeros_like(acc_ref)

        nt = nt_ref[...]
        gate = _dot(wup_ref[0], nt)
        up = _dot(wup_ref[1], nt)
        gu_ref[0] = gate.astype(BF16)
        gu_ref[1] = up.astype(BF16)
        a = gate * _sigmoid(gate) * up
        acc_ref[...] += _dot_tn(a.astype(BF16), wdn_ref[...])

        @pl.when(j == nj - 1)
        def _():
            ho_ref[...] = h_ref[...] + 0.5 * acc_ref[...]

    return _pcall(
        body, name="ffn_fwd", grid=(ni, nj),
        in_specs=[pl.BlockSpec((tm, d), lambda i, j: (i, 0)),
                  pl.BlockSpec((1, d), lambda i, j: (0, 0)),
                  pl.BlockSpec((2, tf, d), lambda i, j: (0, j, 0)),
                  pl.BlockSpec((tf, d), lambda i, j: (j, 0))],
        out_specs=[pl.BlockSpec((tm, d), lambda i, j: (i, 0)),
                   pl.BlockSpec((2, tf, tm), lambda i, j: (0, j, i)),
                   pl.BlockSpec((tm, d), lambda i, j: (i, 0))],
        out_shape=[jax.ShapeDtypeStruct((t, d), F32),
                   jax.ShapeDtypeStruct((2, f, t), BF16),
                   jax.ShapeDtypeStruct((t, d), BF16)],
        scratch_shapes=[pltpu.VMEM((d, tm), BF16), pltpu.VMEM((tm, d), F32)],
        compiler_params=_params(("parallel", "arbitrary")),
    )(h, g, wup_t, wdn)


def _ffn_bwd_act(dho, h, g, gu, wup_t, wdn, dep):
    t, d = h.shape
    f = wdn.shape[0]
    tm, tf = _tok_block(t), _feat_block(f)
    ni, nj = t // tm, f // tf

    def body(dho_ref, h_ref, g_ref, gu_ref, wup_ref, wdn_ref, dep_ref, dh_ref, agu_ref, dyb_ref, dg_ref, dyt_ref, acc_ref):
        i = pl.program_id(0)
        j = pl.program_id(1)

        @pl.when(j == 0)
        def _():
            dy0 = 0.5 * dho_ref[...]
            dyb_ref[...] = dy0.astype(BF16)
            dyt_ref[...] = dy0.T.astype(BF16)

        da = _dot(wdn_ref[...], dyt_ref[...])
        gate = gu_ref[0].astype(F32)
        up = gu_ref[1].astype(F32)
        sg = _sigmoid(gate)
        silu = gate * sg
        dgate = (da * up * (sg * (1.0 + gate * (1.0 - sg)))).astype(BF16)
        dup = (da * silu).astype(BF16)
        agu_ref[0] = dgate
        agu_ref[1] = dup
        agu_ref[2] = (silu * up).astype(BF16)
        dn = _dot_tn(dgate, wup_ref[0]) + _dot_tn(dup, wup_ref[1])

        @pl.when(j == 0)
        def _():
            acc_ref[...] = dn

        @pl.when(j > 0)
        def _():
            acc_ref[...] += dn

        @pl.when(j == nj - 1)
        def _():
            hh = h_ref[...]
            r = _rstd(hh)
            xhat = hh * r
            dnf = acc_ref[...]
            dxh = dnf * g_ref[...]
            dh_ref[...] = dho_ref[...] + r * (dxh - xhat * jnp.mean(dxh * xhat, axis=-1, keepdims=True))
            part = _sum8(dnf * xhat)

            @pl.when(i == 0)
            def _():
                dg_ref[...] = part

            @pl.when(i > 0)
            def _():
                dg_ref[...] += part

    return _pcall(
        body, name="ffn_bwd_act", grid=(ni, nj),
        in_specs=[pl.BlockSpec((tm, d), lambda i, j: (i, 0)),
                  pl.BlockSpec((tm, d), lambda i, j: (i, 0)),
                  pl.BlockSpec((1, d), lambda i, j: (0, 0)),
                  pl.BlockSpec((2, tf, tm), lambda i, j: (0, j, i)),
                  pl.BlockSpec((2, tf, d), lambda i, j: (0, j, 0)),
                  pl.BlockSpec((tf, d), lambda i, j: (j, 0)),
                  pl.BlockSpec(memory_space=pl.ANY)],
        out_specs=[pl.BlockSpec((tm, d), lambda i, j: (i, 0)),
                   pl.BlockSpec((3, tf, tm), lambda i, j: (0, j, i)),
                   pl.BlockSpec((tm, d), lambda i, j: (i, 0)),
                   pl.BlockSpec((8, d), lambda i, j: (0, 0))],
        out_shape=[jax.ShapeDtypeStruct((t, d), F32),
                   jax.ShapeDtypeStruct((3, f, t), BF16),
                   jax.ShapeDtypeStruct((t, d), BF16),
                   jax.ShapeDtypeStruct((8, d), F32)],
        scratch_shapes=[pltpu.VMEM((d, tm), BF16), pltpu.VMEM((tm, d), F32)],
        compiler_params=_params(("arbitrary", "arbitrary")),
    )(dho, h, g, gu, wup_t, wdn, dep)


def _ffn_bwd_w(agu, first, count, rhs, dep, name):
    _, f, t = agu.shape
    d = rhs.shape[1]
    tm, tf = _tok_block(t, 1024), _feat_block(f)
    ni, nj = t // tm, f // tf

    def body(lhs_ref, rhs_ref, dep_ref, dw_ref, acc_ref):
        i = pl.program_id(1)
        rb = rhs_ref[...]
        for k in range(count):
            part = _dot(lhs_ref[k], rb)

            @pl.when(i == 0)
            def _():
                acc_ref[k] = part

            @pl.when(i > 0)
            def _():
                acc_ref[k] += part

        @pl.when(i == ni - 1)
        def _():
            dw_ref[...] = acc_ref[...].astype(BF16)

    return _pcall(
        body, name=name, grid=(nj, ni),
        in_specs=[pl.BlockSpec((count, tf, tm), lambda j, i: (first // count, j, i)),
                  pl.BlockSpec((tm, d), lambda j, i: (i, 0)),
                  pl.BlockSpec(memory_space=pl.ANY)],
        out_specs=pl.BlockSpec((count, tf, d), lambda j, i: (0, j, 0)),
        out_shape=jax.ShapeDtypeStruct((count, f, d), BF16),
        scratch_shapes=[pltpu.VMEM((count, tf, d), F32)],
        compiler_params=_params(("parallel", "arbitrary")),
    )(agu, rhs, dep)


N_HEADS = N_SWA_HEADS + N_MEM_HEADS


def _q_col(hd):
    return COL_Q + HEAD_DIM * hd if hd < N_SWA_HEADS else COL_QM + HEAD_DIM * (hd - N_SWA_HEADS)


def _mix_proj_fwd(h, g, win_t):
    t, d = h.shape
    tm = _tok_block(t)

    def body(h_ref, g_ref, win_ref, p_ref, n_ref, qh_ref):
        hh = h_ref[...]
        n = (hh * _rstd(hh) * g_ref[...]).astype(BF16)
        n_ref[...] = n
        proj = _dot_nt(n, win_ref[...])
        p_ref[...] = proj.astype(BF16)
        for hd in range(N_HEADS):
            c0 = _q_col(hd)
            qh_ref[hd] = (proj[:, c0:c0 + HEAD_DIM] * SCALE).astype(BF16)

    return _pcall(
        body, name="mix_proj_fwd", grid=(t // tm,),
        in_specs=[pl.BlockSpec((tm, d), lambda i: (i, 0)),
                  pl.BlockSpec((1, d), lambda i: (0, 0)),
                  pl.BlockSpec((D_IN, d), lambda i: (0, 0))],
        out_specs=[pl.BlockSpec((tm, D_IN), lambda i: (i, 0)),
                   pl.BlockSpec((tm, d), lambda i: (i, 0)),
                   pl.BlockSpec((N_HEADS, tm, HEAD_DIM), lambda i: (0, i, 0))],
        out_shape=[jax.ShapeDtypeStruct((t, D_IN), BF16), jax.ShapeDtypeStruct((t, d), BF16),
                   jax.ShapeDtypeStruct((N_HEADS, t, HEAD_DIM), BF16)],
        compiler_params=_params(("parallel",)),
    )(h, g, win_t)


def _memkv_fwd(mem, g, wkv, dep):
    m, d = mem.shape

    def body(mem_ref, g_ref, w_ref, dep_ref, mkv_ref, nt_ref):
        mm = mem_ref[...]
        n = mm * _rstd(mm) * g_ref[...]
        nt_ref[...] = n.T.astype(BF16)
        mkv_ref[...] = _dot(n.astype(BF16), w_ref[...]).astype(BF16)

    return _pcall(
        body, name="memkv_fwd", grid=(1,),
        in_specs=[pl.BlockSpec((m, d), lambda i: (0, 0)),
                  pl.BlockSpec((1, d), lambda i: (0, 0)),
                  pl.BlockSpec((d, 2 * D_MEMQ), lambda i: (0, 0)),
                  pl.BlockSpec(memory_space=pl.ANY)],
        out_specs=[pl.BlockSpec((m, 2 * D_MEMQ), lambda i: (0, 0)),
                   pl.BlockSpec((d, m), lambda i: (0, 0))],
        out_shape=[jax.ShapeDtypeStruct((m, 2 * D_MEMQ), BF16), jax.ShapeDtypeStruct((d, m), BF16)],
        compiler_params=_params(("arbitrary",)),
    )(mem, g, wkv, dep)


def _memkv_bwd(dmkv, mem, g, wkv, nt):
    m, d = mem.shape

    def body(dmkv_ref, mem_ref, g_ref, w_ref, nt_ref, dw_ref, dg_ref):
        db = dmkv_ref[...].astype(BF16)
        dw_ref[...] = _dot(nt_ref[...], db).astype(BF16)
        dn = _dot_nt(db, w_ref[...])
        mm = mem_ref[...]
        dg_ref[...] = _sum8(dn * (mm * _rstd(mm)))

    return _pcall(
        body, name="memkv_bwd", grid=(1,),
        in_specs=[pl.BlockSpec((m, 2 * D_MEMQ), lambda i: (0, 0)),
                  pl.BlockSpec((m, d), lambda i: (0, 0)),
                  pl.BlockSpec((1, d), lambda i: (0, 0)),
                  pl.BlockSpec((d, 2 * D_MEMQ), lambda i: (0, 0)),
                  pl.BlockSpec((d, m), lambda i: (0, 0))],
        out_specs=[pl.BlockSpec((d, 2 * D_MEMQ), lambda i: (0, 0)),
                   pl.BlockSpec((8, d), lambda i: (0, 0))],
        out_shape=[jax.ShapeDtypeStruct((d, 2 * D_MEMQ), BF16), jax.ShapeDtypeStruct((8, d), F32)],
        compiler_params=_params(("arbitrary",)),
    )(dmkv, mem, g, wkv, nt)


def _shift_rows(v, k, edge_rows, row):
    out = pltpu.roll(v, k, 0)
    for r in range(k):
        out = jnp.where(row == r, edge_rows[r], out)
    return out


def _shift_rows_up(v, k, edge_rows, row):
    n = v.shape[0]
    out = pltpu.roll(v, n - k, 0)
    for r in range(k):
        out = jnp.where(row == n - k + r, edge_rows[r], out)
    return out


GROUP_ROWS = SWA_GROUP * BLOCK
BIAS_CUR, BIAS_PREV, BIAS_NONE = 0, 1, 2


def _bias_tables():
    tq = np.arange(BLOCK)[:, None]
    sk = np.arange(BLOCK)[None, :]
    slopes = np.asarray(SLOPES, np.float32)[:, None, None]
    cur = np.where(tq >= sk, -slopes * (tq - sk).astype(np.float32), NEG)
    prev = np.where(sk > tq, -slopes * (tq + BLOCK - sk).astype(np.float32), NEG)
    none = np.full_like(cur, NEG)
    tok = np.stack([cur, prev, none]).astype(np.float32).reshape(3, N_SWA_KV, GROUP_ROWS, BLOCK)
    return jnp.asarray(tok), jnp.asarray(np.ascontiguousarray(tok.transpose(0, 1, 3, 2)))


def _head_cols(hd):
    return D_CONV + HEAD_DIM * hd


def _stack_cols(ref, heads):
    return jnp.concatenate([ref[:, hd:hd + 1] for hd in heads], axis=0)


def _mix_core_fwd(p, qh, mkv, convw, sinks, bias_tok):
    t = p.shape[0]
    m = mkv.shape[0]
    nb = t // BLOCK

    def body(sk_ref, pc_ref, pkv_ref, ppc_ref, ppu_ref, qh_ref, mkv_ref, cw_ref, bc_ref, bp_ref, y_ref, l_ref):
        i = pl.program_id(0)
        prevf = (i > 0).astype(F32)
        row = lax.broadcasted_iota(jnp.int32, (BLOCK, D_CONV), 0)

        bg = pc_ref[:, COL_BG:COL_BG + D_CONV].astype(F32)
        cg = pc_ref[:, COL_CG:COL_CG + D_CONV].astype(F32)
        u = pc_ref[:, COL_U:COL_U + D_CONV].astype(F32)
        vv = cg * u
        pvv = ppc_ref[...].astype(F32) * ppu_ref[...].astype(F32) * prevf
        vv1 = _shift_rows(vv, 1, [pvv[15:16]], row)
        vv2 = _shift_rows(vv, 2, [pvv[14:15], pvv[15:16]], row)
        w = cw_ref[...]
        y_ref[:, 0:D_CONV] = bg * (w[0:1] * vv2 + w[1:2] * vv1 + w[2:3] * vv)

        lane = lax.broadcasted_iota(jnp.int32, (BLOCK, 128), 1)
        lse_all = jnp.zeros((BLOCK, 128), F32)
        for kv in range(N_SWA_KV):
            heads = range(kv * SWA_GROUP, (kv + 1) * SWA_GROUP)
            kc = pc_ref[:, COL_K + HEAD_DIM * kv:COL_K + HEAD_DIM * (kv + 1)]
            vc = pc_ref[:, COL_V + HEAD_DIM * kv:COL_V + HEAD_DIM * (kv + 1)]
            kp = pkv_ref[:, HEAD_DIM * kv:HEAD_DIM * (kv + 1)]
            vp = pkv_ref[:, D_KV + HEAD_DIM * kv:D_KV + HEAD_DIM * (kv + 1)]
            qg = qh_ref[kv * SWA_GROUP:(kv + 1) * SWA_GROUP].reshape(GROUP_ROWS, HEAD_DIM)
            sc = _dot_nt(qg, kc) + bc_ref[0, kv]
            sp = _dot_nt(qg, kp) + bp_ref[0, kv]
            sink = jnp.concatenate([jnp.full((BLOCK, 1), sk_ref[0, hd], F32) for hd in heads], axis=0)
            mx = jnp.maximum(jnp.max(jnp.maximum(sc, sp), axis=-1, keepdims=True), sink)
            ec = jnp.exp(sc - mx)
            ep = jnp.exp(sp - mx)
            den = jnp.sum(ec + ep, axis=-1, keepdims=True) + jnp.exp(sink - mx)
            o = (_dot(ec.astype(BF16), vc) + _dot(ep.astype(BF16), vp)) / den
            lse = mx + jnp.log(den)
            for gi, hd in enumerate(heads):
                rows = slice(gi * BLOCK, (gi + 1) * BLOCK)
                y_ref[:, _head_cols(hd):_head_cols(hd) + HEAD_DIM] = o[rows]
                lse_all = jnp.where(lane == hd, lse[rows], lse_all)

        for hm in range(N_MEM_HEADS):
            hd = N_SWA_HEADS + hm
            mk = mkv_ref[:, HEAD_DIM * hm:HEAD_DIM * (hm + 1)]
            mv = mkv_ref[:, D_MEMQ + HEAD_DIM * hm:D_MEMQ + HEAD_DIM * (hm + 1)]
            s = _dot_nt(qh_ref[hd], mk)
            mx = jnp.max(s, axis=-1, keepdims=True)
            e = jnp.exp(s - mx)
            den = jnp.sum(e, axis=-1, keepdims=True)
            y_ref[:, _head_cols(hd):_head_cols(hd) + HEAD_DIM] = _dot(e.astype(BF16), mv) / den
            lse_all = jnp.where(lane == hd, mx + jnp.log(den), lse_all)
        l_ref[...] = lse_all

    kv_col = COL_K // (2 * D_KV)
    bias_block = (1, N_SWA_KV, GROUP_ROWS, BLOCK)
    return _pcall(
        body, name="mix_core_fwd", grid=(nb,),
        in_specs=[pl.BlockSpec(memory_space=pltpu.SMEM),
                  pl.BlockSpec((BLOCK, D_IN), lambda i: (i, 0)),
                  pl.BlockSpec((BLOCK, 2 * D_KV), lambda i: (jnp.maximum(i - 1, 0), kv_col)),
                  pl.BlockSpec((16, D_CONV), lambda i: (jnp.maximum(i * (BLOCK // 16) - 1, 0), COL_CG // D_CONV)),
                  pl.BlockSpec((16, D_CONV), lambda i: (jnp.maximum(i * (BLOCK // 16) - 1, 0), COL_U // D_CONV)),
                  pl.BlockSpec((N_HEADS, BLOCK, HEAD_DIM), lambda i: (0, i, 0)),
                  pl.BlockSpec((m, 2 * D_MEMQ), lambda i: (0, 0)),
                  pl.BlockSpec((3, D_CONV), lambda i: (0, 0)),
                  pl.BlockSpec(bias_block, lambda i: (BIAS_CUR, 0, 0, 0)),
                  pl.BlockSpec(bias_block, lambda i: (jnp.where(i == 0, BIAS_NONE, BIAS_PREV), 0, 0, 0))],
        out_specs=[pl.BlockSpec((BLOCK, D_MIX), lambda i: (i, 0)),
                   pl.BlockSpec((BLOCK, 128), lambda i: (i, 0))],
        out_shape=[jax.ShapeDtypeStruct((t, D_MIX), F32), jax.ShapeDtypeStruct((t, 128), F32)],
        compiler_params=_params(("parallel",)),
    )(sinks, p, p, p, p, qh, mkv, convw, bias_tok, bias_tok)


def _mix_core_bwd(p, qh, dyconv, doh, delta, lse, mkv, convw, sinks, bias_tok, bias_key):
    t = p.shape[0]
    m = mkv.shape[0]
    nb = t // BLOCK

    def body(sk_ref, pc_ref, pkv_ref, ppc_ref, ppu_ref, pnb_ref, dyc_ref, dyn_ref, qc_ref, qn_ref, doc_ref, don_ref,
             dlc_ref, dln_ref, lc_ref, ln_ref, mkv_ref, cw_ref, bp_ref, bct_ref, bnt_ref,
             dp_ref, dmkv_ref, dcw_ref, dsk_ref):
        i = pl.program_id(0)
        prevf = (i > 0).astype(F32)
        nextf = (i < nb - 1).astype(F32)
        row = lax.broadcasted_iota(jnp.int32, (BLOCK, D_CONV), 0)

        @pl.when(i == 0)
        def _():
            dmkv_ref[...] = jnp.zeros_like(dmkv_ref)
            dcw_ref[...] = jnp.zeros_like(dcw_ref)
            dsk_ref[...] = jnp.zeros_like(dsk_ref)

        bg = pc_ref[:, COL_BG:COL_BG + D_CONV].astype(F32)
        cg = pc_ref[:, COL_CG:COL_CG + D_CONV].astype(F32)
        u = pc_ref[:, COL_U:COL_U + D_CONV].astype(F32)
        vv = cg * u
        pvv = ppc_ref[...].astype(F32) * ppu_ref[...].astype(F32) * prevf
        vv1 = _shift_rows(vv, 1, [pvv[15:16]], row)
        vv2 = _shift_rows(vv, 2, [pvv[14:15], pvv[15:16]], row)
        w = cw_ref[...]
        yconv = w[0:1] * vv2 + w[1:2] * vv1 + w[2:3] * vv
        dyo = dyc_ref[...]
        dyc = dyo * bg
        nxt = dyn_ref[...] * pnb_ref[...].astype(F32) * nextf
        d1 = _shift_rows_up(dyc, 1, [nxt[0:1]], row)
        d2 = _shift_rows_up(dyc, 2, [nxt[0:1], nxt[1:2]], row)
        dvv = w[2:3] * dyc + w[1:2] * d1 + w[0:1] * d2
        dp_ref[:, COL_BG:COL_BG + D_CONV] = (dyo * yconv).astype(BF16)
        dp_ref[:, COL_CG:COL_CG + D_CONV] = (dvv * u).astype(BF16)
        dp_ref[:, COL_U:COL_U + D_CONV] = (dvv * cg).astype(BF16)
        dcw_ref[0:1, :] += jnp.sum(dyc * vv2, axis=0, keepdims=True)
        dcw_ref[1:2, :] += jnp.sum(dyc * vv1, axis=0, keepdims=True)
        dcw_ref[2:3, :] += jnp.sum(dyc * vv, axis=0, keepdims=True)

        lse_t, dl_t = lc_ref[...].T, dlc_ref[...].T
        lse_nt, dl_nt = ln_ref[...].T, dln_ref[...].T

        def stack_rows(tile_t, heads):
            return jnp.concatenate([tile_t[hd:hd + 1, :] for hd in heads], axis=1)

        lane8 = jnp.where(lax.broadcasted_iota(jnp.int32, (8, 128), 0) == 0,
                          lax.broadcasted_iota(jnp.int32, (8, 128), 1), -1)
        dsk = jnp.zeros((8, 128), F32)
        for kv in range(N_SWA_KV):
            heads = range(kv * SWA_GROUP, (kv + 1) * SWA_GROUP)
            kc = pc_ref[:, COL_K + HEAD_DIM * kv:COL_K + HEAD_DIM * (kv + 1)]
            vc = pc_ref[:, COL_V + HEAD_DIM * kv:COL_V + HEAD_DIM * (kv + 1)]
            kp = pkv_ref[:, HEAD_DIM * kv:HEAD_DIM * (kv + 1)]
            vp = pkv_ref[:, D_KV + HEAD_DIM * kv:D_KV + HEAD_DIM * (kv + 1)]
            qg = qc_ref[kv * SWA_GROUP:(kv + 1) * SWA_GROUP].reshape(GROUP_ROWS, HEAD_DIM)
            dog = doc_ref[kv * SWA_GROUP:(kv + 1) * SWA_GROUP].reshape(GROUP_ROWS, HEAD_DIM)
            qn = qn_ref[kv * SWA_GROUP:(kv + 1) * SWA_GROUP].reshape(GROUP_ROWS, HEAD_DIM)
            don = don_ref[kv * SWA_GROUP:(kv + 1) * SWA_GROUP].reshape(GROUP_ROWS, HEAD_DIM)
            lse_col, dl_col = _stack_cols(lc_ref, heads), _stack_cols(dlc_ref, heads)
            pp_ = jnp.exp(_dot_nt(qg, kp) + bp_ref[0, kv] - lse_col)
            dsp = (pp_ * (_dot_nt(dog, vp) - dl_col)).astype(BF16)
            dq = _dot(dsp, kp)
            pt = jnp.exp(_dot_nt(kc, qg) + bct_ref[0, kv] - stack_rows(lse_t, heads))
            dst = (pt * (_dot_nt(vc, dog) - stack_rows(dl_t, heads))).astype(BF16)
            dv = _dot(pt.astype(BF16), dog)
            dk = _dot(dst, qg)
            dq = dq + _dot_tn(dst, kc)
            ptn = jnp.exp(_dot_nt(kc, qn) + bnt_ref[0, kv] - stack_rows(lse_nt, heads))
            dstn = (ptn * (_dot_nt(vc, don) - stack_rows(dl_nt, heads))).astype(BF16)
            dv = dv + _dot(ptn.astype(BF16), don)
            dk = dk + _dot(dstn, qn)
            dp_ref[:, COL_K + HEAD_DIM * kv:COL_K + HEAD_DIM * (kv + 1)] = dk.astype(BF16)
            dp_ref[:, COL_V + HEAD_DIM * kv:COL_V + HEAD_DIM * (kv + 1)] = dv.astype(BF16)
            sink = jnp.concatenate([jnp.full((BLOCK, 1), sk_ref[0, hd], F32) for hd in heads], axis=0)
            sink_term = jnp.exp(sink - lse_col) * dl_col
            for gi, hd in enumerate(heads):
                rows = slice(gi * BLOCK, (gi + 1) * BLOCK)
                dp_ref[:, _q_col(hd):_q_col(hd) + HEAD_DIM] = (dq[rows] * SCALE).astype(BF16)
                dsk = dsk + jnp.where(lane8 == hd, -jnp.sum(sink_term[rows], axis=0, keepdims=True), 0.0)
        dsk_ref[...] += dsk

        for hm in range(N_MEM_HEADS):
            hd = N_SWA_HEADS + hm
            qm, dom = qc_ref[hd], doc_ref[hd]
            mk = mkv_ref[:, HEAD_DIM * hm:HEAD_DIM * (hm + 1)]
            mv = mkv_ref[:, D_MEMQ + HEAD_DIM * hm:D_MEMQ + HEAD_DIM * (hm + 1)]
            pt = jnp.exp(_dot_nt(mk, qm) - lse_t[hd:hd + 1, :])
            dst = (pt * (_dot_nt(mv, dom) - dl_t[hd:hd + 1, :])).astype(BF16)
            dp_ref[:, _q_col(hd):_q_col(hd) + HEAD_DIM] = (_dot_tn(dst, mk) * SCALE).astype(BF16)
            dmkv_ref[:, HEAD_DIM * hm:HEAD_DIM * (hm + 1)] += _dot(dst, qm)
            dmkv_ref[:, D_MEMQ + HEAD_DIM * hm:D_MEMQ + HEAD_DIM * (hm + 1)] += _dot(pt.astype(BF16), dom)

    cur = lambda i: (i, 0)
    const = lambda i: (0, 0)
    rows16 = BLOCK // 16
    last16 = t // 16 - 1
    before = lambda col: (lambda i: (jnp.maximum(i * rows16 - 1, 0), col))
    after = lambda i: (jnp.minimum((i + 1) * rows16, last16), 0)
    heads_cur = lambda i: (0, i, 0)
    heads_next = lambda i: (0, jnp.minimum(i + 1, nb - 1), 0)
    stat_next = lambda i: (jnp.minimum(i + 1, nb - 1), 0)
    tok_block = (1, N_SWA_KV, GROUP_ROWS, BLOCK)
    key_block = (1, N_SWA_KV, BLOCK, GROUP_ROWS)
    head_block = (N_HEADS, BLOCK, HEAD_DIM)
    return _pcall(
        body, name="mix_core_bwd", grid=(nb,),
        in_specs=[pl.BlockSpec(memory_space=pltpu.SMEM),
                  pl.BlockSpec((BLOCK, D_IN), cur),
                  pl.BlockSpec((BLOCK, 2 * D_KV), lambda i: (jnp.maximum(i - 1, 0), COL_K // (2 * D_KV))),
                  pl.BlockSpec((16, D_CONV), before(COL_CG // D_CONV)),
                  pl.BlockSpec((16, D_CONV), before(COL_U // D_CONV)),
                  pl.BlockSpec((16, D_CONV), after),
                  pl.BlockSpec((BLOCK, D_CONV), cur),
                  pl.BlockSpec((16, D_CONV), after),
                  pl.BlockSpec(head_block, heads_cur), pl.BlockSpec(head_block, heads_next),
                  pl.BlockSpec(head_block, heads_cur), pl.BlockSpec(head_block, heads_next),
                  pl.BlockSpec((BLOCK, 128), cur), pl.BlockSpec((BLOCK, 128), stat_next),
                  pl.BlockSpec((BLOCK, 128), cur), pl.BlockSpec((BLOCK, 128), stat_next),
                  pl.BlockSpec((m, 2 * D_MEMQ), const),
                  pl.BlockSpec((3, D_CONV), const),
                  pl.BlockSpec(tok_block, lambda i: (jnp.where(i == 0, BIAS_NONE, BIAS_PREV), 0, 0, 0)),
                  pl.BlockSpec(key_block, lambda i: (BIAS_CUR, 0, 0, 0)),
                  pl.BlockSpec(key_block, lambda i: (jnp.where(i == nb - 1, BIAS_NONE, BIAS_PREV), 0, 0, 0))],
        out_specs=[pl.BlockSpec((BLOCK, D_IN), cur),
                   pl.BlockSpec((m, 2 * D_MEMQ), const),
                   pl.BlockSpec((8, D_CONV), const),
                   pl.BlockSpec((8, 128), const)],
        out_shape=[jax.ShapeDtypeStruct((t, D_IN), BF16),
                   jax.ShapeDtypeStruct((m, 2 * D_MEMQ), F32),
                   jax.ShapeDtypeStruct((8, D_CONV), F32),
                   jax.ShapeDtypeStruct((8, 128), F32)],
        compiler_params=_params(("arbitrary",)),
    )(sinks, p, p, p, p, p, dyconv, dyconv, qh, qh, doh, doh, delta, delta, lse, lse, mkv, convw,
      bias_tok, bias_key, bias_key)


def _group_norms(y):
    out = []
    for a, b in MIX_GROUPS:
        ys = y[:, a:b]
        r = _rstd(ys)
        out.append((ys * r, r))
    return out


def _mix_out_fwd(y, h, g, wout):
    t, d = h.shape
    tm = _tok_block(t)

    def body(y_ref, h_ref, g_ref, w_ref, ho_ref, mt_ref):
        yhat = jnp.concatenate([yh for yh, _ in _group_norms(y_ref[...])], axis=-1)
        mixed = yhat * g_ref[...]
        mt_ref[...] = mixed.T.astype(BF16)
        ho_ref[...] = h_ref[...] + _dot(mixed.astype(BF16), w_ref[...])

    return _pcall(
        body, name="mix_out_fwd", grid=(t // tm,),
        in_specs=[pl.BlockSpec((tm, D_MIX), lambda i: (i, 0)),
                  pl.BlockSpec((tm, d), lambda i: (i, 0)),
                  pl.BlockSpec((1, D_MIX), lambda i: (0, 0)),
                  pl.BlockSpec((D_MIX, d), lambda i: (0, 0))],
        out_specs=[pl.BlockSpec((tm, d), lambda i: (i, 0)),
                   pl.BlockSpec((D_MIX, tm), lambda i: (0, i))],
        out_shape=[jax.ShapeDtypeStruct((t, d), F32), jax.ShapeDtypeStruct((D_MIX, t), BF16)],
        compiler_params=_params(("parallel",)),
    )(y, h, g, wout)


def _head_indicator():
    ind = np.zeros((D_MIX, 128), np.float32)
    for hd in range(N_HEADS):
        ind[_head_cols(hd):_head_cols(hd) + HEAD_DIM, hd] = 1.0
    return jnp.asarray(ind, BF16)


def _mix_out_bwd(dho, y, g, wout, mt, dep):
    t, d = dho.shape
    tm = _tok_block(t)
    ni = t // tm

    def body(dho_ref, y_ref, g_ref, w_ref, mt_ref, ind_ref, dep_ref, dyc_ref, doh_ref, dl_ref, dw_ref, dg_ref, acc_ref):
        i = pl.program_id(0)
        dhb = dho_ref[...].astype(BF16)
        dm = _dot_nt(dhb, w_ref[...])
        pw = _dot(mt_ref[...], dhb)
        gg = g_ref[...]
        yy = y_ref[...]
        dys = []
        dgs = []
        for (a, b), (yhat, r) in zip(MIX_GROUPS, _group_norms(yy)):
            dmg = dm[:, a:b]
            dgs.append(_sum8(dmg * yhat))
            dyh = dmg * gg[:, a:b]
            dys.append(r * (dyh - yhat * jnp.mean(dyh * yhat, axis=-1, keepdims=True)))
        dy = jnp.concatenate(dys, axis=-1)
        dyc_ref[...] = dy[:, 0:D_CONV]
        for hd in range(N_HEADS):
            doh_ref[hd] = dy[:, _head_cols(hd):_head_cols(hd) + HEAD_DIM].astype(BF16)
        prod = dy * yy
        hi = prod.astype(BF16)
        lo = (prod - hi.astype(F32)).astype(BF16)
        dl_ref[...] = _dot(hi, ind_ref[...]) + _dot(lo, ind_ref[...])
        part = jnp.concatenate(dgs, axis=-1)

        @pl.when(i == 0)
        def _():
            acc_ref[...] = pw
            dg_ref[...] = part

        @pl.when(i > 0)
        def _():
            acc_ref[...] += pw
            dg_ref[...] += part

        @pl.when(i == ni - 1)
        def _():
            dw_ref[...] = acc_ref[...].astype(BF16)

    return _pcall(
        body, name="mix_out_bwd", grid=(ni,),
        in_specs=[pl.BlockSpec((tm, d), lambda i: (i, 0)),
                  pl.BlockSpec((tm, D_MIX), lambda i: (i, 0)),
                  pl.BlockSpec((1, D_MIX), lambda i: (0, 0)),
                  pl.BlockSpec((D_MIX, d), lambda i: (0, 0)),
                  pl.BlockSpec((D_MIX, tm), lambda i: (0, i)),
                  pl.BlockSpec((D_MIX, 128), lambda i: (0, 0)),
                  pl.BlockSpec(memory_space=pl.ANY)],
        out_specs=[pl.BlockSpec((tm, D_CONV), lambda i: (i, 0)),
                   pl.BlockSpec((N_HEADS, tm, HEAD_DIM), lambda i: (0, i, 0)),
                   pl.BlockSpec((tm, 128), lambda i: (i, 0)),
                   pl.BlockSpec((D_MIX, d), lambda i: (0, 0)),
                   pl.BlockSpec((8, D_MIX), lambda i: (0, 0))],
        out_shape=[jax.ShapeDtypeStruct((t, D_CONV), F32),
                   jax.ShapeDtypeStruct((N_HEADS, t, HEAD_DIM), BF16),
                   jax.ShapeDtypeStruct((t, 128), F32),
                   jax.ShapeDtypeStruct((D_MIX, d), BF16),
                   jax.ShapeDtypeStruct((8, D_MIX), F32)],
        scratch_shapes=[pltpu.VMEM((D_MIX, d), F32)],
        compiler_params=_params(("arbitrary",)),
    )(dho, y, g, wout, mt, _head_indicator(), dep)


def _mix_proj_bwd(dp, dho, h, g, win_t, n):
    t, d = h.shape
    tm = _tok_block(t)
    ni = t // tm

    def body(dp_ref, dho_ref, h_ref, g_ref, w_ref, n_ref, dh_ref, dw_ref, dg_ref, acc_ref):
        i = pl.program_id(0)
        dpb = dp_ref[...]
        dn = _dot(dpb, w_ref[...])
        pw = _dot_tn(dpb, n_ref[...])
        hh = h_ref[...]
        r = _rstd(hh)
        xhat = hh * r
        dxh = dn * g_ref[...]
        dh_ref[...] = dho_ref[...] + r * (dxh - xhat * jnp.mean(dxh * xhat, axis=-1, keepdims=True))
        part = _sum8(dn * xhat)

        @pl.when(i == 0)
        def _():
            acc_ref[...] = pw
            dg_ref[...] = part

        @pl.when(i > 0)
        def _():
            acc_ref[...] += pw
            dg_ref[...] += part

        @pl.when(i == ni - 1)
        def _():
            dw_ref[...] = acc_ref[...].astype(BF16)

    return _pcall(
        body, name="mix_proj_bwd", grid=(ni,),
        in_specs=[pl.BlockSpec((tm, D_IN), lambda i: (i, 0)),
                  pl.BlockSpec((tm, d), lambda i: (i, 0)),
                  pl.BlockSpec((tm, d), lambda i: (i, 0)),
                  pl.BlockSpec((1, d), lambda i: (0, 0)),
                  pl.BlockSpec((D_IN, d), lambda i: (0, 0)),
                  pl.BlockSpec((tm, d), lambda i: (i, 0))],
        out_specs=[pl.BlockSpec((tm, d), lambda i: (i, 0)),
                   pl.BlockSpec((D_IN, d), lambda i: (0, 0)),
                   pl.BlockSpec((8, d), lambda i: (0, 0))],
        out_shape=[jax.ShapeDtypeStruct((t, d), F32),
                   jax.ShapeDtypeStruct((D_IN, d), BF16),
                   jax.ShapeDtypeStruct((8, d), F32)],
        scratch_shapes=[pltpu.VMEM((D_IN, d), F32)],
        compiler_params=_params(("arbitrary",)),
    )(dp, dho, h, g, win_t, n)


def _final_loss(h, g, tgt):
    t, d = h.shape
    tm = _tok_block(t)

    def body(h_ref, g_ref, t_ref, dh_ref, ls_ref, dg_ref):
        i = pl.program_id(0)
        hh = h_ref[...]
        r = _rstd(hh)
        xhat = hh * r
        gg = g_ref[...]
        err = xhat * gg - t_ref[...]
        dy = err * (1.0 / d)
        dxh = dy * gg
        dh_ref[...] = r * (dxh - xhat * jnp.mean(dxh * xhat, axis=-1, keepdims=True))
        lpart = _sum8(err * err)
        gpart = _sum8(dy * xhat)

        @pl.when(i == 0)
        def _():
            ls_ref[...] = lpart
            dg_ref[...] = gpart

        @pl.when(i > 0)
        def _():
            ls_ref[...] += lpart
            dg_ref[...] += gpart

    return _pcall(
        body, name="final_loss", grid=(t // tm,),
        in_specs=[pl.BlockSpec((tm, d), lambda i: (i, 0)),
                  pl.BlockSpec((1, d), lambda i: (0, 0)),
                  pl.BlockSpec((tm, d), lambda i: (i, 0))],
        out_specs=[pl.BlockSpec((tm, d), lambda i: (i, 0)),
                   pl.BlockSpec((8, d), lambda i: (0, 0)),
                   pl.BlockSpec((8, d), lambda i: (0, 0))],
        out_shape=[jax.ShapeDtypeStruct((t, d), F32),
                   jax.ShapeDtypeStruct((8, d), F32),
                   jax.ShapeDtypeStruct((8, d), F32)],
        compiler_params=_params(("arbitrary",)),
    )(h, g, tgt)


def _position():
    return lax.axis_index("x"), lax.axis_index("y"), lax.axis_index("c")


def _flip(v, bit):
    return 1 - v if bit else v


def _peer(k):
    x, y, c = _position()
    return _flip(x, k & 4), _flip(y, k & 2), _flip(c, k & 1)


def _slot(px, py, pc):
    return 4 * px + 2 * py + pc


def _handshake(peers):
    barrier = pltpu.get_barrier_semaphore()
    for peer in peers:
        pl.semaphore_signal(barrier, inc=1, device_id=peer, device_id_type=MESH)
    pl.semaphore_wait(barrier, len(peers))


def _sequencer_call(body, name, collective_id, out_type, scratch_types, operands):
    return pl.kernel(
        body, out_type=out_type, mesh=plsc.ScalarSubcoreMesh(axis_name="sequencer", num_cores=1), name=name,
        scratch_types=scratch_types, compiler_params=pltpu.CompilerParams(collective_id=collective_id),
    )(*operands)


def _all_gather(shards, name, collective_id):
    nt = len(shards)

    def body(*refs):
        xs = refs[:nt]
        outs = refs[nt:2 * nt]
        send_sems, recv_sems, local_sems = refs[2 * nt:]
        x, y, c = _position()
        me, sibling = (x, y, c), (x, y, 1 - c)
        chips = [(1 - x, y), (x, 1 - y), (1 - x, 1 - y)]
        _handshake([sibling] + [(*chip, c) for chip in chips])

        def copy(t, k, block, to, src=None):
            dst = outs[t].at[_slot(*block)]
            return pltpu.make_async_remote_copy(
                src_ref=dst if src is None else src, dst_ref=dst,
                send_sem=send_sems.at[t, k], recv_sem=recv_sems.at[t, k],
                device_id=to, device_id_type=MESH)

        mine = [pltpu.make_async_copy(xs[t], outs[t].at[_slot(*me)], local_sems.at[t]) for t in range(nt)]
        for cp in mine:
            cp.start()
        first = []
        for t in range(nt):
            first.append(copy(t, 0, me, sibling, src=xs[t]))
            first += [copy(t, 1 + j, me, (*chip, c), src=xs[t]) for j, chip in enumerate(chips)]
        for cp in first:
            cp.start()
        passed = []
        for j, chip in enumerate(chips):
            for t in range(nt):
                copy(t, 1 + j, (*chip, c), me).wait_recv()
                fwd = copy(t, 4 + j, (*chip, c), sibling)
                fwd.start()
                passed.append(fwd)
        for t in range(nt):
            copy(t, 0, sibling, me).wait_recv()
            for j, chip in enumerate(chips):
                copy(t, 4 + j, (*chip, 1 - c), me).wait_recv()
        for cp in first + passed:
            cp.wait_send()
        for cp in mine:
            cp.wait()

    return _sequencer_call(
        body, name, collective_id,
        out_type=[jax.ShapeDtypeStruct((N_DEV,) + s.shape, s.dtype) for s in shards],
        scratch_types=[pltpu.SemaphoreType.DMA((nt, 7)), pltpu.SemaphoreType.DMA((nt, 7)),
                       pltpu.SemaphoreType.DMA((nt,))],
        operands=shards)


def _scatter_partials(partials, name, collective_id):
    nt = len(partials)

    def body(*refs):
        srcs = refs[:nt]
        outs = refs[nt:2 * nt]
        send_sems, recv_sems, local_sems = refs[2 * nt:]
        x, y, c = _position()
        _handshake([_peer(k) for k in range(1, N_DEV)])

        def copy(t, k):
            peer = _peer(k)
            return pltpu.make_async_remote_copy(
                src_ref=srcs[t].at[_slot(*peer)], dst_ref=outs[t].at[k],
                send_sem=send_sems.at[t, k - 1], recv_sem=recv_sems.at[t, k - 1],
                device_id=peer, device_id_type=MESH)

        mine = [pltpu.make_async_copy(srcs[t].at[_slot(x, y, c)], outs[t].at[0], local_sems.at[t]) for t in range(nt)]
        for cp in mine:
            cp.start()
        sent = [copy(t, k) for k in range(1, N_DEV) for t in range(nt)]
        for cp in sent:
            cp.start()
        for cp in sent:
            cp.wait_recv()
        for cp in sent:
            cp.wait_send()
        for cp in mine:
            cp.wait()

    return _sequencer_call(
        body, name, collective_id,
        out_type=[jax.ShapeDtypeStruct(p.shape, p.dtype) for p in partials],
        scratch_types=[pltpu.SemaphoreType.DMA((nt, 7)), pltpu.SemaphoreType.DMA((nt, 7)),
                       pltpu.SemaphoreType.DMA((nt,))],
        operands=partials)


def _all_reduce_rows(v):
    nv, _, w = v.shape

    def body(v_ref, out_ref, gath_ref, send_sems, recv_sems):
        x, y, c = _position()
        me = _slot(x, y, c)

        def copy(k):
            return pltpu.make_async_remote_copy(
                src_ref=v_ref, dst_ref=gath_ref.at[me],
                send_sem=send_sems.at[k - 1], recv_sem=recv_sems.at[k - 1],
                device_id=_peer(k), device_id_type=MESH)

        def arrival(k):
            return pltpu.make_async_remote_copy(
                src_ref=v_ref, dst_ref=gath_ref.at[_slot(*_peer(k))],
                send_sem=send_sems.at[k - 1], recv_sem=recv_sems.at[k - 1],
                device_id=_peer(k), device_id_type=MESH)

        sent = [copy(k) for k in range(1, N_DEV)]
        for cp in sent:
            cp.start()
        gath_ref[me] = v_ref[...]
        for k in range(1, N_DEV):
            arrival(k).wait_recv()
        for cp in sent:
            cp.wait_send()
        total = gath_ref[0]
        for s in range(1, N_DEV):
            total = total + gath_ref[s]
        out_ref[...] = jnp.sum(total, axis=1)

    vmem = pl.BlockSpec(memory_space=pltpu.VMEM)
    return _pcall(
        body, name="all_reduce_rows",
        in_specs=[vmem], out_specs=vmem,
        out_shape=jax.ShapeDtypeStruct((nv, w), F32),
        scratch_shapes=[pltpu.VMEM((N_DEV, nv, 8, w), F32),
                        pltpu.SemaphoreType.DMA((7,)), pltpu.SemaphoreType.DMA((7,))],
    )(v)


def _adamw_math(w, g, m, v):
    m2 = ADAM_B1 * m + (1.0 - ADAM_B1) * g
    v2 = ADAM_B2 * v + (1.0 - ADAM_B2) * (g * g)
    m_hat = m2 / (1.0 - ADAM_B1 ** ADAM_STEP)
    v_hat = v2 / (1.0 - ADAM_B2 ** ADAM_STEP)
    delta = -ADAM_LR * (m_hat / (jnp.sqrt(v_hat) + ADAM_EPS) + ADAM_WD * w)
    return delta, m2, v2


def _row_block(r):
    for cand in (256, 176, 128):
        if r % cand == 0:
            return cand
    return r


def _adamw_sharded(recv0, recv1, w, m, v):
    _, r, c = recv0.shape
    tr = _row_block(r)
    nr = r // tr

    def body(r0_ref, r1_ref, w_ref, m_ref, v_ref, g_ref, d_ref, m2_ref, v2_ref):
        layer = pl.program_id(0)

        def total(ref):
            acc = ref[0].astype(F32)
            for k in range(1, N_DEV):
                acc = acc + ref[k].astype(F32)
            return acc

        g = jnp.where(layer == 0, total(r0_ref), total(r1_ref))
        delta, m2, v2 = _adamw_math(w_ref[0], g, m_ref[0], v_ref[0])
        g_ref[0] = g
        d_ref[0] = delta
        m2_ref[0] = m2
        v2_ref[0] = v2

    shard = pl.BlockSpec((1, tr, c), lambda l, i: (l, i, 0))
    out = jax.ShapeDtypeStruct((2, r, c), F32)
    return _pcall(
        body, name="adamw_sharded", grid=(2, nr),
        in_specs=[pl.BlockSpec((N_DEV, tr, c), lambda l, i: (0, jnp.where(l == 0, i, nr - 1), 0)),
                  pl.BlockSpec((N_DEV, tr, c), lambda l, i: (0, jnp.where(l == 1, i, 0), 0)),
                  shard, shard, shard],
        out_specs=[shard, shard, shard, shard],
        out_shape=[out, out, out, out],
        compiler_params=_params(("arbitrary", "arbitrary")),
    )(recv0, recv1, w, m, v)


def _adamw_small(w, g, m, v):
    def body(w_ref, g_ref, m_ref, v_ref, d_ref, m2_ref, v2_ref):
        delta, m2, v2 = _adamw_math(w_ref[...], g_ref[...], m_ref[...], v_ref[...])
        d_ref[...] = delta
        m2_ref[...] = m2
        v2_ref[...] = v2

    spec = pl.BlockSpec(w.shape, lambda i: (0, 0))
    out = jax.ShapeDtypeStruct(w.shape, F32)
    return _pcall(
        body, name="adamw_small", grid=(1,),
        in_specs=[spec] * 4, out_specs=[spec] * 3, out_shape=[out] * 3,
        compiler_params=_params(("arbitrary",)),
    )(w, g, m, v)


def _pack(arrs):
    flat = jnp.concatenate([a.reshape(-1) for a in arrs])
    n = flat.shape[0]
    rows = -(-n // 1024) * 8
    return jnp.pad(flat, (0, rows * 128 - n)).reshape(rows, 128)


def _unpack(packed, like):
    flat = packed.reshape(-1)
    out, off = [], 0
    for a in like:
        out.append(flat[off:off + a.size].reshape(a.shape))
        off += a.size
    return out


def kernel(x, mem, g_ffn1, w_ffn1_up, w_ffn1_down, g_mix, w_in, conv_w, sinks, g_mem, w_mem_kv, g_grp, w_out, g_ffn2, w_ffn2_up, w_ffn2_down, g_final, loss_target, m_g_ffn1, m_w_ffn1_up, m_w_ffn1_down, m_g_mix, m_w_in, m_conv_w, m_sinks, m_g_mem, m_w_mem_kv, m_g_grp, m_w_out, m_g_ffn2, m_w_ffn2_up, m_w_ffn2_down, m_g_final, v_g_ffn1, v_w_ffn1_up, v_w_ffn1_down, v_g_mix, v_w_in, v_conv_w, v_sinks, v_g_mem, v_w_mem_kv, v_g_grp, v_w_out, v_g_ffn2, v_w_ffn2_up, v_w_ffn2_down, v_g_final):
    depth = g_ffn1.shape[0]
    t, d = x.shape[1], x.shape[2]
    width = max(d, D_MIX)
    me = _slot(*_position())
    conv_shard = conv_w.shape[2]

    xin, memin, tgt = x[0], mem[0], loss_target[0]

    conv_tile = jnp.zeros((depth * 8, 128), F32).at[:, :conv_shard].set(
        jnp.pad(conv_w, ((0, 0), (0, 8 - conv_w.shape[1]), (0, 0))).reshape(depth * 8, conv_shard))
    tr = lambda a: jnp.swapaxes(a, -1, -2)
    bf = lambda a: a.astype(BF16)
    weights = []
    collective_id = 0
    for l in range(depth):
        groups = [[bf(tr(w_ffn1_up[l])), bf(w_ffn1_down[l])] + ([conv_tile] if l == 0 else []),
                  [bf(tr(w_in[l])), bf(w_mem_kv[l]), bf(w_out[l])],
                  [bf(tr(w_ffn2_up[l])), bf(w_ffn2_down[l])]]
        full = []
        for gi, shards in enumerate(groups):
            full.append(_all_gather(shards, f"all_gather_l{l}_g{gi}", collective_id))
            collective_id += 1
        if l == 0:
            conv_full = full[0][2].reshape(N_DEV, depth, 8, 128)[:, :, :3, :conv_shard]
            conv_full = conv_full.transpose(1, 2, 0, 3).reshape(depth, 3, N_DEV * conv_shard)
        weights.append(dict(
            up1=full[0][0].reshape(2, -1, d), dn1=full[0][1].reshape(-1, d),
            win=full[1][0].reshape(D_IN, d), wkv=full[1][1].reshape(d, 2 * D_MEMQ), wout=full[1][2].reshape(D_MIX, d),
            up2=full[2][0].reshape(2, -1, d), dn2=full[2][1].reshape(-1, d)))

    row = lambda a: a.reshape(1, -1)
    bias_tok, bias_key = _bias_tables()

    h = xin
    saved = []
    for l in range(depth):
        wl = weights[l]
        s = dict(h0=h)
        h, s["gu1"], s["n1"] = _ffn_fwd(h, row(g_ffn1[l]), wl["up1"], wl["dn1"])
        s["h1"] = h
        s["p"], s["n_mix"], s["qh"] = _mix_proj_fwd(h, row(g_mix[l]), wl["win"])
        s["mkv"], s["nt_mem"] = _memkv_fwd(memin, row(g_mem[l]), wl["wkv"], s["p"])
        s["y"], s["lse"] = _mix_core_fwd(s["p"], s["qh"], s["mkv"], conv_full[l], row(sinks[l]), bias_tok)
        h, s["mt"] = _mix_out_fwd(s["y"], h, row(g_grp[l]), wl["wout"])
        s["h2"] = h
        h, s["gu2"], s["n2"] = _ffn_fwd(h, row(g_ffn2[l]), wl["up2"], wl["dn2"])
        saved.append(s)

    dh, loss_part, dg_final = _final_loss(h, row(g_final), tgt)

    small = {}
    recv = [None] * depth
    dep = loss_part

    def ffn_wgrads(agu, dyb, n, label):
        ddn = _ffn_bwd_w(agu, 2, 1, dyb, agu, f"ffn_bwd_w_down_{label}")
        dup = _ffn_bwd_w(agu, 0, 2, n, ddn, f"ffn_bwd_w_up_{label}")
        return dup.reshape(N_DEV, -1, d), ddn.reshape(N_DEV, -1, d)

    for l in reversed(range(depth)):
        wl, s = weights[l], saved[l]
        dh, agu, dyb, small["g_ffn2", l] = _ffn_bwd_act(dh, s["h2"], row(g_ffn2[l]), s["gu2"], wl["up2"], wl["dn2"], dep)
        dup2, ddn2 = ffn_wgrads(agu, dyb, s["n2"], f"l{l}_ffn2")
        got_ffn2 = _scatter_partials([dup2, ddn2], f"scatter_grads_l{l}_ffn2", collective_id)
        dyconv, doh, delta, dwout, small["g_grp", l] = _mix_out_bwd(dh, s["y"], row(g_grp[l]), wl["wout"], s["mt"], dup2)
        dp, dmkv, small["conv_w", l], small["sinks", l] = _mix_core_bwd(
            s["p"], s["qh"], dyconv, doh, delta, s["lse"], s["mkv"], conv_full[l], row(sinks[l]), bias_tok, bias_key)
        dwkv, small["g_mem", l] = _memkv_bwd(dmkv, memin, row(g_mem[l]), wl["wkv"], s["nt_mem"])
        dh, dwin, small["g_mix", l] = _mix_proj_bwd(dp, dh, s["h1"], row(g_mix[l]), wl["win"], s["n_mix"])
        got_mix = _scatter_partials([dwin.reshape(N_DEV, -1, d), dwkv.reshape(N_DEV, -1, 2 * D_MEMQ),
                                     dwout.reshape(N_DEV, -1, d)], f"scatter_grads_l{l}_mix", collective_id + 1)
        dh, agu, dyb, small["g_ffn1", l] = _ffn_bwd_act(dh, s["h0"], row(g_ffn1[l]), s["gu1"], wl["up1"], wl["dn1"], dwkv)
        dup1, ddn1 = ffn_wgrads(agu, dyb, s["n1"], f"l{l}_ffn1")
        if l > 0:
            got_ffn1 = _scatter_partials([dup1, ddn1], f"scatter_grads_l{l}_ffn1", collective_id + 2)
            collective_id += 3
        else:
            got_ffn1 = (_scatter_partials([ddn1], f"scatter_grads_l{l}_ffn1_down", collective_id + 2)
                        + _scatter_partials([dup1], f"scatter_grads_l{l}_ffn1_up", collective_id + 3))[::-1]
            collective_id += 4
        dep = dup1
        recv[l] = dict(w_ffn2_up=got_ffn2[0], w_ffn2_down=got_ffn2[1], w_in=got_mix[0], w_mem_kv=got_mix[1],
                       w_out=got_mix[2], w_ffn1_up=got_ffn1[0], w_ffn1_down=got_ffn1[1])
    grad_x = dh[None]

    sharded = {}
    big = [("w_ffn2_up", w_ffn2_up, m_w_ffn2_up, v_w_ffn2_up, True), ("w_ffn2_down", w_ffn2_down, m_w_ffn2_down, v_w_ffn2_down, False),
           ("w_in", w_in, m_w_in, v_w_in, True), ("w_mem_kv", w_mem_kv, m_w_mem_kv, v_w_mem_kv, False),
           ("w_out", w_out, m_w_out, v_w_out, False), ("w_ffn1_up", w_ffn1_up, m_w_ffn1_up, v_w_ffn1_up, True),
           ("w_ffn1_down", w_ffn1_down, m_w_ffn1_down, v_w_ffn1_down, False)]
    for name, w, m, v, transposed in big:
        if transposed:
            res = _adamw_sharded(recv[0][name], recv[depth - 1][name], tr(w), tr(m), tr(v))
            sharded[name] = tuple(tr(r) for r in res)
        else:
            sharded[name] = tuple(_adamw_sharded(recv[0][name], recv[depth - 1][name], w, m, v))

    def lanes(a):
        return jnp.pad(a, ((0, 0), (0, width - a.shape[1])))

    def first_row(a):
        return lanes(jnp.pad(a, ((0, 8 - a.shape[0]), (0, 0))))

    vec_names = ["g_ffn1", "g_mix", "g_mem", "g_grp", "g_ffn2", "sinks"]
    tiles = [lanes(small[n, l]) for n in vec_names for l in range(depth)]
    tiles += [first_row(small["conv_w", l][k:k + 1]) for l in range(depth) for k in range(3)]
    tiles.append(lanes(dg_final))
    n_real = len(tiles)
    tiles.append(lanes(loss_part))
    tiles += [jnp.zeros((8, width), F32)] * (-len(tiles) % 8)
    summed = _all_reduce_rows(jnp.stack(tiles))
    loss = 0.5 * jnp.sum(summed[n_real]) / d

    def vec(n, wd):
        return jnp.stack([summed[vec_names.index(n) * depth + l, :wd] for l in range(depth)])

    conv_base = len(vec_names) * depth
    conv_grad = jnp.stack([jnp.stack([summed[conv_base + 3 * l + k, :D_CONV] for k in range(3)]) for l in range(depth)])
    grads_small = {
        "g_ffn1": vec("g_ffn1", d), "g_mix": vec("g_mix", d), "g_mem": vec("g_mem", d),
        "g_grp": vec("g_grp", D_MIX), "g_ffn2": vec("g_ffn2", d), "sinks": vec("sinks", N_SWA_HEADS),
        "conv_w": lax.dynamic_slice_in_dim(conv_grad, me * conv_shard, conv_shard, axis=2),
        "g_final": summed[n_real - 1, :d],
    }
    small_w = [("g_ffn1", g_ffn1, m_g_ffn1, v_g_ffn1), ("g_mix", g_mix, m_g_mix, v_g_mix),
               ("conv_w", conv_w, m_conv_w, v_conv_w), ("sinks", sinks, m_sinks, v_sinks),
               ("g_mem", g_mem, m_g_mem, v_g_mem), ("g_grp", g_grp, m_g_grp, v_g_grp),
               ("g_ffn2", g_ffn2, m_g_ffn2, v_g_ffn2), ("g_final", g_final, m_g_final, v_g_final)]
    like = [w for _, w, _, _ in small_w]
    packed = _adamw_small(_pack(like), _pack([grads_small[n] for n, _, _, _ in small_w]),
                          _pack([m for _, _, m, _ in small_w]), _pack([v for _, _, _, v in small_w]))
    small_out = {n: (grads_small[n], dl, m2, v2)
                 for (n, _, _, _), dl, m2, v2 in zip(small_w, *[_unpack(pk, like) for pk in packed])}

    order = ["g_ffn1", "w_ffn1_up", "w_ffn1_down", "g_mix", "w_in", "conv_w", "sinks", "g_mem", "w_mem_kv", "g_grp",
             "w_out", "g_ffn2", "w_ffn2_up", "w_ffn2_down", "g_final"]
    results = {**sharded, **small_out}
    outs = [loss, grad_x]
    for part in range(4):
        outs += [results[n][part] for n in order]
    return tuple(outs)
```

```python
import numpy as np
import jax
import jax.numpy as jnp
from jax import lax
from jax.experimental import pallas as pl
from jax.experimental.pallas import tpu as pltpu
from jax.experimental.pallas import tpu_sc as plsc

F32 = jnp.float32
BF16 = jnp.bfloat16

N_DEV = 8
EPS = 1e-6
N_SWA_HEADS = 8
N_SWA_KV = 2
SWA_GROUP = N_SWA_HEADS // N_SWA_KV
HEAD_DIM = 64
N_MEM_HEADS = 4
D_CONV = 256
BLOCK = 128
D_SWA = N_SWA_HEADS * HEAD_DIM
D_KV = N_SWA_KV * HEAD_DIM
D_MEMQ = N_MEM_HEADS * HEAD_DIM
D_MIX = D_CONV + D_SWA + D_MEMQ
D_IN = 3 * D_CONV + D_SWA + 2 * D_KV + D_MEMQ
COL_BG, COL_CG, COL_U = 0, D_CONV, 2 * D_CONV
COL_Q = 3 * D_CONV
COL_K = COL_Q + D_SWA
COL_V = COL_K + D_KV
COL_QM = COL_V + D_KV
MIX_GROUPS = ((0, D_CONV), (D_CONV, D_CONV + D_SWA), (D_CONV + D_SWA, D_MIX))
SLOPES = tuple(2.0 ** (-8.0 * (i + 1) / N_SWA_HEADS) for i in range(N_SWA_HEADS))
SCALE = HEAD_DIM ** -0.5
NEG = -1e30

ADAM_LR = 0.001
ADAM_B1 = 0.9
ADAM_B2 = 0.999
ADAM_EPS = 1e-08
ADAM_WD = 0.01
ADAM_STEP = 10

V7X_VMEM_BYTES = 64 * 1024 * 1024
VMEM_LIMIT = (V7X_VMEM_BYTES * 3) // 4
MESH = pl.DeviceIdType.MESH


def _pcall(body, **kw):
    return pl.pallas_call(body, **kw)


def _params(sem=None, vmem=VMEM_LIMIT):
    return pltpu.CompilerParams(dimension_semantics=sem, vmem_limit_bytes=vmem)


def _dot(a, b):
    return lax.dot_general(a, b, (((1,), (0,)), ((), ())), preferred_element_type=F32)


def _dot_nt(a, b):
    return lax.dot_general(a, b, (((1,), (1,)), ((), ())), preferred_element_type=F32)


def _dot_tn(a, b):
    return lax.dot_general(a, b, (((0,), (0,)), ((), ())), preferred_element_type=F32)


def _rstd(x):
    return lax.rsqrt(jnp.mean(x * x, axis=-1, keepdims=True) + EPS)


def _sigmoid(x):
    return 1.0 / (1.0 + jnp.exp(-x))


def _sum8(x):
    r, w = x.shape
    return jnp.sum(x.reshape(r // 8, 8, w), axis=0)


def _tok_block(t, rows=512):
    return min(rows, t)


def _feat_block(f):
    return f // (N_DEV // 2)


def _ffn_fwd(h, g, wup_t, wdn):
    t, d = h.shape
    f = wdn.shape[0]
    tm, tf = _tok_block(t), _feat_block(f)
    ni, nj = t // tm, f // tf

    def body(h_ref, g_ref, wup_ref, wdn_ref, ho_ref, gu_ref, n_ref, nt_ref, acc_ref):
        j = pl.program_id(1)

        @pl.when(j == 0)
        def _():
            hh = h_ref[...]
            n = hh * _rstd(hh) * g_ref[...]
            n_ref[...] = n.astype(BF16)
            nt_ref[...] = n.T.astype(BF16)
            acc_ref[...] = jnp.zeros_like(acc_ref)

        nt = nt_ref[...]
        gate = _dot(wup_ref[0], nt)
        up = _dot(wup_ref[1], nt)
        gu_ref[0] = gate.astype(BF16)
        gu_ref[1] = up.astype(BF16)
        a = gate * _sigmoid(gate) * up
        acc_ref[...] += _dot_tn(a.astype(BF16), wdn_ref[...])

        @pl.when(j == nj - 1)
        def _():
            ho_ref[...] = h_ref[...] + 0.5 * acc_ref[...]

    return _pcall(
        body, name="ffn_fwd", grid=(ni, nj),
        in_specs=[pl.BlockSpec((tm, d), lambda i, j: (i, 0)),
                  pl.BlockSpec((1, d), lambda i, j: (0, 0)),
                  pl.BlockSpec((2, tf, d), lambda i, j: (0, j, 0)),
                  pl.BlockSpec((tf, d), lambda i, j: (j, 0))],
        out_specs=[pl.BlockSpec((tm, d), lambda i, j: (i, 0)),
                   pl.BlockSpec((2, tf, tm), lambda i, j: (0, j, i)),
                   pl.BlockSpec((tm, d), lambda i, j: (i, 0))],
        out_shape=[jax.ShapeDtypeStruct((t, d), F32),
                   jax.ShapeDtypeStruct((2, f, t), BF16),
                   jax.ShapeDtypeStruct((t, d), BF16)],
        scratch_shapes=[pltpu.VMEM((d, tm), BF16), pltpu.VMEM((tm, d), F32)],
        compiler_params=_params(("parallel", "arbitrary")),
    )(h, g, wup_t, wdn)


def _ffn_bwd_act(dho, h, g, gu, wup_t, wdn, dep):
    t, d = h.shape
    f = wdn.shape[0]
    tm, tf = _tok_block(t), _feat_block(f)
    ni, nj = t // tm, f // tf

    def body(dho_ref, h_ref, g_ref, gu_ref, wup_ref, wdn_ref, dep_ref, dh_ref, agu_ref, dyb_ref, dg_ref, dyt_ref, acc_ref):
        i = pl.program_id(0)
        j = pl.program_id(1)

        @pl.when(j == 0)
        def _():
            dy0 = 0.5 * dho_ref[...]
            dyb_ref[...] = dy0.astype(BF16)
            dyt_ref[...] = dy0.T.astype(BF16)

        da = _dot(wdn_ref[...], dyt_ref[...])
        gate = gu_ref[0].astype(F32)
        up = gu_ref[1].astype(F32)
        sg = _sigmoid(gate)
        silu = gate * sg
        dgate = (da * up * (sg * (1.0 + gate * (1.0 - sg)))).astype(BF16)
        dup = (da * silu).astype(BF16)
        agu_ref[0] = dgate
        agu_ref[1] = dup
        agu_ref[2] = (silu * up).astype(BF16)
        dn = _dot_tn(dgate, wup_ref[0]) + _dot_tn(dup, wup_ref[1])

        @pl.when(j == 0)
        def _():
            acc_ref[...] = dn

        @pl.when(j > 0)
        def _():
            acc_ref[...] += dn

        @pl.when(j == nj - 1)
        def _():
            hh = h_ref[...]
            r = _rstd(hh)
            xhat = hh * r
            dnf = acc_ref[...]
            dxh = dnf * g_ref[...]
            dh_ref[...] = dho_ref[...] + r * (dxh - xhat * jnp.mean(dxh * xhat, axis=-1, keepdims=True))
            part = _sum8(dnf * xhat)

            @pl.when(i == 0)
            def _():
                dg_ref[...] = part

            @pl.when(i > 0)
            def _():
                dg_ref[...] += part

    return _pcall(
        body, name="ffn_bwd_act", grid=(ni, nj),
        in_specs=[pl.BlockSpec((tm, d), lambda i, j: (i, 0)),
                  pl.BlockSpec((tm, d), lambda i, j: (i, 0)),
                  pl.BlockSpec((1, d), lambda i, j: (0, 0)),
                  pl.BlockSpec((2, tf, tm), lambda i, j: (0, j, i)),
                  pl.BlockSpec((2, tf, d), lambda i, j: (0, j, 0)),
                  pl.BlockSpec((tf, d), lambda i, j: (j, 0)),
                  pl.BlockSpec(memory_space=pl.ANY)],
        out_specs=[pl.BlockSpec((tm, d), lambda i, j: (i, 0)),
                   pl.BlockSpec((3, tf, tm), lambda i, j: (0, j, i)),
                   pl.BlockSpec((tm, d), lambda i, j: (i, 0)),
                   pl.BlockSpec((8, d), lambda i, j: (0, 0))],
        out_shape=[jax.ShapeDtypeStruct((t, d), F32),
                   jax.ShapeDtypeStruct((3, f, t), BF16),
                   jax.ShapeDtypeStruct((t, d), BF16),
                   jax.ShapeDtypeStruct((8, d), F32)],
        scratch_shapes=[pltpu.VMEM((d, tm), BF16), pltpu.VMEM((tm, d), F32)],
        compiler_params=_params(("arbitrary", "arbitrary")),
    )(dho, h, g, gu, wup_t, wdn, dep)


def _ffn_bwd_w(agu, first, count, rhs, dep, name):
    _, f, t = agu.shape
    d = rhs.shape[1]
    tm, tf = _tok_block(t, 1024), _feat_block(f)
    ni, nj = t // tm, f // tf

    def body(lhs_ref, rhs_ref, dep_ref, dw_ref, acc_ref):
        i = pl.program_id(1)
        rb = rhs_ref[...]
        for k in range(count):
            part = _dot(lhs_ref[k], rb)

            @pl.when(i == 0)
            def _():
                acc_ref[k] = part

            @pl.when(i > 0)
            def _():
                acc_ref[k] += part

        @pl.when(i == ni - 1)
        def _():
            dw_ref[...] = acc_ref[...].astype(BF16)

    return _pcall(
        body, name=name, grid=(nj, ni),
        in_specs=[pl.BlockSpec((count, tf, tm), lambda j, i: (first // count, j, i)),
                  pl.BlockSpec((tm, d), lambda j, i: (i, 0)),
                  pl.BlockSpec(memory_space=pl.ANY)],
        out_specs=pl.BlockSpec((count, tf, d), lambda j, i: (0, j, 0)),
        out_shape=jax.ShapeDtypeStruct((count, f, d), BF16),
        scratch_shapes=[pltpu.VMEM((count, tf, d), F32)],
        compiler_params=_params(("parallel", "arbitrary")),
    )(agu, rhs, dep)


N_HEADS = N_SWA_HEADS + N_MEM_HEADS


def _q_col(hd):
    return COL_Q + HEAD_DIM * hd if hd < N_SWA_HEADS else COL_QM + HEAD_DIM * (hd - N_SWA_HEADS)


def _mix_proj_fwd(h, g, win_t):
    t, d = h.shape
    tm = _tok_block(t)

    def body(h_ref, g_ref, win_ref, p_ref, n_ref, qh_ref):
        hh = h_ref[...]
        n = (hh * _rstd(hh) * g_ref[...]).astype(BF16)
        n_ref[...] = n
        proj = _dot_nt(n, win_ref[...])
        p_ref[...] = proj.astype(BF16)
        for hd in range(N_HEADS):
            c0 = _q_col(hd)
            qh_ref[hd] = (proj[:, c0:c0 + HEAD_DIM] * SCALE).astype(BF16)

    return _pcall(
        body, name="mix_proj_fwd", grid=(t // tm,),
        in_specs=[pl.BlockSpec((tm, d), lambda i: (i, 0)),
                  pl.BlockSpec((1, d), lambda i: (0, 0)),
                  pl.BlockSpec((D_IN, d), lambda i: (0, 0))],
        out_specs=[pl.BlockSpec((tm, D_IN), lambda i: (i, 0)),
                   pl.BlockSpec((tm, d), lambda i: (i, 0)),
                   pl.BlockSpec((N_HEADS, tm, HEAD_DIM), lambda i: (0, i, 0))],
        out_shape=[jax.ShapeDtypeStruct((t, D_IN), BF16), jax.ShapeDtypeStruct((t, d), BF16),
                   jax.ShapeDtypeStruct((N_HEADS, t, HEAD_DIM), BF16)],
        compiler_params=_params(("parallel",)),
    )(h, g, win_t)


def _memkv_fwd(mem, g, wkv, dep):
    m, d = mem.shape

    def body(mem_ref, g_ref, w_ref, dep_ref, mkv_ref, nt_ref):
        mm = mem_ref[...]
        n = mm * _rstd(mm) * g_ref[...]
        nt_ref[...] = n.T.astype(BF16)
        mkv_ref[...] = _dot(n.astype(BF16), w_ref[...]).astype(BF16)

    return _pcall(
        body, name="memkv_fwd", grid=(1,),
        in_specs=[pl.BlockSpec((m, d), lambda i: (0, 0)),
                  pl.BlockSpec((1, d), lambda i: (0, 0)),
                  pl.BlockSpec((d, 2 * D_MEMQ), lambda i: (0, 0)),
                  pl.BlockSpec(memory_space=pl.ANY)],
        out_specs=[pl.BlockSpec((m, 2 * D_MEMQ), lambda i: (0, 0)),
                   pl.BlockSpec((d, m), lambda i: (0, 0))],
        out_shape=[jax.ShapeDtypeStruct((m, 2 * D_MEMQ), BF16), jax.ShapeDtypeStruct((d, m), BF16)],
        compiler_params=_params(("arbitrary",)),
    )(mem, g, wkv, dep)


def _memkv_bwd(dmkv, mem, g, wkv, nt):
    m, d = mem.shape

    def body(dmkv_ref, mem_ref, g_ref, w_ref, nt_ref, dw_ref, dg_ref):
        db = dmkv_ref[...].astype(BF16)
        dw_ref[...] = _dot(nt_ref[...], db).astype(BF16)
        dn = _dot_nt(db, w_ref[...])
        mm = mem_ref[...]
        dg_ref[...] = _sum8(dn * (mm * _rstd(mm)))

    return _pcall(
        body, name="memkv_bwd", grid=(1,),
        in_specs=[pl.BlockSpec((m, 2 * D_MEMQ), lambda i: (0, 0)),
                  pl.BlockSpec((m, d), lambda i: (0, 0)),
                  pl.BlockSpec((1, d), lambda i: (0, 0)),
                  pl.BlockSpec((d, 2 * D_MEMQ), lambda i: (0, 0)),
                  pl.BlockSpec((d, m), lambda i: (0, 0))],
        out_specs=[pl.BlockSpec((d, 2 * D_MEMQ), lambda i: (0, 0)),
                   pl.BlockSpec((8, d), lambda i: (0, 0))],
        out_shape=[jax.ShapeDtypeStruct((d, 2 * D_MEMQ), BF16), jax.ShapeDtypeStruct((8, d), F32)],
        compiler_params=_params(("arbitrary",)),
    )(dmkv, mem, g, wkv, nt)


def _shift_rows(v, k, edge_rows, row):
    out = pltpu.roll(v, k, 0)
    for r in range(k):
        out = jnp.where(row == r, edge_rows[r], out)
    return out


def _shift_rows_up(v, k, edge_rows, row):
    n = v.shape[0]
    out = pltpu.roll(v, n - k, 0)
    for r in range(k):
        out = jnp.where(row == n - k + r, edge_rows[r], out)
    return out


GROUP_ROWS = SWA_GROUP * BLOCK
BIAS_CUR, BIAS_PREV, BIAS_NONE = 0, 1, 2


def _bias_tables():
    tq = np.arange(BLOCK)[:, None]
    sk = np.arange(BLOCK)[None, :]
    slopes = np.asarray(SLOPES, np.float32)[:, None, None]
    cur = np.where(tq >= sk, -slopes * (tq - sk).astype(np.float32), NEG)
    prev = np.where(sk > tq, -slopes * (tq + BLOCK - sk).astype(np.float32), NEG)
    none = np.full_like(cur, NEG)
    tok = np.stack([cur, prev, none]).astype(np.float32).reshape(3, N_SWA_KV, GROUP_ROWS, BLOCK)
    return jnp.asarray(tok), jnp.asarray(np.ascontiguousarray(tok.transpose(0, 1, 3, 2)))


def _head_cols(hd):
    return D_CONV + HEAD_DIM * hd


def _stack_cols(ref, heads):
    return jnp.concatenate([ref[:, hd:hd + 1] for hd in heads], axis=0)


def _mix_core_fwd(p, qh, mkv, convw, sinks, bias_tok):
    t = p.shape[0]
    m = mkv.shape[0]
    nb = t // BLOCK

    def body(sk_ref, pc_ref, pkv_ref, ppc_ref, ppu_ref, qh_ref, mkv_ref, cw_ref, bc_ref, bp_ref, y_ref, l_ref):
        i = pl.program_id(0)
        prevf = (i > 0).astype(F32)
        row = lax.broadcasted_iota(jnp.int32, (BLOCK, D_CONV), 0)

        bg = pc_ref[:, COL_BG:COL_BG + D_CONV].astype(F32)
        cg = pc_ref[:, COL_CG:COL_CG + D_CONV].astype(F32)
        u = pc_ref[:, COL_U:COL_U + D_CONV].astype(F32)
        vv = cg * u
        pvv = ppc_ref[...].astype(F32) * ppu_ref[...].astype(F32) * prevf
        vv1 = _shift_rows(vv, 1, [pvv[15:16]], row)
        vv2 = _shift_rows(vv, 2, [pvv[14:15], pvv[15:16]], row)
        w = cw_ref[...]
        y_ref[:, 0:D_CONV] = bg * (w[0:1] * vv2 + w[1:2] * vv1 + w[2:3] * vv)

        lane = lax.broadcasted_iota(jnp.int32, (BLOCK, 128), 1)
        lse_all = jnp.zeros((BLOCK, 128), F32)
        for kv in range(N_SWA_KV):
            heads = range(kv * SWA_GROUP, (kv + 1) * SWA_GROUP)
            kc = pc_ref[:, COL_K + HEAD_DIM * kv:COL_K + HEAD_DIM * (kv + 1)]
            vc = pc_ref[:, COL_V + HEAD_DIM * kv:COL_V + HEAD_DIM * (kv + 1)]
            kp = pkv_ref[:, HEAD_DIM * kv:HEAD_DIM * (kv + 1)]
            vp = pkv_ref[:, D_KV + HEAD_DIM * kv:D_KV + HEAD_DIM * (kv + 1)]
            qg = qh_ref[kv * SWA_GROUP:(kv + 1) * SWA_GROUP].reshape(GROUP_ROWS, HEAD_DIM)
            sc = _dot_nt(qg, kc) + bc_ref[0, kv]
            sp = _dot_nt(qg, kp) + bp_ref[0, kv]
            sink = jnp.concatenate([jnp.full((BLOCK, 1), sk_ref[0, hd], F32) for hd in heads], axis=0)
            mx = jnp.maximum(jnp.max(jnp.maximum(sc, sp), axis=-1, keepdims=True), sink)
            ec = jnp.exp(sc - mx)
            ep = jnp.exp(sp - mx)
            den = jnp.sum(ec + ep, axis=-1, keepdims=True) + jnp.exp(sink - mx)
            o = (_dot(ec.astype(BF16), vc) + _dot(ep.astype(BF16), vp)) / den
            lse = mx + jnp.log(den)
            for gi, hd in enumerate(heads):
                rows = slice(gi * BLOCK, (gi + 1) * BLOCK)
                y_ref[:, _head_cols(hd):_head_cols(hd) + HEAD_DIM] = o[rows]
                lse_all = jnp.where(lane == hd, lse[rows], lse_all)

        for hm in range(N_MEM_HEADS):
            hd = N_SWA_HEADS + hm
            mk = mkv_ref[:, HEAD_DIM * hm:HEAD_DIM * (hm + 1)]
            mv = mkv_ref[:, D_MEMQ + HEAD_DIM * hm:D_MEMQ + HEAD_DIM * (hm + 1)]
            s = _dot_nt(qh_ref[hd], mk)
            mx = jnp.max(s, axis=-1, keepdims=True)
            e = jnp.exp(s - mx)
            den = jnp.sum(e, axis=-1, keepdims=True)
            y_ref[:, _head_cols(hd):_head_cols(hd) + HEAD_DIM] = _dot(e.astype(BF16), mv) / den
            lse_all = jnp.where(lane == hd, mx + jnp.log(den), lse_all)
        l_ref[...] = lse_all

    kv_col = COL_K // (2 * D_KV)
    bias_block = (1, N_SWA_KV, GROUP_ROWS, BLOCK)
    return _pcall(
        body, name="mix_core_fwd", grid=(nb,),
        in_specs=[pl.BlockSpec(memory_space=pltpu.SMEM),
                  pl.BlockSpec((BLOCK, D_IN), lambda i: (i, 0)),
                  pl.BlockSpec((BLOCK, 2 * D_KV), lambda i: (jnp.maximum(i - 1, 0), kv_col)),
                  pl.BlockSpec((16, D_CONV), lambda i: (jnp.maximum(i * (BLOCK // 16) - 1, 0), COL_CG // D_CONV)),
                  pl.BlockSpec((16, D_CONV), lambda i: (jnp.maximum(i * (BLOCK // 16) - 1, 0), COL_U // D_CONV)),
                  pl.BlockSpec((N_HEADS, BLOCK, HEAD_DIM), lambda i: (0, i, 0)),
                  pl.BlockSpec((m, 2 * D_MEMQ), lambda i: (0, 0)),
                  pl.BlockSpec((3, D_CONV), lambda i: (0, 0)),
                  pl.BlockSpec(bias_block, lambda i: (BIAS_CUR, 0, 0, 0)),
                  pl.BlockSpec(bias_block, lambda i: (jnp.where(i == 0, BIAS_NONE, BIAS_PREV), 0, 0, 0))],
        out_specs=[pl.BlockSpec((BLOCK, D_MIX), lambda i: (i, 0)),
                   pl.BlockSpec((BLOCK, 128), lambda i: (i, 0))],
        out_shape=[jax.ShapeDtypeStruct((t, D_MIX), F32), jax.ShapeDtypeStruct((t, 128), F32)],
        compiler_params=_params(("parallel",)),
    )(sinks, p, p, p, p, qh, mkv, convw, bias_tok, bias_tok)


def _mix_core_bwd(p, qh, dyconv, doh, delta, lse, mkv, convw, sinks, bias_tok, bias_key):
    t = p.shape[0]
    m = mkv.shape[0]
    nb = t // BLOCK

    def body(sk_ref, pc_ref, pkv_ref, ppc_ref, ppu_ref, pnb_ref, dyc_ref, dyn_ref, qc_ref, qn_ref, doc_ref, don_ref,
             dlc_ref, dln_ref, lc_ref, ln_ref, mkv_ref, cw_ref, bp_ref, bct_ref, bnt_ref,
             dp_ref, dmkv_ref, dcw_ref, dsk_ref):
        i = pl.program_id(0)
        prevf = (i > 0).astype(F32)
        nextf = (i < nb - 1).astype(F32)
        row = lax.broadcasted_iota(jnp.int32, (BLOCK, D_CONV), 0)

        @pl.when(i == 0)
        def _():
            dmkv_ref[...] = jnp.zeros_like(dmkv_ref)
            dcw_ref[...] = jnp.zeros_like(dcw_ref)
            dsk_ref[...] = jnp.zeros_like(dsk_ref)

        bg = pc_ref[:, COL_BG:COL_BG + D_CONV].astype(F32)
        cg = pc_ref[:, COL_CG:COL_CG + D_CONV].astype(F32)
        u = pc_ref[:, COL_U:COL_U + D_CONV].astype(F32)
        vv = cg * u
        pvv = ppc_ref[...].astype(F32) * ppu_ref[...].astype(F32) * prevf
        vv1 = _shift_rows(vv, 1, [pvv[15:16]], row)
        vv2 = _shift_rows(vv, 2, [pvv[14:15], pvv[15:16]], row)
        w = cw_ref[...]
        yconv = w[0:1] * vv2 + w[1:2] * vv1 + w[2:3] * vv
        dyo = dyc_ref[...]
        dyc = dyo * bg
        nxt = dyn_ref[...] * pnb_ref[...].astype(F32) * nextf
        d1 = _shift_rows_up(dyc, 1, [nxt[0:1]], row)
        d2 = _shift_rows_up(dyc, 2, [nxt[0:1], nxt[1:2]], row)
        dvv = w[2:3] * dyc + w[1:2] * d1 + w[0:1] * d2
        dp_ref[:, COL_BG:COL_BG + D_CONV] = (dyo * yconv).astype(BF16)
        dp_ref[:, COL_CG:COL_CG + D_CONV] = (dvv * u).astype(BF16)
        dp_ref[:, COL_U:COL_U + D_CONV] = (dvv * cg).astype(BF16)
        dcw_ref[0:1, :] += jnp.sum(dyc * vv2, axis=0, keepdims=True)
        dcw_ref[1:2, :] += jnp.sum(dyc * vv1, axis=0, keepdims=True)
        dcw_ref[2:3, :] += jnp.sum(dyc * vv, axis=0, keepdims=True)

        lse_t, dl_t = lc_ref[...].T, dlc_ref[...].T
        lse_nt, dl_nt = ln_ref[...].T, dln_ref[...].T

        def stack_rows(tile_t, heads):
            return jnp.concatenate([tile_t[hd:hd + 1, :] for hd in heads], axis=1)

        lane8 = jnp.where(lax.broadcasted_iota(jnp.int32, (8, 128), 0) == 0,
                          lax.broadcasted_iota(jnp.int32, (8, 128), 1), -1)
        dsk = jnp.zeros((8, 128), F32)
        for kv in range(N_SWA_KV):
            heads = range(kv * SWA_GROUP, (kv + 1) * SWA_GROUP)
            kc = pc_ref[:, COL_K + HEAD_DIM * kv:COL_K + HEAD_DIM * (kv + 1)]
            vc = pc_ref[:, COL_V + HEAD_DIM * kv:COL_V + HEAD_DIM * (kv + 1)]
            kp = pkv_ref[:, HEAD_DIM * kv:HEAD_DIM * (kv + 1)]
            vp = pkv_ref[:, D_KV + HEAD_DIM * kv:D_KV + HEAD_DIM * (kv + 1)]
            qg = qc_ref[kv * SWA_GROUP:(kv + 1) * SWA_GROUP].reshape(GROUP_ROWS, HEAD_DIM)
            dog = doc_ref[kv * SWA_GROUP:(kv + 1) * SWA_GROUP].reshape(GROUP_ROWS, HEAD_DIM)
            qn = qn_ref[kv * SWA_GROUP:(kv + 1) * SWA_GROUP].reshape(GROUP_ROWS, HEAD_DIM)
            don = don_ref[kv * SWA_GROUP:(kv + 1) * SWA_GROUP].reshape(GROUP_ROWS, HEAD_DIM)
            lse_col, dl_col = _stack_cols(lc_ref, heads), _stack_cols(dlc_ref, heads)
            pp_ = jnp.exp(_dot_nt(qg, kp) + bp_ref[0, kv] - lse_col)
            dsp = (pp_ * (_dot_nt(dog, vp) - dl_col)).astype(BF16)
            dq = _dot(dsp, kp)
            pt = jnp.exp(_dot_nt(kc, qg) + bct_ref[0, kv] - stack_rows(lse_t, heads))
            dst = (pt * (_dot_nt(vc, dog) - stack_rows(dl_t, heads))).astype(BF16)
            dv = _dot(pt.astype(BF16), dog)
            dk = _dot(dst, qg)
            dq = dq + _dot_tn(dst, kc)
            ptn = jnp.exp(_dot_nt(kc, qn) + bnt_ref[0, kv] - stack_rows(lse_nt, heads))
            dstn = (ptn * (_dot_nt(vc, don) - stack_rows(dl_nt, heads))).astype(BF16)
            dv = dv + _dot(ptn.astype(BF16), don)
            dk = dk + _dot(dstn, qn)
            dp_ref[:, COL_K + HEAD_DIM * kv:COL_K + HEAD_DIM * (kv + 1)] = dk.astype(BF16)
            dp_ref[:, COL_V + HEAD_DIM * kv:COL_V + HEAD_DIM * (kv + 1)] = dv.astype(BF16)
            sink = jnp.concatenate([jnp.full((BLOCK, 1), sk_ref[0, hd], F32) for hd in heads], axis=0)
            sink_term = jnp.exp(sink - lse_col) * dl_col
            for gi, hd in enumerate(heads):
                rows = slice(gi * BLOCK, (gi + 1) * BLOCK)
                dp_ref[:, _q_col(hd):_q_col(hd) + HEAD_DIM] = (dq[rows] * SCALE).astype(BF16)
                dsk = dsk + jnp.where(lane8 == hd, -jnp.sum(sink_term[rows], axis=0, keepdims=True), 0.0)
        dsk_ref[...] += dsk

        for hm in range(N_MEM_HEADS):
            hd = N_SWA_HEADS + hm
            qm, dom = qc_ref[hd], doc_ref[hd]
            mk = mkv_ref[:, HEAD_DIM * hm:HEAD_DIM * (hm + 1)]
            mv = mkv_ref[:, D_MEMQ + HEAD_DIM * hm:D_MEMQ + HEAD_DIM * (hm + 1)]
            pt = jnp.exp(_dot_nt(mk, qm) - lse_t[hd:hd + 1, :])
            dst = (pt * (_dot_nt(mv, dom) - dl_t[hd:hd + 1, :])).astype(BF16)
            dp_ref[:, _q_col(hd):_q_col(hd) + HEAD_DIM] = (_dot_tn(dst, mk) * SCALE).astype(BF16)
            dmkv_ref[:, HEAD_DIM * hm:HEAD_DIM * (hm + 1)] += _dot(dst, qm)
            dmkv_ref[:, D_MEMQ + HEAD_DIM * hm:D_MEMQ + HEAD_DIM * (hm + 1)] += _dot(pt.astype(BF16), dom)

    cur = lambda i: (i, 0)
    const = lambda i: (0, 0)
    rows16 = BLOCK // 16
    last16 = t // 16 - 1
    before = lambda col: (lambda i: (jnp.maximum(i * rows16 - 1, 0), col))
    after = lambda i: (jnp.minimum((i + 1) * rows16, last16), 0)
    heads_cur = lambda i: (0, i, 0)
    heads_next = lambda i: (0, jnp.minimum(i + 1, nb - 1), 0)
    stat_next = lambda i: (jnp.minimum(i + 1, nb - 1), 0)
    tok_block = (1, N_SWA_KV, GROUP_ROWS, BLOCK)
    key_block = (1, N_SWA_KV, BLOCK, GROUP_ROWS)
    head_block = (N_HEADS, BLOCK, HEAD_DIM)
    return _pcall(
        body, name="mix_core_bwd", grid=(nb,),
        in_specs=[pl.BlockSpec(memory_space=pltpu.SMEM),
                  pl.BlockSpec((BLOCK, D_IN), cur),
                  pl.BlockSpec((BLOCK, 2 * D_KV), lambda i: (jnp.maximum(i - 1, 0), COL_K // (2 * D_KV))),
                  pl.BlockSpec((16, D_CONV), before(COL_CG // D_CONV)),
                  pl.BlockSpec((16, D_CONV), before(COL_U // D_CONV)),
                  pl.BlockSpec((16, D_CONV), after),
                  pl.BlockSpec((BLOCK, D_CONV), cur),
                  pl.BlockSpec((16, D_CONV), after),
                  pl.BlockSpec(head_block, heads_cur), pl.BlockSpec(head_block, heads_next),
                  pl.BlockSpec(head_block, heads_cur), pl.BlockSpec(head_block, heads_next),
                  pl.BlockSpec((BLOCK, 128), cur), pl.BlockSpec((BLOCK, 128), stat_next),
                  pl.BlockSpec((BLOCK, 128), cur), pl.BlockSpec((BLOCK, 128), stat_next),
                  pl.BlockSpec((m, 2 * D_MEMQ), const),
                  pl.BlockSpec((3, D_CONV), const),
                  pl.BlockSpec(tok_block, lambda i: (jnp.where(i == 0, BIAS_NONE, BIAS_PREV), 0, 0, 0)),
                  pl.BlockSpec(key_block, lambda i: (BIAS_CUR, 0, 0, 0)),
                  pl.BlockSpec(key_block, lambda i: (jnp.where(i == nb - 1, BIAS_NONE, BIAS_PREV), 0, 0, 0))],
        out_specs=[pl.BlockSpec((BLOCK, D_IN), cur),
                   pl.BlockSpec((m, 2 * D_MEMQ), const),
                   pl.BlockSpec((8, D_CONV), const),
                   pl.BlockSpec((8, 128), const)],
        out_shape=[jax.ShapeDtypeStruct((t, D_IN), BF16),
                   jax.ShapeDtypeStruct((m, 2 * D_MEMQ), F32),
                   jax.ShapeDtypeStruct((8, D_CONV), F32),
                   jax.ShapeDtypeStruct((8, 128), F32)],
        compiler_params=_params(("arbitrary",)),
    )(sinks, p, p, p, p, p, dyconv, dyconv, qh, qh, doh, doh, delta, delta, lse, lse, mkv, convw,
      bias_tok, bias_key, bias_key)


def _group_norms(y):
    out = []
    for a, b in MIX_GROUPS:
        ys = y[:, a:b]
        r = _rstd(ys)
        out.append((ys * r, r))
    return out


def _mix_out_fwd(y, h, g, wout):
    t, d = h.shape
    tm = _tok_block(t)

    def body(y_ref, h_ref, g_ref, w_ref, ho_ref, mt_ref):
        yhat = jnp.concatenate([yh for yh, _ in _group_norms(y_ref[...])], axis=-1)
        mixed = yhat * g_ref[...]
        mt_ref[...] = mixed.T.astype(BF16)
        ho_ref[...] = h_ref[...] + _dot(mixed.astype(BF16), w_ref[...])

    return _pcall(
        body, name="mix_out_fwd", grid=(t // tm,),
        in_specs=[pl.BlockSpec((tm, D_MIX), lambda i: (i, 0)),
                  pl.BlockSpec((tm, d), lambda i: (i, 0)),
                  pl.BlockSpec((1, D_MIX), lambda i: (0, 0)),
                  pl.BlockSpec((D_MIX, d), lambda i: (0, 0))],
        out_specs=[pl.BlockSpec((tm, d), lambda i: (i, 0)),
                   pl.BlockSpec((D_MIX, tm), lambda i: (0, i))],
        out_shape=[jax.ShapeDtypeStruct((t, d), F32), jax.ShapeDtypeStruct((D_MIX, t), BF16)],
        compiler_params=_params(("parallel",)),
    )(y, h, g, wout)


def _head_indicator():
    ind = np.zeros((D_MIX, 128), np.float32)
    for hd in range(N_HEADS):
        ind[_head_cols(hd):_head_cols(hd) + HEAD_DIM, hd] = 1.0
    return jnp.asarray(ind, BF16)


def _mix_out_bwd(dho, y, g, wout, mt, dep):
    t, d = dho.shape
    tm = _tok_block(t)
    ni = t // tm

    def body(dho_ref, y_ref, g_ref, w_ref, mt_ref, ind_ref, dep_ref, dyc_ref, doh_ref, dl_ref, dw_ref, dg_ref, acc_ref):
        i = pl.program_id(0)
        dhb = dho_ref[...].astype(BF16)
        dm = _dot_nt(dhb, w_ref[...])
        pw = _dot(mt_ref[...], dhb)
        gg = g_ref[...]
        yy = y_ref[...]
        dys = []
        dgs = []
        for (a, b), (yhat, r) in zip(MIX_GROUPS, _group_norms(yy)):
            dmg = dm[:, a:b]
            dgs.append(_sum8(dmg * yhat))
            dyh = dmg * gg[:, a:b]
            dys.append(r * (dyh - yhat * jnp.mean(dyh * yhat, axis=-1, keepdims=True)))
        dy = jnp.concatenate(dys, axis=-1)
        dyc_ref[...] = dy[:, 0:D_CONV]
        for hd in range(N_HEADS):
            doh_ref[hd] = dy[:, _head_cols(hd):_head_cols(hd) + HEAD_DIM].astype(BF16)
        prod = dy * yy
        hi = prod.astype(BF16)
        lo = (prod - hi.astype(F32)).astype(BF16)
        dl_ref[...] = _dot(hi, ind_ref[...]) + _dot(lo, ind_ref[...])
        part = jnp.concatenate(dgs, axis=-1)

        @pl.when(i == 0)
        def _():
            acc_ref[...] = pw
            dg_ref[...] = part

        @pl.when(i > 0)
        def _():
            acc_ref[...] += pw
            dg_ref[...] += part

        @pl.when(i == ni - 1)
        def _():
            dw_ref[...] = acc_ref[...].astype(BF16)

    return _pcall(
        body, name="mix_out_bwd", grid=(ni,),
        in_specs=[pl.BlockSpec((tm, d), lambda i: (i, 0)),
                  pl.BlockSpec((tm, D_MIX), lambda i: (i, 0)),
                  pl.BlockSpec((1, D_MIX), lambda i: (0, 0)),
                  pl.BlockSpec((D_MIX, d), lambda i: (0, 0)),
                  pl.BlockSpec((D_MIX, tm), lambda i: (0, i)),
                  pl.BlockSpec((D_MIX, 128), lambda i: (0, 0)),
                  pl.BlockSpec(memory_space=pl.ANY)],
        out_specs=[pl.BlockSpec((tm, D_CONV), lambda i: (i, 0)),
                   pl.BlockSpec((N_HEADS, tm, HEAD_DIM), lambda i: (0, i, 0)),
                   pl.BlockSpec((tm, 128), lambda i: (i, 0)),
                   pl.BlockSpec((D_MIX, d), lambda i: (0, 0)),
                   pl.BlockSpec((8, D_MIX), lambda i: (0, 0))],
        out_shape=[jax.ShapeDtypeStruct((t, D_CONV), F32),
                   jax.ShapeDtypeStruct((N_HEADS, t, HEAD_DIM), BF16),
                   jax.ShapeDtypeStruct((t, 128), F32),
                   jax.ShapeDtypeStruct((D_MIX, d), BF16),
                   jax.ShapeDtypeStruct((8, D_MIX), F32)],
        scratch_shapes=[pltpu.VMEM((D_MIX, d), F32)],
        compiler_params=_params(("arbitrary",)),
    )(dho, y, g, wout, mt, _head_indicator(), dep)


def _mix_proj_bwd(dp, dho, h, g, win_t, n):
    t, d = h.shape
    tm = _tok_block(t)
    ni = t // tm

    def body(dp_ref, dho_ref, h_ref, g_ref, w_ref, n_ref, dh_ref, dw_ref, dg_ref, acc_ref):
        i = pl.program_id(0)
        dpb = dp_ref[...]
        dn = _dot(dpb, w_ref[...])
        pw = _dot_tn(dpb, n_ref[...])
        hh = h_ref[...]
        r = _rstd(hh)
        xhat = hh * r
        dxh = dn * g_ref[...]
        dh_ref[...] = dho_ref[...] + r * (dxh - xhat * jnp.mean(dxh * xhat, axis=-1, keepdims=True))
        part = _sum8(dn * xhat)

        @pl.when(i == 0)
        def _():
            acc_ref[...] = pw
            dg_ref[...] = part

        @pl.when(i > 0)
        def _():
            acc_ref[...] += pw
            dg_ref[...] += part

        @pl.when(i == ni - 1)
        def _():
            dw_ref[...] = acc_ref[...].astype(BF16)

    return _pcall(
        body, name="mix_proj_bwd", grid=(ni,),
        in_specs=[pl.BlockSpec((tm, D_IN), lambda i: (i, 0)),
                  pl.BlockSpec((tm, d), lambda i: (i, 0)),
                  pl.BlockSpec((tm, d), lambda i: (i, 0)),
                  pl.BlockSpec((1, d), lambda i: (0, 0)),
                  pl.BlockSpec((D_IN, d), lambda i: (0, 0)),
                  pl.BlockSpec((tm, d), lambda i: (i, 0))],
        out_specs=[pl.BlockSpec((tm, d), lambda i: (i, 0)),
                   pl.BlockSpec((D_IN, d), lambda i: (0, 0)),
                   pl.BlockSpec((8, d), lambda i: (0, 0))],
        out_shape=[jax.ShapeDtypeStruct((t, d), F32),
                   jax.ShapeDtypeStruct((D_IN, d), BF16),
                   jax.ShapeDtypeStruct((8, d), F32)],
        scratch_shapes=[pltpu.VMEM((D_IN, d), F32)],
        compiler_params=_params(("arbitrary",)),
    )(dp, dho, h, g, win_t, n)


def _final_loss(h, g, tgt):
    t, d = h.shape
    tm = _tok_block(t)

    def body(h_ref, g_ref, t_ref, dh_ref, ls_ref, dg_ref):
        i = pl.program_id(0)
        hh = h_ref[...]
        r = _rstd(hh)
        xhat = hh * r
        gg = g_ref[...]
        err = xhat * gg - t_ref[...]
        dy = err * (1.0 / d)
        dxh = dy * gg
        dh_ref[...] = r * (dxh - xhat * jnp.mean(dxh * xhat, axis=-1, keepdims=True))
        lpart = _sum8(err * err)
        gpart = _sum8(dy * xhat)

        @pl.when(i == 0)
        def _():
            ls_ref[...] = lpart
            dg_ref[...] = gpart

        @pl.when(i > 0)
        def _():
            ls_ref[...] += lpart
            dg_ref[...] += gpart

    return _pcall(
        body, name="final_loss", grid=(t // tm,),
        in_specs=[pl.BlockSpec((tm, d), lambda i: (i, 0)),
                  pl.BlockSpec((1, d), lambda i: (0, 0)),
                  pl.BlockSpec((tm, d), lambda i: (i, 0))],
        out_specs=[pl.BlockSpec((tm, d), lambda i: (i, 0)),
                   pl.BlockSpec((8, d), lambda i: (0, 0)),
                   pl.BlockSpec((8, d), lambda i: (0, 0))],
        out_shape=[jax.ShapeDtypeStruct((t, d), F32),
                   jax.ShapeDtypeStruct((8, d), F32),
                   jax.ShapeDtypeStruct((8, d), F32)],
        compiler_params=_params(("arbitrary",)),
    )(h, g, tgt)


def _position():
    return lax.axis_index("x"), lax.axis_index("y"), lax.axis_index("c")


def _flip(v, bit):
    return 1 - v if bit else v


def _peer(k):
    x, y, c = _position()
    return _flip(x, k & 4), _flip(y, k & 2), _flip(c, k & 1)


def _slot(px, py, pc):
    return 4 * px + 2 * py + pc


def _handshake(peers):
    barrier = pltpu.get_barrier_semaphore()
    for peer in peers:
        pl.semaphore_signal(barrier, inc=1, device_id=peer, device_id_type=MESH)
    pl.semaphore_wait(barrier, len(peers))


def _sequencer_call(body, name, collective_id, out_type, scratch_types, operands):
    return pl.kernel(
        body, out_type=out_type, mesh=plsc.ScalarSubcoreMesh(axis_name="sequencer", num_cores=1), name=name,
        scratch_types=scratch_types, compiler_params=pltpu.CompilerParams(collective_id=collective_id),
    )(*operands)


def _all_gather(shards, name, collective_id):
    nt = len(shards)

    def body(*refs):
        xs = refs[:nt]
        outs = refs[nt:2 * nt]
        send_sems, recv_sems, local_sems = refs[2 * nt:]
        x, y, c = _position()
        me, sibling = (x, y, c), (x, y, 1 - c)
        chips = [(1 - x, y), (x, 1 - y), (1 - x, 1 - y)]
        _handshake([sibling] + [(*chip, c) for chip in chips])

        def copy(t, k, block, to, src=None):
            dst = outs[t].at[_slot(*block)]
            return pltpu.make_async_remote_copy(
                src_ref=dst if src is None else src, dst_ref=dst,
                send_sem=send_sems.at[t, k], recv_sem=recv_sems.at[t, k],
                device_id=to, device_id_type=MESH)

        mine = [pltpu.make_async_copy(xs[t], outs[t].at[_slot(*me)], local_sems.at[t]) for t in range(nt)]
        for cp in mine:
            cp.start()
        first = []
        for t in range(nt):
            first.append(copy(t, 0, me, sibling, src=xs[t]))
            first += [copy(t, 1 + j, me, (*chip, c), src=xs[t]) for j, chip in enumerate(chips)]
        for cp in first:
            cp.start()
        passed = []
        for j, chip in enumerate(chips):
            for t in range(nt):
                copy(t, 1 + j, (*chip, c), me).wait_recv()
                fwd = copy(t, 4 + j, (*chip, c), sibling)
                fwd.start()
                passed.append(fwd)
        for t in range(nt):
            copy(t, 0, sibling, me).wait_recv()
            for j, chip in enumerate(chips):
                copy(t, 4 + j, (*chip, 1 - c), me).wait_recv()
        for cp in first + passed:
            cp.wait_send()
        for cp in mine:
            cp.wait()

    return _sequencer_call(
        body, name, collective_id,
        out_type=[jax.ShapeDtypeStruct((N_DEV,) + s.shape, s.dtype) for s in shards],
        scratch_types=[pltpu.SemaphoreType.DMA((nt, 7)), pltpu.SemaphoreType.DMA((nt, 7)),
                       pltpu.SemaphoreType.DMA((nt,))],
        operands=shards)


def _scatter_copy(srcs, lands, send_sems, recv_sems, t, k):
    peer = _peer(k)
    return pltpu.make_async_remote_copy(
        src_ref=srcs[t].at[_slot(*peer)], dst_ref=lands[t].at[k],
        send_sem=send_sems.at[t * (N_DEV - 1) + k - 1], recv_sem=recv_sems.at[t * (N_DEV - 1) + k - 1],
        device_id=peer, device_id_type=MESH)


def _scatter_start(partials, name):
    nt = len(partials)

    def body(*refs):
        srcs, lands = refs[:nt], refs[nt:2 * nt]
        send_sems, recv_sems = refs[2 * nt], refs[2 * nt + 1]
        token = refs[-1]
        for k in range(1, N_DEV):
            for t in range(nt):
                _scatter_copy(srcs, lands, send_sems, recv_sems, t, k).start()
        token[...] = jnp.zeros_like(token)

    hbm = pl.BlockSpec(memory_space=pltpu.HBM)
    sem = pl.BlockSpec(memory_space=pltpu.SEMAPHORE)
    shapes = [pltpu.HBM(p.shape, p.dtype) for p in partials]
    lands = [pltpu.with_memory_space_constraint(lax.empty(p.shape, p.dtype), pltpu.HBM) for p in partials]
    srcs = [pltpu.with_memory_space_constraint(p, pltpu.HBM) for p in partials]
    out = _pcall(
        body, name=name,
        out_shape=[pltpu.SemaphoreType.DMA((nt * (N_DEV - 1),))] * 2 + shapes + shapes
        + [jax.ShapeDtypeStruct((8, 128), F32)],
        in_specs=[hbm] * (2 * nt),
        out_specs=[sem, sem] + [hbm] * (2 * nt) + [pl.BlockSpec(memory_space=pltpu.VMEM)],
        input_output_aliases={i: 2 + i for i in range(2 * nt)},
        compiler_params=pltpu.CompilerParams(has_side_effects=pltpu.SideEffectType.DATAFLOW_SIDE_EFFECTING),
    )(*srcs, *lands)
    return (nt, name, out[:-1]), out[-1]


def _scatter_wait(state, after):
    nt, name, (send_sems, recv_sems, *thru) = state

    def body(*refs):
        srcs, lands = refs[:nt], refs[nt:2 * nt]
        send_sems, recv_sems = refs[2 * nt], refs[2 * nt + 1]
        for k in range(1, N_DEV):
            for t in range(nt):
                copy = _scatter_copy(srcs, lands, send_sems, recv_sems, t, k)
                copy.wait_send()
                copy.wait_recv()

    hbm = pl.BlockSpec(memory_space=pltpu.HBM)
    sem = pl.BlockSpec(memory_space=pltpu.SEMAPHORE)
    out = _pcall(
        body, name=name + "_wait",
        out_shape=[pltpu.HBM(a.shape, a.dtype) for a in thru],
        in_specs=[hbm] * (2 * nt) + [sem, sem, pl.BlockSpec(memory_space=pl.ANY)],
        out_specs=[hbm] * (2 * nt),
        input_output_aliases={i: i for i in range(2 * nt)},
        compiler_params=pltpu.CompilerParams(has_side_effects=pltpu.SideEffectType.DATAFLOW_SIDE_EFFECTING),
    )(*thru, send_sems, recv_sems, after)
    return out[:nt], out[nt:]


def _all_reduce_rows(v):
    nv, _, w = v.shape

    def body(v_ref, out_ref, gath_ref, send_sems, recv_sems):
        x, y, c = _position()
        me = _slot(x, y, c)

        def copy(k):
            return pltpu.make_async_remote_copy(
                src_ref=v_ref, dst_ref=gath_ref.at[me],
                send_sem=send_sems.at[k - 1], recv_sem=recv_sems.at[k - 1],
                device_id=_peer(k), device_id_type=MESH)

        def arrival(k):
            return pltpu.make_async_remote_copy(
                src_ref=v_ref, dst_ref=gath_ref.at[_slot(*_peer(k))],
                send_sem=send_sems.at[k - 1], recv_sem=recv_sems.at[k - 1],
                device_id=_peer(k), device_id_type=MESH)

        sent = [copy(k) for k in range(1, N_DEV)]
        for cp in sent:
            cp.start()
        gath_ref[me] = v_ref[...]
        for k in range(1, N_DEV):
            arrival(k).wait_recv()
        for cp in sent:
            cp.wait_send()
        total = gath_ref[0]
        for s in range(1, N_DEV):
            total = total + gath_ref[s]
        out_ref[...] = jnp.sum(total, axis=1)

    vmem = pl.BlockSpec(memory_space=pltpu.VMEM)
    return _pcall(
        body, name="all_reduce_rows",
        in_specs=[vmem], out_specs=vmem,
        out_shape=jax.ShapeDtypeStruct((nv, w), F32),
        scratch_shapes=[pltpu.VMEM((N_DEV, nv, 8, w), F32),
                        pltpu.SemaphoreType.DMA((7,)), pltpu.SemaphoreType.DMA((7,))],
    )(v)


def _adamw_math(w, g, m, v):
    m2 = ADAM_B1 * m + (1.0 - ADAM_B1) * g
    v2 = ADAM_B2 * v + (1.0 - ADAM_B2) * (g * g)
    m_hat = m2 / (1.0 - ADAM_B1 ** ADAM_STEP)
    v_hat = v2 / (1.0 - ADAM_B2 ** ADAM_STEP)
    delta = -ADAM_LR * (m_hat / (jnp.sqrt(v_hat) + ADAM_EPS) + ADAM_WD * w)
    return delta, m2, v2


def _row_block(r):
    for cand in (256, 176, 128):
        if r % cand == 0:
            return cand
    return r


def _adamw_sharded(me, grads, w, m, v):
    (own0, land0), (own1, land1) = grads
    _, r, c = land0.shape
    tr = _row_block(r)
    nr = r // tr

    def body(me_ref, o0_ref, l0_ref, o1_ref, l1_ref, w_ref, m_ref, v_ref, g_ref, d_ref, m2_ref, v2_ref):
        layer = pl.program_id(0)

        def total(own_ref, land_ref):
            acc = own_ref[0].astype(F32)
            for k in range(1, N_DEV):
                acc = acc + land_ref[k].astype(F32)
            return acc

        g = jnp.where(layer == 0, total(o0_ref, l0_ref), total(o1_ref, l1_ref))
        delta, m2, v2 = _adamw_math(w_ref[0], g, m_ref[0], v_ref[0])
        g_ref[0] = g
        d_ref[0] = delta
        m2_ref[0] = m2
        v2_ref[0] = v2

    rows0 = lambda l, i: jnp.where(l == 0, i, nr - 1)
    rows1 = lambda l, i: jnp.where(l == 1, i, 0)
    shard = pl.BlockSpec((1, tr, c), lambda l, i, me_ref: (l, i, 0))
    out = jax.ShapeDtypeStruct((2, r, c), F32)
    return _pcall(
        body, name="adamw_sharded",
        grid_spec=pltpu.PrefetchScalarGridSpec(
            num_scalar_prefetch=1, grid=(2, nr),
            in_specs=[pl.BlockSpec((1, tr, c), lambda l, i, me_ref: (me_ref[0], rows0(l, i), 0)),
                      pl.BlockSpec((N_DEV, tr, c), lambda l, i, me_ref: (0, rows0(l, i), 0)),
                      pl.BlockSpec((1, tr, c), lambda l, i, me_ref: (me_ref[0], rows1(l, i), 0)),
                      pl.BlockSpec((N_DEV, tr, c), lambda l, i, me_ref: (0, rows1(l, i), 0)),
                      shard, shard, shard],
            out_specs=[shard, shard, shard, shard]),
        out_shape=[out, out, out, out],
        compiler_params=_params(("arbitrary", "arbitrary")),
    )(me, own0, land0, own1, land1, w, m, v)


def _adamw_small(w, g, m, v):
    def body(w_ref, g_ref, m_ref, v_ref, d_ref, m2_ref, v2_ref):
        delta, m2, v2 = _adamw_math(w_ref[...], g_ref[...], m_ref[...], v_ref[...])
        d_ref[...] = delta
        m2_ref[...] = m2
        v2_ref[...] = v2

    spec = pl.BlockSpec(w.shape, lambda i: (0, 0))
    out = jax.ShapeDtypeStruct(w.shape, F32)
    return _pcall(
        body, name="adamw_small", grid=(1,),
        in_specs=[spec] * 4, out_specs=[spec] * 3, out_shape=[out] * 3,
        compiler_params=_params(("arbitrary",)),
    )(w, g, m, v)


def _pack(arrs):
    flat = jnp.concatenate([a.reshape(-1) for a in arrs])
    n = flat.shape[0]
    rows = -(-n // 1024) * 8
    return jnp.pad(flat, (0, rows * 128 - n)).reshape(rows, 128)


def _unpack(packed, like):
    flat = packed.reshape(-1)
    out, off = [], 0
    for a in like:
        out.append(flat[off:off + a.size].reshape(a.shape))
        off += a.size
    return out


def kernel(x, mem, g_ffn1, w_ffn1_up, w_ffn1_down, g_mix, w_in, conv_w, sinks, g_mem, w_mem_kv, g_grp, w_out, g_ffn2, w_ffn2_up, w_ffn2_down, g_final, loss_target, m_g_ffn1, m_w_ffn1_up, m_w_ffn1_down, m_g_mix, m_w_in, m_conv_w, m_sinks, m_g_mem, m_w_mem_kv, m_g_grp, m_w_out, m_g_ffn2, m_w_ffn2_up, m_w_ffn2_down, m_g_final, v_g_ffn1, v_w_ffn1_up, v_w_ffn1_down, v_g_mix, v_w_in, v_conv_w, v_sinks, v_g_mem, v_w_mem_kv, v_g_grp, v_w_out, v_g_ffn2, v_w_ffn2_up, v_w_ffn2_down, v_g_final):
    depth = g_ffn1.shape[0]
    t, d = x.shape[1], x.shape[2]
    width = max(d, D_MIX)
    me = _slot(*_position())
    conv_shard = conv_w.shape[2]

    xin, memin, tgt = x[0], mem[0], loss_target[0]

    conv_tile = jnp.zeros((depth * 8, 128), F32).at[:, :conv_shard].set(
        jnp.pad(conv_w, ((0, 0), (0, 8 - conv_w.shape[1]), (0, 0))).reshape(depth * 8, conv_shard))
    tr = lambda a: jnp.swapaxes(a, -1, -2)
    bf = lambda a: a.astype(BF16)
    weights = []
    collective_id = 0
    for l in range(depth):
        groups = [[bf(tr(w_ffn1_up[l])), bf(w_ffn1_down[l])] + ([conv_tile] if l == 0 else []),
                  [bf(tr(w_in[l])), bf(w_mem_kv[l]), bf(w_out[l])],
                  [bf(tr(w_ffn2_up[l])), bf(w_ffn2_down[l])]]
        full = []
        for gi, shards in enumerate(groups):
            full.append(_all_gather(shards, f"all_gather_l{l}_g{gi}", collective_id))
            collective_id += 1
        if l == 0:
            conv_full = full[0][2].reshape(N_DEV, depth, 8, 128)[:, :, :3, :conv_shard]
            conv_full = conv_full.transpose(1, 2, 0, 3).reshape(depth, 3, N_DEV * conv_shard)
        weights.append(dict(
            up1=full[0][0].reshape(2, -1, d), dn1=full[0][1].reshape(-1, d),
            win=full[1][0].reshape(D_IN, d), wkv=full[1][1].reshape(d, 2 * D_MEMQ), wout=full[1][2].reshape(D_MIX, d),
            up2=full[2][0].reshape(2, -1, d), dn2=full[2][1].reshape(-1, d)))

    row = lambda a: a.reshape(1, -1)
    bias_tok, bias_key = _bias_tables()

    h = xin
    saved = []
    for l in range(depth):
        wl = weights[l]
        s = dict(h0=h)
        h, s["gu1"], s["n1"] = _ffn_fwd(h, row(g_ffn1[l]), wl["up1"], wl["dn1"])
        s["h1"] = h
        s["p"], s["n_mix"], s["qh"] = _mix_proj_fwd(h, row(g_mix[l]), wl["win"])
        s["mkv"], s["nt_mem"] = _memkv_fwd(memin, row(g_mem[l]), wl["wkv"], s["p"])
        s["y"], s["lse"] = _mix_core_fwd(s["p"], s["qh"], s["mkv"], conv_full[l], row(sinks[l]), bias_tok)
        h, s["mt"] = _mix_out_fwd(s["y"], h, row(g_grp[l]), wl["wout"])
        s["h2"] = h
        h, s["gu2"], s["n2"] = _ffn_fwd(h, row(g_ffn2[l]), wl["up2"], wl["dn2"])
        saved.append(s)

    dh, loss_part, dg_final = _final_loss(h, row(g_final), tgt)

    small = {}
    dep = loss_part

    started = []

    def scatter(names, partials, label):
        state, token = _scatter_start(partials, f"scatter_grads_{label}")
        started.append((names, state))
        return token

    for l in reversed(range(depth)):
        wl, s = weights[l], saved[l]
        dh, agu, dyb, small["g_ffn2", l] = _ffn_bwd_act(dh, s["h2"], row(g_ffn2[l]), s["gu2"], wl["up2"], wl["dn2"], dep)
        ddn2 = _ffn_bwd_w(agu, 2, 1, dyb, agu, f"ffn_bwd_w_down_l{l}_ffn2").reshape(N_DEV, -1, d)
        dup2 = _ffn_bwd_w(agu, 0, 2, s["n2"], ddn2, f"ffn_bwd_w_up_l{l}_ffn2").reshape(N_DEV, -1, d)
        dep = scatter([("w_ffn2_up", l), ("w_ffn2_down", l)], [dup2, ddn2], f"l{l}_ffn2")
        dyconv, doh, delta, dwout, small["g_grp", l] = _mix_out_bwd(dh, s["y"], row(g_grp[l]), wl["wout"], s["mt"], dep)
        dp, dmkv, small["conv_w", l], small["sinks", l] = _mix_core_bwd(
            s["p"], s["qh"], dyconv, doh, delta, s["lse"], s["mkv"], conv_full[l], row(sinks[l]), bias_tok, bias_key)
        dwkv, small["g_mem", l] = _memkv_bwd(dmkv, memin, row(g_mem[l]), wl["wkv"], s["nt_mem"])
        dh, dwin, small["g_mix", l] = _mix_proj_bwd(dp, dh, s["h1"], row(g_mix[l]), wl["win"], s["n_mix"])
        dep = scatter([("w_in", l), ("w_mem_kv", l), ("w_out", l)],
                      [dwin.reshape(N_DEV, -1, d), dwkv.reshape(N_DEV, -1, 2 * D_MEMQ), dwout.reshape(N_DEV, -1, d)],
                      f"l{l}_mix")
        dh, agu, dyb, small["g_ffn1", l] = _ffn_bwd_act(dh, s["h0"], row(g_ffn1[l]), s["gu1"], wl["up1"], wl["dn1"], dep)
        ddn1 = _ffn_bwd_w(agu, 2, 1, dyb, agu, f"ffn_bwd_w_down_l{l}_ffn1").reshape(N_DEV, -1, d)
        if l > 0:
            dup1 = _ffn_bwd_w(agu, 0, 2, s["n1"], ddn1, f"ffn_bwd_w_up_l{l}_ffn1").reshape(N_DEV, -1, d)
            dep = scatter([("w_ffn1_up", l), ("w_ffn1_down", l)], [dup1, ddn1], f"l{l}_ffn1")
        else:
            dep = scatter([("w_ffn1_down", l)], [ddn1], f"l{l}_ffn1_down")
            dup1 = _ffn_bwd_w(agu, 0, 2, s["n1"], dep, f"ffn_bwd_w_up_l{l}_ffn1").reshape(N_DEV, -1, d)
            dep = scatter([("w_ffn1_up", l)], [dup1], f"l{l}_ffn1_up")
    grad_x = dh[None]

    def lanes(a):
        return jnp.pad(a, ((0, 0), (0, width - a.shape[1])))

    def first_row(a):
        return lanes(jnp.pad(a, ((0, 8 - a.shape[0]), (0, 0))))

    vec_names = ["g_ffn1", "g_mix", "g_mem", "g_grp", "g_ffn2", "sinks"]
    tiles = [lanes(small[n, l]) for n in vec_names for l in range(depth)]
    tiles += [first_row(small["conv_w", l][k:k + 1]) for l in range(depth) for k in range(3)]
    tiles.append(lanes(dg_final))
    n_real = len(tiles)
    tiles.append(lanes(loss_part))
    tiles.append(lanes(dep))
    tiles += [jnp.zeros((8, width), F32)] * (-len(tiles) % 8)
    summed = _all_reduce_rows(jnp.stack(tiles))
    loss = 0.5 * jnp.sum(summed[n_real]) / d

    def vec(n, wd):
        return jnp.stack([summed[vec_names.index(n) * depth + l, :wd] for l in range(depth)])

    conv_base = len(vec_names) * depth
    conv_grad = jnp.stack([jnp.stack([summed[conv_base + 3 * l + k, :D_CONV] for k in range(3)]) for l in range(depth)])
    grads_small = {
        "g_ffn1": vec("g_ffn1", d), "g_mix": vec("g_mix", d), "g_mem": vec("g_mem", d),
        "g_grp": vec("g_grp", D_MIX), "g_ffn2": vec("g_ffn2", d), "sinks": vec("sinks", N_SWA_HEADS),
        "conv_w": lax.dynamic_slice_in_dim(conv_grad, me * conv_shard, conv_shard, axis=2),
        "g_final": summed[n_real - 1, :d],
    }
    small_w = [("g_ffn1", g_ffn1, m_g_ffn1, v_g_ffn1), ("g_mix", g_mix, m_g_mix, v_g_mix),
               ("conv_w", conv_w, m_conv_w, v_conv_w), ("sinks", sinks, m_sinks, v_sinks),
               ("g_mem", g_mem, m_g_mem, v_g_mem), ("g_grp", g_grp, m_g_grp, v_g_grp),
               ("g_ffn2", g_ffn2, m_g_ffn2, v_g_ffn2), ("g_final", g_final, m_g_final, v_g_final)]
    like = [w for _, w, _, _ in small_w]
    packed = _adamw_small(_pack(like), _pack([grads_small[n] for n, _, _, _ in small_w]),
                          _pack([m for _, _, m, _ in small_w]), _pack([v for _, _, _, v in small_w]))
    small_out = {n: (grads_small[n], dl, m2, v2)
                 for (n, _, _, _), dl, m2, v2 in zip(small_w, *[_unpack(pk, like) for pk in packed])}

    big = {"w_ffn2_up": (w_ffn2_up, m_w_ffn2_up, v_w_ffn2_up, True), "w_ffn2_down": (w_ffn2_down, m_w_ffn2_down, v_w_ffn2_down, False),
           "w_in": (w_in, m_w_in, v_w_in, True), "w_mem_kv": (w_mem_kv, m_w_mem_kv, v_w_mem_kv, False),
           "w_out": (w_out, m_w_out, v_w_out, False), "w_ffn1_up": (w_ffn1_up, m_w_ffn1_up, v_w_ffn1_up, True),
           "w_ffn1_down": (w_ffn1_down, m_w_ffn1_down, v_w_ffn1_down, False)}
    me_index = jnp.reshape(me, (1,)).astype(jnp.int32)
    sharded, landed = {}, {}
    after = packed[0]
    for names, state in started:
        owns, lands = _scatter_wait(state, after)
        for key, own, land in zip(names, owns, lands):
            landed[key] = (own, land)
        after = lands[0]
        for name in dict.fromkeys(n for n, _ in names):
            if name not in sharded and all((name, l) in landed for l in range(depth)):
                w, m, v, transposed = big[name]
                fix = tr if transposed else (lambda a: a)
                res = _adamw_sharded(me_index, [landed[name, l] for l in range(depth)], fix(w), fix(m), fix(v))
                sharded[name] = tuple(fix(r) for r in res)
                after = res[0]

    order = ["g_ffn1", "w_ffn1_up", "w_ffn1_down", "g_mix", "w_in", "conv_w", "sinks", "g_mem", "w_mem_kv", "g_grp",
             "w_out", "g_ffn2", "w_ffn2_up", "w_ffn2_down", "g_final"]
    results = {**sharded, **small_out}
    outs = [loss, grad_x]
    for part in range(4):
        outs += [results[n][part] for n in order]
    return tuple(outs)
```

```python
import numpy as np
import jax
import jax.numpy as jnp
from jax import lax
from jax.experimental import pallas as pl
from jax.experimental.pallas import tpu as pltpu
from jax.experimental.pallas import tpu_sc as plsc

F32 = jnp.float32
BF16 = jnp.bfloat16

N_DEV = 8
EPS = 1e-6
N_SWA_HEADS = 8
N_SWA_KV = 2
SWA_GROUP = N_SWA_HEADS // N_SWA_KV
HEAD_DIM = 64
N_MEM_HEADS = 4
D_CONV = 256
BLOCK = 128
D_SWA = N_SWA_HEADS * HEAD_DIM
D_KV = N_SWA_KV * HEAD_DIM
D_MEMQ = N_MEM_HEADS * HEAD_DIM
D_MIX = D_CONV + D_SWA + D_MEMQ
D_IN = 3 * D_CONV + D_SWA + 2 * D_KV + D_MEMQ
COL_BG, COL_CG, COL_U = 0, D_CONV, 2 * D_CONV
COL_Q = 3 * D_CONV
COL_K = COL_Q + D_SWA
COL_V = COL_K + D_KV
COL_QM = COL_V + D_KV
MIX_GROUPS = ((0, D_CONV), (D_CONV, D_CONV + D_SWA), (D_CONV + D_SWA, D_MIX))
SLOPES = tuple(2.0 ** (-8.0 * (i + 1) / N_SWA_HEADS) for i in range(N_SWA_HEADS))
SCALE = HEAD_DIM ** -0.5
NEG = -1e30

ADAM_LR = 0.001
ADAM_B1 = 0.9
ADAM_B2 = 0.999
ADAM_EPS = 1e-08
ADAM_WD = 0.01
ADAM_STEP = 10

V7X_VMEM_BYTES = 64 * 1024 * 1024
VMEM_LIMIT = (V7X_VMEM_BYTES * 3) // 4
MESH = pl.DeviceIdType.MESH


def _pcall(body, **kw):
    return pl.pallas_call(body, **kw)


def _params(sem=None, vmem=VMEM_LIMIT):
    return pltpu.CompilerParams(dimension_semantics=sem, vmem_limit_bytes=vmem)


def _dot(a, b):
    return lax.dot_general(a, b, (((1,), (0,)), ((), ())), preferred_element_type=F32)


def _dot_nt(a, b):
    return lax.dot_general(a, b, (((1,), (1,)), ((), ())), preferred_element_type=F32)


def _dot_tn(a, b):
    return lax.dot_general(a, b, (((0,), (0,)), ((), ())), preferred_element_type=F32)


def _rstd(x):
    return lax.rsqrt(jnp.mean(x * x, axis=-1, keepdims=True) + EPS)


def _sigmoid(x):
    return 1.0 / (1.0 + jnp.exp(-x))


def _sum8(x):
    r, w = x.shape
    return jnp.sum(x.reshape(r // 8, 8, w), axis=0)


def _tok_block(t, rows=512):
    return min(rows, t)


def _feat_block(f):
    return f // (N_DEV // 2)


FFN_TOKENS = 256
FFN_VMEM_LIMIT = (V7X_VMEM_BYTES * 7) // 8


def _pass_maps(ni, nj):
    first = lambda j, i: (jnp.where(j == 0, i, ni - 1), 0)
    last = lambda j, i: (jnp.where(j == nj - 1, i, 0), 0)
    both = lambda j, i: (jnp.where(j == 0, i, jnp.where(j == nj - 1, i, ni - 1)), 0)
    return first, last, both


def _ffn_fwd(h, g, wup_t, wdn):
    t, d = h.shape
    f = wdn.shape[0]
    tm, tf = _tok_block(t, FFN_TOKENS), _feat_block(f)
    ni, nj = t // tm, f // tf

    def body(h_ref, g_ref, wup_ref, wdn_ref, ho_ref, gu_ref, n_ref, nt_ref, acc_ref):
        j = pl.program_id(0)
        i = pl.program_id(1)
        rows = pl.ds(pl.multiple_of(i * tm, tm), tm)

        @pl.when(j == 0)
        def _():
            hh = h_ref[...]
            n = hh * _rstd(hh) * g_ref[...]
            n_ref[...] = n.astype(BF16)
            nt_ref[:, rows] = n.T.astype(BF16)

        nt = nt_ref[:, rows]
        gate = _dot(wup_ref[0], nt)
        up = _dot(wup_ref[1], nt)
        gu_ref[0] = gate.astype(BF16)
        gu_ref[1] = up.astype(BF16)
        a = gate * _sigmoid(gate) * up
        part = _dot_tn(a.astype(BF16), wdn_ref[...])

        @pl.when(j == 0)
        def _():
            acc_ref[rows, :] = part

        @pl.when(j > 0)
        def _():
            acc_ref[rows, :] += part

        @pl.when(j == nj - 1)
        def _():
            ho_ref[...] = h_ref[...] + 0.5 * acc_ref[rows, :]

    first_pass, last_pass, both_ends = _pass_maps(ni, nj)
    return _pcall(
        body, name="ffn_fwd", grid=(nj, ni),
        in_specs=[pl.BlockSpec((tm, d), both_ends),
                  pl.BlockSpec((1, d), lambda j, i: (0, 0)),
                  pl.BlockSpec((2, tf, d), lambda j, i: (0, j, 0)),
                  pl.BlockSpec((tf, d), lambda j, i: (j, 0))],
        out_specs=[pl.BlockSpec((tm, d), last_pass),
                   pl.BlockSpec((2, tf, tm), lambda j, i: (0, j, i)),
                   pl.BlockSpec((tm, d), first_pass)],
        out_shape=[jax.ShapeDtypeStruct((t, d), F32),
                   jax.ShapeDtypeStruct((2, f, t), BF16),
                   jax.ShapeDtypeStruct((t, d), BF16)],
        scratch_shapes=[pltpu.VMEM((d, t), BF16), pltpu.VMEM((t, d), F32)],
        compiler_params=_params(("arbitrary", "arbitrary"), FFN_VMEM_LIMIT),
    )(h, g, wup_t, wdn)


def _ffn_bwd_act(dho, h, g, gu, wup_t, wdn, dep):
    t, d = h.shape
    f = wdn.shape[0]
    tm, tf = _tok_block(t, FFN_TOKENS), _feat_block(f)
    ni, nj = t // tm, f // tf

    def body(dho_ref, h_ref, g_ref, gu_ref, wup_ref, wdn_ref, dep_ref, dh_ref, agu_ref, dyb_ref, dg_ref, dyt_ref, acc_ref):
        j = pl.program_id(0)
        i = pl.program_id(1)
        rows = pl.ds(pl.multiple_of(i * tm, tm), tm)

        @pl.when(j == 0)
        def _():
            dy0 = 0.5 * dho_ref[...]
            dyb_ref[...] = dy0.astype(BF16)
            dyt_ref[:, rows] = dy0.T.astype(BF16)

        da = _dot(wdn_ref[...], dyt_ref[:, rows])
        gate = gu_ref[0].astype(F32)
        up = gu_ref[1].astype(F32)
        sg = _sigmoid(gate)
        silu = gate * sg
        dgate = (da * up * (sg * (1.0 + gate * (1.0 - sg)))).astype(BF16)
        dup = (da * silu).astype(BF16)
        agu_ref[0] = dgate
        agu_ref[1] = dup
        agu_ref[2] = (silu * up).astype(BF16)
        dn = _dot_tn(dgate, wup_ref[0]) + _dot_tn(dup, wup_ref[1])

        @pl.when(j == 0)
        def _():
            acc_ref[rows, :] = dn

        @pl.when(j > 0)
        def _():
            acc_ref[rows, :] += dn

        @pl.when(j == nj - 1)
        def _():
            hh = h_ref[...]
            r = _rstd(hh)
            xhat = hh * r
            dnf = acc_ref[rows, :]
            dxh = dnf * g_ref[...]
            dh_ref[...] = dho_ref[...] + r * (dxh - xhat * jnp.mean(dxh * xhat, axis=-1, keepdims=True))
            part = _sum8(dnf * xhat)

            @pl.when(i == 0)
            def _():
                dg_ref[...] = part

            @pl.when(i > 0)
            def _():
                dg_ref[...] += part

    first_pass, last_pass, both_ends = _pass_maps(ni, nj)
    return _pcall(
        body, name="ffn_bwd_act", grid=(nj, ni),
        in_specs=[pl.BlockSpec((tm, d), both_ends),
                  pl.BlockSpec((tm, d), last_pass),
                  pl.BlockSpec((1, d), lambda j, i: (0, 0)),
                  pl.BlockSpec((2, tf, tm), lambda j, i: (0, j, i)),
                  pl.BlockSpec((2, tf, d), lambda j, i: (0, j, 0)),
                  pl.BlockSpec((tf, d), lambda j, i: (j, 0)),
                  pl.BlockSpec(memory_space=pl.ANY)],
        out_specs=[pl.BlockSpec((tm, d), last_pass),
                   pl.BlockSpec((3, tf, tm), lambda j, i: (0, j, i)),
                   pl.BlockSpec((tm, d), first_pass),
                   pl.BlockSpec((8, d), lambda j, i: (0, 0))],
        out_shape=[jax.ShapeDtypeStruct((t, d), F32),
                   jax.ShapeDtypeStruct((3, f, t), BF16),
                   jax.ShapeDtypeStruct((t, d), BF16),
                   jax.ShapeDtypeStruct((8, d), F32)],
        scratch_shapes=[pltpu.VMEM((d, t), BF16), pltpu.VMEM((t, d), F32)],
        compiler_params=_params(("arbitrary", "arbitrary"), FFN_VMEM_LIMIT),
    )(dho, h, g, gu, wup_t, wdn, dep)


def _ffn_bwd_w(agu, first, count, rhs, dep, name):
    _, f, t = agu.shape
    d = rhs.shape[1]
    tm, tf = _tok_block(t, 1024), _feat_block(f)
    ni, nj = t // tm, f // tf

    def body(lhs_ref, rhs_ref, dep_ref, dw_ref, acc_ref):
        i = pl.program_id(1)
        rb = rhs_ref[...]
        for k in range(count):
            part = _dot(lhs_ref[k], rb)

            @pl.when(i == 0)
            def _():
                acc_ref[k] = part

            @pl.when(i > 0)
            def _():
                acc_ref[k] += part

        @pl.when(i == ni - 1)
        def _():
            dw_ref[...] = acc_ref[...].astype(BF16)

    return _pcall(
        body, name=name, grid=(nj, ni),
        in_specs=[pl.BlockSpec((count, tf, tm), lambda j, i: (first // count, j, i)),
                  pl.BlockSpec((tm, d), lambda j, i: (i, 0)),
                  pl.BlockSpec(memory_space=pl.ANY)],
        out_specs=pl.BlockSpec((count, tf, d), lambda j, i: (0, j, 0)),
        out_shape=jax.ShapeDtypeStruct((count, f, d), BF16),
        scratch_shapes=[pltpu.VMEM((count, tf, d), F32)],
        compiler_params=_params(("parallel", "arbitrary")),
    )(agu, rhs, dep)


N_HEADS = N_SWA_HEADS + N_MEM_HEADS


def _q_col(hd):
    return COL_Q + HEAD_DIM * hd if hd < N_SWA_HEADS else COL_QM + HEAD_DIM * (hd - N_SWA_HEADS)


def _mix_proj_fwd(h, g, win_t):
    t, d = h.shape
    tm = _tok_block(t)

    def body(h_ref, g_ref, win_ref, p_ref, n_ref, qh_ref):
        hh = h_ref[...]
        n = (hh * _rstd(hh) * g_ref[...]).astype(BF16)
        n_ref[...] = n
        proj = _dot_nt(n, win_ref[...])
        p_ref[...] = proj.astype(BF16)
        for hd in range(N_HEADS):
            c0 = _q_col(hd)
            qh_ref[hd] = (proj[:, c0:c0 + HEAD_DIM] * SCALE).astype(BF16)

    return _pcall(
        body, name="mix_proj_fwd", grid=(t // tm,),
        in_specs=[pl.BlockSpec((tm, d), lambda i: (i, 0)),
                  pl.BlockSpec((1, d), lambda i: (0, 0)),
                  pl.BlockSpec((D_IN, d), lambda i: (0, 0))],
        out_specs=[pl.BlockSpec((tm, D_IN), lambda i: (i, 0)),
                   pl.BlockSpec((tm, d), lambda i: (i, 0)),
                   pl.BlockSpec((N_HEADS, tm, HEAD_DIM), lambda i: (0, i, 0))],
        out_shape=[jax.ShapeDtypeStruct((t, D_IN), BF16), jax.ShapeDtypeStruct((t, d), BF16),
                   jax.ShapeDtypeStruct((N_HEADS, t, HEAD_DIM), BF16)],
        compiler_params=_params(("parallel",)),
    )(h, g, win_t)


def _memkv_fwd(mem, g, wkv, dep):
    m, d = mem.shape

    def body(mem_ref, g_ref, w_ref, dep_ref, mkv_ref, nt_ref):
        mm = mem_ref[...]
        n = mm * _rstd(mm) * g_ref[...]
        nt_ref[...] = n.T.astype(BF16)
        mkv_ref[...] = _dot(n.astype(BF16), w_ref[...]).astype(BF16)

    return _pcall(
        body, name="memkv_fwd", grid=(1,),
        in_specs=[pl.BlockSpec((m, d), lambda i: (0, 0)),
                  pl.BlockSpec((1, d), lambda i: (0, 0)),
                  pl.BlockSpec((d, 2 * D_MEMQ), lambda i: (0, 0)),
                  pl.BlockSpec(memory_space=pl.ANY)],
        out_specs=[pl.BlockSpec((m, 2 * D_MEMQ), lambda i: (0, 0)),
                   pl.BlockSpec((d, m), lambda i: (0, 0))],
        out_shape=[jax.ShapeDtypeStruct((m, 2 * D_MEMQ), BF16), jax.ShapeDtypeStruct((d, m), BF16)],
        compiler_params=_params(("arbitrary",)),
    )(mem, g, wkv, dep)


def _memkv_bwd(dmkv, mem, g, wkv, nt):
    m, d = mem.shape

    def body(dmkv_ref, mem_ref, g_ref, w_ref, nt_ref, dw_ref, dg_ref):
        db = dmkv_ref[...].astype(BF16)
        dw_ref[...] = _dot(nt_ref[...], db).astype(BF16)
        dn = _dot_nt(db, w_ref[...])
        mm = mem_ref[...]
        dg_ref[...] = _sum8(dn * (mm * _rstd(mm)))

    return _pcall(
        body, name="memkv_bwd", grid=(1,),
        in_specs=[pl.BlockSpec((m, 2 * D_MEMQ), lambda i: (0, 0)),
                  pl.BlockSpec((m, d), lambda i: (0, 0)),
                  pl.BlockSpec((1, d), lambda i: (0, 0)),
                  pl.BlockSpec((d, 2 * D_MEMQ), lambda i: (0, 0)),
                  pl.BlockSpec((d, m), lambda i: (0, 0))],
        out_specs=[pl.BlockSpec((d, 2 * D_MEMQ), lambda i: (0, 0)),
                   pl.BlockSpec((8, d), lambda i: (0, 0))],
        out_shape=[jax.ShapeDtypeStruct((d, 2 * D_MEMQ), BF16), jax.ShapeDtypeStruct((8, d), F32)],
        compiler_params=_params(("arbitrary",)),
    )(dmkv, mem, g, wkv, nt)


def _shift_rows(v, k, edge_rows, row):
    out = pltpu.roll(v, k, 0)
    for r in range(k):
        out = jnp.where(row == r, edge_rows[r], out)
    return out


def _shift_rows_up(v, k, edge_rows, row):
    n = v.shape[0]
    out = pltpu.roll(v, n - k, 0)
    for r in range(k):
        out = jnp.where(row == n - k + r, edge_rows[r], out)
    return out


GROUP_ROWS = SWA_GROUP * BLOCK
BIAS_CUR, BIAS_PREV, BIAS_NONE = 0, 1, 2


def _bias_tables():
    tq = np.arange(BLOCK)[:, None]
    sk = np.arange(BLOCK)[None, :]
    slopes = np.asarray(SLOPES, np.float32)[:, None, None]
    cur = np.where(tq >= sk, -slopes * (tq - sk).astype(np.float32), NEG)
    prev = np.where(sk > tq, -slopes * (tq + BLOCK - sk).astype(np.float32), NEG)
    none = np.full_like(cur, NEG)
    tok = np.stack([cur, prev, none]).astype(np.float32).reshape(3, N_SWA_KV, GROUP_ROWS, BLOCK)
    return jnp.asarray(tok), jnp.asarray(np.ascontiguousarray(tok.transpose(0, 1, 3, 2)))


def _head_cols(hd):
    return D_CONV + HEAD_DIM * hd


def _stack_cols(ref, heads):
    return jnp.concatenate([ref[:, hd:hd + 1] for hd in heads], axis=0)


def _mix_core_fwd(p, qh, mkv, convw, sinks, bias_tok):
    t = p.shape[0]
    m = mkv.shape[0]
    nb = t // BLOCK

    def body(sk_ref, pc_ref, pkv_ref, ppc_ref, ppu_ref, qh_ref, mkv_ref, cw_ref, bc_ref, bp_ref, y_ref, l_ref):
        i = pl.program_id(0)
        prevf = (i > 0).astype(F32)
        row = lax.broadcasted_iota(jnp.int32, (BLOCK, D_CONV), 0)

        bg = pc_ref[:, COL_BG:COL_BG + D_CONV].astype(F32)
        cg = pc_ref[:, COL_CG:COL_CG + D_CONV].astype(F32)
        u = pc_ref[:, COL_U:COL_U + D_CONV].astype(F32)
        vv = cg * u
        pvv = ppc_ref[...].astype(F32) * ppu_ref[...].astype(F32) * prevf
        vv1 = _shift_rows(vv, 1, [pvv[15:16]], row)
        vv2 = _shift_rows(vv, 2, [pvv[14:15], pvv[15:16]], row)
        w = cw_ref[...]
        y_ref[:, 0:D_CONV] = bg * (w[0:1] * vv2 + w[1:2] * vv1 + w[2:3] * vv)

        lane = lax.broadcasted_iota(jnp.int32, (BLOCK, 128), 1)
        lse_all = jnp.zeros((BLOCK, 128), F32)
        for kv in range(N_SWA_KV):
            heads = range(kv * SWA_GROUP, (kv + 1) * SWA_GROUP)
            kc = pc_ref[:, COL_K + HEAD_DIM * kv:COL_K + HEAD_DIM * (kv + 1)]
            vc = pc_ref[:, COL_V + HEAD_DIM * kv:COL_V + HEAD_DIM * (kv + 1)]
            kp = pkv_ref[:, HEAD_DIM * kv:HEAD_DIM * (kv + 1)]
            vp = pkv_ref[:, D_KV + HEAD_DIM * kv:D_KV + HEAD_DIM * (kv + 1)]
            qg = qh_ref[kv * SWA_GROUP:(kv + 1) * SWA_GROUP].reshape(GROUP_ROWS, HEAD_DIM)
            sc = _dot_nt(qg, kc) + bc_ref[0, kv]
            sp = _dot_nt(qg, kp) + bp_ref[0, kv]
            sink = jnp.concatenate([jnp.full((BLOCK, 1), sk_ref[0, hd], F32) for hd in heads], axis=0)
            mx = jnp.maximum(jnp.max(jnp.maximum(sc, sp), axis=-1, keepdims=True), sink)
            ec = jnp.exp(sc - mx)
            ep = jnp.exp(sp - mx)
            den = jnp.sum(ec + ep, axis=-1, keepdims=True) + jnp.exp(sink - mx)
            o = (_dot(ec.astype(BF16), vc) + _dot(ep.astype(BF16), vp)) / den
            lse = mx + jnp.log(den)
            for gi, hd in enumerate(heads):
                rows = slice(gi * BLOCK, (gi + 1) * BLOCK)
                y_ref[:, _head_cols(hd):_head_cols(hd) + HEAD_DIM] = o[rows]
                lse_all = jnp.where(lane == hd, lse[rows], lse_all)

        for hm in range(N_MEM_HEADS):
            hd = N_SWA_HEADS + hm
            mk = mkv_ref[:, HEAD_DIM * hm:HEAD_DIM * (hm + 1)]
            mv = mkv_ref[:, D_MEMQ + HEAD_DIM * hm:D_MEMQ + HEAD_DIM * (hm + 1)]
            s = _dot_nt(qh_ref[hd], mk)
            mx = jnp.max(s, axis=-1, keepdims=True)
            e = jnp.exp(s - mx)
            den = jnp.sum(e, axis=-1, keepdims=True)
            y_ref[:, _head_cols(hd):_head_cols(hd) + HEAD_DIM] = _dot(e.astype(BF16), mv) / den
            lse_all = jnp.where(lane == hd, mx + jnp.log(den), lse_all)
        l_ref[...] = lse_all

    kv_col = COL_K // (2 * D_KV)
    bias_block = (1, N_SWA_KV, GROUP_ROWS, BLOCK)
    return _pcall(
        body, name="mix_core_fwd", grid=(nb,),
        in_specs=[pl.BlockSpec(memory_space=pltpu.SMEM),
                  pl.BlockSpec((BLOCK, D_IN), lambda i: (i, 0)),
                  pl.BlockSpec((BLOCK, 2 * D_KV), lambda i: (jnp.maximum(i - 1, 0), kv_col)),
                  pl.BlockSpec((16, D_CONV), lambda i: (jnp.maximum(i * (BLOCK // 16) - 1, 0), COL_CG // D_CONV)),
                  pl.BlockSpec((16, D_CONV), lambda i: (jnp.maximum(i * (BLOCK // 16) - 1, 0), COL_U // D_CONV)),
                  pl.BlockSpec((N_HEADS, BLOCK, HEAD_DIM), lambda i: (0, i, 0)),
                  pl.BlockSpec((m, 2 * D_MEMQ), lambda i: (0, 0)),
                  pl.BlockSpec((3, D_CONV), lambda i: (0, 0)),
                  pl.BlockSpec(bias_block, lambda i: (BIAS_CUR, 0, 0, 0)),
                  pl.BlockSpec(bias_block, lambda i: (jnp.where(i == 0, BIAS_NONE, BIAS_PREV), 0, 0, 0))],
        out_specs=[pl.BlockSpec((BLOCK, D_MIX), lambda i: (i, 0)),
                   pl.BlockSpec((BLOCK, 128), lambda i: (i, 0))],
        out_shape=[jax.ShapeDtypeStruct((t, D_MIX), F32), jax.ShapeDtypeStruct((t, 128), F32)],
        compiler_params=_params(("parallel",)),
    )(sinks, p, p, p, p, qh, mkv, convw, bias_tok, bias_tok)


def _mix_core_bwd(p, qh, dyconv, doh, delta, lse, mkv, convw, sinks, bias_tok, bias_key):
    t = p.shape[0]
    m = mkv.shape[0]
    nb = t // BLOCK

    def body(sk_ref, pc_ref, pkv_ref, ppc_ref, ppu_ref, pnb_ref, dyc_ref, dyn_ref, qc_ref, qn_ref, doc_ref, don_ref,
             dlc_ref, dln_ref, lc_ref, ln_ref, mkv_ref, cw_ref, bp_ref, bct_ref, bnt_ref,
             dp_ref, dmkv_ref, dcw_ref, dsk_ref):
        i = pl.program_id(0)
        prevf = (i > 0).astype(F32)
        nextf = (i < nb - 1).astype(F32)
        row = lax.broadcasted_iota(jnp.int32, (BLOCK, D_CONV), 0)

        @pl.when(i == 0)
        def _():
            dmkv_ref[...] = jnp.zeros_like(dmkv_ref)
            dcw_ref[...] = jnp.zeros_like(dcw_ref)
            dsk_ref[...] = jnp.zeros_like(dsk_ref)

        bg = pc_ref[:, COL_BG:COL_BG + D_CONV].astype(F32)
        cg = pc_ref[:, COL_CG:COL_CG + D_CONV].astype(F32)
        u = pc_ref[:, COL_U:COL_U + D_CONV].astype(F32)
        vv = cg * u
        pvv = ppc_ref[...].astype(F32) * ppu_ref[...].astype(F32) * prevf
        vv1 = _shift_rows(vv, 1, [pvv[15:16]], row)
        vv2 = _shift_rows(vv, 2, [pvv[14:15], pvv[15:16]], row)
        w = cw_ref[...]
        yconv = w[0:1] * vv2 + w[1:2] * vv1 + w[2:3] * vv
        dyo = dyc_ref[...]
        dyc = dyo * bg
        nxt = dyn_ref[...] * pnb_ref[...].astype(F32) * nextf
        d1 = _shift_rows_up(dyc, 1, [nxt[0:1]], row)
        d2 = _shift_rows_up(dyc, 2, [nxt[0:1], nxt[1:2]], row)
        dvv = w[2:3] * dyc + w[1:2] * d1 + w[0:1] * d2
        dp_ref[:, COL_BG:COL_BG + D_CONV] = (dyo * yconv).astype(BF16)
        dp_ref[:, COL_CG:COL_CG + D_CONV] = (dvv * u).astype(BF16)
        dp_ref[:, COL_U:COL_U + D_CONV] = (dvv * cg).astype(BF16)
        dcw_ref[0:1, :] += jnp.sum(dyc * vv2, axis=0, keepdims=True)
        dcw_ref[1:2, :] += jnp.sum(dyc * vv1, axis=0, keepdims=True)
        dcw_ref[2:3, :] += jnp.sum(dyc * vv, axis=0, keepdims=True)

        lse_t, dl_t = lc_ref[...].T, dlc_ref[...].T
        lse_nt, dl_nt = ln_ref[...].T, dln_ref[...].T

        def stack_rows(tile_t, heads):
            return jnp.concatenate([tile_t[hd:hd + 1, :] for hd in heads], axis=1)

        lane8 = jnp.where(lax.broadcasted_iota(jnp.int32, (8, 128), 0) == 0,
                          lax.broadcasted_iota(jnp.int32, (8, 128), 1), -1)
        dsk = jnp.zeros((8, 128), F32)
        for kv in range(N_SWA_KV):
            heads = range(kv * SWA_GROUP, (kv + 1) * SWA_GROUP)
            kc = pc_ref[:, COL_K + HEAD_DIM * kv:COL_K + HEAD_DIM * (kv + 1)]
            vc = pc_ref[:, COL_V + HEAD_DIM * kv:COL_V + HEAD_DIM * (kv + 1)]
            kp = pkv_ref[:, HEAD_DIM * kv:HEAD_DIM * (kv + 1)]
            vp = pkv_ref[:, D_KV + HEAD_DIM * kv:D_KV + HEAD_DIM * (kv + 1)]
            qg = qc_ref[kv * SWA_GROUP:(kv + 1) * SWA_GROUP].reshape(GROUP_ROWS, HEAD_DIM)
            dog = doc_ref[kv * SWA_GROUP:(kv + 1) * SWA_GROUP].reshape(GROUP_ROWS, HEAD_DIM)
            qn = qn_ref[kv * SWA_GROUP:(kv + 1) * SWA_GROUP].reshape(GROUP_ROWS, HEAD_DIM)
            don = don_ref[kv * SWA_GROUP:(kv + 1) * SWA_GROUP].reshape(GROUP_ROWS, HEAD_DIM)
            lse_col, dl_col = _stack_cols(lc_ref, heads), _stack_cols(dlc_ref, heads)
            pp_ = jnp.exp(_dot_nt(qg, kp) + bp_ref[0, kv] - lse_col)
            dsp = (pp_ * (_dot_nt(dog, vp) - dl_col)).astype(BF16)
            dq = _dot(dsp, kp)
            pt = jnp.exp(_dot_nt(kc, qg) + bct_ref[0, kv] - stack_rows(lse_t, heads))
            dst = (pt * (_dot_nt(vc, dog) - stack_rows(dl_t, heads))).astype(BF16)
            dv = _dot(pt.astype(BF16), dog)
            dk = _dot(dst, qg)
            dq = dq + _dot_tn(dst, kc)
            ptn = jnp.exp(_dot_nt(kc, qn) + bnt_ref[0, kv] - stack_rows(lse_nt, heads))
            dstn = (ptn * (_dot_nt(vc, don) - stack_rows(dl_nt, heads))).astype(BF16)
            dv = dv + _dot(ptn.astype(BF16), don)
            dk = dk + _dot(dstn, qn)
            dp_ref[:, COL_K + HEAD_DIM * kv:COL_K + HEAD_DIM * (kv + 1)] = dk.astype(BF16)
            dp_ref[:, COL_V + HEAD_DIM * kv:COL_V + HEAD_DIM * (kv + 1)] = dv.astype(BF16)
            sink = jnp.concatenate([jnp.full((BLOCK, 1), sk_ref[0, hd], F32) for hd in heads], axis=0)
            sink_term = jnp.exp(sink - lse_col) * dl_col
            for gi, hd in enumerate(heads):
                rows = slice(gi * BLOCK, (gi + 1) * BLOCK)
                dp_ref[:, _q_col(hd):_q_col(hd) + HEAD_DIM] = (dq[rows] * SCALE).astype(BF16)
                dsk = dsk + jnp.where(lane8 == hd, -jnp.sum(sink_term[rows], axis=0, keepdims=True), 0.0)
        dsk_ref[...] += dsk

        for hm in range(N_MEM_HEADS):
            hd = N_SWA_HEADS + hm
            qm, dom = qc_ref[hd], doc_ref[hd]
            mk = mkv_ref[:, HEAD_DIM * hm:HEAD_DIM * (hm + 1)]
            mv = mkv_ref[:, D_MEMQ + HEAD_DIM * hm:D_MEMQ + HEAD_DIM * (hm + 1)]
            pt = jnp.exp(_dot_nt(mk, qm) - lse_t[hd:hd + 1, :])
            dst = (pt * (_dot_nt(mv, dom) - dl_t[hd:hd + 1, :])).astype(BF16)
            dp_ref[:, _q_col(hd):_q_col(hd) + HEAD_DIM] = (_dot_tn(dst, mk) * SCALE).astype(BF16)
            dmkv_ref[:, HEAD_DIM * hm:HEAD_DIM * (hm + 1)] += _dot(dst, qm)
            dmkv_ref[:, D_MEMQ + HEAD_DIM * hm:D_MEMQ + HEAD_DIM * (hm + 1)] += _dot(pt.astype(BF16), dom)

    cur = lambda i: (i, 0)
    const = lambda i: (0, 0)
    rows16 = BLOCK // 16
    last16 = t // 16 - 1
    before = lambda col: (lambda i: (jnp.maximum(i * rows16 - 1, 0), col))
    after = lambda i: (jnp.minimum((i + 1) * rows16, last16), 0)
    heads_cur = lambda i: (0, i, 0)
    heads_next = lambda i: (0, jnp.minimum(i + 1, nb - 1), 0)
    stat_next = lambda i: (jnp.minimum(i + 1, nb - 1), 0)
    tok_block = (1, N_SWA_KV, GROUP_ROWS, BLOCK)
    key_block = (1, N_SWA_KV, BLOCK, GROUP_ROWS)
    head_block = (N_HEADS, BLOCK, HEAD_DIM)
    return _pcall(
        body, name="mix_core_bwd", grid=(nb,),
        in_specs=[pl.BlockSpec(memory_space=pltpu.SMEM),
                  pl.BlockSpec((BLOCK, D_IN), cur),
                  pl.BlockSpec((BLOCK, 2 * D_KV), lambda i: (jnp.maximum(i - 1, 0), COL_K // (2 * D_KV))),
                  pl.BlockSpec((16, D_CONV), before(COL_CG // D_CONV)),
                  pl.BlockSpec((16, D_CONV), before(COL_U // D_CONV)),
                  pl.BlockSpec((16, D_CONV), after),
                  pl.BlockSpec((BLOCK, D_CONV), cur),
                  pl.BlockSpec((16, D_CONV), after),
                  pl.BlockSpec(head_block, heads_cur), pl.BlockSpec(head_block, heads_next),
                  pl.BlockSpec(head_block, heads_cur), pl.BlockSpec(head_block, heads_next),
                  pl.BlockSpec((BLOCK, 128), cur), pl.BlockSpec((BLOCK, 128), stat_next),
                  pl.BlockSpec((BLOCK, 128), cur), pl.BlockSpec((BLOCK, 128), stat_next),
                  pl.BlockSpec((m, 2 * D_MEMQ), const),
                  pl.BlockSpec((3, D_CONV), const),
                  pl.BlockSpec(tok_block, lambda i: (jnp.where(i == 0, BIAS_NONE, BIAS_PREV), 0, 0, 0)),
                  pl.BlockSpec(key_block, lambda i: (BIAS_CUR, 0, 0, 0)),
                  pl.BlockSpec(key_block, lambda i: (jnp.where(i == nb - 1, BIAS_NONE, BIAS_PREV), 0, 0, 0))],
        out_specs=[pl.BlockSpec((BLOCK, D_IN), cur),
                   pl.BlockSpec((m, 2 * D_MEMQ), const),
                   pl.BlockSpec((8, D_CONV), const),
                   pl.BlockSpec((8, 128), const)],
        out_shape=[jax.ShapeDtypeStruct((t, D_IN), BF16),
                   jax.ShapeDtypeStruct((m, 2 * D_MEMQ), F32),
                   jax.ShapeDtypeStruct((8, D_CONV), F32),
                   jax.ShapeDtypeStruct((8, 128), F32)],
        compiler_params=_params(("arbitrary",)),
    )(sinks, p, p, p, p, p, dyconv, dyconv, qh, qh, doh, doh, delta, delta, lse, lse, mkv, convw,
      bias_tok, bias_key, bias_key)


def _group_norms(y):
    out = []
    for a, b in MIX_GROUPS:
        ys = y[:, a:b]
        r = _rstd(ys)
        out.append((ys * r, r))
    return out


def _mix_out_fwd(y, h, g, wout):
    t, d = h.shape
    tm = _tok_block(t)

    def body(y_ref, h_ref, g_ref, w_ref, ho_ref, mt_ref):
        yhat = jnp.concatenate([yh for yh, _ in _group_norms(y_ref[...])], axis=-1)
        mixed = yhat * g_ref[...]
        mt_ref[...] = mixed.T.astype(BF16)
        ho_ref[...] = h_ref[...] + _dot(mixed.astype(BF16), w_ref[...])

    return _pcall(
        body, name="mix_out_fwd", grid=(t // tm,),
        in_specs=[pl.BlockSpec((tm, D_MIX), lambda i: (i, 0)),
                  pl.BlockSpec((tm, d), lambda i: (i, 0)),
                  pl.BlockSpec((1, D_MIX), lambda i: (0, 0)),
                  pl.BlockSpec((D_MIX, d), lambda i: (0, 0))],
        out_specs=[pl.BlockSpec((tm, d), lambda i: (i, 0)),
                   pl.BlockSpec((D_MIX, tm), lambda i: (0, i))],
        out_shape=[jax.ShapeDtypeStruct((t, d), F32), jax.ShapeDtypeStruct((D_MIX, t), BF16)],
        compiler_params=_params(("parallel",)),
    )(y, h, g, wout)


def _head_indicator():
    ind = np.zeros((D_MIX, 128), np.float32)
    for hd in range(N_HEADS):
        ind[_head_cols(hd):_head_cols(hd) + HEAD_DIM, hd] = 1.0
    return jnp.asarray(ind, BF16)


def _mix_out_bwd(dho, y, g, wout, mt, dep):
    t, d = dho.shape
    tm = _tok_block(t)
    ni = t // tm

    def body(dho_ref, y_ref, g_ref, w_ref, mt_ref, ind_ref, dep_ref, dyc_ref, doh_ref, dl_ref, dw_ref, dg_ref, acc_ref):
        i = pl.program_id(0)
        dhb = dho_ref[...].astype(BF16)
        dm = _dot_nt(dhb, w_ref[...])
        pw = _dot(mt_ref[...], dhb)
        gg = g_ref[...]
        yy = y_ref[...]
        dys = []
        dgs = []
        for (a, b), (yhat, r) in zip(MIX_GROUPS, _group_norms(yy)):
            dmg = dm[:, a:b]
            dgs.append(_sum8(dmg * yhat))
            dyh = dmg * gg[:, a:b]
            dys.append(r * (dyh - yhat * jnp.mean(dyh * yhat, axis=-1, keepdims=True)))
        dy = jnp.concatenate(dys, axis=-1)
        dyc_ref[...] = dy[:, 0:D_CONV]
        for hd in range(N_HEADS):
            doh_ref[hd] = dy[:, _head_cols(hd):_head_cols(hd) + HEAD_DIM].astype(BF16)
        prod = dy * yy
        hi = prod.astype(BF16)
        lo = (prod - hi.astype(F32)).astype(BF16)
        dl_ref[...] = _dot(hi, ind_ref[...]) + _dot(lo, ind_ref[...])
        part = jnp.concatenate(dgs, axis=-1)

        @pl.when(i == 0)
        def _():
            acc_ref[...] = pw
            dg_ref[...] = part

        @pl.when(i > 0)
        def _():
            acc_ref[...] += pw
            dg_ref[...] += part

        @pl.when(i == ni - 1)
        def _():
            dw_ref[...] = acc_ref[...].astype(BF16)

    return _pcall(
        body, name="mix_out_bwd", grid=(ni,),
        in_specs=[pl.BlockSpec((tm, d), lambda i: (i, 0)),
                  pl.BlockSpec((tm, D_MIX), lambda i: (i, 0)),
                  pl.BlockSpec((1, D_MIX), lambda i: (0, 0)),
                  pl.BlockSpec((D_MIX, d), lambda i: (0, 0)),
                  pl.BlockSpec((D_MIX, tm), lambda i: (0, i)),
                  pl.BlockSpec((D_MIX, 128), lambda i: (0, 0)),
                  pl.BlockSpec(memory_space=pl.ANY)],
        out_specs=[pl.BlockSpec((tm, D_CONV), lambda i: (i, 0)),
                   pl.BlockSpec((N_HEADS, tm, HEAD_DIM), lambda i: (0, i, 0)),
                   pl.BlockSpec((tm, 128), lambda i: (i, 0)),
                   pl.BlockSpec((D_MIX, d), lambda i: (0, 0)),
                   pl.BlockSpec((8, D_MIX), lambda i: (0, 0))],
        out_shape=[jax.ShapeDtypeStruct((t, D_CONV), F32),
                   jax.ShapeDtypeStruct((N_HEADS, t, HEAD_DIM), BF16),
                   jax.ShapeDtypeStruct((t, 128), F32),
                   jax.ShapeDtypeStruct((D_MIX, d), BF16),
                   jax.ShapeDtypeStruct((8, D_MIX), F32)],
        scratch_shapes=[pltpu.VMEM((D_MIX, d), F32)],
        compiler_params=_params(("arbitrary",)),
    )(dho, y, g, wout, mt, _head_indicator(), dep)


def _mix_proj_bwd(dp, dho, h, g, win_t, n):
    t, d = h.shape
    tm = _tok_block(t)
    ni = t // tm

    def body(dp_ref, dho_ref, h_ref, g_ref, w_ref, n_ref, dh_ref, dw_ref, dg_ref, acc_ref):
        i = pl.program_id(0)
        dpb = dp_ref[...]
        dn = _dot(dpb, w_ref[...])
        pw = _dot_tn(dpb, n_ref[...])
        hh = h_ref[...]
        r = _rstd(hh)
        xhat = hh * r
        dxh = dn * g_ref[...]
        dh_ref[...] = dho_ref[...] + r * (dxh - xhat * jnp.mean(dxh * xhat, axis=-1, keepdims=True))
        part = _sum8(dn * xhat)

        @pl.when(i == 0)
        def _():
            acc_ref[...] = pw
            dg_ref[...] = part

        @pl.when(i > 0)
        def _():
            acc_ref[...] += pw
            dg_ref[...] += part

        @pl.when(i == ni - 1)
        def _():
            dw_ref[...] = acc_ref[...].astype(BF16)

    return _pcall(
        body, name="mix_proj_bwd", grid=(ni,),
        in_specs=[pl.BlockSpec((tm, D_IN), lambda i: (i, 0)),
                  pl.BlockSpec((tm, d), lambda i: (i, 0)),
                  pl.BlockSpec((tm, d), lambda i: (i, 0)),
                  pl.BlockSpec((1, d), lambda i: (0, 0)),
                  pl.BlockSpec((D_IN, d), lambda i: (0, 0)),
                  pl.BlockSpec((tm, d), lambda i: (i, 0))],
        out_specs=[pl.BlockSpec((tm, d), lambda i: (i, 0)),
                   pl.BlockSpec((D_IN, d), lambda i: (0, 0)),
                   pl.BlockSpec((8, d), lambda i: (0, 0))],
        out_shape=[jax.ShapeDtypeStruct((t, d), F32),
                   jax.ShapeDtypeStruct((D_IN, d), BF16),
                   jax.ShapeDtypeStruct((8, d), F32)],
        scratch_shapes=[pltpu.VMEM((D_IN, d), F32)],
        compiler_params=_params(("arbitrary",)),
    )(dp, dho, h, g, win_t, n)


def _final_loss(h, g, tgt):
    t, d = h.shape
    tm = _tok_block(t)

    def body(h_ref, g_ref, t_ref, dh_ref, ls_ref, dg_ref):
        i = pl.program_id(0)
        hh = h_ref[...]
        r = _rstd(hh)
        xhat = hh * r
        gg = g_ref[...]
        err = xhat * gg - t_ref[...]
        dy = err * (1.0 / d)
        dxh = dy * gg
        dh_ref[...] = r * (dxh - xhat * jnp.mean(dxh * xhat, axis=-1, keepdims=True))
        lpart = _sum8(err * err)
        gpart = _sum8(dy * xhat)

        @pl.when(i == 0)
        def _():
            ls_ref[...] = lpart
            dg_ref[...] = gpart

        @pl.when(i > 0)
        def _():
            ls_ref[...] += lpart
            dg_ref[...] += gpart

    return _pcall(
        body, name="final_loss", grid=(t // tm,),
        in_specs=[pl.BlockSpec((tm, d), lambda i: (i, 0)),
                  pl.BlockSpec((1, d), lambda i: (0, 0)),
                  pl.BlockSpec((tm, d), lambda i: (i, 0))],
        out_specs=[pl.BlockSpec((tm, d), lambda i: (i, 0)),
                   pl.BlockSpec((8, d), lambda i: (0, 0)),
                   pl.BlockSpec((8, d), lambda i: (0, 0))],
        out_shape=[jax.ShapeDtypeStruct((t, d), F32),
                   jax.ShapeDtypeStruct((8, d), F32),
                   jax.ShapeDtypeStruct((8, d), F32)],
        compiler_params=_params(("arbitrary",)),
    )(h, g, tgt)


def _position():
    return lax.axis_index("x"), lax.axis_index("y"), lax.axis_index("c")


def _flip(v, bit):
    return 1 - v if bit else v


def _peer(k):
    x, y, c = _position()
    return _flip(x, k & 4), _flip(y, k & 2), _flip(c, k & 1)


def _slot(px, py, pc):
    return 4 * px + 2 * py + pc


def _handshake(peers):
    barrier = pltpu.get_barrier_semaphore()
    for peer in peers:
        pl.semaphore_signal(barrier, inc=1, device_id=peer, device_id_type=MESH)
    pl.semaphore_wait(barrier, len(peers))


def _sequencer_call(body, name, collective_id, out_type, scratch_types, operands):
    return pl.kernel(
        body, out_type=out_type, mesh=plsc.ScalarSubcoreMesh(axis_name="sequencer", num_cores=1), name=name,
        scratch_types=scratch_types, compiler_params=pltpu.CompilerParams(collective_id=collective_id),
    )(*operands)


def _all_gather(shards, name, collective_id):
    nt = len(shards)

    def body(*refs):
        xs = refs[:nt]
        outs = refs[nt:2 * nt]
        send_sems, recv_sems, local_sems = refs[2 * nt:]
        x, y, c = _position()
        me, sibling = (x, y, c), (x, y, 1 - c)
        chips = [(1 - x, y), (x, 1 - y), (1 - x, 1 - y)]
        _handshake([sibling] + [(*chip, c) for chip in chips])

        def copy(t, k, block, to, src=None):
            dst = outs[t].at[_slot(*block)]
            return pltpu.make_async_remote_copy(
                src_ref=dst if src is None else src, dst_ref=dst,
                send_sem=send_sems.at[t, k], recv_sem=recv_sems.at[t, k],
                device_id=to, device_id_type=MESH)

        mine = [pltpu.make_async_copy(xs[t], outs[t].at[_slot(*me)], local_sems.at[t]) for t in range(nt)]
        for cp in mine:
            cp.start()
        first = []
        for t in range(nt):
            first.append(copy(t, 0, me, sibling, src=xs[t]))
            first += [copy(t, 1 + j, me, (*chip, c), src=xs[t]) for j, chip in enumerate(chips)]
        for cp in first:
            cp.start()
        passed = []
        for j, chip in enumerate(chips):
            for t in range(nt):
                copy(t, 1 + j, (*chip, c), me).wait_recv()
                fwd = copy(t, 4 + j, (*chip, c), sibling)
                fwd.start()
                passed.append(fwd)
        for t in range(nt):
            copy(t, 0, sibling, me).wait_recv()
            for j, chip in enumerate(chips):
                copy(t, 4 + j, (*chip, 1 - c), me).wait_recv()
        for cp in first + passed:
            cp.wait_send()
        for cp in mine:
            cp.wait()

    return _sequencer_call(
        body, name, collective_id,
        out_type=[jax.ShapeDtypeStruct((N_DEV,) + s.shape, s.dtype) for s in shards],
        scratch_types=[pltpu.SemaphoreType.DMA((nt, 7)), pltpu.SemaphoreType.DMA((nt, 7)),
                       pltpu.SemaphoreType.DMA((nt,))],
        operands=shards)


def _scatter_copy(srcs, lands, send_sems, recv_sems, t, k):
    peer = _peer(k)
    return pltpu.make_async_remote_copy(
        src_ref=srcs[t].at[_slot(*peer)], dst_ref=lands[t].at[k],
        send_sem=send_sems.at[t * (N_DEV - 1) + k - 1], recv_sem=recv_sems.at[t * (N_DEV - 1) + k - 1],
        device_id=peer, device_id_type=MESH)


def _scatter_start(partials, name):
    nt = len(partials)

    def body(*refs):
        srcs, lands = refs[:nt], refs[nt:2 * nt]
        send_sems, recv_sems = refs[2 * nt], refs[2 * nt + 1]
        token = refs[-1]
        for k in range(1, N_DEV):
            for t in range(nt):
                _scatter_copy(srcs, lands, send_sems, recv_sems, t, k).start()
        token[...] = jnp.zeros_like(token)

    hbm = pl.BlockSpec(memory_space=pltpu.HBM)
    sem = pl.BlockSpec(memory_space=pltpu.SEMAPHORE)
    shapes = [pltpu.HBM(p.shape, p.dtype) for p in partials]
    lands = [pltpu.with_memory_space_constraint(lax.empty(p.shape, p.dtype), pltpu.HBM) for p in partials]
    srcs = [pltpu.with_memory_space_constraint(p, pltpu.HBM) for p in partials]
    out = _pcall(
        body, name=name,
        out_shape=[pltpu.SemaphoreType.DMA((nt * (N_DEV - 1),))] * 2 + shapes + shapes
        + [jax.ShapeDtypeStruct((8, 128), F32)],
        in_specs=[hbm] * (2 * nt),
        out_specs=[sem, sem] + [hbm] * (2 * nt) + [pl.BlockSpec(memory_space=pltpu.VMEM)],
        input_output_aliases={i: 2 + i for i in range(2 * nt)},
        compiler_params=pltpu.CompilerParams(has_side_effects=pltpu.SideEffectType.DATAFLOW_SIDE_EFFECTING),
    )(*srcs, *lands)
    return (nt, name, out[:-1]), out[-1]


def _scatter_wait(state, after):
    nt, name, (send_sems, recv_sems, *thru) = state

    def body(*refs):
        srcs, lands = refs[:nt], refs[nt:2 * nt]
        send_sems, recv_sems = refs[2 * nt], refs[2 * nt + 1]
        for k in range(1, N_DEV):
            for t in range(nt):
                copy = _scatter_copy(srcs, lands, send_sems, recv_sems, t, k)
                copy.wait_send()
                copy.wait_recv()

    hbm = pl.BlockSpec(memory_space=pltpu.HBM)
    sem = pl.BlockSpec(memory_space=pltpu.SEMAPHORE)
    out = _pcall(
        body, name=name + "_wait",
        out_shape=[pltpu.HBM(a.shape, a.dtype) for a in thru],
        in_specs=[hbm] * (2 * nt) + [sem, sem, pl.BlockSpec(memory_space=pl.ANY)],
        out_specs=[hbm] * (2 * nt),
        input_output_aliases={i: i for i in range(2 * nt)},
        compiler_params=pltpu.CompilerParams(has_side_effects=pltpu.SideEffectType.DATAFLOW_SIDE_EFFECTING),
    )(*thru, send_sems, recv_sems, after)
    return out[:nt], out[nt:]


def _all_reduce_rows(v):
    nv, _, w = v.shape

    def body(v_ref, out_ref, gath_ref, send_sems, recv_sems):
        x, y, c = _position()
        me = _slot(x, y, c)

        def copy(k):
            return pltpu.make_async_remote_copy(
                src_ref=v_ref, dst_ref=gath_ref.at[me],
                send_sem=send_sems.at[k - 1], recv_sem=recv_sems.at[k - 1],
                device_id=_peer(k), device_id_type=MESH)

        def arrival(k):
            return pltpu.make_async_remote_copy(
                src_ref=v_ref, dst_ref=gath_ref.at[_slot(*_peer(k))],
                send_sem=send_sems.at[k - 1], recv_sem=recv_sems.at[k - 1],
                device_id=_peer(k), device_id_type=MESH)

        sent = [copy(k) for k in range(1, N_DEV)]
        for cp in sent:
            cp.start()
        gath_ref[me] = v_ref[...]
        for k in range(1, N_DEV):
            arrival(k).wait_recv()
        for cp in sent:
            cp.wait_send()
        total = gath_ref[0]
        for s in range(1, N_DEV):
            total = total + gath_ref[s]
        out_ref[...] = jnp.sum(total, axis=1)

    vmem = pl.BlockSpec(memory_space=pltpu.VMEM)
    return _pcall(
        body, name="all_reduce_rows",
        in_specs=[vmem], out_specs=vmem,
        out_shape=jax.ShapeDtypeStruct((nv, w), F32),
        scratch_shapes=[pltpu.VMEM((N_DEV, nv, 8, w), F32),
                        pltpu.SemaphoreType.DMA((7,)), pltpu.SemaphoreType.DMA((7,))],
    )(v)


def _adamw_math(w, g, m, v):
    m2 = ADAM_B1 * m + (1.0 - ADAM_B1) * g
    v2 = ADAM_B2 * v + (1.0 - ADAM_B2) * (g * g)
    m_hat = m2 / (1.0 - ADAM_B1 ** ADAM_STEP)
    v_hat = v2 / (1.0 - ADAM_B2 ** ADAM_STEP)
    delta = -ADAM_LR * (m_hat / (jnp.sqrt(v_hat) + ADAM_EPS) + ADAM_WD * w)
    return delta, m2, v2


def _row_block(r):
    for cand in (256, 176, 128):
        if r % cand == 0:
            return cand
    return r


def _adamw_sharded(me, grads, w, m, v):
    (own0, land0), (own1, land1) = grads
    _, r, c = land0.shape
    tr = _row_block(r)
    nr = r // tr

    def body(me_ref, o0_ref, l0_ref, o1_ref, l1_ref, w_ref, m_ref, v_ref, g_ref, d_ref, m2_ref, v2_ref):
        layer = pl.program_id(0)

        def total(own_ref, land_ref):
            acc = own_ref[0].astype(F32)
            for k in range(1, N_DEV):
                acc = acc + land_ref[k].astype(F32)
            return acc

        g = jnp.where(layer == 0, total(o0_ref, l0_ref), total(o1_ref, l1_ref))
        delta, m2, v2 = _adamw_math(w_ref[0], g, m_ref[0], v_ref[0])
        g_ref[0] = g
        d_ref[0] = delta
        m2_ref[0] = m2
        v2_ref[0] = v2

    rows0 = lambda l, i: jnp.where(l == 0, i, nr - 1)
    rows1 = lambda l, i: jnp.where(l == 1, i, 0)
    shard = pl.BlockSpec((1, tr, c), lambda l, i, me_ref: (l, i, 0))
    out = jax.ShapeDtypeStruct((2, r, c), F32)
    return _pcall(
        body, name="adamw_sharded",
        grid_spec=pltpu.PrefetchScalarGridSpec(
            num_scalar_prefetch=1, grid=(2, nr),
            in_specs=[pl.BlockSpec((1, tr, c), lambda l, i, me_ref: (me_ref[0], rows0(l, i), 0)),
                      pl.BlockSpec((N_DEV, tr, c), lambda l, i, me_ref: (0, rows0(l, i), 0)),
                      pl.BlockSpec((1, tr, c), lambda l, i, me_ref: (me_ref[0], rows1(l, i), 0)),
                      pl.BlockSpec((N_DEV, tr, c), lambda l, i, me_ref: (0, rows1(l, i), 0)),
                      shard, shard, shard],
            out_specs=[shard, shard, shard, shard]),
        out_shape=[out, out, out, out],
        compiler_params=_params(("arbitrary", "arbitrary")),
    )(me, own0, land0, own1, land1, w, m, v)


def _adamw_small(w, g, m, v):
    def body(w_ref, g_ref, m_ref, v_ref, d_ref, m2_ref, v2_ref):
        delta, m2, v2 = _adamw_math(w_ref[...], g_ref[...], m_ref[...], v_ref[...])
        d_ref[...] = delta
        m2_ref[...] = m2
        v2_ref[...] = v2

    spec = pl.BlockSpec(w.shape, lambda i: (0, 0))
    out = jax.ShapeDtypeStruct(w.shape, F32)
    return _pcall(
        body, name="adamw_small", grid=(1,),
        in_specs=[spec] * 4, out_specs=[spec] * 3, out_shape=[out] * 3,
        compiler_params=_params(("arbitrary",)),
    )(w, g, m, v)


def _pack(arrs):
    flat = jnp.concatenate([a.reshape(-1) for a in arrs])
    n = flat.shape[0]
    rows = -(-n // 1024) * 8
    return jnp.pad(flat, (0, rows * 128 - n)).reshape(rows, 128)


def _unpack(packed, like):
    flat = packed.reshape(-1)
    out, off = [], 0
    for a in like:
        out.append(flat[off:off + a.size].reshape(a.shape))
        off += a.size
    return out


def kernel(x, mem, g_ffn1, w_ffn1_up, w_ffn1_down, g_mix, w_in, conv_w, sinks, g_mem, w_mem_kv, g_grp, w_out, g_ffn2, w_ffn2_up, w_ffn2_down, g_final, loss_target, m_g_ffn1, m_w_ffn1_up, m_w_ffn1_down, m_g_mix, m_w_in, m_conv_w, m_sinks, m_g_mem, m_w_mem_kv, m_g_grp, m_w_out, m_g_ffn2, m_w_ffn2_up, m_w_ffn2_down, m_g_final, v_g_ffn1, v_w_ffn1_up, v_w_ffn1_down, v_g_mix, v_w_in, v_conv_w, v_sinks, v_g_mem, v_w_mem_kv, v_g_grp, v_w_out, v_g_ffn2, v_w_ffn2_up, v_w_ffn2_down, v_g_final):
    depth = g_ffn1.shape[0]
    t, d = x.shape[1], x.shape[2]
    width = max(d, D_MIX)
    me = _slot(*_position())
    conv_shard = conv_w.shape[2]

    xin, memin, tgt = x[0], mem[0], loss_target[0]

    conv_tile = jnp.zeros((depth * 8, 128), F32).at[:, :conv_shard].set(
        jnp.pad(conv_w, ((0, 0), (0, 8 - conv_w.shape[1]), (0, 0))).reshape(depth * 8, conv_shard))
    tr = lambda a: jnp.swapaxes(a, -1, -2)
    bf = lambda a: a.astype(BF16)
    weights = []
    collective_id = 0
    for l in range(depth):
        groups = [[bf(tr(w_ffn1_up[l])), bf(w_ffn1_down[l])] + ([conv_tile] if l == 0 else []),
                  [bf(tr(w_in[l])), bf(w_mem_kv[l]), bf(w_out[l])],
                  [bf(tr(w_ffn2_up[l])), bf(w_ffn2_down[l])]]
        full = []
        for gi, shards in enumerate(groups):
            full.append(_all_gather(shards, f"all_gather_l{l}_g{gi}", collective_id))
            collective_id += 1
        if l == 0:
            conv_full = full[0][2].reshape(N_DEV, depth, 8, 128)[:, :, :3, :conv_shard]
            conv_full = conv_full.transpose(1, 2, 0, 3).reshape(depth, 3, N_DEV * conv_shard)
        weights.append(dict(
            up1=full[0][0].reshape(2, -1, d), dn1=full[0][1].reshape(-1, d),
            win=full[1][0].reshape(D_IN, d), wkv=full[1][1].reshape(d, 2 * D_MEMQ), wout=full[1][2].reshape(D_MIX, d),
            up2=full[2][0].reshape(2, -1, d), dn2=full[2][1].reshape(-1, d)))

    row = lambda a: a.reshape(1, -1)
    bias_tok, bias_key = _bias_tables()

    h = xin
    saved = []
    for l in range(depth):
        wl = weights[l]
        s = dict(h0=h)
        h, s["gu1"], s["n1"] = _ffn_fwd(h, row(g_ffn1[l]), wl["up1"], wl["dn1"])
        s["h1"] = h
        s["p"], s["n_mix"], s["qh"] = _mix_proj_fwd(h, row(g_mix[l]), wl["win"])
        s["mkv"], s["nt_mem"] = _memkv_fwd(memin, row(g_mem[l]), wl["wkv"], s["p"])
        s["y"], s["lse"] = _mix_core_fwd(s["p"], s["qh"], s["mkv"], conv_full[l], row(sinks[l]), bias_tok)
        h, s["mt"] = _mix_out_fwd(s["y"], h, row(g_grp[l]), wl["wout"])
        s["h2"] = h
        h, s["gu2"], s["n2"] = _ffn_fwd(h, row(g_ffn2[l]), wl["up2"], wl["dn2"])
        saved.append(s)

    dh, loss_part, dg_final = _final_loss(h, row(g_final), tgt)

    small = {}
    dep = loss_part

    started = []

    def scatter(names, partials, label):
        state, token = _scatter_start(partials, f"scatter_grads_{label}")
        started.append((names, state))
        return token

    for l in reversed(range(depth)):
        wl, s = weights[l], saved[l]
        dh, agu, dyb, small["g_ffn2", l] = _ffn_bwd_act(dh, s["h2"], row(g_ffn2[l]), s["gu2"], wl["up2"], wl["dn2"], dep)
        ddn2 = _ffn_bwd_w(agu, 2, 1, dyb, agu, f"ffn_bwd_w_down_l{l}_ffn2").reshape(N_DEV, -1, d)
        dup2 = _ffn_bwd_w(agu, 0, 2, s["n2"], ddn2, f"ffn_bwd_w_up_l{l}_ffn2").reshape(N_DEV, -1, d)
        dep = scatter([("w_ffn2_up", l), ("w_ffn2_down", l)], [dup2, ddn2], f"l{l}_ffn2")
        dyconv, doh, delta, dwout, small["g_grp", l] = _mix_out_bwd(dh, s["y"], row(g_grp[l]), wl["wout"], s["mt"], dep)
        dp, dmkv, small["conv_w", l], small["sinks", l] = _mix_core_bwd(
            s["p"], s["qh"], dyconv, doh, delta, s["lse"], s["mkv"], conv_full[l], row(sinks[l]), bias_tok, bias_key)
        dwkv, small["g_mem", l] = _memkv_bwd(dmkv, memin, row(g_mem[l]), wl["wkv"], s["nt_mem"])
        dh, dwin, small["g_mix", l] = _mix_proj_bwd(dp, dh, s["h1"], row(g_mix[l]), wl["win"], s["n_mix"])
        dep = scatter([("w_in", l), ("w_mem_kv", l), ("w_out", l)],
                      [dwin.reshape(N_DEV, -1, d), dwkv.reshape(N_DEV, -1, 2 * D_MEMQ), dwout.reshape(N_DEV, -1, d)],
                      f"l{l}_mix")
        dh, agu, dyb, small["g_ffn1", l] = _ffn_bwd_act(dh, s["h0"], row(g_ffn1[l]), s["gu1"], wl["up1"], wl["dn1"], dep)
        ddn1 = _ffn_bwd_w(agu, 2, 1, dyb, agu, f"ffn_bwd_w_down_l{l}_ffn1").reshape(N_DEV, -1, d)
        if l > 0:
            dup1 = _ffn_bwd_w(agu, 0, 2, s["n1"], ddn1, f"ffn_bwd_w_up_l{l}_ffn1").reshape(N_DEV, -1, d)
            dep = scatter([("w_ffn1_up", l), ("w_ffn1_down", l)], [dup1, ddn1], f"l{l}_ffn1")
        else:
            dep = scatter([("w_ffn1_down", l)], [ddn1], f"l{l}_ffn1_down")
            dup1 = _ffn_bwd_w(agu, 0, 2, s["n1"], dep, f"ffn_bwd_w_up_l{l}_ffn1").reshape(N_DEV, -1, d)
            dep = scatter([("w_ffn1_up", l)], [dup1], f"l{l}_ffn1_up")
    grad_x = dh[None]

    def lanes(a):
        return jnp.pad(a, ((0, 0), (0, width - a.shape[1])))

    def first_row(a):
        return lanes(jnp.pad(a, ((0, 8 - a.shape[0]), (0, 0))))

    vec_names = ["g_ffn1", "g_mix", "g_mem", "g_grp", "g_ffn2", "sinks"]
    tiles = [lanes(small[n, l]) for n in vec_names for l in range(depth)]
    tiles += [first_row(small["conv_w", l][k:k + 1]) for l in range(depth) for k in range(3)]
    tiles.append(lanes(dg_final))
    n_real = len(tiles)
    tiles.append(lanes(loss_part))
    tiles.append(lanes(dep))
    tiles += [jnp.zeros((8, width), F32)] * (-len(tiles) % 8)
    summed = _all_reduce_rows(jnp.stack(tiles))
    loss = 0.5 * jnp.sum(summed[n_real]) / d

    def vec(n, wd):
        return jnp.stack([summed[vec_names.index(n) * depth + l, :wd] for l in range(depth)])

    conv_base = len(vec_names) * depth
    conv_grad = jnp.stack([jnp.stack([summed[conv_base + 3 * l + k, :D_CONV] for k in range(3)]) for l in range(depth)])
    grads_small = {
        "g_ffn1": vec("g_ffn1", d), "g_mix": vec("g_mix", d), "g_mem": vec("g_mem", d),
        "g_grp": vec("g_grp", D_MIX), "g_ffn2": vec("g_ffn2", d), "sinks": vec("sinks", N_SWA_HEADS),
        "conv_w": lax.dynamic_slice_in_dim(conv_grad, me * conv_shard, conv_shard, axis=2),
        "g_final": summed[n_real - 1, :d],
    }
    small_w = [("g_ffn1", g_ffn1, m_g_ffn1, v_g_ffn1), ("g_mix", g_mix, m_g_mix, v_g_mix),
               ("conv_w", conv_w, m_conv_w, v_conv_w), ("sinks", sinks, m_sinks, v_sinks),
               ("g_mem", g_mem, m_g_mem, v_g_mem), ("g_grp", g_grp, m_g_grp, v_g_grp),
               ("g_ffn2", g_ffn2, m_g_ffn2, v_g_ffn2), ("g_final", g_final, m_g_final, v_g_final)]
    like = [w for _, w, _, _ in small_w]
    packed = _adamw_small(_pack(like), _pack([grads_small[n] for n, _, _, _ in small_w]),
                          _pack([m for _, _, m, _ in small_w]), _pack([v for _, _, _, v in small_w]))
    small_out = {n: (grads_small[n], dl, m2, v2)
                 for (n, _, _, _), dl, m2, v2 in zip(small_w, *[_unpack(pk, like) for pk in packed])}

    big = {"w_ffn2_up": (w_ffn2_up, m_w_ffn2_up, v_w_ffn2_up, True), "w_ffn2_down": (w_ffn2_down, m_w_ffn2_down, v_w_ffn2_down, False),
           "w_in": (w_in, m_w_in, v_w_in, True), "w_mem_kv": (w_mem_kv, m_w_mem_kv, v_w_mem_kv, False),
           "w_out": (w_out, m_w_out, v_w_out, False), "w_ffn1_up": (w_ffn1_up, m_w_ffn1_up, v_w_ffn1_up, True),
           "w_ffn1_down": (w_ffn1_down, m_w_ffn1_down, v_w_ffn1_down, False)}
    me_index = jnp.reshape(me, (1,)).astype(jnp.int32)
    sharded, landed = {}, {}
    after = packed[0]
    for names, state in started:
        owns, lands = _scatter_wait(state, after)
        for key, own, land in zip(names, owns, lands):
            landed[key] = (own, land)
        after = lands[0]
        for name in dict.fromkeys(n for n, _ in names):
            if name not in sharded and all((name, l) in landed for l in range(depth)):
                w, m, v, transposed = big[name]
                fix = tr if transposed else (lambda a: a)
                res = _adamw_sharded(me_index, [landed[name, l] for l in range(depth)], fix(w), fix(m), fix(v))
                sharded[name] = tuple(fix(r) for r in res)
                after = res[0]

    order = ["g_ffn1", "w_ffn1_up", "w_ffn1_down", "g_mix", "w_in", "conv_w", "sinks", "g_mem", "w_mem_kv", "g_grp",
             "w_out", "g_ffn2", "w_ffn2_up", "w_ffn2_down", "g_final"]
    results = {**sharded, **small_out}
    outs = [loss, grad_x]
    for part in range(4):
        outs += [results[n][part] for n in order]
    return tuple(outs)
```

```python
import numpy as np
import jax
import jax.numpy as jnp
from jax import lax
from jax.experimental import pallas as pl
from jax.experimental.pallas import tpu as pltpu
from jax.experimental.pallas import tpu_sc as plsc

F32 = jnp.float32
BF16 = jnp.bfloat16

N_DEV = 8
EPS = 1e-6
N_SWA_HEADS = 8
N_SWA_KV = 2
SWA_GROUP = N_SWA_HEADS // N_SWA_KV
HEAD_DIM = 64
N_MEM_HEADS = 4
D_CONV = 256
BLOCK = 128
D_SWA = N_SWA_HEADS * HEAD_DIM
D_KV = N_SWA_KV * HEAD_DIM
D_MEMQ = N_MEM_HEADS * HEAD_DIM
D_MIX = D_CONV + D_SWA + D_MEMQ
D_IN = 3 * D_CONV + D_SWA + 2 * D_KV + D_MEMQ
COL_BG, COL_CG, COL_U = 0, D_CONV, 2 * D_CONV
COL_Q = 3 * D_CONV
COL_K = COL_Q + D_SWA
COL_V = COL_K + D_KV
COL_QM = COL_V + D_KV
MIX_GROUPS = ((0, D_CONV), (D_CONV, D_CONV + D_SWA), (D_CONV + D_SWA, D_MIX))
SLOPES = tuple(2.0 ** (-8.0 * (i + 1) / N_SWA_HEADS) for i in range(N_SWA_HEADS))
SCALE = HEAD_DIM ** -0.5
NEG = -1e30

ADAM_LR = 0.001
ADAM_B1 = 0.9
ADAM_B2 = 0.999
ADAM_EPS = 1e-08
ADAM_WD = 0.01
ADAM_STEP = 10

V7X_VMEM_BYTES = 64 * 1024 * 1024
VMEM_LIMIT = (V7X_VMEM_BYTES * 3) // 4
MESH = pl.DeviceIdType.MESH


def _pcall(body, **kw):
    return pl.pallas_call(body, **kw)


def _params(sem=None, vmem=VMEM_LIMIT):
    return pltpu.CompilerParams(dimension_semantics=sem, vmem_limit_bytes=vmem)


def _dot(a, b):
    return lax.dot_general(a, b, (((1,), (0,)), ((), ())), preferred_element_type=F32)


def _dot_nt(a, b):
    return lax.dot_general(a, b, (((1,), (1,)), ((), ())), preferred_element_type=F32)


def _dot_tn(a, b):
    return lax.dot_general(a, b, (((0,), (0,)), ((), ())), preferred_element_type=F32)


def _rstd(x):
    return lax.rsqrt(jnp.mean(x * x, axis=-1, keepdims=True) + EPS)


def _sigmoid(x):
    return 1.0 / (1.0 + jnp.exp(-x))


def _sum8(x):
    r, w = x.shape
    return jnp.sum(x.reshape(r // 8, 8, w), axis=0)


def _tok_block(t, rows=512):
    return min(rows, t)


def _feat_block(f):
    return f // (N_DEV // 2)


def _ffn_fwd(h, g, wup_t, wdn):
    t, d = h.shape
    f = wdn.shape[0]
    tm, tf = _tok_block(t, 1024), _feat_block(f)
    ni, nj = t // tm, f // tf
    once = pl.Buffered(1)

    def body(h_ref, g_ref, wup_ref, wdn_ref, ho_ref, act_ref, n_ref, nt_ref, acc_ref):
        j = pl.program_id(1)

        @pl.when(j == 0)
        def _():
            hh = h_ref[...]
            n = hh * _rstd(hh) * g_ref[...]
            n_ref[...] = n.astype(BF16)
            nt_ref[...] = n.T.astype(BF16)
            acc_ref[...] = jnp.zeros_like(acc_ref)

        nt = nt_ref[...]
        gate = _dot(wup_ref[0], nt)
        up = _dot(wup_ref[1], nt)
        sg = _sigmoid(gate)
        silu = gate * sg
        a = (silu * up).astype(BF16)
        act_ref[0] = (up * (sg * (1.0 + gate * (1.0 - sg)))).astype(BF16)
        act_ref[1] = silu.astype(BF16)
        act_ref[2] = a
        acc_ref[...] += _dot_tn(a, wdn_ref[...])

        @pl.when(j == nj - 1)
        def _():
            ho_ref[...] = h_ref[...] + 0.5 * acc_ref[...]

    return _pcall(
        body, name="ffn_fwd", grid=(ni, nj),
        in_specs=[pl.BlockSpec((tm, d), lambda i, j: (i, 0), pipeline_mode=once),
                  pl.BlockSpec((1, d), lambda i, j: (0, 0)),
                  pl.BlockSpec((2, tf, d), lambda i, j: (0, j, 0)),
                  pl.BlockSpec((tf, d), lambda i, j: (j, 0))],
        out_specs=[pl.BlockSpec((tm, d), lambda i, j: (i, 0), pipeline_mode=once),
                   pl.BlockSpec((3, tf, tm), lambda i, j: (0, j, i)),
                   pl.BlockSpec((tm, d), lambda i, j: (i, 0), pipeline_mode=once)],
        out_shape=[jax.ShapeDtypeStruct((t, d), F32),
                   jax.ShapeDtypeStruct((3, f, t), BF16),
                   jax.ShapeDtypeStruct((t, d), BF16)],
        scratch_shapes=[pltpu.VMEM((d, tm), BF16), pltpu.VMEM((tm, d), F32)],
        compiler_params=_params(("parallel", "arbitrary")),
    )(h, g, wup_t, wdn)


def _ffn_bwd_act(dho, h, g, act, wup_t, wdn, dep):
    t, d = h.shape
    f = wdn.shape[0]
    tm, tf = _tok_block(t), _feat_block(f)
    ni, nj = t // tm, f // tf

    def body(dho_ref, h_ref, g_ref, act_ref, wup_ref, wdn_ref, dep_ref, dh_ref, dgu_ref, dyb_ref, dg_ref, dyt_ref, acc_ref):
        i = pl.program_id(0)
        j = pl.program_id(1)

        @pl.when(j == 0)
        def _():
            dy0 = 0.5 * dho_ref[...]
            dyb_ref[...] = dy0.astype(BF16)
            dyt_ref[...] = dy0.T.astype(BF16)

        da = _dot(wdn_ref[...], dyt_ref[...])
        dgate = (da * act_ref[0].astype(F32)).astype(BF16)
        dup = (da * act_ref[1].astype(F32)).astype(BF16)
        dgu_ref[0] = dgate
        dgu_ref[1] = dup
        dn = _dot_tn(dgate, wup_ref[0]) + _dot_tn(dup, wup_ref[1])

        @pl.when(j == 0)
        def _():
            acc_ref[...] = dn

        @pl.when(j > 0)
        def _():
            acc_ref[...] += dn

        @pl.when(j == nj - 1)
        def _():
            hh = h_ref[...]
            r = _rstd(hh)
            xhat = hh * r
            dnf = acc_ref[...]
            dxh = dnf * g_ref[...]
            dh_ref[...] = dho_ref[...] + r * (dxh - xhat * jnp.mean(dxh * xhat, axis=-1, keepdims=True))
            part = _sum8(dnf * xhat)

            @pl.when(i == 0)
            def _():
                dg_ref[...] = part

            @pl.when(i > 0)
            def _():
                dg_ref[...] += part

    return _pcall(
        body, name="ffn_bwd_act", grid=(ni, nj),
        in_specs=[pl.BlockSpec((tm, d), lambda i, j: (i, 0)),
                  pl.BlockSpec((tm, d), lambda i, j: (i, 0)),
                  pl.BlockSpec((1, d), lambda i, j: (0, 0)),
                  pl.BlockSpec((2, tf, tm), lambda i, j: (0, j, i)),
                  pl.BlockSpec((2, tf, d), lambda i, j: (0, j, 0)),
                  pl.BlockSpec((tf, d), lambda i, j: (j, 0)),
                  pl.BlockSpec(memory_space=pl.ANY)],
        out_specs=[pl.BlockSpec((tm, d), lambda i, j: (i, 0)),
                   pl.BlockSpec((2, tf, tm), lambda i, j: (0, j, i)),
                   pl.BlockSpec((tm, d), lambda i, j: (i, 0)),
                   pl.BlockSpec((8, d), lambda i, j: (0, 0))],
        out_shape=[jax.ShapeDtypeStruct((t, d), F32),
                   jax.ShapeDtypeStruct((2, f, t), BF16),
                   jax.ShapeDtypeStruct((t, d), BF16),
                   jax.ShapeDtypeStruct((8, d), F32)],
        scratch_shapes=[pltpu.VMEM((d, tm), BF16), pltpu.VMEM((tm, d), F32)],
        compiler_params=_params(("arbitrary", "arbitrary")),
    )(dho, h, g, act, wup_t, wdn, dep)


def _ffn_bwd_w(planes, first, count, rhs, dep, name):
    _, f, t = planes.shape
    d = rhs.shape[1]
    tm, tf = _tok_block(t, 1024), _feat_block(f)
    ni, nj = t // tm, f // tf

    def body(lhs_ref, rhs_ref, dep_ref, dw_ref, acc_ref):
        i = pl.program_id(1)
        rb = rhs_ref[...]
        for k in range(count):
            part = _dot(lhs_ref[k], rb)

            @pl.when(i == 0)
            def _():
                acc_ref[k] = part

            @pl.when(i > 0)
            def _():
                acc_ref[k] += part

        @pl.when(i == ni - 1)
        def _():
            dw_ref[...] = acc_ref[...].astype(BF16)

    return _pcall(
        body, name=name, grid=(nj, ni),
        in_specs=[pl.BlockSpec((count, tf, tm), lambda j, i: (first // count, j, i)),
                  pl.BlockSpec((tm, d), lambda j, i: (i, 0)),
                  pl.BlockSpec(memory_space=pl.ANY)],
        out_specs=pl.BlockSpec((count, tf, d), lambda j, i: (0, j, 0)),
        out_shape=jax.ShapeDtypeStruct((count, f, d), BF16),
        scratch_shapes=[pltpu.VMEM((count, tf, d), F32)],
        compiler_params=_params(("parallel", "arbitrary")),
    )(planes, rhs, dep)


N_HEADS = N_SWA_HEADS + N_MEM_HEADS


def _q_col(hd):
    return COL_Q + HEAD_DIM * hd if hd < N_SWA_HEADS else COL_QM + HEAD_DIM * (hd - N_SWA_HEADS)


def _mix_proj_fwd(h, g, win_t):
    t, d = h.shape
    tm = _tok_block(t)

    def body(h_ref, g_ref, win_ref, p_ref, n_ref, qh_ref):
        hh = h_ref[...]
        n = (hh * _rstd(hh) * g_ref[...]).astype(BF16)
        n_ref[...] = n
        proj = _dot_nt(n, win_ref[...])
        p_ref[...] = proj.astype(BF16)
        for hd in range(N_HEADS):
            c0 = _q_col(hd)
            qh_ref[hd] = (proj[:, c0:c0 + HEAD_DIM] * SCALE).astype(BF16)

    return _pcall(
        body, name="mix_proj_fwd", grid=(t // tm,),
        in_specs=[pl.BlockSpec((tm, d), lambda i: (i, 0)),
                  pl.BlockSpec((1, d), lambda i: (0, 0)),
                  pl.BlockSpec((D_IN, d), lambda i: (0, 0))],
        out_specs=[pl.BlockSpec((tm, D_IN), lambda i: (i, 0)),
                   pl.BlockSpec((tm, d), lambda i: (i, 0)),
                   pl.BlockSpec((N_HEADS, tm, HEAD_DIM), lambda i: (0, i, 0))],
        out_shape=[jax.ShapeDtypeStruct((t, D_IN), BF16), jax.ShapeDtypeStruct((t, d), BF16),
                   jax.ShapeDtypeStruct((N_HEADS, t, HEAD_DIM), BF16)],
        compiler_params=_params(("parallel",)),
    )(h, g, win_t)


def _memkv_fwd(mem, g, wkv, dep):
    m, d = mem.shape

    def body(mem_ref, g_ref, w_ref, dep_ref, mkv_ref, nt_ref):
        mm = mem_ref[...]
        n = mm * _rstd(mm) * g_ref[...]
        nt_ref[...] = n.T.astype(BF16)
        mkv_ref[...] = _dot(n.astype(BF16), w_ref[...]).astype(BF16)

    return _pcall(
        body, name="memkv_fwd", grid=(1,),
        in_specs=[pl.BlockSpec((m, d), lambda i: (0, 0)),
                  pl.BlockSpec((1, d), lambda i: (0, 0)),
                  pl.BlockSpec((d, 2 * D_MEMQ), lambda i: (0, 0)),
                  pl.BlockSpec(memory_space=pl.ANY)],
        out_specs=[pl.BlockSpec((m, 2 * D_MEMQ), lambda i: (0, 0)),
                   pl.BlockSpec((d, m), lambda i: (0, 0))],
        out_shape=[jax.ShapeDtypeStruct((m, 2 * D_MEMQ), BF16), jax.ShapeDtypeStruct((d, m), BF16)],
        compiler_params=_params(("arbitrary",)),
    )(mem, g, wkv, dep)


def _memkv_bwd(dmkv, mem, g, wkv, nt):
    m, d = mem.shape

    def body(dmkv_ref, mem_ref, g_ref, w_ref, nt_ref, dw_ref, dg_ref):
        db = dmkv_ref[...].astype(BF16)
        dw_ref[...] = _dot(nt_ref[...], db).astype(BF16)
        dn = _dot_nt(db, w_ref[...])
        mm = mem_ref[...]
        dg_ref[...] = _sum8(dn * (mm * _rstd(mm)))

    return _pcall(
        body, name="memkv_bwd", grid=(1,),
        in_specs=[pl.BlockSpec((m, 2 * D_MEMQ), lambda i: (0, 0)),
                  pl.BlockSpec((m, d), lambda i: (0, 0)),
                  pl.BlockSpec((1, d), lambda i: (0, 0)),
                  pl.BlockSpec((d, 2 * D_MEMQ), lambda i: (0, 0)),
                  pl.BlockSpec((d, m), lambda i: (0, 0))],
        out_specs=[pl.BlockSpec((d, 2 * D_MEMQ), lambda i: (0, 0)),
                   pl.BlockSpec((8, d), lambda i: (0, 0))],
        out_shape=[jax.ShapeDtypeStruct((d, 2 * D_MEMQ), BF16), jax.ShapeDtypeStruct((8, d), F32)],
        compiler_params=_params(("arbitrary",)),
    )(dmkv, mem, g, wkv, nt)


def _shift_rows(v, k, edge_rows, row):
    out = pltpu.roll(v, k, 0)
    for r in range(k):
        out = jnp.where(row == r, edge_rows[r], out)
    return out


def _shift_rows_up(v, k, edge_rows, row):
    n = v.shape[0]
    out = pltpu.roll(v, n - k, 0)
    for r in range(k):
        out = jnp.where(row == n - k + r, edge_rows[r], out)
    return out


GROUP_ROWS = SWA_GROUP * BLOCK
BIAS_CUR, BIAS_PREV, BIAS_NONE = 0, 1, 2


def _bias_tables():
    tq = np.arange(BLOCK)[:, None]
    sk = np.arange(BLOCK)[None, :]
    slopes = np.asarray(SLOPES, np.float32)[:, None, None]
    cur = np.where(tq >= sk, -slopes * (tq - sk).astype(np.float32), NEG)
    prev = np.where(sk > tq, -slopes * (tq + BLOCK - sk).astype(np.float32), NEG)
    none = np.full_like(cur, NEG)
    tok = np.stack([cur, prev, none]).astype(np.float32).reshape(3, N_SWA_KV, GROUP_ROWS, BLOCK)
    return jnp.asarray(tok), jnp.asarray(np.ascontiguousarray(tok.transpose(0, 1, 3, 2)))


def _head_cols(hd):
    return D_CONV + HEAD_DIM * hd


def _stack_cols(ref, heads):
    return jnp.concatenate([ref[:, hd:hd + 1] for hd in heads], axis=0)


def _mix_core_fwd(p, qh, mkv, convw, sinks, bias_tok):
    t = p.shape[0]
    m = mkv.shape[0]
    nb = t // BLOCK

    def body(sk_ref, pc_ref, pkv_ref, ppc_ref, ppu_ref, qh_ref, mkv_ref, cw_ref, bc_ref, bp_ref, y_ref, l_ref):
        i = pl.program_id(0)
        prevf = (i > 0).astype(F32)
        row = lax.broadcasted_iota(jnp.int32, (BLOCK, D_CONV), 0)

        bg = pc_ref[:, COL_BG:COL_BG + D_CONV].astype(F32)
        cg = pc_ref[:, COL_CG:COL_CG + D_CONV].astype(F32)
        u = pc_ref[:, COL_U:COL_U + D_CONV].astype(F32)
        vv = cg * u
        pvv = ppc_ref[...].astype(F32) * ppu_ref[...].astype(F32) * prevf
        vv1 = _shift_rows(vv, 1, [pvv[15:16]], row)
        vv2 = _shift_rows(vv, 2, [pvv[14:15], pvv[15:16]], row)
        w = cw_ref[...]
        y_ref[:, 0:D_CONV] = bg * (w[0:1] * vv2 + w[1:2] * vv1 + w[2:3] * vv)

        lane = lax.broadcasted_iota(jnp.int32, (BLOCK, 128), 1)
        lse_all = jnp.zeros((BLOCK, 128), F32)
        for kv in range(N_SWA_KV):
            heads = range(kv * SWA_GROUP, (kv + 1) * SWA_GROUP)
            kc = pc_ref[:, COL_K + HEAD_DIM * kv:COL_K + HEAD_DIM * (kv + 1)]
            vc = pc_ref[:, COL_V + HEAD_DIM * kv:COL_V + HEAD_DIM * (kv + 1)]
            kp = pkv_ref[:, HEAD_DIM * kv:HEAD_DIM * (kv + 1)]
            vp = pkv_ref[:, D_KV + HEAD_DIM * kv:D_KV + HEAD_DIM * (kv + 1)]
            qg = qh_ref[kv * SWA_GROUP:(kv + 1) * SWA_GROUP].reshape(GROUP_ROWS, HEAD_DIM)
            sc = _dot_nt(qg, kc) + bc_ref[0, kv]
            sp = _dot_nt(qg, kp) + bp_ref[0, kv]
            sink = jnp.concatenate([jnp.full((BLOCK, 1), sk_ref[0, hd], F32) for hd in heads], axis=0)
            mx = jnp.maximum(jnp.max(jnp.maximum(sc, sp), axis=-1, keepdims=True), sink)
            ec = jnp.exp(sc - mx)
            ep = jnp.exp(sp - mx)
            den = jnp.sum(ec + ep, axis=-1, keepdims=True) + jnp.exp(sink - mx)
            o = (_dot(ec.astype(BF16), vc) + _dot(ep.astype(BF16), vp)) / den
            lse = mx + jnp.log(den)
            for gi, hd in enumerate(heads):
                rows = slice(gi * BLOCK, (gi + 1) * BLOCK)
                y_ref[:, _head_cols(hd):_head_cols(hd) + HEAD_DIM] = o[rows]
                lse_all = jnp.where(lane == hd, lse[rows], lse_all)

        for hm in range(N_MEM_HEADS):
            hd = N_SWA_HEADS + hm
            mk = mkv_ref[:, HEAD_DIM * hm:HEAD_DIM * (hm + 1)]
            mv = mkv_ref[:, D_MEMQ + HEAD_DIM * hm:D_MEMQ + HEAD_DIM * (hm + 1)]
            s = _dot_nt(qh_ref[hd], mk)
            mx = jnp.max(s, axis=-1, keepdims=True)
            e = jnp.exp(s - mx)
            den = jnp.sum(e, axis=-1, keepdims=True)
            y_ref[:, _head_cols(hd):_head_cols(hd) + HEAD_DIM] = _dot(e.astype(BF16), mv) / den
            lse_all = jnp.where(lane == hd, mx + jnp.log(den), lse_all)
        l_ref[...] = lse_all

    kv_col = COL_K // (2 * D_KV)
    bias_block = (1, N_SWA_KV, GROUP_ROWS, BLOCK)
    return _pcall(
        body, name="mix_core_fwd", grid=(nb,),
        in_specs=[pl.BlockSpec(memory_space=pltpu.SMEM),
                  pl.BlockSpec((BLOCK, D_IN), lambda i: (i, 0)),
                  pl.BlockSpec((BLOCK, 2 * D_KV), lambda i: (jnp.maximum(i - 1, 0), kv_col)),
                  pl.BlockSpec((16, D_CONV), lambda i: (jnp.maximum(i * (BLOCK // 16) - 1, 0), COL_CG // D_CONV)),
                  pl.BlockSpec((16, D_CONV), lambda i: (jnp.maximum(i * (BLOCK // 16) - 1, 0), COL_U // D_CONV)),
                  pl.BlockSpec((N_HEADS, BLOCK, HEAD_DIM), lambda i: (0, i, 0)),
                  pl.BlockSpec((m, 2 * D_MEMQ), lambda i: (0, 0)),
                  pl.BlockSpec((3, D_CONV), lambda i: (0, 0)),
                  pl.BlockSpec(bias_block, lambda i: (BIAS_CUR, 0, 0, 0)),
                  pl.BlockSpec(bias_block, lambda i: (jnp.where(i == 0, BIAS_NONE, BIAS_PREV), 0, 0, 0))],
        out_specs=[pl.BlockSpec((BLOCK, D_MIX), lambda i: (i, 0)),
                   pl.BlockSpec((BLOCK, 128), lambda i: (i, 0))],
        out_shape=[jax.ShapeDtypeStruct((t, D_MIX), F32), jax.ShapeDtypeStruct((t, 128), F32)],
        compiler_params=_params(("parallel",)),
    )(sinks, p, p, p, p, qh, mkv, convw, bias_tok, bias_tok)


def _mix_core_bwd(p, qh, dyconv, doh, delta, lse, mkv, convw, sinks, bias_tok, bias_key):
    t = p.shape[0]
    m = mkv.shape[0]
    nb = t // BLOCK

    def body(sk_ref, pc_ref, pkv_ref, ppc_ref, ppu_ref, pnb_ref, dyc_ref, dyn_ref, qc_ref, qn_ref, doc_ref, don_ref,
             dlc_ref, dln_ref, lc_ref, ln_ref, mkv_ref, cw_ref, bp_ref, bct_ref, bnt_ref,
             dp_ref, dmkv_ref, dcw_ref, dsk_ref):
        i = pl.program_id(0)
        prevf = (i > 0).astype(F32)
        nextf = (i < nb - 1).astype(F32)
        row = lax.broadcasted_iota(jnp.int32, (BLOCK, D_CONV), 0)

        @pl.when(i == 0)
        def _():
            dmkv_ref[...] = jnp.zeros_like(dmkv_ref)
            dcw_ref[...] = jnp.zeros_like(dcw_ref)
            dsk_ref[...] = jnp.zeros_like(dsk_ref)

        bg = pc_ref[:, COL_BG:COL_BG + D_CONV].astype(F32)
        cg = pc_ref[:, COL_CG:COL_CG + D_CONV].astype(F32)
        u = pc_ref[:, COL_U:COL_U + D_CONV].astype(F32)
        vv = cg * u
        pvv = ppc_ref[...].astype(F32) * ppu_ref[...].astype(F32) * prevf
        vv1 = _shift_rows(vv, 1, [pvv[15:16]], row)
        vv2 = _shift_rows(vv, 2, [pvv[14:15], pvv[15:16]], row)
        w = cw_ref[...]
        yconv = w[0:1] * vv2 + w[1:2] * vv1 + w[2:3] * vv
        dyo = dyc_ref[...]
        dyc = dyo * bg
        nxt = dyn_ref[...] * pnb_ref[...].astype(F32) * nextf
        d1 = _shift_rows_up(dyc, 1, [nxt[0:1]], row)
        d2 = _shift_rows_up(dyc, 2, [nxt[0:1], nxt[1:2]], row)
        dvv = w[2:3] * dyc + w[1:2] * d1 + w[0:1] * d2
        dp_ref[:, COL_BG:COL_BG + D_CONV] = (dyo * yconv).astype(BF16)
        dp_ref[:, COL_CG:COL_CG + D_CONV] = (dvv * u).astype(BF16)
        dp_ref[:, COL_U:COL_U + D_CONV] = (dvv * cg).astype(BF16)
        dcw_ref[0:1, :] += jnp.sum(dyc * vv2, axis=0, keepdims=True)
        dcw_ref[1:2, :] += jnp.sum(dyc * vv1, axis=0, keepdims=True)
        dcw_ref[2:3, :] += jnp.sum(dyc * vv, axis=0, keepdims=True)

        lse_t, dl_t = lc_ref[...].T, dlc_ref[...].T
        lse_nt, dl_nt = ln_ref[...].T, dln_ref[...].T

        def stack_rows(tile_t, heads):
            return jnp.concatenate([tile_t[hd:hd + 1, :] for hd in heads], axis=1)

        lane8 = jnp.where(lax.broadcasted_iota(jnp.int32, (8, 128), 0) == 0,
                          lax.broadcasted_iota(jnp.int32, (8, 128), 1), -1)
        dsk = jnp.zeros((8, 128), F32)
        for kv in range(N_SWA_KV):
            heads = range(kv * SWA_GROUP, (kv + 1) * SWA_GROUP)
            kc = pc_ref[:, COL_K + HEAD_DIM * kv:COL_K + HEAD_DIM * (kv + 1)]
            vc = pc_ref[:, COL_V + HEAD_DIM * kv:COL_V + HEAD_DIM * (kv + 1)]
            kp = pkv_ref[:, HEAD_DIM * kv:HEAD_DIM * (kv + 1)]
            vp = pkv_ref[:, D_KV + HEAD_DIM * kv:D_KV + HEAD_DIM * (kv + 1)]
            qg = qc_ref[kv * SWA_GROUP:(kv + 1) * SWA_GROUP].reshape(GROUP_ROWS, HEAD_DIM)
            dog = doc_ref[kv * SWA_GROUP:(kv + 1) * SWA_GROUP].reshape(GROUP_ROWS, HEAD_DIM)
            qn = qn_ref[kv * SWA_GROUP:(kv + 1) * SWA_GROUP].reshape(GROUP_ROWS, HEAD_DIM)
            don = don_ref[kv * SWA_GROUP:(kv + 1) * SWA_GROUP].reshape(GROUP_ROWS, HEAD_DIM)
            lse_col, dl_col = _stack_cols(lc_ref, heads), _stack_cols(dlc_ref, heads)
            pp_ = jnp.exp(_dot_nt(qg, kp) + bp_ref[0, kv] - lse_col)
            dsp = (pp_ * (_dot_nt(dog, vp) - dl_col)).astype(BF16)
            dq = _dot(dsp, kp)
            pt = jnp.exp(_dot_nt(kc, qg) + bct_ref[0, kv] - stack_rows(lse_t, heads))
            dst = (pt * (_dot_nt(vc, dog) - stack_rows(dl_t, heads))).astype(BF16)
            dv = _dot(pt.astype(BF16), dog)
            dk = _dot(dst, qg)
            dq = dq + _dot_tn(dst, kc)
            ptn = jnp.exp(_dot_nt(kc, qn) + bnt_ref[0, kv] - stack_rows(lse_nt, heads))
            dstn = (ptn * (_dot_nt(vc, don) - stack_rows(dl_nt, heads))).astype(BF16)
            dv = dv + _dot(ptn.astype(BF16), don)
            dk = dk + _dot(dstn, qn)
            dp_ref[:, COL_K + HEAD_DIM * kv:COL_K + HEAD_DIM * (kv + 1)] = dk.astype(BF16)
            dp_ref[:, COL_V + HEAD_DIM * kv:COL_V + HEAD_DIM * (kv + 1)] = dv.astype(BF16)
            sink = jnp.concatenate([jnp.full((BLOCK, 1), sk_ref[0, hd], F32) for hd in heads], axis=0)
            sink_term = jnp.exp(sink - lse_col) * dl_col
            for gi, hd in enumerate(heads):
                rows = slice(gi * BLOCK, (gi + 1) * BLOCK)
                dp_ref[:, _q_col(hd):_q_col(hd) + HEAD_DIM] = (dq[rows] * SCALE).astype(BF16)
                dsk = dsk + jnp.where(lane8 == hd, -jnp.sum(sink_term[rows], axis=0, keepdims=True), 0.0)
        dsk_ref[...] += dsk

        for hm in range(N_MEM_HEADS):
            hd = N_SWA_HEADS + hm
            qm, dom = qc_ref[hd], doc_ref[hd]
            mk = mkv_ref[:, HEAD_DIM * hm:HEAD_DIM * (hm + 1)]
            mv = mkv_ref[:, D_MEMQ + HEAD_DIM * hm:D_MEMQ + HEAD_DIM * (hm + 1)]
            pt = jnp.exp(_dot_nt(mk, qm) - lse_t[hd:hd + 1, :])
            dst = (pt * (_dot_nt(mv, dom) - dl_t[hd:hd + 1, :])).astype(BF16)
            dp_ref[:, _q_col(hd):_q_col(hd) + HEAD_DIM] = (_dot_tn(dst, mk) * SCALE).astype(BF16)
            dmkv_ref[:, HEAD_DIM * hm:HEAD_DIM * (hm + 1)] += _dot(dst, qm)
            dmkv_ref[:, D_MEMQ + HEAD_DIM * hm:D_MEMQ + HEAD_DIM * (hm + 1)] += _dot(pt.astype(BF16), dom)

    cur = lambda i: (i, 0)
    const = lambda i: (0, 0)
    rows16 = BLOCK // 16
    last16 = t // 16 - 1
    before = lambda col: (lambda i: (jnp.maximum(i * rows16 - 1, 0), col))
    after = lambda i: (jnp.minimum((i + 1) * rows16, last16), 0)
    heads_cur = lambda i: (0, i, 0)
    heads_next = lambda i: (0, jnp.minimum(i + 1, nb - 1), 0)
    stat_next = lambda i: (jnp.minimum(i + 1, nb - 1), 0)
    tok_block = (1, N_SWA_KV, GROUP_ROWS, BLOCK)
    key_block = (1, N_SWA_KV, BLOCK, GROUP_ROWS)
    head_block = (N_HEADS, BLOCK, HEAD_DIM)
    return _pcall(
        body, name="mix_core_bwd", grid=(nb,),
        in_specs=[pl.BlockSpec(memory_space=pltpu.SMEM),
                  pl.BlockSpec((BLOCK, D_IN), cur),
                  pl.BlockSpec((BLOCK, 2 * D_KV), lambda i: (jnp.maximum(i - 1, 0), COL_K // (2 * D_KV))),
                  pl.BlockSpec((16, D_CONV), before(COL_CG // D_CONV)),
                  pl.BlockSpec((16, D_CONV), before(COL_U // D_CONV)),
                  pl.BlockSpec((16, D_CONV), after),
                  pl.BlockSpec((BLOCK, D_CONV), cur),
                  pl.BlockSpec((16, D_CONV), after),
                  pl.BlockSpec(head_block, heads_cur), pl.BlockSpec(head_block, heads_next),
                  pl.BlockSpec(head_block, heads_cur), pl.BlockSpec(head_block, heads_next),
                  pl.BlockSpec((BLOCK, 128), cur), pl.BlockSpec((BLOCK, 128), stat_next),
                  pl.BlockSpec((BLOCK, 128), cur), pl.BlockSpec((BLOCK, 128), stat_next),
                  pl.BlockSpec((m, 2 * D_MEMQ), const),
                  pl.BlockSpec((3, D_CONV), const),
                  pl.BlockSpec(tok_block, lambda i: (jnp.where(i == 0, BIAS_NONE, BIAS_PREV), 0, 0, 0)),
                  pl.BlockSpec(key_block, lambda i: (BIAS_CUR, 0, 0, 0)),
                  pl.BlockSpec(key_block, lambda i: (jnp.where(i == nb - 1, BIAS_NONE, BIAS_PREV), 0, 0, 0))],
        out_specs=[pl.BlockSpec((BLOCK, D_IN), cur),
                   pl.BlockSpec((m, 2 * D_MEMQ), const),
                   pl.BlockSpec((8, D_CONV), const),
                   pl.BlockSpec((8, 128), const)],
        out_shape=[jax.ShapeDtypeStruct((t, D_IN), BF16),
                   jax.ShapeDtypeStruct((m, 2 * D_MEMQ), F32),
                   jax.ShapeDtypeStruct((8, D_CONV), F32),
                   jax.ShapeDtypeStruct((8, 128), F32)],
        compiler_params=_params(("arbitrary",)),
    )(sinks, p, p, p, p, p, dyconv, dyconv, qh, qh, doh, doh, delta, delta, lse, lse, mkv, convw,
      bias_tok, bias_key, bias_key)


def _group_norms(y):
    out = []
    for a, b in MIX_GROUPS:
        ys = y[:, a:b]
        r = _rstd(ys)
        out.append((ys * r, r))
    return out


def _mix_out_fwd(y, h, g, wout):
    t, d = h.shape
    tm = _tok_block(t)

    def body(y_ref, h_ref, g_ref, w_ref, ho_ref, mt_ref):
        yhat = jnp.concatenate([yh for yh, _ in _group_norms(y_ref[...])], axis=-1)
        mixed = yhat * g_ref[...]
        mt_ref[...] = mixed.T.astype(BF16)
        ho_ref[...] = h_ref[...] + _dot(mixed.astype(BF16), w_ref[...])

    return _pcall(
        body, name="mix_out_fwd", grid=(t // tm,),
        in_specs=[pl.BlockSpec((tm, D_MIX), lambda i: (i, 0)),
                  pl.BlockSpec((tm, d), lambda i: (i, 0)),
                  pl.BlockSpec((1, D_MIX), lambda i: (0, 0)),
                  pl.BlockSpec((D_MIX, d), lambda i: (0, 0))],
        out_specs=[pl.BlockSpec((tm, d), lambda i: (i, 0)),
                   pl.BlockSpec((D_MIX, tm), lambda i: (0, i))],
        out_shape=[jax.ShapeDtypeStruct((t, d), F32), jax.ShapeDtypeStruct((D_MIX, t), BF16)],
        compiler_params=_params(("parallel",)),
    )(y, h, g, wout)


def _head_indicator():
    ind = np.zeros((D_MIX, 128), np.float32)
    for hd in range(N_HEADS):
        ind[_head_cols(hd):_head_cols(hd) + HEAD_DIM, hd] = 1.0
    return jnp.asarray(ind, BF16)


def _mix_out_bwd(dho, y, g, wout, mt, dep):
    t, d = dho.shape
    tm = _tok_block(t)
    ni = t // tm

    def body(dho_ref, y_ref, g_ref, w_ref, mt_ref, ind_ref, dep_ref, dyc_ref, doh_ref, dl_ref, dw_ref, dg_ref, acc_ref):
        i = pl.program_id(0)
        dhb = dho_ref[...].astype(BF16)
        dm = _dot_nt(dhb, w_ref[...])
        pw = _dot(mt_ref[...], dhb)
        gg = g_ref[...]
        yy = y_ref[...]
        dys = []
        dgs = []
        for (a, b), (yhat, r) in zip(MIX_GROUPS, _group_norms(yy)):
            dmg = dm[:, a:b]
            dgs.append(_sum8(dmg * yhat))
            dyh = dmg * gg[:, a:b]
            dys.append(r * (dyh - yhat * jnp.mean(dyh * yhat, axis=-1, keepdims=True)))
        dy = jnp.concatenate(dys, axis=-1)
        dyc_ref[...] = dy[:, 0:D_CONV]
        for hd in range(N_HEADS):
            doh_ref[hd] = dy[:, _head_cols(hd):_head_cols(hd) + HEAD_DIM].astype(BF16)
        prod = dy * yy
        hi = prod.astype(BF16)
        lo = (prod - hi.astype(F32)).astype(BF16)
        dl_ref[...] = _dot(hi, ind_ref[...]) + _dot(lo, ind_ref[...])
        part = jnp.concatenate(dgs, axis=-1)

        @pl.when(i == 0)
        def _():
            acc_ref[...] = pw
            dg_ref[...] = part

        @pl.when(i > 0)
        def _():
            acc_ref[...] += pw
            dg_ref[...] += part

        @pl.when(i == ni - 1)
        def _():
            dw_ref[...] = acc_ref[...].astype(BF16)

    return _pcall(
        body, name="mix_out_bwd", grid=(ni,),
        in_specs=[pl.BlockSpec((tm, d), lambda i: (i, 0)),
                  pl.BlockSpec((tm, D_MIX), lambda i: (i, 0)),
                  pl.BlockSpec((1, D_MIX), lambda i: (0, 0)),
                  pl.BlockSpec((D_MIX, d), lambda i: (0, 0)),
                  pl.BlockSpec((D_MIX, tm), lambda i: (0, i)),
                  pl.BlockSpec((D_MIX, 128), lambda i: (0, 0)),
                  pl.BlockSpec(memory_space=pl.ANY)],
        out_specs=[pl.BlockSpec((tm, D_CONV), lambda i: (i, 0)),
                   pl.BlockSpec((N_HEADS, tm, HEAD_DIM), lambda i: (0, i, 0)),
                   pl.BlockSpec((tm, 128), lambda i: (i, 0)),
                   pl.BlockSpec((D_MIX, d), lambda i: (0, 0)),
                   pl.BlockSpec((8, D_MIX), lambda i: (0, 0))],
        out_shape=[jax.ShapeDtypeStruct((t, D_CONV), F32),
                   jax.ShapeDtypeStruct((N_HEADS, t, HEAD_DIM), BF16),
                   jax.ShapeDtypeStruct((t, 128), F32),
                   jax.ShapeDtypeStruct((D_MIX, d), BF16),
                   jax.ShapeDtypeStruct((8, D_MIX), F32)],
        scratch_shapes=[pltpu.VMEM((D_MIX, d), F32)],
        compiler_params=_params(("arbitrary",)),
    )(dho, y, g, wout, mt, _head_indicator(), dep)


def _mix_proj_bwd(dp, dho, h, g, win_t, n):
    t, d = h.shape
    tm = _tok_block(t)
    ni = t // tm

    def body(dp_ref, dho_ref, h_ref, g_ref, w_ref, n_ref, dh_ref, dw_ref, dg_ref, acc_ref):
        i = pl.program_id(0)
        dpb = dp_ref[...]
        dn = _dot(dpb, w_ref[...])
        pw = _dot_tn(dpb, n_ref[...])
        hh = h_ref[...]
        r = _rstd(hh)
        xhat = hh * r
        dxh = dn * g_ref[...]
        dh_ref[...] = dho_ref[...] + r * (dxh - xhat * jnp.mean(dxh * xhat, axis=-1, keepdims=True))
        part = _sum8(dn * xhat)

        @pl.when(i == 0)
        def _():
            acc_ref[...] = pw
            dg_ref[...] = part

        @pl.when(i > 0)
        def _():
            acc_ref[...] += pw
            dg_ref[...] += part

        @pl.when(i == ni - 1)
        def _():
            dw_ref[...] = acc_ref[...].astype(BF16)

    return _pcall(
        body, name="mix_proj_bwd", grid=(ni,),
        in_specs=[pl.BlockSpec((tm, D_IN), lambda i: (i, 0)),
                  pl.BlockSpec((tm, d), lambda i: (i, 0)),
                  pl.BlockSpec((tm, d), lambda i: (i, 0)),
                  pl.BlockSpec((1, d), lambda i: (0, 0)),
                  pl.BlockSpec((D_IN, d), lambda i: (0, 0)),
                  pl.BlockSpec((tm, d), lambda i: (i, 0))],
        out_specs=[pl.BlockSpec((tm, d), lambda i: (i, 0)),
                   pl.BlockSpec((D_IN, d), lambda i: (0, 0)),
                   pl.BlockSpec((8, d), lambda i: (0, 0))],
        out_shape=[jax.ShapeDtypeStruct((t, d), F32),
                   jax.ShapeDtypeStruct((D_IN, d), BF16),
                   jax.ShapeDtypeStruct((8, d), F32)],
        scratch_shapes=[pltpu.VMEM((D_IN, d), F32)],
        compiler_params=_params(("arbitrary",)),
    )(dp, dho, h, g, win_t, n)


def _final_loss(h, g, tgt):
    t, d = h.shape
    tm = _tok_block(t)

    def body(h_ref, g_ref, t_ref, dh_ref, ls_ref, dg_ref):
        i = pl.program_id(0)
        hh = h_ref[...]
        r = _rstd(hh)
        xhat = hh * r
        gg = g_ref[...]
        err = xhat * gg - t_ref[...]
        dy = err * (1.0 / d)
        dxh = dy * gg
        dh_ref[...] = r * (dxh - xhat * jnp.mean(dxh * xhat, axis=-1, keepdims=True))
        lpart = _sum8(err * err)
        gpart = _sum8(dy * xhat)

        @pl.when(i == 0)
        def _():
            ls_ref[...] = lpart
            dg_ref[...] = gpart

        @pl.when(i > 0)
        def _():
            ls_ref[...] += lpart
            dg_ref[...] += gpart

    return _pcall(
        body, name="final_loss", grid=(t // tm,),
        in_specs=[pl.BlockSpec((tm, d), lambda i: (i, 0)),
                  pl.BlockSpec((1, d), lambda i: (0, 0)),
                  pl.BlockSpec((tm, d), lambda i: (i, 0))],
        out_specs=[pl.BlockSpec((tm, d), lambda i: (i, 0)),
                   pl.BlockSpec((8, d), lambda i: (0, 0)),
                   pl.BlockSpec((8, d), lambda i: (0, 0))],
        out_shape=[jax.ShapeDtypeStruct((t, d), F32),
                   jax.ShapeDtypeStruct((8, d), F32),
                   jax.ShapeDtypeStruct((8, d), F32)],
        compiler_params=_params(("arbitrary",)),
    )(h, g, tgt)


def _position():
    return lax.axis_index("x"), lax.axis_index("y"), lax.axis_index("c")


def _flip(v, bit):
    return 1 - v if bit else v


def _peer(k):
    x, y, c = _position()
    return _flip(x, k & 4), _flip(y, k & 2), _flip(c, k & 1)


def _slot(px, py, pc):
    return 4 * px + 2 * py + pc


def _handshake(peers):
    barrier = pltpu.get_barrier_semaphore()
    for peer in peers:
        pl.semaphore_signal(barrier, inc=1, device_id=peer, device_id_type=MESH)
    pl.semaphore_wait(barrier, len(peers))


def _sequencer_call(body, name, collective_id, out_type, scratch_types, operands):
    return pl.kernel(
        body, out_type=out_type, mesh=plsc.ScalarSubcoreMesh(axis_name="sequencer", num_cores=1), name=name,
        scratch_types=scratch_types, compiler_params=pltpu.CompilerParams(collective_id=collective_id),
    )(*operands)


def _all_gather(shards, name, collective_id):
    nt = len(shards)

    def body(*refs):
        xs = refs[:nt]
        outs = refs[nt:2 * nt]
        send_sems, recv_sems, local_sems = refs[2 * nt:]
        x, y, c = _position()
        me, sibling = (x, y, c), (x, y, 1 - c)
        chips = [(1 - x, y), (x, 1 - y), (1 - x, 1 - y)]
        _handshake([sibling] + [(*chip, c) for chip in chips])

        def copy(t, k, block, to, src=None):
            dst = outs[t].at[_slot(*block)]
            return pltpu.make_async_remote_copy(
                src_ref=dst if src is None else src, dst_ref=dst,
                send_sem=send_sems.at[t, k], recv_sem=recv_sems.at[t, k],
                device_id=to, device_id_type=MESH)

        mine = [pltpu.make_async_copy(xs[t], outs[t].at[_slot(*me)], local_sems.at[t]) for t in range(nt)]
        for cp in mine:
            cp.start()
        first = []
        for t in range(nt):
            first.append(copy(t, 0, me, sibling, src=xs[t]))
            first += [copy(t, 1 + j, me, (*chip, c), src=xs[t]) for j, chip in enumerate(chips)]
        for cp in first:
            cp.start()
        passed = []
        for j, chip in enumerate(chips):
            for t in range(nt):
                copy(t, 1 + j, (*chip, c), me).wait_recv()
                fwd = copy(t, 4 + j, (*chip, c), sibling)
                fwd.start()
                passed.append(fwd)
        for t in range(nt):
            copy(t, 0, sibling, me).wait_recv()
            for j, chip in enumerate(chips):
                copy(t, 4 + j, (*chip, 1 - c), me).wait_recv()
        for cp in first + passed:
            cp.wait_send()
        for cp in mine:
            cp.wait()

    return _sequencer_call(
        body, name, collective_id,
        out_type=[jax.ShapeDtypeStruct((N_DEV,) + s.shape, s.dtype) for s in shards],
        scratch_types=[pltpu.SemaphoreType.DMA((nt, 7)), pltpu.SemaphoreType.DMA((nt, 7)),
                       pltpu.SemaphoreType.DMA((nt,))],
        operands=shards)


def _scatter_copy(srcs, lands, send_sems, recv_sems, t, k):
    peer = _peer(k)
    return pltpu.make_async_remote_copy(
        src_ref=srcs[t].at[_slot(*peer)], dst_ref=lands[t].at[k],
        send_sem=send_sems.at[t * (N_DEV - 1) + k - 1], recv_sem=recv_sems.at[t * (N_DEV - 1) + k - 1],
        device_id=peer, device_id_type=MESH)


def _scatter_start(partials, name):
    nt = len(partials)

    def body(*refs):
        srcs, lands = refs[:nt], refs[nt:2 * nt]
        send_sems, recv_sems = refs[2 * nt], refs[2 * nt + 1]
        token = refs[-1]
        for k in range(1, N_DEV):
            for t in range(nt):
                _scatter_copy(srcs, lands, send_sems, recv_sems, t, k).start()
        token[...] = jnp.zeros_like(token)

    hbm = pl.BlockSpec(memory_space=pltpu.HBM)
    sem = pl.BlockSpec(memory_space=pltpu.SEMAPHORE)
    shapes = [pltpu.HBM(p.shape, p.dtype) for p in partials]
    lands = [pltpu.with_memory_space_constraint(lax.empty(p.shape, p.dtype), pltpu.HBM) for p in partials]
    srcs = [pltpu.with_memory_space_constraint(p, pltpu.HBM) for p in partials]
    out = _pcall(
        body, name=name,
        out_shape=[pltpu.SemaphoreType.DMA((nt * (N_DEV - 1),))] * 2 + shapes + shapes
        + [jax.ShapeDtypeStruct((8, 128), F32)],
        in_specs=[hbm] * (2 * nt),
        out_specs=[sem, sem] + [hbm] * (2 * nt) + [pl.BlockSpec(memory_space=pltpu.VMEM)],
        input_output_aliases={i: 2 + i for i in range(2 * nt)},
        compiler_params=pltpu.CompilerParams(has_side_effects=pltpu.SideEffectType.DATAFLOW_SIDE_EFFECTING),
    )(*srcs, *lands)
    return (nt, name, out[:-1]), out[-1]


def _scatter_wait(state, after):
    nt, name, (send_sems, recv_sems, *thru) = state

    def body(*refs):
        srcs, lands = refs[:nt], refs[nt:2 * nt]
        send_sems, recv_sems = refs[2 * nt], refs[2 * nt + 1]
        for k in range(1, N_DEV):
            for t in range(nt):
                copy = _scatter_copy(srcs, lands, send_sems, recv_sems, t, k)
                copy.wait_send()
                copy.wait_recv()

    hbm = pl.BlockSpec(memory_space=pltpu.HBM)
    sem = pl.BlockSpec(memory_space=pltpu.SEMAPHORE)
    out = _pcall(
        body, name=name + "_wait",
        out_shape=[pltpu.HBM(a.shape, a.dtype) for a in thru],
        in_specs=[hbm] * (2 * nt) + [sem, sem, pl.BlockSpec(memory_space=pl.ANY)],
        out_specs=[hbm] * (2 * nt),
        input_output_aliases={i: i for i in range(2 * nt)},
        compiler_params=pltpu.CompilerParams(has_side_effects=pltpu.SideEffectType.DATAFLOW_SIDE_EFFECTING),
    )(*thru, send_sems, recv_sems, after)
    return out[:nt], out[nt:]


def _all_reduce_rows(v):
    nv, _, w = v.shape

    def body(v_ref, out_ref, gath_ref, send_sems, recv_sems):
        x, y, c = _position()
        me = _slot(x, y, c)

        def copy(k):
            return pltpu.make_async_remote_copy(
                src_ref=v_ref, dst_ref=gath_ref.at[me],
                send_sem=send_sems.at[k - 1], recv_sem=recv_sems.at[k - 1],
                device_id=_peer(k), device_id_type=MESH)

        def arrival(k):
            return pltpu.make_async_remote_copy(
                src_ref=v_ref, dst_ref=gath_ref.at[_slot(*_peer(k))],
                send_sem=send_sems.at[k - 1], recv_sem=recv_sems.at[k - 1],
                device_id=_peer(k), device_id_type=MESH)

        sent = [copy(k) for k in range(1, N_DEV)]
        for cp in sent:
            cp.start()
        gath_ref[me] = v_ref[...]
        for k in range(1, N_DEV):
            arrival(k).wait_recv()
        for cp in sent:
            cp.wait_send()
        total = gath_ref[0]
        for s in range(1, N_DEV):
            total = total + gath_ref[s]
        out_ref[...] = jnp.sum(total, axis=1)

    vmem = pl.BlockSpec(memory_space=pltpu.VMEM)
    return _pcall(
        body, name="all_reduce_rows",
        in_specs=[vmem], out_specs=vmem,
        out_shape=jax.ShapeDtypeStruct((nv, w), F32),
        scratch_shapes=[pltpu.VMEM((N_DEV, nv, 8, w), F32),
                        pltpu.SemaphoreType.DMA((7,)), pltpu.SemaphoreType.DMA((7,))],
    )(v)


def _adamw_math(w, g, m, v):
    m2 = ADAM_B1 * m + (1.0 - ADAM_B1) * g
    v2 = ADAM_B2 * v + (1.0 - ADAM_B2) * (g * g)
    m_hat = m2 / (1.0 - ADAM_B1 ** ADAM_STEP)
    v_hat = v2 / (1.0 - ADAM_B2 ** ADAM_STEP)
    delta = -ADAM_LR * (m_hat / (jnp.sqrt(v_hat) + ADAM_EPS) + ADAM_WD * w)
    return delta, m2, v2


def _row_block(r):
    for cand in (256, 176, 128):
        if r % cand == 0:
            return cand
    return r


def _adamw_sharded(me, grads, w, m, v):
    (own0, land0), (own1, land1) = grads
    _, r, c = land0.shape
    tr = _row_block(r)
    nr = r // tr

    def body(me_ref, o0_ref, l0_ref, o1_ref, l1_ref, w_ref, m_ref, v_ref, g_ref, d_ref, m2_ref, v2_ref):
        layer = pl.program_id(0)

        def total(own_ref, land_ref):
            acc = own_ref[0].astype(F32)
            for k in range(1, N_DEV):
                acc = acc + land_ref[k].astype(F32)
            return acc

        g = jnp.where(layer == 0, total(o0_ref, l0_ref), total(o1_ref, l1_ref))
        delta, m2, v2 = _adamw_math(w_ref[0], g, m_ref[0], v_ref[0])
        g_ref[0] = g
        d_ref[0] = delta
        m2_ref[0] = m2
        v2_ref[0] = v2

    rows0 = lambda l, i: jnp.where(l == 0, i, nr - 1)
    rows1 = lambda l, i: jnp.where(l == 1, i, 0)
    shard = pl.BlockSpec((1, tr, c), lambda l, i, me_ref: (l, i, 0))
    out = jax.ShapeDtypeStruct((2, r, c), F32)
    return _pcall(
        body, name="adamw_sharded",
        grid_spec=pltpu.PrefetchScalarGridSpec(
            num_scalar_prefetch=1, grid=(2, nr),
            in_specs=[pl.BlockSpec((1, tr, c), lambda l, i, me_ref: (me_ref[0], rows0(l, i), 0)),
                      pl.BlockSpec((N_DEV, tr, c), lambda l, i, me_ref: (0, rows0(l, i), 0)),
                      pl.BlockSpec((1, tr, c), lambda l, i, me_ref: (me_ref[0], rows1(l, i), 0)),
                      pl.BlockSpec((N_DEV, tr, c), lambda l, i, me_ref: (0, rows1(l, i), 0)),
                      shard, shard, shard],
            out_specs=[shard, shard, shard, shard]),
        out_shape=[out, out, out, out],
        compiler_params=_params(("arbitrary", "arbitrary")),
    )(me, own0, land0, own1, land1, w, m, v)


def _adamw_small(w, g, m, v):
    def body(w_ref, g_ref, m_ref, v_ref, d_ref, m2_ref, v2_ref):
        delta, m2, v2 = _adamw_math(w_ref[...], g_ref[...], m_ref[...], v_ref[...])
        d_ref[...] = delta
        m2_ref[...] = m2
        v2_ref[...] = v2

    spec = pl.BlockSpec(w.shape, lambda i: (0, 0))
    out = jax.ShapeDtypeStruct(w.shape, F32)
    return _pcall(
        body, name="adamw_small", grid=(1,),
        in_specs=[spec] * 4, out_specs=[spec] * 3, out_shape=[out] * 3,
        compiler_params=_params(("arbitrary",)),
    )(w, g, m, v)


def _pack(arrs):
    flat = jnp.concatenate([a.reshape(-1) for a in arrs])
    n = flat.shape[0]
    rows = -(-n // 1024) * 8
    return jnp.pad(flat, (0, rows * 128 - n)).reshape(rows, 128)


def _unpack(packed, like):
    flat = packed.reshape(-1)
    out, off = [], 0
    for a in like:
        out.append(flat[off:off + a.size].reshape(a.shape))
        off += a.size
    return out


def kernel(x, mem, g_ffn1, w_ffn1_up, w_ffn1_down, g_mix, w_in, conv_w, sinks, g_mem, w_mem_kv, g_grp, w_out, g_ffn2, w_ffn2_up, w_ffn2_down, g_final, loss_target, m_g_ffn1, m_w_ffn1_up, m_w_ffn1_down, m_g_mix, m_w_in, m_conv_w, m_sinks, m_g_mem, m_w_mem_kv, m_g_grp, m_w_out, m_g_ffn2, m_w_ffn2_up, m_w_ffn2_down, m_g_final, v_g_ffn1, v_w_ffn1_up, v_w_ffn1_down, v_g_mix, v_w_in, v_conv_w, v_sinks, v_g_mem, v_w_mem_kv, v_g_grp, v_w_out, v_g_ffn2, v_w_ffn2_up, v_w_ffn2_down, v_g_final):
    depth = g_ffn1.shape[0]
    t, d = x.shape[1], x.shape[2]
    width = max(d, D_MIX)
    me = _slot(*_position())
    conv_shard = conv_w.shape[2]

    xin, memin, tgt = x[0], mem[0], loss_target[0]

    conv_tile = jnp.zeros((depth * 8, 128), F32).at[:, :conv_shard].set(
        jnp.pad(conv_w, ((0, 0), (0, 8 - conv_w.shape[1]), (0, 0))).reshape(depth * 8, conv_shard))
    tr = lambda a: jnp.swapaxes(a, -1, -2)
    bf = lambda a: a.astype(BF16)
    weights = []
    collective_id = 0
    for l in range(depth):
        groups = [[bf(tr(w_ffn1_up[l])), bf(w_ffn1_down[l])] + ([conv_tile] if l == 0 else []),
                  [bf(tr(w_in[l])), bf(w_mem_kv[l]), bf(w_out[l])],
                  [bf(tr(w_ffn2_up[l])), bf(w_ffn2_down[l])]]
        full = []
        for gi, shards in enumerate(groups):
            full.append(_all_gather(shards, f"all_gather_l{l}_g{gi}", collective_id))
            collective_id += 1
        if l == 0:
            conv_full = full[0][2].reshape(N_DEV, depth, 8, 128)[:, :, :3, :conv_shard]
            conv_full = conv_full.transpose(1, 2, 0, 3).reshape(depth, 3, N_DEV * conv_shard)
        weights.append(dict(
            up1=full[0][0].reshape(2, -1, d), dn1=full[0][1].reshape(-1, d),
            win=full[1][0].reshape(D_IN, d), wkv=full[1][1].reshape(d, 2 * D_MEMQ), wout=full[1][2].reshape(D_MIX, d),
            up2=full[2][0].reshape(2, -1, d), dn2=full[2][1].reshape(-1, d)))

    row = lambda a: a.reshape(1, -1)
    bias_tok, bias_key = _bias_tables()

    h = xin
    saved = []
    for l in range(depth):
        wl = weights[l]
        s = dict(h0=h)
        h, s["act1"], s["n1"] = _ffn_fwd(h, row(g_ffn1[l]), wl["up1"], wl["dn1"])
        s["h1"] = h
        s["p"], s["n_mix"], s["qh"] = _mix_proj_fwd(h, row(g_mix[l]), wl["win"])
        s["mkv"], s["nt_mem"] = _memkv_fwd(memin, row(g_mem[l]), wl["wkv"], s["p"])
        s["y"], s["lse"] = _mix_core_fwd(s["p"], s["qh"], s["mkv"], conv_full[l], row(sinks[l]), bias_tok)
        h, s["mt"] = _mix_out_fwd(s["y"], h, row(g_grp[l]), wl["wout"])
        s["h2"] = h
        h, s["act2"], s["n2"] = _ffn_fwd(h, row(g_ffn2[l]), wl["up2"], wl["dn2"])
        saved.append(s)

    dh, loss_part, dg_final = _final_loss(h, row(g_final), tgt)

    small = {}
    dep = loss_part

    started = []

    def scatter(names, partials, label):
        state, token = _scatter_start(partials, f"scatter_grads_{label}")
        started.append((names, state))
        return token

    for l in reversed(range(depth)):
        wl, s = weights[l], saved[l]
        dh, dgu, dyb, small["g_ffn2", l] = _ffn_bwd_act(dh, s["h2"], row(g_ffn2[l]), s["act2"], wl["up2"], wl["dn2"], dep)
        ddn2 = _ffn_bwd_w(s["act2"], 2, 1, dyb, dgu, f"ffn_bwd_w_down_l{l}_ffn2").reshape(N_DEV, -1, d)
        dup2 = _ffn_bwd_w(dgu, 0, 2, s["n2"], ddn2, f"ffn_bwd_w_up_l{l}_ffn2").reshape(N_DEV, -1, d)
        dep = scatter([("w_ffn2_up", l), ("w_ffn2_down", l)], [dup2, ddn2], f"l{l}_ffn2")
        dyconv, doh, delta, dwout, small["g_grp", l] = _mix_out_bwd(dh, s["y"], row(g_grp[l]), wl["wout"], s["mt"], dep)
        dp, dmkv, small["conv_w", l], small["sinks", l] = _mix_core_bwd(
            s["p"], s["qh"], dyconv, doh, delta, s["lse"], s["mkv"], conv_full[l], row(sinks[l]), bias_tok, bias_key)
        dwkv, small["g_mem", l] = _memkv_bwd(dmkv, memin, row(g_mem[l]), wl["wkv"], s["nt_mem"])
        dh, dwin, small["g_mix", l] = _mix_proj_bwd(dp, dh, s["h1"], row(g_mix[l]), wl["win"], s["n_mix"])
        dep = scatter([("w_in", l), ("w_mem_kv", l), ("w_out", l)],
                      [dwin.reshape(N_DEV, -1, d), dwkv.reshape(N_DEV, -1, 2 * D_MEMQ), dwout.reshape(N_DEV, -1, d)],
                      f"l{l}_mix")
        dh, dgu, dyb, small["g_ffn1", l] = _ffn_bwd_act(dh, s["h0"], row(g_ffn1[l]), s["act1"], wl["up1"], wl["dn1"], dep)
        ddn1 = _ffn_bwd_w(s["act1"], 2, 1, dyb, dgu, f"ffn_bwd_w_down_l{l}_ffn1").reshape(N_DEV, -1, d)
        if l > 0:
            dup1 = _ffn_bwd_w(dgu, 0, 2, s["n1"], ddn1, f"ffn_bwd_w_up_l{l}_ffn1").reshape(N_DEV, -1, d)
            dep = scatter([("w_ffn1_up", l), ("w_ffn1_down", l)], [dup1, ddn1], f"l{l}_ffn1")
        else:
            dep = scatter([("w_ffn1_down", l)], [ddn1], f"l{l}_ffn1_down")
            dup1 = _ffn_bwd_w(dgu, 0, 2, s["n1"], dep, f"ffn_bwd_w_up_l{l}_ffn1").reshape(N_DEV, -1, d)
            dep = scatter([("w_ffn1_up", l)], [dup1], f"l{l}_ffn1_up")
    grad_x = dh[None]

    def lanes(a):
        return jnp.pad(a, ((0, 0), (0, width - a.shape[1])))

    def first_row(a):
        return lanes(jnp.pad(a, ((0, 8 - a.shape[0]), (0, 0))))

    vec_names = ["g_ffn1", "g_mix", "g_mem", "g_grp", "g_ffn2", "sinks"]
    tiles = [lanes(small[n, l]) for n in vec_names for l in range(depth)]
    tiles += [first_row(small["conv_w", l][k:k + 1]) for l in range(depth) for k in range(3)]
    tiles.append(lanes(dg_final))
    n_real = len(tiles)
    tiles.append(lanes(loss_part))
    tiles.append(lanes(dep))
    tiles += [jnp.zeros((8, width), F32)] * (-len(tiles) % 8)
    summed = _all_reduce_rows(jnp.stack(tiles))
    loss = 0.5 * jnp.sum(summed[n_real]) / d

    def vec(n, wd):
        return jnp.stack([summed[vec_names.index(n) * depth + l, :wd] for l in range(depth)])

    conv_base = len(vec_names) * depth
    conv_grad = jnp.stack([jnp.stack([summed[conv_base + 3 * l + k, :D_CONV] for k in range(3)]) for l in range(depth)])
    grads_small = {
        "g_ffn1": vec("g_ffn1", d), "g_mix": vec("g_mix", d), "g_mem": vec("g_mem", d),
        "g_grp": vec("g_grp", D_MIX), "g_ffn2": vec("g_ffn2", d), "sinks": vec("sinks", N_SWA_HEADS),
        "conv_w": lax.dynamic_slice_in_dim(conv_grad, me * conv_shard, conv_shard, axis=2),
        "g_final": summed[n_real - 1, :d],
    }
    small_w = [("g_ffn1", g_ffn1, m_g_ffn1, v_g_ffn1), ("g_mix", g_mix, m_g_mix, v_g_mix),
               ("conv_w", conv_w, m_conv_w, v_conv_w), ("sinks", sinks, m_sinks, v_sinks),
               ("g_mem", g_mem, m_g_mem, v_g_mem), ("g_grp", g_grp, m_g_grp, v_g_grp),
               ("g_ffn2", g_ffn2, m_g_ffn2, v_g_ffn2), ("g_final", g_final, m_g_final, v_g_final)]
    like = [w for _, w, _, _ in small_w]
    packed = _adamw_small(_pack(like), _pack([grads_small[n] for n, _, _, _ in small_w]),
                          _pack([m for _, _, m, _ in small_w]), _pack([v for _, _, _, v in small_w]))
    small_out = {n: (grads_small[n], dl, m2, v2)
                 for (n, _, _, _), dl, m2, v2 in zip(small_w, *[_unpack(pk, like) for pk in packed])}

    big = {"w_ffn2_up": (w_ffn2_up, m_w_ffn2_up, v_w_ffn2_up, True), "w_ffn2_down": (w_ffn2_down, m_w_ffn2_down, v_w_ffn2_down, False),
           "w_in": (w_in, m_w_in, v_w_in, True), "w_mem_kv": (w_mem_kv, m_w_mem_kv, v_w_mem_kv, False),
           "w_out": (w_out, m_w_out, v_w_out, False), "w_ffn1_up": (w_ffn1_up, m_w_ffn1_up, v_w_ffn1_up, True),
           "w_ffn1_down": (w_ffn1_down, m_w_ffn1_down, v_w_ffn1_down, False)}
    me_index = jnp.reshape(me, (1,)).astype(jnp.int32)
    sharded, landed = {}, {}
    after = packed[0]
    for names, state in started:
        owns, lands = _scatter_wait(state, after)
        for key, own, land in zip(names, owns, lands):
            landed[key] = (own, land)
        after = lands[0]
        for name in dict.fromkeys(n for n, _ in names):
            if name not in sharded and all((name, l) in landed for l in range(depth)):
                w, m, v, transposed = big[name]
                fix = tr if transposed else (lambda a: a)
                res = _adamw_sharded(me_index, [landed[name, l] for l in range(depth)], fix(w), fix(m), fix(v))
                sharded[name] = tuple(fix(r) for r in res)
                after = res[0]

    order = ["g_ffn1", "w_ffn1_up", "w_ffn1_down", "g_mix", "w_in", "conv_w", "sinks", "g_mem", "w_mem_kv", "g_grp",
             "w_out", "g_ffn2", "w_ffn2_up", "w_ffn2_down", "g_final"]
    results = {**sharded, **small_out}
    outs = [loss, grad_x]
    for part in range(4):
        outs += [results[n][part] for n in order]
    return tuple(outs)
```

```python
import numpy as np
import jax
import jax.numpy as jnp
from jax import lax
from jax.experimental import pallas as pl
from jax.experimental.pallas import tpu as pltpu
from jax.experimental.pallas import tpu_sc as plsc

F32 = jnp.float32
BF16 = jnp.bfloat16

N_DEV = 8
EPS = 1e-6
N_SWA_HEADS = 8
N_SWA_KV = 2
SWA_GROUP = N_SWA_HEADS // N_SWA_KV
HEAD_DIM = 64
N_MEM_HEADS = 4
D_CONV = 256
BLOCK = 128
D_SWA = N_SWA_HEADS * HEAD_DIM
D_KV = N_SWA_KV * HEAD_DIM
D_MEMQ = N_MEM_HEADS * HEAD_DIM
D_MIX = D_CONV + D_SWA + D_MEMQ
D_IN = 3 * D_CONV + D_SWA + 2 * D_KV + D_MEMQ
COL_BG, COL_CG, COL_U = 0, D_CONV, 2 * D_CONV
COL_Q = 3 * D_CONV
COL_K = COL_Q + D_SWA
COL_V = COL_K + D_KV
COL_QM = COL_V + D_KV
MIX_GROUPS = ((0, D_CONV), (D_CONV, D_CONV + D_SWA), (D_CONV + D_SWA, D_MIX))
SLOPES = tuple(2.0 ** (-8.0 * (i + 1) / N_SWA_HEADS) for i in range(N_SWA_HEADS))
SCALE = HEAD_DIM ** -0.5
NEG = -1e30

ADAM_LR = 0.001
ADAM_B1 = 0.9
ADAM_B2 = 0.999
ADAM_EPS = 1e-08
ADAM_WD = 0.01
ADAM_STEP = 10

V7X_VMEM_BYTES = 64 * 1024 * 1024
VMEM_LIMIT = (V7X_VMEM_BYTES * 3) // 4
MESH = pl.DeviceIdType.MESH


def _pcall(body, **kw):
    return pl.pallas_call(body, **kw)


def _params(sem=None, vmem=VMEM_LIMIT):
    return pltpu.CompilerParams(dimension_semantics=sem, vmem_limit_bytes=vmem)


def _dot(a, b):
    return lax.dot_general(a, b, (((1,), (0,)), ((), ())), preferred_element_type=F32)


def _dot_nt(a, b):
    return lax.dot_general(a, b, (((1,), (1,)), ((), ())), preferred_element_type=F32)


def _dot_tn(a, b):
    return lax.dot_general(a, b, (((0,), (0,)), ((), ())), preferred_element_type=F32)


def _rstd(x):
    return lax.rsqrt(jnp.mean(x * x, axis=-1, keepdims=True) + EPS)


def _sigmoid(x):
    return 1.0 / (1.0 + jnp.exp(-x))


def _sum8(x):
    r, w = x.shape
    return jnp.sum(x.reshape(r // 8, 8, w), axis=0)


def _tok_block(t, rows=512):
    return min(rows, t)


def _feat_block(f):
    return f // (N_DEV // 2)


def _ffn_fwd(h, g, wup_t, wdn):
    t, d = h.shape
    f = wdn.shape[0]
    tm, tf = _tok_block(t), _feat_block(f)
    ni, nj = t // tm, f // tf

    def body(h_ref, g_ref, wup_ref, wdn_ref, ho_ref, gu_ref, n_ref, nt_ref, acc_ref):
        j = pl.program_id(1)

        @pl.when(j == 0)
        def _():
            hh = h_ref[...]
            n = hh * _rstd(hh) * g_ref[...]
            n_ref[...] = n.astype(BF16)
            nt_ref[...] = n.T.astype(BF16)
            acc_ref[...] = jnp.zeros_like(acc_ref)

        nt = nt_ref[...]
        gate = _dot(wup_ref[0], nt)
        up = _dot(wup_ref[1], nt)
        gu_ref[0] = gate.astype(BF16)
        gu_ref[1] = up.astype(BF16)
        a = gate * _sigmoid(gate) * up
        acc_ref[...] += _dot_tn(a.astype(BF16), wdn_ref[...])

        @pl.when(j == nj - 1)
        def _():
            ho_ref[...] = h_ref[...] + 0.5 * acc_ref[...]

    return _pcall(
        body, name="ffn_fwd", grid=(ni, nj),
        in_specs=[pl.BlockSpec((tm, d), lambda i, j: (i, 0)),
                  pl.BlockSpec((1, d), lambda i, j: (0, 0)),
                  pl.BlockSpec((2, tf, d), lambda i, j: (0, j, 0)),
                  pl.BlockSpec((tf, d), lambda i, j: (j, 0))],
        out_specs=[pl.BlockSpec((tm, d), lambda i, j: (i, 0)),
                   pl.BlockSpec((2, tf, tm), lambda i, j: (0, j, i)),
                   pl.BlockSpec((tm, d), lambda i, j: (i, 0))],
        out_shape=[jax.ShapeDtypeStruct((t, d), F32),
                   jax.ShapeDtypeStruct((2, f, t), BF16),
                   jax.ShapeDtypeStruct((t, d), BF16)],
        scratch_shapes=[pltpu.VMEM((d, tm), BF16), pltpu.VMEM((tm, d), F32)],
        compiler_params=_params(("parallel", "arbitrary")),
    )(h, g, wup_t, wdn)


def _ffn_bwd_act(dho, h, g, gu, wup_t, wdn, dep):
    t, d = h.shape
    f = wdn.shape[0]
    tm, tf = _tok_block(t), _feat_block(f)
    ni, nj = t // tm, f // tf

    def body(dho_ref, h_ref, g_ref, gu_ref, wup_ref, wdn_ref, dep_ref, dh_ref, agu_ref, dyb_ref, dg_ref, dyt_ref, acc_ref):
        i = pl.program_id(0)
        j = pl.program_id(1)

        @pl.when(j == 0)
        def _():
            dy0 = 0.5 * dho_ref[...]
            dyb_ref[...] = dy0.astype(BF16)
            dyt_ref[...] = dy0.T.astype(BF16)

        da = _dot(wdn_ref[...], dyt_ref[...])
        gate = gu_ref[0].astype(F32)
        up = gu_ref[1].astype(F32)
        sg = _sigmoid(gate)
        silu = gate * sg
        dgate = (da * up * (sg * (1.0 + gate * (1.0 - sg)))).astype(BF16)
        dup = (da * silu).astype(BF16)
        agu_ref[0] = dgate
        agu_ref[1] = dup
        agu_ref[2] = (silu * up).astype(BF16)
        dn = _dot_tn(dgate, wup_ref[0]) + _dot_tn(dup, wup_ref[1])

        @pl.when(j == 0)
        def _():
            acc_ref[...] = dn

        @pl.when(j > 0)
        def _():
            acc_ref[...] += dn

        @pl.when(j == nj - 1)
        def _():
            hh = h_ref[...]
            r = _rstd(hh)
            xhat = hh * r
            dnf = acc_ref[...]
            dxh = dnf * g_ref[...]
            dh_ref[...] = dho_ref[...] + r * (dxh - xhat * jnp.mean(dxh * xhat, axis=-1, keepdims=True))
            part = _sum8(dnf * xhat)

            @pl.when(i == 0)
            def _():
                dg_ref[...] = part

            @pl.when(i > 0)
            def _():
                dg_ref[...] += part

    return _pcall(
        body, name="ffn_bwd_act", grid=(ni, nj),
        in_specs=[pl.BlockSpec((tm, d), lambda i, j: (i, 0)),
                  pl.BlockSpec((tm, d), lambda i, j: (i, 0)),
                  pl.BlockSpec((1, d), lambda i, j: (0, 0)),
                  pl.BlockSpec((2, tf, tm), lambda i, j: (0, j, i)),
                  pl.BlockSpec((2, tf, d), lambda i, j: (0, j, 0)),
                  pl.BlockSpec((tf, d), lambda i, j: (j, 0)),
                  pl.BlockSpec(memory_space=pl.ANY)],
        out_specs=[pl.BlockSpec((tm, d), lambda i, j: (i, 0)),
                   pl.BlockSpec((3, tf, tm), lambda i, j: (0, j, i)),
                   pl.BlockSpec((tm, d), lambda i, j: (i, 0)),
                   pl.BlockSpec((8, d), lambda i, j: (0, 0))],
        out_shape=[jax.ShapeDtypeStruct((t, d), F32),
                   jax.ShapeDtypeStruct((3, f, t), BF16),
                   jax.ShapeDtypeStruct((t, d), BF16),
                   jax.ShapeDtypeStruct((8, d), F32)],
        scratch_shapes=[pltpu.VMEM((d, tm), BF16), pltpu.VMEM((tm, d), F32)],
        compiler_params=_params(("arbitrary", "arbitrary")),
    )(dho, h, g, gu, wup_t, wdn, dep)


def _ffn_bwd_w(agu, first, count, rhs, dep, name):
    _, f, t = agu.shape
    d = rhs.shape[1]
    tm, tf = _tok_block(t, 1024), _feat_block(f)
    ni, nj = t // tm, f // tf

    def body(lhs_ref, rhs_ref, dep_ref, dw_ref, acc_ref):
        i = pl.program_id(1)
        rb = rhs_ref[...]
        for k in range(count):
            part = _dot(lhs_ref[k], rb)

            @pl.when(i == 0)
            def _():
                acc_ref[k] = part

            @pl.when(i > 0)
            def _():
                acc_ref[k] += part

        @pl.when(i == ni - 1)
        def _():
            dw_ref[...] = acc_ref[...].astype(BF16)

    return _pcall(
        body, name=name, grid=(nj, ni),
        in_specs=[pl.BlockSpec((count, tf, tm), lambda j, i: (first // count, j, i)),
                  pl.BlockSpec((tm, d), lambda j, i: (i, 0)),
                  pl.BlockSpec(memory_space=pl.ANY)],
        out_specs=pl.BlockSpec((count, tf, d), lambda j, i: (0, j, 0)),
        out_shape=jax.ShapeDtypeStruct((count, f, d), BF16),
        scratch_shapes=[pltpu.VMEM((count, tf, d), F32)],
        compiler_params=_params(("parallel", "arbitrary")),
    )(agu, rhs, dep)


N_HEADS = N_SWA_HEADS + N_MEM_HEADS


def _q_col(hd):
    return COL_Q + HEAD_DIM * hd if hd < N_SWA_HEADS else COL_QM + HEAD_DIM * (hd - N_SWA_HEADS)


def _mix_proj_fwd(h, g, win_t):
    t, d = h.shape
    tm = _tok_block(t)

    def body(h_ref, g_ref, win_ref, p_ref, n_ref, qh_ref):
        hh = h_ref[...]
        n = (hh * _rstd(hh) * g_ref[...]).astype(BF16)
        n_ref[...] = n
        proj = _dot_nt(n, win_ref[...])
        p_ref[...] = proj.astype(BF16)
        for hd in range(N_HEADS):
            c0 = _q_col(hd)
            qh_ref[hd] = (proj[:, c0:c0 + HEAD_DIM] * SCALE).astype(BF16)

    return _pcall(
        body, name="mix_proj_fwd", grid=(t // tm,),
        in_specs=[pl.BlockSpec((tm, d), lambda i: (i, 0)),
                  pl.BlockSpec((1, d), lambda i: (0, 0)),
                  pl.BlockSpec((D_IN, d), lambda i: (0, 0))],
        out_specs=[pl.BlockSpec((tm, D_IN), lambda i: (i, 0)),
                   pl.BlockSpec((tm, d), lambda i: (i, 0)),
                   pl.BlockSpec((N_HEADS, tm, HEAD_DIM), lambda i: (0, i, 0))],
        out_shape=[jax.ShapeDtypeStruct((t, D_IN), BF16), jax.ShapeDtypeStruct((t, d), BF16),
                   jax.ShapeDtypeStruct((N_HEADS, t, HEAD_DIM), BF16)],
        compiler_params=_params(("parallel",)),
    )(h, g, win_t)


def _memkv_fwd(mem, g, wkv, dep):
    m, d = mem.shape

    def body(mem_ref, g_ref, w_ref, dep_ref, mkv_ref, nt_ref):
        mm = mem_ref[...]
        n = mm * _rstd(mm) * g_ref[...]
        nt_ref[...] = n.T.astype(BF16)
        mkv_ref[...] = _dot(n.astype(BF16), w_ref[...]).astype(BF16)

    return _pcall(
        body, name="memkv_fwd", grid=(1,),
        in_specs=[pl.BlockSpec((m, d), lambda i: (0, 0)),
                  pl.BlockSpec((1, d), lambda i: (0, 0)),
                  pl.BlockSpec((d, 2 * D_MEMQ), lambda i: (0, 0)),
                  pl.BlockSpec(memory_space=pl.ANY)],
        out_specs=[pl.BlockSpec((m, 2 * D_MEMQ), lambda i: (0, 0)),
                   pl.BlockSpec((d, m), lambda i: (0, 0))],
        out_shape=[jax.ShapeDtypeStruct((m, 2 * D_MEMQ), BF16), jax.ShapeDtypeStruct((d, m), BF16)],
        compiler_params=_params(("arbitrary",)),
    )(mem, g, wkv, dep)


def _memkv_bwd(dmkv, mem, g, wkv, nt):
    m, d = mem.shape

    def body(dmkv_ref, mem_ref, g_ref, w_ref, nt_ref, dw_ref, dg_ref):
        db = dmkv_ref[...].astype(BF16)
        dw_ref[...] = _dot(nt_ref[...], db).astype(BF16)
        dn = _dot_nt(db, w_ref[...])
        mm = mem_ref[...]
        dg_ref[...] = _sum8(dn * (mm * _rstd(mm)))

    return _pcall(
        body, name="memkv_bwd", grid=(1,),
        in_specs=[pl.BlockSpec((m, 2 * D_MEMQ), lambda i: (0, 0)),
                  pl.BlockSpec((m, d), lambda i: (0, 0)),
                  pl.BlockSpec((1, d), lambda i: (0, 0)),
                  pl.BlockSpec((d, 2 * D_MEMQ), lambda i: (0, 0)),
                  pl.BlockSpec((d, m), lambda i: (0, 0))],
        out_specs=[pl.BlockSpec((d, 2 * D_MEMQ), lambda i: (0, 0)),
                   pl.BlockSpec((8, d), lambda i: (0, 0))],
        out_shape=[jax.ShapeDtypeStruct((d, 2 * D_MEMQ), BF16), jax.ShapeDtypeStruct((8, d), F32)],
        compiler_params=_params(("arbitrary",)),
    )(dmkv, mem, g, wkv, nt)


def _shift_rows(v, k, edge_rows, row):
    out = pltpu.roll(v, k, 0)
    for r in range(k):
        out = jnp.where(row == r, edge_rows[r], out)
    return out


def _shift_rows_up(v, k, edge_rows, row):
    n = v.shape[0]
    out = pltpu.roll(v, n - k, 0)
    for r in range(k):
        out = jnp.where(row == n - k + r, edge_rows[r], out)
    return out


GROUP_ROWS = SWA_GROUP * BLOCK
BIAS_CUR, BIAS_PREV, BIAS_NONE = 0, 1, 2


def _bias_tables():
    tq = np.arange(BLOCK)[:, None]
    sk = np.arange(BLOCK)[None, :]
    slopes = np.asarray(SLOPES, np.float32)[:, None, None]
    cur = np.where(tq >= sk, -slopes * (tq - sk).astype(np.float32), NEG)
    prev = np.where(sk > tq, -slopes * (tq + BLOCK - sk).astype(np.float32), NEG)
    none = np.full_like(cur, NEG)
    tok = np.stack([cur, prev, none]).astype(np.float32).reshape(3, N_SWA_KV, GROUP_ROWS, BLOCK)
    return jnp.asarray(tok), jnp.asarray(np.ascontiguousarray(tok.transpose(0, 1, 3, 2)))


def _head_cols(hd):
    return D_CONV + HEAD_DIM * hd


def _stack_cols(ref, heads):
    return jnp.concatenate([ref[:, hd:hd + 1] for hd in heads], axis=0)


def _mix_core_fwd(p, qh, mkv, convw, sinks, bias_tok):
    t = p.shape[0]
    m = mkv.shape[0]
    nb = t // BLOCK

    def body(sk_ref, pc_ref, pkv_ref, ppc_ref, ppu_ref, qh_ref, mkv_ref, cw_ref, bc_ref, bp_ref, y_ref, l_ref):
        i = pl.program_id(0)
        prevf = (i > 0).astype(F32)
        row = lax.broadcasted_iota(jnp.int32, (BLOCK, D_CONV), 0)

        bg = pc_ref[:, COL_BG:COL_BG + D_CONV].astype(F32)
        cg = pc_ref[:, COL_CG:COL_CG + D_CONV].astype(F32)
        u = pc_ref[:, COL_U:COL_U + D_CONV].astype(F32)
        vv = cg * u
        pvv = ppc_ref[...].astype(F32) * ppu_ref[...].astype(F32) * prevf
        vv1 = _shift_rows(vv, 1, [pvv[15:16]], row)
        vv2 = _shift_rows(vv, 2, [pvv[14:15], pvv[15:16]], row)
        w = cw_ref[...]
        y_ref[:, 0:D_CONV] = bg * (w[0:1] * vv2 + w[1:2] * vv1 + w[2:3] * vv)

        lane = lax.broadcasted_iota(jnp.int32, (BLOCK, 128), 1)
        lse_all = jnp.zeros((BLOCK, 128), F32)
        for kv in range(N_SWA_KV):
            heads = range(kv * SWA_GROUP, (kv + 1) * SWA_GROUP)
            kc = pc_ref[:, COL_K + HEAD_DIM * kv:COL_K + HEAD_DIM * (kv + 1)]
            vc = pc_ref[:, COL_V + HEAD_DIM * kv:COL_V + HEAD_DIM * (kv + 1)]
            kp = pkv_ref[:, HEAD_DIM * kv:HEAD_DIM * (kv + 1)]
            vp = pkv_ref[:, D_KV + HEAD_DIM * kv:D_KV + HEAD_DIM * (kv + 1)]
            qg = qh_ref[kv * SWA_GROUP:(kv + 1) * SWA_GROUP].reshape(GROUP_ROWS, HEAD_DIM)
            sc = _dot_nt(qg, kc) + bc_ref[0, kv]
            sp = _dot_nt(qg, kp) + bp_ref[0, kv]
            sink = jnp.concatenate([jnp.full((BLOCK, 1), sk_ref[0, hd], F32) for hd in heads], axis=0)
            mx = jnp.maximum(jnp.max(jnp.maximum(sc, sp), axis=-1, keepdims=True), sink)
            ec = jnp.exp(sc - mx)
            ep = jnp.exp(sp - mx)
            den = jnp.sum(ec + ep, axis=-1, keepdims=True) + jnp.exp(sink - mx)
            o = (_dot(ec.astype(BF16), vc) + _dot(ep.astype(BF16), vp)) / den
            lse = mx + jnp.log(den)
            for gi, hd in enumerate(heads):
                rows = slice(gi * BLOCK, (gi + 1) * BLOCK)
                y_ref[:, _head_cols(hd):_head_cols(hd) + HEAD_DIM] = o[rows]
                lse_all = jnp.where(lane == hd, lse[rows], lse_all)

        for hm in range(N_MEM_HEADS):
            hd = N_SWA_HEADS + hm
            mk = mkv_ref[:, HEAD_DIM * hm:HEAD_DIM * (hm + 1)]
            mv = mkv_ref[:, D_MEMQ + HEAD_DIM * hm:D_MEMQ + HEAD_DIM * (hm + 1)]
            s = _dot_nt(qh_ref[hd], mk)
            mx = jnp.max(s, axis=-1, keepdims=True)
            e = jnp.exp(s - mx)
            den = jnp.sum(e, axis=-1, keepdims=True)
            y_ref[:, _head_cols(hd):_head_cols(hd) + HEAD_DIM] = _dot(e.astype(BF16), mv) / den
            lse_all = jnp.where(lane == hd, mx + jnp.log(den), lse_all)
        l_ref[...] = lse_all

    kv_col = COL_K // (2 * D_KV)
    bias_block = (1, N_SWA_KV, GROUP_ROWS, BLOCK)
    return _pcall(
        body, name="mix_core_fwd", grid=(nb,),
        in_specs=[pl.BlockSpec(memory_space=pltpu.SMEM),
                  pl.BlockSpec((BLOCK, D_IN), lambda i: (i, 0)),
                  pl.BlockSpec((BLOCK, 2 * D_KV), lambda i: (jnp.maximum(i - 1, 0), kv_col)),
                  pl.BlockSpec((16, D_CONV), lambda i: (jnp.maximum(i * (BLOCK // 16) - 1, 0), COL_CG // D_CONV)),
                  pl.BlockSpec((16, D_CONV), lambda i: (jnp.maximum(i * (BLOCK // 16) - 1, 0), COL_U // D_CONV)),
                  pl.BlockSpec((N_HEADS, BLOCK, HEAD_DIM), lambda i: (0, i, 0)),
                  pl.BlockSpec((m, 2 * D_MEMQ), lambda i: (0, 0)),
                  pl.BlockSpec((3, D_CONV), lambda i: (0, 0)),
                  pl.BlockSpec(bias_block, lambda i: (BIAS_CUR, 0, 0, 0)),
                  pl.BlockSpec(bias_block, lambda i: (jnp.where(i == 0, BIAS_NONE, BIAS_PREV), 0, 0, 0))],
        out_specs=[pl.BlockSpec((BLOCK, D_MIX), lambda i: (i, 0)),
                   pl.BlockSpec((BLOCK, 128), lambda i: (i, 0))],
        out_shape=[jax.ShapeDtypeStruct((t, D_MIX), F32), jax.ShapeDtypeStruct((t, 128), F32)],
        compiler_params=_params(("parallel",)),
    )(sinks, p, p, p, p, qh, mkv, convw, bias_tok, bias_tok)


def _mix_core_bwd(p, qh, dyconv, doh, delta, lse, mkv, convw, sinks, bias_tok, bias_key):
    t = p.shape[0]
    m = mkv.shape[0]
    nb = t // BLOCK

    def body(sk_ref, pc_ref, pkv_ref, ppc_ref, ppu_ref, pnb_ref, dyc_ref, dyn_ref, qc_ref, qn_ref, doc_ref, don_ref,
             dlc_ref, dln_ref, lc_ref, ln_ref, mkv_ref, cw_ref, bp_ref, bct_ref, bnt_ref,
             dp_ref, dmkv_ref, dcw_ref, dsk_ref):
        i = pl.program_id(0)
        prevf = (i > 0).astype(F32)
        nextf = (i < nb - 1).astype(F32)
        row = lax.broadcasted_iota(jnp.int32, (BLOCK, D_CONV), 0)

        @pl.when(i == 0)
        def _():
            dmkv_ref[...] = jnp.zeros_like(dmkv_ref)
            dcw_ref[...] = jnp.zeros_like(dcw_ref)
            dsk_ref[...] = jnp.zeros_like(dsk_ref)

        bg = pc_ref[:, COL_BG:COL_BG + D_CONV].astype(F32)
        cg = pc_ref[:, COL_CG:COL_CG + D_CONV].astype(F32)
        u = pc_ref[:, COL_U:COL_U + D_CONV].astype(F32)
        vv = cg * u
        pvv = ppc_ref[...].astype(F32) * ppu_ref[...].astype(F32) * prevf
        vv1 = _shift_rows(vv, 1, [pvv[15:16]], row)
        vv2 = _shift_rows(vv, 2, [pvv[14:15], pvv[15:16]], row)
        w = cw_ref[...]
        yconv = w[0:1] * vv2 + w[1:2] * vv1 + w[2:3] * vv
        dyo = dyc_ref[...]
        dyc = dyo * bg
        nxt = dyn_ref[...] * pnb_ref[...].astype(F32) * nextf
        d1 = _shift_rows_up(dyc, 1, [nxt[0:1]], row)
        d2 = _shift_rows_up(dyc, 2, [nxt[0:1], nxt[1:2]], row)
        dvv = w[2:3] * dyc + w[1:2] * d1 + w[0:1] * d2
        dp_ref[:, COL_BG:COL_BG + D_CONV] = (dyo * yconv).astype(BF16)
        dp_ref[:, COL_CG:COL_CG + D_CONV] = (dvv * u).astype(BF16)
        dp_ref[:, COL_U:COL_U + D_CONV] = (dvv * cg).astype(BF16)
        dcw_ref[0:1, :] += jnp.sum(dyc * vv2, axis=0, keepdims=True)
        dcw_ref[1:2, :] += jnp.sum(dyc * vv1, axis=0, keepdims=True)
        dcw_ref[2:3, :] += jnp.sum(dyc * vv, axis=0, keepdims=True)

        lse_t, dl_t = lc_ref[...].T, dlc_ref[...].T
        lse_nt, dl_nt = ln_ref[...].T, dln_ref[...].T

        def stack_rows(tile_t, heads):
            return jnp.concatenate([tile_t[hd:hd + 1, :] for hd in heads], axis=1)

        lane8 = jnp.where(lax.broadcasted_iota(jnp.int32, (8, 128), 0) == 0,
                          lax.broadcasted_iota(jnp.int32, (8, 128), 1), -1)
        dsk = jnp.zeros((8, 128), F32)
        for kv in range(N_SWA_KV):
            heads = range(kv * SWA_GROUP, (kv + 1) * SWA_GROUP)
            kc = pc_ref[:, COL_K + HEAD_DIM * kv:COL_K + HEAD_DIM * (kv + 1)]
            vc = pc_ref[:, COL_V + HEAD_DIM * kv:COL_V + HEAD_DIM * (kv + 1)]
            kp = pkv_ref[:, HEAD_DIM * kv:HEAD_DIM * (kv + 1)]
            vp = pkv_ref[:, D_KV + HEAD_DIM * kv:D_KV + HEAD_DIM * (kv + 1)]
            qg = qc_ref[kv * SWA_GROUP:(kv + 1) * SWA_GROUP].reshape(GROUP_ROWS, HEAD_DIM)
            dog = doc_ref[kv * SWA_GROUP:(kv + 1) * SWA_GROUP].reshape(GROUP_ROWS, HEAD_DIM)
            qn = qn_ref[kv * SWA_GROUP:(kv + 1) * SWA_GROUP].reshape(GROUP_ROWS, HEAD_DIM)
            don = don_ref[kv * SWA_GROUP:(kv + 1) * SWA_GROUP].reshape(GROUP_ROWS, HEAD_DIM)
            lse_col, dl_col = _stack_cols(lc_ref, heads), _stack_cols(dlc_ref, heads)
            pp_ = jnp.exp(_dot_nt(qg, kp) + bp_ref[0, kv] - lse_col)
            dsp = (pp_ * (_dot_nt(dog, vp) - dl_col)).astype(BF16)
            dq = _dot(dsp, kp)
            pt = jnp.exp(_dot_nt(kc, qg) + bct_ref[0, kv] - stack_rows(lse_t, heads))
            dst = (pt * (_dot_nt(vc, dog) - stack_rows(dl_t, heads))).astype(BF16)
            dv = _dot(pt.astype(BF16), dog)
            dk = _dot(dst, qg)
            dq = dq + _dot_tn(dst, kc)
            ptn = jnp.exp(_dot_nt(kc, qn) + bnt_ref[0, kv] - stack_rows(lse_nt, heads))
            dstn = (ptn * (_dot_nt(vc, don) - stack_rows(dl_nt, heads))).astype(BF16)
            dv = dv + _dot(ptn.astype(BF16), don)
            dk = dk + _dot(dstn, qn)
            dp_ref[:, COL_K + HEAD_DIM * kv:COL_K + HEAD_DIM * (kv + 1)] = dk.astype(BF16)
            dp_ref[:, COL_V + HEAD_DIM * kv:COL_V + HEAD_DIM * (kv + 1)] = dv.astype(BF16)
            sink = jnp.concatenate([jnp.full((BLOCK, 1), sk_ref[0, hd], F32) for hd in heads], axis=0)
            sink_term = jnp.exp(sink - lse_col) * dl_col
            for gi, hd in enumerate(heads):
                rows = slice(gi * BLOCK, (gi + 1) * BLOCK)
                dp_ref[:, _q_col(hd):_q_col(hd) + HEAD_DIM] = (dq[rows] * SCALE).astype(BF16)
                dsk = dsk + jnp.where(lane8 == hd, -jnp.sum(sink_term[rows], axis=0, keepdims=True), 0.0)
        dsk_ref[...] += dsk

        for hm in range(N_MEM_HEADS):
            hd = N_SWA_HEADS + hm
            qm, dom = qc_ref[hd], doc_ref[hd]
            mk = mkv_ref[:, HEAD_DIM * hm:HEAD_DIM * (hm + 1)]
            mv = mkv_ref[:, D_MEMQ + HEAD_DIM * hm:D_MEMQ + HEAD_DIM * (hm + 1)]
            pt = jnp.exp(_dot_nt(mk, qm) - lse_t[hd:hd + 1, :])
            dst = (pt * (_dot_nt(mv, dom) - dl_t[hd:hd + 1, :])).astype(BF16)
            dp_ref[:, _q_col(hd):_q_col(hd) + HEAD_DIM] = (_dot_tn(dst, mk) * SCALE).astype(BF16)
            dmkv_ref[:, HEAD_DIM * hm:HEAD_DIM * (hm + 1)] += _dot(dst, qm)
            dmkv_ref[:, D_MEMQ + HEAD_DIM * hm:D_MEMQ + HEAD_DIM * (hm + 1)] += _dot(pt.astype(BF16), dom)

    cur = lambda i: (i, 0)
    const = lambda i: (0, 0)
    rows16 = BLOCK // 16
    last16 = t // 16 - 1
    before = lambda col: (lambda i: (jnp.maximum(i * rows16 - 1, 0), col))
    after = lambda i: (jnp.minimum((i + 1) * rows16, last16), 0)
    heads_cur = lambda i: (0, i, 0)
    heads_next = lambda i: (0, jnp.minimum(i + 1, nb - 1), 0)
    stat_next = lambda i: (jnp.minimum(i + 1, nb - 1), 0)
    tok_block = (1, N_SWA_KV, GROUP_ROWS, BLOCK)
    key_block = (1, N_SWA_KV, BLOCK, GROUP_ROWS)
    head_block = (N_HEADS, BLOCK, HEAD_DIM)
    return _pcall(
        body, name="mix_core_bwd", grid=(nb,),
        in_specs=[pl.BlockSpec(memory_space=pltpu.SMEM),
                  pl.BlockSpec((BLOCK, D_IN), cur),
                  pl.BlockSpec((BLOCK, 2 * D_KV), lambda i: (jnp.maximum(i - 1, 0), COL_K // (2 * D_KV))),
                  pl.BlockSpec((16, D_CONV), before(COL_CG // D_CONV)),
                  pl.BlockSpec((16, D_CONV), before(COL_U // D_CONV)),
                  pl.BlockSpec((16, D_CONV), after),
                  pl.BlockSpec((BLOCK, D_CONV), cur),
                  pl.BlockSpec((16, D_CONV), after),
                  pl.BlockSpec(head_block, heads_cur), pl.BlockSpec(head_block, heads_next),
                  pl.BlockSpec(head_block, heads_cur), pl.BlockSpec(head_block, heads_next),
                  pl.BlockSpec((BLOCK, 128), cur), pl.BlockSpec((BLOCK, 128), stat_next),
                  pl.BlockSpec((BLOCK, 128), cur), pl.BlockSpec((BLOCK, 128), stat_next),
                  pl.BlockSpec((m, 2 * D_MEMQ), const),
                  pl.BlockSpec((3, D_CONV), const),
                  pl.BlockSpec(tok_block, lambda i: (jnp.where(i == 0, BIAS_NONE, BIAS_PREV), 0, 0, 0)),
                  pl.BlockSpec(key_block, lambda i: (BIAS_CUR, 0, 0, 0)),
                  pl.BlockSpec(key_block, lambda i: (jnp.where(i == nb - 1, BIAS_NONE, BIAS_PREV), 0, 0, 0))],
        out_specs=[pl.BlockSpec((BLOCK, D_IN), cur),
                   pl.BlockSpec((m, 2 * D_MEMQ), const),
                   pl.BlockSpec((8, D_CONV), const),
                   pl.BlockSpec((8, 128), const)],
        out_shape=[jax.ShapeDtypeStruct((t, D_IN), BF16),
                   jax.ShapeDtypeStruct((m, 2 * D_MEMQ), F32),
                   jax.ShapeDtypeStruct((8, D_CONV), F32),
                   jax.ShapeDtypeStruct((8, 128), F32)],
        compiler_params=_params(("arbitrary",)),
    )(sinks, p, p, p, p, p, dyconv, dyconv, qh, qh, doh, doh, delta, delta, lse, lse, mkv, convw,
      bias_tok, bias_key, bias_key)


def _group_norms(y):
    out = []
    for a, b in MIX_GROUPS:
        ys = y[:, a:b]
        r = _rstd(ys)
        out.append((ys * r, r))
    return out


def _mix_out_fwd(y, h, g, wout):
    t, d = h.shape
    tm = _tok_block(t)

    def body(y_ref, h_ref, g_ref, w_ref, ho_ref, mt_ref):
        yhat = jnp.concatenate([yh for yh, _ in _group_norms(y_ref[...])], axis=-1)
        mixed = yhat * g_ref[...]
        mt_ref[...] = mixed.T.astype(BF16)
        ho_ref[...] = h_ref[...] + _dot(mixed.astype(BF16), w_ref[...])

    return _pcall(
        body, name="mix_out_fwd", grid=(t // tm,),
        in_specs=[pl.BlockSpec((tm, D_MIX), lambda i: (i, 0)),
                  pl.BlockSpec((tm, d), lambda i: (i, 0)),
                  pl.BlockSpec((1, D_MIX), lambda i: (0, 0)),
                  pl.BlockSpec((D_MIX, d), lambda i: (0, 0))],
        out_specs=[pl.BlockSpec((tm, d), lambda i: (i, 0)),
                   pl.BlockSpec((D_MIX, tm), lambda i: (0, i))],
        out_shape=[jax.ShapeDtypeStruct((t, d), F32), jax.ShapeDtypeStruct((D_MIX, t), BF16)],
        compiler_params=_params(("parallel",)),
    )(y, h, g, wout)


def _head_indicator():
    ind = np.zeros((D_MIX, 128), np.float32)
    for hd in range(N_HEADS):
        ind[_head_cols(hd):_head_cols(hd) + HEAD_DIM, hd] = 1.0
    return jnp.asarray(ind, BF16)


def _mix_out_bwd(dho, y, g, wout, mt, dep):
    t, d = dho.shape
    tm = _tok_block(t)
    ni = t // tm

    def body(dho_ref, y_ref, g_ref, w_ref, mt_ref, ind_ref, dep_ref, dyc_ref, doh_ref, dl_ref, dw_ref, dg_ref, acc_ref):
        i = pl.program_id(0)
        dhb = dho_ref[...].astype(BF16)
        dm = _dot_nt(dhb, w_ref[...])
        pw = _dot(mt_ref[...], dhb)
        gg = g_ref[...]
        yy = y_ref[...]
        dys = []
        dgs = []
        for (a, b), (yhat, r) in zip(MIX_GROUPS, _group_norms(yy)):
            dmg = dm[:, a:b]
            dgs.append(_sum8(dmg * yhat))
            dyh = dmg * gg[:, a:b]
            dys.append(r * (dyh - yhat * jnp.mean(dyh * yhat, axis=-1, keepdims=True)))
        dy = jnp.concatenate(dys, axis=-1)
        dyc_ref[...] = dy[:, 0:D_CONV]
        for hd in range(N_HEADS):
            doh_ref[hd] = dy[:, _head_cols(hd):_head_cols(hd) + HEAD_DIM].astype(BF16)
        prod = dy * yy
        hi = prod.astype(BF16)
        lo = (prod - hi.astype(F32)).astype(BF16)
        dl_ref[...] = _dot(hi, ind_ref[...]) + _dot(lo, ind_ref[...])
        part = jnp.concatenate(dgs, axis=-1)

        @pl.when(i == 0)
        def _():
            acc_ref[...] = pw
            dg_ref[...] = part

        @pl.when(i > 0)
        def _():
            acc_ref[...] += pw
            dg_ref[...] += part

        @pl.when(i == ni - 1)
        def _():
            dw_ref[...] = acc_ref[...].astype(BF16)

    return _pcall(
        body, name="mix_out_bwd", grid=(ni,),
        in_specs=[pl.BlockSpec((tm, d), lambda i: (i, 0)),
                  pl.BlockSpec((tm, D_MIX), lambda i: (i, 0)),
                  pl.BlockSpec((1, D_MIX), lambda i: (0, 0)),
                  pl.BlockSpec((D_MIX, d), lambda i: (0, 0)),
                  pl.BlockSpec((D_MIX, tm), lambda i: (0, i)),
                  pl.BlockSpec((D_MIX, 128), lambda i: (0, 0)),
                  pl.BlockSpec(memory_space=pl.ANY)],
        out_specs=[pl.BlockSpec((tm, D_CONV), lambda i: (i, 0)),
                   pl.BlockSpec((N_HEADS, tm, HEAD_DIM), lambda i: (0, i, 0)),
                   pl.BlockSpec((tm, 128), lambda i: (i, 0)),
                   pl.BlockSpec((D_MIX, d), lambda i: (0, 0)),
                   pl.BlockSpec((8, D_MIX), lambda i: (0, 0))],
        out_shape=[jax.ShapeDtypeStruct((t, D_CONV), F32),
                   jax.ShapeDtypeStruct((N_HEADS, t, HEAD_DIM), BF16),
                   jax.ShapeDtypeStruct((t, 128), F32),
                   jax.ShapeDtypeStruct((D_MIX, d), BF16),
                   jax.ShapeDtypeStruct((8, D_MIX), F32)],
        scratch_shapes=[pltpu.VMEM((D_MIX, d), F32)],
        compiler_params=_params(("arbitrary",)),
    )(dho, y, g, wout, mt, _head_indicator(), dep)


def _mix_proj_bwd(dp, dho, h, g, win_t, n):
    t, d = h.shape
    tm = _tok_block(t)
    ni = t // tm

    def body(dp_ref, dho_ref, h_ref, g_ref, w_ref, n_ref, dh_ref, dw_ref, dg_ref, acc_ref):
        i = pl.program_id(0)
        dpb = dp_ref[...]
        dn = _dot(dpb, w_ref[...])
        pw = _dot_tn(dpb, n_ref[...])
        hh = h_ref[...]
        r = _rstd(hh)
        xhat = hh * r
        dxh = dn * g_ref[...]
        dh_ref[...] = dho_ref[...] + r * (dxh - xhat * jnp.mean(dxh * xhat, axis=-1, keepdims=True))
        part = _sum8(dn * xhat)

        @pl.when(i == 0)
        def _():
            acc_ref[...] = pw
            dg_ref[...] = part

        @pl.when(i > 0)
        def _():
            acc_ref[...] += pw
            dg_ref[...] += part

        @pl.when(i == ni - 1)
        def _():
            dw_ref[...] = acc_ref[...].astype(BF16)

    return _pcall(
        body, name="mix_proj_bwd", grid=(ni,),
        in_specs=[pl.BlockSpec((tm, D_IN), lambda i: (i, 0)),
                  pl.BlockSpec((tm, d), lambda i: (i, 0)),
                  pl.BlockSpec((tm, d), lambda i: (i, 0)),
                  pl.BlockSpec((1, d), lambda i: (0, 0)),
                  pl.BlockSpec((D_IN, d), lambda i: (0, 0)),
                  pl.BlockSpec((tm, d), lambda i: (i, 0))],
        out_specs=[pl.BlockSpec((tm, d), lambda i: (i, 0)),
                   pl.BlockSpec((D_IN, d), lambda i: (0, 0)),
                   pl.BlockSpec((8, d), lambda i: (0, 0))],
        out_shape=[jax.ShapeDtypeStruct((t, d), F32),
                   jax.ShapeDtypeStruct((D_IN, d), BF16),
                   jax.ShapeDtypeStruct((8, d), F32)],
        scratch_shapes=[pltpu.VMEM((D_IN, d), F32)],
        compiler_params=_params(("arbitrary",)),
    )(dp, dho, h, g, win_t, n)


def _final_loss(h, g, tgt):
    t, d = h.shape
    tm = _tok_block(t)

    def body(h_ref, g_ref, t_ref, dh_ref, ls_ref, dg_ref):
        i = pl.program_id(0)
        hh = h_ref[...]
        r = _rstd(hh)
        xhat = hh * r
        gg = g_ref[...]
        err = xhat * gg - t_ref[...]
        dy = err * (1.0 / d)
        dxh = dy * gg
        dh_ref[...] = r * (dxh - xhat * jnp.mean(dxh * xhat, axis=-1, keepdims=True))
        lpart = _sum8(err * err)
        gpart = _sum8(dy * xhat)

        @pl.when(i == 0)
        def _():
            ls_ref[...] = lpart
            dg_ref[...] = gpart

        @pl.when(i > 0)
        def _():
            ls_ref[...] += lpart
            dg_ref[...] += gpart

    return _pcall(
        body, name="final_loss", grid=(t // tm,),
        in_specs=[pl.BlockSpec((tm, d), lambda i: (i, 0)),
                  pl.BlockSpec((1, d), lambda i: (0, 0)),
                  pl.BlockSpec((tm, d), lambda i: (i, 0))],
        out_specs=[pl.BlockSpec((tm, d), lambda i: (i, 0)),
                   pl.BlockSpec((8, d), lambda i: (0, 0)),
                   pl.BlockSpec((8, d), lambda i: (0, 0))],
        out_shape=[jax.ShapeDtypeStruct((t, d), F32),
                   jax.ShapeDtypeStruct((8, d), F32),
                   jax.ShapeDtypeStruct((8, d), F32)],
        compiler_params=_params(("arbitrary",)),
    )(h, g, tgt)


def _position():
    return lax.axis_index("x"), lax.axis_index("y"), lax.axis_index("c")


def _flip(v, bit):
    return 1 - v if bit else v


def _peer(k):
    x, y, c = _position()
    return _flip(x, k & 4), _flip(y, k & 2), _flip(c, k & 1)


def _slot(px, py, pc):
    return 4 * px + 2 * py + pc


def _handshake(peers):
    barrier = pltpu.get_barrier_semaphore()
    for peer in peers:
        pl.semaphore_signal(barrier, inc=1, device_id=peer, device_id_type=MESH)
    pl.semaphore_wait(barrier, len(peers))


def _sequencer_call(body, name, collective_id, out_type, scratch_types, operands):
    return pl.kernel(
        body, out_type=out_type, mesh=plsc.ScalarSubcoreMesh(axis_name="sequencer", num_cores=1), name=name,
        scratch_types=scratch_types, compiler_params=pltpu.CompilerParams(collective_id=collective_id),
    )(*operands)


def _all_gather(shards, name, collective_id):
    nt = len(shards)

    def body(*refs):
        xs = refs[:nt]
        outs = refs[nt:2 * nt]
        send_sems, recv_sems, local_sems = refs[2 * nt:]
        x, y, c = _position()
        me, sibling = (x, y, c), (x, y, 1 - c)
        chips = [(1 - x, y), (x, 1 - y), (1 - x, 1 - y)]
        _handshake([sibling] + [(*chip, c) for chip in chips])

        def copy(t, k, block, to, src=None):
            dst = outs[t].at[_slot(*block)]
            return pltpu.make_async_remote_copy(
                src_ref=dst if src is None else src, dst_ref=dst,
                send_sem=send_sems.at[t, k], recv_sem=recv_sems.at[t, k],
                device_id=to, device_id_type=MESH)

        mine = [pltpu.make_async_copy(xs[t], outs[t].at[_slot(*me)], local_sems.at[t]) for t in range(nt)]
        for cp in mine:
            cp.start()
        first = []
        for t in range(nt):
            first.append(copy(t, 0, me, sibling, src=xs[t]))
            first += [copy(t, 1 + j, me, (*chip, c), src=xs[t]) for j, chip in enumerate(chips)]
        for cp in first:
            cp.start()
        passed = []
        for j, chip in enumerate(chips):
            for t in range(nt):
                copy(t, 1 + j, (*chip, c), me).wait_recv()
                fwd = copy(t, 4 + j, (*chip, c), sibling)
                fwd.start()
                passed.append(fwd)
        for t in range(nt):
            copy(t, 0, sibling, me).wait_recv()
            for j, chip in enumerate(chips):
                copy(t, 4 + j, (*chip, 1 - c), me).wait_recv()
        for cp in first + passed:
            cp.wait_send()
        for cp in mine:
            cp.wait()

    return _sequencer_call(
        body, name, collective_id,
        out_type=[jax.ShapeDtypeStruct((N_DEV,) + s.shape, s.dtype) for s in shards],
        scratch_types=[pltpu.SemaphoreType.DMA((nt, 7)), pltpu.SemaphoreType.DMA((nt, 7)),
                       pltpu.SemaphoreType.DMA((nt,))],
        operands=shards)


def _scatter_copy(srcs, lands, send_sems, recv_sems, t, k):
    peer = _peer(k)
    return pltpu.make_async_remote_copy(
        src_ref=srcs[t].at[_slot(*peer)], dst_ref=lands[t].at[k],
        send_sem=send_sems.at[t * (N_DEV - 1) + k - 1], recv_sem=recv_sems.at[t * (N_DEV - 1) + k - 1],
        device_id=peer, device_id_type=MESH)


def _scatter_start(partials, name):
    nt = len(partials)

    def body(*refs):
        srcs, lands = refs[:nt], refs[nt:2 * nt]
        send_sems, recv_sems = refs[2 * nt], refs[2 * nt + 1]
        token = refs[-1]
        for k in range(1, N_DEV):
            for t in range(nt):
                _scatter_copy(srcs, lands, send_sems, recv_sems, t, k).start()
        token[...] = jnp.zeros_like(token)

    hbm = pl.BlockSpec(memory_space=pltpu.HBM)
    sem = pl.BlockSpec(memory_space=pltpu.SEMAPHORE)
    shapes = [pltpu.HBM(p.shape, p.dtype) for p in partials]
    lands = [pltpu.with_memory_space_constraint(lax.empty(p.shape, p.dtype), pltpu.HBM) for p in partials]
    srcs = [pltpu.with_memory_space_constraint(p, pltpu.HBM) for p in partials]
    out = _pcall(
        body, name=name,
        out_shape=[pltpu.SemaphoreType.DMA((nt * (N_DEV - 1),))] * 2 + shapes + shapes
        + [jax.ShapeDtypeStruct((8, 128), F32)],
        in_specs=[hbm] * (2 * nt),
        out_specs=[sem, sem] + [hbm] * (2 * nt) + [pl.BlockSpec(memory_space=pltpu.VMEM)],
        input_output_aliases={i: 2 + i for i in range(2 * nt)},
        compiler_params=pltpu.CompilerParams(has_side_effects=pltpu.SideEffectType.DATAFLOW_SIDE_EFFECTING),
    )(*srcs, *lands)
    return (nt, name, out[:-1]), out[-1]


def _scatter_wait(state, after):
    nt, name, (send_sems, recv_sems, *thru) = state

    def body(*refs):
        srcs, lands = refs[:nt], refs[nt:2 * nt]
        send_sems, recv_sems = refs[2 * nt], refs[2 * nt + 1]
        for k in range(1, N_DEV):
            for t in range(nt):
                copy = _scatter_copy(srcs, lands, send_sems, recv_sems, t, k)
                copy.wait_send()
                copy.wait_recv()

    hbm = pl.BlockSpec(memory_space=pltpu.HBM)
    sem = pl.BlockSpec(memory_space=pltpu.SEMAPHORE)
    out = _pcall(
        body, name=name + "_wait",
        out_shape=[pltpu.HBM(a.shape, a.dtype) for a in thru],
        in_specs=[hbm] * (2 * nt) + [sem, sem, pl.BlockSpec(memory_space=pl.ANY)],
        out_specs=[hbm] * (2 * nt),
        input_output_aliases={i: i for i in range(2 * nt)},
        compiler_params=pltpu.CompilerParams(has_side_effects=pltpu.SideEffectType.DATAFLOW_SIDE_EFFECTING),
    )(*thru, send_sems, recv_sems, after)
    return out[:nt], out[nt:]


def _all_reduce_rows(v):
    nv, _, w = v.shape

    def body(v_ref, out_ref, gath_ref, send_sems, recv_sems):
        x, y, c = _position()
        me = _slot(x, y, c)

        def copy(k):
            return pltpu.make_async_remote_copy(
                src_ref=v_ref, dst_ref=gath_ref.at[me],
                send_sem=send_sems.at[k - 1], recv_sem=recv_sems.at[k - 1],
                device_id=_peer(k), device_id_type=MESH)

        def arrival(k):
            return pltpu.make_async_remote_copy(
                src_ref=v_ref, dst_ref=gath_ref.at[_slot(*_peer(k))],
                send_sem=send_sems.at[k - 1], recv_sem=recv_sems.at[k - 1],
                device_id=_peer(k), device_id_type=MESH)

        sent = [copy(k) for k in range(1, N_DEV)]
        for cp in sent:
            cp.start()
        gath_ref[me] = v_ref[...]
        for k in range(1, N_DEV):
            arrival(k).wait_recv()
        for cp in sent:
            cp.wait_send()
        total = gath_ref[0]
        for s in range(1, N_DEV):
            total = total + gath_ref[s]
        out_ref[...] = jnp.sum(total, axis=1)

    vmem = pl.BlockSpec(memory_space=pltpu.VMEM)
    return _pcall(
        body, name="all_reduce_rows",
        in_specs=[vmem], out_specs=vmem,
        out_shape=jax.ShapeDtypeStruct((nv, w), F32),
        scratch_shapes=[pltpu.VMEM((N_DEV, nv, 8, w), F32),
                        pltpu.SemaphoreType.DMA((7,)), pltpu.SemaphoreType.DMA((7,))],
    )(v)


def _adamw_math(w, g, m, v):
    m2 = ADAM_B1 * m + (1.0 - ADAM_B1) * g
    v2 = ADAM_B2 * v + (1.0 - ADAM_B2) * (g * g)
    m_hat = m2 / (1.0 - ADAM_B1 ** ADAM_STEP)
    v_hat = v2 / (1.0 - ADAM_B2 ** ADAM_STEP)
    delta = -ADAM_LR * (m_hat / (jnp.sqrt(v_hat) + ADAM_EPS) + ADAM_WD * w)
    return delta, m2, v2


def _row_block(r):
    for cand in (256, 176, 128):
        if r % cand == 0:
            return cand
    return r


def _adamw_sharded(me, grads, w, m, v, dep):
    (own0, land0), (own1, land1) = grads
    _, r, c = land0.shape
    tr = _row_block(r)
    nr = r // tr

    def body(me_ref, o0_ref, l0_ref, o1_ref, l1_ref, w_ref, m_ref, v_ref, dep_ref, g_ref, d_ref, m2_ref, v2_ref):
        layer = pl.program_id(0)

        def total(own_ref, land_ref):
            acc = own_ref[0].astype(F32)
            for k in range(1, N_DEV):
                acc = acc + land_ref[k].astype(F32)
            return acc

        g = jnp.where(layer == 0, total(o0_ref, l0_ref), total(o1_ref, l1_ref))
        delta, m2, v2 = _adamw_math(w_ref[0], g, m_ref[0], v_ref[0])
        g_ref[0] = g
        d_ref[0] = delta
        m2_ref[0] = m2
        v2_ref[0] = v2

    rows0 = lambda l, i: jnp.where(l == 0, i, nr - 1)
    rows1 = lambda l, i: jnp.where(l == 1, i, 0)
    shard = pl.BlockSpec((1, tr, c), lambda l, i, me_ref: (l, i, 0))
    out = jax.ShapeDtypeStruct((2, r, c), F32)
    return _pcall(
        body, name="adamw_sharded",
        grid_spec=pltpu.PrefetchScalarGridSpec(
            num_scalar_prefetch=1, grid=(2, nr),
            in_specs=[pl.BlockSpec((1, tr, c), lambda l, i, me_ref: (me_ref[0], rows0(l, i), 0)),
                      pl.BlockSpec((N_DEV, tr, c), lambda l, i, me_ref: (0, rows0(l, i), 0)),
                      pl.BlockSpec((1, tr, c), lambda l, i, me_ref: (me_ref[0], rows1(l, i), 0)),
                      pl.BlockSpec((N_DEV, tr, c), lambda l, i, me_ref: (0, rows1(l, i), 0)),
                      shard, shard, shard, pl.BlockSpec(memory_space=pl.ANY)],
            out_specs=[shard, shard, shard, shard]),
        out_shape=[out, out, out, out],
        compiler_params=_params(("arbitrary", "arbitrary")),
    )(me, own0, land0, own1, land1, w, m, v, dep)


def _adamw_small(w, g, m, v):
    def body(w_ref, g_ref, m_ref, v_ref, d_ref, m2_ref, v2_ref):
        delta, m2, v2 = _adamw_math(w_ref[...], g_ref[...], m_ref[...], v_ref[...])
        d_ref[...] = delta
        m2_ref[...] = m2
        v2_ref[...] = v2

    spec = pl.BlockSpec(w.shape, lambda i: (0, 0))
    out = jax.ShapeDtypeStruct(w.shape, F32)
    return _pcall(
        body, name="adamw_small", grid=(1,),
        in_specs=[spec] * 4, out_specs=[spec] * 3, out_shape=[out] * 3,
        compiler_params=_params(("arbitrary",)),
    )(w, g, m, v)


def _pack(arrs):
    flat = jnp.concatenate([a.reshape(-1) for a in arrs])
    n = flat.shape[0]
    rows = -(-n // 1024) * 8
    return jnp.pad(flat, (0, rows * 128 - n)).reshape(rows, 128)


def _unpack(packed, like):
    flat = packed.reshape(-1)
    out, off = [], 0
    for a in like:
        out.append(flat[off:off + a.size].reshape(a.shape))
        off += a.size
    return out


def kernel(x, mem, g_ffn1, w_ffn1_up, w_ffn1_down, g_mix, w_in, conv_w, sinks, g_mem, w_mem_kv, g_grp, w_out, g_ffn2, w_ffn2_up, w_ffn2_down, g_final, loss_target, m_g_ffn1, m_w_ffn1_up, m_w_ffn1_down, m_g_mix, m_w_in, m_conv_w, m_sinks, m_g_mem, m_w_mem_kv, m_g_grp, m_w_out, m_g_ffn2, m_w_ffn2_up, m_w_ffn2_down, m_g_final, v_g_ffn1, v_w_ffn1_up, v_w_ffn1_down, v_g_mix, v_w_in, v_conv_w, v_sinks, v_g_mem, v_w_mem_kv, v_g_grp, v_w_out, v_g_ffn2, v_w_ffn2_up, v_w_ffn2_down, v_g_final):
    depth = g_ffn1.shape[0]
    t, d = x.shape[1], x.shape[2]
    width = max(d, D_MIX)
    me = _slot(*_position())
    conv_shard = conv_w.shape[2]

    xin, memin, tgt = x[0], mem[0], loss_target[0]

    conv_tile = jnp.zeros((depth * 8, 128), F32).at[:, :conv_shard].set(
        jnp.pad(conv_w, ((0, 0), (0, 8 - conv_w.shape[1]), (0, 0))).reshape(depth * 8, conv_shard))
    tr = lambda a: jnp.swapaxes(a, -1, -2)
    bf = lambda a: a.astype(BF16)
    weights = []
    collective_id = 0
    for l in range(depth):
        groups = [[bf(tr(w_ffn1_up[l])), bf(w_ffn1_down[l])] + ([conv_tile] if l == 0 else []),
                  [bf(tr(w_in[l])), bf(w_mem_kv[l]), bf(w_out[l])],
                  [bf(tr(w_ffn2_up[l])), bf(w_ffn2_down[l])]]
        full = []
        for gi, shards in enumerate(groups):
            full.append(_all_gather(shards, f"all_gather_l{l}_g{gi}", collective_id))
            collective_id += 1
        if l == 0:
            conv_full = full[0][2].reshape(N_DEV, depth, 8, 128)[:, :, :3, :conv_shard]
            conv_full = conv_full.transpose(1, 2, 0, 3).reshape(depth, 3, N_DEV * conv_shard)
        weights.append(dict(
            up1=full[0][0].reshape(2, -1, d), dn1=full[0][1].reshape(-1, d),
            win=full[1][0].reshape(D_IN, d), wkv=full[1][1].reshape(d, 2 * D_MEMQ), wout=full[1][2].reshape(D_MIX, d),
            up2=full[2][0].reshape(2, -1, d), dn2=full[2][1].reshape(-1, d)))

    row = lambda a: a.reshape(1, -1)
    bias_tok, bias_key = _bias_tables()

    h = xin
    saved = []
    for l in range(depth):
        wl = weights[l]
        s = dict(h0=h)
        h, s["gu1"], s["n1"] = _ffn_fwd(h, row(g_ffn1[l]), wl["up1"], wl["dn1"])
        s["h1"] = h
        s["p"], s["n_mix"], s["qh"] = _mix_proj_fwd(h, row(g_mix[l]), wl["win"])
        s["mkv"], s["nt_mem"] = _memkv_fwd(memin, row(g_mem[l]), wl["wkv"], s["p"])
        s["y"], s["lse"] = _mix_core_fwd(s["p"], s["qh"], s["mkv"], conv_full[l], row(sinks[l]), bias_tok)
        h, s["mt"] = _mix_out_fwd(s["y"], h, row(g_grp[l]), wl["wout"])
        s["h2"] = h
        h, s["gu2"], s["n2"] = _ffn_fwd(h, row(g_ffn2[l]), wl["up2"], wl["dn2"])
        saved.append(s)

    dh, loss_part, dg_final = _final_loss(h, row(g_final), tgt)

    small = {}
    dep = loss_part

    started = []

    def scatter(names, partials, label):
        state, token = _scatter_start(partials, f"scatter_grads_{label}")
        started.append((names, state))
        return token

    for l in reversed(range(depth)):
        wl, s = weights[l], saved[l]
        dh, agu, dyb, small["g_ffn2", l] = _ffn_bwd_act(dh, s["h2"], row(g_ffn2[l]), s["gu2"], wl["up2"], wl["dn2"], dep)
        ddn2 = _ffn_bwd_w(agu, 2, 1, dyb, agu, f"ffn_bwd_w_down_l{l}_ffn2").reshape(N_DEV, -1, d)
        dup2 = _ffn_bwd_w(agu, 0, 2, s["n2"], ddn2, f"ffn_bwd_w_up_l{l}_ffn2").reshape(N_DEV, -1, d)
        dep = scatter([("w_ffn2_up", l), ("w_ffn2_down", l)], [dup2, ddn2], f"l{l}_ffn2")
        dyconv, doh, delta, dwout, small["g_grp", l] = _mix_out_bwd(dh, s["y"], row(g_grp[l]), wl["wout"], s["mt"], dep)
        dp, dmkv, small["conv_w", l], small["sinks", l] = _mix_core_bwd(
            s["p"], s["qh"], dyconv, doh, delta, s["lse"], s["mkv"], conv_full[l], row(sinks[l]), bias_tok, bias_key)
        dwkv, small["g_mem", l] = _memkv_bwd(dmkv, memin, row(g_mem[l]), wl["wkv"], s["nt_mem"])
        dh, dwin, small["g_mix", l] = _mix_proj_bwd(dp, dh, s["h1"], row(g_mix[l]), wl["win"], s["n_mix"])
        dep = scatter([("w_in", l), ("w_mem_kv", l), ("w_out", l)],
                      [dwin.reshape(N_DEV, -1, d), dwkv.reshape(N_DEV, -1, 2 * D_MEMQ), dwout.reshape(N_DEV, -1, d)],
                      f"l{l}_mix")
        dh, agu, dyb, small["g_ffn1", l] = _ffn_bwd_act(dh, s["h0"], row(g_ffn1[l]), s["gu1"], wl["up1"], wl["dn1"], dep)
        ddn1 = _ffn_bwd_w(agu, 2, 1, dyb, agu, f"ffn_bwd_w_down_l{l}_ffn1").reshape(N_DEV, -1, d)
        if l > 0:
            dup1 = _ffn_bwd_w(agu, 0, 2, s["n1"], ddn1, f"ffn_bwd_w_up_l{l}_ffn1").reshape(N_DEV, -1, d)
            dep = scatter([("w_ffn1_up", l), ("w_ffn1_down", l)], [dup1, ddn1], f"l{l}_ffn1")
        else:
            dep = scatter([("w_ffn1_down", l)], [ddn1], f"l{l}_ffn1_down")
            dup1 = _ffn_bwd_w(agu, 0, 2, s["n1"], dep, f"ffn_bwd_w_up_l{l}_ffn1").reshape(N_DEV, -1, d)
            dep = scatter([("w_ffn1_up", l)], [dup1], f"l{l}_ffn1_up")
    grad_x = dh[None]

    big = {"w_ffn2_up": (w_ffn2_up, m_w_ffn2_up, v_w_ffn2_up, True), "w_ffn2_down": (w_ffn2_down, m_w_ffn2_down, v_w_ffn2_down, False),
           "w_in": (w_in, m_w_in, v_w_in, True), "w_mem_kv": (w_mem_kv, m_w_mem_kv, v_w_mem_kv, False),
           "w_out": (w_out, m_w_out, v_w_out, False), "w_ffn1_up": (w_ffn1_up, m_w_ffn1_up, v_w_ffn1_up, True),
           "w_ffn1_down": (w_ffn1_down, m_w_ffn1_down, v_w_ffn1_down, False)}
    me_index = jnp.reshape(me, (1,)).astype(jnp.int32)
    sharded, landed = {}, {}

    def finish(groups, after):
        for names, state in groups:
            owns, lands = _scatter_wait(state, after)
            for key, own, land in zip(names, owns, lands):
                landed[key] = (own, land)
            after = lands[0]
            for name in dict.fromkeys(n for n, _ in names):
                if name not in sharded and all((name, l) in landed for l in range(depth)):
                    w, m, v, transposed = big[name]
                    fix = tr if transposed else (lambda a: a)
                    res = _adamw_sharded(me_index, [landed[name, l] for l in range(depth)], fix(w), fix(m), fix(v), after)
                    sharded[name] = tuple(fix(r) for r in res)
                    after = res[0]
        return after

    dep = finish(started[:-2], dep)

    def lanes(a):
        return jnp.pad(a, ((0, 0), (0, width - a.shape[1])))

    def first_row(a):
        return lanes(jnp.pad(a, ((0, 8 - a.shape[0]), (0, 0))))

    vec_names = ["g_ffn1", "g_mix", "g_mem", "g_grp", "g_ffn2", "sinks"]
    tiles = [lanes(small[n, l]) for n in vec_names for l in range(depth)]
    tiles += [first_row(small["conv_w", l][k:k + 1]) for l in range(depth) for k in range(3)]
    tiles.append(lanes(dg_final))
    n_real = len(tiles)
    tiles.append(lanes(loss_part))
    tiles.append(lanes(dep[0, :8, :128]))
    tiles += [jnp.zeros((8, width), F32)] * (-len(tiles) % 8)
    summed = _all_reduce_rows(jnp.stack(tiles))
    loss = 0.5 * jnp.sum(summed[n_real]) / d

    def vec(n, wd):
        return jnp.stack([summed[vec_names.index(n) * depth + l, :wd] for l in range(depth)])

    conv_base = len(vec_names) * depth
    conv_grad = jnp.stack([jnp.stack([summed[conv_base + 3 * l + k, :D_CONV] for k in range(3)]) for l in range(depth)])
    grads_small = {
        "g_ffn1": vec("g_ffn1", d), "g_mix": vec("g_mix", d), "g_mem": vec("g_mem", d),
        "g_grp": vec("g_grp", D_MIX), "g_ffn2": vec("g_ffn2", d), "sinks": vec("sinks", N_SWA_HEADS),
        "conv_w": lax.dynamic_slice_in_dim(conv_grad, me * conv_shard, conv_shard, axis=2),
        "g_final": summed[n_real - 1, :d],
    }
    small_w = [("g_ffn1", g_ffn1, m_g_ffn1, v_g_ffn1), ("g_mix", g_mix, m_g_mix, v_g_mix),
               ("conv_w", conv_w, m_conv_w, v_conv_w), ("sinks", sinks, m_sinks, v_sinks),
               ("g_mem", g_mem, m_g_mem, v_g_mem), ("g_grp", g_grp, m_g_grp, v_g_grp),
               ("g_ffn2", g_ffn2, m_g_ffn2, v_g_ffn2), ("g_final", g_final, m_g_final, v_g_final)]
    like = [w for _, w, _, _ in small_w]
    packed = _adamw_small(_pack(like), _pack([grads_small[n] for n, _, _, _ in small_w]),
                          _pack([m for _, _, m, _ in small_w]), _pack([v for _, _, _, v in small_w]))
    small_out = {n: (grads_small[n], dl, m2, v2)
                 for (n, _, _, _), dl, m2, v2 in zip(small_w, *[_unpack(pk, like) for pk in packed])}

    finish(started[-2:], packed[0])

    order = ["g_ffn1", "w_ffn1_up", "w_ffn1_down", "g_mix", "w_in", "conv_w", "sinks", "g_mem", "w_mem_kv", "g_grp",
             "w_out", "g_ffn2", "w_ffn2_up", "w_ffn2_down", "g_final"]
    results = {**sharded, **small_out}
    outs = [loss, grad_x]
    for part in range(4):
        outs += [results[n][part] for n in order]
    return tuple(outs)
```

```python
import numpy as np
import jax
import jax.numpy as jnp
from jax import lax
from jax.experimental import pallas as pl
from jax.experimental.pallas import tpu as pltpu
from jax.experimental.pallas import tpu_sc as plsc

F32 = jnp.float32
BF16 = jnp.bfloat16

N_DEV = 8
EPS = 1e-6
N_SWA_HEADS = 8
N_SWA_KV = 2
SWA_GROUP = N_SWA_HEADS // N_SWA_KV
HEAD_DIM = 64
N_MEM_HEADS = 4
D_CONV = 256
BLOCK = 128
D_SWA = N_SWA_HEADS * HEAD_DIM
D_KV = N_SWA_KV * HEAD_DIM
D_MEMQ = N_MEM_HEADS * HEAD_DIM
D_MIX = D_CONV + D_SWA + D_MEMQ
D_IN = 3 * D_CONV + D_SWA + 2 * D_KV + D_MEMQ
COL_BG, COL_CG, COL_U = 0, D_CONV, 2 * D_CONV
COL_Q = 3 * D_CONV
COL_K = COL_Q + D_SWA
COL_V = COL_K + D_KV
COL_QM = COL_V + D_KV
MIX_GROUPS = ((0, D_CONV), (D_CONV, D_CONV + D_SWA), (D_CONV + D_SWA, D_MIX))
SLOPES = tuple(2.0 ** (-8.0 * (i + 1) / N_SWA_HEADS) for i in range(N_SWA_HEADS))
SCALE = HEAD_DIM ** -0.5
NEG = -1e30

ADAM_LR = 0.001
ADAM_B1 = 0.9
ADAM_B2 = 0.999
ADAM_EPS = 1e-08
ADAM_WD = 0.01
ADAM_STEP = 10

V7X_VMEM_BYTES = 64 * 1024 * 1024
VMEM_LIMIT = (V7X_VMEM_BYTES * 3) // 4
MESH = pl.DeviceIdType.MESH


def _pcall(body, **kw):
    return pl.pallas_call(body, **kw)


def _params(sem=None, vmem=VMEM_LIMIT):
    return pltpu.CompilerParams(dimension_semantics=sem, vmem_limit_bytes=vmem)


def _dot(a, b):
    return lax.dot_general(a, b, (((1,), (0,)), ((), ())), preferred_element_type=F32)


def _dot_nt(a, b):
    return lax.dot_general(a, b, (((1,), (1,)), ((), ())), preferred_element_type=F32)


def _dot_tn(a, b):
    return lax.dot_general(a, b, (((0,), (0,)), ((), ())), preferred_element_type=F32)


def _rstd(x):
    return lax.rsqrt(jnp.mean(x * x, axis=-1, keepdims=True) + EPS)


def _sigmoid(x):
    return 1.0 / (1.0 + jnp.exp(-x))


def _sum8(x):
    r, w = x.shape
    return jnp.sum(x.reshape(r // 8, 8, w), axis=0)


def _tok_block(t, rows=512):
    return min(rows, t)


def _feat_block(f):
    return f // (N_DEV // 2)


def _ffn_fwd(h, g, wup_t, wdn):
    t, d = h.shape
    f = wdn.shape[0]
    tm, tf = _tok_block(t), _feat_block(f)
    ni, nj = t // tm, f // tf

    def body(h_ref, g_ref, wup_ref, wdn_ref, ho_ref, gu_ref, n_ref, nt_ref, acc_ref):
        j = pl.program_id(1)

        @pl.when(j == 0)
        def _():
            hh = h_ref[...]
            n = hh * _rstd(hh) * g_ref[...]
            n_ref[...] = n.astype(BF16)
            nt_ref[...] = n.T.astype(BF16)
            acc_ref[...] = jnp.zeros_like(acc_ref)

        nt = nt_ref[...]
        gate = _dot(wup_ref[0], nt)
        up = _dot(wup_ref[1], nt)
        gu_ref[0] = gate.astype(BF16)
        gu_ref[1] = up.astype(BF16)
        a = gate * _sigmoid(gate) * up
        acc_ref[...] += _dot_tn(a.astype(BF16), wdn_ref[...])

        @pl.when(j == nj - 1)
        def _():
            ho_ref[...] = h_ref[...] + 0.5 * acc_ref[...]

    return _pcall(
        body, name="ffn_fwd", grid=(ni, nj),
        in_specs=[pl.BlockSpec((tm, d), lambda i, j: (i, 0)),
                  pl.BlockSpec((1, d), lambda i, j: (0, 0)),
                  pl.BlockSpec((2, tf, d), lambda i, j: (0, j, 0)),
                  pl.BlockSpec((tf, d), lambda i, j: (j, 0))],
        out_specs=[pl.BlockSpec((tm, d), lambda i, j: (i, 0)),
                   pl.BlockSpec((2, tf, tm), lambda i, j: (0, j, i)),
                   pl.BlockSpec((tm, d), lambda i, j: (i, 0))],
        out_shape=[jax.ShapeDtypeStruct((t, d), F32),
                   jax.ShapeDtypeStruct((2, f, t), BF16),
                   jax.ShapeDtypeStruct((t, d), BF16)],
        scratch_shapes=[pltpu.VMEM((d, tm), BF16), pltpu.VMEM((tm, d), F32)],
        compiler_params=_params(("parallel", "arbitrary")),
    )(h, g, wup_t, wdn)


def _ffn_bwd_act(dho, h, g, gu, wup_t, wdn, dep):
    t, d = h.shape
    f = wdn.shape[0]
    tm, tf = _tok_block(t), _feat_block(f)
    ni, nj = t // tm, f // tf

    def body(dho_ref, h_ref, g_ref, gu_ref, wup_ref, wdn_ref, dep_ref, dh_ref, agu_ref, dyb_ref, dg_ref, dyt_ref, acc_ref):
        i = pl.program_id(0)
        j = pl.program_id(1)

        @pl.when(j == 0)
        def _():
            dy0 = 0.5 * dho_ref[...]
            dyb_ref[...] = dy0.astype(BF16)
            dyt_ref[...] = dy0.T.astype(BF16)
            acc_ref[...] = jnp.zeros_like(acc_ref)

        da = _dot(wdn_ref[...], dyt_ref[...])
        gate = gu_ref[0].astype(F32)
        up = gu_ref[1].astype(F32)
        sg = _sigmoid(gate)
        silu = gate * sg
        dgate = (da * up * (sg * (1.0 + gate * (1.0 - sg)))).astype(BF16)
        dup = (da * silu).astype(BF16)
        agu_ref[0] = dgate
        agu_ref[1] = dup
        agu_ref[2] = (silu * up).astype(BF16)
        acc_ref[...] += _dot_tn(dgate, wup_ref[0])
        acc_ref[...] += _dot_tn(dup, wup_ref[1])

        @pl.when(j == nj - 1)
        def _():
            hh = h_ref[...]
            r = _rstd(hh)
            xhat = hh * r
            dnf = acc_ref[...]
            dxh = dnf * g_ref[...]
            dh_ref[...] = dho_ref[...] + r * (dxh - xhat * jnp.mean(dxh * xhat, axis=-1, keepdims=True))
            part = _sum8(dnf * xhat)

            @pl.when(i == 0)
            def _():
                dg_ref[...] = part

            @pl.when(i > 0)
            def _():
                dg_ref[...] += part

    return _pcall(
        body, name="ffn_bwd_act", grid=(ni, nj),
        in_specs=[pl.BlockSpec((tm, d), lambda i, j: (i, 0)),
                  pl.BlockSpec((tm, d), lambda i, j: (i, 0)),
                  pl.BlockSpec((1, d), lambda i, j: (0, 0)),
                  pl.BlockSpec((2, tf, tm), lambda i, j: (0, j, i)),
                  pl.BlockSpec((2, tf, d), lambda i, j: (0, j, 0)),
                  pl.BlockSpec((tf, d), lambda i, j: (j, 0)),
                  pl.BlockSpec(memory_space=pl.ANY)],
        out_specs=[pl.BlockSpec((tm, d), lambda i, j: (i, 0)),
                   pl.BlockSpec((3, tf, tm), lambda i, j: (0, j, i)),
                   pl.BlockSpec((tm, d), lambda i, j: (i, 0)),
                   pl.BlockSpec((8, d), lambda i, j: (0, 0))],
        out_shape=[jax.ShapeDtypeStruct((t, d), F32),
                   jax.ShapeDtypeStruct((3, f, t), BF16),
                   jax.ShapeDtypeStruct((t, d), BF16),
                   jax.ShapeDtypeStruct((8, d), F32)],
        scratch_shapes=[pltpu.VMEM((d, tm), BF16), pltpu.VMEM((tm, d), F32)],
        compiler_params=_params(("arbitrary", "arbitrary")),
    )(dho, h, g, gu, wup_t, wdn, dep)


def _ffn_bwd_w(agu, first, count, rhs, dep, name):
    _, f, t = agu.shape
    d = rhs.shape[1]
    tm, tf = _tok_block(t, 2048), _feat_block(f)
    ni, nj = t // tm, f // tf

    def body(lhs_ref, rhs_ref, dep_ref, dw_ref, acc_ref):
        i = pl.program_id(1)
        @pl.when(i == 0)
        def _():
            acc_ref[...] = jnp.zeros_like(acc_ref)

        rb = rhs_ref[...]
        for k in range(count):
            acc_ref[k] += _dot(lhs_ref[k], rb)

        @pl.when(i == ni - 1)
        def _():
            dw_ref[...] = acc_ref[...].astype(BF16)

    return _pcall(
        body, name=name, grid=(nj, ni),
        in_specs=[pl.BlockSpec((count, tf, tm), lambda j, i: (first // count, j, i)),
                  pl.BlockSpec((tm, d), lambda j, i: (i, 0)),
                  pl.BlockSpec(memory_space=pl.ANY)],
        out_specs=pl.BlockSpec((count, tf, d), lambda j, i: (0, j, 0)),
        out_shape=jax.ShapeDtypeStruct((count, f, d), BF16),
        scratch_shapes=[pltpu.VMEM((count, tf, d), F32)],
        compiler_params=_params(("parallel", "arbitrary")),
    )(agu, rhs, dep)


N_HEADS = N_SWA_HEADS + N_MEM_HEADS


def _q_col(hd):
    return COL_Q + HEAD_DIM * hd if hd < N_SWA_HEADS else COL_QM + HEAD_DIM * (hd - N_SWA_HEADS)


def _mix_proj_fwd(h, g, win_t):
    t, d = h.shape
    tm = _tok_block(t)

    def body(h_ref, g_ref, win_ref, p_ref, n_ref, qh_ref):
        hh = h_ref[...]
        n = (hh * _rstd(hh) * g_ref[...]).astype(BF16)
        n_ref[...] = n
        proj = _dot_nt(n, win_ref[...])
        p_ref[...] = proj.astype(BF16)
        for hd in range(N_HEADS):
            c0 = _q_col(hd)
            qh_ref[hd] = (proj[:, c0:c0 + HEAD_DIM] * SCALE).astype(BF16)

    return _pcall(
        body, name="mix_proj_fwd", grid=(t // tm,),
        in_specs=[pl.BlockSpec((tm, d), lambda i: (i, 0)),
                  pl.BlockSpec((1, d), lambda i: (0, 0)),
                  pl.BlockSpec((D_IN, d), lambda i: (0, 0))],
        out_specs=[pl.BlockSpec((tm, D_IN), lambda i: (i, 0)),
                   pl.BlockSpec((tm, d), lambda i: (i, 0)),
                   pl.BlockSpec((N_HEADS, tm, HEAD_DIM), lambda i: (0, i, 0))],
        out_shape=[jax.ShapeDtypeStruct((t, D_IN), BF16), jax.ShapeDtypeStruct((t, d), BF16),
                   jax.ShapeDtypeStruct((N_HEADS, t, HEAD_DIM), BF16)],
        compiler_params=_params(("parallel",)),
    )(h, g, win_t)


def _memkv_fwd(mem, g, wkv, dep):
    m, d = mem.shape

    def body(mem_ref, g_ref, w_ref, dep_ref, mkv_ref, nt_ref):
        mm = mem_ref[...]
        n = mm * _rstd(mm) * g_ref[...]
        nt_ref[...] = n.T.astype(BF16)
        mkv_ref[...] = _dot(n.astype(BF16), w_ref[...]).astype(BF16)

    return _pcall(
        body, name="memkv_fwd", grid=(1,),
        in_specs=[pl.BlockSpec((m, d), lambda i: (0, 0)),
                  pl.BlockSpec((1, d), lambda i: (0, 0)),
                  pl.BlockSpec((d, 2 * D_MEMQ), lambda i: (0, 0)),
                  pl.BlockSpec(memory_space=pl.ANY)],
        out_specs=[pl.BlockSpec((m, 2 * D_MEMQ), lambda i: (0, 0)),
                   pl.BlockSpec((d, m), lambda i: (0, 0))],
        out_shape=[jax.ShapeDtypeStruct((m, 2 * D_MEMQ), BF16), jax.ShapeDtypeStruct((d, m), BF16)],
        compiler_params=_params(("arbitrary",)),
    )(mem, g, wkv, dep)


def _memkv_bwd(dmkv, mem, g, wkv, nt):
    m, d = mem.shape

    def body(dmkv_ref, mem_ref, g_ref, w_ref, nt_ref, dw_ref, dg_ref):
        db = dmkv_ref[...].astype(BF16)
        dw_ref[...] = _dot(nt_ref[...], db).astype(BF16)
        dn = _dot_nt(db, w_ref[...])
        mm = mem_ref[...]
        dg_ref[...] = _sum8(dn * (mm * _rstd(mm)))

    return _pcall(
        body, name="memkv_bwd", grid=(1,),
        in_specs=[pl.BlockSpec((m, 2 * D_MEMQ), lambda i: (0, 0)),
                  pl.BlockSpec((m, d), lambda i: (0, 0)),
                  pl.BlockSpec((1, d), lambda i: (0, 0)),
                  pl.BlockSpec((d, 2 * D_MEMQ), lambda i: (0, 0)),
                  pl.BlockSpec((d, m), lambda i: (0, 0))],
        out_specs=[pl.BlockSpec((d, 2 * D_MEMQ), lambda i: (0, 0)),
                   pl.BlockSpec((8, d), lambda i: (0, 0))],
        out_shape=[jax.ShapeDtypeStruct((d, 2 * D_MEMQ), BF16), jax.ShapeDtypeStruct((8, d), F32)],
        compiler_params=_params(("arbitrary",)),
    )(dmkv, mem, g, wkv, nt)


def _shift_rows(v, k, edge_rows, row):
    out = pltpu.roll(v, k, 0)
    for r in range(k):
        out = jnp.where(row == r, edge_rows[r], out)
    return out


def _shift_rows_up(v, k, edge_rows, row):
    n = v.shape[0]
    out = pltpu.roll(v, n - k, 0)
    for r in range(k):
        out = jnp.where(row == n - k + r, edge_rows[r], out)
    return out


GROUP_ROWS = SWA_GROUP * BLOCK
BIAS_CUR, BIAS_PREV, BIAS_NONE = 0, 1, 2


def _bias_tables():
    tq = np.arange(BLOCK)[:, None]
    sk = np.arange(BLOCK)[None, :]
    slopes = np.asarray(SLOPES, np.float32)[:, None, None]
    cur = np.where(tq >= sk, -slopes * (tq - sk).astype(np.float32), NEG)
    prev = np.where(sk > tq, -slopes * (tq + BLOCK - sk).astype(np.float32), NEG)
    none = np.full_like(cur, NEG)
    tok = np.stack([cur, prev, none]).astype(np.float32).reshape(3, N_SWA_KV, GROUP_ROWS, BLOCK)
    return jnp.asarray(tok), jnp.asarray(np.ascontiguousarray(tok.transpose(0, 1, 3, 2)))


def _head_cols(hd):
    return D_CONV + HEAD_DIM * hd


def _stack_cols(ref, heads):
    return jnp.concatenate([ref[:, hd:hd + 1] for hd in heads], axis=0)


def _mix_core_fwd(p, qh, mkv, convw, sinks, bias_tok):
    t = p.shape[0]
    m = mkv.shape[0]
    nb = t // BLOCK

    def body(sk_ref, pc_ref, pkv_ref, ppc_ref, ppu_ref, qh_ref, mkv_ref, cw_ref, bc_ref, bp_ref, y_ref, l_ref):
        i = pl.program_id(0)
        prevf = (i > 0).astype(F32)
        row = lax.broadcasted_iota(jnp.int32, (BLOCK, D_CONV), 0)

        bg = pc_ref[:, COL_BG:COL_BG + D_CONV].astype(F32)
        cg = pc_ref[:, COL_CG:COL_CG + D_CONV].astype(F32)
        u = pc_ref[:, COL_U:COL_U + D_CONV].astype(F32)
        vv = cg * u
        pvv = ppc_ref[...].astype(F32) * ppu_ref[...].astype(F32) * prevf
        vv1 = _shift_rows(vv, 1, [pvv[15:16]], row)
        vv2 = _shift_rows(vv, 2, [pvv[14:15], pvv[15:16]], row)
        w = cw_ref[...]
        y_ref[:, 0:D_CONV] = bg * (w[0:1] * vv2 + w[1:2] * vv1 + w[2:3] * vv)

        lane = lax.broadcasted_iota(jnp.int32, (BLOCK, 128), 1)
        lse_all = jnp.zeros((BLOCK, 128), F32)
        for kv in range(N_SWA_KV):
            heads = range(kv * SWA_GROUP, (kv + 1) * SWA_GROUP)
            kc = pc_ref[:, COL_K + HEAD_DIM * kv:COL_K + HEAD_DIM * (kv + 1)]
            vc = pc_ref[:, COL_V + HEAD_DIM * kv:COL_V + HEAD_DIM * (kv + 1)]
            kp = pkv_ref[:, HEAD_DIM * kv:HEAD_DIM * (kv + 1)]
            vp = pkv_ref[:, D_KV + HEAD_DIM * kv:D_KV + HEAD_DIM * (kv + 1)]
            qg = qh_ref[kv * SWA_GROUP:(kv + 1) * SWA_GROUP].reshape(GROUP_ROWS, HEAD_DIM)
            sc = _dot_nt(qg, kc) + bc_ref[0, kv]
            sp = _dot_nt(qg, kp) + bp_ref[0, kv]
            sink = jnp.concatenate([jnp.full((BLOCK, 1), sk_ref[0, hd], F32) for hd in heads], axis=0)
            mx = jnp.maximum(jnp.max(jnp.maximum(sc, sp), axis=-1, keepdims=True), sink)
            ec = jnp.exp(sc - mx)
            ep = jnp.exp(sp - mx)
            den = jnp.sum(ec + ep, axis=-1, keepdims=True) + jnp.exp(sink - mx)
            o = (_dot(ec.astype(BF16), vc) + _dot(ep.astype(BF16), vp)) / den
            lse = mx + jnp.log(den)
            for gi, hd in enumerate(heads):
                rows = slice(gi * BLOCK, (gi + 1) * BLOCK)
                y_ref[:, _head_cols(hd):_head_cols(hd) + HEAD_DIM] = o[rows]
                lse_all = jnp.where(lane == hd, lse[rows], lse_all)

        for hm in range(N_MEM_HEADS):
            hd = N_SWA_HEADS + hm
            mk = mkv_ref[:, HEAD_DIM * hm:HEAD_DIM * (hm + 1)]
            mv = mkv_ref[:, D_MEMQ + HEAD_DIM * hm:D_MEMQ + HEAD_DIM * (hm + 1)]
            s = _dot_nt(qh_ref[hd], mk)
            mx = jnp.max(s, axis=-1, keepdims=True)
            e = jnp.exp(s - mx)
            den = jnp.sum(e, axis=-1, keepdims=True)
            y_ref[:, _head_cols(hd):_head_cols(hd) + HEAD_DIM] = _dot(e.astype(BF16), mv) / den
            lse_all = jnp.where(lane == hd, mx + jnp.log(den), lse_all)
        l_ref[...] = lse_all

    kv_col = COL_K // (2 * D_KV)
    bias_block = (1, N_SWA_KV, GROUP_ROWS, BLOCK)
    return _pcall(
        body, name="mix_core_fwd", grid=(nb,),
        in_specs=[pl.BlockSpec(memory_space=pltpu.SMEM),
                  pl.BlockSpec((BLOCK, D_IN), lambda i: (i, 0)),
                  pl.BlockSpec((BLOCK, 2 * D_KV), lambda i: (jnp.maximum(i - 1, 0), kv_col)),
                  pl.BlockSpec((16, D_CONV), lambda i: (jnp.maximum(i * (BLOCK // 16) - 1, 0), COL_CG // D_CONV)),
                  pl.BlockSpec((16, D_CONV), lambda i: (jnp.maximum(i * (BLOCK // 16) - 1, 0), COL_U // D_CONV)),
                  pl.BlockSpec((N_HEADS, BLOCK, HEAD_DIM), lambda i: (0, i, 0)),
                  pl.BlockSpec((m, 2 * D_MEMQ), lambda i: (0, 0)),
                  pl.BlockSpec((3, D_CONV), lambda i: (0, 0)),
                  pl.BlockSpec(bias_block, lambda i: (BIAS_CUR, 0, 0, 0)),
                  pl.BlockSpec(bias_block, lambda i: (jnp.where(i == 0, BIAS_NONE, BIAS_PREV), 0, 0, 0))],
        out_specs=[pl.BlockSpec((BLOCK, D_MIX), lambda i: (i, 0)),
                   pl.BlockSpec((BLOCK, 128), lambda i: (i, 0))],
        out_shape=[jax.ShapeDtypeStruct((t, D_MIX), F32), jax.ShapeDtypeStruct((t, 128), F32)],
        compiler_params=_params(("parallel",)),
    )(sinks, p, p, p, p, qh, mkv, convw, bias_tok, bias_tok)


def _mix_core_bwd(p, qh, dyconv, doh, delta, lse, mkv, convw, sinks, bias_tok, bias_key):
    t = p.shape[0]
    m = mkv.shape[0]
    nb = t // BLOCK

    def body(sk_ref, pc_ref, pkv_ref, ppc_ref, ppu_ref, pnb_ref, dyc_ref, dyn_ref, qc_ref, qn_ref, doc_ref, don_ref,
             dlc_ref, dln_ref, lc_ref, ln_ref, mkv_ref, cw_ref, bp_ref, bct_ref, bnt_ref,
             dp_ref, dmkv_ref, dcw_ref, dsk_ref):
        i = pl.program_id(0)
        prevf = (i > 0).astype(F32)
        nextf = (i < nb - 1).astype(F32)
        row = lax.broadcasted_iota(jnp.int32, (BLOCK, D_CONV), 0)

        @pl.when(i == 0)
        def _():
            dmkv_ref[...] = jnp.zeros_like(dmkv_ref)
            dcw_ref[...] = jnp.zeros_like(dcw_ref)
            dsk_ref[...] = jnp.zeros_like(dsk_ref)

        bg = pc_ref[:, COL_BG:COL_BG + D_CONV].astype(F32)
        cg = pc_ref[:, COL_CG:COL_CG + D_CONV].astype(F32)
        u = pc_ref[:, COL_U:COL_U + D_CONV].astype(F32)
        vv = cg * u
        pvv = ppc_ref[...].astype(F32) * ppu_ref[...].astype(F32) * prevf
        vv1 = _shift_rows(vv, 1, [pvv[15:16]], row)
        vv2 = _shift_rows(vv, 2, [pvv[14:15], pvv[15:16]], row)
        w = cw_ref[...]
        yconv = w[0:1] * vv2 + w[1:2] * vv1 + w[2:3] * vv
        dyo = dyc_ref[...]
        dyc = dyo * bg
        nxt = dyn_ref[...] * pnb_ref[...].astype(F32) * nextf
        d1 = _shift_rows_up(dyc, 1, [nxt[0:1]], row)
        d2 = _shift_rows_up(dyc, 2, [nxt[0:1], nxt[1:2]], row)
        dvv = w[2:3] * dyc + w[1:2] * d1 + w[0:1] * d2
        dp_ref[:, COL_BG:COL_BG + D_CONV] = (dyo * yconv).astype(BF16)
        dp_ref[:, COL_CG:COL_CG + D_CONV] = (dvv * u).astype(BF16)
        dp_ref[:, COL_U:COL_U + D_CONV] = (dvv * cg).astype(BF16)
        dcw_ref[0:1, :] += jnp.sum(dyc * vv2, axis=0, keepdims=True)
        dcw_ref[1:2, :] += jnp.sum(dyc * vv1, axis=0, keepdims=True)
        dcw_ref[2:3, :] += jnp.sum(dyc * vv, axis=0, keepdims=True)

        lse_t, dl_t = lc_ref[...].T, dlc_ref[...].T
        lse_nt, dl_nt = ln_ref[...].T, dln_ref[...].T

        def stack_rows(tile_t, heads):
            return jnp.concatenate([tile_t[hd:hd + 1, :] for hd in heads], axis=1)

        lane8 = jnp.where(lax.broadcasted_iota(jnp.int32, (8, 128), 0) == 0,
                          lax.broadcasted_iota(jnp.int32, (8, 128), 1), -1)
        dsk = jnp.zeros((8, 128), F32)
        for kv in range(N_SWA_KV):
            heads = range(kv * SWA_GROUP, (kv + 1) * SWA_GROUP)
            kc = pc_ref[:, COL_K + HEAD_DIM * kv:COL_K + HEAD_DIM * (kv + 1)]
            vc = pc_ref[:, COL_V + HEAD_DIM * kv:COL_V + HEAD_DIM * (kv + 1)]
            kp = pkv_ref[:, HEAD_DIM * kv:HEAD_DIM * (kv + 1)]
            vp = pkv_ref[:, D_KV + HEAD_DIM * kv:D_KV + HEAD_DIM * (kv + 1)]
            qg = qc_ref[kv * SWA_GROUP:(kv + 1) * SWA_GROUP].reshape(GROUP_ROWS, HEAD_DIM)
            dog = doc_ref[kv * SWA_GROUP:(kv + 1) * SWA_GROUP].reshape(GROUP_ROWS, HEAD_DIM)
            qn = qn_ref[kv * SWA_GROUP:(kv + 1) * SWA_GROUP].reshape(GROUP_ROWS, HEAD_DIM)
            don = don_ref[kv * SWA_GROUP:(kv + 1) * SWA_GROUP].reshape(GROUP_ROWS, HEAD_DIM)
            lse_col, dl_col = _stack_cols(lc_ref, heads), _stack_cols(dlc_ref, heads)
            pp_ = jnp.exp(_dot_nt(qg, kp) + bp_ref[0, kv] - lse_col)
            dsp = (pp_ * (_dot_nt(dog, vp) - dl_col)).astype(BF16)
            dq = _dot(dsp, kp)
            pt = jnp.exp(_dot_nt(kc, qg) + bct_ref[0, kv] - stack_rows(lse_t, heads))
            dst = (pt * (_dot_nt(vc, dog) - stack_rows(dl_t, heads))).astype(BF16)
            dv = _dot(pt.astype(BF16), dog)
            dk = _dot(dst, qg)
            dq = dq + _dot_tn(dst, kc)
            ptn = jnp.exp(_dot_nt(kc, qn) + bnt_ref[0, kv] - stack_rows(lse_nt, heads))
            dstn = (ptn * (_dot_nt(vc, don) - stack_rows(dl_nt, heads))).astype(BF16)
            dv = dv + _dot(ptn.astype(BF16), don)
            dk = dk + _dot(dstn, qn)
            dp_ref[:, COL_K + HEAD_DIM * kv:COL_K + HEAD_DIM * (kv + 1)] = dk.astype(BF16)
            dp_ref[:, COL_V + HEAD_DIM * kv:COL_V + HEAD_DIM * (kv + 1)] = dv.astype(BF16)
            sink = jnp.concatenate([jnp.full((BLOCK, 1), sk_ref[0, hd], F32) for hd in heads], axis=0)
            sink_term = jnp.exp(sink - lse_col) * dl_col
            for gi, hd in enumerate(heads):
                rows = slice(gi * BLOCK, (gi + 1) * BLOCK)
                dp_ref[:, _q_col(hd):_q_col(hd) + HEAD_DIM] = (dq[rows] * SCALE).astype(BF16)
                dsk = dsk + jnp.where(lane8 == hd, -jnp.sum(sink_term[rows], axis=0, keepdims=True), 0.0)
        dsk_ref[...] += dsk

        for hm in range(N_MEM_HEADS):
            hd = N_SWA_HEADS + hm
            qm, dom = qc_ref[hd], doc_ref[hd]
            mk = mkv_ref[:, HEAD_DIM * hm:HEAD_DIM * (hm + 1)]
            mv = mkv_ref[:, D_MEMQ + HEAD_DIM * hm:D_MEMQ + HEAD_DIM * (hm + 1)]
            pt = jnp.exp(_dot_nt(mk, qm) - lse_t[hd:hd + 1, :])
            dst = (pt * (_dot_nt(mv, dom) - dl_t[hd:hd + 1, :])).astype(BF16)
            dp_ref[:, _q_col(hd):_q_col(hd) + HEAD_DIM] = (_dot_tn(dst, mk) * SCALE).astype(BF16)
            dmkv_ref[:, HEAD_DIM * hm:HEAD_DIM * (hm + 1)] += _dot(dst, qm)
            dmkv_ref[:, D_MEMQ + HEAD_DIM * hm:D_MEMQ + HEAD_DIM * (hm + 1)] += _dot(pt.astype(BF16), dom)

    cur = lambda i: (i, 0)
    const = lambda i: (0, 0)
    rows16 = BLOCK // 16
    last16 = t // 16 - 1
    before = lambda col: (lambda i: (jnp.maximum(i * rows16 - 1, 0), col))
    after = lambda i: (jnp.minimum((i + 1) * rows16, last16), 0)
    heads_cur = lambda i: (0, i, 0)
    heads_next = lambda i: (0, jnp.minimum(i + 1, nb - 1), 0)
    stat_next = lambda i: (jnp.minimum(i + 1, nb - 1), 0)
    tok_block = (1, N_SWA_KV, GROUP_ROWS, BLOCK)
    key_block = (1, N_SWA_KV, BLOCK, GROUP_ROWS)
    head_block = (N_HEADS, BLOCK, HEAD_DIM)
    return _pcall(
        body, name="mix_core_bwd", grid=(nb,),
        in_specs=[pl.BlockSpec(memory_space=pltpu.SMEM),
                  pl.BlockSpec((BLOCK, D_IN), cur),
                  pl.BlockSpec((BLOCK, 2 * D_KV), lambda i: (jnp.maximum(i - 1, 0), COL_K // (2 * D_KV))),
                  pl.BlockSpec((16, D_CONV), before(COL_CG // D_CONV)),
                  pl.BlockSpec((16, D_CONV), before(COL_U // D_CONV)),
                  pl.BlockSpec((16, D_CONV), after),
                  pl.BlockSpec((BLOCK, D_CONV), cur),
                  pl.BlockSpec((16, D_CONV), after),
                  pl.BlockSpec(head_block, heads_cur), pl.BlockSpec(head_block, heads_next),
                  pl.BlockSpec(head_block, heads_cur), pl.BlockSpec(head_block, heads_next),
                  pl.BlockSpec((BLOCK, 128), cur), pl.BlockSpec((BLOCK, 128), stat_next),
                  pl.BlockSpec((BLOCK, 128), cur), pl.BlockSpec((BLOCK, 128), stat_next),
                  pl.BlockSpec((m, 2 * D_MEMQ), const),
                  pl.BlockSpec((3, D_CONV), const),
                  pl.BlockSpec(tok_block, lambda i: (jnp.where(i == 0, BIAS_NONE, BIAS_PREV), 0, 0, 0)),
                  pl.BlockSpec(key_block, lambda i: (BIAS_CUR, 0, 0, 0)),
                  pl.BlockSpec(key_block, lambda i: (jnp.where(i == nb - 1, BIAS_NONE, BIAS_PREV), 0, 0, 0))],
        out_specs=[pl.BlockSpec((BLOCK, D_IN), cur),
                   pl.BlockSpec((m, 2 * D_MEMQ), const),
                   pl.BlockSpec((8, D_CONV), const),
                   pl.BlockSpec((8, 128), const)],
        out_shape=[jax.ShapeDtypeStruct((t, D_IN), BF16),
                   jax.ShapeDtypeStruct((m, 2 * D_MEMQ), F32),
                   jax.ShapeDtypeStruct((8, D_CONV), F32),
                   jax.ShapeDtypeStruct((8, 128), F32)],
        compiler_params=_params(("arbitrary",)),
    )(sinks, p, p, p, p, p, dyconv, dyconv, qh, qh, doh, doh, delta, delta, lse, lse, mkv, convw,
      bias_tok, bias_key, bias_key)


def _group_norms(y):
    out = []
    for a, b in MIX_GROUPS:
        ys = y[:, a:b]
        r = _rstd(ys)
        out.append((ys * r, r))
    return out


def _mix_out_fwd(y, h, g, wout):
    t, d = h.shape
    tm = _tok_block(t)

    def body(y_ref, h_ref, g_ref, w_ref, ho_ref, mt_ref):
        yhat = jnp.concatenate([yh for yh, _ in _group_norms(y_ref[...])], axis=-1)
        mixed = yhat * g_ref[...]
        mt_ref[...] = mixed.T.astype(BF16)
        ho_ref[...] = h_ref[...] + _dot(mixed.astype(BF16), w_ref[...])

    return _pcall(
        body, name="mix_out_fwd", grid=(t // tm,),
        in_specs=[pl.BlockSpec((tm, D_MIX), lambda i: (i, 0)),
                  pl.BlockSpec((tm, d), lambda i: (i, 0)),
                  pl.BlockSpec((1, D_MIX), lambda i: (0, 0)),
                  pl.BlockSpec((D_MIX, d), lambda i: (0, 0))],
        out_specs=[pl.BlockSpec((tm, d), lambda i: (i, 0)),
                   pl.BlockSpec((D_MIX, tm), lambda i: (0, i))],
        out_shape=[jax.ShapeDtypeStruct((t, d), F32), jax.ShapeDtypeStruct((D_MIX, t), BF16)],
        compiler_params=_params(("parallel",)),
    )(y, h, g, wout)


def _head_indicator():
    ind = np.zeros((D_MIX, 128), np.float32)
    for hd in range(N_HEADS):
        ind[_head_cols(hd):_head_cols(hd) + HEAD_DIM, hd] = 1.0
    return jnp.asarray(ind, BF16)


def _mix_out_bwd(dho, y, g, wout, mt, dep):
    t, d = dho.shape
    tm = _tok_block(t)
    ni = t // tm

    def body(dho_ref, y_ref, g_ref, w_ref, mt_ref, ind_ref, dep_ref, dyc_ref, doh_ref, dl_ref, dw_ref, dg_ref, acc_ref):
        i = pl.program_id(0)
        dhb = dho_ref[...].astype(BF16)
        dm = _dot_nt(dhb, w_ref[...])
        pw = _dot(mt_ref[...], dhb)
        gg = g_ref[...]
        yy = y_ref[...]
        dys = []
        dgs = []
        for (a, b), (yhat, r) in zip(MIX_GROUPS, _group_norms(yy)):
            dmg = dm[:, a:b]
            dgs.append(_sum8(dmg * yhat))
            dyh = dmg * gg[:, a:b]
            dys.append(r * (dyh - yhat * jnp.mean(dyh * yhat, axis=-1, keepdims=True)))
        dy = jnp.concatenate(dys, axis=-1)
        dyc_ref[...] = dy[:, 0:D_CONV]
        for hd in range(N_HEADS):
            doh_ref[hd] = dy[:, _head_cols(hd):_head_cols(hd) + HEAD_DIM].astype(BF16)
        prod = dy * yy
        hi = prod.astype(BF16)
        lo = (prod - hi.astype(F32)).astype(BF16)
        dl_ref[...] = _dot(hi, ind_ref[...]) + _dot(lo, ind_ref[...])
        part = jnp.concatenate(dgs, axis=-1)

        @pl.when(i == 0)
        def _():
            acc_ref[...] = pw
            dg_ref[...] = part

        @pl.when(i > 0)
        def _():
            acc_ref[...] += pw
            dg_ref[...] += part

        @pl.when(i == ni - 1)
        def _():
            dw_ref[...] = acc_ref[...].astype(BF16)

    return _pcall(
        body, name="mix_out_bwd", grid=(ni,),
        in_specs=[pl.BlockSpec((tm, d), lambda i: (i, 0)),
                  pl.BlockSpec((tm, D_MIX), lambda i: (i, 0)),
                  pl.BlockSpec((1, D_MIX), lambda i: (0, 0)),
                  pl.BlockSpec((D_MIX, d), lambda i: (0, 0)),
                  pl.BlockSpec((D_MIX, tm), lambda i: (0, i)),
                  pl.BlockSpec((D_MIX, 128), lambda i: (0, 0)),
                  pl.BlockSpec(memory_space=pl.ANY)],
        out_specs=[pl.BlockSpec((tm, D_CONV), lambda i: (i, 0)),
                   pl.BlockSpec((N_HEADS, tm, HEAD_DIM), lambda i: (0, i, 0)),
                   pl.BlockSpec((tm, 128), lambda i: (i, 0)),
                   pl.BlockSpec((D_MIX, d), lambda i: (0, 0)),
                   pl.BlockSpec((8, D_MIX), lambda i: (0, 0))],
        out_shape=[jax.ShapeDtypeStruct((t, D_CONV), F32),
                   jax.ShapeDtypeStruct((N_HEADS, t, HEAD_DIM), BF16),
                   jax.ShapeDtypeStruct((t, 128), F32),
                   jax.ShapeDtypeStruct((D_MIX, d), BF16),
                   jax.ShapeDtypeStruct((8, D_MIX), F32)],
        scratch_shapes=[pltpu.VMEM((D_MIX, d), F32)],
        compiler_params=_params(("arbitrary",)),
    )(dho, y, g, wout, mt, _head_indicator(), dep)


def _mix_proj_bwd(dp, dho, h, g, win_t, n):
    t, d = h.shape
    tm = _tok_block(t)
    ni = t // tm

    def body(dp_ref, dho_ref, h_ref, g_ref, w_ref, n_ref, dh_ref, dw_ref, dg_ref, acc_ref):
        i = pl.program_id(0)
        dpb = dp_ref[...]
        dn = _dot(dpb, w_ref[...])

        @pl.when(i == 0)
        def _():
            acc_ref[...] = jnp.zeros_like(acc_ref)

        acc_ref[...] += _dot_tn(dpb, n_ref[...])
        hh = h_ref[...]
        r = _rstd(hh)
        xhat = hh * r
        dxh = dn * g_ref[...]
        dh_ref[...] = dho_ref[...] + r * (dxh - xhat * jnp.mean(dxh * xhat, axis=-1, keepdims=True))
        part = _sum8(dn * xhat)

        @pl.when(i == 0)
        def _():
            dg_ref[...] = part

        @pl.when(i > 0)
        def _():
            dg_ref[...] += part

        @pl.when(i == ni - 1)
        def _():
            dw_ref[...] = acc_ref[...].astype(BF16)

    return _pcall(
        body, name="mix_proj_bwd", grid=(ni,),
        in_specs=[pl.BlockSpec((tm, D_IN), lambda i: (i, 0)),
                  pl.BlockSpec((tm, d), lambda i: (i, 0)),
                  pl.BlockSpec((tm, d), lambda i: (i, 0)),
                  pl.BlockSpec((1, d), lambda i: (0, 0)),
                  pl.BlockSpec((D_IN, d), lambda i: (0, 0)),
                  pl.BlockSpec((tm, d), lambda i: (i, 0))],
        out_specs=[pl.BlockSpec((tm, d), lambda i: (i, 0)),
                   pl.BlockSpec((D_IN, d), lambda i: (0, 0)),
                   pl.BlockSpec((8, d), lambda i: (0, 0))],
        out_shape=[jax.ShapeDtypeStruct((t, d), F32),
                   jax.ShapeDtypeStruct((D_IN, d), BF16),
                   jax.ShapeDtypeStruct((8, d), F32)],
        scratch_shapes=[pltpu.VMEM((D_IN, d), F32)],
        compiler_params=_params(("arbitrary",)),
    )(dp, dho, h, g, win_t, n)


def _final_loss(h, g, tgt):
    t, d = h.shape
    tm = _tok_block(t)

    def body(h_ref, g_ref, t_ref, dh_ref, ls_ref, dg_ref):
        i = pl.program_id(0)
        hh = h_ref[...]
        r = _rstd(hh)
        xhat = hh * r
        gg = g_ref[...]
        err = xhat * gg - t_ref[...]
        dy = err * (1.0 / d)
        dxh = dy * gg
        dh_ref[...] = r * (dxh - xhat * jnp.mean(dxh * xhat, axis=-1, keepdims=True))
        lpart = _sum8(err * err)
        gpart = _sum8(dy * xhat)

        @pl.when(i == 0)
        def _():
            ls_ref[...] = lpart
            dg_ref[...] = gpart

        @pl.when(i > 0)
        def _():
            ls_ref[...] += lpart
            dg_ref[...] += gpart

    return _pcall(
        body, name="final_loss", grid=(t // tm,),
        in_specs=[pl.BlockSpec((tm, d), lambda i: (i, 0)),
                  pl.BlockSpec((1, d), lambda i: (0, 0)),
                  pl.BlockSpec((tm, d), lambda i: (i, 0))],
        out_specs=[pl.BlockSpec((tm, d), lambda i: (i, 0)),
                   pl.BlockSpec((8, d), lambda i: (0, 0)),
                   pl.BlockSpec((8, d), lambda i: (0, 0))],
        out_shape=[jax.ShapeDtypeStruct((t, d), F32),
                   jax.ShapeDtypeStruct((8, d), F32),
                   jax.ShapeDtypeStruct((8, d), F32)],
        compiler_params=_params(("arbitrary",)),
    )(h, g, tgt)


def _position():
    return lax.axis_index("x"), lax.axis_index("y"), lax.axis_index("c")


def _flip(v, bit):
    return 1 - v if bit else v


def _peer(k):
    x, y, c = _position()
    return _flip(x, k & 4), _flip(y, k & 2), _flip(c, k & 1)


def _slot(px, py, pc):
    return 4 * px + 2 * py + pc


def _handshake(peers):
    barrier = pltpu.get_barrier_semaphore()
    for peer in peers:
        pl.semaphore_signal(barrier, inc=1, device_id=peer, device_id_type=MESH)
    pl.semaphore_wait(barrier, len(peers))


def _sequencer_call(body, name, collective_id, out_type, scratch_types, operands):
    return pl.kernel(
        body, out_type=out_type, mesh=plsc.ScalarSubcoreMesh(axis_name="sequencer", num_cores=1), name=name,
        scratch_types=scratch_types, compiler_params=pltpu.CompilerParams(collective_id=collective_id),
    )(*operands)


def _all_gather(shards, name, collective_id):
    nt = len(shards)

    def body(*refs):
        xs = refs[:nt]
        outs = refs[nt:2 * nt]
        send_sems, recv_sems, local_sems = refs[2 * nt:]
        x, y, c = _position()
        me, sibling = (x, y, c), (x, y, 1 - c)
        chips = [(1 - x, y), (x, 1 - y), (1 - x, 1 - y)]
        _handshake([sibling] + [(*chip, c) for chip in chips])

        def copy(t, k, block, to, src=None):
            dst = outs[t].at[_slot(*block)]
            return pltpu.make_async_remote_copy(
                src_ref=dst if src is None else src, dst_ref=dst,
                send_sem=send_sems.at[t, k], recv_sem=recv_sems.at[t, k],
                device_id=to, device_id_type=MESH)

        mine = [pltpu.make_async_copy(xs[t], outs[t].at[_slot(*me)], local_sems.at[t]) for t in range(nt)]
        for cp in mine:
            cp.start()
        first = []
        for t in range(nt):
            first.append(copy(t, 0, me, sibling, src=xs[t]))
            first += [copy(t, 1 + j, me, (*chip, c), src=xs[t]) for j, chip in enumerate(chips)]
        for cp in first:
            cp.start()
        passed = []
        for j, chip in enumerate(chips):
            for t in range(nt):
                copy(t, 1 + j, (*chip, c), me).wait_recv()
                fwd = copy(t, 4 + j, (*chip, c), sibling)
                fwd.start()
                passed.append(fwd)
        for t in range(nt):
            copy(t, 0, sibling, me).wait_recv()
            for j, chip in enumerate(chips):
                copy(t, 4 + j, (*chip, 1 - c), me).wait_recv()
        for cp in first + passed:
            cp.wait_send()
        for cp in mine:
            cp.wait()

    return _sequencer_call(
        body, name, collective_id,
        out_type=[jax.ShapeDtypeStruct((N_DEV,) + s.shape, s.dtype) for s in shards],
        scratch_types=[pltpu.SemaphoreType.DMA((nt, 7)), pltpu.SemaphoreType.DMA((nt, 7)),
                       pltpu.SemaphoreType.DMA((nt,))],
        operands=shards)


def _scatter_copy(srcs, lands, send_sems, recv_sems, t, k):
    peer = _peer(k)
    return pltpu.make_async_remote_copy(
        src_ref=srcs[t].at[_slot(*peer)], dst_ref=lands[t].at[k],
        send_sem=send_sems.at[t * (N_DEV - 1) + k - 1], recv_sem=recv_sems.at[t * (N_DEV - 1) + k - 1],
        device_id=peer, device_id_type=MESH)


def _scatter_start(partials, name):
    nt = len(partials)

    def body(*refs):
        srcs, lands = refs[:nt], refs[nt:2 * nt]
        send_sems, recv_sems = refs[2 * nt], refs[2 * nt + 1]
        token = refs[-1]
        for k in range(1, N_DEV):
            for t in range(nt):
                _scatter_copy(srcs, lands, send_sems, recv_sems, t, k).start()
        token[...] = jnp.zeros_like(token)

    hbm = pl.BlockSpec(memory_space=pltpu.HBM)
    sem = pl.BlockSpec(memory_space=pltpu.SEMAPHORE)
    shapes = [pltpu.HBM(p.shape, p.dtype) for p in partials]
    lands = [pltpu.with_memory_space_constraint(lax.empty(p.shape, p.dtype), pltpu.HBM) for p in partials]
    srcs = [pltpu.with_memory_space_constraint(p, pltpu.HBM) for p in partials]
    out = _pcall(
        body, name=name,
        out_shape=[pltpu.SemaphoreType.DMA((nt * (N_DEV - 1),))] * 2 + shapes + shapes
        + [jax.ShapeDtypeStruct((8, 128), F32)],
        in_specs=[hbm] * (2 * nt),
        out_specs=[sem, sem] + [hbm] * (2 * nt) + [pl.BlockSpec(memory_space=pltpu.VMEM)],
        input_output_aliases={i: 2 + i for i in range(2 * nt)},
        compiler_params=pltpu.CompilerParams(has_side_effects=pltpu.SideEffectType.DATAFLOW_SIDE_EFFECTING),
    )(*srcs, *lands)
    return (nt, name, out[:-1]), out[-1]


def _scatter_wait(state, after):
    nt, name, (send_sems, recv_sems, *thru) = state

    def body(*refs):
        srcs, lands = refs[:nt], refs[nt:2 * nt]
        send_sems, recv_sems = refs[2 * nt], refs[2 * nt + 1]
        for k in range(1, N_DEV):
            for t in range(nt):
                copy = _scatter_copy(srcs, lands, send_sems, recv_sems, t, k)
                copy.wait_send()
                copy.wait_recv()

    hbm = pl.BlockSpec(memory_space=pltpu.HBM)
    sem = pl.BlockSpec(memory_space=pltpu.SEMAPHORE)
    out = _pcall(
        body, name=name + "_wait",
        out_shape=[pltpu.HBM(a.shape, a.dtype) for a in thru],
        in_specs=[hbm] * (2 * nt) + [sem, sem, pl.BlockSpec(memory_space=pl.ANY)],
        out_specs=[hbm] * (2 * nt),
        input_output_aliases={i: i for i in range(2 * nt)},
        compiler_params=pltpu.CompilerParams(has_side_effects=pltpu.SideEffectType.DATAFLOW_SIDE_EFFECTING),
    )(*thru, send_sems, recv_sems, after)
    return out[:nt], out[nt:]


def _all_reduce_rows(v):
    nv, _, w = v.shape

    def body(v_ref, out_ref, gath_ref, send_sems, recv_sems):
        x, y, c = _position()
        me = _slot(x, y, c)

        def copy(k):
            return pltpu.make_async_remote_copy(
                src_ref=v_ref, dst_ref=gath_ref.at[me],
                send_sem=send_sems.at[k - 1], recv_sem=recv_sems.at[k - 1],
                device_id=_peer(k), device_id_type=MESH)

        def arrival(k):
            return pltpu.make_async_remote_copy(
                src_ref=v_ref, dst_ref=gath_ref.at[_slot(*_peer(k))],
                send_sem=send_sems.at[k - 1], recv_sem=recv_sems.at[k - 1],
                device_id=_peer(k), device_id_type=MESH)

        sent = [copy(k) for k in range(1, N_DEV)]
        for cp in sent:
            cp.start()
        gath_ref[me] = v_ref[...]
        for k in range(1, N_DEV):
            arrival(k).wait_recv()
        for cp in sent:
            cp.wait_send()
        total = gath_ref[0]
        for s in range(1, N_DEV):
            total = total + gath_ref[s]
        out_ref[...] = jnp.sum(total, axis=1)

    vmem = pl.BlockSpec(memory_space=pltpu.VMEM)
    return _pcall(
        body, name="all_reduce_rows",
        in_specs=[vmem], out_specs=vmem,
        out_shape=jax.ShapeDtypeStruct((nv, w), F32),
        scratch_shapes=[pltpu.VMEM((N_DEV, nv, 8, w), F32),
                        pltpu.SemaphoreType.DMA((7,)), pltpu.SemaphoreType.DMA((7,))],
    )(v)


def _adamw_math(w, g, m, v):
    m2 = ADAM_B1 * m + (1.0 - ADAM_B1) * g
    v2 = ADAM_B2 * v + (1.0 - ADAM_B2) * (g * g)
    m_hat = m2 / (1.0 - ADAM_B1 ** ADAM_STEP)
    v_hat = v2 / (1.0 - ADAM_B2 ** ADAM_STEP)
    delta = -ADAM_LR * (m_hat / (jnp.sqrt(v_hat) + ADAM_EPS) + ADAM_WD * w)
    return delta, m2, v2


def _row_block(r):
    for cand in (256, 176, 128):
        if r % cand == 0:
            return cand
    return r


def _adamw_sharded(me, grads, w, m, v, dep):
    (own0, land0), (own1, land1) = grads
    _, r, c = land0.shape
    tr = _row_block(r)
    nr = r // tr

    def body(me_ref, o0_ref, l0_ref, o1_ref, l1_ref, w_ref, m_ref, v_ref, dep_ref, g_ref, d_ref, m2_ref, v2_ref):
        layer = pl.program_id(0)

        def total(own_ref, land_ref):
            acc = own_ref[0].astype(F32)
            for k in range(1, N_DEV):
                acc = acc + land_ref[k].astype(F32)
            return acc

        g = jnp.where(layer == 0, total(o0_ref, l0_ref), total(o1_ref, l1_ref))
        delta, m2, v2 = _adamw_math(w_ref[0], g, m_ref[0], v_ref[0])
        g_ref[0] = g
        d_ref[0] = delta
        m2_ref[0] = m2
        v2_ref[0] = v2

    rows0 = lambda l, i: jnp.where(l == 0, i, nr - 1)
    rows1 = lambda l, i: jnp.where(l == 1, i, 0)
    shard = pl.BlockSpec((1, tr, c), lambda l, i, me_ref: (l, i, 0))
    out = jax.ShapeDtypeStruct((2, r, c), F32)
    return _pcall(
        body, name="adamw_sharded",
        grid_spec=pltpu.PrefetchScalarGridSpec(
            num_scalar_prefetch=1, grid=(2, nr),
            in_specs=[pl.BlockSpec((1, tr, c), lambda l, i, me_ref: (me_ref[0], rows0(l, i), 0)),
                      pl.BlockSpec((N_DEV, tr, c), lambda l, i, me_ref: (0, rows0(l, i), 0)),
                      pl.BlockSpec((1, tr, c), lambda l, i, me_ref: (me_ref[0], rows1(l, i), 0)),
                      pl.BlockSpec((N_DEV, tr, c), lambda l, i, me_ref: (0, rows1(l, i), 0)),
                      shard, shard, shard, pl.BlockSpec(memory_space=pl.ANY)],
            out_specs=[shard, shard, shard, shard]),
        out_shape=[out, out, out, out],
        compiler_params=_params(("arbitrary", "arbitrary")),
    )(me, own0, land0, own1, land1, w, m, v, dep)


def _adamw_small(w, g, m, v):
    def body(w_ref, g_ref, m_ref, v_ref, d_ref, m2_ref, v2_ref):
        delta, m2, v2 = _adamw_math(w_ref[...], g_ref[...], m_ref[...], v_ref[...])
        d_ref[...] = delta
        m2_ref[...] = m2
        v2_ref[...] = v2

    spec = pl.BlockSpec(w.shape, lambda i: (0, 0))
    out = jax.ShapeDtypeStruct(w.shape, F32)
    return _pcall(
        body, name="adamw_small", grid=(1,),
        in_specs=[spec] * 4, out_specs=[spec] * 3, out_shape=[out] * 3,
        compiler_params=_params(("arbitrary",)),
    )(w, g, m, v)


def _pack(arrs):
    flat = jnp.concatenate([a.reshape(-1) for a in arrs])
    n = flat.shape[0]
    rows = -(-n // 1024) * 8
    return jnp.pad(flat, (0, rows * 128 - n)).reshape(rows, 128)


def _unpack(packed, like):
    flat = packed.reshape(-1)
    out, off = [], 0
    for a in like:
        out.append(flat[off:off + a.size].reshape(a.shape))
        off += a.size
    return out


def kernel(x, mem, g_ffn1, w_ffn1_up, w_ffn1_down, g_mix, w_in, conv_w, sinks, g_mem, w_mem_kv, g_grp, w_out, g_ffn2, w_ffn2_up, w_ffn2_down, g_final, loss_target, m_g_ffn1, m_w_ffn1_up, m_w_ffn1_down, m_g_mix, m_w_in, m_conv_w, m_sinks, m_g_mem, m_w_mem_kv, m_g_grp, m_w_out, m_g_ffn2, m_w_ffn2_up, m_w_ffn2_down, m_g_final, v_g_ffn1, v_w_ffn1_up, v_w_ffn1_down, v_g_mix, v_w_in, v_conv_w, v_sinks, v_g_mem, v_w_mem_kv, v_g_grp, v_w_out, v_g_ffn2, v_w_ffn2_up, v_w_ffn2_down, v_g_final):
    depth = g_ffn1.shape[0]
    t, d = x.shape[1], x.shape[2]
    width = max(d, D_MIX)
    me = _slot(*_position())
    conv_shard = conv_w.shape[2]

    xin, memin, tgt = x[0], mem[0], loss_target[0]

    conv_tile = jnp.zeros((depth * 8, 128), F32).at[:, :conv_shard].set(
        jnp.pad(conv_w, ((0, 0), (0, 8 - conv_w.shape[1]), (0, 0))).reshape(depth * 8, conv_shard))
    tr = lambda a: jnp.swapaxes(a, -1, -2)
    bf = lambda a: a.astype(BF16)
    weights = []
    collective_id = 0
    for l in range(depth):
        groups = [[bf(tr(w_ffn1_up[l])), bf(w_ffn1_down[l])] + ([conv_tile] if l == 0 else []),
                  [bf(tr(w_in[l])), bf(w_mem_kv[l]), bf(w_out[l])],
                  [bf(tr(w_ffn2_up[l])), bf(w_ffn2_down[l])]]
        full = []
        for gi, shards in enumerate(groups):
            full.append(_all_gather(shards, f"all_gather_l{l}_g{gi}", collective_id))
            collective_id += 1
        if l == 0:
            conv_full = full[0][2].reshape(N_DEV, depth, 8, 128)[:, :, :3, :conv_shard]
            conv_full = conv_full.transpose(1, 2, 0, 3).reshape(depth, 3, N_DEV * conv_shard)
        weights.append(dict(
            up1=full[0][0].reshape(2, -1, d), dn1=full[0][1].reshape(-1, d),
            win=full[1][0].reshape(D_IN, d), wkv=full[1][1].reshape(d, 2 * D_MEMQ), wout=full[1][2].reshape(D_MIX, d),
            up2=full[2][0].reshape(2, -1, d), dn2=full[2][1].reshape(-1, d)))

    row = lambda a: a.reshape(1, -1)
    bias_tok, bias_key = _bias_tables()

    h = xin
    saved = []
    for l in range(depth):
        wl = weights[l]
        s = dict(h0=h)
        h, s["gu1"], s["n1"] = _ffn_fwd(h, row(g_ffn1[l]), wl["up1"], wl["dn1"])
        s["h1"] = h
        s["p"], s["n_mix"], s["qh"] = _mix_proj_fwd(h, row(g_mix[l]), wl["win"])
        s["mkv"], s["nt_mem"] = _memkv_fwd(memin, row(g_mem[l]), wl["wkv"], s["p"])
        s["y"], s["lse"] = _mix_core_fwd(s["p"], s["qh"], s["mkv"], conv_full[l], row(sinks[l]), bias_tok)
        h, s["mt"] = _mix_out_fwd(s["y"], h, row(g_grp[l]), wl["wout"])
        s["h2"] = h
        h, s["gu2"], s["n2"] = _ffn_fwd(h, row(g_ffn2[l]), wl["up2"], wl["dn2"])
        saved.append(s)

    dh, loss_part, dg_final = _final_loss(h, row(g_final), tgt)

    small = {}
    dep = loss_part

    started = []

    def scatter(names, partials, label):
        state, token = _scatter_start(partials, f"scatter_grads_{label}")
        started.append((names, state))
        return token

    for l in reversed(range(depth)):
        wl, s = weights[l], saved[l]
        dh, agu, dyb, small["g_ffn2", l] = _ffn_bwd_act(dh, s["h2"], row(g_ffn2[l]), s["gu2"], wl["up2"], wl["dn2"], dep)
        ddn2 = _ffn_bwd_w(agu, 2, 1, dyb, agu, f"ffn_bwd_w_down_l{l}_ffn2").reshape(N_DEV, -1, d)
        dup2 = _ffn_bwd_w(agu, 0, 2, s["n2"], ddn2, f"ffn_bwd_w_up_l{l}_ffn2").reshape(N_DEV, -1, d)
        dep = scatter([("w_ffn2_up", l), ("w_ffn2_down", l)], [dup2, ddn2], f"l{l}_ffn2")
        dyconv, doh, delta, dwout, small["g_grp", l] = _mix_out_bwd(dh, s["y"], row(g_grp[l]), wl["wout"], s["mt"], dep)
        dp, dmkv, small["conv_w", l], small["sinks", l] = _mix_core_bwd(
            s["p"], s["qh"], dyconv, doh, delta, s["lse"], s["mkv"], conv_full[l], row(sinks[l]), bias_tok, bias_key)
        dwkv, small["g_mem", l] = _memkv_bwd(dmkv, memin, row(g_mem[l]), wl["wkv"], s["nt_mem"])
        dh, dwin, small["g_mix", l] = _mix_proj_bwd(dp, dh, s["h1"], row(g_mix[l]), wl["win"], s["n_mix"])
        dep = scatter([("w_in", l), ("w_mem_kv", l), ("w_out", l)],
                      [dwin.reshape(N_DEV, -1, d), dwkv.reshape(N_DEV, -1, 2 * D_MEMQ), dwout.reshape(N_DEV, -1, d)],
                      f"l{l}_mix")
        dh, agu, dyb, small["g_ffn1", l] = _ffn_bwd_act(dh, s["h0"], row(g_ffn1[l]), s["gu1"], wl["up1"], wl["dn1"], dep)
        ddn1 = _ffn_bwd_w(agu, 2, 1, dyb, agu, f"ffn_bwd_w_down_l{l}_ffn1").reshape(N_DEV, -1, d)
        if l > 0:
            dup1 = _ffn_bwd_w(agu, 0, 2, s["n1"], ddn1, f"ffn_bwd_w_up_l{l}_ffn1").reshape(N_DEV, -1, d)
            dep = scatter([("w_ffn1_up", l), ("w_ffn1_down", l)], [dup1, ddn1], f"l{l}_ffn1")
        else:
            dep = scatter([("w_ffn1_down", l)], [ddn1], f"l{l}_ffn1_down")
            dup1 = _ffn_bwd_w(agu, 0, 2, s["n1"], dep, f"ffn_bwd_w_up_l{l}_ffn1").reshape(N_DEV, -1, d)
            dep = scatter([("w_ffn1_up", l)], [dup1], f"l{l}_ffn1_up")
    grad_x = dh[None]

    big = {"w_ffn2_up": (w_ffn2_up, m_w_ffn2_up, v_w_ffn2_up, True), "w_ffn2_down": (w_ffn2_down, m_w_ffn2_down, v_w_ffn2_down, False),
           "w_in": (w_in, m_w_in, v_w_in, True), "w_mem_kv": (w_mem_kv, m_w_mem_kv, v_w_mem_kv, False),
           "w_out": (w_out, m_w_out, v_w_out, False), "w_ffn1_up": (w_ffn1_up, m_w_ffn1_up, v_w_ffn1_up, True),
           "w_ffn1_down": (w_ffn1_down, m_w_ffn1_down, v_w_ffn1_down, False)}
    me_index = jnp.reshape(me, (1,)).astype(jnp.int32)
    sharded, landed = {}, {}

    def finish(groups, after):
        for names, state in groups:
            owns, lands = _scatter_wait(state, after)
            for key, own, land in zip(names, owns, lands):
                landed[key] = (own, land)
            after = lands[0]
            for name in dict.fromkeys(n for n, _ in names):
                if name not in sharded and all((name, l) in landed for l in range(depth)):
                    w, m, v, transposed = big[name]
                    fix = tr if transposed else (lambda a: a)
                    res = _adamw_sharded(me_index, [landed[name, l] for l in range(depth)], fix(w), fix(m), fix(v), after)
                    sharded[name] = tuple(fix(r) for r in res)
                    after = res[0]
        return after

    dep = finish(started[:-2], dep)

    def lanes(a):
        return jnp.pad(a, ((0, 0), (0, width - a.shape[1])))

    def first_row(a):
        return lanes(jnp.pad(a, ((0, 8 - a.shape[0]), (0, 0))))

    vec_names = ["g_ffn1", "g_mix", "g_mem", "g_grp", "g_ffn2", "sinks"]
    tiles = [lanes(small[n, l]) for n in vec_names for l in range(depth)]
    tiles += [first_row(small["conv_w", l][k:k + 1]) for l in range(depth) for k in range(3)]
    tiles.append(lanes(dg_final))
    n_real = len(tiles)
    tiles.append(lanes(loss_part))
    tiles.append(lanes(dep[0, :8, :128]))
    tiles += [jnp.zeros((8, width), F32)] * (-len(tiles) % 8)
    summed = _all_reduce_rows(jnp.stack(tiles))
    loss = 0.5 * jnp.sum(summed[n_real]) / d

    def vec(n, wd):
        return jnp.stack([summed[vec_names.index(n) * depth + l, :wd] for l in range(depth)])

    conv_base = len(vec_names) * depth
    conv_grad = jnp.stack([jnp.stack([summed[conv_base + 3 * l + k, :D_CONV] for k in range(3)]) for l in range(depth)])
    grads_small = {
        "g_ffn1": vec("g_ffn1", d), "g_mix": vec("g_mix", d), "g_mem": vec("g_mem", d),
        "g_grp": vec("g_grp", D_MIX), "g_ffn2": vec("g_ffn2", d), "sinks": vec("sinks", N_SWA_HEADS),
        "conv_w": lax.dynamic_slice_in_dim(conv_grad, me * conv_shard, conv_shard, axis=2),
        "g_final": summed[n_real - 1, :d],
    }
    small_w = [("g_ffn1", g_ffn1, m_g_ffn1, v_g_ffn1), ("g_mix", g_mix, m_g_mix, v_g_mix),
               ("conv_w", conv_w, m_conv_w, v_conv_w), ("sinks", sinks, m_sinks, v_sinks),
               ("g_mem", g_mem, m_g_mem, v_g_mem), ("g_grp", g_grp, m_g_grp, v_g_grp),
               ("g_ffn2", g_ffn2, m_g_ffn2, v_g_ffn2), ("g_final", g_final, m_g_final, v_g_final)]
    like = [w for _, w, _, _ in small_w]
    packed = _adamw_small(_pack(like), _pack([grads_small[n] for n, _, _, _ in small_w]),
                          _pack([m for _, _, m, _ in small_w]), _pack([v for _, _, _, v in small_w]))
    small_out = {n: (grads_small[n], dl, m2, v2)
                 for (n, _, _, _), dl, m2, v2 in zip(small_w, *[_unpack(pk, like) for pk in packed])}

    finish(started[-2:], packed[0])

    order = ["g_ffn1", "w_ffn1_up", "w_ffn1_down", "g_mix", "w_in", "conv_w", "sinks", "g_mem", "w_mem_kv", "g_grp",
             "w_out", "g_ffn2", "w_ffn2_up", "w_ffn2_down", "g_final"]
    results = {**sharded, **small_out}
    outs = [loss, grad_x]
    for part in range(4):
        outs += [results[n][part] for n in order]
    return tuple(outs)
```

```python
import numpy as np
import jax
import jax.numpy as jnp
from jax import lax
from jax.experimental import pallas as pl
from jax.experimental.pallas import tpu as pltpu
from jax.experimental.pallas import tpu_sc as plsc

F32 = jnp.float32
BF16 = jnp.bfloat16

N_DEV = 8
EPS = 1e-6
N_SWA_HEADS = 8
N_SWA_KV = 2
SWA_GROUP = N_SWA_HEADS // N_SWA_KV
HEAD_DIM = 64
N_MEM_HEADS = 4
D_CONV = 256
BLOCK = 128
D_SWA = N_SWA_HEADS * HEAD_DIM
D_KV = N_SWA_KV * HEAD_DIM
D_MEMQ = N_MEM_HEADS * HEAD_DIM
D_MIX = D_CONV + D_SWA + D_MEMQ
D_IN = 3 * D_CONV + D_SWA + 2 * D_KV + D_MEMQ
COL_BG, COL_CG, COL_U = 0, D_CONV, 2 * D_CONV
COL_Q = 3 * D_CONV
COL_K = COL_Q + D_SWA
COL_V = COL_K + D_KV
COL_QM = COL_V + D_KV
MIX_GROUPS = ((0, D_CONV), (D_CONV, D_CONV + D_SWA), (D_CONV + D_SWA, D_MIX))
SLOPES = tuple(2.0 ** (-8.0 * (i + 1) / N_SWA_HEADS) for i in range(N_SWA_HEADS))
SCALE = HEAD_DIM ** -0.5
NEG = -1e30

ADAM_LR = 0.001
ADAM_B1 = 0.9
ADAM_B2 = 0.999
ADAM_EPS = 1e-08
ADAM_WD = 0.01
ADAM_STEP = 10

V7X_VMEM_BYTES = 64 * 1024 * 1024
VMEM_LIMIT = (V7X_VMEM_BYTES * 3) // 4
MESH = pl.DeviceIdType.MESH


def _pcall(body, **kw):
    call = pl.pallas_call(body, **kw)
    grid_spec = kw.get("grid_spec")
    specs = grid_spec.in_specs if grid_spec is not None else kw["in_specs"]
    skip = grid_spec.num_scalar_prefetch if grid_spec is not None else 0

    def run(*args):
        args = list(args)
        for idx, spec in enumerate(specs):
            if spec.memory_space in (None, pl.ANY):
                args[skip + idx] = pltpu.with_memory_space_constraint(args[skip + idx], pltpu.HBM)
        return call(*args)

    return run


def _params(sem=None, vmem=VMEM_LIMIT):
    return pltpu.CompilerParams(dimension_semantics=sem, vmem_limit_bytes=vmem)


def _dot(a, b):
    return lax.dot_general(a, b, (((1,), (0,)), ((), ())), preferred_element_type=F32)


def _dot_nt(a, b):
    return lax.dot_general(a, b, (((1,), (1,)), ((), ())), preferred_element_type=F32)


def _dot_tn(a, b):
    return lax.dot_general(a, b, (((0,), (0,)), ((), ())), preferred_element_type=F32)


def _rstd(x):
    return lax.rsqrt(jnp.mean(x * x, axis=-1, keepdims=True) + EPS)


def _sigmoid(x):
    return 1.0 / (1.0 + jnp.exp(-x))


def _sum8(x):
    r, w = x.shape
    return jnp.sum(x.reshape(r // 8, 8, w), axis=0)


def _tok_block(t, rows=512):
    return min(rows, t)


def _feat_block(f):
    return f // (N_DEV // 2)


def _ffn_fwd(h, g, wup_t, wdn):
    t, d = h.shape
    f = wdn.shape[0]
    tm, tf = _tok_block(t), _feat_block(f)
    ni, nj = t // tm, f // tf

    def body(h_ref, g_ref, wup_ref, wdn_ref, ho_ref, gu_ref, n_ref, nt_ref, acc_ref):
        j = pl.program_id(1)

        @pl.when(j == 0)
        def _():
            hh = h_ref[...]
            n = hh * _rstd(hh) * g_ref[...]
            n_ref[...] = n.astype(BF16)
            nt_ref[...] = n.T.astype(BF16)
            acc_ref[...] = jnp.zeros_like(acc_ref)

        nt = nt_ref[...]
        gate = _dot(wup_ref[0], nt)
        up = _dot(wup_ref[1], nt)
        gu_ref[0] = gate.astype(BF16)
        gu_ref[1] = up.astype(BF16)
        a = gate * _sigmoid(gate) * up
        acc_ref[...] += _dot_tn(a.astype(BF16), wdn_ref[...])

        @pl.when(j == nj - 1)
        def _():
            ho_ref[...] = h_ref[...] + 0.5 * acc_ref[...]

    return _pcall(
        body, name="ffn_fwd", grid=(ni, nj),
        in_specs=[pl.BlockSpec((tm, d), lambda i, j: (i, 0)),
                  pl.BlockSpec((1, d), lambda i, j: (0, 0)),
                  pl.BlockSpec((2, tf, d), lambda i, j: (0, j, 0)),
                  pl.BlockSpec((tf, d), lambda i, j: (j, 0))],
        out_specs=[pl.BlockSpec((tm, d), lambda i, j: (i, 0)),
                   pl.BlockSpec((2, tf, tm), lambda i, j: (0, j, i)),
                   pl.BlockSpec((tm, d), lambda i, j: (i, 0))],
        out_shape=[jax.ShapeDtypeStruct((t, d), F32),
                   jax.ShapeDtypeStruct((2, f, t), BF16),
                   jax.ShapeDtypeStruct((t, d), BF16)],
        scratch_shapes=[pltpu.VMEM((d, tm), BF16), pltpu.VMEM((tm, d), F32)],
        compiler_params=_params(("parallel", "arbitrary")),
    )(h, g, wup_t, wdn)


def _ffn_bwd_act(dho, h, g, gu, wup_t, wdn, dep):
    t, d = h.shape
    f = wdn.shape[0]
    tm, tf = _tok_block(t), _feat_block(f)
    ni, nj = t // tm, f // tf

    def body(dho_ref, h_ref, g_ref, gu_ref, wup_ref, wdn_ref, dep_ref, dh_ref, agu_ref, dyb_ref, dg_ref, dyt_ref, acc_ref):
        i = pl.program_id(0)
        j = pl.program_id(1)

        @pl.when(j == 0)
        def _():
            dy0 = 0.5 * dho_ref[...]
            dyb_ref[...] = dy0.astype(BF16)
            dyt_ref[...] = dy0.T.astype(BF16)
            acc_ref[...] = jnp.zeros_like(acc_ref)

        da = _dot(wdn_ref[...], dyt_ref[...])
        gate = gu_ref[0].astype(F32)
        up = gu_ref[1].astype(F32)
        sg = _sigmoid(gate)
        silu = gate * sg
        dgate = (da * up * (sg * (1.0 + gate * (1.0 - sg)))).astype(BF16)
        dup = (da * silu).astype(BF16)
        agu_ref[0] = dgate
        agu_ref[1] = dup
        agu_ref[2] = (silu * up).astype(BF16)
        acc_ref[...] += _dot_tn(dgate, wup_ref[0])
        acc_ref[...] += _dot_tn(dup, wup_ref[1])

        @pl.when(j == nj - 1)
        def _():
            hh = h_ref[...]
            r = _rstd(hh)
            xhat = hh * r
            dnf = acc_ref[...]
            dxh = dnf * g_ref[...]
            dh_ref[...] = dho_ref[...] + r * (dxh - xhat * jnp.mean(dxh * xhat, axis=-1, keepdims=True))
            part = _sum8(dnf * xhat)

            @pl.when(i == 0)
            def _():
                dg_ref[...] = part

            @pl.when(i > 0)
            def _():
                dg_ref[...] += part

    return _pcall(
        body, name="ffn_bwd_act", grid=(ni, nj),
        in_specs=[pl.BlockSpec((tm, d), lambda i, j: (i, 0)),
                  pl.BlockSpec((tm, d), lambda i, j: (i, 0)),
                  pl.BlockSpec((1, d), lambda i, j: (0, 0)),
                  pl.BlockSpec((2, tf, tm), lambda i, j: (0, j, i)),
                  pl.BlockSpec((2, tf, d), lambda i, j: (0, j, 0)),
                  pl.BlockSpec((tf, d), lambda i, j: (j, 0)),
                  pl.BlockSpec(memory_space=pl.ANY)],
        out_specs=[pl.BlockSpec((tm, d), lambda i, j: (i, 0)),
                   pl.BlockSpec((3, tf, tm), lambda i, j: (0, j, i)),
                   pl.BlockSpec((tm, d), lambda i, j: (i, 0)),
                   pl.BlockSpec((8, d), lambda i, j: (0, 0))],
        out_shape=[jax.ShapeDtypeStruct((t, d), F32),
                   jax.ShapeDtypeStruct((3, f, t), BF16),
                   jax.ShapeDtypeStruct((t, d), BF16),
                   jax.ShapeDtypeStruct((8, d), F32)],
        scratch_shapes=[pltpu.VMEM((d, tm), BF16), pltpu.VMEM((tm, d), F32)],
        compiler_params=_params(("arbitrary", "arbitrary")),
    )(dho, h, g, gu, wup_t, wdn, dep)


def _ffn_bwd_w(agu, first, count, rhs, dep, name):
    _, f, t = agu.shape
    d = rhs.shape[1]
    tm, tf = _tok_block(t, 2048), _feat_block(f)
    ni, nj = t // tm, f // tf

    def body(lhs_ref, rhs_ref, dep_ref, dw_ref, acc_ref):
        i = pl.program_id(1)
        @pl.when(i == 0)
        def _():
            acc_ref[...] = jnp.zeros_like(acc_ref)

        rb = rhs_ref[...]
        for k in range(count):
            acc_ref[k] += _dot(lhs_ref[k], rb)

        @pl.when(i == ni - 1)
        def _():
            dw_ref[...] = acc_ref[...].astype(BF16)

    return _pcall(
        body, name=name, grid=(nj, ni),
        in_specs=[pl.BlockSpec((count, tf, tm), lambda j, i: (first // count, j, i)),
                  pl.BlockSpec((tm, d), lambda j, i: (i, 0)),
                  pl.BlockSpec(memory_space=pl.ANY)],
        out_specs=pl.BlockSpec((count, tf, d), lambda j, i: (0, j, 0)),
        out_shape=jax.ShapeDtypeStruct((count, f, d), BF16),
        scratch_shapes=[pltpu.VMEM((count, tf, d), F32)],
        compiler_params=_params(("parallel", "arbitrary")),
    )(agu, rhs, dep)


N_HEADS = N_SWA_HEADS + N_MEM_HEADS


def _q_col(hd):
    return COL_Q + HEAD_DIM * hd if hd < N_SWA_HEADS else COL_QM + HEAD_DIM * (hd - N_SWA_HEADS)


def _mix_proj_fwd(h, g, win_t):
    t, d = h.shape
    tm = _tok_block(t)

    def body(h_ref, g_ref, win_ref, p_ref, n_ref, qh_ref):
        hh = h_ref[...]
        n = (hh * _rstd(hh) * g_ref[...]).astype(BF16)
        n_ref[...] = n
        proj = _dot_nt(n, win_ref[...])
        p_ref[...] = proj.astype(BF16)
        for hd in range(N_HEADS):
            c0 = _q_col(hd)
            qh_ref[hd] = (proj[:, c0:c0 + HEAD_DIM] * SCALE).astype(BF16)

    return _pcall(
        body, name="mix_proj_fwd", grid=(t // tm,),
        in_specs=[pl.BlockSpec((tm, d), lambda i: (i, 0)),
                  pl.BlockSpec((1, d), lambda i: (0, 0)),
                  pl.BlockSpec((D_IN, d), lambda i: (0, 0))],
        out_specs=[pl.BlockSpec((tm, D_IN), lambda i: (i, 0)),
                   pl.BlockSpec((tm, d), lambda i: (i, 0)),
                   pl.BlockSpec((N_HEADS, tm, HEAD_DIM), lambda i: (0, i, 0))],
        out_shape=[jax.ShapeDtypeStruct((t, D_IN), BF16), jax.ShapeDtypeStruct((t, d), BF16),
                   jax.ShapeDtypeStruct((N_HEADS, t, HEAD_DIM), BF16)],
        compiler_params=_params(("parallel",)),
    )(h, g, win_t)


def _memkv_fwd(mem, g, wkv, dep):
    m, d = mem.shape

    def body(mem_ref, g_ref, w_ref, dep_ref, mkv_ref, nt_ref):
        mm = mem_ref[...]
        n = mm * _rstd(mm) * g_ref[...]
        nt_ref[...] = n.T.astype(BF16)
        mkv_ref[...] = _dot(n.astype(BF16), w_ref[...]).astype(BF16)

    return _pcall(
        body, name="memkv_fwd", grid=(1,),
        in_specs=[pl.BlockSpec((m, d), lambda i: (0, 0)),
                  pl.BlockSpec((1, d), lambda i: (0, 0)),
                  pl.BlockSpec((d, 2 * D_MEMQ), lambda i: (0, 0)),
                  pl.BlockSpec(memory_space=pl.ANY)],
        out_specs=[pl.BlockSpec((m, 2 * D_MEMQ), lambda i: (0, 0)),
                   pl.BlockSpec((d, m), lambda i: (0, 0))],
        out_shape=[jax.ShapeDtypeStruct((m, 2 * D_MEMQ), BF16), jax.ShapeDtypeStruct((d, m), BF16)],
        compiler_params=_params(("arbitrary",)),
    )(mem, g, wkv, dep)


def _memkv_bwd(dmkv, mem, g, wkv, nt):
    m, d = mem.shape

    def body(dmkv_ref, mem_ref, g_ref, w_ref, nt_ref, dw_ref, dg_ref):
        db = dmkv_ref[...].astype(BF16)
        dw_ref[...] = _dot(nt_ref[...], db).astype(BF16)
        dn = _dot_nt(db, w_ref[...])
        mm = mem_ref[...]
        dg_ref[...] = _sum8(dn * (mm * _rstd(mm)))

    return _pcall(
        body, name="memkv_bwd", grid=(1,),
        in_specs=[pl.BlockSpec((m, 2 * D_MEMQ), lambda i: (0, 0)),
                  pl.BlockSpec((m, d), lambda i: (0, 0)),
                  pl.BlockSpec((1, d), lambda i: (0, 0)),
                  pl.BlockSpec((d, 2 * D_MEMQ), lambda i: (0, 0)),
                  pl.BlockSpec((d, m), lambda i: (0, 0))],
        out_specs=[pl.BlockSpec((d, 2 * D_MEMQ), lambda i: (0, 0)),
                   pl.BlockSpec((8, d), lambda i: (0, 0))],
        out_shape=[jax.ShapeDtypeStruct((d, 2 * D_MEMQ), BF16), jax.ShapeDtypeStruct((8, d), F32)],
        compiler_params=_params(("arbitrary",)),
    )(dmkv, mem, g, wkv, nt)


def _shift_rows(v, k, edge_rows, row):
    out = pltpu.roll(v, k, 0)
    for r in range(k):
        out = jnp.where(row == r, edge_rows[r], out)
    return out


def _shift_rows_up(v, k, edge_rows, row):
    n = v.shape[0]
    out = pltpu.roll(v, n - k, 0)
    for r in range(k):
        out = jnp.where(row == n - k + r, edge_rows[r], out)
    return out


GROUP_ROWS = SWA_GROUP * BLOCK
BIAS_CUR, BIAS_PREV, BIAS_NONE = 0, 1, 2


def _bias_tables():
    tq = np.arange(BLOCK)[:, None]
    sk = np.arange(BLOCK)[None, :]
    slopes = np.asarray(SLOPES, np.float32)[:, None, None]
    cur = np.where(tq >= sk, -slopes * (tq - sk).astype(np.float32), NEG)
    prev = np.where(sk > tq, -slopes * (tq + BLOCK - sk).astype(np.float32), NEG)
    none = np.full_like(cur, NEG)
    tok = np.stack([cur, prev, none]).astype(np.float32).reshape(3, N_SWA_KV, GROUP_ROWS, BLOCK)
    return jnp.asarray(tok), jnp.asarray(np.ascontiguousarray(tok.transpose(0, 1, 3, 2)))


def _head_cols(hd):
    return D_CONV + HEAD_DIM * hd


def _stack_cols(ref, heads):
    return jnp.concatenate([ref[:, hd:hd + 1] for hd in heads], axis=0)


def _mix_core_fwd(p, qh, mkv, convw, sinks, bias_tok):
    t = p.shape[0]
    m = mkv.shape[0]
    nb = t // BLOCK

    def body(sk_ref, pc_ref, pkv_ref, ppc_ref, ppu_ref, qh_ref, mkv_ref, cw_ref, bc_ref, bp_ref, y_ref, l_ref):
        i = pl.program_id(0)
        prevf = (i > 0).astype(F32)
        row = lax.broadcasted_iota(jnp.int32, (BLOCK, D_CONV), 0)

        bg = pc_ref[:, COL_BG:COL_BG + D_CONV].astype(F32)
        cg = pc_ref[:, COL_CG:COL_CG + D_CONV].astype(F32)
        u = pc_ref[:, COL_U:COL_U + D_CONV].astype(F32)
        vv = cg * u
        pvv = ppc_ref[...].astype(F32) * ppu_ref[...].astype(F32) * prevf
        vv1 = _shift_rows(vv, 1, [pvv[15:16]], row)
        vv2 = _shift_rows(vv, 2, [pvv[14:15], pvv[15:16]], row)
        w = cw_ref[...]
        y_ref[:, 0:D_CONV] = bg * (w[0:1] * vv2 + w[1:2] * vv1 + w[2:3] * vv)

        lane = lax.broadcasted_iota(jnp.int32, (BLOCK, 128), 1)
        lse_all = jnp.zeros((BLOCK, 128), F32)
        for kv in range(N_SWA_KV):
            heads = range(kv * SWA_GROUP, (kv + 1) * SWA_GROUP)
            kc = pc_ref[:, COL_K + HEAD_DIM * kv:COL_K + HEAD_DIM * (kv + 1)]
            vc = pc_ref[:, COL_V + HEAD_DIM * kv:COL_V + HEAD_DIM * (kv + 1)]
            kp = pkv_ref[:, HEAD_DIM * kv:HEAD_DIM * (kv + 1)]
            vp = pkv_ref[:, D_KV + HEAD_DIM * kv:D_KV + HEAD_DIM * (kv + 1)]
            qg = qh_ref[kv * SWA_GROUP:(kv + 1) * SWA_GROUP].reshape(GROUP_ROWS, HEAD_DIM)
            sc = _dot_nt(qg, kc) + bc_ref[0, kv]
            sp = _dot_nt(qg, kp) + bp_ref[0, kv]
            sink = jnp.concatenate([jnp.full((BLOCK, 1), sk_ref[0, hd], F32) for hd in heads], axis=0)
            mx = jnp.maximum(jnp.max(jnp.maximum(sc, sp), axis=-1, keepdims=True), sink)
            ec = jnp.exp(sc - mx)
            ep = jnp.exp(sp - mx)
            den = jnp.sum(ec + ep, axis=-1, keepdims=True) + jnp.exp(sink - mx)
            o = (_dot(ec.astype(BF16), vc) + _dot(ep.astype(BF16), vp)) / den
            lse = mx + jnp.log(den)
            for gi, hd in enumerate(heads):
                rows = slice(gi * BLOCK, (gi + 1) * BLOCK)
                y_ref[:, _head_cols(hd):_head_cols(hd) + HEAD_DIM] = o[rows]
                lse_all = jnp.where(lane == hd, lse[rows], lse_all)

        for hm in range(N_MEM_HEADS):
            hd = N_SWA_HEADS + hm
            mk = mkv_ref[:, HEAD_DIM * hm:HEAD_DIM * (hm + 1)]
            mv = mkv_ref[:, D_MEMQ + HEAD_DIM * hm:D_MEMQ + HEAD_DIM * (hm + 1)]
            s = _dot_nt(qh_ref[hd], mk)
            mx = jnp.max(s, axis=-1, keepdims=True)
            e = jnp.exp(s - mx)
            den = jnp.sum(e, axis=-1, keepdims=True)
            y_ref[:, _head_cols(hd):_head_cols(hd) + HEAD_DIM] = _dot(e.astype(BF16), mv) / den
            lse_all = jnp.where(lane == hd, mx + jnp.log(den), lse_all)
        l_ref[...] = lse_all

    kv_col = COL_K // (2 * D_KV)
    bias_block = (1, N_SWA_KV, GROUP_ROWS, BLOCK)
    return _pcall(
        body, name="mix_core_fwd", grid=(nb,),
        in_specs=[pl.BlockSpec(memory_space=pltpu.SMEM),
                  pl.BlockSpec((BLOCK, D_IN), lambda i: (i, 0)),
                  pl.BlockSpec((BLOCK, 2 * D_KV), lambda i: (jnp.maximum(i - 1, 0), kv_col)),
                  pl.BlockSpec((16, D_CONV), lambda i: (jnp.maximum(i * (BLOCK // 16) - 1, 0), COL_CG // D_CONV)),
                  pl.BlockSpec((16, D_CONV), lambda i: (jnp.maximum(i * (BLOCK // 16) - 1, 0), COL_U // D_CONV)),
                  pl.BlockSpec((N_HEADS, BLOCK, HEAD_DIM), lambda i: (0, i, 0)),
                  pl.BlockSpec((m, 2 * D_MEMQ), lambda i: (0, 0)),
                  pl.BlockSpec((3, D_CONV), lambda i: (0, 0)),
                  pl.BlockSpec(bias_block, lambda i: (BIAS_CUR, 0, 0, 0)),
                  pl.BlockSpec(bias_block, lambda i: (jnp.where(i == 0, BIAS_NONE, BIAS_PREV), 0, 0, 0))],
        out_specs=[pl.BlockSpec((BLOCK, D_MIX), lambda i: (i, 0)),
                   pl.BlockSpec((BLOCK, 128), lambda i: (i, 0))],
        out_shape=[jax.ShapeDtypeStruct((t, D_MIX), F32), jax.ShapeDtypeStruct((t, 128), F32)],
        compiler_params=_params(("parallel",)),
    )(sinks, p, p, p, p, qh, mkv, convw, bias_tok, bias_tok)


def _mix_core_bwd(p, qh, dyconv, doh, delta, lse, mkv, convw, sinks, bias_tok, bias_key):
    t = p.shape[0]
    m = mkv.shape[0]
    nb = t // BLOCK

    def body(sk_ref, pc_ref, pkv_ref, ppc_ref, ppu_ref, pnb_ref, dyc_ref, dyn_ref, qc_ref, qn_ref, doc_ref, don_ref,
             dlc_ref, dln_ref, lc_ref, ln_ref, mkv_ref, cw_ref, bp_ref, bct_ref, bnt_ref,
             dp_ref, dmkv_ref, dcw_ref, dsk_ref):
        i = pl.program_id(0)
        prevf = (i > 0).astype(F32)
        nextf = (i < nb - 1).astype(F32)
        row = lax.broadcasted_iota(jnp.int32, (BLOCK, D_CONV), 0)

        @pl.when(i == 0)
        def _():
            dmkv_ref[...] = jnp.zeros_like(dmkv_ref)
            dcw_ref[...] = jnp.zeros_like(dcw_ref)
            dsk_ref[...] = jnp.zeros_like(dsk_ref)

        bg = pc_ref[:, COL_BG:COL_BG + D_CONV].astype(F32)
        cg = pc_ref[:, COL_CG:COL_CG + D_CONV].astype(F32)
        u = pc_ref[:, COL_U:COL_U + D_CONV].astype(F32)
        vv = cg * u
        pvv = ppc_ref[...].astype(F32) * ppu_ref[...].astype(F32) * prevf
        vv1 = _shift_rows(vv, 1, [pvv[15:16]], row)
        vv2 = _shift_rows(vv, 2, [pvv[14:15], pvv[15:16]], row)
        w = cw_ref[...]
        yconv = w[0:1] * vv2 + w[1:2] * vv1 + w[2:3] * vv
        dyo = dyc_ref[...]
        dyc = dyo * bg
        nxt = dyn_ref[...] * pnb_ref[...].astype(F32) * nextf
        d1 = _shift_rows_up(dyc, 1, [nxt[0:1]], row)
        d2 = _shift_rows_up(dyc, 2, [nxt[0:1], nxt[1:2]], row)
        dvv = w[2:3] * dyc + w[1:2] * d1 + w[0:1] * d2
        dp_ref[:, COL_BG:COL_BG + D_CONV] = (dyo * yconv).astype(BF16)
        dp_ref[:, COL_CG:COL_CG + D_CONV] = (dvv * u).astype(BF16)
        dp_ref[:, COL_U:COL_U + D_CONV] = (dvv * cg).astype(BF16)
        dcw_ref[0:1, :] += jnp.sum(dyc * vv2, axis=0, keepdims=True)
        dcw_ref[1:2, :] += jnp.sum(dyc * vv1, axis=0, keepdims=True)
        dcw_ref[2:3, :] += jnp.sum(dyc * vv, axis=0, keepdims=True)

        lse_t, dl_t = lc_ref[...].T, dlc_ref[...].T
        lse_nt, dl_nt = ln_ref[...].T, dln_ref[...].T

        def stack_rows(tile_t, heads):
            return jnp.concatenate([tile_t[hd:hd + 1, :] for hd in heads], axis=1)

        lane8 = jnp.where(lax.broadcasted_iota(jnp.int32, (8, 128), 0) == 0,
                          lax.broadcasted_iota(jnp.int32, (8, 128), 1), -1)
        dsk = jnp.zeros((8, 128), F32)
        for kv in range(N_SWA_KV):
            heads = range(kv * SWA_GROUP, (kv + 1) * SWA_GROUP)
            kc = pc_ref[:, COL_K + HEAD_DIM * kv:COL_K + HEAD_DIM * (kv + 1)]
            vc = pc_ref[:, COL_V + HEAD_DIM * kv:COL_V + HEAD_DIM * (kv + 1)]
            kp = pkv_ref[:, HEAD_DIM * kv:HEAD_DIM * (kv + 1)]
            vp = pkv_ref[:, D_KV + HEAD_DIM * kv:D_KV + HEAD_DIM * (kv + 1)]
            qg = qc_ref[kv * SWA_GROUP:(kv + 1) * SWA_GROUP].reshape(GROUP_ROWS, HEAD_DIM)
            dog = doc_ref[kv * SWA_GROUP:(kv + 1) * SWA_GROUP].reshape(GROUP_ROWS, HEAD_DIM)
            qn = qn_ref[kv * SWA_GROUP:(kv + 1) * SWA_GROUP].reshape(GROUP_ROWS, HEAD_DIM)
            don = don_ref[kv * SWA_GROUP:(kv + 1) * SWA_GROUP].reshape(GROUP_ROWS, HEAD_DIM)
            lse_col, dl_col = _stack_cols(lc_ref, heads), _stack_cols(dlc_ref, heads)
            pp_ = jnp.exp(_dot_nt(qg, kp) + bp_ref[0, kv] - lse_col)
            dsp = (pp_ * (_dot_nt(dog, vp) - dl_col)).astype(BF16)
            dq = _dot(dsp, kp)
            pt = jnp.exp(_dot_nt(kc, qg) + bct_ref[0, kv] - stack_rows(lse_t, heads))
            dst = (pt * (_dot_nt(vc, dog) - stack_rows(dl_t, heads))).astype(BF16)
            dv = _dot(pt.astype(BF16), dog)
            dk = _dot(dst, qg)
            dq = dq + _dot_tn(dst, kc)
            ptn = jnp.exp(_dot_nt(kc, qn) + bnt_ref[0, kv] - stack_rows(lse_nt, heads))
            dstn = (ptn * (_dot_nt(vc, don) - stack_rows(dl_nt, heads))).astype(BF16)
            dv = dv + _dot(ptn.astype(BF16), don)
            dk = dk + _dot(dstn, qn)
            dp_ref[:, COL_K + HEAD_DIM * kv:COL_K + HEAD_DIM * (kv + 1)] = dk.astype(BF16)
            dp_ref[:, COL_V + HEAD_DIM * kv:COL_V + HEAD_DIM * (kv + 1)] = dv.astype(BF16)
            sink = jnp.concatenate([jnp.full((BLOCK, 1), sk_ref[0, hd], F32) for hd in heads], axis=0)
            sink_term = jnp.exp(sink - lse_col) * dl_col
            for gi, hd in enumerate(heads):
                rows = slice(gi * BLOCK, (gi + 1) * BLOCK)
                dp_ref[:, _q_col(hd):_q_col(hd) + HEAD_DIM] = (dq[rows] * SCALE).astype(BF16)
                dsk = dsk + jnp.where(lane8 == hd, -jnp.sum(sink_term[rows], axis=0, keepdims=True), 0.0)
        dsk_ref[...] += dsk

        for hm in range(N_MEM_HEADS):
            hd = N_SWA_HEADS + hm
            qm, dom = qc_ref[hd], doc_ref[hd]
            mk = mkv_ref[:, HEAD_DIM * hm:HEAD_DIM * (hm + 1)]
            mv = mkv_ref[:, D_MEMQ + HEAD_DIM * hm:D_MEMQ + HEAD_DIM * (hm + 1)]
            pt = jnp.exp(_dot_nt(mk, qm) - lse_t[hd:hd + 1, :])
            dst = (pt * (_dot_nt(mv, dom) - dl_t[hd:hd + 1, :])).astype(BF16)
            dp_ref[:, _q_col(hd):_q_col(hd) + HEAD_DIM] = (_dot_tn(dst, mk) * SCALE).astype(BF16)
            dmkv_ref[:, HEAD_DIM * hm:HEAD_DIM * (hm + 1)] += _dot(dst, qm)
            dmkv_ref[:, D_MEMQ + HEAD_DIM * hm:D_MEMQ + HEAD_DIM * (hm + 1)] += _dot(pt.astype(BF16), dom)

    cur = lambda i: (i, 0)
    const = lambda i: (0, 0)
    rows16 = BLOCK // 16
    last16 = t // 16 - 1
    before = lambda col: (lambda i: (jnp.maximum(i * rows16 - 1, 0), col))
    after = lambda i: (jnp.minimum((i + 1) * rows16, last16), 0)
    heads_cur = lambda i: (0, i, 0)
    heads_next = lambda i: (0, jnp.minimum(i + 1, nb - 1), 0)
    stat_next = lambda i: (jnp.minimum(i + 1, nb - 1), 0)
    tok_block = (1, N_SWA_KV, GROUP_ROWS, BLOCK)
    key_block = (1, N_SWA_KV, BLOCK, GROUP_ROWS)
    head_block = (N_HEADS, BLOCK, HEAD_DIM)
    return _pcall(
        body, name="mix_core_bwd", grid=(nb,),
        in_specs=[pl.BlockSpec(memory_space=pltpu.SMEM),
                  pl.BlockSpec((BLOCK, D_IN), cur),
                  pl.BlockSpec((BLOCK, 2 * D_KV), lambda i: (jnp.maximum(i - 1, 0), COL_K // (2 * D_KV))),
                  pl.BlockSpec((16, D_CONV), before(COL_CG // D_CONV)),
                  pl.BlockSpec((16, D_CONV), before(COL_U // D_CONV)),
                  pl.BlockSpec((16, D_CONV), after),
                  pl.BlockSpec((BLOCK, D_CONV), cur),
                  pl.BlockSpec((16, D_CONV), after),
                  pl.BlockSpec(head_block, heads_cur), pl.BlockSpec(head_block, heads_next),
                  pl.BlockSpec(head_block, heads_cur), pl.BlockSpec(head_block, heads_next),
                  pl.BlockSpec((BLOCK, 128), cur), pl.BlockSpec((BLOCK, 128), stat_next),
                  pl.BlockSpec((BLOCK, 128), cur), pl.BlockSpec((BLOCK, 128), stat_next),
                  pl.BlockSpec((m, 2 * D_MEMQ), const),
                  pl.BlockSpec((3, D_CONV), const),
                  pl.BlockSpec(tok_block, lambda i: (jnp.where(i == 0, BIAS_NONE, BIAS_PREV), 0, 0, 0)),
                  pl.BlockSpec(key_block, lambda i: (BIAS_CUR, 0, 0, 0)),
                  pl.BlockSpec(key_block, lambda i: (jnp.where(i == nb - 1, BIAS_NONE, BIAS_PREV), 0, 0, 0))],
        out_specs=[pl.BlockSpec((BLOCK, D_IN), cur),
                   pl.BlockSpec((m, 2 * D_MEMQ), const),
                   pl.BlockSpec((8, D_CONV), const),
                   pl.BlockSpec((8, 128), const)],
        out_shape=[jax.ShapeDtypeStruct((t, D_IN), BF16),
                   jax.ShapeDtypeStruct((m, 2 * D_MEMQ), F32),
                   jax.ShapeDtypeStruct((8, D_CONV), F32),
                   jax.ShapeDtypeStruct((8, 128), F32)],
        compiler_params=_params(("arbitrary",)),
    )(sinks, p, p, p, p, p, dyconv, dyconv, qh, qh, doh, doh, delta, delta, lse, lse, mkv, convw,
      bias_tok, bias_key, bias_key)


def _group_norms(y):
    out = []
    for a, b in MIX_GROUPS:
        ys = y[:, a:b]
        r = _rstd(ys)
        out.append((ys * r, r))
    return out


def _mix_out_fwd(y, h, g, wout):
    t, d = h.shape
    tm = _tok_block(t)

    def body(y_ref, h_ref, g_ref, w_ref, ho_ref, mt_ref):
        yhat = jnp.concatenate([yh for yh, _ in _group_norms(y_ref[...])], axis=-1)
        mixed = yhat * g_ref[...]
        mt_ref[...] = mixed.T.astype(BF16)
        ho_ref[...] = h_ref[...] + _dot(mixed.astype(BF16), w_ref[...])

    return _pcall(
        body, name="mix_out_fwd", grid=(t // tm,),
        in_specs=[pl.BlockSpec((tm, D_MIX), lambda i: (i, 0)),
                  pl.BlockSpec((tm, d), lambda i: (i, 0)),
                  pl.BlockSpec((1, D_MIX), lambda i: (0, 0)),
                  pl.BlockSpec((D_MIX, d), lambda i: (0, 0))],
        out_specs=[pl.BlockSpec((tm, d), lambda i: (i, 0)),
                   pl.BlockSpec((D_MIX, tm), lambda i: (0, i))],
        out_shape=[jax.ShapeDtypeStruct((t, d), F32), jax.ShapeDtypeStruct((D_MIX, t), BF16)],
        compiler_params=_params(("parallel",)),
    )(y, h, g, wout)


def _head_indicator():
    ind = np.zeros((D_MIX, 128), np.float32)
    for hd in range(N_HEADS):
        ind[_head_cols(hd):_head_cols(hd) + HEAD_DIM, hd] = 1.0
    return jnp.asarray(ind, BF16)


def _mix_out_bwd(dho, y, g, wout, mt, dep):
    t, d = dho.shape
    tm = _tok_block(t)
    ni = t // tm

    def body(dho_ref, y_ref, g_ref, w_ref, mt_ref, ind_ref, dep_ref, dyc_ref, doh_ref, dl_ref, dw_ref, dg_ref, acc_ref):
        i = pl.program_id(0)
        dhb = dho_ref[...].astype(BF16)
        dm = _dot_nt(dhb, w_ref[...])
        pw = _dot(mt_ref[...], dhb)
        gg = g_ref[...]
        yy = y_ref[...]
        dys = []
        dgs = []
        for (a, b), (yhat, r) in zip(MIX_GROUPS, _group_norms(yy)):
            dmg = dm[:, a:b]
            dgs.append(_sum8(dmg * yhat))
            dyh = dmg * gg[:, a:b]
            dys.append(r * (dyh - yhat * jnp.mean(dyh * yhat, axis=-1, keepdims=True)))
        dy = jnp.concatenate(dys, axis=-1)
        dyc_ref[...] = dy[:, 0:D_CONV]
        for hd in range(N_HEADS):
            doh_ref[hd] = dy[:, _head_cols(hd):_head_cols(hd) + HEAD_DIM].astype(BF16)
        prod = dy * yy
        hi = prod.astype(BF16)
        lo = (prod - hi.astype(F32)).astype(BF16)
        dl_ref[...] = _dot(hi, ind_ref[...]) + _dot(lo, ind_ref[...])
        part = jnp.concatenate(dgs, axis=-1)

        @pl.when(i == 0)
        def _():
            acc_ref[...] = pw
            dg_ref[...] = part

        @pl.when(i > 0)
        def _():
            acc_ref[...] += pw
            dg_ref[...] += part

        @pl.when(i == ni - 1)
        def _():
            dw_ref[...] = acc_ref[...].astype(BF16)

    return _pcall(
        body, name="mix_out_bwd", grid=(ni,),
        in_specs=[pl.BlockSpec((tm, d), lambda i: (i, 0)),
                  pl.BlockSpec((tm, D_MIX), lambda i: (i, 0)),
                  pl.BlockSpec((1, D_MIX), lambda i: (0, 0)),
                  pl.BlockSpec((D_MIX, d), lambda i: (0, 0)),
                  pl.BlockSpec((D_MIX, tm), lambda i: (0, i)),
                  pl.BlockSpec((D_MIX, 128), lambda i: (0, 0)),
                  pl.BlockSpec(memory_space=pl.ANY)],
        out_specs=[pl.BlockSpec((tm, D_CONV), lambda i: (i, 0)),
                   pl.BlockSpec((N_HEADS, tm, HEAD_DIM), lambda i: (0, i, 0)),
                   pl.BlockSpec((tm, 128), lambda i: (i, 0)),
                   pl.BlockSpec((D_MIX, d), lambda i: (0, 0)),
                   pl.BlockSpec((8, D_MIX), lambda i: (0, 0))],
        out_shape=[jax.ShapeDtypeStruct((t, D_CONV), F32),
                   jax.ShapeDtypeStruct((N_HEADS, t, HEAD_DIM), BF16),
                   jax.ShapeDtypeStruct((t, 128), F32),
                   jax.ShapeDtypeStruct((D_MIX, d), BF16),
                   jax.ShapeDtypeStruct((8, D_MIX), F32)],
        scratch_shapes=[pltpu.VMEM((D_MIX, d), F32)],
        compiler_params=_params(("arbitrary",)),
    )(dho, y, g, wout, mt, _head_indicator(), dep)


def _mix_proj_bwd(dp, dho, h, g, win_t, n):
    t, d = h.shape
    tm = _tok_block(t)
    ni = t // tm

    def body(dp_ref, dho_ref, h_ref, g_ref, w_ref, n_ref, dh_ref, dw_ref, dg_ref, acc_ref):
        i = pl.program_id(0)
        dpb = dp_ref[...]
        dn = _dot(dpb, w_ref[...])

        @pl.when(i == 0)
        def _():
            acc_ref[...] = jnp.zeros_like(acc_ref)

        acc_ref[...] += _dot_tn(dpb, n_ref[...])
        hh = h_ref[...]
        r = _rstd(hh)
        xhat = hh * r
        dxh = dn * g_ref[...]
        dh_ref[...] = dho_ref[...] + r * (dxh - xhat * jnp.mean(dxh * xhat, axis=-1, keepdims=True))
        part = _sum8(dn * xhat)

        @pl.when(i == 0)
        def _():
            dg_ref[...] = part

        @pl.when(i > 0)
        def _():
            dg_ref[...] += part

        @pl.when(i == ni - 1)
        def _():
            dw_ref[...] = acc_ref[...].astype(BF16)

    return _pcall(
        body, name="mix_proj_bwd", grid=(ni,),
        in_specs=[pl.BlockSpec((tm, D_IN), lambda i: (i, 0)),
                  pl.BlockSpec((tm, d), lambda i: (i, 0)),
                  pl.BlockSpec((tm, d), lambda i: (i, 0)),
                  pl.BlockSpec((1, d), lambda i: (0, 0)),
                  pl.BlockSpec((D_IN, d), lambda i: (0, 0)),
                  pl.BlockSpec((tm, d), lambda i: (i, 0))],
        out_specs=[pl.BlockSpec((tm, d), lambda i: (i, 0)),
                   pl.BlockSpec((D_IN, d), lambda i: (0, 0)),
                   pl.BlockSpec((8, d), lambda i: (0, 0))],
        out_shape=[jax.ShapeDtypeStruct((t, d), F32),
                   jax.ShapeDtypeStruct((D_IN, d), BF16),
                   jax.ShapeDtypeStruct((8, d), F32)],
        scratch_shapes=[pltpu.VMEM((D_IN, d), F32)],
        compiler_params=_params(("arbitrary",)),
    )(dp, dho, h, g, win_t, n)


def _final_loss(h, g, tgt):
    t, d = h.shape
    tm = _tok_block(t)

    def body(h_ref, g_ref, t_ref, dh_ref, ls_ref, dg_ref):
        i = pl.program_id(0)
        hh = h_ref[...]
        r = _rstd(hh)
        xhat = hh * r
        gg = g_ref[...]
        err = xhat * gg - t_ref[...]
        dy = err * (1.0 / d)
        dxh = dy * gg
        dh_ref[...] = r * (dxh - xhat * jnp.mean(dxh * xhat, axis=-1, keepdims=True))
        lpart = _sum8(err * err)
        gpart = _sum8(dy * xhat)

        @pl.when(i == 0)
        def _():
            ls_ref[...] = lpart
            dg_ref[...] = gpart

        @pl.when(i > 0)
        def _():
            ls_ref[...] += lpart
            dg_ref[...] += gpart

    return _pcall(
        body, name="final_loss", grid=(t // tm,),
        in_specs=[pl.BlockSpec((tm, d), lambda i: (i, 0)),
                  pl.BlockSpec((1, d), lambda i: (0, 0)),
                  pl.BlockSpec((tm, d), lambda i: (i, 0))],
        out_specs=[pl.BlockSpec((tm, d), lambda i: (i, 0)),
                   pl.BlockSpec((8, d), lambda i: (0, 0)),
                   pl.BlockSpec((8, d), lambda i: (0, 0))],
        out_shape=[jax.ShapeDtypeStruct((t, d), F32),
                   jax.ShapeDtypeStruct((8, d), F32),
                   jax.ShapeDtypeStruct((8, d), F32)],
        compiler_params=_params(("arbitrary",)),
    )(h, g, tgt)


def _position():
    return lax.axis_index("x"), lax.axis_index("y"), lax.axis_index("c")


def _flip(v, bit):
    return 1 - v if bit else v


def _peer(k):
    x, y, c = _position()
    return _flip(x, k & 4), _flip(y, k & 2), _flip(c, k & 1)


def _slot(px, py, pc):
    return 4 * px + 2 * py + pc


def _handshake(peers):
    barrier = pltpu.get_barrier_semaphore()
    for peer in peers:
        pl.semaphore_signal(barrier, inc=1, device_id=peer, device_id_type=MESH)
    pl.semaphore_wait(barrier, len(peers))


def _sequencer_call(body, name, collective_id, out_type, scratch_types, operands):
    return pl.kernel(
        body, out_type=out_type, mesh=plsc.ScalarSubcoreMesh(axis_name="sequencer", num_cores=1), name=name,
        scratch_types=scratch_types, compiler_params=pltpu.CompilerParams(collective_id=collective_id),
    )(*operands)


def _all_gather(shards, name, collective_id):
    nt = len(shards)

    def body(*refs):
        xs = refs[:nt]
        outs = refs[nt:2 * nt]
        send_sems, recv_sems, local_sems = refs[2 * nt:]
        x, y, c = _position()
        me, sibling = (x, y, c), (x, y, 1 - c)
        chips = [(1 - x, y), (x, 1 - y), (1 - x, 1 - y)]
        _handshake([sibling] + [(*chip, c) for chip in chips])

        def copy(t, k, block, to, src=None):
            dst = outs[t].at[_slot(*block)]
            return pltpu.make_async_remote_copy(
                src_ref=dst if src is None else src, dst_ref=dst,
                send_sem=send_sems.at[t, k], recv_sem=recv_sems.at[t, k],
                device_id=to, device_id_type=MESH)

        mine = [pltpu.make_async_copy(xs[t], outs[t].at[_slot(*me)], local_sems.at[t]) for t in range(nt)]
        for cp in mine:
            cp.start()
        first = []
        for t in range(nt):
            first.append(copy(t, 0, me, sibling, src=xs[t]))
            first += [copy(t, 1 + j, me, (*chip, c), src=xs[t]) for j, chip in enumerate(chips)]
        for cp in first:
            cp.start()
        passed = []
        for j, chip in enumerate(chips):
            for t in range(nt):
                copy(t, 1 + j, (*chip, c), me).wait_recv()
                fwd = copy(t, 4 + j, (*chip, c), sibling)
                fwd.start()
                passed.append(fwd)
        for t in range(nt):
            copy(t, 0, sibling, me).wait_recv()
            for j, chip in enumerate(chips):
                copy(t, 4 + j, (*chip, 1 - c), me).wait_recv()
        for cp in first + passed:
            cp.wait_send()
        for cp in mine:
            cp.wait()

    return _sequencer_call(
        body, name, collective_id,
        out_type=[jax.ShapeDtypeStruct((N_DEV,) + s.shape, s.dtype) for s in shards],
        scratch_types=[pltpu.SemaphoreType.DMA((nt, 7)), pltpu.SemaphoreType.DMA((nt, 7)),
                       pltpu.SemaphoreType.DMA((nt,))],
        operands=shards)


def _scatter_copy(srcs, lands, send_sems, recv_sems, t, k):
    peer = _peer(k)
    return pltpu.make_async_remote_copy(
        src_ref=srcs[t].at[_slot(*peer)], dst_ref=lands[t].at[k],
        send_sem=send_sems.at[t * (N_DEV - 1) + k - 1], recv_sem=recv_sems.at[t * (N_DEV - 1) + k - 1],
        device_id=peer, device_id_type=MESH)


def _scatter_start(partials, name):
    nt = len(partials)

    def body(*refs):
        srcs, lands = refs[:nt], refs[nt:2 * nt]
        send_sems, recv_sems = refs[2 * nt], refs[2 * nt + 1]
        token = refs[-1]
        for k in range(1, N_DEV):
            for t in range(nt):
                _scatter_copy(srcs, lands, send_sems, recv_sems, t, k).start()
        token[...] = jnp.zeros_like(token)

    hbm = pl.BlockSpec(memory_space=pltpu.HBM)
    sem = pl.BlockSpec(memory_space=pltpu.SEMAPHORE)
    shapes = [pltpu.HBM(p.shape, p.dtype) for p in partials]
    lands = [pltpu.with_memory_space_constraint(lax.empty(p.shape, p.dtype), pltpu.HBM) for p in partials]
    srcs = [pltpu.with_memory_space_constraint(p, pltpu.HBM) for p in partials]
    out = _pcall(
        body, name=name,
        out_shape=[pltpu.SemaphoreType.DMA((nt * (N_DEV - 1),))] * 2 + shapes + shapes
        + [jax.ShapeDtypeStruct((8, 128), F32)],
        in_specs=[hbm] * (2 * nt),
        out_specs=[sem, sem] + [hbm] * (2 * nt) + [pl.BlockSpec(memory_space=pltpu.VMEM)],
        input_output_aliases={i: 2 + i for i in range(2 * nt)},
        compiler_params=pltpu.CompilerParams(has_side_effects=pltpu.SideEffectType.DATAFLOW_SIDE_EFFECTING),
    )(*srcs, *lands)
    return (nt, name, out[:-1]), out[-1]


def _scatter_wait(state, after):
    nt, name, (send_sems, recv_sems, *thru) = state

    def body(*refs):
        srcs, lands = refs[:nt], refs[nt:2 * nt]
        send_sems, recv_sems = refs[2 * nt], refs[2 * nt + 1]
        for k in range(1, N_DEV):
            for t in range(nt):
                copy = _scatter_copy(srcs, lands, send_sems, recv_sems, t, k)
                copy.wait_send()
                copy.wait_recv()

    hbm = pl.BlockSpec(memory_space=pltpu.HBM)
    sem = pl.BlockSpec(memory_space=pltpu.SEMAPHORE)
    out = _pcall(
        body, name=name + "_wait",
        out_shape=[pltpu.HBM(a.shape, a.dtype) for a in thru],
        in_specs=[hbm] * (2 * nt) + [sem, sem, pl.BlockSpec(memory_space=pl.ANY)],
        out_specs=[hbm] * (2 * nt),
        input_output_aliases={i: i for i in range(2 * nt)},
        compiler_params=pltpu.CompilerParams(has_side_effects=pltpu.SideEffectType.DATAFLOW_SIDE_EFFECTING),
    )(*thru, send_sems, recv_sems, after)
    return out[:nt], out[nt:]


def _all_reduce_rows(v):
    nv, _, w = v.shape

    def body(v_ref, out_ref, gath_ref, send_sems, recv_sems):
        x, y, c = _position()
        me = _slot(x, y, c)

        def copy(k):
            return pltpu.make_async_remote_copy(
                src_ref=v_ref, dst_ref=gath_ref.at[me],
                send_sem=send_sems.at[k - 1], recv_sem=recv_sems.at[k - 1],
                device_id=_peer(k), device_id_type=MESH)

        def arrival(k):
            return pltpu.make_async_remote_copy(
                src_ref=v_ref, dst_ref=gath_ref.at[_slot(*_peer(k))],
                send_sem=send_sems.at[k - 1], recv_sem=recv_sems.at[k - 1],
                device_id=_peer(k), device_id_type=MESH)

        sent = [copy(k) for k in range(1, N_DEV)]
        for cp in sent:
            cp.start()
        gath_ref[me] = v_ref[...]
        for k in range(1, N_DEV):
            arrival(k).wait_recv()
        for cp in sent:
            cp.wait_send()
        total = gath_ref[0]
        for s in range(1, N_DEV):
            total = total + gath_ref[s]
        out_ref[...] = jnp.sum(total, axis=1)

    vmem = pl.BlockSpec(memory_space=pltpu.VMEM)
    return _pcall(
        body, name="all_reduce_rows",
        in_specs=[vmem], out_specs=vmem,
        out_shape=jax.ShapeDtypeStruct((nv, w), F32),
        scratch_shapes=[pltpu.VMEM((N_DEV, nv, 8, w), F32),
                        pltpu.SemaphoreType.DMA((7,)), pltpu.SemaphoreType.DMA((7,))],
    )(v)


def _adamw_math(w, g, m, v):
    m2 = ADAM_B1 * m + (1.0 - ADAM_B1) * g
    v2 = ADAM_B2 * v + (1.0 - ADAM_B2) * (g * g)
    m_hat = m2 / (1.0 - ADAM_B1 ** ADAM_STEP)
    v_hat = v2 / (1.0 - ADAM_B2 ** ADAM_STEP)
    delta = -ADAM_LR * (m_hat / (jnp.sqrt(v_hat) + ADAM_EPS) + ADAM_WD * w)
    return delta, m2, v2


def _row_block(r):
    for cand in (256, 176, 128):
        if r % cand == 0:
            return cand
    return r


def _adamw_sharded(me, grads, w, m, v, dep):
    (own0, land0), (own1, land1) = grads
    _, r, c = land0.shape
    tr = _row_block(r)
    nr = r // tr

    def body(me_ref, o0_ref, l0_ref, o1_ref, l1_ref, w_ref, m_ref, v_ref, dep_ref, g_ref, d_ref, m2_ref, v2_ref):
        layer = pl.program_id(0)

        def total(own_ref, land_ref):
            acc = own_ref[0].astype(F32)
            for k in range(1, N_DEV):
                acc = acc + land_ref[k].astype(F32)
            return acc

        g = jnp.where(layer == 0, total(o0_ref, l0_ref), total(o1_ref, l1_ref))
        delta, m2, v2 = _adamw_math(w_ref[0], g, m_ref[0], v_ref[0])
        g_ref[0] = g
        d_ref[0] = delta
        m2_ref[0] = m2
        v2_ref[0] = v2

    rows0 = lambda l, i: jnp.where(l == 0, i, nr - 1)
    rows1 = lambda l, i: jnp.where(l == 1, i, 0)
    shard = pl.BlockSpec((1, tr, c), lambda l, i, me_ref: (l, i, 0))
    out = jax.ShapeDtypeStruct((2, r, c), F32)
    return _pcall(
        body, name="adamw_sharded",
        grid_spec=pltpu.PrefetchScalarGridSpec(
            num_scalar_prefetch=1, grid=(2, nr),
            in_specs=[pl.BlockSpec((1, tr, c), lambda l, i, me_ref: (me_ref[0], rows0(l, i), 0)),
                      pl.BlockSpec((N_DEV, tr, c), lambda l, i, me_ref: (0, rows0(l, i), 0)),
                      pl.BlockSpec((1, tr, c), lambda l, i, me_ref: (me_ref[0], rows1(l, i), 0)),
                      pl.BlockSpec((N_DEV, tr, c), lambda l, i, me_ref: (0, rows1(l, i), 0)),
                      shard, shard, shard, pl.BlockSpec(memory_space=pl.ANY)],
            out_specs=[shard, shard, shard, shard]),
        out_shape=[out, out, out, out],
        compiler_params=_params(("arbitrary", "arbitrary")),
    )(me, own0, land0, own1, land1, w, m, v, dep)


def _adamw_small(w, g, m, v):
    def body(w_ref, g_ref, m_ref, v_ref, d_ref, m2_ref, v2_ref):
        delta, m2, v2 = _adamw_math(w_ref[...], g_ref[...], m_ref[...], v_ref[...])
        d_ref[...] = delta
        m2_ref[...] = m2
        v2_ref[...] = v2

    spec = pl.BlockSpec(w.shape, lambda i: (0, 0))
    out = jax.ShapeDtypeStruct(w.shape, F32)
    return _pcall(
        body, name="adamw_small", grid=(1,),
        in_specs=[spec] * 4, out_specs=[spec] * 3, out_shape=[out] * 3,
        compiler_params=_params(("arbitrary",)),
    )(w, g, m, v)


def _pack(arrs):
    flat = jnp.concatenate([a.reshape(-1) for a in arrs])
    n = flat.shape[0]
    rows = -(-n // 1024) * 8
    return jnp.pad(flat, (0, rows * 128 - n)).reshape(rows, 128)


def _unpack(packed, like):
    flat = packed.reshape(-1)
    out, off = [], 0
    for a in like:
        out.append(flat[off:off + a.size].reshape(a.shape))
        off += a.size
    return out


def kernel(x, mem, g_ffn1, w_ffn1_up, w_ffn1_down, g_mix, w_in, conv_w, sinks, g_mem, w_mem_kv, g_grp, w_out, g_ffn2, w_ffn2_up, w_ffn2_down, g_final, loss_target, m_g_ffn1, m_w_ffn1_up, m_w_ffn1_down, m_g_mix, m_w_in, m_conv_w, m_sinks, m_g_mem, m_w_mem_kv, m_g_grp, m_w_out, m_g_ffn2, m_w_ffn2_up, m_w_ffn2_down, m_g_final, v_g_ffn1, v_w_ffn1_up, v_w_ffn1_down, v_g_mix, v_w_in, v_conv_w, v_sinks, v_g_mem, v_w_mem_kv, v_g_grp, v_w_out, v_g_ffn2, v_w_ffn2_up, v_w_ffn2_down, v_g_final):
    depth = g_ffn1.shape[0]
    t, d = x.shape[1], x.shape[2]
    width = max(d, D_MIX)
    me = _slot(*_position())
    conv_shard = conv_w.shape[2]

    xin, memin, tgt = x[0], mem[0], loss_target[0]

    conv_tile = jnp.zeros((depth * 8, 128), F32).at[:, :conv_shard].set(
        jnp.pad(conv_w, ((0, 0), (0, 8 - conv_w.shape[1]), (0, 0))).reshape(depth * 8, conv_shard))
    tr = lambda a: jnp.swapaxes(a, -1, -2)
    bf = lambda a: a.astype(BF16)
    weights = []
    collective_id = 0
    for l in range(depth):
        groups = [[bf(tr(w_ffn1_up[l])), bf(w_ffn1_down[l])] + ([conv_tile] if l == 0 else []),
                  [bf(tr(w_in[l])), bf(w_mem_kv[l]), bf(w_out[l])],
                  [bf(tr(w_ffn2_up[l])), bf(w_ffn2_down[l])]]
        full = []
        for gi, shards in enumerate(groups):
            full.append(_all_gather(shards, f"all_gather_l{l}_g{gi}", collective_id))
            collective_id += 1
        if l == 0:
            conv_full = full[0][2].reshape(N_DEV, depth, 8, 128)[:, :, :3, :conv_shard]
            conv_full = conv_full.transpose(1, 2, 0, 3).reshape(depth, 3, N_DEV * conv_shard)
        weights.append(dict(
            up1=full[0][0].reshape(2, -1, d), dn1=full[0][1].reshape(-1, d),
            win=full[1][0].reshape(D_IN, d), wkv=full[1][1].reshape(d, 2 * D_MEMQ), wout=full[1][2].reshape(D_MIX, d),
            up2=full[2][0].reshape(2, -1, d), dn2=full[2][1].reshape(-1, d)))

    row = lambda a: a.reshape(1, -1)
    bias_tok, bias_key = _bias_tables()

    h = xin
    saved = []
    for l in range(depth):
        wl = weights[l]
        s = dict(h0=h)
        h, s["gu1"], s["n1"] = _ffn_fwd(h, row(g_ffn1[l]), wl["up1"], wl["dn1"])
        s["h1"] = h
        s["p"], s["n_mix"], s["qh"] = _mix_proj_fwd(h, row(g_mix[l]), wl["win"])
        s["mkv"], s["nt_mem"] = _memkv_fwd(memin, row(g_mem[l]), wl["wkv"], s["p"])
        s["y"], s["lse"] = _mix_core_fwd(s["p"], s["qh"], s["mkv"], conv_full[l], row(sinks[l]), bias_tok)
        h, s["mt"] = _mix_out_fwd(s["y"], h, row(g_grp[l]), wl["wout"])
        s["h2"] = h
        h, s["gu2"], s["n2"] = _ffn_fwd(h, row(g_ffn2[l]), wl["up2"], wl["dn2"])
        saved.append(s)

    dh, loss_part, dg_final = _final_loss(h, row(g_final), tgt)

    small = {}
    dep = loss_part

    started = []

    def scatter(names, partials, label):
        state, token = _scatter_start(partials, f"scatter_grads_{label}")
        started.append((names, state))
        return token

    for l in reversed(range(depth)):
        wl, s = weights[l], saved[l]
        dh, agu, dyb, small["g_ffn2", l] = _ffn_bwd_act(dh, s["h2"], row(g_ffn2[l]), s["gu2"], wl["up2"], wl["dn2"], dep)
        ddn2 = _ffn_bwd_w(agu, 2, 1, dyb, agu, f"ffn_bwd_w_down_l{l}_ffn2").reshape(N_DEV, -1, d)
        dup2 = _ffn_bwd_w(agu, 0, 2, s["n2"], ddn2, f"ffn_bwd_w_up_l{l}_ffn2").reshape(N_DEV, -1, d)
        dep = scatter([("w_ffn2_up", l), ("w_ffn2_down", l)], [dup2, ddn2], f"l{l}_ffn2")
        dyconv, doh, delta, dwout, small["g_grp", l] = _mix_out_bwd(dh, s["y"], row(g_grp[l]), wl["wout"], s["mt"], dep)
        dp, dmkv, small["conv_w", l], small["sinks", l] = _mix_core_bwd(
            s["p"], s["qh"], dyconv, doh, delta, s["lse"], s["mkv"], conv_full[l], row(sinks[l]), bias_tok, bias_key)
        dwkv, small["g_mem", l] = _memkv_bwd(dmkv, memin, row(g_mem[l]), wl["wkv"], s["nt_mem"])
        dh, dwin, small["g_mix", l] = _mix_proj_bwd(dp, dh, s["h1"], row(g_mix[l]), wl["win"], s["n_mix"])
        dep = scatter([("w_in", l), ("w_mem_kv", l), ("w_out", l)],
                      [dwin.reshape(N_DEV, -1, d), dwkv.reshape(N_DEV, -1, 2 * D_MEMQ), dwout.reshape(N_DEV, -1, d)],
                      f"l{l}_mix")
        dh, agu, dyb, small["g_ffn1", l] = _ffn_bwd_act(dh, s["h0"], row(g_ffn1[l]), s["gu1"], wl["up1"], wl["dn1"], dep)
        ddn1 = _ffn_bwd_w(agu, 2, 1, dyb, agu, f"ffn_bwd_w_down_l{l}_ffn1").reshape(N_DEV, -1, d)
        if l > 0:
            dup1 = _ffn_bwd_w(agu, 0, 2, s["n1"], ddn1, f"ffn_bwd_w_up_l{l}_ffn1").reshape(N_DEV, -1, d)
            dep = scatter([("w_ffn1_up", l), ("w_ffn1_down", l)], [dup1, ddn1], f"l{l}_ffn1")
        else:
            dep = scatter([("w_ffn1_down", l)], [ddn1], f"l{l}_ffn1_down")
            dup1 = _ffn_bwd_w(agu, 0, 2, s["n1"], dep, f"ffn_bwd_w_up_l{l}_ffn1").reshape(N_DEV, -1, d)
            dep = scatter([("w_ffn1_up", l)], [dup1], f"l{l}_ffn1_up")
    grad_x = dh[None]

    big = {"w_ffn2_up": (w_ffn2_up, m_w_ffn2_up, v_w_ffn2_up, True), "w_ffn2_down": (w_ffn2_down, m_w_ffn2_down, v_w_ffn2_down, False),
           "w_in": (w_in, m_w_in, v_w_in, True), "w_mem_kv": (w_mem_kv, m_w_mem_kv, v_w_mem_kv, False),
           "w_out": (w_out, m_w_out, v_w_out, False), "w_ffn1_up": (w_ffn1_up, m_w_ffn1_up, v_w_ffn1_up, True),
           "w_ffn1_down": (w_ffn1_down, m_w_ffn1_down, v_w_ffn1_down, False)}
    me_index = jnp.reshape(me, (1,)).astype(jnp.int32)
    sharded, landed = {}, {}

    def finish(groups, after):
        for names, state in groups:
            owns, lands = _scatter_wait(state, after)
            for key, own, land in zip(names, owns, lands):
                landed[key] = (own, land)
            after = lands[0]
            for name in dict.fromkeys(n for n, _ in names):
                if name not in sharded and all((name, l) in landed for l in range(depth)):
                    w, m, v, transposed = big[name]
                    fix = tr if transposed else (lambda a: a)
                    res = _adamw_sharded(me_index, [landed[name, l] for l in range(depth)], fix(w), fix(m), fix(v), after)
                    sharded[name] = tuple(fix(r) for r in res)
                    after = res[0]
        return after

    dep = finish(started[:-2], dep)

    def lanes(a):
        return jnp.pad(a, ((0, 0), (0, width - a.shape[1])))

    def first_row(a):
        return lanes(jnp.pad(a, ((0, 8 - a.shape[0]), (0, 0))))

    vec_names = ["g_ffn1", "g_mix", "g_mem", "g_grp", "g_ffn2", "sinks"]
    tiles = [lanes(small[n, l]) for n in vec_names for l in range(depth)]
    tiles += [first_row(small["conv_w", l][k:k + 1]) for l in range(depth) for k in range(3)]
    tiles.append(lanes(dg_final))
    n_real = len(tiles)
    tiles.append(lanes(loss_part))
    tiles.append(lanes(dep[0, :8, :128]))
    tiles += [jnp.zeros((8, width), F32)] * (-len(tiles) % 8)
    summed = _all_reduce_rows(jnp.stack(tiles))
    loss = 0.5 * jnp.sum(summed[n_real]) / d

    def vec(n, wd):
        return jnp.stack([summed[vec_names.index(n) * depth + l, :wd] for l in range(depth)])

    conv_base = len(vec_names) * depth
    conv_grad = jnp.stack([jnp.stack([summed[conv_base + 3 * l + k, :D_CONV] for k in range(3)]) for l in range(depth)])
    grads_small = {
        "g_ffn1": vec("g_ffn1", d), "g_mix": vec("g_mix", d), "g_mem": vec("g_mem", d),
        "g_grp": vec("g_grp", D_MIX), "g_ffn2": vec("g_ffn2", d), "sinks": vec("sinks", N_SWA_HEADS),
        "conv_w": lax.dynamic_slice_in_dim(conv_grad, me * conv_shard, conv_shard, axis=2),
        "g_final": summed[n_real - 1, :d],
    }
    small_w = [("g_ffn1", g_ffn1, m_g_ffn1, v_g_ffn1), ("g_mix", g_mix, m_g_mix, v_g_mix),
               ("conv_w", conv_w, m_conv_w, v_conv_w), ("sinks", sinks, m_sinks, v_sinks),
               ("g_mem", g_mem, m_g_mem, v_g_mem), ("g_grp", g_grp, m_g_grp, v_g_grp),
               ("g_ffn2", g_ffn2, m_g_ffn2, v_g_ffn2), ("g_final", g_final, m_g_final, v_g_final)]
    like = [w for _, w, _, _ in small_w]
    packed = _adamw_small(_pack(like), _pack([grads_small[n] for n, _, _, _ in small_w]),
                          _pack([m for _, _, m, _ in small_w]), _pack([v for _, _, _, v in small_w]))
    small_out = {n: (grads_small[n], dl, m2, v2)
                 for (n, _, _, _), dl, m2, v2 in zip(small_w, *[_unpack(pk, like) for pk in packed])}

    finish(started[-2:], packed[0])

    order = ["g_ffn1", "w_ffn1_up", "w_ffn1_down", "g_mix", "w_in", "conv_w", "sinks", "g_mem", "w_mem_kv", "g_grp",
             "w_out", "g_ffn2", "w_ffn2_up", "w_ffn2_down", "g_final"]
    results = {**sharded, **small_out}
    outs = [loss, grad_x]
    for part in range(4):
        outs += [results[n][part] for n in order]
    return tuple(outs)
```

```python
import numpy as np
import jax
import jax.numpy as jnp
from jax import lax
from jax.experimental import pallas as pl
from jax.experimental.pallas import tpu as pltpu
from jax.experimental.pallas import tpu_sc as plsc

F32 = jnp.float32
BF16 = jnp.bfloat16

N_DEV = 8
EPS = 1e-6
N_SWA_HEADS = 8
N_SWA_KV = 2
SWA_GROUP = N_SWA_HEADS // N_SWA_KV
HEAD_DIM = 64
N_MEM_HEADS = 4
D_CONV = 256
BLOCK = 128
D_SWA = N_SWA_HEADS * HEAD_DIM
D_KV = N_SWA_KV * HEAD_DIM
D_MEMQ = N_MEM_HEADS * HEAD_DIM
D_MIX = D_CONV + D_SWA + D_MEMQ
D_IN = 3 * D_CONV + D_SWA + 2 * D_KV + D_MEMQ
COL_BG, COL_CG, COL_U = 0, D_CONV, 2 * D_CONV
COL_Q = 3 * D_CONV
COL_K = COL_Q + D_SWA
COL_V = COL_K + D_KV
COL_QM = COL_V + D_KV
MIX_GROUPS = ((0, D_CONV), (D_CONV, D_CONV + D_SWA), (D_CONV + D_SWA, D_MIX))
SLOPES = tuple(2.0 ** (-8.0 * (i + 1) / N_SWA_HEADS) for i in range(N_SWA_HEADS))
SCALE = HEAD_DIM ** -0.5
NEG = -1e30

ADAM_LR = 0.001
ADAM_B1 = 0.9
ADAM_B2 = 0.999
ADAM_EPS = 1e-08
ADAM_WD = 0.01
ADAM_STEP = 10

V7X_VMEM_BYTES = 64 * 1024 * 1024
VMEM_LIMIT = (V7X_VMEM_BYTES * 3) // 4
MESH = pl.DeviceIdType.MESH


def _pcall(body, **kw):
    return pl.pallas_call(body, **kw)


def _params(sem=None, vmem=VMEM_LIMIT):
    return pltpu.CompilerParams(dimension_semantics=sem, vmem_limit_bytes=vmem)


def _dot(a, b):
    return lax.dot_general(a, b, (((1,), (0,)), ((), ())), preferred_element_type=F32)


def _dot_nt(a, b):
    return lax.dot_general(a, b, (((1,), (1,)), ((), ())), preferred_element_type=F32)


def _dot_tn(a, b):
    return lax.dot_general(a, b, (((0,), (0,)), ((), ())), preferred_element_type=F32)


def _rstd(x):
    return lax.rsqrt(jnp.mean(x * x, axis=-1, keepdims=True) + EPS)


def _sigmoid(x):
    return 1.0 / (1.0 + jnp.exp(-x))


def _sum8(x):
    r, w = x.shape
    return jnp.sum(x.reshape(r // 8, 8, w), axis=0)


def _tok_block(t, rows=512):
    return min(rows, t)


def _feat_block(f):
    return f // (N_DEV // 2)


def _ffn_fwd(h, g, wup_t, wdn):
    t, d = h.shape
    f = wdn.shape[0]
    tm, tf = _tok_block(t), _feat_block(f)
    ni, nj = t // tm, f // tf

    def body(h_ref, g_ref, wup_ref, wdn_ref, ho_ref, gu_ref, n_ref, nt_ref, acc_ref):
        j = pl.program_id(1)

        @pl.when(j == 0)
        def _():
            hh = h_ref[...]
            n = hh * _rstd(hh) * g_ref[...]
            n_ref[...] = n.astype(BF16)
            nt_ref[...] = n.T.astype(BF16)
            acc_ref[...] = jnp.zeros_like(acc_ref)

        nt = nt_ref[...]
        gate = _dot(wup_ref[0], nt)
        up = _dot(wup_ref[1], nt)
        gu_ref[0] = gate.astype(BF16)
        gu_ref[1] = up.astype(BF16)
        a = gate * _sigmoid(gate) * up
        acc_ref[...] += _dot_tn(a.astype(BF16), wdn_ref[...])

        @pl.when(j == nj - 1)
        def _():
            ho_ref[...] = h_ref[...] + 0.5 * acc_ref[...]

    return _pcall(
        body, name="ffn_fwd", grid=(ni, nj),
        in_specs=[pl.BlockSpec((tm, d), lambda i, j: (i, 0)),
                  pl.BlockSpec((1, d), lambda i, j: (0, 0)),
                  pl.BlockSpec((2, tf, d), lambda i, j: (0, j, 0)),
                  pl.BlockSpec((tf, d), lambda i, j: (j, 0))],
        out_specs=[pl.BlockSpec((tm, d), lambda i, j: (i, 0)),
                   pl.BlockSpec((2, tf, tm), lambda i, j: (0, j, i)),
                   pl.BlockSpec((tm, d), lambda i, j: (i, 0))],
        out_shape=[jax.ShapeDtypeStruct((t, d), F32),
                   jax.ShapeDtypeStruct((2, f, t), BF16),
                   jax.ShapeDtypeStruct((t, d), BF16)],
        scratch_shapes=[pltpu.VMEM((d, tm), BF16), pltpu.VMEM((tm, d), F32)],
        compiler_params=_params(("parallel", "arbitrary")),
    )(h, g, wup_t, wdn)


def _ffn_bwd_act(dho, h, g, gu, wup_t, wdn, dep):
    t, d = h.shape
    f = wdn.shape[0]
    tm, tf = _tok_block(t), _feat_block(f)
    ni, nj = t // tm, f // tf

    def body(dho_ref, h_ref, g_ref, gu_ref, wup_ref, wdn_ref, dep_ref, dh_ref, agu_ref, dyb_ref, dg_ref, dyt_ref, acc_ref):
        i = pl.program_id(0)
        j = pl.program_id(1)

        @pl.when(j == 0)
        def _():
            dy0 = 0.5 * dho_ref[...]
            dyb_ref[...] = dy0.astype(BF16)
            dyt_ref[...] = dy0.T.astype(BF16)
            acc_ref[...] = jnp.zeros_like(acc_ref)

        da = _dot(wdn_ref[...], dyt_ref[...])
        gate = gu_ref[0].astype(F32)
        up = gu_ref[1].astype(F32)
        sg = _sigmoid(gate)
        silu = gate * sg
        dgate = (da * up * (sg * (1.0 + gate * (1.0 - sg)))).astype(BF16)
        dup = (da * silu).astype(BF16)
        agu_ref[0] = dgate
        agu_ref[1] = dup
        agu_ref[2] = (silu * up).astype(BF16)
        acc_ref[...] += _dot_tn(dgate, wup_ref[0])
        acc_ref[...] += _dot_tn(dup, wup_ref[1])

        @pl.when(j == nj - 1)
        def _():
            hh = h_ref[...]
            r = _rstd(hh)
            xhat = hh * r
            dnf = acc_ref[...]
            dxh = dnf * g_ref[...]
            dh_ref[...] = dho_ref[...] + r * (dxh - xhat * jnp.mean(dxh * xhat, axis=-1, keepdims=True))
            part = _sum8(dnf * xhat)

            @pl.when(i == 0)
            def _():
                dg_ref[...] = part

            @pl.when(i > 0)
            def _():
                dg_ref[...] += part

    return _pcall(
        body, name="ffn_bwd_act", grid=(ni, nj),
        in_specs=[pl.BlockSpec((tm, d), lambda i, j: (i, 0)),
                  pl.BlockSpec((tm, d), lambda i, j: (i, 0)),
                  pl.BlockSpec((1, d), lambda i, j: (0, 0)),
                  pl.BlockSpec((2, tf, tm), lambda i, j: (0, j, i)),
                  pl.BlockSpec((2, tf, d), lambda i, j: (0, j, 0)),
                  pl.BlockSpec((tf, d), lambda i, j: (j, 0)),
                  pl.BlockSpec(memory_space=pl.ANY)],
        out_specs=[pl.BlockSpec((tm, d), lambda i, j: (i, 0)),
                   pl.BlockSpec((3, tf, tm), lambda i, j: (0, j, i)),
                   pl.BlockSpec((tm, d), lambda i, j: (i, 0)),
                   pl.BlockSpec((8, d), lambda i, j: (0, 0))],
        out_shape=[jax.ShapeDtypeStruct((t, d), F32),
                   jax.ShapeDtypeStruct((3, f, t), BF16),
                   jax.ShapeDtypeStruct((t, d), BF16),
                   jax.ShapeDtypeStruct((8, d), F32)],
        scratch_shapes=[pltpu.VMEM((d, tm), BF16), pltpu.VMEM((tm, d), F32)],
        compiler_params=_params(("arbitrary", "arbitrary")),
    )(dho, h, g, gu, wup_t, wdn, dep)


def _ffn_bwd_w(agu, first, count, rhs, dep, name):
    _, f, t = agu.shape
    d = rhs.shape[1]
    tm, tf = _tok_block(t, 2048), _feat_block(f)
    ni, nj = t // tm, f // tf

    def body(lhs_ref, rhs_ref, dep_ref, dw_ref, acc_ref):
        i = pl.program_id(1)
        @pl.when(i == 0)
        def _():
            acc_ref[...] = jnp.zeros_like(acc_ref)

        rb = rhs_ref[...]
        for k in range(count):
            acc_ref[k] += _dot(lhs_ref[k], rb)

        @pl.when(i == ni - 1)
        def _():
            dw_ref[...] = acc_ref[...].astype(BF16)

    return _pcall(
        body, name=name, grid=(nj, ni),
        in_specs=[pl.BlockSpec((count, tf, tm), lambda j, i: (first // count, j, i)),
                  pl.BlockSpec((tm, d), lambda j, i: (i, 0)),
                  pl.BlockSpec(memory_space=pl.ANY)],
        out_specs=pl.BlockSpec((count, tf, d), lambda j, i: (0, j, 0)),
        out_shape=jax.ShapeDtypeStruct((count, f, d), BF16),
        scratch_shapes=[pltpu.VMEM((count, tf, d), F32)],
        compiler_params=_params(("parallel", "arbitrary")),
    )(agu, rhs, dep)


N_HEADS = N_SWA_HEADS + N_MEM_HEADS


def _q_col(hd):
    return COL_Q + HEAD_DIM * hd if hd < N_SWA_HEADS else COL_QM + HEAD_DIM * (hd - N_SWA_HEADS)


def _mix_proj_fwd(h, g, win_t):
    t, d = h.shape
    tm = _tok_block(t)

    def body(h_ref, g_ref, win_ref, p_ref, n_ref, qh_ref):
        hh = h_ref[...]
        n = (hh * _rstd(hh) * g_ref[...]).astype(BF16)
        n_ref[...] = n
        proj = _dot_nt(n, win_ref[...])
        p_ref[...] = proj.astype(BF16)
        for hd in range(N_HEADS):
            c0 = _q_col(hd)
            qh_ref[hd] = (proj[:, c0:c0 + HEAD_DIM] * SCALE).astype(BF16)

    return _pcall(
        body, name="mix_proj_fwd", grid=(t // tm,),
        in_specs=[pl.BlockSpec((tm, d), lambda i: (i, 0)),
                  pl.BlockSpec((1, d), lambda i: (0, 0)),
                  pl.BlockSpec((D_IN, d), lambda i: (0, 0))],
        out_specs=[pl.BlockSpec((tm, D_IN), lambda i: (i, 0)),
                   pl.BlockSpec((tm, d), lambda i: (i, 0)),
                   pl.BlockSpec((N_HEADS, tm, HEAD_DIM), lambda i: (0, i, 0))],
        out_shape=[jax.ShapeDtypeStruct((t, D_IN), BF16), jax.ShapeDtypeStruct((t, d), BF16),
                   jax.ShapeDtypeStruct((N_HEADS, t, HEAD_DIM), BF16)],
        compiler_params=_params(("parallel",)),
    )(h, g, win_t)


def _memkv_fwd(mem, g, wkv, dep):
    m, d = mem.shape

    def body(mem_ref, g_ref, w_ref, dep_ref, mkv_ref, nt_ref):
        mm = mem_ref[...]
        n = mm * _rstd(mm) * g_ref[...]
        nt_ref[...] = n.T.astype(BF16)
        mkv_ref[...] = _dot(n.astype(BF16), w_ref[...]).astype(BF16)

    return _pcall(
        body, name="memkv_fwd", grid=(1,),
        in_specs=[pl.BlockSpec((m, d), lambda i: (0, 0)),
                  pl.BlockSpec((1, d), lambda i: (0, 0)),
                  pl.BlockSpec((d, 2 * D_MEMQ), lambda i: (0, 0)),
                  pl.BlockSpec(memory_space=pl.ANY)],
        out_specs=[pl.BlockSpec((m, 2 * D_MEMQ), lambda i: (0, 0)),
                   pl.BlockSpec((d, m), lambda i: (0, 0))],
        out_shape=[jax.ShapeDtypeStruct((m, 2 * D_MEMQ), BF16), jax.ShapeDtypeStruct((d, m), BF16)],
        compiler_params=_params(("arbitrary",)),
    )(mem, g, wkv, dep)


def _memkv_bwd(dmkv, mem, g, wkv, nt):
    m, d = mem.shape

    def body(dmkv_ref, mem_ref, g_ref, w_ref, nt_ref, dw_ref, dg_ref):
        db = dmkv_ref[...].astype(BF16)
        dw_ref[...] = _dot(nt_ref[...], db).astype(BF16)
        dn = _dot_nt(db, w_ref[...])
        mm = mem_ref[...]
        dg_ref[...] = _sum8(dn * (mm * _rstd(mm)))

    return _pcall(
        body, name="memkv_bwd", grid=(1,),
        in_specs=[pl.BlockSpec((m, 2 * D_MEMQ), lambda i: (0, 0)),
                  pl.BlockSpec((m, d), lambda i: (0, 0)),
                  pl.BlockSpec((1, d), lambda i: (0, 0)),
                  pl.BlockSpec((d, 2 * D_MEMQ), lambda i: (0, 0)),
                  pl.BlockSpec((d, m), lambda i: (0, 0))],
        out_specs=[pl.BlockSpec((d, 2 * D_MEMQ), lambda i: (0, 0)),
                   pl.BlockSpec((8, d), lambda i: (0, 0))],
        out_shape=[jax.ShapeDtypeStruct((d, 2 * D_MEMQ), BF16), jax.ShapeDtypeStruct((8, d), F32)],
        compiler_params=_params(("arbitrary",)),
    )(dmkv, mem, g, wkv, nt)


def _shift_rows(v, k, edge_rows, row):
    out = pltpu.roll(v, k, 0)
    for r in range(k):
        out = jnp.where(row == r, edge_rows[r], out)
    return out


def _shift_rows_up(v, k, edge_rows, row):
    n = v.shape[0]
    out = pltpu.roll(v, n - k, 0)
    for r in range(k):
        out = jnp.where(row == n - k + r, edge_rows[r], out)
    return out


GROUP_ROWS = SWA_GROUP * BLOCK
BIAS_CUR, BIAS_PREV, BIAS_NONE = 0, 1, 2


def _bias_table():
    tq = np.arange(BLOCK)[:, None]
    sk = np.arange(BLOCK)[None, :]
    slopes = np.asarray(SLOPES, np.float32)[:, None, None]
    cur = np.where(tq >= sk, -slopes * (tq - sk).astype(np.float32), NEG)
    prev = np.where(sk > tq, -slopes * (tq + BLOCK - sk).astype(np.float32), NEG)
    none = np.full_like(cur, NEG)
    tok = np.stack([cur, prev, none]).astype(np.float32).reshape(3, N_SWA_KV, GROUP_ROWS, BLOCK)
    return jnp.asarray(np.ascontiguousarray(tok.transpose(0, 1, 3, 2)))


def _head_cols(hd):
    return D_CONV + HEAD_DIM * hd


def _mix_core_fwd(p, qh, mkv, convw, sinks, bias_key):
    t = p.shape[0]
    m = mkv.shape[0]
    nb = t // BLOCK

    def body(sk_ref, pc_ref, pkv_ref, ppc_ref, ppu_ref, qh_ref, mkv_ref, cw_ref, bc_ref, bp_ref, y_ref, l_ref):
        i = pl.program_id(0)
        prevf = (i > 0).astype(F32)
        row = lax.broadcasted_iota(jnp.int32, (BLOCK, D_CONV), 0)

        bg = pc_ref[:, COL_BG:COL_BG + D_CONV].astype(F32)
        cg = pc_ref[:, COL_CG:COL_CG + D_CONV].astype(F32)
        u = pc_ref[:, COL_U:COL_U + D_CONV].astype(F32)
        vv = cg * u
        pvv = ppc_ref[...].astype(F32) * ppu_ref[...].astype(F32) * prevf
        vv1 = _shift_rows(vv, 1, [pvv[15:16]], row)
        vv2 = _shift_rows(vv, 2, [pvv[14:15], pvv[15:16]], row)
        w = cw_ref[...]
        y_ref[:, 0:D_CONV] = bg * (w[0:1] * vv2 + w[1:2] * vv1 + w[2:3] * vv)

        head_row = lax.broadcasted_iota(jnp.int32, (128, BLOCK), 0)
        lse_t = jnp.zeros((128, BLOCK), F32)
        for kv in range(N_SWA_KV):
            heads = range(kv * SWA_GROUP, (kv + 1) * SWA_GROUP)
            kc = pc_ref[:, COL_K + HEAD_DIM * kv:COL_K + HEAD_DIM * (kv + 1)]
            vc = pc_ref[:, COL_V + HEAD_DIM * kv:COL_V + HEAD_DIM * (kv + 1)]
            kp = pkv_ref[:, HEAD_DIM * kv:HEAD_DIM * (kv + 1)]
            vp = pkv_ref[:, D_KV + HEAD_DIM * kv:D_KV + HEAD_DIM * (kv + 1)]
            qg = qh_ref[kv * SWA_GROUP:(kv + 1) * SWA_GROUP].reshape(GROUP_ROWS, HEAD_DIM)
            sc = _dot_nt(kc, qg) + bc_ref[0, kv]
            sp = _dot_nt(kp, qg) + bp_ref[0, kv]
            sink = jnp.concatenate([jnp.full((1, BLOCK), sk_ref[0, hd], F32) for hd in heads], axis=1)
            mx = jnp.maximum(jnp.max(jnp.maximum(sc, sp), axis=0, keepdims=True), sink)
            ec = jnp.exp(sc - mx)
            ep = jnp.exp(sp - mx)
            den = jnp.sum(ec + ep, axis=0, keepdims=True) + jnp.exp(sink - mx)
            ot = (_dot_tn(vc, ec.astype(BF16)) + _dot_tn(vp, ep.astype(BF16))) / den
            lse = mx + jnp.log(den)
            for gi, hd in enumerate(heads):
                span = slice(gi * BLOCK, (gi + 1) * BLOCK)
                y_ref[:, _head_cols(hd):_head_cols(hd) + HEAD_DIM] = ot[:, span].T
                lse_t = jnp.where(head_row == hd, lse[:, span], lse_t)

        for hm in range(N_MEM_HEADS):
            hd = N_SWA_HEADS + hm
            mk = mkv_ref[:, HEAD_DIM * hm:HEAD_DIM * (hm + 1)]
            mv = mkv_ref[:, D_MEMQ + HEAD_DIM * hm:D_MEMQ + HEAD_DIM * (hm + 1)]
            s = _dot_nt(mk, qh_ref[hd])
            mx = jnp.max(s, axis=0, keepdims=True)
            e = jnp.exp(s - mx)
            den = jnp.sum(e, axis=0, keepdims=True)
            y_ref[:, _head_cols(hd):_head_cols(hd) + HEAD_DIM] = (_dot_tn(mv, e.astype(BF16)) / den).T
            lse_t = jnp.where(head_row == hd, mx + jnp.log(den), lse_t)
        l_ref[...] = lse_t.T

    kv_col = COL_K // (2 * D_KV)
    bias_block = (1, N_SWA_KV, BLOCK, GROUP_ROWS)
    return _pcall(
        body, name="mix_core_fwd", grid=(nb,),
        in_specs=[pl.BlockSpec(memory_space=pltpu.SMEM),
                  pl.BlockSpec((BLOCK, D_IN), lambda i: (i, 0)),
                  pl.BlockSpec((BLOCK, 2 * D_KV), lambda i: (jnp.maximum(i - 1, 0), kv_col)),
                  pl.BlockSpec((16, D_CONV), lambda i: (jnp.maximum(i * (BLOCK // 16) - 1, 0), COL_CG // D_CONV)),
                  pl.BlockSpec((16, D_CONV), lambda i: (jnp.maximum(i * (BLOCK // 16) - 1, 0), COL_U // D_CONV)),
                  pl.BlockSpec((N_HEADS, BLOCK, HEAD_DIM), lambda i: (0, i, 0)),
                  pl.BlockSpec((m, 2 * D_MEMQ), lambda i: (0, 0)),
                  pl.BlockSpec((3, D_CONV), lambda i: (0, 0)),
                  pl.BlockSpec(bias_block, lambda i: (BIAS_CUR, 0, 0, 0)),
                  pl.BlockSpec(bias_block, lambda i: (jnp.where(i == 0, BIAS_NONE, BIAS_PREV), 0, 0, 0))],
        out_specs=[pl.BlockSpec((BLOCK, D_MIX), lambda i: (i, 0)),
                   pl.BlockSpec((BLOCK, 128), lambda i: (i, 0))],
        out_shape=[jax.ShapeDtypeStruct((t, D_MIX), F32), jax.ShapeDtypeStruct((t, 128), F32)],
        compiler_params=_params(("parallel",)),
    )(sinks, p, p, p, p, qh, mkv, convw, bias_key, bias_key)


def _mix_core_bwd(p, qh, dyconv, doh, delta, lse, mkv, convw, sinks, bias_key):
    t = p.shape[0]
    m = mkv.shape[0]
    nb = t // BLOCK

    def body(sk_ref, pc_ref, pkv_ref, ppc_ref, ppu_ref, pnb_ref, dyc_ref, dyn_ref, qc_ref, qn_ref, doc_ref, don_ref,
             dlc_ref, dln_ref, lc_ref, ln_ref, mkv_ref, cw_ref, bp_ref, bct_ref, bnt_ref,
             dp_ref, dmkv_ref, dcw_ref, dsk_ref):
        i = pl.program_id(0)
        prevf = (i > 0).astype(F32)
        nextf = (i < nb - 1).astype(F32)
        row = lax.broadcasted_iota(jnp.int32, (BLOCK, D_CONV), 0)

        @pl.when(i == 0)
        def _():
            dmkv_ref[...] = jnp.zeros_like(dmkv_ref)
            dcw_ref[...] = jnp.zeros_like(dcw_ref)
            dsk_ref[...] = jnp.zeros_like(dsk_ref)

        bg = pc_ref[:, COL_BG:COL_BG + D_CONV].astype(F32)
        cg = pc_ref[:, COL_CG:COL_CG + D_CONV].astype(F32)
        u = pc_ref[:, COL_U:COL_U + D_CONV].astype(F32)
        vv = cg * u
        pvv = ppc_ref[...].astype(F32) * ppu_ref[...].astype(F32) * prevf
        vv1 = _shift_rows(vv, 1, [pvv[15:16]], row)
        vv2 = _shift_rows(vv, 2, [pvv[14:15], pvv[15:16]], row)
        w = cw_ref[...]
        yconv = w[0:1] * vv2 + w[1:2] * vv1 + w[2:3] * vv
        dyo = dyc_ref[...]
        dyc = dyo * bg
        nxt = dyn_ref[...] * pnb_ref[...].astype(F32) * nextf
        d1 = _shift_rows_up(dyc, 1, [nxt[0:1]], row)
        d2 = _shift_rows_up(dyc, 2, [nxt[0:1], nxt[1:2]], row)
        dvv = w[2:3] * dyc + w[1:2] * d1 + w[0:1] * d2
        dp_ref[:, COL_BG:COL_BG + D_CONV] = (dyo * yconv).astype(BF16)
        dp_ref[:, COL_CG:COL_CG + D_CONV] = (dvv * u).astype(BF16)
        dp_ref[:, COL_U:COL_U + D_CONV] = (dvv * cg).astype(BF16)
        dcw_ref[0:1, :] += jnp.sum(dyc * vv2, axis=0, keepdims=True)
        dcw_ref[1:2, :] += jnp.sum(dyc * vv1, axis=0, keepdims=True)
        dcw_ref[2:3, :] += jnp.sum(dyc * vv, axis=0, keepdims=True)

        lse_t, dl_t = lc_ref[...].T, dlc_ref[...].T
        lse_nt, dl_nt = ln_ref[...].T, dln_ref[...].T

        def stack_rows(tile_t, heads):
            return jnp.concatenate([tile_t[hd:hd + 1, :] for hd in heads], axis=1)

        lane8 = jnp.where(lax.broadcasted_iota(jnp.int32, (8, 128), 0) == 0,
                          lax.broadcasted_iota(jnp.int32, (8, 128), 1), -1)
        dsk = jnp.zeros((8, 128), F32)
        for kv in range(N_SWA_KV):
            heads = range(kv * SWA_GROUP, (kv + 1) * SWA_GROUP)
            kc = pc_ref[:, COL_K + HEAD_DIM * kv:COL_K + HEAD_DIM * (kv + 1)]
            vc = pc_ref[:, COL_V + HEAD_DIM * kv:COL_V + HEAD_DIM * (kv + 1)]
            kp = pkv_ref[:, HEAD_DIM * kv:HEAD_DIM * (kv + 1)]
            vp = pkv_ref[:, D_KV + HEAD_DIM * kv:D_KV + HEAD_DIM * (kv + 1)]
            qg = qc_ref[kv * SWA_GROUP:(kv + 1) * SWA_GROUP].reshape(GROUP_ROWS, HEAD_DIM)
            dog = doc_ref[kv * SWA_GROUP:(kv + 1) * SWA_GROUP].reshape(GROUP_ROWS, HEAD_DIM)
            qn = qn_ref[kv * SWA_GROUP:(kv + 1) * SWA_GROUP].reshape(GROUP_ROWS, HEAD_DIM)
            don = don_ref[kv * SWA_GROUP:(kv + 1) * SWA_GROUP].reshape(GROUP_ROWS, HEAD_DIM)
            lse_row, dl_row = stack_rows(lse_t, heads), stack_rows(dl_t, heads)
            ptp = jnp.exp(_dot_nt(kp, qg) + bp_ref[0, kv] - lse_row)
            dstp = (ptp * (_dot_nt(vp, dog) - dl_row)).astype(BF16)
            dq = _dot_tn(dstp, kp)
            pt = jnp.exp(_dot_nt(kc, qg) + bct_ref[0, kv] - lse_row)
            dst = (pt * (_dot_nt(vc, dog) - dl_row)).astype(BF16)
            dv = _dot(pt.astype(BF16), dog)
            dk = _dot(dst, qg)
            dq = dq + _dot_tn(dst, kc)
            ptn = jnp.exp(_dot_nt(kc, qn) + bnt_ref[0, kv] - stack_rows(lse_nt, heads))
            dstn = (ptn * (_dot_nt(vc, don) - stack_rows(dl_nt, heads))).astype(BF16)
            dv = dv + _dot(ptn.astype(BF16), don)
            dk = dk + _dot(dstn, qn)
            dp_ref[:, COL_K + HEAD_DIM * kv:COL_K + HEAD_DIM * (kv + 1)] = dk.astype(BF16)
            dp_ref[:, COL_V + HEAD_DIM * kv:COL_V + HEAD_DIM * (kv + 1)] = dv.astype(BF16)
            sink = jnp.concatenate([jnp.full((1, BLOCK), sk_ref[0, hd], F32) for hd in heads], axis=1)
            sink_term = jnp.exp(sink - lse_row) * dl_row
            for gi, hd in enumerate(heads):
                span = slice(gi * BLOCK, (gi + 1) * BLOCK)
                dp_ref[:, _q_col(hd):_q_col(hd) + HEAD_DIM] = (dq[span] * SCALE).astype(BF16)
                dsk = dsk + jnp.where(lane8 == hd, -jnp.sum(sink_term[:, span], axis=1, keepdims=True), 0.0)
        dsk_ref[...] += dsk

        for hm in range(N_MEM_HEADS):
            hd = N_SWA_HEADS + hm
            qm, dom = qc_ref[hd], doc_ref[hd]
            mk = mkv_ref[:, HEAD_DIM * hm:HEAD_DIM * (hm + 1)]
            mv = mkv_ref[:, D_MEMQ + HEAD_DIM * hm:D_MEMQ + HEAD_DIM * (hm + 1)]
            pt = jnp.exp(_dot_nt(mk, qm) - lse_t[hd:hd + 1, :])
            dst = (pt * (_dot_nt(mv, dom) - dl_t[hd:hd + 1, :])).astype(BF16)
            dp_ref[:, _q_col(hd):_q_col(hd) + HEAD_DIM] = (_dot_tn(dst, mk) * SCALE).astype(BF16)
            dmkv_ref[:, HEAD_DIM * hm:HEAD_DIM * (hm + 1)] += _dot(dst, qm)
            dmkv_ref[:, D_MEMQ + HEAD_DIM * hm:D_MEMQ + HEAD_DIM * (hm + 1)] += _dot(pt.astype(BF16), dom)

    cur = lambda i: (i, 0)
    const = lambda i: (0, 0)
    rows16 = BLOCK // 16
    last16 = t // 16 - 1
    before = lambda col: (lambda i: (jnp.maximum(i * rows16 - 1, 0), col))
    after = lambda i: (jnp.minimum((i + 1) * rows16, last16), 0)
    heads_cur = lambda i: (0, i, 0)
    heads_next = lambda i: (0, jnp.minimum(i + 1, nb - 1), 0)
    stat_next = lambda i: (jnp.minimum(i + 1, nb - 1), 0)
    key_block = (1, N_SWA_KV, BLOCK, GROUP_ROWS)
    head_block = (N_HEADS, BLOCK, HEAD_DIM)
    return _pcall(
        body, name="mix_core_bwd", grid=(nb,),
        in_specs=[pl.BlockSpec(memory_space=pltpu.SMEM),
                  pl.BlockSpec((BLOCK, D_IN), cur),
                  pl.BlockSpec((BLOCK, 2 * D_KV), lambda i: (jnp.maximum(i - 1, 0), COL_K // (2 * D_KV))),
                  pl.BlockSpec((16, D_CONV), before(COL_CG // D_CONV)),
                  pl.BlockSpec((16, D_CONV), before(COL_U // D_CONV)),
                  pl.BlockSpec((16, D_CONV), after),
                  pl.BlockSpec((BLOCK, D_CONV), cur),
                  pl.BlockSpec((16, D_CONV), after),
                  pl.BlockSpec(head_block, heads_cur), pl.BlockSpec(head_block, heads_next),
                  pl.BlockSpec(head_block, heads_cur), pl.BlockSpec(head_block, heads_next),
                  pl.BlockSpec((BLOCK, 128), cur), pl.BlockSpec((BLOCK, 128), stat_next),
                  pl.BlockSpec((BLOCK, 128), cur), pl.BlockSpec((BLOCK, 128), stat_next),
                  pl.BlockSpec((m, 2 * D_MEMQ), const),
                  pl.BlockSpec((3, D_CONV), const),
                  pl.BlockSpec(key_block, lambda i: (jnp.where(i == 0, BIAS_NONE, BIAS_PREV), 0, 0, 0)),
                  pl.BlockSpec(key_block, lambda i: (BIAS_CUR, 0, 0, 0)),
                  pl.BlockSpec(key_block, lambda i: (jnp.where(i == nb - 1, BIAS_NONE, BIAS_PREV), 0, 0, 0))],
        out_specs=[pl.BlockSpec((BLOCK, D_IN), cur),
                   pl.BlockSpec((m, 2 * D_MEMQ), const),
                   pl.BlockSpec((8, D_CONV), const),
                   pl.BlockSpec((8, 128), const)],
        out_shape=[jax.ShapeDtypeStruct((t, D_IN), BF16),
                   jax.ShapeDtypeStruct((m, 2 * D_MEMQ), F32),
                   jax.ShapeDtypeStruct((8, D_CONV), F32),
                   jax.ShapeDtypeStruct((8, 128), F32)],
        compiler_params=_params(("arbitrary",)),
    )(sinks, p, p, p, p, p, dyconv, dyconv, qh, qh, doh, doh, delta, delta, lse, lse, mkv, convw,
      bias_key, bias_key, bias_key)


def _group_norms(y):
    out = []
    for a, b in MIX_GROUPS:
        ys = y[:, a:b]
        r = _rstd(ys)
        out.append((ys * r, r))
    return out


def _mix_out_fwd(y, h, g, wout):
    t, d = h.shape
    tm = _tok_block(t)

    def body(y_ref, h_ref, g_ref, w_ref, ho_ref, mt_ref):
        yhat = jnp.concatenate([yh for yh, _ in _group_norms(y_ref[...])], axis=-1)
        mixed = yhat * g_ref[...]
        mt_ref[...] = mixed.T.astype(BF16)
        ho_ref[...] = h_ref[...] + _dot(mixed.astype(BF16), w_ref[...])

    return _pcall(
        body, name="mix_out_fwd", grid=(t // tm,),
        in_specs=[pl.BlockSpec((tm, D_MIX), lambda i: (i, 0)),
                  pl.BlockSpec((tm, d), lambda i: (i, 0)),
                  pl.BlockSpec((1, D_MIX), lambda i: (0, 0)),
                  pl.BlockSpec((D_MIX, d), lambda i: (0, 0))],
        out_specs=[pl.BlockSpec((tm, d), lambda i: (i, 0)),
                   pl.BlockSpec((D_MIX, tm), lambda i: (0, i))],
        out_shape=[jax.ShapeDtypeStruct((t, d), F32), jax.ShapeDtypeStruct((D_MIX, t), BF16)],
        compiler_params=_params(("parallel",)),
    )(y, h, g, wout)


def _head_indicator():
    ind = np.zeros((D_MIX, 128), np.float32)
    for hd in range(N_HEADS):
        ind[_head_cols(hd):_head_cols(hd) + HEAD_DIM, hd] = 1.0
    return jnp.asarray(ind, BF16)


def _mix_out_bwd(dho, y, g, wout, mt, dep):
    t, d = dho.shape
    tm = _tok_block(t)
    ni = t // tm

    def body(dho_ref, y_ref, g_ref, w_ref, mt_ref, ind_ref, dep_ref, dyc_ref, doh_ref, dl_ref, dw_ref, dg_ref, acc_ref):
        i = pl.program_id(0)
        dhb = dho_ref[...].astype(BF16)
        dm = _dot_nt(dhb, w_ref[...])
        pw = _dot(mt_ref[...], dhb)
        gg = g_ref[...]
        yy = y_ref[...]
        dys = []
        dgs = []
        for (a, b), (yhat, r) in zip(MIX_GROUPS, _group_norms(yy)):
            dmg = dm[:, a:b]
            dgs.append(_sum8(dmg * yhat))
            dyh = dmg * gg[:, a:b]
            dys.append(r * (dyh - yhat * jnp.mean(dyh * yhat, axis=-1, keepdims=True)))
        dy = jnp.concatenate(dys, axis=-1)
        dyc_ref[...] = dy[:, 0:D_CONV]
        for hd in range(N_HEADS):
            doh_ref[hd] = dy[:, _head_cols(hd):_head_cols(hd) + HEAD_DIM].astype(BF16)
        prod = dy * yy
        hi = prod.astype(BF16)
        lo = (prod - hi.astype(F32)).astype(BF16)
        dl_ref[...] = _dot(hi, ind_ref[...]) + _dot(lo, ind_ref[...])
        part = jnp.concatenate(dgs, axis=-1)

        @pl.when(i == 0)
        def _():
            acc_ref[...] = pw
            dg_ref[...] = part

        @pl.when(i > 0)
        def _():
            acc_ref[...] += pw
            dg_ref[...] += part

        @pl.when(i == ni - 1)
        def _():
            dw_ref[...] = acc_ref[...].astype(BF16)

    return _pcall(
        body, name="mix_out_bwd", grid=(ni,),
        in_specs=[pl.BlockSpec((tm, d), lambda i: (i, 0)),
                  pl.BlockSpec((tm, D_MIX), lambda i: (i, 0)),
                  pl.BlockSpec((1, D_MIX), lambda i: (0, 0)),
                  pl.BlockSpec((D_MIX, d), lambda i: (0, 0)),
                  pl.BlockSpec((D_MIX, tm), lambda i: (0, i)),
                  pl.BlockSpec((D_MIX, 128), lambda i: (0, 0)),
                  pl.BlockSpec(memory_space=pl.ANY)],
        out_specs=[pl.BlockSpec((tm, D_CONV), lambda i: (i, 0)),
                   pl.BlockSpec((N_HEADS, tm, HEAD_DIM), lambda i: (0, i, 0)),
                   pl.BlockSpec((tm, 128), lambda i: (i, 0)),
                   pl.BlockSpec((D_MIX, d), lambda i: (0, 0)),
                   pl.BlockSpec((8, D_MIX), lambda i: (0, 0))],
        out_shape=[jax.ShapeDtypeStruct((t, D_CONV), F32),
                   jax.ShapeDtypeStruct((N_HEADS, t, HEAD_DIM), BF16),
                   jax.ShapeDtypeStruct((t, 128), F32),
                   jax.ShapeDtypeStruct((D_MIX, d), BF16),
                   jax.ShapeDtypeStruct((8, D_MIX), F32)],
        scratch_shapes=[pltpu.VMEM((D_MIX, d), F32)],
        compiler_params=_params(("arbitrary",)),
    )(dho, y, g, wout, mt, _head_indicator(), dep)


def _mix_proj_bwd(dp, dho, h, g, win_t, n):
    t, d = h.shape
    tm = _tok_block(t)
    ni = t // tm

    def body(dp_ref, dho_ref, h_ref, g_ref, w_ref, n_ref, dh_ref, dw_ref, dg_ref, acc_ref):
        i = pl.program_id(0)
        dpb = dp_ref[...]
        dn = _dot(dpb, w_ref[...])

        @pl.when(i == 0)
        def _():
            acc_ref[...] = jnp.zeros_like(acc_ref)

        acc_ref[...] += _dot_tn(dpb, n_ref[...])
        hh = h_ref[...]
        r = _rstd(hh)
        xhat = hh * r
        dxh = dn * g_ref[...]
        dh_ref[...] = dho_ref[...] + r * (dxh - xhat * jnp.mean(dxh * xhat, axis=-1, keepdims=True))
        part = _sum8(dn * xhat)

        @pl.when(i == 0)
        def _():
            dg_ref[...] = part

        @pl.when(i > 0)
        def _():
            dg_ref[...] += part

        @pl.when(i == ni - 1)
        def _():
            dw_ref[...] = acc_ref[...].astype(BF16)

    return _pcall(
        body, name="mix_proj_bwd", grid=(ni,),
        in_specs=[pl.BlockSpec((tm, D_IN), lambda i: (i, 0)),
                  pl.BlockSpec((tm, d), lambda i: (i, 0)),
                  pl.BlockSpec((tm, d), lambda i: (i, 0)),
                  pl.BlockSpec((1, d), lambda i: (0, 0)),
                  pl.BlockSpec((D_IN, d), lambda i: (0, 0)),
                  pl.BlockSpec((tm, d), lambda i: (i, 0))],
        out_specs=[pl.BlockSpec((tm, d), lambda i: (i, 0)),
                   pl.BlockSpec((D_IN, d), lambda i: (0, 0)),
                   pl.BlockSpec((8, d), lambda i: (0, 0))],
        out_shape=[jax.ShapeDtypeStruct((t, d), F32),
                   jax.ShapeDtypeStruct((D_IN, d), BF16),
                   jax.ShapeDtypeStruct((8, d), F32)],
        scratch_shapes=[pltpu.VMEM((D_IN, d), F32)],
        compiler_params=_params(("arbitrary",)),
    )(dp, dho, h, g, win_t, n)


def _final_loss(h, g, tgt):
    t, d = h.shape
    tm = _tok_block(t)

    def body(h_ref, g_ref, t_ref, dh_ref, ls_ref, dg_ref):
        i = pl.program_id(0)
        hh = h_ref[...]
        r = _rstd(hh)
        xhat = hh * r
        gg = g_ref[...]
        err = xhat * gg - t_ref[...]
        dy = err * (1.0 / d)
        dxh = dy * gg
        dh_ref[...] = r * (dxh - xhat * jnp.mean(dxh * xhat, axis=-1, keepdims=True))
        lpart = _sum8(err * err)
        gpart = _sum8(dy * xhat)

        @pl.when(i == 0)
        def _():
            ls_ref[...] = lpart
            dg_ref[...] = gpart

        @pl.when(i > 0)
        def _():
            ls_ref[...] += lpart
            dg_ref[...] += gpart

    return _pcall(
        body, name="final_loss", grid=(t // tm,),
        in_specs=[pl.BlockSpec((tm, d), lambda i: (i, 0)),
                  pl.BlockSpec((1, d), lambda i: (0, 0)),
                  pl.BlockSpec((tm, d), lambda i: (i, 0))],
        out_specs=[pl.BlockSpec((tm, d), lambda i: (i, 0)),
                   pl.BlockSpec((8, d), lambda i: (0, 0)),
                   pl.BlockSpec((8, d), lambda i: (0, 0))],
        out_shape=[jax.ShapeDtypeStruct((t, d), F32),
                   jax.ShapeDtypeStruct((8, d), F32),
                   jax.ShapeDtypeStruct((8, d), F32)],
        compiler_params=_params(("arbitrary",)),
    )(h, g, tgt)


def _position():
    return lax.axis_index("x"), lax.axis_index("y"), lax.axis_index("c")


def _flip(v, bit):
    return 1 - v if bit else v


def _peer(k):
    x, y, c = _position()
    return _flip(x, k & 4), _flip(y, k & 2), _flip(c, k & 1)


def _slot(px, py, pc):
    return 4 * px + 2 * py + pc


def _handshake(peers):
    barrier = pltpu.get_barrier_semaphore()
    for peer in peers:
        pl.semaphore_signal(barrier, inc=1, device_id=peer, device_id_type=MESH)
    pl.semaphore_wait(barrier, len(peers))


def _sequencer_call(body, name, collective_id, out_type, scratch_types, operands):
    return pl.kernel(
        body, out_type=out_type, mesh=plsc.ScalarSubcoreMesh(axis_name="sequencer", num_cores=1), name=name,
        scratch_types=scratch_types, compiler_params=pltpu.CompilerParams(collective_id=collective_id),
    )(*operands)


def _all_gather(shards, name, collective_id):
    nt = len(shards)

    def body(*refs):
        xs = refs[:nt]
        outs = refs[nt:2 * nt]
        send_sems, recv_sems, local_sems = refs[2 * nt:]
        x, y, c = _position()
        me, sibling = (x, y, c), (x, y, 1 - c)
        chips = [(1 - x, y), (x, 1 - y), (1 - x, 1 - y)]
        _handshake([sibling] + [(*chip, c) for chip in chips])

        def copy(t, k, block, to, src=None):
            dst = outs[t].at[_slot(*block)]
            return pltpu.make_async_remote_copy(
                src_ref=dst if src is None else src, dst_ref=dst,
                send_sem=send_sems.at[t, k], recv_sem=recv_sems.at[t, k],
                device_id=to, device_id_type=MESH)

        mine = [pltpu.make_async_copy(xs[t], outs[t].at[_slot(*me)], local_sems.at[t]) for t in range(nt)]
        for cp in mine:
            cp.start()
        first = []
        for t in range(nt):
            first.append(copy(t, 0, me, sibling, src=xs[t]))
            first += [copy(t, 1 + j, me, (*chip, c), src=xs[t]) for j, chip in enumerate(chips)]
        for cp in first:
            cp.start()
        passed = []
        for j, chip in enumerate(chips):
            for t in range(nt):
                copy(t, 1 + j, (*chip, c), me).wait_recv()
                fwd = copy(t, 4 + j, (*chip, c), sibling)
                fwd.start()
                passed.append(fwd)
        for t in range(nt):
            copy(t, 0, sibling, me).wait_recv()
            for j, chip in enumerate(chips):
                copy(t, 4 + j, (*chip, 1 - c), me).wait_recv()
        for cp in first + passed:
            cp.wait_send()
        for cp in mine:
            cp.wait()

    return _sequencer_call(
        body, name, collective_id,
        out_type=[jax.ShapeDtypeStruct((N_DEV,) + s.shape, s.dtype) for s in shards],
        scratch_types=[pltpu.SemaphoreType.DMA((nt, 7)), pltpu.SemaphoreType.DMA((nt, 7)),
                       pltpu.SemaphoreType.DMA((nt,))],
        operands=shards)


def _scatter_copy(srcs, lands, send_sems, recv_sems, t, k):
    peer = _peer(k)
    return pltpu.make_async_remote_copy(
        src_ref=srcs[t].at[_slot(*peer)], dst_ref=lands[t].at[k],
        send_sem=send_sems.at[t * (N_DEV - 1) + k - 1], recv_sem=recv_sems.at[t * (N_DEV - 1) + k - 1],
        device_id=peer, device_id_type=MESH)


def _scatter_start(partials, name):
    nt = len(partials)

    def body(*refs):
        srcs, lands = refs[:nt], refs[nt:2 * nt]
        send_sems, recv_sems = refs[2 * nt], refs[2 * nt + 1]
        token = refs[-1]
        for k in range(1, N_DEV):
            for t in range(nt):
                _scatter_copy(srcs, lands, send_sems, recv_sems, t, k).start()
        token[...] = jnp.zeros_like(token)

    hbm = pl.BlockSpec(memory_space=pltpu.HBM)
    sem = pl.BlockSpec(memory_space=pltpu.SEMAPHORE)
    shapes = [pltpu.HBM(p.shape, p.dtype) for p in partials]
    lands = [pltpu.with_memory_space_constraint(lax.empty(p.shape, p.dtype), pltpu.HBM) for p in partials]
    srcs = [pltpu.with_memory_space_constraint(p, pltpu.HBM) for p in partials]
    out = _pcall(
        body, name=name,
        out_shape=[pltpu.SemaphoreType.DMA((nt * (N_DEV - 1),))] * 2 + shapes + shapes
        + [jax.ShapeDtypeStruct((8, 128), F32)],
        in_specs=[hbm] * (2 * nt),
        out_specs=[sem, sem] + [hbm] * (2 * nt) + [pl.BlockSpec(memory_space=pltpu.VMEM)],
        input_output_aliases={i: 2 + i for i in range(2 * nt)},
        compiler_params=pltpu.CompilerParams(has_side_effects=pltpu.SideEffectType.DATAFLOW_SIDE_EFFECTING),
    )(*srcs, *lands)
    return (nt, name, out[:-1]), out[-1]


def _scatter_wait(state, after):
    nt, name, (send_sems, recv_sems, *thru) = state

    def body(*refs):
        srcs, lands = refs[:nt], refs[nt:2 * nt]
        send_sems, recv_sems = refs[2 * nt], refs[2 * nt + 1]
        for k in range(1, N_DEV):
            for t in range(nt):
                copy = _scatter_copy(srcs, lands, send_sems, recv_sems, t, k)
                copy.wait_send()
                copy.wait_recv()

    hbm = pl.BlockSpec(memory_space=pltpu.HBM)
    sem = pl.BlockSpec(memory_space=pltpu.SEMAPHORE)
    out = _pcall(
        body, name=name + "_wait",
        out_shape=[pltpu.HBM(a.shape, a.dtype) for a in thru],
        in_specs=[hbm] * (2 * nt) + [sem, sem, pl.BlockSpec(memory_space=pl.ANY)],
        out_specs=[hbm] * (2 * nt),
        input_output_aliases={i: i for i in range(2 * nt)},
        compiler_params=pltpu.CompilerParams(has_side_effects=pltpu.SideEffectType.DATAFLOW_SIDE_EFFECTING),
    )(*thru, send_sems, recv_sems, after)
    return out[:nt], out[nt:]


def _all_reduce_rows(v):
    nv, _, w = v.shape

    def body(v_ref, out_ref, gath_ref, send_sems, recv_sems):
        x, y, c = _position()
        me = _slot(x, y, c)

        def copy(k):
            return pltpu.make_async_remote_copy(
                src_ref=v_ref, dst_ref=gath_ref.at[me],
                send_sem=send_sems.at[k - 1], recv_sem=recv_sems.at[k - 1],
                device_id=_peer(k), device_id_type=MESH)

        def arrival(k):
            return pltpu.make_async_remote_copy(
                src_ref=v_ref, dst_ref=gath_ref.at[_slot(*_peer(k))],
                send_sem=send_sems.at[k - 1], recv_sem=recv_sems.at[k - 1],
                device_id=_peer(k), device_id_type=MESH)

        sent = [copy(k) for k in range(1, N_DEV)]
        for cp in sent:
            cp.start()
        gath_ref[me] = v_ref[...]
        for k in range(1, N_DEV):
            arrival(k).wait_recv()
        for cp in sent:
            cp.wait_send()
        total = gath_ref[0]
        for s in range(1, N_DEV):
            total = total + gath_ref[s]
        out_ref[...] = jnp.sum(total, axis=1)

    vmem = pl.BlockSpec(memory_space=pltpu.VMEM)
    return _pcall(
        body, name="all_reduce_rows",
        in_specs=[vmem], out_specs=vmem,
        out_shape=jax.ShapeDtypeStruct((nv, w), F32),
        scratch_shapes=[pltpu.VMEM((N_DEV, nv, 8, w), F32),
                        pltpu.SemaphoreType.DMA((7,)), pltpu.SemaphoreType.DMA((7,))],
    )(v)


def _adamw_math(w, g, m, v):
    m2 = ADAM_B1 * m + (1.0 - ADAM_B1) * g
    v2 = ADAM_B2 * v + (1.0 - ADAM_B2) * (g * g)
    m_hat = m2 / (1.0 - ADAM_B1 ** ADAM_STEP)
    v_hat = v2 / (1.0 - ADAM_B2 ** ADAM_STEP)
    delta = -ADAM_LR * (m_hat / (jnp.sqrt(v_hat) + ADAM_EPS) + ADAM_WD * w)
    return delta, m2, v2


def _row_block(r):
    for cand in (256, 176, 128):
        if r % cand == 0:
            return cand
    return r


def _adamw_sharded(me, grads, w, m, v, dep):
    (own0, land0), (own1, land1) = grads
    _, r, c = land0.shape
    tr = _row_block(r)
    nr = r // tr

    def body(me_ref, o0_ref, l0_ref, o1_ref, l1_ref, w_ref, m_ref, v_ref, dep_ref, g_ref, d_ref, m2_ref, v2_ref):
        layer = pl.program_id(0)

        def total(own_ref, land_ref):
            acc = own_ref[0].astype(F32)
            for k in range(1, N_DEV):
                acc = acc + land_ref[k].astype(F32)
            return acc

        g = jnp.where(layer == 0, total(o0_ref, l0_ref), total(o1_ref, l1_ref))
        delta, m2, v2 = _adamw_math(w_ref[0], g, m_ref[0], v_ref[0])
        g_ref[0] = g
        d_ref[0] = delta
        m2_ref[0] = m2
        v2_ref[0] = v2

    rows0 = lambda l, i: jnp.where(l == 0, i, nr - 1)
    rows1 = lambda l, i: jnp.where(l == 1, i, 0)
    shard = pl.BlockSpec((1, tr, c), lambda l, i, me_ref: (l, i, 0))
    out = jax.ShapeDtypeStruct((2, r, c), F32)
    return _pcall(
        body, name="adamw_sharded",
        grid_spec=pltpu.PrefetchScalarGridSpec(
            num_scalar_prefetch=1, grid=(2, nr),
            in_specs=[pl.BlockSpec((1, tr, c), lambda l, i, me_ref: (me_ref[0], rows0(l, i), 0)),
                      pl.BlockSpec((N_DEV, tr, c), lambda l, i, me_ref: (0, rows0(l, i), 0)),
                      pl.BlockSpec((1, tr, c), lambda l, i, me_ref: (me_ref[0], rows1(l, i), 0)),
                      pl.BlockSpec((N_DEV, tr, c), lambda l, i, me_ref: (0, rows1(l, i), 0)),
                      shard, shard, shard, pl.BlockSpec(memory_space=pl.ANY)],
            out_specs=[shard, shard, shard, shard]),
        out_shape=[out, out, out, out],
        compiler_params=_params(("arbitrary", "arbitrary")),
    )(me, own0, land0, own1, land1, w, m, v, dep)


def _adamw_small(w, g, m, v):
    def body(w_ref, g_ref, m_ref, v_ref, d_ref, m2_ref, v2_ref):
        delta, m2, v2 = _adamw_math(w_ref[...], g_ref[...], m_ref[...], v_ref[...])
        d_ref[...] = delta
        m2_ref[...] = m2
        v2_ref[...] = v2

    spec = pl.BlockSpec(w.shape, lambda i: (0, 0))
    out = jax.ShapeDtypeStruct(w.shape, F32)
    return _pcall(
        body, name="adamw_small", grid=(1,),
        in_specs=[spec] * 4, out_specs=[spec] * 3, out_shape=[out] * 3,
        compiler_params=_params(("arbitrary",)),
    )(w, g, m, v)


def _pack(arrs):
    flat = jnp.concatenate([a.reshape(-1) for a in arrs])
    n = flat.shape[0]
    rows = -(-n // 1024) * 8
    return jnp.pad(flat, (0, rows * 128 - n)).reshape(rows, 128)


def _unpack(packed, like):
    flat = packed.reshape(-1)
    out, off = [], 0
    for a in like:
        out.append(flat[off:off + a.size].reshape(a.shape))
        off += a.size
    return out


def kernel(x, mem, g_ffn1, w_ffn1_up, w_ffn1_down, g_mix, w_in, conv_w, sinks, g_mem, w_mem_kv, g_grp, w_out, g_ffn2, w_ffn2_up, w_ffn2_down, g_final, loss_target, m_g_ffn1, m_w_ffn1_up, m_w_ffn1_down, m_g_mix, m_w_in, m_conv_w, m_sinks, m_g_mem, m_w_mem_kv, m_g_grp, m_w_out, m_g_ffn2, m_w_ffn2_up, m_w_ffn2_down, m_g_final, v_g_ffn1, v_w_ffn1_up, v_w_ffn1_down, v_g_mix, v_w_in, v_conv_w, v_sinks, v_g_mem, v_w_mem_kv, v_g_grp, v_w_out, v_g_ffn2, v_w_ffn2_up, v_w_ffn2_down, v_g_final):
    depth = g_ffn1.shape[0]
    t, d = x.shape[1], x.shape[2]
    width = max(d, D_MIX)
    me = _slot(*_position())
    conv_shard = conv_w.shape[2]

    xin, memin, tgt = x[0], mem[0], loss_target[0]

    conv_tile = jnp.zeros((depth * 8, 128), F32).at[:, :conv_shard].set(
        jnp.pad(conv_w, ((0, 0), (0, 8 - conv_w.shape[1]), (0, 0))).reshape(depth * 8, conv_shard))
    tr = lambda a: jnp.swapaxes(a, -1, -2)
    bf = lambda a: a.astype(BF16)
    weights = []
    collective_id = 0
    for l in range(depth):
        groups = [[bf(tr(w_ffn1_up[l])), bf(w_ffn1_down[l])] + ([conv_tile] if l == 0 else []),
                  [bf(tr(w_in[l])), bf(w_mem_kv[l]), bf(w_out[l])],
                  [bf(tr(w_ffn2_up[l])), bf(w_ffn2_down[l])]]
        full = []
        for gi, shards in enumerate(groups):
            full.append(_all_gather(shards, f"all_gather_l{l}_g{gi}", collective_id))
            collective_id += 1
        if l == 0:
            conv_full = full[0][2].reshape(N_DEV, depth, 8, 128)[:, :, :3, :conv_shard]
            conv_full = conv_full.transpose(1, 2, 0, 3).reshape(depth, 3, N_DEV * conv_shard)
        weights.append(dict(
            up1=full[0][0].reshape(2, -1, d), dn1=full[0][1].reshape(-1, d),
            win=full[1][0].reshape(D_IN, d), wkv=full[1][1].reshape(d, 2 * D_MEMQ), wout=full[1][2].reshape(D_MIX, d),
            up2=full[2][0].reshape(2, -1, d), dn2=full[2][1].reshape(-1, d)))

    row = lambda a: a.reshape(1, -1)
    bias_key = _bias_table()

    h = xin
    saved = []
    for l in range(depth):
        wl = weights[l]
        s = dict(h0=h)
        h, s["gu1"], s["n1"] = _ffn_fwd(h, row(g_ffn1[l]), wl["up1"], wl["dn1"])
        s["h1"] = h
        s["p"], s["n_mix"], s["qh"] = _mix_proj_fwd(h, row(g_mix[l]), wl["win"])
        s["mkv"], s["nt_mem"] = _memkv_fwd(memin, row(g_mem[l]), wl["wkv"], s["p"])
        s["y"], s["lse"] = _mix_core_fwd(s["p"], s["qh"], s["mkv"], conv_full[l], row(sinks[l]), bias_key)
        h, s["mt"] = _mix_out_fwd(s["y"], h, row(g_grp[l]), wl["wout"])
        s["h2"] = h
        h, s["gu2"], s["n2"] = _ffn_fwd(h, row(g_ffn2[l]), wl["up2"], wl["dn2"])
        saved.append(s)

    dh, loss_part, dg_final = _final_loss(h, row(g_final), tgt)

    small = {}
    dep = loss_part

    started = []

    def scatter(names, partials, label):
        state, token = _scatter_start(partials, f"scatter_grads_{label}")
        started.append((names, state))
        return token

    for l in reversed(range(depth)):
        wl, s = weights[l], saved[l]
        dh, agu, dyb, small["g_ffn2", l] = _ffn_bwd_act(dh, s["h2"], row(g_ffn2[l]), s["gu2"], wl["up2"], wl["dn2"], dep)
        ddn2 = _ffn_bwd_w(agu, 2, 1, dyb, agu, f"ffn_bwd_w_down_l{l}_ffn2").reshape(N_DEV, -1, d)
        dup2 = _ffn_bwd_w(agu, 0, 2, s["n2"], ddn2, f"ffn_bwd_w_up_l{l}_ffn2").reshape(N_DEV, -1, d)
        dep = scatter([("w_ffn2_up", l), ("w_ffn2_down", l)], [dup2, ddn2], f"l{l}_ffn2")
        dyconv, doh, delta, dwout, small["g_grp", l] = _mix_out_bwd(dh, s["y"], row(g_grp[l]), wl["wout"], s["mt"], dep)
        dp, dmkv, small["conv_w", l], small["sinks", l] = _mix_core_bwd(
            s["p"], s["qh"], dyconv, doh, delta, s["lse"], s["mkv"], conv_full[l], row(sinks[l]), bias_key)
        dwkv, small["g_mem", l] = _memkv_bwd(dmkv, memin, row(g_mem[l]), wl["wkv"], s["nt_mem"])
        dh, dwin, small["g_mix", l] = _mix_proj_bwd(dp, dh, s["h1"], row(g_mix[l]), wl["win"], s["n_mix"])
        dep = scatter([("w_in", l), ("w_mem_kv", l), ("w_out", l)],
                      [dwin.reshape(N_DEV, -1, d), dwkv.reshape(N_DEV, -1, 2 * D_MEMQ), dwout.reshape(N_DEV, -1, d)],
                      f"l{l}_mix")
        dh, agu, dyb, small["g_ffn1", l] = _ffn_bwd_act(dh, s["h0"], row(g_ffn1[l]), s["gu1"], wl["up1"], wl["dn1"], dep)
        ddn1 = _ffn_bwd_w(agu, 2, 1, dyb, agu, f"ffn_bwd_w_down_l{l}_ffn1").reshape(N_DEV, -1, d)
        if l > 0:
            dup1 = _ffn_bwd_w(agu, 0, 2, s["n1"], ddn1, f"ffn_bwd_w_up_l{l}_ffn1").reshape(N_DEV, -1, d)
            dep = scatter([("w_ffn1_up", l), ("w_ffn1_down", l)], [dup1, ddn1], f"l{l}_ffn1")
        else:
            dep = scatter([("w_ffn1_down", l)], [ddn1], f"l{l}_ffn1_down")
            dup1 = _ffn_bwd_w(agu, 0, 2, s["n1"], dep, f"ffn_bwd_w_up_l{l}_ffn1").reshape(N_DEV, -1, d)
            dep = scatter([("w_ffn1_up", l)], [dup1], f"l{l}_ffn1_up")
    grad_x = dh[None]

    big = {"w_ffn2_up": (w_ffn2_up, m_w_ffn2_up, v_w_ffn2_up, True), "w_ffn2_down": (w_ffn2_down, m_w_ffn2_down, v_w_ffn2_down, False),
           "w_in": (w_in, m_w_in, v_w_in, True), "w_mem_kv": (w_mem_kv, m_w_mem_kv, v_w_mem_kv, False),
           "w_out": (w_out, m_w_out, v_w_out, False), "w_ffn1_up": (w_ffn1_up, m_w_ffn1_up, v_w_ffn1_up, True),
           "w_ffn1_down": (w_ffn1_down, m_w_ffn1_down, v_w_ffn1_down, False)}
    me_index = jnp.reshape(me, (1,)).astype(jnp.int32)
    sharded, landed = {}, {}

    def finish(groups, after):
        for names, state in groups:
            owns, lands = _scatter_wait(state, after)
            for key, own, land in zip(names, owns, lands):
                landed[key] = (own, land)
            after = lands[0]
            for name in dict.fromkeys(n for n, _ in names):
                if name not in sharded and all((name, l) in landed for l in range(depth)):
                    w, m, v, transposed = big[name]
                    fix = tr if transposed else (lambda a: a)
                    res = _adamw_sharded(me_index, [landed[name, l] for l in range(depth)], fix(w), fix(m), fix(v), after)
                    sharded[name] = tuple(fix(r) for r in res)
                    after = res[0]
        return after

    dep = finish(started[:-2], dep)

    def lanes(a):
        return jnp.pad(a, ((0, 0), (0, width - a.shape[1])))

    def first_row(a):
        return lanes(jnp.pad(a, ((0, 8 - a.shape[0]), (0, 0))))

    vec_names = ["g_ffn1", "g_mix", "g_mem", "g_grp", "g_ffn2", "sinks"]
    tiles = [lanes(small[n, l]) for n in vec_names for l in range(depth)]
    tiles += [first_row(small["conv_w", l][k:k + 1]) for l in range(depth) for k in range(3)]
    tiles.append(lanes(dg_final))
    n_real = len(tiles)
    tiles.append(lanes(loss_part))
    tiles.append(lanes(dep[0, :8, :128]))
    tiles += [jnp.zeros((8, width), F32)] * (-len(tiles) % 8)
    summed = _all_reduce_rows(jnp.stack(tiles))
    loss = 0.5 * jnp.sum(summed[n_real]) / d

    def vec(n, wd):
        return jnp.stack([summed[vec_names.index(n) * depth + l, :wd] for l in range(depth)])

    conv_base = len(vec_names) * depth
    conv_grad = jnp.stack([jnp.stack([summed[conv_base + 3 * l + k, :D_CONV] for k in range(3)]) for l in range(depth)])
    grads_small = {
        "g_ffn1": vec("g_ffn1", d), "g_mix": vec("g_mix", d), "g_mem": vec("g_mem", d),
        "g_grp": vec("g_grp", D_MIX), "g_ffn2": vec("g_ffn2", d), "sinks": vec("sinks", N_SWA_HEADS),
        "conv_w": lax.dynamic_slice_in_dim(conv_grad, me * conv_shard, conv_shard, axis=2),
        "g_final": summed[n_real - 1, :d],
    }
    small_w = [("g_ffn1", g_ffn1, m_g_ffn1, v_g_ffn1), ("g_mix", g_mix, m_g_mix, v_g_mix),
               ("conv_w", conv_w, m_conv_w, v_conv_w), ("sinks", sinks, m_sinks, v_sinks),
               ("g_mem", g_mem, m_g_mem, v_g_mem), ("g_grp", g_grp, m_g_grp, v_g_grp),
               ("g_ffn2", g_ffn2, m_g_ffn2, v_g_ffn2), ("g_final", g_final, m_g_final, v_g_final)]
    like = [w for _, w, _, _ in small_w]
    packed = _adamw_small(_pack(like), _pack([grads_small[n] for n, _, _, _ in small_w]),
                          _pack([m for _, _, m, _ in small_w]), _pack([v for _, _, _, v in small_w]))
    small_out = {n: (grads_small[n], dl, m2, v2)
                 for (n, _, _, _), dl, m2, v2 in zip(small_w, *[_unpack(pk, like) for pk in packed])}

    finish(started[-2:], packed[0])

    order = ["g_ffn1", "w_ffn1_up", "w_ffn1_down", "g_mix", "w_in", "conv_w", "sinks", "g_mem", "w_mem_kv", "g_grp",
             "w_out", "g_ffn2", "w_ffn2_up", "w_ffn2_down", "g_final"]
    results = {**sharded, **small_out}
    outs = [loss, grad_x]
    for part in range(4):
        outs += [results[n][part] for n in order]
    return tuple(outs)
```

```python
import numpy as np
import jax
import jax.numpy as jnp
from jax import lax
from jax.experimental import pallas as pl
from jax.experimental.pallas import tpu as pltpu
from jax.experimental.pallas import tpu_sc as plsc

F32 = jnp.float32
BF16 = jnp.bfloat16

N_DEV = 8
EPS = 1e-6
N_SWA_HEADS = 8
N_SWA_KV = 2
SWA_GROUP = N_SWA_HEADS // N_SWA_KV
HEAD_DIM = 64
N_MEM_HEADS = 4
D_CONV = 256
BLOCK = 128
D_SWA = N_SWA_HEADS * HEAD_DIM
D_KV = N_SWA_KV * HEAD_DIM
D_MEMQ = N_MEM_HEADS * HEAD_DIM
D_MIX = D_CONV + D_SWA + D_MEMQ
D_IN = 3 * D_CONV + D_SWA + 2 * D_KV + D_MEMQ
COL_BG, COL_CG, COL_U = 0, D_CONV, 2 * D_CONV
COL_Q = 3 * D_CONV
COL_K = COL_Q + D_SWA
COL_V = COL_K + D_KV
COL_QM = COL_V + D_KV
MIX_GROUPS = ((0, D_CONV), (D_CONV, D_CONV + D_SWA), (D_CONV + D_SWA, D_MIX))
SLOPES = tuple(2.0 ** (-8.0 * (i + 1) / N_SWA_HEADS) for i in range(N_SWA_HEADS))
SCALE = HEAD_DIM ** -0.5
NEG = -1e30

ADAM_LR = 0.001
ADAM_B1 = 0.9
ADAM_B2 = 0.999
ADAM_EPS = 1e-08
ADAM_WD = 0.01
ADAM_STEP = 10

V7X_VMEM_BYTES = 64 * 1024 * 1024
VMEM_LIMIT = (V7X_VMEM_BYTES * 3) // 4
MESH = pl.DeviceIdType.MESH


def _pcall(body, **kw):
    return pl.pallas_call(body, **kw)


def _params(sem=None, vmem=VMEM_LIMIT):
    return pltpu.CompilerParams(dimension_semantics=sem, vmem_limit_bytes=vmem)


def _dot(a, b):
    return lax.dot_general(a, b, (((1,), (0,)), ((), ())), preferred_element_type=F32)


def _dot_nt(a, b):
    return lax.dot_general(a, b, (((1,), (1,)), ((), ())), preferred_element_type=F32)


def _dot_tn(a, b):
    return lax.dot_general(a, b, (((0,), (0,)), ((), ())), preferred_element_type=F32)


def _rstd(x):
    return lax.rsqrt(jnp.mean(x * x, axis=-1, keepdims=True) + EPS)


def _sigmoid(x):
    return 1.0 / (1.0 + jnp.exp(-x))


def _sum8(x):
    r, w = x.shape
    return jnp.sum(x.reshape(r // 8, 8, w), axis=0)


def _tok_block(t, rows=512):
    return min(rows, t)


def _feat_block(f):
    return f // (N_DEV // 2)


def _ffn_fwd(h, g, wup_t, wdn):
    t, d = h.shape
    f = wdn.shape[0]
    tm, tf = _tok_block(t), _feat_block(f)
    ni, nj = t // tm, f // tf

    def body(h_ref, g_ref, wup_ref, wdn_ref, ho_ref, gu_ref, n_ref, nt_ref, acc_ref):
        j = pl.program_id(1)

        @pl.when(j == 0)
        def _():
            hh = h_ref[...]
            n = hh * _rstd(hh) * g_ref[...]
            n_ref[...] = n.astype(BF16)
            nt_ref[...] = n.T.astype(BF16)
            acc_ref[...] = jnp.zeros_like(acc_ref)

        nt = nt_ref[...]
        gate = _dot(wup_ref[0], nt)
        up = _dot(wup_ref[1], nt)
        gu_ref[0] = gate.astype(BF16)
        gu_ref[1] = up.astype(BF16)
        a = gate * _sigmoid(gate) * up
        acc_ref[...] += _dot_tn(a.astype(BF16), wdn_ref[...])

        @pl.when(j == nj - 1)
        def _():
            ho_ref[...] = h_ref[...] + 0.5 * acc_ref[...]

    return _pcall(
        body, name="ffn_fwd", grid=(ni, nj),
        in_specs=[pl.BlockSpec((tm, d), lambda i, j: (i, 0)),
                  pl.BlockSpec((1, d), lambda i, j: (0, 0)),
                  pl.BlockSpec((2, tf, d), lambda i, j: (0, j, 0)),
                  pl.BlockSpec((tf, d), lambda i, j: (j, 0))],
        out_specs=[pl.BlockSpec((tm, d), lambda i, j: (i, 0)),
                   pl.BlockSpec((2, tf, tm), lambda i, j: (0, j, i)),
                   pl.BlockSpec((tm, d), lambda i, j: (i, 0))],
        out_shape=[jax.ShapeDtypeStruct((t, d), F32),
                   jax.ShapeDtypeStruct((2, f, t), BF16),
                   jax.ShapeDtypeStruct((t, d), BF16)],
        scratch_shapes=[pltpu.VMEM((d, tm), BF16), pltpu.VMEM((tm, d), F32)],
        compiler_params=_params(("parallel", "arbitrary")),
    )(h, g, wup_t, wdn)


def _ffn_bwd_act(dho, h, g, gu, wup_t, wdn, dep):
    t, d = h.shape
    f = wdn.shape[0]
    tm, tf = _tok_block(t), _feat_block(f)
    ni, nj = t // tm, f // tf

    def body(dho_ref, h_ref, g_ref, gu_ref, wup_ref, wdn_ref, dep_ref, dh_ref, agu_ref, dyb_ref, dg_ref, dyt_ref, acc_ref):
        i = pl.program_id(0)
        j = pl.program_id(1)

        @pl.when(j == 0)
        def _():
            dy0 = 0.5 * dho_ref[...]
            dyb_ref[...] = dy0.astype(BF16)
            dyt_ref[...] = dy0.T.astype(BF16)
            acc_ref[...] = jnp.zeros_like(acc_ref)

        da = _dot(wdn_ref[...], dyt_ref[...]).astype(BF16)
        gate = gu_ref[0]
        up = gu_ref[1]
        sg = _sigmoid(gate)
        silu = gate * sg
        dgate = da * up * (sg * (1.0 + gate * (1.0 - sg)))
        dup = da * silu
        agu_ref[0] = dgate
        agu_ref[1] = dup
        agu_ref[2] = silu * up
        acc_ref[...] += _dot_tn(dgate, wup_ref[0])
        acc_ref[...] += _dot_tn(dup, wup_ref[1])

        @pl.when(j == nj - 1)
        def _():
            hh = h_ref[...]
            r = _rstd(hh)
            xhat = hh * r
            dnf = acc_ref[...]
            dxh = dnf * g_ref[...]
            dh_ref[...] = dho_ref[...] + r * (dxh - xhat * jnp.mean(dxh * xhat, axis=-1, keepdims=True))
            part = _sum8(dnf * xhat)

            @pl.when(i == 0)
            def _():
                dg_ref[...] = part

            @pl.when(i > 0)
            def _():
                dg_ref[...] += part

    return _pcall(
        body, name="ffn_bwd_act", grid=(ni, nj),
        in_specs=[pl.BlockSpec((tm, d), lambda i, j: (i, 0)),
                  pl.BlockSpec((tm, d), lambda i, j: (i, 0)),
                  pl.BlockSpec((1, d), lambda i, j: (0, 0)),
                  pl.BlockSpec((2, tf, tm), lambda i, j: (0, j, i)),
                  pl.BlockSpec((2, tf, d), lambda i, j: (0, j, 0)),
                  pl.BlockSpec((tf, d), lambda i, j: (j, 0)),
                  pl.BlockSpec(memory_space=pl.ANY)],
        out_specs=[pl.BlockSpec((tm, d), lambda i, j: (i, 0)),
                   pl.BlockSpec((3, tf, tm), lambda i, j: (0, j, i)),
                   pl.BlockSpec((tm, d), lambda i, j: (i, 0)),
                   pl.BlockSpec((8, d), lambda i, j: (0, 0))],
        out_shape=[jax.ShapeDtypeStruct((t, d), F32),
                   jax.ShapeDtypeStruct((3, f, t), BF16),
                   jax.ShapeDtypeStruct((t, d), BF16),
                   jax.ShapeDtypeStruct((8, d), F32)],
        scratch_shapes=[pltpu.VMEM((d, tm), BF16), pltpu.VMEM((tm, d), F32)],
        compiler_params=_params(("arbitrary", "arbitrary")),
    )(dho, h, g, gu, wup_t, wdn, dep)


def _ffn_bwd_w(agu, first, count, rhs, dep, name):
    _, f, t = agu.shape
    d = rhs.shape[1]
    tm, tf = _tok_block(t, 2048), _feat_block(f)
    ni, nj = t // tm, f // tf

    def body(lhs_ref, rhs_ref, dep_ref, dw_ref, acc_ref):
        i = pl.program_id(1)
        @pl.when(i == 0)
        def _():
            acc_ref[...] = jnp.zeros_like(acc_ref)

        rb = rhs_ref[...]
        for k in range(count):
            acc_ref[k] += _dot(lhs_ref[k], rb)

        @pl.when(i == ni - 1)
        def _():
            dw_ref[...] = acc_ref[...].astype(BF16)

    return _pcall(
        body, name=name, grid=(nj, ni),
        in_specs=[pl.BlockSpec((count, tf, tm), lambda j, i: (first // count, j, i)),
                  pl.BlockSpec((tm, d), lambda j, i: (i, 0)),
                  pl.BlockSpec(memory_space=pl.ANY)],
        out_specs=pl.BlockSpec((count, tf, d), lambda j, i: (0, j, 0)),
        out_shape=jax.ShapeDtypeStruct((count, f, d), BF16),
        scratch_shapes=[pltpu.VMEM((count, tf, d), F32)],
        compiler_params=_params(("parallel", "arbitrary")),
    )(agu, rhs, dep)


N_HEADS = N_SWA_HEADS + N_MEM_HEADS


def _q_col(hd):
    return COL_Q + HEAD_DIM * hd if hd < N_SWA_HEADS else COL_QM + HEAD_DIM * (hd - N_SWA_HEADS)


def _mix_proj_fwd(h, g, win_t):
    t, d = h.shape
    tm = _tok_block(t)

    def body(h_ref, g_ref, win_ref, p_ref, n_ref, qh_ref):
        hh = h_ref[...]
        n = (hh * _rstd(hh) * g_ref[...]).astype(BF16)
        n_ref[...] = n
        proj = _dot_nt(n, win_ref[...])
        p_ref[...] = proj.astype(BF16)
        for hd in range(N_HEADS):
            c0 = _q_col(hd)
            qh_ref[hd] = (proj[:, c0:c0 + HEAD_DIM] * SCALE).astype(BF16)

    return _pcall(
        body, name="mix_proj_fwd", grid=(t // tm,),
        in_specs=[pl.BlockSpec((tm, d), lambda i: (i, 0)),
                  pl.BlockSpec((1, d), lambda i: (0, 0)),
                  pl.BlockSpec((D_IN, d), lambda i: (0, 0))],
        out_specs=[pl.BlockSpec((tm, D_IN), lambda i: (i, 0)),
                   pl.BlockSpec((tm, d), lambda i: (i, 0)),
                   pl.BlockSpec((N_HEADS, tm, HEAD_DIM), lambda i: (0, i, 0))],
        out_shape=[jax.ShapeDtypeStruct((t, D_IN), BF16), jax.ShapeDtypeStruct((t, d), BF16),
                   jax.ShapeDtypeStruct((N_HEADS, t, HEAD_DIM), BF16)],
        compiler_params=_params(("parallel",)),
    )(h, g, win_t)


def _memkv_fwd(mem, g, wkv, dep):
    m, d = mem.shape

    def body(mem_ref, g_ref, w_ref, dep_ref, mkv_ref, nt_ref):
        mm = mem_ref[...]
        n = mm * _rstd(mm) * g_ref[...]
        nt_ref[...] = n.T.astype(BF16)
        mkv_ref[...] = _dot(n.astype(BF16), w_ref[...]).astype(BF16)

    return _pcall(
        body, name="memkv_fwd", grid=(1,),
        in_specs=[pl.BlockSpec((m, d), lambda i: (0, 0)),
                  pl.BlockSpec((1, d), lambda i: (0, 0)),
                  pl.BlockSpec((d, 2 * D_MEMQ), lambda i: (0, 0)),
                  pl.BlockSpec(memory_space=pl.ANY)],
        out_specs=[pl.BlockSpec((m, 2 * D_MEMQ), lambda i: (0, 0)),
                   pl.BlockSpec((d, m), lambda i: (0, 0))],
        out_shape=[jax.ShapeDtypeStruct((m, 2 * D_MEMQ), BF16), jax.ShapeDtypeStruct((d, m), BF16)],
        compiler_params=_params(("arbitrary",)),
    )(mem, g, wkv, dep)


def _memkv_bwd(dmkv, mem, g, wkv, nt):
    m, d = mem.shape

    def body(dmkv_ref, mem_ref, g_ref, w_ref, nt_ref, dw_ref, dg_ref):
        db = dmkv_ref[...].astype(BF16)
        dw_ref[...] = _dot(nt_ref[...], db).astype(BF16)
        dn = _dot_nt(db, w_ref[...])
        mm = mem_ref[...]
        dg_ref[...] = _sum8(dn * (mm * _rstd(mm)))

    return _pcall(
        body, name="memkv_bwd", grid=(1,),
        in_specs=[pl.BlockSpec((m, 2 * D_MEMQ), lambda i: (0, 0)),
                  pl.BlockSpec((m, d), lambda i: (0, 0)),
                  pl.BlockSpec((1, d), lambda i: (0, 0)),
                  pl.BlockSpec((d, 2 * D_MEMQ), lambda i: (0, 0)),
                  pl.BlockSpec((d, m), lambda i: (0, 0))],
        out_specs=[pl.BlockSpec((d, 2 * D_MEMQ), lambda i: (0, 0)),
                   pl.BlockSpec((8, d), lambda i: (0, 0))],
        out_shape=[jax.ShapeDtypeStruct((d, 2 * D_MEMQ), BF16), jax.ShapeDtypeStruct((8, d), F32)],
        compiler_params=_params(("arbitrary",)),
    )(dmkv, mem, g, wkv, nt)


def _shift_rows(v, k, edge_rows, row):
    out = pltpu.roll(v, k, 0)
    for r in range(k):
        out = jnp.where(row == r, edge_rows[r], out)
    return out


def _shift_rows_up(v, k, edge_rows, row):
    n = v.shape[0]
    out = pltpu.roll(v, n - k, 0)
    for r in range(k):
        out = jnp.where(row == n - k + r, edge_rows[r], out)
    return out


GROUP_ROWS = SWA_GROUP * BLOCK
BIAS_CUR, BIAS_PREV, BIAS_NONE = 0, 1, 2


def _bias_table():
    tq = np.arange(BLOCK)[:, None]
    sk = np.arange(BLOCK)[None, :]
    slopes = np.asarray(SLOPES, np.float32)[:, None, None]
    cur = np.where(tq >= sk, -slopes * (tq - sk).astype(np.float32), NEG)
    prev = np.where(sk > tq, -slopes * (tq + BLOCK - sk).astype(np.float32), NEG)
    none = np.full_like(cur, NEG)
    tok = np.stack([cur, prev, none]).astype(np.float32).reshape(3, N_SWA_KV, GROUP_ROWS, BLOCK)
    return jnp.asarray(np.ascontiguousarray(tok.transpose(0, 1, 3, 2)))


def _head_cols(hd):
    return D_CONV + HEAD_DIM * hd


def _mix_core_fwd(p, qh, mkv, convw, sinks, bias_key):
    t = p.shape[0]
    m = mkv.shape[0]
    nb = t // BLOCK

    def body(sk_ref, pc_ref, pkv_ref, ppc_ref, ppu_ref, qh_ref, mkv_ref, cw_ref, bc_ref, bp_ref, y_ref, l_ref):
        i = pl.program_id(0)
        prevf = (i > 0).astype(F32)
        row = lax.broadcasted_iota(jnp.int32, (BLOCK, D_CONV), 0)

        bg = pc_ref[:, COL_BG:COL_BG + D_CONV].astype(F32)
        cg = pc_ref[:, COL_CG:COL_CG + D_CONV].astype(F32)
        u = pc_ref[:, COL_U:COL_U + D_CONV].astype(F32)
        vv = cg * u
        pvv = ppc_ref[...].astype(F32) * ppu_ref[...].astype(F32) * prevf
        vv1 = _shift_rows(vv, 1, [pvv[15:16]], row)
        vv2 = _shift_rows(vv, 2, [pvv[14:15], pvv[15:16]], row)
        w = cw_ref[...]
        y_ref[:, 0:D_CONV] = bg * (w[0:1] * vv2 + w[1:2] * vv1 + w[2:3] * vv)

        head_row = lax.broadcasted_iota(jnp.int32, (128, BLOCK), 0)
        lse_t = jnp.zeros((128, BLOCK), F32)
        for kv in range(N_SWA_KV):
            heads = range(kv * SWA_GROUP, (kv + 1) * SWA_GROUP)
            kc = pc_ref[:, COL_K + HEAD_DIM * kv:COL_K + HEAD_DIM * (kv + 1)]
            vc = pc_ref[:, COL_V + HEAD_DIM * kv:COL_V + HEAD_DIM * (kv + 1)]
            kp = pkv_ref[:, HEAD_DIM * kv:HEAD_DIM * (kv + 1)]
            vp = pkv_ref[:, D_KV + HEAD_DIM * kv:D_KV + HEAD_DIM * (kv + 1)]
            qg = qh_ref[kv * SWA_GROUP:(kv + 1) * SWA_GROUP].reshape(GROUP_ROWS, HEAD_DIM)
            sc = _dot_nt(kc, qg) + bc_ref[0, kv]
            sp = _dot_nt(kp, qg) + bp_ref[0, kv]
            sink = jnp.concatenate([jnp.full((1, BLOCK), sk_ref[0, hd], F32) for hd in heads], axis=1)
            mx = jnp.maximum(jnp.max(jnp.maximum(sc, sp), axis=0, keepdims=True), sink)
            ec = jnp.exp(sc - mx)
            ep = jnp.exp(sp - mx)
            den = jnp.sum(ec + ep, axis=0, keepdims=True) + jnp.exp(sink - mx)
            ot = (_dot_tn(vc, ec.astype(BF16)) + _dot_tn(vp, ep.astype(BF16))) / den
            lse = mx + jnp.log(den)
            for gi, hd in enumerate(heads):
                span = slice(gi * BLOCK, (gi + 1) * BLOCK)
                y_ref[:, _head_cols(hd):_head_cols(hd) + HEAD_DIM] = ot[:, span].T
                lse_t = jnp.where(head_row == hd, lse[:, span], lse_t)

        for hm in range(N_MEM_HEADS):
            hd = N_SWA_HEADS + hm
            mk = mkv_ref[:, HEAD_DIM * hm:HEAD_DIM * (hm + 1)]
            mv = mkv_ref[:, D_MEMQ + HEAD_DIM * hm:D_MEMQ + HEAD_DIM * (hm + 1)]
            s = _dot_nt(mk, qh_ref[hd])
            mx = jnp.max(s, axis=0, keepdims=True)
            e = jnp.exp(s - mx)
            den = jnp.sum(e, axis=0, keepdims=True)
            y_ref[:, _head_cols(hd):_head_cols(hd) + HEAD_DIM] = (_dot_tn(mv, e.astype(BF16)) / den).T
            lse_t = jnp.where(head_row == hd, mx + jnp.log(den), lse_t)
        l_ref[...] = lse_t.T

    kv_col = COL_K // (2 * D_KV)
    bias_block = (1, N_SWA_KV, BLOCK, GROUP_ROWS)
    return _pcall(
        body, name="mix_core_fwd", grid=(nb,),
        in_specs=[pl.BlockSpec(memory_space=pltpu.SMEM),
                  pl.BlockSpec((BLOCK, D_IN), lambda i: (i, 0)),
                  pl.BlockSpec((BLOCK, 2 * D_KV), lambda i: (jnp.maximum(i - 1, 0), kv_col)),
                  pl.BlockSpec((16, D_CONV), lambda i: (jnp.maximum(i * (BLOCK // 16) - 1, 0), COL_CG // D_CONV)),
                  pl.BlockSpec((16, D_CONV), lambda i: (jnp.maximum(i * (BLOCK // 16) - 1, 0), COL_U // D_CONV)),
                  pl.BlockSpec((N_HEADS, BLOCK, HEAD_DIM), lambda i: (0, i, 0)),
                  pl.BlockSpec((m, 2 * D_MEMQ), lambda i: (0, 0)),
                  pl.BlockSpec((3, D_CONV), lambda i: (0, 0)),
                  pl.BlockSpec(bias_block, lambda i: (BIAS_CUR, 0, 0, 0)),
                  pl.BlockSpec(bias_block, lambda i: (jnp.where(i == 0, BIAS_NONE, BIAS_PREV), 0, 0, 0))],
        out_specs=[pl.BlockSpec((BLOCK, D_MIX), lambda i: (i, 0)),
                   pl.BlockSpec((BLOCK, 128), lambda i: (i, 0))],
        out_shape=[jax.ShapeDtypeStruct((t, D_MIX), F32), jax.ShapeDtypeStruct((t, 128), F32)],
        compiler_params=_params(("parallel",)),
    )(sinks, p, p, p, p, qh, mkv, convw, bias_key, bias_key)


def _mix_core_bwd(p, qh, dyconv, doh, delta, lse, mkv, convw, sinks, bias_key):
    t = p.shape[0]
    m = mkv.shape[0]
    nb = t // BLOCK

    def body(sk_ref, pc_ref, pkv_ref, ppc_ref, ppu_ref, pnb_ref, dyc_ref, dyn_ref, qc_ref, qn_ref, doc_ref, don_ref,
             dlc_ref, dln_ref, lc_ref, ln_ref, mkv_ref, cw_ref, bp_ref, bct_ref, bnt_ref,
             dp_ref, dmkv_ref, dcw_ref, dsk_ref):
        i = pl.program_id(0)
        prevf = (i > 0).astype(F32)
        nextf = (i < nb - 1).astype(F32)
        row = lax.broadcasted_iota(jnp.int32, (BLOCK, D_CONV), 0)

        @pl.when(i == 0)
        def _():
            dmkv_ref[...] = jnp.zeros_like(dmkv_ref)
            dcw_ref[...] = jnp.zeros_like(dcw_ref)
            dsk_ref[...] = jnp.zeros_like(dsk_ref)

        bg = pc_ref[:, COL_BG:COL_BG + D_CONV].astype(F32)
        cg = pc_ref[:, COL_CG:COL_CG + D_CONV].astype(F32)
        u = pc_ref[:, COL_U:COL_U + D_CONV].astype(F32)
        vv = cg * u
        pvv = ppc_ref[...].astype(F32) * ppu_ref[...].astype(F32) * prevf
        vv1 = _shift_rows(vv, 1, [pvv[15:16]], row)
        vv2 = _shift_rows(vv, 2, [pvv[14:15], pvv[15:16]], row)
        w = cw_ref[...]
        yconv = w[0:1] * vv2 + w[1:2] * vv1 + w[2:3] * vv
        dyo = dyc_ref[...]
        dyc = dyo * bg
        nxt = dyn_ref[...] * pnb_ref[...].astype(F32) * nextf
        d1 = _shift_rows_up(dyc, 1, [nxt[0:1]], row)
        d2 = _shift_rows_up(dyc, 2, [nxt[0:1], nxt[1:2]], row)
        dvv = w[2:3] * dyc + w[1:2] * d1 + w[0:1] * d2
        dp_ref[:, COL_BG:COL_BG + D_CONV] = (dyo * yconv).astype(BF16)
        dp_ref[:, COL_CG:COL_CG + D_CONV] = (dvv * u).astype(BF16)
        dp_ref[:, COL_U:COL_U + D_CONV] = (dvv * cg).astype(BF16)
        dcw_ref[0:1, :] += jnp.sum(dyc * vv2, axis=0, keepdims=True)
        dcw_ref[1:2, :] += jnp.sum(dyc * vv1, axis=0, keepdims=True)
        dcw_ref[2:3, :] += jnp.sum(dyc * vv, axis=0, keepdims=True)

        lse_t, dl_t = lc_ref[...].T, dlc_ref[...].T
        lse_nt, dl_nt = ln_ref[...].T, dln_ref[...].T

        def stack_rows(tile_t, heads):
            return jnp.concatenate([tile_t[hd:hd + 1, :] for hd in heads], axis=1)

        lane8 = jnp.where(lax.broadcasted_iota(jnp.int32, (8, 128), 0) == 0,
                          lax.broadcasted_iota(jnp.int32, (8, 128), 1), -1)
        dsk = jnp.zeros((8, 128), F32)
        for kv in range(N_SWA_KV):
            heads = range(kv * SWA_GROUP, (kv + 1) * SWA_GROUP)
            kc = pc_ref[:, COL_K + HEAD_DIM * kv:COL_K + HEAD_DIM * (kv + 1)]
            vc = pc_ref[:, COL_V + HEAD_DIM * kv:COL_V + HEAD_DIM * (kv + 1)]
            kp = pkv_ref[:, HEAD_DIM * kv:HEAD_DIM * (kv + 1)]
            vp = pkv_ref[:, D_KV + HEAD_DIM * kv:D_KV + HEAD_DIM * (kv + 1)]
            qg = qc_ref[kv * SWA_GROUP:(kv + 1) * SWA_GROUP].reshape(GROUP_ROWS, HEAD_DIM)
            dog = doc_ref[kv * SWA_GROUP:(kv + 1) * SWA_GROUP].reshape(GROUP_ROWS, HEAD_DIM)
            qn = qn_ref[kv * SWA_GROUP:(kv + 1) * SWA_GROUP].reshape(GROUP_ROWS, HEAD_DIM)
            don = don_ref[kv * SWA_GROUP:(kv + 1) * SWA_GROUP].reshape(GROUP_ROWS, HEAD_DIM)
            lse_row, dl_row = stack_rows(lse_t, heads), stack_rows(dl_t, heads)
            ptp = jnp.exp(_dot_nt(kp, qg) + bp_ref[0, kv] - lse_row)
            dstp = (ptp * (_dot_nt(vp, dog) - dl_row)).astype(BF16)
            dq = _dot_tn(dstp, kp)
            pt = jnp.exp(_dot_nt(kc, qg) + bct_ref[0, kv] - lse_row)
            dst = (pt * (_dot_nt(vc, dog) - dl_row)).astype(BF16)
            dv = _dot(pt.astype(BF16), dog)
            dk = _dot(dst, qg)
            dq = dq + _dot_tn(dst, kc)
            ptn = jnp.exp(_dot_nt(kc, qn) + bnt_ref[0, kv] - stack_rows(lse_nt, heads))
            dstn = (ptn * (_dot_nt(vc, don) - stack_rows(dl_nt, heads))).astype(BF16)
            dv = dv + _dot(ptn.astype(BF16), don)
            dk = dk + _dot(dstn, qn)
            dp_ref[:, COL_K + HEAD_DIM * kv:COL_K + HEAD_DIM * (kv + 1)] = dk.astype(BF16)
            dp_ref[:, COL_V + HEAD_DIM * kv:COL_V + HEAD_DIM * (kv + 1)] = dv.astype(BF16)
            sink = jnp.concatenate([jnp.full((1, BLOCK), sk_ref[0, hd], F32) for hd in heads], axis=1)
            sink_term = jnp.exp(sink - lse_row) * dl_row
            for gi, hd in enumerate(heads):
                span = slice(gi * BLOCK, (gi + 1) * BLOCK)
                dp_ref[:, _q_col(hd):_q_col(hd) + HEAD_DIM] = (dq[span] * SCALE).astype(BF16)
                dsk = dsk + jnp.where(lane8 == hd, -jnp.sum(sink_term[:, span], axis=1, keepdims=True), 0.0)
        dsk_ref[...] += dsk

        for hm in range(N_MEM_HEADS):
            hd = N_SWA_HEADS + hm
            qm, dom = qc_ref[hd], doc_ref[hd]
            mk = mkv_ref[:, HEAD_DIM * hm:HEAD_DIM * (hm + 1)]
            mv = mkv_ref[:, D_MEMQ + HEAD_DIM * hm:D_MEMQ + HEAD_DIM * (hm + 1)]
            pt = jnp.exp(_dot_nt(mk, qm) - lse_t[hd:hd + 1, :])
            dst = (pt * (_dot_nt(mv, dom) - dl_t[hd:hd + 1, :])).astype(BF16)
            dp_ref[:, _q_col(hd):_q_col(hd) + HEAD_DIM] = (_dot_tn(dst, mk) * SCALE).astype(BF16)
            dmkv_ref[:, HEAD_DIM * hm:HEAD_DIM * (hm + 1)] += _dot(dst, qm)
            dmkv_ref[:, D_MEMQ + HEAD_DIM * hm:D_MEMQ + HEAD_DIM * (hm + 1)] += _dot(pt.astype(BF16), dom)

    cur = lambda i: (i, 0)
    const = lambda i: (0, 0)
    rows16 = BLOCK // 16
    last16 = t // 16 - 1
    before = lambda col: (lambda i: (jnp.maximum(i * rows16 - 1, 0), col))
    after = lambda i: (jnp.minimum((i + 1) * rows16, last16), 0)
    heads_cur = lambda i: (0, i, 0)
    heads_next = lambda i: (0, jnp.minimum(i + 1, nb - 1), 0)
    stat_next = lambda i: (jnp.minimum(i + 1, nb - 1), 0)
    key_block = (1, N_SWA_KV, BLOCK, GROUP_ROWS)
    head_block = (N_HEADS, BLOCK, HEAD_DIM)
    return _pcall(
        body, name="mix_core_bwd", grid=(nb,),
        in_specs=[pl.BlockSpec(memory_space=pltpu.SMEM),
                  pl.BlockSpec((BLOCK, D_IN), cur),
                  pl.BlockSpec((BLOCK, 2 * D_KV), lambda i: (jnp.maximum(i - 1, 0), COL_K // (2 * D_KV))),
                  pl.BlockSpec((16, D_CONV), before(COL_CG // D_CONV)),
                  pl.BlockSpec((16, D_CONV), before(COL_U // D_CONV)),
                  pl.BlockSpec((16, D_CONV), after),
                  pl.BlockSpec((BLOCK, D_CONV), cur),
                  pl.BlockSpec((16, D_CONV), after),
                  pl.BlockSpec(head_block, heads_cur), pl.BlockSpec(head_block, heads_next),
                  pl.BlockSpec(head_block, heads_cur), pl.BlockSpec(head_block, heads_next),
                  pl.BlockSpec((BLOCK, 128), cur), pl.BlockSpec((BLOCK, 128), stat_next),
                  pl.BlockSpec((BLOCK, 128), cur), pl.BlockSpec((BLOCK, 128), stat_next),
                  pl.BlockSpec((m, 2 * D_MEMQ), const),
                  pl.BlockSpec((3, D_CONV), const),
                  pl.BlockSpec(key_block, lambda i: (jnp.where(i == 0, BIAS_NONE, BIAS_PREV), 0, 0, 0)),
                  pl.BlockSpec(key_block, lambda i: (BIAS_CUR, 0, 0, 0)),
                  pl.BlockSpec(key_block, lambda i: (jnp.where(i == nb - 1, BIAS_NONE, BIAS_PREV), 0, 0, 0))],
        out_specs=[pl.BlockSpec((BLOCK, D_IN), cur),
                   pl.BlockSpec((m, 2 * D_MEMQ), const),
                   pl.BlockSpec((8, D_CONV), const),
                   pl.BlockSpec((8, 128), const)],
        out_shape=[jax.ShapeDtypeStruct((t, D_IN), BF16),
                   jax.ShapeDtypeStruct((m, 2 * D_MEMQ), F32),
                   jax.ShapeDtypeStruct((8, D_CONV), F32),
                   jax.ShapeDtypeStruct((8, 128), F32)],
        compiler_params=_params(("arbitrary",)),
    )(sinks, p, p, p, p, p, dyconv, dyconv, qh, qh, doh, doh, delta, delta, lse, lse, mkv, convw,
      bias_key, bias_key, bias_key)


def _group_norms(y):
    out = []
    for a, b in MIX_GROUPS:
        ys = y[:, a:b]
        r = _rstd(ys)
        out.append((ys * r, r))
    return out


def _mix_out_fwd(y, h, g, wout):
    t, d = h.shape
    tm = _tok_block(t)

    def body(y_ref, h_ref, g_ref, w_ref, ho_ref, mt_ref):
        yhat = jnp.concatenate([yh for yh, _ in _group_norms(y_ref[...])], axis=-1)
        mixed = yhat * g_ref[...]
        mt_ref[...] = mixed.T.astype(BF16)
        ho_ref[...] = h_ref[...] + _dot(mixed.astype(BF16), w_ref[...])

    return _pcall(
        body, name="mix_out_fwd", grid=(t // tm,),
        in_specs=[pl.BlockSpec((tm, D_MIX), lambda i: (i, 0)),
                  pl.BlockSpec((tm, d), lambda i: (i, 0)),
                  pl.BlockSpec((1, D_MIX), lambda i: (0, 0)),
                  pl.BlockSpec((D_MIX, d), lambda i: (0, 0))],
        out_specs=[pl.BlockSpec((tm, d), lambda i: (i, 0)),
                   pl.BlockSpec((D_MIX, tm), lambda i: (0, i))],
        out_shape=[jax.ShapeDtypeStruct((t, d), F32), jax.ShapeDtypeStruct((D_MIX, t), BF16)],
        compiler_params=_params(("parallel",)),
    )(y, h, g, wout)


def _head_indicator():
    ind = np.zeros((D_MIX, 128), np.float32)
    for hd in range(N_HEADS):
        ind[_head_cols(hd):_head_cols(hd) + HEAD_DIM, hd] = 1.0
    return jnp.asarray(ind, BF16)


def _mix_out_bwd(dho, y, g, wout, mt, dep):
    t, d = dho.shape
    tm = _tok_block(t)
    ni = t // tm

    def body(dho_ref, y_ref, g_ref, w_ref, mt_ref, ind_ref, dep_ref, dyc_ref, doh_ref, dl_ref, dw_ref, dg_ref, acc_ref):
        i = pl.program_id(0)
        dhb = dho_ref[...].astype(BF16)
        dm = _dot_nt(dhb, w_ref[...])
        pw = _dot(mt_ref[...], dhb)
        gg = g_ref[...]
        yy = y_ref[...]
        dys = []
        dgs = []
        for (a, b), (yhat, r) in zip(MIX_GROUPS, _group_norms(yy)):
            dmg = dm[:, a:b]
            dgs.append(_sum8(dmg * yhat))
            dyh = dmg * gg[:, a:b]
            dys.append(r * (dyh - yhat * jnp.mean(dyh * yhat, axis=-1, keepdims=True)))
        dy = jnp.concatenate(dys, axis=-1)
        dyc_ref[...] = dy[:, 0:D_CONV]
        for hd in range(N_HEADS):
            doh_ref[hd] = dy[:, _head_cols(hd):_head_cols(hd) + HEAD_DIM].astype(BF16)
        prod = dy * yy
        hi = prod.astype(BF16)
        lo = (prod - hi.astype(F32)).astype(BF16)
        dl_ref[...] = _dot(hi, ind_ref[...]) + _dot(lo, ind_ref[...])
        part = jnp.concatenate(dgs, axis=-1)

        @pl.when(i == 0)
        def _():
            acc_ref[...] = pw
            dg_ref[...] = part

        @pl.when(i > 0)
        def _():
            acc_ref[...] += pw
            dg_ref[...] += part

        @pl.when(i == ni - 1)
        def _():
            dw_ref[...] = acc_ref[...].astype(BF16)

    return _pcall(
        body, name="mix_out_bwd", grid=(ni,),
        in_specs=[pl.BlockSpec((tm, d), lambda i: (i, 0)),
                  pl.BlockSpec((tm, D_MIX), lambda i: (i, 0)),
                  pl.BlockSpec((1, D_MIX), lambda i: (0, 0)),
                  pl.BlockSpec((D_MIX, d), lambda i: (0, 0)),
                  pl.BlockSpec((D_MIX, tm), lambda i: (0, i)),
                  pl.BlockSpec((D_MIX, 128), lambda i: (0, 0)),
                  pl.BlockSpec(memory_space=pl.ANY)],
        out_specs=[pl.BlockSpec((tm, D_CONV), lambda i: (i, 0)),
                   pl.BlockSpec((N_HEADS, tm, HEAD_DIM), lambda i: (0, i, 0)),
                   pl.BlockSpec((tm, 128), lambda i: (i, 0)),
                   pl.BlockSpec((D_MIX, d), lambda i: (0, 0)),
                   pl.BlockSpec((8, D_MIX), lambda i: (0, 0))],
        out_shape=[jax.ShapeDtypeStruct((t, D_CONV), F32),
                   jax.ShapeDtypeStruct((N_HEADS, t, HEAD_DIM), BF16),
                   jax.ShapeDtypeStruct((t, 128), F32),
                   jax.ShapeDtypeStruct((D_MIX, d), BF16),
                   jax.ShapeDtypeStruct((8, D_MIX), F32)],
        scratch_shapes=[pltpu.VMEM((D_MIX, d), F32)],
        compiler_params=_params(("arbitrary",)),
    )(dho, y, g, wout, mt, _head_indicator(), dep)


def _mix_proj_bwd(dp, dho, h, g, win_t, n):
    t, d = h.shape
    tm = _tok_block(t)
    ni = t // tm

    def body(dp_ref, dho_ref, h_ref, g_ref, w_ref, n_ref, dh_ref, dw_ref, dg_ref, acc_ref):
        i = pl.program_id(0)
        dpb = dp_ref[...]
        dn = _dot(dpb, w_ref[...])

        @pl.when(i == 0)
        def _():
            acc_ref[...] = jnp.zeros_like(acc_ref)

        acc_ref[...] += _dot_tn(dpb, n_ref[...])
        hh = h_ref[...]
        r = _rstd(hh)
        xhat = hh * r
        dxh = dn * g_ref[...]
        dh_ref[...] = dho_ref[...] + r * (dxh - xhat * jnp.mean(dxh * xhat, axis=-1, keepdims=True))
        part = _sum8(dn * xhat)

        @pl.when(i == 0)
        def _():
            dg_ref[...] = part

        @pl.when(i > 0)
        def _():
            dg_ref[...] += part

        @pl.when(i == ni - 1)
        def _():
            dw_ref[...] = acc_ref[...].astype(BF16)

    return _pcall(
        body, name="mix_proj_bwd", grid=(ni,),
        in_specs=[pl.BlockSpec((tm, D_IN), lambda i: (i, 0)),
                  pl.BlockSpec((tm, d), lambda i: (i, 0)),
                  pl.BlockSpec((tm, d), lambda i: (i, 0)),
                  pl.BlockSpec((1, d), lambda i: (0, 0)),
                  pl.BlockSpec((D_IN, d), lambda i: (0, 0)),
                  pl.BlockSpec((tm, d), lambda i: (i, 0))],
        out_specs=[pl.BlockSpec((tm, d), lambda i: (i, 0)),
                   pl.BlockSpec((D_IN, d), lambda i: (0, 0)),
                   pl.BlockSpec((8, d), lambda i: (0, 0))],
        out_shape=[jax.ShapeDtypeStruct((t, d), F32),
                   jax.ShapeDtypeStruct((D_IN, d), BF16),
                   jax.ShapeDtypeStruct((8, d), F32)],
        scratch_shapes=[pltpu.VMEM((D_IN, d), F32)],
        compiler_params=_params(("arbitrary",)),
    )(dp, dho, h, g, win_t, n)


def _final_loss(h, g, tgt):
    t, d = h.shape
    tm = _tok_block(t)

    def body(h_ref, g_ref, t_ref, dh_ref, ls_ref, dg_ref):
        i = pl.program_id(0)
        hh = h_ref[...]
        r = _rstd(hh)
        xhat = hh * r
        gg = g_ref[...]
        err = xhat * gg - t_ref[...]
        dy = err * (1.0 / d)
        dxh = dy * gg
        dh_ref[...] = r * (dxh - xhat * jnp.mean(dxh * xhat, axis=-1, keepdims=True))
        lpart = _sum8(err * err)
        gpart = _sum8(dy * xhat)

        @pl.when(i == 0)
        def _():
            ls_ref[...] = lpart
            dg_ref[...] = gpart

        @pl.when(i > 0)
        def _():
            ls_ref[...] += lpart
            dg_ref[...] += gpart

    return _pcall(
        body, name="final_loss", grid=(t // tm,),
        in_specs=[pl.BlockSpec((tm, d), lambda i: (i, 0)),
                  pl.BlockSpec((1, d), lambda i: (0, 0)),
                  pl.BlockSpec((tm, d), lambda i: (i, 0))],
        out_specs=[pl.BlockSpec((tm, d), lambda i: (i, 0)),
                   pl.BlockSpec((8, d), lambda i: (0, 0)),
                   pl.BlockSpec((8, d), lambda i: (0, 0))],
        out_shape=[jax.ShapeDtypeStruct((t, d), F32),
                   jax.ShapeDtypeStruct((8, d), F32),
                   jax.ShapeDtypeStruct((8, d), F32)],
        compiler_params=_params(("arbitrary",)),
    )(h, g, tgt)


def _position():
    return lax.axis_index("x"), lax.axis_index("y"), lax.axis_index("c")


def _flip(v, bit):
    return 1 - v if bit else v


def _peer(k):
    x, y, c = _position()
    return _flip(x, k & 4), _flip(y, k & 2), _flip(c, k & 1)


def _slot(px, py, pc):
    return 4 * px + 2 * py + pc


def _handshake(peers):
    barrier = pltpu.get_barrier_semaphore()
    for peer in peers:
        pl.semaphore_signal(barrier, inc=1, device_id=peer, device_id_type=MESH)
    pl.semaphore_wait(barrier, len(peers))


def _sequencer_call(body, name, collective_id, out_type, scratch_types, operands):
    return pl.kernel(
        body, out_type=out_type, mesh=plsc.ScalarSubcoreMesh(axis_name="sequencer", num_cores=1), name=name,
        scratch_types=scratch_types, compiler_params=pltpu.CompilerParams(collective_id=collective_id),
    )(*operands)


def _all_gather(shards, name, collective_id):
    nt = len(shards)

    def body(*refs):
        xs = refs[:nt]
        outs = refs[nt:2 * nt]
        send_sems, recv_sems, local_sems = refs[2 * nt:]
        x, y, c = _position()
        me, sibling = (x, y, c), (x, y, 1 - c)
        chips = [(1 - x, y), (x, 1 - y), (1 - x, 1 - y)]
        _handshake([sibling] + [(*chip, c) for chip in chips])

        def copy(t, k, block, to, src=None):
            dst = outs[t].at[_slot(*block)]
            return pltpu.make_async_remote_copy(
                src_ref=dst if src is None else src, dst_ref=dst,
                send_sem=send_sems.at[t, k], recv_sem=recv_sems.at[t, k],
                device_id=to, device_id_type=MESH)

        mine = [pltpu.make_async_copy(xs[t], outs[t].at[_slot(*me)], local_sems.at[t]) for t in range(nt)]
        for cp in mine:
            cp.start()
        first = []
        for t in range(nt):
            first.append(copy(t, 0, me, sibling, src=xs[t]))
            first += [copy(t, 1 + j, me, (*chip, c), src=xs[t]) for j, chip in enumerate(chips)]
        for cp in first:
            cp.start()
        passed = []
        for j, chip in enumerate(chips):
            for t in range(nt):
                copy(t, 1 + j, (*chip, c), me).wait_recv()
                fwd = copy(t, 4 + j, (*chip, c), sibling)
                fwd.start()
                passed.append(fwd)
        for t in range(nt):
            copy(t, 0, sibling, me).wait_recv()
            for j, chip in enumerate(chips):
                copy(t, 4 + j, (*chip, 1 - c), me).wait_recv()
        for cp in first + passed:
            cp.wait_send()
        for cp in mine:
            cp.wait()

    return _sequencer_call(
        body, name, collective_id,
        out_type=[jax.ShapeDtypeStruct((N_DEV,) + s.shape, s.dtype) for s in shards],
        scratch_types=[pltpu.SemaphoreType.DMA((nt, 7)), pltpu.SemaphoreType.DMA((nt, 7)),
                       pltpu.SemaphoreType.DMA((nt,))],
        operands=shards)


def _scatter_copy(srcs, lands, send_sems, recv_sems, t, k):
    peer = _peer(k)
    return pltpu.make_async_remote_copy(
        src_ref=srcs[t].at[_slot(*peer)], dst_ref=lands[t].at[k],
        send_sem=send_sems.at[t * (N_DEV - 1) + k - 1], recv_sem=recv_sems.at[t * (N_DEV - 1) + k - 1],
        device_id=peer, device_id_type=MESH)


def _scatter_start(partials, name):
    nt = len(partials)

    def body(*refs):
        srcs, lands = refs[:nt], refs[nt:2 * nt]
        send_sems, recv_sems = refs[2 * nt], refs[2 * nt + 1]
        token = refs[-1]
        for k in range(1, N_DEV):
            for t in range(nt):
                _scatter_copy(srcs, lands, send_sems, recv_sems, t, k).start()
        token[...] = jnp.zeros_like(token)

    hbm = pl.BlockSpec(memory_space=pltpu.HBM)
    sem = pl.BlockSpec(memory_space=pltpu.SEMAPHORE)
    shapes = [pltpu.HBM(p.shape, p.dtype) for p in partials]
    lands = [pltpu.with_memory_space_constraint(lax.empty(p.shape, p.dtype), pltpu.HBM) for p in partials]
    srcs = [pltpu.with_memory_space_constraint(p, pltpu.HBM) for p in partials]
    out = _pcall(
        body, name=name,
        out_shape=[pltpu.SemaphoreType.DMA((nt * (N_DEV - 1),))] * 2 + shapes + shapes
        + [jax.ShapeDtypeStruct((8, 128), F32)],
        in_specs=[hbm] * (2 * nt),
        out_specs=[sem, sem] + [hbm] * (2 * nt) + [pl.BlockSpec(memory_space=pltpu.VMEM)],
        input_output_aliases={i: 2 + i for i in range(2 * nt)},
        compiler_params=pltpu.CompilerParams(has_side_effects=pltpu.SideEffectType.DATAFLOW_SIDE_EFFECTING),
    )(*srcs, *lands)
    return (nt, name, out[:-1]), out[-1]


def _scatter_wait(state, after):
    nt, name, (send_sems, recv_sems, *thru) = state

    def body(*refs):
        srcs, lands = refs[:nt], refs[nt:2 * nt]
        send_sems, recv_sems = refs[2 * nt], refs[2 * nt + 1]
        for k in range(1, N_DEV):
            for t in range(nt):
                copy = _scatter_copy(srcs, lands, send_sems, recv_sems, t, k)
                copy.wait_send()
                copy.wait_recv()

    hbm = pl.BlockSpec(memory_space=pltpu.HBM)
    sem = pl.BlockSpec(memory_space=pltpu.SEMAPHORE)
    out = _pcall(
        body, name=name + "_wait",
        out_shape=[pltpu.HBM(a.shape, a.dtype) for a in thru],
        in_specs=[hbm] * (2 * nt) + [sem, sem, pl.BlockSpec(memory_space=pl.ANY)],
        out_specs=[hbm] * (2 * nt),
        input_output_aliases={i: i for i in range(2 * nt)},
        compiler_params=pltpu.CompilerParams(has_side_effects=pltpu.SideEffectType.DATAFLOW_SIDE_EFFECTING),
    )(*thru, send_sems, recv_sems, after)
    return out[:nt], out[nt:]


def _all_reduce_rows(v):
    nv, _, w = v.shape

    def body(v_ref, out_ref, gath_ref, send_sems, recv_sems):
        x, y, c = _position()
        me = _slot(x, y, c)

        def copy(k):
            return pltpu.make_async_remote_copy(
                src_ref=v_ref, dst_ref=gath_ref.at[me],
                send_sem=send_sems.at[k - 1], recv_sem=recv_sems.at[k - 1],
                device_id=_peer(k), device_id_type=MESH)

        def arrival(k):
            return pltpu.make_async_remote_copy(
                src_ref=v_ref, dst_ref=gath_ref.at[_slot(*_peer(k))],
                send_sem=send_sems.at[k - 1], recv_sem=recv_sems.at[k - 1],
                device_id=_peer(k), device_id_type=MESH)

        sent = [copy(k) for k in range(1, N_DEV)]
        for cp in sent:
            cp.start()
        gath_ref[me] = v_ref[...]
        for k in range(1, N_DEV):
            arrival(k).wait_recv()
        for cp in sent:
            cp.wait_send()
        total = gath_ref[0]
        for s in range(1, N_DEV):
            total = total + gath_ref[s]
        out_ref[...] = jnp.sum(total, axis=1)

    vmem = pl.BlockSpec(memory_space=pltpu.VMEM)
    return _pcall(
        body, name="all_reduce_rows",
        in_specs=[vmem], out_specs=vmem,
        out_shape=jax.ShapeDtypeStruct((nv, w), F32),
        scratch_shapes=[pltpu.VMEM((N_DEV, nv, 8, w), F32),
                        pltpu.SemaphoreType.DMA((7,)), pltpu.SemaphoreType.DMA((7,))],
    )(v)


def _adamw_math(w, g, m, v):
    m2 = ADAM_B1 * m + (1.0 - ADAM_B1) * g
    v2 = ADAM_B2 * v + (1.0 - ADAM_B2) * (g * g)
    m_hat = m2 / (1.0 - ADAM_B1 ** ADAM_STEP)
    v_hat = v2 / (1.0 - ADAM_B2 ** ADAM_STEP)
    delta = -ADAM_LR * (m_hat / (jnp.sqrt(v_hat) + ADAM_EPS) + ADAM_WD * w)
    return delta, m2, v2


def _row_block(r):
    for cand in (256, 176, 128):
        if r % cand == 0:
            return cand
    return r


def _adamw_sharded(me, grads, w, m, v, dep):
    (own0, land0), (own1, land1) = grads
    _, r, c = land0.shape
    tr = _row_block(r)
    nr = r // tr

    def body(me_ref, o0_ref, l0_ref, o1_ref, l1_ref, w_ref, m_ref, v_ref, dep_ref, g_ref, d_ref, m2_ref, v2_ref):
        layer = pl.program_id(0)

        def total(own_ref, land_ref):
            acc = own_ref[0].astype(F32)
            for k in range(1, N_DEV):
                acc = acc + land_ref[k].astype(F32)
            return acc

        g = jnp.where(layer == 0, total(o0_ref, l0_ref), total(o1_ref, l1_ref))
        delta, m2, v2 = _adamw_math(w_ref[0], g, m_ref[0], v_ref[0])
        g_ref[0] = g
        d_ref[0] = delta
        m2_ref[0] = m2
        v2_ref[0] = v2

    rows0 = lambda l, i: jnp.where(l == 0, i, nr - 1)
    rows1 = lambda l, i: jnp.where(l == 1, i, 0)
    shard = pl.BlockSpec((1, tr, c), lambda l, i, me_ref: (l, i, 0))
    out = jax.ShapeDtypeStruct((2, r, c), F32)
    return _pcall(
        body, name="adamw_sharded",
        grid_spec=pltpu.PrefetchScalarGridSpec(
            num_scalar_prefetch=1, grid=(2, nr),
            in_specs=[pl.BlockSpec((1, tr, c), lambda l, i, me_ref: (me_ref[0], rows0(l, i), 0)),
                      pl.BlockSpec((N_DEV, tr, c), lambda l, i, me_ref: (0, rows0(l, i), 0)),
                      pl.BlockSpec((1, tr, c), lambda l, i, me_ref: (me_ref[0], rows1(l, i), 0)),
                      pl.BlockSpec((N_DEV, tr, c), lambda l, i, me_ref: (0, rows1(l, i), 0)),
                      shard, shard, shard, pl.BlockSpec(memory_space=pl.ANY)],
            out_specs=[shard, shard, shard, shard]),
        out_shape=[out, out, out, out],
        compiler_params=_params(("arbitrary", "arbitrary")),
    )(me, own0, land0, own1, land1, w, m, v, dep)


def _adamw_small(w, g, m, v):
    def body(w_ref, g_ref, m_ref, v_ref, d_ref, m2_ref, v2_ref):
        delta, m2, v2 = _adamw_math(w_ref[...], g_ref[...], m_ref[...], v_ref[...])
        d_ref[...] = delta
        m2_ref[...] = m2
        v2_ref[...] = v2

    spec = pl.BlockSpec(w.shape, lambda i: (0, 0))
    out = jax.ShapeDtypeStruct(w.shape, F32)
    return _pcall(
        body, name="adamw_small", grid=(1,),
        in_specs=[spec] * 4, out_specs=[spec] * 3, out_shape=[out] * 3,
        compiler_params=_params(("arbitrary",)),
    )(w, g, m, v)


def _pack(arrs):
    flat = jnp.concatenate([a.reshape(-1) for a in arrs])
    n = flat.shape[0]
    rows = -(-n // 1024) * 8
    return jnp.pad(flat, (0, rows * 128 - n)).reshape(rows, 128)


def _unpack(packed, like):
    flat = packed.reshape(-1)
    out, off = [], 0
    for a in like:
        out.append(flat[off:off + a.size].reshape(a.shape))
        off += a.size
    return out


def kernel(x, mem, g_ffn1, w_ffn1_up, w_ffn1_down, g_mix, w_in, conv_w, sinks, g_mem, w_mem_kv, g_grp, w_out, g_ffn2, w_ffn2_up, w_ffn2_down, g_final, loss_target, m_g_ffn1, m_w_ffn1_up, m_w_ffn1_down, m_g_mix, m_w_in, m_conv_w, m_sinks, m_g_mem, m_w_mem_kv, m_g_grp, m_w_out, m_g_ffn2, m_w_ffn2_up, m_w_ffn2_down, m_g_final, v_g_ffn1, v_w_ffn1_up, v_w_ffn1_down, v_g_mix, v_w_in, v_conv_w, v_sinks, v_g_mem, v_w_mem_kv, v_g_grp, v_w_out, v_g_ffn2, v_w_ffn2_up, v_w_ffn2_down, v_g_final):
    depth = g_ffn1.shape[0]
    t, d = x.shape[1], x.shape[2]
    width = max(d, D_MIX)
    me = _slot(*_position())
    conv_shard = conv_w.shape[2]

    xin, memin, tgt = x[0], mem[0], loss_target[0]

    conv_tile = jnp.zeros((depth * 8, 128), F32).at[:, :conv_shard].set(
        jnp.pad(conv_w, ((0, 0), (0, 8 - conv_w.shape[1]), (0, 0))).reshape(depth * 8, conv_shard))
    tr = lambda a: jnp.swapaxes(a, -1, -2)
    bf = lambda a: a.astype(BF16)
    weights = []
    collective_id = 0
    for l in range(depth):
        groups = [[bf(tr(w_ffn1_up[l])), bf(w_ffn1_down[l])] + ([conv_tile] if l == 0 else []),
                  [bf(tr(w_in[l])), bf(w_mem_kv[l]), bf(w_out[l])],
                  [bf(tr(w_ffn2_up[l])), bf(w_ffn2_down[l])]]
        full = []
        for gi, shards in enumerate(groups):
            full.append(_all_gather(shards, f"all_gather_l{l}_g{gi}", collective_id))
            collective_id += 1
        if l == 0:
            conv_full = full[0][2].reshape(N_DEV, depth, 8, 128)[:, :, :3, :conv_shard]
            conv_full = conv_full.transpose(1, 2, 0, 3).reshape(depth, 3, N_DEV * conv_shard)
        weights.append(dict(
            up1=full[0][0].reshape(2, -1, d), dn1=full[0][1].reshape(-1, d),
            win=full[1][0].reshape(D_IN, d), wkv=full[1][1].reshape(d, 2 * D_MEMQ), wout=full[1][2].reshape(D_MIX, d),
            up2=full[2][0].reshape(2, -1, d), dn2=full[2][1].reshape(-1, d)))

    row = lambda a: a.reshape(1, -1)
    bias_key = _bias_table()

    h = xin
    saved = []
    for l in range(depth):
        wl = weights[l]
        s = dict(h0=h)
        h, s["gu1"], s["n1"] = _ffn_fwd(h, row(g_ffn1[l]), wl["up1"], wl["dn1"])
        s["h1"] = h
        s["p"], s["n_mix"], s["qh"] = _mix_proj_fwd(h, row(g_mix[l]), wl["win"])
        s["mkv"], s["nt_mem"] = _memkv_fwd(memin, row(g_mem[l]), wl["wkv"], s["p"])
        s["y"], s["lse"] = _mix_core_fwd(s["p"], s["qh"], s["mkv"], conv_full[l], row(sinks[l]), bias_key)
        h, s["mt"] = _mix_out_fwd(s["y"], h, row(g_grp[l]), wl["wout"])
        s["h2"] = h
        h, s["gu2"], s["n2"] = _ffn_fwd(h, row(g_ffn2[l]), wl["up2"], wl["dn2"])
        saved.append(s)

    dh, loss_part, dg_final = _final_loss(h, row(g_final), tgt)

    small = {}
    dep = loss_part

    started = []

    def scatter(names, partials, label):
        state, token = _scatter_start(partials, f"scatter_grads_{label}")
        started.append((names, state))
        return token

    for l in reversed(range(depth)):
        wl, s = weights[l], saved[l]
        dh, agu, dyb, small["g_ffn2", l] = _ffn_bwd_act(dh, s["h2"], row(g_ffn2[l]), s["gu2"], wl["up2"], wl["dn2"], dep)
        ddn2 = _ffn_bwd_w(agu, 2, 1, dyb, agu, f"ffn_bwd_w_down_l{l}_ffn2").reshape(N_DEV, -1, d)
        dup2 = _ffn_bwd_w(agu, 0, 2, s["n2"], ddn2, f"ffn_bwd_w_up_l{l}_ffn2").reshape(N_DEV, -1, d)
        dep = scatter([("w_ffn2_up", l), ("w_ffn2_down", l)], [dup2, ddn2], f"l{l}_ffn2")
        dyconv, doh, delta, dwout, small["g_grp", l] = _mix_out_bwd(dh, s["y"], row(g_grp[l]), wl["wout"], s["mt"], dep)
        dp, dmkv, small["conv_w", l], small["sinks", l] = _mix_core_bwd(
            s["p"], s["qh"], dyconv, doh, delta, s["lse"], s["mkv"], conv_full[l], row(sinks[l]), bias_key)
        dwkv, small["g_mem", l] = _memkv_bwd(dmkv, memin, row(g_mem[l]), wl["wkv"], s["nt_mem"])
        dh, dwin, small["g_mix", l] = _mix_proj_bwd(dp, dh, s["h1"], row(g_mix[l]), wl["win"], s["n_mix"])
        dep = scatter([("w_in", l), ("w_mem_kv", l), ("w_out", l)],
                      [dwin.reshape(N_DEV, -1, d), dwkv.reshape(N_DEV, -1, 2 * D_MEMQ), dwout.reshape(N_DEV, -1, d)],
                      f"l{l}_mix")
        dh, agu, dyb, small["g_ffn1", l] = _ffn_bwd_act(dh, s["h0"], row(g_ffn1[l]), s["gu1"], wl["up1"], wl["dn1"], dep)
        ddn1 = _ffn_bwd_w(agu, 2, 1, dyb, agu, f"ffn_bwd_w_down_l{l}_ffn1").reshape(N_DEV, -1, d)
        if l > 0:
            dup1 = _ffn_bwd_w(agu, 0, 2, s["n1"], ddn1, f"ffn_bwd_w_up_l{l}_ffn1").reshape(N_DEV, -1, d)
            dep = scatter([("w_ffn1_up", l), ("w_ffn1_down", l)], [dup1, ddn1], f"l{l}_ffn1")
        else:
            dep = scatter([("w_ffn1_down", l)], [ddn1], f"l{l}_ffn1_down")
            dup1 = _ffn_bwd_w(agu, 0, 2, s["n1"], dep, f"ffn_bwd_w_up_l{l}_ffn1").reshape(N_DEV, -1, d)
            dep = scatter([("w_ffn1_up", l)], [dup1], f"l{l}_ffn1_up")
    grad_x = dh[None]

    big = {"w_ffn2_up": (w_ffn2_up, m_w_ffn2_up, v_w_ffn2_up, True), "w_ffn2_down": (w_ffn2_down, m_w_ffn2_down, v_w_ffn2_down, False),
           "w_in": (w_in, m_w_in, v_w_in, True), "w_mem_kv": (w_mem_kv, m_w_mem_kv, v_w_mem_kv, False),
           "w_out": (w_out, m_w_out, v_w_out, False), "w_ffn1_up": (w_ffn1_up, m_w_ffn1_up, v_w_ffn1_up, True),
           "w_ffn1_down": (w_ffn1_down, m_w_ffn1_down, v_w_ffn1_down, False)}
    me_index = jnp.reshape(me, (1,)).astype(jnp.int32)
    sharded, landed = {}, {}

    def finish(groups, after):
        for names, state in groups:
            owns, lands = _scatter_wait(state, after)
            for key, own, land in zip(names, owns, lands):
                landed[key] = (own, land)
            after = lands[0]
            for name in dict.fromkeys(n for n, _ in names):
                if name not in sharded and all((name, l) in landed for l in range(depth)):
                    w, m, v, transposed = big[name]
                    fix = tr if transposed else (lambda a: a)
                    res = _adamw_sharded(me_index, [landed[name, l] for l in range(depth)], fix(w), fix(m), fix(v), after)
                    sharded[name] = tuple(fix(r) for r in res)
                    after = res[0]
        return after

    dep = finish(started[:-2], dep)

    def lanes(a):
        return jnp.pad(a, ((0, 0), (0, width - a.shape[1])))

    def first_row(a):
        return lanes(jnp.pad(a, ((0, 8 - a.shape[0]), (0, 0))))

    vec_names = ["g_ffn1", "g_mix", "g_mem", "g_grp", "g_ffn2", "sinks"]
    tiles = [lanes(small[n, l]) for n in vec_names for l in range(depth)]
    tiles += [first_row(small["conv_w", l][k:k + 1]) for l in range(depth) for k in range(3)]
    tiles.append(lanes(dg_final))
    n_real = len(tiles)
    tiles.append(lanes(loss_part))
    tiles.append(lanes(dep[0, :8, :128]))
    tiles += [jnp.zeros((8, width), F32)] * (-len(tiles) % 8)
    summed = _all_reduce_rows(jnp.stack(tiles))
    loss = 0.5 * jnp.sum(summed[n_real]) / d

    def vec(n, wd):
        return jnp.stack([summed[vec_names.index(n) * depth + l, :wd] for l in range(depth)])

    conv_base = len(vec_names) * depth
    conv_grad = jnp.stack([jnp.stack([summed[conv_base + 3 * l + k, :D_CONV] for k in range(3)]) for l in range(depth)])
    grads_small = {
        "g_ffn1": vec("g_ffn1", d), "g_mix": vec("g_mix", d), "g_mem": vec("g_mem", d),
        "g_grp": vec("g_grp", D_MIX), "g_ffn2": vec("g_ffn2", d), "sinks": vec("sinks", N_SWA_HEADS),
        "conv_w": lax.dynamic_slice_in_dim(conv_grad, me * conv_shard, conv_shard, axis=2),
        "g_final": summed[n_real - 1, :d],
    }
    small_w = [("g_ffn1", g_ffn1, m_g_ffn1, v_g_ffn1), ("g_mix", g_mix, m_g_mix, v_g_mix),
               ("conv_w", conv_w, m_conv_w, v_conv_w), ("sinks", sinks, m_sinks, v_sinks),
               ("g_mem", g_mem, m_g_mem, v_g_mem), ("g_grp", g_grp, m_g_grp, v_g_grp),
               ("g_ffn2", g_ffn2, m_g_ffn2, v_g_ffn2), ("g_final", g_final, m_g_final, v_g_final)]
    like = [w for _, w, _, _ in small_w]
    packed = _adamw_small(_pack(like), _pack([grads_small[n] for n, _, _, _ in small_w]),
                          _pack([m for _, _, m, _ in small_w]), _pack([v for _, _, _, v in small_w]))
    small_out = {n: (grads_small[n], dl, m2, v2)
                 for (n, _, _, _), dl, m2, v2 in zip(small_w, *[_unpack(pk, like) for pk in packed])}

    finish(started[-2:], packed[0])

    order = ["g_ffn1", "w_ffn1_up", "w_ffn1_down", "g_mix", "w_in", "conv_w", "sinks", "g_mem", "w_mem_kv", "g_grp",
             "w_out", "g_ffn2", "w_ffn2_up", "w_ffn2_down", "g_final"]
    results = {**sharded, **small_out}
    outs = [loss, grad_x]
    for part in range(4):
        outs += [results[n][part] for n in order]
    return tuple(outs)
```

```python
import numpy as np
import jax
import jax.numpy as jnp
from jax import lax
from jax.experimental import pallas as pl
from jax.experimental.pallas import tpu as pltpu
from jax.experimental.pallas import tpu_sc as plsc

F32 = jnp.float32
BF16 = jnp.bfloat16

N_DEV = 8
EPS = 1e-6
N_SWA_HEADS = 8
N_SWA_KV = 2
SWA_GROUP = N_SWA_HEADS // N_SWA_KV
HEAD_DIM = 64
N_MEM_HEADS = 4
D_CONV = 256
BLOCK = 128
D_SWA = N_SWA_HEADS * HEAD_DIM
D_KV = N_SWA_KV * HEAD_DIM
D_MEMQ = N_MEM_HEADS * HEAD_DIM
D_MIX = D_CONV + D_SWA + D_MEMQ
D_IN = 3 * D_CONV + D_SWA + 2 * D_KV + D_MEMQ
COL_BG, COL_CG, COL_U = 0, D_CONV, 2 * D_CONV
COL_Q = 3 * D_CONV
COL_K = COL_Q + D_SWA
COL_V = COL_K + D_KV
COL_QM = COL_V + D_KV
MIX_GROUPS = ((0, D_CONV), (D_CONV, D_CONV + D_SWA), (D_CONV + D_SWA, D_MIX))
SLOPES = tuple(2.0 ** (-8.0 * (i + 1) / N_SWA_HEADS) for i in range(N_SWA_HEADS))
SCALE = HEAD_DIM ** -0.5
NEG = -1e30

ADAM_LR = 0.001
ADAM_B1 = 0.9
ADAM_B2 = 0.999
ADAM_EPS = 1e-08
ADAM_WD = 0.01
ADAM_STEP = 10

V7X_VMEM_BYTES = 64 * 1024 * 1024
VMEM_LIMIT = (V7X_VMEM_BYTES * 3) // 4
MESH = pl.DeviceIdType.MESH


def _pcall(body, **kw):
    return pl.pallas_call(body, **kw)


def _params(sem=None, vmem=VMEM_LIMIT):
    return pltpu.CompilerParams(dimension_semantics=sem, vmem_limit_bytes=vmem)


def _dot(a, b):
    return lax.dot_general(a, b, (((1,), (0,)), ((), ())), preferred_element_type=F32)


def _dot_nt(a, b):
    return lax.dot_general(a, b, (((1,), (1,)), ((), ())), preferred_element_type=F32)


def _dot_tn(a, b):
    return lax.dot_general(a, b, (((0,), (0,)), ((), ())), preferred_element_type=F32)


def _rstd(x):
    return lax.rsqrt(jnp.mean(x * x, axis=-1, keepdims=True) + EPS)


def _sigmoid(x):
    return 1.0 / (1.0 + jnp.exp(-x))


def _sum8(x):
    r, w = x.shape
    return jnp.sum(x.reshape(r // 8, 8, w), axis=0)


def _tok_block(t, rows=512):
    return min(rows, t)


def _feat_block(f):
    return f // (N_DEV // 2)


def _ffn_fwd(h, g, wup_t, wdn):
    t, d = h.shape
    f = wdn.shape[0]
    tm, tf = _tok_block(t), _feat_block(f)
    ni, nj = t // tm, f // tf

    def body(h_ref, g_ref, wup_ref, wdn_ref, ho_ref, gu_ref, n_ref, nt_ref, acc_ref):
        j = pl.program_id(1)

        @pl.when(j == 0)
        def _():
            hh = h_ref[...]
            n = hh * _rstd(hh) * g_ref[...]
            n_ref[...] = n.astype(BF16)
            nt_ref[...] = n.T.astype(BF16)
            acc_ref[...] = jnp.zeros_like(acc_ref)

        nt = nt_ref[...]
        gate = _dot(wup_ref[0], nt)
        up = _dot(wup_ref[1], nt)
        gu_ref[0] = gate.astype(BF16)
        gu_ref[1] = up.astype(BF16)
        a = gate * _sigmoid(gate) * up
        acc_ref[...] += _dot_tn(a.astype(BF16), wdn_ref[...])

        @pl.when(j == nj - 1)
        def _():
            ho_ref[...] = h_ref[...] + 0.5 * acc_ref[...]

    return _pcall(
        body, name="ffn_fwd", grid=(ni, nj),
        in_specs=[pl.BlockSpec((tm, d), lambda i, j: (i, 0)),
                  pl.BlockSpec((1, d), lambda i, j: (0, 0)),
                  pl.BlockSpec((2, tf, d), lambda i, j: (0, j, 0)),
                  pl.BlockSpec((tf, d), lambda i, j: (j, 0))],
        out_specs=[pl.BlockSpec((tm, d), lambda i, j: (i, 0)),
                   pl.BlockSpec((2, tf, tm), lambda i, j: (0, j, i)),
                   pl.BlockSpec((tm, d), lambda i, j: (i, 0))],
        out_shape=[jax.ShapeDtypeStruct((t, d), F32),
                   jax.ShapeDtypeStruct((2, f, t), BF16),
                   jax.ShapeDtypeStruct((t, d), BF16)],
        scratch_shapes=[pltpu.VMEM((d, tm), BF16), pltpu.VMEM((tm, d), F32)],
        compiler_params=_params(("parallel", "arbitrary")),
    )(h, g, wup_t, wdn)


def _ffn_bwd_act(dho, h, g, gu, wup_t, wdn, dep):
    t, d = h.shape
    f = wdn.shape[0]
    tm, tf = _tok_block(t), _feat_block(f)
    ni, nj = t // tm, f // tf

    def body(dho_ref, h_ref, g_ref, gu_ref, wup_ref, wdn_ref, dep_ref, dh_ref, agu_ref, dyb_ref, dg_ref, dyt_ref, acc_ref):
        i = pl.program_id(0)
        j = pl.program_id(1)

        @pl.when(j == 0)
        def _():
            dy0 = 0.5 * dho_ref[...]
            dyb_ref[...] = dy0.astype(BF16)
            dyt_ref[...] = dy0.T.astype(BF16)
            acc_ref[...] = jnp.zeros_like(acc_ref)

        da = _dot(wdn_ref[...], dyt_ref[...]).astype(BF16)
        gate = gu_ref[0]
        up = gu_ref[1]
        sg = _sigmoid(gate)
        silu = gate * sg
        dgate = da * up * (sg * (1.0 + gate * (1.0 - sg)))
        dup = da * silu
        agu_ref[0] = dgate
        agu_ref[1] = dup
        agu_ref[2] = silu * up
        acc_ref[...] += _dot_tn(dgate, wup_ref[0])
        acc_ref[...] += _dot_tn(dup, wup_ref[1])

        @pl.when(j == nj - 1)
        def _():
            hh = h_ref[...]
            r = _rstd(hh)
            xhat = hh * r
            dnf = acc_ref[...]
            dxh = dnf * g_ref[...]
            dh_ref[...] = dho_ref[...] + r * (dxh - xhat * jnp.mean(dxh * xhat, axis=-1, keepdims=True))
            part = _sum8(dnf * xhat)

            @pl.when(i == 0)
            def _():
                dg_ref[...] = part

            @pl.when(i > 0)
            def _():
                dg_ref[...] += part

    return _pcall(
        body, name="ffn_bwd_act", grid=(ni, nj),
        in_specs=[pl.BlockSpec((tm, d), lambda i, j: (i, 0)),
                  pl.BlockSpec((tm, d), lambda i, j: (i, 0)),
                  pl.BlockSpec((1, d), lambda i, j: (0, 0)),
                  pl.BlockSpec((2, tf, tm), lambda i, j: (0, j, i)),
                  pl.BlockSpec((2, tf, d), lambda i, j: (0, j, 0)),
                  pl.BlockSpec((tf, d), lambda i, j: (j, 0)),
                  pl.BlockSpec(memory_space=pl.ANY)],
        out_specs=[pl.BlockSpec((tm, d), lambda i, j: (i, 0)),
                   pl.BlockSpec((3, tf, tm), lambda i, j: (0, j, i)),
                   pl.BlockSpec((tm, d), lambda i, j: (i, 0)),
                   pl.BlockSpec((8, d), lambda i, j: (0, 0))],
        out_shape=[jax.ShapeDtypeStruct((t, d), F32),
                   jax.ShapeDtypeStruct((3, f, t), BF16),
                   jax.ShapeDtypeStruct((t, d), BF16),
                   jax.ShapeDtypeStruct((8, d), F32)],
        scratch_shapes=[pltpu.VMEM((d, tm), BF16), pltpu.VMEM((tm, d), F32)],
        compiler_params=_params(("arbitrary", "arbitrary")),
    )(dho, h, g, gu, wup_t, wdn, dep)


def _ffn_bwd_w(agu, first, count, rhs, dep, name):
    _, f, t = agu.shape
    d = rhs.shape[1]
    tm, tf = _tok_block(t, 2048), _feat_block(f)
    ni, nj = t // tm, f // tf

    def body(lhs_ref, rhs_ref, dep_ref, dw_ref, acc_ref):
        i = pl.program_id(1)
        @pl.when(i == 0)
        def _():
            acc_ref[...] = jnp.zeros_like(acc_ref)

        rb = rhs_ref[...]
        for k in range(count):
            acc_ref[k] += _dot(lhs_ref[k], rb)

        @pl.when(i == ni - 1)
        def _():
            dw_ref[...] = acc_ref[...].astype(BF16)

    return _pcall(
        body, name=name, grid=(nj, ni),
        in_specs=[pl.BlockSpec((count, tf, tm), lambda j, i: (first // count, j, i)),
                  pl.BlockSpec((tm, d), lambda j, i: (i, 0)),
                  pl.BlockSpec(memory_space=pl.ANY)],
        out_specs=pl.BlockSpec((count, tf, d), lambda j, i: (0, j, 0)),
        out_shape=jax.ShapeDtypeStruct((count, f, d), BF16),
        scratch_shapes=[pltpu.VMEM((count, tf, d), F32)],
        compiler_params=_params(("parallel", "arbitrary")),
    )(agu, rhs, dep)


N_HEADS = N_SWA_HEADS + N_MEM_HEADS


def _q_col(hd):
    return COL_Q + HEAD_DIM * hd if hd < N_SWA_HEADS else COL_QM + HEAD_DIM * (hd - N_SWA_HEADS)


def _mix_proj_fwd(h, g, win_t):
    t, d = h.shape
    tm = _tok_block(t)

    def body(h_ref, g_ref, win_ref, p_ref, n_ref, qh_ref):
        hh = h_ref[...]
        n = (hh * _rstd(hh) * g_ref[...]).astype(BF16)
        n_ref[...] = n
        proj = _dot_nt(n, win_ref[...])
        p_ref[...] = proj.astype(BF16)
        for hd in range(N_HEADS):
            c0 = _q_col(hd)
            qh_ref[hd] = (proj[:, c0:c0 + HEAD_DIM] * SCALE).astype(BF16)

    return _pcall(
        body, name="mix_proj_fwd", grid=(t // tm,),
        in_specs=[pl.BlockSpec((tm, d), lambda i: (i, 0)),
                  pl.BlockSpec((1, d), lambda i: (0, 0)),
                  pl.BlockSpec((D_IN, d), lambda i: (0, 0))],
        out_specs=[pl.BlockSpec((tm, D_IN), lambda i: (i, 0)),
                   pl.BlockSpec((tm, d), lambda i: (i, 0)),
                   pl.BlockSpec((N_HEADS, tm, HEAD_DIM), lambda i: (0, i, 0))],
        out_shape=[jax.ShapeDtypeStruct((t, D_IN), BF16), jax.ShapeDtypeStruct((t, d), BF16),
                   jax.ShapeDtypeStruct((N_HEADS, t, HEAD_DIM), BF16)],
        compiler_params=_params(("parallel",)),
    )(h, g, win_t)


def _memkv_fwd(mem, g, wkv, dep):
    m, d = mem.shape

    def body(mem_ref, g_ref, w_ref, dep_ref, mkv_ref, nt_ref):
        mm = mem_ref[...]
        n = mm * _rstd(mm) * g_ref[...]
        nt_ref[...] = n.T.astype(BF16)
        mkv_ref[...] = _dot(n.astype(BF16), w_ref[...]).astype(BF16)

    return _pcall(
        body, name="memkv_fwd", grid=(1,),
        in_specs=[pl.BlockSpec((m, d), lambda i: (0, 0)),
                  pl.BlockSpec((1, d), lambda i: (0, 0)),
                  pl.BlockSpec((d, 2 * D_MEMQ), lambda i: (0, 0)),
                  pl.BlockSpec(memory_space=pl.ANY)],
        out_specs=[pl.BlockSpec((m, 2 * D_MEMQ), lambda i: (0, 0)),
                   pl.BlockSpec((d, m), lambda i: (0, 0))],
        out_shape=[jax.ShapeDtypeStruct((m, 2 * D_MEMQ), BF16), jax.ShapeDtypeStruct((d, m), BF16)],
        compiler_params=_params(("arbitrary",)),
    )(mem, g, wkv, dep)


def _memkv_bwd(dmkv, mem, g, wkv, nt):
    m, d = mem.shape

    def body(dmkv_ref, mem_ref, g_ref, w_ref, nt_ref, dw_ref, dg_ref):
        db = dmkv_ref[...].astype(BF16)
        dw_ref[...] = _dot(nt_ref[...], db).astype(BF16)
        dn = _dot_nt(db, w_ref[...])
        mm = mem_ref[...]
        dg_ref[...] = _sum8(dn * (mm * _rstd(mm)))

    return _pcall(
        body, name="memkv_bwd", grid=(1,),
        in_specs=[pl.BlockSpec((m, 2 * D_MEMQ), lambda i: (0, 0)),
                  pl.BlockSpec((m, d), lambda i: (0, 0)),
                  pl.BlockSpec((1, d), lambda i: (0, 0)),
                  pl.BlockSpec((d, 2 * D_MEMQ), lambda i: (0, 0)),
                  pl.BlockSpec((d, m), lambda i: (0, 0))],
        out_specs=[pl.BlockSpec((d, 2 * D_MEMQ), lambda i: (0, 0)),
                   pl.BlockSpec((8, d), lambda i: (0, 0))],
        out_shape=[jax.ShapeDtypeStruct((d, 2 * D_MEMQ), BF16), jax.ShapeDtypeStruct((8, d), F32)],
        compiler_params=_params(("arbitrary",)),
    )(dmkv, mem, g, wkv, nt)


def _shift_rows(v, k, edge_rows, row):
    out = pltpu.roll(v, k, 0)
    for r in range(k):
        out = jnp.where(row == r, edge_rows[r], out)
    return out


def _shift_rows_up(v, k, edge_rows, row):
    n = v.shape[0]
    out = pltpu.roll(v, n - k, 0)
    for r in range(k):
        out = jnp.where(row == n - k + r, edge_rows[r], out)
    return out


GROUP_ROWS = SWA_GROUP * BLOCK
BIAS_CUR, BIAS_PREV, BIAS_NONE = 0, 1, 2


def _bias_table():
    tq = np.arange(BLOCK)[:, None]
    sk = np.arange(BLOCK)[None, :]
    slopes = np.asarray(SLOPES, np.float32)[:, None, None]
    cur = np.where(tq >= sk, -slopes * (tq - sk).astype(np.float32), NEG)
    prev = np.where(sk > tq, -slopes * (tq + BLOCK - sk).astype(np.float32), NEG)
    none = np.full_like(cur, NEG)
    tok = np.stack([cur, prev, none]).astype(np.float32).reshape(3, N_SWA_KV, GROUP_ROWS, BLOCK)
    return jnp.asarray(np.ascontiguousarray(tok.transpose(0, 1, 3, 2)))


def _head_cols(hd):
    return D_CONV + HEAD_DIM * hd


def _mix_core_fwd(p, qh, mkv, convw, sinks, bias_key):
    t = p.shape[0]
    m = mkv.shape[0]
    nb = t // BLOCK

    def body(sk_ref, pc_ref, pkv_ref, ppc_ref, ppu_ref, qh_ref, mkv_ref, cw_ref, bc_ref, bp_ref, y_ref, l_ref):
        i = pl.program_id(0)
        prevf = (i > 0).astype(F32)
        row = lax.broadcasted_iota(jnp.int32, (BLOCK, D_CONV), 0)

        bg = pc_ref[:, COL_BG:COL_BG + D_CONV].astype(F32)
        cg = pc_ref[:, COL_CG:COL_CG + D_CONV].astype(F32)
        u = pc_ref[:, COL_U:COL_U + D_CONV].astype(F32)
        vv = cg * u
        pvv = ppc_ref[...].astype(F32) * ppu_ref[...].astype(F32) * prevf
        vv1 = _shift_rows(vv, 1, [pvv[15:16]], row)
        vv2 = _shift_rows(vv, 2, [pvv[14:15], pvv[15:16]], row)
        w = cw_ref[...]
        y_ref[:, 0:D_CONV] = bg * (w[0:1] * vv2 + w[1:2] * vv1 + w[2:3] * vv)

        head_row = lax.broadcasted_iota(jnp.int32, (128, BLOCK), 0)
        lse_t = jnp.zeros((128, BLOCK), F32)
        for kv in range(N_SWA_KV):
            heads = range(kv * SWA_GROUP, (kv + 1) * SWA_GROUP)
            kc = pc_ref[:, COL_K + HEAD_DIM * kv:COL_K + HEAD_DIM * (kv + 1)]
            vc = pc_ref[:, COL_V + HEAD_DIM * kv:COL_V + HEAD_DIM * (kv + 1)]
            kp = pkv_ref[:, HEAD_DIM * kv:HEAD_DIM * (kv + 1)]
            vp = pkv_ref[:, D_KV + HEAD_DIM * kv:D_KV + HEAD_DIM * (kv + 1)]
            qg = qh_ref[kv * SWA_GROUP:(kv + 1) * SWA_GROUP].reshape(GROUP_ROWS, HEAD_DIM)
            sc = _dot_nt(kc, qg) + bc_ref[0, kv]
            sp = _dot_nt(kp, qg) + bp_ref[0, kv]
            sink = jnp.concatenate([jnp.full((1, BLOCK), sk_ref[0, hd], F32) for hd in heads], axis=1)
            mx = jnp.maximum(jnp.max(jnp.maximum(sc, sp), axis=0, keepdims=True), sink)
            ec = jnp.exp(sc - mx)
            ep = jnp.exp(sp - mx)
            den = jnp.sum(ec + ep, axis=0, keepdims=True) + jnp.exp(sink - mx)
            ot = (_dot_tn(vc, ec.astype(BF16)) + _dot_tn(vp, ep.astype(BF16))) / den
            lse = mx + jnp.log(den)
            for gi, hd in enumerate(heads):
                span = slice(gi * BLOCK, (gi + 1) * BLOCK)
                y_ref[:, _head_cols(hd):_head_cols(hd) + HEAD_DIM] = ot[:, span].T
                lse_t = jnp.where(head_row == hd, lse[:, span], lse_t)

        for hm in range(N_MEM_HEADS):
            hd = N_SWA_HEADS + hm
            mk = mkv_ref[:, HEAD_DIM * hm:HEAD_DIM * (hm + 1)]
            mv = mkv_ref[:, D_MEMQ + HEAD_DIM * hm:D_MEMQ + HEAD_DIM * (hm + 1)]
            s = _dot_nt(mk, qh_ref[hd])
            mx = jnp.max(s, axis=0, keepdims=True)
            e = jnp.exp(s - mx)
            den = jnp.sum(e, axis=0, keepdims=True)
            y_ref[:, _head_cols(hd):_head_cols(hd) + HEAD_DIM] = (_dot_tn(mv, e.astype(BF16)) / den).T
            lse_t = jnp.where(head_row == hd, mx + jnp.log(den), lse_t)
        l_ref[...] = lse_t.T

    kv_col = COL_K // (2 * D_KV)
    bias_block = (1, N_SWA_KV, BLOCK, GROUP_ROWS)
    return _pcall(
        body, name="mix_core_fwd", grid=(nb,),
        in_specs=[pl.BlockSpec(memory_space=pltpu.SMEM),
                  pl.BlockSpec((BLOCK, D_IN), lambda i: (i, 0)),
                  pl.BlockSpec((BLOCK, 2 * D_KV), lambda i: (jnp.maximum(i - 1, 0), kv_col)),
                  pl.BlockSpec((16, D_CONV), lambda i: (jnp.maximum(i * (BLOCK // 16) - 1, 0), COL_CG // D_CONV)),
                  pl.BlockSpec((16, D_CONV), lambda i: (jnp.maximum(i * (BLOCK // 16) - 1, 0), COL_U // D_CONV)),
                  pl.BlockSpec((N_HEADS, BLOCK, HEAD_DIM), lambda i: (0, i, 0)),
                  pl.BlockSpec((m, 2 * D_MEMQ), lambda i: (0, 0)),
                  pl.BlockSpec((3, D_CONV), lambda i: (0, 0)),
                  pl.BlockSpec(bias_block, lambda i: (BIAS_CUR, 0, 0, 0)),
                  pl.BlockSpec(bias_block, lambda i: (jnp.where(i == 0, BIAS_NONE, BIAS_PREV), 0, 0, 0))],
        out_specs=[pl.BlockSpec((BLOCK, D_MIX), lambda i: (i, 0)),
                   pl.BlockSpec((BLOCK, 128), lambda i: (i, 0))],
        out_shape=[jax.ShapeDtypeStruct((t, D_MIX), F32), jax.ShapeDtypeStruct((t, 128), F32)],
        compiler_params=_params(("parallel",)),
    )(sinks, p, p, p, p, qh, mkv, convw, bias_key, bias_key)


def _mix_core_bwd(p, qh, dyconv, doh, delta, lse, mkv, convw, sinks, bias_key):
    t = p.shape[0]
    m = mkv.shape[0]
    nb = t // BLOCK

    def body(sk_ref, pc_ref, pkv_ref, ppc_ref, ppu_ref, pnb_ref, dyc_ref, dyn_ref, qc_ref, qn_ref, doc_ref, don_ref,
             dlc_ref, dln_ref, lc_ref, ln_ref, mkv_ref, cw_ref, bp_ref, bct_ref, bnt_ref,
             dp_ref, dmkv_ref, dcw_ref, dsk_ref):
        i = pl.program_id(0)
        prevf = (i > 0).astype(F32)
        nextf = (i < nb - 1).astype(F32)
        row = lax.broadcasted_iota(jnp.int32, (BLOCK, D_CONV), 0)

        @pl.when(i == 0)
        def _():
            dmkv_ref[...] = jnp.zeros_like(dmkv_ref)
            dcw_ref[...] = jnp.zeros_like(dcw_ref)
            dsk_ref[...] = jnp.zeros_like(dsk_ref)

        bg = pc_ref[:, COL_BG:COL_BG + D_CONV].astype(F32)
        cg = pc_ref[:, COL_CG:COL_CG + D_CONV].astype(F32)
        u = pc_ref[:, COL_U:COL_U + D_CONV].astype(F32)
        vv = cg * u
        pvv = ppc_ref[...].astype(F32) * ppu_ref[...].astype(F32) * prevf
        vv1 = _shift_rows(vv, 1, [pvv[15:16]], row)
        vv2 = _shift_rows(vv, 2, [pvv[14:15], pvv[15:16]], row)
        w = cw_ref[...]
        yconv = w[0:1] * vv2 + w[1:2] * vv1 + w[2:3] * vv
        dyo = dyc_ref[...]
        dyc = dyo * bg
        nxt = dyn_ref[...] * pnb_ref[...].astype(F32) * nextf
        d1 = _shift_rows_up(dyc, 1, [nxt[0:1]], row)
        d2 = _shift_rows_up(dyc, 2, [nxt[0:1], nxt[1:2]], row)
        dvv = w[2:3] * dyc + w[1:2] * d1 + w[0:1] * d2
        dp_ref[:, COL_BG:COL_BG + D_CONV] = (dyo * yconv).astype(BF16)
        dp_ref[:, COL_CG:COL_CG + D_CONV] = (dvv * u).astype(BF16)
        dp_ref[:, COL_U:COL_U + D_CONV] = (dvv * cg).astype(BF16)
        dcw_ref[0:1, :] += jnp.sum(dyc * vv2, axis=0, keepdims=True)
        dcw_ref[1:2, :] += jnp.sum(dyc * vv1, axis=0, keepdims=True)
        dcw_ref[2:3, :] += jnp.sum(dyc * vv, axis=0, keepdims=True)

        lse_t, dl_t = lc_ref[...].T, dlc_ref[...].T
        lse_nt, dl_nt = ln_ref[...].T, dln_ref[...].T

        def stack_rows(tile_t, heads):
            return jnp.concatenate([tile_t[hd:hd + 1, :] for hd in heads], axis=1)

        lane8 = jnp.where(lax.broadcasted_iota(jnp.int32, (8, 128), 0) == 0,
                          lax.broadcasted_iota(jnp.int32, (8, 128), 1), -1)
        dsk = jnp.zeros((8, 128), F32)
        for kv in range(N_SWA_KV):
            heads = range(kv * SWA_GROUP, (kv + 1) * SWA_GROUP)
            kc = pc_ref[:, COL_K + HEAD_DIM * kv:COL_K + HEAD_DIM * (kv + 1)]
            vc = pc_ref[:, COL_V + HEAD_DIM * kv:COL_V + HEAD_DIM * (kv + 1)]
            kp = pkv_ref[:, HEAD_DIM * kv:HEAD_DIM * (kv + 1)]
            vp = pkv_ref[:, D_KV + HEAD_DIM * kv:D_KV + HEAD_DIM * (kv + 1)]
            qg = qc_ref[kv * SWA_GROUP:(kv + 1) * SWA_GROUP].reshape(GROUP_ROWS, HEAD_DIM)
            dog = doc_ref[kv * SWA_GROUP:(kv + 1) * SWA_GROUP].reshape(GROUP_ROWS, HEAD_DIM)
            qn = qn_ref[kv * SWA_GROUP:(kv + 1) * SWA_GROUP].reshape(GROUP_ROWS, HEAD_DIM)
            don = don_ref[kv * SWA_GROUP:(kv + 1) * SWA_GROUP].reshape(GROUP_ROWS, HEAD_DIM)
            lse_row, dl_row = stack_rows(lse_t, heads), stack_rows(dl_t, heads)
            ptp = jnp.exp(_dot_nt(kp, qg) + bp_ref[0, kv] - lse_row)
            dstp = (ptp * (_dot_nt(vp, dog) - dl_row)).astype(BF16)
            dq = _dot_tn(dstp, kp)
            pt = jnp.exp(_dot_nt(kc, qg) + bct_ref[0, kv] - lse_row)
            dst = (pt * (_dot_nt(vc, dog) - dl_row)).astype(BF16)
            dv = _dot(pt.astype(BF16), dog)
            dk = _dot(dst, qg)
            dq = dq + _dot_tn(dst, kc)
            ptn = jnp.exp(_dot_nt(kc, qn) + bnt_ref[0, kv] - stack_rows(lse_nt, heads))
            dstn = (ptn * (_dot_nt(vc, don) - stack_rows(dl_nt, heads))).astype(BF16)
            dv = dv + _dot(ptn.astype(BF16), don)
            dk = dk + _dot(dstn, qn)
            dp_ref[:, COL_K + HEAD_DIM * kv:COL_K + HEAD_DIM * (kv + 1)] = dk.astype(BF16)
            dp_ref[:, COL_V + HEAD_DIM * kv:COL_V + HEAD_DIM * (kv + 1)] = dv.astype(BF16)
            sink = jnp.concatenate([jnp.full((1, BLOCK), sk_ref[0, hd], F32) for hd in heads], axis=1)
            sink_term = jnp.exp(sink - lse_row) * dl_row
            for gi, hd in enumerate(heads):
                span = slice(gi * BLOCK, (gi + 1) * BLOCK)
                dp_ref[:, _q_col(hd):_q_col(hd) + HEAD_DIM] = (dq[span] * SCALE).astype(BF16)
                dsk = dsk + jnp.where(lane8 == hd, -jnp.sum(sink_term[:, span], axis=1, keepdims=True), 0.0)
        dsk_ref[...] += dsk

        for hm in range(N_MEM_HEADS):
            hd = N_SWA_HEADS + hm
            qm, dom = qc_ref[hd], doc_ref[hd]
            mk = mkv_ref[:, HEAD_DIM * hm:HEAD_DIM * (hm + 1)]
            mv = mkv_ref[:, D_MEMQ + HEAD_DIM * hm:D_MEMQ + HEAD_DIM * (hm + 1)]
            pt = jnp.exp(_dot_nt(mk, qm) - lse_t[hd:hd + 1, :])
            dst = (pt * (_dot_nt(mv, dom) - dl_t[hd:hd + 1, :])).astype(BF16)
            dp_ref[:, _q_col(hd):_q_col(hd) + HEAD_DIM] = (_dot_tn(dst, mk) * SCALE).astype(BF16)
            dmkv_ref[:, HEAD_DIM * hm:HEAD_DIM * (hm + 1)] += _dot(dst, qm)
            dmkv_ref[:, D_MEMQ + HEAD_DIM * hm:D_MEMQ + HEAD_DIM * (hm + 1)] += _dot(pt.astype(BF16), dom)

    cur = lambda i: (i, 0)
    const = lambda i: (0, 0)
    rows16 = BLOCK // 16
    last16 = t // 16 - 1
    before = lambda col: (lambda i: (jnp.maximum(i * rows16 - 1, 0), col))
    after = lambda i: (jnp.minimum((i + 1) * rows16, last16), 0)
    heads_cur = lambda i: (0, i, 0)
    heads_next = lambda i: (0, jnp.minimum(i + 1, nb - 1), 0)
    stat_next = lambda i: (jnp.minimum(i + 1, nb - 1), 0)
    key_block = (1, N_SWA_KV, BLOCK, GROUP_ROWS)
    head_block = (N_HEADS, BLOCK, HEAD_DIM)
    return _pcall(
        body, name="mix_core_bwd", grid=(nb,),
        in_specs=[pl.BlockSpec(memory_space=pltpu.SMEM),
                  pl.BlockSpec((BLOCK, D_IN), cur),
                  pl.BlockSpec((BLOCK, 2 * D_KV), lambda i: (jnp.maximum(i - 1, 0), COL_K // (2 * D_KV))),
                  pl.BlockSpec((16, D_CONV), before(COL_CG // D_CONV)),
                  pl.BlockSpec((16, D_CONV), before(COL_U // D_CONV)),
                  pl.BlockSpec((16, D_CONV), after),
                  pl.BlockSpec((BLOCK, D_CONV), cur),
                  pl.BlockSpec((16, D_CONV), after),
                  pl.BlockSpec(head_block, heads_cur), pl.BlockSpec(head_block, heads_next),
                  pl.BlockSpec(head_block, heads_cur), pl.BlockSpec(head_block, heads_next),
                  pl.BlockSpec((BLOCK, 128), cur), pl.BlockSpec((BLOCK, 128), stat_next),
                  pl.BlockSpec((BLOCK, 128), cur), pl.BlockSpec((BLOCK, 128), stat_next),
                  pl.BlockSpec((m, 2 * D_MEMQ), const),
                  pl.BlockSpec((3, D_CONV), const),
                  pl.BlockSpec(key_block, lambda i: (jnp.where(i == 0, BIAS_NONE, BIAS_PREV), 0, 0, 0)),
                  pl.BlockSpec(key_block, lambda i: (BIAS_CUR, 0, 0, 0)),
                  pl.BlockSpec(key_block, lambda i: (jnp.where(i == nb - 1, BIAS_NONE, BIAS_PREV), 0, 0, 0))],
        out_specs=[pl.BlockSpec((BLOCK, D_IN), cur),
                   pl.BlockSpec((m, 2 * D_MEMQ), const),
                   pl.BlockSpec((8, D_CONV), const),
                   pl.BlockSpec((8, 128), const)],
        out_shape=[jax.ShapeDtypeStruct((t, D_IN), BF16),
                   jax.ShapeDtypeStruct((m, 2 * D_MEMQ), F32),
                   jax.ShapeDtypeStruct((8, D_CONV), F32),
                   jax.ShapeDtypeStruct((8, 128), F32)],
        compiler_params=_params(("arbitrary",)),
    )(sinks, p, p, p, p, p, dyconv, dyconv, qh, qh, doh, doh, delta, delta, lse, lse, mkv, convw,
      bias_key, bias_key, bias_key)


def _group_norms(y):
    out = []
    for a, b in MIX_GROUPS:
        ys = y[:, a:b]
        r = _rstd(ys)
        out.append((ys * r, r))
    return out


def _mix_out_fwd(y, h, g, wout):
    t, d = h.shape
    tm = _tok_block(t)

    def body(y_ref, h_ref, g_ref, w_ref, ho_ref, mt_ref):
        yhat = jnp.concatenate([yh for yh, _ in _group_norms(y_ref[...])], axis=-1)
        mixed = yhat * g_ref[...]
        mt_ref[...] = mixed.T.astype(BF16)
        ho_ref[...] = h_ref[...] + _dot(mixed.astype(BF16), w_ref[...])

    return _pcall(
        body, name="mix_out_fwd", grid=(t // tm,),
        in_specs=[pl.BlockSpec((tm, D_MIX), lambda i: (i, 0)),
                  pl.BlockSpec((tm, d), lambda i: (i, 0)),
                  pl.BlockSpec((1, D_MIX), lambda i: (0, 0)),
                  pl.BlockSpec((D_MIX, d), lambda i: (0, 0))],
        out_specs=[pl.BlockSpec((tm, d), lambda i: (i, 0)),
                   pl.BlockSpec((D_MIX, tm), lambda i: (0, i))],
        out_shape=[jax.ShapeDtypeStruct((t, d), F32), jax.ShapeDtypeStruct((D_MIX, t), BF16)],
        compiler_params=_params(("parallel",)),
    )(y, h, g, wout)


def _head_indicator():
    ind = np.zeros((D_MIX, 128), np.float32)
    for hd in range(N_HEADS):
        ind[_head_cols(hd):_head_cols(hd) + HEAD_DIM, hd] = 1.0
    return jnp.asarray(ind, BF16)


def _mix_out_bwd(dho, y, g, wout, mt, dep):
    t, d = dho.shape
    tm = _tok_block(t)
    ni = t // tm

    def body(dho_ref, y_ref, g_ref, w_ref, mt_ref, ind_ref, dep_ref, dyc_ref, doh_ref, dl_ref, dw_ref, dg_ref, acc_ref):
        i = pl.program_id(0)
        dhb = dho_ref[...].astype(BF16)
        dm = _dot_nt(dhb, w_ref[...])
        pw = _dot(mt_ref[...], dhb)
        gg = g_ref[...]
        yy = y_ref[...]
        dys = []
        dgs = []
        for (a, b), (yhat, r) in zip(MIX_GROUPS, _group_norms(yy)):
            dmg = dm[:, a:b]
            dgs.append(_sum8(dmg * yhat))
            dyh = dmg * gg[:, a:b]
            dys.append(r * (dyh - yhat * jnp.mean(dyh * yhat, axis=-1, keepdims=True)))
        dy = jnp.concatenate(dys, axis=-1)
        dyc_ref[...] = dy[:, 0:D_CONV]
        for hd in range(N_HEADS):
            doh_ref[hd] = dy[:, _head_cols(hd):_head_cols(hd) + HEAD_DIM].astype(BF16)
        prod = dy * yy
        hi = prod.astype(BF16)
        lo = (prod - hi.astype(F32)).astype(BF16)
        dl_ref[...] = _dot(hi, ind_ref[...]) + _dot(lo, ind_ref[...])
        part = jnp.concatenate(dgs, axis=-1)

        @pl.when(i == 0)
        def _():
            acc_ref[...] = pw
            dg_ref[...] = part

        @pl.when(i > 0)
        def _():
            acc_ref[...] += pw
            dg_ref[...] += part

        @pl.when(i == ni - 1)
        def _():
            dw_ref[...] = acc_ref[...].astype(BF16)

    return _pcall(
        body, name="mix_out_bwd", grid=(ni,),
        in_specs=[pl.BlockSpec((tm, d), lambda i: (i, 0)),
                  pl.BlockSpec((tm, D_MIX), lambda i: (i, 0)),
                  pl.BlockSpec((1, D_MIX), lambda i: (0, 0)),
                  pl.BlockSpec((D_MIX, d), lambda i: (0, 0)),
                  pl.BlockSpec((D_MIX, tm), lambda i: (0, i)),
                  pl.BlockSpec((D_MIX, 128), lambda i: (0, 0)),
                  pl.BlockSpec(memory_space=pl.ANY)],
        out_specs=[pl.BlockSpec((tm, D_CONV), lambda i: (i, 0)),
                   pl.BlockSpec((N_HEADS, tm, HEAD_DIM), lambda i: (0, i, 0)),
                   pl.BlockSpec((tm, 128), lambda i: (i, 0)),
                   pl.BlockSpec((D_MIX, d), lambda i: (0, 0)),
                   pl.BlockSpec((8, D_MIX), lambda i: (0, 0))],
        out_shape=[jax.ShapeDtypeStruct((t, D_CONV), F32),
                   jax.ShapeDtypeStruct((N_HEADS, t, HEAD_DIM), BF16),
                   jax.ShapeDtypeStruct((t, 128), F32),
                   jax.ShapeDtypeStruct((D_MIX, d), BF16),
                   jax.ShapeDtypeStruct((8, D_MIX), F32)],
        scratch_shapes=[pltpu.VMEM((D_MIX, d), F32)],
        compiler_params=_params(("arbitrary",)),
    )(dho, y, g, wout, mt, _head_indicator(), dep)


def _mix_proj_bwd(dp, dho, h, g, win_t, n):
    t, d = h.shape
    tm = _tok_block(t)
    ni = t // tm

    def body(dp_ref, dho_ref, h_ref, g_ref, w_ref, n_ref, dh_ref, dw_ref, dg_ref, acc_ref):
        i = pl.program_id(0)
        dpb = dp_ref[...]
        dn = _dot(dpb, w_ref[...])

        @pl.when(i == 0)
        def _():
            acc_ref[...] = jnp.zeros_like(acc_ref)

        acc_ref[...] += _dot_tn(dpb, n_ref[...])
        hh = h_ref[...]
        r = _rstd(hh)
        xhat = hh * r
        dxh = dn * g_ref[...]
        dh_ref[...] = dho_ref[...] + r * (dxh - xhat * jnp.mean(dxh * xhat, axis=-1, keepdims=True))
        part = _sum8(dn * xhat)

        @pl.when(i == 0)
        def _():
            dg_ref[...] = part

        @pl.when(i > 0)
        def _():
            dg_ref[...] += part

        @pl.when(i == ni - 1)
        def _():
            dw_ref[...] = acc_ref[...].astype(BF16)

    return _pcall(
        body, name="mix_proj_bwd", grid=(ni,),
        in_specs=[pl.BlockSpec((tm, D_IN), lambda i: (i, 0)),
                  pl.BlockSpec((tm, d), lambda i: (i, 0)),
                  pl.BlockSpec((tm, d), lambda i: (i, 0)),
                  pl.BlockSpec((1, d), lambda i: (0, 0)),
                  pl.BlockSpec((D_IN, d), lambda i: (0, 0)),
                  pl.BlockSpec((tm, d), lambda i: (i, 0))],
        out_specs=[pl.BlockSpec((tm, d), lambda i: (i, 0)),
                   pl.BlockSpec((D_IN, d), lambda i: (0, 0)),
                   pl.BlockSpec((8, d), lambda i: (0, 0))],
        out_shape=[jax.ShapeDtypeStruct((t, d), F32),
                   jax.ShapeDtypeStruct((D_IN, d), BF16),
                   jax.ShapeDtypeStruct((8, d), F32)],
        scratch_shapes=[pltpu.VMEM((D_IN, d), F32)],
        compiler_params=_params(("arbitrary",)),
    )(dp, dho, h, g, win_t, n)


def _final_loss(h, g, tgt):
    t, d = h.shape
    tm = _tok_block(t)

    def body(h_ref, g_ref, t_ref, dh_ref, ls_ref, dg_ref):
        i = pl.program_id(0)
        hh = h_ref[...]
        r = _rstd(hh)
        xhat = hh * r
        gg = g_ref[...]
        err = xhat * gg - t_ref[...]
        dy = err * (1.0 / d)
        dxh = dy * gg
        dh_ref[...] = r * (dxh - xhat * jnp.mean(dxh * xhat, axis=-1, keepdims=True))
        lpart = _sum8(err * err)
        gpart = _sum8(dy * xhat)

        @pl.when(i == 0)
        def _():
            ls_ref[...] = lpart
            dg_ref[...] = gpart

        @pl.when(i > 0)
        def _():
            ls_ref[...] += lpart
            dg_ref[...] += gpart

    return _pcall(
        body, name="final_loss", grid=(t // tm,),
        in_specs=[pl.BlockSpec((tm, d), lambda i: (i, 0)),
                  pl.BlockSpec((1, d), lambda i: (0, 0)),
                  pl.BlockSpec((tm, d), lambda i: (i, 0))],
        out_specs=[pl.BlockSpec((tm, d), lambda i: (i, 0)),
                   pl.BlockSpec((8, d), lambda i: (0, 0)),
                   pl.BlockSpec((8, d), lambda i: (0, 0))],
        out_shape=[jax.ShapeDtypeStruct((t, d), F32),
                   jax.ShapeDtypeStruct((8, d), F32),
                   jax.ShapeDtypeStruct((8, d), F32)],
        compiler_params=_params(("arbitrary",)),
    )(h, g, tgt)


def _position():
    return lax.axis_index("x"), lax.axis_index("y"), lax.axis_index("c")


def _flip(v, bit):
    return 1 - v if bit else v


def _peer(k):
    x, y, c = _position()
    return _flip(x, k & 4), _flip(y, k & 2), _flip(c, k & 1)


def _slot(px, py, pc):
    return 4 * px + 2 * py + pc


def _handshake(peers):
    barrier = pltpu.get_barrier_semaphore()
    for peer in peers:
        pl.semaphore_signal(barrier, inc=1, device_id=peer, device_id_type=MESH)
    pl.semaphore_wait(barrier, len(peers))


def _sequencer_call(body, name, collective_id, out_type, scratch_types, operands):
    return pl.kernel(
        body, out_type=out_type, mesh=plsc.ScalarSubcoreMesh(axis_name="sequencer", num_cores=1), name=name,
        scratch_types=scratch_types, compiler_params=pltpu.CompilerParams(collective_id=collective_id),
    )(*operands)


def _all_gather(shards, name, collective_id):
    nt = len(shards)

    def body(*refs):
        xs = refs[:nt]
        outs = refs[nt:2 * nt]
        send_sems, recv_sems, local_sems = refs[2 * nt:]
        x, y, c = _position()
        me, sibling = (x, y, c), (x, y, 1 - c)
        chips = [(1 - x, y), (x, 1 - y), (1 - x, 1 - y)]
        _handshake([sibling] + [(*chip, c) for chip in chips])

        def copy(t, k, block, to, src=None):
            dst = outs[t].at[_slot(*block)]
            return pltpu.make_async_remote_copy(
                src_ref=dst if src is None else src, dst_ref=dst,
                send_sem=send_sems.at[t, k], recv_sem=recv_sems.at[t, k],
                device_id=to, device_id_type=MESH)

        mine = [pltpu.make_async_copy(xs[t], outs[t].at[_slot(*me)], local_sems.at[t]) for t in range(nt)]
        for cp in mine:
            cp.start()
        first = []
        for t in range(nt):
            first.append(copy(t, 0, me, sibling, src=xs[t]))
            first += [copy(t, 1 + j, me, (*chip, c), src=xs[t]) for j, chip in enumerate(chips)]
        for cp in first:
            cp.start()
        passed = []
        for j, chip in enumerate(chips):
            for t in range(nt):
                copy(t, 1 + j, (*chip, c), me).wait_recv()
                fwd = copy(t, 4 + j, (*chip, c), sibling)
                fwd.start()
                passed.append(fwd)
        for t in range(nt):
            copy(t, 0, sibling, me).wait_recv()
            for j, chip in enumerate(chips):
                copy(t, 4 + j, (*chip, 1 - c), me).wait_recv()
        for cp in first + passed:
            cp.wait_send()
        for cp in mine:
            cp.wait()

    return _sequencer_call(
        body, name, collective_id,
        out_type=[jax.ShapeDtypeStruct((N_DEV,) + s.shape, s.dtype) for s in shards],
        scratch_types=[pltpu.SemaphoreType.DMA((nt, 7)), pltpu.SemaphoreType.DMA((nt, 7)),
                       pltpu.SemaphoreType.DMA((nt,))],
        operands=shards)


def _scatter_copy(srcs, lands, send_sems, recv_sems, t, k):
    peer = _peer(k)
    return pltpu.make_async_remote_copy(
        src_ref=srcs[t].at[_slot(*peer)], dst_ref=lands[t].at[k],
        send_sem=send_sems.at[t * (N_DEV - 1) + k - 1], recv_sem=recv_sems.at[t * (N_DEV - 1) + k - 1],
        device_id=peer, device_id_type=MESH)


def _scatter_start(partials, name):
    nt = len(partials)

    def body(*refs):
        srcs, lands = refs[:nt], refs[nt:2 * nt]
        send_sems, recv_sems = refs[2 * nt], refs[2 * nt + 1]
        token = refs[-1]
        for k in range(1, N_DEV):
            for t in range(nt):
                _scatter_copy(srcs, lands, send_sems, recv_sems, t, k).start()
        token[...] = jnp.zeros_like(token)

    hbm = pl.BlockSpec(memory_space=pltpu.HBM)
    sem = pl.BlockSpec(memory_space=pltpu.SEMAPHORE)
    shapes = [pltpu.HBM(p.shape, p.dtype) for p in partials]
    lands = [pltpu.with_memory_space_constraint(lax.empty(p.shape, p.dtype), pltpu.HBM) for p in partials]
    srcs = [pltpu.with_memory_space_constraint(p, pltpu.HBM) for p in partials]
    out = _pcall(
        body, name=name,
        out_shape=[pltpu.SemaphoreType.DMA((nt * (N_DEV - 1),))] * 2 + shapes + shapes
        + [jax.ShapeDtypeStruct((8, 128), F32)],
        in_specs=[hbm] * (2 * nt),
        out_specs=[sem, sem] + [hbm] * (2 * nt) + [pl.BlockSpec(memory_space=pltpu.VMEM)],
        input_output_aliases={i: 2 + i for i in range(2 * nt)},
        compiler_params=pltpu.CompilerParams(has_side_effects=pltpu.SideEffectType.DATAFLOW_SIDE_EFFECTING),
    )(*srcs, *lands)
    return (nt, name, out[:-1]), out[-1]


def _scatter_wait(state, after):
    nt, name, (send_sems, recv_sems, *thru) = state

    def body(*refs):
        srcs, lands = refs[:nt], refs[nt:2 * nt]
        send_sems, recv_sems = refs[2 * nt], refs[2 * nt + 1]
        for k in range(1, N_DEV):
            for t in range(nt):
                copy = _scatter_copy(srcs, lands, send_sems, recv_sems, t, k)
                copy.wait_send()
                copy.wait_recv()

    hbm = pl.BlockSpec(memory_space=pltpu.HBM)
    sem = pl.BlockSpec(memory_space=pltpu.SEMAPHORE)
    out = _pcall(
        body, name=name + "_wait",
        out_shape=[pltpu.HBM(a.shape, a.dtype) for a in thru],
        in_specs=[hbm] * (2 * nt) + [sem, sem, pl.BlockSpec(memory_space=pl.ANY)],
        out_specs=[hbm] * (2 * nt),
        input_output_aliases={i: i for i in range(2 * nt)},
        compiler_params=pltpu.CompilerParams(has_side_effects=pltpu.SideEffectType.DATAFLOW_SIDE_EFFECTING),
    )(*thru, send_sems, recv_sems, after)
    return out[:nt], out[nt:]


def _all_reduce_rows(v):
    nv, _, w = v.shape

    def body(v_ref, out_ref, gath_ref, send_sems, recv_sems):
        x, y, c = _position()
        me = _slot(x, y, c)

        def copy(k):
            return pltpu.make_async_remote_copy(
                src_ref=v_ref, dst_ref=gath_ref.at[me],
                send_sem=send_sems.at[k - 1], recv_sem=recv_sems.at[k - 1],
                device_id=_peer(k), device_id_type=MESH)

        def arrival(k):
            return pltpu.make_async_remote_copy(
                src_ref=v_ref, dst_ref=gath_ref.at[_slot(*_peer(k))],
                send_sem=send_sems.at[k - 1], recv_sem=recv_sems.at[k - 1],
                device_id=_peer(k), device_id_type=MESH)

        sent = [copy(k) for k in range(1, N_DEV)]
        for cp in sent:
            cp.start()
        gath_ref[me] = v_ref[...]
        for k in range(1, N_DEV):
            arrival(k).wait_recv()
        for cp in sent:
            cp.wait_send()
        total = gath_ref[0]
        for s in range(1, N_DEV):
            total = total + gath_ref[s]
        out_ref[...] = jnp.sum(total, axis=1)

    vmem = pl.BlockSpec(memory_space=pltpu.VMEM)
    return _pcall(
        body, name="all_reduce_rows",
        in_specs=[vmem], out_specs=vmem,
        out_shape=jax.ShapeDtypeStruct((nv, w), F32),
        scratch_shapes=[pltpu.VMEM((N_DEV, nv, 8, w), F32),
                        pltpu.SemaphoreType.DMA((7,)), pltpu.SemaphoreType.DMA((7,))],
    )(v)


def _adamw_math(w, g, m, v):
    m2 = ADAM_B1 * m + (1.0 - ADAM_B1) * g
    v2 = ADAM_B2 * v + (1.0 - ADAM_B2) * (g * g)
    m_hat = m2 / (1.0 - ADAM_B1 ** ADAM_STEP)
    v_hat = v2 / (1.0 - ADAM_B2 ** ADAM_STEP)
    delta = -ADAM_LR * (m_hat / (jnp.sqrt(v_hat) + ADAM_EPS) + ADAM_WD * w)
    return delta, m2, v2


def _row_block(r):
    for cand in (256, 176, 128):
        if r % cand == 0:
            return cand
    return r


def _adamw_sharded(me, grads, w, m, v, dep):
    (own0, land0), (own1, land1) = grads
    _, r, c = land0.shape
    tr = _row_block(r)
    nr = r // tr

    def body(me_ref, o0_ref, l0_ref, o1_ref, l1_ref, w_ref, m_ref, v_ref, dep_ref, g_ref, d_ref, m2_ref, v2_ref):
        layer = pl.program_id(0)

        def total(own_ref, land_ref):
            acc = own_ref[0].astype(F32)
            for k in range(1, N_DEV):
                acc = acc + land_ref[k].astype(F32)
            return acc

        g = jnp.where(layer == 0, total(o0_ref, l0_ref), total(o1_ref, l1_ref))
        delta, m2, v2 = _adamw_math(w_ref[0], g, m_ref[0], v_ref[0])
        g_ref[0] = g
        d_ref[0] = delta
        m2_ref[0] = m2
        v2_ref[0] = v2

    rows0 = lambda l, i: jnp.where(l == 0, i, nr - 1)
    rows1 = lambda l, i: jnp.where(l == 1, i, 0)
    shard = pl.BlockSpec((1, tr, c), lambda l, i, me_ref: (l, i, 0))
    out = jax.ShapeDtypeStruct((2, r, c), F32)
    return _pcall(
        body, name="adamw_sharded",
        grid_spec=pltpu.PrefetchScalarGridSpec(
            num_scalar_prefetch=1, grid=(2, nr),
            in_specs=[pl.BlockSpec((1, tr, c), lambda l, i, me_ref: (me_ref[0], rows0(l, i), 0)),
                      pl.BlockSpec((N_DEV, tr, c), lambda l, i, me_ref: (0, rows0(l, i), 0)),
                      pl.BlockSpec((1, tr, c), lambda l, i, me_ref: (me_ref[0], rows1(l, i), 0)),
                      pl.BlockSpec((N_DEV, tr, c), lambda l, i, me_ref: (0, rows1(l, i), 0)),
                      shard, shard, shard, pl.BlockSpec(memory_space=pl.ANY)],
            out_specs=[shard, shard, shard, shard]),
        out_shape=[out, out, out, out],
        compiler_params=_params(("arbitrary", "arbitrary")),
    )(me, own0, land0, own1, land1, w, m, v, dep)


def _adamw_small(w, g, m, v):
    def body(w_ref, g_ref, m_ref, v_ref, d_ref, m2_ref, v2_ref):
        delta, m2, v2 = _adamw_math(w_ref[...], g_ref[...], m_ref[...], v_ref[...])
        d_ref[...] = delta
        m2_ref[...] = m2
        v2_ref[...] = v2

    spec = pl.BlockSpec(w.shape, lambda i: (0, 0))
    out = jax.ShapeDtypeStruct(w.shape, F32)
    return _pcall(
        body, name="adamw_small", grid=(1,),
        in_specs=[spec] * 4, out_specs=[spec] * 3, out_shape=[out] * 3,
        compiler_params=_params(("arbitrary",)),
    )(w, g, m, v)


def _pack(arrs):
    flat = jnp.concatenate([a.reshape(-1) for a in arrs])
    n = flat.shape[0]
    rows = -(-n // 1024) * 8
    return jnp.pad(flat, (0, rows * 128 - n)).reshape(rows, 128)


def _unpack(packed, like):
    flat = packed.reshape(-1)
    out, off = [], 0
    for a in like:
        out.append(flat[off:off + a.size].reshape(a.shape))
        off += a.size
    return out


def kernel(x, mem, g_ffn1, w_ffn1_up, w_ffn1_down, g_mix, w_in, conv_w, sinks, g_mem, w_mem_kv, g_grp, w_out, g_ffn2, w_ffn2_up, w_ffn2_down, g_final, loss_target, m_g_ffn1, m_w_ffn1_up, m_w_ffn1_down, m_g_mix, m_w_in, m_conv_w, m_sinks, m_g_mem, m_w_mem_kv, m_g_grp, m_w_out, m_g_ffn2, m_w_ffn2_up, m_w_ffn2_down, m_g_final, v_g_ffn1, v_w_ffn1_up, v_w_ffn1_down, v_g_mix, v_w_in, v_conv_w, v_sinks, v_g_mem, v_w_mem_kv, v_g_grp, v_w_out, v_g_ffn2, v_w_ffn2_up, v_w_ffn2_down, v_g_final):
    depth = g_ffn1.shape[0]
    t, d = x.shape[1], x.shape[2]
    width = max(d, D_MIX)
    me = _slot(*_position())
    conv_shard = conv_w.shape[2]

    xin, memin, tgt = x[0], mem[0], loss_target[0]

    conv_tile = jnp.zeros((depth * 8, 128), F32).at[:, :conv_shard].set(
        jnp.pad(conv_w, ((0, 0), (0, 8 - conv_w.shape[1]), (0, 0))).reshape(depth * 8, conv_shard))
    tr = lambda a: jnp.swapaxes(a, -1, -2)
    bf = lambda a: a.astype(BF16)
    weights = []
    collective_id = 0
    for l in range(depth):
        groups = [[bf(tr(w_ffn1_up[l])), bf(w_ffn1_down[l])] + ([conv_tile] if l == 0 else []),
                  [bf(tr(w_in[l])), bf(w_mem_kv[l]), bf(w_out[l])],
                  [bf(tr(w_ffn2_up[l])), bf(w_ffn2_down[l])]]
        full = []
        for gi, shards in enumerate(groups):
            full.append(_all_gather(shards, f"all_gather_l{l}_g{gi}", collective_id))
            collective_id += 1
        if l == 0:
            conv_full = full[0][2].reshape(N_DEV, depth, 8, 128)[:, :, :3, :conv_shard]
            conv_full = conv_full.transpose(1, 2, 0, 3).reshape(depth, 3, N_DEV * conv_shard)
        weights.append(dict(
            up1=full[0][0].reshape(2, -1, d), dn1=full[0][1].reshape(-1, d),
            win=full[1][0].reshape(D_IN, d), wkv=full[1][1].reshape(d, 2 * D_MEMQ), wout=full[1][2].reshape(D_MIX, d),
            up2=full[2][0].reshape(2, -1, d), dn2=full[2][1].reshape(-1, d)))

    row = lambda a: a.reshape(1, -1)
    bias_key = _bias_table()

    h = xin
    saved = []
    for l in range(depth):
        wl = weights[l]
        s = dict(h0=h)
        h, s["gu1"], s["n1"] = _ffn_fwd(h, row(g_ffn1[l]), wl["up1"], wl["dn1"])
        s["h1"] = h
        s["p"], s["n_mix"], s["qh"] = _mix_proj_fwd(h, row(g_mix[l]), wl["win"])
        s["mkv"], s["nt_mem"] = _memkv_fwd(memin, row(g_mem[l]), wl["wkv"], s["p"])
        s["y"], s["lse"] = _mix_core_fwd(s["p"], s["qh"], s["mkv"], conv_full[l], row(sinks[l]), bias_key)
        h, s["mt"] = _mix_out_fwd(s["y"], h, row(g_grp[l]), wl["wout"])
        s["h2"] = h
        h, s["gu2"], s["n2"] = _ffn_fwd(h, row(g_ffn2[l]), wl["up2"], wl["dn2"])
        saved.append(s)

    dh, loss_part, dg_final = _final_loss(h, row(g_final), tgt)

    small = {}
    dep = loss_part

    def reduce_small():
        def lanes(a):
            return jnp.pad(a, ((0, 0), (0, width - a.shape[1])))

        def first_row(a):
            return lanes(jnp.pad(a, ((0, 8 - a.shape[0]), (0, 0))))

        vec_names = ["g_ffn1", "g_mix", "g_mem", "g_grp", "g_ffn2", "sinks"]
        tiles = [lanes(small[n, l]) for n in vec_names for l in range(depth)]
        tiles += [first_row(small["conv_w", l][k:k + 1]) for l in range(depth) for k in range(3)]
        tiles.append(lanes(dg_final))
        n_real = len(tiles)
        tiles.append(lanes(loss_part))
        tiles += [jnp.zeros((8, width), F32)] * (-len(tiles) % 8)
        summed = _all_reduce_rows(jnp.stack(tiles))
        loss_all = 0.5 * jnp.sum(summed[n_real]) / d

        def vec(n, wd):
            return jnp.stack([summed[vec_names.index(n) * depth + l, :wd] for l in range(depth)])

        conv_base = len(vec_names) * depth
        conv_grad = jnp.stack([jnp.stack([summed[conv_base + 3 * l + k, :D_CONV] for k in range(3)])
                               for l in range(depth)])
        grads_small = {
            "g_ffn1": vec("g_ffn1", d), "g_mix": vec("g_mix", d), "g_mem": vec("g_mem", d),
            "g_grp": vec("g_grp", D_MIX), "g_ffn2": vec("g_ffn2", d), "sinks": vec("sinks", N_SWA_HEADS),
            "conv_w": lax.dynamic_slice_in_dim(conv_grad, me * conv_shard, conv_shard, axis=2),
            "g_final": summed[n_real - 1, :d],
        }
        small_w = [("g_ffn1", g_ffn1, m_g_ffn1, v_g_ffn1), ("g_mix", g_mix, m_g_mix, v_g_mix),
                   ("conv_w", conv_w, m_conv_w, v_conv_w), ("sinks", sinks, m_sinks, v_sinks),
                   ("g_mem", g_mem, m_g_mem, v_g_mem), ("g_grp", g_grp, m_g_grp, v_g_grp),
                   ("g_ffn2", g_ffn2, m_g_ffn2, v_g_ffn2), ("g_final", g_final, m_g_final, v_g_final)]
        like = [w for _, w, _, _ in small_w]
        packed = _adamw_small(_pack(like), _pack([grads_small[n] for n, _, _, _ in small_w]),
                              _pack([m for _, _, m, _ in small_w]), _pack([v for _, _, _, v in small_w]))
        updated = {n: (grads_small[n], dl, m2, v2)
                   for (n, _, _, _), dl, m2, v2 in zip(small_w, *[_unpack(pk, like) for pk in packed])}
        return loss_all, updated, packed[0]

    started = []

    def scatter(names, partials, label):
        state, token = _scatter_start(partials, f"scatter_grads_{label}")
        started.append((names, state))
        return token

    for l in reversed(range(depth)):
        wl, s = weights[l], saved[l]
        dh, agu, dyb, small["g_ffn2", l] = _ffn_bwd_act(dh, s["h2"], row(g_ffn2[l]), s["gu2"], wl["up2"], wl["dn2"], dep)
        ddn2 = _ffn_bwd_w(agu, 2, 1, dyb, agu, f"ffn_bwd_w_down_l{l}_ffn2").reshape(N_DEV, -1, d)
        dup2 = _ffn_bwd_w(agu, 0, 2, s["n2"], ddn2, f"ffn_bwd_w_up_l{l}_ffn2").reshape(N_DEV, -1, d)
        dep = scatter([("w_ffn2_up", l), ("w_ffn2_down", l)], [dup2, ddn2], f"l{l}_ffn2")
        dyconv, doh, delta, dwout, small["g_grp", l] = _mix_out_bwd(dh, s["y"], row(g_grp[l]), wl["wout"], s["mt"], dep)
        dp, dmkv, small["conv_w", l], small["sinks", l] = _mix_core_bwd(
            s["p"], s["qh"], dyconv, doh, delta, s["lse"], s["mkv"], conv_full[l], row(sinks[l]), bias_key)
        dwkv, small["g_mem", l] = _memkv_bwd(dmkv, memin, row(g_mem[l]), wl["wkv"], s["nt_mem"])
        dh, dwin, small["g_mix", l] = _mix_proj_bwd(dp, dh, s["h1"], row(g_mix[l]), wl["win"], s["n_mix"])
        dep = scatter([("w_in", l), ("w_mem_kv", l), ("w_out", l)],
                      [dwin.reshape(N_DEV, -1, d), dwkv.reshape(N_DEV, -1, 2 * D_MEMQ), dwout.reshape(N_DEV, -1, d)],
                      f"l{l}_mix")
        dh, agu, dyb, small["g_ffn1", l] = _ffn_bwd_act(dh, s["h0"], row(g_ffn1[l]), s["gu1"], wl["up1"], wl["dn1"], dep)
        order_after = agu
        if l == 0:
            loss, small_out, order_after = reduce_small()
        ddn1 = _ffn_bwd_w(agu, 2, 1, dyb, order_after, f"ffn_bwd_w_down_l{l}_ffn1").reshape(N_DEV, -1, d)
        if l > 0:
            dup1 = _ffn_bwd_w(agu, 0, 2, s["n1"], ddn1, f"ffn_bwd_w_up_l{l}_ffn1").reshape(N_DEV, -1, d)
            dep = scatter([("w_ffn1_up", l), ("w_ffn1_down", l)], [dup1, ddn1], f"l{l}_ffn1")
        else:
            dep = scatter([("w_ffn1_down", l)], [ddn1], f"l{l}_ffn1_down")
            dup1 = _ffn_bwd_w(agu, 0, 2, s["n1"], dep, f"ffn_bwd_w_up_l{l}_ffn1").reshape(N_DEV, -1, d)
            dep = scatter([("w_ffn1_up", l)], [dup1], f"l{l}_ffn1_up")
    grad_x = dh[None]

    big = {"w_ffn2_up": (w_ffn2_up, m_w_ffn2_up, v_w_ffn2_up, True), "w_ffn2_down": (w_ffn2_down, m_w_ffn2_down, v_w_ffn2_down, False),
           "w_in": (w_in, m_w_in, v_w_in, True), "w_mem_kv": (w_mem_kv, m_w_mem_kv, v_w_mem_kv, False),
           "w_out": (w_out, m_w_out, v_w_out, False), "w_ffn1_up": (w_ffn1_up, m_w_ffn1_up, v_w_ffn1_up, True),
           "w_ffn1_down": (w_ffn1_down, m_w_ffn1_down, v_w_ffn1_down, False)}
    me_index = jnp.reshape(me, (1,)).astype(jnp.int32)
    sharded, landed = {}, {}

    def finish(groups, after):
        for names, state in groups:
            owns, lands = _scatter_wait(state, after)
            for key, own, land in zip(names, owns, lands):
                landed[key] = (own, land)
            after = lands[0]
            for name in dict.fromkeys(n for n, _ in names):
                if name not in sharded and all((name, l) in landed for l in range(depth)):
                    w, m, v, transposed = big[name]
                    fix = tr if transposed else (lambda a: a)
                    res = _adamw_sharded(me_index, [landed[name, l] for l in range(depth)], fix(w), fix(m), fix(v), after)
                    sharded[name] = tuple(fix(r) for r in res)
                    after = res[0]
        return after

    finish(started[-2:], finish(started[:-2], dep))

    order = ["g_ffn1", "w_ffn1_up", "w_ffn1_down", "g_mix", "w_in", "conv_w", "sinks", "g_mem", "w_mem_kv", "g_grp",
             "w_out", "g_ffn2", "w_ffn2_up", "w_ffn2_down", "g_final"]
    results = {**sharded, **small_out}
    outs = [loss, grad_x]
    for part in range(4):
        outs += [results[n][part] for n in order]
    return tuple(outs)
```

```python
import numpy as np
import jax
import jax.numpy as jnp
from jax import lax
from jax.experimental import pallas as pl
from jax.experimental.pallas import tpu as pltpu
from jax.experimental.pallas import tpu_sc as plsc

F32 = jnp.float32
BF16 = jnp.bfloat16

N_DEV = 8
EPS = 1e-6
N_SWA_HEADS = 8
N_SWA_KV = 2
SWA_GROUP = N_SWA_HEADS // N_SWA_KV
HEAD_DIM = 64
N_MEM_HEADS = 4
D_CONV = 256
BLOCK = 128
D_SWA = N_SWA_HEADS * HEAD_DIM
D_KV = N_SWA_KV * HEAD_DIM
D_MEMQ = N_MEM_HEADS * HEAD_DIM
D_MIX = D_CONV + D_SWA + D_MEMQ
D_IN = 3 * D_CONV + D_SWA + 2 * D_KV + D_MEMQ
COL_BG, COL_CG, COL_U = 0, D_CONV, 2 * D_CONV
COL_Q = 3 * D_CONV
COL_K = COL_Q + D_SWA
COL_V = COL_K + D_KV
COL_QM = COL_V + D_KV
MIX_GROUPS = ((0, D_CONV), (D_CONV, D_CONV + D_SWA), (D_CONV + D_SWA, D_MIX))
SLOPES = tuple(2.0 ** (-8.0 * (i + 1) / N_SWA_HEADS) for i in range(N_SWA_HEADS))
SCALE = HEAD_DIM ** -0.5
NEG = -1e30

ADAM_LR = 0.001
ADAM_B1 = 0.9
ADAM_B2 = 0.999
ADAM_EPS = 1e-08
ADAM_WD = 0.01
ADAM_STEP = 10

V7X_VMEM_BYTES = 64 * 1024 * 1024
VMEM_LIMIT = (V7X_VMEM_BYTES * 3) // 4
MESH = pl.DeviceIdType.MESH


def _pcall(body, **kw):
    return pl.pallas_call(body, **kw)


def _params(sem=None, vmem=VMEM_LIMIT):
    return pltpu.CompilerParams(dimension_semantics=sem, vmem_limit_bytes=vmem)


def _dot(a, b):
    return lax.dot_general(a, b, (((1,), (0,)), ((), ())), preferred_element_type=F32)


def _dot_nt(a, b):
    return lax.dot_general(a, b, (((1,), (1,)), ((), ())), preferred_element_type=F32)


def _dot_tn(a, b):
    return lax.dot_general(a, b, (((0,), (0,)), ((), ())), preferred_element_type=F32)


def _rstd(x):
    return lax.rsqrt(jnp.mean(x * x, axis=-1, keepdims=True) + EPS)


def _sigmoid(x):
    return 1.0 / (1.0 + jnp.exp(-x))


def _sum8(x):
    r, w = x.shape
    return jnp.sum(x.reshape(r // 8, 8, w), axis=0)


def _tok_block(t, rows=512):
    return min(rows, t)


def _feat_block(f):
    return f // (N_DEV // 2)


def _ffn_fwd(h, g, wup_t, wdn):
    t, d = h.shape
    f = wdn.shape[0]
    tm, tf = _tok_block(t), _feat_block(f)
    ni, nj = t // tm, f // tf

    def body(h_ref, g_ref, wup_ref, wdn_ref, ho_ref, gu_ref, n_ref, nt_ref, acc_ref):
        j = pl.program_id(1)

        @pl.when(j == 0)
        def _():
            hh = h_ref[...]
            n = hh * _rstd(hh) * g_ref[...]
            n_ref[...] = n.astype(BF16)
            nt_ref[...] = n.T.astype(BF16)
            acc_ref[...] = jnp.zeros_like(acc_ref)

        nt = nt_ref[...]
        gate = _dot(wup_ref[0], nt)
        up = _dot(wup_ref[1], nt)
        gu_ref[0] = gate.astype(BF16)
        gu_ref[1] = up.astype(BF16)
        a = gate * _sigmoid(gate) * up
        acc_ref[...] += _dot_tn(a.astype(BF16), wdn_ref[...])

        @pl.when(j == nj - 1)
        def _():
            ho_ref[...] = h_ref[...] + 0.5 * acc_ref[...]

    return _pcall(
        body, name="ffn_fwd", grid=(ni, nj),
        in_specs=[pl.BlockSpec((tm, d), lambda i, j: (i, 0)),
                  pl.BlockSpec((1, d), lambda i, j: (0, 0)),
                  pl.BlockSpec((2, tf, d), lambda i, j: (0, j, 0)),
                  pl.BlockSpec((tf, d), lambda i, j: (j, 0))],
        out_specs=[pl.BlockSpec((tm, d), lambda i, j: (i, 0)),
                   pl.BlockSpec((2, tf, tm), lambda i, j: (0, j, i)),
                   pl.BlockSpec((tm, d), lambda i, j: (i, 0))],
        out_shape=[jax.ShapeDtypeStruct((t, d), F32),
                   jax.ShapeDtypeStruct((2, f, t), BF16),
                   jax.ShapeDtypeStruct((t, d), BF16)],
        scratch_shapes=[pltpu.VMEM((d, tm), BF16), pltpu.VMEM((tm, d), F32)],
        compiler_params=_params(("parallel", "arbitrary")),
    )(h, g, wup_t, wdn)


def _ffn_bwd_act(dho, h, g, gu, wup_t, wdn, dep):
    t, d = h.shape
    f = wdn.shape[0]
    tm, tf = _tok_block(t), _feat_block(f)
    ni, nj = t // tm, f // tf

    def body(dho_ref, h_ref, g_ref, gu_ref, wup_ref, wdn_ref, dep_ref, dh_ref, agu_ref, dyb_ref, dg_ref, dyt_ref, acc_ref):
        i = pl.program_id(0)
        j = pl.program_id(1)

        @pl.when(j == 0)
        def _():
            dy0 = 0.5 * dho_ref[...]
            dyb_ref[...] = dy0.astype(BF16)
            dyt_ref[...] = dy0.T.astype(BF16)
            acc_ref[...] = jnp.zeros_like(acc_ref)

        da = _dot(wdn_ref[...], dyt_ref[...]).astype(BF16)
        gate = gu_ref[0]
        up = gu_ref[1]
        sg = _sigmoid(gate)
        silu = gate * sg
        dgate = da * up * (sg * (1.0 + gate * (1.0 - sg)))
        dup = da * silu
        agu_ref[0] = dgate
        agu_ref[1] = dup
        agu_ref[2] = silu * up
        acc_ref[...] += _dot_tn(dgate, wup_ref[0])
        acc_ref[...] += _dot_tn(dup, wup_ref[1])

        @pl.when(j == nj - 1)
        def _():
            hh = h_ref[...]
            r = _rstd(hh)
            xhat = hh * r
            dnf = acc_ref[...]
            dxh = dnf * g_ref[...]
            dh_ref[...] = dho_ref[...] + r * (dxh - xhat * jnp.mean(dxh * xhat, axis=-1, keepdims=True))
            part = _sum8(dnf * xhat)

            @pl.when(i == 0)
            def _():
                dg_ref[...] = part

            @pl.when(i > 0)
            def _():
                dg_ref[...] += part

    return _pcall(
        body, name="ffn_bwd_act", grid=(ni, nj),
        in_specs=[pl.BlockSpec((tm, d), lambda i, j: (i, 0)),
                  pl.BlockSpec((tm, d), lambda i, j: (i, 0)),
                  pl.BlockSpec((1, d), lambda i, j: (0, 0)),
                  pl.BlockSpec((2, tf, tm), lambda i, j: (0, j, i)),
                  pl.BlockSpec((2, tf, d), lambda i, j: (0, j, 0)),
                  pl.BlockSpec((tf, d), lambda i, j: (j, 0)),
                  pl.BlockSpec(memory_space=pl.ANY)],
        out_specs=[pl.BlockSpec((tm, d), lambda i, j: (i, 0)),
                   pl.BlockSpec((3, tf, tm), lambda i, j: (0, j, i)),
                   pl.BlockSpec((tm, d), lambda i, j: (i, 0)),
                   pl.BlockSpec((8, d), lambda i, j: (0, 0))],
        out_shape=[jax.ShapeDtypeStruct((t, d), F32),
                   jax.ShapeDtypeStruct((3, f, t), BF16),
                   jax.ShapeDtypeStruct((t, d), BF16),
                   jax.ShapeDtypeStruct((8, d), F32)],
        scratch_shapes=[pltpu.VMEM((d, tm), BF16), pltpu.VMEM((tm, d), F32)],
        compiler_params=_params(("arbitrary", "arbitrary")),
    )(dho, h, g, gu, wup_t, wdn, dep)


def _ffn_bwd_w(agu, first, count, rhs, dep, name):
    _, f, t = agu.shape
    d = rhs.shape[1]
    tm, tf = _tok_block(t, 2048), _feat_block(f)
    ni, nj = t // tm, f // tf

    def body(lhs_ref, rhs_ref, dep_ref, dw_ref, acc_ref):
        i = pl.program_id(1)
        @pl.when(i == 0)
        def _():
            acc_ref[...] = jnp.zeros_like(acc_ref)

        rb = rhs_ref[...]
        for k in range(count):
            acc_ref[k] += _dot(lhs_ref[k], rb)

        @pl.when(i == ni - 1)
        def _():
            dw_ref[...] = acc_ref[...].astype(BF16)

    return _pcall(
        body, name=name, grid=(nj, ni),
        in_specs=[pl.BlockSpec((count, tf, tm), lambda j, i: (first // count, j, i)),
                  pl.BlockSpec((tm, d), lambda j, i: (i, 0)),
                  pl.BlockSpec(memory_space=pl.ANY)],
        out_specs=pl.BlockSpec((count, tf, d), lambda j, i: (0, j, 0)),
        out_shape=jax.ShapeDtypeStruct((count, f, d), BF16),
        scratch_shapes=[pltpu.VMEM((count, tf, d), F32)],
        compiler_params=_params(("parallel", "arbitrary")),
    )(agu, rhs, dep)


N_HEADS = N_SWA_HEADS + N_MEM_HEADS


def _q_col(hd):
    return COL_Q + HEAD_DIM * hd if hd < N_SWA_HEADS else COL_QM + HEAD_DIM * (hd - N_SWA_HEADS)


def _mix_proj_fwd(h, g, win_t):
    t, d = h.shape
    tm = _tok_block(t)

    def body(h_ref, g_ref, win_ref, p_ref, n_ref, qh_ref):
        hh = h_ref[...]
        n = (hh * _rstd(hh) * g_ref[...]).astype(BF16)
        n_ref[...] = n
        proj = _dot_nt(n, win_ref[...])
        p_ref[...] = proj.astype(BF16)
        for hd in range(N_HEADS):
            c0 = _q_col(hd)
            qh_ref[hd] = (proj[:, c0:c0 + HEAD_DIM] * SCALE).astype(BF16)

    return _pcall(
        body, name="mix_proj_fwd", grid=(t // tm,),
        in_specs=[pl.BlockSpec((tm, d), lambda i: (i, 0)),
                  pl.BlockSpec((1, d), lambda i: (0, 0)),
                  pl.BlockSpec((D_IN, d), lambda i: (0, 0))],
        out_specs=[pl.BlockSpec((tm, D_IN), lambda i: (i, 0)),
                   pl.BlockSpec((tm, d), lambda i: (i, 0)),
                   pl.BlockSpec((N_HEADS, tm, HEAD_DIM), lambda i: (0, i, 0))],
        out_shape=[jax.ShapeDtypeStruct((t, D_IN), BF16), jax.ShapeDtypeStruct((t, d), BF16),
                   jax.ShapeDtypeStruct((N_HEADS, t, HEAD_DIM), BF16)],
        compiler_params=_params(("parallel",)),
    )(h, g, win_t)


def _memkv_fwd(mem, g, wkv, dep):
    m, d = mem.shape

    def body(mem_ref, g_ref, w_ref, dep_ref, mkv_ref, nt_ref):
        mm = mem_ref[...]
        n = mm * _rstd(mm) * g_ref[...]
        nt_ref[...] = n.T.astype(BF16)
        mkv_ref[...] = _dot(n.astype(BF16), w_ref[...]).astype(BF16)

    return _pcall(
        body, name="memkv_fwd", grid=(1,),
        in_specs=[pl.BlockSpec((m, d), lambda i: (0, 0)),
                  pl.BlockSpec((1, d), lambda i: (0, 0)),
                  pl.BlockSpec((d, 2 * D_MEMQ), lambda i: (0, 0)),
                  pl.BlockSpec(memory_space=pl.ANY)],
        out_specs=[pl.BlockSpec((m, 2 * D_MEMQ), lambda i: (0, 0)),
                   pl.BlockSpec((d, m), lambda i: (0, 0))],
        out_shape=[jax.ShapeDtypeStruct((m, 2 * D_MEMQ), BF16), jax.ShapeDtypeStruct((d, m), BF16)],
        compiler_params=_params(("arbitrary",)),
    )(mem, g, wkv, dep)


def _memkv_bwd(dmkv, mem, g, wkv, nt):
    m, d = mem.shape

    def body(dmkv_ref, mem_ref, g_ref, w_ref, nt_ref, dw_ref, dg_ref):
        db = dmkv_ref[...].astype(BF16)
        dw_ref[...] = _dot(nt_ref[...], db).astype(BF16)
        dn = _dot_nt(db, w_ref[...])
        mm = mem_ref[...]
        dg_ref[...] = _sum8(dn * (mm * _rstd(mm)))

    return _pcall(
        body, name="memkv_bwd", grid=(1,),
        in_specs=[pl.BlockSpec((m, 2 * D_MEMQ), lambda i: (0, 0)),
                  pl.BlockSpec((m, d), lambda i: (0, 0)),
                  pl.BlockSpec((1, d), lambda i: (0, 0)),
                  pl.BlockSpec((d, 2 * D_MEMQ), lambda i: (0, 0)),
                  pl.BlockSpec((d, m), lambda i: (0, 0))],
        out_specs=[pl.BlockSpec((d, 2 * D_MEMQ), lambda i: (0, 0)),
                   pl.BlockSpec((8, d), lambda i: (0, 0))],
        out_shape=[jax.ShapeDtypeStruct((d, 2 * D_MEMQ), BF16), jax.ShapeDtypeStruct((8, d), F32)],
        compiler_params=_params(("arbitrary",)),
    )(dmkv, mem, g, wkv, nt)


def _shift_rows(v, k, edge_rows, row):
    out = pltpu.roll(v, k, 0)
    for r in range(k):
        out = jnp.where(row == r, edge_rows[r], out)
    return out


def _shift_rows_up(v, k, edge_rows, row):
    n = v.shape[0]
    out = pltpu.roll(v, n - k, 0)
    for r in range(k):
        out = jnp.where(row == n - k + r, edge_rows[r], out)
    return out


GROUP_ROWS = SWA_GROUP * BLOCK
BIAS_CUR, BIAS_PREV, BIAS_NONE = 0, 1, 2


def _bias_table():
    tq = np.arange(BLOCK)[:, None]
    sk = np.arange(BLOCK)[None, :]
    slopes = np.asarray(SLOPES, np.float32)[:, None, None]
    cur = np.where(tq >= sk, -slopes * (tq - sk).astype(np.float32), NEG)
    prev = np.where(sk > tq, -slopes * (tq + BLOCK - sk).astype(np.float32), NEG)
    none = np.full_like(cur, NEG)
    tok = np.stack([cur, prev, none]).astype(np.float32).reshape(3, N_SWA_KV, GROUP_ROWS, BLOCK)
    return jnp.asarray(np.ascontiguousarray(tok.transpose(0, 1, 3, 2)))


def _head_cols(hd):
    return D_CONV + HEAD_DIM * hd


def _mix_core_fwd(p, qh, mkv, convw, sinks, bias_key):
    t = p.shape[0]
    m = mkv.shape[0]
    nb = t // BLOCK

    def body(sk_ref, pc_ref, pkv_ref, ppc_ref, ppu_ref, qh_ref, mkv_ref, cw_ref, bc_ref, bp_ref, y_ref, l_ref):
        i = pl.program_id(0)
        prevf = (i > 0).astype(F32)
        row = lax.broadcasted_iota(jnp.int32, (BLOCK, D_CONV), 0)

        bg = pc_ref[:, COL_BG:COL_BG + D_CONV].astype(F32)
        cg = pc_ref[:, COL_CG:COL_CG + D_CONV].astype(F32)
        u = pc_ref[:, COL_U:COL_U + D_CONV].astype(F32)
        vv = cg * u
        pvv = ppc_ref[...].astype(F32) * ppu_ref[...].astype(F32) * prevf
        vv1 = _shift_rows(vv, 1, [pvv[15:16]], row)
        vv2 = _shift_rows(vv, 2, [pvv[14:15], pvv[15:16]], row)
        w = cw_ref[...]
        y_ref[:, 0:D_CONV] = bg * (w[0:1] * vv2 + w[1:2] * vv1 + w[2:3] * vv)

        head_row = lax.broadcasted_iota(jnp.int32, (128, BLOCK), 0)
        lse_t = jnp.zeros((128, BLOCK), F32)
        for kv in range(N_SWA_KV):
            heads = range(kv * SWA_GROUP, (kv + 1) * SWA_GROUP)
            kc = pc_ref[:, COL_K + HEAD_DIM * kv:COL_K + HEAD_DIM * (kv + 1)]
            vc = pc_ref[:, COL_V + HEAD_DIM * kv:COL_V + HEAD_DIM * (kv + 1)]
            kp = pkv_ref[:, HEAD_DIM * kv:HEAD_DIM * (kv + 1)]
            vp = pkv_ref[:, D_KV + HEAD_DIM * kv:D_KV + HEAD_DIM * (kv + 1)]
            qg = qh_ref[kv * SWA_GROUP:(kv + 1) * SWA_GROUP].reshape(GROUP_ROWS, HEAD_DIM)
            sc = _dot_nt(kc, qg) + bc_ref[0, kv]
            sp = _dot_nt(kp, qg) + bp_ref[0, kv]
            sink = jnp.concatenate([jnp.full((1, BLOCK), sk_ref[0, hd], F32) for hd in heads], axis=1)
            mx = jnp.maximum(jnp.max(jnp.maximum(sc, sp), axis=0, keepdims=True), sink)
            ec = jnp.exp(sc - mx)
            ep = jnp.exp(sp - mx)
            den = jnp.sum(ec + ep, axis=0, keepdims=True) + jnp.exp(sink - mx)
            ot = (_dot_tn(vc, ec.astype(BF16)) + _dot_tn(vp, ep.astype(BF16))) / den
            lse = mx + jnp.log(den)
            for gi, hd in enumerate(heads):
                span = slice(gi * BLOCK, (gi + 1) * BLOCK)
                y_ref[:, _head_cols(hd):_head_cols(hd) + HEAD_DIM] = ot[:, span].T
                lse_t = jnp.where(head_row == hd, lse[:, span], lse_t)

        for hm in range(N_MEM_HEADS):
            hd = N_SWA_HEADS + hm
            mk = mkv_ref[:, HEAD_DIM * hm:HEAD_DIM * (hm + 1)]
            mv = mkv_ref[:, D_MEMQ + HEAD_DIM * hm:D_MEMQ + HEAD_DIM * (hm + 1)]
            s = _dot_nt(mk, qh_ref[hd])
            mx = jnp.max(s, axis=0, keepdims=True)
            e = jnp.exp(s - mx)
            den = jnp.sum(e, axis=0, keepdims=True)
            y_ref[:, _head_cols(hd):_head_cols(hd) + HEAD_DIM] = (_dot_tn(mv, e.astype(BF16)) / den).T
            lse_t = jnp.where(head_row == hd, mx + jnp.log(den), lse_t)
        l_ref[...] = lse_t.T

    kv_col = COL_K // (2 * D_KV)
    bias_block = (1, N_SWA_KV, BLOCK, GROUP_ROWS)
    return _pcall(
        body, name="mix_core_fwd", grid=(nb,),
        in_specs=[pl.BlockSpec(memory_space=pltpu.SMEM),
                  pl.BlockSpec((BLOCK, D_IN), lambda i: (i, 0)),
                  pl.BlockSpec((BLOCK, 2 * D_KV), lambda i: (jnp.maximum(i - 1, 0), kv_col)),
                  pl.BlockSpec((16, D_CONV), lambda i: (jnp.maximum(i * (BLOCK // 16) - 1, 0), COL_CG // D_CONV)),
                  pl.BlockSpec((16, D_CONV), lambda i: (jnp.maximum(i * (BLOCK // 16) - 1, 0), COL_U // D_CONV)),
                  pl.BlockSpec((N_HEADS, BLOCK, HEAD_DIM), lambda i: (0, i, 0)),
                  pl.BlockSpec((m, 2 * D_MEMQ), lambda i: (0, 0)),
                  pl.BlockSpec((3, D_CONV), lambda i: (0, 0)),
                  pl.BlockSpec(bias_block, lambda i: (BIAS_CUR, 0, 0, 0)),
                  pl.BlockSpec(bias_block, lambda i: (jnp.where(i == 0, BIAS_NONE, BIAS_PREV), 0, 0, 0))],
        out_specs=[pl.BlockSpec((BLOCK, D_MIX), lambda i: (i, 0)),
                   pl.BlockSpec((BLOCK, 128), lambda i: (i, 0))],
        out_shape=[jax.ShapeDtypeStruct((t, D_MIX), F32), jax.ShapeDtypeStruct((t, 128), F32)],
        compiler_params=_params(("parallel",)),
    )(sinks, p, p, p, p, qh, mkv, convw, bias_key, bias_key)


def _mix_core_bwd(p, qh, dyconv, doh, delta, lse, mkv, convw, sinks, bias_key):
    t = p.shape[0]
    m = mkv.shape[0]
    nb = t // BLOCK

    def body(sk_ref, pc_ref, pkv_ref, ppc_ref, ppu_ref, pnb_ref, dyc_ref, dyn_ref, qc_ref, qn_ref, doc_ref, don_ref,
             dlc_ref, dln_ref, lc_ref, ln_ref, mkv_ref, cw_ref, bp_ref, bct_ref, bnt_ref,
             dp_ref, dmkv_ref, dcw_ref, dsk_ref):
        i = pl.program_id(0)
        prevf = (i > 0).astype(F32)
        nextf = (i < nb - 1).astype(F32)
        row = lax.broadcasted_iota(jnp.int32, (BLOCK, D_CONV), 0)

        @pl.when(i == 0)
        def _():
            dmkv_ref[...] = jnp.zeros_like(dmkv_ref)
            dcw_ref[...] = jnp.zeros_like(dcw_ref)
            dsk_ref[...] = jnp.zeros_like(dsk_ref)

        bg = pc_ref[:, COL_BG:COL_BG + D_CONV].astype(F32)
        cg = pc_ref[:, COL_CG:COL_CG + D_CONV].astype(F32)
        u = pc_ref[:, COL_U:COL_U + D_CONV].astype(F32)
        vv = cg * u
        pvv = ppc_ref[...].astype(F32) * ppu_ref[...].astype(F32) * prevf
        vv1 = _shift_rows(vv, 1, [pvv[15:16]], row)
        vv2 = _shift_rows(vv, 2, [pvv[14:15], pvv[15:16]], row)
        w = cw_ref[...]
        yconv = w[0:1] * vv2 + w[1:2] * vv1 + w[2:3] * vv
        dyo = dyc_ref[...]
        dyc = dyo * bg
        nxt = dyn_ref[...] * pnb_ref[...].astype(F32) * nextf
        d1 = _shift_rows_up(dyc, 1, [nxt[0:1]], row)
        d2 = _shift_rows_up(dyc, 2, [nxt[0:1], nxt[1:2]], row)
        dvv = w[2:3] * dyc + w[1:2] * d1 + w[0:1] * d2
        dp_ref[:, COL_BG:COL_BG + D_CONV] = (dyo * yconv).astype(BF16)
        dp_ref[:, COL_CG:COL_CG + D_CONV] = (dvv * u).astype(BF16)
        dp_ref[:, COL_U:COL_U + D_CONV] = (dvv * cg).astype(BF16)
        dcw_ref[0:1, :] += jnp.sum(dyc * vv2, axis=0, keepdims=True)
        dcw_ref[1:2, :] += jnp.sum(dyc * vv1, axis=0, keepdims=True)
        dcw_ref[2:3, :] += jnp.sum(dyc * vv, axis=0, keepdims=True)

        lse_t, dl_t = lc_ref[...].T, dlc_ref[...].T
        lse_nt, dl_nt = ln_ref[...].T, dln_ref[...].T

        def stack_rows(tile_t, heads):
            return jnp.concatenate([tile_t[hd:hd + 1, :] for hd in heads], axis=1)

        lane8 = jnp.where(lax.broadcasted_iota(jnp.int32, (8, 128), 0) == 0,
                          lax.broadcasted_iota(jnp.int32, (8, 128), 1), -1)
        dsk = jnp.zeros((8, 128), F32)
        for kv in range(N_SWA_KV):
            heads = range(kv * SWA_GROUP, (kv + 1) * SWA_GROUP)
            kc = pc_ref[:, COL_K + HEAD_DIM * kv:COL_K + HEAD_DIM * (kv + 1)]
            vc = pc_ref[:, COL_V + HEAD_DIM * kv:COL_V + HEAD_DIM * (kv + 1)]
            kp = pkv_ref[:, HEAD_DIM * kv:HEAD_DIM * (kv + 1)]
            vp = pkv_ref[:, D_KV + HEAD_DIM * kv:D_KV + HEAD_DIM * (kv + 1)]
            qg = qc_ref[kv * SWA_GROUP:(kv + 1) * SWA_GROUP].reshape(GROUP_ROWS, HEAD_DIM)
            dog = doc_ref[kv * SWA_GROUP:(kv + 1) * SWA_GROUP].reshape(GROUP_ROWS, HEAD_DIM)
            qn = qn_ref[kv * SWA_GROUP:(kv + 1) * SWA_GROUP].reshape(GROUP_ROWS, HEAD_DIM)
            don = don_ref[kv * SWA_GROUP:(kv + 1) * SWA_GROUP].reshape(GROUP_ROWS, HEAD_DIM)
            lse_row, dl_row = stack_rows(lse_t, heads), stack_rows(dl_t, heads)
            ptp = jnp.exp(_dot_nt(kp, qg) + bp_ref[0, kv] - lse_row)
            dstp = (ptp * (_dot_nt(vp, dog) - dl_row)).astype(BF16)
            dq = _dot_tn(dstp, kp)
            pt = jnp.exp(_dot_nt(kc, qg) + bct_ref[0, kv] - lse_row)
            dst = (pt * (_dot_nt(vc, dog) - dl_row)).astype(BF16)
            dv = _dot(pt.astype(BF16), dog)
            dk = _dot(dst, qg)
            dq = dq + _dot_tn(dst, kc)
            ptn = jnp.exp(_dot_nt(kc, qn) + bnt_ref[0, kv] - stack_rows(lse_nt, heads))
            dstn = (ptn * (_dot_nt(vc, don) - stack_rows(dl_nt, heads))).astype(BF16)
            dv = dv + _dot(ptn.astype(BF16), don)
            dk = dk + _dot(dstn, qn)
            dp_ref[:, COL_K + HEAD_DIM * kv:COL_K + HEAD_DIM * (kv + 1)] = dk.astype(BF16)
            dp_ref[:, COL_V + HEAD_DIM * kv:COL_V + HEAD_DIM * (kv + 1)] = dv.astype(BF16)
            sink = jnp.concatenate([jnp.full((1, BLOCK), sk_ref[0, hd], F32) for hd in heads], axis=1)
            sink_term = jnp.exp(sink - lse_row) * dl_row
            for gi, hd in enumerate(heads):
                span = slice(gi * BLOCK, (gi + 1) * BLOCK)
                dp_ref[:, _q_col(hd):_q_col(hd) + HEAD_DIM] = (dq[span] * SCALE).astype(BF16)
                dsk = dsk + jnp.where(lane8 == hd, -jnp.sum(sink_term[:, span], axis=1, keepdims=True), 0.0)
        dsk_ref[...] += dsk

        for hm in range(N_MEM_HEADS):
            hd = N_SWA_HEADS + hm
            qm, dom = qc_ref[hd], doc_ref[hd]
            mk = mkv_ref[:, HEAD_DIM * hm:HEAD_DIM * (hm + 1)]
            mv = mkv_ref[:, D_MEMQ + HEAD_DIM * hm:D_MEMQ + HEAD_DIM * (hm + 1)]
            pt = jnp.exp(_dot_nt(mk, qm) - lse_t[hd:hd + 1, :])
            dst = (pt * (_dot_nt(mv, dom) - dl_t[hd:hd + 1, :])).astype(BF16)
            dp_ref[:, _q_col(hd):_q_col(hd) + HEAD_DIM] = (_dot_tn(dst, mk) * SCALE).astype(BF16)
            dmkv_ref[:, HEAD_DIM * hm:HEAD_DIM * (hm + 1)] += _dot(dst, qm)
            dmkv_ref[:, D_MEMQ + HEAD_DIM * hm:D_MEMQ + HEAD_DIM * (hm + 1)] += _dot(pt.astype(BF16), dom)

    cur = lambda i: (i, 0)
    const = lambda i: (0, 0)
    rows16 = BLOCK // 16
    last16 = t // 16 - 1
    before = lambda col: (lambda i: (jnp.maximum(i * rows16 - 1, 0), col))
    after = lambda i: (jnp.minimum((i + 1) * rows16, last16), 0)
    heads_cur = lambda i: (0, i, 0)
    heads_next = lambda i: (0, jnp.minimum(i + 1, nb - 1), 0)
    stat_next = lambda i: (jnp.minimum(i + 1, nb - 1), 0)
    key_block = (1, N_SWA_KV, BLOCK, GROUP_ROWS)
    head_block = (N_HEADS, BLOCK, HEAD_DIM)
    return _pcall(
        body, name="mix_core_bwd", grid=(nb,),
        in_specs=[pl.BlockSpec(memory_space=pltpu.SMEM),
                  pl.BlockSpec((BLOCK, D_IN), cur),
                  pl.BlockSpec((BLOCK, 2 * D_KV), lambda i: (jnp.maximum(i - 1, 0), COL_K // (2 * D_KV))),
                  pl.BlockSpec((16, D_CONV), before(COL_CG // D_CONV)),
                  pl.BlockSpec((16, D_CONV), before(COL_U // D_CONV)),
                  pl.BlockSpec((16, D_CONV), after),
                  pl.BlockSpec((BLOCK, D_CONV), cur),
                  pl.BlockSpec((16, D_CONV), after),
                  pl.BlockSpec(head_block, heads_cur), pl.BlockSpec(head_block, heads_next),
                  pl.BlockSpec(head_block, heads_cur), pl.BlockSpec(head_block, heads_next),
                  pl.BlockSpec((BLOCK, 128), cur), pl.BlockSpec((BLOCK, 128), stat_next),
                  pl.BlockSpec((BLOCK, 128), cur), pl.BlockSpec((BLOCK, 128), stat_next),
                  pl.BlockSpec((m, 2 * D_MEMQ), const),
                  pl.BlockSpec((3, D_CONV), const),
                  pl.BlockSpec(key_block, lambda i: (jnp.where(i == 0, BIAS_NONE, BIAS_PREV), 0, 0, 0)),
                  pl.BlockSpec(key_block, lambda i: (BIAS_CUR, 0, 0, 0)),
                  pl.BlockSpec(key_block, lambda i: (jnp.where(i == nb - 1, BIAS_NONE, BIAS_PREV), 0, 0, 0))],
        out_specs=[pl.BlockSpec((BLOCK, D_IN), cur),
                   pl.BlockSpec((m, 2 * D_MEMQ), const),
                   pl.BlockSpec((8, D_CONV), const),
                   pl.BlockSpec((8, 128), const)],
        out_shape=[jax.ShapeDtypeStruct((t, D_IN), BF16),
                   jax.ShapeDtypeStruct((m, 2 * D_MEMQ), F32),
                   jax.ShapeDtypeStruct((8, D_CONV), F32),
                   jax.ShapeDtypeStruct((8, 128), F32)],
        compiler_params=_params(("arbitrary",)),
    )(sinks, p, p, p, p, p, dyconv, dyconv, qh, qh, doh, doh, delta, delta, lse, lse, mkv, convw,
      bias_key, bias_key, bias_key)


def _group_norms(y):
    out = []
    for a, b in MIX_GROUPS:
        ys = y[:, a:b]
        r = _rstd(ys)
        out.append((ys * r, r))
    return out


def _mix_out_fwd(y, h, g, wout):
    t, d = h.shape
    tm = _tok_block(t)

    def body(y_ref, h_ref, g_ref, w_ref, ho_ref, mt_ref):
        yhat = jnp.concatenate([yh for yh, _ in _group_norms(y_ref[...])], axis=-1)
        mixed = yhat * g_ref[...]
        mt_ref[...] = mixed.T.astype(BF16)
        ho_ref[...] = h_ref[...] + _dot(mixed.astype(BF16), w_ref[...])

    return _pcall(
        body, name="mix_out_fwd", grid=(t // tm,),
        in_specs=[pl.BlockSpec((tm, D_MIX), lambda i: (i, 0)),
                  pl.BlockSpec((tm, d), lambda i: (i, 0)),
                  pl.BlockSpec((1, D_MIX), lambda i: (0, 0)),
                  pl.BlockSpec((D_MIX, d), lambda i: (0, 0))],
        out_specs=[pl.BlockSpec((tm, d), lambda i: (i, 0)),
                   pl.BlockSpec((D_MIX, tm), lambda i: (0, i))],
        out_shape=[jax.ShapeDtypeStruct((t, d), F32), jax.ShapeDtypeStruct((D_MIX, t), BF16)],
        compiler_params=_params(("parallel",)),
    )(y, h, g, wout)


def _head_indicator():
    ind = np.zeros((D_MIX, 128), np.float32)
    for hd in range(N_HEADS):
        ind[_head_cols(hd):_head_cols(hd) + HEAD_DIM, hd] = 1.0
    return jnp.asarray(ind, BF16)


def _mix_out_bwd(dho, y, g, wout, mt, dep):
    t, d = dho.shape
    tm = _tok_block(t)
    ni = t // tm

    def body(dho_ref, y_ref, g_ref, w_ref, mt_ref, ind_ref, dep_ref, dyc_ref, doh_ref, dl_ref, dw_ref, dg_ref, acc_ref):
        i = pl.program_id(0)
        dhb = dho_ref[...].astype(BF16)
        dm = _dot_nt(dhb, w_ref[...])
        pw = _dot(mt_ref[...], dhb)
        gg = g_ref[...]
        yy = y_ref[...]
        dys = []
        dgs = []
        for (a, b), (yhat, r) in zip(MIX_GROUPS, _group_norms(yy)):
            dmg = dm[:, a:b]
            dgs.append(_sum8(dmg * yhat))
            dyh = dmg * gg[:, a:b]
            dys.append(r * (dyh - yhat * jnp.mean(dyh * yhat, axis=-1, keepdims=True)))
        dy = jnp.concatenate(dys, axis=-1)
        dyc_ref[...] = dy[:, 0:D_CONV]
        for hd in range(N_HEADS):
            doh_ref[hd] = dy[:, _head_cols(hd):_head_cols(hd) + HEAD_DIM].astype(BF16)
        prod = dy * yy
        hi = prod.astype(BF16)
        lo = (prod - hi.astype(F32)).astype(BF16)
        dl_ref[...] = _dot(hi, ind_ref[...]) + _dot(lo, ind_ref[...])
        part = jnp.concatenate(dgs, axis=-1)

        @pl.when(i == 0)
        def _():
            acc_ref[...] = pw
            dg_ref[...] = part

        @pl.when(i > 0)
        def _():
            acc_ref[...] += pw
            dg_ref[...] += part

        @pl.when(i == ni - 1)
        def _():
            dw_ref[...] = acc_ref[...].astype(BF16)

    return _pcall(
        body, name="mix_out_bwd", grid=(ni,),
        in_specs=[pl.BlockSpec((tm, d), lambda i: (i, 0)),
                  pl.BlockSpec((tm, D_MIX), lambda i: (i, 0)),
                  pl.BlockSpec((1, D_MIX), lambda i: (0, 0)),
                  pl.BlockSpec((D_MIX, d), lambda i: (0, 0)),
                  pl.BlockSpec((D_MIX, tm), lambda i: (0, i)),
                  pl.BlockSpec((D_MIX, 128), lambda i: (0, 0)),
                  pl.BlockSpec(memory_space=pl.ANY)],
        out_specs=[pl.BlockSpec((tm, D_CONV), lambda i: (i, 0)),
                   pl.BlockSpec((N_HEADS, tm, HEAD_DIM), lambda i: (0, i, 0)),
                   pl.BlockSpec((tm, 128), lambda i: (i, 0)),
                   pl.BlockSpec((D_MIX, d), lambda i: (0, 0)),
                   pl.BlockSpec((8, D_MIX), lambda i: (0, 0))],
        out_shape=[jax.ShapeDtypeStruct((t, D_CONV), F32),
                   jax.ShapeDtypeStruct((N_HEADS, t, HEAD_DIM), BF16),
                   jax.ShapeDtypeStruct((t, 128), F32),
                   jax.ShapeDtypeStruct((D_MIX, d), BF16),
                   jax.ShapeDtypeStruct((8, D_MIX), F32)],
        scratch_shapes=[pltpu.VMEM((D_MIX, d), F32)],
        compiler_params=_params(("arbitrary",)),
    )(dho, y, g, wout, mt, _head_indicator(), dep)


def _mix_proj_bwd(dp, dho, h, g, win_t, n):
    t, d = h.shape
    tm = _tok_block(t)
    ni = t // tm

    def body(dp_ref, dho_ref, h_ref, g_ref, w_ref, n_ref, dh_ref, dw_ref, dg_ref, acc_ref):
        i = pl.program_id(0)
        dpb = dp_ref[...]
        dn = _dot(dpb, w_ref[...])

        @pl.when(i == 0)
        def _():
            acc_ref[...] = jnp.zeros_like(acc_ref)

        acc_ref[...] += _dot_tn(dpb, n_ref[...])
        hh = h_ref[...]
        r = _rstd(hh)
        xhat = hh * r
        dxh = dn * g_ref[...]
        dh_ref[...] = dho_ref[...] + r * (dxh - xhat * jnp.mean(dxh * xhat, axis=-1, keepdims=True))
        part = _sum8(dn * xhat)

        @pl.when(i == 0)
        def _():
            dg_ref[...] = part

        @pl.when(i > 0)
        def _():
            dg_ref[...] += part

        @pl.when(i == ni - 1)
        def _():
            dw_ref[...] = acc_ref[...].astype(BF16)

    return _pcall(
        body, name="mix_proj_bwd", grid=(ni,),
        in_specs=[pl.BlockSpec((tm, D_IN), lambda i: (i, 0)),
                  pl.BlockSpec((tm, d), lambda i: (i, 0)),
                  pl.BlockSpec((tm, d), lambda i: (i, 0)),
                  pl.BlockSpec((1, d), lambda i: (0, 0)),
                  pl.BlockSpec((D_IN, d), lambda i: (0, 0)),
                  pl.BlockSpec((tm, d), lambda i: (i, 0))],
        out_specs=[pl.BlockSpec((tm, d), lambda i: (i, 0)),
                   pl.BlockSpec((D_IN, d), lambda i: (0, 0)),
                   pl.BlockSpec((8, d), lambda i: (0, 0))],
        out_shape=[jax.ShapeDtypeStruct((t, d), F32),
                   jax.ShapeDtypeStruct((D_IN, d), BF16),
                   jax.ShapeDtypeStruct((8, d), F32)],
        scratch_shapes=[pltpu.VMEM((D_IN, d), F32)],
        compiler_params=_params(("arbitrary",)),
    )(dp, dho, h, g, win_t, n)


def _final_loss(h, g, tgt):
    t, d = h.shape
    tm = _tok_block(t)

    def body(h_ref, g_ref, t_ref, dh_ref, ls_ref, dg_ref):
        i = pl.program_id(0)
        hh = h_ref[...]
        r = _rstd(hh)
        xhat = hh * r
        gg = g_ref[...]
        err = xhat * gg - t_ref[...]
        dy = err * (1.0 / d)
        dxh = dy * gg
        dh_ref[...] = r * (dxh - xhat * jnp.mean(dxh * xhat, axis=-1, keepdims=True))
        lpart = _sum8(err * err)
        gpart = _sum8(dy * xhat)

        @pl.when(i == 0)
        def _():
            ls_ref[...] = lpart
            dg_ref[...] = gpart

        @pl.when(i > 0)
        def _():
            ls_ref[...] += lpart
            dg_ref[...] += gpart

    return _pcall(
        body, name="final_loss", grid=(t // tm,),
        in_specs=[pl.BlockSpec((tm, d), lambda i: (i, 0)),
                  pl.BlockSpec((1, d), lambda i: (0, 0)),
                  pl.BlockSpec((tm, d), lambda i: (i, 0))],
        out_specs=[pl.BlockSpec((tm, d), lambda i: (i, 0)),
                   pl.BlockSpec((8, d), lambda i: (0, 0)),
                   pl.BlockSpec((8, d), lambda i: (0, 0))],
        out_shape=[jax.ShapeDtypeStruct((t, d), F32),
                   jax.ShapeDtypeStruct((8, d), F32),
                   jax.ShapeDtypeStruct((8, d), F32)],
        compiler_params=_params(("arbitrary",)),
    )(h, g, tgt)


def _position():
    return lax.axis_index("x"), lax.axis_index("y"), lax.axis_index("c")


def _flip(v, bit):
    return 1 - v if bit else v


def _peer(k):
    x, y, c = _position()
    return _flip(x, k & 4), _flip(y, k & 2), _flip(c, k & 1)


def _slot(px, py, pc):
    return 4 * px + 2 * py + pc


def _handshake(peers):
    barrier = pltpu.get_barrier_semaphore()
    for peer in peers:
        pl.semaphore_signal(barrier, inc=1, device_id=peer, device_id_type=MESH)
    pl.semaphore_wait(barrier, len(peers))


def _sequencer_call(body, name, collective_id, out_type, scratch_types, operands):
    return pl.kernel(
        body, out_type=out_type, mesh=plsc.ScalarSubcoreMesh(axis_name="sequencer", num_cores=1), name=name,
        scratch_types=scratch_types, compiler_params=pltpu.CompilerParams(collective_id=collective_id),
    )(*operands)


def _all_gather(shards, name, collective_id):
    nt = len(shards)

    def body(*refs):
        xs = refs[:nt]
        outs = refs[nt:2 * nt]
        send_sems, recv_sems, local_sems = refs[2 * nt:]
        x, y, c = _position()
        me, sibling = (x, y, c), (x, y, 1 - c)
        chips = [(1 - x, y), (x, 1 - y), (1 - x, 1 - y)]
        _handshake([sibling] + [(*chip, c) for chip in chips])

        def copy(t, k, block, to, src=None):
            dst = outs[t].at[_slot(*block)]
            return pltpu.make_async_remote_copy(
                src_ref=dst if src is None else src, dst_ref=dst,
                send_sem=send_sems.at[t, k], recv_sem=recv_sems.at[t, k],
                device_id=to, device_id_type=MESH)

        mine = [pltpu.make_async_copy(xs[t], outs[t].at[_slot(*me)], local_sems.at[t]) for t in range(nt)]
        for cp in mine:
            cp.start()
        first = []
        for t in range(nt):
            first.append(copy(t, 0, me, sibling, src=xs[t]))
            first += [copy(t, 1 + j, me, (*chip, c), src=xs[t]) for j, chip in enumerate(chips)]
        for cp in first:
            cp.start()
        passed = []
        for j, chip in enumerate(chips):
            for t in range(nt):
                copy(t, 1 + j, (*chip, c), me).wait_recv()
                fwd = copy(t, 4 + j, (*chip, c), sibling)
                fwd.start()
                passed.append(fwd)
        for t in range(nt):
            copy(t, 0, sibling, me).wait_recv()
            for j, chip in enumerate(chips):
                copy(t, 4 + j, (*chip, 1 - c), me).wait_recv()
        for cp in first + passed:
            cp.wait_send()
        for cp in mine:
            cp.wait()

    return _sequencer_call(
        body, name, collective_id,
        out_type=[jax.ShapeDtypeStruct((N_DEV,) + s.shape, s.dtype) for s in shards],
        scratch_types=[pltpu.SemaphoreType.DMA((nt, 7)), pltpu.SemaphoreType.DMA((nt, 7)),
                       pltpu.SemaphoreType.DMA((nt,))],
        operands=shards)


def _scatter_copy(srcs, lands, send_sems, recv_sems, t, k):
    peer = _peer(k)
    return pltpu.make_async_remote_copy(
        src_ref=srcs[t].at[_slot(*peer)], dst_ref=lands[t].at[k],
        send_sem=send_sems.at[t * (N_DEV - 1) + k - 1], recv_sem=recv_sems.at[t * (N_DEV - 1) + k - 1],
        device_id=peer, device_id_type=MESH)


def _scatter_start(partials, name):
    nt = len(partials)

    def body(*refs):
        srcs, lands = refs[:nt], refs[nt:2 * nt]
        send_sems, recv_sems = refs[2 * nt], refs[2 * nt + 1]
        token = refs[-1]
        for k in range(1, N_DEV):
            for t in range(nt):
                _scatter_copy(srcs, lands, send_sems, recv_sems, t, k).start()
        token[...] = jnp.zeros_like(token)

    hbm = pl.BlockSpec(memory_space=pltpu.HBM)
    sem = pl.BlockSpec(memory_space=pltpu.SEMAPHORE)
    shapes = [pltpu.HBM(p.shape, p.dtype) for p in partials]
    lands = [pltpu.with_memory_space_constraint(lax.empty(p.shape, p.dtype), pltpu.HBM) for p in partials]
    srcs = [pltpu.with_memory_space_constraint(p, pltpu.HBM) for p in partials]
    out = _pcall(
        body, name=name,
        out_shape=[pltpu.SemaphoreType.DMA((nt * (N_DEV - 1),))] * 2 + shapes + shapes
        + [jax.ShapeDtypeStruct((8, 128), F32)],
        in_specs=[hbm] * (2 * nt),
        out_specs=[sem, sem] + [hbm] * (2 * nt) + [pl.BlockSpec(memory_space=pltpu.VMEM)],
        input_output_aliases={i: 2 + i for i in range(2 * nt)},
        compiler_params=pltpu.CompilerParams(has_side_effects=pltpu.SideEffectType.DATAFLOW_SIDE_EFFECTING),
    )(*srcs, *lands)
    return (nt, name, out[:-1]), out[-1]


def _scatter_wait(state, after):
    nt, name, (send_sems, recv_sems, *thru) = state

    def body(*refs):
        srcs, lands = refs[:nt], refs[nt:2 * nt]
        send_sems, recv_sems = refs[2 * nt], refs[2 * nt + 1]
        for k in range(1, N_DEV):
            for t in range(nt):
                copy = _scatter_copy(srcs, lands, send_sems, recv_sems, t, k)
                copy.wait_send()
                copy.wait_recv()

    hbm = pl.BlockSpec(memory_space=pltpu.HBM)
    sem = pl.BlockSpec(memory_space=pltpu.SEMAPHORE)
    out = _pcall(
        body, name=name + "_wait",
        out_shape=[pltpu.HBM(a.shape, a.dtype) for a in thru],
        in_specs=[hbm] * (2 * nt) + [sem, sem, pl.BlockSpec(memory_space=pl.ANY)],
        out_specs=[hbm] * (2 * nt),
        input_output_aliases={i: i for i in range(2 * nt)},
        compiler_params=pltpu.CompilerParams(has_side_effects=pltpu.SideEffectType.DATAFLOW_SIDE_EFFECTING),
    )(*thru, send_sems, recv_sems, after)
    return out[:nt], out[nt:]


def _all_reduce_rows(v):
    nv, _, w = v.shape

    def body(v_ref, out_ref, mine_ref, gath_ref, send_sems, recv_sems):
        x, y, c = _position()
        me = _slot(x, y, c)
        mine_ref[...] = jnp.sum(v_ref[...], axis=1)

        def copy(k):
            return pltpu.make_async_remote_copy(
                src_ref=mine_ref, dst_ref=gath_ref.at[me],
                send_sem=send_sems.at[k - 1], recv_sem=recv_sems.at[k - 1],
                device_id=_peer(k), device_id_type=MESH)

        def arrival(k):
            return pltpu.make_async_remote_copy(
                src_ref=mine_ref, dst_ref=gath_ref.at[_slot(*_peer(k))],
                send_sem=send_sems.at[k - 1], recv_sem=recv_sems.at[k - 1],
                device_id=_peer(k), device_id_type=MESH)

        sent = [copy(k) for k in range(1, N_DEV)]
        for cp in sent:
            cp.start()
        gath_ref[me] = mine_ref[...]
        for k in range(1, N_DEV):
            arrival(k).wait_recv()
        for cp in sent:
            cp.wait_send()
        total = gath_ref[0]
        for s in range(1, N_DEV):
            total = total + gath_ref[s]
        out_ref[...] = total

    vmem = pl.BlockSpec(memory_space=pltpu.VMEM)
    return _pcall(
        body, name="all_reduce_rows",
        in_specs=[vmem], out_specs=vmem,
        out_shape=jax.ShapeDtypeStruct((nv, w), F32),
        scratch_shapes=[pltpu.VMEM((nv, w), F32), pltpu.VMEM((N_DEV, nv, w), F32),
                        pltpu.SemaphoreType.DMA((7,)), pltpu.SemaphoreType.DMA((7,))],
    )(v)


def _adamw_math(w, g, m, v):
    m2 = ADAM_B1 * m + (1.0 - ADAM_B1) * g
    v2 = ADAM_B2 * v + (1.0 - ADAM_B2) * (g * g)
    m_hat = m2 / (1.0 - ADAM_B1 ** ADAM_STEP)
    v_hat = v2 / (1.0 - ADAM_B2 ** ADAM_STEP)
    delta = -ADAM_LR * (m_hat / (jnp.sqrt(v_hat) + ADAM_EPS) + ADAM_WD * w)
    return delta, m2, v2


def _row_block(r):
    for cand in (256, 176, 128):
        if r % cand == 0:
            return cand
    return r


def _adamw_sharded(me, grads, w, m, v, dep):
    (own0, land0), (own1, land1) = grads
    _, r, c = land0.shape
    tr = _row_block(r)
    nr = r // tr

    def body(me_ref, o0_ref, l0_ref, o1_ref, l1_ref, w_ref, m_ref, v_ref, dep_ref, g_ref, d_ref, m2_ref, v2_ref):
        layer = pl.program_id(0)

        def total(own_ref, land_ref):
            acc = own_ref[0].astype(F32)
            for k in range(1, N_DEV):
                acc = acc + land_ref[k].astype(F32)
            return acc

        g = jnp.where(layer == 0, total(o0_ref, l0_ref), total(o1_ref, l1_ref))
        delta, m2, v2 = _adamw_math(w_ref[0], g, m_ref[0], v_ref[0])
        g_ref[0] = g
        d_ref[0] = delta
        m2_ref[0] = m2
        v2_ref[0] = v2

    rows0 = lambda l, i: jnp.where(l == 0, i, nr - 1)
    rows1 = lambda l, i: jnp.where(l == 1, i, 0)
    shard = pl.BlockSpec((1, tr, c), lambda l, i, me_ref: (l, i, 0))
    out = jax.ShapeDtypeStruct((2, r, c), F32)
    return _pcall(
        body, name="adamw_sharded",
        grid_spec=pltpu.PrefetchScalarGridSpec(
            num_scalar_prefetch=1, grid=(2, nr),
            in_specs=[pl.BlockSpec((1, tr, c), lambda l, i, me_ref: (me_ref[0], rows0(l, i), 0)),
                      pl.BlockSpec((N_DEV, tr, c), lambda l, i, me_ref: (0, rows0(l, i), 0)),
                      pl.BlockSpec((1, tr, c), lambda l, i, me_ref: (me_ref[0], rows1(l, i), 0)),
                      pl.BlockSpec((N_DEV, tr, c), lambda l, i, me_ref: (0, rows1(l, i), 0)),
                      shard, shard, shard, pl.BlockSpec(memory_space=pl.ANY)],
            out_specs=[shard, shard, shard, shard]),
        out_shape=[out, out, out, out],
        compiler_params=_params(("arbitrary", "arbitrary")),
    )(me, own0, land0, own1, land1, w, m, v, dep)


def _adamw_small(w, g, m, v):
    def body(w_ref, g_ref, m_ref, v_ref, d_ref, m2_ref, v2_ref):
        delta, m2, v2 = _adamw_math(w_ref[...], g_ref[...], m_ref[...], v_ref[...])
        d_ref[...] = delta
        m2_ref[...] = m2
        v2_ref[...] = v2

    spec = pl.BlockSpec(w.shape, lambda i: (0, 0))
    out = jax.ShapeDtypeStruct(w.shape, F32)
    return _pcall(
        body, name="adamw_small", grid=(1,),
        in_specs=[spec] * 4, out_specs=[spec] * 3, out_shape=[out] * 3,
        compiler_params=_params(("arbitrary",)),
    )(w, g, m, v)


def _pack(arrs):
    flat = jnp.concatenate([a.reshape(-1) for a in arrs])
    n = flat.shape[0]
    rows = -(-n // 1024) * 8
    return jnp.pad(flat, (0, rows * 128 - n)).reshape(rows, 128)


def _unpack(packed, like):
    flat = packed.reshape(-1)
    out, off = [], 0
    for a in like:
        out.append(flat[off:off + a.size].reshape(a.shape))
        off += a.size
    return out


def kernel(x, mem, g_ffn1, w_ffn1_up, w_ffn1_down, g_mix, w_in, conv_w, sinks, g_mem, w_mem_kv, g_grp, w_out, g_ffn2, w_ffn2_up, w_ffn2_down, g_final, loss_target, m_g_ffn1, m_w_ffn1_up, m_w_ffn1_down, m_g_mix, m_w_in, m_conv_w, m_sinks, m_g_mem, m_w_mem_kv, m_g_grp, m_w_out, m_g_ffn2, m_w_ffn2_up, m_w_ffn2_down, m_g_final, v_g_ffn1, v_w_ffn1_up, v_w_ffn1_down, v_g_mix, v_w_in, v_conv_w, v_sinks, v_g_mem, v_w_mem_kv, v_g_grp, v_w_out, v_g_ffn2, v_w_ffn2_up, v_w_ffn2_down, v_g_final):
    depth = g_ffn1.shape[0]
    t, d = x.shape[1], x.shape[2]
    width = max(d, D_MIX)
    me = _slot(*_position())
    conv_shard = conv_w.shape[2]

    xin, memin, tgt = x[0], mem[0], loss_target[0]

    conv_tile = jnp.zeros((depth * 8, 128), F32).at[:, :conv_shard].set(
        jnp.pad(conv_w, ((0, 0), (0, 8 - conv_w.shape[1]), (0, 0))).reshape(depth * 8, conv_shard))
    tr = lambda a: jnp.swapaxes(a, -1, -2)
    bf = lambda a: a.astype(BF16)
    weights = []
    collective_id = 0
    for l in range(depth):
        groups = [[bf(tr(w_ffn1_up[l])), bf(w_ffn1_down[l])] + ([conv_tile] if l == 0 else []),
                  [bf(tr(w_in[l])), bf(w_mem_kv[l]), bf(w_out[l])],
                  [bf(tr(w_ffn2_up[l])), bf(w_ffn2_down[l])]]
        full = []
        for gi, shards in enumerate(groups):
            full.append(_all_gather(shards, f"all_gather_l{l}_g{gi}", collective_id))
            collective_id += 1
        if l == 0:
            conv_full = full[0][2].reshape(N_DEV, depth, 8, 128)[:, :, :3, :conv_shard]
            conv_full = conv_full.transpose(1, 2, 0, 3).reshape(depth, 3, N_DEV * conv_shard)
        weights.append(dict(
            up1=full[0][0].reshape(2, -1, d), dn1=full[0][1].reshape(-1, d),
            win=full[1][0].reshape(D_IN, d), wkv=full[1][1].reshape(d, 2 * D_MEMQ), wout=full[1][2].reshape(D_MIX, d),
            up2=full[2][0].reshape(2, -1, d), dn2=full[2][1].reshape(-1, d)))

    row = lambda a: a.reshape(1, -1)
    bias_key = _bias_table()

    h = xin
    saved = []
    for l in range(depth):
        wl = weights[l]
        s = dict(h0=h)
        h, s["gu1"], s["n1"] = _ffn_fwd(h, row(g_ffn1[l]), wl["up1"], wl["dn1"])
        s["h1"] = h
        s["p"], s["n_mix"], s["qh"] = _mix_proj_fwd(h, row(g_mix[l]), wl["win"])
        s["mkv"], s["nt_mem"] = _memkv_fwd(memin, row(g_mem[l]), wl["wkv"], s["p"])
        s["y"], s["lse"] = _mix_core_fwd(s["p"], s["qh"], s["mkv"], conv_full[l], row(sinks[l]), bias_key)
        h, s["mt"] = _mix_out_fwd(s["y"], h, row(g_grp[l]), wl["wout"])
        s["h2"] = h
        h, s["gu2"], s["n2"] = _ffn_fwd(h, row(g_ffn2[l]), wl["up2"], wl["dn2"])
        saved.append(s)

    dh, loss_part, dg_final = _final_loss(h, row(g_final), tgt)

    small = {}
    dep = loss_part

    def reduce_small():
        def lanes(a):
            return jnp.pad(a, ((0, 0), (0, width - a.shape[1])))

        def first_row(a):
            return lanes(jnp.pad(a, ((0, 8 - a.shape[0]), (0, 0))))

        vec_names = ["g_ffn1", "g_mix", "g_mem", "g_grp", "g_ffn2", "sinks"]
        tiles = [lanes(small[n, l]) for n in vec_names for l in range(depth)]
        tiles += [first_row(small["conv_w", l][k:k + 1]) for l in range(depth) for k in range(3)]
        tiles.append(lanes(dg_final))
        n_real = len(tiles)
        tiles.append(lanes(loss_part))
        tiles += [jnp.zeros((8, width), F32)] * (-len(tiles) % 8)
        summed = _all_reduce_rows(jnp.stack(tiles))
        loss_all = 0.5 * jnp.sum(summed[n_real]) / d

        def vec(n, wd):
            return jnp.stack([summed[vec_names.index(n) * depth + l, :wd] for l in range(depth)])

        conv_base = len(vec_names) * depth
        conv_grad = jnp.stack([jnp.stack([summed[conv_base + 3 * l + k, :D_CONV] for k in range(3)])
                               for l in range(depth)])
        grads_small = {
            "g_ffn1": vec("g_ffn1", d), "g_mix": vec("g_mix", d), "g_mem": vec("g_mem", d),
            "g_grp": vec("g_grp", D_MIX), "g_ffn2": vec("g_ffn2", d), "sinks": vec("sinks", N_SWA_HEADS),
            "conv_w": lax.dynamic_slice_in_dim(conv_grad, me * conv_shard, conv_shard, axis=2),
            "g_final": summed[n_real - 1, :d],
        }
        small_w = [("g_ffn1", g_ffn1, m_g_ffn1, v_g_ffn1), ("g_mix", g_mix, m_g_mix, v_g_mix),
                   ("conv_w", conv_w, m_conv_w, v_conv_w), ("sinks", sinks, m_sinks, v_sinks),
                   ("g_mem", g_mem, m_g_mem, v_g_mem), ("g_grp", g_grp, m_g_grp, v_g_grp),
                   ("g_ffn2", g_ffn2, m_g_ffn2, v_g_ffn2), ("g_final", g_final, m_g_final, v_g_final)]
        like = [w for _, w, _, _ in small_w]
        packed = _adamw_small(_pack(like), _pack([grads_small[n] for n, _, _, _ in small_w]),
                              _pack([m for _, _, m, _ in small_w]), _pack([v for _, _, _, v in small_w]))
        updated = {n: (grads_small[n], dl, m2, v2)
                   for (n, _, _, _), dl, m2, v2 in zip(small_w, *[_unpack(pk, like) for pk in packed])}
        return loss_all, updated, packed[0]

    started = []

    def scatter(names, partials, label):
        state, token = _scatter_start(partials, f"scatter_grads_{label}")
        started.append((names, state))
        return token

    for l in reversed(range(depth)):
        wl, s = weights[l], saved[l]
        dh, agu, dyb, small["g_ffn2", l] = _ffn_bwd_act(dh, s["h2"], row(g_ffn2[l]), s["gu2"], wl["up2"], wl["dn2"], dep)
        ddn2 = _ffn_bwd_w(agu, 2, 1, dyb, agu, f"ffn_bwd_w_down_l{l}_ffn2").reshape(N_DEV, -1, d)
        dup2 = _ffn_bwd_w(agu, 0, 2, s["n2"], ddn2, f"ffn_bwd_w_up_l{l}_ffn2").reshape(N_DEV, -1, d)
        dep = scatter([("w_ffn2_up", l), ("w_ffn2_down", l)], [dup2, ddn2], f"l{l}_ffn2")
        dyconv, doh, delta, dwout, small["g_grp", l] = _mix_out_bwd(dh, s["y"], row(g_grp[l]), wl["wout"], s["mt"], dep)
        dp, dmkv, small["conv_w", l], small["sinks", l] = _mix_core_bwd(
            s["p"], s["qh"], dyconv, doh, delta, s["lse"], s["mkv"], conv_full[l], row(sinks[l]), bias_key)
        dwkv, small["g_mem", l] = _memkv_bwd(dmkv, memin, row(g_mem[l]), wl["wkv"], s["nt_mem"])
        dh, dwin, small["g_mix", l] = _mix_proj_bwd(dp, dh, s["h1"], row(g_mix[l]), wl["win"], s["n_mix"])
        dep = scatter([("w_in", l), ("w_mem_kv", l), ("w_out", l)],
                      [dwin.reshape(N_DEV, -1, d), dwkv.reshape(N_DEV, -1, 2 * D_MEMQ), dwout.reshape(N_DEV, -1, d)],
                      f"l{l}_mix")
        dh, agu, dyb, small["g_ffn1", l] = _ffn_bwd_act(dh, s["h0"], row(g_ffn1[l]), s["gu1"], wl["up1"], wl["dn1"], dep)
        order_after = agu
        if l == 0:
            loss, small_out, order_after = reduce_small()
        ddn1 = _ffn_bwd_w(agu, 2, 1, dyb, order_after, f"ffn_bwd_w_down_l{l}_ffn1").reshape(N_DEV, -1, d)
        if l > 0:
            dup1 = _ffn_bwd_w(agu, 0, 2, s["n1"], ddn1, f"ffn_bwd_w_up_l{l}_ffn1").reshape(N_DEV, -1, d)
            dep = scatter([("w_ffn1_up", l), ("w_ffn1_down", l)], [dup1, ddn1], f"l{l}_ffn1")
        else:
            dep = scatter([("w_ffn1_down", l)], [ddn1], f"l{l}_ffn1_down")
            dup1 = _ffn_bwd_w(agu, 0, 2, s["n1"], dep, f"ffn_bwd_w_up_l{l}_ffn1").reshape(N_DEV, -1, d)
            dep = scatter([("w_ffn1_up", l)], [dup1], f"l{l}_ffn1_up")
    grad_x = dh[None]

    big = {"w_ffn2_up": (w_ffn2_up, m_w_ffn2_up, v_w_ffn2_up, True), "w_ffn2_down": (w_ffn2_down, m_w_ffn2_down, v_w_ffn2_down, False),
           "w_in": (w_in, m_w_in, v_w_in, True), "w_mem_kv": (w_mem_kv, m_w_mem_kv, v_w_mem_kv, False),
           "w_out": (w_out, m_w_out, v_w_out, False), "w_ffn1_up": (w_ffn1_up, m_w_ffn1_up, v_w_ffn1_up, True),
           "w_ffn1_down": (w_ffn1_down, m_w_ffn1_down, v_w_ffn1_down, False)}
    me_index = jnp.reshape(me, (1,)).astype(jnp.int32)
    sharded, landed = {}, {}

    def finish(groups, after):
        for names, state in groups:
            owns, lands = _scatter_wait(state, after)
            for key, own, land in zip(names, owns, lands):
                landed[key] = (own, land)
            after = lands[0]
            for name in dict.fromkeys(n for n, _ in names):
                if name not in sharded and all((name, l) in landed for l in range(depth)):
                    w, m, v, transposed = big[name]
                    fix = tr if transposed else (lambda a: a)
                    res = _adamw_sharded(me_index, [landed[name, l] for l in range(depth)], fix(w), fix(m), fix(v), after)
                    sharded[name] = tuple(fix(r) for r in res)
                    after = res[0]
        return after

    finish(started[-2:], finish(started[:-2], dep))

    order = ["g_ffn1", "w_ffn1_up", "w_ffn1_down", "g_mix", "w_in", "conv_w", "sinks", "g_mem", "w_mem_kv", "g_grp",
             "w_out", "g_ffn2", "w_ffn2_up", "w_ffn2_down", "g_final"]
    results = {**sharded, **small_out}
    outs = [loss, grad_x]
    for part in range(4):
        outs += [results[n][part] for n in order]
    return tuple(outs)
```

```python
import numpy as np
import jax
import jax.numpy as jnp
from jax import lax
from jax.experimental import pallas as pl
from jax.experimental.pallas import tpu as pltpu
from jax.experimental.pallas import tpu_sc as plsc

F32 = jnp.float32
BF16 = jnp.bfloat16

N_DEV = 8
EPS = 1e-6
N_SWA_HEADS = 8
N_SWA_KV = 2
SWA_GROUP = N_SWA_HEADS // N_SWA_KV
HEAD_DIM = 64
N_MEM_HEADS = 4
D_CONV = 256
BLOCK = 128
D_SWA = N_SWA_HEADS * HEAD_DIM
D_KV = N_SWA_KV * HEAD_DIM
D_MEMQ = N_MEM_HEADS * HEAD_DIM
D_MIX = D_CONV + D_SWA + D_MEMQ
D_IN = 3 * D_CONV + D_SWA + 2 * D_KV + D_MEMQ
COL_BG, COL_CG, COL_U = 0, D_CONV, 2 * D_CONV
COL_Q = 3 * D_CONV
COL_K = COL_Q + D_SWA
COL_V = COL_K + D_KV
COL_QM = COL_V + D_KV
MIX_GROUPS = ((0, D_CONV), (D_CONV, D_CONV + D_SWA), (D_CONV + D_SWA, D_MIX))
SLOPES = tuple(2.0 ** (-8.0 * (i + 1) / N_SWA_HEADS) for i in range(N_SWA_HEADS))
SCALE = HEAD_DIM ** -0.5
NEG = -1e30

ADAM_LR = 0.001
ADAM_B1 = 0.9
ADAM_B2 = 0.999
ADAM_EPS = 1e-08
ADAM_WD = 0.01
ADAM_STEP = 10

V7X_VMEM_BYTES = 64 * 1024 * 1024
VMEM_LIMIT = (V7X_VMEM_BYTES * 3) // 4
MESH = pl.DeviceIdType.MESH


def _pcall(body, **kw):
    return pl.pallas_call(body, **kw)


def _params(sem=None, vmem=VMEM_LIMIT):
    return pltpu.CompilerParams(dimension_semantics=sem, vmem_limit_bytes=vmem)


def _dot(a, b):
    return lax.dot_general(a, b, (((1,), (0,)), ((), ())), preferred_element_type=F32)


def _dot_nt(a, b):
    return lax.dot_general(a, b, (((1,), (1,)), ((), ())), preferred_element_type=F32)


def _dot_tn(a, b):
    return lax.dot_general(a, b, (((0,), (0,)), ((), ())), preferred_element_type=F32)


def _rstd(x):
    return lax.rsqrt(jnp.mean(x * x, axis=-1, keepdims=True) + EPS)


def _sigmoid(x):
    return 1.0 / (1.0 + jnp.exp(-x))


def _token(x):
    return x[(0,) * (x.ndim - 2)][:8, :128]


def _sum8(x):
    r, w = x.shape
    return jnp.sum(x.reshape(r // 8, 8, w), axis=0)


def _tok_block(t, rows=512):
    return min(rows, t)


def _feat_block(f):
    return f // (N_DEV // 2)


def _ffn_fwd(h, g, wup_t, wdn):
    t, d = h.shape
    f = wdn.shape[0]
    tm, tf = _tok_block(t), _feat_block(f)
    ni, nj = t // tm, f // tf

    def body(h_ref, g_ref, wup_ref, wdn_ref, ho_ref, gu_ref, n_ref, nt_ref, acc_ref):
        j = pl.program_id(1)

        @pl.when(j == 0)
        def _():
            hh = h_ref[...]
            n = hh * _rstd(hh) * g_ref[...]
            n_ref[...] = n.astype(BF16)
            nt_ref[...] = n.T.astype(BF16)
            acc_ref[...] = jnp.zeros_like(acc_ref)

        nt = nt_ref[...]
        gate = _dot(wup_ref[0], nt)
        up = _dot(wup_ref[1], nt)
        gu_ref[0] = gate.astype(BF16)
        gu_ref[1] = up.astype(BF16)
        a = gate * _sigmoid(gate) * up
        acc_ref[...] += _dot_tn(a.astype(BF16), wdn_ref[...])

        @pl.when(j == nj - 1)
        def _():
            ho_ref[...] = h_ref[...] + 0.5 * acc_ref[...]

    return _pcall(
        body, name="ffn_fwd", grid=(ni, nj),
        in_specs=[pl.BlockSpec((tm, d), lambda i, j: (i, 0)),
                  pl.BlockSpec((1, d), lambda i, j: (0, 0)),
                  pl.BlockSpec((2, tf, d), lambda i, j: (0, j, 0)),
                  pl.BlockSpec((tf, d), lambda i, j: (j, 0))],
        out_specs=[pl.BlockSpec((tm, d), lambda i, j: (i, 0)),
                   pl.BlockSpec((2, tf, tm), lambda i, j: (0, j, i)),
                   pl.BlockSpec((tm, d), lambda i, j: (i, 0))],
        out_shape=[jax.ShapeDtypeStruct((t, d), F32),
                   jax.ShapeDtypeStruct((2, f, t), BF16),
                   jax.ShapeDtypeStruct((t, d), BF16)],
        scratch_shapes=[pltpu.VMEM((d, tm), BF16), pltpu.VMEM((tm, d), F32)],
        compiler_params=_params(("parallel", "arbitrary")),
    )(h, g, wup_t, wdn)


def _ffn_bwd_act(dho, h, g, gu, wup_t, wdn, dep):
    t, d = h.shape
    f = wdn.shape[0]
    tm, tf = _tok_block(t), _feat_block(f)
    ni, nj = t // tm, f // tf

    def body(dho_ref, h_ref, g_ref, gu_ref, wup_ref, wdn_ref, dep_ref, dh_ref, agu_ref, dyb_ref, dg_ref, dyt_ref, acc_ref):
        i = pl.program_id(0)
        j = pl.program_id(1)

        @pl.when(j == 0)
        def _():
            dy0 = 0.5 * dho_ref[...]
            dyb_ref[...] = dy0.astype(BF16)
            dyt_ref[...] = dy0.T.astype(BF16)
            acc_ref[...] = jnp.zeros_like(acc_ref)

        da = _dot(wdn_ref[...], dyt_ref[...]).astype(BF16)
        gate = gu_ref[0]
        up = gu_ref[1]
        sg = _sigmoid(gate)
        silu = gate * sg
        dgate = da * up * (sg * (1.0 + gate * (1.0 - sg)))
        dup = da * silu
        agu_ref[0] = dgate
        agu_ref[1] = dup
        agu_ref[2] = silu * up
        acc_ref[...] += _dot_tn(dgate, wup_ref[0])
        acc_ref[...] += _dot_tn(dup, wup_ref[1])

        @pl.when(j == nj - 1)
        def _():
            hh = h_ref[...]
            r = _rstd(hh)
            xhat = hh * r
            dnf = acc_ref[...]
            dxh = dnf * g_ref[...]
            dh_ref[...] = dho_ref[...] + r * (dxh - xhat * jnp.mean(dxh * xhat, axis=-1, keepdims=True))
            part = _sum8(dnf * xhat)

            @pl.when(i == 0)
            def _():
                dg_ref[...] = part

            @pl.when(i > 0)
            def _():
                dg_ref[...] += part

    return _pcall(
        body, name="ffn_bwd_act", grid=(ni, nj),
        in_specs=[pl.BlockSpec((tm, d), lambda i, j: (i, 0)),
                  pl.BlockSpec((tm, d), lambda i, j: (i, 0)),
                  pl.BlockSpec((1, d), lambda i, j: (0, 0)),
                  pl.BlockSpec((2, tf, tm), lambda i, j: (0, j, i)),
                  pl.BlockSpec((2, tf, d), lambda i, j: (0, j, 0)),
                  pl.BlockSpec((tf, d), lambda i, j: (j, 0)),
                  pl.BlockSpec(memory_space=pl.ANY)],
        out_specs=[pl.BlockSpec((tm, d), lambda i, j: (i, 0)),
                   pl.BlockSpec((3, tf, tm), lambda i, j: (0, j, i)),
                   pl.BlockSpec((tm, d), lambda i, j: (i, 0)),
                   pl.BlockSpec((8, d), lambda i, j: (0, 0))],
        out_shape=[jax.ShapeDtypeStruct((t, d), F32),
                   jax.ShapeDtypeStruct((3, f, t), BF16),
                   jax.ShapeDtypeStruct((t, d), BF16),
                   jax.ShapeDtypeStruct((8, d), F32)],
        scratch_shapes=[pltpu.VMEM((d, tm), BF16), pltpu.VMEM((tm, d), F32)],
        compiler_params=_params(("arbitrary", "arbitrary")),
    )(dho, h, g, gu, wup_t, wdn, _token(dep))


def _ffn_bwd_w(agu, first, count, rhs, dep, name):
    _, f, t = agu.shape
    d = rhs.shape[1]
    tm, tf = _tok_block(t, 2048), _feat_block(f)
    ni, nj = t // tm, f // tf

    def body(lhs_ref, rhs_ref, dep_ref, dw_ref, acc_ref):
        i = pl.program_id(1)
        @pl.when(i == 0)
        def _():
            acc_ref[...] = jnp.zeros_like(acc_ref)

        rb = rhs_ref[...]
        for k in range(count):
            acc_ref[k] += _dot(lhs_ref[k], rb)

        @pl.when(i == ni - 1)
        def _():
            dw_ref[...] = acc_ref[...].astype(BF16)

    return _pcall(
        body, name=name, grid=(nj, ni),
        in_specs=[pl.BlockSpec((count, tf, tm), lambda j, i: (first // count, j, i)),
                  pl.BlockSpec((tm, d), lambda j, i: (i, 0)),
                  pl.BlockSpec(memory_space=pl.ANY)],
        out_specs=pl.BlockSpec((count, tf, d), lambda j, i: (0, j, 0)),
        out_shape=jax.ShapeDtypeStruct((count, f, d), BF16),
        scratch_shapes=[pltpu.VMEM((count, tf, d), F32)],
        compiler_params=_params(("parallel", "arbitrary")),
    )(agu, rhs, _token(dep))


N_HEADS = N_SWA_HEADS + N_MEM_HEADS


def _q_col(hd):
    return COL_Q + HEAD_DIM * hd if hd < N_SWA_HEADS else COL_QM + HEAD_DIM * (hd - N_SWA_HEADS)


def _mix_proj_fwd(h, g, win_t):
    t, d = h.shape
    tm = _tok_block(t)

    def body(h_ref, g_ref, win_ref, p_ref, n_ref, qh_ref):
        hh = h_ref[...]
        n = (hh * _rstd(hh) * g_ref[...]).astype(BF16)
        n_ref[...] = n
        proj = _dot_nt(n, win_ref[...])
        p_ref[...] = proj.astype(BF16)
        for hd in range(N_HEADS):
            c0 = _q_col(hd)
            qh_ref[hd] = (proj[:, c0:c0 + HEAD_DIM] * SCALE).astype(BF16)

    return _pcall(
        body, name="mix_proj_fwd", grid=(t // tm,),
        in_specs=[pl.BlockSpec((tm, d), lambda i: (i, 0)),
                  pl.BlockSpec((1, d), lambda i: (0, 0)),
                  pl.BlockSpec((D_IN, d), lambda i: (0, 0))],
        out_specs=[pl.BlockSpec((tm, D_IN), lambda i: (i, 0)),
                   pl.BlockSpec((tm, d), lambda i: (i, 0)),
                   pl.BlockSpec((N_HEADS, tm, HEAD_DIM), lambda i: (0, i, 0))],
        out_shape=[jax.ShapeDtypeStruct((t, D_IN), BF16), jax.ShapeDtypeStruct((t, d), BF16),
                   jax.ShapeDtypeStruct((N_HEADS, t, HEAD_DIM), BF16)],
        compiler_params=_params(("parallel",)),
    )(h, g, win_t)


def _memkv_fwd(mem, g, wkv, dep):
    m, d = mem.shape

    def body(mem_ref, g_ref, w_ref, dep_ref, mkv_ref, nt_ref):
        mm = mem_ref[...]
        n = mm * _rstd(mm) * g_ref[...]
        nt_ref[...] = n.T.astype(BF16)
        mkv_ref[...] = _dot(n.astype(BF16), w_ref[...]).astype(BF16)

    return _pcall(
        body, name="memkv_fwd", grid=(1,),
        in_specs=[pl.BlockSpec((m, d), lambda i: (0, 0)),
                  pl.BlockSpec((1, d), lambda i: (0, 0)),
                  pl.BlockSpec((d, 2 * D_MEMQ), lambda i: (0, 0)),
                  pl.BlockSpec(memory_space=pl.ANY)],
        out_specs=[pl.BlockSpec((m, 2 * D_MEMQ), lambda i: (0, 0)),
                   pl.BlockSpec((d, m), lambda i: (0, 0))],
        out_shape=[jax.ShapeDtypeStruct((m, 2 * D_MEMQ), BF16), jax.ShapeDtypeStruct((d, m), BF16)],
        compiler_params=_params(("arbitrary",)),
    )(mem, g, wkv, _token(dep))


def _memkv_bwd(dmkv, mem, g, wkv, nt):
    m, d = mem.shape

    def body(dmkv_ref, mem_ref, g_ref, w_ref, nt_ref, dw_ref, dg_ref):
        db = dmkv_ref[...].astype(BF16)
        dw_ref[...] = _dot(nt_ref[...], db).astype(BF16)
        dn = _dot_nt(db, w_ref[...])
        mm = mem_ref[...]
        dg_ref[...] = _sum8(dn * (mm * _rstd(mm)))

    return _pcall(
        body, name="memkv_bwd", grid=(1,),
        in_specs=[pl.BlockSpec((m, 2 * D_MEMQ), lambda i: (0, 0)),
                  pl.BlockSpec((m, d), lambda i: (0, 0)),
                  pl.BlockSpec((1, d), lambda i: (0, 0)),
                  pl.BlockSpec((d, 2 * D_MEMQ), lambda i: (0, 0)),
                  pl.BlockSpec((d, m), lambda i: (0, 0))],
        out_specs=[pl.BlockSpec((d, 2 * D_MEMQ), lambda i: (0, 0)),
                   pl.BlockSpec((8, d), lambda i: (0, 0))],
        out_shape=[jax.ShapeDtypeStruct((d, 2 * D_MEMQ), BF16), jax.ShapeDtypeStruct((8, d), F32)],
        compiler_params=_params(("arbitrary",)),
    )(dmkv, mem, g, wkv, nt)


def _shift_rows(v, k, edge_rows, row):
    out = pltpu.roll(v, k, 0)
    for r in range(k):
        out = jnp.where(row == r, edge_rows[r], out)
    return out


def _shift_rows_up(v, k, edge_rows, row):
    n = v.shape[0]
    out = pltpu.roll(v, n - k, 0)
    for r in range(k):
        out = jnp.where(row == n - k + r, edge_rows[r], out)
    return out


GROUP_ROWS = SWA_GROUP * BLOCK
BIAS_CUR, BIAS_PREV, BIAS_NONE = 0, 1, 2


def _bias_table():
    tq = np.arange(BLOCK)[:, None]
    sk = np.arange(BLOCK)[None, :]
    slopes = np.asarray(SLOPES, np.float32)[:, None, None]
    cur = np.where(tq >= sk, -slopes * (tq - sk).astype(np.float32), NEG)
    prev = np.where(sk > tq, -slopes * (tq + BLOCK - sk).astype(np.float32), NEG)
    none = np.full_like(cur, NEG)
    tok = np.stack([cur, prev, none]).astype(np.float32).reshape(3, N_SWA_KV, GROUP_ROWS, BLOCK)
    return jnp.asarray(np.ascontiguousarray(tok.transpose(0, 1, 3, 2)))


def _head_cols(hd):
    return D_CONV + HEAD_DIM * hd


def _mix_core_fwd(p, qh, mkv, convw, sinks, bias_key):
    t = p.shape[0]
    m = mkv.shape[0]
    nb = t // BLOCK

    def body(sk_ref, pc_ref, pkv_ref, ppc_ref, ppu_ref, qh_ref, mkv_ref, cw_ref, bc_ref, bp_ref, y_ref, l_ref):
        i = pl.program_id(0)
        prevf = (i > 0).astype(F32)
        row = lax.broadcasted_iota(jnp.int32, (BLOCK, D_CONV), 0)

        bg = pc_ref[:, COL_BG:COL_BG + D_CONV].astype(F32)
        cg = pc_ref[:, COL_CG:COL_CG + D_CONV].astype(F32)
        u = pc_ref[:, COL_U:COL_U + D_CONV].astype(F32)
        vv = cg * u
        pvv = ppc_ref[...].astype(F32) * ppu_ref[...].astype(F32) * prevf
        vv1 = _shift_rows(vv, 1, [pvv[15:16]], row)
        vv2 = _shift_rows(vv, 2, [pvv[14:15], pvv[15:16]], row)
        w = cw_ref[...]
        y_ref[:, 0:D_CONV] = bg * (w[0:1] * vv2 + w[1:2] * vv1 + w[2:3] * vv)

        head_row = lax.broadcasted_iota(jnp.int32, (128, BLOCK), 0)
        lse_t = jnp.zeros((128, BLOCK), F32)
        for kv in range(N_SWA_KV):
            heads = range(kv * SWA_GROUP, (kv + 1) * SWA_GROUP)
            kc = pc_ref[:, COL_K + HEAD_DIM * kv:COL_K + HEAD_DIM * (kv + 1)]
            vc = pc_ref[:, COL_V + HEAD_DIM * kv:COL_V + HEAD_DIM * (kv + 1)]
            kp = pkv_ref[:, HEAD_DIM * kv:HEAD_DIM * (kv + 1)]
            vp = pkv_ref[:, D_KV + HEAD_DIM * kv:D_KV + HEAD_DIM * (kv + 1)]
            qg = qh_ref[kv * SWA_GROUP:(kv + 1) * SWA_GROUP].reshape(GROUP_ROWS, HEAD_DIM)
            sc = _dot_nt(kc, qg) + bc_ref[0, kv]
            sp = _dot_nt(kp, qg) + bp_ref[0, kv]
            sink = jnp.concatenate([jnp.full((1, BLOCK), sk_ref[0, hd], F32) for hd in heads], axis=1)
            mx = jnp.maximum(jnp.max(jnp.maximum(sc, sp), axis=0, keepdims=True), sink)
            ec = jnp.exp(sc - mx)
            ep = jnp.exp(sp - mx)
            den = jnp.sum(ec + ep, axis=0, keepdims=True) + jnp.exp(sink - mx)
            ot = (_dot_tn(vc, ec.astype(BF16)) + _dot_tn(vp, ep.astype(BF16))) / den
            lse = mx + jnp.log(den)
            for gi, hd in enumerate(heads):
                span = slice(gi * BLOCK, (gi + 1) * BLOCK)
                y_ref[:, _head_cols(hd):_head_cols(hd) + HEAD_DIM] = ot[:, span].T
                lse_t = jnp.where(head_row == hd, lse[:, span], lse_t)

        for hm in range(N_MEM_HEADS):
            hd = N_SWA_HEADS + hm
            mk = mkv_ref[:, HEAD_DIM * hm:HEAD_DIM * (hm + 1)]
            mv = mkv_ref[:, D_MEMQ + HEAD_DIM * hm:D_MEMQ + HEAD_DIM * (hm + 1)]
            s = _dot_nt(mk, qh_ref[hd])
            mx = jnp.max(s, axis=0, keepdims=True)
            e = jnp.exp(s - mx)
            den = jnp.sum(e, axis=0, keepdims=True)
            y_ref[:, _head_cols(hd):_head_cols(hd) + HEAD_DIM] = (_dot_tn(mv, e.astype(BF16)) / den).T
            lse_t = jnp.where(head_row == hd, mx + jnp.log(den), lse_t)
        l_ref[...] = lse_t.T

    kv_col = COL_K // (2 * D_KV)
    bias_block = (1, N_SWA_KV, BLOCK, GROUP_ROWS)
    return _pcall(
        body, name="mix_core_fwd", grid=(nb,),
        in_specs=[pl.BlockSpec(memory_space=pltpu.SMEM),
                  pl.BlockSpec((BLOCK, D_IN), lambda i: (i, 0)),
                  pl.BlockSpec((BLOCK, 2 * D_KV), lambda i: (jnp.maximum(i - 1, 0), kv_col)),
                  pl.BlockSpec((16, D_CONV), lambda i: (jnp.maximum(i * (BLOCK // 16) - 1, 0), COL_CG // D_CONV)),
                  pl.BlockSpec((16, D_CONV), lambda i: (jnp.maximum(i * (BLOCK // 16) - 1, 0), COL_U // D_CONV)),
                  pl.BlockSpec((N_HEADS, BLOCK, HEAD_DIM), lambda i: (0, i, 0)),
                  pl.BlockSpec((m, 2 * D_MEMQ), lambda i: (0, 0)),
                  pl.BlockSpec((3, D_CONV), lambda i: (0, 0)),
                  pl.BlockSpec(bias_block, lambda i: (BIAS_CUR, 0, 0, 0)),
                  pl.BlockSpec(bias_block, lambda i: (jnp.where(i == 0, BIAS_NONE, BIAS_PREV), 0, 0, 0))],
        out_specs=[pl.BlockSpec((BLOCK, D_MIX), lambda i: (i, 0)),
                   pl.BlockSpec((BLOCK, 128), lambda i: (i, 0))],
        out_shape=[jax.ShapeDtypeStruct((t, D_MIX), F32), jax.ShapeDtypeStruct((t, 128), F32)],
        compiler_params=_params(("parallel",)),
    )(sinks, p, p, p, p, qh, mkv, convw, bias_key, bias_key)


def _mix_core_bwd(p, qh, dyconv, doh, delta, lse, mkv, convw, sinks, bias_key):
    t = p.shape[0]
    m = mkv.shape[0]
    nb = t // BLOCK

    def body(sk_ref, pc_ref, pkv_ref, ppc_ref, ppu_ref, pnb_ref, dyc_ref, dyn_ref, qc_ref, qn_ref, doc_ref, don_ref,
             dlc_ref, dln_ref, lc_ref, ln_ref, mkv_ref, cw_ref, bp_ref, bct_ref, bnt_ref,
             dp_ref, dmkv_ref, dcw_ref, dsk_ref):
        i = pl.program_id(0)
        prevf = (i > 0).astype(F32)
        nextf = (i < nb - 1).astype(F32)
        row = lax.broadcasted_iota(jnp.int32, (BLOCK, D_CONV), 0)

        @pl.when(i == 0)
        def _():
            dmkv_ref[...] = jnp.zeros_like(dmkv_ref)
            dcw_ref[...] = jnp.zeros_like(dcw_ref)
            dsk_ref[...] = jnp.zeros_like(dsk_ref)

        bg = pc_ref[:, COL_BG:COL_BG + D_CONV].astype(F32)
        cg = pc_ref[:, COL_CG:COL_CG + D_CONV].astype(F32)
        u = pc_ref[:, COL_U:COL_U + D_CONV].astype(F32)
        vv = cg * u
        pvv = ppc_ref[...].astype(F32) * ppu_ref[...].astype(F32) * prevf
        vv1 = _shift_rows(vv, 1, [pvv[15:16]], row)
        vv2 = _shift_rows(vv, 2, [pvv[14:15], pvv[15:16]], row)
        w = cw_ref[...]
        yconv = w[0:1] * vv2 + w[1:2] * vv1 + w[2:3] * vv
        dyo = dyc_ref[...]
        dyc = dyo * bg
        nxt = dyn_ref[...] * pnb_ref[...].astype(F32) * nextf
        d1 = _shift_rows_up(dyc, 1, [nxt[0:1]], row)
        d2 = _shift_rows_up(dyc, 2, [nxt[0:1], nxt[1:2]], row)
        dvv = w[2:3] * dyc + w[1:2] * d1 + w[0:1] * d2
        dp_ref[:, COL_BG:COL_BG + D_CONV] = (dyo * yconv).astype(BF16)
        dp_ref[:, COL_CG:COL_CG + D_CONV] = (dvv * u).astype(BF16)
        dp_ref[:, COL_U:COL_U + D_CONV] = (dvv * cg).astype(BF16)
        dcw_ref[0:1, :] += jnp.sum(dyc * vv2, axis=0, keepdims=True)
        dcw_ref[1:2, :] += jnp.sum(dyc * vv1, axis=0, keepdims=True)
        dcw_ref[2:3, :] += jnp.sum(dyc * vv, axis=0, keepdims=True)

        lse_t, dl_t = lc_ref[...].T, dlc_ref[...].T
        lse_nt, dl_nt = ln_ref[...].T, dln_ref[...].T

        def stack_rows(tile_t, heads):
            return jnp.concatenate([tile_t[hd:hd + 1, :] for hd in heads], axis=1)

        lane8 = jnp.where(lax.broadcasted_iota(jnp.int32, (8, 128), 0) == 0,
                          lax.broadcasted_iota(jnp.int32, (8, 128), 1), -1)
        dsk = jnp.zeros((8, 128), F32)
        for kv in range(N_SWA_KV):
            heads = range(kv * SWA_GROUP, (kv + 1) * SWA_GROUP)
            kc = pc_ref[:, COL_K + HEAD_DIM * kv:COL_K + HEAD_DIM * (kv + 1)]
            vc = pc_ref[:, COL_V + HEAD_DIM * kv:COL_V + HEAD_DIM * (kv + 1)]
            kp = pkv_ref[:, HEAD_DIM * kv:HEAD_DIM * (kv + 1)]
            vp = pkv_ref[:, D_KV + HEAD_DIM * kv:D_KV + HEAD_DIM * (kv + 1)]
            qg = qc_ref[kv * SWA_GROUP:(kv + 1) * SWA_GROUP].reshape(GROUP_ROWS, HEAD_DIM)
            dog = doc_ref[kv * SWA_GROUP:(kv + 1) * SWA_GROUP].reshape(GROUP_ROWS, HEAD_DIM)
            qn = qn_ref[kv * SWA_GROUP:(kv + 1) * SWA_GROUP].reshape(GROUP_ROWS, HEAD_DIM)
            don = don_ref[kv * SWA_GROUP:(kv + 1) * SWA_GROUP].reshape(GROUP_ROWS, HEAD_DIM)
            lse_row, dl_row = stack_rows(lse_t, heads), stack_rows(dl_t, heads)
            ptp = jnp.exp(_dot_nt(kp, qg) + bp_ref[0, kv] - lse_row)
            dstp = (ptp * (_dot_nt(vp, dog) - dl_row)).astype(BF16)
            dq = _dot_tn(dstp, kp)
            pt = jnp.exp(_dot_nt(kc, qg) + bct_ref[0, kv] - lse_row)
            dst = (pt * (_dot_nt(vc, dog) - dl_row)).astype(BF16)
            dv = _dot(pt.astype(BF16), dog)
            dk = _dot(dst, qg)
            dq = dq + _dot_tn(dst, kc)
            ptn = jnp.exp(_dot_nt(kc, qn) + bnt_ref[0, kv] - stack_rows(lse_nt, heads))
            dstn = (ptn * (_dot_nt(vc, don) - stack_rows(dl_nt, heads))).astype(BF16)
            dv = dv + _dot(ptn.astype(BF16), don)
            dk = dk + _dot(dstn, qn)
            dp_ref[:, COL_K + HEAD_DIM * kv:COL_K + HEAD_DIM * (kv + 1)] = dk.astype(BF16)
            dp_ref[:, COL_V + HEAD_DIM * kv:COL_V + HEAD_DIM * (kv + 1)] = dv.astype(BF16)
            sink = jnp.concatenate([jnp.full((1, BLOCK), sk_ref[0, hd], F32) for hd in heads], axis=1)
            sink_term = jnp.exp(sink - lse_row) * dl_row
            for gi, hd in enumerate(heads):
                span = slice(gi * BLOCK, (gi + 1) * BLOCK)
                dp_ref[:, _q_col(hd):_q_col(hd) + HEAD_DIM] = (dq[span] * SCALE).astype(BF16)
                dsk = dsk + jnp.where(lane8 == hd, -jnp.sum(sink_term[:, span], axis=1, keepdims=True), 0.0)
        dsk_ref[...] += dsk

        for hm in range(N_MEM_HEADS):
            hd = N_SWA_HEADS + hm
            qm, dom = qc_ref[hd], doc_ref[hd]
            mk = mkv_ref[:, HEAD_DIM * hm:HEAD_DIM * (hm + 1)]
            mv = mkv_ref[:, D_MEMQ + HEAD_DIM * hm:D_MEMQ + HEAD_DIM * (hm + 1)]
            pt = jnp.exp(_dot_nt(mk, qm) - lse_t[hd:hd + 1, :])
            dst = (pt * (_dot_nt(mv, dom) - dl_t[hd:hd + 1, :])).astype(BF16)
            dp_ref[:, _q_col(hd):_q_col(hd) + HEAD_DIM] = (_dot_tn(dst, mk) * SCALE).astype(BF16)
            dmkv_ref[:, HEAD_DIM * hm:HEAD_DIM * (hm + 1)] += _dot(dst, qm)
            dmkv_ref[:, D_MEMQ + HEAD_DIM * hm:D_MEMQ + HEAD_DIM * (hm + 1)] += _dot(pt.astype(BF16), dom)

    cur = lambda i: (i, 0)
    const = lambda i: (0, 0)
    rows16 = BLOCK // 16
    last16 = t // 16 - 1
    before = lambda col: (lambda i: (jnp.maximum(i * rows16 - 1, 0), col))
    after = lambda i: (jnp.minimum((i + 1) * rows16, last16), 0)
    heads_cur = lambda i: (0, i, 0)
    heads_next = lambda i: (0, jnp.minimum(i + 1, nb - 1), 0)
    stat_next = lambda i: (jnp.minimum(i + 1, nb - 1), 0)
    key_block = (1, N_SWA_KV, BLOCK, GROUP_ROWS)
    head_block = (N_HEADS, BLOCK, HEAD_DIM)
    return _pcall(
        body, name="mix_core_bwd", grid=(nb,),
        in_specs=[pl.BlockSpec(memory_space=pltpu.SMEM),
                  pl.BlockSpec((BLOCK, D_IN), cur),
                  pl.BlockSpec((BLOCK, 2 * D_KV), lambda i: (jnp.maximum(i - 1, 0), COL_K // (2 * D_KV))),
                  pl.BlockSpec((16, D_CONV), before(COL_CG // D_CONV)),
                  pl.BlockSpec((16, D_CONV), before(COL_U // D_CONV)),
                  pl.BlockSpec((16, D_CONV), after),
                  pl.BlockSpec((BLOCK, D_CONV), cur),
                  pl.BlockSpec((16, D_CONV), after),
                  pl.BlockSpec(head_block, heads_cur), pl.BlockSpec(head_block, heads_next),
                  pl.BlockSpec(head_block, heads_cur), pl.BlockSpec(head_block, heads_next),
                  pl.BlockSpec((BLOCK, 128), cur), pl.BlockSpec((BLOCK, 128), stat_next),
                  pl.BlockSpec((BLOCK, 128), cur), pl.BlockSpec((BLOCK, 128), stat_next),
                  pl.BlockSpec((m, 2 * D_MEMQ), const),
                  pl.BlockSpec((3, D_CONV), const),
                  pl.BlockSpec(key_block, lambda i: (jnp.where(i == 0, BIAS_NONE, BIAS_PREV), 0, 0, 0)),
                  pl.BlockSpec(key_block, lambda i: (BIAS_CUR, 0, 0, 0)),
                  pl.BlockSpec(key_block, lambda i: (jnp.where(i == nb - 1, BIAS_NONE, BIAS_PREV), 0, 0, 0))],
        out_specs=[pl.BlockSpec((BLOCK, D_IN), cur),
                   pl.BlockSpec((m, 2 * D_MEMQ), const),
                   pl.BlockSpec((8, D_CONV), const),
                   pl.BlockSpec((8, 128), const)],
        out_shape=[jax.ShapeDtypeStruct((t, D_IN), BF16),
                   jax.ShapeDtypeStruct((m, 2 * D_MEMQ), F32),
                   jax.ShapeDtypeStruct((8, D_CONV), F32),
                   jax.ShapeDtypeStruct((8, 128), F32)],
        compiler_params=_params(("arbitrary",)),
    )(sinks, p, p, p, p, p, dyconv, dyconv, qh, qh, doh, doh, delta, delta, lse, lse, mkv, convw,
      bias_key, bias_key, bias_key)


def _group_norms(y):
    out = []
    for a, b in MIX_GROUPS:
        ys = y[:, a:b]
        r = _rstd(ys)
        out.append((ys * r, r))
    return out


def _mix_out_fwd(y, h, g, wout):
    t, d = h.shape
    tm = _tok_block(t)

    def body(y_ref, h_ref, g_ref, w_ref, ho_ref, mt_ref):
        yhat = jnp.concatenate([yh for yh, _ in _group_norms(y_ref[...])], axis=-1)
        mixed = yhat * g_ref[...]
        mt_ref[...] = mixed.T.astype(BF16)
        ho_ref[...] = h_ref[...] + _dot(mixed.astype(BF16), w_ref[...])

    return _pcall(
        body, name="mix_out_fwd", grid=(t // tm,),
        in_specs=[pl.BlockSpec((tm, D_MIX), lambda i: (i, 0)),
                  pl.BlockSpec((tm, d), lambda i: (i, 0)),
                  pl.BlockSpec((1, D_MIX), lambda i: (0, 0)),
                  pl.BlockSpec((D_MIX, d), lambda i: (0, 0))],
        out_specs=[pl.BlockSpec((tm, d), lambda i: (i, 0)),
                   pl.BlockSpec((D_MIX, tm), lambda i: (0, i))],
        out_shape=[jax.ShapeDtypeStruct((t, d), F32), jax.ShapeDtypeStruct((D_MIX, t), BF16)],
        compiler_params=_params(("parallel",)),
    )(y, h, g, wout)


def _head_indicator():
    ind = np.zeros((D_MIX, 128), np.float32)
    for hd in range(N_HEADS):
        ind[_head_cols(hd):_head_cols(hd) + HEAD_DIM, hd] = 1.0
    return jnp.asarray(ind, BF16)


def _mix_out_bwd(dho, y, g, wout, mt, dep):
    t, d = dho.shape
    tm = _tok_block(t, 1024)
    ni = t // tm

    def body(dho_ref, y_ref, g_ref, w_ref, mt_ref, ind_ref, dep_ref, dyc_ref, doh_ref, dl_ref, dw_ref, dg_ref, acc_ref):
        i = pl.program_id(0)
        dhb = dho_ref[...].astype(BF16)
        dm = _dot_nt(dhb, w_ref[...])
        pw = _dot(mt_ref[...], dhb)
        gg = g_ref[...]
        yy = y_ref[...]
        dys = []
        dgs = []
        for (a, b), (yhat, r) in zip(MIX_GROUPS, _group_norms(yy)):
            dmg = dm[:, a:b]
            dgs.append(_sum8(dmg * yhat))
            dyh = dmg * gg[:, a:b]
            dys.append(r * (dyh - yhat * jnp.mean(dyh * yhat, axis=-1, keepdims=True)))
        dy = jnp.concatenate(dys, axis=-1)
        dyc_ref[...] = dy[:, 0:D_CONV]
        for hd in range(N_HEADS):
            doh_ref[hd] = dy[:, _head_cols(hd):_head_cols(hd) + HEAD_DIM].astype(BF16)
        prod = dy * yy
        hi = prod.astype(BF16)
        lo = (prod - hi.astype(F32)).astype(BF16)
        dl_ref[...] = _dot(hi, ind_ref[...]) + _dot(lo, ind_ref[...])
        part = jnp.concatenate(dgs, axis=-1)

        @pl.when(i == 0)
        def _():
            acc_ref[...] = pw
            dg_ref[...] = part

        @pl.when(i > 0)
        def _():
            acc_ref[...] += pw
            dg_ref[...] += part

        @pl.when(i == ni - 1)
        def _():
            dw_ref[...] = acc_ref[...].astype(BF16)

    return _pcall(
        body, name="mix_out_bwd", grid=(ni,),
        in_specs=[pl.BlockSpec((tm, d), lambda i: (i, 0)),
                  pl.BlockSpec((tm, D_MIX), lambda i: (i, 0)),
                  pl.BlockSpec((1, D_MIX), lambda i: (0, 0)),
                  pl.BlockSpec((D_MIX, d), lambda i: (0, 0)),
                  pl.BlockSpec((D_MIX, tm), lambda i: (0, i)),
                  pl.BlockSpec((D_MIX, 128), lambda i: (0, 0)),
                  pl.BlockSpec(memory_space=pl.ANY)],
        out_specs=[pl.BlockSpec((tm, D_CONV), lambda i: (i, 0)),
                   pl.BlockSpec((N_HEADS, tm, HEAD_DIM), lambda i: (0, i, 0)),
                   pl.BlockSpec((tm, 128), lambda i: (i, 0)),
                   pl.BlockSpec((D_MIX, d), lambda i: (0, 0)),
                   pl.BlockSpec((8, D_MIX), lambda i: (0, 0))],
        out_shape=[jax.ShapeDtypeStruct((t, D_CONV), F32),
                   jax.ShapeDtypeStruct((N_HEADS, t, HEAD_DIM), BF16),
                   jax.ShapeDtypeStruct((t, 128), F32),
                   jax.ShapeDtypeStruct((D_MIX, d), BF16),
                   jax.ShapeDtypeStruct((8, D_MIX), F32)],
        scratch_shapes=[pltpu.VMEM((D_MIX, d), F32)],
        compiler_params=_params(("arbitrary",)),
    )(dho, y, g, wout, mt, _head_indicator(), _token(dep))


def _mix_proj_bwd(dp, dho, h, g, win_t, n):
    t, d = h.shape
    tm = _tok_block(t)
    ni = t // tm

    def body(dp_ref, dho_ref, h_ref, g_ref, w_ref, n_ref, dh_ref, dw_ref, dg_ref, acc_ref):
        i = pl.program_id(0)
        dpb = dp_ref[...]
        dn = _dot(dpb, w_ref[...])

        @pl.when(i == 0)
        def _():
            acc_ref[...] = jnp.zeros_like(acc_ref)

        acc_ref[...] += _dot_tn(dpb, n_ref[...])
        hh = h_ref[...]
        r = _rstd(hh)
        xhat = hh * r
        dxh = dn * g_ref[...]
        dh_ref[...] = dho_ref[...] + r * (dxh - xhat * jnp.mean(dxh * xhat, axis=-1, keepdims=True))
        part = _sum8(dn * xhat)

        @pl.when(i == 0)
        def _():
            dg_ref[...] = part

        @pl.when(i > 0)
        def _():
            dg_ref[...] += part

        @pl.when(i == ni - 1)
        def _():
            dw_ref[...] = acc_ref[...].astype(BF16)

    return _pcall(
        body, name="mix_proj_bwd", grid=(ni,),
        in_specs=[pl.BlockSpec((tm, D_IN), lambda i: (i, 0)),
                  pl.BlockSpec((tm, d), lambda i: (i, 0)),
                  pl.BlockSpec((tm, d), lambda i: (i, 0)),
                  pl.BlockSpec((1, d), lambda i: (0, 0)),
                  pl.BlockSpec((D_IN, d), lambda i: (0, 0)),
                  pl.BlockSpec((tm, d), lambda i: (i, 0))],
        out_specs=[pl.BlockSpec((tm, d), lambda i: (i, 0)),
                   pl.BlockSpec((D_IN, d), lambda i: (0, 0)),
                   pl.BlockSpec((8, d), lambda i: (0, 0))],
        out_shape=[jax.ShapeDtypeStruct((t, d), F32),
                   jax.ShapeDtypeStruct((D_IN, d), BF16),
                   jax.ShapeDtypeStruct((8, d), F32)],
        scratch_shapes=[pltpu.VMEM((D_IN, d), F32)],
        compiler_params=_params(("arbitrary",)),
    )(dp, dho, h, g, win_t, n)


def _final_loss(h, g, tgt):
    t, d = h.shape
    tm = _tok_block(t)

    def body(h_ref, g_ref, t_ref, dh_ref, ls_ref, dg_ref):
        i = pl.program_id(0)
        hh = h_ref[...]
        r = _rstd(hh)
        xhat = hh * r
        gg = g_ref[...]
        err = xhat * gg - t_ref[...]
        dy = err * (1.0 / d)
        dxh = dy * gg
        dh_ref[...] = r * (dxh - xhat * jnp.mean(dxh * xhat, axis=-1, keepdims=True))
        lpart = _sum8(err * err)
        gpart = _sum8(dy * xhat)

        @pl.when(i == 0)
        def _():
            ls_ref[...] = lpart
            dg_ref[...] = gpart

        @pl.when(i > 0)
        def _():
            ls_ref[...] += lpart
            dg_ref[...] += gpart

    return _pcall(
        body, name="final_loss", grid=(t // tm,),
        in_specs=[pl.BlockSpec((tm, d), lambda i: (i, 0)),
                  pl.BlockSpec((1, d), lambda i: (0, 0)),
                  pl.BlockSpec((tm, d), lambda i: (i, 0))],
        out_specs=[pl.BlockSpec((tm, d), lambda i: (i, 0)),
                   pl.BlockSpec((8, d), lambda i: (0, 0)),
                   pl.BlockSpec((8, d), lambda i: (0, 0))],
        out_shape=[jax.ShapeDtypeStruct((t, d), F32),
                   jax.ShapeDtypeStruct((8, d), F32),
                   jax.ShapeDtypeStruct((8, d), F32)],
        compiler_params=_params(("arbitrary",)),
    )(h, g, tgt)


def _position():
    return lax.axis_index("x"), lax.axis_index("y"), lax.axis_index("c")


def _flip(v, bit):
    return 1 - v if bit else v


def _peer(k):
    x, y, c = _position()
    return _flip(x, k & 4), _flip(y, k & 2), _flip(c, k & 1)


def _slot(px, py, pc):
    return 4 * px + 2 * py + pc


def _handshake(peers):
    barrier = pltpu.get_barrier_semaphore()
    for peer in peers:
        pl.semaphore_signal(barrier, inc=1, device_id=peer, device_id_type=MESH)
    pl.semaphore_wait(barrier, len(peers))


def _sequencer_call(body, name, collective_id, out_type, scratch_types, operands):
    return pl.kernel(
        body, out_type=out_type, mesh=plsc.ScalarSubcoreMesh(axis_name="sequencer", num_cores=1), name=name,
        scratch_types=scratch_types, compiler_params=pltpu.CompilerParams(collective_id=collective_id),
    )(*operands)


def _all_gather(shards, name, collective_id):
    nt = len(shards)

    def body(*refs):
        xs = refs[:nt]
        outs = refs[nt:2 * nt]
        send_sems, recv_sems, local_sems = refs[2 * nt:]
        x, y, c = _position()
        me, sibling = (x, y, c), (x, y, 1 - c)
        chips = [(1 - x, y), (x, 1 - y), (1 - x, 1 - y)]
        _handshake([sibling] + [(*chip, c) for chip in chips])

        def copy(t, k, block, to, src=None):
            dst = outs[t].at[_slot(*block)]
            return pltpu.make_async_remote_copy(
                src_ref=dst if src is None else src, dst_ref=dst,
                send_sem=send_sems.at[t, k], recv_sem=recv_sems.at[t, k],
                device_id=to, device_id_type=MESH)

        mine = [pltpu.make_async_copy(xs[t], outs[t].at[_slot(*me)], local_sems.at[t]) for t in range(nt)]
        for cp in mine:
            cp.start()
        first = []
        for t in range(nt):
            first.append(copy(t, 0, me, sibling, src=xs[t]))
            first += [copy(t, 1 + j, me, (*chip, c), src=xs[t]) for j, chip in enumerate(chips)]
        for cp in first:
            cp.start()
        passed = []
        for j, chip in enumerate(chips):
            for t in range(nt):
                copy(t, 1 + j, (*chip, c), me).wait_recv()
                fwd = copy(t, 4 + j, (*chip, c), sibling)
                fwd.start()
                passed.append(fwd)
        for t in range(nt):
            copy(t, 0, sibling, me).wait_recv()
            for j, chip in enumerate(chips):
                copy(t, 4 + j, (*chip, 1 - c), me).wait_recv()
        for cp in first + passed:
            cp.wait_send()
        for cp in mine:
            cp.wait()

    return _sequencer_call(
        body, name, collective_id,
        out_type=[jax.ShapeDtypeStruct((N_DEV,) + s.shape, s.dtype) for s in shards],
        scratch_types=[pltpu.SemaphoreType.DMA((nt, 7)), pltpu.SemaphoreType.DMA((nt, 7)),
                       pltpu.SemaphoreType.DMA((nt,))],
        operands=shards)


def _scatter_copy(srcs, lands, send_sems, recv_sems, t, k):
    peer = _peer(k)
    return pltpu.make_async_remote_copy(
        src_ref=srcs[t].at[_slot(*peer)], dst_ref=lands[t].at[k],
        send_sem=send_sems.at[t * (N_DEV - 1) + k - 1], recv_sem=recv_sems.at[t * (N_DEV - 1) + k - 1],
        device_id=peer, device_id_type=MESH)


def _scatter_start(partials, name):
    nt = len(partials)

    def body(*refs):
        srcs, lands = refs[:nt], refs[nt:2 * nt]
        send_sems, recv_sems = refs[2 * nt], refs[2 * nt + 1]
        token = refs[-1]
        for k in range(1, N_DEV):
            for t in range(nt):
                _scatter_copy(srcs, lands, send_sems, recv_sems, t, k).start()
        token[...] = jnp.zeros_like(token)

    hbm = pl.BlockSpec(memory_space=pltpu.HBM)
    sem = pl.BlockSpec(memory_space=pltpu.SEMAPHORE)
    shapes = [pltpu.HBM(p.shape, p.dtype) for p in partials]
    lands = [pltpu.with_memory_space_constraint(lax.empty(p.shape, p.dtype), pltpu.HBM) for p in partials]
    srcs = [pltpu.with_memory_space_constraint(p, pltpu.HBM) for p in partials]
    out = _pcall(
        body, name=name,
        out_shape=[pltpu.SemaphoreType.DMA((nt * (N_DEV - 1),))] * 2 + shapes + shapes
        + [jax.ShapeDtypeStruct((8, 128), F32)],
        in_specs=[hbm] * (2 * nt),
        out_specs=[sem, sem] + [hbm] * (2 * nt) + [pl.BlockSpec(memory_space=pltpu.VMEM)],
        input_output_aliases={i: 2 + i for i in range(2 * nt)},
        compiler_params=pltpu.CompilerParams(has_side_effects=pltpu.SideEffectType.DATAFLOW_SIDE_EFFECTING),
    )(*srcs, *lands)
    return (nt, name, out[:-1]), out[-1]


def _scatter_wait(state, after):
    nt, name, (send_sems, recv_sems, *thru) = state

    def body(*refs):
        srcs, lands = refs[:nt], refs[nt:2 * nt]
        send_sems, recv_sems = refs[2 * nt], refs[2 * nt + 1]
        for k in range(1, N_DEV):
            for t in range(nt):
                copy = _scatter_copy(srcs, lands, send_sems, recv_sems, t, k)
                copy.wait_send()
                copy.wait_recv()

    hbm = pl.BlockSpec(memory_space=pltpu.HBM)
    sem = pl.BlockSpec(memory_space=pltpu.SEMAPHORE)
    out = _pcall(
        body, name=name + "_wait",
        out_shape=[pltpu.HBM(a.shape, a.dtype) for a in thru],
        in_specs=[hbm] * (2 * nt) + [sem, sem, pl.BlockSpec(memory_space=pl.ANY)],
        out_specs=[hbm] * (2 * nt),
        input_output_aliases={i: i for i in range(2 * nt)},
        compiler_params=pltpu.CompilerParams(has_side_effects=pltpu.SideEffectType.DATAFLOW_SIDE_EFFECTING),
    )(*thru, send_sems, recv_sems, _token(after))
    return out[:nt], out[nt:]


def _all_reduce_rows(v):
    nv, _, w = v.shape

    def body(v_ref, out_ref, mine_ref, gath_ref, send_sems, recv_sems):
        x, y, c = _position()
        me = _slot(x, y, c)
        mine_ref[...] = jnp.sum(v_ref[...], axis=1)

        def copy(k):
            return pltpu.make_async_remote_copy(
                src_ref=mine_ref, dst_ref=gath_ref.at[me],
                send_sem=send_sems.at[k - 1], recv_sem=recv_sems.at[k - 1],
                device_id=_peer(k), device_id_type=MESH)

        def arrival(k):
            return pltpu.make_async_remote_copy(
                src_ref=mine_ref, dst_ref=gath_ref.at[_slot(*_peer(k))],
                send_sem=send_sems.at[k - 1], recv_sem=recv_sems.at[k - 1],
                device_id=_peer(k), device_id_type=MESH)

        sent = [copy(k) for k in range(1, N_DEV)]
        for cp in sent:
            cp.start()
        gath_ref[me] = mine_ref[...]
        for k in range(1, N_DEV):
            arrival(k).wait_recv()
        for cp in sent:
            cp.wait_send()
        total = gath_ref[0]
        for s in range(1, N_DEV):
            total = total + gath_ref[s]
        out_ref[...] = total

    vmem = pl.BlockSpec(memory_space=pltpu.VMEM)
    return _pcall(
        body, name="all_reduce_rows",
        in_specs=[vmem], out_specs=vmem,
        out_shape=jax.ShapeDtypeStruct((nv, w), F32),
        scratch_shapes=[pltpu.VMEM((nv, w), F32), pltpu.VMEM((N_DEV, nv, w), F32),
                        pltpu.SemaphoreType.DMA((7,)), pltpu.SemaphoreType.DMA((7,))],
    )(v)


def _adamw_math(w, g, m, v):
    m2 = ADAM_B1 * m + (1.0 - ADAM_B1) * g
    v2 = ADAM_B2 * v + (1.0 - ADAM_B2) * (g * g)
    m_hat = m2 / (1.0 - ADAM_B1 ** ADAM_STEP)
    v_hat = v2 / (1.0 - ADAM_B2 ** ADAM_STEP)
    delta = -ADAM_LR * (m_hat / (jnp.sqrt(v_hat) + ADAM_EPS) + ADAM_WD * w)
    return delta, m2, v2


def _row_block(r):
    for cand in (256, 176, 128):
        if r % cand == 0:
            return cand
    return r


def _adamw_sharded(me, grads, w, m, v, dep):
    (own0, land0), (own1, land1) = grads
    _, r, c = land0.shape
    tr = _row_block(r)
    nr = r // tr

    def body(me_ref, o0_ref, l0_ref, o1_ref, l1_ref, w_ref, m_ref, v_ref, dep_ref, g_ref, d_ref, m2_ref, v2_ref):
        layer = pl.program_id(0)

        def total(own_ref, land_ref):
            acc = own_ref[0].astype(F32)
            for k in range(1, N_DEV):
                acc = acc + land_ref[k].astype(F32)
            return acc

        g = jnp.where(layer == 0, total(o0_ref, l0_ref), total(o1_ref, l1_ref))
        delta, m2, v2 = _adamw_math(w_ref[0], g, m_ref[0], v_ref[0])
        g_ref[0] = g
        d_ref[0] = delta
        m2_ref[0] = m2
        v2_ref[0] = v2

    rows0 = lambda l, i: jnp.where(l == 0, i, nr - 1)
    rows1 = lambda l, i: jnp.where(l == 1, i, 0)
    shard = pl.BlockSpec((1, tr, c), lambda l, i, me_ref: (l, i, 0))
    out = jax.ShapeDtypeStruct((2, r, c), F32)
    return _pcall(
        body, name="adamw_sharded",
        grid_spec=pltpu.PrefetchScalarGridSpec(
            num_scalar_prefetch=1, grid=(2, nr),
            in_specs=[pl.BlockSpec((1, tr, c), lambda l, i, me_ref: (me_ref[0], rows0(l, i), 0)),
                      pl.BlockSpec((N_DEV, tr, c), lambda l, i, me_ref: (0, rows0(l, i), 0)),
                      pl.BlockSpec((1, tr, c), lambda l, i, me_ref: (me_ref[0], rows1(l, i), 0)),
                      pl.BlockSpec((N_DEV, tr, c), lambda l, i, me_ref: (0, rows1(l, i), 0)),
                      shard, shard, shard, pl.BlockSpec(memory_space=pl.ANY)],
            out_specs=[shard, shard, shard, shard]),
        out_shape=[out, out, out, out],
        compiler_params=_params(("arbitrary", "arbitrary")),
    )(me, own0, land0, own1, land1, w, m, v, _token(dep))


def _adamw_small(w, g, m, v):
    def body(w_ref, g_ref, m_ref, v_ref, d_ref, m2_ref, v2_ref):
        delta, m2, v2 = _adamw_math(w_ref[...], g_ref[...], m_ref[...], v_ref[...])
        d_ref[...] = delta
        m2_ref[...] = m2
        v2_ref[...] = v2

    spec = pl.BlockSpec(w.shape, lambda i: (0, 0))
    out = jax.ShapeDtypeStruct(w.shape, F32)
    return _pcall(
        body, name="adamw_small", grid=(1,),
        in_specs=[spec] * 4, out_specs=[spec] * 3, out_shape=[out] * 3,
        compiler_params=_params(("arbitrary",)),
    )(w, g, m, v)


def _pack(arrs):
    flat = jnp.concatenate([a.reshape(-1) for a in arrs])
    n = flat.shape[0]
    rows = -(-n // 1024) * 8
    return jnp.pad(flat, (0, rows * 128 - n)).reshape(rows, 128)


def _unpack(packed, like):
    flat = packed.reshape(-1)
    out, off = [], 0
    for a in like:
        out.append(flat[off:off + a.size].reshape(a.shape))
        off += a.size
    return out


def kernel(x, mem, g_ffn1, w_ffn1_up, w_ffn1_down, g_mix, w_in, conv_w, sinks, g_mem, w_mem_kv, g_grp, w_out, g_ffn2, w_ffn2_up, w_ffn2_down, g_final, loss_target, m_g_ffn1, m_w_ffn1_up, m_w_ffn1_down, m_g_mix, m_w_in, m_conv_w, m_sinks, m_g_mem, m_w_mem_kv, m_g_grp, m_w_out, m_g_ffn2, m_w_ffn2_up, m_w_ffn2_down, m_g_final, v_g_ffn1, v_w_ffn1_up, v_w_ffn1_down, v_g_mix, v_w_in, v_conv_w, v_sinks, v_g_mem, v_w_mem_kv, v_g_grp, v_w_out, v_g_ffn2, v_w_ffn2_up, v_w_ffn2_down, v_g_final):
    depth = g_ffn1.shape[0]
    t, d = x.shape[1], x.shape[2]
    width = max(d, D_MIX)
    me = _slot(*_position())
    conv_shard = conv_w.shape[2]

    xin, memin, tgt = x[0], mem[0], loss_target[0]

    conv_tile = jnp.zeros((depth * 8, 128), F32).at[:, :conv_shard].set(
        jnp.pad(conv_w, ((0, 0), (0, 8 - conv_w.shape[1]), (0, 0))).reshape(depth * 8, conv_shard))
    tr = lambda a: jnp.swapaxes(a, -1, -2)
    bf = lambda a: a.astype(BF16)
    weights = []
    collective_id = 0
    for l in range(depth):
        groups = [[bf(tr(w_ffn1_up[l])), bf(w_ffn1_down[l])] + ([conv_tile] if l == 0 else []),
                  [bf(tr(w_in[l])), bf(w_mem_kv[l]), bf(w_out[l])],
                  [bf(tr(w_ffn2_up[l])), bf(w_ffn2_down[l])]]
        full = []
        for gi, shards in enumerate(groups):
            full.append(_all_gather(shards, f"all_gather_l{l}_g{gi}", collective_id))
            collective_id += 1
        if l == 0:
            conv_full = full[0][2].reshape(N_DEV, depth, 8, 128)[:, :, :3, :conv_shard]
            conv_full = conv_full.transpose(1, 2, 0, 3).reshape(depth, 3, N_DEV * conv_shard)
        weights.append(dict(
            up1=full[0][0].reshape(2, -1, d), dn1=full[0][1].reshape(-1, d),
            win=full[1][0].reshape(D_IN, d), wkv=full[1][1].reshape(d, 2 * D_MEMQ), wout=full[1][2].reshape(D_MIX, d),
            up2=full[2][0].reshape(2, -1, d), dn2=full[2][1].reshape(-1, d)))

    row = lambda a: a.reshape(1, -1)
    bias_key = _bias_table()

    h = xin
    saved = []
    for l in range(depth):
        wl = weights[l]
        s = dict(h0=h)
        h, s["gu1"], s["n1"] = _ffn_fwd(h, row(g_ffn1[l]), wl["up1"], wl["dn1"])
        s["h1"] = h
        s["p"], s["n_mix"], s["qh"] = _mix_proj_fwd(h, row(g_mix[l]), wl["win"])
        s["mkv"], s["nt_mem"] = _memkv_fwd(memin, row(g_mem[l]), wl["wkv"], s["p"])
        s["y"], s["lse"] = _mix_core_fwd(s["p"], s["qh"], s["mkv"], conv_full[l], row(sinks[l]), bias_key)
        h, s["mt"] = _mix_out_fwd(s["y"], h, row(g_grp[l]), wl["wout"])
        s["h2"] = h
        h, s["gu2"], s["n2"] = _ffn_fwd(h, row(g_ffn2[l]), wl["up2"], wl["dn2"])
        saved.append(s)

    dh, loss_part, dg_final = _final_loss(h, row(g_final), tgt)

    small = {}
    dep = loss_part

    def reduce_small():
        def lanes(a):
            return jnp.pad(a, ((0, 0), (0, width - a.shape[1])))

        def first_row(a):
            return lanes(jnp.pad(a, ((0, 8 - a.shape[0]), (0, 0))))

        vec_names = ["g_ffn1", "g_mix", "g_mem", "g_grp", "g_ffn2", "sinks"]
        tiles = [lanes(small[n, l]) for n in vec_names for l in range(depth)]
        tiles += [first_row(small["conv_w", l][k:k + 1]) for l in range(depth) for k in range(3)]
        tiles.append(lanes(dg_final))
        n_real = len(tiles)
        tiles.append(lanes(loss_part))
        tiles += [jnp.zeros((8, width), F32)] * (-len(tiles) % 8)
        summed = _all_reduce_rows(jnp.stack(tiles))
        loss_all = 0.5 * jnp.sum(summed[n_real]) / d

        def vec(n, wd):
            return jnp.stack([summed[vec_names.index(n) * depth + l, :wd] for l in range(depth)])

        conv_base = len(vec_names) * depth
        conv_grad = jnp.stack([jnp.stack([summed[conv_base + 3 * l + k, :D_CONV] for k in range(3)])
                               for l in range(depth)])
        grads_small = {
            "g_ffn1": vec("g_ffn1", d), "g_mix": vec("g_mix", d), "g_mem": vec("g_mem", d),
            "g_grp": vec("g_grp", D_MIX), "g_ffn2": vec("g_ffn2", d), "sinks": vec("sinks", N_SWA_HEADS),
            "conv_w": lax.dynamic_slice_in_dim(conv_grad, me * conv_shard, conv_shard, axis=2),
            "g_final": summed[n_real - 1, :d],
        }
        small_w = [("g_ffn1", g_ffn1, m_g_ffn1, v_g_ffn1), ("g_mix", g_mix, m_g_mix, v_g_mix),
                   ("conv_w", conv_w, m_conv_w, v_conv_w), ("sinks", sinks, m_sinks, v_sinks),
                   ("g_mem", g_mem, m_g_mem, v_g_mem), ("g_grp", g_grp, m_g_grp, v_g_grp),
                   ("g_ffn2", g_ffn2, m_g_ffn2, v_g_ffn2), ("g_final", g_final, m_g_final, v_g_final)]
        like = [w for _, w, _, _ in small_w]
        packed = _adamw_small(_pack(like), _pack([grads_small[n] for n, _, _, _ in small_w]),
                              _pack([m for _, _, m, _ in small_w]), _pack([v for _, _, _, v in small_w]))
        updated = {n: (grads_small[n], dl, m2, v2)
                   for (n, _, _, _), dl, m2, v2 in zip(small_w, *[_unpack(pk, like) for pk in packed])}
        return loss_all, updated, packed[0]

    started = []

    def scatter(names, partials, label):
        state, token = _scatter_start(partials, f"scatter_grads_{label}")
        started.append((names, state))
        return token

    for l in reversed(range(depth)):
        wl, s = weights[l], saved[l]
        dh, agu, dyb, small["g_ffn2", l] = _ffn_bwd_act(dh, s["h2"], row(g_ffn2[l]), s["gu2"], wl["up2"], wl["dn2"], dep)
        ddn2 = _ffn_bwd_w(agu, 2, 1, dyb, agu, f"ffn_bwd_w_down_l{l}_ffn2").reshape(N_DEV, -1, d)
        dup2 = _ffn_bwd_w(agu, 0, 2, s["n2"], ddn2, f"ffn_bwd_w_up_l{l}_ffn2").reshape(N_DEV, -1, d)
        dep = scatter([("w_ffn2_up", l), ("w_ffn2_down", l)], [dup2, ddn2], f"l{l}_ffn2")
        dyconv, doh, delta, dwout, small["g_grp", l] = _mix_out_bwd(dh, s["y"], row(g_grp[l]), wl["wout"], s["mt"], dep)
        dp, dmkv, small["conv_w", l], small["sinks", l] = _mix_core_bwd(
            s["p"], s["qh"], dyconv, doh, delta, s["lse"], s["mkv"], conv_full[l], row(sinks[l]), bias_key)
        dwkv, small["g_mem", l] = _memkv_bwd(dmkv, memin, row(g_mem[l]), wl["wkv"], s["nt_mem"])
        dh, dwin, small["g_mix", l] = _mix_proj_bwd(dp, dh, s["h1"], row(g_mix[l]), wl["win"], s["n_mix"])
        dep = scatter([("w_in", l), ("w_mem_kv", l), ("w_out", l)],
                      [dwin.reshape(N_DEV, -1, d), dwkv.reshape(N_DEV, -1, 2 * D_MEMQ), dwout.reshape(N_DEV, -1, d)],
                      f"l{l}_mix")
        dh, agu, dyb, small["g_ffn1", l] = _ffn_bwd_act(dh, s["h0"], row(g_ffn1[l]), s["gu1"], wl["up1"], wl["dn1"], dep)
        order_after = agu
        if l == 0:
            loss, small_out, order_after = reduce_small()
        ddn1 = _ffn_bwd_w(agu, 2, 1, dyb, order_after, f"ffn_bwd_w_down_l{l}_ffn1").reshape(N_DEV, -1, d)
        if l > 0:
            dup1 = _ffn_bwd_w(agu, 0, 2, s["n1"], ddn1, f"ffn_bwd_w_up_l{l}_ffn1").reshape(N_DEV, -1, d)
            dep = scatter([("w_ffn1_up", l), ("w_ffn1_down", l)], [dup1, ddn1], f"l{l}_ffn1")
        else:
            dep = scatter([("w_ffn1_down", l)], [ddn1], f"l{l}_ffn1_down")
            dup1 = _ffn_bwd_w(agu, 0, 2, s["n1"], dep, f"ffn_bwd_w_up_l{l}_ffn1").reshape(N_DEV, -1, d)
            dep = scatter([("w_ffn1_up", l)], [dup1], f"l{l}_ffn1_up")
    grad_x = dh[None]

    big = {"w_ffn2_up": (w_ffn2_up, m_w_ffn2_up, v_w_ffn2_up, True), "w_ffn2_down": (w_ffn2_down, m_w_ffn2_down, v_w_ffn2_down, False),
           "w_in": (w_in, m_w_in, v_w_in, True), "w_mem_kv": (w_mem_kv, m_w_mem_kv, v_w_mem_kv, False),
           "w_out": (w_out, m_w_out, v_w_out, False), "w_ffn1_up": (w_ffn1_up, m_w_ffn1_up, v_w_ffn1_up, True),
           "w_ffn1_down": (w_ffn1_down, m_w_ffn1_down, v_w_ffn1_down, False)}
    me_index = jnp.reshape(me, (1,)).astype(jnp.int32)
    sharded, landed = {}, {}

    def finish(groups, after):
        for names, state in groups:
            owns, lands = _scatter_wait(state, after)
            for key, own, land in zip(names, owns, lands):
                landed[key] = (own, land)
            after = lands[0]
            for name in dict.fromkeys(n for n, _ in names):
                if name not in sharded and all((name, l) in landed for l in range(depth)):
                    w, m, v, transposed = big[name]
                    fix = tr if transposed else (lambda a: a)
                    res = _adamw_sharded(me_index, [landed[name, l] for l in range(depth)], fix(w), fix(m), fix(v), after)
                    sharded[name] = tuple(fix(r) for r in res)
                    after = res[0]
        return after

    finish(started[-2:], finish(started[:-2], dep))

    order = ["g_ffn1", "w_ffn1_up", "w_ffn1_down", "g_mix", "w_in", "conv_w", "sinks", "g_mem", "w_mem_kv", "g_grp",
             "w_out", "g_ffn2", "w_ffn2_up", "w_ffn2_down", "g_final"]
    results = {**sharded, **small_out}
    outs = [loss, grad_x]
    for part in range(4):
        outs += [results[n][part] for n in order]
    return tuple(outs)
```

```python
import numpy as np
import jax
import jax.numpy as jnp
from jax import lax
from jax.experimental import pallas as pl
from jax.experimental.pallas import tpu as pltpu
from jax.experimental.pallas import tpu_sc as plsc

F32 = jnp.float32
BF16 = jnp.bfloat16

N_DEV = 8
EPS = 1e-6
N_SWA_HEADS = 8
N_SWA_KV = 2
SWA_GROUP = N_SWA_HEADS // N_SWA_KV
HEAD_DIM = 64
N_MEM_HEADS = 4
D_CONV = 256
BLOCK = 128
D_SWA = N_SWA_HEADS * HEAD_DIM
D_KV = N_SWA_KV * HEAD_DIM
D_MEMQ = N_MEM_HEADS * HEAD_DIM
D_MIX = D_CONV + D_SWA + D_MEMQ
D_IN = 3 * D_CONV + D_SWA + 2 * D_KV + D_MEMQ
COL_BG, COL_CG, COL_U = 0, D_CONV, 2 * D_CONV
COL_Q = 3 * D_CONV
COL_K = COL_Q + D_SWA
COL_V = COL_K + D_KV
COL_QM = COL_V + D_KV
MIX_GROUPS = ((0, D_CONV), (D_CONV, D_CONV + D_SWA), (D_CONV + D_SWA, D_MIX))
SLOPES = tuple(2.0 ** (-8.0 * (i + 1) / N_SWA_HEADS) for i in range(N_SWA_HEADS))
SCALE = HEAD_DIM ** -0.5
NEG = -1e30

ADAM_LR = 0.001
ADAM_B1 = 0.9
ADAM_B2 = 0.999
ADAM_EPS = 1e-08
ADAM_WD = 0.01
ADAM_STEP = 10

V7X_VMEM_BYTES = 64 * 1024 * 1024
VMEM_LIMIT = (V7X_VMEM_BYTES * 3) // 4
MESH = pl.DeviceIdType.MESH


def _pcall(body, **kw):
    return pl.pallas_call(body, **kw)


def _params(sem=None, vmem=VMEM_LIMIT):
    return pltpu.CompilerParams(dimension_semantics=sem, vmem_limit_bytes=vmem)


def _dot(a, b):
    return lax.dot_general(a, b, (((1,), (0,)), ((), ())), preferred_element_type=F32)


def _dot_nt(a, b):
    return lax.dot_general(a, b, (((1,), (1,)), ((), ())), preferred_element_type=F32)


def _dot_tn(a, b):
    return lax.dot_general(a, b, (((0,), (0,)), ((), ())), preferred_element_type=F32)


def _rstd(x):
    return lax.rsqrt(jnp.mean(x * x, axis=-1, keepdims=True) + EPS)


def _sigmoid(x):
    return 1.0 / (1.0 + jnp.exp(-x))


def _sum8(x):
    r, w = x.shape
    return jnp.sum(x.reshape(r // 8, 8, w), axis=0)


def _tok_block(t, rows=512):
    return min(rows, t)


def _feat_block(f):
    return f // (N_DEV // 2)


def _ffn_fwd(h, g, wup_t, wdn):
    t, d = h.shape
    f = wdn.shape[0]
    tm, tf = _tok_block(t), _feat_block(f)
    ni, nj = t // tm, f // tf

    def body(h_ref, g_ref, wup_ref, wdn_ref, ho_ref, gu_ref, n_ref, nt_ref, acc_ref):
        j = pl.program_id(1)

        @pl.when(j == 0)
        def _():
            hh = h_ref[...]
            n = hh * _rstd(hh) * g_ref[...]
            n_ref[...] = n.astype(BF16)
            nt_ref[...] = n.T.astype(BF16)
            acc_ref[...] = jnp.zeros_like(acc_ref)

        nt = nt_ref[...]
        gate = _dot(wup_ref[0], nt)
        up = _dot(wup_ref[1], nt)
        gu_ref[0] = gate.astype(BF16)
        gu_ref[1] = up.astype(BF16)
        a = gate * _sigmoid(gate) * up
        acc_ref[...] += _dot_tn(a.astype(BF16), wdn_ref[...])

        @pl.when(j == nj - 1)
        def _():
            ho_ref[...] = h_ref[...] + 0.5 * acc_ref[...]

    return _pcall(
        body, name="ffn_fwd", grid=(ni, nj),
        in_specs=[pl.BlockSpec((tm, d), lambda i, j: (i, 0)),
                  pl.BlockSpec((1, d), lambda i, j: (0, 0)),
                  pl.BlockSpec((2, tf, d), lambda i, j: (0, j, 0)),
                  pl.BlockSpec((tf, d), lambda i, j: (j, 0))],
        out_specs=[pl.BlockSpec((tm, d), lambda i, j: (i, 0)),
                   pl.BlockSpec((2, tf, tm), lambda i, j: (0, j, i)),
                   pl.BlockSpec((tm, d), lambda i, j: (i, 0))],
        out_shape=[jax.ShapeDtypeStruct((t, d), F32),
                   jax.ShapeDtypeStruct((2, f, t), BF16),
                   jax.ShapeDtypeStruct((t, d), BF16)],
        scratch_shapes=[pltpu.VMEM((d, tm), BF16), pltpu.VMEM((tm, d), F32)],
        compiler_params=_params(("parallel", "arbitrary")),
    )(h, g, wup_t, wdn)


def _ffn_bwd_act(dho, h, g, gu, wup_t, wdn, dep):
    t, d = h.shape
    f = wdn.shape[0]
    tm, tf = _tok_block(t), _feat_block(f)
    ni, nj = t // tm, f // tf

    def body(dho_ref, h_ref, g_ref, gu_ref, wup_ref, wdn_ref, dep_ref, dh_ref, agu_ref, dyb_ref, dg_ref, dyt_ref, acc_ref):
        i = pl.program_id(0)
        j = pl.program_id(1)

        @pl.when(j == 0)
        def _():
            dy0 = 0.5 * dho_ref[...]
            dyb_ref[...] = dy0.astype(BF16)
            dyt_ref[...] = dy0.T.astype(BF16)
            acc_ref[...] = jnp.zeros_like(acc_ref)

        da = _dot(wdn_ref[...], dyt_ref[...]).astype(BF16)
        gate = gu_ref[0]
        up = gu_ref[1]
        sg = _sigmoid(gate)
        silu = gate * sg
        dgate = da * up * (sg * (1.0 + gate * (1.0 - sg)))
        dup = da * silu
        agu_ref[0] = dgate
        agu_ref[1] = dup
        agu_ref[2] = silu * up
        acc_ref[...] += _dot_tn(dgate, wup_ref[0])
        acc_ref[...] += _dot_tn(dup, wup_ref[1])

        @pl.when(j == nj - 1)
        def _():
            hh = h_ref[...]
            r = _rstd(hh)
            xhat = hh * r
            dnf = acc_ref[...]
            dxh = dnf * g_ref[...]
            dh_ref[...] = dho_ref[...] + r * (dxh - xhat * jnp.mean(dxh * xhat, axis=-1, keepdims=True))
            part = _sum8(dnf * xhat)

            @pl.when(i == 0)
            def _():
                dg_ref[...] = part

            @pl.when(i > 0)
            def _():
                dg_ref[...] += part

    return _pcall(
        body, name="ffn_bwd_act", grid=(ni, nj),
        in_specs=[pl.BlockSpec((tm, d), lambda i, j: (i, 0)),
                  pl.BlockSpec((tm, d), lambda i, j: (i, 0)),
                  pl.BlockSpec((1, d), lambda i, j: (0, 0)),
                  pl.BlockSpec((2, tf, tm), lambda i, j: (0, j, i)),
                  pl.BlockSpec((2, tf, d), lambda i, j: (0, j, 0)),
                  pl.BlockSpec((tf, d), lambda i, j: (j, 0)),
                  pl.BlockSpec(memory_space=pl.ANY)],
        out_specs=[pl.BlockSpec((tm, d), lambda i, j: (i, 0)),
                   pl.BlockSpec((3, tf, tm), lambda i, j: (0, j, i)),
                   pl.BlockSpec((tm, d), lambda i, j: (i, 0)),
                   pl.BlockSpec((8, d), lambda i, j: (0, 0))],
        out_shape=[jax.ShapeDtypeStruct((t, d), F32),
                   jax.ShapeDtypeStruct((3, f, t), BF16),
                   jax.ShapeDtypeStruct((t, d), BF16),
                   jax.ShapeDtypeStruct((8, d), F32)],
        scratch_shapes=[pltpu.VMEM((d, tm), BF16), pltpu.VMEM((tm, d), F32)],
        compiler_params=_params(("arbitrary", "arbitrary")),
    )(dho, h, g, gu, wup_t, wdn, dep)


def _ffn_bwd_w(agu, first, count, rhs, dep, name):
    _, f, t = agu.shape
    d = rhs.shape[1]
    tm, tf = _tok_block(t, 2048), _feat_block(f)
    ni, nj = t // tm, f // tf

    def body(lhs_ref, rhs_ref, dep_ref, dw_ref, acc_ref):
        i = pl.program_id(1)
        @pl.when(i == 0)
        def _():
            acc_ref[...] = jnp.zeros_like(acc_ref)

        rb = rhs_ref[...]
        for k in range(count):
            acc_ref[k] += _dot(lhs_ref[k], rb)

        @pl.when(i == ni - 1)
        def _():
            dw_ref[...] = acc_ref[...].astype(BF16)

    return _pcall(
        body, name=name, grid=(nj, ni),
        in_specs=[pl.BlockSpec((count, tf, tm), lambda j, i: (first // count, j, i)),
                  pl.BlockSpec((tm, d), lambda j, i: (i, 0)),
                  pl.BlockSpec(memory_space=pl.ANY)],
        out_specs=pl.BlockSpec((count, tf, d), lambda j, i: (0, j, 0)),
        out_shape=jax.ShapeDtypeStruct((count, f, d), BF16),
        scratch_shapes=[pltpu.VMEM((count, tf, d), F32)],
        compiler_params=_params(("parallel", "arbitrary")),
    )(agu, rhs, dep)


N_HEADS = N_SWA_HEADS + N_MEM_HEADS


def _q_col(hd):
    return COL_Q + HEAD_DIM * hd if hd < N_SWA_HEADS else COL_QM + HEAD_DIM * (hd - N_SWA_HEADS)


def _mix_proj_fwd(h, g, win_t):
    t, d = h.shape
    tm = _tok_block(t)

    def body(h_ref, g_ref, win_ref, p_ref, n_ref, qh_ref):
        hh = h_ref[...]
        n = (hh * _rstd(hh) * g_ref[...]).astype(BF16)
        n_ref[...] = n
        proj = _dot_nt(n, win_ref[...])
        p_ref[...] = proj.astype(BF16)
        for hd in range(N_HEADS):
            c0 = _q_col(hd)
            qh_ref[hd] = (proj[:, c0:c0 + HEAD_DIM] * SCALE).astype(BF16)

    return _pcall(
        body, name="mix_proj_fwd", grid=(t // tm,),
        in_specs=[pl.BlockSpec((tm, d), lambda i: (i, 0)),
                  pl.BlockSpec((1, d), lambda i: (0, 0)),
                  pl.BlockSpec((D_IN, d), lambda i: (0, 0))],
        out_specs=[pl.BlockSpec((tm, D_IN), lambda i: (i, 0)),
                   pl.BlockSpec((tm, d), lambda i: (i, 0)),
                   pl.BlockSpec((N_HEADS, tm, HEAD_DIM), lambda i: (0, i, 0))],
        out_shape=[jax.ShapeDtypeStruct((t, D_IN), BF16), jax.ShapeDtypeStruct((t, d), BF16),
                   jax.ShapeDtypeStruct((N_HEADS, t, HEAD_DIM), BF16)],
        compiler_params=_params(("parallel",)),
    )(h, g, win_t)


def _memkv_fwd(mem, g, wkv, dep):
    m, d = mem.shape

    def body(mem_ref, g_ref, w_ref, dep_ref, mkv_ref, nt_ref):
        mm = mem_ref[...]
        n = mm * _rstd(mm) * g_ref[...]
        nt_ref[...] = n.T.astype(BF16)
        mkv_ref[...] = _dot(n.astype(BF16), w_ref[...]).astype(BF16)

    return _pcall(
        body, name="memkv_fwd", grid=(1,),
        in_specs=[pl.BlockSpec((m, d), lambda i: (0, 0)),
                  pl.BlockSpec((1, d), lambda i: (0, 0)),
                  pl.BlockSpec((d, 2 * D_MEMQ), lambda i: (0, 0)),
                  pl.BlockSpec(memory_space=pl.ANY)],
        out_specs=[pl.BlockSpec((m, 2 * D_MEMQ), lambda i: (0, 0)),
                   pl.BlockSpec((d, m), lambda i: (0, 0))],
        out_shape=[jax.ShapeDtypeStruct((m, 2 * D_MEMQ), BF16), jax.ShapeDtypeStruct((d, m), BF16)],
        compiler_params=_params(("arbitrary",)),
    )(mem, g, wkv, dep)


def _memkv_bwd(dmkv, mem, g, wkv, nt):
    m, d = mem.shape

    def body(dmkv_ref, mem_ref, g_ref, w_ref, nt_ref, dw_ref, dg_ref):
        db = dmkv_ref[...].astype(BF16)
        dw_ref[...] = _dot(nt_ref[...], db).astype(BF16)
        dn = _dot_nt(db, w_ref[...])
        mm = mem_ref[...]
        dg_ref[...] = _sum8(dn * (mm * _rstd(mm)))

    return _pcall(
        body, name="memkv_bwd", grid=(1,),
        in_specs=[pl.BlockSpec((m, 2 * D_MEMQ), lambda i: (0, 0)),
                  pl.BlockSpec((m, d), lambda i: (0, 0)),
                  pl.BlockSpec((1, d), lambda i: (0, 0)),
                  pl.BlockSpec((d, 2 * D_MEMQ), lambda i: (0, 0)),
                  pl.BlockSpec((d, m), lambda i: (0, 0))],
        out_specs=[pl.BlockSpec((d, 2 * D_MEMQ), lambda i: (0, 0)),
                   pl.BlockSpec((8, d), lambda i: (0, 0))],
        out_shape=[jax.ShapeDtypeStruct((d, 2 * D_MEMQ), BF16), jax.ShapeDtypeStruct((8, d), F32)],
        compiler_params=_params(("arbitrary",)),
    )(dmkv, mem, g, wkv, nt)


def _shift_rows(v, k, edge_rows, row):
    out = pltpu.roll(v, k, 0)
    for r in range(k):
        out = jnp.where(row == r, edge_rows[r], out)
    return out


def _shift_rows_up(v, k, edge_rows, row):
    n = v.shape[0]
    out = pltpu.roll(v, n - k, 0)
    for r in range(k):
        out = jnp.where(row == n - k + r, edge_rows[r], out)
    return out


GROUP_ROWS = SWA_GROUP * BLOCK
BIAS_CUR, BIAS_PREV, BIAS_NONE = 0, 1, 2


def _bias_table():
    tq = np.arange(BLOCK)[:, None]
    sk = np.arange(BLOCK)[None, :]
    slopes = np.asarray(SLOPES, np.float32)[:, None, None]
    cur = np.where(tq >= sk, -slopes * (tq - sk).astype(np.float32), NEG)
    prev = np.where(sk > tq, -slopes * (tq + BLOCK - sk).astype(np.float32), NEG)
    none = np.full_like(cur, NEG)
    tok = np.stack([cur, prev, none]).astype(np.float32).reshape(3, N_SWA_KV, GROUP_ROWS, BLOCK)
    return jnp.asarray(np.ascontiguousarray(tok.transpose(0, 1, 3, 2)))


def _head_cols(hd):
    return D_CONV + HEAD_DIM * hd


def _mix_core_fwd(p, qh, mkv, convw, sinks, bias_key):
    t = p.shape[0]
    m = mkv.shape[0]
    nb = t // BLOCK

    def body(sk_ref, pc_ref, pkv_ref, ppc_ref, ppu_ref, qh_ref, mkv_ref, cw_ref, bc_ref, bp_ref, y_ref, l_ref):
        i = pl.program_id(0)
        prevf = (i > 0).astype(F32)
        row = lax.broadcasted_iota(jnp.int32, (BLOCK, D_CONV), 0)

        bg = pc_ref[:, COL_BG:COL_BG + D_CONV].astype(F32)
        cg = pc_ref[:, COL_CG:COL_CG + D_CONV].astype(F32)
        u = pc_ref[:, COL_U:COL_U + D_CONV].astype(F32)
        vv = cg * u
        pvv = ppc_ref[...].astype(F32) * ppu_ref[...].astype(F32) * prevf
        vv1 = _shift_rows(vv, 1, [pvv[15:16]], row)
        vv2 = _shift_rows(vv, 2, [pvv[14:15], pvv[15:16]], row)
        w = cw_ref[...]
        y_ref[:, 0:D_CONV] = bg * (w[0:1] * vv2 + w[1:2] * vv1 + w[2:3] * vv)

        head_row = lax.broadcasted_iota(jnp.int32, (128, BLOCK), 0)
        lse_t = jnp.zeros((128, BLOCK), F32)
        for kv in range(N_SWA_KV):
            heads = range(kv * SWA_GROUP, (kv + 1) * SWA_GROUP)
            kc = pc_ref[:, COL_K + HEAD_DIM * kv:COL_K + HEAD_DIM * (kv + 1)]
            vc = pc_ref[:, COL_V + HEAD_DIM * kv:COL_V + HEAD_DIM * (kv + 1)]
            kp = pkv_ref[:, HEAD_DIM * kv:HEAD_DIM * (kv + 1)]
            vp = pkv_ref[:, D_KV + HEAD_DIM * kv:D_KV + HEAD_DIM * (kv + 1)]
            qg = qh_ref[kv * SWA_GROUP:(kv + 1) * SWA_GROUP].reshape(GROUP_ROWS, HEAD_DIM)
            sc = _dot_nt(kc, qg) + bc_ref[0, kv]
            sp = _dot_nt(kp, qg) + bp_ref[0, kv]
            sink = jnp.concatenate([jnp.full((1, BLOCK), sk_ref[0, hd], F32) for hd in heads], axis=1)
            mx = jnp.maximum(jnp.max(jnp.maximum(sc, sp), axis=0, keepdims=True), sink)
            ec = jnp.exp(sc - mx)
            ep = jnp.exp(sp - mx)
            den = jnp.sum(ec + ep, axis=0, keepdims=True) + jnp.exp(sink - mx)
            ot = (_dot_tn(vc, ec.astype(BF16)) + _dot_tn(vp, ep.astype(BF16))) / den
            lse = mx + jnp.log(den)
            for gi, hd in enumerate(heads):
                span = slice(gi * BLOCK, (gi + 1) * BLOCK)
                y_ref[:, _head_cols(hd):_head_cols(hd) + HEAD_DIM] = ot[:, span].T
                lse_t = jnp.where(head_row == hd, lse[:, span], lse_t)

        for hm in range(N_MEM_HEADS):
            hd = N_SWA_HEADS + hm
            mk = mkv_ref[:, HEAD_DIM * hm:HEAD_DIM * (hm + 1)]
            mv = mkv_ref[:, D_MEMQ + HEAD_DIM * hm:D_MEMQ + HEAD_DIM * (hm + 1)]
            s = _dot_nt(mk, qh_ref[hd])
            mx = jnp.max(s, axis=0, keepdims=True)
            e = jnp.exp(s - mx)
            den = jnp.sum(e, axis=0, keepdims=True)
            y_ref[:, _head_cols(hd):_head_cols(hd) + HEAD_DIM] = (_dot_tn(mv, e.astype(BF16)) / den).T
            lse_t = jnp.where(head_row == hd, mx + jnp.log(den), lse_t)
        l_ref[...] = lse_t.T

    kv_col = COL_K // (2 * D_KV)
    bias_block = (1, N_SWA_KV, BLOCK, GROUP_ROWS)
    return _pcall(
        body, name="mix_core_fwd", grid=(nb,),
        in_specs=[pl.BlockSpec(memory_space=pltpu.SMEM),
                  pl.BlockSpec((BLOCK, D_IN), lambda i: (i, 0)),
                  pl.BlockSpec((BLOCK, 2 * D_KV), lambda i: (jnp.maximum(i - 1, 0), kv_col)),
                  pl.BlockSpec((16, D_CONV), lambda i: (jnp.maximum(i * (BLOCK // 16) - 1, 0), COL_CG // D_CONV)),
                  pl.BlockSpec((16, D_CONV), lambda i: (jnp.maximum(i * (BLOCK // 16) - 1, 0), COL_U // D_CONV)),
                  pl.BlockSpec((N_HEADS, BLOCK, HEAD_DIM), lambda i: (0, i, 0)),
                  pl.BlockSpec((m, 2 * D_MEMQ), lambda i: (0, 0)),
                  pl.BlockSpec((3, D_CONV), lambda i: (0, 0)),
                  pl.BlockSpec(bias_block, lambda i: (BIAS_CUR, 0, 0, 0)),
                  pl.BlockSpec(bias_block, lambda i: (jnp.where(i == 0, BIAS_NONE, BIAS_PREV), 0, 0, 0))],
        out_specs=[pl.BlockSpec((BLOCK, D_MIX), lambda i: (i, 0)),
                   pl.BlockSpec((BLOCK, 128), lambda i: (i, 0))],
        out_shape=[jax.ShapeDtypeStruct((t, D_MIX), F32), jax.ShapeDtypeStruct((t, 128), F32)],
        compiler_params=_params(("parallel",)),
    )(sinks, p, p, p, p, qh, mkv, convw, bias_key, bias_key)


def _mix_core_bwd(p, qh, dyconv, doh, delta, lse, mkv, convw, sinks, bias_key):
    t = p.shape[0]
    m = mkv.shape[0]
    nb = t // BLOCK

    def body(sk_ref, pc_ref, pkv_ref, ppc_ref, ppu_ref, pnb_ref, dyc_ref, dyn_ref, qc_ref, qn_ref, doc_ref, don_ref,
             dlc_ref, dln_ref, lc_ref, ln_ref, mkv_ref, cw_ref, bp_ref, bct_ref, bnt_ref,
             dp_ref, dmkv_ref, dcw_ref, dsk_ref):
        i = pl.program_id(0)
        prevf = (i > 0).astype(F32)
        nextf = (i < nb - 1).astype(F32)
        row = lax.broadcasted_iota(jnp.int32, (BLOCK, D_CONV), 0)

        @pl.when(i == 0)
        def _():
            dmkv_ref[...] = jnp.zeros_like(dmkv_ref)
            dcw_ref[...] = jnp.zeros_like(dcw_ref)
            dsk_ref[...] = jnp.zeros_like(dsk_ref)

        bg = pc_ref[:, COL_BG:COL_BG + D_CONV].astype(F32)
        cg = pc_ref[:, COL_CG:COL_CG + D_CONV].astype(F32)
        u = pc_ref[:, COL_U:COL_U + D_CONV].astype(F32)
        vv = cg * u
        pvv = ppc_ref[...].astype(F32) * ppu_ref[...].astype(F32) * prevf
        vv1 = _shift_rows(vv, 1, [pvv[15:16]], row)
        vv2 = _shift_rows(vv, 2, [pvv[14:15], pvv[15:16]], row)
        w = cw_ref[...]
        yconv = w[0:1] * vv2 + w[1:2] * vv1 + w[2:3] * vv
        dyo = dyc_ref[...]
        dyc = dyo * bg
        nxt = dyn_ref[...] * pnb_ref[...].astype(F32) * nextf
        d1 = _shift_rows_up(dyc, 1, [nxt[0:1]], row)
        d2 = _shift_rows_up(dyc, 2, [nxt[0:1], nxt[1:2]], row)
        dvv = w[2:3] * dyc + w[1:2] * d1 + w[0:1] * d2
        dp_ref[:, COL_BG:COL_BG + D_CONV] = (dyo * yconv).astype(BF16)
        dp_ref[:, COL_CG:COL_CG + D_CONV] = (dvv * u).astype(BF16)
        dp_ref[:, COL_U:COL_U + D_CONV] = (dvv * cg).astype(BF16)
        dcw_ref[0:1, :] += jnp.sum(dyc * vv2, axis=0, keepdims=True)
        dcw_ref[1:2, :] += jnp.sum(dyc * vv1, axis=0, keepdims=True)
        dcw_ref[2:3, :] += jnp.sum(dyc * vv, axis=0, keepdims=True)

        lse_t, dl_t = lc_ref[...].T, dlc_ref[...].T
        lse_nt, dl_nt = ln_ref[...].T, dln_ref[...].T

        def stack_rows(tile_t, heads):
            return jnp.concatenate([tile_t[hd:hd + 1, :] for hd in heads], axis=1)

        lane8 = jnp.where(lax.broadcasted_iota(jnp.int32, (8, 128), 0) == 0,
                          lax.broadcasted_iota(jnp.int32, (8, 128), 1), -1)
        dsk = jnp.zeros((8, 128), F32)
        for kv in range(N_SWA_KV):
            heads = range(kv * SWA_GROUP, (kv + 1) * SWA_GROUP)
            kc = pc_ref[:, COL_K + HEAD_DIM * kv:COL_K + HEAD_DIM * (kv + 1)]
            vc = pc_ref[:, COL_V + HEAD_DIM * kv:COL_V + HEAD_DIM * (kv + 1)]
            kp = pkv_ref[:, HEAD_DIM * kv:HEAD_DIM * (kv + 1)]
            vp = pkv_ref[:, D_KV + HEAD_DIM * kv:D_KV + HEAD_DIM * (kv + 1)]
            qg = qc_ref[kv * SWA_GROUP:(kv + 1) * SWA_GROUP].reshape(GROUP_ROWS, HEAD_DIM)
            dog = doc_ref[kv * SWA_GROUP:(kv + 1) * SWA_GROUP].reshape(GROUP_ROWS, HEAD_DIM)
            qn = qn_ref[kv * SWA_GROUP:(kv + 1) * SWA_GROUP].reshape(GROUP_ROWS, HEAD_DIM)
            don = don_ref[kv * SWA_GROUP:(kv + 1) * SWA_GROUP].reshape(GROUP_ROWS, HEAD_DIM)
            lse_row, dl_row = stack_rows(lse_t, heads), stack_rows(dl_t, heads)
            ptp = jnp.exp(_dot_nt(kp, qg) + bp_ref[0, kv] - lse_row)
            dstp = (ptp * (_dot_nt(vp, dog) - dl_row)).astype(BF16)
            dq = _dot_tn(dstp, kp)
            pt = jnp.exp(_dot_nt(kc, qg) + bct_ref[0, kv] - lse_row)
            dst = (pt * (_dot_nt(vc, dog) - dl_row)).astype(BF16)
            dv = _dot(pt.astype(BF16), dog)
            dk = _dot(dst, qg)
            dq = dq + _dot_tn(dst, kc)
            ptn = jnp.exp(_dot_nt(kc, qn) + bnt_ref[0, kv] - stack_rows(lse_nt, heads))
            dstn = (ptn * (_dot_nt(vc, don) - stack_rows(dl_nt, heads))).astype(BF16)
            dv = dv + _dot(ptn.astype(BF16), don)
            dk = dk + _dot(dstn, qn)
            dp_ref[:, COL_K + HEAD_DIM * kv:COL_K + HEAD_DIM * (kv + 1)] = dk.astype(BF16)
            dp_ref[:, COL_V + HEAD_DIM * kv:COL_V + HEAD_DIM * (kv + 1)] = dv.astype(BF16)
            sink = jnp.concatenate([jnp.full((1, BLOCK), sk_ref[0, hd], F32) for hd in heads], axis=1)
            sink_term = jnp.exp(sink - lse_row) * dl_row
            for gi, hd in enumerate(heads):
                span = slice(gi * BLOCK, (gi + 1) * BLOCK)
                dp_ref[:, _q_col(hd):_q_col(hd) + HEAD_DIM] = (dq[span] * SCALE).astype(BF16)
                dsk = dsk + jnp.where(lane8 == hd, -jnp.sum(sink_term[:, span], axis=1, keepdims=True), 0.0)
        dsk_ref[...] += dsk

        for hm in range(N_MEM_HEADS):
            hd = N_SWA_HEADS + hm
            qm, dom = qc_ref[hd], doc_ref[hd]
            mk = mkv_ref[:, HEAD_DIM * hm:HEAD_DIM * (hm + 1)]
            mv = mkv_ref[:, D_MEMQ + HEAD_DIM * hm:D_MEMQ + HEAD_DIM * (hm + 1)]
            pt = jnp.exp(_dot_nt(mk, qm) - lse_t[hd:hd + 1, :])
            dst = (pt * (_dot_nt(mv, dom) - dl_t[hd:hd + 1, :])).astype(BF16)
            dp_ref[:, _q_col(hd):_q_col(hd) + HEAD_DIM] = (_dot_tn(dst, mk) * SCALE).astype(BF16)
            dmkv_ref[:, HEAD_DIM * hm:HEAD_DIM * (hm + 1)] += _dot(dst, qm)
            dmkv_ref[:, D_MEMQ + HEAD_DIM * hm:D_MEMQ + HEAD_DIM * (hm + 1)] += _dot(pt.astype(BF16), dom)

    cur = lambda i: (i, 0)
    const = lambda i: (0, 0)
    rows16 = BLOCK // 16
    last16 = t // 16 - 1
    before = lambda col: (lambda i: (jnp.maximum(i * rows16 - 1, 0), col))
    after = lambda i: (jnp.minimum((i + 1) * rows16, last16), 0)
    heads_cur = lambda i: (0, i, 0)
    heads_next = lambda i: (0, jnp.minimum(i + 1, nb - 1), 0)
    stat_next = lambda i: (jnp.minimum(i + 1, nb - 1), 0)
    key_block = (1, N_SWA_KV, BLOCK, GROUP_ROWS)
    head_block = (N_HEADS, BLOCK, HEAD_DIM)
    return _pcall(
        body, name="mix_core_bwd", grid=(nb,),
        in_specs=[pl.BlockSpec(memory_space=pltpu.SMEM),
                  pl.BlockSpec((BLOCK, D_IN), cur),
                  pl.BlockSpec((BLOCK, 2 * D_KV), lambda i: (jnp.maximum(i - 1, 0), COL_K // (2 * D_KV))),
                  pl.BlockSpec((16, D_CONV), before(COL_CG // D_CONV)),
                  pl.BlockSpec((16, D_CONV), before(COL_U // D_CONV)),
                  pl.BlockSpec((16, D_CONV), after),
                  pl.BlockSpec((BLOCK, D_CONV), cur),
                  pl.BlockSpec((16, D_CONV), after),
                  pl.BlockSpec(head_block, heads_cur), pl.BlockSpec(head_block, heads_next),
                  pl.BlockSpec(head_block, heads_cur), pl.BlockSpec(head_block, heads_next),
                  pl.BlockSpec((BLOCK, 128), cur), pl.BlockSpec((BLOCK, 128), stat_next),
                  pl.BlockSpec((BLOCK, 128), cur), pl.BlockSpec((BLOCK, 128), stat_next),
                  pl.BlockSpec((m, 2 * D_MEMQ), const),
                  pl.BlockSpec((3, D_CONV), const),
                  pl.BlockSpec(key_block, lambda i: (jnp.where(i == 0, BIAS_NONE, BIAS_PREV), 0, 0, 0)),
                  pl.BlockSpec(key_block, lambda i: (BIAS_CUR, 0, 0, 0)),
                  pl.BlockSpec(key_block, lambda i: (jnp.where(i == nb - 1, BIAS_NONE, BIAS_PREV), 0, 0, 0))],
        out_specs=[pl.BlockSpec((BLOCK, D_IN), cur),
                   pl.BlockSpec((m, 2 * D_MEMQ), const),
                   pl.BlockSpec((8, D_CONV), const),
                   pl.BlockSpec((8, 128), const)],
        out_shape=[jax.ShapeDtypeStruct((t, D_IN), BF16),
                   jax.ShapeDtypeStruct((m, 2 * D_MEMQ), F32),
                   jax.ShapeDtypeStruct((8, D_CONV), F32),
                   jax.ShapeDtypeStruct((8, 128), F32)],
        compiler_params=_params(("arbitrary",)),
    )(sinks, p, p, p, p, p, dyconv, dyconv, qh, qh, doh, doh, delta, delta, lse, lse, mkv, convw,
      bias_key, bias_key, bias_key)


def _group_norms(y):
    out = []
    for a, b in MIX_GROUPS:
        ys = y[:, a:b]
        r = _rstd(ys)
        out.append((ys * r, r))
    return out


def _mix_out_fwd(y, h, g, wout):
    t, d = h.shape
    tm = _tok_block(t)

    def body(y_ref, h_ref, g_ref, w_ref, ho_ref, mt_ref):
        yhat = jnp.concatenate([yh for yh, _ in _group_norms(y_ref[...])], axis=-1)
        mixed = yhat * g_ref[...]
        mt_ref[...] = mixed.T.astype(BF16)
        ho_ref[...] = h_ref[...] + _dot(mixed.astype(BF16), w_ref[...])

    return _pcall(
        body, name="mix_out_fwd", grid=(t // tm,),
        in_specs=[pl.BlockSpec((tm, D_MIX), lambda i: (i, 0)),
                  pl.BlockSpec((tm, d), lambda i: (i, 0)),
                  pl.BlockSpec((1, D_MIX), lambda i: (0, 0)),
                  pl.BlockSpec((D_MIX, d), lambda i: (0, 0))],
        out_specs=[pl.BlockSpec((tm, d), lambda i: (i, 0)),
                   pl.BlockSpec((D_MIX, tm), lambda i: (0, i))],
        out_shape=[jax.ShapeDtypeStruct((t, d), F32), jax.ShapeDtypeStruct((D_MIX, t), BF16)],
        compiler_params=_params(("parallel",)),
    )(y, h, g, wout)


def _head_indicator():
    ind = np.zeros((D_MIX, 128), np.float32)
    for hd in range(N_HEADS):
        ind[_head_cols(hd):_head_cols(hd) + HEAD_DIM, hd] = 1.0
    return jnp.asarray(ind, BF16)


def _mix_out_bwd(dho, y, g, wout, mt, dep):
    t, d = dho.shape
    tm = _tok_block(t)
    ni = t // tm

    def body(dho_ref, y_ref, g_ref, w_ref, mt_ref, ind_ref, dep_ref, dyc_ref, doh_ref, dl_ref, dw_ref, dg_ref, acc_ref):
        i = pl.program_id(0)
        dhb = dho_ref[...].astype(BF16)
        dm = _dot_nt(dhb, w_ref[...])
        pw = _dot(mt_ref[...], dhb)
        gg = g_ref[...]
        yy = y_ref[...]
        dys = []
        dgs = []
        for (a, b), (yhat, r) in zip(MIX_GROUPS, _group_norms(yy)):
            dmg = dm[:, a:b]
            dgs.append(_sum8(dmg * yhat))
            dyh = dmg * gg[:, a:b]
            dys.append(r * (dyh - yhat * jnp.mean(dyh * yhat, axis=-1, keepdims=True)))
        dy = jnp.concatenate(dys, axis=-1)
        dyc_ref[...] = dy[:, 0:D_CONV]
        for hd in range(N_HEADS):
            doh_ref[hd] = dy[:, _head_cols(hd):_head_cols(hd) + HEAD_DIM].astype(BF16)
        prod = dy * yy
        hi = prod.astype(BF16)
        lo = (prod - hi.astype(F32)).astype(BF16)
        dl_ref[...] = _dot(hi, ind_ref[...]) + _dot(lo, ind_ref[...])
        part = jnp.concatenate(dgs, axis=-1)

        @pl.when(i == 0)
        def _():
            acc_ref[...] = pw
            dg_ref[...] = part

        @pl.when(i > 0)
        def _():
            acc_ref[...] += pw
            dg_ref[...] += part

        @pl.when(i == ni - 1)
        def _():
            dw_ref[...] = acc_ref[...].astype(BF16)

    return _pcall(
        body, name="mix_out_bwd", grid=(ni,),
        in_specs=[pl.BlockSpec((tm, d), lambda i: (i, 0)),
                  pl.BlockSpec((tm, D_MIX), lambda i: (i, 0)),
                  pl.BlockSpec((1, D_MIX), lambda i: (0, 0)),
                  pl.BlockSpec((D_MIX, d), lambda i: (0, 0)),
                  pl.BlockSpec((D_MIX, tm), lambda i: (0, i)),
                  pl.BlockSpec((D_MIX, 128), lambda i: (0, 0)),
                  pl.BlockSpec(memory_space=pl.ANY)],
        out_specs=[pl.BlockSpec((tm, D_CONV), lambda i: (i, 0)),
                   pl.BlockSpec((N_HEADS, tm, HEAD_DIM), lambda i: (0, i, 0)),
                   pl.BlockSpec((tm, 128), lambda i: (i, 0)),
                   pl.BlockSpec((D_MIX, d), lambda i: (0, 0)),
                   pl.BlockSpec((8, D_MIX), lambda i: (0, 0))],
        out_shape=[jax.ShapeDtypeStruct((t, D_CONV), F32),
                   jax.ShapeDtypeStruct((N_HEADS, t, HEAD_DIM), BF16),
                   jax.ShapeDtypeStruct((t, 128), F32),
                   jax.ShapeDtypeStruct((D_MIX, d), BF16),
                   jax.ShapeDtypeStruct((8, D_MIX), F32)],
        scratch_shapes=[pltpu.VMEM((D_MIX, d), F32)],
        compiler_params=_params(("arbitrary",)),
    )(dho, y, g, wout, mt, _head_indicator(), dep)


def _mix_proj_bwd(dp, dho, h, g, win_t, n):
    t, d = h.shape
    tm = _tok_block(t)
    ni = t // tm

    def body(dp_ref, dho_ref, h_ref, g_ref, w_ref, n_ref, dh_ref, dw_ref, dg_ref, acc_ref):
        i = pl.program_id(0)
        dpb = dp_ref[...]
        dn = _dot(dpb, w_ref[...])

        @pl.when(i == 0)
        def _():
            acc_ref[...] = jnp.zeros_like(acc_ref)

        acc_ref[...] += _dot_tn(dpb, n_ref[...])
        hh = h_ref[...]
        r = _rstd(hh)
        xhat = hh * r
        dxh = dn * g_ref[...]
        dh_ref[...] = dho_ref[...] + r * (dxh - xhat * jnp.mean(dxh * xhat, axis=-1, keepdims=True))
        part = _sum8(dn * xhat)

        @pl.when(i == 0)
        def _():
            dg_ref[...] = part

        @pl.when(i > 0)
        def _():
            dg_ref[...] += part

        @pl.when(i == ni - 1)
        def _():
            dw_ref[...] = acc_ref[...].astype(BF16)

    return _pcall(
        body, name="mix_proj_bwd", grid=(ni,),
        in_specs=[pl.BlockSpec((tm, D_IN), lambda i: (i, 0)),
                  pl.BlockSpec((tm, d), lambda i: (i, 0)),
                  pl.BlockSpec((tm, d), lambda i: (i, 0)),
                  pl.BlockSpec((1, d), lambda i: (0, 0)),
                  pl.BlockSpec((D_IN, d), lambda i: (0, 0)),
                  pl.BlockSpec((tm, d), lambda i: (i, 0))],
        out_specs=[pl.BlockSpec((tm, d), lambda i: (i, 0)),
                   pl.BlockSpec((D_IN, d), lambda i: (0, 0)),
                   pl.BlockSpec((8, d), lambda i: (0, 0))],
        out_shape=[jax.ShapeDtypeStruct((t, d), F32),
                   jax.ShapeDtypeStruct((D_IN, d), BF16),
                   jax.ShapeDtypeStruct((8, d), F32)],
        scratch_shapes=[pltpu.VMEM((D_IN, d), F32)],
        compiler_params=_params(("arbitrary",)),
    )(dp, dho, h, g, win_t, n)


def _final_loss(h, g, tgt):
    t, d = h.shape
    tm = _tok_block(t)

    def body(h_ref, g_ref, t_ref, dh_ref, ls_ref, dg_ref):
        i = pl.program_id(0)
        hh = h_ref[...]
        r = _rstd(hh)
        xhat = hh * r
        gg = g_ref[...]
        err = xhat * gg - t_ref[...]
        dy = err * (1.0 / d)
        dxh = dy * gg
        dh_ref[...] = r * (dxh - xhat * jnp.mean(dxh * xhat, axis=-1, keepdims=True))
        lpart = _sum8(err * err)
        gpart = _sum8(dy * xhat)

        @pl.when(i == 0)
        def _():
            ls_ref[...] = lpart
            dg_ref[...] = gpart

        @pl.when(i > 0)
        def _():
            ls_ref[...] += lpart
            dg_ref[...] += gpart

    return _pcall(
        body, name="final_loss", grid=(t // tm,),
        in_specs=[pl.BlockSpec((tm, d), lambda i: (i, 0)),
                  pl.BlockSpec((1, d), lambda i: (0, 0)),
                  pl.BlockSpec((tm, d), lambda i: (i, 0))],
        out_specs=[pl.BlockSpec((tm, d), lambda i: (i, 0)),
                   pl.BlockSpec((8, d), lambda i: (0, 0)),
                   pl.BlockSpec((8, d), lambda i: (0, 0))],
        out_shape=[jax.ShapeDtypeStruct((t, d), F32),
                   jax.ShapeDtypeStruct((8, d), F32),
                   jax.ShapeDtypeStruct((8, d), F32)],
        compiler_params=_params(("arbitrary",)),
    )(h, g, tgt)


def _position():
    return lax.axis_index("x"), lax.axis_index("y"), lax.axis_index("c")


def _flip(v, bit):
    return 1 - v if bit else v


def _peer(k):
    x, y, c = _position()
    return _flip(x, k & 4), _flip(y, k & 2), _flip(c, k & 1)


def _slot(px, py, pc):
    return 4 * px + 2 * py + pc


def _handshake(peers):
    barrier = pltpu.get_barrier_semaphore()
    for peer in peers:
        pl.semaphore_signal(barrier, inc=1, device_id=peer, device_id_type=MESH)
    pl.semaphore_wait(barrier, len(peers))


def _sequencer_call(body, name, collective_id, out_type, scratch_types, operands):
    return pl.kernel(
        body, out_type=out_type, mesh=plsc.ScalarSubcoreMesh(axis_name="sequencer", num_cores=1), name=name,
        scratch_types=scratch_types, compiler_params=pltpu.CompilerParams(collective_id=collective_id),
    )(*operands)


def _all_gather(shards, name, collective_id):
    nt = len(shards)

    def body(*refs):
        xs = refs[:nt]
        outs = refs[nt:2 * nt]
        send_sems, recv_sems, local_sems = refs[2 * nt:]
        x, y, c = _position()
        me, sibling = (x, y, c), (x, y, 1 - c)
        xn, yn, dg = (1 - x, y), (x, 1 - y), (1 - x, 1 - y)
        pick = lambda a, b: (jnp.where(c == 0, a[0], b[0]), jnp.where(c == 0, a[1], b[1]))
        relay_from, relay_to = pick(yn, xn), pick(xn, yn)
        _handshake([sibling, (*xn, c), (*yn, c)])

        def copy(t, k, block, to, src=None):
            dst = outs[t].at[_slot(*block)]
            return pltpu.make_async_remote_copy(
                src_ref=dst if src is None else src, dst_ref=dst,
                send_sem=send_sems.at[t, k], recv_sem=recv_sems.at[t, k],
                device_id=to, device_id_type=MESH)

        mine = [pltpu.make_async_copy(xs[t], outs[t].at[_slot(*me)], local_sems.at[t]) for t in range(nt)]
        for cp in mine:
            cp.start()
        sent = []
        for t in range(nt):
            sent += [copy(t, 0, me, sibling, src=xs[t]), copy(t, 1, me, (*xn, c), src=xs[t]),
                     copy(t, 2, me, (*yn, c), src=xs[t])]
        for cp in sent:
            cp.start()
        for t in range(nt):
            copy(t, 1, (*xn, c), me).wait_recv()
            copy(t, 2, (*yn, c), me).wait_recv()
            passed = [copy(t, 3, (*relay_from, c), (*relay_to, c)),
                      copy(t, 4, (*xn, c), sibling), copy(t, 5, (*yn, c), sibling)]
            for cp in passed:
                cp.start()
            sent += passed
        for t in range(nt):
            copy(t, 3, (*dg, c), me).wait_recv()
            fwd = copy(t, 6, (*dg, c), sibling)
            fwd.start()
            sent.append(fwd)
        for t in range(nt):
            copy(t, 0, sibling, me).wait_recv()
            for k, chip in ((4, xn), (5, yn), (6, dg)):
                copy(t, k, (*chip, 1 - c), me).wait_recv()
        for cp in sent:
            cp.wait_send()
        for cp in mine:
            cp.wait()

    return _sequencer_call(
        body, name, collective_id,
        out_type=[jax.ShapeDtypeStruct((N_DEV,) + s.shape, s.dtype) for s in shards],
        scratch_types=[pltpu.SemaphoreType.DMA((nt, 7)), pltpu.SemaphoreType.DMA((nt, 7)),
                       pltpu.SemaphoreType.DMA((nt,))],
        operands=shards)


def _scatter_copy(srcs, lands, send_sems, recv_sems, t, k):
    peer = _peer(k)
    return pltpu.make_async_remote_copy(
        src_ref=srcs[t].at[_slot(*peer)], dst_ref=lands[t].at[k],
        send_sem=send_sems.at[t * (N_DEV - 1) + k - 1], recv_sem=recv_sems.at[t * (N_DEV - 1) + k - 1],
        device_id=peer, device_id_type=MESH)


def _scatter_start(partials, name):
    nt = len(partials)

    def body(*refs):
        srcs, lands = refs[:nt], refs[nt:2 * nt]
        send_sems, recv_sems = refs[2 * nt], refs[2 * nt + 1]
        token = refs[-1]
        for k in range(1, N_DEV):
            for t in range(nt):
                _scatter_copy(srcs, lands, send_sems, recv_sems, t, k).start()
        token[...] = jnp.zeros_like(token)

    hbm = pl.BlockSpec(memory_space=pltpu.HBM)
    sem = pl.BlockSpec(memory_space=pltpu.SEMAPHORE)
    shapes = [pltpu.HBM(p.shape, p.dtype) for p in partials]
    lands = [pltpu.with_memory_space_constraint(lax.empty(p.shape, p.dtype), pltpu.HBM) for p in partials]
    srcs = [pltpu.with_memory_space_constraint(p, pltpu.HBM) for p in partials]
    out = _pcall(
        body, name=name,
        out_shape=[pltpu.SemaphoreType.DMA((nt * (N_DEV - 1),))] * 2 + shapes + shapes
        + [jax.ShapeDtypeStruct((8, 128), F32)],
        in_specs=[hbm] * (2 * nt),
        out_specs=[sem, sem] + [hbm] * (2 * nt) + [pl.BlockSpec(memory_space=pltpu.VMEM)],
        input_output_aliases={i: 2 + i for i in range(2 * nt)},
        compiler_params=pltpu.CompilerParams(has_side_effects=pltpu.SideEffectType.DATAFLOW_SIDE_EFFECTING),
    )(*srcs, *lands)
    return (nt, name, out[:-1]), out[-1]


def _scatter_wait(state, after):
    nt, name, (send_sems, recv_sems, *thru) = state

    def body(*refs):
        srcs, lands = refs[:nt], refs[nt:2 * nt]
        send_sems, recv_sems = refs[2 * nt], refs[2 * nt + 1]
        for k in range(1, N_DEV):
            for t in range(nt):
                copy = _scatter_copy(srcs, lands, send_sems, recv_sems, t, k)
                copy.wait_send()
                copy.wait_recv()

    hbm = pl.BlockSpec(memory_space=pltpu.HBM)
    sem = pl.BlockSpec(memory_space=pltpu.SEMAPHORE)
    out = _pcall(
        body, name=name + "_wait",
        out_shape=[pltpu.HBM(a.shape, a.dtype) for a in thru],
        in_specs=[hbm] * (2 * nt) + [sem, sem, pl.BlockSpec(memory_space=pl.ANY)],
        out_specs=[hbm] * (2 * nt),
        input_output_aliases={i: i for i in range(2 * nt)},
        compiler_params=pltpu.CompilerParams(has_side_effects=pltpu.SideEffectType.DATAFLOW_SIDE_EFFECTING),
    )(*thru, send_sems, recv_sems, after)
    return out[:nt], out[nt:]


def _all_reduce_rows(v):
    nv, _, w = v.shape

    def body(v_ref, out_ref, mine_ref, gath_ref, send_sems, recv_sems):
        x, y, c = _position()
        me = _slot(x, y, c)
        mine_ref[...] = jnp.sum(v_ref[...], axis=1)

        def copy(k):
            return pltpu.make_async_remote_copy(
                src_ref=mine_ref, dst_ref=gath_ref.at[me],
                send_sem=send_sems.at[k - 1], recv_sem=recv_sems.at[k - 1],
                device_id=_peer(k), device_id_type=MESH)

        def arrival(k):
            return pltpu.make_async_remote_copy(
                src_ref=mine_ref, dst_ref=gath_ref.at[_slot(*_peer(k))],
                send_sem=send_sems.at[k - 1], recv_sem=recv_sems.at[k - 1],
                device_id=_peer(k), device_id_type=MESH)

        sent = [copy(k) for k in range(1, N_DEV)]
        for cp in sent:
            cp.start()
        gath_ref[me] = mine_ref[...]
        for k in range(1, N_DEV):
            arrival(k).wait_recv()
        for cp in sent:
            cp.wait_send()
        total = gath_ref[0]
        for s in range(1, N_DEV):
            total = total + gath_ref[s]
        out_ref[...] = total

    vmem = pl.BlockSpec(memory_space=pltpu.VMEM)
    return _pcall(
        body, name="all_reduce_rows",
        in_specs=[vmem], out_specs=vmem,
        out_shape=jax.ShapeDtypeStruct((nv, w), F32),
        scratch_shapes=[pltpu.VMEM((nv, w), F32), pltpu.VMEM((N_DEV, nv, w), F32),
                        pltpu.SemaphoreType.DMA((7,)), pltpu.SemaphoreType.DMA((7,))],
    )(v)


def _adamw_math(w, g, m, v):
    m2 = ADAM_B1 * m + (1.0 - ADAM_B1) * g
    v2 = ADAM_B2 * v + (1.0 - ADAM_B2) * (g * g)
    m_hat = m2 / (1.0 - ADAM_B1 ** ADAM_STEP)
    v_hat = v2 / (1.0 - ADAM_B2 ** ADAM_STEP)
    delta = -ADAM_LR * (m_hat / (jnp.sqrt(v_hat) + ADAM_EPS) + ADAM_WD * w)
    return delta, m2, v2


def _row_block(r):
    for cand in (256, 176, 128):
        if r % cand == 0:
            return cand
    return r


def _adamw_sharded(me, grads, w, m, v, dep):
    (own0, land0), (own1, land1) = grads
    _, r, c = land0.shape
    tr = _row_block(r)
    nr = r // tr

    def body(me_ref, o0_ref, l0_ref, o1_ref, l1_ref, w_ref, m_ref, v_ref, dep_ref, g_ref, d_ref, m2_ref, v2_ref):
        layer = pl.program_id(0)

        def total(own_ref, land_ref):
            acc = own_ref[0].astype(F32)
            for k in range(1, N_DEV):
                acc = acc + land_ref[k].astype(F32)
            return acc

        g = jnp.where(layer == 0, total(o0_ref, l0_ref), total(o1_ref, l1_ref))
        delta, m2, v2 = _adamw_math(w_ref[0], g, m_ref[0], v_ref[0])
        g_ref[0] = g
        d_ref[0] = delta
        m2_ref[0] = m2
        v2_ref[0] = v2

    rows0 = lambda l, i: jnp.where(l == 0, i, nr - 1)
    rows1 = lambda l, i: jnp.where(l == 1, i, 0)
    shard = pl.BlockSpec((1, tr, c), lambda l, i, me_ref: (l, i, 0))
    out = jax.ShapeDtypeStruct((2, r, c), F32)
    return _pcall(
        body, name="adamw_sharded",
        grid_spec=pltpu.PrefetchScalarGridSpec(
            num_scalar_prefetch=1, grid=(2, nr),
            in_specs=[pl.BlockSpec((1, tr, c), lambda l, i, me_ref: (me_ref[0], rows0(l, i), 0)),
                      pl.BlockSpec((N_DEV, tr, c), lambda l, i, me_ref: (0, rows0(l, i), 0)),
                      pl.BlockSpec((1, tr, c), lambda l, i, me_ref: (me_ref[0], rows1(l, i), 0)),
                      pl.BlockSpec((N_DEV, tr, c), lambda l, i, me_ref: (0, rows1(l, i), 0)),
                      shard, shard, shard, pl.BlockSpec(memory_space=pl.ANY)],
            out_specs=[shard, shard, shard, shard]),
        out_shape=[out, out, out, out],
        compiler_params=_params(("arbitrary", "arbitrary")),
    )(me, own0, land0, own1, land1, w, m, v, dep)


def _adamw_small(w, g, m, v):
    def body(w_ref, g_ref, m_ref, v_ref, d_ref, m2_ref, v2_ref):
        delta, m2, v2 = _adamw_math(w_ref[...], g_ref[...], m_ref[...], v_ref[...])
        d_ref[...] = delta
        m2_ref[...] = m2
        v2_ref[...] = v2

    spec = pl.BlockSpec(w.shape, lambda i: (0, 0))
    out = jax.ShapeDtypeStruct(w.shape, F32)
    return _pcall(
        body, name="adamw_small", grid=(1,),
        in_specs=[spec] * 4, out_specs=[spec] * 3, out_shape=[out] * 3,
        compiler_params=_params(("arbitrary",)),
    )(w, g, m, v)


def _pack(arrs):
    flat = jnp.concatenate([a.reshape(-1) for a in arrs])
    n = flat.shape[0]
    rows = -(-n // 1024) * 8
    return jnp.pad(flat, (0, rows * 128 - n)).reshape(rows, 128)


def _unpack(packed, like):
    flat = packed.reshape(-1)
    out, off = [], 0
    for a in like:
        out.append(flat[off:off + a.size].reshape(a.shape))
        off += a.size
    return out


def kernel(x, mem, g_ffn1, w_ffn1_up, w_ffn1_down, g_mix, w_in, conv_w, sinks, g_mem, w_mem_kv, g_grp, w_out, g_ffn2, w_ffn2_up, w_ffn2_down, g_final, loss_target, m_g_ffn1, m_w_ffn1_up, m_w_ffn1_down, m_g_mix, m_w_in, m_conv_w, m_sinks, m_g_mem, m_w_mem_kv, m_g_grp, m_w_out, m_g_ffn2, m_w_ffn2_up, m_w_ffn2_down, m_g_final, v_g_ffn1, v_w_ffn1_up, v_w_ffn1_down, v_g_mix, v_w_in, v_conv_w, v_sinks, v_g_mem, v_w_mem_kv, v_g_grp, v_w_out, v_g_ffn2, v_w_ffn2_up, v_w_ffn2_down, v_g_final):
    depth = g_ffn1.shape[0]
    t, d = x.shape[1], x.shape[2]
    width = max(d, D_MIX)
    me = _slot(*_position())
    conv_shard = conv_w.shape[2]

    xin, memin, tgt = x[0], mem[0], loss_target[0]

    conv_tile = jnp.zeros((depth * 8, 128), F32).at[:, :conv_shard].set(
        jnp.pad(conv_w, ((0, 0), (0, 8 - conv_w.shape[1]), (0, 0))).reshape(depth * 8, conv_shard))
    tr = lambda a: jnp.swapaxes(a, -1, -2)
    bf = lambda a: a.astype(BF16)
    weights = []
    collective_id = 0
    for l in range(depth):
        groups = [[bf(tr(w_ffn1_up[l])), bf(w_ffn1_down[l])] + ([conv_tile] if l == 0 else []),
                  [bf(tr(w_in[l])), bf(w_mem_kv[l]), bf(w_out[l])],
                  [bf(tr(w_ffn2_up[l])), bf(w_ffn2_down[l])]]
        full = []
        for gi, shards in enumerate(groups):
            full.append(_all_gather(shards, f"all_gather_l{l}_g{gi}", collective_id))
            collective_id += 1
        if l == 0:
            conv_full = full[0][2].reshape(N_DEV, depth, 8, 128)[:, :, :3, :conv_shard]
            conv_full = conv_full.transpose(1, 2, 0, 3).reshape(depth, 3, N_DEV * conv_shard)
        weights.append(dict(
            up1=full[0][0].reshape(2, -1, d), dn1=full[0][1].reshape(-1, d),
            win=full[1][0].reshape(D_IN, d), wkv=full[1][1].reshape(d, 2 * D_MEMQ), wout=full[1][2].reshape(D_MIX, d),
            up2=full[2][0].reshape(2, -1, d), dn2=full[2][1].reshape(-1, d)))

    row = lambda a: a.reshape(1, -1)
    bias_key = _bias_table()

    h = xin
    saved = []
    for l in range(depth):
        wl = weights[l]
        s = dict(h0=h)
        h, s["gu1"], s["n1"] = _ffn_fwd(h, row(g_ffn1[l]), wl["up1"], wl["dn1"])
        s["h1"] = h
        s["p"], s["n_mix"], s["qh"] = _mix_proj_fwd(h, row(g_mix[l]), wl["win"])
        s["mkv"], s["nt_mem"] = _memkv_fwd(memin, row(g_mem[l]), wl["wkv"], s["p"])
        s["y"], s["lse"] = _mix_core_fwd(s["p"], s["qh"], s["mkv"], conv_full[l], row(sinks[l]), bias_key)
        h, s["mt"] = _mix_out_fwd(s["y"], h, row(g_grp[l]), wl["wout"])
        s["h2"] = h
        h, s["gu2"], s["n2"] = _ffn_fwd(h, row(g_ffn2[l]), wl["up2"], wl["dn2"])
        saved.append(s)

    dh, loss_part, dg_final = _final_loss(h, row(g_final), tgt)

    small = {}
    dep = loss_part

    def reduce_small():
        def lanes(a):
            return jnp.pad(a, ((0, 0), (0, width - a.shape[1])))

        def first_row(a):
            return lanes(jnp.pad(a, ((0, 8 - a.shape[0]), (0, 0))))

        vec_names = ["g_ffn1", "g_mix", "g_mem", "g_grp", "g_ffn2", "sinks"]
        tiles = [lanes(small[n, l]) for n in vec_names for l in range(depth)]
        tiles += [first_row(small["conv_w", l][k:k + 1]) for l in range(depth) for k in range(3)]
        tiles.append(lanes(dg_final))
        n_real = len(tiles)
        tiles.append(lanes(loss_part))
        tiles += [jnp.zeros((8, width), F32)] * (-len(tiles) % 8)
        summed = _all_reduce_rows(jnp.stack(tiles))
        loss_all = 0.5 * jnp.sum(summed[n_real]) / d

        def vec(n, wd):
            return jnp.stack([summed[vec_names.index(n) * depth + l, :wd] for l in range(depth)])

        conv_base = len(vec_names) * depth
        conv_grad = jnp.stack([jnp.stack([summed[conv_base + 3 * l + k, :D_CONV] for k in range(3)])
                               for l in range(depth)])
        grads_small = {
            "g_ffn1": vec("g_ffn1", d), "g_mix": vec("g_mix", d), "g_mem": vec("g_mem", d),
            "g_grp": vec("g_grp", D_MIX), "g_ffn2": vec("g_ffn2", d), "sinks": vec("sinks", N_SWA_HEADS),
            "conv_w": lax.dynamic_slice_in_dim(conv_grad, me * conv_shard, conv_shard, axis=2),
            "g_final": summed[n_real - 1, :d],
        }
        small_w = [("g_ffn1", g_ffn1, m_g_ffn1, v_g_ffn1), ("g_mix", g_mix, m_g_mix, v_g_mix),
                   ("conv_w", conv_w, m_conv_w, v_conv_w), ("sinks", sinks, m_sinks, v_sinks),
                   ("g_mem", g_mem, m_g_mem, v_g_mem), ("g_grp", g_grp, m_g_grp, v_g_grp),
                   ("g_ffn2", g_ffn2, m_g_ffn2, v_g_ffn2), ("g_final", g_final, m_g_final, v_g_final)]
        like = [w for _, w, _, _ in small_w]
        packed = _adamw_small(_pack(like), _pack([grads_small[n] for n, _, _, _ in small_w]),
                              _pack([m for _, _, m, _ in small_w]), _pack([v for _, _, _, v in small_w]))
        updated = {n: (grads_small[n], dl, m2, v2)
                   for (n, _, _, _), dl, m2, v2 in zip(small_w, *[_unpack(pk, like) for pk in packed])}
        return loss_all, updated, packed[0]

    started = []

    def scatter(names, partials, label):
        state, token = _scatter_start(partials, f"scatter_grads_{label}")
        started.append((names, state))
        return token

    for l in reversed(range(depth)):
        wl, s = weights[l], saved[l]
        dh, agu, dyb, small["g_ffn2", l] = _ffn_bwd_act(dh, s["h2"], row(g_ffn2[l]), s["gu2"], wl["up2"], wl["dn2"], dep)
        ddn2 = _ffn_bwd_w(agu, 2, 1, dyb, agu, f"ffn_bwd_w_down_l{l}_ffn2").reshape(N_DEV, -1, d)
        dup2 = _ffn_bwd_w(agu, 0, 2, s["n2"], ddn2, f"ffn_bwd_w_up_l{l}_ffn2").reshape(N_DEV, -1, d)
        dep = scatter([("w_ffn2_up", l), ("w_ffn2_down", l)], [dup2, ddn2], f"l{l}_ffn2")
        dyconv, doh, delta, dwout, small["g_grp", l] = _mix_out_bwd(dh, s["y"], row(g_grp[l]), wl["wout"], s["mt"], dep)
        dp, dmkv, small["conv_w", l], small["sinks", l] = _mix_core_bwd(
            s["p"], s["qh"], dyconv, doh, delta, s["lse"], s["mkv"], conv_full[l], row(sinks[l]), bias_key)
        dwkv, small["g_mem", l] = _memkv_bwd(dmkv, memin, row(g_mem[l]), wl["wkv"], s["nt_mem"])
        dh, dwin, small["g_mix", l] = _mix_proj_bwd(dp, dh, s["h1"], row(g_mix[l]), wl["win"], s["n_mix"])
        dep = scatter([("w_in", l), ("w_mem_kv", l), ("w_out", l)],
                      [dwin.reshape(N_DEV, -1, d), dwkv.reshape(N_DEV, -1, 2 * D_MEMQ), dwout.reshape(N_DEV, -1, d)],
                      f"l{l}_mix")
        dh, agu, dyb, small["g_ffn1", l] = _ffn_bwd_act(dh, s["h0"], row(g_ffn1[l]), s["gu1"], wl["up1"], wl["dn1"], dep)
        order_after = agu
        if l == 0:
            loss, small_out, order_after = reduce_small()
        ddn1 = _ffn_bwd_w(agu, 2, 1, dyb, order_after, f"ffn_bwd_w_down_l{l}_ffn1").reshape(N_DEV, -1, d)
        if l > 0:
            dup1 = _ffn_bwd_w(agu, 0, 2, s["n1"], ddn1, f"ffn_bwd_w_up_l{l}_ffn1").reshape(N_DEV, -1, d)
            dep = scatter([("w_ffn1_up", l), ("w_ffn1_down", l)], [dup1, ddn1], f"l{l}_ffn1")
        else:
            dep = scatter([("w_ffn1_down", l)], [ddn1], f"l{l}_ffn1_down")
            dup1 = _ffn_bwd_w(agu, 0, 2, s["n1"], dep, f"ffn_bwd_w_up_l{l}_ffn1").reshape(N_DEV, -1, d)
            dep = scatter([("w_ffn1_up", l)], [dup1], f"l{l}_ffn1_up")
    grad_x = dh[None]

    big = {"w_ffn2_up": (w_ffn2_up, m_w_ffn2_up, v_w_ffn2_up, True), "w_ffn2_down": (w_ffn2_down, m_w_ffn2_down, v_w_ffn2_down, False),
           "w_in": (w_in, m_w_in, v_w_in, True), "w_mem_kv": (w_mem_kv, m_w_mem_kv, v_w_mem_kv, False),
           "w_out": (w_out, m_w_out, v_w_out, False), "w_ffn1_up": (w_ffn1_up, m_w_ffn1_up, v_w_ffn1_up, True),
           "w_ffn1_down": (w_ffn1_down, m_w_ffn1_down, v_w_ffn1_down, False)}
    me_index = jnp.reshape(me, (1,)).astype(jnp.int32)
    sharded, landed = {}, {}

    def finish(groups, after):
        for names, state in groups:
            owns, lands = _scatter_wait(state, after)
            for key, own, land in zip(names, owns, lands):
                landed[key] = (own, land)
            after = lands[0]
            for name in dict.fromkeys(n for n, _ in names):
                if name not in sharded and all((name, l) in landed for l in range(depth)):
                    w, m, v, transposed = big[name]
                    fix = tr if transposed else (lambda a: a)
                    res = _adamw_sharded(me_index, [landed[name, l] for l in range(depth)], fix(w), fix(m), fix(v), after)
                    sharded[name] = tuple(fix(r) for r in res)
                    after = res[0]
        return after

    finish(started[-2:], finish(started[:-2], dep))

    order = ["g_ffn1", "w_ffn1_up", "w_ffn1_down", "g_mix", "w_in", "conv_w", "sinks", "g_mem", "w_mem_kv", "g_grp",
             "w_out", "g_ffn2", "w_ffn2_up", "w_ffn2_down", "g_final"]
    results = {**sharded, **small_out}
    outs = [loss, grad_x]
    for part in range(4):
        outs += [results[n][part] for n in order]
    return tuple(outs)
```

```python
import numpy as np
import jax
import jax.numpy as jnp
from jax import lax
from jax.experimental import pallas as pl
from jax.experimental.pallas import tpu as pltpu
from jax.experimental.pallas import tpu_sc as plsc

F32 = jnp.float32
BF16 = jnp.bfloat16

N_DEV = 8
EPS = 1e-6
N_SWA_HEADS = 8
N_SWA_KV = 2
SWA_GROUP = N_SWA_HEADS // N_SWA_KV
HEAD_DIM = 64
N_MEM_HEADS = 4
D_CONV = 256
BLOCK = 128
D_SWA = N_SWA_HEADS * HEAD_DIM
D_KV = N_SWA_KV * HEAD_DIM
D_MEMQ = N_MEM_HEADS * HEAD_DIM
D_MIX = D_CONV + D_SWA + D_MEMQ
D_IN = 3 * D_CONV + D_SWA + 2 * D_KV + D_MEMQ
COL_BG, COL_CG, COL_U = 0, D_CONV, 2 * D_CONV
COL_Q = 3 * D_CONV
COL_K = COL_Q + D_SWA
COL_V = COL_K + D_KV
COL_QM = COL_V + D_KV
MIX_GROUPS = ((0, D_CONV), (D_CONV, D_CONV + D_SWA), (D_CONV + D_SWA, D_MIX))
SLOPES = tuple(2.0 ** (-8.0 * (i + 1) / N_SWA_HEADS) for i in range(N_SWA_HEADS))
SCALE = HEAD_DIM ** -0.5
NEG = -1e30

ADAM_LR = 0.001
ADAM_B1 = 0.9
ADAM_B2 = 0.999
ADAM_EPS = 1e-08
ADAM_WD = 0.01
ADAM_STEP = 10

V7X_VMEM_BYTES = 64 * 1024 * 1024
VMEM_LIMIT = (V7X_VMEM_BYTES * 3) // 4
MESH = pl.DeviceIdType.MESH


def _pcall(body, **kw):
    return pl.pallas_call(body, **kw)


def _params(sem=None, vmem=VMEM_LIMIT):
    return pltpu.CompilerParams(dimension_semantics=sem, vmem_limit_bytes=vmem)


def _dot(a, b):
    return lax.dot_general(a, b, (((1,), (0,)), ((), ())), preferred_element_type=F32)


def _dot_nt(a, b):
    return lax.dot_general(a, b, (((1,), (1,)), ((), ())), preferred_element_type=F32)


def _dot_tn(a, b):
    return lax.dot_general(a, b, (((0,), (0,)), ((), ())), preferred_element_type=F32)


def _rstd(x):
    return lax.rsqrt(jnp.mean(x * x, axis=-1, keepdims=True) + EPS)


def _sigmoid(x):
    return 1.0 / (1.0 + jnp.exp(-x))


def _sum8(x):
    r, w = x.shape
    return jnp.sum(x.reshape(r // 8, 8, w), axis=0)


def _tok_block(t, rows=512):
    return min(rows, t)


def _feat_block(f):
    return f // (N_DEV // 2)


def _ffn_fwd(h, g, wup_t, wdn):
    t, d = h.shape
    f = wdn.shape[0]
    tm, tf = _tok_block(t), _feat_block(f)
    ni, nj = t // tm, f // tf

    def body(h_ref, g_ref, wup_ref, wdn_ref, ho_ref, gu_ref, n_ref, nt_ref, acc_ref):
        j = pl.program_id(1)

        @pl.when(j == 0)
        def _():
            hh = h_ref[...]
            n = hh * _rstd(hh) * g_ref[...]
            n_ref[...] = n.astype(BF16)
            nt_ref[...] = n.T.astype(BF16)
            acc_ref[...] = jnp.zeros_like(acc_ref)

        nt = nt_ref[...]
        gate = _dot(wup_ref[0], nt)
        up = _dot(wup_ref[1], nt)
        gu_ref[0] = gate.astype(BF16)
        gu_ref[1] = up.astype(BF16)
        a = gate * _sigmoid(gate) * up
        acc_ref[...] += _dot_tn(a.astype(BF16), wdn_ref[...])

        @pl.when(j == nj - 1)
        def _():
            ho_ref[...] = h_ref[...] + 0.5 * acc_ref[...]

    return _pcall(
        body, name="ffn_fwd", grid=(ni, nj),
        in_specs=[pl.BlockSpec((tm, d), lambda i, j: (i, 0)),
                  pl.BlockSpec((1, d), lambda i, j: (0, 0)),
                  pl.BlockSpec((2, tf, d), lambda i, j: (0, j, 0)),
                  pl.BlockSpec((tf, d), lambda i, j: (j, 0))],
        out_specs=[pl.BlockSpec((tm, d), lambda i, j: (i, 0)),
                   pl.BlockSpec((2, tf, tm), lambda i, j: (0, j, i)),
                   pl.BlockSpec((tm, d), lambda i, j: (i, 0))],
        out_shape=[jax.ShapeDtypeStruct((t, d), F32),
                   jax.ShapeDtypeStruct((2, f, t), BF16),
                   jax.ShapeDtypeStruct((t, d), BF16)],
        scratch_shapes=[pltpu.VMEM((d, tm), BF16), pltpu.VMEM((tm, d), F32)],
        compiler_params=_params(("parallel", "arbitrary")),
    )(h, g, wup_t, wdn)


def _ffn_bwd_act(dho, h, g, gu, wup_t, wdn, dep):
    t, d = h.shape
    f = wdn.shape[0]
    tm, tf = _tok_block(t), _feat_block(f)
    ni, nj = t // tm, f // tf

    def body(dho_ref, h_ref, g_ref, gu_ref, wup_ref, wdn_ref, dep_ref, dh_ref, agu_ref, dyb_ref, dg_ref, dyt_ref, acc_ref):
        i = pl.program_id(0)
        j = pl.program_id(1)

        @pl.when(j == 0)
        def _():
            dy0 = 0.5 * dho_ref[...]
            dyb_ref[...] = dy0.astype(BF16)
            dyt_ref[...] = dy0.T.astype(BF16)
            acc_ref[...] = jnp.zeros_like(acc_ref)

        da = _dot(wdn_ref[...], dyt_ref[...]).astype(BF16)
        gate = gu_ref[0]
        up = gu_ref[1]
        sg = _sigmoid(gate)
        silu = gate * sg
        dgate = da * up * (sg * (1.0 + gate * (1.0 - sg)))
        dup = da * silu
        agu_ref[0] = dgate
        agu_ref[1] = dup
        agu_ref[2] = silu * up
        acc_ref[...] += _dot_tn(dgate, wup_ref[0])
        acc_ref[...] += _dot_tn(dup, wup_ref[1])

        @pl.when(j == nj - 1)
        def _():
            hh = h_ref[...]
            r = _rstd(hh)
            xhat = hh * r
            dnf = acc_ref[...]
            dxh = dnf * g_ref[...]
            dh_ref[...] = dho_ref[...] + r * (dxh - xhat * jnp.mean(dxh * xhat, axis=-1, keepdims=True))
            part = _sum8(dnf * xhat)

            @pl.when(i == 0)
            def _():
                dg_ref[...] = part

            @pl.when(i > 0)
            def _():
                dg_ref[...] += part

    return _pcall(
        body, name="ffn_bwd_act", grid=(ni, nj),
        in_specs=[pl.BlockSpec((tm, d), lambda i, j: (i, 0)),
                  pl.BlockSpec((tm, d), lambda i, j: (i, 0)),
                  pl.BlockSpec((1, d), lambda i, j: (0, 0)),
                  pl.BlockSpec((2, tf, tm), lambda i, j: (0, j, i)),
                  pl.BlockSpec((2, tf, d), lambda i, j: (0, j, 0)),
                  pl.BlockSpec((tf, d), lambda i, j: (j, 0)),
                  pl.BlockSpec(memory_space=pl.ANY)],
        out_specs=[pl.BlockSpec((tm, d), lambda i, j: (i, 0)),
                   pl.BlockSpec((3, tf, tm), lambda i, j: (0, j, i)),
                   pl.BlockSpec((tm, d), lambda i, j: (i, 0)),
                   pl.BlockSpec((8, d), lambda i, j: (0, 0))],
        out_shape=[jax.ShapeDtypeStruct((t, d), F32),
                   jax.ShapeDtypeStruct((3, f, t), BF16),
                   jax.ShapeDtypeStruct((t, d), BF16),
                   jax.ShapeDtypeStruct((8, d), F32)],
        scratch_shapes=[pltpu.VMEM((d, tm), BF16), pltpu.VMEM((tm, d), F32)],
        compiler_params=_params(("arbitrary", "arbitrary")),
    )(dho, h, g, gu, wup_t, wdn, dep)


def _ffn_bwd_w(agu, plane, rhs, dep, name, out_planes=1, slot=0):
    _, f, t = agu.shape
    d = rhs.shape[1]
    tm, tf = _tok_block(t, 4096), _feat_block(f)
    ni, nj = t // tm, f // tf

    def body(lhs_ref, rhs_ref, dep_ref, dw_ref, acc_ref):
        i = pl.program_id(1)

        @pl.when(i == 0)
        def _():
            acc_ref[...] = jnp.zeros_like(acc_ref)

        acc_ref[...] += _dot(lhs_ref[0], rhs_ref[...])

        @pl.when(i == ni - 1)
        def _():
            dw_ref[0] = acc_ref[...].astype(BF16)

    return _pcall(
        body, name=name, grid=(nj, ni),
        in_specs=[pl.BlockSpec((1, tf, tm), lambda j, i: (plane, j, i)),
                  pl.BlockSpec((tm, d), lambda j, i: (i, 0)),
                  pl.BlockSpec(memory_space=pl.ANY)],
        out_specs=pl.BlockSpec((1, tf, d), lambda j, i: (slot, j, 0)),
        out_shape=jax.ShapeDtypeStruct((out_planes, f, d), BF16),
        scratch_shapes=[pltpu.VMEM((tf, d), F32)],
        input_output_aliases={2: 0} if slot > 0 else {},
        compiler_params=_params(("parallel", "arbitrary")),
    )(agu, rhs, dep)


N_HEADS = N_SWA_HEADS + N_MEM_HEADS


def _q_col(hd):
    return COL_Q + HEAD_DIM * hd if hd < N_SWA_HEADS else COL_QM + HEAD_DIM * (hd - N_SWA_HEADS)


def _mix_proj_fwd(h, g, win_t):
    t, d = h.shape
    tm = _tok_block(t)

    def body(h_ref, g_ref, win_ref, p_ref, n_ref, qh_ref):
        hh = h_ref[...]
        n = (hh * _rstd(hh) * g_ref[...]).astype(BF16)
        n_ref[...] = n
        proj = _dot_nt(n, win_ref[...])
        p_ref[...] = proj.astype(BF16)
        for hd in range(N_HEADS):
            c0 = _q_col(hd)
            qh_ref[hd] = (proj[:, c0:c0 + HEAD_DIM] * SCALE).astype(BF16)

    return _pcall(
        body, name="mix_proj_fwd", grid=(t // tm,),
        in_specs=[pl.BlockSpec((tm, d), lambda i: (i, 0)),
                  pl.BlockSpec((1, d), lambda i: (0, 0)),
                  pl.BlockSpec((D_IN, d), lambda i: (0, 0))],
        out_specs=[pl.BlockSpec((tm, D_IN), lambda i: (i, 0)),
                   pl.BlockSpec((tm, d), lambda i: (i, 0)),
                   pl.BlockSpec((N_HEADS, tm, HEAD_DIM), lambda i: (0, i, 0))],
        out_shape=[jax.ShapeDtypeStruct((t, D_IN), BF16), jax.ShapeDtypeStruct((t, d), BF16),
                   jax.ShapeDtypeStruct((N_HEADS, t, HEAD_DIM), BF16)],
        compiler_params=_params(("parallel",)),
    )(h, g, win_t)


def _memkv_fwd(mem, g, wkv, dep):
    m, d = mem.shape

    def body(mem_ref, g_ref, w_ref, dep_ref, mkv_ref, nt_ref):
        mm = mem_ref[...]
        n = mm * _rstd(mm) * g_ref[...]
        nt_ref[...] = n.T.astype(BF16)
        mkv_ref[...] = _dot(n.astype(BF16), w_ref[...]).astype(BF16)

    return _pcall(
        body, name="memkv_fwd", grid=(1,),
        in_specs=[pl.BlockSpec((m, d), lambda i: (0, 0)),
                  pl.BlockSpec((1, d), lambda i: (0, 0)),
                  pl.BlockSpec((d, 2 * D_MEMQ), lambda i: (0, 0)),
                  pl.BlockSpec(memory_space=pl.ANY)],
        out_specs=[pl.BlockSpec((m, 2 * D_MEMQ), lambda i: (0, 0)),
                   pl.BlockSpec((d, m), lambda i: (0, 0))],
        out_shape=[jax.ShapeDtypeStruct((m, 2 * D_MEMQ), BF16), jax.ShapeDtypeStruct((d, m), BF16)],
        compiler_params=_params(("arbitrary",)),
    )(mem, g, wkv, dep)


def _memkv_bwd(dmkv, mem, g, wkv, nt):
    m, d = mem.shape

    def body(dmkv_ref, mem_ref, g_ref, w_ref, nt_ref, dw_ref, dg_ref):
        db = dmkv_ref[...].astype(BF16)
        dw_ref[...] = _dot(nt_ref[...], db).astype(BF16)
        dn = _dot_nt(db, w_ref[...])
        mm = mem_ref[...]
        dg_ref[...] = _sum8(dn * (mm * _rstd(mm)))

    return _pcall(
        body, name="memkv_bwd", grid=(1,),
        in_specs=[pl.BlockSpec((m, 2 * D_MEMQ), lambda i: (0, 0)),
                  pl.BlockSpec((m, d), lambda i: (0, 0)),
                  pl.BlockSpec((1, d), lambda i: (0, 0)),
                  pl.BlockSpec((d, 2 * D_MEMQ), lambda i: (0, 0)),
                  pl.BlockSpec((d, m), lambda i: (0, 0))],
        out_specs=[pl.BlockSpec((d, 2 * D_MEMQ), lambda i: (0, 0)),
                   pl.BlockSpec((8, d), lambda i: (0, 0))],
        out_shape=[jax.ShapeDtypeStruct((d, 2 * D_MEMQ), BF16), jax.ShapeDtypeStruct((8, d), F32)],
        compiler_params=_params(("arbitrary",)),
    )(dmkv, mem, g, wkv, nt)


def _shift_rows(v, k, edge_rows, row):
    out = pltpu.roll(v, k, 0)
    for r in range(k):
        out = jnp.where(row == r, edge_rows[r], out)
    return out


def _shift_rows_up(v, k, edge_rows, row):
    n = v.shape[0]
    out = pltpu.roll(v, n - k, 0)
    for r in range(k):
        out = jnp.where(row == n - k + r, edge_rows[r], out)
    return out


GROUP_ROWS = SWA_GROUP * BLOCK
BIAS_CUR, BIAS_PREV, BIAS_NONE = 0, 1, 2


def _bias_table():
    tq = np.arange(BLOCK)[:, None]
    sk = np.arange(BLOCK)[None, :]
    slopes = np.asarray(SLOPES, np.float32)[:, None, None]
    cur = np.where(tq >= sk, -slopes * (tq - sk).astype(np.float32), NEG)
    prev = np.where(sk > tq, -slopes * (tq + BLOCK - sk).astype(np.float32), NEG)
    none = np.full_like(cur, NEG)
    tok = np.stack([cur, prev, none]).astype(np.float32).reshape(3, N_SWA_KV, GROUP_ROWS, BLOCK)
    return jnp.asarray(np.ascontiguousarray(tok.transpose(0, 1, 3, 2)))


def _head_cols(hd):
    return D_CONV + HEAD_DIM * hd


def _mix_core_fwd(p, qh, mkv, convw, sinks, bias_key):
    t = p.shape[0]
    m = mkv.shape[0]
    nb = t // BLOCK

    def body(sk_ref, pc_ref, pkv_ref, ppc_ref, ppu_ref, qh_ref, mkv_ref, cw_ref, bc_ref, bp_ref, y_ref, l_ref):
        i = pl.program_id(0)
        prevf = (i > 0).astype(F32)
        row = lax.broadcasted_iota(jnp.int32, (BLOCK, D_CONV), 0)

        bg = pc_ref[:, COL_BG:COL_BG + D_CONV].astype(F32)
        cg = pc_ref[:, COL_CG:COL_CG + D_CONV].astype(F32)
        u = pc_ref[:, COL_U:COL_U + D_CONV].astype(F32)
        vv = cg * u
        pvv = ppc_ref[...].astype(F32) * ppu_ref[...].astype(F32) * prevf
        vv1 = _shift_rows(vv, 1, [pvv[15:16]], row)
        vv2 = _shift_rows(vv, 2, [pvv[14:15], pvv[15:16]], row)
        w = cw_ref[...]
        y_ref[:, 0:D_CONV] = bg * (w[0:1] * vv2 + w[1:2] * vv1 + w[2:3] * vv)

        head_row = lax.broadcasted_iota(jnp.int32, (128, BLOCK), 0)
        lse_t = jnp.zeros((128, BLOCK), F32)
        for kv in range(N_SWA_KV):
            heads = range(kv * SWA_GROUP, (kv + 1) * SWA_GROUP)
            kc = pc_ref[:, COL_K + HEAD_DIM * kv:COL_K + HEAD_DIM * (kv + 1)]
            vc = pc_ref[:, COL_V + HEAD_DIM * kv:COL_V + HEAD_DIM * (kv + 1)]
            kp = pkv_ref[:, HEAD_DIM * kv:HEAD_DIM * (kv + 1)]
            vp = pkv_ref[:, D_KV + HEAD_DIM * kv:D_KV + HEAD_DIM * (kv + 1)]
            qg = qh_ref[kv * SWA_GROUP:(kv + 1) * SWA_GROUP].reshape(GROUP_ROWS, HEAD_DIM)
            sc = _dot_nt(kc, qg) + bc_ref[0, kv]
            sp = _dot_nt(kp, qg) + bp_ref[0, kv]
            sink = jnp.concatenate([jnp.full((1, BLOCK), sk_ref[0, hd], F32) for hd in heads], axis=1)
            mx = jnp.maximum(jnp.max(jnp.maximum(sc, sp), axis=0, keepdims=True), sink)
            ec = jnp.exp(sc - mx)
            ep = jnp.exp(sp - mx)
            den = jnp.sum(ec + ep, axis=0, keepdims=True) + jnp.exp(sink - mx)
            ot = (_dot_tn(vc, ec.astype(BF16)) + _dot_tn(vp, ep.astype(BF16))) / den
            lse = mx + jnp.log(den)
            for gi, hd in enumerate(heads):
                span = slice(gi * BLOCK, (gi + 1) * BLOCK)
                y_ref[:, _head_cols(hd):_head_cols(hd) + HEAD_DIM] = ot[:, span].T
                lse_t = jnp.where(head_row == hd, lse[:, span], lse_t)

        for hm in range(N_MEM_HEADS):
            hd = N_SWA_HEADS + hm
            mk = mkv_ref[:, HEAD_DIM * hm:HEAD_DIM * (hm + 1)]
            mv = mkv_ref[:, D_MEMQ + HEAD_DIM * hm:D_MEMQ + HEAD_DIM * (hm + 1)]
            s = _dot_nt(mk, qh_ref[hd])
            mx = jnp.max(s, axis=0, keepdims=True)
            e = jnp.exp(s - mx)
            den = jnp.sum(e, axis=0, keepdims=True)
            y_ref[:, _head_cols(hd):_head_cols(hd) + HEAD_DIM] = (_dot_tn(mv, e.astype(BF16)) / den).T
            lse_t = jnp.where(head_row == hd, mx + jnp.log(den), lse_t)
        l_ref[...] = lse_t.T

    kv_col = COL_K // (2 * D_KV)
    bias_block = (1, N_SWA_KV, BLOCK, GROUP_ROWS)
    return _pcall(
        body, name="mix_core_fwd", grid=(nb,),
        in_specs=[pl.BlockSpec(memory_space=pltpu.SMEM),
                  pl.BlockSpec((BLOCK, D_IN), lambda i: (i, 0)),
                  pl.BlockSpec((BLOCK, 2 * D_KV), lambda i: (jnp.maximum(i - 1, 0), kv_col)),
                  pl.BlockSpec((16, D_CONV), lambda i: (jnp.maximum(i * (BLOCK // 16) - 1, 0), COL_CG // D_CONV)),
                  pl.BlockSpec((16, D_CONV), lambda i: (jnp.maximum(i * (BLOCK // 16) - 1, 0), COL_U // D_CONV)),
                  pl.BlockSpec((N_HEADS, BLOCK, HEAD_DIM), lambda i: (0, i, 0)),
                  pl.BlockSpec((m, 2 * D_MEMQ), lambda i: (0, 0)),
                  pl.BlockSpec((3, D_CONV), lambda i: (0, 0)),
                  pl.BlockSpec(bias_block, lambda i: (BIAS_CUR, 0, 0, 0)),
                  pl.BlockSpec(bias_block, lambda i: (jnp.where(i == 0, BIAS_NONE, BIAS_PREV), 0, 0, 0))],
        out_specs=[pl.BlockSpec((BLOCK, D_MIX), lambda i: (i, 0)),
                   pl.BlockSpec((BLOCK, 128), lambda i: (i, 0))],
        out_shape=[jax.ShapeDtypeStruct((t, D_MIX), F32), jax.ShapeDtypeStruct((t, 128), F32)],
        compiler_params=_params(("parallel",)),
    )(sinks, p, p, p, p, qh, mkv, convw, bias_key, bias_key)


def _mix_core_bwd(p, qh, dyconv, doh, delta, lse, mkv, convw, sinks, bias_key):
    t = p.shape[0]
    m = mkv.shape[0]
    nb = t // BLOCK

    def body(sk_ref, pc_ref, pkv_ref, ppc_ref, ppu_ref, pnb_ref, dyc_ref, dyn_ref, qc_ref, qn_ref, doc_ref, don_ref,
             dlc_ref, dln_ref, lc_ref, ln_ref, mkv_ref, cw_ref, bp_ref, bct_ref, bnt_ref,
             dp_ref, dmkv_ref, dcw_ref, dsk_ref):
        i = pl.program_id(0)
        prevf = (i > 0).astype(F32)
        nextf = (i < nb - 1).astype(F32)
        row = lax.broadcasted_iota(jnp.int32, (BLOCK, D_CONV), 0)

        @pl.when(i == 0)
        def _():
            dmkv_ref[...] = jnp.zeros_like(dmkv_ref)
            dcw_ref[...] = jnp.zeros_like(dcw_ref)
            dsk_ref[...] = jnp.zeros_like(dsk_ref)

        bg = pc_ref[:, COL_BG:COL_BG + D_CONV].astype(F32)
        cg = pc_ref[:, COL_CG:COL_CG + D_CONV].astype(F32)
        u = pc_ref[:, COL_U:COL_U + D_CONV].astype(F32)
        vv = cg * u
        pvv = ppc_ref[...].astype(F32) * ppu_ref[...].astype(F32) * prevf
        vv1 = _shift_rows(vv, 1, [pvv[15:16]], row)
        vv2 = _shift_rows(vv, 2, [pvv[14:15], pvv[15:16]], row)
        w = cw_ref[...]
        yconv = w[0:1] * vv2 + w[1:2] * vv1 + w[2:3] * vv
        dyo = dyc_ref[...]
        dyc = dyo * bg
        nxt = dyn_ref[...] * pnb_ref[...].astype(F32) * nextf
        d1 = _shift_rows_up(dyc, 1, [nxt[0:1]], row)
        d2 = _shift_rows_up(dyc, 2, [nxt[0:1], nxt[1:2]], row)
        dvv = w[2:3] * dyc + w[1:2] * d1 + w[0:1] * d2
        dp_ref[:, COL_BG:COL_BG + D_CONV] = (dyo * yconv).astype(BF16)
        dp_ref[:, COL_CG:COL_CG + D_CONV] = (dvv * u).astype(BF16)
        dp_ref[:, COL_U:COL_U + D_CONV] = (dvv * cg).astype(BF16)
        dcw_ref[0:1, :] += jnp.sum(dyc * vv2, axis=0, keepdims=True)
        dcw_ref[1:2, :] += jnp.sum(dyc * vv1, axis=0, keepdims=True)
        dcw_ref[2:3, :] += jnp.sum(dyc * vv, axis=0, keepdims=True)

        lse_t, dl_t = lc_ref[...].T, dlc_ref[...].T
        lse_nt, dl_nt = ln_ref[...].T, dln_ref[...].T

        def stack_rows(tile_t, heads):
            return jnp.concatenate([tile_t[hd:hd + 1, :] for hd in heads], axis=1)

        lane8 = jnp.where(lax.broadcasted_iota(jnp.int32, (8, 128), 0) == 0,
                          lax.broadcasted_iota(jnp.int32, (8, 128), 1), -1)
        dsk = jnp.zeros((8, 128), F32)
        for kv in range(N_SWA_KV):
            heads = range(kv * SWA_GROUP, (kv + 1) * SWA_GROUP)
            kc = pc_ref[:, COL_K + HEAD_DIM * kv:COL_K + HEAD_DIM * (kv + 1)]
            vc = pc_ref[:, COL_V + HEAD_DIM * kv:COL_V + HEAD_DIM * (kv + 1)]
            kp = pkv_ref[:, HEAD_DIM * kv:HEAD_DIM * (kv + 1)]
            vp = pkv_ref[:, D_KV + HEAD_DIM * kv:D_KV + HEAD_DIM * (kv + 1)]
            qg = qc_ref[kv * SWA_GROUP:(kv + 1) * SWA_GROUP].reshape(GROUP_ROWS, HEAD_DIM)
            dog = doc_ref[kv * SWA_GROUP:(kv + 1) * SWA_GROUP].reshape(GROUP_ROWS, HEAD_DIM)
            qn = qn_ref[kv * SWA_GROUP:(kv + 1) * SWA_GROUP].reshape(GROUP_ROWS, HEAD_DIM)
            don = don_ref[kv * SWA_GROUP:(kv + 1) * SWA_GROUP].reshape(GROUP_ROWS, HEAD_DIM)
            lse_row, dl_row = stack_rows(lse_t, heads), stack_rows(dl_t, heads)
            ptp = jnp.exp(_dot_nt(kp, qg) + bp_ref[0, kv] - lse_row)
            dstp = (ptp * (_dot_nt(vp, dog) - dl_row)).astype(BF16)
            dq = _dot_tn(dstp, kp)
            pt = jnp.exp(_dot_nt(kc, qg) + bct_ref[0, kv] - lse_row)
            dst = (pt * (_dot_nt(vc, dog) - dl_row)).astype(BF16)
            dv = _dot(pt.astype(BF16), dog)
            dk = _dot(dst, qg)
            dq = dq + _dot_tn(dst, kc)
            ptn = jnp.exp(_dot_nt(kc, qn) + bnt_ref[0, kv] - stack_rows(lse_nt, heads))
            dstn = (ptn * (_dot_nt(vc, don) - stack_rows(dl_nt, heads))).astype(BF16)
            dv = dv + _dot(ptn.astype(BF16), don)
            dk = dk + _dot(dstn, qn)
            dp_ref[:, COL_K + HEAD_DIM * kv:COL_K + HEAD_DIM * (kv + 1)] = dk.astype(BF16)
            dp_ref[:, COL_V + HEAD_DIM * kv:COL_V + HEAD_DIM * (kv + 1)] = dv.astype(BF16)
            sink = jnp.concatenate([jnp.full((1, BLOCK), sk_ref[0, hd], F32) for hd in heads], axis=1)
            sink_term = jnp.exp(sink - lse_row) * dl_row
            for gi, hd in enumerate(heads):
                span = slice(gi * BLOCK, (gi + 1) * BLOCK)
                dp_ref[:, _q_col(hd):_q_col(hd) + HEAD_DIM] = (dq[span] * SCALE).astype(BF16)
                dsk = dsk + jnp.where(lane8 == hd, -jnp.sum(sink_term[:, span], axis=1, keepdims=True), 0.0)
        dsk_ref[...] += dsk

        for hm in range(N_MEM_HEADS):
            hd = N_SWA_HEADS + hm
            qm, dom = qc_ref[hd], doc_ref[hd]
            mk = mkv_ref[:, HEAD_DIM * hm:HEAD_DIM * (hm + 1)]
            mv = mkv_ref[:, D_MEMQ + HEAD_DIM * hm:D_MEMQ + HEAD_DIM * (hm + 1)]
            pt = jnp.exp(_dot_nt(mk, qm) - lse_t[hd:hd + 1, :])
            dst = (pt * (_dot_nt(mv, dom) - dl_t[hd:hd + 1, :])).astype(BF16)
            dp_ref[:, _q_col(hd):_q_col(hd) + HEAD_DIM] = (_dot_tn(dst, mk) * SCALE).astype(BF16)
            dmkv_ref[:, HEAD_DIM * hm:HEAD_DIM * (hm + 1)] += _dot(dst, qm)
            dmkv_ref[:, D_MEMQ + HEAD_DIM * hm:D_MEMQ + HEAD_DIM * (hm + 1)] += _dot(pt.astype(BF16), dom)

    cur = lambda i: (i, 0)
    const = lambda i: (0, 0)
    rows16 = BLOCK // 16
    last16 = t // 16 - 1
    before = lambda col: (lambda i: (jnp.maximum(i * rows16 - 1, 0), col))
    after = lambda i: (jnp.minimum((i + 1) * rows16, last16), 0)
    heads_cur = lambda i: (0, i, 0)
    heads_next = lambda i: (0, jnp.minimum(i + 1, nb - 1), 0)
    stat_next = lambda i: (jnp.minimum(i + 1, nb - 1), 0)
    key_block = (1, N_SWA_KV, BLOCK, GROUP_ROWS)
    head_block = (N_HEADS, BLOCK, HEAD_DIM)
    return _pcall(
        body, name="mix_core_bwd", grid=(nb,),
        in_specs=[pl.BlockSpec(memory_space=pltpu.SMEM),
                  pl.BlockSpec((BLOCK, D_IN), cur),
                  pl.BlockSpec((BLOCK, 2 * D_KV), lambda i: (jnp.maximum(i - 1, 0), COL_K // (2 * D_KV))),
                  pl.BlockSpec((16, D_CONV), before(COL_CG // D_CONV)),
                  pl.BlockSpec((16, D_CONV), before(COL_U // D_CONV)),
                  pl.BlockSpec((16, D_CONV), after),
                  pl.BlockSpec((BLOCK, D_CONV), cur),
                  pl.BlockSpec((16, D_CONV), after),
                  pl.BlockSpec(head_block, heads_cur), pl.BlockSpec(head_block, heads_next),
                  pl.BlockSpec(head_block, heads_cur), pl.BlockSpec(head_block, heads_next),
                  pl.BlockSpec((BLOCK, 128), cur), pl.BlockSpec((BLOCK, 128), stat_next),
                  pl.BlockSpec((BLOCK, 128), cur), pl.BlockSpec((BLOCK, 128), stat_next),
                  pl.BlockSpec((m, 2 * D_MEMQ), const),
                  pl.BlockSpec((3, D_CONV), const),
                  pl.BlockSpec(key_block, lambda i: (jnp.where(i == 0, BIAS_NONE, BIAS_PREV), 0, 0, 0)),
                  pl.BlockSpec(key_block, lambda i: (BIAS_CUR, 0, 0, 0)),
                  pl.BlockSpec(key_block, lambda i: (jnp.where(i == nb - 1, BIAS_NONE, BIAS_PREV), 0, 0, 0))],
        out_specs=[pl.BlockSpec((BLOCK, D_IN), cur),
                   pl.BlockSpec((m, 2 * D_MEMQ), const),
                   pl.BlockSpec((8, D_CONV), const),
                   pl.BlockSpec((8, 128), const)],
        out_shape=[jax.ShapeDtypeStruct((t, D_IN), BF16),
                   jax.ShapeDtypeStruct((m, 2 * D_MEMQ), F32),
                   jax.ShapeDtypeStruct((8, D_CONV), F32),
                   jax.ShapeDtypeStruct((8, 128), F32)],
        compiler_params=_params(("arbitrary",)),
    )(sinks, p, p, p, p, p, dyconv, dyconv, qh, qh, doh, doh, delta, delta, lse, lse, mkv, convw,
      bias_key, bias_key, bias_key)


def _group_norms(y):
    out = []
    for a, b in MIX_GROUPS:
        ys = y[:, a:b]
        r = _rstd(ys)
        out.append((ys * r, r))
    return out


def _mix_out_fwd(y, h, g, wout):
    t, d = h.shape
    tm = _tok_block(t)

    def body(y_ref, h_ref, g_ref, w_ref, ho_ref, mt_ref):
        yhat = jnp.concatenate([yh for yh, _ in _group_norms(y_ref[...])], axis=-1)
        mixed = yhat * g_ref[...]
        mt_ref[...] = mixed.T.astype(BF16)
        ho_ref[...] = h_ref[...] + _dot(mixed.astype(BF16), w_ref[...])

    return _pcall(
        body, name="mix_out_fwd", grid=(t // tm,),
        in_specs=[pl.BlockSpec((tm, D_MIX), lambda i: (i, 0)),
                  pl.BlockSpec((tm, d), lambda i: (i, 0)),
                  pl.BlockSpec((1, D_MIX), lambda i: (0, 0)),
                  pl.BlockSpec((D_MIX, d), lambda i: (0, 0))],
        out_specs=[pl.BlockSpec((tm, d), lambda i: (i, 0)),
                   pl.BlockSpec((D_MIX, tm), lambda i: (0, i))],
        out_shape=[jax.ShapeDtypeStruct((t, d), F32), jax.ShapeDtypeStruct((D_MIX, t), BF16)],
        compiler_params=_params(("parallel",)),
    )(y, h, g, wout)


def _head_indicator():
    ind = np.zeros((D_MIX, 128), np.float32)
    for hd in range(N_HEADS):
        ind[_head_cols(hd):_head_cols(hd) + HEAD_DIM, hd] = 1.0
    return jnp.asarray(ind, BF16)


def _mix_out_bwd(dho, y, g, wout, mt, dep):
    t, d = dho.shape
    tm = _tok_block(t)
    ni = t // tm

    def body(dho_ref, y_ref, g_ref, w_ref, mt_ref, ind_ref, dep_ref, dyc_ref, doh_ref, dl_ref, dw_ref, dg_ref, acc_ref):
        i = pl.program_id(0)
        dhb = dho_ref[...].astype(BF16)
        dm = _dot_nt(dhb, w_ref[...])
        pw = _dot(mt_ref[...], dhb)
        gg = g_ref[...]
        yy = y_ref[...]
        dys = []
        dgs = []
        for (a, b), (yhat, r) in zip(MIX_GROUPS, _group_norms(yy)):
            dmg = dm[:, a:b]
            dgs.append(_sum8(dmg * yhat))
            dyh = dmg * gg[:, a:b]
            dys.append(r * (dyh - yhat * jnp.mean(dyh * yhat, axis=-1, keepdims=True)))
        dy = jnp.concatenate(dys, axis=-1)
        dyc_ref[...] = dy[:, 0:D_CONV]
        for hd in range(N_HEADS):
            doh_ref[hd] = dy[:, _head_cols(hd):_head_cols(hd) + HEAD_DIM].astype(BF16)
        prod = dy * yy
        hi = prod.astype(BF16)
        lo = (prod - hi.astype(F32)).astype(BF16)
        dl_ref[...] = _dot(hi, ind_ref[...]) + _dot(lo, ind_ref[...])
        part = jnp.concatenate(dgs, axis=-1)

        @pl.when(i == 0)
        def _():
            acc_ref[...] = pw
            dg_ref[...] = part

        @pl.when(i > 0)
        def _():
            acc_ref[...] += pw
            dg_ref[...] += part

        @pl.when(i == ni - 1)
        def _():
            dw_ref[...] = acc_ref[...].astype(BF16)

    return _pcall(
        body, name="mix_out_bwd", grid=(ni,),
        in_specs=[pl.BlockSpec((tm, d), lambda i: (i, 0)),
                  pl.BlockSpec((tm, D_MIX), lambda i: (i, 0)),
                  pl.BlockSpec((1, D_MIX), lambda i: (0, 0)),
                  pl.BlockSpec((D_MIX, d), lambda i: (0, 0)),
                  pl.BlockSpec((D_MIX, tm), lambda i: (0, i)),
                  pl.BlockSpec((D_MIX, 128), lambda i: (0, 0)),
                  pl.BlockSpec(memory_space=pl.ANY)],
        out_specs=[pl.BlockSpec((tm, D_CONV), lambda i: (i, 0)),
                   pl.BlockSpec((N_HEADS, tm, HEAD_DIM), lambda i: (0, i, 0)),
                   pl.BlockSpec((tm, 128), lambda i: (i, 0)),
                   pl.BlockSpec((D_MIX, d), lambda i: (0, 0)),
                   pl.BlockSpec((8, D_MIX), lambda i: (0, 0))],
        out_shape=[jax.ShapeDtypeStruct((t, D_CONV), F32),
                   jax.ShapeDtypeStruct((N_HEADS, t, HEAD_DIM), BF16),
                   jax.ShapeDtypeStruct((t, 128), F32),
                   jax.ShapeDtypeStruct((D_MIX, d), BF16),
                   jax.ShapeDtypeStruct((8, D_MIX), F32)],
        scratch_shapes=[pltpu.VMEM((D_MIX, d), F32)],
        compiler_params=_params(("arbitrary",)),
    )(dho, y, g, wout, mt, _head_indicator(), dep)


def _mix_proj_bwd(dp, dho, h, g, win_t, n):
    t, d = h.shape
    tm = _tok_block(t)
    ni = t // tm

    def body(dp_ref, dho_ref, h_ref, g_ref, w_ref, n_ref, dh_ref, dw_ref, dg_ref, acc_ref):
        i = pl.program_id(0)
        dpb = dp_ref[...]
        dn = _dot(dpb, w_ref[...])

        @pl.when(i == 0)
        def _():
            acc_ref[...] = jnp.zeros_like(acc_ref)

        acc_ref[...] += _dot_tn(dpb, n_ref[...])
        hh = h_ref[...]
        r = _rstd(hh)
        xhat = hh * r
        dxh = dn * g_ref[...]
        dh_ref[...] = dho_ref[...] + r * (dxh - xhat * jnp.mean(dxh * xhat, axis=-1, keepdims=True))
        part = _sum8(dn * xhat)

        @pl.when(i == 0)
        def _():
            dg_ref[...] = part

        @pl.when(i > 0)
        def _():
            dg_ref[...] += part

        @pl.when(i == ni - 1)
        def _():
            dw_ref[...] = acc_ref[...].astype(BF16)

    return _pcall(
        body, name="mix_proj_bwd", grid=(ni,),
        in_specs=[pl.BlockSpec((tm, D_IN), lambda i: (i, 0)),
                  pl.BlockSpec((tm, d), lambda i: (i, 0)),
                  pl.BlockSpec((tm, d), lambda i: (i, 0)),
                  pl.BlockSpec((1, d), lambda i: (0, 0)),
                  pl.BlockSpec((D_IN, d), lambda i: (0, 0)),
                  pl.BlockSpec((tm, d), lambda i: (i, 0))],
        out_specs=[pl.BlockSpec((tm, d), lambda i: (i, 0)),
                   pl.BlockSpec((D_IN, d), lambda i: (0, 0)),
                   pl.BlockSpec((8, d), lambda i: (0, 0))],
        out_shape=[jax.ShapeDtypeStruct((t, d), F32),
                   jax.ShapeDtypeStruct((D_IN, d), BF16),
                   jax.ShapeDtypeStruct((8, d), F32)],
        scratch_shapes=[pltpu.VMEM((D_IN, d), F32)],
        compiler_params=_params(("arbitrary",)),
    )(dp, dho, h, g, win_t, n)


def _final_loss(h, g, tgt):
    t, d = h.shape
    tm = _tok_block(t)

    def body(h_ref, g_ref, t_ref, dh_ref, ls_ref, dg_ref):
        i = pl.program_id(0)
        hh = h_ref[...]
        r = _rstd(hh)
        xhat = hh * r
        gg = g_ref[...]
        err = xhat * gg - t_ref[...]
        dy = err * (1.0 / d)
        dxh = dy * gg
        dh_ref[...] = r * (dxh - xhat * jnp.mean(dxh * xhat, axis=-1, keepdims=True))
        lpart = _sum8(err * err)
        gpart = _sum8(dy * xhat)

        @pl.when(i == 0)
        def _():
            ls_ref[...] = lpart
            dg_ref[...] = gpart

        @pl.when(i > 0)
        def _():
            ls_ref[...] += lpart
            dg_ref[...] += gpart

    return _pcall(
        body, name="final_loss", grid=(t // tm,),
        in_specs=[pl.BlockSpec((tm, d), lambda i: (i, 0)),
                  pl.BlockSpec((1, d), lambda i: (0, 0)),
                  pl.BlockSpec((tm, d), lambda i: (i, 0))],
        out_specs=[pl.BlockSpec((tm, d), lambda i: (i, 0)),
                   pl.BlockSpec((8, d), lambda i: (0, 0)),
                   pl.BlockSpec((8, d), lambda i: (0, 0))],
        out_shape=[jax.ShapeDtypeStruct((t, d), F32),
                   jax.ShapeDtypeStruct((8, d), F32),
                   jax.ShapeDtypeStruct((8, d), F32)],
        compiler_params=_params(("arbitrary",)),
    )(h, g, tgt)


def _position():
    return lax.axis_index("x"), lax.axis_index("y"), lax.axis_index("c")


def _flip(v, bit):
    return 1 - v if bit else v


def _peer(k):
    x, y, c = _position()
    return _flip(x, k & 4), _flip(y, k & 2), _flip(c, k & 1)


def _slot(px, py, pc):
    return 4 * px + 2 * py + pc


def _handshake(peers):
    barrier = pltpu.get_barrier_semaphore()
    for peer in peers:
        pl.semaphore_signal(barrier, inc=1, device_id=peer, device_id_type=MESH)
    pl.semaphore_wait(barrier, len(peers))


def _sequencer_call(body, name, collective_id, out_type, scratch_types, operands):
    return pl.kernel(
        body, out_type=out_type, mesh=plsc.ScalarSubcoreMesh(axis_name="sequencer", num_cores=1), name=name,
        scratch_types=scratch_types, compiler_params=pltpu.CompilerParams(collective_id=collective_id),
    )(*operands)


def _all_gather(shards, name, collective_id):
    nt = len(shards)

    def body(*refs):
        xs = refs[:nt]
        outs = refs[nt:2 * nt]
        send_sems, recv_sems, local_sems = refs[2 * nt:]
        x, y, c = _position()
        me, sibling = (x, y, c), (x, y, 1 - c)
        xn, yn, dg = (1 - x, y), (x, 1 - y), (1 - x, 1 - y)
        pick = lambda a, b: (jnp.where(c == 0, a[0], b[0]), jnp.where(c == 0, a[1], b[1]))
        relay_from, relay_to = pick(yn, xn), pick(xn, yn)
        _handshake([sibling, (*xn, c), (*yn, c)])

        def copy(t, k, block, to, src=None):
            dst = outs[t].at[_slot(*block)]
            return pltpu.make_async_remote_copy(
                src_ref=dst if src is None else src, dst_ref=dst,
                send_sem=send_sems.at[t, k], recv_sem=recv_sems.at[t, k],
                device_id=to, device_id_type=MESH)

        mine = [pltpu.make_async_copy(xs[t], outs[t].at[_slot(*me)], local_sems.at[t]) for t in range(nt)]
        for cp in mine:
            cp.start()
        sent = []
        for t in range(nt):
            sent += [copy(t, 0, me, sibling, src=xs[t]), copy(t, 1, me, (*xn, c), src=xs[t]),
                     copy(t, 2, me, (*yn, c), src=xs[t])]
        for cp in sent:
            cp.start()
        for t in range(nt):
            copy(t, 1, (*xn, c), me).wait_recv()
            copy(t, 2, (*yn, c), me).wait_recv()
            passed = [copy(t, 3, (*relay_from, c), (*relay_to, c)),
                      copy(t, 4, (*xn, c), sibling), copy(t, 5, (*yn, c), sibling)]
            for cp in passed:
                cp.start()
            sent += passed
        for t in range(nt):
            copy(t, 3, (*dg, c), me).wait_recv()
            fwd = copy(t, 6, (*dg, c), sibling)
            fwd.start()
            sent.append(fwd)
        for t in range(nt):
            copy(t, 0, sibling, me).wait_recv()
            for k, chip in ((4, xn), (5, yn), (6, dg)):
                copy(t, k, (*chip, 1 - c), me).wait_recv()
        for cp in sent:
            cp.wait_send()
        for cp in mine:
            cp.wait()

    return _sequencer_call(
        body, name, collective_id,
        out_type=[jax.ShapeDtypeStruct((N_DEV,) + s.shape, s.dtype) for s in shards],
        scratch_types=[pltpu.SemaphoreType.DMA((nt, 7)), pltpu.SemaphoreType.DMA((nt, 7)),
                       pltpu.SemaphoreType.DMA((nt,))],
        operands=shards)


def _scatter_copy(srcs, lands, send_sems, recv_sems, t, k):
    peer = _peer(k)
    return pltpu.make_async_remote_copy(
        src_ref=srcs[t].at[_slot(*peer)], dst_ref=lands[t].at[k],
        send_sem=send_sems.at[t * (N_DEV - 1) + k - 1], recv_sem=recv_sems.at[t * (N_DEV - 1) + k - 1],
        device_id=peer, device_id_type=MESH)


def _scatter_start(partials, name):
    nt = len(partials)

    def body(*refs):
        srcs, lands = refs[:nt], refs[nt:2 * nt]
        send_sems, recv_sems = refs[2 * nt], refs[2 * nt + 1]
        token = refs[-1]
        for k in range(1, N_DEV):
            for t in range(nt):
                _scatter_copy(srcs, lands, send_sems, recv_sems, t, k).start()
        token[...] = jnp.zeros_like(token)

    hbm = pl.BlockSpec(memory_space=pltpu.HBM)
    sem = pl.BlockSpec(memory_space=pltpu.SEMAPHORE)
    shapes = [pltpu.HBM(p.shape, p.dtype) for p in partials]
    lands = [pltpu.with_memory_space_constraint(lax.empty(p.shape, p.dtype), pltpu.HBM) for p in partials]
    srcs = [pltpu.with_memory_space_constraint(p, pltpu.HBM) for p in partials]
    out = _pcall(
        body, name=name,
        out_shape=[pltpu.SemaphoreType.DMA((nt * (N_DEV - 1),))] * 2 + shapes + shapes
        + [jax.ShapeDtypeStruct((8, 128), F32)],
        in_specs=[hbm] * (2 * nt),
        out_specs=[sem, sem] + [hbm] * (2 * nt) + [pl.BlockSpec(memory_space=pltpu.VMEM)],
        input_output_aliases={i: 2 + i for i in range(2 * nt)},
        compiler_params=pltpu.CompilerParams(has_side_effects=pltpu.SideEffectType.DATAFLOW_SIDE_EFFECTING),
    )(*srcs, *lands)
    return (nt, name, out[:-1]), out[-1]


def _scatter_wait(state, after):
    nt, name, (send_sems, recv_sems, *thru) = state

    def body(*refs):
        srcs, lands = refs[:nt], refs[nt:2 * nt]
        send_sems, recv_sems = refs[2 * nt], refs[2 * nt + 1]
        for k in range(1, N_DEV):
            for t in range(nt):
                copy = _scatter_copy(srcs, lands, send_sems, recv_sems, t, k)
                copy.wait_send()
                copy.wait_recv()

    hbm = pl.BlockSpec(memory_space=pltpu.HBM)
    sem = pl.BlockSpec(memory_space=pltpu.SEMAPHORE)
    out = _pcall(
        body, name=name + "_wait",
        out_shape=[pltpu.HBM(a.shape, a.dtype) for a in thru],
        in_specs=[hbm] * (2 * nt) + [sem, sem, pl.BlockSpec(memory_space=pl.ANY)],
        out_specs=[hbm] * (2 * nt),
        input_output_aliases={i: i for i in range(2 * nt)},
        compiler_params=pltpu.CompilerParams(has_side_effects=pltpu.SideEffectType.DATAFLOW_SIDE_EFFECTING),
    )(*thru, send_sems, recv_sems, after)
    return out[:nt], out[nt:]


def _all_reduce_rows(v):
    nv, _, w = v.shape

    def body(v_ref, out_ref, mine_ref, gath_ref, send_sems, recv_sems):
        x, y, c = _position()
        me = _slot(x, y, c)
        mine_ref[...] = jnp.sum(v_ref[...], axis=1)

        def copy(k):
            return pltpu.make_async_remote_copy(
                src_ref=mine_ref, dst_ref=gath_ref.at[me],
                send_sem=send_sems.at[k - 1], recv_sem=recv_sems.at[k - 1],
                device_id=_peer(k), device_id_type=MESH)

        def arrival(k):
            return pltpu.make_async_remote_copy(
                src_ref=mine_ref, dst_ref=gath_ref.at[_slot(*_peer(k))],
                send_sem=send_sems.at[k - 1], recv_sem=recv_sems.at[k - 1],
                device_id=_peer(k), device_id_type=MESH)

        sent = [copy(k) for k in range(1, N_DEV)]
        for cp in sent:
            cp.start()
        gath_ref[me] = mine_ref[...]
        for k in range(1, N_DEV):
            arrival(k).wait_recv()
        for cp in sent:
            cp.wait_send()
        total = gath_ref[0]
        for s in range(1, N_DEV):
            total = total + gath_ref[s]
        out_ref[...] = total

    vmem = pl.BlockSpec(memory_space=pltpu.VMEM)
    return _pcall(
        body, name="all_reduce_rows",
        in_specs=[vmem], out_specs=vmem,
        out_shape=jax.ShapeDtypeStruct((nv, w), F32),
        scratch_shapes=[pltpu.VMEM((nv, w), F32), pltpu.VMEM((N_DEV, nv, w), F32),
                        pltpu.SemaphoreType.DMA((7,)), pltpu.SemaphoreType.DMA((7,))],
    )(v)


def _adamw_math(w, g, m, v):
    m2 = ADAM_B1 * m + (1.0 - ADAM_B1) * g
    v2 = ADAM_B2 * v + (1.0 - ADAM_B2) * (g * g)
    m_hat = m2 / (1.0 - ADAM_B1 ** ADAM_STEP)
    v_hat = v2 / (1.0 - ADAM_B2 ** ADAM_STEP)
    delta = -ADAM_LR * (m_hat / (jnp.sqrt(v_hat) + ADAM_EPS) + ADAM_WD * w)
    return delta, m2, v2


def _row_block(r):
    for cand in (256, 176, 128):
        if r % cand == 0:
            return cand
    return r


def _adamw_sharded(me, grads, w, m, v, dep):
    (own0, land0), (own1, land1) = grads
    _, r, c = land0.shape
    tr = _row_block(r)
    nr = r // tr

    def body(me_ref, o0_ref, l0_ref, o1_ref, l1_ref, w_ref, m_ref, v_ref, dep_ref, g_ref, d_ref, m2_ref, v2_ref):
        layer = pl.program_id(0)

        def total(own_ref, land_ref):
            acc = own_ref[0].astype(F32)
            for k in range(1, N_DEV):
                acc = acc + land_ref[k].astype(F32)
            return acc

        g = jnp.where(layer == 0, total(o0_ref, l0_ref), total(o1_ref, l1_ref))
        delta, m2, v2 = _adamw_math(w_ref[0], g, m_ref[0], v_ref[0])
        g_ref[0] = g
        d_ref[0] = delta
        m2_ref[0] = m2
        v2_ref[0] = v2

    rows0 = lambda l, i: jnp.where(l == 0, i, nr - 1)
    rows1 = lambda l, i: jnp.where(l == 1, i, 0)
    shard = pl.BlockSpec((1, tr, c), lambda l, i, me_ref: (l, i, 0))
    out = jax.ShapeDtypeStruct((2, r, c), F32)
    return _pcall(
        body, name="adamw_sharded",
        grid_spec=pltpu.PrefetchScalarGridSpec(
            num_scalar_prefetch=1, grid=(2, nr),
            in_specs=[pl.BlockSpec((1, tr, c), lambda l, i, me_ref: (me_ref[0], rows0(l, i), 0)),
                      pl.BlockSpec((N_DEV, tr, c), lambda l, i, me_ref: (0, rows0(l, i), 0)),
                      pl.BlockSpec((1, tr, c), lambda l, i, me_ref: (me_ref[0], rows1(l, i), 0)),
                      pl.BlockSpec((N_DEV, tr, c), lambda l, i, me_ref: (0, rows1(l, i), 0)),
                      shard, shard, shard, pl.BlockSpec(memory_space=pl.ANY)],
            out_specs=[shard, shard, shard, shard]),
        out_shape=[out, out, out, out],
        compiler_params=_params(("arbitrary", "arbitrary")),
    )(me, own0, land0, own1, land1, w, m, v, dep)


def _adamw_small(w, g, m, v):
    def body(w_ref, g_ref, m_ref, v_ref, d_ref, m2_ref, v2_ref):
        delta, m2, v2 = _adamw_math(w_ref[...], g_ref[...], m_ref[...], v_ref[...])
        d_ref[...] = delta
        m2_ref[...] = m2
        v2_ref[...] = v2

    spec = pl.BlockSpec(w.shape, lambda i: (0, 0))
    out = jax.ShapeDtypeStruct(w.shape, F32)
    return _pcall(
        body, name="adamw_small", grid=(1,),
        in_specs=[spec] * 4, out_specs=[spec] * 3, out_shape=[out] * 3,
        compiler_params=_params(("arbitrary",)),
    )(w, g, m, v)


def _pack(arrs):
    flat = jnp.concatenate([a.reshape(-1) for a in arrs])
    n = flat.shape[0]
    rows = -(-n // 1024) * 8
    return jnp.pad(flat, (0, rows * 128 - n)).reshape(rows, 128)


def _unpack(packed, like):
    flat = packed.reshape(-1)
    out, off = [], 0
    for a in like:
        out.append(flat[off:off + a.size].reshape(a.shape))
        off += a.size
    return out


def kernel(x, mem, g_ffn1, w_ffn1_up, w_ffn1_down, g_mix, w_in, conv_w, sinks, g_mem, w_mem_kv, g_grp, w_out, g_ffn2, w_ffn2_up, w_ffn2_down, g_final, loss_target, m_g_ffn1, m_w_ffn1_up, m_w_ffn1_down, m_g_mix, m_w_in, m_conv_w, m_sinks, m_g_mem, m_w_mem_kv, m_g_grp, m_w_out, m_g_ffn2, m_w_ffn2_up, m_w_ffn2_down, m_g_final, v_g_ffn1, v_w_ffn1_up, v_w_ffn1_down, v_g_mix, v_w_in, v_conv_w, v_sinks, v_g_mem, v_w_mem_kv, v_g_grp, v_w_out, v_g_ffn2, v_w_ffn2_up, v_w_ffn2_down, v_g_final):
    depth = g_ffn1.shape[0]
    t, d = x.shape[1], x.shape[2]
    width = max(d, D_MIX)
    me = _slot(*_position())
    conv_shard = conv_w.shape[2]

    xin, memin, tgt = x[0], mem[0], loss_target[0]

    conv_tile = jnp.zeros((depth * 8, 128), F32).at[:, :conv_shard].set(
        jnp.pad(conv_w, ((0, 0), (0, 8 - conv_w.shape[1]), (0, 0))).reshape(depth * 8, conv_shard))
    tr = lambda a: jnp.swapaxes(a, -1, -2)
    bf = lambda a: a.astype(BF16)
    weights = []
    collective_id = 0
    for l in range(depth):
        groups = [[bf(tr(w_ffn1_up[l])), bf(w_ffn1_down[l])] + ([conv_tile] if l == 0 else []),
                  [bf(tr(w_in[l])), bf(w_mem_kv[l]), bf(w_out[l])],
                  [bf(tr(w_ffn2_up[l])), bf(w_ffn2_down[l])]]
        full = []
        for gi, shards in enumerate(groups):
            full.append(_all_gather(shards, f"all_gather_l{l}_g{gi}", collective_id))
            collective_id += 1
        if l == 0:
            conv_full = full[0][2].reshape(N_DEV, depth, 8, 128)[:, :, :3, :conv_shard]
            conv_full = conv_full.transpose(1, 2, 0, 3).reshape(depth, 3, N_DEV * conv_shard)
        weights.append(dict(
            up1=full[0][0].reshape(2, -1, d), dn1=full[0][1].reshape(-1, d),
            win=full[1][0].reshape(D_IN, d), wkv=full[1][1].reshape(d, 2 * D_MEMQ), wout=full[1][2].reshape(D_MIX, d),
            up2=full[2][0].reshape(2, -1, d), dn2=full[2][1].reshape(-1, d)))

    row = lambda a: a.reshape(1, -1)
    bias_key = _bias_table()

    h = xin
    saved = []
    for l in range(depth):
        wl = weights[l]
        s = dict(h0=h)
        h, s["gu1"], s["n1"] = _ffn_fwd(h, row(g_ffn1[l]), wl["up1"], wl["dn1"])
        s["h1"] = h
        s["p"], s["n_mix"], s["qh"] = _mix_proj_fwd(h, row(g_mix[l]), wl["win"])
        s["mkv"], s["nt_mem"] = _memkv_fwd(memin, row(g_mem[l]), wl["wkv"], s["p"])
        s["y"], s["lse"] = _mix_core_fwd(s["p"], s["qh"], s["mkv"], conv_full[l], row(sinks[l]), bias_key)
        h, s["mt"] = _mix_out_fwd(s["y"], h, row(g_grp[l]), wl["wout"])
        s["h2"] = h
        h, s["gu2"], s["n2"] = _ffn_fwd(h, row(g_ffn2[l]), wl["up2"], wl["dn2"])
        saved.append(s)

    dh, loss_part, dg_final = _final_loss(h, row(g_final), tgt)

    small = {}
    dep = loss_part

    def reduce_small():
        def lanes(a):
            return jnp.pad(a, ((0, 0), (0, width - a.shape[1])))

        def first_row(a):
            return lanes(jnp.pad(a, ((0, 8 - a.shape[0]), (0, 0))))

        vec_names = ["g_ffn1", "g_mix", "g_mem", "g_grp", "g_ffn2", "sinks"]
        tiles = [lanes(small[n, l]) for n in vec_names for l in range(depth)]
        tiles += [first_row(small["conv_w", l][k:k + 1]) for l in range(depth) for k in range(3)]
        tiles.append(lanes(dg_final))
        n_real = len(tiles)
        tiles.append(lanes(loss_part))
        tiles += [jnp.zeros((8, width), F32)] * (-len(tiles) % 8)
        summed = _all_reduce_rows(jnp.stack(tiles))
        loss_all = 0.5 * jnp.sum(summed[n_real]) / d

        def vec(n, wd):
            return jnp.stack([summed[vec_names.index(n) * depth + l, :wd] for l in range(depth)])

        conv_base = len(vec_names) * depth
        conv_grad = jnp.stack([jnp.stack([summed[conv_base + 3 * l + k, :D_CONV] for k in range(3)])
                               for l in range(depth)])
        grads_small = {
            "g_ffn1": vec("g_ffn1", d), "g_mix": vec("g_mix", d), "g_mem": vec("g_mem", d),
            "g_grp": vec("g_grp", D_MIX), "g_ffn2": vec("g_ffn2", d), "sinks": vec("sinks", N_SWA_HEADS),
            "conv_w": lax.dynamic_slice_in_dim(conv_grad, me * conv_shard, conv_shard, axis=2),
            "g_final": summed[n_real - 1, :d],
        }
        small_w = [("g_ffn1", g_ffn1, m_g_ffn1, v_g_ffn1), ("g_mix", g_mix, m_g_mix, v_g_mix),
                   ("conv_w", conv_w, m_conv_w, v_conv_w), ("sinks", sinks, m_sinks, v_sinks),
                   ("g_mem", g_mem, m_g_mem, v_g_mem), ("g_grp", g_grp, m_g_grp, v_g_grp),
                   ("g_ffn2", g_ffn2, m_g_ffn2, v_g_ffn2), ("g_final", g_final, m_g_final, v_g_final)]
        like = [w for _, w, _, _ in small_w]
        packed = _adamw_small(_pack(like), _pack([grads_small[n] for n, _, _, _ in small_w]),
                              _pack([m for _, _, m, _ in small_w]), _pack([v for _, _, _, v in small_w]))
        updated = {n: (grads_small[n], dl, m2, v2)
                   for (n, _, _, _), dl, m2, v2 in zip(small_w, *[_unpack(pk, like) for pk in packed])}
        return loss_all, updated, packed[0]

    started = []

    def scatter(names, partials, label):
        state, token = _scatter_start(partials, f"scatter_grads_{label}")
        started.append((names, state))
        return token

    def up_wgrads(agu, n, after, label):
        half = _ffn_bwd_w(agu, 0, n, after, f"ffn_bwd_w_gate_{label}", out_planes=2, slot=0)
        return _ffn_bwd_w(agu, 1, n, half, f"ffn_bwd_w_up_{label}", out_planes=2, slot=1).reshape(N_DEV, -1, d)

    for l in reversed(range(depth)):
        wl, s = weights[l], saved[l]
        dh, agu, dyb, small["g_ffn2", l] = _ffn_bwd_act(dh, s["h2"], row(g_ffn2[l]), s["gu2"], wl["up2"], wl["dn2"], dep)
        ddn2 = _ffn_bwd_w(agu, 2, dyb, agu, f"ffn_bwd_w_down_l{l}_ffn2").reshape(N_DEV, -1, d)
        dup2 = up_wgrads(agu, s["n2"], ddn2, f"l{l}_ffn2")
        dep = scatter([("w_ffn2_up", l), ("w_ffn2_down", l)], [dup2, ddn2], f"l{l}_ffn2")
        dyconv, doh, delta, dwout, small["g_grp", l] = _mix_out_bwd(dh, s["y"], row(g_grp[l]), wl["wout"], s["mt"], dep)
        dp, dmkv, small["conv_w", l], small["sinks", l] = _mix_core_bwd(
            s["p"], s["qh"], dyconv, doh, delta, s["lse"], s["mkv"], conv_full[l], row(sinks[l]), bias_key)
        dwkv, small["g_mem", l] = _memkv_bwd(dmkv, memin, row(g_mem[l]), wl["wkv"], s["nt_mem"])
        dh, dwin, small["g_mix", l] = _mix_proj_bwd(dp, dh, s["h1"], row(g_mix[l]), wl["win"], s["n_mix"])
        dep = scatter([("w_in", l), ("w_mem_kv", l), ("w_out", l)],
                      [dwin.reshape(N_DEV, -1, d), dwkv.reshape(N_DEV, -1, 2 * D_MEMQ), dwout.reshape(N_DEV, -1, d)],
                      f"l{l}_mix")
        dh, agu, dyb, small["g_ffn1", l] = _ffn_bwd_act(dh, s["h0"], row(g_ffn1[l]), s["gu1"], wl["up1"], wl["dn1"], dep)
        order_after = agu
        if l == 0:
            loss, small_out, order_after = reduce_small()
        ddn1 = _ffn_bwd_w(agu, 2, dyb, order_after, f"ffn_bwd_w_down_l{l}_ffn1").reshape(N_DEV, -1, d)
        if l > 0:
            dup1 = up_wgrads(agu, s["n1"], ddn1, f"l{l}_ffn1")
            dep = scatter([("w_ffn1_up", l), ("w_ffn1_down", l)], [dup1, ddn1], f"l{l}_ffn1")
        else:
            dep = scatter([("w_ffn1_down", l)], [ddn1], f"l{l}_ffn1_down")
            dup1 = up_wgrads(agu, s["n1"], dep, f"l{l}_ffn1")
            dep = scatter([("w_ffn1_up", l)], [dup1], f"l{l}_ffn1_up")
    grad_x = dh[None]

    big = {"w_ffn2_up": (w_ffn2_up, m_w_ffn2_up, v_w_ffn2_up, True), "w_ffn2_down": (w_ffn2_down, m_w_ffn2_down, v_w_ffn2_down, False),
           "w_in": (w_in, m_w_in, v_w_in, True), "w_mem_kv": (w_mem_kv, m_w_mem_kv, v_w_mem_kv, False),
           "w_out": (w_out, m_w_out, v_w_out, False), "w_ffn1_up": (w_ffn1_up, m_w_ffn1_up, v_w_ffn1_up, True),
           "w_ffn1_down": (w_ffn1_down, m_w_ffn1_down, v_w_ffn1_down, False)}
    me_index = jnp.reshape(me, (1,)).astype(jnp.int32)
    sharded, landed = {}, {}

    def finish(groups, after):
        for names, state in groups:
            owns, lands = _scatter_wait(state, after)
            for key, own, land in zip(names, owns, lands):
                landed[key] = (own, land)
            after = lands[0]
            for name in dict.fromkeys(n for n, _ in names):
                if name not in sharded and all((name, l) in landed for l in range(depth)):
                    w, m, v, transposed = big[name]
                    fix = tr if transposed else (lambda a: a)
                    res = _adamw_sharded(me_index, [landed[name, l] for l in range(depth)], fix(w), fix(m), fix(v), after)
                    sharded[name] = tuple(fix(r) for r in res)
                    after = res[0]
        return after

    finish(started[-2:], finish(started[:-2], dep))

    order = ["g_ffn1", "w_ffn1_up", "w_ffn1_down", "g_mix", "w_in", "conv_w", "sinks", "g_mem", "w_mem_kv", "g_grp",
             "w_out", "g_ffn2", "w_ffn2_up", "w_ffn2_down", "g_final"]
    results = {**sharded, **small_out}
    outs = [loss, grad_x]
    for part in range(4):
        outs += [results[n][part] for n in order]
    return tuple(outs)
```

```python
import numpy as np
import jax
import jax.numpy as jnp
from jax import lax
from jax.experimental import pallas as pl
from jax.experimental.pallas import tpu as pltpu
from jax.experimental.pallas import tpu_sc as plsc

F32 = jnp.float32
BF16 = jnp.bfloat16

N_DEV = 8
EPS = 1e-6
N_SWA_HEADS = 8
N_SWA_KV = 2
SWA_GROUP = N_SWA_HEADS // N_SWA_KV
HEAD_DIM = 64
N_MEM_HEADS = 4
D_CONV = 256
BLOCK = 128
D_SWA = N_SWA_HEADS * HEAD_DIM
D_KV = N_SWA_KV * HEAD_DIM
D_MEMQ = N_MEM_HEADS * HEAD_DIM
D_MIX = D_CONV + D_SWA + D_MEMQ
D_IN = 3 * D_CONV + D_SWA + 2 * D_KV + D_MEMQ
COL_BG, COL_CG, COL_U = 0, D_CONV, 2 * D_CONV
COL_Q = 3 * D_CONV
COL_K = COL_Q + D_SWA
COL_V = COL_K + D_KV
COL_QM = COL_V + D_KV
MIX_GROUPS = ((0, D_CONV), (D_CONV, D_CONV + D_SWA), (D_CONV + D_SWA, D_MIX))
SLOPES = tuple(2.0 ** (-8.0 * (i + 1) / N_SWA_HEADS) for i in range(N_SWA_HEADS))
SCALE = HEAD_DIM ** -0.5
NEG = -1e30

ADAM_LR = 0.001
ADAM_B1 = 0.9
ADAM_B2 = 0.999
ADAM_EPS = 1e-08
ADAM_WD = 0.01
ADAM_STEP = 10

V7X_VMEM_BYTES = 64 * 1024 * 1024
VMEM_LIMIT = (V7X_VMEM_BYTES * 3) // 4
MESH = pl.DeviceIdType.MESH


def _pcall(body, **kw):
    return pl.pallas_call(body, **kw)


def _params(sem=None, vmem=VMEM_LIMIT):
    return pltpu.CompilerParams(dimension_semantics=sem, vmem_limit_bytes=vmem)


def _dot(a, b):
    return lax.dot_general(a, b, (((1,), (0,)), ((), ())), preferred_element_type=F32)


def _dot_nt(a, b):
    return lax.dot_general(a, b, (((1,), (1,)), ((), ())), preferred_element_type=F32)


def _dot_tn(a, b):
    return lax.dot_general(a, b, (((0,), (0,)), ((), ())), preferred_element_type=F32)


def _rstd(x):
    return lax.rsqrt(jnp.mean(x * x, axis=-1, keepdims=True) + EPS)


def _sigmoid(x):
    return 1.0 / (1.0 + jnp.exp(-x))


def _sum8(x):
    r, w = x.shape
    return jnp.sum(x.reshape(r // 8, 8, w), axis=0)


def _tok_block(t, rows=512):
    return min(rows, t)


def _feat_block(f):
    return f // (N_DEV // 2)


def _ffn_fwd(h, g, wup_t, wdn):
    t, d = h.shape
    f = wdn.shape[0]
    tm, tf = _tok_block(t), _feat_block(f)
    ni, nj = t // tm, f // tf

    def body(h_ref, g_ref, wup_ref, wdn_ref, ho_ref, gu_ref, n_ref, nt_ref, acc_ref):
        j = pl.program_id(1)

        @pl.when(j == 0)
        def _():
            hh = h_ref[...]
            n = hh * _rstd(hh) * g_ref[...]
            n_ref[...] = n.astype(BF16)
            nt_ref[...] = n.T.astype(BF16)
            acc_ref[...] = jnp.zeros_like(acc_ref)

        nt = nt_ref[...]
        gate = _dot(wup_ref[0], nt)
        up = _dot(wup_ref[1], nt)
        gu_ref[0] = gate.astype(BF16)
        gu_ref[1] = up.astype(BF16)
        a = gate * _sigmoid(gate) * up
        acc_ref[...] += _dot_tn(a.astype(BF16), wdn_ref[...])

        @pl.when(j == nj - 1)
        def _():
            ho_ref[...] = h_ref[...] + 0.5 * acc_ref[...]

    return _pcall(
        body, name="ffn_fwd", grid=(ni, nj),
        in_specs=[pl.BlockSpec((tm, d), lambda i, j: (i, 0)),
                  pl.BlockSpec((1, d), lambda i, j: (0, 0)),
                  pl.BlockSpec((2, tf, d), lambda i, j: (0, j, 0)),
                  pl.BlockSpec((tf, d), lambda i, j: (j, 0))],
        out_specs=[pl.BlockSpec((tm, d), lambda i, j: (i, 0)),
                   pl.BlockSpec((2, tf, tm), lambda i, j: (0, j, i)),
                   pl.BlockSpec((tm, d), lambda i, j: (i, 0))],
        out_shape=[jax.ShapeDtypeStruct((t, d), F32),
                   jax.ShapeDtypeStruct((2, f, t), BF16),
                   jax.ShapeDtypeStruct((t, d), BF16)],
        scratch_shapes=[pltpu.VMEM((d, tm), BF16), pltpu.VMEM((tm, d), F32)],
        compiler_params=_params(("parallel", "arbitrary")),
    )(h, g, wup_t, wdn)


def _ffn_bwd_act(dho, h, g, gu, wup_t, wdn, dep):
    t, d = h.shape
    f = wdn.shape[0]
    tm, tf = _tok_block(t), _feat_block(f)
    ni, nj = t // tm, f // tf

    def body(dho_ref, h_ref, g_ref, gu_ref, wup_ref, wdn_ref, dep_ref, dh_ref, agu_ref, dyb_ref, dg_ref, dyt_ref, acc_ref):
        i = pl.program_id(0)
        j = pl.program_id(1)

        @pl.when(j == 0)
        def _():
            dy0 = 0.5 * dho_ref[...]
            dyb_ref[...] = dy0.astype(BF16)
            dyt_ref[...] = dy0.T.astype(BF16)
            acc_ref[...] = jnp.zeros_like(acc_ref)

        da = _dot(wdn_ref[...], dyt_ref[...]).astype(BF16)
        gate = gu_ref[0]
        up = gu_ref[1]
        sg = _sigmoid(gate)
        silu = gate * sg
        dgate = da * up * (sg * (1.0 + gate * (1.0 - sg)))
        dup = da * silu
        agu_ref[0] = dgate
        agu_ref[1] = dup
        agu_ref[2] = silu * up
        acc_ref[...] += _dot_tn(dgate, wup_ref[0])
        acc_ref[...] += _dot_tn(dup, wup_ref[1])

        @pl.when(j == nj - 1)
        def _():
            hh = h_ref[...]
            r = _rstd(hh)
            xhat = hh * r
            dnf = acc_ref[...]
            dxh = dnf * g_ref[...]
            dh_ref[...] = dho_ref[...] + r * (dxh - xhat * jnp.mean(dxh * xhat, axis=-1, keepdims=True))
            part = _sum8(dnf * xhat)

            @pl.when(i == 0)
            def _():
                dg_ref[...] = part

            @pl.when(i > 0)
            def _():
                dg_ref[...] += part

    return _pcall(
        body, name="ffn_bwd_act", grid=(ni, nj),
        in_specs=[pl.BlockSpec((tm, d), lambda i, j: (i, 0)),
                  pl.BlockSpec((tm, d), lambda i, j: (i, 0)),
                  pl.BlockSpec((1, d), lambda i, j: (0, 0)),
                  pl.BlockSpec((2, tf, tm), lambda i, j: (0, j, i)),
                  pl.BlockSpec((2, tf, d), lambda i, j: (0, j, 0)),
                  pl.BlockSpec((tf, d), lambda i, j: (j, 0)),
                  pl.BlockSpec(memory_space=pl.ANY)],
        out_specs=[pl.BlockSpec((tm, d), lambda i, j: (i, 0)),
                   pl.BlockSpec((3, tf, tm), lambda i, j: (0, j, i)),
                   pl.BlockSpec((tm, d), lambda i, j: (i, 0)),
                   pl.BlockSpec((8, d), lambda i, j: (0, 0))],
        out_shape=[jax.ShapeDtypeStruct((t, d), F32),
                   jax.ShapeDtypeStruct((3, f, t), BF16),
                   jax.ShapeDtypeStruct((t, d), BF16),
                   jax.ShapeDtypeStruct((8, d), F32)],
        scratch_shapes=[pltpu.VMEM((d, tm), BF16), pltpu.VMEM((tm, d), F32)],
        compiler_params=_params(("arbitrary", "arbitrary")),
    )(dho, h, g, gu, wup_t, wdn, dep)


def _ffn_bwd_w(agu, first, count, rhs, dep, name):
    _, f, t = agu.shape
    d = rhs.shape[1]
    tm, tf = _tok_block(t, 2048), _feat_block(f)
    ni, nj = t // tm, f // tf

    def body(lhs_ref, rhs_ref, dep_ref, dw_ref, acc_ref):
        i = pl.program_id(1)
        @pl.when(i == 0)
        def _():
            acc_ref[...] = jnp.zeros_like(acc_ref)

        rb = rhs_ref[...]
        for k in range(count):
            acc_ref[k] += _dot(lhs_ref[k], rb)

        @pl.when(i == ni - 1)
        def _():
            dw_ref[...] = acc_ref[...].astype(BF16)

    return _pcall(
        body, name=name, grid=(nj, ni),
        in_specs=[pl.BlockSpec((count, tf, tm), lambda j, i: (first // count, j, i)),
                  pl.BlockSpec((tm, d), lambda j, i: (i, 0)),
                  pl.BlockSpec(memory_space=pl.ANY)],
        out_specs=pl.BlockSpec((count, tf, d), lambda j, i: (0, j, 0)),
        out_shape=jax.ShapeDtypeStruct((count, f, d), BF16),
        scratch_shapes=[pltpu.VMEM((count, tf, d), F32)],
        compiler_params=_params(("parallel", "arbitrary")),
    )(agu, rhs, dep)


N_HEADS = N_SWA_HEADS + N_MEM_HEADS


def _q_col(hd):
    return COL_Q + HEAD_DIM * hd if hd < N_SWA_HEADS else COL_QM + HEAD_DIM * (hd - N_SWA_HEADS)


def _mix_proj_fwd(h, g, win_t):
    t, d = h.shape
    tm = _tok_block(t)

    def body(h_ref, g_ref, win_ref, p_ref, n_ref, qh_ref):
        hh = h_ref[...]
        n = (hh * _rstd(hh) * g_ref[...]).astype(BF16)
        n_ref[...] = n
        proj = _dot_nt(n, win_ref[...])
        p_ref[...] = proj.astype(BF16)
        for hd in range(N_HEADS):
            c0 = _q_col(hd)
            qh_ref[hd] = (proj[:, c0:c0 + HEAD_DIM] * SCALE).astype(BF16)

    return _pcall(
        body, name="mix_proj_fwd", grid=(t // tm,),
        in_specs=[pl.BlockSpec((tm, d), lambda i: (i, 0)),
                  pl.BlockSpec((1, d), lambda i: (0, 0)),
                  pl.BlockSpec((D_IN, d), lambda i: (0, 0))],
        out_specs=[pl.BlockSpec((tm, D_IN), lambda i: (i, 0)),
                   pl.BlockSpec((tm, d), lambda i: (i, 0)),
                   pl.BlockSpec((N_HEADS, tm, HEAD_DIM), lambda i: (0, i, 0))],
        out_shape=[jax.ShapeDtypeStruct((t, D_IN), BF16), jax.ShapeDtypeStruct((t, d), BF16),
                   jax.ShapeDtypeStruct((N_HEADS, t, HEAD_DIM), BF16)],
        compiler_params=_params(("parallel",)),
    )(h, g, win_t)


def _memkv_fwd(mem, g, wkv, dep):
    m, d = mem.shape

    def body(mem_ref, g_ref, w_ref, dep_ref, mkv_ref, nt_ref):
        mm = mem_ref[...]
        n = mm * _rstd(mm) * g_ref[...]
        nt_ref[...] = n.T.astype(BF16)
        mkv_ref[...] = _dot(n.astype(BF16), w_ref[...]).astype(BF16)

    return _pcall(
        body, name="memkv_fwd", grid=(1,),
        in_specs=[pl.BlockSpec((m, d), lambda i: (0, 0)),
                  pl.BlockSpec((1, d), lambda i: (0, 0)),
                  pl.BlockSpec((d, 2 * D_MEMQ), lambda i: (0, 0)),
                  pl.BlockSpec(memory_space=pl.ANY)],
        out_specs=[pl.BlockSpec((m, 2 * D_MEMQ), lambda i: (0, 0)),
                   pl.BlockSpec((d, m), lambda i: (0, 0))],
        out_shape=[jax.ShapeDtypeStruct((m, 2 * D_MEMQ), BF16), jax.ShapeDtypeStruct((d, m), BF16)],
        compiler_params=_params(("arbitrary",)),
    )(mem, g, wkv, dep)


def _memkv_bwd(dmkv, mem, g, wkv, nt):
    m, d = mem.shape

    def body(dmkv_ref, mem_ref, g_ref, w_ref, nt_ref, dw_ref, dg_ref):
        db = dmkv_ref[...].astype(BF16)
        dw_ref[...] = _dot(nt_ref[...], db).astype(BF16)
        dn = _dot_nt(db, w_ref[...])
        mm = mem_ref[...]
        dg_ref[...] = _sum8(dn * (mm * _rstd(mm)))

    return _pcall(
        body, name="memkv_bwd", grid=(1,),
        in_specs=[pl.BlockSpec((m, 2 * D_MEMQ), lambda i: (0, 0)),
                  pl.BlockSpec((m, d), lambda i: (0, 0)),
                  pl.BlockSpec((1, d), lambda i: (0, 0)),
                  pl.BlockSpec((d, 2 * D_MEMQ), lambda i: (0, 0)),
                  pl.BlockSpec((d, m), lambda i: (0, 0))],
        out_specs=[pl.BlockSpec((d, 2 * D_MEMQ), lambda i: (0, 0)),
                   pl.BlockSpec((8, d), lambda i: (0, 0))],
        out_shape=[jax.ShapeDtypeStruct((d, 2 * D_MEMQ), BF16), jax.ShapeDtypeStruct((8, d), F32)],
        compiler_params=_params(("arbitrary",)),
    )(dmkv, mem, g, wkv, nt)


def _shift_rows(v, k, edge_rows, row):
    out = pltpu.roll(v, k, 0)
    for r in range(k):
        out = jnp.where(row == r, edge_rows[r], out)
    return out


def _shift_rows_up(v, k, edge_rows, row):
    n = v.shape[0]
    out = pltpu.roll(v, n - k, 0)
    for r in range(k):
        out = jnp.where(row == n - k + r, edge_rows[r], out)
    return out


GROUP_ROWS = SWA_GROUP * BLOCK
BIAS_CUR, BIAS_PREV, BIAS_NONE = 0, 1, 2


def _bias_table():
    tq = np.arange(BLOCK)[:, None]
    sk = np.arange(BLOCK)[None, :]
    slopes = np.asarray(SLOPES, np.float32)[:, None, None]
    cur = np.where(tq >= sk, -slopes * (tq - sk).astype(np.float32), NEG)
    prev = np.where(sk > tq, -slopes * (tq + BLOCK - sk).astype(np.float32), NEG)
    none = np.full_like(cur, NEG)
    tok = np.stack([cur, prev, none]).astype(np.float32).reshape(3, N_SWA_KV, GROUP_ROWS, BLOCK)
    return jnp.asarray(np.ascontiguousarray(tok.transpose(0, 1, 3, 2)))


def _head_cols(hd):
    return D_CONV + HEAD_DIM * hd


def _mix_core_fwd(p, qh, mkv, convw, sinks, bias_key):
    t = p.shape[0]
    m = mkv.shape[0]
    nb = t // BLOCK

    def body(sk_ref, pc_ref, pkv_ref, ppc_ref, ppu_ref, qh_ref, mkv_ref, cw_ref, bc_ref, bp_ref, y_ref, l_ref):
        i = pl.program_id(0)
        prevf = (i > 0).astype(F32)
        row = lax.broadcasted_iota(jnp.int32, (BLOCK, D_CONV), 0)

        bg = pc_ref[:, COL_BG:COL_BG + D_CONV].astype(F32)
        cg = pc_ref[:, COL_CG:COL_CG + D_CONV].astype(F32)
        u = pc_ref[:, COL_U:COL_U + D_CONV].astype(F32)
        vv = cg * u
        pvv = ppc_ref[...].astype(F32) * ppu_ref[...].astype(F32) * prevf
        vv1 = _shift_rows(vv, 1, [pvv[15:16]], row)
        vv2 = _shift_rows(vv, 2, [pvv[14:15], pvv[15:16]], row)
        w = cw_ref[...]
        y_ref[:, 0:D_CONV] = bg * (w[0:1] * vv2 + w[1:2] * vv1 + w[2:3] * vv)

        head_row = lax.broadcasted_iota(jnp.int32, (128, BLOCK), 0)
        lse_t = jnp.zeros((128, BLOCK), F32)
        for kv in range(N_SWA_KV):
            heads = range(kv * SWA_GROUP, (kv + 1) * SWA_GROUP)
            kc = pc_ref[:, COL_K + HEAD_DIM * kv:COL_K + HEAD_DIM * (kv + 1)]
            vc = pc_ref[:, COL_V + HEAD_DIM * kv:COL_V + HEAD_DIM * (kv + 1)]
            kp = pkv_ref[:, HEAD_DIM * kv:HEAD_DIM * (kv + 1)]
            vp = pkv_ref[:, D_KV + HEAD_DIM * kv:D_KV + HEAD_DIM * (kv + 1)]
            qg = qh_ref[kv * SWA_GROUP:(kv + 1) * SWA_GROUP].reshape(GROUP_ROWS, HEAD_DIM)
            sc = _dot_nt(kc, qg) + bc_ref[0, kv]
            sp = _dot_nt(kp, qg) + bp_ref[0, kv]
            sink = jnp.concatenate([jnp.full((1, BLOCK), sk_ref[0, hd], F32) for hd in heads], axis=1)
            mx = jnp.maximum(jnp.max(jnp.maximum(sc, sp), axis=0, keepdims=True), sink)
            ec = jnp.exp(sc - mx)
            ep = jnp.exp(sp - mx)
            den = jnp.sum(ec + ep, axis=0, keepdims=True) + jnp.exp(sink - mx)
            ot = (_dot_tn(vc, ec.astype(BF16)) + _dot_tn(vp, ep.astype(BF16))) / den
            lse = mx + jnp.log(den)
            for gi, hd in enumerate(heads):
                span = slice(gi * BLOCK, (gi + 1) * BLOCK)
                y_ref[:, _head_cols(hd):_head_cols(hd) + HEAD_DIM] = ot[:, span].T
                lse_t = jnp.where(head_row == hd, lse[:, span], lse_t)

        for hm in range(N_MEM_HEADS):
            hd = N_SWA_HEADS + hm
            mk = mkv_ref[:, HEAD_DIM * hm:HEAD_DIM * (hm + 1)]
            mv = mkv_ref[:, D_MEMQ + HEAD_DIM * hm:D_MEMQ + HEAD_DIM * (hm + 1)]
            s = _dot_nt(mk, qh_ref[hd])
            mx = jnp.max(s, axis=0, keepdims=True)
            e = jnp.exp(s - mx)
            den = jnp.sum(e, axis=0, keepdims=True)
            y_ref[:, _head_cols(hd):_head_cols(hd) + HEAD_DIM] = (_dot_tn(mv, e.astype(BF16)) / den).T
            lse_t = jnp.where(head_row == hd, mx + jnp.log(den), lse_t)
        l_ref[...] = lse_t.T

    kv_col = COL_K // (2 * D_KV)
    bias_block = (1, N_SWA_KV, BLOCK, GROUP_ROWS)
    return _pcall(
        body, name="mix_core_fwd", grid=(nb,),
        in_specs=[pl.BlockSpec(memory_space=pltpu.SMEM),
                  pl.BlockSpec((BLOCK, D_IN), lambda i: (i, 0)),
                  pl.BlockSpec((BLOCK, 2 * D_KV), lambda i: (jnp.maximum(i - 1, 0), kv_col)),
                  pl.BlockSpec((16, D_CONV), lambda i: (jnp.maximum(i * (BLOCK // 16) - 1, 0), COL_CG // D_CONV)),
                  pl.BlockSpec((16, D_CONV), lambda i: (jnp.maximum(i * (BLOCK // 16) - 1, 0), COL_U // D_CONV)),
                  pl.BlockSpec((N_HEADS, BLOCK, HEAD_DIM), lambda i: (0, i, 0)),
                  pl.BlockSpec((m, 2 * D_MEMQ), lambda i: (0, 0)),
                  pl.BlockSpec((3, D_CONV), lambda i: (0, 0)),
                  pl.BlockSpec(bias_block, lambda i: (BIAS_CUR, 0, 0, 0)),
                  pl.BlockSpec(bias_block, lambda i: (jnp.where(i == 0, BIAS_NONE, BIAS_PREV), 0, 0, 0))],
        out_specs=[pl.BlockSpec((BLOCK, D_MIX), lambda i: (i, 0)),
                   pl.BlockSpec((BLOCK, 128), lambda i: (i, 0))],
        out_shape=[jax.ShapeDtypeStruct((t, D_MIX), F32), jax.ShapeDtypeStruct((t, 128), F32)],
        compiler_params=_params(("parallel",)),
    )(sinks, p, p, p, p, qh, mkv, convw, bias_key, bias_key)


def _mix_core_bwd(p, qh, dyconv, doh, delta, lse, mkv, convw, sinks, bias_key):
    t = p.shape[0]
    m = mkv.shape[0]
    nb = t // BLOCK

    def body(sk_ref, pc_ref, pkv_ref, ppc_ref, ppu_ref, pnb_ref, dyc_ref, dyn_ref, qc_ref, qn_ref, doc_ref, don_ref,
             dlc_ref, dln_ref, lc_ref, ln_ref, mkv_ref, cw_ref, bp_ref, bct_ref, bnt_ref,
             dp_ref, dmkv_ref, dcw_ref, dsk_ref):
        i = pl.program_id(0)
        prevf = (i > 0).astype(F32)
        nextf = (i < nb - 1).astype(F32)
        row = lax.broadcasted_iota(jnp.int32, (BLOCK, D_CONV), 0)

        @pl.when(i == 0)
        def _():
            dmkv_ref[...] = jnp.zeros_like(dmkv_ref)
            dcw_ref[...] = jnp.zeros_like(dcw_ref)
            dsk_ref[...] = jnp.zeros_like(dsk_ref)

        bg = pc_ref[:, COL_BG:COL_BG + D_CONV].astype(F32)
        cg = pc_ref[:, COL_CG:COL_CG + D_CONV].astype(F32)
        u = pc_ref[:, COL_U:COL_U + D_CONV].astype(F32)
        vv = cg * u
        pvv = ppc_ref[...].astype(F32) * ppu_ref[...].astype(F32) * prevf
        vv1 = _shift_rows(vv, 1, [pvv[15:16]], row)
        vv2 = _shift_rows(vv, 2, [pvv[14:15], pvv[15:16]], row)
        w = cw_ref[...]
        yconv = w[0:1] * vv2 + w[1:2] * vv1 + w[2:3] * vv
        dyo = dyc_ref[...]
        dyc = dyo * bg
        nxt = dyn_ref[...] * pnb_ref[...].astype(F32) * nextf
        d1 = _shift_rows_up(dyc, 1, [nxt[0:1]], row)
        d2 = _shift_rows_up(dyc, 2, [nxt[0:1], nxt[1:2]], row)
        dvv = w[2:3] * dyc + w[1:2] * d1 + w[0:1] * d2
        dp_ref[:, COL_BG:COL_BG + D_CONV] = (dyo * yconv).astype(BF16)
        dp_ref[:, COL_CG:COL_CG + D_CONV] = (dvv * u).astype(BF16)
        dp_ref[:, COL_U:COL_U + D_CONV] = (dvv * cg).astype(BF16)
        dcw_ref[0:1, :] += jnp.sum(dyc * vv2, axis=0, keepdims=True)
        dcw_ref[1:2, :] += jnp.sum(dyc * vv1, axis=0, keepdims=True)
        dcw_ref[2:3, :] += jnp.sum(dyc * vv, axis=0, keepdims=True)

        lse_t, dl_t = lc_ref[...].T, dlc_ref[...].T
        lse_nt, dl_nt = ln_ref[...].T, dln_ref[...].T

        def stack_rows(tile_t, heads):
            return jnp.concatenate([tile_t[hd:hd + 1, :] for hd in heads], axis=1)

        lane8 = jnp.where(lax.broadcasted_iota(jnp.int32, (8, 128), 0) == 0,
                          lax.broadcasted_iota(jnp.int32, (8, 128), 1), -1)
        dsk = jnp.zeros((8, 128), F32)
        for kv in range(N_SWA_KV):
            heads = range(kv * SWA_GROUP, (kv + 1) * SWA_GROUP)
            kc = pc_ref[:, COL_K + HEAD_DIM * kv:COL_K + HEAD_DIM * (kv + 1)]
            vc = pc_ref[:, COL_V + HEAD_DIM * kv:COL_V + HEAD_DIM * (kv + 1)]
            kp = pkv_ref[:, HEAD_DIM * kv:HEAD_DIM * (kv + 1)]
            vp = pkv_ref[:, D_KV + HEAD_DIM * kv:D_KV + HEAD_DIM * (kv + 1)]
            qg = qc_ref[kv * SWA_GROUP:(kv + 1) * SWA_GROUP].reshape(GROUP_ROWS, HEAD_DIM)
            dog = doc_ref[kv * SWA_GROUP:(kv + 1) * SWA_GROUP].reshape(GROUP_ROWS, HEAD_DIM)
            qn = qn_ref[kv * SWA_GROUP:(kv + 1) * SWA_GROUP].reshape(GROUP_ROWS, HEAD_DIM)
            don = don_ref[kv * SWA_GROUP:(kv + 1) * SWA_GROUP].reshape(GROUP_ROWS, HEAD_DIM)
            lse_row, dl_row = stack_rows(lse_t, heads), stack_rows(dl_t, heads)
            ptp = jnp.exp(_dot_nt(kp, qg) + bp_ref[0, kv] - lse_row)
            dstp = (ptp * (_dot_nt(vp, dog) - dl_row)).astype(BF16)
            dq = _dot_tn(dstp, kp)
            pt = jnp.exp(_dot_nt(kc, qg) + bct_ref[0, kv] - lse_row)
            dst = (pt * (_dot_nt(vc, dog) - dl_row)).astype(BF16)
            dv = _dot(pt.astype(BF16), dog)
            dk = _dot(dst, qg)
            dq = dq + _dot_tn(dst, kc)
            ptn = jnp.exp(_dot_nt(kc, qn) + bnt_ref[0, kv] - stack_rows(lse_nt, heads))
            dstn = (ptn * (_dot_nt(vc, don) - stack_rows(dl_nt, heads))).astype(BF16)
            dv = dv + _dot(ptn.astype(BF16), don)
            dk = dk + _dot(dstn, qn)
            dp_ref[:, COL_K + HEAD_DIM * kv:COL_K + HEAD_DIM * (kv + 1)] = dk.astype(BF16)
            dp_ref[:, COL_V + HEAD_DIM * kv:COL_V + HEAD_DIM * (kv + 1)] = dv.astype(BF16)
            sink = jnp.concatenate([jnp.full((1, BLOCK), sk_ref[0, hd], F32) for hd in heads], axis=1)
            sink_term = jnp.exp(sink - lse_row) * dl_row
            for gi, hd in enumerate(heads):
                span = slice(gi * BLOCK, (gi + 1) * BLOCK)
                dp_ref[:, _q_col(hd):_q_col(hd) + HEAD_DIM] = (dq[span] * SCALE).astype(BF16)
                dsk = dsk + jnp.where(lane8 == hd, -jnp.sum(sink_term[:, span], axis=1, keepdims=True), 0.0)
        dsk_ref[...] += dsk

        for hm in range(N_MEM_HEADS):
            hd = N_SWA_HEADS + hm
            qm, dom = qc_ref[hd], doc_ref[hd]
            mk = mkv_ref[:, HEAD_DIM * hm:HEAD_DIM * (hm + 1)]
            mv = mkv_ref[:, D_MEMQ + HEAD_DIM * hm:D_MEMQ + HEAD_DIM * (hm + 1)]
            pt = jnp.exp(_dot_nt(mk, qm) - lse_t[hd:hd + 1, :])
            dst = (pt * (_dot_nt(mv, dom) - dl_t[hd:hd + 1, :])).astype(BF16)
            dp_ref[:, _q_col(hd):_q_col(hd) + HEAD_DIM] = (_dot_tn(dst, mk) * SCALE).astype(BF16)
            dmkv_ref[:, HEAD_DIM * hm:HEAD_DIM * (hm + 1)] += _dot(dst, qm)
            dmkv_ref[:, D_MEMQ + HEAD_DIM * hm:D_MEMQ + HEAD_DIM * (hm + 1)] += _dot(pt.astype(BF16), dom)

    cur = lambda i: (i, 0)
    const = lambda i: (0, 0)
    rows16 = BLOCK // 16
    last16 = t // 16 - 1
    before = lambda col: (lambda i: (jnp.maximum(i * rows16 - 1, 0), col))
    after = lambda i: (jnp.minimum((i + 1) * rows16, last16), 0)
    heads_cur = lambda i: (0, i, 0)
    heads_next = lambda i: (0, jnp.minimum(i + 1, nb - 1), 0)
    stat_next = lambda i: (jnp.minimum(i + 1, nb - 1), 0)
    key_block = (1, N_SWA_KV, BLOCK, GROUP_ROWS)
    head_block = (N_HEADS, BLOCK, HEAD_DIM)
    return _pcall(
        body, name="mix_core_bwd", grid=(nb,),
        in_specs=[pl.BlockSpec(memory_space=pltpu.SMEM),
                  pl.BlockSpec((BLOCK, D_IN), cur),
                  pl.BlockSpec((BLOCK, 2 * D_KV), lambda i: (jnp.maximum(i - 1, 0), COL_K // (2 * D_KV))),
                  pl.BlockSpec((16, D_CONV), before(COL_CG // D_CONV)),
                  pl.BlockSpec((16, D_CONV), before(COL_U // D_CONV)),
                  pl.BlockSpec((16, D_CONV), after),
                  pl.BlockSpec((BLOCK, D_CONV), cur),
                  pl.BlockSpec((16, D_CONV), after),
                  pl.BlockSpec(head_block, heads_cur), pl.BlockSpec(head_block, heads_next),
                  pl.BlockSpec(head_block, heads_cur), pl.BlockSpec(head_block, heads_next),
                  pl.BlockSpec((BLOCK, 128), cur), pl.BlockSpec((BLOCK, 128), stat_next),
                  pl.BlockSpec((BLOCK, 128), cur), pl.BlockSpec((BLOCK, 128), stat_next),
                  pl.BlockSpec((m, 2 * D_MEMQ), const),
                  pl.BlockSpec((3, D_CONV), const),
                  pl.BlockSpec(key_block, lambda i: (jnp.where(i == 0, BIAS_NONE, BIAS_PREV), 0, 0, 0)),
                  pl.BlockSpec(key_block, lambda i: (BIAS_CUR, 0, 0, 0)),
                  pl.BlockSpec(key_block, lambda i: (jnp.where(i == nb - 1, BIAS_NONE, BIAS_PREV), 0, 0, 0))],
        out_specs=[pl.BlockSpec((BLOCK, D_IN), cur),
                   pl.BlockSpec((m, 2 * D_MEMQ), const),
                   pl.BlockSpec((8, D_CONV), const),
                   pl.BlockSpec((8, 128), const)],
        out_shape=[jax.ShapeDtypeStruct((t, D_IN), BF16),
                   jax.ShapeDtypeStruct((m, 2 * D_MEMQ), F32),
                   jax.ShapeDtypeStruct((8, D_CONV), F32),
                   jax.ShapeDtypeStruct((8, 128), F32)],
        compiler_params=_params(("arbitrary",)),
    )(sinks, p, p, p, p, p, dyconv, dyconv, qh, qh, doh, doh, delta, delta, lse, lse, mkv, convw,
      bias_key, bias_key, bias_key)


def _group_norms(y):
    out = []
    for a, b in MIX_GROUPS:
        ys = y[:, a:b]
        r = _rstd(ys)
        out.append((ys * r, r))
    return out


def _mix_out_fwd(y, h, g, wout):
    t, d = h.shape
    tm = _tok_block(t)

    def body(y_ref, h_ref, g_ref, w_ref, ho_ref, mt_ref):
        yhat = jnp.concatenate([yh for yh, _ in _group_norms(y_ref[...])], axis=-1)
        mixed = yhat * g_ref[...]
        mt_ref[...] = mixed.T.astype(BF16)
        ho_ref[...] = h_ref[...] + _dot(mixed.astype(BF16), w_ref[...])

    return _pcall(
        body, name="mix_out_fwd", grid=(t // tm,),
        in_specs=[pl.BlockSpec((tm, D_MIX), lambda i: (i, 0)),
                  pl.BlockSpec((tm, d), lambda i: (i, 0)),
                  pl.BlockSpec((1, D_MIX), lambda i: (0, 0)),
                  pl.BlockSpec((D_MIX, d), lambda i: (0, 0))],
        out_specs=[pl.BlockSpec((tm, d), lambda i: (i, 0)),
                   pl.BlockSpec((D_MIX, tm), lambda i: (0, i))],
        out_shape=[jax.ShapeDtypeStruct((t, d), F32), jax.ShapeDtypeStruct((D_MIX, t), BF16)],
        compiler_params=_params(("parallel",)),
    )(y, h, g, wout)


def _head_indicator():
    ind = np.zeros((D_MIX, 128), np.float32)
    for hd in range(N_HEADS):
        ind[_head_cols(hd):_head_cols(hd) + HEAD_DIM, hd] = 1.0
    return jnp.asarray(ind, BF16)


def _mix_out_bwd(dho, y, g, wout, mt, dep):
    t, d = dho.shape
    tm = _tok_block(t)
    ni = t // tm

    def body(dho_ref, y_ref, g_ref, w_ref, mt_ref, ind_ref, dep_ref, dyc_ref, doh_ref, dl_ref, dw_ref, dg_ref, acc_ref):
        i = pl.program_id(0)
        dhb = dho_ref[...].astype(BF16)
        dm = _dot_nt(dhb, w_ref[...])
        pw = _dot(mt_ref[...], dhb)
        gg = g_ref[...]
        yy = y_ref[...]
        dys = []
        dgs = []
        for (a, b), (yhat, r) in zip(MIX_GROUPS, _group_norms(yy)):
            dmg = dm[:, a:b]
            dgs.append(_sum8(dmg * yhat))
            dyh = dmg * gg[:, a:b]
            dys.append(r * (dyh - yhat * jnp.mean(dyh * yhat, axis=-1, keepdims=True)))
        dy = jnp.concatenate(dys, axis=-1)
        dyc_ref[...] = dy[:, 0:D_CONV]
        for hd in range(N_HEADS):
            doh_ref[hd] = dy[:, _head_cols(hd):_head_cols(hd) + HEAD_DIM].astype(BF16)
        prod = dy * yy
        hi = prod.astype(BF16)
        lo = (prod - hi.astype(F32)).astype(BF16)
        dl_ref[...] = _dot(hi, ind_ref[...]) + _dot(lo, ind_ref[...])
        part = jnp.concatenate(dgs, axis=-1)

        @pl.when(i == 0)
        def _():
            acc_ref[...] = pw
            dg_ref[...] = part

        @pl.when(i > 0)
        def _():
            acc_ref[...] += pw
            dg_ref[...] += part

        @pl.when(i == ni - 1)
        def _():
            dw_ref[...] = acc_ref[...].astype(BF16)

    return _pcall(
        body, name="mix_out_bwd", grid=(ni,),
        in_specs=[pl.BlockSpec((tm, d), lambda i: (i, 0)),
                  pl.BlockSpec((tm, D_MIX), lambda i: (i, 0)),
                  pl.BlockSpec((1, D_MIX), lambda i: (0, 0)),
                  pl.BlockSpec((D_MIX, d), lambda i: (0, 0)),
                  pl.BlockSpec((D_MIX, tm), lambda i: (0, i)),
                  pl.BlockSpec((D_MIX, 128), lambda i: (0, 0)),
                  pl.BlockSpec(memory_space=pl.ANY)],
        out_specs=[pl.BlockSpec((tm, D_CONV), lambda i: (i, 0)),
                   pl.BlockSpec((N_HEADS, tm, HEAD_DIM), lambda i: (0, i, 0)),
                   pl.BlockSpec((tm, 128), lambda i: (i, 0)),
                   pl.BlockSpec((D_MIX, d), lambda i: (0, 0)),
                   pl.BlockSpec((8, D_MIX), lambda i: (0, 0))],
        out_shape=[jax.ShapeDtypeStruct((t, D_CONV), F32),
                   jax.ShapeDtypeStruct((N_HEADS, t, HEAD_DIM), BF16),
                   jax.ShapeDtypeStruct((t, 128), F32),
                   jax.ShapeDtypeStruct((D_MIX, d), BF16),
                   jax.ShapeDtypeStruct((8, D_MIX), F32)],
        scratch_shapes=[pltpu.VMEM((D_MIX, d), F32)],
        compiler_params=_params(("arbitrary",)),
    )(dho, y, g, wout, mt, _head_indicator(), dep)


def _mix_proj_bwd(dp, dho, h, g, win_t, n):
    t, d = h.shape
    tm = _tok_block(t)
    ni = t // tm

    def body(dp_ref, dho_ref, h_ref, g_ref, w_ref, n_ref, dh_ref, dw_ref, dg_ref, acc_ref):
        i = pl.program_id(0)
        dpb = dp_ref[...]
        dn = _dot(dpb, w_ref[...])

        @pl.when(i == 0)
        def _():
            acc_ref[...] = jnp.zeros_like(acc_ref)

        acc_ref[...] += _dot_tn(dpb, n_ref[...])
        hh = h_ref[...]
        r = _rstd(hh)
        xhat = hh * r
        dxh = dn * g_ref[...]
        dh_ref[...] = dho_ref[...] + r * (dxh - xhat * jnp.mean(dxh * xhat, axis=-1, keepdims=True))
        part = _sum8(dn * xhat)

        @pl.when(i == 0)
        def _():
            dg_ref[...] = part

        @pl.when(i > 0)
        def _():
            dg_ref[...] += part

        @pl.when(i == ni - 1)
        def _():
            dw_ref[...] = acc_ref[...].astype(BF16)

    return _pcall(
        body, name="mix_proj_bwd", grid=(ni,),
        in_specs=[pl.BlockSpec((tm, D_IN), lambda i: (i, 0)),
                  pl.BlockSpec((tm, d), lambda i: (i, 0)),
                  pl.BlockSpec((tm, d), lambda i: (i, 0)),
                  pl.BlockSpec((1, d), lambda i: (0, 0)),
                  pl.BlockSpec((D_IN, d), lambda i: (0, 0)),
                  pl.BlockSpec((tm, d), lambda i: (i, 0))],
        out_specs=[pl.BlockSpec((tm, d), lambda i: (i, 0)),
                   pl.BlockSpec((D_IN, d), lambda i: (0, 0)),
                   pl.BlockSpec((8, d), lambda i: (0, 0))],
        out_shape=[jax.ShapeDtypeStruct((t, d), F32),
                   jax.ShapeDtypeStruct((D_IN, d), BF16),
                   jax.ShapeDtypeStruct((8, d), F32)],
        scratch_shapes=[pltpu.VMEM((D_IN, d), F32)],
        compiler_params=_params(("arbitrary",)),
    )(dp, dho, h, g, win_t, n)


def _final_loss(h, g, tgt):
    t, d = h.shape
    tm = _tok_block(t)

    def body(h_ref, g_ref, t_ref, dh_ref, ls_ref, dg_ref):
        i = pl.program_id(0)
        hh = h_ref[...]
        r = _rstd(hh)
        xhat = hh * r
        gg = g_ref[...]
        err = xhat * gg - t_ref[...]
        dy = err * (1.0 / d)
        dxh = dy * gg
        dh_ref[...] = r * (dxh - xhat * jnp.mean(dxh * xhat, axis=-1, keepdims=True))
        lpart = _sum8(err * err)
        gpart = _sum8(dy * xhat)

        @pl.when(i == 0)
        def _():
            ls_ref[...] = lpart
            dg_ref[...] = gpart

        @pl.when(i > 0)
        def _():
            ls_ref[...] += lpart
            dg_ref[...] += gpart

    return _pcall(
        body, name="final_loss", grid=(t // tm,),
        in_specs=[pl.BlockSpec((tm, d), lambda i: (i, 0)),
                  pl.BlockSpec((1, d), lambda i: (0, 0)),
                  pl.BlockSpec((tm, d), lambda i: (i, 0))],
        out_specs=[pl.BlockSpec((tm, d), lambda i: (i, 0)),
                   pl.BlockSpec((8, d), lambda i: (0, 0)),
                   pl.BlockSpec((8, d), lambda i: (0, 0))],
        out_shape=[jax.ShapeDtypeStruct((t, d), F32),
                   jax.ShapeDtypeStruct((8, d), F32),
                   jax.ShapeDtypeStruct((8, d), F32)],
        compiler_params=_params(("arbitrary",)),
    )(h, g, tgt)


def _position():
    return lax.axis_index("x"), lax.axis_index("y"), lax.axis_index("c")


def _flip(v, bit):
    return 1 - v if bit else v


def _peer(k):
    x, y, c = _position()
    return _flip(x, k & 4), _flip(y, k & 2), _flip(c, k & 1)


def _slot(px, py, pc):
    return 4 * px + 2 * py + pc


def _handshake(peers):
    barrier = pltpu.get_barrier_semaphore()
    for peer in peers:
        pl.semaphore_signal(barrier, inc=1, device_id=peer, device_id_type=MESH)
    pl.semaphore_wait(barrier, len(peers))


def _sequencer_call(body, name, collective_id, out_type, scratch_types, operands):
    return pl.kernel(
        body, out_type=out_type, mesh=plsc.ScalarSubcoreMesh(axis_name="sequencer", num_cores=1), name=name,
        scratch_types=scratch_types, compiler_params=pltpu.CompilerParams(collective_id=collective_id),
    )(*operands)


def _all_gather(shards, name, collective_id):
    nt = len(shards)

    def body(*refs):
        xs = refs[:nt]
        outs = refs[nt:2 * nt]
        send_sems, recv_sems, local_sems = refs[2 * nt:]
        x, y, c = _position()
        me, sibling = (x, y, c), (x, y, 1 - c)
        xn, yn, dg = (1 - x, y), (x, 1 - y), (1 - x, 1 - y)
        pick = lambda a, b: (jnp.where(c == 0, a[0], b[0]), jnp.where(c == 0, a[1], b[1]))
        relay_from, relay_to = pick(yn, xn), pick(xn, yn)
        _handshake([sibling, (*xn, c), (*yn, c)])

        def copy(t, k, block, to, src=None):
            dst = outs[t].at[_slot(*block)]
            return pltpu.make_async_remote_copy(
                src_ref=dst if src is None else src, dst_ref=dst,
                send_sem=send_sems.at[t, k], recv_sem=recv_sems.at[t, k],
                device_id=to, device_id_type=MESH)

        mine = [pltpu.make_async_copy(xs[t], outs[t].at[_slot(*me)], local_sems.at[t]) for t in range(nt)]
        for cp in mine:
            cp.start()
        sent = []
        for t in range(nt):
            sent += [copy(t, 0, me, sibling, src=xs[t]), copy(t, 1, me, (*xn, c), src=xs[t]),
                     copy(t, 2, me, (*yn, c), src=xs[t])]
        for cp in sent:
            cp.start()
        for t in range(nt):
            copy(t, 1, (*xn, c), me).wait_recv()
            copy(t, 2, (*yn, c), me).wait_recv()
            passed = [copy(t, 3, (*relay_from, c), (*relay_to, c)),
                      copy(t, 4, (*xn, c), sibling), copy(t, 5, (*yn, c), sibling)]
            for cp in passed:
                cp.start()
            sent += passed
        for t in range(nt):
            copy(t, 3, (*dg, c), me).wait_recv()
            fwd = copy(t, 6, (*dg, c), sibling)
            fwd.start()
            sent.append(fwd)
        for t in range(nt):
            copy(t, 0, sibling, me).wait_recv()
            for k, chip in ((4, xn), (5, yn), (6, dg)):
                copy(t, k, (*chip, 1 - c), me).wait_recv()
        for cp in sent:
            cp.wait_send()
        for cp in mine:
            cp.wait()

    return _sequencer_call(
        body, name, collective_id,
        out_type=[jax.ShapeDtypeStruct((N_DEV,) + s.shape, s.dtype) for s in shards],
        scratch_types=[pltpu.SemaphoreType.DMA((nt, 7)), pltpu.SemaphoreType.DMA((nt, 7)),
                       pltpu.SemaphoreType.DMA((nt,))],
        operands=shards)


def _scatter_copy(srcs, lands, send_sems, recv_sems, t, k):
    peer = _peer(k)
    return pltpu.make_async_remote_copy(
        src_ref=srcs[t].at[_slot(*peer)], dst_ref=lands[t].at[k],
        send_sem=send_sems.at[t * (N_DEV - 1) + k - 1], recv_sem=recv_sems.at[t * (N_DEV - 1) + k - 1],
        device_id=peer, device_id_type=MESH)


def _scatter_start(partials, name):
    nt = len(partials)

    def body(*refs):
        srcs, lands = refs[:nt], refs[nt:2 * nt]
        send_sems, recv_sems = refs[2 * nt], refs[2 * nt + 1]
        token = refs[-1]
        for k in range(1, N_DEV):
            for t in range(nt):
                _scatter_copy(srcs, lands, send_sems, recv_sems, t, k).start()
        token[...] = jnp.zeros_like(token)

    hbm = pl.BlockSpec(memory_space=pltpu.HBM)
    sem = pl.BlockSpec(memory_space=pltpu.SEMAPHORE)
    shapes = [pltpu.HBM(p.shape, p.dtype) for p in partials]
    lands = [pltpu.with_memory_space_constraint(lax.empty(p.shape, p.dtype), pltpu.HBM) for p in partials]
    srcs = [pltpu.with_memory_space_constraint(p, pltpu.HBM) for p in partials]
    out = _pcall(
        body, name=name,
        out_shape=[pltpu.SemaphoreType.DMA((nt * (N_DEV - 1),))] * 2 + shapes + shapes
        + [jax.ShapeDtypeStruct((8, 128), F32)],
        in_specs=[hbm] * (2 * nt),
        out_specs=[sem, sem] + [hbm] * (2 * nt) + [pl.BlockSpec(memory_space=pltpu.VMEM)],
        input_output_aliases={i: 2 + i for i in range(2 * nt)},
        compiler_params=pltpu.CompilerParams(has_side_effects=pltpu.SideEffectType.DATAFLOW_SIDE_EFFECTING),
    )(*srcs, *lands)
    return (nt, name, out[:-1]), out[-1]


def _scatter_wait(state, after):
    nt, name, (send_sems, recv_sems, *thru) = state

    def body(*refs):
        srcs, lands = refs[:nt], refs[nt:2 * nt]
        send_sems, recv_sems = refs[2 * nt], refs[2 * nt + 1]
        for k in range(1, N_DEV):
            for t in range(nt):
                copy = _scatter_copy(srcs, lands, send_sems, recv_sems, t, k)
                copy.wait_send()
                copy.wait_recv()

    hbm = pl.BlockSpec(memory_space=pltpu.HBM)
    sem = pl.BlockSpec(memory_space=pltpu.SEMAPHORE)
    out = _pcall(
        body, name=name + "_wait",
        out_shape=[pltpu.HBM(a.shape, a.dtype) for a in thru],
        in_specs=[hbm] * (2 * nt) + [sem, sem, pl.BlockSpec(memory_space=pl.ANY)],
        out_specs=[hbm] * (2 * nt),
        input_output_aliases={i: i for i in range(2 * nt)},
        compiler_params=pltpu.CompilerParams(has_side_effects=pltpu.SideEffectType.DATAFLOW_SIDE_EFFECTING),
    )(*thru, send_sems, recv_sems, after)
    return out[:nt], out[nt:]


def _all_reduce_rows(v, dep):
    nv, _, w = v.shape

    def body(v_ref, dep_ref, out_ref, mine_ref, gath_ref, send_sems, recv_sems):
        x, y, c = _position()
        me = _slot(x, y, c)
        mine_ref[...] = jnp.sum(v_ref[...], axis=1)

        def copy(k):
            return pltpu.make_async_remote_copy(
                src_ref=mine_ref, dst_ref=gath_ref.at[me],
                send_sem=send_sems.at[k - 1], recv_sem=recv_sems.at[k - 1],
                device_id=_peer(k), device_id_type=MESH)

        def arrival(k):
            return pltpu.make_async_remote_copy(
                src_ref=mine_ref, dst_ref=gath_ref.at[_slot(*_peer(k))],
                send_sem=send_sems.at[k - 1], recv_sem=recv_sems.at[k - 1],
                device_id=_peer(k), device_id_type=MESH)

        sent = [copy(k) for k in range(1, N_DEV)]
        for cp in sent:
            cp.start()
        gath_ref[me] = mine_ref[...]
        for k in range(1, N_DEV):
            arrival(k).wait_recv()
        for cp in sent:
            cp.wait_send()
        total = gath_ref[0]
        for s in range(1, N_DEV):
            total = total + gath_ref[s]
        out_ref[...] = total

    vmem = pl.BlockSpec(memory_space=pltpu.VMEM)
    return _pcall(
        body, name="all_reduce_rows",
        in_specs=[vmem, pl.BlockSpec(memory_space=pl.ANY)], out_specs=vmem,
        out_shape=jax.ShapeDtypeStruct((nv, w), F32),
        scratch_shapes=[pltpu.VMEM((nv, w), F32), pltpu.VMEM((N_DEV, nv, w), F32),
                        pltpu.SemaphoreType.DMA((7,)), pltpu.SemaphoreType.DMA((7,))],
    )(v, dep)


def _adamw_math(w, g, m, v):
    m2 = ADAM_B1 * m + (1.0 - ADAM_B1) * g
    v2 = ADAM_B2 * v + (1.0 - ADAM_B2) * (g * g)
    m_hat = m2 / (1.0 - ADAM_B1 ** ADAM_STEP)
    v_hat = v2 / (1.0 - ADAM_B2 ** ADAM_STEP)
    delta = -ADAM_LR * (m_hat / (jnp.sqrt(v_hat) + ADAM_EPS) + ADAM_WD * w)
    return delta, m2, v2


def _row_block(r):
    for cand in (256, 176, 128):
        if r % cand == 0:
            return cand
    return r


def _adamw_sharded(me, grads, w, m, v, dep, first_layer=0, prev=None):
    nl = len(grads)
    _, r, c = grads[0][1].shape
    tr = _row_block(r)
    nr = r // tr
    prev = list(prev or ())

    def body(me_ref, *refs):
        grad_refs = refs[:2 * nl]
        w_ref, m_ref, v_ref = refs[2 * nl:2 * nl + 3]
        g_ref, d_ref, m2_ref, v2_ref = refs[-4:]
        layer = pl.program_id(0)

        def total(own_ref, land_ref):
            acc = own_ref[0].astype(F32)
            for k in range(1, N_DEV):
                acc = acc + land_ref[k].astype(F32)
            return acc

        g = total(grad_refs[0], grad_refs[1])
        for k in range(1, nl):
            g = jnp.where(layer == k, total(grad_refs[2 * k], grad_refs[2 * k + 1]), g)
        delta, m2, v2 = _adamw_math(w_ref[0], g, m_ref[0], v_ref[0])
        g_ref[0] = g
        d_ref[0] = delta
        m2_ref[0] = m2
        v2_ref[0] = v2

    def grad_pair_specs(k):
        def rows(l, i):
            return jnp.where(l == k, i, jnp.where(l < k, 0, nr - 1))
        return [pl.BlockSpec((1, tr, c), lambda l, i, me_ref: (me_ref[0], rows(l, i), 0)),
                pl.BlockSpec((N_DEV, tr, c), lambda l, i, me_ref: (0, rows(l, i), 0))]

    grad_specs = [spec for k in range(nl) for spec in grad_pair_specs(k)]
    shard = pl.BlockSpec((1, tr, c), lambda l, i, me_ref: (first_layer + l, i, 0))
    untouched = pl.BlockSpec(memory_space=pl.ANY)
    out = jax.ShapeDtypeStruct(w.shape, F32)
    first_prev = 1 + 2 * nl + 4
    return _pcall(
        body, name="adamw_sharded",
        grid_spec=pltpu.PrefetchScalarGridSpec(
            num_scalar_prefetch=1, grid=(nl, nr),
            in_specs=grad_specs + [shard, shard, shard] + [untouched] * (1 + len(prev)),
            out_specs=[shard, shard, shard, shard]),
        out_shape=[out, out, out, out],
        input_output_aliases={first_prev + k: k for k in range(len(prev))},
        compiler_params=_params(("arbitrary", "arbitrary")),
    )(me, *[a for pair in grads for a in pair], w, m, v, dep, *prev)


def _adamw_small(w, g, m, v):
    def body(w_ref, g_ref, m_ref, v_ref, d_ref, m2_ref, v2_ref):
        delta, m2, v2 = _adamw_math(w_ref[...], g_ref[...], m_ref[...], v_ref[...])
        d_ref[...] = delta
        m2_ref[...] = m2
        v2_ref[...] = v2

    spec = pl.BlockSpec(w.shape, lambda i: (0, 0))
    out = jax.ShapeDtypeStruct(w.shape, F32)
    return _pcall(
        body, name="adamw_small", grid=(1,),
        in_specs=[spec] * 4, out_specs=[spec] * 3, out_shape=[out] * 3,
        compiler_params=_params(("arbitrary",)),
    )(w, g, m, v)


def _pack(arrs):
    flat = jnp.concatenate([a.reshape(-1) for a in arrs])
    n = flat.shape[0]
    rows = -(-n // 1024) * 8
    return jnp.pad(flat, (0, rows * 128 - n)).reshape(rows, 128)


def _unpack(packed, like):
    flat = packed.reshape(-1)
    out, off = [], 0
    for a in like:
        out.append(flat[off:off + a.size].reshape(a.shape))
        off += a.size
    return out


def kernel(x, mem, g_ffn1, w_ffn1_up, w_ffn1_down, g_mix, w_in, conv_w, sinks, g_mem, w_mem_kv, g_grp, w_out, g_ffn2, w_ffn2_up, w_ffn2_down, g_final, loss_target, m_g_ffn1, m_w_ffn1_up, m_w_ffn1_down, m_g_mix, m_w_in, m_conv_w, m_sinks, m_g_mem, m_w_mem_kv, m_g_grp, m_w_out, m_g_ffn2, m_w_ffn2_up, m_w_ffn2_down, m_g_final, v_g_ffn1, v_w_ffn1_up, v_w_ffn1_down, v_g_mix, v_w_in, v_conv_w, v_sinks, v_g_mem, v_w_mem_kv, v_g_grp, v_w_out, v_g_ffn2, v_w_ffn2_up, v_w_ffn2_down, v_g_final):
    depth = g_ffn1.shape[0]
    t, d = x.shape[1], x.shape[2]
    width = max(d, D_MIX)
    me = _slot(*_position())
    conv_shard = conv_w.shape[2]

    xin, memin, tgt = x[0], mem[0], loss_target[0]

    conv_tile = jnp.zeros((depth * 8, 128), F32).at[:, :conv_shard].set(
        jnp.pad(conv_w, ((0, 0), (0, 8 - conv_w.shape[1]), (0, 0))).reshape(depth * 8, conv_shard))
    tr = lambda a: jnp.swapaxes(a, -1, -2)
    bf = lambda a: a.astype(BF16)
    weights = []
    collective_id = 0
    for l in range(depth):
        groups = [[bf(tr(w_ffn1_up[l])), bf(w_ffn1_down[l])] + ([conv_tile] if l == 0 else []),
                  [bf(tr(w_in[l])), bf(w_mem_kv[l]), bf(w_out[l])],
                  [bf(tr(w_ffn2_up[l])), bf(w_ffn2_down[l])]]
        full = []
        for gi, shards in enumerate(groups):
            full.append(_all_gather(shards, f"all_gather_l{l}_g{gi}", collective_id))
            collective_id += 1
        if l == 0:
            conv_full = full[0][2].reshape(N_DEV, depth, 8, 128)[:, :, :3, :conv_shard]
            conv_full = conv_full.transpose(1, 2, 0, 3).reshape(depth, 3, N_DEV * conv_shard)
        weights.append(dict(
            up1=full[0][0].reshape(2, -1, d), dn1=full[0][1].reshape(-1, d),
            win=full[1][0].reshape(D_IN, d), wkv=full[1][1].reshape(d, 2 * D_MEMQ), wout=full[1][2].reshape(D_MIX, d),
            up2=full[2][0].reshape(2, -1, d), dn2=full[2][1].reshape(-1, d)))

    row = lambda a: a.reshape(1, -1)
    bias_key = _bias_table()

    h = xin
    saved = []
    for l in range(depth):
        wl = weights[l]
        s = dict(h0=h)
        h, s["gu1"], s["n1"] = _ffn_fwd(h, row(g_ffn1[l]), wl["up1"], wl["dn1"])
        s["h1"] = h
        s["p"], s["n_mix"], s["qh"] = _mix_proj_fwd(h, row(g_mix[l]), wl["win"])
        s["mkv"], s["nt_mem"] = _memkv_fwd(memin, row(g_mem[l]), wl["wkv"], s["p"])
        s["y"], s["lse"] = _mix_core_fwd(s["p"], s["qh"], s["mkv"], conv_full[l], row(sinks[l]), bias_key)
        h, s["mt"] = _mix_out_fwd(s["y"], h, row(g_grp[l]), wl["wout"])
        s["h2"] = h
        h, s["gu2"], s["n2"] = _ffn_fwd(h, row(g_ffn2[l]), wl["up2"], wl["dn2"])
        saved.append(s)

    dh, loss_part, dg_final = _final_loss(h, row(g_final), tgt)

    small = {}
    dep = loss_part

    def reduce_small(after):
        def lanes(a):
            return jnp.pad(a, ((0, 0), (0, width - a.shape[1])))

        def first_row(a):
            return lanes(jnp.pad(a, ((0, 8 - a.shape[0]), (0, 0))))

        vec_names = ["g_ffn1", "g_mix", "g_mem", "g_grp", "g_ffn2", "sinks"]
        tiles = [lanes(small[n, l]) for n in vec_names for l in range(depth)]
        tiles += [first_row(small["conv_w", l][k:k + 1]) for l in range(depth) for k in range(3)]
        tiles.append(lanes(dg_final))
        n_real = len(tiles)
        tiles.append(lanes(loss_part))
        tiles += [jnp.zeros((8, width), F32)] * (-len(tiles) % 8)
        summed = _all_reduce_rows(jnp.stack(tiles), after)
        loss_all = 0.5 * jnp.sum(summed[n_real]) / d

        def vec(n, wd):
            return jnp.stack([summed[vec_names.index(n) * depth + l, :wd] for l in range(depth)])

        conv_base = len(vec_names) * depth
        conv_grad = jnp.stack([jnp.stack([summed[conv_base + 3 * l + k, :D_CONV] for k in range(3)])
                               for l in range(depth)])
        grads_small = {
            "g_ffn1": vec("g_ffn1", d), "g_mix": vec("g_mix", d), "g_mem": vec("g_mem", d),
            "g_grp": vec("g_grp", D_MIX), "g_ffn2": vec("g_ffn2", d), "sinks": vec("sinks", N_SWA_HEADS),
            "conv_w": lax.dynamic_slice_in_dim(conv_grad, me * conv_shard, conv_shard, axis=2),
            "g_final": summed[n_real - 1, :d],
        }
        small_w = [("g_ffn1", g_ffn1, m_g_ffn1, v_g_ffn1), ("g_mix", g_mix, m_g_mix, v_g_mix),
                   ("conv_w", conv_w, m_conv_w, v_conv_w), ("sinks", sinks, m_sinks, v_sinks),
                   ("g_mem", g_mem, m_g_mem, v_g_mem), ("g_grp", g_grp, m_g_grp, v_g_grp),
                   ("g_ffn2", g_ffn2, m_g_ffn2, v_g_ffn2), ("g_final", g_final, m_g_final, v_g_final)]
        like = [w for _, w, _, _ in small_w]
        packed = _adamw_small(_pack(like), _pack([grads_small[n] for n, _, _, _ in small_w]),
                              _pack([m for _, _, m, _ in small_w]), _pack([v for _, _, _, v in small_w]))
        updated = {n: (grads_small[n], dl, m2, v2)
                   for (n, _, _, _), dl, m2, v2 in zip(small_w, *[_unpack(pk, like) for pk in packed])}
        return loss_all, updated, packed[0]

    started = []

    def scatter(names, partials, label):
        state, token = _scatter_start(partials, f"scatter_grads_{label}")
        started.append((names, state))
        return token

    for l in reversed(range(depth)):
        wl, s = weights[l], saved[l]
        dh, agu, dyb, small["g_ffn2", l] = _ffn_bwd_act(dh, s["h2"], row(g_ffn2[l]), s["gu2"], wl["up2"], wl["dn2"], dep)
        ddn2 = _ffn_bwd_w(agu, 2, 1, dyb, agu, f"ffn_bwd_w_down_l{l}_ffn2").reshape(N_DEV, -1, d)
        dup2 = _ffn_bwd_w(agu, 0, 2, s["n2"], ddn2, f"ffn_bwd_w_up_l{l}_ffn2").reshape(N_DEV, -1, d)
        dep = scatter([("w_ffn2_up", l), ("w_ffn2_down", l)], [dup2, ddn2], f"l{l}_ffn2")
        dyconv, doh, delta, dwout, small["g_grp", l] = _mix_out_bwd(dh, s["y"], row(g_grp[l]), wl["wout"], s["mt"], dep)
        dp, dmkv, small["conv_w", l], small["sinks", l] = _mix_core_bwd(
            s["p"], s["qh"], dyconv, doh, delta, s["lse"], s["mkv"], conv_full[l], row(sinks[l]), bias_key)
        dwkv, small["g_mem", l] = _memkv_bwd(dmkv, memin, row(g_mem[l]), wl["wkv"], s["nt_mem"])
        dh, dwin, small["g_mix", l] = _mix_proj_bwd(dp, dh, s["h1"], row(g_mix[l]), wl["win"], s["n_mix"])
        dep = scatter([("w_in", l), ("w_mem_kv", l), ("w_out", l)],
                      [dwin.reshape(N_DEV, -1, d), dwkv.reshape(N_DEV, -1, 2 * D_MEMQ), dwout.reshape(N_DEV, -1, d)],
                      f"l{l}_mix")
        dh, agu, dyb, small["g_ffn1", l] = _ffn_bwd_act(dh, s["h0"], row(g_ffn1[l]), s["gu1"], wl["up1"], wl["dn1"], dep)
        ddn1 = _ffn_bwd_w(agu, 2, 1, dyb, agu, f"ffn_bwd_w_down_l{l}_ffn1").reshape(N_DEV, -1, d)
        if l > 0:
            dup1 = _ffn_bwd_w(agu, 0, 2, s["n1"], ddn1, f"ffn_bwd_w_up_l{l}_ffn1").reshape(N_DEV, -1, d)
            dep = scatter([("w_ffn1_up", l), ("w_ffn1_down", l)], [dup1, ddn1], f"l{l}_ffn1")
        else:
            dep = scatter([("w_ffn1_down", l)], [ddn1], f"l{l}_ffn1_down")
            dup1 = _ffn_bwd_w(agu, 0, 2, s["n1"], dep, f"ffn_bwd_w_up_l{l}_ffn1").reshape(N_DEV, -1, d)
            dep = scatter([("w_ffn1_up", l)], [dup1], f"l{l}_ffn1_up")
    grad_x = dh[None]

    big = {"w_ffn2_up": (w_ffn2_up, m_w_ffn2_up, v_w_ffn2_up, True), "w_ffn2_down": (w_ffn2_down, m_w_ffn2_down, v_w_ffn2_down, False),
           "w_in": (w_in, m_w_in, v_w_in, True), "w_mem_kv": (w_mem_kv, m_w_mem_kv, v_w_mem_kv, False),
           "w_out": (w_out, m_w_out, v_w_out, False), "w_ffn1_up": (w_ffn1_up, m_w_ffn1_up, v_w_ffn1_up, True),
           "w_ffn1_down": (w_ffn1_down, m_w_ffn1_down, v_w_ffn1_down, False)}
    me_index = jnp.reshape(me, (1,)).astype(jnp.int32)
    sharded, landed, begun = {}, {}, {}
    by_layer = {name for names, _ in started[-2:] for name, _ in names}

    def finish(groups, after):
        for names, state in groups:
            owns, lands = _scatter_wait(state, after)
            for key, own, land in zip(names, owns, lands):
                landed[key] = (own, land)
            after = lands[0]
            for name, l in names:
                w, m, v, transposed = big[name]
                fix = tr if transposed else (lambda a: a)
                if name in by_layer:
                    res = _adamw_sharded(me_index, [landed[name, l]], fix(w), fix(m), fix(v), after, l, begun.get(name))
                    done = name in begun
                    begun[name] = res
                elif all((name, k) in landed for k in range(depth)):
                    res = _adamw_sharded(me_index, [landed[name, k] for k in range(depth)], fix(w), fix(m), fix(v), after)
                    done = True
                else:
                    continue
                if done:
                    sharded[name] = tuple(fix(r) for r in res)
                after = res[0]
        return after

    loss, small_out, dep = reduce_small(finish(started[:-2], dep))
    finish(started[-2:], dep)

    order = ["g_ffn1", "w_ffn1_up", "w_ffn1_down", "g_mix", "w_in", "conv_w", "sinks", "g_mem", "w_mem_kv", "g_grp",
             "w_out", "g_ffn2", "w_ffn2_up", "w_ffn2_down", "g_final"]
    results = {**sharded, **small_out}
    outs = [loss, grad_x]
    for part in range(4):
        outs += [results[n][part] for n in order]
    return tuple(outs)
```

```python
import numpy as np
import jax
import jax.numpy as jnp
from jax import lax
from jax.experimental import pallas as pl
from jax.experimental.pallas import tpu as pltpu
from jax.experimental.pallas import tpu_sc as plsc

F32 = jnp.float32
BF16 = jnp.bfloat16

N_DEV = 8
EPS = 1e-6
N_SWA_HEADS = 8
N_SWA_KV = 2
SWA_GROUP = N_SWA_HEADS // N_SWA_KV
HEAD_DIM = 64
N_MEM_HEADS = 4
D_CONV = 256
BLOCK = 128
D_SWA = N_SWA_HEADS * HEAD_DIM
D_KV = N_SWA_KV * HEAD_DIM
D_MEMQ = N_MEM_HEADS * HEAD_DIM
D_MIX = D_CONV + D_SWA + D_MEMQ
D_IN = 3 * D_CONV + D_SWA + 2 * D_KV + D_MEMQ
COL_BG, COL_CG, COL_U = 0, D_CONV, 2 * D_CONV
COL_Q = 3 * D_CONV
COL_K = COL_Q + D_SWA
COL_V = COL_K + D_KV
COL_QM = COL_V + D_KV
MIX_GROUPS = ((0, D_CONV), (D_CONV, D_CONV + D_SWA), (D_CONV + D_SWA, D_MIX))
SLOPES = tuple(2.0 ** (-8.0 * (i + 1) / N_SWA_HEADS) for i in range(N_SWA_HEADS))
SCALE = HEAD_DIM ** -0.5
NEG = -1e30

ADAM_LR = 0.001
ADAM_B1 = 0.9
ADAM_B2 = 0.999
ADAM_EPS = 1e-08
ADAM_WD = 0.01
ADAM_STEP = 10

V7X_VMEM_BYTES = 64 * 1024 * 1024
VMEM_LIMIT = (V7X_VMEM_BYTES * 3) // 4
MESH = pl.DeviceIdType.MESH


def _pcall(body, **kw):
    return pl.pallas_call(body, **kw)


def _params(sem=None, vmem=VMEM_LIMIT):
    return pltpu.CompilerParams(dimension_semantics=sem, vmem_limit_bytes=vmem)


def _dot(a, b):
    return lax.dot_general(a, b, (((1,), (0,)), ((), ())), preferred_element_type=F32)


def _dot_nt(a, b):
    return lax.dot_general(a, b, (((1,), (1,)), ((), ())), preferred_element_type=F32)


def _dot_tn(a, b):
    return lax.dot_general(a, b, (((0,), (0,)), ((), ())), preferred_element_type=F32)


def _rstd(x):
    return lax.rsqrt(jnp.mean(x * x, axis=-1, keepdims=True) + EPS)


def _sigmoid(x):
    return 1.0 / (1.0 + jnp.exp(-x))


def _sum8(x):
    r, w = x.shape
    return jnp.sum(x.reshape(r // 8, 8, w), axis=0)


def _tok_block(t, rows=512):
    return min(rows, t)


def _feat_block(f):
    return f // (N_DEV // 2)


def _ffn_fwd(h, g, wup_t, wdn):
    t, d = h.shape
    f = wdn.shape[0]
    tm, tf = _tok_block(t), _feat_block(f)
    ni, nj = t // tm, f // tf

    def body(h_ref, g_ref, wup_ref, wdn_ref, ho_ref, gu_ref, n_ref, nt_ref, acc_ref):
        j = pl.program_id(1)

        @pl.when(j == 0)
        def _():
            hh = h_ref[...]
            n = hh * _rstd(hh) * g_ref[...]
            n_ref[...] = n.astype(BF16)
            nt_ref[...] = n.T.astype(BF16)
            acc_ref[...] = jnp.zeros_like(acc_ref)

        nt = nt_ref[...]
        gate = _dot(wup_ref[0], nt)
        up = _dot(wup_ref[1], nt)
        gu_ref[0] = gate.astype(BF16)
        gu_ref[1] = up.astype(BF16)
        a = gate * _sigmoid(gate) * up
        acc_ref[...] += _dot_tn(a.astype(BF16), wdn_ref[...])

        @pl.when(j == nj - 1)
        def _():
            ho_ref[...] = h_ref[...] + 0.5 * acc_ref[...]

    return _pcall(
        body, name="ffn_fwd", grid=(ni, nj),
        in_specs=[pl.BlockSpec((tm, d), lambda i, j: (i, 0)),
                  pl.BlockSpec((1, d), lambda i, j: (0, 0)),
                  pl.BlockSpec((2, tf, d), lambda i, j: (0, j, 0)),
                  pl.BlockSpec((tf, d), lambda i, j: (j, 0))],
        out_specs=[pl.BlockSpec((tm, d), lambda i, j: (i, 0)),
                   pl.BlockSpec((2, tf, tm), lambda i, j: (0, j, i)),
                   pl.BlockSpec((tm, d), lambda i, j: (i, 0))],
        out_shape=[jax.ShapeDtypeStruct((t, d), F32),
                   jax.ShapeDtypeStruct((2, f, t), BF16),
                   jax.ShapeDtypeStruct((t, d), BF16)],
        scratch_shapes=[pltpu.VMEM((d, tm), BF16), pltpu.VMEM((tm, d), F32)],
        compiler_params=_params(("parallel", "arbitrary")),
    )(h, g, wup_t, wdn)


def _ffn_bwd_act(dho, h, g, gu, wup_t, wdn, dep):
    t, d = h.shape
    f = wdn.shape[0]
    tm, tf = _tok_block(t), _feat_block(f)
    ni, nj = t // tm, f // tf

    def body(dho_ref, h_ref, g_ref, gu_ref, wup_ref, wdn_ref, dep_ref, dh_ref, agu_ref, dyb_ref, dg_ref, dyt_ref, acc_ref):
        i = pl.program_id(0)
        j = pl.program_id(1)

        @pl.when(j == 0)
        def _():
            dy0 = 0.5 * dho_ref[...]
            dyb_ref[...] = dy0.astype(BF16)
            dyt_ref[...] = dy0.T.astype(BF16)
            acc_ref[...] = jnp.zeros_like(acc_ref)

        da = _dot(wdn_ref[...], dyt_ref[...]).astype(BF16)
        gate = gu_ref[0]
        up = gu_ref[1]
        sg = _sigmoid(gate)
        silu = gate * sg
        dgate = da * up * (sg * (1.0 + gate * (1.0 - sg)))
        dup = da * silu
        agu_ref[0] = dgate
        agu_ref[1] = dup
        agu_ref[2] = silu * up
        acc_ref[...] += _dot_tn(dgate, wup_ref[0])
        acc_ref[...] += _dot_tn(dup, wup_ref[1])

        @pl.when(j == nj - 1)
        def _():
            hh = h_ref[...]
            r = _rstd(hh)
            xhat = hh * r
            dnf = acc_ref[...]
            dxh = dnf * g_ref[...]
            dh_ref[...] = dho_ref[...] + r * (dxh - xhat * jnp.mean(dxh * xhat, axis=-1, keepdims=True))
            part = _sum8(dnf * xhat)

            @pl.when(i == 0)
            def _():
                dg_ref[...] = part

            @pl.when(i > 0)
            def _():
                dg_ref[...] += part

    return _pcall(
        body, name="ffn_bwd_act", grid=(ni, nj),
        in_specs=[pl.BlockSpec((tm, d), lambda i, j: (i, 0)),
                  pl.BlockSpec((tm, d), lambda i, j: (i, 0)),
                  pl.BlockSpec((1, d), lambda i, j: (0, 0)),
                  pl.BlockSpec((2, tf, tm), lambda i, j: (0, j, i)),
                  pl.BlockSpec((2, tf, d), lambda i, j: (0, j, 0)),
                  pl.BlockSpec((tf, d), lambda i, j: (j, 0)),
                  pl.BlockSpec(memory_space=pl.ANY)],
        out_specs=[pl.BlockSpec((tm, d), lambda i, j: (i, 0)),
                   pl.BlockSpec((3, tf, tm), lambda i, j: (0, j, i)),
                   pl.BlockSpec((tm, d), lambda i, j: (i, 0)),
                   pl.BlockSpec((8, d), lambda i, j: (0, 0))],
        out_shape=[jax.ShapeDtypeStruct((t, d), F32),
                   jax.ShapeDtypeStruct((3, f, t), BF16),
                   jax.ShapeDtypeStruct((t, d), BF16),
                   jax.ShapeDtypeStruct((8, d), F32)],
        scratch_shapes=[pltpu.VMEM((d, tm), BF16), pltpu.VMEM((tm, d), F32)],
        compiler_params=_params(("arbitrary", "arbitrary")),
    )(dho, h, g, gu, wup_t, wdn, dep)


def _ffn_bwd_w(agu, first, count, rhs, dep, name):
    _, f, t = agu.shape
    d = rhs.shape[1]
    tm, tf = _tok_block(t, 2048), _feat_block(f)
    ni, nj = t // tm, f // tf

    def body(lhs_ref, rhs_ref, dep_ref, dw_ref, acc_ref):
        i = pl.program_id(1)
        @pl.when(i == 0)
        def _():
            acc_ref[...] = jnp.zeros_like(acc_ref)

        rb = rhs_ref[...]
        for k in range(count):
            acc_ref[k] += _dot(lhs_ref[k], rb)

        @pl.when(i == ni - 1)
        def _():
            dw_ref[...] = acc_ref[...].astype(BF16)

    return _pcall(
        body, name=name, grid=(nj, ni),
        in_specs=[pl.BlockSpec((count, tf, tm), lambda j, i: (first // count, j, i)),
                  pl.BlockSpec((tm, d), lambda j, i: (i, 0)),
                  pl.BlockSpec(memory_space=pl.ANY)],
        out_specs=pl.BlockSpec((count, tf, d), lambda j, i: (0, j, 0)),
        out_shape=jax.ShapeDtypeStruct((count, f, d), BF16),
        scratch_shapes=[pltpu.VMEM((count, tf, d), F32)],
        compiler_params=_params(("parallel", "arbitrary")),
    )(agu, rhs, dep)


N_HEADS = N_SWA_HEADS + N_MEM_HEADS


def _q_col(hd):
    return COL_Q + HEAD_DIM * hd if hd < N_SWA_HEADS else COL_QM + HEAD_DIM * (hd - N_SWA_HEADS)


def _mix_proj_fwd(h, g, win_t):
    t, d = h.shape
    tm = _tok_block(t)

    def body(h_ref, g_ref, win_ref, p_ref, n_ref, qh_ref):
        hh = h_ref[...]
        n = (hh * _rstd(hh) * g_ref[...]).astype(BF16)
        n_ref[...] = n
        proj = _dot_nt(n, win_ref[...])
        p_ref[...] = proj.astype(BF16)
        for hd in range(N_HEADS):
            c0 = _q_col(hd)
            qh_ref[hd] = (proj[:, c0:c0 + HEAD_DIM] * SCALE).astype(BF16)

    return _pcall(
        body, name="mix_proj_fwd", grid=(t // tm,),
        in_specs=[pl.BlockSpec((tm, d), lambda i: (i, 0)),
                  pl.BlockSpec((1, d), lambda i: (0, 0)),
                  pl.BlockSpec((D_IN, d), lambda i: (0, 0))],
        out_specs=[pl.BlockSpec((tm, D_IN), lambda i: (i, 0)),
                   pl.BlockSpec((tm, d), lambda i: (i, 0)),
                   pl.BlockSpec((N_HEADS, tm, HEAD_DIM), lambda i: (0, i, 0))],
        out_shape=[jax.ShapeDtypeStruct((t, D_IN), BF16), jax.ShapeDtypeStruct((t, d), BF16),
                   jax.ShapeDtypeStruct((N_HEADS, t, HEAD_DIM), BF16)],
        compiler_params=_params(("parallel",)),
    )(h, g, win_t)


def _memkv_fwd(mem, g, wkv, dep):
    m, d = mem.shape

    def body(mem_ref, g_ref, w_ref, dep_ref, mkv_ref, nt_ref):
        mm = mem_ref[...]
        n = mm * _rstd(mm) * g_ref[...]
        nt_ref[...] = n.T.astype(BF16)
        mkv_ref[...] = _dot(n.astype(BF16), w_ref[...]).astype(BF16)

    return _pcall(
        body, name="memkv_fwd", grid=(1,),
        in_specs=[pl.BlockSpec((m, d), lambda i: (0, 0)),
                  pl.BlockSpec((1, d), lambda i: (0, 0)),
                  pl.BlockSpec((d, 2 * D_MEMQ), lambda i: (0, 0)),
                  pl.BlockSpec(memory_space=pl.ANY)],
        out_specs=[pl.BlockSpec((m, 2 * D_MEMQ), lambda i: (0, 0)),
                   pl.BlockSpec((d, m), lambda i: (0, 0))],
        out_shape=[jax.ShapeDtypeStruct((m, 2 * D_MEMQ), BF16), jax.ShapeDtypeStruct((d, m), BF16)],
        compiler_params=_params(("arbitrary",)),
    )(mem, g, wkv, dep)


def _memkv_bwd(dmkv, mem, g, wkv, nt):
    m, d = mem.shape

    def body(dmkv_ref, mem_ref, g_ref, w_ref, nt_ref, dw_ref, dg_ref):
        db = dmkv_ref[...].astype(BF16)
        dw_ref[...] = _dot(nt_ref[...], db).astype(BF16)
        dn = _dot_nt(db, w_ref[...])
        mm = mem_ref[...]
        dg_ref[...] = _sum8(dn * (mm * _rstd(mm)))

    return _pcall(
        body, name="memkv_bwd", grid=(1,),
        in_specs=[pl.BlockSpec((m, 2 * D_MEMQ), lambda i: (0, 0)),
                  pl.BlockSpec((m, d), lambda i: (0, 0)),
                  pl.BlockSpec((1, d), lambda i: (0, 0)),
                  pl.BlockSpec((d, 2 * D_MEMQ), lambda i: (0, 0)),
                  pl.BlockSpec((d, m), lambda i: (0, 0))],
        out_specs=[pl.BlockSpec((d, 2 * D_MEMQ), lambda i: (0, 0)),
                   pl.BlockSpec((8, d), lambda i: (0, 0))],
        out_shape=[jax.ShapeDtypeStruct((d, 2 * D_MEMQ), BF16), jax.ShapeDtypeStruct((8, d), F32)],
        compiler_params=_params(("arbitrary",)),
    )(dmkv, mem, g, wkv, nt)


def _shift_rows(v, k, edge_rows, row):
    out = pltpu.roll(v, k, 0)
    for r in range(k):
        out = jnp.where(row == r, edge_rows[r], out)
    return out


def _shift_rows_up(v, k, edge_rows, row):
    n = v.shape[0]
    out = pltpu.roll(v, n - k, 0)
    for r in range(k):
        out = jnp.where(row == n - k + r, edge_rows[r], out)
    return out


GROUP_ROWS = SWA_GROUP * BLOCK
BIAS_CUR, BIAS_PREV, BIAS_NONE = 0, 1, 2


def _bias_table():
    tq = np.arange(BLOCK)[:, None]
    sk = np.arange(BLOCK)[None, :]
    slopes = np.asarray(SLOPES, np.float32)[:, None, None]
    cur = np.where(tq >= sk, -slopes * (tq - sk).astype(np.float32), NEG)
    prev = np.where(sk > tq, -slopes * (tq + BLOCK - sk).astype(np.float32), NEG)
    none = np.full_like(cur, NEG)
    tok = np.stack([cur, prev, none]).astype(np.float32).reshape(3, N_SWA_KV, GROUP_ROWS, BLOCK)
    return jnp.asarray(np.ascontiguousarray(tok.transpose(0, 1, 3, 2)))


def _head_cols(hd):
    return D_CONV + HEAD_DIM * hd


def _mix_core_fwd(p, qh, mkv, convw, sinks, bias_key):
    t = p.shape[0]
    m = mkv.shape[0]
    nb = t // BLOCK

    def body(sk_ref, pc_ref, pkv_ref, ppc_ref, ppu_ref, qh_ref, mkv_ref, cw_ref, bc_ref, bp_ref, y_ref, l_ref):
        i = pl.program_id(0)
        prevf = (i > 0).astype(F32)
        row = lax.broadcasted_iota(jnp.int32, (BLOCK, D_CONV), 0)

        bg = pc_ref[:, COL_BG:COL_BG + D_CONV].astype(F32)
        cg = pc_ref[:, COL_CG:COL_CG + D_CONV].astype(F32)
        u = pc_ref[:, COL_U:COL_U + D_CONV].astype(F32)
        vv = cg * u
        pvv = ppc_ref[...].astype(F32) * ppu_ref[...].astype(F32) * prevf
        vv1 = _shift_rows(vv, 1, [pvv[15:16]], row)
        vv2 = _shift_rows(vv, 2, [pvv[14:15], pvv[15:16]], row)
        w = cw_ref[...]
        y_ref[:, 0:D_CONV] = bg * (w[0:1] * vv2 + w[1:2] * vv1 + w[2:3] * vv)

        head_row = lax.broadcasted_iota(jnp.int32, (128, BLOCK), 0)
        lse_t = jnp.zeros((128, BLOCK), F32)
        for kv in range(N_SWA_KV):
            heads = range(kv * SWA_GROUP, (kv + 1) * SWA_GROUP)
            kc = pc_ref[:, COL_K + HEAD_DIM * kv:COL_K + HEAD_DIM * (kv + 1)]
            vc = pc_ref[:, COL_V + HEAD_DIM * kv:COL_V + HEAD_DIM * (kv + 1)]
            kp = pkv_ref[:, HEAD_DIM * kv:HEAD_DIM * (kv + 1)]
            vp = pkv_ref[:, D_KV + HEAD_DIM * kv:D_KV + HEAD_DIM * (kv + 1)]
            qg = qh_ref[kv * SWA_GROUP:(kv + 1) * SWA_GROUP].reshape(GROUP_ROWS, HEAD_DIM)
            sc = _dot_nt(kc, qg) + bc_ref[0, kv]
            sp = _dot_nt(kp, qg) + bp_ref[0, kv]
            sink = jnp.concatenate([jnp.full((1, BLOCK), sk_ref[0, hd], F32) for hd in heads], axis=1)
            mx = jnp.maximum(jnp.max(jnp.maximum(sc, sp), axis=0, keepdims=True), sink)
            ec = jnp.exp(sc - mx)
            ep = jnp.exp(sp - mx)
            den = jnp.sum(ec + ep, axis=0, keepdims=True) + jnp.exp(sink - mx)
            ot = (_dot_tn(vc, ec.astype(BF16)) + _dot_tn(vp, ep.astype(BF16))) / den
            lse = mx + jnp.log(den)
            for gi, hd in enumerate(heads):
                span = slice(gi * BLOCK, (gi + 1) * BLOCK)
                y_ref[:, _head_cols(hd):_head_cols(hd) + HEAD_DIM] = ot[:, span].T
                lse_t = jnp.where(head_row == hd, lse[:, span], lse_t)

        for hm in range(N_MEM_HEADS):
            hd = N_SWA_HEADS + hm
            mk = mkv_ref[:, HEAD_DIM * hm:HEAD_DIM * (hm + 1)]
            mv = mkv_ref[:, D_MEMQ + HEAD_DIM * hm:D_MEMQ + HEAD_DIM * (hm + 1)]
            s = _dot_nt(mk, qh_ref[hd])
            mx = jnp.max(s, axis=0, keepdims=True)
            e = jnp.exp(s - mx)
            den = jnp.sum(e, axis=0, keepdims=True)
            y_ref[:, _head_cols(hd):_head_cols(hd) + HEAD_DIM] = (_dot_tn(mv, e.astype(BF16)) / den).T
            lse_t = jnp.where(head_row == hd, mx + jnp.log(den), lse_t)
        l_ref[...] = lse_t.T

    kv_col = COL_K // (2 * D_KV)
    bias_block = (1, N_SWA_KV, BLOCK, GROUP_ROWS)
    return _pcall(
        body, name="mix_core_fwd", grid=(nb,),
        in_specs=[pl.BlockSpec(memory_space=pltpu.SMEM),
                  pl.BlockSpec((BLOCK, D_IN), lambda i: (i, 0)),
                  pl.BlockSpec((BLOCK, 2 * D_KV), lambda i: (jnp.maximum(i - 1, 0), kv_col)),
                  pl.BlockSpec((16, D_CONV), lambda i: (jnp.maximum(i * (BLOCK // 16) - 1, 0), COL_CG // D_CONV)),
                  pl.BlockSpec((16, D_CONV), lambda i: (jnp.maximum(i * (BLOCK // 16) - 1, 0), COL_U // D_CONV)),
                  pl.BlockSpec((N_HEADS, BLOCK, HEAD_DIM), lambda i: (0, i, 0)),
                  pl.BlockSpec((m, 2 * D_MEMQ), lambda i: (0, 0)),
                  pl.BlockSpec((3, D_CONV), lambda i: (0, 0)),
                  pl.BlockSpec(bias_block, lambda i: (BIAS_CUR, 0, 0, 0)),
                  pl.BlockSpec(bias_block, lambda i: (jnp.where(i == 0, BIAS_NONE, BIAS_PREV), 0, 0, 0))],
        out_specs=[pl.BlockSpec((BLOCK, D_MIX), lambda i: (i, 0)),
                   pl.BlockSpec((BLOCK, 128), lambda i: (i, 0))],
        out_shape=[jax.ShapeDtypeStruct((t, D_MIX), F32), jax.ShapeDtypeStruct((t, 128), F32)],
        compiler_params=_params(("parallel",)),
    )(sinks, p, p, p, p, qh, mkv, convw, bias_key, bias_key)


def _mix_core_bwd(p, qh, dyconv, doh, delta, lse, mkv, convw, sinks, bias_key):
    t = p.shape[0]
    m = mkv.shape[0]
    nb = t // BLOCK

    def body(sk_ref, pc_ref, pkv_ref, ppc_ref, ppu_ref, pnb_ref, dyc_ref, dyn_ref, qc_ref, qn_ref, doc_ref, don_ref,
             dlc_ref, dln_ref, lc_ref, ln_ref, mkv_ref, cw_ref, bp_ref, bct_ref, bnt_ref,
             dp_ref, dmkv_ref, dcw_ref, dsk_ref):
        i = pl.program_id(0)
        prevf = (i > 0).astype(F32)
        nextf = (i < nb - 1).astype(F32)
        row = lax.broadcasted_iota(jnp.int32, (BLOCK, D_CONV), 0)

        @pl.when(i == 0)
        def _():
            dmkv_ref[...] = jnp.zeros_like(dmkv_ref)
            dcw_ref[...] = jnp.zeros_like(dcw_ref)
            dsk_ref[...] = jnp.zeros_like(dsk_ref)

        bg = pc_ref[:, COL_BG:COL_BG + D_CONV].astype(F32)
        cg = pc_ref[:, COL_CG:COL_CG + D_CONV].astype(F32)
        u = pc_ref[:, COL_U:COL_U + D_CONV].astype(F32)
        vv = cg * u
        pvv = ppc_ref[...].astype(F32) * ppu_ref[...].astype(F32) * prevf
        vv1 = _shift_rows(vv, 1, [pvv[15:16]], row)
        vv2 = _shift_rows(vv, 2, [pvv[14:15], pvv[15:16]], row)
        w = cw_ref[...]
        yconv = w[0:1] * vv2 + w[1:2] * vv1 + w[2:3] * vv
        dyo = dyc_ref[...]
        dyc = dyo * bg
        nxt = dyn_ref[...] * pnb_ref[...].astype(F32) * nextf
        d1 = _shift_rows_up(dyc, 1, [nxt[0:1]], row)
        d2 = _shift_rows_up(dyc, 2, [nxt[0:1], nxt[1:2]], row)
        dvv = w[2:3] * dyc + w[1:2] * d1 + w[0:1] * d2
        dp_ref[:, COL_BG:COL_BG + D_CONV] = (dyo * yconv).astype(BF16)
        dp_ref[:, COL_CG:COL_CG + D_CONV] = (dvv * u).astype(BF16)
        dp_ref[:, COL_U:COL_U + D_CONV] = (dvv * cg).astype(BF16)
        dcw_ref[0:1, :] += jnp.sum(dyc * vv2, axis=0, keepdims=True)
        dcw_ref[1:2, :] += jnp.sum(dyc * vv1, axis=0, keepdims=True)
        dcw_ref[2:3, :] += jnp.sum(dyc * vv, axis=0, keepdims=True)

        lse_t, dl_t = lc_ref[...].T, dlc_ref[...].T
        lse_nt, dl_nt = ln_ref[...].T, dln_ref[...].T

        def stack_rows(tile_t, heads):
            return jnp.concatenate([tile_t[hd:hd + 1, :] for hd in heads], axis=1)

        lane8 = jnp.where(lax.broadcasted_iota(jnp.int32, (8, 128), 0) == 0,
                          lax.broadcasted_iota(jnp.int32, (8, 128), 1), -1)
        dsk = jnp.zeros((8, 128), F32)
        for kv in range(N_SWA_KV):
            heads = range(kv * SWA_GROUP, (kv + 1) * SWA_GROUP)
            kc = pc_ref[:, COL_K + HEAD_DIM * kv:COL_K + HEAD_DIM * (kv + 1)]
            vc = pc_ref[:, COL_V + HEAD_DIM * kv:COL_V + HEAD_DIM * (kv + 1)]
            kp = pkv_ref[:, HEAD_DIM * kv:HEAD_DIM * (kv + 1)]
            vp = pkv_ref[:, D_KV + HEAD_DIM * kv:D_KV + HEAD_DIM * (kv + 1)]
            qg = qc_ref[kv * SWA_GROUP:(kv + 1) * SWA_GROUP].reshape(GROUP_ROWS, HEAD_DIM)
            dog = doc_ref[kv * SWA_GROUP:(kv + 1) * SWA_GROUP].reshape(GROUP_ROWS, HEAD_DIM)
            qn = qn_ref[kv * SWA_GROUP:(kv + 1) * SWA_GROUP].reshape(GROUP_ROWS, HEAD_DIM)
            don = don_ref[kv * SWA_GROUP:(kv + 1) * SWA_GROUP].reshape(GROUP_ROWS, HEAD_DIM)
            lse_row, dl_row = stack_rows(lse_t, heads), stack_rows(dl_t, heads)
            ptp = jnp.exp(_dot_nt(kp, qg) + bp_ref[0, kv] - lse_row)
            dstp = (ptp * (_dot_nt(vp, dog) - dl_row)).astype(BF16)
            dq = _dot_tn(dstp, kp)
            pt = jnp.exp(_dot_nt(kc, qg) + bct_ref[0, kv] - lse_row)
            dst = (pt * (_dot_nt(vc, dog) - dl_row)).astype(BF16)
            dv = _dot(pt.astype(BF16), dog)
            dk = _dot(dst, qg)
            dq = dq + _dot_tn(dst, kc)
            ptn = jnp.exp(_dot_nt(kc, qn) + bnt_ref[0, kv] - stack_rows(lse_nt, heads))
            dstn = (ptn * (_dot_nt(vc, don) - stack_rows(dl_nt, heads))).astype(BF16)
            dv = dv + _dot(ptn.astype(BF16), don)
            dk = dk + _dot(dstn, qn)
            dp_ref[:, COL_K + HEAD_DIM * kv:COL_K + HEAD_DIM * (kv + 1)] = dk.astype(BF16)
            dp_ref[:, COL_V + HEAD_DIM * kv:COL_V + HEAD_DIM * (kv + 1)] = dv.astype(BF16)
            sink = jnp.concatenate([jnp.full((1, BLOCK), sk_ref[0, hd], F32) for hd in heads], axis=1)
            sink_term = jnp.exp(sink - lse_row) * dl_row
            for gi, hd in enumerate(heads):
                span = slice(gi * BLOCK, (gi + 1) * BLOCK)
                dp_ref[:, _q_col(hd):_q_col(hd) + HEAD_DIM] = (dq[span] * SCALE).astype(BF16)
                dsk = dsk + jnp.where(lane8 == hd, -jnp.sum(sink_term[:, span], axis=1, keepdims=True), 0.0)
        dsk_ref[...] += dsk

        for hm in range(N_MEM_HEADS):
            hd = N_SWA_HEADS + hm
            qm, dom = qc_ref[hd], doc_ref[hd]
            mk = mkv_ref[:, HEAD_DIM * hm:HEAD_DIM * (hm + 1)]
            mv = mkv_ref[:, D_MEMQ + HEAD_DIM * hm:D_MEMQ + HEAD_DIM * (hm + 1)]
            pt = jnp.exp(_dot_nt(mk, qm) - lse_t[hd:hd + 1, :])
            dst = (pt * (_dot_nt(mv, dom) - dl_t[hd:hd + 1, :])).astype(BF16)
            dp_ref[:, _q_col(hd):_q_col(hd) + HEAD_DIM] = (_dot_tn(dst, mk) * SCALE).astype(BF16)
            dmkv_ref[:, HEAD_DIM * hm:HEAD_DIM * (hm + 1)] += _dot(dst, qm)
            dmkv_ref[:, D_MEMQ + HEAD_DIM * hm:D_MEMQ + HEAD_DIM * (hm + 1)] += _dot(pt.astype(BF16), dom)

    cur = lambda i: (i, 0)
    const = lambda i: (0, 0)
    rows16 = BLOCK // 16
    last16 = t // 16 - 1
    before = lambda col: (lambda i: (jnp.maximum(i * rows16 - 1, 0), col))
    after = lambda i: (jnp.minimum((i + 1) * rows16, last16), 0)
    heads_cur = lambda i: (0, i, 0)
    heads_next = lambda i: (0, jnp.minimum(i + 1, nb - 1), 0)
    stat_next = lambda i: (jnp.minimum(i + 1, nb - 1), 0)
    key_block = (1, N_SWA_KV, BLOCK, GROUP_ROWS)
    head_block = (N_HEADS, BLOCK, HEAD_DIM)
    return _pcall(
        body, name="mix_core_bwd", grid=(nb,),
        in_specs=[pl.BlockSpec(memory_space=pltpu.SMEM),
                  pl.BlockSpec((BLOCK, D_IN), cur),
                  pl.BlockSpec((BLOCK, 2 * D_KV), lambda i: (jnp.maximum(i - 1, 0), COL_K // (2 * D_KV))),
                  pl.BlockSpec((16, D_CONV), before(COL_CG // D_CONV)),
                  pl.BlockSpec((16, D_CONV), before(COL_U // D_CONV)),
                  pl.BlockSpec((16, D_CONV), after),
                  pl.BlockSpec((BLOCK, D_CONV), cur),
                  pl.BlockSpec((16, D_CONV), after),
                  pl.BlockSpec(head_block, heads_cur), pl.BlockSpec(head_block, heads_next),
                  pl.BlockSpec(head_block, heads_cur), pl.BlockSpec(head_block, heads_next),
                  pl.BlockSpec((BLOCK, 128), cur), pl.BlockSpec((BLOCK, 128), stat_next),
                  pl.BlockSpec((BLOCK, 128), cur), pl.BlockSpec((BLOCK, 128), stat_next),
                  pl.BlockSpec((m, 2 * D_MEMQ), const),
                  pl.BlockSpec((3, D_CONV), const),
                  pl.BlockSpec(key_block, lambda i: (jnp.where(i == 0, BIAS_NONE, BIAS_PREV), 0, 0, 0)),
                  pl.BlockSpec(key_block, lambda i: (BIAS_CUR, 0, 0, 0)),
                  pl.BlockSpec(key_block, lambda i: (jnp.where(i == nb - 1, BIAS_NONE, BIAS_PREV), 0, 0, 0))],
        out_specs=[pl.BlockSpec((BLOCK, D_IN), cur),
                   pl.BlockSpec((m, 2 * D_MEMQ), const),
                   pl.BlockSpec((8, D_CONV), const),
                   pl.BlockSpec((8, 128), const)],
        out_shape=[jax.ShapeDtypeStruct((t, D_IN), BF16),
                   jax.ShapeDtypeStruct((m, 2 * D_MEMQ), F32),
                   jax.ShapeDtypeStruct((8, D_CONV), F32),
                   jax.ShapeDtypeStruct((8, 128), F32)],
        compiler_params=_params(("arbitrary",)),
    )(sinks, p, p, p, p, p, dyconv, dyconv, qh, qh, doh, doh, delta, delta, lse, lse, mkv, convw,
      bias_key, bias_key, bias_key)


def _group_norms(y):
    out = []
    for a, b in MIX_GROUPS:
        ys = y[:, a:b]
        r = _rstd(ys)
        out.append((ys * r, r))
    return out


def _mix_out_fwd(y, h, g, wout):
    t, d = h.shape
    tm = _tok_block(t)

    def body(y_ref, h_ref, g_ref, w_ref, ho_ref, mt_ref):
        yhat = jnp.concatenate([yh for yh, _ in _group_norms(y_ref[...])], axis=-1)
        mixed = yhat * g_ref[...]
        mt_ref[...] = mixed.T.astype(BF16)
        ho_ref[...] = h_ref[...] + _dot(mixed.astype(BF16), w_ref[...])

    return _pcall(
        body, name="mix_out_fwd", grid=(t // tm,),
        in_specs=[pl.BlockSpec((tm, D_MIX), lambda i: (i, 0)),
                  pl.BlockSpec((tm, d), lambda i: (i, 0)),
                  pl.BlockSpec((1, D_MIX), lambda i: (0, 0)),
                  pl.BlockSpec((D_MIX, d), lambda i: (0, 0))],
        out_specs=[pl.BlockSpec((tm, d), lambda i: (i, 0)),
                   pl.BlockSpec((D_MIX, tm), lambda i: (0, i))],
        out_shape=[jax.ShapeDtypeStruct((t, d), F32), jax.ShapeDtypeStruct((D_MIX, t), BF16)],
        compiler_params=_params(("parallel",)),
    )(y, h, g, wout)


def _head_indicator():
    ind = np.zeros((D_MIX, 128), np.float32)
    for hd in range(N_HEADS):
        ind[_head_cols(hd):_head_cols(hd) + HEAD_DIM, hd] = 1.0
    return jnp.asarray(ind, BF16)


def _mix_out_bwd(dho, y, g, wout, mt, dep):
    t, d = dho.shape
    tm = _tok_block(t)
    ni = t // tm

    def body(dho_ref, y_ref, g_ref, w_ref, mt_ref, ind_ref, dep_ref, dyc_ref, doh_ref, dl_ref, dw_ref, dg_ref, acc_ref):
        i = pl.program_id(0)
        dhb = dho_ref[...].astype(BF16)
        dm = _dot_nt(dhb, w_ref[...])
        pw = _dot(mt_ref[...], dhb)
        gg = g_ref[...]
        yy = y_ref[...]
        dys = []
        dgs = []
        for (a, b), (yhat, r) in zip(MIX_GROUPS, _group_norms(yy)):
            dmg = dm[:, a:b]
            dgs.append(_sum8(dmg * yhat))
            dyh = dmg * gg[:, a:b]
            dys.append(r * (dyh - yhat * jnp.mean(dyh * yhat, axis=-1, keepdims=True)))
        dy = jnp.concatenate(dys, axis=-1)
        dyc_ref[...] = dy[:, 0:D_CONV]
        for hd in range(N_HEADS):
            doh_ref[hd] = dy[:, _head_cols(hd):_head_cols(hd) + HEAD_DIM].astype(BF16)
        prod = dy * yy
        hi = prod.astype(BF16)
        lo = (prod - hi.astype(F32)).astype(BF16)
        dl_ref[...] = _dot(hi, ind_ref[...]) + _dot(lo, ind_ref[...])
        part = jnp.concatenate(dgs, axis=-1)

        @pl.when(i == 0)
        def _():
            acc_ref[...] = pw
            dg_ref[...] = part

        @pl.when(i > 0)
        def _():
            acc_ref[...] += pw
            dg_ref[...] += part

        @pl.when(i == ni - 1)
        def _():
            dw_ref[...] = acc_ref[...].astype(BF16)

    return _pcall(
        body, name="mix_out_bwd", grid=(ni,),
        in_specs=[pl.BlockSpec((tm, d), lambda i: (i, 0)),
                  pl.BlockSpec((tm, D_MIX), lambda i: (i, 0)),
                  pl.BlockSpec((1, D_MIX), lambda i: (0, 0)),
                  pl.BlockSpec((D_MIX, d), lambda i: (0, 0)),
                  pl.BlockSpec((D_MIX, tm), lambda i: (0, i)),
                  pl.BlockSpec((D_MIX, 128), lambda i: (0, 0)),
                  pl.BlockSpec(memory_space=pl.ANY)],
        out_specs=[pl.BlockSpec((tm, D_CONV), lambda i: (i, 0)),
                   pl.BlockSpec((N_HEADS, tm, HEAD_DIM), lambda i: (0, i, 0)),
                   pl.BlockSpec((tm, 128), lambda i: (i, 0)),
                   pl.BlockSpec((D_MIX, d), lambda i: (0, 0)),
                   pl.BlockSpec((8, D_MIX), lambda i: (0, 0))],
        out_shape=[jax.ShapeDtypeStruct((t, D_CONV), F32),
                   jax.ShapeDtypeStruct((N_HEADS, t, HEAD_DIM), BF16),
                   jax.ShapeDtypeStruct((t, 128), F32),
                   jax.ShapeDtypeStruct((D_MIX, d), BF16),
                   jax.ShapeDtypeStruct((8, D_MIX), F32)],
        scratch_shapes=[pltpu.VMEM((D_MIX, d), F32)],
        compiler_params=_params(("arbitrary",)),
    )(dho, y, g, wout, mt, _head_indicator(), dep)


def _mix_proj_bwd(dp, dho, h, g, win_t, n):
    t, d = h.shape
    tm = _tok_block(t)
    ni = t // tm

    def body(dp_ref, dho_ref, h_ref, g_ref, w_ref, n_ref, dh_ref, dw_ref, dg_ref, acc_ref):
        i = pl.program_id(0)
        dpb = dp_ref[...]
        dn = _dot(dpb, w_ref[...])

        @pl.when(i == 0)
        def _():
            acc_ref[...] = jnp.zeros_like(acc_ref)

        acc_ref[...] += _dot_tn(dpb, n_ref[...])
        hh = h_ref[...]
        r = _rstd(hh)
        xhat = hh * r
        dxh = dn * g_ref[...]
        dh_ref[...] = dho_ref[...] + r * (dxh - xhat * jnp.mean(dxh * xhat, axis=-1, keepdims=True))
        part = _sum8(dn * xhat)

        @pl.when(i == 0)
        def _():
            dg_ref[...] = part

        @pl.when(i > 0)
        def _():
            dg_ref[...] += part

        @pl.when(i == ni - 1)
        def _():
            dw_ref[...] = acc_ref[...].astype(BF16)

    return _pcall(
        body, name="mix_proj_bwd", grid=(ni,),
        in_specs=[pl.BlockSpec((tm, D_IN), lambda i: (i, 0)),
                  pl.BlockSpec((tm, d), lambda i: (i, 0)),
                  pl.BlockSpec((tm, d), lambda i: (i, 0)),
                  pl.BlockSpec((1, d), lambda i: (0, 0)),
                  pl.BlockSpec((D_IN, d), lambda i: (0, 0)),
                  pl.BlockSpec((tm, d), lambda i: (i, 0))],
        out_specs=[pl.BlockSpec((tm, d), lambda i: (i, 0)),
                   pl.BlockSpec((D_IN, d), lambda i: (0, 0)),
                   pl.BlockSpec((8, d), lambda i: (0, 0))],
        out_shape=[jax.ShapeDtypeStruct((t, d), F32),
                   jax.ShapeDtypeStruct((D_IN, d), BF16),
                   jax.ShapeDtypeStruct((8, d), F32)],
        scratch_shapes=[pltpu.VMEM((D_IN, d), F32)],
        compiler_params=_params(("arbitrary",)),
    )(dp, dho, h, g, win_t, n)


def _final_loss(h, g, tgt):
    t, d = h.shape
    tm = _tok_block(t)

    def body(h_ref, g_ref, t_ref, dh_ref, ls_ref, dg_ref):
        i = pl.program_id(0)
        hh = h_ref[...]
        r = _rstd(hh)
        xhat = hh * r
        gg = g_ref[...]
        err = xhat * gg - t_ref[...]
        dy = err * (1.0 / d)
        dxh = dy * gg
        dh_ref[...] = r * (dxh - xhat * jnp.mean(dxh * xhat, axis=-1, keepdims=True))
        lpart = _sum8(err * err)
        gpart = _sum8(dy * xhat)

        @pl.when(i == 0)
        def _():
            ls_ref[...] = lpart
            dg_ref[...] = gpart

        @pl.when(i > 0)
        def _():
            ls_ref[...] += lpart
            dg_ref[...] += gpart

    return _pcall(
        body, name="final_loss", grid=(t // tm,),
        in_specs=[pl.BlockSpec((tm, d), lambda i: (i, 0)),
                  pl.BlockSpec((1, d), lambda i: (0, 0)),
                  pl.BlockSpec((tm, d), lambda i: (i, 0))],
        out_specs=[pl.BlockSpec((tm, d), lambda i: (i, 0)),
                   pl.BlockSpec((8, d), lambda i: (0, 0)),
                   pl.BlockSpec((8, d), lambda i: (0, 0))],
        out_shape=[jax.ShapeDtypeStruct((t, d), F32),
                   jax.ShapeDtypeStruct((8, d), F32),
                   jax.ShapeDtypeStruct((8, d), F32)],
        compiler_params=_params(("arbitrary",)),
    )(h, g, tgt)


def _position():
    return lax.axis_index("x"), lax.axis_index("y"), lax.axis_index("c")


def _flip(v, bit):
    return 1 - v if bit else v


def _peer(k):
    x, y, c = _position()
    return _flip(x, k & 4), _flip(y, k & 2), _flip(c, k & 1)


def _slot(px, py, pc):
    return 4 * px + 2 * py + pc


def _handshake(peers):
    barrier = pltpu.get_barrier_semaphore()
    for peer in peers:
        pl.semaphore_signal(barrier, inc=1, device_id=peer, device_id_type=MESH)
    pl.semaphore_wait(barrier, len(peers))


def _sequencer_call(body, name, collective_id, out_type, scratch_types, operands):
    return pl.kernel(
        body, out_type=out_type, mesh=plsc.ScalarSubcoreMesh(axis_name="sequencer", num_cores=1), name=name,
        scratch_types=scratch_types, compiler_params=pltpu.CompilerParams(collective_id=collective_id),
    )(*operands)


def _all_gather(shards, name, collective_id):
    nt = len(shards)

    def body(*refs):
        xs = refs[:nt]
        outs = refs[nt:2 * nt]
        send_sems, recv_sems, local_sems = refs[2 * nt:]
        x, y, c = _position()
        me, sibling = (x, y, c), (x, y, 1 - c)
        xn, yn, dg = (1 - x, y), (x, 1 - y), (1 - x, 1 - y)
        pick = lambda a, b: (jnp.where(c == 0, a[0], b[0]), jnp.where(c == 0, a[1], b[1]))
        relay_from, relay_to = pick(yn, xn), pick(xn, yn)
        _handshake([sibling, (*xn, c), (*yn, c)])

        def copy(t, k, block, to, src=None):
            dst = outs[t].at[_slot(*block)]
            return pltpu.make_async_remote_copy(
                src_ref=dst if src is None else src, dst_ref=dst,
                send_sem=send_sems.at[t, k], recv_sem=recv_sems.at[t, k],
                device_id=to, device_id_type=MESH)

        mine = [pltpu.make_async_copy(xs[t], outs[t].at[_slot(*me)], local_sems.at[t]) for t in range(nt)]
        for cp in mine:
            cp.start()
        sent = []
        for t in range(nt):
            sent += [copy(t, 0, me, sibling, src=xs[t]), copy(t, 1, me, (*xn, c), src=xs[t]),
                     copy(t, 2, me, (*yn, c), src=xs[t])]
        for cp in sent:
            cp.start()
        for t in range(nt):
            copy(t, 1, (*xn, c), me).wait_recv()
            copy(t, 2, (*yn, c), me).wait_recv()
            passed = [copy(t, 3, (*relay_from, c), (*relay_to, c)),
                      copy(t, 4, (*xn, c), sibling), copy(t, 5, (*yn, c), sibling)]
            for cp in passed:
                cp.start()
            sent += passed
        for t in range(nt):
            copy(t, 3, (*dg, c), me).wait_recv()
            fwd = copy(t, 6, (*dg, c), sibling)
            fwd.start()
            sent.append(fwd)
        for t in range(nt):
            copy(t, 0, sibling, me).wait_recv()
            for k, chip in ((4, xn), (5, yn), (6, dg)):
                copy(t, k, (*chip, 1 - c), me).wait_recv()
        for cp in sent:
            cp.wait_send()
        for cp in mine:
            cp.wait()

    return _sequencer_call(
        body, name, collective_id,
        out_type=[jax.ShapeDtypeStruct((N_DEV,) + s.shape, s.dtype) for s in shards],
        scratch_types=[pltpu.SemaphoreType.DMA((nt, 7)), pltpu.SemaphoreType.DMA((nt, 7)),
                       pltpu.SemaphoreType.DMA((nt,))],
        operands=shards)


def _scatter_copy(srcs, lands, send_sems, recv_sems, t, k):
    peer = _peer(k)
    return pltpu.make_async_remote_copy(
        src_ref=srcs[t].at[_slot(*peer)], dst_ref=lands[t].at[k],
        send_sem=send_sems.at[t * (N_DEV - 1) + k - 1], recv_sem=recv_sems.at[t * (N_DEV - 1) + k - 1],
        device_id=peer, device_id_type=MESH)


def _scatter_start(partials, name):
    nt = len(partials)

    def body(*refs):
        srcs, lands = refs[:nt], refs[nt:2 * nt]
        send_sems, recv_sems = refs[2 * nt], refs[2 * nt + 1]
        token = refs[-1]
        for k in range(1, N_DEV):
            for t in range(nt):
                _scatter_copy(srcs, lands, send_sems, recv_sems, t, k).start()
        token[...] = jnp.zeros_like(token)

    hbm = pl.BlockSpec(memory_space=pltpu.HBM)
    sem = pl.BlockSpec(memory_space=pltpu.SEMAPHORE)
    shapes = [pltpu.HBM(p.shape, p.dtype) for p in partials]
    lands = [pltpu.with_memory_space_constraint(lax.empty(p.shape, p.dtype), pltpu.HBM) for p in partials]
    srcs = [pltpu.with_memory_space_constraint(p, pltpu.HBM) for p in partials]
    out = _pcall(
        body, name=name,
        out_shape=[pltpu.SemaphoreType.DMA((nt * (N_DEV - 1),))] * 2 + shapes + shapes
        + [jax.ShapeDtypeStruct((8, 128), F32)],
        in_specs=[hbm] * (2 * nt),
        out_specs=[sem, sem] + [hbm] * (2 * nt) + [pl.BlockSpec(memory_space=pltpu.VMEM)],
        input_output_aliases={i: 2 + i for i in range(2 * nt)},
        compiler_params=pltpu.CompilerParams(has_side_effects=pltpu.SideEffectType.DATAFLOW_SIDE_EFFECTING),
    )(*srcs, *lands)
    return (nt, name, out[:-1]), out[-1]


def _scatter_wait(state, after):
    nt, name, (send_sems, recv_sems, *thru) = state

    def body(*refs):
        srcs, lands = refs[:nt], refs[nt:2 * nt]
        send_sems, recv_sems = refs[2 * nt], refs[2 * nt + 1]
        for k in range(1, N_DEV):
            for t in range(nt):
                copy = _scatter_copy(srcs, lands, send_sems, recv_sems, t, k)
                copy.wait_send()
                copy.wait_recv()

    hbm = pl.BlockSpec(memory_space=pltpu.HBM)
    sem = pl.BlockSpec(memory_space=pltpu.SEMAPHORE)
    out = _pcall(
        body, name=name + "_wait",
        out_shape=[pltpu.HBM(a.shape, a.dtype) for a in thru],
        in_specs=[hbm] * (2 * nt) + [sem, sem, pl.BlockSpec(memory_space=pl.ANY)],
        out_specs=[hbm] * (2 * nt),
        input_output_aliases={i: i for i in range(2 * nt)},
        compiler_params=pltpu.CompilerParams(has_side_effects=pltpu.SideEffectType.DATAFLOW_SIDE_EFFECTING),
    )(*thru, send_sems, recv_sems, after)
    return out[:nt], out[nt:]


def _all_reduce_rows(v, dep):
    nv, _, w = v.shape

    def body(v_ref, dep_ref, out_ref, mine_ref, gath_ref, send_sems, recv_sems):
        x, y, c = _position()
        me = _slot(x, y, c)
        mine_ref[...] = jnp.sum(v_ref[...], axis=1)

        def copy(k):
            return pltpu.make_async_remote_copy(
                src_ref=mine_ref, dst_ref=gath_ref.at[me],
                send_sem=send_sems.at[k - 1], recv_sem=recv_sems.at[k - 1],
                device_id=_peer(k), device_id_type=MESH)

        def arrival(k):
            return pltpu.make_async_remote_copy(
                src_ref=mine_ref, dst_ref=gath_ref.at[_slot(*_peer(k))],
                send_sem=send_sems.at[k - 1], recv_sem=recv_sems.at[k - 1],
                device_id=_peer(k), device_id_type=MESH)

        sent = [copy(k) for k in range(1, N_DEV)]
        for cp in sent:
            cp.start()
        gath_ref[me] = mine_ref[...]
        for k in range(1, N_DEV):
            arrival(k).wait_recv()
        for cp in sent:
            cp.wait_send()
        total = gath_ref[0]
        for s in range(1, N_DEV):
            total = total + gath_ref[s]
        out_ref[...] = total

    vmem = pl.BlockSpec(memory_space=pltpu.VMEM)
    return _pcall(
        body, name="all_reduce_rows",
        in_specs=[vmem, pl.BlockSpec(memory_space=pl.ANY)], out_specs=vmem,
        out_shape=jax.ShapeDtypeStruct((nv, w), F32),
        scratch_shapes=[pltpu.VMEM((nv, w), F32), pltpu.VMEM((N_DEV, nv, w), F32),
                        pltpu.SemaphoreType.DMA((7,)), pltpu.SemaphoreType.DMA((7,))],
    )(v, dep)


def _adamw_math(w, g, m, v):
    m2 = ADAM_B1 * m + (1.0 - ADAM_B1) * g
    v2 = ADAM_B2 * v + (1.0 - ADAM_B2) * (g * g)
    m_hat = m2 / (1.0 - ADAM_B1 ** ADAM_STEP)
    v_hat = v2 / (1.0 - ADAM_B2 ** ADAM_STEP)
    delta = -ADAM_LR * (m_hat / (jnp.sqrt(v_hat) + ADAM_EPS) + ADAM_WD * w)
    return delta, m2, v2


def _row_block(r):
    for cand in (256, 176, 128):
        if r % cand == 0:
            return cand
    return r


def _adamw_sharded(me, grads, w, m, v, dep, first_layer=0, prev=None):
    nl = len(grads)
    _, r, c = grads[0][1].shape
    tr = _row_block(r)
    nr = r // tr
    prev = list(prev or ())

    def body(me_ref, *refs):
        grad_refs = refs[:2 * nl]
        w_ref, m_ref, v_ref = refs[2 * nl:2 * nl + 3]
        g_ref, d_ref, m2_ref, v2_ref = refs[-4:]
        layer = pl.program_id(0)

        def total(own_ref, land_ref):
            acc = own_ref[0].astype(F32)
            for k in range(1, N_DEV):
                acc = acc + land_ref[k].astype(F32)
            return acc

        g = total(grad_refs[0], grad_refs[1])
        for k in range(1, nl):
            g = jnp.where(layer == k, total(grad_refs[2 * k], grad_refs[2 * k + 1]), g)
        delta, m2, v2 = _adamw_math(w_ref[0], g, m_ref[0], v_ref[0])
        g_ref[0] = g
        d_ref[0] = delta
        m2_ref[0] = m2
        v2_ref[0] = v2

    def grad_pair_specs(k):
        def rows(l, i):
            return jnp.where(l == k, i, jnp.where(l < k, 0, nr - 1))
        return [pl.BlockSpec((1, tr, c), lambda l, i, me_ref: (me_ref[0], rows(l, i), 0)),
                pl.BlockSpec((N_DEV, tr, c), lambda l, i, me_ref: (0, rows(l, i), 0))]

    grad_specs = [spec for k in range(nl) for spec in grad_pair_specs(k)]
    shard = pl.BlockSpec((1, tr, c), lambda l, i, me_ref: (first_layer + l, i, 0))
    untouched = pl.BlockSpec(memory_space=pl.ANY)
    out = jax.ShapeDtypeStruct(w.shape, F32)
    first_prev = 1 + 2 * nl + 4
    return _pcall(
        body, name="adamw_sharded",
        grid_spec=pltpu.PrefetchScalarGridSpec(
            num_scalar_prefetch=1, grid=(nl, nr),
            in_specs=grad_specs + [shard, shard, shard] + [untouched] * (1 + len(prev)),
            out_specs=[shard, shard, shard, shard]),
        out_shape=[out, out, out, out],
        input_output_aliases={first_prev + k: k for k in range(len(prev))},
        compiler_params=_params(("arbitrary", "arbitrary")),
    )(me, *[a for pair in grads for a in pair], w, m, v, dep, *prev)


def _adamw_small(w, g, m, v):
    def body(w_ref, g_ref, m_ref, v_ref, d_ref, m2_ref, v2_ref):
        delta, m2, v2 = _adamw_math(w_ref[...], g_ref[...], m_ref[...], v_ref[...])
        d_ref[...] = delta
        m2_ref[...] = m2
        v2_ref[...] = v2

    spec = pl.BlockSpec(w.shape, lambda i: (0, 0))
    out = jax.ShapeDtypeStruct(w.shape, F32)
    return _pcall(
        body, name="adamw_small", grid=(1,),
        in_specs=[spec] * 4, out_specs=[spec] * 3, out_shape=[out] * 3,
        compiler_params=_params(("arbitrary",)),
    )(w, g, m, v)


def _pack(arrs):
    flat = jnp.concatenate([a.reshape(-1) for a in arrs])
    n = flat.shape[0]
    rows = -(-n // 1024) * 8
    return jnp.pad(flat, (0, rows * 128 - n)).reshape(rows, 128)


def _unpack(packed, like):
    flat = packed.reshape(-1)
    out, off = [], 0
    for a in like:
        out.append(flat[off:off + a.size].reshape(a.shape))
        off += a.size
    return out


def kernel(x, mem, g_ffn1, w_ffn1_up, w_ffn1_down, g_mix, w_in, conv_w, sinks, g_mem, w_mem_kv, g_grp, w_out, g_ffn2, w_ffn2_up, w_ffn2_down, g_final, loss_target, m_g_ffn1, m_w_ffn1_up, m_w_ffn1_down, m_g_mix, m_w_in, m_conv_w, m_sinks, m_g_mem, m_w_mem_kv, m_g_grp, m_w_out, m_g_ffn2, m_w_ffn2_up, m_w_ffn2_down, m_g_final, v_g_ffn1, v_w_ffn1_up, v_w_ffn1_down, v_g_mix, v_w_in, v_conv_w, v_sinks, v_g_mem, v_w_mem_kv, v_g_grp, v_w_out, v_g_ffn2, v_w_ffn2_up, v_w_ffn2_down, v_g_final):
    depth = g_ffn1.shape[0]
    t, d = x.shape[1], x.shape[2]
    width = max(d, D_MIX)
    me = _slot(*_position())
    conv_shard = conv_w.shape[2]

    xin, memin, tgt = x[0], mem[0], loss_target[0]

    conv_tile = jnp.zeros((depth * 8, 128), F32).at[:, :conv_shard].set(
        jnp.pad(conv_w, ((0, 0), (0, 8 - conv_w.shape[1]), (0, 0))).reshape(depth * 8, conv_shard))
    tr = lambda a: jnp.swapaxes(a, -1, -2)
    bf = lambda a: a.astype(BF16)
    weights = []
    collective_id = 0
    for l in range(depth):
        groups = [[bf(tr(w_ffn1_up[l])), bf(w_ffn1_down[l])] + ([conv_tile] if l == 0 else []),
                  [bf(tr(w_in[l])), bf(w_mem_kv[l]), bf(w_out[l])],
                  [bf(tr(w_ffn2_up[l])), bf(w_ffn2_down[l])]]
        full = []
        for gi, shards in enumerate(groups):
            full.append(_all_gather(shards, f"all_gather_l{l}_g{gi}", collective_id))
            collective_id += 1
        if l == 0:
            conv_full = full[0][2].reshape(N_DEV, depth, 8, 128)[:, :, :3, :conv_shard]
            conv_full = conv_full.transpose(1, 2, 0, 3).reshape(depth, 3, N_DEV * conv_shard)
        weights.append(dict(
            up1=full[0][0].reshape(2, -1, d), dn1=full[0][1].reshape(-1, d),
            win=full[1][0].reshape(D_IN, d), wkv=full[1][1].reshape(d, 2 * D_MEMQ), wout=full[1][2].reshape(D_MIX, d),
            up2=full[2][0].reshape(2, -1, d), dn2=full[2][1].reshape(-1, d)))

    row = lambda a: a.reshape(1, -1)
    bias_key = _bias_table()

    h = xin
    saved = []
    for l in range(depth):
        wl = weights[l]
        s = dict(h0=h)
        h, s["gu1"], s["n1"] = _ffn_fwd(h, row(g_ffn1[l]), wl["up1"], wl["dn1"])
        s["h1"] = h
        s["p"], s["n_mix"], s["qh"] = _mix_proj_fwd(h, row(g_mix[l]), wl["win"])
        s["mkv"], s["nt_mem"] = _memkv_fwd(memin, row(g_mem[l]), wl["wkv"], s["p"])
        s["y"], s["lse"] = _mix_core_fwd(s["p"], s["qh"], s["mkv"], conv_full[l], row(sinks[l]), bias_key)
        h, s["mt"] = _mix_out_fwd(s["y"], h, row(g_grp[l]), wl["wout"])
        s["h2"] = h
        h, s["gu2"], s["n2"] = _ffn_fwd(h, row(g_ffn2[l]), wl["up2"], wl["dn2"])
        saved.append(s)

    dh, loss_part, dg_final = _final_loss(h, row(g_final), tgt)

    small = {}
    dep = loss_part

    def reduce_small(after):
        def lanes(a):
            return jnp.pad(a, ((0, 0), (0, width - a.shape[1])))

        def first_row(a):
            return lanes(jnp.pad(a, ((0, 8 - a.shape[0]), (0, 0))))

        vec_names = ["g_ffn1", "g_mix", "g_mem", "g_grp", "g_ffn2", "sinks"]
        tiles = [lanes(small[n, l]) for n in vec_names for l in range(depth)]
        tiles += [first_row(small["conv_w", l][k:k + 1]) for l in range(depth) for k in range(3)]
        tiles.append(lanes(dg_final))
        n_real = len(tiles)
        tiles.append(lanes(loss_part))
        tiles += [jnp.zeros((8, width), F32)] * (-len(tiles) % 8)
        summed = _all_reduce_rows(jnp.stack(tiles), after)
        loss_all = 0.5 * jnp.sum(summed[n_real]) / d

        def vec(n, wd):
            return jnp.stack([summed[vec_names.index(n) * depth + l, :wd] for l in range(depth)])

        conv_base = len(vec_names) * depth
        conv_grad = jnp.stack([jnp.stack([summed[conv_base + 3 * l + k, :D_CONV] for k in range(3)])
                               for l in range(depth)])
        grads_small = {
            "g_ffn1": vec("g_ffn1", d), "g_mix": vec("g_mix", d), "g_mem": vec("g_mem", d),
            "g_grp": vec("g_grp", D_MIX), "g_ffn2": vec("g_ffn2", d), "sinks": vec("sinks", N_SWA_HEADS),
            "conv_w": lax.dynamic_slice_in_dim(conv_grad, me * conv_shard, conv_shard, axis=2),
            "g_final": summed[n_real - 1, :d],
        }
        small_w = [("g_ffn1", g_ffn1, m_g_ffn1, v_g_ffn1), ("g_mix", g_mix, m_g_mix, v_g_mix),
                   ("conv_w", conv_w, m_conv_w, v_conv_w), ("sinks", sinks, m_sinks, v_sinks),
                   ("g_mem", g_mem, m_g_mem, v_g_mem), ("g_grp", g_grp, m_g_grp, v_g_grp),
                   ("g_ffn2", g_ffn2, m_g_ffn2, v_g_ffn2), ("g_final", g_final, m_g_final, v_g_final)]
        like = [w for _, w, _, _ in small_w]
        packed = _adamw_small(_pack(like), _pack([grads_small[n] for n, _, _, _ in small_w]),
                              _pack([m for _, _, m, _ in small_w]), _pack([v for _, _, _, v in small_w]))
        updated = {n: (grads_small[n], dl, m2, v2)
                   for (n, _, _, _), dl, m2, v2 in zip(small_w, *[_unpack(pk, like) for pk in packed])}
        return loss_all, updated, packed[0]

    started = []

    def scatter(names, partials, label):
        state, token = _scatter_start(partials, f"scatter_grads_{label}")
        started.append((names, state))
        return token

    for l in reversed(range(depth)):
        wl, s = weights[l], saved[l]
        dh, agu, dyb, small["g_ffn2", l] = _ffn_bwd_act(dh, s["h2"], row(g_ffn2[l]), s["gu2"], wl["up2"], wl["dn2"], dep)
        ddn2 = _ffn_bwd_w(agu, 2, 1, dyb, agu, f"ffn_bwd_w_down_l{l}_ffn2").reshape(N_DEV, -1, d)
        dup2 = _ffn_bwd_w(agu, 0, 2, s["n2"], ddn2, f"ffn_bwd_w_up_l{l}_ffn2").reshape(N_DEV, -1, d)
        dep = scatter([("w_ffn2_up", l), ("w_ffn2_down", l)], [dup2, ddn2], f"l{l}_ffn2")
        dyconv, doh, delta, dwout, small["g_grp", l] = _mix_out_bwd(dh, s["y"], row(g_grp[l]), wl["wout"], s["mt"], dep)
        dp, dmkv, small["conv_w", l], small["sinks", l] = _mix_core_bwd(
            s["p"], s["qh"], dyconv, doh, delta, s["lse"], s["mkv"], conv_full[l], row(sinks[l]), bias_key)
        dwkv, small["g_mem", l] = _memkv_bwd(dmkv, memin, row(g_mem[l]), wl["wkv"], s["nt_mem"])
        dh, dwin, small["g_mix", l] = _mix_proj_bwd(dp, dh, s["h1"], row(g_mix[l]), wl["win"], s["n_mix"])
        dep = scatter([("w_in", l), ("w_mem_kv", l), ("w_out", l)],
                      [dwin.reshape(N_DEV, -1, d), dwkv.reshape(N_DEV, -1, 2 * D_MEMQ), dwout.reshape(N_DEV, -1, d)],
                      f"l{l}_mix")
        dh, agu, dyb, small["g_ffn1", l] = _ffn_bwd_act(dh, s["h0"], row(g_ffn1[l]), s["gu1"], wl["up1"], wl["dn1"], dep)
        ddn1 = _ffn_bwd_w(agu, 2, 1, dyb, agu, f"ffn_bwd_w_down_l{l}_ffn1").reshape(N_DEV, -1, d)
        if l > 0:
            dup1 = _ffn_bwd_w(agu, 0, 2, s["n1"], ddn1, f"ffn_bwd_w_up_l{l}_ffn1").reshape(N_DEV, -1, d)
            dep = scatter([("w_ffn1_up", l), ("w_ffn1_down", l)], [dup1, ddn1], f"l{l}_ffn1")
        else:
            dep = scatter([("w_ffn1_down", l)], [ddn1], f"l{l}_ffn1_down")
            dup1 = _ffn_bwd_w(agu, 0, 2, s["n1"], dep, f"ffn_bwd_w_up_l{l}_ffn1").reshape(N_DEV, -1, d)
            dep = scatter([("w_ffn1_up", l)], [dup1], f"l{l}_ffn1_up")
    grad_x = dh[None]

    big = {"w_ffn2_up": (w_ffn2_up, m_w_ffn2_up, v_w_ffn2_up, True), "w_ffn2_down": (w_ffn2_down, m_w_ffn2_down, v_w_ffn2_down, False),
           "w_in": (w_in, m_w_in, v_w_in, True), "w_mem_kv": (w_mem_kv, m_w_mem_kv, v_w_mem_kv, False),
           "w_out": (w_out, m_w_out, v_w_out, False), "w_ffn1_up": (w_ffn1_up, m_w_ffn1_up, v_w_ffn1_up, True),
           "w_ffn1_down": (w_ffn1_down, m_w_ffn1_down, v_w_ffn1_down, False)}
    me_index = jnp.reshape(me, (1,)).astype(jnp.int32)
    sharded, landed, begun = {}, {}, {}
    by_layer = {name for name, _ in started[-1][0]}

    def finish(groups, after):
        for names, state in groups:
            owns, lands = _scatter_wait(state, after)
            for key, own, land in zip(names, owns, lands):
                landed[key] = (own, land)
            after = lands[0]
            for name, l in names:
                w, m, v, transposed = big[name]
                fix = tr if transposed else (lambda a: a)
                if name in by_layer:
                    res = _adamw_sharded(me_index, [landed[name, l]], fix(w), fix(m), fix(v), after, l, begun.get(name))
                    done = name in begun
                    begun[name] = res
                elif all((name, k) in landed for k in range(depth)):
                    res = _adamw_sharded(me_index, [landed[name, k] for k in range(depth)], fix(w), fix(m), fix(v), after)
                    done = True
                else:
                    continue
                if done:
                    sharded[name] = tuple(fix(r) for r in res)
                after = res[0]
        return after

    loss, small_out, dep = reduce_small(finish(started[:-1], dep))
    finish(started[-1:], dep)

    order = ["g_ffn1", "w_ffn1_up", "w_ffn1_down", "g_mix", "w_in", "conv_w", "sinks", "g_mem", "w_mem_kv", "g_grp",
             "w_out", "g_ffn2", "w_ffn2_up", "w_ffn2_down", "g_final"]
    results = {**sharded, **small_out}
    outs = [loss, grad_x]
    for part in range(4):
        outs += [results[n][part] for n in order]
    return tuple(outs)
```

```python
import numpy as np
import jax
import jax.numpy as jnp
from jax import lax
from jax.experimental import pallas as pl
from jax.experimental.pallas import tpu as pltpu
from jax.experimental.pallas import tpu_sc as plsc

F32 = jnp.float32
BF16 = jnp.bfloat16

N_DEV = 8
EPS = 1e-6
N_SWA_HEADS = 8
N_SWA_KV = 2
SWA_GROUP = N_SWA_HEADS // N_SWA_KV
HEAD_DIM = 64
N_MEM_HEADS = 4
D_CONV = 256
BLOCK = 128
D_SWA = N_SWA_HEADS * HEAD_DIM
D_KV = N_SWA_KV * HEAD_DIM
D_MEMQ = N_MEM_HEADS * HEAD_DIM
D_MIX = D_CONV + D_SWA + D_MEMQ
D_IN = 3 * D_CONV + D_SWA + 2 * D_KV + D_MEMQ
COL_BG, COL_CG, COL_U = 0, D_CONV, 2 * D_CONV
COL_Q = 3 * D_CONV
COL_K = COL_Q + D_SWA
COL_V = COL_K + D_KV
COL_QM = COL_V + D_KV
MIX_GROUPS = ((0, D_CONV), (D_CONV, D_CONV + D_SWA), (D_CONV + D_SWA, D_MIX))
SLOPES = tuple(2.0 ** (-8.0 * (i + 1) / N_SWA_HEADS) for i in range(N_SWA_HEADS))
SCALE = HEAD_DIM ** -0.5
NEG = -1e30

ADAM_LR = 0.001
ADAM_B1 = 0.9
ADAM_B2 = 0.999
ADAM_EPS = 1e-08
ADAM_WD = 0.01
ADAM_STEP = 10

V7X_VMEM_BYTES = 64 * 1024 * 1024
VMEM_LIMIT = (V7X_VMEM_BYTES * 3) // 4
MESH = pl.DeviceIdType.MESH


def _pcall(body, **kw):
    return pl.pallas_call(body, **kw)


def _params(sem=None, vmem=VMEM_LIMIT):
    return pltpu.CompilerParams(dimension_semantics=sem, vmem_limit_bytes=vmem)


def _dot(a, b):
    return lax.dot_general(a, b, (((1,), (0,)), ((), ())), preferred_element_type=F32)


def _dot_nt(a, b):
    return lax.dot_general(a, b, (((1,), (1,)), ((), ())), preferred_element_type=F32)


def _dot_tn(a, b):
    return lax.dot_general(a, b, (((0,), (0,)), ((), ())), preferred_element_type=F32)


def _rstd(x):
    return lax.rsqrt(jnp.mean(x * x, axis=-1, keepdims=True) + EPS)


def _sigmoid(x):
    return 1.0 / (1.0 + jnp.exp(-x))


def _sum8(x):
    r, w = x.shape
    return jnp.sum(x.reshape(r // 8, 8, w), axis=0)


def _tok_block(t, rows=512):
    return min(rows, t)


def _feat_block(f, parts=N_DEV // 2):
    return f // parts


def _ffn_fwd(h, g, wup_t, wdn):
    t, d = h.shape
    f = wdn.shape[0]
    tm, tf = _tok_block(t), _feat_block(f, 2)
    ni, nj = t // tm, f // tf

    def body(h_ref, g_ref, wup_ref, wdn_ref, ho_ref, gu_ref, n_ref, nt_ref, acc_ref):
        j = pl.program_id(1)

        @pl.when(j == 0)
        def _():
            hh = h_ref[...]
            n = hh * _rstd(hh) * g_ref[...]
            n_ref[...] = n.astype(BF16)
            nt_ref[...] = n.T.astype(BF16)
            acc_ref[...] = jnp.zeros_like(acc_ref)

        nt = nt_ref[...]
        gate = _dot(wup_ref[0], nt)
        up = _dot(wup_ref[1], nt)
        gu_ref[0] = gate.astype(BF16)
        gu_ref[1] = up.astype(BF16)
        a = gate * _sigmoid(gate) * up
        acc_ref[...] += _dot_tn(a.astype(BF16), wdn_ref[...])

        @pl.when(j == nj - 1)
        def _():
            ho_ref[...] = h_ref[...] + 0.5 * acc_ref[...]

    return _pcall(
        body, name="ffn_fwd", grid=(ni, nj),
        in_specs=[pl.BlockSpec((tm, d), lambda i, j: (i, 0)),
                  pl.BlockSpec((1, d), lambda i, j: (0, 0)),
                  pl.BlockSpec((2, tf, d), lambda i, j: (0, j, 0)),
                  pl.BlockSpec((tf, d), lambda i, j: (j, 0))],
        out_specs=[pl.BlockSpec((tm, d), lambda i, j: (i, 0)),
                   pl.BlockSpec((2, tf, tm), lambda i, j: (0, j, i)),
                   pl.BlockSpec((tm, d), lambda i, j: (i, 0))],
        out_shape=[jax.ShapeDtypeStruct((t, d), F32),
                   jax.ShapeDtypeStruct((2, f, t), BF16),
                   jax.ShapeDtypeStruct((t, d), BF16)],
        scratch_shapes=[pltpu.VMEM((d, tm), BF16), pltpu.VMEM((tm, d), F32)],
        compiler_params=_params(("parallel", "arbitrary")),
    )(h, g, wup_t, wdn)


def _ffn_bwd_act(dho, h, g, gu, wup_t, wdn, dep):
    t, d = h.shape
    f = wdn.shape[0]
    tm, tf = _tok_block(t), _feat_block(f)
    ni, nj = t // tm, f // tf

    def body(dho_ref, h_ref, g_ref, gu_ref, wup_ref, wdn_ref, dep_ref, dh_ref, agu_ref, dyb_ref, dg_ref, dyt_ref, acc_ref):
        i = pl.program_id(0)
        j = pl.program_id(1)

        @pl.when(j == 0)
        def _():
            dy0 = 0.5 * dho_ref[...]
            dyb_ref[...] = dy0.astype(BF16)
            dyt_ref[...] = dy0.T.astype(BF16)
            acc_ref[...] = jnp.zeros_like(acc_ref)

        da = _dot(wdn_ref[...], dyt_ref[...]).astype(BF16)
        gate = gu_ref[0]
        up = gu_ref[1]
        sg = _sigmoid(gate)
        silu = gate * sg
        dgate = da * up * (sg * (1.0 + gate * (1.0 - sg)))
        dup = da * silu
        agu_ref[0] = dgate
        agu_ref[1] = dup
        agu_ref[2] = silu * up
        acc_ref[...] += _dot_tn(dgate, wup_ref[0])
        acc_ref[...] += _dot_tn(dup, wup_ref[1])

        @pl.when(j == nj - 1)
        def _():
            hh = h_ref[...]
            r = _rstd(hh)
            xhat = hh * r
            dnf = acc_ref[...]
            dxh = dnf * g_ref[...]
            dh_ref[...] = dho_ref[...] + r * (dxh - xhat * jnp.mean(dxh * xhat, axis=-1, keepdims=True))
            part = _sum8(dnf * xhat)

            @pl.when(i == 0)
            def _():
                dg_ref[...] = part

            @pl.when(i > 0)
            def _():
                dg_ref[...] += part

    return _pcall(
        body, name="ffn_bwd_act", grid=(ni, nj),
        in_specs=[pl.BlockSpec((tm, d), lambda i, j: (i, 0)),
                  pl.BlockSpec((tm, d), lambda i, j: (i, 0)),
                  pl.BlockSpec((1, d), lambda i, j: (0, 0)),
                  pl.BlockSpec((2, tf, tm), lambda i, j: (0, j, i)),
                  pl.BlockSpec((2, tf, d), lambda i, j: (0, j, 0)),
                  pl.BlockSpec((tf, d), lambda i, j: (j, 0)),
                  pl.BlockSpec(memory_space=pl.ANY)],
        out_specs=[pl.BlockSpec((tm, d), lambda i, j: (i, 0)),
                   pl.BlockSpec((3, tf, tm), lambda i, j: (0, j, i)),
                   pl.BlockSpec((tm, d), lambda i, j: (i, 0)),
                   pl.BlockSpec((8, d), lambda i, j: (0, 0))],
        out_shape=[jax.ShapeDtypeStruct((t, d), F32),
                   jax.ShapeDtypeStruct((3, f, t), BF16),
                   jax.ShapeDtypeStruct((t, d), BF16),
                   jax.ShapeDtypeStruct((8, d), F32)],
        scratch_shapes=[pltpu.VMEM((d, tm), BF16), pltpu.VMEM((tm, d), F32)],
        compiler_params=_params(("arbitrary", "arbitrary")),
    )(dho, h, g, gu, wup_t, wdn, dep)


def _ffn_bwd_w(agu, first, count, rhs, dep, name):
    _, f, t = agu.shape
    d = rhs.shape[1]
    tm, tf = _tok_block(t, 2048), _feat_block(f)
    ni, nj = t // tm, f // tf

    def body(lhs_ref, rhs_ref, dep_ref, dw_ref, acc_ref):
        i = pl.program_id(1)
        @pl.when(i == 0)
        def _():
            acc_ref[...] = jnp.zeros_like(acc_ref)

        rb = rhs_ref[...]
        for k in range(count):
            acc_ref[k] += _dot(lhs_ref[k], rb)

        @pl.when(i == ni - 1)
        def _():
            dw_ref[...] = acc_ref[...].astype(BF16)

    return _pcall(
        body, name=name, grid=(nj, ni),
        in_specs=[pl.BlockSpec((count, tf, tm), lambda j, i: (first // count, j, i)),
                  pl.BlockSpec((tm, d), lambda j, i: (i, 0)),
                  pl.BlockSpec(memory_space=pl.ANY)],
        out_specs=pl.BlockSpec((count, tf, d), lambda j, i: (0, j, 0)),
        out_shape=jax.ShapeDtypeStruct((count, f, d), BF16),
        scratch_shapes=[pltpu.VMEM((count, tf, d), F32)],
        compiler_params=_params(("parallel", "arbitrary")),
    )(agu, rhs, dep)


N_HEADS = N_SWA_HEADS + N_MEM_HEADS


def _q_col(hd):
    return COL_Q + HEAD_DIM * hd if hd < N_SWA_HEADS else COL_QM + HEAD_DIM * (hd - N_SWA_HEADS)


def _mix_proj_fwd(h, g, win_t):
    t, d = h.shape
    tm = _tok_block(t)

    def body(h_ref, g_ref, win_ref, p_ref, n_ref, qh_ref):
        hh = h_ref[...]
        n = (hh * _rstd(hh) * g_ref[...]).astype(BF16)
        n_ref[...] = n
        proj = _dot_nt(n, win_ref[...])
        p_ref[...] = proj.astype(BF16)
        for hd in range(N_HEADS):
            c0 = _q_col(hd)
            qh_ref[hd] = (proj[:, c0:c0 + HEAD_DIM] * SCALE).astype(BF16)

    return _pcall(
        body, name="mix_proj_fwd", grid=(t // tm,),
        in_specs=[pl.BlockSpec((tm, d), lambda i: (i, 0)),
                  pl.BlockSpec((1, d), lambda i: (0, 0)),
                  pl.BlockSpec((D_IN, d), lambda i: (0, 0))],
        out_specs=[pl.BlockSpec((tm, D_IN), lambda i: (i, 0)),
                   pl.BlockSpec((tm, d), lambda i: (i, 0)),
                   pl.BlockSpec((N_HEADS, tm, HEAD_DIM), lambda i: (0, i, 0))],
        out_shape=[jax.ShapeDtypeStruct((t, D_IN), BF16), jax.ShapeDtypeStruct((t, d), BF16),
                   jax.ShapeDtypeStruct((N_HEADS, t, HEAD_DIM), BF16)],
        compiler_params=_params(("parallel",)),
    )(h, g, win_t)


def _memkv_fwd(mem, g, wkv, dep):
    m, d = mem.shape

    def body(mem_ref, g_ref, w_ref, dep_ref, mkv_ref, nt_ref):
        mm = mem_ref[...]
        n = mm * _rstd(mm) * g_ref[...]
        nt_ref[...] = n.T.astype(BF16)
        mkv_ref[...] = _dot(n.astype(BF16), w_ref[...]).astype(BF16)

    return _pcall(
        body, name="memkv_fwd", grid=(1,),
        in_specs=[pl.BlockSpec((m, d), lambda i: (0, 0)),
                  pl.BlockSpec((1, d), lambda i: (0, 0)),
                  pl.BlockSpec((d, 2 * D_MEMQ), lambda i: (0, 0)),
                  pl.BlockSpec(memory_space=pl.ANY)],
        out_specs=[pl.BlockSpec((m, 2 * D_MEMQ), lambda i: (0, 0)),
                   pl.BlockSpec((d, m), lambda i: (0, 0))],
        out_shape=[jax.ShapeDtypeStruct((m, 2 * D_MEMQ), BF16), jax.ShapeDtypeStruct((d, m), BF16)],
        compiler_params=_params(("arbitrary",)),
    )(mem, g, wkv, dep)


def _memkv_bwd(dmkv, mem, g, wkv, nt):
    m, d = mem.shape

    def body(dmkv_ref, mem_ref, g_ref, w_ref, nt_ref, dw_ref, dg_ref):
        db = dmkv_ref[...].astype(BF16)
        dw_ref[...] = _dot(nt_ref[...], db).astype(BF16)
        dn = _dot_nt(db, w_ref[...])
        mm = mem_ref[...]
        dg_ref[...] = _sum8(dn * (mm * _rstd(mm)))

    return _pcall(
        body, name="memkv_bwd", grid=(1,),
        in_specs=[pl.BlockSpec((m, 2 * D_MEMQ), lambda i: (0, 0)),
                  pl.BlockSpec((m, d), lambda i: (0, 0)),
                  pl.BlockSpec((1, d), lambda i: (0, 0)),
                  pl.BlockSpec((d, 2 * D_MEMQ), lambda i: (0, 0)),
                  pl.BlockSpec((d, m), lambda i: (0, 0))],
        out_specs=[pl.BlockSpec((d, 2 * D_MEMQ), lambda i: (0, 0)),
                   pl.BlockSpec((8, d), lambda i: (0, 0))],
        out_shape=[jax.ShapeDtypeStruct((d, 2 * D_MEMQ), BF16), jax.ShapeDtypeStruct((8, d), F32)],
        compiler_params=_params(("arbitrary",)),
    )(dmkv, mem, g, wkv, nt)


def _shift_rows(v, k, edge_rows, row):
    out = pltpu.roll(v, k, 0)
    for r in range(k):
        out = jnp.where(row == r, edge_rows[r], out)
    return out


def _shift_rows_up(v, k, edge_rows, row):
    n = v.shape[0]
    out = pltpu.roll(v, n - k, 0)
    for r in range(k):
        out = jnp.where(row == n - k + r, edge_rows[r], out)
    return out


GROUP_ROWS = SWA_GROUP * BLOCK
BIAS_CUR, BIAS_PREV, BIAS_NONE = 0, 1, 2


def _bias_table():
    tq = np.arange(BLOCK)[:, None]
    sk = np.arange(BLOCK)[None, :]
    slopes = np.asarray(SLOPES, np.float32)[:, None, None]
    cur = np.where(tq >= sk, -slopes * (tq - sk).astype(np.float32), NEG)
    prev = np.where(sk > tq, -slopes * (tq + BLOCK - sk).astype(np.float32), NEG)
    none = np.full_like(cur, NEG)
    tok = np.stack([cur, prev, none]).astype(np.float32).reshape(3, N_SWA_KV, GROUP_ROWS, BLOCK)
    return jnp.asarray(np.ascontiguousarray(tok.transpose(0, 1, 3, 2)))


def _head_cols(hd):
    return D_CONV + HEAD_DIM * hd


def _mix_core_fwd(p, qh, mkv, convw, sinks, bias_key):
    t = p.shape[0]
    m = mkv.shape[0]
    nb = t // BLOCK

    def body(sk_ref, pc_ref, pkv_ref, ppc_ref, ppu_ref, qh_ref, mkv_ref, cw_ref, bc_ref, bp_ref, y_ref, l_ref):
        i = pl.program_id(0)
        prevf = (i > 0).astype(F32)
        row = lax.broadcasted_iota(jnp.int32, (BLOCK, D_CONV), 0)

        bg = pc_ref[:, COL_BG:COL_BG + D_CONV].astype(F32)
        cg = pc_ref[:, COL_CG:COL_CG + D_CONV].astype(F32)
        u = pc_ref[:, COL_U:COL_U + D_CONV].astype(F32)
        vv = cg * u
        pvv = ppc_ref[...].astype(F32) * ppu_ref[...].astype(F32) * prevf
        vv1 = _shift_rows(vv, 1, [pvv[15:16]], row)
        vv2 = _shift_rows(vv, 2, [pvv[14:15], pvv[15:16]], row)
        w = cw_ref[...]
        y_ref[:, 0:D_CONV] = bg * (w[0:1] * vv2 + w[1:2] * vv1 + w[2:3] * vv)

        head_row = lax.broadcasted_iota(jnp.int32, (128, BLOCK), 0)
        lse_t = jnp.zeros((128, BLOCK), F32)
        for kv in range(N_SWA_KV):
            heads = range(kv * SWA_GROUP, (kv + 1) * SWA_GROUP)
            kc = pc_ref[:, COL_K + HEAD_DIM * kv:COL_K + HEAD_DIM * (kv + 1)]
            vc = pc_ref[:, COL_V + HEAD_DIM * kv:COL_V + HEAD_DIM * (kv + 1)]
            kp = pkv_ref[:, HEAD_DIM * kv:HEAD_DIM * (kv + 1)]
            vp = pkv_ref[:, D_KV + HEAD_DIM * kv:D_KV + HEAD_DIM * (kv + 1)]
            qg = qh_ref[kv * SWA_GROUP:(kv + 1) * SWA_GROUP].reshape(GROUP_ROWS, HEAD_DIM)
            sc = _dot_nt(kc, qg) + bc_ref[0, kv]
            sp = _dot_nt(kp, qg) + bp_ref[0, kv]
            sink = jnp.concatenate([jnp.full((1, BLOCK), sk_ref[0, hd], F32) for hd in heads], axis=1)
            mx = jnp.maximum(jnp.max(jnp.maximum(sc, sp), axis=0, keepdims=True), sink)
            ec = jnp.exp(sc - mx)
            ep = jnp.exp(sp - mx)
            den = jnp.sum(ec + ep, axis=0, keepdims=True) + jnp.exp(sink - mx)
            ot = (_dot_tn(vc, ec.astype(BF16)) + _dot_tn(vp, ep.astype(BF16))) / den
            lse = mx + jnp.log(den)
            for gi, hd in enumerate(heads):
                span = slice(gi * BLOCK, (gi + 1) * BLOCK)
                y_ref[:, _head_cols(hd):_head_cols(hd) + HEAD_DIM] = ot[:, span].T
                lse_t = jnp.where(head_row == hd, lse[:, span], lse_t)

        for hm in range(N_MEM_HEADS):
            hd = N_SWA_HEADS + hm
            mk = mkv_ref[:, HEAD_DIM * hm:HEAD_DIM * (hm + 1)]
            mv = mkv_ref[:, D_MEMQ + HEAD_DIM * hm:D_MEMQ + HEAD_DIM * (hm + 1)]
            s = _dot_nt(mk, qh_ref[hd])
            mx = jnp.max(s, axis=0, keepdims=True)
            e = jnp.exp(s - mx)
            den = jnp.sum(e, axis=0, keepdims=True)
            y_ref[:, _head_cols(hd):_head_cols(hd) + HEAD_DIM] = (_dot_tn(mv, e.astype(BF16)) / den).T
            lse_t = jnp.where(head_row == hd, mx + jnp.log(den), lse_t)
        l_ref[...] = lse_t.T

    kv_col = COL_K // (2 * D_KV)
    bias_block = (1, N_SWA_KV, BLOCK, GROUP_ROWS)
    return _pcall(
        body, name="mix_core_fwd", grid=(nb,),
        in_specs=[pl.BlockSpec(memory_space=pltpu.SMEM),
                  pl.BlockSpec((BLOCK, D_IN), lambda i: (i, 0)),
                  pl.BlockSpec((BLOCK, 2 * D_KV), lambda i: (jnp.maximum(i - 1, 0), kv_col)),
                  pl.BlockSpec((16, D_CONV), lambda i: (jnp.maximum(i * (BLOCK // 16) - 1, 0), COL_CG // D_CONV)),
                  pl.BlockSpec((16, D_CONV), lambda i: (jnp.maximum(i * (BLOCK // 16) - 1, 0), COL_U // D_CONV)),
                  pl.BlockSpec((N_HEADS, BLOCK, HEAD_DIM), lambda i: (0, i, 0)),
                  pl.BlockSpec((m, 2 * D_MEMQ), lambda i: (0, 0)),
                  pl.BlockSpec((3, D_CONV), lambda i: (0, 0)),
                  pl.BlockSpec(bias_block, lambda i: (BIAS_CUR, 0, 0, 0)),
                  pl.BlockSpec(bias_block, lambda i: (jnp.where(i == 0, BIAS_NONE, BIAS_PREV), 0, 0, 0))],
        out_specs=[pl.BlockSpec((BLOCK, D_MIX), lambda i: (i, 0)),
                   pl.BlockSpec((BLOCK, 128), lambda i: (i, 0))],
        out_shape=[jax.ShapeDtypeStruct((t, D_MIX), F32), jax.ShapeDtypeStruct((t, 128), F32)],
        compiler_params=_params(("parallel",)),
    )(sinks, p, p, p, p, qh, mkv, convw, bias_key, bias_key)


def _mix_core_bwd(p, qh, dyconv, doh, delta, lse, mkv, convw, sinks, bias_key):
    t = p.shape[0]
    m = mkv.shape[0]
    nb = t // BLOCK

    def body(sk_ref, pc_ref, pkv_ref, ppc_ref, ppu_ref, pnb_ref, dyc_ref, dyn_ref, qc_ref, qn_ref, doc_ref, don_ref,
             dlc_ref, dln_ref, lc_ref, ln_ref, mkv_ref, cw_ref, bp_ref, bct_ref, bnt_ref,
             dp_ref, dmkv_ref, dcw_ref, dsk_ref):
        i = pl.program_id(0)
        prevf = (i > 0).astype(F32)
        nextf = (i < nb - 1).astype(F32)
        row = lax.broadcasted_iota(jnp.int32, (BLOCK, D_CONV), 0)

        @pl.when(i == 0)
        def _():
            dmkv_ref[...] = jnp.zeros_like(dmkv_ref)
            dcw_ref[...] = jnp.zeros_like(dcw_ref)
            dsk_ref[...] = jnp.zeros_like(dsk_ref)

        bg = pc_ref[:, COL_BG:COL_BG + D_CONV].astype(F32)
        cg = pc_ref[:, COL_CG:COL_CG + D_CONV].astype(F32)
        u = pc_ref[:, COL_U:COL_U + D_CONV].astype(F32)
        vv = cg * u
        pvv = ppc_ref[...].astype(F32) * ppu_ref[...].astype(F32) * prevf
        vv1 = _shift_rows(vv, 1, [pvv[15:16]], row)
        vv2 = _shift_rows(vv, 2, [pvv[14:15], pvv[15:16]], row)
        w = cw_ref[...]
        yconv = w[0:1] * vv2 + w[1:2] * vv1 + w[2:3] * vv
        dyo = dyc_ref[...]
        dyc = dyo * bg
        nxt = dyn_ref[...] * pnb_ref[...].astype(F32) * nextf
        d1 = _shift_rows_up(dyc, 1, [nxt[0:1]], row)
        d2 = _shift_rows_up(dyc, 2, [nxt[0:1], nxt[1:2]], row)
        dvv = w[2:3] * dyc + w[1:2] * d1 + w[0:1] * d2
        dp_ref[:, COL_BG:COL_BG + D_CONV] = (dyo * yconv).astype(BF16)
        dp_ref[:, COL_CG:COL_CG + D_CONV] = (dvv * u).astype(BF16)
        dp_ref[:, COL_U:COL_U + D_CONV] = (dvv * cg).astype(BF16)
        dcw_ref[0:1, :] += jnp.sum(dyc * vv2, axis=0, keepdims=True)
        dcw_ref[1:2, :] += jnp.sum(dyc * vv1, axis=0, keepdims=True)
        dcw_ref[2:3, :] += jnp.sum(dyc * vv, axis=0, keepdims=True)

        lse_t, dl_t = lc_ref[...].T, dlc_ref[...].T
        lse_nt, dl_nt = ln_ref[...].T, dln_ref[...].T

        def stack_rows(tile_t, heads):
            return jnp.concatenate([tile_t[hd:hd + 1, :] for hd in heads], axis=1)

        lane8 = jnp.where(lax.broadcasted_iota(jnp.int32, (8, 128), 0) == 0,
                          lax.broadcasted_iota(jnp.int32, (8, 128), 1), -1)
        dsk = jnp.zeros((8, 128), F32)
        for kv in range(N_SWA_KV):
            heads = range(kv * SWA_GROUP, (kv + 1) * SWA_GROUP)
            kc = pc_ref[:, COL_K + HEAD_DIM * kv:COL_K + HEAD_DIM * (kv + 1)]
            vc = pc_ref[:, COL_V + HEAD_DIM * kv:COL_V + HEAD_DIM * (kv + 1)]
            kp = pkv_ref[:, HEAD_DIM * kv:HEAD_DIM * (kv + 1)]
            vp = pkv_ref[:, D_KV + HEAD_DIM * kv:D_KV + HEAD_DIM * (kv + 1)]
            qg = qc_ref[kv * SWA_GROUP:(kv + 1) * SWA_GROUP].reshape(GROUP_ROWS, HEAD_DIM)
            dog = doc_ref[kv * SWA_GROUP:(kv + 1) * SWA_GROUP].reshape(GROUP_ROWS, HEAD_DIM)
            qn = qn_ref[kv * SWA_GROUP:(kv + 1) * SWA_GROUP].reshape(GROUP_ROWS, HEAD_DIM)
            don = don_ref[kv * SWA_GROUP:(kv + 1) * SWA_GROUP].reshape(GROUP_ROWS, HEAD_DIM)
            lse_row, dl_row = stack_rows(lse_t, heads), stack_rows(dl_t, heads)
            ptp = jnp.exp(_dot_nt(kp, qg) + bp_ref[0, kv] - lse_row)
            dstp = (ptp * (_dot_nt(vp, dog) - dl_row)).astype(BF16)
            dq = _dot_tn(dstp, kp)
            pt = jnp.exp(_dot_nt(kc, qg) + bct_ref[0, kv] - lse_row)
            dst = (pt * (_dot_nt(vc, dog) - dl_row)).astype(BF16)
            dv = _dot(pt.astype(BF16), dog)
            dk = _dot(dst, qg)
            dq = dq + _dot_tn(dst, kc)
            ptn = jnp.exp(_dot_nt(kc, qn) + bnt_ref[0, kv] - stack_rows(lse_nt, heads))
            dstn = (ptn * (_dot_nt(vc, don) - stack_rows(dl_nt, heads))).astype(BF16)
            dv = dv + _dot(ptn.astype(BF16), don)
            dk = dk + _dot(dstn, qn)
            dp_ref[:, COL_K + HEAD_DIM * kv:COL_K + HEAD_DIM * (kv + 1)] = dk.astype(BF16)
            dp_ref[:, COL_V + HEAD_DIM * kv:COL_V + HEAD_DIM * (kv + 1)] = dv.astype(BF16)
            sink = jnp.concatenate([jnp.full((1, BLOCK), sk_ref[0, hd], F32) for hd in heads], axis=1)
            sink_term = jnp.exp(sink - lse_row) * dl_row
            for gi, hd in enumerate(heads):
                span = slice(gi * BLOCK, (gi + 1) * BLOCK)
                dp_ref[:, _q_col(hd):_q_col(hd) + HEAD_DIM] = (dq[span] * SCALE).astype(BF16)
                dsk = dsk + jnp.where(lane8 == hd, -jnp.sum(sink_term[:, span], axis=1, keepdims=True), 0.0)
        dsk_ref[...] += dsk

        for hm in range(N_MEM_HEADS):
            hd = N_SWA_HEADS + hm
            qm, dom = qc_ref[hd], doc_ref[hd]
            mk = mkv_ref[:, HEAD_DIM * hm:HEAD_DIM * (hm + 1)]
            mv = mkv_ref[:, D_MEMQ + HEAD_DIM * hm:D_MEMQ + HEAD_DIM * (hm + 1)]
            pt = jnp.exp(_dot_nt(mk, qm) - lse_t[hd:hd + 1, :])
            dst = (pt * (_dot_nt(mv, dom) - dl_t[hd:hd + 1, :])).astype(BF16)
            dp_ref[:, _q_col(hd):_q_col(hd) + HEAD_DIM] = (_dot_tn(dst, mk) * SCALE).astype(BF16)
            dmkv_ref[:, HEAD_DIM * hm:HEAD_DIM * (hm + 1)] += _dot(dst, qm)
            dmkv_ref[:, D_MEMQ + HEAD_DIM * hm:D_MEMQ + HEAD_DIM * (hm + 1)] += _dot(pt.astype(BF16), dom)

    cur = lambda i: (i, 0)
    const = lambda i: (0, 0)
    rows16 = BLOCK // 16
    last16 = t // 16 - 1
    before = lambda col: (lambda i: (jnp.maximum(i * rows16 - 1, 0), col))
    after = lambda i: (jnp.minimum((i + 1) * rows16, last16), 0)
    heads_cur = lambda i: (0, i, 0)
    heads_next = lambda i: (0, jnp.minimum(i + 1, nb - 1), 0)
    stat_next = lambda i: (jnp.minimum(i + 1, nb - 1), 0)
    key_block = (1, N_SWA_KV, BLOCK, GROUP_ROWS)
    head_block = (N_HEADS, BLOCK, HEAD_DIM)
    return _pcall(
        body, name="mix_core_bwd", grid=(nb,),
        in_specs=[pl.BlockSpec(memory_space=pltpu.SMEM),
                  pl.BlockSpec((BLOCK, D_IN), cur),
                  pl.BlockSpec((BLOCK, 2 * D_KV), lambda i: (jnp.maximum(i - 1, 0), COL_K // (2 * D_KV))),
                  pl.BlockSpec((16, D_CONV), before(COL_CG // D_CONV)),
                  pl.BlockSpec((16, D_CONV), before(COL_U // D_CONV)),
                  pl.BlockSpec((16, D_CONV), after),
                  pl.BlockSpec((BLOCK, D_CONV), cur),
                  pl.BlockSpec((16, D_CONV), after),
                  pl.BlockSpec(head_block, heads_cur), pl.BlockSpec(head_block, heads_next),
                  pl.BlockSpec(head_block, heads_cur), pl.BlockSpec(head_block, heads_next),
                  pl.BlockSpec((BLOCK, 128), cur), pl.BlockSpec((BLOCK, 128), stat_next),
                  pl.BlockSpec((BLOCK, 128), cur), pl.BlockSpec((BLOCK, 128), stat_next),
                  pl.BlockSpec((m, 2 * D_MEMQ), const),
                  pl.BlockSpec((3, D_CONV), const),
                  pl.BlockSpec(key_block, lambda i: (jnp.where(i == 0, BIAS_NONE, BIAS_PREV), 0, 0, 0)),
                  pl.BlockSpec(key_block, lambda i: (BIAS_CUR, 0, 0, 0)),
                  pl.BlockSpec(key_block, lambda i: (jnp.where(i == nb - 1, BIAS_NONE, BIAS_PREV), 0, 0, 0))],
        out_specs=[pl.BlockSpec((BLOCK, D_IN), cur),
                   pl.BlockSpec((m, 2 * D_MEMQ), const),
                   pl.BlockSpec((8, D_CONV), const),
                   pl.BlockSpec((8, 128), const)],
        out_shape=[jax.ShapeDtypeStruct((t, D_IN), BF16),
                   jax.ShapeDtypeStruct((m, 2 * D_MEMQ), F32),
                   jax.ShapeDtypeStruct((8, D_CONV), F32),
                   jax.ShapeDtypeStruct((8, 128), F32)],
        compiler_params=_params(("arbitrary",)),
    )(sinks, p, p, p, p, p, dyconv, dyconv, qh, qh, doh, doh, delta, delta, lse, lse, mkv, convw,
      bias_key, bias_key, bias_key)


def _group_norms(y):
    out = []
    for a, b in MIX_GROUPS:
        ys = y[:, a:b]
        r = _rstd(ys)
        out.append((ys * r, r))
    return out


def _mix_out_fwd(y, h, g, wout):
    t, d = h.shape
    tm = _tok_block(t)

    def body(y_ref, h_ref, g_ref, w_ref, ho_ref, mt_ref):
        yhat = jnp.concatenate([yh for yh, _ in _group_norms(y_ref[...])], axis=-1)
        mixed = yhat * g_ref[...]
        mt_ref[...] = mixed.T.astype(BF16)
        ho_ref[...] = h_ref[...] + _dot(mixed.astype(BF16), w_ref[...])

    return _pcall(
        body, name="mix_out_fwd", grid=(t // tm,),
        in_specs=[pl.BlockSpec((tm, D_MIX), lambda i: (i, 0)),
                  pl.BlockSpec((tm, d), lambda i: (i, 0)),
                  pl.BlockSpec((1, D_MIX), lambda i: (0, 0)),
                  pl.BlockSpec((D_MIX, d), lambda i: (0, 0))],
        out_specs=[pl.BlockSpec((tm, d), lambda i: (i, 0)),
                   pl.BlockSpec((D_MIX, tm), lambda i: (0, i))],
        out_shape=[jax.ShapeDtypeStruct((t, d), F32), jax.ShapeDtypeStruct((D_MIX, t), BF16)],
        compiler_params=_params(("parallel",)),
    )(y, h, g, wout)


def _head_indicator():
    ind = np.zeros((D_MIX, 128), np.float32)
    for hd in range(N_HEADS):
        ind[_head_cols(hd):_head_cols(hd) + HEAD_DIM, hd] = 1.0
    return jnp.asarray(ind, BF16)


def _mix_out_bwd(dho, y, g, wout, mt, dep):
    t, d = dho.shape
    tm = _tok_block(t)
    ni = t // tm

    def body(dho_ref, y_ref, g_ref, w_ref, mt_ref, ind_ref, dep_ref, dyc_ref, doh_ref, dl_ref, dw_ref, dg_ref, acc_ref):
        i = pl.program_id(0)
        dhb = dho_ref[...].astype(BF16)
        dm = _dot_nt(dhb, w_ref[...])
        pw = _dot(mt_ref[...], dhb)
        gg = g_ref[...]
        yy = y_ref[...]
        dys = []
        dgs = []
        for (a, b), (yhat, r) in zip(MIX_GROUPS, _group_norms(yy)):
            dmg = dm[:, a:b]
            dgs.append(_sum8(dmg * yhat))
            dyh = dmg * gg[:, a:b]
            dys.append(r * (dyh - yhat * jnp.mean(dyh * yhat, axis=-1, keepdims=True)))
        dy = jnp.concatenate(dys, axis=-1)
        dyc_ref[...] = dy[:, 0:D_CONV]
        for hd in range(N_HEADS):
            doh_ref[hd] = dy[:, _head_cols(hd):_head_cols(hd) + HEAD_DIM].astype(BF16)
        prod = dy * yy
        hi = prod.astype(BF16)
        lo = (prod - hi.astype(F32)).astype(BF16)
        dl_ref[...] = _dot(hi, ind_ref[...]) + _dot(lo, ind_ref[...])
        part = jnp.concatenate(dgs, axis=-1)

        @pl.when(i == 0)
        def _():
            acc_ref[...] = pw
            dg_ref[...] = part

        @pl.when(i > 0)
        def _():
            acc_ref[...] += pw
            dg_ref[...] += part

        @pl.when(i == ni - 1)
        def _():
            dw_ref[...] = acc_ref[...].astype(BF16)

    return _pcall(
        body, name="mix_out_bwd", grid=(ni,),
        in_specs=[pl.BlockSpec((tm, d), lambda i: (i, 0)),
                  pl.BlockSpec((tm, D_MIX), lambda i: (i, 0)),
                  pl.BlockSpec((1, D_MIX), lambda i: (0, 0)),
                  pl.BlockSpec((D_MIX, d), lambda i: (0, 0)),
                  pl.BlockSpec((D_MIX, tm), lambda i: (0, i)),
                  pl.BlockSpec((D_MIX, 128), lambda i: (0, 0)),
                  pl.BlockSpec(memory_space=pl.ANY)],
        out_specs=[pl.BlockSpec((tm, D_CONV), lambda i: (i, 0)),
                   pl.BlockSpec((N_HEADS, tm, HEAD_DIM), lambda i: (0, i, 0)),
                   pl.BlockSpec((tm, 128), lambda i: (i, 0)),
                   pl.BlockSpec((D_MIX, d), lambda i: (0, 0)),
                   pl.BlockSpec((8, D_MIX), lambda i: (0, 0))],
        out_shape=[jax.ShapeDtypeStruct((t, D_CONV), F32),
                   jax.ShapeDtypeStruct((N_HEADS, t, HEAD_DIM), BF16),
                   jax.ShapeDtypeStruct((t, 128), F32),
                   jax.ShapeDtypeStruct((D_MIX, d), BF16),
                   jax.ShapeDtypeStruct((8, D_MIX), F32)],
        scratch_shapes=[pltpu.VMEM((D_MIX, d), F32)],
        compiler_params=_params(("arbitrary",)),
    )(dho, y, g, wout, mt, _head_indicator(), dep)


def _mix_proj_bwd(dp, dho, h, g, win_t, n):
    t, d = h.shape
    tm = _tok_block(t)
    ni = t // tm

    def body(dp_ref, dho_ref, h_ref, g_ref, w_ref, n_ref, dh_ref, dw_ref, dg_ref, acc_ref):
        i = pl.program_id(0)
        dpb = dp_ref[...]
        dn = _dot(dpb, w_ref[...])

        @pl.when(i == 0)
        def _():
            acc_ref[...] = jnp.zeros_like(acc_ref)

        acc_ref[...] += _dot_tn(dpb, n_ref[...])
        hh = h_ref[...]
        r = _rstd(hh)
        xhat = hh * r
        dxh = dn * g_ref[...]
        dh_ref[...] = dho_ref[...] + r * (dxh - xhat * jnp.mean(dxh * xhat, axis=-1, keepdims=True))
        part = _sum8(dn * xhat)

        @pl.when(i == 0)
        def _():
            dg_ref[...] = part

        @pl.when(i > 0)
        def _():
            dg_ref[...] += part

        @pl.when(i == ni - 1)
        def _():
            dw_ref[...] = acc_ref[...].astype(BF16)

    return _pcall(
        body, name="mix_proj_bwd", grid=(ni,),
        in_specs=[pl.BlockSpec((tm, D_IN), lambda i: (i, 0)),
                  pl.BlockSpec((tm, d), lambda i: (i, 0)),
                  pl.BlockSpec((tm, d), lambda i: (i, 0)),
                  pl.BlockSpec((1, d), lambda i: (0, 0)),
                  pl.BlockSpec((D_IN, d), lambda i: (0, 0)),
                  pl.BlockSpec((tm, d), lambda i: (i, 0))],
        out_specs=[pl.BlockSpec((tm, d), lambda i: (i, 0)),
                   pl.BlockSpec((D_IN, d), lambda i: (0, 0)),
                   pl.BlockSpec((8, d), lambda i: (0, 0))],
        out_shape=[jax.ShapeDtypeStruct((t, d), F32),
                   jax.ShapeDtypeStruct((D_IN, d), BF16),
                   jax.ShapeDtypeStruct((8, d), F32)],
        scratch_shapes=[pltpu.VMEM((D_IN, d), F32)],
        compiler_params=_params(("arbitrary",)),
    )(dp, dho, h, g, win_t, n)


def _final_loss(h, g, tgt):
    t, d = h.shape
    tm = _tok_block(t)

    def body(h_ref, g_ref, t_ref, dh_ref, ls_ref, dg_ref):
        i = pl.program_id(0)
        hh = h_ref[...]
        r = _rstd(hh)
        xhat = hh * r
        gg = g_ref[...]
        err = xhat * gg - t_ref[...]
        dy = err * (1.0 / d)
        dxh = dy * gg
        dh_ref[...] = r * (dxh - xhat * jnp.mean(dxh * xhat, axis=-1, keepdims=True))
        lpart = _sum8(err * err)
        gpart = _sum8(dy * xhat)

        @pl.when(i == 0)
        def _():
            ls_ref[...] = lpart
            dg_ref[...] = gpart

        @pl.when(i > 0)
        def _():
            ls_ref[...] += lpart
            dg_ref[...] += gpart

    return _pcall(
        body, name="final_loss", grid=(t // tm,),
        in_specs=[pl.BlockSpec((tm, d), lambda i: (i, 0)),
                  pl.BlockSpec((1, d), lambda i: (0, 0)),
                  pl.BlockSpec((tm, d), lambda i: (i, 0))],
        out_specs=[pl.BlockSpec((tm, d), lambda i: (i, 0)),
                   pl.BlockSpec((8, d), lambda i: (0, 0)),
                   pl.BlockSpec((8, d), lambda i: (0, 0))],
        out_shape=[jax.ShapeDtypeStruct((t, d), F32),
                   jax.ShapeDtypeStruct((8, d), F32),
                   jax.ShapeDtypeStruct((8, d), F32)],
        compiler_params=_params(("arbitrary",)),
    )(h, g, tgt)


def _position():
    return lax.axis_index("x"), lax.axis_index("y"), lax.axis_index("c")


def _flip(v, bit):
    return 1 - v if bit else v


def _peer(k):
    x, y, c = _position()
    return _flip(x, k & 4), _flip(y, k & 2), _flip(c, k & 1)


def _slot(px, py, pc):
    return 4 * px + 2 * py + pc


def _handshake(peers):
    barrier = pltpu.get_barrier_semaphore()
    for peer in peers:
        pl.semaphore_signal(barrier, inc=1, device_id=peer, device_id_type=MESH)
    pl.semaphore_wait(barrier, len(peers))


def _sequencer_call(body, name, collective_id, out_type, scratch_types, operands):
    return pl.kernel(
        body, out_type=out_type, mesh=plsc.ScalarSubcoreMesh(axis_name="sequencer", num_cores=1), name=name,
        scratch_types=scratch_types, compiler_params=pltpu.CompilerParams(collective_id=collective_id),
    )(*operands)


def _all_gather(shards, name, collective_id):
    nt = len(shards)

    def body(*refs):
        xs = refs[:nt]
        outs = refs[nt:2 * nt]
        send_sems, recv_sems, local_sems = refs[2 * nt:]
        x, y, c = _position()
        me, sibling = (x, y, c), (x, y, 1 - c)
        xn, yn, dg = (1 - x, y), (x, 1 - y), (1 - x, 1 - y)
        pick = lambda a, b: (jnp.where(c == 0, a[0], b[0]), jnp.where(c == 0, a[1], b[1]))
        relay_from, relay_to = pick(yn, xn), pick(xn, yn)
        _handshake([sibling, (*xn, c), (*yn, c)])

        def copy(t, k, block, to, src=None):
            dst = outs[t].at[_slot(*block)]
            return pltpu.make_async_remote_copy(
                src_ref=dst if src is None else src, dst_ref=dst,
                send_sem=send_sems.at[t, k], recv_sem=recv_sems.at[t, k],
                device_id=to, device_id_type=MESH)

        mine = [pltpu.make_async_copy(xs[t], outs[t].at[_slot(*me)], local_sems.at[t]) for t in range(nt)]
        for cp in mine:
            cp.start()
        sent = []
        for t in range(nt):
            sent += [copy(t, 0, me, sibling, src=xs[t]), copy(t, 1, me, (*xn, c), src=xs[t]),
                     copy(t, 2, me, (*yn, c), src=xs[t])]
        for cp in sent:
            cp.start()
        for t in range(nt):
            copy(t, 1, (*xn, c), me).wait_recv()
            copy(t, 2, (*yn, c), me).wait_recv()
            passed = [copy(t, 3, (*relay_from, c), (*relay_to, c)),
                      copy(t, 4, (*xn, c), sibling), copy(t, 5, (*yn, c), sibling)]
            for cp in passed:
                cp.start()
            sent += passed
        for t in range(nt):
            copy(t, 3, (*dg, c), me).wait_recv()
            fwd = copy(t, 6, (*dg, c), sibling)
            fwd.start()
            sent.append(fwd)
        for t in range(nt):
            copy(t, 0, sibling, me).wait_recv()
            for k, chip in ((4, xn), (5, yn), (6, dg)):
                copy(t, k, (*chip, 1 - c), me).wait_recv()
        for cp in sent:
            cp.wait_send()
        for cp in mine:
            cp.wait()

    return _sequencer_call(
        body, name, collective_id,
        out_type=[jax.ShapeDtypeStruct((N_DEV,) + s.shape, s.dtype) for s in shards],
        scratch_types=[pltpu.SemaphoreType.DMA((nt, 7)), pltpu.SemaphoreType.DMA((nt, 7)),
                       pltpu.SemaphoreType.DMA((nt,))],
        operands=shards)


def _scatter_copy(srcs, lands, send_sems, recv_sems, t, k):
    peer = _peer(k)
    return pltpu.make_async_remote_copy(
        src_ref=srcs[t].at[_slot(*peer)], dst_ref=lands[t].at[k],
        send_sem=send_sems.at[t * (N_DEV - 1) + k - 1], recv_sem=recv_sems.at[t * (N_DEV - 1) + k - 1],
        device_id=peer, device_id_type=MESH)


def _scatter_start(partials, name):
    nt = len(partials)

    def body(*refs):
        srcs, lands = refs[:nt], refs[nt:2 * nt]
        send_sems, recv_sems = refs[2 * nt], refs[2 * nt + 1]
        token = refs[-1]
        for k in range(1, N_DEV):
            for t in range(nt):
                _scatter_copy(srcs, lands, send_sems, recv_sems, t, k).start()
        token[...] = jnp.zeros_like(token)

    hbm = pl.BlockSpec(memory_space=pltpu.HBM)
    sem = pl.BlockSpec(memory_space=pltpu.SEMAPHORE)
    shapes = [pltpu.HBM(p.shape, p.dtype) for p in partials]
    lands = [pltpu.with_memory_space_constraint(lax.empty(p.shape, p.dtype), pltpu.HBM) for p in partials]
    srcs = [pltpu.with_memory_space_constraint(p, pltpu.HBM) for p in partials]
    out = _pcall(
        body, name=name,
        out_shape=[pltpu.SemaphoreType.DMA((nt * (N_DEV - 1),))] * 2 + shapes + shapes
        + [jax.ShapeDtypeStruct((8, 128), F32)],
        in_specs=[hbm] * (2 * nt),
        out_specs=[sem, sem] + [hbm] * (2 * nt) + [pl.BlockSpec(memory_space=pltpu.VMEM)],
        input_output_aliases={i: 2 + i for i in range(2 * nt)},
        compiler_params=pltpu.CompilerParams(has_side_effects=pltpu.SideEffectType.DATAFLOW_SIDE_EFFECTING),
    )(*srcs, *lands)
    return (nt, name, out[:-1]), out[-1]


def _scatter_wait(state, after):
    nt, name, (send_sems, recv_sems, *thru) = state

    def body(*refs):
        srcs, lands = refs[:nt], refs[nt:2 * nt]
        send_sems, recv_sems = refs[2 * nt], refs[2 * nt + 1]
        for k in range(1, N_DEV):
            for t in range(nt):
                copy = _scatter_copy(srcs, lands, send_sems, recv_sems, t, k)
                copy.wait_send()
                copy.wait_recv()

    hbm = pl.BlockSpec(memory_space=pltpu.HBM)
    sem = pl.BlockSpec(memory_space=pltpu.SEMAPHORE)
    out = _pcall(
        body, name=name + "_wait",
        out_shape=[pltpu.HBM(a.shape, a.dtype) for a in thru],
        in_specs=[hbm] * (2 * nt) + [sem, sem, pl.BlockSpec(memory_space=pl.ANY)],
        out_specs=[hbm] * (2 * nt),
        input_output_aliases={i: i for i in range(2 * nt)},
        compiler_params=pltpu.CompilerParams(has_side_effects=pltpu.SideEffectType.DATAFLOW_SIDE_EFFECTING),
    )(*thru, send_sems, recv_sems, after)
    return out[:nt], out[nt:]


def _all_reduce_rows(v, dep):
    nv, _, w = v.shape

    def body(v_ref, dep_ref, out_ref, mine_ref, gath_ref, send_sems, recv_sems):
        x, y, c = _position()
        me = _slot(x, y, c)
        mine_ref[...] = jnp.sum(v_ref[...], axis=1)

        def copy(k):
            return pltpu.make_async_remote_copy(
                src_ref=mine_ref, dst_ref=gath_ref.at[me],
                send_sem=send_sems.at[k - 1], recv_sem=recv_sems.at[k - 1],
                device_id=_peer(k), device_id_type=MESH)

        def arrival(k):
            return pltpu.make_async_remote_copy(
                src_ref=mine_ref, dst_ref=gath_ref.at[_slot(*_peer(k))],
                send_sem=send_sems.at[k - 1], recv_sem=recv_sems.at[k - 1],
                device_id=_peer(k), device_id_type=MESH)

        sent = [copy(k) for k in range(1, N_DEV)]
        for cp in sent:
            cp.start()
        gath_ref[me] = mine_ref[...]
        for k in range(1, N_DEV):
            arrival(k).wait_recv()
        for cp in sent:
            cp.wait_send()
        total = gath_ref[0]
        for s in range(1, N_DEV):
            total = total + gath_ref[s]
        out_ref[...] = total

    vmem = pl.BlockSpec(memory_space=pltpu.VMEM)
    return _pcall(
        body, name="all_reduce_rows",
        in_specs=[vmem, pl.BlockSpec(memory_space=pl.ANY)], out_specs=vmem,
        out_shape=jax.ShapeDtypeStruct((nv, w), F32),
        scratch_shapes=[pltpu.VMEM((nv, w), F32), pltpu.VMEM((N_DEV, nv, w), F32),
                        pltpu.SemaphoreType.DMA((7,)), pltpu.SemaphoreType.DMA((7,))],
    )(v, dep)


def _adamw_math(w, g, m, v):
    m2 = ADAM_B1 * m + (1.0 - ADAM_B1) * g
    v2 = ADAM_B2 * v + (1.0 - ADAM_B2) * (g * g)
    m_hat = m2 / (1.0 - ADAM_B1 ** ADAM_STEP)
    v_hat = v2 / (1.0 - ADAM_B2 ** ADAM_STEP)
    delta = -ADAM_LR * (m_hat / (jnp.sqrt(v_hat) + ADAM_EPS) + ADAM_WD * w)
    return delta, m2, v2


def _row_block(r):
    for cand in (256, 176, 128):
        if r % cand == 0:
            return cand
    return r


def _adamw_sharded(me, grads, w, m, v, dep, first_layer=0, prev=None):
    nl = len(grads)
    _, r, c = grads[0][1].shape
    tr = _row_block(r)
    nr = r // tr
    prev = list(prev or ())

    def body(me_ref, *refs):
        grad_refs = refs[:2 * nl]
        w_ref, m_ref, v_ref = refs[2 * nl:2 * nl + 3]
        g_ref, d_ref, m2_ref, v2_ref = refs[-4:]
        layer = pl.program_id(0)

        def total(own_ref, land_ref):
            acc = own_ref[0].astype(F32)
            for k in range(1, N_DEV):
                acc = acc + land_ref[k].astype(F32)
            return acc

        g = total(grad_refs[0], grad_refs[1])
        for k in range(1, nl):
            g = jnp.where(layer == k, total(grad_refs[2 * k], grad_refs[2 * k + 1]), g)
        delta, m2, v2 = _adamw_math(w_ref[0], g, m_ref[0], v_ref[0])
        g_ref[0] = g
        d_ref[0] = delta
        m2_ref[0] = m2
        v2_ref[0] = v2

    def grad_pair_specs(k):
        def rows(l, i):
            return jnp.where(l == k, i, jnp.where(l < k, 0, nr - 1))
        return [pl.BlockSpec((1, tr, c), lambda l, i, me_ref: (me_ref[0], rows(l, i), 0)),
                pl.BlockSpec((N_DEV, tr, c), lambda l, i, me_ref: (0, rows(l, i), 0))]

    grad_specs = [spec for k in range(nl) for spec in grad_pair_specs(k)]
    shard = pl.BlockSpec((1, tr, c), lambda l, i, me_ref: (first_layer + l, i, 0))
    untouched = pl.BlockSpec(memory_space=pl.ANY)
    out = jax.ShapeDtypeStruct(w.shape, F32)
    first_prev = 1 + 2 * nl + 4
    return _pcall(
        body, name="adamw_sharded",
        grid_spec=pltpu.PrefetchScalarGridSpec(
            num_scalar_prefetch=1, grid=(nl, nr),
            in_specs=grad_specs + [shard, shard, shard] + [untouched] * (1 + len(prev)),
            out_specs=[shard, shard, shard, shard]),
        out_shape=[out, out, out, out],
        input_output_aliases={first_prev + k: k for k in range(len(prev))},
        compiler_params=_params(("arbitrary", "arbitrary")),
    )(me, *[a for pair in grads for a in pair], w, m, v, dep, *prev)


def _adamw_small(w, g, m, v):
    def body(w_ref, g_ref, m_ref, v_ref, d_ref, m2_ref, v2_ref):
        delta, m2, v2 = _adamw_math(w_ref[...], g_ref[...], m_ref[...], v_ref[...])
        d_ref[...] = delta
        m2_ref[...] = m2
        v2_ref[...] = v2

    spec = pl.BlockSpec(w.shape, lambda i: (0, 0))
    out = jax.ShapeDtypeStruct(w.shape, F32)
    return _pcall(
        body, name="adamw_small", grid=(1,),
        in_specs=[spec] * 4, out_specs=[spec] * 3, out_shape=[out] * 3,
        compiler_params=_params(("arbitrary",)),
    )(w, g, m, v)


def _pack(arrs):
    flat = jnp.concatenate([a.reshape(-1) for a in arrs])
    n = flat.shape[0]
    rows = -(-n // 1024) * 8
    return jnp.pad(flat, (0, rows * 128 - n)).reshape(rows, 128)


def _unpack(packed, like):
    flat = packed.reshape(-1)
    out, off = [], 0
    for a in like:
        out.append(flat[off:off + a.size].reshape(a.shape))
        off += a.size
    return out


def kernel(x, mem, g_ffn1, w_ffn1_up, w_ffn1_down, g_mix, w_in, conv_w, sinks, g_mem, w_mem_kv, g_grp, w_out, g_ffn2, w_ffn2_up, w_ffn2_down, g_final, loss_target, m_g_ffn1, m_w_ffn1_up, m_w_ffn1_down, m_g_mix, m_w_in, m_conv_w, m_sinks, m_g_mem, m_w_mem_kv, m_g_grp, m_w_out, m_g_ffn2, m_w_ffn2_up, m_w_ffn2_down, m_g_final, v_g_ffn1, v_w_ffn1_up, v_w_ffn1_down, v_g_mix, v_w_in, v_conv_w, v_sinks, v_g_mem, v_w_mem_kv, v_g_grp, v_w_out, v_g_ffn2, v_w_ffn2_up, v_w_ffn2_down, v_g_final):
    depth = g_ffn1.shape[0]
    t, d = x.shape[1], x.shape[2]
    width = max(d, D_MIX)
    me = _slot(*_position())
    conv_shard = conv_w.shape[2]

    xin, memin, tgt = x[0], mem[0], loss_target[0]

    conv_tile = jnp.zeros((depth * 8, 128), F32).at[:, :conv_shard].set(
        jnp.pad(conv_w, ((0, 0), (0, 8 - conv_w.shape[1]), (0, 0))).reshape(depth * 8, conv_shard))
    tr = lambda a: jnp.swapaxes(a, -1, -2)
    bf = lambda a: a.astype(BF16)
    weights = []
    collective_id = 0
    for l in range(depth):
        groups = [[bf(tr(w_ffn1_up[l])), bf(w_ffn1_down[l])] + ([conv_tile] if l == 0 else []),
                  [bf(tr(w_in[l])), bf(w_mem_kv[l]), bf(w_out[l])],
                  [bf(tr(w_ffn2_up[l])), bf(w_ffn2_down[l])]]
        full = []
        for gi, shards in enumerate(groups):
            full.append(_all_gather(shards, f"all_gather_l{l}_g{gi}", collective_id))
            collective_id += 1
        if l == 0:
            conv_full = full[0][2].reshape(N_DEV, depth, 8, 128)[:, :, :3, :conv_shard]
            conv_full = conv_full.transpose(1, 2, 0, 3).reshape(depth, 3, N_DEV * conv_shard)
        weights.append(dict(
            up1=full[0][0].reshape(2, -1, d), dn1=full[0][1].reshape(-1, d),
            win=full[1][0].reshape(D_IN, d), wkv=full[1][1].reshape(d, 2 * D_MEMQ), wout=full[1][2].reshape(D_MIX, d),
            up2=full[2][0].reshape(2, -1, d), dn2=full[2][1].reshape(-1, d)))

    row = lambda a: a.reshape(1, -1)
    bias_key = _bias_table()

    h = xin
    saved = []
    for l in range(depth):
        wl = weights[l]
        s = dict(h0=h)
        h, s["gu1"], s["n1"] = _ffn_fwd(h, row(g_ffn1[l]), wl["up1"], wl["dn1"])
        s["h1"] = h
        s["p"], s["n_mix"], s["qh"] = _mix_proj_fwd(h, row(g_mix[l]), wl["win"])
        s["mkv"], s["nt_mem"] = _memkv_fwd(memin, row(g_mem[l]), wl["wkv"], s["p"])
        s["y"], s["lse"] = _mix_core_fwd(s["p"], s["qh"], s["mkv"], conv_full[l], row(sinks[l]), bias_key)
        h, s["mt"] = _mix_out_fwd(s["y"], h, row(g_grp[l]), wl["wout"])
        s["h2"] = h
        h, s["gu2"], s["n2"] = _ffn_fwd(h, row(g_ffn2[l]), wl["up2"], wl["dn2"])
        saved.append(s)

    dh, loss_part, dg_final = _final_loss(h, row(g_final), tgt)

    small = {}
    dep = loss_part

    def reduce_small(after):
        def lanes(a):
            return jnp.pad(a, ((0, 0), (0, width - a.shape[1])))

        def first_row(a):
            return lanes(jnp.pad(a, ((0, 8 - a.shape[0]), (0, 0))))

        vec_names = ["g_ffn1", "g_mix", "g_mem", "g_grp", "g_ffn2", "sinks"]
        tiles = [lanes(small[n, l]) for n in vec_names for l in range(depth)]
        tiles += [first_row(small["conv_w", l][k:k + 1]) for l in range(depth) for k in range(3)]
        tiles.append(lanes(dg_final))
        n_real = len(tiles)
        tiles.append(lanes(loss_part))
        tiles += [jnp.zeros((8, width), F32)] * (-len(tiles) % 8)
        summed = _all_reduce_rows(jnp.stack(tiles), after)
        loss_all = 0.5 * jnp.sum(summed[n_real]) / d

        def vec(n, wd):
            return jnp.stack([summed[vec_names.index(n) * depth + l, :wd] for l in range(depth)])

        conv_base = len(vec_names) * depth
        conv_grad = jnp.stack([jnp.stack([summed[conv_base + 3 * l + k, :D_CONV] for k in range(3)])
                               for l in range(depth)])
        grads_small = {
            "g_ffn1": vec("g_ffn1", d), "g_mix": vec("g_mix", d), "g_mem": vec("g_mem", d),
            "g_grp": vec("g_grp", D_MIX), "g_ffn2": vec("g_ffn2", d), "sinks": vec("sinks", N_SWA_HEADS),
            "conv_w": lax.dynamic_slice_in_dim(conv_grad, me * conv_shard, conv_shard, axis=2),
            "g_final": summed[n_real - 1, :d],
        }
        small_w = [("g_ffn1", g_ffn1, m_g_ffn1, v_g_ffn1), ("g_mix", g_mix, m_g_mix, v_g_mix),
                   ("conv_w", conv_w, m_conv_w, v_conv_w), ("sinks", sinks, m_sinks, v_sinks),
                   ("g_mem", g_mem, m_g_mem, v_g_mem), ("g_grp", g_grp, m_g_grp, v_g_grp),
                   ("g_ffn2", g_ffn2, m_g_ffn2, v_g_ffn2), ("g_final", g_final, m_g_final, v_g_final)]
        like = [w for _, w, _, _ in small_w]
        packed = _adamw_small(_pack(like), _pack([grads_small[n] for n, _, _, _ in small_w]),
                              _pack([m for _, _, m, _ in small_w]), _pack([v for _, _, _, v in small_w]))
        updated = {n: (grads_small[n], dl, m2, v2)
                   for (n, _, _, _), dl, m2, v2 in zip(small_w, *[_unpack(pk, like) for pk in packed])}
        return loss_all, updated, packed[0]

    started = []

    def scatter(names, partials, label):
        state, token = _scatter_start(partials, f"scatter_grads_{label}")
        started.append((names, state))
        return token

    for l in reversed(range(depth)):
        wl, s = weights[l], saved[l]
        dh, agu, dyb, small["g_ffn2", l] = _ffn_bwd_act(dh, s["h2"], row(g_ffn2[l]), s["gu2"], wl["up2"], wl["dn2"], dep)
        ddn2 = _ffn_bwd_w(agu, 2, 1, dyb, agu, f"ffn_bwd_w_down_l{l}_ffn2").reshape(N_DEV, -1, d)
        dup2 = _ffn_bwd_w(agu, 0, 2, s["n2"], ddn2, f"ffn_bwd_w_up_l{l}_ffn2").reshape(N_DEV, -1, d)
        dep = scatter([("w_ffn2_up", l), ("w_ffn2_down", l)], [dup2, ddn2], f"l{l}_ffn2")
        dyconv, doh, delta, dwout, small["g_grp", l] = _mix_out_bwd(dh, s["y"], row(g_grp[l]), wl["wout"], s["mt"], dep)
        dp, dmkv, small["conv_w", l], small["sinks", l] = _mix_core_bwd(
            s["p"], s["qh"], dyconv, doh, delta, s["lse"], s["mkv"], conv_full[l], row(sinks[l]), bias_key)
        dwkv, small["g_mem", l] = _memkv_bwd(dmkv, memin, row(g_mem[l]), wl["wkv"], s["nt_mem"])
        dh, dwin, small["g_mix", l] = _mix_proj_bwd(dp, dh, s["h1"], row(g_mix[l]), wl["win"], s["n_mix"])
        dep = scatter([("w_in", l), ("w_mem_kv", l), ("w_out", l)],
                      [dwin.reshape(N_DEV, -1, d), dwkv.reshape(N_DEV, -1, 2 * D_MEMQ), dwout.reshape(N_DEV, -1, d)],
                      f"l{l}_mix")
        dh, agu, dyb, small["g_ffn1", l] = _ffn_bwd_act(dh, s["h0"], row(g_ffn1[l]), s["gu1"], wl["up1"], wl["dn1"], dep)
        ddn1 = _ffn_bwd_w(agu, 2, 1, dyb, agu, f"ffn_bwd_w_down_l{l}_ffn1").reshape(N_DEV, -1, d)
        if l > 0:
            dup1 = _ffn_bwd_w(agu, 0, 2, s["n1"], ddn1, f"ffn_bwd_w_up_l{l}_ffn1").reshape(N_DEV, -1, d)
            dep = scatter([("w_ffn1_up", l), ("w_ffn1_down", l)], [dup1, ddn1], f"l{l}_ffn1")
        else:
            dep = scatter([("w_ffn1_down", l)], [ddn1], f"l{l}_ffn1_down")
            dup1 = _ffn_bwd_w(agu, 0, 2, s["n1"], dep, f"ffn_bwd_w_up_l{l}_ffn1").reshape(N_DEV, -1, d)
            dep = scatter([("w_ffn1_up", l)], [dup1], f"l{l}_ffn1_up")
    grad_x = dh[None]

    big = {"w_ffn2_up": (w_ffn2_up, m_w_ffn2_up, v_w_ffn2_up, True), "w_ffn2_down": (w_ffn2_down, m_w_ffn2_down, v_w_ffn2_down, False),
           "w_in": (w_in, m_w_in, v_w_in, True), "w_mem_kv": (w_mem_kv, m_w_mem_kv, v_w_mem_kv, False),
           "w_out": (w_out, m_w_out, v_w_out, False), "w_ffn1_up": (w_ffn1_up, m_w_ffn1_up, v_w_ffn1_up, True),
           "w_ffn1_down": (w_ffn1_down, m_w_ffn1_down, v_w_ffn1_down, False)}
    me_index = jnp.reshape(me, (1,)).astype(jnp.int32)
    sharded, landed, begun = {}, {}, {}
    by_layer = {name for name, _ in started[-1][0]}

    def finish(groups, after):
        for names, state in groups:
            owns, lands = _scatter_wait(state, after)
            for key, own, land in zip(names, owns, lands):
                landed[key] = (own, land)
            after = lands[0]
            for name, l in names:
                w, m, v, transposed = big[name]
                fix = tr if transposed else (lambda a: a)
                if name in by_layer:
                    res = _adamw_sharded(me_index, [landed[name, l]], fix(w), fix(m), fix(v), after, l, begun.get(name))
                    done = name in begun
                    begun[name] = res
                elif all((name, k) in landed for k in range(depth)):
                    res = _adamw_sharded(me_index, [landed[name, k] for k in range(depth)], fix(w), fix(m), fix(v), after)
                    done = True
                else:
                    continue
                if done:
                    sharded[name] = tuple(fix(r) for r in res)
                after = res[0]
        return after

    loss, small_out, dep = reduce_small(finish(started[:-1], dep))
    finish(started[-1:], dep)

    order = ["g_ffn1", "w_ffn1_up", "w_ffn1_down", "g_mix", "w_in", "conv_w", "sinks", "g_mem", "w_mem_kv", "g_grp",
             "w_out", "g_ffn2", "w_ffn2_up", "w_ffn2_down", "g_final"]
    results = {**sharded, **small_out}
    outs = [loss, grad_x]
    for part in range(4):
        outs += [results[n][part] for n in order]
    return tuple(outs)
```

```python
import numpy as np
import jax
import jax.numpy as jnp
from jax import lax
from jax.experimental import pallas as pl
from jax.experimental.pallas import tpu as pltpu
from jax.experimental.pallas import tpu_sc as plsc

F32 = jnp.float32
BF16 = jnp.bfloat16

N_DEV = 8
EPS = 1e-6
N_SWA_HEADS = 8
N_SWA_KV = 2
SWA_GROUP = N_SWA_HEADS // N_SWA_KV
HEAD_DIM = 64
N_MEM_HEADS = 4
D_CONV = 256
BLOCK = 128
D_SWA = N_SWA_HEADS * HEAD_DIM
D_KV = N_SWA_KV * HEAD_DIM
D_MEMQ = N_MEM_HEADS * HEAD_DIM
D_MIX = D_CONV + D_SWA + D_MEMQ
D_IN = 3 * D_CONV + D_SWA + 2 * D_KV + D_MEMQ
COL_BG, COL_CG, COL_U = 0, D_CONV, 2 * D_CONV
COL_Q = 3 * D_CONV
COL_K = COL_Q + D_SWA
COL_V = COL_K + D_KV
COL_QM = COL_V + D_KV
MIX_GROUPS = ((0, D_CONV), (D_CONV, D_CONV + D_SWA), (D_CONV + D_SWA, D_MIX))
SLOPES = tuple(2.0 ** (-8.0 * (i + 1) / N_SWA_HEADS) for i in range(N_SWA_HEADS))
SCALE = HEAD_DIM ** -0.5
NEG = -1e30

ADAM_LR = 0.001
ADAM_B1 = 0.9
ADAM_B2 = 0.999
ADAM_EPS = 1e-08
ADAM_WD = 0.01
ADAM_STEP = 10

V7X_VMEM_BYTES = 64 * 1024 * 1024
VMEM_LIMIT = (V7X_VMEM_BYTES * 3) // 4
VMEM_LIMIT_WIDE = (V7X_VMEM_BYTES * 15) // 16
MESH = pl.DeviceIdType.MESH


def _pcall(body, **kw):
    return pl.pallas_call(body, **kw)


def _params(sem=None, vmem=VMEM_LIMIT):
    return pltpu.CompilerParams(dimension_semantics=sem, vmem_limit_bytes=vmem)


def _dot(a, b):
    return lax.dot_general(a, b, (((1,), (0,)), ((), ())), preferred_element_type=F32)


def _dot_nt(a, b):
    return lax.dot_general(a, b, (((1,), (1,)), ((), ())), preferred_element_type=F32)


def _dot_tn(a, b):
    return lax.dot_general(a, b, (((0,), (0,)), ((), ())), preferred_element_type=F32)


def _rstd(x):
    return lax.rsqrt(jnp.mean(x * x, axis=-1, keepdims=True) + EPS)


def _sigmoid(x):
    return 1.0 / (1.0 + jnp.exp(-x))


def _sum8(x):
    r, w = x.shape
    return jnp.sum(x.reshape(r // 8, 8, w), axis=0)


def _tok_block(t, rows=512):
    return min(rows, t)


def _feat_block(f, parts=N_DEV // 2):
    return f // parts


def _ffn_fwd(h, g, wup_t, wdn):
    t, d = h.shape
    f = wdn.shape[0]
    tm, tf = _tok_block(t), _feat_block(f, 2)
    ni, nj = t // tm, f // tf

    def body(h_ref, g_ref, wup_ref, wdn_ref, ho_ref, gu_ref, n_ref, nt_ref, acc_ref):
        j = pl.program_id(1)

        @pl.when(j == 0)
        def _():
            hh = h_ref[...]
            n = hh * _rstd(hh) * g_ref[...]
            n_ref[...] = n.astype(BF16)
            nt_ref[...] = n.T.astype(BF16)
            acc_ref[...] = jnp.zeros_like(acc_ref)

        nt = nt_ref[...]
        gate = _dot(wup_ref[0], nt)
        up = _dot(wup_ref[1], nt)
        gu_ref[0] = gate.astype(BF16)
        gu_ref[1] = up.astype(BF16)
        a = gate * _sigmoid(gate) * up
        acc_ref[...] += _dot_tn(a.astype(BF16), wdn_ref[...])

        @pl.when(j == nj - 1)
        def _():
            ho_ref[...] = h_ref[...] + 0.5 * acc_ref[...]

    return _pcall(
        body, name="ffn_fwd", grid=(ni, nj),
        in_specs=[pl.BlockSpec((tm, d), lambda i, j: (i, 0)),
                  pl.BlockSpec((1, d), lambda i, j: (0, 0)),
                  pl.BlockSpec((2, tf, d), lambda i, j: (0, j, 0)),
                  pl.BlockSpec((tf, d), lambda i, j: (j, 0))],
        out_specs=[pl.BlockSpec((tm, d), lambda i, j: (i, 0)),
                   pl.BlockSpec((2, tf, tm), lambda i, j: (0, j, i)),
                   pl.BlockSpec((tm, d), lambda i, j: (i, 0))],
        out_shape=[jax.ShapeDtypeStruct((t, d), F32),
                   jax.ShapeDtypeStruct((2, f, t), BF16),
                   jax.ShapeDtypeStruct((t, d), BF16)],
        scratch_shapes=[pltpu.VMEM((d, tm), BF16), pltpu.VMEM((tm, d), F32)],
        compiler_params=_params(("parallel", "arbitrary")),
    )(h, g, wup_t, wdn)


def _ffn_bwd_act(dho, h, g, gu, wup_t, wdn, dep):
    t, d = h.shape
    f = wdn.shape[0]
    tm, tf = _tok_block(t), _feat_block(f, 2)
    ni, nj = t // tm, f // tf

    def body(dho_ref, h_ref, g_ref, gu_ref, wup_ref, wdn_ref, dep_ref, dh_ref, agu_ref, dyb_ref, dg_ref, dyt_ref, acc_ref):
        i = pl.program_id(0)
        j = pl.program_id(1)

        @pl.when(j == 0)
        def _():
            dy0 = 0.5 * dho_ref[...]
            dyb_ref[...] = dy0.astype(BF16)
            dyt_ref[...] = dy0.T.astype(BF16)
            acc_ref[...] = jnp.zeros_like(acc_ref)

        da = _dot(wdn_ref[...], dyt_ref[...]).astype(BF16)
        gate = gu_ref[0]
        up = gu_ref[1]
        sg = _sigmoid(gate)
        silu = gate * sg
        dgate = da * up * (sg * (1.0 + gate * (1.0 - sg)))
        dup = da * silu
        agu_ref[0] = dgate
        agu_ref[1] = dup
        agu_ref[2] = silu * up
        acc_ref[...] += _dot_tn(dgate, wup_ref[0])
        acc_ref[...] += _dot_tn(dup, wup_ref[1])

        @pl.when(j == nj - 1)
        def _():
            hh = h_ref[...]
            r = _rstd(hh)
            xhat = hh * r
            dnf = acc_ref[...]
            dxh = dnf * g_ref[...]
            dh_ref[...] = dho_ref[...] + r * (dxh - xhat * jnp.mean(dxh * xhat, axis=-1, keepdims=True))
            part = _sum8(dnf * xhat)

            @pl.when(i == 0)
            def _():
                dg_ref[...] = part

            @pl.when(i > 0)
            def _():
                dg_ref[...] += part

    return _pcall(
        body, name="ffn_bwd_act", grid=(ni, nj),
        in_specs=[pl.BlockSpec((tm, d), lambda i, j: (i, 0)),
                  pl.BlockSpec((tm, d), lambda i, j: (i, 0)),
                  pl.BlockSpec((1, d), lambda i, j: (0, 0)),
                  pl.BlockSpec((2, tf, tm), lambda i, j: (0, j, i)),
                  pl.BlockSpec((2, tf, d), lambda i, j: (0, j, 0)),
                  pl.BlockSpec((tf, d), lambda i, j: (j, 0)),
                  pl.BlockSpec(memory_space=pl.ANY)],
        out_specs=[pl.BlockSpec((tm, d), lambda i, j: (i, 0)),
                   pl.BlockSpec((3, tf, tm), lambda i, j: (0, j, i)),
                   pl.BlockSpec((tm, d), lambda i, j: (i, 0)),
                   pl.BlockSpec((8, d), lambda i, j: (0, 0))],
        out_shape=[jax.ShapeDtypeStruct((t, d), F32),
                   jax.ShapeDtypeStruct((3, f, t), BF16),
                   jax.ShapeDtypeStruct((t, d), BF16),
                   jax.ShapeDtypeStruct((8, d), F32)],
        scratch_shapes=[pltpu.VMEM((d, tm), BF16), pltpu.VMEM((tm, d), F32)],
        compiler_params=_params(("arbitrary", "arbitrary"), VMEM_LIMIT_WIDE),
    )(dho, h, g, gu, wup_t, wdn, dep)


def _ffn_bwd_w(agu, first, count, rhs, dep, name):
    _, f, t = agu.shape
    d = rhs.shape[1]
    tm, tf = _tok_block(t, 2048), _feat_block(f)
    ni, nj = t // tm, f // tf

    def body(lhs_ref, rhs_ref, dep_ref, dw_ref, acc_ref):
        i = pl.program_id(1)
        @pl.when(i == 0)
        def _():
            acc_ref[...] = jnp.zeros_like(acc_ref)

        rb = rhs_ref[...]
        for k in range(count):
            acc_ref[k] += _dot(lhs_ref[k], rb)

        @pl.when(i == ni - 1)
        def _():
            dw_ref[...] = acc_ref[...].astype(BF16)

    return _pcall(
        body, name=name, grid=(nj, ni),
        in_specs=[pl.BlockSpec((count, tf, tm), lambda j, i: (first // count, j, i)),
                  pl.BlockSpec((tm, d), lambda j, i: (i, 0)),
                  pl.BlockSpec(memory_space=pl.ANY)],
        out_specs=pl.BlockSpec((count, tf, d), lambda j, i: (0, j, 0)),
        out_shape=jax.ShapeDtypeStruct((count, f, d), BF16),
        scratch_shapes=[pltpu.VMEM((count, tf, d), F32)],
        compiler_params=_params(("parallel", "arbitrary")),
    )(agu, rhs, dep)


N_HEADS = N_SWA_HEADS + N_MEM_HEADS


def _q_col(hd):
    return COL_Q + HEAD_DIM * hd if hd < N_SWA_HEADS else COL_QM + HEAD_DIM * (hd - N_SWA_HEADS)


def _mix_proj_fwd(h, g, win_t):
    t, d = h.shape
    tm = _tok_block(t)

    def body(h_ref, g_ref, win_ref, p_ref, n_ref, qh_ref):
        hh = h_ref[...]
        n = (hh * _rstd(hh) * g_ref[...]).astype(BF16)
        n_ref[...] = n
        proj = _dot_nt(n, win_ref[...])
        p_ref[...] = proj.astype(BF16)
        for hd in range(N_HEADS):
            c0 = _q_col(hd)
            qh_ref[hd] = (proj[:, c0:c0 + HEAD_DIM] * SCALE).astype(BF16)

    return _pcall(
        body, name="mix_proj_fwd", grid=(t // tm,),
        in_specs=[pl.BlockSpec((tm, d), lambda i: (i, 0)),
                  pl.BlockSpec((1, d), lambda i: (0, 0)),
                  pl.BlockSpec((D_IN, d), lambda i: (0, 0))],
        out_specs=[pl.BlockSpec((tm, D_IN), lambda i: (i, 0)),
                   pl.BlockSpec((tm, d), lambda i: (i, 0)),
                   pl.BlockSpec((N_HEADS, tm, HEAD_DIM), lambda i: (0, i, 0))],
        out_shape=[jax.ShapeDtypeStruct((t, D_IN), BF16), jax.ShapeDtypeStruct((t, d), BF16),
                   jax.ShapeDtypeStruct((N_HEADS, t, HEAD_DIM), BF16)],
        compiler_params=_params(("parallel",)),
    )(h, g, win_t)


def _memkv_fwd(mem, g, wkv, dep):
    m, d = mem.shape

    def body(mem_ref, g_ref, w_ref, dep_ref, mkv_ref, nt_ref):
        mm = mem_ref[...]
        n = mm * _rstd(mm) * g_ref[...]
        nt_ref[...] = n.T.astype(BF16)
        mkv_ref[...] = _dot(n.astype(BF16), w_ref[...]).astype(BF16)

    return _pcall(
        body, name="memkv_fwd", grid=(1,),
        in_specs=[pl.BlockSpec((m, d), lambda i: (0, 0)),
                  pl.BlockSpec((1, d), lambda i: (0, 0)),
                  pl.BlockSpec((d, 2 * D_MEMQ), lambda i: (0, 0)),
                  pl.BlockSpec(memory_space=pl.ANY)],
        out_specs=[pl.BlockSpec((m, 2 * D_MEMQ), lambda i: (0, 0)),
                   pl.BlockSpec((d, m), lambda i: (0, 0))],
        out_shape=[jax.ShapeDtypeStruct((m, 2 * D_MEMQ), BF16), jax.ShapeDtypeStruct((d, m), BF16)],
        compiler_params=_params(("arbitrary",)),
    )(mem, g, wkv, dep)


def _memkv_bwd(dmkv, mem, g, wkv, nt):
    m, d = mem.shape

    def body(dmkv_ref, mem_ref, g_ref, w_ref, nt_ref, dw_ref, dg_ref):
        db = dmkv_ref[...].astype(BF16)
        dw_ref[...] = _dot(nt_ref[...], db).astype(BF16)
        dn = _dot_nt(db, w_ref[...])
        mm = mem_ref[...]
        dg_ref[...] = _sum8(dn * (mm * _rstd(mm)))

    return _pcall(
        body, name="memkv_bwd", grid=(1,),
        in_specs=[pl.BlockSpec((m, 2 * D_MEMQ), lambda i: (0, 0)),
                  pl.BlockSpec((m, d), lambda i: (0, 0)),
                  pl.BlockSpec((1, d), lambda i: (0, 0)),
                  pl.BlockSpec((d, 2 * D_MEMQ), lambda i: (0, 0)),
                  pl.BlockSpec((d, m), lambda i: (0, 0))],
        out_specs=[pl.BlockSpec((d, 2 * D_MEMQ), lambda i: (0, 0)),
                   pl.BlockSpec((8, d), lambda i: (0, 0))],
        out_shape=[jax.ShapeDtypeStruct((d, 2 * D_MEMQ), BF16), jax.ShapeDtypeStruct((8, d), F32)],
        compiler_params=_params(("arbitrary",)),
    )(dmkv, mem, g, wkv, nt)


def _shift_rows(v, k, edge_rows, row):
    out = pltpu.roll(v, k, 0)
    for r in range(k):
        out = jnp.where(row == r, edge_rows[r], out)
    return out


def _shift_rows_up(v, k, edge_rows, row):
    n = v.shape[0]
    out = pltpu.roll(v, n - k, 0)
    for r in range(k):
        out = jnp.where(row == n - k + r, edge_rows[r], out)
    return out


GROUP_ROWS = SWA_GROUP * BLOCK
BIAS_CUR, BIAS_PREV, BIAS_NONE = 0, 1, 2


def _bias_table():
    tq = np.arange(BLOCK)[:, None]
    sk = np.arange(BLOCK)[None, :]
    slopes = np.asarray(SLOPES, np.float32)[:, None, None]
    cur = np.where(tq >= sk, -slopes * (tq - sk).astype(np.float32), NEG)
    prev = np.where(sk > tq, -slopes * (tq + BLOCK - sk).astype(np.float32), NEG)
    none = np.full_like(cur, NEG)
    tok = np.stack([cur, prev, none]).astype(np.float32).reshape(3, N_SWA_KV, GROUP_ROWS, BLOCK)
    return jnp.asarray(np.ascontiguousarray(tok.transpose(0, 1, 3, 2)))


def _head_cols(hd):
    return D_CONV + HEAD_DIM * hd


def _mix_core_fwd(p, qh, mkv, convw, sinks, bias_key):
    t = p.shape[0]
    m = mkv.shape[0]
    nb = t // BLOCK

    def body(sk_ref, pc_ref, pkv_ref, ppc_ref, ppu_ref, qh_ref, mkv_ref, cw_ref, bc_ref, bp_ref, y_ref, l_ref):
        i = pl.program_id(0)
        prevf = (i > 0).astype(F32)
        row = lax.broadcasted_iota(jnp.int32, (BLOCK, D_CONV), 0)

        bg = pc_ref[:, COL_BG:COL_BG + D_CONV].astype(F32)
        cg = pc_ref[:, COL_CG:COL_CG + D_CONV].astype(F32)
        u = pc_ref[:, COL_U:COL_U + D_CONV].astype(F32)
        vv = cg * u
        pvv = ppc_ref[...].astype(F32) * ppu_ref[...].astype(F32) * prevf
        vv1 = _shift_rows(vv, 1, [pvv[15:16]], row)
        vv2 = _shift_rows(vv, 2, [pvv[14:15], pvv[15:16]], row)
        w = cw_ref[...]
        y_ref[:, 0:D_CONV] = bg * (w[0:1] * vv2 + w[1:2] * vv1 + w[2:3] * vv)

        head_row = lax.broadcasted_iota(jnp.int32, (128, BLOCK), 0)
        lse_t = jnp.zeros((128, BLOCK), F32)
        for kv in range(N_SWA_KV):
            heads = range(kv * SWA_GROUP, (kv + 1) * SWA_GROUP)
            kc = pc_ref[:, COL_K + HEAD_DIM * kv:COL_K + HEAD_DIM * (kv + 1)]
            vc = pc_ref[:, COL_V + HEAD_DIM * kv:COL_V + HEAD_DIM * (kv + 1)]
            kp = pkv_ref[:, HEAD_DIM * kv:HEAD_DIM * (kv + 1)]
            vp = pkv_ref[:, D_KV + HEAD_DIM * kv:D_KV + HEAD_DIM * (kv + 1)]
            qg = qh_ref[kv * SWA_GROUP:(kv + 1) * SWA_GROUP].reshape(GROUP_ROWS, HEAD_DIM)
            sc = _dot_nt(kc, qg) + bc_ref[0, kv]
            sp = _dot_nt(kp, qg) + bp_ref[0, kv]
            sink = jnp.concatenate([jnp.full((1, BLOCK), sk_ref[0, hd], F32) for hd in heads], axis=1)
            mx = jnp.maximum(jnp.max(jnp.maximum(sc, sp), axis=0, keepdims=True), sink)
            ec = jnp.exp(sc - mx)
            ep = jnp.exp(sp - mx)
            den = jnp.sum(ec + ep, axis=0, keepdims=True) + jnp.exp(sink - mx)
            ot = (_dot_tn(vc, ec.astype(BF16)) + _dot_tn(vp, ep.astype(BF16))) / den
            lse = mx + jnp.log(den)
            for gi, hd in enumerate(heads):
                span = slice(gi * BLOCK, (gi + 1) * BLOCK)
                y_ref[:, _head_cols(hd):_head_cols(hd) + HEAD_DIM] = ot[:, span].T
                lse_t = jnp.where(head_row == hd, lse[:, span], lse_t)

        for hm in range(N_MEM_HEADS):
            hd = N_SWA_HEADS + hm
            mk = mkv_ref[:, HEAD_DIM * hm:HEAD_DIM * (hm + 1)]
            mv = mkv_ref[:, D_MEMQ + HEAD_DIM * hm:D_MEMQ + HEAD_DIM * (hm + 1)]
            s = _dot_nt(mk, qh_ref[hd])
            mx = jnp.max(s, axis=0, keepdims=True)
            e = jnp.exp(s - mx)
            den = jnp.sum(e, axis=0, keepdims=True)
            y_ref[:, _head_cols(hd):_head_cols(hd) + HEAD_DIM] = (_dot_tn(mv, e.astype(BF16)) / den).T
            lse_t = jnp.where(head_row == hd, mx + jnp.log(den), lse_t)
        l_ref[...] = lse_t.T

    kv_col = COL_K // (2 * D_KV)
    bias_block = (1, N_SWA_KV, BLOCK, GROUP_ROWS)
    return _pcall(
        body, name="mix_core_fwd", grid=(nb,),
        in_specs=[pl.BlockSpec(memory_space=pltpu.SMEM),
                  pl.BlockSpec((BLOCK, D_IN), lambda i: (i, 0)),
                  pl.BlockSpec((BLOCK, 2 * D_KV), lambda i: (jnp.maximum(i - 1, 0), kv_col)),
                  pl.BlockSpec((16, D_CONV), lambda i: (jnp.maximum(i * (BLOCK // 16) - 1, 0), COL_CG // D_CONV)),
                  pl.BlockSpec((16, D_CONV), lambda i: (jnp.maximum(i * (BLOCK // 16) - 1, 0), COL_U // D_CONV)),
                  pl.BlockSpec((N_HEADS, BLOCK, HEAD_DIM), lambda i: (0, i, 0)),
                  pl.BlockSpec((m, 2 * D_MEMQ), lambda i: (0, 0)),
                  pl.BlockSpec((3, D_CONV), lambda i: (0, 0)),
                  pl.BlockSpec(bias_block, lambda i: (BIAS_CUR, 0, 0, 0)),
                  pl.BlockSpec(bias_block, lambda i: (jnp.where(i == 0, BIAS_NONE, BIAS_PREV), 0, 0, 0))],
        out_specs=[pl.BlockSpec((BLOCK, D_MIX), lambda i: (i, 0)),
                   pl.BlockSpec((BLOCK, 128), lambda i: (i, 0))],
        out_shape=[jax.ShapeDtypeStruct((t, D_MIX), F32), jax.ShapeDtypeStruct((t, 128), F32)],
        compiler_params=_params(("parallel",)),
    )(sinks, p, p, p, p, qh, mkv, convw, bias_key, bias_key)


def _mix_core_bwd(p, qh, dyconv, doh, delta, lse, mkv, convw, sinks, bias_key):
    t = p.shape[0]
    m = mkv.shape[0]
    nb = t // BLOCK

    def body(sk_ref, pc_ref, pkv_ref, ppc_ref, ppu_ref, pnb_ref, dyc_ref, dyn_ref, qc_ref, qn_ref, doc_ref, don_ref,
             dlc_ref, dln_ref, lc_ref, ln_ref, mkv_ref, cw_ref, bp_ref, bct_ref, bnt_ref,
             dp_ref, dmkv_ref, dcw_ref, dsk_ref):
        i = pl.program_id(0)
        prevf = (i > 0).astype(F32)
        nextf = (i < nb - 1).astype(F32)
        row = lax.broadcasted_iota(jnp.int32, (BLOCK, D_CONV), 0)

        @pl.when(i == 0)
        def _():
            dmkv_ref[...] = jnp.zeros_like(dmkv_ref)
            dcw_ref[...] = jnp.zeros_like(dcw_ref)
            dsk_ref[...] = jnp.zeros_like(dsk_ref)

        bg = pc_ref[:, COL_BG:COL_BG + D_CONV].astype(F32)
        cg = pc_ref[:, COL_CG:COL_CG + D_CONV].astype(F32)
        u = pc_ref[:, COL_U:COL_U + D_CONV].astype(F32)
        vv = cg * u
        pvv = ppc_ref[...].astype(F32) * ppu_ref[...].astype(F32) * prevf
        vv1 = _shift_rows(vv, 1, [pvv[15:16]], row)
        vv2 = _shift_rows(vv, 2, [pvv[14:15], pvv[15:16]], row)
        w = cw_ref[...]
        yconv = w[0:1] * vv2 + w[1:2] * vv1 + w[2:3] * vv
        dyo = dyc_ref[...]
        dyc = dyo * bg
        nxt = dyn_ref[...] * pnb_ref[...].astype(F32) * nextf
        d1 = _shift_rows_up(dyc, 1, [nxt[0:1]], row)
        d2 = _shift_rows_up(dyc, 2, [nxt[0:1], nxt[1:2]], row)
        dvv = w[2:3] * dyc + w[1:2] * d1 + w[0:1] * d2
        dp_ref[:, COL_BG:COL_BG + D_CONV] = (dyo * yconv).astype(BF16)
        dp_ref[:, COL_CG:COL_CG + D_CONV] = (dvv * u).astype(BF16)
        dp_ref[:, COL_U:COL_U + D_CONV] = (dvv * cg).astype(BF16)
        dcw_ref[0:1, :] += jnp.sum(dyc * vv2, axis=0, keepdims=True)
        dcw_ref[1:2, :] += jnp.sum(dyc * vv1, axis=0, keepdims=True)
        dcw_ref[2:3, :] += jnp.sum(dyc * vv, axis=0, keepdims=True)

        lse_t, dl_t = lc_ref[...].T, dlc_ref[...].T
        lse_nt, dl_nt = ln_ref[...].T, dln_ref[...].T

        def stack_rows(tile_t, heads):
            return jnp.concatenate([tile_t[hd:hd + 1, :] for hd in heads], axis=1)

        lane8 = jnp.where(lax.broadcasted_iota(jnp.int32, (8, 128), 0) == 0,
                          lax.broadcasted_iota(jnp.int32, (8, 128), 1), -1)
        dsk = jnp.zeros((8, 128), F32)
        for kv in range(N_SWA_KV):
            heads = range(kv * SWA_GROUP, (kv + 1) * SWA_GROUP)
            kc = pc_ref[:, COL_K + HEAD_DIM * kv:COL_K + HEAD_DIM * (kv + 1)]
            vc = pc_ref[:, COL_V + HEAD_DIM * kv:COL_V + HEAD_DIM * (kv + 1)]
            kp = pkv_ref[:, HEAD_DIM * kv:HEAD_DIM * (kv + 1)]
            vp = pkv_ref[:, D_KV + HEAD_DIM * kv:D_KV + HEAD_DIM * (kv + 1)]
            qg = qc_ref[kv * SWA_GROUP:(kv + 1) * SWA_GROUP].reshape(GROUP_ROWS, HEAD_DIM)
            dog = doc_ref[kv * SWA_GROUP:(kv + 1) * SWA_GROUP].reshape(GROUP_ROWS, HEAD_DIM)
            qn = qn_ref[kv * SWA_GROUP:(kv + 1) * SWA_GROUP].reshape(GROUP_ROWS, HEAD_DIM)
            don = don_ref[kv * SWA_GROUP:(kv + 1) * SWA_GROUP].reshape(GROUP_ROWS, HEAD_DIM)
            lse_row, dl_row = stack_rows(lse_t, heads), stack_rows(dl_t, heads)
            ptp = jnp.exp(_dot_nt(kp, qg) + bp_ref[0, kv] - lse_row)
            dstp = (ptp * (_dot_nt(vp, dog) - dl_row)).astype(BF16)
            dq = _dot_tn(dstp, kp)
            pt = jnp.exp(_dot_nt(kc, qg) + bct_ref[0, kv] - lse_row)
            dst = (pt * (_dot_nt(vc, dog) - dl_row)).astype(BF16)
            dv = _dot(pt.astype(BF16), dog)
            dk = _dot(dst, qg)
            dq = dq + _dot_tn(dst, kc)
            ptn = jnp.exp(_dot_nt(kc, qn) + bnt_ref[0, kv] - stack_rows(lse_nt, heads))
            dstn = (ptn * (_dot_nt(vc, don) - stack_rows(dl_nt, heads))).astype(BF16)
            dv = dv + _dot(ptn.astype(BF16), don)
            dk = dk + _dot(dstn, qn)
            dp_ref[:, COL_K + HEAD_DIM * kv:COL_K + HEAD_DIM * (kv + 1)] = dk.astype(BF16)
            dp_ref[:, COL_V + HEAD_DIM * kv:COL_V + HEAD_DIM * (kv + 1)] = dv.astype(BF16)
            sink = jnp.concatenate([jnp.full((1, BLOCK), sk_ref[0, hd], F32) for hd in heads], axis=1)
            sink_term = jnp.exp(sink - lse_row) * dl_row
            for gi, hd in enumerate(heads):
                span = slice(gi * BLOCK, (gi + 1) * BLOCK)
                dp_ref[:, _q_col(hd):_q_col(hd) + HEAD_DIM] = (dq[span] * SCALE).astype(BF16)
                dsk = dsk + jnp.where(lane8 == hd, -jnp.sum(sink_term[:, span], axis=1, keepdims=True), 0.0)
        dsk_ref[...] += dsk

        for hm in range(N_MEM_HEADS):
            hd = N_SWA_HEADS + hm
            qm, dom = qc_ref[hd], doc_ref[hd]
            mk = mkv_ref[:, HEAD_DIM * hm:HEAD_DIM * (hm + 1)]
            mv = mkv_ref[:, D_MEMQ + HEAD_DIM * hm:D_MEMQ + HEAD_DIM * (hm + 1)]
            pt = jnp.exp(_dot_nt(mk, qm) - lse_t[hd:hd + 1, :])
            dst = (pt * (_dot_nt(mv, dom) - dl_t[hd:hd + 1, :])).astype(BF16)
            dp_ref[:, _q_col(hd):_q_col(hd) + HEAD_DIM] = (_dot_tn(dst, mk) * SCALE).astype(BF16)
            dmkv_ref[:, HEAD_DIM * hm:HEAD_DIM * (hm + 1)] += _dot(dst, qm)
            dmkv_ref[:, D_MEMQ + HEAD_DIM * hm:D_MEMQ + HEAD_DIM * (hm + 1)] += _dot(pt.astype(BF16), dom)

    cur = lambda i: (i, 0)
    const = lambda i: (0, 0)
    rows16 = BLOCK // 16
    last16 = t // 16 - 1
    before = lambda col: (lambda i: (jnp.maximum(i * rows16 - 1, 0), col))
    after = lambda i: (jnp.minimum((i + 1) * rows16, last16), 0)
    heads_cur = lambda i: (0, i, 0)
    heads_next = lambda i: (0, jnp.minimum(i + 1, nb - 1), 0)
    stat_next = lambda i: (jnp.minimum(i + 1, nb - 1), 0)
    key_block = (1, N_SWA_KV, BLOCK, GROUP_ROWS)
    head_block = (N_HEADS, BLOCK, HEAD_DIM)
    return _pcall(
        body, name="mix_core_bwd", grid=(nb,),
        in_specs=[pl.BlockSpec(memory_space=pltpu.SMEM),
                  pl.BlockSpec((BLOCK, D_IN), cur),
                  pl.BlockSpec((BLOCK, 2 * D_KV), lambda i: (jnp.maximum(i - 1, 0), COL_K // (2 * D_KV))),
                  pl.BlockSpec((16, D_CONV), before(COL_CG // D_CONV)),
                  pl.BlockSpec((16, D_CONV), before(COL_U // D_CONV)),
                  pl.BlockSpec((16, D_CONV), after),
                  pl.BlockSpec((BLOCK, D_CONV), cur),
                  pl.BlockSpec((16, D_CONV), after),
                  pl.BlockSpec(head_block, heads_cur), pl.BlockSpec(head_block, heads_next),
                  pl.BlockSpec(head_block, heads_cur), pl.BlockSpec(head_block, heads_next),
                  pl.BlockSpec((BLOCK, 128), cur), pl.BlockSpec((BLOCK, 128), stat_next),
                  pl.BlockSpec((BLOCK, 128), cur), pl.BlockSpec((BLOCK, 128), stat_next),
                  pl.BlockSpec((m, 2 * D_MEMQ), const),
                  pl.BlockSpec((3, D_CONV), const),
                  pl.BlockSpec(key_block, lambda i: (jnp.where(i == 0, BIAS_NONE, BIAS_PREV), 0, 0, 0)),
                  pl.BlockSpec(key_block, lambda i: (BIAS_CUR, 0, 0, 0)),
                  pl.BlockSpec(key_block, lambda i: (jnp.where(i == nb - 1, BIAS_NONE, BIAS_PREV), 0, 0, 0))],
        out_specs=[pl.BlockSpec((BLOCK, D_IN), cur),
                   pl.BlockSpec((m, 2 * D_MEMQ), const),
                   pl.BlockSpec((8, D_CONV), const),
                   pl.BlockSpec((8, 128), const)],
        out_shape=[jax.ShapeDtypeStruct((t, D_IN), BF16),
                   jax.ShapeDtypeStruct((m, 2 * D_MEMQ), F32),
                   jax.ShapeDtypeStruct((8, D_CONV), F32),
                   jax.ShapeDtypeStruct((8, 128), F32)],
        compiler_params=_params(("arbitrary",)),
    )(sinks, p, p, p, p, p, dyconv, dyconv, qh, qh, doh, doh, delta, delta, lse, lse, mkv, convw,
      bias_key, bias_key, bias_key)


def _group_norms(y):
    out = []
    for a, b in MIX_GROUPS:
        ys = y[:, a:b]
        r = _rstd(ys)
        out.append((ys * r, r))
    return out


def _mix_out_fwd(y, h, g, wout):
    t, d = h.shape
    tm = _tok_block(t)

    def body(y_ref, h_ref, g_ref, w_ref, ho_ref, mt_ref):
        yhat = jnp.concatenate([yh for yh, _ in _group_norms(y_ref[...])], axis=-1)
        mixed = yhat * g_ref[...]
        mt_ref[...] = mixed.T.astype(BF16)
        ho_ref[...] = h_ref[...] + _dot(mixed.astype(BF16), w_ref[...])

    return _pcall(
        body, name="mix_out_fwd", grid=(t // tm,),
        in_specs=[pl.BlockSpec((tm, D_MIX), lambda i: (i, 0)),
                  pl.BlockSpec((tm, d), lambda i: (i, 0)),
                  pl.BlockSpec((1, D_MIX), lambda i: (0, 0)),
                  pl.BlockSpec((D_MIX, d), lambda i: (0, 0))],
        out_specs=[pl.BlockSpec((tm, d), lambda i: (i, 0)),
                   pl.BlockSpec((D_MIX, tm), lambda i: (0, i))],
        out_shape=[jax.ShapeDtypeStruct((t, d), F32), jax.ShapeDtypeStruct((D_MIX, t), BF16)],
        compiler_params=_params(("parallel",)),
    )(y, h, g, wout)


def _head_indicator():
    ind = np.zeros((D_MIX, 128), np.float32)
    for hd in range(N_HEADS):
        ind[_head_cols(hd):_head_cols(hd) + HEAD_DIM, hd] = 1.0
    return jnp.asarray(ind, BF16)


def _mix_out_bwd(dho, y, g, wout, mt, dep):
    t, d = dho.shape
    tm = _tok_block(t)
    ni = t // tm

    def body(dho_ref, y_ref, g_ref, w_ref, mt_ref, ind_ref, dep_ref, dyc_ref, doh_ref, dl_ref, dw_ref, dg_ref, acc_ref):
        i = pl.program_id(0)
        dhb = dho_ref[...].astype(BF16)
        dm = _dot_nt(dhb, w_ref[...])
        pw = _dot(mt_ref[...], dhb)
        gg = g_ref[...]
        yy = y_ref[...]
        dys = []
        dgs = []
        for (a, b), (yhat, r) in zip(MIX_GROUPS, _group_norms(yy)):
            dmg = dm[:, a:b]
            dgs.append(_sum8(dmg * yhat))
            dyh = dmg * gg[:, a:b]
            dys.append(r * (dyh - yhat * jnp.mean(dyh * yhat, axis=-1, keepdims=True)))
        dy = jnp.concatenate(dys, axis=-1)
        dyc_ref[...] = dy[:, 0:D_CONV]
        for hd in range(N_HEADS):
            doh_ref[hd] = dy[:, _head_cols(hd):_head_cols(hd) + HEAD_DIM].astype(BF16)
        prod = dy * yy
        hi = prod.astype(BF16)
        lo = (prod - hi.astype(F32)).astype(BF16)
        dl_ref[...] = _dot(hi, ind_ref[...]) + _dot(lo, ind_ref[...])
        part = jnp.concatenate(dgs, axis=-1)

        @pl.when(i == 0)
        def _():
            acc_ref[...] = pw
            dg_ref[...] = part

        @pl.when(i > 0)
        def _():
            acc_ref[...] += pw
            dg_ref[...] += part

        @pl.when(i == ni - 1)
        def _():
            dw_ref[...] = acc_ref[...].astype(BF16)

    return _pcall(
        body, name="mix_out_bwd", grid=(ni,),
        in_specs=[pl.BlockSpec((tm, d), lambda i: (i, 0)),
                  pl.BlockSpec((tm, D_MIX), lambda i: (i, 0)),
                  pl.BlockSpec((1, D_MIX), lambda i: (0, 0)),
                  pl.BlockSpec((D_MIX, d), lambda i: (0, 0)),
                  pl.BlockSpec((D_MIX, tm), lambda i: (0, i)),
                  pl.BlockSpec((D_MIX, 128), lambda i: (0, 0)),
                  pl.BlockSpec(memory_space=pl.ANY)],
        out_specs=[pl.BlockSpec((tm, D_CONV), lambda i: (i, 0)),
                   pl.BlockSpec((N_HEADS, tm, HEAD_DIM), lambda i: (0, i, 0)),
                   pl.BlockSpec((tm, 128), lambda i: (i, 0)),
                   pl.BlockSpec((D_MIX, d), lambda i: (0, 0)),
                   pl.BlockSpec((8, D_MIX), lambda i: (0, 0))],
        out_shape=[jax.ShapeDtypeStruct((t, D_CONV), F32),
                   jax.ShapeDtypeStruct((N_HEADS, t, HEAD_DIM), BF16),
                   jax.ShapeDtypeStruct((t, 128), F32),
                   jax.ShapeDtypeStruct((D_MIX, d), BF16),
                   jax.ShapeDtypeStruct((8, D_MIX), F32)],
        scratch_shapes=[pltpu.VMEM((D_MIX, d), F32)],
        compiler_params=_params(("arbitrary",)),
    )(dho, y, g, wout, mt, _head_indicator(), dep)


def _mix_proj_bwd(dp, dho, h, g, win_t, n):
    t, d = h.shape
    tm = _tok_block(t)
    ni = t // tm

    def body(dp_ref, dho_ref, h_ref, g_ref, w_ref, n_ref, dh_ref, dw_ref, dg_ref, acc_ref):
        i = pl.program_id(0)
        dpb = dp_ref[...]
        dn = _dot(dpb, w_ref[...])

        @pl.when(i == 0)
        def _():
            acc_ref[...] = jnp.zeros_like(acc_ref)

        acc_ref[...] += _dot_tn(dpb, n_ref[...])
        hh = h_ref[...]
        r = _rstd(hh)
        xhat = hh * r
        dxh = dn * g_ref[...]
        dh_ref[...] = dho_ref[...] + r * (dxh - xhat * jnp.mean(dxh * xhat, axis=-1, keepdims=True))
        part = _sum8(dn * xhat)

        @pl.when(i == 0)
        def _():
            dg_ref[...] = part

        @pl.when(i > 0)
        def _():
            dg_ref[...] += part

        @pl.when(i == ni - 1)
        def _():
            dw_ref[...] = acc_ref[...].astype(BF16)

    return _pcall(
        body, name="mix_proj_bwd", grid=(ni,),
        in_specs=[pl.BlockSpec((tm, D_IN), lambda i: (i, 0)),
                  pl.BlockSpec((tm, d), lambda i: (i, 0)),
                  pl.BlockSpec((tm, d), lambda i: (i, 0)),
                  pl.BlockSpec((1, d), lambda i: (0, 0)),
                  pl.BlockSpec((D_IN, d), lambda i: (0, 0)),
                  pl.BlockSpec((tm, d), lambda i: (i, 0))],
        out_specs=[pl.BlockSpec((tm, d), lambda i: (i, 0)),
                   pl.BlockSpec((D_IN, d), lambda i: (0, 0)),
                   pl.BlockSpec((8, d), lambda i: (0, 0))],
        out_shape=[jax.ShapeDtypeStruct((t, d), F32),
                   jax.ShapeDtypeStruct((D_IN, d), BF16),
                   jax.ShapeDtypeStruct((8, d), F32)],
        scratch_shapes=[pltpu.VMEM((D_IN, d), F32)],
        compiler_params=_params(("arbitrary",)),
    )(dp, dho, h, g, win_t, n)


def _final_loss(h, g, tgt):
    t, d = h.shape
    tm = _tok_block(t)

    def body(h_ref, g_ref, t_ref, dh_ref, ls_ref, dg_ref):
        i = pl.program_id(0)
        hh = h_ref[...]
        r = _rstd(hh)
        xhat = hh * r
        gg = g_ref[...]
        err = xhat * gg - t_ref[...]
        dy = err * (1.0 / d)
        dxh = dy * gg
        dh_ref[...] = r * (dxh - xhat * jnp.mean(dxh * xhat, axis=-1, keepdims=True))
        lpart = _sum8(err * err)
        gpart = _sum8(dy * xhat)

        @pl.when(i == 0)
        def _():
            ls_ref[...] = lpart
            dg_ref[...] = gpart

        @pl.when(i > 0)
        def _():
            ls_ref[...] += lpart
            dg_ref[...] += gpart

    return _pcall(
        body, name="final_loss", grid=(t // tm,),
        in_specs=[pl.BlockSpec((tm, d), lambda i: (i, 0)),
                  pl.BlockSpec((1, d), lambda i: (0, 0)),
                  pl.BlockSpec((tm, d), lambda i: (i, 0))],
        out_specs=[pl.BlockSpec((tm, d), lambda i: (i, 0)),
                   pl.BlockSpec((8, d), lambda i: (0, 0)),
                   pl.BlockSpec((8, d), lambda i: (0, 0))],
        out_shape=[jax.ShapeDtypeStruct((t, d), F32),
                   jax.ShapeDtypeStruct((8, d), F32),
                   jax.ShapeDtypeStruct((8, d), F32)],
        compiler_params=_params(("arbitrary",)),
    )(h, g, tgt)


def _position():
    return lax.axis_index("x"), lax.axis_index("y"), lax.axis_index("c")


def _flip(v, bit):
    return 1 - v if bit else v


def _peer(k):
    x, y, c = _position()
    return _flip(x, k & 4), _flip(y, k & 2), _flip(c, k & 1)


def _slot(px, py, pc):
    return 4 * px + 2 * py + pc


def _handshake(peers):
    barrier = pltpu.get_barrier_semaphore()
    for peer in peers:
        pl.semaphore_signal(barrier, inc=1, device_id=peer, device_id_type=MESH)
    pl.semaphore_wait(barrier, len(peers))


def _sequencer_call(body, name, collective_id, out_type, scratch_types, operands):
    return pl.kernel(
        body, out_type=out_type, mesh=plsc.ScalarSubcoreMesh(axis_name="sequencer", num_cores=1), name=name,
        scratch_types=scratch_types, compiler_params=pltpu.CompilerParams(collective_id=collective_id),
    )(*operands)


def _all_gather(shards, name, collective_id):
    nt = len(shards)

    def body(*refs):
        xs = refs[:nt]
        outs = refs[nt:2 * nt]
        send_sems, recv_sems, local_sems = refs[2 * nt:]
        x, y, c = _position()
        me, sibling = (x, y, c), (x, y, 1 - c)
        xn, yn, dg = (1 - x, y), (x, 1 - y), (1 - x, 1 - y)
        pick = lambda a, b: (jnp.where(c == 0, a[0], b[0]), jnp.where(c == 0, a[1], b[1]))
        relay_from, relay_to = pick(yn, xn), pick(xn, yn)
        _handshake([sibling, (*xn, c), (*yn, c)])

        def copy(t, k, block, to, src=None):
            dst = outs[t].at[_slot(*block)]
            return pltpu.make_async_remote_copy(
                src_ref=dst if src is None else src, dst_ref=dst,
                send_sem=send_sems.at[t, k], recv_sem=recv_sems.at[t, k],
                device_id=to, device_id_type=MESH)

        mine = [pltpu.make_async_copy(xs[t], outs[t].at[_slot(*me)], local_sems.at[t]) for t in range(nt)]
        for cp in mine:
            cp.start()
        sent = []
        for t in range(nt):
            sent += [copy(t, 0, me, sibling, src=xs[t]), copy(t, 1, me, (*xn, c), src=xs[t]),
                     copy(t, 2, me, (*yn, c), src=xs[t])]
        for cp in sent:
            cp.start()
        for t in range(nt):
            copy(t, 1, (*xn, c), me).wait_recv()
            copy(t, 2, (*yn, c), me).wait_recv()
            passed = [copy(t, 3, (*relay_from, c), (*relay_to, c)),
                      copy(t, 4, (*xn, c), sibling), copy(t, 5, (*yn, c), sibling)]
            for cp in passed:
                cp.start()
            sent += passed
        for t in range(nt):
            copy(t, 3, (*dg, c), me).wait_recv()
            fwd = copy(t, 6, (*dg, c), sibling)
            fwd.start()
            sent.append(fwd)
        for t in range(nt):
            copy(t, 0, sibling, me).wait_recv()
            for k, chip in ((4, xn), (5, yn), (6, dg)):
                copy(t, k, (*chip, 1 - c), me).wait_recv()
        for cp in sent:
            cp.wait_send()
        for cp in mine:
            cp.wait()

    return _sequencer_call(
        body, name, collective_id,
        out_type=[jax.ShapeDtypeStruct((N_DEV,) + s.shape, s.dtype) for s in shards],
        scratch_types=[pltpu.SemaphoreType.DMA((nt, 7)), pltpu.SemaphoreType.DMA((nt, 7)),
                       pltpu.SemaphoreType.DMA((nt,))],
        operands=shards)


def _scatter_copy(srcs, lands, send_sems, recv_sems, t, k):
    peer = _peer(k)
    return pltpu.make_async_remote_copy(
        src_ref=srcs[t].at[_slot(*peer)], dst_ref=lands[t].at[k],
        send_sem=send_sems.at[t * (N_DEV - 1) + k - 1], recv_sem=recv_sems.at[t * (N_DEV - 1) + k - 1],
        device_id=peer, device_id_type=MESH)


def _scatter_start(partials, name):
    nt = len(partials)

    def body(*refs):
        srcs, lands = refs[:nt], refs[nt:2 * nt]
        send_sems, recv_sems = refs[2 * nt], refs[2 * nt + 1]
        token = refs[-1]
        for k in range(1, N_DEV):
            for t in range(nt):
                _scatter_copy(srcs, lands, send_sems, recv_sems, t, k).start()
        token[...] = jnp.zeros_like(token)

    hbm = pl.BlockSpec(memory_space=pltpu.HBM)
    sem = pl.BlockSpec(memory_space=pltpu.SEMAPHORE)
    shapes = [pltpu.HBM(p.shape, p.dtype) for p in partials]
    lands = [pltpu.with_memory_space_constraint(lax.empty(p.shape, p.dtype), pltpu.HBM) for p in partials]
    srcs = [pltpu.with_memory_space_constraint(p, pltpu.HBM) for p in partials]
    out = _pcall(
        body, name=name,
        out_shape=[pltpu.SemaphoreType.DMA((nt * (N_DEV - 1),))] * 2 + shapes + shapes
        + [jax.ShapeDtypeStruct((8, 128), F32)],
        in_specs=[hbm] * (2 * nt),
        out_specs=[sem, sem] + [hbm] * (2 * nt) + [pl.BlockSpec(memory_space=pltpu.VMEM)],
        input_output_aliases={i: 2 + i for i in range(2 * nt)},
        compiler_params=pltpu.CompilerParams(has_side_effects=pltpu.SideEffectType.DATAFLOW_SIDE_EFFECTING),
    )(*srcs, *lands)
    return (nt, name, out[:-1]), out[-1]


def _scatter_wait(state, after):
    nt, name, (send_sems, recv_sems, *thru) = state

    def body(*refs):
        srcs, lands = refs[:nt], refs[nt:2 * nt]
        send_sems, recv_sems = refs[2 * nt], refs[2 * nt + 1]
        for k in range(1, N_DEV):
            for t in range(nt):
                copy = _scatter_copy(srcs, lands, send_sems, recv_sems, t, k)
                copy.wait_send()
                copy.wait_recv()

    hbm = pl.BlockSpec(memory_space=pltpu.HBM)
    sem = pl.BlockSpec(memory_space=pltpu.SEMAPHORE)
    out = _pcall(
        body, name=name + "_wait",
        out_shape=[pltpu.HBM(a.shape, a.dtype) for a in thru],
        in_specs=[hbm] * (2 * nt) + [sem, sem, pl.BlockSpec(memory_space=pl.ANY)],
        out_specs=[hbm] * (2 * nt),
        input_output_aliases={i: i for i in range(2 * nt)},
        compiler_params=pltpu.CompilerParams(has_side_effects=pltpu.SideEffectType.DATAFLOW_SIDE_EFFECTING),
    )(*thru, send_sems, recv_sems, after)
    return out[:nt], out[nt:]


def _all_reduce_rows(v, dep):
    nv, _, w = v.shape

    def body(v_ref, dep_ref, out_ref, mine_ref, gath_ref, send_sems, recv_sems):
        x, y, c = _position()
        me = _slot(x, y, c)
        mine_ref[...] = jnp.sum(v_ref[...], axis=1)

        def copy(k):
            return pltpu.make_async_remote_copy(
                src_ref=mine_ref, dst_ref=gath_ref.at[me],
                send_sem=send_sems.at[k - 1], recv_sem=recv_sems.at[k - 1],
                device_id=_peer(k), device_id_type=MESH)

        def arrival(k):
            return pltpu.make_async_remote_copy(
                src_ref=mine_ref, dst_ref=gath_ref.at[_slot(*_peer(k))],
                send_sem=send_sems.at[k - 1], recv_sem=recv_sems.at[k - 1],
                device_id=_peer(k), device_id_type=MESH)

        sent = [copy(k) for k in range(1, N_DEV)]
        for cp in sent:
            cp.start()
        gath_ref[me] = mine_ref[...]
        for k in range(1, N_DEV):
            arrival(k).wait_recv()
        for cp in sent:
            cp.wait_send()
        total = gath_ref[0]
        for s in range(1, N_DEV):
            total = total + gath_ref[s]
        out_ref[...] = total

    vmem = pl.BlockSpec(memory_space=pltpu.VMEM)
    return _pcall(
        body, name="all_reduce_rows",
        in_specs=[vmem, pl.BlockSpec(memory_space=pl.ANY)], out_specs=vmem,
        out_shape=jax.ShapeDtypeStruct((nv, w), F32),
        scratch_shapes=[pltpu.VMEM((nv, w), F32), pltpu.VMEM((N_DEV, nv, w), F32),
                        pltpu.SemaphoreType.DMA((7,)), pltpu.SemaphoreType.DMA((7,))],
    )(v, dep)


def _adamw_math(w, g, m, v):
    m2 = ADAM_B1 * m + (1.0 - ADAM_B1) * g
    v2 = ADAM_B2 * v + (1.0 - ADAM_B2) * (g * g)
    m_hat = m2 / (1.0 - ADAM_B1 ** ADAM_STEP)
    v_hat = v2 / (1.0 - ADAM_B2 ** ADAM_STEP)
    delta = -ADAM_LR * (m_hat / (jnp.sqrt(v_hat) + ADAM_EPS) + ADAM_WD * w)
    return delta, m2, v2


def _row_block(r):
    for cand in (256, 176, 128):
        if r % cand == 0:
            return cand
    return r


def _adamw_sharded(me, grads, w, m, v, dep, first_layer=0, prev=None):
    nl = len(grads)
    _, r, c = grads[0][1].shape
    tr = _row_block(r)
    nr = r // tr
    prev = list(prev or ())

    def body(me_ref, *refs):
        grad_refs = refs[:2 * nl]
        w_ref, m_ref, v_ref = refs[2 * nl:2 * nl + 3]
        g_ref, d_ref, m2_ref, v2_ref = refs[-4:]
        layer = pl.program_id(0)

        def total(own_ref, land_ref):
            acc = own_ref[0].astype(F32)
            for k in range(1, N_DEV):
                acc = acc + land_ref[k].astype(F32)
            return acc

        g = total(grad_refs[0], grad_refs[1])
        for k in range(1, nl):
            g = jnp.where(layer == k, total(grad_refs[2 * k], grad_refs[2 * k + 1]), g)
        delta, m2, v2 = _adamw_math(w_ref[0], g, m_ref[0], v_ref[0])
        g_ref[0] = g
        d_ref[0] = delta
        m2_ref[0] = m2
        v2_ref[0] = v2

    def grad_pair_specs(k):
        def rows(l, i):
            return jnp.where(l == k, i, jnp.where(l < k, 0, nr - 1))
        return [pl.BlockSpec((1, tr, c), lambda l, i, me_ref: (me_ref[0], rows(l, i), 0)),
                pl.BlockSpec((N_DEV, tr, c), lambda l, i, me_ref: (0, rows(l, i), 0))]

    grad_specs = [spec for k in range(nl) for spec in grad_pair_specs(k)]
    shard = pl.BlockSpec((1, tr, c), lambda l, i, me_ref: (first_layer + l, i, 0))
    untouched = pl.BlockSpec(memory_space=pl.ANY)
    out = jax.ShapeDtypeStruct(w.shape, F32)
    first_prev = 1 + 2 * nl + 4
    return _pcall(
        body, name="adamw_sharded",
        grid_spec=pltpu.PrefetchScalarGridSpec(
            num_scalar_prefetch=1, grid=(nl, nr),
            in_specs=grad_specs + [shard, shard, shard] + [untouched] * (1 + len(prev)),
            out_specs=[shard, shard, shard, shard]),
        out_shape=[out, out, out, out],
        input_output_aliases={first_prev + k: k for k in range(len(prev))},
        compiler_params=_params(("arbitrary", "arbitrary")),
    )(me, *[a for pair in grads for a in pair], w, m, v, dep, *prev)


def _adamw_small(w, g, m, v):
    def body(w_ref, g_ref, m_ref, v_ref, d_ref, m2_ref, v2_ref):
        delta, m2, v2 = _adamw_math(w_ref[...], g_ref[...], m_ref[...], v_ref[...])
        d_ref[...] = delta
        m2_ref[...] = m2
        v2_ref[...] = v2

    spec = pl.BlockSpec(w.shape, lambda i: (0, 0))
    out = jax.ShapeDtypeStruct(w.shape, F32)
    return _pcall(
        body, name="adamw_small", grid=(1,),
        in_specs=[spec] * 4, out_specs=[spec] * 3, out_shape=[out] * 3,
        compiler_params=_params(("arbitrary",)),
    )(w, g, m, v)


def _pack(arrs):
    flat = jnp.concatenate([a.reshape(-1) for a in arrs])
    n = flat.shape[0]
    rows = -(-n // 1024) * 8
    return jnp.pad(flat, (0, rows * 128 - n)).reshape(rows, 128)


def _unpack(packed, like):
    flat = packed.reshape(-1)
    out, off = [], 0
    for a in like:
        out.append(flat[off:off + a.size].reshape(a.shape))
        off += a.size
    return out


def kernel(x, mem, g_ffn1, w_ffn1_up, w_ffn1_down, g_mix, w_in, conv_w, sinks, g_mem, w_mem_kv, g_grp, w_out, g_ffn2, w_ffn2_up, w_ffn2_down, g_final, loss_target, m_g_ffn1, m_w_ffn1_up, m_w_ffn1_down, m_g_mix, m_w_in, m_conv_w, m_sinks, m_g_mem, m_w_mem_kv, m_g_grp, m_w_out, m_g_ffn2, m_w_ffn2_up, m_w_ffn2_down, m_g_final, v_g_ffn1, v_w_ffn1_up, v_w_ffn1_down, v_g_mix, v_w_in, v_conv_w, v_sinks, v_g_mem, v_w_mem_kv, v_g_grp, v_w_out, v_g_ffn2, v_w_ffn2_up, v_w_ffn2_down, v_g_final):
    depth = g_ffn1.shape[0]
    t, d = x.shape[1], x.shape[2]
    width = max(d, D_MIX)
    me = _slot(*_position())
    conv_shard = conv_w.shape[2]

    xin, memin, tgt = x[0], mem[0], loss_target[0]

    conv_tile = jnp.zeros((depth * 8, 128), F32).at[:, :conv_shard].set(
        jnp.pad(conv_w, ((0, 0), (0, 8 - conv_w.shape[1]), (0, 0))).reshape(depth * 8, conv_shard))
    tr = lambda a: jnp.swapaxes(a, -1, -2)
    bf = lambda a: a.astype(BF16)
    weights = []
    collective_id = 0
    for l in range(depth):
        groups = [[bf(tr(w_ffn1_up[l])), bf(w_ffn1_down[l])] + ([conv_tile] if l == 0 else []),
                  [bf(tr(w_in[l])), bf(w_mem_kv[l]), bf(w_out[l])],
                  [bf(tr(w_ffn2_up[l])), bf(w_ffn2_down[l])]]
        full = []
        for gi, shards in enumerate(groups):
            full.append(_all_gather(shards, f"all_gather_l{l}_g{gi}", collective_id))
            collective_id += 1
        if l == 0:
            conv_full = full[0][2].reshape(N_DEV, depth, 8, 128)[:, :, :3, :conv_shard]
            conv_full = conv_full.transpose(1, 2, 0, 3).reshape(depth, 3, N_DEV * conv_shard)
        weights.append(dict(
            up1=full[0][0].reshape(2, -1, d), dn1=full[0][1].reshape(-1, d),
            win=full[1][0].reshape(D_IN, d), wkv=full[1][1].reshape(d, 2 * D_MEMQ), wout=full[1][2].reshape(D_MIX, d),
            up2=full[2][0].reshape(2, -1, d), dn2=full[2][1].reshape(-1, d)))

    row = lambda a: a.reshape(1, -1)
    bias_key = _bias_table()

    h = xin
    saved = []
    for l in range(depth):
        wl = weights[l]
        s = dict(h0=h)
        h, s["gu1"], s["n1"] = _ffn_fwd(h, row(g_ffn1[l]), wl["up1"], wl["dn1"])
        s["h1"] = h
        s["p"], s["n_mix"], s["qh"] = _mix_proj_fwd(h, row(g_mix[l]), wl["win"])
        s["mkv"], s["nt_mem"] = _memkv_fwd(memin, row(g_mem[l]), wl["wkv"], s["p"])
        s["y"], s["lse"] = _mix_core_fwd(s["p"], s["qh"], s["mkv"], conv_full[l], row(sinks[l]), bias_key)
        h, s["mt"] = _mix_out_fwd(s["y"], h, row(g_grp[l]), wl["wout"])
        s["h2"] = h
        h, s["gu2"], s["n2"] = _ffn_fwd(h, row(g_ffn2[l]), wl["up2"], wl["dn2"])
        saved.append(s)

    dh, loss_part, dg_final = _final_loss(h, row(g_final), tgt)

    small = {}
    dep = loss_part

    def reduce_small(after):
        def lanes(a):
            return jnp.pad(a, ((0, 0), (0, width - a.shape[1])))

        def first_row(a):
            return lanes(jnp.pad(a, ((0, 8 - a.shape[0]), (0, 0))))

        vec_names = ["g_ffn1", "g_mix", "g_mem", "g_grp", "g_ffn2", "sinks"]
        tiles = [lanes(small[n, l]) for n in vec_names for l in range(depth)]
        tiles += [first_row(small["conv_w", l][k:k + 1]) for l in range(depth) for k in range(3)]
        tiles.append(lanes(dg_final))
        n_real = len(tiles)
        tiles.append(lanes(loss_part))
        tiles += [jnp.zeros((8, width), F32)] * (-len(tiles) % 8)
        summed = _all_reduce_rows(jnp.stack(tiles), after)
        loss_all = 0.5 * jnp.sum(summed[n_real]) / d

        def vec(n, wd):
            return jnp.stack([summed[vec_names.index(n) * depth + l, :wd] for l in range(depth)])

        conv_base = len(vec_names) * depth
        conv_grad = jnp.stack([jnp.stack([summed[conv_base + 3 * l + k, :D_CONV] for k in range(3)])
                               for l in range(depth)])
        grads_small = {
            "g_ffn1": vec("g_ffn1", d), "g_mix": vec("g_mix", d), "g_mem": vec("g_mem", d),
            "g_grp": vec("g_grp", D_MIX), "g_ffn2": vec("g_ffn2", d), "sinks": vec("sinks", N_SWA_HEADS),
            "conv_w": lax.dynamic_slice_in_dim(conv_grad, me * conv_shard, conv_shard, axis=2),
            "g_final": summed[n_real - 1, :d],
        }
        small_w = [("g_ffn1", g_ffn1, m_g_ffn1, v_g_ffn1), ("g_mix", g_mix, m_g_mix, v_g_mix),
                   ("conv_w", conv_w, m_conv_w, v_conv_w), ("sinks", sinks, m_sinks, v_sinks),
                   ("g_mem", g_mem, m_g_mem, v_g_mem), ("g_grp", g_grp, m_g_grp, v_g_grp),
                   ("g_ffn2", g_ffn2, m_g_ffn2, v_g_ffn2), ("g_final", g_final, m_g_final, v_g_final)]
        like = [w for _, w, _, _ in small_w]
        packed = _adamw_small(_pack(like), _pack([grads_small[n] for n, _, _, _ in small_w]),
                              _pack([m for _, _, m, _ in small_w]), _pack([v for _, _, _, v in small_w]))
        updated = {n: (grads_small[n], dl, m2, v2)
                   for (n, _, _, _), dl, m2, v2 in zip(small_w, *[_unpack(pk, like) for pk in packed])}
        return loss_all, updated, packed[0]

    started = []

    def scatter(names, partials, label):
        state, token = _scatter_start(partials, f"scatter_grads_{label}")
        started.append((names, state))
        return token

    for l in reversed(range(depth)):
        wl, s = weights[l], saved[l]
        dh, agu, dyb, small["g_ffn2", l] = _ffn_bwd_act(dh, s["h2"], row(g_ffn2[l]), s["gu2"], wl["up2"], wl["dn2"], dep)
        ddn2 = _ffn_bwd_w(agu, 2, 1, dyb, agu, f"ffn_bwd_w_down_l{l}_ffn2").reshape(N_DEV, -1, d)
        dup2 = _ffn_bwd_w(agu, 0, 2, s["n2"], ddn2, f"ffn_bwd_w_up_l{l}_ffn2").reshape(N_DEV, -1, d)
        dep = scatter([("w_ffn2_up", l), ("w_ffn2_down", l)], [dup2, ddn2], f"l{l}_ffn2")
        dyconv, doh, delta, dwout, small["g_grp", l] = _mix_out_bwd(dh, s["y"], row(g_grp[l]), wl["wout"], s["mt"], dep)
        dp, dmkv, small["conv_w", l], small["sinks", l] = _mix_core_bwd(
            s["p"], s["qh"], dyconv, doh, delta, s["lse"], s["mkv"], conv_full[l], row(sinks[l]), bias_key)
        dwkv, small["g_mem", l] = _memkv_bwd(dmkv, memin, row(g_mem[l]), wl["wkv"], s["nt_mem"])
        dh, dwin, small["g_mix", l] = _mix_proj_bwd(dp, dh, s["h1"], row(g_mix[l]), wl["win"], s["n_mix"])
        dep = scatter([("w_in", l), ("w_mem_kv", l), ("w_out", l)],
                      [dwin.reshape(N_DEV, -1, d), dwkv.reshape(N_DEV, -1, 2 * D_MEMQ), dwout.reshape(N_DEV, -1, d)],
                      f"l{l}_mix")
        dh, agu, dyb, small["g_ffn1", l] = _ffn_bwd_act(dh, s["h0"], row(g_ffn1[l]), s["gu1"], wl["up1"], wl["dn1"], dep)
        ddn1 = _ffn_bwd_w(agu, 2, 1, dyb, agu, f"ffn_bwd_w_down_l{l}_ffn1").reshape(N_DEV, -1, d)
        if l > 0:
            dup1 = _ffn_bwd_w(agu, 0, 2, s["n1"], ddn1, f"ffn_bwd_w_up_l{l}_ffn1").reshape(N_DEV, -1, d)
            dep = scatter([("w_ffn1_up", l), ("w_ffn1_down", l)], [dup1, ddn1], f"l{l}_ffn1")
        else:
            dep = scatter([("w_ffn1_down", l)], [ddn1], f"l{l}_ffn1_down")
            dup1 = _ffn_bwd_w(agu, 0, 2, s["n1"], dep, f"ffn_bwd_w_up_l{l}_ffn1").reshape(N_DEV, -1, d)
            dep = scatter([("w_ffn1_up", l)], [dup1], f"l{l}_ffn1_up")
    grad_x = dh[None]

    big = {"w_ffn2_up": (w_ffn2_up, m_w_ffn2_up, v_w_ffn2_up, True), "w_ffn2_down": (w_ffn2_down, m_w_ffn2_down, v_w_ffn2_down, False),
           "w_in": (w_in, m_w_in, v_w_in, True), "w_mem_kv": (w_mem_kv, m_w_mem_kv, v_w_mem_kv, False),
           "w_out": (w_out, m_w_out, v_w_out, False), "w_ffn1_up": (w_ffn1_up, m_w_ffn1_up, v_w_ffn1_up, True),
           "w_ffn1_down": (w_ffn1_down, m_w_ffn1_down, v_w_ffn1_down, False)}
    me_index = jnp.reshape(me, (1,)).astype(jnp.int32)
    sharded, landed, begun = {}, {}, {}
    by_layer = {name for name, _ in started[-1][0]}

    def finish(groups, after):
        for names, state in groups:
            owns, lands = _scatter_wait(state, after)
            for key, own, land in zip(names, owns, lands):
                landed[key] = (own, land)
            after = lands[0]
            for name, l in names:
                w, m, v, transposed = big[name]
                fix = tr if transposed else (lambda a: a)
                if name in by_layer:
                    res = _adamw_sharded(me_index, [landed[name, l]], fix(w), fix(m), fix(v), after, l, begun.get(name))
                    done = name in begun
                    begun[name] = res
                elif all((name, k) in landed for k in range(depth)):
                    res = _adamw_sharded(me_index, [landed[name, k] for k in range(depth)], fix(w), fix(m), fix(v), after)
                    done = True
                else:
                    continue
                if done:
                    sharded[name] = tuple(fix(r) for r in res)
                after = res[0]
        return after

    loss, small_out, dep = reduce_small(finish(started[:-1], dep))
    finish(started[-1:], dep)

    order = ["g_ffn1", "w_ffn1_up", "w_ffn1_down", "g_mix", "w_in", "conv_w", "sinks", "g_mem", "w_mem_kv", "g_grp",
             "w_out", "g_ffn2", "w_ffn2_up", "w_ffn2_down", "g_final"]
    results = {**sharded, **small_out}
    outs = [loss, grad_x]
    for part in range(4):
        outs += [results[n][part] for n in order]
    return tuple(outs)
```

```python
import numpy as np
import jax
import jax.numpy as jnp
from jax import lax
from jax.experimental import pallas as pl
from jax.experimental.pallas import tpu as pltpu
from jax.experimental.pallas import tpu_sc as plsc

F32 = jnp.float32
BF16 = jnp.bfloat16

N_DEV = 8
EPS = 1e-6
N_SWA_HEADS = 8
N_SWA_KV = 2
SWA_GROUP = N_SWA_HEADS // N_SWA_KV
HEAD_DIM = 64
N_MEM_HEADS = 4
D_CONV = 256
BLOCK = 128
D_SWA = N_SWA_HEADS * HEAD_DIM
D_KV = N_SWA_KV * HEAD_DIM
D_MEMQ = N_MEM_HEADS * HEAD_DIM
D_MIX = D_CONV + D_SWA + D_MEMQ
D_IN = 3 * D_CONV + D_SWA + 2 * D_KV + D_MEMQ
COL_BG, COL_CG, COL_U = 0, D_CONV, 2 * D_CONV
COL_Q = 3 * D_CONV
COL_K = COL_Q + D_SWA
COL_V = COL_K + D_KV
COL_QM = COL_V + D_KV
MIX_GROUPS = ((0, D_CONV), (D_CONV, D_CONV + D_SWA), (D_CONV + D_SWA, D_MIX))
SLOPES = tuple(2.0 ** (-8.0 * (i + 1) / N_SWA_HEADS) for i in range(N_SWA_HEADS))
SCALE = HEAD_DIM ** -0.5
NEG = -1e30

ADAM_LR = 0.001
ADAM_B1 = 0.9
ADAM_B2 = 0.999
ADAM_EPS = 1e-08
ADAM_WD = 0.01
ADAM_STEP = 10

V7X_VMEM_BYTES = 64 * 1024 * 1024
VMEM_LIMIT = (V7X_VMEM_BYTES * 3) // 4
VMEM_LIMIT_WIDE = (V7X_VMEM_BYTES * 15) // 16
MESH = pl.DeviceIdType.MESH


def _pcall(body, **kw):
    return pl.pallas_call(body, **kw)


def _params(sem=None, vmem=VMEM_LIMIT):
    return pltpu.CompilerParams(dimension_semantics=sem, vmem_limit_bytes=vmem)


def _dot(a, b):
    return lax.dot_general(a, b, (((1,), (0,)), ((), ())), preferred_element_type=F32)


def _dot_nt(a, b):
    return lax.dot_general(a, b, (((1,), (1,)), ((), ())), preferred_element_type=F32)


def _dot_tn(a, b):
    return lax.dot_general(a, b, (((0,), (0,)), ((), ())), preferred_element_type=F32)


def _rstd(x):
    return lax.rsqrt(jnp.mean(x * x, axis=-1, keepdims=True) + EPS)


def _sigmoid(x):
    return 1.0 / (1.0 + jnp.exp(-x))


def _sum8(x):
    r, w = x.shape
    return jnp.sum(x.reshape(r // 8, 8, w), axis=0)


def _tok_block(t, rows=512):
    return min(rows, t)


def _feat_block(f, parts=N_DEV // 2):
    return f // parts


def _ffn_fwd(h, g, wup_t, wdn):
    t, d = h.shape
    f = wdn.shape[0]
    tm, tf = _tok_block(t), _feat_block(f, 2)
    ni, nj = t // tm, f // tf

    def body(h_ref, g_ref, wup_ref, wdn_ref, ho_ref, gu_ref, n_ref, nt_ref, acc_ref):
        j = pl.program_id(1)

        @pl.when(j == 0)
        def _():
            hh = h_ref[...]
            n = hh * _rstd(hh) * g_ref[...]
            n_ref[...] = n.astype(BF16)
            nt_ref[...] = n.T.astype(BF16)
            acc_ref[...] = jnp.zeros_like(acc_ref)

        nt = nt_ref[...]
        gate = _dot(wup_ref[0], nt)
        up = _dot(wup_ref[1], nt)
        gu_ref[0] = gate.astype(BF16)
        gu_ref[1] = up.astype(BF16)
        a = gate * _sigmoid(gate) * up
        acc_ref[...] += _dot_tn(a.astype(BF16), wdn_ref[...])

        @pl.when(j == nj - 1)
        def _():
            ho_ref[...] = h_ref[...] + 0.5 * acc_ref[...]

    return _pcall(
        body, name="ffn_fwd", grid=(ni, nj),
        in_specs=[pl.BlockSpec((tm, d), lambda i, j: (i, 0)),
                  pl.BlockSpec((1, d), lambda i, j: (0, 0)),
                  pl.BlockSpec((2, tf, d), lambda i, j: (0, j, 0)),
                  pl.BlockSpec((tf, d), lambda i, j: (j, 0))],
        out_specs=[pl.BlockSpec((tm, d), lambda i, j: (i, 0)),
                   pl.BlockSpec((2, tf, tm), lambda i, j: (0, j, i)),
                   pl.BlockSpec((tm, d), lambda i, j: (i, 0))],
        out_shape=[jax.ShapeDtypeStruct((t, d), F32),
                   jax.ShapeDtypeStruct((2, f, t), BF16),
                   jax.ShapeDtypeStruct((t, d), BF16)],
        scratch_shapes=[pltpu.VMEM((d, tm), BF16), pltpu.VMEM((tm, d), F32)],
        compiler_params=_params(("parallel", "arbitrary")),
    )(h, g, wup_t, wdn)


def _ffn_bwd_act(dho, h, g, gu, wup_t, wdn, dep):
    t, d = h.shape
    f = wdn.shape[0]
    tm, tf = _tok_block(t), _feat_block(f, 2)
    ni, nj = t // tm, f // tf

    def body(dho_ref, h_ref, g_ref, gu_ref, wup_ref, wdn_ref, dep_ref, dh_ref, agu_ref, dyb_ref, dg_ref, dyt_ref, acc_ref):
        i = pl.program_id(0)
        j = pl.program_id(1)

        @pl.when(j == 0)
        def _():
            dy0 = 0.5 * dho_ref[...]
            dyb_ref[...] = dy0.astype(BF16)
            dyt_ref[...] = dy0.T.astype(BF16)
            acc_ref[...] = jnp.zeros_like(acc_ref)

        da = _dot(wdn_ref[...], dyt_ref[...]).astype(BF16)
        gate = gu_ref[0]
        up = gu_ref[1]
        sg = _sigmoid(gate)
        silu = gate * sg
        dgate = da * up * (sg * (1.0 + gate * (1.0 - sg)))
        dup = da * silu
        agu_ref[0] = dgate
        agu_ref[1] = dup
        agu_ref[2] = silu * up
        acc_ref[...] += _dot_tn(dgate, wup_ref[0])
        acc_ref[...] += _dot_tn(dup, wup_ref[1])

        @pl.when(j == nj - 1)
        def _():
            hh = h_ref[...]
            r = _rstd(hh)
            xhat = hh * r
            dnf = acc_ref[...]
            dxh = dnf * g_ref[...]
            dh_ref[...] = dho_ref[...] + r * (dxh - xhat * jnp.mean(dxh * xhat, axis=-1, keepdims=True))
            part = _sum8(dnf * xhat)

            @pl.when(i == 0)
            def _():
                dg_ref[...] = part

            @pl.when(i > 0)
            def _():
                dg_ref[...] += part

    return _pcall(
        body, name="ffn_bwd_act", grid=(ni, nj),
        in_specs=[pl.BlockSpec((tm, d), lambda i, j: (i, 0)),
                  pl.BlockSpec((tm, d), lambda i, j: (i, 0)),
                  pl.BlockSpec((1, d), lambda i, j: (0, 0)),
                  pl.BlockSpec((2, tf, tm), lambda i, j: (0, j, i)),
                  pl.BlockSpec((2, tf, d), lambda i, j: (0, j, 0)),
                  pl.BlockSpec((tf, d), lambda i, j: (j, 0)),
                  pl.BlockSpec(memory_space=pl.ANY)],
        out_specs=[pl.BlockSpec((tm, d), lambda i, j: (i, 0)),
                   pl.BlockSpec((3, tf, tm), lambda i, j: (0, j, i)),
                   pl.BlockSpec((tm, d), lambda i, j: (i, 0)),
                   pl.BlockSpec((8, d), lambda i, j: (0, 0))],
        out_shape=[jax.ShapeDtypeStruct((t, d), F32),
                   jax.ShapeDtypeStruct((3, f, t), BF16),
                   jax.ShapeDtypeStruct((t, d), BF16),
                   jax.ShapeDtypeStruct((8, d), F32)],
        scratch_shapes=[pltpu.VMEM((d, tm), BF16), pltpu.VMEM((tm, d), F32)],
        compiler_params=_params(("arbitrary", "arbitrary"), VMEM_LIMIT_WIDE),
    )(dho, h, g, gu, wup_t, wdn, dep)


def _ffn_bwd_w(agu, first, count, rhs, dep, name):
    _, f, t = agu.shape
    d = rhs.shape[1]
    tm = _tok_block(t, 2048)
    tf = _feat_block(f, 2) if count == 1 else _feat_block(f)
    ni, nj = t // tm, f // tf

    def body(lhs_ref, rhs_ref, dep_ref, dw_ref, acc_ref):
        i = pl.program_id(1)
        @pl.when(i == 0)
        def _():
            acc_ref[...] = jnp.zeros_like(acc_ref)

        rb = rhs_ref[...]
        for k in range(count):
            acc_ref[k] += _dot(lhs_ref[k], rb)

        @pl.when(i == ni - 1)
        def _():
            dw_ref[...] = acc_ref[...].astype(BF16)

    return _pcall(
        body, name=name, grid=(nj, ni),
        in_specs=[pl.BlockSpec((count, tf, tm), lambda j, i: (first // count, j, i)),
                  pl.BlockSpec((tm, d), lambda j, i: (i, 0)),
                  pl.BlockSpec(memory_space=pl.ANY)],
        out_specs=pl.BlockSpec((count, tf, d), lambda j, i: (0, j, 0)),
        out_shape=jax.ShapeDtypeStruct((count, f, d), BF16),
        scratch_shapes=[pltpu.VMEM((count, tf, d), F32)],
        compiler_params=_params(("parallel", "arbitrary")),
    )(agu, rhs, dep)


N_HEADS = N_SWA_HEADS + N_MEM_HEADS


def _q_col(hd):
    return COL_Q + HEAD_DIM * hd if hd < N_SWA_HEADS else COL_QM + HEAD_DIM * (hd - N_SWA_HEADS)


def _mix_proj_fwd(h, g, win_t):
    t, d = h.shape
    tm = _tok_block(t)

    def body(h_ref, g_ref, win_ref, p_ref, n_ref, qh_ref):
        hh = h_ref[...]
        n = (hh * _rstd(hh) * g_ref[...]).astype(BF16)
        n_ref[...] = n
        proj = _dot_nt(n, win_ref[...])
        p_ref[...] = proj.astype(BF16)
        for hd in range(N_HEADS):
            c0 = _q_col(hd)
            qh_ref[hd] = (proj[:, c0:c0 + HEAD_DIM] * SCALE).astype(BF16)

    return _pcall(
        body, name="mix_proj_fwd", grid=(t // tm,),
        in_specs=[pl.BlockSpec((tm, d), lambda i: (i, 0)),
                  pl.BlockSpec((1, d), lambda i: (0, 0)),
                  pl.BlockSpec((D_IN, d), lambda i: (0, 0))],
        out_specs=[pl.BlockSpec((tm, D_IN), lambda i: (i, 0)),
                   pl.BlockSpec((tm, d), lambda i: (i, 0)),
                   pl.BlockSpec((N_HEADS, tm, HEAD_DIM), lambda i: (0, i, 0))],
        out_shape=[jax.ShapeDtypeStruct((t, D_IN), BF16), jax.ShapeDtypeStruct((t, d), BF16),
                   jax.ShapeDtypeStruct((N_HEADS, t, HEAD_DIM), BF16)],
        compiler_params=_params(("parallel",)),
    )(h, g, win_t)


def _memkv_fwd(mem, g, wkv, dep):
    m, d = mem.shape

    def body(mem_ref, g_ref, w_ref, dep_ref, mkv_ref, nt_ref):
        mm = mem_ref[...]
        n = mm * _rstd(mm) * g_ref[...]
        nt_ref[...] = n.T.astype(BF16)
        mkv_ref[...] = _dot(n.astype(BF16), w_ref[...]).astype(BF16)

    return _pcall(
        body, name="memkv_fwd", grid=(1,),
        in_specs=[pl.BlockSpec((m, d), lambda i: (0, 0)),
                  pl.BlockSpec((1, d), lambda i: (0, 0)),
                  pl.BlockSpec((d, 2 * D_MEMQ), lambda i: (0, 0)),
                  pl.BlockSpec(memory_space=pl.ANY)],
        out_specs=[pl.BlockSpec((m, 2 * D_MEMQ), lambda i: (0, 0)),
                   pl.BlockSpec((d, m), lambda i: (0, 0))],
        out_shape=[jax.ShapeDtypeStruct((m, 2 * D_MEMQ), BF16), jax.ShapeDtypeStruct((d, m), BF16)],
        compiler_params=_params(("arbitrary",)),
    )(mem, g, wkv, dep)


def _memkv_bwd(dmkv, mem, g, wkv, nt):
    m, d = mem.shape

    def body(dmkv_ref, mem_ref, g_ref, w_ref, nt_ref, dw_ref, dg_ref):
        db = dmkv_ref[...].astype(BF16)
        dw_ref[...] = _dot(nt_ref[...], db).astype(BF16)
        dn = _dot_nt(db, w_ref[...])
        mm = mem_ref[...]
        dg_ref[...] = _sum8(dn * (mm * _rstd(mm)))

    return _pcall(
        body, name="memkv_bwd", grid=(1,),
        in_specs=[pl.BlockSpec((m, 2 * D_MEMQ), lambda i: (0, 0)),
                  pl.BlockSpec((m, d), lambda i: (0, 0)),
                  pl.BlockSpec((1, d), lambda i: (0, 0)),
                  pl.BlockSpec((d, 2 * D_MEMQ), lambda i: (0, 0)),
                  pl.BlockSpec((d, m), lambda i: (0, 0))],
        out_specs=[pl.BlockSpec((d, 2 * D_MEMQ), lambda i: (0, 0)),
                   pl.BlockSpec((8, d), lambda i: (0, 0))],
        out_shape=[jax.ShapeDtypeStruct((d, 2 * D_MEMQ), BF16), jax.ShapeDtypeStruct((8, d), F32)],
        compiler_params=_params(("arbitrary",)),
    )(dmkv, mem, g, wkv, nt)


def _shift_rows(v, k, edge_rows, row):
    out = pltpu.roll(v, k, 0)
    for r in range(k):
        out = jnp.where(row == r, edge_rows[r], out)
    return out


def _shift_rows_up(v, k, edge_rows, row):
    n = v.shape[0]
    out = pltpu.roll(v, n - k, 0)
    for r in range(k):
        out = jnp.where(row == n - k + r, edge_rows[r], out)
    return out


GROUP_ROWS = SWA_GROUP * BLOCK
BIAS_CUR, BIAS_PREV, BIAS_NONE = 0, 1, 2


def _bias_table():
    tq = np.arange(BLOCK)[:, None]
    sk = np.arange(BLOCK)[None, :]
    slopes = np.asarray(SLOPES, np.float32)[:, None, None]
    cur = np.where(tq >= sk, -slopes * (tq - sk).astype(np.float32), NEG)
    prev = np.where(sk > tq, -slopes * (tq + BLOCK - sk).astype(np.float32), NEG)
    none = np.full_like(cur, NEG)
    tok = np.stack([cur, prev, none]).astype(np.float32).reshape(3, N_SWA_KV, GROUP_ROWS, BLOCK)
    return jnp.asarray(np.ascontiguousarray(tok.transpose(0, 1, 3, 2)))


def _head_cols(hd):
    return D_CONV + HEAD_DIM * hd


def _mix_core_fwd(p, qh, mkv, convw, sinks, bias_key):
    t = p.shape[0]
    m = mkv.shape[0]
    nb = t // BLOCK

    def body(sk_ref, pc_ref, pkv_ref, ppc_ref, ppu_ref, qh_ref, mkv_ref, cw_ref, bc_ref, bp_ref, y_ref, l_ref):
        i = pl.program_id(0)
        prevf = (i > 0).astype(F32)
        row = lax.broadcasted_iota(jnp.int32, (BLOCK, D_CONV), 0)

        bg = pc_ref[:, COL_BG:COL_BG + D_CONV].astype(F32)
        cg = pc_ref[:, COL_CG:COL_CG + D_CONV].astype(F32)
        u = pc_ref[:, COL_U:COL_U + D_CONV].astype(F32)
        vv = cg * u
        pvv = ppc_ref[...].astype(F32) * ppu_ref[...].astype(F32) * prevf
        vv1 = _shift_rows(vv, 1, [pvv[15:16]], row)
        vv2 = _shift_rows(vv, 2, [pvv[14:15], pvv[15:16]], row)
        w = cw_ref[...]
        y_ref[:, 0:D_CONV] = bg * (w[0:1] * vv2 + w[1:2] * vv1 + w[2:3] * vv)

        head_row = lax.broadcasted_iota(jnp.int32, (128, BLOCK), 0)
        lse_t = jnp.zeros((128, BLOCK), F32)
        for kv in range(N_SWA_KV):
            heads = range(kv * SWA_GROUP, (kv + 1) * SWA_GROUP)
            kc = pc_ref[:, COL_K + HEAD_DIM * kv:COL_K + HEAD_DIM * (kv + 1)]
            vc = pc_ref[:, COL_V + HEAD_DIM * kv:COL_V + HEAD_DIM * (kv + 1)]
            kp = pkv_ref[:, HEAD_DIM * kv:HEAD_DIM * (kv + 1)]
            vp = pkv_ref[:, D_KV + HEAD_DIM * kv:D_KV + HEAD_DIM * (kv + 1)]
            qg = qh_ref[kv * SWA_GROUP:(kv + 1) * SWA_GROUP].reshape(GROUP_ROWS, HEAD_DIM)
            sc = _dot_nt(kc, qg) + bc_ref[0, kv]
            sp = _dot_nt(kp, qg) + bp_ref[0, kv]
            sink = jnp.concatenate([jnp.full((1, BLOCK), sk_ref[0, hd], F32) for hd in heads], axis=1)
            mx = jnp.maximum(jnp.max(jnp.maximum(sc, sp), axis=0, keepdims=True), sink)
            ec = jnp.exp(sc - mx)
            ep = jnp.exp(sp - mx)
            den = jnp.sum(ec + ep, axis=0, keepdims=True) + jnp.exp(sink - mx)
            ot = (_dot_tn(vc, ec.astype(BF16)) + _dot_tn(vp, ep.astype(BF16))) / den
            lse = mx + jnp.log(den)
            for gi, hd in enumerate(heads):
                span = slice(gi * BLOCK, (gi + 1) * BLOCK)
                y_ref[:, _head_cols(hd):_head_cols(hd) + HEAD_DIM] = ot[:, span].T
                lse_t = jnp.where(head_row == hd, lse[:, span], lse_t)

        for hm in range(N_MEM_HEADS):
            hd = N_SWA_HEADS + hm
            mk = mkv_ref[:, HEAD_DIM * hm:HEAD_DIM * (hm + 1)]
            mv = mkv_ref[:, D_MEMQ + HEAD_DIM * hm:D_MEMQ + HEAD_DIM * (hm + 1)]
            s = _dot_nt(mk, qh_ref[hd])
            mx = jnp.max(s, axis=0, keepdims=True)
            e = jnp.exp(s - mx)
            den = jnp.sum(e, axis=0, keepdims=True)
            y_ref[:, _head_cols(hd):_head_cols(hd) + HEAD_DIM] = (_dot_tn(mv, e.astype(BF16)) / den).T
            lse_t = jnp.where(head_row == hd, mx + jnp.log(den), lse_t)
        l_ref[...] = lse_t.T

    kv_col = COL_K // (2 * D_KV)
    bias_block = (1, N_SWA_KV, BLOCK, GROUP_ROWS)
    return _pcall(
        body, name="mix_core_fwd", grid=(nb,),
        in_specs=[pl.BlockSpec(memory_space=pltpu.SMEM),
                  pl.BlockSpec((BLOCK, D_IN), lambda i: (i, 0)),
                  pl.BlockSpec((BLOCK, 2 * D_KV), lambda i: (jnp.maximum(i - 1, 0), kv_col)),
                  pl.BlockSpec((16, D_CONV), lambda i: (jnp.maximum(i * (BLOCK // 16) - 1, 0), COL_CG // D_CONV)),
                  pl.BlockSpec((16, D_CONV), lambda i: (jnp.maximum(i * (BLOCK // 16) - 1, 0), COL_U // D_CONV)),
                  pl.BlockSpec((N_HEADS, BLOCK, HEAD_DIM), lambda i: (0, i, 0)),
                  pl.BlockSpec((m, 2 * D_MEMQ), lambda i: (0, 0)),
                  pl.BlockSpec((3, D_CONV), lambda i: (0, 0)),
                  pl.BlockSpec(bias_block, lambda i: (BIAS_CUR, 0, 0, 0)),
                  pl.BlockSpec(bias_block, lambda i: (jnp.where(i == 0, BIAS_NONE, BIAS_PREV), 0, 0, 0))],
        out_specs=[pl.BlockSpec((BLOCK, D_MIX), lambda i: (i, 0)),
                   pl.BlockSpec((BLOCK, 128), lambda i: (i, 0))],
        out_shape=[jax.ShapeDtypeStruct((t, D_MIX), F32), jax.ShapeDtypeStruct((t, 128), F32)],
        compiler_params=_params(("parallel",)),
    )(sinks, p, p, p, p, qh, mkv, convw, bias_key, bias_key)


def _mix_core_bwd(p, qh, dyconv, doh, delta, lse, mkv, convw, sinks, bias_key):
    t = p.shape[0]
    m = mkv.shape[0]
    nb = t // BLOCK

    def body(sk_ref, pc_ref, pkv_ref, ppc_ref, ppu_ref, pnb_ref, dyc_ref, dyn_ref, qc_ref, qn_ref, doc_ref, don_ref,
             dlc_ref, dln_ref, lc_ref, ln_ref, mkv_ref, cw_ref, bp_ref, bct_ref, bnt_ref,
             dp_ref, dmkv_ref, dcw_ref, dsk_ref):
        i = pl.program_id(0)
        prevf = (i > 0).astype(F32)
        nextf = (i < nb - 1).astype(F32)
        row = lax.broadcasted_iota(jnp.int32, (BLOCK, D_CONV), 0)

        @pl.when(i == 0)
        def _():
            dmkv_ref[...] = jnp.zeros_like(dmkv_ref)
            dcw_ref[...] = jnp.zeros_like(dcw_ref)
            dsk_ref[...] = jnp.zeros_like(dsk_ref)

        bg = pc_ref[:, COL_BG:COL_BG + D_CONV].astype(F32)
        cg = pc_ref[:, COL_CG:COL_CG + D_CONV].astype(F32)
        u = pc_ref[:, COL_U:COL_U + D_CONV].astype(F32)
        vv = cg * u
        pvv = ppc_ref[...].astype(F32) * ppu_ref[...].astype(F32) * prevf
        vv1 = _shift_rows(vv, 1, [pvv[15:16]], row)
        vv2 = _shift_rows(vv, 2, [pvv[14:15], pvv[15:16]], row)
        w = cw_ref[...]
        yconv = w[0:1] * vv2 + w[1:2] * vv1 + w[2:3] * vv
        dyo = dyc_ref[...]
        dyc = dyo * bg
        nxt = dyn_ref[...] * pnb_ref[...].astype(F32) * nextf
        d1 = _shift_rows_up(dyc, 1, [nxt[0:1]], row)
        d2 = _shift_rows_up(dyc, 2, [nxt[0:1], nxt[1:2]], row)
        dvv = w[2:3] * dyc + w[1:2] * d1 + w[0:1] * d2
        dp_ref[:, COL_BG:COL_BG + D_CONV] = (dyo * yconv).astype(BF16)
        dp_ref[:, COL_CG:COL_CG + D_CONV] = (dvv * u).astype(BF16)
        dp_ref[:, COL_U:COL_U + D_CONV] = (dvv * cg).astype(BF16)
        dcw_ref[0:1, :] += jnp.sum(dyc * vv2, axis=0, keepdims=True)
        dcw_ref[1:2, :] += jnp.sum(dyc * vv1, axis=0, keepdims=True)
        dcw_ref[2:3, :] += jnp.sum(dyc * vv, axis=0, keepdims=True)

        lse_t, dl_t = lc_ref[...].T, dlc_ref[...].T
        lse_nt, dl_nt = ln_ref[...].T, dln_ref[...].T

        def stack_rows(tile_t, heads):
            return jnp.concatenate([tile_t[hd:hd + 1, :] for hd in heads], axis=1)

        lane8 = jnp.where(lax.broadcasted_iota(jnp.int32, (8, 128), 0) == 0,
                          lax.broadcasted_iota(jnp.int32, (8, 128), 1), -1)
        dsk = jnp.zeros((8, 128), F32)
        for kv in range(N_SWA_KV):
            heads = range(kv * SWA_GROUP, (kv + 1) * SWA_GROUP)
            kc = pc_ref[:, COL_K + HEAD_DIM * kv:COL_K + HEAD_DIM * (kv + 1)]
            vc = pc_ref[:, COL_V + HEAD_DIM * kv:COL_V + HEAD_DIM * (kv + 1)]
            kp = pkv_ref[:, HEAD_DIM * kv:HEAD_DIM * (kv + 1)]
            vp = pkv_ref[:, D_KV + HEAD_DIM * kv:D_KV + HEAD_DIM * (kv + 1)]
            qg = qc_ref[kv * SWA_GROUP:(kv + 1) * SWA_GROUP].reshape(GROUP_ROWS, HEAD_DIM)
            dog = doc_ref[kv * SWA_GROUP:(kv + 1) * SWA_GROUP].reshape(GROUP_ROWS, HEAD_DIM)
            qn = qn_ref[kv * SWA_GROUP:(kv + 1) * SWA_GROUP].reshape(GROUP_ROWS, HEAD_DIM)
            don = don_ref[kv * SWA_GROUP:(kv + 1) * SWA_GROUP].reshape(GROUP_ROWS, HEAD_DIM)
            lse_row, dl_row = stack_rows(lse_t, heads), stack_rows(dl_t, heads)
            ptp = jnp.exp(_dot_nt(kp, qg) + bp_ref[0, kv] - lse_row)
            dstp = (ptp * (_dot_nt(vp, dog) - dl_row)).astype(BF16)
            dq = _dot_tn(dstp, kp)
            pt = jnp.exp(_dot_nt(kc, qg) + bct_ref[0, kv] - lse_row)
            dst = (pt * (_dot_nt(vc, dog) - dl_row)).astype(BF16)
            dv = _dot(pt.astype(BF16), dog)
            dk = _dot(dst, qg)
            dq = dq + _dot_tn(dst, kc)
            ptn = jnp.exp(_dot_nt(kc, qn) + bnt_ref[0, kv] - stack_rows(lse_nt, heads))
            dstn = (ptn * (_dot_nt(vc, don) - stack_rows(dl_nt, heads))).astype(BF16)
            dv = dv + _dot(ptn.astype(BF16), don)
            dk = dk + _dot(dstn, qn)
            dp_ref[:, COL_K + HEAD_DIM * kv:COL_K + HEAD_DIM * (kv + 1)] = dk.astype(BF16)
            dp_ref[:, COL_V + HEAD_DIM * kv:COL_V + HEAD_DIM * (kv + 1)] = dv.astype(BF16)
            sink = jnp.concatenate([jnp.full((1, BLOCK), sk_ref[0, hd], F32) for hd in heads], axis=1)
            sink_term = jnp.exp(sink - lse_row) * dl_row
            for gi, hd in enumerate(heads):
                span = slice(gi * BLOCK, (gi + 1) * BLOCK)
                dp_ref[:, _q_col(hd):_q_col(hd) + HEAD_DIM] = (dq[span] * SCALE).astype(BF16)
                dsk = dsk + jnp.where(lane8 == hd, -jnp.sum(sink_term[:, span], axis=1, keepdims=True), 0.0)
        dsk_ref[...] += dsk

        for hm in range(N_MEM_HEADS):
            hd = N_SWA_HEADS + hm
            qm, dom = qc_ref[hd], doc_ref[hd]
            mk = mkv_ref[:, HEAD_DIM * hm:HEAD_DIM * (hm + 1)]
            mv = mkv_ref[:, D_MEMQ + HEAD_DIM * hm:D_MEMQ + HEAD_DIM * (hm + 1)]
            pt = jnp.exp(_dot_nt(mk, qm) - lse_t[hd:hd + 1, :])
            dst = (pt * (_dot_nt(mv, dom) - dl_t[hd:hd + 1, :])).astype(BF16)
            dp_ref[:, _q_col(hd):_q_col(hd) + HEAD_DIM] = (_dot_tn(dst, mk) * SCALE).astype(BF16)
            dmkv_ref[:, HEAD_DIM * hm:HEAD_DIM * (hm + 1)] += _dot(dst, qm)
            dmkv_ref[:, D_MEMQ + HEAD_DIM * hm:D_MEMQ + HEAD_DIM * (hm + 1)] += _dot(pt.astype(BF16), dom)

    cur = lambda i: (i, 0)
    const = lambda i: (0, 0)
    rows16 = BLOCK // 16
    last16 = t // 16 - 1
    before = lambda col: (lambda i: (jnp.maximum(i * rows16 - 1, 0), col))
    after = lambda i: (jnp.minimum((i + 1) * rows16, last16), 0)
    heads_cur = lambda i: (0, i, 0)
    heads_next = lambda i: (0, jnp.minimum(i + 1, nb - 1), 0)
    stat_next = lambda i: (jnp.minimum(i + 1, nb - 1), 0)
    key_block = (1, N_SWA_KV, BLOCK, GROUP_ROWS)
    head_block = (N_HEADS, BLOCK, HEAD_DIM)
    return _pcall(
        body, name="mix_core_bwd", grid=(nb,),
        in_specs=[pl.BlockSpec(memory_space=pltpu.SMEM),
                  pl.BlockSpec((BLOCK, D_IN), cur),
                  pl.BlockSpec((BLOCK, 2 * D_KV), lambda i: (jnp.maximum(i - 1, 0), COL_K // (2 * D_KV))),
                  pl.BlockSpec((16, D_CONV), before(COL_CG // D_CONV)),
                  pl.BlockSpec((16, D_CONV), before(COL_U // D_CONV)),
                  pl.BlockSpec((16, D_CONV), after),
                  pl.BlockSpec((BLOCK, D_CONV), cur),
                  pl.BlockSpec((16, D_CONV), after),
                  pl.BlockSpec(head_block, heads_cur), pl.BlockSpec(head_block, heads_next),
                  pl.BlockSpec(head_block, heads_cur), pl.BlockSpec(head_block, heads_next),
                  pl.BlockSpec((BLOCK, 128), cur), pl.BlockSpec((BLOCK, 128), stat_next),
                  pl.BlockSpec((BLOCK, 128), cur), pl.BlockSpec((BLOCK, 128), stat_next),
                  pl.BlockSpec((m, 2 * D_MEMQ), const),
                  pl.BlockSpec((3, D_CONV), const),
                  pl.BlockSpec(key_block, lambda i: (jnp.where(i == 0, BIAS_NONE, BIAS_PREV), 0, 0, 0)),
                  pl.BlockSpec(key_block, lambda i: (BIAS_CUR, 0, 0, 0)),
                  pl.BlockSpec(key_block, lambda i: (jnp.where(i == nb - 1, BIAS_NONE, BIAS_PREV), 0, 0, 0))],
        out_specs=[pl.BlockSpec((BLOCK, D_IN), cur),
                   pl.BlockSpec((m, 2 * D_MEMQ), const),
                   pl.BlockSpec((8, D_CONV), const),
                   pl.BlockSpec((8, 128), const)],
        out_shape=[jax.ShapeDtypeStruct((t, D_IN), BF16),
                   jax.ShapeDtypeStruct((m, 2 * D_MEMQ), F32),
                   jax.ShapeDtypeStruct((8, D_CONV), F32),
                   jax.ShapeDtypeStruct((8, 128), F32)],
        compiler_params=_params(("arbitrary",)),
    )(sinks, p, p, p, p, p, dyconv, dyconv, qh, qh, doh, doh, delta, delta, lse, lse, mkv, convw,
      bias_key, bias_key, bias_key)


def _group_norms(y):
    out = []
    for a, b in MIX_GROUPS:
        ys = y[:, a:b]
        r = _rstd(ys)
        out.append((ys * r, r))
    return out


def _mix_out_fwd(y, h, g, wout):
    t, d = h.shape
    tm = _tok_block(t)

    def body(y_ref, h_ref, g_ref, w_ref, ho_ref, mt_ref):
        yhat = jnp.concatenate([yh for yh, _ in _group_norms(y_ref[...])], axis=-1)
        mixed = yhat * g_ref[...]
        mt_ref[...] = mixed.T.astype(BF16)
        ho_ref[...] = h_ref[...] + _dot(mixed.astype(BF16), w_ref[...])

    return _pcall(
        body, name="mix_out_fwd", grid=(t // tm,),
        in_specs=[pl.BlockSpec((tm, D_MIX), lambda i: (i, 0)),
                  pl.BlockSpec((tm, d), lambda i: (i, 0)),
                  pl.BlockSpec((1, D_MIX), lambda i: (0, 0)),
                  pl.BlockSpec((D_MIX, d), lambda i: (0, 0))],
        out_specs=[pl.BlockSpec((tm, d), lambda i: (i, 0)),
                   pl.BlockSpec((D_MIX, tm), lambda i: (0, i))],
        out_shape=[jax.ShapeDtypeStruct((t, d), F32), jax.ShapeDtypeStruct((D_MIX, t), BF16)],
        compiler_params=_params(("parallel",)),
    )(y, h, g, wout)


def _head_indicator():
    ind = np.zeros((D_MIX, 128), np.float32)
    for hd in range(N_HEADS):
        ind[_head_cols(hd):_head_cols(hd) + HEAD_DIM, hd] = 1.0
    return jnp.asarray(ind, BF16)


def _mix_out_bwd(dho, y, g, wout, mt, dep):
    t, d = dho.shape
    tm = _tok_block(t)
    ni = t // tm

    def body(dho_ref, y_ref, g_ref, w_ref, mt_ref, ind_ref, dep_ref, dyc_ref, doh_ref, dl_ref, dw_ref, dg_ref, acc_ref):
        i = pl.program_id(0)
        dhb = dho_ref[...].astype(BF16)
        dm = _dot_nt(dhb, w_ref[...])
        pw = _dot(mt_ref[...], dhb)
        gg = g_ref[...]
        yy = y_ref[...]
        dys = []
        dgs = []
        for (a, b), (yhat, r) in zip(MIX_GROUPS, _group_norms(yy)):
            dmg = dm[:, a:b]
            dgs.append(_sum8(dmg * yhat))
            dyh = dmg * gg[:, a:b]
            dys.append(r * (dyh - yhat * jnp.mean(dyh * yhat, axis=-1, keepdims=True)))
        dy = jnp.concatenate(dys, axis=-1)
        dyc_ref[...] = dy[:, 0:D_CONV]
        for hd in range(N_HEADS):
            doh_ref[hd] = dy[:, _head_cols(hd):_head_cols(hd) + HEAD_DIM].astype(BF16)
        prod = dy * yy
        hi = prod.astype(BF16)
        lo = (prod - hi.astype(F32)).astype(BF16)
        dl_ref[...] = _dot(hi, ind_ref[...]) + _dot(lo, ind_ref[...])
        part = jnp.concatenate(dgs, axis=-1)

        @pl.when(i == 0)
        def _():
            acc_ref[...] = pw
            dg_ref[...] = part

        @pl.when(i > 0)
        def _():
            acc_ref[...] += pw
            dg_ref[...] += part

        @pl.when(i == ni - 1)
        def _():
            dw_ref[...] = acc_ref[...].astype(BF16)

    return _pcall(
        body, name="mix_out_bwd", grid=(ni,),
        in_specs=[pl.BlockSpec((tm, d), lambda i: (i, 0)),
                  pl.BlockSpec((tm, D_MIX), lambda i: (i, 0)),
                  pl.BlockSpec((1, D_MIX), lambda i: (0, 0)),
                  pl.BlockSpec((D_MIX, d), lambda i: (0, 0)),
                  pl.BlockSpec((D_MIX, tm), lambda i: (0, i)),
                  pl.BlockSpec((D_MIX, 128), lambda i: (0, 0)),
                  pl.BlockSpec(memory_space=pl.ANY)],
        out_specs=[pl.BlockSpec((tm, D_CONV), lambda i: (i, 0)),
                   pl.BlockSpec((N_HEADS, tm, HEAD_DIM), lambda i: (0, i, 0)),
                   pl.BlockSpec((tm, 128), lambda i: (i, 0)),
                   pl.BlockSpec((D_MIX, d), lambda i: (0, 0)),
                   pl.BlockSpec((8, D_MIX), lambda i: (0, 0))],
        out_shape=[jax.ShapeDtypeStruct((t, D_CONV), F32),
                   jax.ShapeDtypeStruct((N_HEADS, t, HEAD_DIM), BF16),
                   jax.ShapeDtypeStruct((t, 128), F32),
                   jax.ShapeDtypeStruct((D_MIX, d), BF16),
                   jax.ShapeDtypeStruct((8, D_MIX), F32)],
        scratch_shapes=[pltpu.VMEM((D_MIX, d), F32)],
        compiler_params=_params(("arbitrary",)),
    )(dho, y, g, wout, mt, _head_indicator(), dep)


def _mix_proj_bwd(dp, dho, h, g, win_t, n):
    t, d = h.shape
    tm = _tok_block(t)
    ni = t // tm

    def body(dp_ref, dho_ref, h_ref, g_ref, w_ref, n_ref, dh_ref, dw_ref, dg_ref, acc_ref):
        i = pl.program_id(0)
        dpb = dp_ref[...]
        dn = _dot(dpb, w_ref[...])

        @pl.when(i == 0)
        def _():
            acc_ref[...] = jnp.zeros_like(acc_ref)

        acc_ref[...] += _dot_tn(dpb, n_ref[...])
        hh = h_ref[...]
        r = _rstd(hh)
        xhat = hh * r
        dxh = dn * g_ref[...]
        dh_ref[...] = dho_ref[...] + r * (dxh - xhat * jnp.mean(dxh * xhat, axis=-1, keepdims=True))
        part = _sum8(dn * xhat)

        @pl.when(i == 0)
        def _():
            dg_ref[...] = part

        @pl.when(i > 0)
        def _():
            dg_ref[...] += part

        @pl.when(i == ni - 1)
        def _():
            dw_ref[...] = acc_ref[...].astype(BF16)

    return _pcall(
        body, name="mix_proj_bwd", grid=(ni,),
        in_specs=[pl.BlockSpec((tm, D_IN), lambda i: (i, 0)),
                  pl.BlockSpec((tm, d), lambda i: (i, 0)),
                  pl.BlockSpec((tm, d), lambda i: (i, 0)),
                  pl.BlockSpec((1, d), lambda i: (0, 0)),
                  pl.BlockSpec((D_IN, d), lambda i: (0, 0)),
                  pl.BlockSpec((tm, d), lambda i: (i, 0))],
        out_specs=[pl.BlockSpec((tm, d), lambda i: (i, 0)),
                   pl.BlockSpec((D_IN, d), lambda i: (0, 0)),
                   pl.BlockSpec((8, d), lambda i: (0, 0))],
        out_shape=[jax.ShapeDtypeStruct((t, d), F32),
                   jax.ShapeDtypeStruct((D_IN, d), BF16),
                   jax.ShapeDtypeStruct((8, d), F32)],
        scratch_shapes=[pltpu.VMEM((D_IN, d), F32)],
        compiler_params=_params(("arbitrary",)),
    )(dp, dho, h, g, win_t, n)


def _final_loss(h, g, tgt):
    t, d = h.shape
    tm = _tok_block(t)

    def body(h_ref, g_ref, t_ref, dh_ref, ls_ref, dg_ref):
        i = pl.program_id(0)
        hh = h_ref[...]
        r = _rstd(hh)
        xhat = hh * r
        gg = g_ref[...]
        err = xhat * gg - t_ref[...]
        dy = err * (1.0 / d)
        dxh = dy * gg
        dh_ref[...] = r * (dxh - xhat * jnp.mean(dxh * xhat, axis=-1, keepdims=True))
        lpart = _sum8(err * err)
        gpart = _sum8(dy * xhat)

        @pl.when(i == 0)
        def _():
            ls_ref[...] = lpart
            dg_ref[...] = gpart

        @pl.when(i > 0)
        def _():
            ls_ref[...] += lpart
            dg_ref[...] += gpart

    return _pcall(
        body, name="final_loss", grid=(t // tm,),
        in_specs=[pl.BlockSpec((tm, d), lambda i: (i, 0)),
                  pl.BlockSpec((1, d), lambda i: (0, 0)),
                  pl.BlockSpec((tm, d), lambda i: (i, 0))],
        out_specs=[pl.BlockSpec((tm, d), lambda i: (i, 0)),
                   pl.BlockSpec((8, d), lambda i: (0, 0)),
                   pl.BlockSpec((8, d), lambda i: (0, 0))],
        out_shape=[jax.ShapeDtypeStruct((t, d), F32),
                   jax.ShapeDtypeStruct((8, d), F32),
                   jax.ShapeDtypeStruct((8, d), F32)],
        compiler_params=_params(("arbitrary",)),
    )(h, g, tgt)


def _position():
    return lax.axis_index("x"), lax.axis_index("y"), lax.axis_index("c")


def _flip(v, bit):
    return 1 - v if bit else v


def _peer(k):
    x, y, c = _position()
    return _flip(x, k & 4), _flip(y, k & 2), _flip(c, k & 1)


def _slot(px, py, pc):
    return 4 * px + 2 * py + pc


def _handshake(peers):
    barrier = pltpu.get_barrier_semaphore()
    for peer in peers:
        pl.semaphore_signal(barrier, inc=1, device_id=peer, device_id_type=MESH)
    pl.semaphore_wait(barrier, len(peers))


def _sequencer_call(body, name, collective_id, out_type, scratch_types, operands):
    return pl.kernel(
        body, out_type=out_type, mesh=plsc.ScalarSubcoreMesh(axis_name="sequencer", num_cores=1), name=name,
        scratch_types=scratch_types, compiler_params=pltpu.CompilerParams(collective_id=collective_id),
    )(*operands)


def _all_gather(shards, name, collective_id):
    nt = len(shards)

    def body(*refs):
        xs = refs[:nt]
        outs = refs[nt:2 * nt]
        send_sems, recv_sems, local_sems = refs[2 * nt:]
        x, y, c = _position()
        me, sibling = (x, y, c), (x, y, 1 - c)
        xn, yn, dg = (1 - x, y), (x, 1 - y), (1 - x, 1 - y)
        pick = lambda a, b: (jnp.where(c == 0, a[0], b[0]), jnp.where(c == 0, a[1], b[1]))
        relay_from, relay_to = pick(yn, xn), pick(xn, yn)
        _handshake([sibling, (*xn, c), (*yn, c)])

        def copy(t, k, block, to, src=None):
            dst = outs[t].at[_slot(*block)]
            return pltpu.make_async_remote_copy(
                src_ref=dst if src is None else src, dst_ref=dst,
                send_sem=send_sems.at[t, k], recv_sem=recv_sems.at[t, k],
                device_id=to, device_id_type=MESH)

        mine = [pltpu.make_async_copy(xs[t], outs[t].at[_slot(*me)], local_sems.at[t]) for t in range(nt)]
        for cp in mine:
            cp.start()
        sent = []
        for t in range(nt):
            sent += [copy(t, 0, me, sibling, src=xs[t]), copy(t, 1, me, (*xn, c), src=xs[t]),
                     copy(t, 2, me, (*yn, c), src=xs[t])]
        for cp in sent:
            cp.start()
        for t in range(nt):
            copy(t, 1, (*xn, c), me).wait_recv()
            copy(t, 2, (*yn, c), me).wait_recv()
            passed = [copy(t, 3, (*relay_from, c), (*relay_to, c)),
                      copy(t, 4, (*xn, c), sibling), copy(t, 5, (*yn, c), sibling)]
            for cp in passed:
                cp.start()
            sent += passed
        for t in range(nt):
            copy(t, 3, (*dg, c), me).wait_recv()
            fwd = copy(t, 6, (*dg, c), sibling)
            fwd.start()
            sent.append(fwd)
        for t in range(nt):
            copy(t, 0, sibling, me).wait_recv()
            for k, chip in ((4, xn), (5, yn), (6, dg)):
                copy(t, k, (*chip, 1 - c), me).wait_recv()
        for cp in sent:
            cp.wait_send()
        for cp in mine:
            cp.wait()

    return _sequencer_call(
        body, name, collective_id,
        out_type=[jax.ShapeDtypeStruct((N_DEV,) + s.shape, s.dtype) for s in shards],
        scratch_types=[pltpu.SemaphoreType.DMA((nt, 7)), pltpu.SemaphoreType.DMA((nt, 7)),
                       pltpu.SemaphoreType.DMA((nt,))],
        operands=shards)


def _scatter_copy(srcs, lands, send_sems, recv_sems, t, k):
    peer = _peer(k)
    return pltpu.make_async_remote_copy(
        src_ref=srcs[t].at[_slot(*peer)], dst_ref=lands[t].at[k],
        send_sem=send_sems.at[t * (N_DEV - 1) + k - 1], recv_sem=recv_sems.at[t * (N_DEV - 1) + k - 1],
        device_id=peer, device_id_type=MESH)


def _scatter_start(partials, name):
    nt = len(partials)

    def body(*refs):
        srcs, lands = refs[:nt], refs[nt:2 * nt]
        send_sems, recv_sems = refs[2 * nt], refs[2 * nt + 1]
        token = refs[-1]
        for k in range(1, N_DEV):
            for t in range(nt):
                _scatter_copy(srcs, lands, send_sems, recv_sems, t, k).start()
        token[...] = jnp.zeros_like(token)

    hbm = pl.BlockSpec(memory_space=pltpu.HBM)
    sem = pl.BlockSpec(memory_space=pltpu.SEMAPHORE)
    shapes = [pltpu.HBM(p.shape, p.dtype) for p in partials]
    lands = [pltpu.with_memory_space_constraint(lax.empty(p.shape, p.dtype), pltpu.HBM) for p in partials]
    srcs = [pltpu.with_memory_space_constraint(p, pltpu.HBM) for p in partials]
    out = _pcall(
        body, name=name,
        out_shape=[pltpu.SemaphoreType.DMA((nt * (N_DEV - 1),))] * 2 + shapes + shapes
        + [jax.ShapeDtypeStruct((8, 128), F32)],
        in_specs=[hbm] * (2 * nt),
        out_specs=[sem, sem] + [hbm] * (2 * nt) + [pl.BlockSpec(memory_space=pltpu.VMEM)],
        input_output_aliases={i: 2 + i for i in range(2 * nt)},
        compiler_params=pltpu.CompilerParams(has_side_effects=pltpu.SideEffectType.DATAFLOW_SIDE_EFFECTING),
    )(*srcs, *lands)
    return (nt, name, out[:-1]), out[-1]


def _scatter_wait(state, after):
    nt, name, (send_sems, recv_sems, *thru) = state

    def body(*refs):
        srcs, lands = refs[:nt], refs[nt:2 * nt]
        send_sems, recv_sems = refs[2 * nt], refs[2 * nt + 1]
        for k in range(1, N_DEV):
            for t in range(nt):
                copy = _scatter_copy(srcs, lands, send_sems, recv_sems, t, k)
                copy.wait_send()
                copy.wait_recv()

    hbm = pl.BlockSpec(memory_space=pltpu.HBM)
    sem = pl.BlockSpec(memory_space=pltpu.SEMAPHORE)
    out = _pcall(
        body, name=name + "_wait",
        out_shape=[pltpu.HBM(a.shape, a.dtype) for a in thru],
        in_specs=[hbm] * (2 * nt) + [sem, sem, pl.BlockSpec(memory_space=pl.ANY)],
        out_specs=[hbm] * (2 * nt),
        input_output_aliases={i: i for i in range(2 * nt)},
        compiler_params=pltpu.CompilerParams(has_side_effects=pltpu.SideEffectType.DATAFLOW_SIDE_EFFECTING),
    )(*thru, send_sems, recv_sems, after)
    return out[:nt], out[nt:]


def _all_reduce_rows(v, dep):
    nv, _, w = v.shape

    def body(v_ref, dep_ref, out_ref, mine_ref, gath_ref, send_sems, recv_sems):
        x, y, c = _position()
        me = _slot(x, y, c)
        mine_ref[...] = jnp.sum(v_ref[...], axis=1)

        def copy(k):
            return pltpu.make_async_remote_copy(
                src_ref=mine_ref, dst_ref=gath_ref.at[me],
                send_sem=send_sems.at[k - 1], recv_sem=recv_sems.at[k - 1],
                device_id=_peer(k), device_id_type=MESH)

        def arrival(k):
            return pltpu.make_async_remote_copy(
                src_ref=mine_ref, dst_ref=gath_ref.at[_slot(*_peer(k))],
                send_sem=send_sems.at[k - 1], recv_sem=recv_sems.at[k - 1],
                device_id=_peer(k), device_id_type=MESH)

        sent = [copy(k) for k in range(1, N_DEV)]
        for cp in sent:
            cp.start()
        gath_ref[me] = mine_ref[...]
        for k in range(1, N_DEV):
            arrival(k).wait_recv()
        for cp in sent:
            cp.wait_send()
        total = gath_ref[0]
        for s in range(1, N_DEV):
            total = total + gath_ref[s]
        out_ref[...] = total

    vmem = pl.BlockSpec(memory_space=pltpu.VMEM)
    return _pcall(
        body, name="all_reduce_rows",
        in_specs=[vmem, pl.BlockSpec(memory_space=pl.ANY)], out_specs=vmem,
        out_shape=jax.ShapeDtypeStruct((nv, w), F32),
        scratch_shapes=[pltpu.VMEM((nv, w), F32), pltpu.VMEM((N_DEV, nv, w), F32),
                        pltpu.SemaphoreType.DMA((7,)), pltpu.SemaphoreType.DMA((7,))],
    )(v, dep)


def _adamw_math(w, g, m, v):
    m2 = ADAM_B1 * m + (1.0 - ADAM_B1) * g
    v2 = ADAM_B2 * v + (1.0 - ADAM_B2) * (g * g)
    m_hat = m2 / (1.0 - ADAM_B1 ** ADAM_STEP)
    v_hat = v2 / (1.0 - ADAM_B2 ** ADAM_STEP)
    delta = -ADAM_LR * (m_hat / (jnp.sqrt(v_hat) + ADAM_EPS) + ADAM_WD * w)
    return delta, m2, v2


def _row_block(r):
    for cand in (256, 176, 128):
        if r % cand == 0:
            return cand
    return r


def _adamw_sharded(me, grads, w, m, v, dep, first_layer=0, prev=None):
    nl = len(grads)
    _, r, c = grads[0][1].shape
    tr = _row_block(r)
    nr = r // tr
    prev = list(prev or ())

    def body(me_ref, *refs):
        grad_refs = refs[:2 * nl]
        w_ref, m_ref, v_ref = refs[2 * nl:2 * nl + 3]
        g_ref, d_ref, m2_ref, v2_ref = refs[-4:]
        layer = pl.program_id(0)

        def total(own_ref, land_ref):
            acc = own_ref[0].astype(F32)
            for k in range(1, N_DEV):
                acc = acc + land_ref[k].astype(F32)
            return acc

        g = total(grad_refs[0], grad_refs[1])
        for k in range(1, nl):
            g = jnp.where(layer == k, total(grad_refs[2 * k], grad_refs[2 * k + 1]), g)
        delta, m2, v2 = _adamw_math(w_ref[0], g, m_ref[0], v_ref[0])
        g_ref[0] = g
        d_ref[0] = delta
        m2_ref[0] = m2
        v2_ref[0] = v2

    def grad_pair_specs(k):
        def rows(l, i):
            return jnp.where(l == k, i, jnp.where(l < k, 0, nr - 1))
        return [pl.BlockSpec((1, tr, c), lambda l, i, me_ref: (me_ref[0], rows(l, i), 0)),
                pl.BlockSpec((N_DEV, tr, c), lambda l, i, me_ref: (0, rows(l, i), 0))]

    grad_specs = [spec for k in range(nl) for spec in grad_pair_specs(k)]
    shard = pl.BlockSpec((1, tr, c), lambda l, i, me_ref: (first_layer + l, i, 0))
    untouched = pl.BlockSpec(memory_space=pl.ANY)
    out = jax.ShapeDtypeStruct(w.shape, F32)
    first_prev = 1 + 2 * nl + 4
    return _pcall(
        body, name="adamw_sharded",
        grid_spec=pltpu.PrefetchScalarGridSpec(
            num_scalar_prefetch=1, grid=(nl, nr),
            in_specs=grad_specs + [shard, shard, shard] + [untouched] * (1 + len(prev)),
            out_specs=[shard, shard, shard, shard]),
        out_shape=[out, out, out, out],
        input_output_aliases={first_prev + k: k for k in range(len(prev))},
        compiler_params=_params(("arbitrary", "arbitrary")),
    )(me, *[a for pair in grads for a in pair], w, m, v, dep, *prev)


def _adamw_small(w, g, m, v):
    def body(w_ref, g_ref, m_ref, v_ref, d_ref, m2_ref, v2_ref):
        delta, m2, v2 = _adamw_math(w_ref[...], g_ref[...], m_ref[...], v_ref[...])
        d_ref[...] = delta
        m2_ref[...] = m2
        v2_ref[...] = v2

    spec = pl.BlockSpec(w.shape, lambda i: (0, 0))
    out = jax.ShapeDtypeStruct(w.shape, F32)
    return _pcall(
        body, name="adamw_small", grid=(1,),
        in_specs=[spec] * 4, out_specs=[spec] * 3, out_shape=[out] * 3,
        compiler_params=_params(("arbitrary",)),
    )(w, g, m, v)


def _pack(arrs):
    flat = jnp.concatenate([a.reshape(-1) for a in arrs])
    n = flat.shape[0]
    rows = -(-n // 1024) * 8
    return jnp.pad(flat, (0, rows * 128 - n)).reshape(rows, 128)


def _unpack(packed, like):
    flat = packed.reshape(-1)
    out, off = [], 0
    for a in like:
        out.append(flat[off:off + a.size].reshape(a.shape))
        off += a.size
    return out


def kernel(x, mem, g_ffn1, w_ffn1_up, w_ffn1_down, g_mix, w_in, conv_w, sinks, g_mem, w_mem_kv, g_grp, w_out, g_ffn2, w_ffn2_up, w_ffn2_down, g_final, loss_target, m_g_ffn1, m_w_ffn1_up, m_w_ffn1_down, m_g_mix, m_w_in, m_conv_w, m_sinks, m_g_mem, m_w_mem_kv, m_g_grp, m_w_out, m_g_ffn2, m_w_ffn2_up, m_w_ffn2_down, m_g_final, v_g_ffn1, v_w_ffn1_up, v_w_ffn1_down, v_g_mix, v_w_in, v_conv_w, v_sinks, v_g_mem, v_w_mem_kv, v_g_grp, v_w_out, v_g_ffn2, v_w_ffn2_up, v_w_ffn2_down, v_g_final):
    depth = g_ffn1.shape[0]
    t, d = x.shape[1], x.shape[2]
    width = max(d, D_MIX)
    me = _slot(*_position())
    conv_shard = conv_w.shape[2]

    xin, memin, tgt = x[0], mem[0], loss_target[0]

    conv_tile = jnp.zeros((depth * 8, 128), F32).at[:, :conv_shard].set(
        jnp.pad(conv_w, ((0, 0), (0, 8 - conv_w.shape[1]), (0, 0))).reshape(depth * 8, conv_shard))
    tr = lambda a: jnp.swapaxes(a, -1, -2)
    bf = lambda a: a.astype(BF16)
    weights = []
    collective_id = 0
    for l in range(depth):
        groups = [[bf(tr(w_ffn1_up[l])), bf(w_ffn1_down[l])] + ([conv_tile] if l == 0 else []),
                  [bf(tr(w_in[l])), bf(w_mem_kv[l]), bf(w_out[l])],
                  [bf(tr(w_ffn2_up[l])), bf(w_ffn2_down[l])]]
        full = []
        for gi, shards in enumerate(groups):
            full.append(_all_gather(shards, f"all_gather_l{l}_g{gi}", collective_id))
            collective_id += 1
        if l == 0:
            conv_full = full[0][2].reshape(N_DEV, depth, 8, 128)[:, :, :3, :conv_shard]
            conv_full = conv_full.transpose(1, 2, 0, 3).reshape(depth, 3, N_DEV * conv_shard)
        weights.append(dict(
            up1=full[0][0].reshape(2, -1, d), dn1=full[0][1].reshape(-1, d),
            win=full[1][0].reshape(D_IN, d), wkv=full[1][1].reshape(d, 2 * D_MEMQ), wout=full[1][2].reshape(D_MIX, d),
            up2=full[2][0].reshape(2, -1, d), dn2=full[2][1].reshape(-1, d)))

    row = lambda a: a.reshape(1, -1)
    bias_key = _bias_table()

    h = xin
    saved = []
    for l in range(depth):
        wl = weights[l]
        s = dict(h0=h)
        h, s["gu1"], s["n1"] = _ffn_fwd(h, row(g_ffn1[l]), wl["up1"], wl["dn1"])
        s["h1"] = h
        s["p"], s["n_mix"], s["qh"] = _mix_proj_fwd(h, row(g_mix[l]), wl["win"])
        s["mkv"], s["nt_mem"] = _memkv_fwd(memin, row(g_mem[l]), wl["wkv"], s["p"])
        s["y"], s["lse"] = _mix_core_fwd(s["p"], s["qh"], s["mkv"], conv_full[l], row(sinks[l]), bias_key)
        h, s["mt"] = _mix_out_fwd(s["y"], h, row(g_grp[l]), wl["wout"])
        s["h2"] = h
        h, s["gu2"], s["n2"] = _ffn_fwd(h, row(g_ffn2[l]), wl["up2"], wl["dn2"])
        saved.append(s)

    dh, loss_part, dg_final = _final_loss(h, row(g_final), tgt)

    small = {}
    dep = loss_part

    def reduce_small(after):
        def lanes(a):
            return jnp.pad(a, ((0, 0), (0, width - a.shape[1])))

        def first_row(a):
            return lanes(jnp.pad(a, ((0, 8 - a.shape[0]), (0, 0))))

        vec_names = ["g_ffn1", "g_mix", "g_mem", "g_grp", "g_ffn2", "sinks"]
        tiles = [lanes(small[n, l]) for n in vec_names for l in range(depth)]
        tiles += [first_row(small["conv_w", l][k:k + 1]) for l in range(depth) for k in range(3)]
        tiles.append(lanes(dg_final))
        n_real = len(tiles)
        tiles.append(lanes(loss_part))
        tiles += [jnp.zeros((8, width), F32)] * (-len(tiles) % 8)
        summed = _all_reduce_rows(jnp.stack(tiles), after)
        loss_all = 0.5 * jnp.sum(summed[n_real]) / d

        def vec(n, wd):
            return jnp.stack([summed[vec_names.index(n) * depth + l, :wd] for l in range(depth)])

        conv_base = len(vec_names) * depth
        conv_grad = jnp.stack([jnp.stack([summed[conv_base + 3 * l + k, :D_CONV] for k in range(3)])
                               for l in range(depth)])
        grads_small = {
            "g_ffn1": vec("g_ffn1", d), "g_mix": vec("g_mix", d), "g_mem": vec("g_mem", d),
            "g_grp": vec("g_grp", D_MIX), "g_ffn2": vec("g_ffn2", d), "sinks": vec("sinks", N_SWA_HEADS),
            "conv_w": lax.dynamic_slice_in_dim(conv_grad, me * conv_shard, conv_shard, axis=2),
            "g_final": summed[n_real - 1, :d],
        }
        small_w = [("g_ffn1", g_ffn1, m_g_ffn1, v_g_ffn1), ("g_mix", g_mix, m_g_mix, v_g_mix),
                   ("conv_w", conv_w, m_conv_w, v_conv_w), ("sinks", sinks, m_sinks, v_sinks),
                   ("g_mem", g_mem, m_g_mem, v_g_mem), ("g_grp", g_grp, m_g_grp, v_g_grp),
                   ("g_ffn2", g_ffn2, m_g_ffn2, v_g_ffn2), ("g_final", g_final, m_g_final, v_g_final)]
        like = [w for _, w, _, _ in small_w]
        packed = _adamw_small(_pack(like), _pack([grads_small[n] for n, _, _, _ in small_w]),
                              _pack([m for _, _, m, _ in small_w]), _pack([v for _, _, _, v in small_w]))
        updated = {n: (grads_small[n], dl, m2, v2)
                   for (n, _, _, _), dl, m2, v2 in zip(small_w, *[_unpack(pk, like) for pk in packed])}
        return loss_all, updated, packed[0]

    started = []

    def scatter(names, partials, label):
        state, token = _scatter_start(partials, f"scatter_grads_{label}")
        started.append((names, state))
        return token

    for l in reversed(range(depth)):
        wl, s = weights[l], saved[l]
        dh, agu, dyb, small["g_ffn2", l] = _ffn_bwd_act(dh, s["h2"], row(g_ffn2[l]), s["gu2"], wl["up2"], wl["dn2"], dep)
        ddn2 = _ffn_bwd_w(agu, 2, 1, dyb, agu, f"ffn_bwd_w_down_l{l}_ffn2").reshape(N_DEV, -1, d)
        dup2 = _ffn_bwd_w(agu, 0, 2, s["n2"], ddn2, f"ffn_bwd_w_up_l{l}_ffn2").reshape(N_DEV, -1, d)
        dep = scatter([("w_ffn2_up", l), ("w_ffn2_down", l)], [dup2, ddn2], f"l{l}_ffn2")
        dyconv, doh, delta, dwout, small["g_grp", l] = _mix_out_bwd(dh, s["y"], row(g_grp[l]), wl["wout"], s["mt"], dep)
        dp, dmkv, small["conv_w", l], small["sinks", l] = _mix_core_bwd(
            s["p"], s["qh"], dyconv, doh, delta, s["lse"], s["mkv"], conv_full[l], row(sinks[l]), bias_key)
        dwkv, small["g_mem", l] = _memkv_bwd(dmkv, memin, row(g_mem[l]), wl["wkv"], s["nt_mem"])
        dh, dwin, small["g_mix", l] = _mix_proj_bwd(dp, dh, s["h1"], row(g_mix[l]), wl["win"], s["n_mix"])
        dep = scatter([("w_in", l), ("w_mem_kv", l), ("w_out", l)],
                      [dwin.reshape(N_DEV, -1, d), dwkv.reshape(N_DEV, -1, 2 * D_MEMQ), dwout.reshape(N_DEV, -1, d)],
                      f"l{l}_mix")
        dh, agu, dyb, small["g_ffn1", l] = _ffn_bwd_act(dh, s["h0"], row(g_ffn1[l]), s["gu1"], wl["up1"], wl["dn1"], dep)
        ddn1 = _ffn_bwd_w(agu, 2, 1, dyb, agu, f"ffn_bwd_w_down_l{l}_ffn1").reshape(N_DEV, -1, d)
        if l > 0:
            dup1 = _ffn_bwd_w(agu, 0, 2, s["n1"], ddn1, f"ffn_bwd_w_up_l{l}_ffn1").reshape(N_DEV, -1, d)
            dep = scatter([("w_ffn1_up", l), ("w_ffn1_down", l)], [dup1, ddn1], f"l{l}_ffn1")
        else:
            dep = scatter([("w_ffn1_down", l)], [ddn1], f"l{l}_ffn1_down")
            dup1 = _ffn_bwd_w(agu, 0, 2, s["n1"], dep, f"ffn_bwd_w_up_l{l}_ffn1").reshape(N_DEV, -1, d)
            dep = scatter([("w_ffn1_up", l)], [dup1], f"l{l}_ffn1_up")
    grad_x = dh[None]

    big = {"w_ffn2_up": (w_ffn2_up, m_w_ffn2_up, v_w_ffn2_up, True), "w_ffn2_down": (w_ffn2_down, m_w_ffn2_down, v_w_ffn2_down, False),
           "w_in": (w_in, m_w_in, v_w_in, True), "w_mem_kv": (w_mem_kv, m_w_mem_kv, v_w_mem_kv, False),
           "w_out": (w_out, m_w_out, v_w_out, False), "w_ffn1_up": (w_ffn1_up, m_w_ffn1_up, v_w_ffn1_up, True),
           "w_ffn1_down": (w_ffn1_down, m_w_ffn1_down, v_w_ffn1_down, False)}
    me_index = jnp.reshape(me, (1,)).astype(jnp.int32)
    sharded, landed, begun = {}, {}, {}
    by_layer = {name for name, _ in started[-1][0]}

    def finish(groups, after):
        for names, state in groups:
            owns, lands = _scatter_wait(state, after)
            for key, own, land in zip(names, owns, lands):
                landed[key] = (own, land)
            after = lands[0]
            for name, l in names:
                w, m, v, transposed = big[name]
                fix = tr if transposed else (lambda a: a)
                if name in by_layer:
                    res = _adamw_sharded(me_index, [landed[name, l]], fix(w), fix(m), fix(v), after, l, begun.get(name))
                    done = name in begun
                    begun[name] = res
                elif all((name, k) in landed for k in range(depth)):
                    res = _adamw_sharded(me_index, [landed[name, k] for k in range(depth)], fix(w), fix(m), fix(v), after)
                    done = True
                else:
                    continue
                if done:
                    sharded[name] = tuple(fix(r) for r in res)
                after = res[0]
        return after

    loss, small_out, dep = reduce_small(finish(started[:-1], dep))
    finish(started[-1:], dep)

    order = ["g_ffn1", "w_ffn1_up", "w_ffn1_down", "g_mix", "w_in", "conv_w", "sinks", "g_mem", "w_mem_kv", "g_grp",
             "w_out", "g_ffn2", "w_ffn2_up", "w_ffn2_down", "g_final"]
    results = {**sharded, **small_out}
    outs = [loss, grad_x]
    for part in range(4):
        outs += [results[n][part] for n in order]
    return tuple(outs)
```

```python
import numpy as np
import jax
import jax.numpy as jnp
from jax import lax
from jax.experimental import pallas as pl
from jax.experimental.pallas import tpu as pltpu
from jax.experimental.pallas import tpu_sc as plsc

F32 = jnp.float32
BF16 = jnp.bfloat16

N_DEV = 8
EPS = 1e-6
N_SWA_HEADS = 8
N_SWA_KV = 2
SWA_GROUP = N_SWA_HEADS // N_SWA_KV
HEAD_DIM = 64
N_MEM_HEADS = 4
D_CONV = 256
BLOCK = 128
D_SWA = N_SWA_HEADS * HEAD_DIM
D_KV = N_SWA_KV * HEAD_DIM
D_MEMQ = N_MEM_HEADS * HEAD_DIM
D_MIX = D_CONV + D_SWA + D_MEMQ
D_IN = 3 * D_CONV + D_SWA + 2 * D_KV + D_MEMQ
COL_BG, COL_CG, COL_U = 0, D_CONV, 2 * D_CONV
COL_Q = 3 * D_CONV
COL_K = COL_Q + D_SWA
COL_V = COL_K + D_KV
COL_QM = COL_V + D_KV
MIX_GROUPS = ((0, D_CONV), (D_CONV, D_CONV + D_SWA), (D_CONV + D_SWA, D_MIX))
SLOPES = tuple(2.0 ** (-8.0 * (i + 1) / N_SWA_HEADS) for i in range(N_SWA_HEADS))
SCALE = HEAD_DIM ** -0.5
NEG = -1e30

ADAM_LR = 0.001
ADAM_B1 = 0.9
ADAM_B2 = 0.999
ADAM_EPS = 1e-08
ADAM_WD = 0.01
ADAM_STEP = 10

V7X_VMEM_BYTES = 64 * 1024 * 1024
VMEM_LIMIT = (V7X_VMEM_BYTES * 3) // 4
VMEM_LIMIT_WIDE = (V7X_VMEM_BYTES * 15) // 16
MESH = pl.DeviceIdType.MESH


def _pcall(body, **kw):
    return pl.pallas_call(body, **kw)


def _params(sem=None, vmem=VMEM_LIMIT):
    return pltpu.CompilerParams(dimension_semantics=sem, vmem_limit_bytes=vmem)


def _dot(a, b):
    return lax.dot_general(a, b, (((1,), (0,)), ((), ())), preferred_element_type=F32)


def _dot_nt(a, b):
    return lax.dot_general(a, b, (((1,), (1,)), ((), ())), preferred_element_type=F32)


def _dot_tn(a, b):
    return lax.dot_general(a, b, (((0,), (0,)), ((), ())), preferred_element_type=F32)


def _rstd(x):
    return lax.rsqrt(jnp.mean(x * x, axis=-1, keepdims=True) + EPS)


def _sigmoid(x):
    return 1.0 / (1.0 + jnp.exp(-x))


def _sum8(x):
    r, w = x.shape
    return jnp.sum(x.reshape(r // 8, 8, w), axis=0)


def _tok_block(t, rows=512):
    return min(rows, t)


def _feat_block(f, parts=N_DEV // 2):
    return f // parts


def _ffn_fwd(h, g, wup_t, wdn):
    t, d = h.shape
    f = wdn.shape[0]
    tm, tf = _tok_block(t), _feat_block(f, 2)
    ni, nj = t // tm, f // tf

    def body(h_ref, g_ref, wup_ref, wdn_ref, ho_ref, gu_ref, n_ref, nt_ref, acc_ref):
        j = pl.program_id(1)

        @pl.when(j == 0)
        def _():
            hh = h_ref[...]
            n = hh * _rstd(hh) * g_ref[...]
            n_ref[...] = n.astype(BF16)
            nt_ref[...] = n.T.astype(BF16)
            acc_ref[...] = jnp.zeros_like(acc_ref)

        nt = nt_ref[...]
        gate = _dot(wup_ref[0], nt)
        up = _dot(wup_ref[1], nt)
        gu_ref[0] = gate.astype(BF16)
        gu_ref[1] = up.astype(BF16)
        a = gate * _sigmoid(gate) * up
        acc_ref[...] += _dot_tn(a.astype(BF16), wdn_ref[...])

        @pl.when(j == nj - 1)
        def _():
            ho_ref[...] = h_ref[...] + 0.5 * acc_ref[...]

    return _pcall(
        body, name="ffn_fwd", grid=(ni, nj),
        in_specs=[pl.BlockSpec((tm, d), lambda i, j: (i, 0)),
                  pl.BlockSpec((1, d), lambda i, j: (0, 0)),
                  pl.BlockSpec((2, tf, d), lambda i, j: (0, j, 0)),
                  pl.BlockSpec((tf, d), lambda i, j: (j, 0))],
        out_specs=[pl.BlockSpec((tm, d), lambda i, j: (i, 0)),
                   pl.BlockSpec((2, tf, tm), lambda i, j: (0, j, i)),
                   pl.BlockSpec((tm, d), lambda i, j: (i, 0))],
        out_shape=[jax.ShapeDtypeStruct((t, d), F32),
                   jax.ShapeDtypeStruct((2, f, t), BF16),
                   jax.ShapeDtypeStruct((t, d), BF16)],
        scratch_shapes=[pltpu.VMEM((d, tm), BF16), pltpu.VMEM((tm, d), F32)],
        compiler_params=_params(("parallel", "arbitrary")),
    )(h, g, wup_t, wdn)


def _ffn_bwd_act(dho, h, g, gu, wup_t, wdn, dep):
    t, d = h.shape
    f = wdn.shape[0]
    tm, tf = _tok_block(t), _feat_block(f, 2)
    ni, nj = t // tm, f // tf

    def body(dho_ref, h_ref, g_ref, gu_ref, wup_ref, wdn_ref, dep_ref, dh_ref, agu_ref, dyb_ref, dg_ref, dyt_ref, acc_ref):
        i = pl.program_id(0)
        j = pl.program_id(1)

        @pl.when(j == 0)
        def _():
            dy0 = 0.5 * dho_ref[...]
            dyb_ref[...] = dy0.astype(BF16)
            dyt_ref[...] = dy0.T.astype(BF16)
            acc_ref[...] = jnp.zeros_like(acc_ref)

        da = _dot(wdn_ref[...], dyt_ref[...]).astype(BF16)
        gate = gu_ref[0]
        up = gu_ref[1]
        sg = _sigmoid(gate)
        silu = gate * sg
        dgate = da * up * (sg * (1.0 + gate * (1.0 - sg)))
        dup = da * silu
        agu_ref[0] = dgate
        agu_ref[1] = dup
        agu_ref[2] = silu * up
        acc_ref[...] += _dot_tn(dgate, wup_ref[0])
        acc_ref[...] += _dot_tn(dup, wup_ref[1])

        @pl.when(j == nj - 1)
        def _():
            hh = h_ref[...]
            r = _rstd(hh)
            xhat = hh * r
            dnf = acc_ref[...]
            dxh = dnf * g_ref[...]
            dh_ref[...] = dho_ref[...] + r * (dxh - xhat * jnp.mean(dxh * xhat, axis=-1, keepdims=True))
            part = _sum8(dnf * xhat)

            @pl.when(i == 0)
            def _():
                dg_ref[...] = part

            @pl.when(i > 0)
            def _():
                dg_ref[...] += part

    return _pcall(
        body, name="ffn_bwd_act", grid=(ni, nj),
        in_specs=[pl.BlockSpec((tm, d), lambda i, j: (i, 0)),
                  pl.BlockSpec((tm, d), lambda i, j: (i, 0)),
                  pl.BlockSpec((1, d), lambda i, j: (0, 0)),
                  pl.BlockSpec((2, tf, tm), lambda i, j: (0, j, i)),
                  pl.BlockSpec((2, tf, d), lambda i, j: (0, j, 0)),
                  pl.BlockSpec((tf, d), lambda i, j: (j, 0)),
                  pl.BlockSpec(memory_space=pl.ANY)],
        out_specs=[pl.BlockSpec((tm, d), lambda i, j: (i, 0)),
                   pl.BlockSpec((3, tf, tm), lambda i, j: (0, j, i)),
                   pl.BlockSpec((tm, d), lambda i, j: (i, 0)),
                   pl.BlockSpec((8, d), lambda i, j: (0, 0))],
        out_shape=[jax.ShapeDtypeStruct((t, d), F32),
                   jax.ShapeDtypeStruct((3, f, t), BF16),
                   jax.ShapeDtypeStruct((t, d), BF16),
                   jax.ShapeDtypeStruct((8, d), F32)],
        scratch_shapes=[pltpu.VMEM((d, tm), BF16), pltpu.VMEM((tm, d), F32)],
        compiler_params=_params(("arbitrary", "arbitrary"), VMEM_LIMIT_WIDE),
    )(dho, h, g, gu, wup_t, wdn, dep)


def _ffn_bwd_w(agu, first, count, rhs, dep, name):
    _, f, t = agu.shape
    d = rhs.shape[1]
    tm = _tok_block(t, 2048)
    tf = _feat_block(f, 2) if count == 1 else _feat_block(f)
    ni, nj = t // tm, f // tf

    def body(lhs_ref, rhs_ref, dep_ref, dw_ref, acc_ref):
        i = pl.program_id(1)
        @pl.when(i == 0)
        def _():
            acc_ref[...] = jnp.zeros_like(acc_ref)

        rb = rhs_ref[...]
        for k in range(count):
            acc_ref[k] += _dot(lhs_ref[k], rb)

        @pl.when(i == ni - 1)
        def _():
            dw_ref[...] = acc_ref[...].astype(BF16)

    return _pcall(
        body, name=name, grid=(nj, ni),
        in_specs=[pl.BlockSpec((count, tf, tm), lambda j, i: (first // count, j, i)),
                  pl.BlockSpec((tm, d), lambda j, i: (i, 0)),
                  pl.BlockSpec(memory_space=pl.ANY)],
        out_specs=pl.BlockSpec((count, tf, d), lambda j, i: (0, j, 0)),
        out_shape=jax.ShapeDtypeStruct((count, f, d), BF16),
        scratch_shapes=[pltpu.VMEM((count, tf, d), F32)],
        compiler_params=_params(("parallel", "arbitrary")),
    )(agu, rhs, dep)


N_HEADS = N_SWA_HEADS + N_MEM_HEADS


def _q_col(hd):
    return COL_Q + HEAD_DIM * hd if hd < N_SWA_HEADS else COL_QM + HEAD_DIM * (hd - N_SWA_HEADS)


def _mix_proj_fwd(h, g, win_t):
    t, d = h.shape
    tm = _tok_block(t)

    def body(h_ref, g_ref, win_ref, p_ref, n_ref, qh_ref):
        hh = h_ref[...]
        n = (hh * _rstd(hh) * g_ref[...]).astype(BF16)
        n_ref[...] = n
        proj = _dot_nt(n, win_ref[...])
        p_ref[...] = proj.astype(BF16)
        for hd in range(N_HEADS):
            c0 = _q_col(hd)
            qh_ref[hd] = (proj[:, c0:c0 + HEAD_DIM] * SCALE).astype(BF16)

    return _pcall(
        body, name="mix_proj_fwd", grid=(t // tm,),
        in_specs=[pl.BlockSpec((tm, d), lambda i: (i, 0)),
                  pl.BlockSpec((1, d), lambda i: (0, 0)),
                  pl.BlockSpec((D_IN, d), lambda i: (0, 0))],
        out_specs=[pl.BlockSpec((tm, D_IN), lambda i: (i, 0)),
                   pl.BlockSpec((tm, d), lambda i: (i, 0)),
                   pl.BlockSpec((N_HEADS, tm, HEAD_DIM), lambda i: (0, i, 0))],
        out_shape=[jax.ShapeDtypeStruct((t, D_IN), BF16), jax.ShapeDtypeStruct((t, d), BF16),
                   jax.ShapeDtypeStruct((N_HEADS, t, HEAD_DIM), BF16)],
        compiler_params=_params(("parallel",)),
    )(h, g, win_t)


def _memkv_fwd(mem, g, wkv, dep):
    m, d = mem.shape

    def body(mem_ref, g_ref, w_ref, dep_ref, mkv_ref, nt_ref):
        mm = mem_ref[...]
        n = mm * _rstd(mm) * g_ref[...]
        nt_ref[...] = n.T.astype(BF16)
        mkv_ref[...] = _dot(n.astype(BF16), w_ref[...]).astype(BF16)

    return _pcall(
        body, name="memkv_fwd", grid=(1,),
        in_specs=[pl.BlockSpec((m, d), lambda i: (0, 0)),
                  pl.BlockSpec((1, d), lambda i: (0, 0)),
                  pl.BlockSpec((d, 2 * D_MEMQ), lambda i: (0, 0)),
                  pl.BlockSpec(memory_space=pl.ANY)],
        out_specs=[pl.BlockSpec((m, 2 * D_MEMQ), lambda i: (0, 0)),
                   pl.BlockSpec((d, m), lambda i: (0, 0))],
        out_shape=[jax.ShapeDtypeStruct((m, 2 * D_MEMQ), BF16), jax.ShapeDtypeStruct((d, m), BF16)],
        compiler_params=_params(("arbitrary",)),
    )(mem, g, wkv, dep)


def _memkv_bwd(dmkv, mem, g, wkv, nt):
    m, d = mem.shape

    def body(dmkv_ref, mem_ref, g_ref, w_ref, nt_ref, dw_ref, dg_ref):
        db = dmkv_ref[...].astype(BF16)
        dw_ref[...] = _dot(nt_ref[...], db).astype(BF16)
        dn = _dot_nt(db, w_ref[...])
        mm = mem_ref[...]
        dg_ref[...] = _sum8(dn * (mm * _rstd(mm)))

    return _pcall(
        body, name="memkv_bwd", grid=(1,),
        in_specs=[pl.BlockSpec((m, 2 * D_MEMQ), lambda i: (0, 0)),
                  pl.BlockSpec((m, d), lambda i: (0, 0)),
                  pl.BlockSpec((1, d), lambda i: (0, 0)),
                  pl.BlockSpec((d, 2 * D_MEMQ), lambda i: (0, 0)),
                  pl.BlockSpec((d, m), lambda i: (0, 0))],
        out_specs=[pl.BlockSpec((d, 2 * D_MEMQ), lambda i: (0, 0)),
                   pl.BlockSpec((8, d), lambda i: (0, 0))],
        out_shape=[jax.ShapeDtypeStruct((d, 2 * D_MEMQ), BF16), jax.ShapeDtypeStruct((8, d), F32)],
        compiler_params=_params(("arbitrary",)),
    )(dmkv, mem, g, wkv, nt)


def _shift_rows(v, k, edge_rows, row):
    out = pltpu.roll(v, k, 0)
    for r in range(k):
        out = jnp.where(row == r, edge_rows[r], out)
    return out


def _shift_rows_up(v, k, edge_rows, row):
    n = v.shape[0]
    out = pltpu.roll(v, n - k, 0)
    for r in range(k):
        out = jnp.where(row == n - k + r, edge_rows[r], out)
    return out


GROUP_ROWS = SWA_GROUP * BLOCK
BIAS_CUR, BIAS_PREV, BIAS_NONE = 0, 1, 2


def _bias_table():
    tq = np.arange(BLOCK)[:, None]
    sk = np.arange(BLOCK)[None, :]
    slopes = np.asarray(SLOPES, np.float32)[:, None, None]
    cur = np.where(tq >= sk, -slopes * (tq - sk).astype(np.float32), NEG)
    prev = np.where(sk > tq, -slopes * (tq + BLOCK - sk).astype(np.float32), NEG)
    none = np.full_like(cur, NEG)
    tok = np.stack([cur, prev, none]).astype(np.float32).reshape(3, N_SWA_KV, GROUP_ROWS, BLOCK)
    return jnp.asarray(np.ascontiguousarray(tok.transpose(0, 1, 3, 2)))


def _head_cols(hd):
    return D_CONV + HEAD_DIM * hd


def _mix_core_fwd(p, qh, mkv, convw, sinks, bias_key):
    t = p.shape[0]
    m = mkv.shape[0]
    nb = t // BLOCK
    per_step = 2 if nb % 2 == 0 else 1
    rows = per_step * BLOCK

    def body(sk_ref, pc_ref, pkv_ref, ppc_ref, ppu_ref, qh_ref, mkv_ref, cw_ref, bc_ref, bf_ref, bp_ref, y_ref, l_ref):
        i = pl.program_id(0)
        prevf = (i > 0).astype(F32)
        row = lax.broadcasted_iota(jnp.int32, (BLOCK, D_CONV), 0)
        head_row = lax.broadcasted_iota(jnp.int32, (128, BLOCK), 0)
        w = cw_ref[...]

        for b in range(per_step):
            tok = slice(b * BLOCK, (b + 1) * BLOCK)
            before = slice((b - 1) * BLOCK, b * BLOCK)
            tail = slice(b * BLOCK - 16, b * BLOCK)
            bg = pc_ref[tok, COL_BG:COL_BG + D_CONV].astype(F32)
            cg = pc_ref[tok, COL_CG:COL_CG + D_CONV].astype(F32)
            u = pc_ref[tok, COL_U:COL_U + D_CONV].astype(F32)
            vv = cg * u
            if b == 0:
                pvv = ppc_ref[...].astype(F32) * ppu_ref[...].astype(F32) * prevf
            else:
                pvv = (pc_ref[tail, COL_CG:COL_CG + D_CONV].astype(F32)
                       * pc_ref[tail, COL_U:COL_U + D_CONV].astype(F32))
            vv1 = _shift_rows(vv, 1, [pvv[15:16]], row)
            vv2 = _shift_rows(vv, 2, [pvv[14:15], pvv[15:16]], row)
            y_ref[tok, 0:D_CONV] = bg * (w[0:1] * vv2 + w[1:2] * vv1 + w[2:3] * vv)

            lse_t = jnp.zeros((128, BLOCK), F32)
            for kv in range(N_SWA_KV):
                heads = range(kv * SWA_GROUP, (kv + 1) * SWA_GROUP)
                kc = pc_ref[tok, COL_K + HEAD_DIM * kv:COL_K + HEAD_DIM * (kv + 1)]
                vc = pc_ref[tok, COL_V + HEAD_DIM * kv:COL_V + HEAD_DIM * (kv + 1)]
                if b == 0:
                    kp = pkv_ref[:, HEAD_DIM * kv:HEAD_DIM * (kv + 1)]
                    vp = pkv_ref[:, D_KV + HEAD_DIM * kv:D_KV + HEAD_DIM * (kv + 1)]
                    bias_prev = bf_ref[0, kv]
                else:
                    kp = pc_ref[before, COL_K + HEAD_DIM * kv:COL_K + HEAD_DIM * (kv + 1)]
                    vp = pc_ref[before, COL_V + HEAD_DIM * kv:COL_V + HEAD_DIM * (kv + 1)]
                    bias_prev = bp_ref[0, kv]
                qg = qh_ref[kv * SWA_GROUP:(kv + 1) * SWA_GROUP, tok].reshape(GROUP_ROWS, HEAD_DIM)
                sc = _dot_nt(kc, qg) + bc_ref[0, kv]
                sp = _dot_nt(kp, qg) + bias_prev
                sink = jnp.concatenate([jnp.full((1, BLOCK), sk_ref[0, hd], F32) for hd in heads], axis=1)
                mx = jnp.maximum(jnp.max(jnp.maximum(sc, sp), axis=0, keepdims=True), sink)
                ec = jnp.exp(sc - mx)
                ep = jnp.exp(sp - mx)
                den = jnp.sum(ec + ep, axis=0, keepdims=True) + jnp.exp(sink - mx)
                ot = (_dot_tn(vc, ec.astype(BF16)) + _dot_tn(vp, ep.astype(BF16))) / den
                lse = mx + jnp.log(den)
                for gi, hd in enumerate(heads):
                    span = slice(gi * BLOCK, (gi + 1) * BLOCK)
                    y_ref[tok, _head_cols(hd):_head_cols(hd) + HEAD_DIM] = ot[:, span].T
                    lse_t = jnp.where(head_row == hd, lse[:, span], lse_t)

            for hm in range(N_MEM_HEADS):
                hd = N_SWA_HEADS + hm
                mk = mkv_ref[:, HEAD_DIM * hm:HEAD_DIM * (hm + 1)]
                mv = mkv_ref[:, D_MEMQ + HEAD_DIM * hm:D_MEMQ + HEAD_DIM * (hm + 1)]
                s = _dot_nt(mk, qh_ref[hd, tok])
                mx = jnp.max(s, axis=0, keepdims=True)
                e = jnp.exp(s - mx)
                den = jnp.sum(e, axis=0, keepdims=True)
                y_ref[tok, _head_cols(hd):_head_cols(hd) + HEAD_DIM] = (_dot_tn(mv, e.astype(BF16)) / den).T
                lse_t = jnp.where(head_row == hd, mx + jnp.log(den), lse_t)
            l_ref[tok, :] = lse_t.T

    kv_col = COL_K // (2 * D_KV)
    bias_block = (1, N_SWA_KV, BLOCK, GROUP_ROWS)
    return _pcall(
        body, name="mix_core_fwd", grid=(nb // per_step,),
        in_specs=[pl.BlockSpec(memory_space=pltpu.SMEM),
                  pl.BlockSpec((rows, D_IN), lambda i: (i, 0)),
                  pl.BlockSpec((BLOCK, 2 * D_KV), lambda i: (jnp.maximum(i * per_step - 1, 0), kv_col)),
                  pl.BlockSpec((16, D_CONV), lambda i: (jnp.maximum(i * (rows // 16) - 1, 0), COL_CG // D_CONV)),
                  pl.BlockSpec((16, D_CONV), lambda i: (jnp.maximum(i * (rows // 16) - 1, 0), COL_U // D_CONV)),
                  pl.BlockSpec((N_HEADS, rows, HEAD_DIM), lambda i: (0, i, 0)),
                  pl.BlockSpec((m, 2 * D_MEMQ), lambda i: (0, 0)),
                  pl.BlockSpec((3, D_CONV), lambda i: (0, 0)),
                  pl.BlockSpec(bias_block, lambda i: (BIAS_CUR, 0, 0, 0)),
                  pl.BlockSpec(bias_block, lambda i: (jnp.where(i == 0, BIAS_NONE, BIAS_PREV), 0, 0, 0)),
                  pl.BlockSpec(bias_block, lambda i: (BIAS_PREV, 0, 0, 0))],
        out_specs=[pl.BlockSpec((rows, D_MIX), lambda i: (i, 0)),
                   pl.BlockSpec((rows, 128), lambda i: (i, 0))],
        out_shape=[jax.ShapeDtypeStruct((t, D_MIX), F32), jax.ShapeDtypeStruct((t, 128), F32)],
        compiler_params=_params(("parallel",)),
    )(sinks, p, p, p, p, qh, mkv, convw, bias_key, bias_key, bias_key)


def _mix_core_bwd(p, qh, dyconv, doh, delta, lse, mkv, convw, sinks, bias_key):
    t = p.shape[0]
    m = mkv.shape[0]
    nb = t // BLOCK

    def body(sk_ref, pc_ref, pkv_ref, ppc_ref, ppu_ref, pnb_ref, dyc_ref, dyn_ref, qc_ref, qn_ref, doc_ref, don_ref,
             dlc_ref, dln_ref, lc_ref, ln_ref, mkv_ref, cw_ref, bp_ref, bct_ref, bnt_ref,
             dp_ref, dmkv_ref, dcw_ref, dsk_ref):
        i = pl.program_id(0)
        prevf = (i > 0).astype(F32)
        nextf = (i < nb - 1).astype(F32)
        row = lax.broadcasted_iota(jnp.int32, (BLOCK, D_CONV), 0)

        @pl.when(i == 0)
        def _():
            dmkv_ref[...] = jnp.zeros_like(dmkv_ref)
            dcw_ref[...] = jnp.zeros_like(dcw_ref)
            dsk_ref[...] = jnp.zeros_like(dsk_ref)

        bg = pc_ref[:, COL_BG:COL_BG + D_CONV].astype(F32)
        cg = pc_ref[:, COL_CG:COL_CG + D_CONV].astype(F32)
        u = pc_ref[:, COL_U:COL_U + D_CONV].astype(F32)
        vv = cg * u
        pvv = ppc_ref[...].astype(F32) * ppu_ref[...].astype(F32) * prevf
        vv1 = _shift_rows(vv, 1, [pvv[15:16]], row)
        vv2 = _shift_rows(vv, 2, [pvv[14:15], pvv[15:16]], row)
        w = cw_ref[...]
        yconv = w[0:1] * vv2 + w[1:2] * vv1 + w[2:3] * vv
        dyo = dyc_ref[...]
        dyc = dyo * bg
        nxt = dyn_ref[...] * pnb_ref[...].astype(F32) * nextf
        d1 = _shift_rows_up(dyc, 1, [nxt[0:1]], row)
        d2 = _shift_rows_up(dyc, 2, [nxt[0:1], nxt[1:2]], row)
        dvv = w[2:3] * dyc + w[1:2] * d1 + w[0:1] * d2
        dp_ref[:, COL_BG:COL_BG + D_CONV] = (dyo * yconv).astype(BF16)
        dp_ref[:, COL_CG:COL_CG + D_CONV] = (dvv * u).astype(BF16)
        dp_ref[:, COL_U:COL_U + D_CONV] = (dvv * cg).astype(BF16)
        dcw_ref[0:1, :] += jnp.sum(dyc * vv2, axis=0, keepdims=True)
        dcw_ref[1:2, :] += jnp.sum(dyc * vv1, axis=0, keepdims=True)
        dcw_ref[2:3, :] += jnp.sum(dyc * vv, axis=0, keepdims=True)

        lse_t, dl_t = lc_ref[...].T, dlc_ref[...].T
        lse_nt, dl_nt = ln_ref[...].T, dln_ref[...].T

        def stack_rows(tile_t, heads):
            return jnp.concatenate([tile_t[hd:hd + 1, :] for hd in heads], axis=1)

        lane8 = jnp.where(lax.broadcasted_iota(jnp.int32, (8, 128), 0) == 0,
                          lax.broadcasted_iota(jnp.int32, (8, 128), 1), -1)
        dsk = jnp.zeros((8, 128), F32)
        for kv in range(N_SWA_KV):
            heads = range(kv * SWA_GROUP, (kv + 1) * SWA_GROUP)
            kc = pc_ref[:, COL_K + HEAD_DIM * kv:COL_K + HEAD_DIM * (kv + 1)]
            vc = pc_ref[:, COL_V + HEAD_DIM * kv:COL_V + HEAD_DIM * (kv + 1)]
            kp = pkv_ref[:, HEAD_DIM * kv:HEAD_DIM * (kv + 1)]
            vp = pkv_ref[:, D_KV + HEAD_DIM * kv:D_KV + HEAD_DIM * (kv + 1)]
            qg = qc_ref[kv * SWA_GROUP:(kv + 1) * SWA_GROUP].reshape(GROUP_ROWS, HEAD_DIM)
            dog = doc_ref[kv * SWA_GROUP:(kv + 1) * SWA_GROUP].reshape(GROUP_ROWS, HEAD_DIM)
            qn = qn_ref[kv * SWA_GROUP:(kv + 1) * SWA_GROUP].reshape(GROUP_ROWS, HEAD_DIM)
            don = don_ref[kv * SWA_GROUP:(kv + 1) * SWA_GROUP].reshape(GROUP_ROWS, HEAD_DIM)
            lse_row, dl_row = stack_rows(lse_t, heads), stack_rows(dl_t, heads)
            ptp = jnp.exp(_dot_nt(kp, qg) + bp_ref[0, kv] - lse_row)
            dstp = (ptp * (_dot_nt(vp, dog) - dl_row)).astype(BF16)
            dq = _dot_tn(dstp, kp)
            pt = jnp.exp(_dot_nt(kc, qg) + bct_ref[0, kv] - lse_row)
            dst = (pt * (_dot_nt(vc, dog) - dl_row)).astype(BF16)
            dv = _dot(pt.astype(BF16), dog)
            dk = _dot(dst, qg)
            dq = dq + _dot_tn(dst, kc)
            ptn = jnp.exp(_dot_nt(kc, qn) + bnt_ref[0, kv] - stack_rows(lse_nt, heads))
            dstn = (ptn * (_dot_nt(vc, don) - stack_rows(dl_nt, heads))).astype(BF16)
            dv = dv + _dot(ptn.astype(BF16), don)
            dk = dk + _dot(dstn, qn)
            dp_ref[:, COL_K + HEAD_DIM * kv:COL_K + HEAD_DIM * (kv + 1)] = dk.astype(BF16)
            dp_ref[:, COL_V + HEAD_DIM * kv:COL_V + HEAD_DIM * (kv + 1)] = dv.astype(BF16)
            sink = jnp.concatenate([jnp.full((1, BLOCK), sk_ref[0, hd], F32) for hd in heads], axis=1)
            sink_term = jnp.exp(sink - lse_row) * dl_row
            for gi, hd in enumerate(heads):
                span = slice(gi * BLOCK, (gi + 1) * BLOCK)
                dp_ref[:, _q_col(hd):_q_col(hd) + HEAD_DIM] = (dq[span] * SCALE).astype(BF16)
                dsk = dsk + jnp.where(lane8 == hd, -jnp.sum(sink_term[:, span], axis=1, keepdims=True), 0.0)
        dsk_ref[...] += dsk

        for hm in range(N_MEM_HEADS):
            hd = N_SWA_HEADS + hm
            qm, dom = qc_ref[hd], doc_ref[hd]
            mk = mkv_ref[:, HEAD_DIM * hm:HEAD_DIM * (hm + 1)]
            mv = mkv_ref[:, D_MEMQ + HEAD_DIM * hm:D_MEMQ + HEAD_DIM * (hm + 1)]
            pt = jnp.exp(_dot_nt(mk, qm) - lse_t[hd:hd + 1, :])
            dst = (pt * (_dot_nt(mv, dom) - dl_t[hd:hd + 1, :])).astype(BF16)
            dp_ref[:, _q_col(hd):_q_col(hd) + HEAD_DIM] = (_dot_tn(dst, mk) * SCALE).astype(BF16)
            dmkv_ref[:, HEAD_DIM * hm:HEAD_DIM * (hm + 1)] += _dot(dst, qm)
            dmkv_ref[:, D_MEMQ + HEAD_DIM * hm:D_MEMQ + HEAD_DIM * (hm + 1)] += _dot(pt.astype(BF16), dom)

    cur = lambda i: (i, 0)
    const = lambda i: (0, 0)
    rows16 = BLOCK // 16
    last16 = t // 16 - 1
    before = lambda col: (lambda i: (jnp.maximum(i * rows16 - 1, 0), col))
    after = lambda i: (jnp.minimum((i + 1) * rows16, last16), 0)
    heads_cur = lambda i: (0, i, 0)
    heads_next = lambda i: (0, jnp.minimum(i + 1, nb - 1), 0)
    stat_next = lambda i: (jnp.minimum(i + 1, nb - 1), 0)
    key_block = (1, N_SWA_KV, BLOCK, GROUP_ROWS)
    head_block = (N_HEADS, BLOCK, HEAD_DIM)
    return _pcall(
        body, name="mix_core_bwd", grid=(nb,),
        in_specs=[pl.BlockSpec(memory_space=pltpu.SMEM),
                  pl.BlockSpec((BLOCK, D_IN), cur),
                  pl.BlockSpec((BLOCK, 2 * D_KV), lambda i: (jnp.maximum(i - 1, 0), COL_K // (2 * D_KV))),
                  pl.BlockSpec((16, D_CONV), before(COL_CG // D_CONV)),
                  pl.BlockSpec((16, D_CONV), before(COL_U // D_CONV)),
                  pl.BlockSpec((16, D_CONV), after),
                  pl.BlockSpec((BLOCK, D_CONV), cur),
                  pl.BlockSpec((16, D_CONV), after),
                  pl.BlockSpec(head_block, heads_cur), pl.BlockSpec(head_block, heads_next),
                  pl.BlockSpec(head_block, heads_cur), pl.BlockSpec(head_block, heads_next),
                  pl.BlockSpec((BLOCK, 128), cur), pl.BlockSpec((BLOCK, 128), stat_next),
                  pl.BlockSpec((BLOCK, 128), cur), pl.BlockSpec((BLOCK, 128), stat_next),
                  pl.BlockSpec((m, 2 * D_MEMQ), const),
                  pl.BlockSpec((3, D_CONV), const),
                  pl.BlockSpec(key_block, lambda i: (jnp.where(i == 0, BIAS_NONE, BIAS_PREV), 0, 0, 0)),
                  pl.BlockSpec(key_block, lambda i: (BIAS_CUR, 0, 0, 0)),
                  pl.BlockSpec(key_block, lambda i: (jnp.where(i == nb - 1, BIAS_NONE, BIAS_PREV), 0, 0, 0))],
        out_specs=[pl.BlockSpec((BLOCK, D_IN), cur),
                   pl.BlockSpec((m, 2 * D_MEMQ), const),
                   pl.BlockSpec((8, D_CONV), const),
                   pl.BlockSpec((8, 128), const)],
        out_shape=[jax.ShapeDtypeStruct((t, D_IN), BF16),
                   jax.ShapeDtypeStruct((m, 2 * D_MEMQ), F32),
                   jax.ShapeDtypeStruct((8, D_CONV), F32),
                   jax.ShapeDtypeStruct((8, 128), F32)],
        compiler_params=_params(("arbitrary",)),
    )(sinks, p, p, p, p, p, dyconv, dyconv, qh, qh, doh, doh, delta, delta, lse, lse, mkv, convw,
      bias_key, bias_key, bias_key)


def _group_norms(y):
    out = []
    for a, b in MIX_GROUPS:
        ys = y[:, a:b]
        r = _rstd(ys)
        out.append((ys * r, r))
    return out


def _mix_out_fwd(y, h, g, wout):
    t, d = h.shape
    tm = _tok_block(t)

    def body(y_ref, h_ref, g_ref, w_ref, ho_ref, mt_ref):
        yhat = jnp.concatenate([yh for yh, _ in _group_norms(y_ref[...])], axis=-1)
        mixed = yhat * g_ref[...]
        mt_ref[...] = mixed.T.astype(BF16)
        ho_ref[...] = h_ref[...] + _dot(mixed.astype(BF16), w_ref[...])

    return _pcall(
        body, name="mix_out_fwd", grid=(t // tm,),
        in_specs=[pl.BlockSpec((tm, D_MIX), lambda i: (i, 0)),
                  pl.BlockSpec((tm, d), lambda i: (i, 0)),
                  pl.BlockSpec((1, D_MIX), lambda i: (0, 0)),
                  pl.BlockSpec((D_MIX, d), lambda i: (0, 0))],
        out_specs=[pl.BlockSpec((tm, d), lambda i: (i, 0)),
                   pl.BlockSpec((D_MIX, tm), lambda i: (0, i))],
        out_shape=[jax.ShapeDtypeStruct((t, d), F32), jax.ShapeDtypeStruct((D_MIX, t), BF16)],
        compiler_params=_params(("parallel",)),
    )(y, h, g, wout)


def _head_indicator():
    ind = np.zeros((D_MIX, 128), np.float32)
    for hd in range(N_HEADS):
        ind[_head_cols(hd):_head_cols(hd) + HEAD_DIM, hd] = 1.0
    return jnp.asarray(ind, BF16)


def _mix_out_bwd(dho, y, g, wout, mt, dep):
    t, d = dho.shape
    tm = _tok_block(t)
    ni = t // tm

    def body(dho_ref, y_ref, g_ref, w_ref, mt_ref, ind_ref, dep_ref, dyc_ref, doh_ref, dl_ref, dw_ref, dg_ref, acc_ref):
        i = pl.program_id(0)
        dhb = dho_ref[...].astype(BF16)
        dm = _dot_nt(dhb, w_ref[...])
        pw = _dot(mt_ref[...], dhb)
        gg = g_ref[...]
        yy = y_ref[...]
        dys = []
        dgs = []
        for (a, b), (yhat, r) in zip(MIX_GROUPS, _group_norms(yy)):
            dmg = dm[:, a:b]
            dgs.append(_sum8(dmg * yhat))
            dyh = dmg * gg[:, a:b]
            dys.append(r * (dyh - yhat * jnp.mean(dyh * yhat, axis=-1, keepdims=True)))
        dy = jnp.concatenate(dys, axis=-1)
        dyc_ref[...] = dy[:, 0:D_CONV]
        for hd in range(N_HEADS):
            doh_ref[hd] = dy[:, _head_cols(hd):_head_cols(hd) + HEAD_DIM].astype(BF16)
        prod = dy * yy
        hi = prod.astype(BF16)
        lo = (prod - hi.astype(F32)).astype(BF16)
        dl_ref[...] = _dot(hi, ind_ref[...]) + _dot(lo, ind_ref[...])
        part = jnp.concatenate(dgs, axis=-1)

        @pl.when(i == 0)
        def _():
            acc_ref[...] = pw
            dg_ref[...] = part

        @pl.when(i > 0)
        def _():
            acc_ref[...] += pw
            dg_ref[...] += part

        @pl.when(i == ni - 1)
        def _():
            dw_ref[...] = acc_ref[...].astype(BF16)

    return _pcall(
        body, name="mix_out_bwd", grid=(ni,),
        in_specs=[pl.BlockSpec((tm, d), lambda i: (i, 0)),
                  pl.BlockSpec((tm, D_MIX), lambda i: (i, 0)),
                  pl.BlockSpec((1, D_MIX), lambda i: (0, 0)),
                  pl.BlockSpec((D_MIX, d), lambda i: (0, 0)),
                  pl.BlockSpec((D_MIX, tm), lambda i: (0, i)),
                  pl.BlockSpec((D_MIX, 128), lambda i: (0, 0)),
                  pl.BlockSpec(memory_space=pl.ANY)],
        out_specs=[pl.BlockSpec((tm, D_CONV), lambda i: (i, 0)),
                   pl.BlockSpec((N_HEADS, tm, HEAD_DIM), lambda i: (0, i, 0)),
                   pl.BlockSpec((tm, 128), lambda i: (i, 0)),
                   pl.BlockSpec((D_MIX, d), lambda i: (0, 0)),
                   pl.BlockSpec((8, D_MIX), lambda i: (0, 0))],
        out_shape=[jax.ShapeDtypeStruct((t, D_CONV), F32),
                   jax.ShapeDtypeStruct((N_HEADS, t, HEAD_DIM), BF16),
                   jax.ShapeDtypeStruct((t, 128), F32),
                   jax.ShapeDtypeStruct((D_MIX, d), BF16),
                   jax.ShapeDtypeStruct((8, D_MIX), F32)],
        scratch_shapes=[pltpu.VMEM((D_MIX, d), F32)],
        compiler_params=_params(("arbitrary",)),
    )(dho, y, g, wout, mt, _head_indicator(), dep)


def _mix_proj_bwd(dp, dho, h, g, win_t, n):
    t, d = h.shape
    tm = _tok_block(t)
    ni = t // tm

    def body(dp_ref, dho_ref, h_ref, g_ref, w_ref, n_ref, dh_ref, dw_ref, dg_ref, acc_ref):
        i = pl.program_id(0)
        dpb = dp_ref[...]
        dn = _dot(dpb, w_ref[...])

        @pl.when(i == 0)
        def _():
            acc_ref[...] = jnp.zeros_like(acc_ref)

        acc_ref[...] += _dot_tn(dpb, n_ref[...])
        hh = h_ref[...]
        r = _rstd(hh)
        xhat = hh * r
        dxh = dn * g_ref[...]
        dh_ref[...] = dho_ref[...] + r * (dxh - xhat * jnp.mean(dxh * xhat, axis=-1, keepdims=True))
        part = _sum8(dn * xhat)

        @pl.when(i == 0)
        def _():
            dg_ref[...] = part

        @pl.when(i > 0)
        def _():
            dg_ref[...] += part

        @pl.when(i == ni - 1)
        def _():
            dw_ref[...] = acc_ref[...].astype(BF16)

    return _pcall(
        body, name="mix_proj_bwd", grid=(ni,),
        in_specs=[pl.BlockSpec((tm, D_IN), lambda i: (i, 0)),
                  pl.BlockSpec((tm, d), lambda i: (i, 0)),
                  pl.BlockSpec((tm, d), lambda i: (i, 0)),
                  pl.BlockSpec((1, d), lambda i: (0, 0)),
                  pl.BlockSpec((D_IN, d), lambda i: (0, 0)),
                  pl.BlockSpec((tm, d), lambda i: (i, 0))],
        out_specs=[pl.BlockSpec((tm, d), lambda i: (i, 0)),
                   pl.BlockSpec((D_IN, d), lambda i: (0, 0)),
                   pl.BlockSpec((8, d), lambda i: (0, 0))],
        out_shape=[jax.ShapeDtypeStruct((t, d), F32),
                   jax.ShapeDtypeStruct((D_IN, d), BF16),
                   jax.ShapeDtypeStruct((8, d), F32)],
        scratch_shapes=[pltpu.VMEM((D_IN, d), F32)],
        compiler_params=_params(("arbitrary",)),
    )(dp, dho, h, g, win_t, n)


def _final_loss(h, g, tgt):
    t, d = h.shape
    tm = _tok_block(t)

    def body(h_ref, g_ref, t_ref, dh_ref, ls_ref, dg_ref):
        i = pl.program_id(0)
        hh = h_ref[...]
        r = _rstd(hh)
        xhat = hh * r
        gg = g_ref[...]
        err = xhat * gg - t_ref[...]
        dy = err * (1.0 / d)
        dxh = dy * gg
        dh_ref[...] = r * (dxh - xhat * jnp.mean(dxh * xhat, axis=-1, keepdims=True))
        lpart = _sum8(err * err)
        gpart = _sum8(dy * xhat)

        @pl.when(i == 0)
        def _():
            ls_ref[...] = lpart
            dg_ref[...] = gpart

        @pl.when(i > 0)
        def _():
            ls_ref[...] += lpart
            dg_ref[...] += gpart

    return _pcall(
        body, name="final_loss", grid=(t // tm,),
        in_specs=[pl.BlockSpec((tm, d), lambda i: (i, 0)),
                  pl.BlockSpec((1, d), lambda i: (0, 0)),
                  pl.BlockSpec((tm, d), lambda i: (i, 0))],
        out_specs=[pl.BlockSpec((tm, d), lambda i: (i, 0)),
                   pl.BlockSpec((8, d), lambda i: (0, 0)),
                   pl.BlockSpec((8, d), lambda i: (0, 0))],
        out_shape=[jax.ShapeDtypeStruct((t, d), F32),
                   jax.ShapeDtypeStruct((8, d), F32),
                   jax.ShapeDtypeStruct((8, d), F32)],
        compiler_params=_params(("arbitrary",)),
    )(h, g, tgt)


def _position():
    return lax.axis_index("x"), lax.axis_index("y"), lax.axis_index("c")


def _flip(v, bit):
    return 1 - v if bit else v


def _peer(k):
    x, y, c = _position()
    return _flip(x, k & 4), _flip(y, k & 2), _flip(c, k & 1)


def _slot(px, py, pc):
    return 4 * px + 2 * py + pc


def _handshake(peers):
    barrier = pltpu.get_barrier_semaphore()
    for peer in peers:
        pl.semaphore_signal(barrier, inc=1, device_id=peer, device_id_type=MESH)
    pl.semaphore_wait(barrier, len(peers))


def _sequencer_call(body, name, collective_id, out_type, scratch_types, operands):
    return pl.kernel(
        body, out_type=out_type, mesh=plsc.ScalarSubcoreMesh(axis_name="sequencer", num_cores=1), name=name,
        scratch_types=scratch_types, compiler_params=pltpu.CompilerParams(collective_id=collective_id),
    )(*operands)


def _all_gather(shards, name, collective_id):
    nt = len(shards)

    def body(*refs):
        xs = refs[:nt]
        outs = refs[nt:2 * nt]
        send_sems, recv_sems, local_sems = refs[2 * nt:]
        x, y, c = _position()
        me, sibling = (x, y, c), (x, y, 1 - c)
        xn, yn, dg = (1 - x, y), (x, 1 - y), (1 - x, 1 - y)
        pick = lambda a, b: (jnp.where(c == 0, a[0], b[0]), jnp.where(c == 0, a[1], b[1]))
        relay_from, relay_to = pick(yn, xn), pick(xn, yn)
        _handshake([sibling, (*xn, c), (*yn, c)])

        def copy(t, k, block, to, src=None):
            dst = outs[t].at[_slot(*block)]
            return pltpu.make_async_remote_copy(
                src_ref=dst if src is None else src, dst_ref=dst,
                send_sem=send_sems.at[t, k], recv_sem=recv_sems.at[t, k],
                device_id=to, device_id_type=MESH)

        mine = [pltpu.make_async_copy(xs[t], outs[t].at[_slot(*me)], local_sems.at[t]) for t in range(nt)]
        for cp in mine:
            cp.start()
        sent = []
        for t in range(nt):
            sent += [copy(t, 0, me, sibling, src=xs[t]), copy(t, 1, me, (*xn, c), src=xs[t]),
                     copy(t, 2, me, (*yn, c), src=xs[t])]
        for cp in sent:
            cp.start()
        for t in range(nt):
            copy(t, 1, (*xn, c), me).wait_recv()
            copy(t, 2, (*yn, c), me).wait_recv()
            passed = [copy(t, 3, (*relay_from, c), (*relay_to, c)),
                      copy(t, 4, (*xn, c), sibling), copy(t, 5, (*yn, c), sibling)]
            for cp in passed:
                cp.start()
            sent += passed
        for t in range(nt):
            copy(t, 3, (*dg, c), me).wait_recv()
            fwd = copy(t, 6, (*dg, c), sibling)
            fwd.start()
            sent.append(fwd)
        for t in range(nt):
            copy(t, 0, sibling, me).wait_recv()
            for k, chip in ((4, xn), (5, yn), (6, dg)):
                copy(t, k, (*chip, 1 - c), me).wait_recv()
        for cp in sent:
            cp.wait_send()
        for cp in mine:
            cp.wait()

    return _sequencer_call(
        body, name, collective_id,
        out_type=[jax.ShapeDtypeStruct((N_DEV,) + s.shape, s.dtype) for s in shards],
        scratch_types=[pltpu.SemaphoreType.DMA((nt, 7)), pltpu.SemaphoreType.DMA((nt, 7)),
                       pltpu.SemaphoreType.DMA((nt,))],
        operands=shards)


def _scatter_copy(srcs, lands, send_sems, recv_sems, t, k):
    peer = _peer(k)
    return pltpu.make_async_remote_copy(
        src_ref=srcs[t].at[_slot(*peer)], dst_ref=lands[t].at[k],
        send_sem=send_sems.at[t * (N_DEV - 1) + k - 1], recv_sem=recv_sems.at[t * (N_DEV - 1) + k - 1],
        device_id=peer, device_id_type=MESH)


def _scatter_start(partials, name):
    nt = len(partials)

    def body(*refs):
        srcs, lands = refs[:nt], refs[nt:2 * nt]
        send_sems, recv_sems = refs[2 * nt], refs[2 * nt + 1]
        token = refs[-1]
        for k in range(1, N_DEV):
            for t in range(nt):
                _scatter_copy(srcs, lands, send_sems, recv_sems, t, k).start()
        token[...] = jnp.zeros_like(token)

    hbm = pl.BlockSpec(memory_space=pltpu.HBM)
    sem = pl.BlockSpec(memory_space=pltpu.SEMAPHORE)
    shapes = [pltpu.HBM(p.shape, p.dtype) for p in partials]
    lands = [pltpu.with_memory_space_constraint(lax.empty(p.shape, p.dtype), pltpu.HBM) for p in partials]
    srcs = [pltpu.with_memory_space_constraint(p, pltpu.HBM) for p in partials]
    out = _pcall(
        body, name=name,
        out_shape=[pltpu.SemaphoreType.DMA((nt * (N_DEV - 1),))] * 2 + shapes + shapes
        + [jax.ShapeDtypeStruct((8, 128), F32)],
        in_specs=[hbm] * (2 * nt),
        out_specs=[sem, sem] + [hbm] * (2 * nt) + [pl.BlockSpec(memory_space=pltpu.VMEM)],
        input_output_aliases={i: 2 + i for i in range(2 * nt)},
        compiler_params=pltpu.CompilerParams(has_side_effects=pltpu.SideEffectType.DATAFLOW_SIDE_EFFECTING),
    )(*srcs, *lands)
    return (nt, name, out[:-1]), out[-1]


def _scatter_wait(state, after):
    nt, name, (send_sems, recv_sems, *thru) = state

    def body(*refs):
        srcs, lands = refs[:nt], refs[nt:2 * nt]
        send_sems, recv_sems = refs[2 * nt], refs[2 * nt + 1]
        for k in range(1, N_DEV):
            for t in range(nt):
                copy = _scatter_copy(srcs, lands, send_sems, recv_sems, t, k)
                copy.wait_send()
                copy.wait_recv()

    hbm = pl.BlockSpec(memory_space=pltpu.HBM)
    sem = pl.BlockSpec(memory_space=pltpu.SEMAPHORE)
    out = _pcall(
        body, name=name + "_wait",
        out_shape=[pltpu.HBM(a.shape, a.dtype) for a in thru],
        in_specs=[hbm] * (2 * nt) + [sem, sem, pl.BlockSpec(memory_space=pl.ANY)],
        out_specs=[hbm] * (2 * nt),
        input_output_aliases={i: i for i in range(2 * nt)},
        compiler_params=pltpu.CompilerParams(has_side_effects=pltpu.SideEffectType.DATAFLOW_SIDE_EFFECTING),
    )(*thru, send_sems, recv_sems, after)
    return out[:nt], out[nt:]


def _all_reduce_rows(v, dep):
    nv, _, w = v.shape

    def body(v_ref, dep_ref, out_ref, mine_ref, gath_ref, send_sems, recv_sems):
        x, y, c = _position()
        me = _slot(x, y, c)
        mine_ref[...] = jnp.sum(v_ref[...], axis=1)

        def copy(k):
            return pltpu.make_async_remote_copy(
                src_ref=mine_ref, dst_ref=gath_ref.at[me],
                send_sem=send_sems.at[k - 1], recv_sem=recv_sems.at[k - 1],
                device_id=_peer(k), device_id_type=MESH)

        def arrival(k):
            return pltpu.make_async_remote_copy(
                src_ref=mine_ref, dst_ref=gath_ref.at[_slot(*_peer(k))],
                send_sem=send_sems.at[k - 1], recv_sem=recv_sems.at[k - 1],
                device_id=_peer(k), device_id_type=MESH)

        sent = [copy(k) for k in range(1, N_DEV)]
        for cp in sent:
            cp.start()
        gath_ref[me] = mine_ref[...]
        for k in range(1, N_DEV):
            arrival(k).wait_recv()
        for cp in sent:
            cp.wait_send()
        total = gath_ref[0]
        for s in range(1, N_DEV):
            total = total + gath_ref[s]
        out_ref[...] = total

    vmem = pl.BlockSpec(memory_space=pltpu.VMEM)
    return _pcall(
        body, name="all_reduce_rows",
        in_specs=[vmem, pl.BlockSpec(memory_space=pl.ANY)], out_specs=vmem,
        out_shape=jax.ShapeDtypeStruct((nv, w), F32),
        scratch_shapes=[pltpu.VMEM((nv, w), F32), pltpu.VMEM((N_DEV, nv, w), F32),
                        pltpu.SemaphoreType.DMA((7,)), pltpu.SemaphoreType.DMA((7,))],
    )(v, dep)


def _adamw_math(w, g, m, v):
    m2 = ADAM_B1 * m + (1.0 - ADAM_B1) * g
    v2 = ADAM_B2 * v + (1.0 - ADAM_B2) * (g * g)
    m_hat = m2 / (1.0 - ADAM_B1 ** ADAM_STEP)
    v_hat = v2 / (1.0 - ADAM_B2 ** ADAM_STEP)
    delta = -ADAM_LR * (m_hat / (jnp.sqrt(v_hat) + ADAM_EPS) + ADAM_WD * w)
    return delta, m2, v2


def _row_block(r):
    for cand in (256, 176, 128):
        if r % cand == 0:
            return cand
    return r


def _adamw_sharded(me, grads, w, m, v, dep, first_layer=0, prev=None):
    nl = len(grads)
    _, r, c = grads[0][1].shape
    tr = _row_block(r)
    nr = r // tr
    prev = list(prev or ())

    def body(me_ref, *refs):
        grad_refs = refs[:2 * nl]
        w_ref, m_ref, v_ref = refs[2 * nl:2 * nl + 3]
        g_ref, d_ref, m2_ref, v2_ref = refs[-4:]
        layer = pl.program_id(0)

        def total(own_ref, land_ref):
            acc = own_ref[0].astype(F32)
            for k in range(1, N_DEV):
                acc = acc + land_ref[k].astype(F32)
            return acc

        g = total(grad_refs[0], grad_refs[1])
        for k in range(1, nl):
            g = jnp.where(layer == k, total(grad_refs[2 * k], grad_refs[2 * k + 1]), g)
        delta, m2, v2 = _adamw_math(w_ref[0], g, m_ref[0], v_ref[0])
        g_ref[0] = g
        d_ref[0] = delta
        m2_ref[0] = m2
        v2_ref[0] = v2

    def grad_pair_specs(k):
        def rows(l, i):
            return jnp.where(l == k, i, jnp.where(l < k, 0, nr - 1))
        return [pl.BlockSpec((1, tr, c), lambda l, i, me_ref: (me_ref[0], rows(l, i), 0)),
                pl.BlockSpec((N_DEV, tr, c), lambda l, i, me_ref: (0, rows(l, i), 0))]

    grad_specs = [spec for k in range(nl) for spec in grad_pair_specs(k)]
    shard = pl.BlockSpec((1, tr, c), lambda l, i, me_ref: (first_layer + l, i, 0))
    untouched = pl.BlockSpec(memory_space=pl.ANY)
    out = jax.ShapeDtypeStruct(w.shape, F32)
    first_prev = 1 + 2 * nl + 4
    return _pcall(
        body, name="adamw_sharded",
        grid_spec=pltpu.PrefetchScalarGridSpec(
            num_scalar_prefetch=1, grid=(nl, nr),
            in_specs=grad_specs + [shard, shard, shard] + [untouched] * (1 + len(prev)),
            out_specs=[shard, shard, shard, shard]),
        out_shape=[out, out, out, out],
        input_output_aliases={first_prev + k: k for k in range(len(prev))},
        compiler_params=_params(("arbitrary", "arbitrary")),
    )(me, *[a for pair in grads for a in pair], w, m, v, dep, *prev)


def _adamw_small(w, g, m, v):
    def body(w_ref, g_ref, m_ref, v_ref, d_ref, m2_ref, v2_ref):
        delta, m2, v2 = _adamw_math(w_ref[...], g_ref[...], m_ref[...], v_ref[...])
        d_ref[...] = delta
        m2_ref[...] = m2
        v2_ref[...] = v2

    spec = pl.BlockSpec(w.shape, lambda i: (0, 0))
    out = jax.ShapeDtypeStruct(w.shape, F32)
    return _pcall(
        body, name="adamw_small", grid=(1,),
        in_specs=[spec] * 4, out_specs=[spec] * 3, out_shape=[out] * 3,
        compiler_params=_params(("arbitrary",)),
    )(w, g, m, v)


def _pack(arrs):
    flat = jnp.concatenate([a.reshape(-1) for a in arrs])
    n = flat.shape[0]
    rows = -(-n // 1024) * 8
    return jnp.pad(flat, (0, rows * 128 - n)).reshape(rows, 128)


def _unpack(packed, like):
    flat = packed.reshape(-1)
    out, off = [], 0
    for a in like:
        out.append(flat[off:off + a.size].reshape(a.shape))
        off += a.size
    return out


def kernel(x, mem, g_ffn1, w_ffn1_up, w_ffn1_down, g_mix, w_in, conv_w, sinks, g_mem, w_mem_kv, g_grp, w_out, g_ffn2, w_ffn2_up, w_ffn2_down, g_final, loss_target, m_g_ffn1, m_w_ffn1_up, m_w_ffn1_down, m_g_mix, m_w_in, m_conv_w, m_sinks, m_g_mem, m_w_mem_kv, m_g_grp, m_w_out, m_g_ffn2, m_w_ffn2_up, m_w_ffn2_down, m_g_final, v_g_ffn1, v_w_ffn1_up, v_w_ffn1_down, v_g_mix, v_w_in, v_conv_w, v_sinks, v_g_mem, v_w_mem_kv, v_g_grp, v_w_out, v_g_ffn2, v_w_ffn2_up, v_w_ffn2_down, v_g_final):
    depth = g_ffn1.shape[0]
    t, d = x.shape[1], x.shape[2]
    width = max(d, D_MIX)
    me = _slot(*_position())
    conv_shard = conv_w.shape[2]

    xin, memin, tgt = x[0], mem[0], loss_target[0]

    conv_tile = jnp.zeros((depth * 8, 128), F32).at[:, :conv_shard].set(
        jnp.pad(conv_w, ((0, 0), (0, 8 - conv_w.shape[1]), (0, 0))).reshape(depth * 8, conv_shard))
    tr = lambda a: jnp.swapaxes(a, -1, -2)
    bf = lambda a: a.astype(BF16)
    weights = []
    collective_id = 0
    for l in range(depth):
        groups = [[bf(tr(w_ffn1_up[l])), bf(w_ffn1_down[l])] + ([conv_tile] if l == 0 else []),
                  [bf(tr(w_in[l])), bf(w_mem_kv[l]), bf(w_out[l])],
                  [bf(tr(w_ffn2_up[l])), bf(w_ffn2_down[l])]]
        full = []
        for gi, shards in enumerate(groups):
            full.append(_all_gather(shards, f"all_gather_l{l}_g{gi}", collective_id))
            collective_id += 1
        if l == 0:
            conv_full = full[0][2].reshape(N_DEV, depth, 8, 128)[:, :, :3, :conv_shard]
            conv_full = conv_full.transpose(1, 2, 0, 3).reshape(depth, 3, N_DEV * conv_shard)
        weights.append(dict(
            up1=full[0][0].reshape(2, -1, d), dn1=full[0][1].reshape(-1, d),
            win=full[1][0].reshape(D_IN, d), wkv=full[1][1].reshape(d, 2 * D_MEMQ), wout=full[1][2].reshape(D_MIX, d),
            up2=full[2][0].reshape(2, -1, d), dn2=full[2][1].reshape(-1, d)))

    row = lambda a: a.reshape(1, -1)
    bias_key = _bias_table()

    h = xin
    saved = []
    for l in range(depth):
        wl = weights[l]
        s = dict(h0=h)
        h, s["gu1"], s["n1"] = _ffn_fwd(h, row(g_ffn1[l]), wl["up1"], wl["dn1"])
        s["h1"] = h
        s["p"], s["n_mix"], s["qh"] = _mix_proj_fwd(h, row(g_mix[l]), wl["win"])
        s["mkv"], s["nt_mem"] = _memkv_fwd(memin, row(g_mem[l]), wl["wkv"], s["p"])
        s["y"], s["lse"] = _mix_core_fwd(s["p"], s["qh"], s["mkv"], conv_full[l], row(sinks[l]), bias_key)
        h, s["mt"] = _mix_out_fwd(s["y"], h, row(g_grp[l]), wl["wout"])
        s["h2"] = h
        h, s["gu2"], s["n2"] = _ffn_fwd(h, row(g_ffn2[l]), wl["up2"], wl["dn2"])
        saved.append(s)

    dh, loss_part, dg_final = _final_loss(h, row(g_final), tgt)

    small = {}
    dep = loss_part

    def reduce_small(after):
        def lanes(a):
            return jnp.pad(a, ((0, 0), (0, width - a.shape[1])))

        def first_row(a):
            return lanes(jnp.pad(a, ((0, 8 - a.shape[0]), (0, 0))))

        vec_names = ["g_ffn1", "g_mix", "g_mem", "g_grp", "g_ffn2", "sinks"]
        tiles = [lanes(small[n, l]) for n in vec_names for l in range(depth)]
        tiles += [first_row(small["conv_w", l][k:k + 1]) for l in range(depth) for k in range(3)]
        tiles.append(lanes(dg_final))
        n_real = len(tiles)
        tiles.append(lanes(loss_part))
        tiles += [jnp.zeros((8, width), F32)] * (-len(tiles) % 8)
        summed = _all_reduce_rows(jnp.stack(tiles), after)
        loss_all = 0.5 * jnp.sum(summed[n_real]) / d

        def vec(n, wd):
            return jnp.stack([summed[vec_names.index(n) * depth + l, :wd] for l in range(depth)])

        conv_base = len(vec_names) * depth
        conv_grad = jnp.stack([jnp.stack([summed[conv_base + 3 * l + k, :D_CONV] for k in range(3)])
                               for l in range(depth)])
        grads_small = {
            "g_ffn1": vec("g_ffn1", d), "g_mix": vec("g_mix", d), "g_mem": vec("g_mem", d),
            "g_grp": vec("g_grp", D_MIX), "g_ffn2": vec("g_ffn2", d), "sinks": vec("sinks", N_SWA_HEADS),
            "conv_w": lax.dynamic_slice_in_dim(conv_grad, me * conv_shard, conv_shard, axis=2),
            "g_final": summed[n_real - 1, :d],
        }
        small_w = [("g_ffn1", g_ffn1, m_g_ffn1, v_g_ffn1), ("g_mix", g_mix, m_g_mix, v_g_mix),
                   ("conv_w", conv_w, m_conv_w, v_conv_w), ("sinks", sinks, m_sinks, v_sinks),
                   ("g_mem", g_mem, m_g_mem, v_g_mem), ("g_grp", g_grp, m_g_grp, v_g_grp),
                   ("g_ffn2", g_ffn2, m_g_ffn2, v_g_ffn2), ("g_final", g_final, m_g_final, v_g_final)]
        like = [w for _, w, _, _ in small_w]
        packed = _adamw_small(_pack(like), _pack([grads_small[n] for n, _, _, _ in small_w]),
                              _pack([m for _, _, m, _ in small_w]), _pack([v for _, _, _, v in small_w]))
        updated = {n: (grads_small[n], dl, m2, v2)
                   for (n, _, _, _), dl, m2, v2 in zip(small_w, *[_unpack(pk, like) for pk in packed])}
        return loss_all, updated, packed[0]

    started = []

    def scatter(names, partials, label):
        state, token = _scatter_start(partials, f"scatter_grads_{label}")
        started.append((names, state))
        return token

    for l in reversed(range(depth)):
        wl, s = weights[l], saved[l]
        dh, agu, dyb, small["g_ffn2", l] = _ffn_bwd_act(dh, s["h2"], row(g_ffn2[l]), s["gu2"], wl["up2"], wl["dn2"], dep)
        ddn2 = _ffn_bwd_w(agu, 2, 1, dyb, agu, f"ffn_bwd_w_down_l{l}_ffn2").reshape(N_DEV, -1, d)
        dup2 = _ffn_bwd_w(agu, 0, 2, s["n2"], ddn2, f"ffn_bwd_w_up_l{l}_ffn2").reshape(N_DEV, -1, d)
        dep = scatter([("w_ffn2_up", l), ("w_ffn2_down", l)], [dup2, ddn2], f"l{l}_ffn2")
        dyconv, doh, delta, dwout, small["g_grp", l] = _mix_out_bwd(dh, s["y"], row(g_grp[l]), wl["wout"], s["mt"], dep)
        dp, dmkv, small["conv_w", l], small["sinks", l] = _mix_core_bwd(
            s["p"], s["qh"], dyconv, doh, delta, s["lse"], s["mkv"], conv_full[l], row(sinks[l]), bias_key)
        dwkv, small["g_mem", l] = _memkv_bwd(dmkv, memin, row(g_mem[l]), wl["wkv"], s["nt_mem"])
        dh, dwin, small["g_mix", l] = _mix_proj_bwd(dp, dh, s["h1"], row(g_mix[l]), wl["win"], s["n_mix"])
        dep = scatter([("w_in", l), ("w_mem_kv", l), ("w_out", l)],
                      [dwin.reshape(N_DEV, -1, d), dwkv.reshape(N_DEV, -1, 2 * D_MEMQ), dwout.reshape(N_DEV, -1, d)],
                      f"l{l}_mix")
        dh, agu, dyb, small["g_ffn1", l] = _ffn_bwd_act(dh, s["h0"], row(g_ffn1[l]), s["gu1"], wl["up1"], wl["dn1"], dep)
        ddn1 = _ffn_bwd_w(agu, 2, 1, dyb, agu, f"ffn_bwd_w_down_l{l}_ffn1").reshape(N_DEV, -1, d)
        if l > 0:
            dup1 = _ffn_bwd_w(agu, 0, 2, s["n1"], ddn1, f"ffn_bwd_w_up_l{l}_ffn1").reshape(N_DEV, -1, d)
            dep = scatter([("w_ffn1_up", l), ("w_ffn1_down", l)], [dup1, ddn1], f"l{l}_ffn1")
        else:
            dep = scatter([("w_ffn1_down", l)], [ddn1], f"l{l}_ffn1_down")
            dup1 = _ffn_bwd_w(agu, 0, 2, s["n1"], dep, f"ffn_bwd_w_up_l{l}_ffn1").reshape(N_DEV, -1, d)
            dep = scatter([("w_ffn1_up", l)], [dup1], f"l{l}_ffn1_up")
    grad_x = dh[None]

    big = {"w_ffn2_up": (w_ffn2_up, m_w_ffn2_up, v_w_ffn2_up, True), "w_ffn2_down": (w_ffn2_down, m_w_ffn2_down, v_w_ffn2_down, False),
           "w_in": (w_in, m_w_in, v_w_in, True), "w_mem_kv": (w_mem_kv, m_w_mem_kv, v_w_mem_kv, False),
           "w_out": (w_out, m_w_out, v_w_out, False), "w_ffn1_up": (w_ffn1_up, m_w_ffn1_up, v_w_ffn1_up, True),
           "w_ffn1_down": (w_ffn1_down, m_w_ffn1_down, v_w_ffn1_down, False)}
    me_index = jnp.reshape(me, (1,)).astype(jnp.int32)
    sharded, landed, begun = {}, {}, {}
    by_layer = {name for name, _ in started[-1][0]}

    def finish(groups, after):
        for names, state in groups:
            owns, lands = _scatter_wait(state, after)
            for key, own, land in zip(names, owns, lands):
                landed[key] = (own, land)
            after = lands[0]
            for name, l in names:
                w, m, v, transposed = big[name]
                fix = tr if transposed else (lambda a: a)
                if name in by_layer:
                    res = _adamw_sharded(me_index, [landed[name, l]], fix(w), fix(m), fix(v), after, l, begun.get(name))
                    done = name in begun
                    begun[name] = res
                elif all((name, k) in landed for k in range(depth)):
                    res = _adamw_sharded(me_index, [landed[name, k] for k in range(depth)], fix(w), fix(m), fix(v), after)
                    done = True
                else:
                    continue
                if done:
                    sharded[name] = tuple(fix(r) for r in res)
                after = res[0]
        return after

    loss, small_out, dep = reduce_small(finish(started[:-1], dep))
    finish(started[-1:], dep)

    order = ["g_ffn1", "w_ffn1_up", "w_ffn1_down", "g_mix", "w_in", "conv_w", "sinks", "g_mem", "w_mem_kv", "g_grp",
             "w_out", "g_ffn2", "w_ffn2_up", "w_ffn2_down", "g_final"]
    results = {**sharded, **small_out}
    outs = [loss, grad_x]
    for part in range(4):
        outs += [results[n][part] for n in order]
    return tuple(outs)
```

```python
import numpy as np
import jax
import jax.numpy as jnp
from jax import lax
from jax.experimental import pallas as pl
from jax.experimental.pallas import tpu as pltpu
from jax.experimental.pallas import tpu_sc as plsc

F32 = jnp.float32
BF16 = jnp.bfloat16

N_DEV = 8
EPS = 1e-6
N_SWA_HEADS = 8
N_SWA_KV = 2
SWA_GROUP = N_SWA_HEADS // N_SWA_KV
HEAD_DIM = 64
N_MEM_HEADS = 4
D_CONV = 256
BLOCK = 128
D_SWA = N_SWA_HEADS * HEAD_DIM
D_KV = N_SWA_KV * HEAD_DIM
D_MEMQ = N_MEM_HEADS * HEAD_DIM
D_MIX = D_CONV + D_SWA + D_MEMQ
D_IN = 3 * D_CONV + D_SWA + 2 * D_KV + D_MEMQ
COL_BG, COL_CG, COL_U = 0, D_CONV, 2 * D_CONV
COL_Q = 3 * D_CONV
COL_K = COL_Q + D_SWA
COL_V = COL_K + D_KV
COL_QM = COL_V + D_KV
MIX_GROUPS = ((0, D_CONV), (D_CONV, D_CONV + D_SWA), (D_CONV + D_SWA, D_MIX))
SLOPES = tuple(2.0 ** (-8.0 * (i + 1) / N_SWA_HEADS) for i in range(N_SWA_HEADS))
SCALE = HEAD_DIM ** -0.5
NEG = -1e30

ADAM_LR = 0.001
ADAM_B1 = 0.9
ADAM_B2 = 0.999
ADAM_EPS = 1e-08
ADAM_WD = 0.01
ADAM_STEP = 10

V7X_VMEM_BYTES = 64 * 1024 * 1024
VMEM_LIMIT = (V7X_VMEM_BYTES * 3) // 4
VMEM_LIMIT_WIDE = (V7X_VMEM_BYTES * 15) // 16
MESH = pl.DeviceIdType.MESH


def _pcall(body, **kw):
    return pl.pallas_call(body, **kw)


def _params(sem=None, vmem=VMEM_LIMIT):
    return pltpu.CompilerParams(dimension_semantics=sem, vmem_limit_bytes=vmem)


def _dot(a, b):
    return lax.dot_general(a, b, (((1,), (0,)), ((), ())), preferred_element_type=F32)


def _dot_nt(a, b):
    return lax.dot_general(a, b, (((1,), (1,)), ((), ())), preferred_element_type=F32)


def _dot_tn(a, b):
    return lax.dot_general(a, b, (((0,), (0,)), ((), ())), preferred_element_type=F32)


def _rstd(x):
    return lax.rsqrt(jnp.mean(x * x, axis=-1, keepdims=True) + EPS)


def _sigmoid(x):
    return 1.0 / (1.0 + jnp.exp(-x))


def _sum8(x):
    r, w = x.shape
    return jnp.sum(x.reshape(r // 8, 8, w), axis=0)


def _tok_block(t, rows=512):
    return min(rows, t)


def _feat_block(f, parts=N_DEV // 2):
    return f // parts


def _ffn_fwd(h, g, wup_t, wdn):
    t, d = h.shape
    f = wdn.shape[0]
    tm, tf = _tok_block(t), _feat_block(f, 2)
    ni, nj = t // tm, f // tf

    def body(h_ref, g_ref, wup_ref, wdn_ref, ho_ref, gu_ref, n_ref, nt_ref, acc_ref):
        j = pl.program_id(1)

        @pl.when(j == 0)
        def _():
            hh = h_ref[...]
            n = hh * _rstd(hh) * g_ref[...]
            n_ref[...] = n.astype(BF16)
            nt_ref[...] = n.T.astype(BF16)
            acc_ref[...] = jnp.zeros_like(acc_ref)

        nt = nt_ref[...]
        gate = _dot(wup_ref[0], nt)
        up = _dot(wup_ref[1], nt)
        gu_ref[0] = gate.astype(BF16)
        gu_ref[1] = up.astype(BF16)
        a = gate * _sigmoid(gate) * up
        acc_ref[...] += _dot_tn(a.astype(BF16), wdn_ref[...])

        @pl.when(j == nj - 1)
        def _():
            ho_ref[...] = h_ref[...] + 0.5 * acc_ref[...]

    return _pcall(
        body, name="ffn_fwd", grid=(ni, nj),
        in_specs=[pl.BlockSpec((tm, d), lambda i, j: (i, 0)),
                  pl.BlockSpec((1, d), lambda i, j: (0, 0)),
                  pl.BlockSpec((2, tf, d), lambda i, j: (0, j, 0)),
                  pl.BlockSpec((tf, d), lambda i, j: (j, 0))],
        out_specs=[pl.BlockSpec((tm, d), lambda i, j: (i, 0)),
                   pl.BlockSpec((2, tf, tm), lambda i, j: (0, j, i)),
                   pl.BlockSpec((tm, d), lambda i, j: (i, 0))],
        out_shape=[jax.ShapeDtypeStruct((t, d), F32),
                   jax.ShapeDtypeStruct((2, f, t), BF16),
                   jax.ShapeDtypeStruct((t, d), BF16)],
        scratch_shapes=[pltpu.VMEM((d, tm), BF16), pltpu.VMEM((tm, d), F32)],
        compiler_params=_params(("parallel", "arbitrary")),
    )(h, g, wup_t, wdn)


def _ffn_bwd_act(dho, h, g, gu, wup_t, wdn, dep):
    t, d = h.shape
    f = wdn.shape[0]
    tm, tf = _tok_block(t), _feat_block(f, 2)
    ni, nj = t // tm, f // tf

    def body(dho_ref, h_ref, g_ref, gu_ref, wup_ref, wdn_ref, dep_ref, dh_ref, agu_ref, dyb_ref, dg_ref, dyt_ref, acc_ref):
        i = pl.program_id(0)
        j = pl.program_id(1)

        @pl.when(j == 0)
        def _():
            dy0 = 0.5 * dho_ref[...]
            dyb_ref[...] = dy0.astype(BF16)
            dyt_ref[...] = dy0.T.astype(BF16)
            acc_ref[...] = jnp.zeros_like(acc_ref)

        da = _dot(wdn_ref[...], dyt_ref[...]).astype(BF16)
        gate = gu_ref[0]
        up = gu_ref[1]
        sg = _sigmoid(gate)
        silu = gate * sg
        dgate = da * up * (sg * (1.0 + gate * (1.0 - sg)))
        dup = da * silu
        agu_ref[0] = dgate
        agu_ref[1] = dup
        agu_ref[2] = silu * up
        acc_ref[...] += _dot_tn(dgate, wup_ref[0])
        acc_ref[...] += _dot_tn(dup, wup_ref[1])

        @pl.when(j == nj - 1)
        def _():
            hh = h_ref[...]
            r = _rstd(hh)
            xhat = hh * r
            dnf = acc_ref[...]
            dxh = dnf * g_ref[...]
            dh_ref[...] = dho_ref[...] + r * (dxh - xhat * jnp.mean(dxh * xhat, axis=-1, keepdims=True))
            part = _sum8(dnf * xhat)

            @pl.when(i == 0)
            def _():
                dg_ref[...] = part

            @pl.when(i > 0)
            def _():
                dg_ref[...] += part

    return _pcall(
        body, name="ffn_bwd_act", grid=(ni, nj),
        in_specs=[pl.BlockSpec((tm, d), lambda i, j: (i, 0)),
                  pl.BlockSpec((tm, d), lambda i, j: (i, 0)),
                  pl.BlockSpec((1, d), lambda i, j: (0, 0)),
                  pl.BlockSpec((2, tf, tm), lambda i, j: (0, j, i)),
                  pl.BlockSpec((2, tf, d), lambda i, j: (0, j, 0)),
                  pl.BlockSpec((tf, d), lambda i, j: (j, 0)),
                  pl.BlockSpec(memory_space=pl.ANY)],
        out_specs=[pl.BlockSpec((tm, d), lambda i, j: (i, 0)),
                   pl.BlockSpec((3, tf, tm), lambda i, j: (0, j, i)),
                   pl.BlockSpec((tm, d), lambda i, j: (i, 0)),
                   pl.BlockSpec((8, d), lambda i, j: (0, 0))],
        out_shape=[jax.ShapeDtypeStruct((t, d), F32),
                   jax.ShapeDtypeStruct((3, f, t), BF16),
                   jax.ShapeDtypeStruct((t, d), BF16),
                   jax.ShapeDtypeStruct((8, d), F32)],
        scratch_shapes=[pltpu.VMEM((d, tm), BF16), pltpu.VMEM((tm, d), F32)],
        compiler_params=_params(("arbitrary", "arbitrary"), VMEM_LIMIT_WIDE),
    )(dho, h, g, gu, wup_t, wdn, dep)


def _ffn_bwd_w(agu, first, count, rhs, dep, name):
    _, f, t = agu.shape
    d = rhs.shape[1]
    tm = _tok_block(t, 2048)
    tf = _feat_block(f, 2) if count == 1 else _feat_block(f)
    ni, nj = t // tm, f // tf

    def body(lhs_ref, rhs_ref, dep_ref, dw_ref, acc_ref):
        i = pl.program_id(1)
        @pl.when(i == 0)
        def _():
            acc_ref[...] = jnp.zeros_like(acc_ref)

        rb = rhs_ref[...]
        for k in range(count):
            acc_ref[k] += _dot(lhs_ref[k], rb)

        @pl.when(i == ni - 1)
        def _():
            dw_ref[...] = acc_ref[...].astype(BF16)

    return _pcall(
        body, name=name, grid=(nj, ni),
        in_specs=[pl.BlockSpec((count, tf, tm), lambda j, i: (first // count, j, i)),
                  pl.BlockSpec((tm, d), lambda j, i: (i, 0)),
                  pl.BlockSpec(memory_space=pl.ANY)],
        out_specs=pl.BlockSpec((count, tf, d), lambda j, i: (0, j, 0)),
        out_shape=jax.ShapeDtypeStruct((count, f, d), BF16),
        scratch_shapes=[pltpu.VMEM((count, tf, d), F32)],
        compiler_params=_params(("parallel", "arbitrary")),
    )(agu, rhs, dep)


N_HEADS = N_SWA_HEADS + N_MEM_HEADS


def _q_col(hd):
    return COL_Q + HEAD_DIM * hd if hd < N_SWA_HEADS else COL_QM + HEAD_DIM * (hd - N_SWA_HEADS)


def _mix_proj_fwd(h, g, win_t):
    t, d = h.shape
    tm = _tok_block(t)

    def body(h_ref, g_ref, win_ref, p_ref, n_ref, qh_ref):
        hh = h_ref[...]
        n = (hh * _rstd(hh) * g_ref[...]).astype(BF16)
        n_ref[...] = n
        proj = _dot_nt(n, win_ref[...])
        p_ref[...] = proj.astype(BF16)
        for hd in range(N_HEADS):
            c0 = _q_col(hd)
            qh_ref[hd] = (proj[:, c0:c0 + HEAD_DIM] * SCALE).astype(BF16)

    return _pcall(
        body, name="mix_proj_fwd", grid=(t // tm,),
        in_specs=[pl.BlockSpec((tm, d), lambda i: (i, 0)),
                  pl.BlockSpec((1, d), lambda i: (0, 0)),
                  pl.BlockSpec((D_IN, d), lambda i: (0, 0))],
        out_specs=[pl.BlockSpec((tm, D_IN), lambda i: (i, 0)),
                   pl.BlockSpec((tm, d), lambda i: (i, 0)),
                   pl.BlockSpec((N_HEADS, tm, HEAD_DIM), lambda i: (0, i, 0))],
        out_shape=[jax.ShapeDtypeStruct((t, D_IN), BF16), jax.ShapeDtypeStruct((t, d), BF16),
                   jax.ShapeDtypeStruct((N_HEADS, t, HEAD_DIM), BF16)],
        compiler_params=_params(("parallel",)),
    )(h, g, win_t)


def _memkv_fwd(mem, g, wkv, dep):
    m, d = mem.shape

    def body(mem_ref, g_ref, w_ref, dep_ref, mkv_ref, nt_ref):
        mm = mem_ref[...]
        n = mm * _rstd(mm) * g_ref[...]
        nt_ref[...] = n.T.astype(BF16)
        mkv_ref[...] = _dot(n.astype(BF16), w_ref[...]).astype(BF16)

    return _pcall(
        body, name="memkv_fwd", grid=(1,),
        in_specs=[pl.BlockSpec((m, d), lambda i: (0, 0)),
                  pl.BlockSpec((1, d), lambda i: (0, 0)),
                  pl.BlockSpec((d, 2 * D_MEMQ), lambda i: (0, 0)),
                  pl.BlockSpec(memory_space=pl.ANY)],
        out_specs=[pl.BlockSpec((m, 2 * D_MEMQ), lambda i: (0, 0)),
                   pl.BlockSpec((d, m), lambda i: (0, 0))],
        out_shape=[jax.ShapeDtypeStruct((m, 2 * D_MEMQ), BF16), jax.ShapeDtypeStruct((d, m), BF16)],
        compiler_params=_params(("arbitrary",)),
    )(mem, g, wkv, dep)


def _memkv_bwd(dmkv, mem, g, wkv, nt):
    m, d = mem.shape

    def body(dmkv_ref, mem_ref, g_ref, w_ref, nt_ref, dw_ref, dg_ref):
        db = dmkv_ref[...].astype(BF16)
        dw_ref[...] = _dot(nt_ref[...], db).astype(BF16)
        dn = _dot_nt(db, w_ref[...])
        mm = mem_ref[...]
        dg_ref[...] = _sum8(dn * (mm * _rstd(mm)))

    return _pcall(
        body, name="memkv_bwd", grid=(1,),
        in_specs=[pl.BlockSpec((m, 2 * D_MEMQ), lambda i: (0, 0)),
                  pl.BlockSpec((m, d), lambda i: (0, 0)),
                  pl.BlockSpec((1, d), lambda i: (0, 0)),
                  pl.BlockSpec((d, 2 * D_MEMQ), lambda i: (0, 0)),
                  pl.BlockSpec((d, m), lambda i: (0, 0))],
        out_specs=[pl.BlockSpec((d, 2 * D_MEMQ), lambda i: (0, 0)),
                   pl.BlockSpec((8, d), lambda i: (0, 0))],
        out_shape=[jax.ShapeDtypeStruct((d, 2 * D_MEMQ), BF16), jax.ShapeDtypeStruct((8, d), F32)],
        compiler_params=_params(("arbitrary",)),
    )(dmkv, mem, g, wkv, nt)


def _shift_rows(v, k, edge_rows, row):
    out = pltpu.roll(v, k, 0)
    for r in range(k):
        out = jnp.where(row == r, edge_rows[r], out)
    return out


def _shift_rows_up(v, k, edge_rows, row):
    n = v.shape[0]
    out = pltpu.roll(v, n - k, 0)
    for r in range(k):
        out = jnp.where(row == n - k + r, edge_rows[r], out)
    return out


GROUP_ROWS = SWA_GROUP * BLOCK
BIAS_CUR, BIAS_PREV, BIAS_NONE = 0, 1, 2


def _bias_table():
    tq = np.arange(BLOCK)[:, None]
    sk = np.arange(BLOCK)[None, :]
    slopes = np.asarray(SLOPES, np.float32)[:, None, None]
    cur = np.where(tq >= sk, -slopes * (tq - sk).astype(np.float32), NEG)
    prev = np.where(sk > tq, -slopes * (tq + BLOCK - sk).astype(np.float32), NEG)
    none = np.full_like(cur, NEG)
    tok = np.stack([cur, prev, none]).astype(np.float32).reshape(3, N_SWA_KV, GROUP_ROWS, BLOCK)
    return jnp.asarray(np.ascontiguousarray(tok.transpose(0, 1, 3, 2)))


def _head_cols(hd):
    return D_CONV + HEAD_DIM * hd


def _mix_core_fwd(p, qh, mkv, convw, sinks, bias_key):
    t = p.shape[0]
    m = mkv.shape[0]
    nb = t // BLOCK
    per_step = next(n for n in (4, 2, 1) if nb % n == 0)
    rows = per_step * BLOCK

    def body(sk_ref, pc_ref, pkv_ref, ppc_ref, ppu_ref, qh_ref, mkv_ref, cw_ref, bc_ref, bf_ref, bp_ref, y_ref, l_ref):
        i = pl.program_id(0)
        prevf = (i > 0).astype(F32)
        row = lax.broadcasted_iota(jnp.int32, (BLOCK, D_CONV), 0)
        head_row = lax.broadcasted_iota(jnp.int32, (128, BLOCK), 0)
        w = cw_ref[...]

        for b in range(per_step):
            tok = slice(b * BLOCK, (b + 1) * BLOCK)
            before = slice((b - 1) * BLOCK, b * BLOCK)
            tail = slice(b * BLOCK - 16, b * BLOCK)
            bg = pc_ref[tok, COL_BG:COL_BG + D_CONV].astype(F32)
            cg = pc_ref[tok, COL_CG:COL_CG + D_CONV].astype(F32)
            u = pc_ref[tok, COL_U:COL_U + D_CONV].astype(F32)
            vv = cg * u
            if b == 0:
                pvv = ppc_ref[...].astype(F32) * ppu_ref[...].astype(F32) * prevf
            else:
                pvv = (pc_ref[tail, COL_CG:COL_CG + D_CONV].astype(F32)
                       * pc_ref[tail, COL_U:COL_U + D_CONV].astype(F32))
            vv1 = _shift_rows(vv, 1, [pvv[15:16]], row)
            vv2 = _shift_rows(vv, 2, [pvv[14:15], pvv[15:16]], row)
            y_ref[tok, 0:D_CONV] = bg * (w[0:1] * vv2 + w[1:2] * vv1 + w[2:3] * vv)

            lse_t = jnp.zeros((128, BLOCK), F32)
            for kv in range(N_SWA_KV):
                heads = range(kv * SWA_GROUP, (kv + 1) * SWA_GROUP)
                kc = pc_ref[tok, COL_K + HEAD_DIM * kv:COL_K + HEAD_DIM * (kv + 1)]
                vc = pc_ref[tok, COL_V + HEAD_DIM * kv:COL_V + HEAD_DIM * (kv + 1)]
                if b == 0:
                    kp = pkv_ref[:, HEAD_DIM * kv:HEAD_DIM * (kv + 1)]
                    vp = pkv_ref[:, D_KV + HEAD_DIM * kv:D_KV + HEAD_DIM * (kv + 1)]
                    bias_prev = bf_ref[0, kv]
                else:
                    kp = pc_ref[before, COL_K + HEAD_DIM * kv:COL_K + HEAD_DIM * (kv + 1)]
                    vp = pc_ref[before, COL_V + HEAD_DIM * kv:COL_V + HEAD_DIM * (kv + 1)]
                    bias_prev = bp_ref[0, kv]
                qg = qh_ref[kv * SWA_GROUP:(kv + 1) * SWA_GROUP, tok].reshape(GROUP_ROWS, HEAD_DIM)
                sc = _dot_nt(kc, qg) + bc_ref[0, kv]
                sp = _dot_nt(kp, qg) + bias_prev
                sink = jnp.concatenate([jnp.full((1, BLOCK), sk_ref[0, hd], F32) for hd in heads], axis=1)
                mx = jnp.maximum(jnp.max(jnp.maximum(sc, sp), axis=0, keepdims=True), sink)
                ec = jnp.exp(sc - mx)
                ep = jnp.exp(sp - mx)
                den = jnp.sum(ec + ep, axis=0, keepdims=True) + jnp.exp(sink - mx)
                ot = (_dot_tn(vc, ec.astype(BF16)) + _dot_tn(vp, ep.astype(BF16))) / den
                lse = mx + jnp.log(den)
                for gi, hd in enumerate(heads):
                    span = slice(gi * BLOCK, (gi + 1) * BLOCK)
                    y_ref[tok, _head_cols(hd):_head_cols(hd) + HEAD_DIM] = ot[:, span].T
                    lse_t = jnp.where(head_row == hd, lse[:, span], lse_t)

            for hm in range(N_MEM_HEADS):
                hd = N_SWA_HEADS + hm
                mk = mkv_ref[:, HEAD_DIM * hm:HEAD_DIM * (hm + 1)]
                mv = mkv_ref[:, D_MEMQ + HEAD_DIM * hm:D_MEMQ + HEAD_DIM * (hm + 1)]
                s = _dot_nt(mk, qh_ref[hd, tok])
                mx = jnp.max(s, axis=0, keepdims=True)
                e = jnp.exp(s - mx)
                den = jnp.sum(e, axis=0, keepdims=True)
                y_ref[tok, _head_cols(hd):_head_cols(hd) + HEAD_DIM] = (_dot_tn(mv, e.astype(BF16)) / den).T
                lse_t = jnp.where(head_row == hd, mx + jnp.log(den), lse_t)
            l_ref[tok, :] = lse_t.T

    kv_col = COL_K // (2 * D_KV)
    bias_block = (1, N_SWA_KV, BLOCK, GROUP_ROWS)
    return _pcall(
        body, name="mix_core_fwd", grid=(nb // per_step,),
        in_specs=[pl.BlockSpec(memory_space=pltpu.SMEM),
                  pl.BlockSpec((rows, D_IN), lambda i: (i, 0)),
                  pl.BlockSpec((BLOCK, 2 * D_KV), lambda i: (jnp.maximum(i * per_step - 1, 0), kv_col)),
                  pl.BlockSpec((16, D_CONV), lambda i: (jnp.maximum(i * (rows // 16) - 1, 0), COL_CG // D_CONV)),
                  pl.BlockSpec((16, D_CONV), lambda i: (jnp.maximum(i * (rows // 16) - 1, 0), COL_U // D_CONV)),
                  pl.BlockSpec((N_HEADS, rows, HEAD_DIM), lambda i: (0, i, 0)),
                  pl.BlockSpec((m, 2 * D_MEMQ), lambda i: (0, 0)),
                  pl.BlockSpec((3, D_CONV), lambda i: (0, 0)),
                  pl.BlockSpec(bias_block, lambda i: (BIAS_CUR, 0, 0, 0)),
                  pl.BlockSpec(bias_block, lambda i: (jnp.where(i == 0, BIAS_NONE, BIAS_PREV), 0, 0, 0)),
                  pl.BlockSpec(bias_block, lambda i: (BIAS_PREV, 0, 0, 0))],
        out_specs=[pl.BlockSpec((rows, D_MIX), lambda i: (i, 0)),
                   pl.BlockSpec((rows, 128), lambda i: (i, 0))],
        out_shape=[jax.ShapeDtypeStruct((t, D_MIX), F32), jax.ShapeDtypeStruct((t, 128), F32)],
        compiler_params=_params(("parallel",)),
    )(sinks, p, p, p, p, qh, mkv, convw, bias_key, bias_key, bias_key)


def _mix_core_bwd(p, qh, dyconv, doh, delta, lse, mkv, convw, sinks, bias_key):
    t = p.shape[0]
    m = mkv.shape[0]
    nb = t // BLOCK

    def body(sk_ref, pc_ref, pkv_ref, ppc_ref, ppu_ref, pnb_ref, dyc_ref, dyn_ref, qc_ref, qn_ref, doc_ref, don_ref,
             dlc_ref, dln_ref, lc_ref, ln_ref, mkv_ref, cw_ref, bp_ref, bct_ref, bnt_ref,
             dp_ref, dmkv_ref, dcw_ref, dsk_ref):
        i = pl.program_id(0)
        prevf = (i > 0).astype(F32)
        nextf = (i < nb - 1).astype(F32)
        row = lax.broadcasted_iota(jnp.int32, (BLOCK, D_CONV), 0)

        @pl.when(i == 0)
        def _():
            dmkv_ref[...] = jnp.zeros_like(dmkv_ref)
            dcw_ref[...] = jnp.zeros_like(dcw_ref)
            dsk_ref[...] = jnp.zeros_like(dsk_ref)

        bg = pc_ref[:, COL_BG:COL_BG + D_CONV].astype(F32)
        cg = pc_ref[:, COL_CG:COL_CG + D_CONV].astype(F32)
        u = pc_ref[:, COL_U:COL_U + D_CONV].astype(F32)
        vv = cg * u
        pvv = ppc_ref[...].astype(F32) * ppu_ref[...].astype(F32) * prevf
        vv1 = _shift_rows(vv, 1, [pvv[15:16]], row)
        vv2 = _shift_rows(vv, 2, [pvv[14:15], pvv[15:16]], row)
        w = cw_ref[...]
        yconv = w[0:1] * vv2 + w[1:2] * vv1 + w[2:3] * vv
        dyo = dyc_ref[...]
        dyc = dyo * bg
        nxt = dyn_ref[...] * pnb_ref[...].astype(F32) * nextf
        d1 = _shift_rows_up(dyc, 1, [nxt[0:1]], row)
        d2 = _shift_rows_up(dyc, 2, [nxt[0:1], nxt[1:2]], row)
        dvv = w[2:3] * dyc + w[1:2] * d1 + w[0:1] * d2
        dp_ref[:, COL_BG:COL_BG + D_CONV] = (dyo * yconv).astype(BF16)
        dp_ref[:, COL_CG:COL_CG + D_CONV] = (dvv * u).astype(BF16)
        dp_ref[:, COL_U:COL_U + D_CONV] = (dvv * cg).astype(BF16)
        dcw_ref[0:1, :] += jnp.sum(dyc * vv2, axis=0, keepdims=True)
        dcw_ref[1:2, :] += jnp.sum(dyc * vv1, axis=0, keepdims=True)
        dcw_ref[2:3, :] += jnp.sum(dyc * vv, axis=0, keepdims=True)

        lse_t, dl_t = lc_ref[...].T, dlc_ref[...].T
        lse_nt, dl_nt = ln_ref[...].T, dln_ref[...].T

        def stack_rows(tile_t, heads):
            return jnp.concatenate([tile_t[hd:hd + 1, :] for hd in heads], axis=1)

        lane8 = jnp.where(lax.broadcasted_iota(jnp.int32, (8, 128), 0) == 0,
                          lax.broadcasted_iota(jnp.int32, (8, 128), 1), -1)
        dsk = jnp.zeros((8, 128), F32)
        for kv in range(N_SWA_KV):
            heads = range(kv * SWA_GROUP, (kv + 1) * SWA_GROUP)
            kc = pc_ref[:, COL_K + HEAD_DIM * kv:COL_K + HEAD_DIM * (kv + 1)]
            vc = pc_ref[:, COL_V + HEAD_DIM * kv:COL_V + HEAD_DIM * (kv + 1)]
            kp = pkv_ref[:, HEAD_DIM * kv:HEAD_DIM * (kv + 1)]
            vp = pkv_ref[:, D_KV + HEAD_DIM * kv:D_KV + HEAD_DIM * (kv + 1)]
            qg = qc_ref[kv * SWA_GROUP:(kv + 1) * SWA_GROUP].reshape(GROUP_ROWS, HEAD_DIM)
            dog = doc_ref[kv * SWA_GROUP:(kv + 1) * SWA_GROUP].reshape(GROUP_ROWS, HEAD_DIM)
            qn = qn_ref[kv * SWA_GROUP:(kv + 1) * SWA_GROUP].reshape(GROUP_ROWS, HEAD_DIM)
            don = don_ref[kv * SWA_GROUP:(kv + 1) * SWA_GROUP].reshape(GROUP_ROWS, HEAD_DIM)
            lse_row, dl_row = stack_rows(lse_t, heads), stack_rows(dl_t, heads)
            ptp = jnp.exp(_dot_nt(kp, qg) + bp_ref[0, kv] - lse_row)
            dstp = (ptp * (_dot_nt(vp, dog) - dl_row)).astype(BF16)
            dq = _dot_tn(dstp, kp)
            pt = jnp.exp(_dot_nt(kc, qg) + bct_ref[0, kv] - lse_row)
            dst = (pt * (_dot_nt(vc, dog) - dl_row)).astype(BF16)
            dv = _dot(pt.astype(BF16), dog)
            dk = _dot(dst, qg)
            dq = dq + _dot_tn(dst, kc)
            ptn = jnp.exp(_dot_nt(kc, qn) + bnt_ref[0, kv] - stack_rows(lse_nt, heads))
            dstn = (ptn * (_dot_nt(vc, don) - stack_rows(dl_nt, heads))).astype(BF16)
            dv = dv + _dot(ptn.astype(BF16), don)
            dk = dk + _dot(dstn, qn)
            dp_ref[:, COL_K + HEAD_DIM * kv:COL_K + HEAD_DIM * (kv + 1)] = dk.astype(BF16)
            dp_ref[:, COL_V + HEAD_DIM * kv:COL_V + HEAD_DIM * (kv + 1)] = dv.astype(BF16)
            sink = jnp.concatenate([jnp.full((1, BLOCK), sk_ref[0, hd], F32) for hd in heads], axis=1)
            sink_term = jnp.exp(sink - lse_row) * dl_row
            for gi, hd in enumerate(heads):
                span = slice(gi * BLOCK, (gi + 1) * BLOCK)
                dp_ref[:, _q_col(hd):_q_col(hd) + HEAD_DIM] = (dq[span] * SCALE).astype(BF16)
                dsk = dsk + jnp.where(lane8 == hd, -jnp.sum(sink_term[:, span], axis=1, keepdims=True), 0.0)
        dsk_ref[...] += dsk

        for hm in range(N_MEM_HEADS):
            hd = N_SWA_HEADS + hm
            qm, dom = qc_ref[hd], doc_ref[hd]
            mk = mkv_ref[:, HEAD_DIM * hm:HEAD_DIM * (hm + 1)]
            mv = mkv_ref[:, D_MEMQ + HEAD_DIM * hm:D_MEMQ + HEAD_DIM * (hm + 1)]
            pt = jnp.exp(_dot_nt(mk, qm) - lse_t[hd:hd + 1, :])
            dst = (pt * (_dot_nt(mv, dom) - dl_t[hd:hd + 1, :])).astype(BF16)
            dp_ref[:, _q_col(hd):_q_col(hd) + HEAD_DIM] = (_dot_tn(dst, mk) * SCALE).astype(BF16)
            dmkv_ref[:, HEAD_DIM * hm:HEAD_DIM * (hm + 1)] += _dot(dst, qm)
            dmkv_ref[:, D_MEMQ + HEAD_DIM * hm:D_MEMQ + HEAD_DIM * (hm + 1)] += _dot(pt.astype(BF16), dom)

    cur = lambda i: (i, 0)
    const = lambda i: (0, 0)
    rows16 = BLOCK // 16
    last16 = t // 16 - 1
    before = lambda col: (lambda i: (jnp.maximum(i * rows16 - 1, 0), col))
    after = lambda i: (jnp.minimum((i + 1) * rows16, last16), 0)
    heads_cur = lambda i: (0, i, 0)
    heads_next = lambda i: (0, jnp.minimum(i + 1, nb - 1), 0)
    stat_next = lambda i: (jnp.minimum(i + 1, nb - 1), 0)
    key_block = (1, N_SWA_KV, BLOCK, GROUP_ROWS)
    head_block = (N_HEADS, BLOCK, HEAD_DIM)
    return _pcall(
        body, name="mix_core_bwd", grid=(nb,),
        in_specs=[pl.BlockSpec(memory_space=pltpu.SMEM),
                  pl.BlockSpec((BLOCK, D_IN), cur),
                  pl.BlockSpec((BLOCK, 2 * D_KV), lambda i: (jnp.maximum(i - 1, 0), COL_K // (2 * D_KV))),
                  pl.BlockSpec((16, D_CONV), before(COL_CG // D_CONV)),
                  pl.BlockSpec((16, D_CONV), before(COL_U // D_CONV)),
                  pl.BlockSpec((16, D_CONV), after),
                  pl.BlockSpec((BLOCK, D_CONV), cur),
                  pl.BlockSpec((16, D_CONV), after),
                  pl.BlockSpec(head_block, heads_cur), pl.BlockSpec(head_block, heads_next),
                  pl.BlockSpec(head_block, heads_cur), pl.BlockSpec(head_block, heads_next),
                  pl.BlockSpec((BLOCK, 128), cur), pl.BlockSpec((BLOCK, 128), stat_next),
                  pl.BlockSpec((BLOCK, 128), cur), pl.BlockSpec((BLOCK, 128), stat_next),
                  pl.BlockSpec((m, 2 * D_MEMQ), const),
                  pl.BlockSpec((3, D_CONV), const),
                  pl.BlockSpec(key_block, lambda i: (jnp.where(i == 0, BIAS_NONE, BIAS_PREV), 0, 0, 0)),
                  pl.BlockSpec(key_block, lambda i: (BIAS_CUR, 0, 0, 0)),
                  pl.BlockSpec(key_block, lambda i: (jnp.where(i == nb - 1, BIAS_NONE, BIAS_PREV), 0, 0, 0))],
        out_specs=[pl.BlockSpec((BLOCK, D_IN), cur),
                   pl.BlockSpec((m, 2 * D_MEMQ), const),
                   pl.BlockSpec((8, D_CONV), const),
                   pl.BlockSpec((8, 128), const)],
        out_shape=[jax.ShapeDtypeStruct((t, D_IN), BF16),
                   jax.ShapeDtypeStruct((m, 2 * D_MEMQ), F32),
                   jax.ShapeDtypeStruct((8, D_CONV), F32),
                   jax.ShapeDtypeStruct((8, 128), F32)],
        compiler_params=_params(("arbitrary",)),
    )(sinks, p, p, p, p, p, dyconv, dyconv, qh, qh, doh, doh, delta, delta, lse, lse, mkv, convw,
      bias_key, bias_key, bias_key)


def _group_norms(y):
    out = []
    for a, b in MIX_GROUPS:
        ys = y[:, a:b]
        r = _rstd(ys)
        out.append((ys * r, r))
    return out


def _mix_out_fwd(y, h, g, wout):
    t, d = h.shape
    tm = _tok_block(t)

    def body(y_ref, h_ref, g_ref, w_ref, ho_ref, mt_ref):
        yhat = jnp.concatenate([yh for yh, _ in _group_norms(y_ref[...])], axis=-1)
        mixed = yhat * g_ref[...]
        mt_ref[...] = mixed.T.astype(BF16)
        ho_ref[...] = h_ref[...] + _dot(mixed.astype(BF16), w_ref[...])

    return _pcall(
        body, name="mix_out_fwd", grid=(t // tm,),
        in_specs=[pl.BlockSpec((tm, D_MIX), lambda i: (i, 0)),
                  pl.BlockSpec((tm, d), lambda i: (i, 0)),
                  pl.BlockSpec((1, D_MIX), lambda i: (0, 0)),
                  pl.BlockSpec((D_MIX, d), lambda i: (0, 0))],
        out_specs=[pl.BlockSpec((tm, d), lambda i: (i, 0)),
                   pl.BlockSpec((D_MIX, tm), lambda i: (0, i))],
        out_shape=[jax.ShapeDtypeStruct((t, d), F32), jax.ShapeDtypeStruct((D_MIX, t), BF16)],
        compiler_params=_params(("parallel",)),
    )(y, h, g, wout)


def _head_indicator():
    ind = np.zeros((D_MIX, 128), np.float32)
    for hd in range(N_HEADS):
        ind[_head_cols(hd):_head_cols(hd) + HEAD_DIM, hd] = 1.0
    return jnp.asarray(ind, BF16)


def _mix_out_bwd(dho, y, g, wout, mt, dep):
    t, d = dho.shape
    tm = _tok_block(t)
    ni = t // tm

    def body(dho_ref, y_ref, g_ref, w_ref, mt_ref, ind_ref, dep_ref, dyc_ref, doh_ref, dl_ref, dw_ref, dg_ref, acc_ref):
        i = pl.program_id(0)
        dhb = dho_ref[...].astype(BF16)
        dm = _dot_nt(dhb, w_ref[...])
        pw = _dot(mt_ref[...], dhb)
        gg = g_ref[...]
        yy = y_ref[...]
        dys = []
        dgs = []
        for (a, b), (yhat, r) in zip(MIX_GROUPS, _group_norms(yy)):
            dmg = dm[:, a:b]
            dgs.append(_sum8(dmg * yhat))
            dyh = dmg * gg[:, a:b]
            dys.append(r * (dyh - yhat * jnp.mean(dyh * yhat, axis=-1, keepdims=True)))
        dy = jnp.concatenate(dys, axis=-1)
        dyc_ref[...] = dy[:, 0:D_CONV]
        for hd in range(N_HEADS):
            doh_ref[hd] = dy[:, _head_cols(hd):_head_cols(hd) + HEAD_DIM].astype(BF16)
        prod = dy * yy
        hi = prod.astype(BF16)
        lo = (prod - hi.astype(F32)).astype(BF16)
        dl_ref[...] = _dot(hi, ind_ref[...]) + _dot(lo, ind_ref[...])
        part = jnp.concatenate(dgs, axis=-1)

        @pl.when(i == 0)
        def _():
            acc_ref[...] = pw
            dg_ref[...] = part

        @pl.when(i > 0)
        def _():
            acc_ref[...] += pw
            dg_ref[...] += part

        @pl.when(i == ni - 1)
        def _():
            dw_ref[...] = acc_ref[...].astype(BF16)

    return _pcall(
        body, name="mix_out_bwd", grid=(ni,),
        in_specs=[pl.BlockSpec((tm, d), lambda i: (i, 0)),
                  pl.BlockSpec((tm, D_MIX), lambda i: (i, 0)),
                  pl.BlockSpec((1, D_MIX), lambda i: (0, 0)),
                  pl.BlockSpec((D_MIX, d), lambda i: (0, 0)),
                  pl.BlockSpec((D_MIX, tm), lambda i: (0, i)),
                  pl.BlockSpec((D_MIX, 128), lambda i: (0, 0)),
                  pl.BlockSpec(memory_space=pl.ANY)],
        out_specs=[pl.BlockSpec((tm, D_CONV), lambda i: (i, 0)),
                   pl.BlockSpec((N_HEADS, tm, HEAD_DIM), lambda i: (0, i, 0)),
                   pl.BlockSpec((tm, 128), lambda i: (i, 0)),
                   pl.BlockSpec((D_MIX, d), lambda i: (0, 0)),
                   pl.BlockSpec((8, D_MIX), lambda i: (0, 0))],
        out_shape=[jax.ShapeDtypeStruct((t, D_CONV), F32),
                   jax.ShapeDtypeStruct((N_HEADS, t, HEAD_DIM), BF16),
                   jax.ShapeDtypeStruct((t, 128), F32),
                   jax.ShapeDtypeStruct((D_MIX, d), BF16),
                   jax.ShapeDtypeStruct((8, D_MIX), F32)],
        scratch_shapes=[pltpu.VMEM((D_MIX, d), F32)],
        compiler_params=_params(("arbitrary",)),
    )(dho, y, g, wout, mt, _head_indicator(), dep)


def _mix_proj_bwd(dp, dho, h, g, win_t, n):
    t, d = h.shape
    tm = _tok_block(t)
    ni = t // tm

    def body(dp_ref, dho_ref, h_ref, g_ref, w_ref, n_ref, dh_ref, dw_ref, dg_ref, acc_ref):
        i = pl.program_id(0)
        dpb = dp_ref[...]
        dn = _dot(dpb, w_ref[...])

        @pl.when(i == 0)
        def _():
            acc_ref[...] = jnp.zeros_like(acc_ref)

        acc_ref[...] += _dot_tn(dpb, n_ref[...])
        hh = h_ref[...]
        r = _rstd(hh)
        xhat = hh * r
        dxh = dn * g_ref[...]
        dh_ref[...] = dho_ref[...] + r * (dxh - xhat * jnp.mean(dxh * xhat, axis=-1, keepdims=True))
        part = _sum8(dn * xhat)

        @pl.when(i == 0)
        def _():
            dg_ref[...] = part

        @pl.when(i > 0)
        def _():
            dg_ref[...] += part

        @pl.when(i == ni - 1)
        def _():
            dw_ref[...] = acc_ref[...].astype(BF16)

    return _pcall(
        body, name="mix_proj_bwd", grid=(ni,),
        in_specs=[pl.BlockSpec((tm, D_IN), lambda i: (i, 0)),
                  pl.BlockSpec((tm, d), lambda i: (i, 0)),
                  pl.BlockSpec((tm, d), lambda i: (i, 0)),
                  pl.BlockSpec((1, d), lambda i: (0, 0)),
                  pl.BlockSpec((D_IN, d), lambda i: (0, 0)),
                  pl.BlockSpec((tm, d), lambda i: (i, 0))],
        out_specs=[pl.BlockSpec((tm, d), lambda i: (i, 0)),
                   pl.BlockSpec((D_IN, d), lambda i: (0, 0)),
                   pl.BlockSpec((8, d), lambda i: (0, 0))],
        out_shape=[jax.ShapeDtypeStruct((t, d), F32),
                   jax.ShapeDtypeStruct((D_IN, d), BF16),
                   jax.ShapeDtypeStruct((8, d), F32)],
        scratch_shapes=[pltpu.VMEM((D_IN, d), F32)],
        compiler_params=_params(("arbitrary",)),
    )(dp, dho, h, g, win_t, n)


def _final_loss(h, g, tgt):
    t, d = h.shape
    tm = _tok_block(t)

    def body(h_ref, g_ref, t_ref, dh_ref, ls_ref, dg_ref):
        i = pl.program_id(0)
        hh = h_ref[...]
        r = _rstd(hh)
        xhat = hh * r
        gg = g_ref[...]
        err = xhat * gg - t_ref[...]
        dy = err * (1.0 / d)
        dxh = dy * gg
        dh_ref[...] = r * (dxh - xhat * jnp.mean(dxh * xhat, axis=-1, keepdims=True))
        lpart = _sum8(err * err)
        gpart = _sum8(dy * xhat)

        @pl.when(i == 0)
        def _():
            ls_ref[...] = lpart
            dg_ref[...] = gpart

        @pl.when(i > 0)
        def _():
            ls_ref[...] += lpart
            dg_ref[...] += gpart

    return _pcall(
        body, name="final_loss", grid=(t // tm,),
        in_specs=[pl.BlockSpec((tm, d), lambda i: (i, 0)),
                  pl.BlockSpec((1, d), lambda i: (0, 0)),
                  pl.BlockSpec((tm, d), lambda i: (i, 0))],
        out_specs=[pl.BlockSpec((tm, d), lambda i: (i, 0)),
                   pl.BlockSpec((8, d), lambda i: (0, 0)),
                   pl.BlockSpec((8, d), lambda i: (0, 0))],
        out_shape=[jax.ShapeDtypeStruct((t, d), F32),
                   jax.ShapeDtypeStruct((8, d), F32),
                   jax.ShapeDtypeStruct((8, d), F32)],
        compiler_params=_params(("arbitrary",)),
    )(h, g, tgt)


def _position():
    return lax.axis_index("x"), lax.axis_index("y"), lax.axis_index("c")


def _flip(v, bit):
    return 1 - v if bit else v


def _peer(k):
    x, y, c = _position()
    return _flip(x, k & 4), _flip(y, k & 2), _flip(c, k & 1)


def _slot(px, py, pc):
    return 4 * px + 2 * py + pc


def _handshake(peers):
    barrier = pltpu.get_barrier_semaphore()
    for peer in peers:
        pl.semaphore_signal(barrier, inc=1, device_id=peer, device_id_type=MESH)
    pl.semaphore_wait(barrier, len(peers))


def _sequencer_call(body, name, collective_id, out_type, scratch_types, operands):
    return pl.kernel(
        body, out_type=out_type, mesh=plsc.ScalarSubcoreMesh(axis_name="sequencer", num_cores=1), name=name,
        scratch_types=scratch_types, compiler_params=pltpu.CompilerParams(collective_id=collective_id),
    )(*operands)


def _all_gather(shards, name, collective_id):
    nt = len(shards)

    def body(*refs):
        xs = refs[:nt]
        outs = refs[nt:2 * nt]
        send_sems, recv_sems, local_sems = refs[2 * nt:]
        x, y, c = _position()
        me, sibling = (x, y, c), (x, y, 1 - c)
        xn, yn, dg = (1 - x, y), (x, 1 - y), (1 - x, 1 - y)
        pick = lambda a, b: (jnp.where(c == 0, a[0], b[0]), jnp.where(c == 0, a[1], b[1]))
        relay_from, relay_to = pick(yn, xn), pick(xn, yn)
        _handshake([sibling, (*xn, c), (*yn, c)])

        def copy(t, k, block, to, src=None):
            dst = outs[t].at[_slot(*block)]
            return pltpu.make_async_remote_copy(
                src_ref=dst if src is None else src, dst_ref=dst,
                send_sem=send_sems.at[t, k], recv_sem=recv_sems.at[t, k],
                device_id=to, device_id_type=MESH)

        mine = [pltpu.make_async_copy(xs[t], outs[t].at[_slot(*me)], local_sems.at[t]) for t in range(nt)]
        for cp in mine:
            cp.start()
        sent = []
        for t in range(nt):
            sent += [copy(t, 0, me, sibling, src=xs[t]), copy(t, 1, me, (*xn, c), src=xs[t]),
                     copy(t, 2, me, (*yn, c), src=xs[t])]
        for cp in sent:
            cp.start()
        for t in range(nt):
            copy(t, 1, (*xn, c), me).wait_recv()
            copy(t, 2, (*yn, c), me).wait_recv()
            passed = [copy(t, 3, (*relay_from, c), (*relay_to, c)),
                      copy(t, 4, (*xn, c), sibling), copy(t, 5, (*yn, c), sibling)]
            for cp in passed:
                cp.start()
            sent += passed
        for t in range(nt):
            copy(t, 3, (*dg, c), me).wait_recv()
            fwd = copy(t, 6, (*dg, c), sibling)
            fwd.start()
            sent.append(fwd)
        for t in range(nt):
            copy(t, 0, sibling, me).wait_recv()
            for k, chip in ((4, xn), (5, yn), (6, dg)):
                copy(t, k, (*chip, 1 - c), me).wait_recv()
        for cp in sent:
            cp.wait_send()
        for cp in mine:
            cp.wait()

    return _sequencer_call(
        body, name, collective_id,
        out_type=[jax.ShapeDtypeStruct((N_DEV,) + s.shape, s.dtype) for s in shards],
        scratch_types=[pltpu.SemaphoreType.DMA((nt, 7)), pltpu.SemaphoreType.DMA((nt, 7)),
                       pltpu.SemaphoreType.DMA((nt,))],
        operands=shards)


def _scatter_copy(srcs, lands, send_sems, recv_sems, t, k):
    peer = _peer(k)
    return pltpu.make_async_remote_copy(
        src_ref=srcs[t].at[_slot(*peer)], dst_ref=lands[t].at[k],
        send_sem=send_sems.at[t * (N_DEV - 1) + k - 1], recv_sem=recv_sems.at[t * (N_DEV - 1) + k - 1],
        device_id=peer, device_id_type=MESH)


def _scatter_start(partials, name):
    nt = len(partials)

    def body(*refs):
        srcs, lands = refs[:nt], refs[nt:2 * nt]
        send_sems, recv_sems = refs[2 * nt], refs[2 * nt + 1]
        token = refs[-1]
        for k in range(1, N_DEV):
            for t in range(nt):
                _scatter_copy(srcs, lands, send_sems, recv_sems, t, k).start()
        token[...] = jnp.zeros_like(token)

    hbm = pl.BlockSpec(memory_space=pltpu.HBM)
    sem = pl.BlockSpec(memory_space=pltpu.SEMAPHORE)
    shapes = [pltpu.HBM(p.shape, p.dtype) for p in partials]
    lands = [pltpu.with_memory_space_constraint(lax.empty(p.shape, p.dtype), pltpu.HBM) for p in partials]
    srcs = [pltpu.with_memory_space_constraint(p, pltpu.HBM) for p in partials]
    out = _pcall(
        body, name=name,
        out_shape=[pltpu.SemaphoreType.DMA((nt * (N_DEV - 1),))] * 2 + shapes + shapes
        + [jax.ShapeDtypeStruct((8, 128), F32)],
        in_specs=[hbm] * (2 * nt),
        out_specs=[sem, sem] + [hbm] * (2 * nt) + [pl.BlockSpec(memory_space=pltpu.VMEM)],
        input_output_aliases={i: 2 + i for i in range(2 * nt)},
        compiler_params=pltpu.CompilerParams(has_side_effects=pltpu.SideEffectType.DATAFLOW_SIDE_EFFECTING),
    )(*srcs, *lands)
    return (nt, name, out[:-1]), out[-1]


def _scatter_wait(state, after):
    nt, name, (send_sems, recv_sems, *thru) = state

    def body(*refs):
        srcs, lands = refs[:nt], refs[nt:2 * nt]
        send_sems, recv_sems = refs[2 * nt], refs[2 * nt + 1]
        for k in range(1, N_DEV):
            for t in range(nt):
                copy = _scatter_copy(srcs, lands, send_sems, recv_sems, t, k)
                copy.wait_send()
                copy.wait_recv()

    hbm = pl.BlockSpec(memory_space=pltpu.HBM)
    sem = pl.BlockSpec(memory_space=pltpu.SEMAPHORE)
    out = _pcall(
        body, name=name + "_wait",
        out_shape=[pltpu.HBM(a.shape, a.dtype) for a in thru],
        in_specs=[hbm] * (2 * nt) + [sem, sem, pl.BlockSpec(memory_space=pl.ANY)],
        out_specs=[hbm] * (2 * nt),
        input_output_aliases={i: i for i in range(2 * nt)},
        compiler_params=pltpu.CompilerParams(has_side_effects=pltpu.SideEffectType.DATAFLOW_SIDE_EFFECTING),
    )(*thru, send_sems, recv_sems, after)
    return out[:nt], out[nt:]


def _all_reduce_rows(v, dep):
    nv, _, w = v.shape

    def body(v_ref, dep_ref, out_ref, mine_ref, gath_ref, send_sems, recv_sems):
        x, y, c = _position()
        me = _slot(x, y, c)
        mine_ref[...] = jnp.sum(v_ref[...], axis=1)

        def copy(k):
            return pltpu.make_async_remote_copy(
                src_ref=mine_ref, dst_ref=gath_ref.at[me],
                send_sem=send_sems.at[k - 1], recv_sem=recv_sems.at[k - 1],
                device_id=_peer(k), device_id_type=MESH)

        def arrival(k):
            return pltpu.make_async_remote_copy(
                src_ref=mine_ref, dst_ref=gath_ref.at[_slot(*_peer(k))],
                send_sem=send_sems.at[k - 1], recv_sem=recv_sems.at[k - 1],
                device_id=_peer(k), device_id_type=MESH)

        sent = [copy(k) for k in range(1, N_DEV)]
        for cp in sent:
            cp.start()
        gath_ref[me] = mine_ref[...]
        for k in range(1, N_DEV):
            arrival(k).wait_recv()
        for cp in sent:
            cp.wait_send()
        total = gath_ref[0]
        for s in range(1, N_DEV):
            total = total + gath_ref[s]
        out_ref[...] = total

    vmem = pl.BlockSpec(memory_space=pltpu.VMEM)
    return _pcall(
        body, name="all_reduce_rows",
        in_specs=[vmem, pl.BlockSpec(memory_space=pl.ANY)], out_specs=vmem,
        out_shape=jax.ShapeDtypeStruct((nv, w), F32),
        scratch_shapes=[pltpu.VMEM((nv, w), F32), pltpu.VMEM((N_DEV, nv, w), F32),
                        pltpu.SemaphoreType.DMA((7,)), pltpu.SemaphoreType.DMA((7,))],
    )(v, dep)


def _adamw_math(w, g, m, v):
    m2 = ADAM_B1 * m + (1.0 - ADAM_B1) * g
    v2 = ADAM_B2 * v + (1.0 - ADAM_B2) * (g * g)
    m_hat = m2 / (1.0 - ADAM_B1 ** ADAM_STEP)
    v_hat = v2 / (1.0 - ADAM_B2 ** ADAM_STEP)
    delta = -ADAM_LR * (m_hat / (jnp.sqrt(v_hat) + ADAM_EPS) + ADAM_WD * w)
    return delta, m2, v2


def _row_block(r):
    for cand in (256, 176, 128):
        if r % cand == 0:
            return cand
    return r


def _adamw_sharded(me, grads, w, m, v, dep, first_layer=0, prev=None):
    nl = len(grads)
    _, r, c = grads[0][1].shape
    tr = _row_block(r)
    nr = r // tr
    prev = list(prev or ())

    def body(me_ref, *refs):
        grad_refs = refs[:2 * nl]
        w_ref, m_ref, v_ref = refs[2 * nl:2 * nl + 3]
        g_ref, d_ref, m2_ref, v2_ref = refs[-4:]
        layer = pl.program_id(0)

        def total(own_ref, land_ref):
            acc = own_ref[0].astype(F32)
            for k in range(1, N_DEV):
                acc = acc + land_ref[k].astype(F32)
            return acc

        g = total(grad_refs[0], grad_refs[1])
        for k in range(1, nl):
            g = jnp.where(layer == k, total(grad_refs[2 * k], grad_refs[2 * k + 1]), g)
        delta, m2, v2 = _adamw_math(w_ref[0], g, m_ref[0], v_ref[0])
        g_ref[0] = g
        d_ref[0] = delta
        m2_ref[0] = m2
        v2_ref[0] = v2

    def grad_pair_specs(k):
        def rows(l, i):
            return jnp.where(l == k, i, jnp.where(l < k, 0, nr - 1))
        return [pl.BlockSpec((1, tr, c), lambda l, i, me_ref: (me_ref[0], rows(l, i), 0)),
                pl.BlockSpec((N_DEV, tr, c), lambda l, i, me_ref: (0, rows(l, i), 0))]

    grad_specs = [spec for k in range(nl) for spec in grad_pair_specs(k)]
    shard = pl.BlockSpec((1, tr, c), lambda l, i, me_ref: (first_layer + l, i, 0))
    untouched = pl.BlockSpec(memory_space=pl.ANY)
    out = jax.ShapeDtypeStruct(w.shape, F32)
    first_prev = 1 + 2 * nl + 4
    return _pcall(
        body, name="adamw_sharded",
        grid_spec=pltpu.PrefetchScalarGridSpec(
            num_scalar_prefetch=1, grid=(nl, nr),
            in_specs=grad_specs + [shard, shard, shard] + [untouched] * (1 + len(prev)),
            out_specs=[shard, shard, shard, shard]),
        out_shape=[out, out, out, out],
        input_output_aliases={first_prev + k: k for k in range(len(prev))},
        compiler_params=_params(("arbitrary", "arbitrary")),
    )(me, *[a for pair in grads for a in pair], w, m, v, dep, *prev)


def _adamw_small(w, g, m, v):
    def body(w_ref, g_ref, m_ref, v_ref, d_ref, m2_ref, v2_ref):
        delta, m2, v2 = _adamw_math(w_ref[...], g_ref[...], m_ref[...], v_ref[...])
        d_ref[...] = delta
        m2_ref[...] = m2
        v2_ref[...] = v2

    spec = pl.BlockSpec(w.shape, lambda i: (0, 0))
    out = jax.ShapeDtypeStruct(w.shape, F32)
    return _pcall(
        body, name="adamw_small", grid=(1,),
        in_specs=[spec] * 4, out_specs=[spec] * 3, out_shape=[out] * 3,
        compiler_params=_params(("arbitrary",)),
    )(w, g, m, v)


def _pack(arrs):
    flat = jnp.concatenate([a.reshape(-1) for a in arrs])
    n = flat.shape[0]
    rows = -(-n // 1024) * 8
    return jnp.pad(flat, (0, rows * 128 - n)).reshape(rows, 128)


def _unpack(packed, like):
    flat = packed.reshape(-1)
    out, off = [], 0
    for a in like:
        out.append(flat[off:off + a.size].reshape(a.shape))
        off += a.size
    return out


def kernel(x, mem, g_ffn1, w_ffn1_up, w_ffn1_down, g_mix, w_in, conv_w, sinks, g_mem, w_mem_kv, g_grp, w_out, g_ffn2, w_ffn2_up, w_ffn2_down, g_final, loss_target, m_g_ffn1, m_w_ffn1_up, m_w_ffn1_down, m_g_mix, m_w_in, m_conv_w, m_sinks, m_g_mem, m_w_mem_kv, m_g_grp, m_w_out, m_g_ffn2, m_w_ffn2_up, m_w_ffn2_down, m_g_final, v_g_ffn1, v_w_ffn1_up, v_w_ffn1_down, v_g_mix, v_w_in, v_conv_w, v_sinks, v_g_mem, v_w_mem_kv, v_g_grp, v_w_out, v_g_ffn2, v_w_ffn2_up, v_w_ffn2_down, v_g_final):
    depth = g_ffn1.shape[0]
    t, d = x.shape[1], x.shape[2]
    width = max(d, D_MIX)
    me = _slot(*_position())
    conv_shard = conv_w.shape[2]

    xin, memin, tgt = x[0], mem[0], loss_target[0]

    conv_tile = jnp.zeros((depth * 8, 128), F32).at[:, :conv_shard].set(
        jnp.pad(conv_w, ((0, 0), (0, 8 - conv_w.shape[1]), (0, 0))).reshape(depth * 8, conv_shard))
    tr = lambda a: jnp.swapaxes(a, -1, -2)
    bf = lambda a: a.astype(BF16)
    weights = []
    collective_id = 0
    for l in range(depth):
        groups = [[bf(tr(w_ffn1_up[l])), bf(w_ffn1_down[l])] + ([conv_tile] if l == 0 else []),
                  [bf(tr(w_in[l])), bf(w_mem_kv[l]), bf(w_out[l])],
                  [bf(tr(w_ffn2_up[l])), bf(w_ffn2_down[l])]]
        full = []
        for gi, shards in enumerate(groups):
            full.append(_all_gather(shards, f"all_gather_l{l}_g{gi}", collective_id))
            collective_id += 1
        if l == 0:
            conv_full = full[0][2].reshape(N_DEV, depth, 8, 128)[:, :, :3, :conv_shard]
            conv_full = conv_full.transpose(1, 2, 0, 3).reshape(depth, 3, N_DEV * conv_shard)
        weights.append(dict(
            up1=full[0][0].reshape(2, -1, d), dn1=full[0][1].reshape(-1, d),
            win=full[1][0].reshape(D_IN, d), wkv=full[1][1].reshape(d, 2 * D_MEMQ), wout=full[1][2].reshape(D_MIX, d),
            up2=full[2][0].reshape(2, -1, d), dn2=full[2][1].reshape(-1, d)))

    row = lambda a: a.reshape(1, -1)
    bias_key = _bias_table()

    h = xin
    saved = []
    for l in range(depth):
        wl = weights[l]
        s = dict(h0=h)
        h, s["gu1"], s["n1"] = _ffn_fwd(h, row(g_ffn1[l]), wl["up1"], wl["dn1"])
        s["h1"] = h
        s["p"], s["n_mix"], s["qh"] = _mix_proj_fwd(h, row(g_mix[l]), wl["win"])
        s["mkv"], s["nt_mem"] = _memkv_fwd(memin, row(g_mem[l]), wl["wkv"], s["p"])
        s["y"], s["lse"] = _mix_core_fwd(s["p"], s["qh"], s["mkv"], conv_full[l], row(sinks[l]), bias_key)
        h, s["mt"] = _mix_out_fwd(s["y"], h, row(g_grp[l]), wl["wout"])
        s["h2"] = h
        h, s["gu2"], s["n2"] = _ffn_fwd(h, row(g_ffn2[l]), wl["up2"], wl["dn2"])
        saved.append(s)

    dh, loss_part, dg_final = _final_loss(h, row(g_final), tgt)

    small = {}
    dep = loss_part

    def reduce_small(after):
        def lanes(a):
            return jnp.pad(a, ((0, 0), (0, width - a.shape[1])))

        def first_row(a):
            return lanes(jnp.pad(a, ((0, 8 - a.shape[0]), (0, 0))))

        vec_names = ["g_ffn1", "g_mix", "g_mem", "g_grp", "g_ffn2", "sinks"]
        tiles = [lanes(small[n, l]) for n in vec_names for l in range(depth)]
        tiles += [first_row(small["conv_w", l][k:k + 1]) for l in range(depth) for k in range(3)]
        tiles.append(lanes(dg_final))
        n_real = len(tiles)
        tiles.append(lanes(loss_part))
        tiles += [jnp.zeros((8, width), F32)] * (-len(tiles) % 8)
        summed = _all_reduce_rows(jnp.stack(tiles), after)
        loss_all = 0.5 * jnp.sum(summed[n_real]) / d

        def vec(n, wd):
            return jnp.stack([summed[vec_names.index(n) * depth + l, :wd] for l in range(depth)])

        conv_base = len(vec_names) * depth
        conv_grad = jnp.stack([jnp.stack([summed[conv_base + 3 * l + k, :D_CONV] for k in range(3)])
                               for l in range(depth)])
        grads_small = {
            "g_ffn1": vec("g_ffn1", d), "g_mix": vec("g_mix", d), "g_mem": vec("g_mem", d),
            "g_grp": vec("g_grp", D_MIX), "g_ffn2": vec("g_ffn2", d), "sinks": vec("sinks", N_SWA_HEADS),
            "conv_w": lax.dynamic_slice_in_dim(conv_grad, me * conv_shard, conv_shard, axis=2),
            "g_final": summed[n_real - 1, :d],
        }
        small_w = [("g_ffn1", g_ffn1, m_g_ffn1, v_g_ffn1), ("g_mix", g_mix, m_g_mix, v_g_mix),
                   ("conv_w", conv_w, m_conv_w, v_conv_w), ("sinks", sinks, m_sinks, v_sinks),
                   ("g_mem", g_mem, m_g_mem, v_g_mem), ("g_grp", g_grp, m_g_grp, v_g_grp),
                   ("g_ffn2", g_ffn2, m_g_ffn2, v_g_ffn2), ("g_final", g_final, m_g_final, v_g_final)]
        like = [w for _, w, _, _ in small_w]
        packed = _adamw_small(_pack(like), _pack([grads_small[n] for n, _, _, _ in small_w]),
                              _pack([m for _, _, m, _ in small_w]), _pack([v for _, _, _, v in small_w]))
        updated = {n: (grads_small[n], dl, m2, v2)
                   for (n, _, _, _), dl, m2, v2 in zip(small_w, *[_unpack(pk, like) for pk in packed])}
        return loss_all, updated, packed[0]

    started = []

    def scatter(names, partials, label):
        state, token = _scatter_start(partials, f"scatter_grads_{label}")
        started.append((names, state))
        return token

    for l in reversed(range(depth)):
        wl, s = weights[l], saved[l]
        dh, agu, dyb, small["g_ffn2", l] = _ffn_bwd_act(dh, s["h2"], row(g_ffn2[l]), s["gu2"], wl["up2"], wl["dn2"], dep)
        ddn2 = _ffn_bwd_w(agu, 2, 1, dyb, agu, f"ffn_bwd_w_down_l{l}_ffn2").reshape(N_DEV, -1, d)
        dup2 = _ffn_bwd_w(agu, 0, 2, s["n2"], ddn2, f"ffn_bwd_w_up_l{l}_ffn2").reshape(N_DEV, -1, d)
        dep = scatter([("w_ffn2_up", l), ("w_ffn2_down", l)], [dup2, ddn2], f"l{l}_ffn2")
        dyconv, doh, delta, dwout, small["g_grp", l] = _mix_out_bwd(dh, s["y"], row(g_grp[l]), wl["wout"], s["mt"], dep)
        dp, dmkv, small["conv_w", l], small["sinks", l] = _mix_core_bwd(
            s["p"], s["qh"], dyconv, doh, delta, s["lse"], s["mkv"], conv_full[l], row(sinks[l]), bias_key)
        dwkv, small["g_mem", l] = _memkv_bwd(dmkv, memin, row(g_mem[l]), wl["wkv"], s["nt_mem"])
        dh, dwin, small["g_mix", l] = _mix_proj_bwd(dp, dh, s["h1"], row(g_mix[l]), wl["win"], s["n_mix"])
        dep = scatter([("w_in", l), ("w_mem_kv", l), ("w_out", l)],
                      [dwin.reshape(N_DEV, -1, d), dwkv.reshape(N_DEV, -1, 2 * D_MEMQ), dwout.reshape(N_DEV, -1, d)],
                      f"l{l}_mix")
        dh, agu, dyb, small["g_ffn1", l] = _ffn_bwd_act(dh, s["h0"], row(g_ffn1[l]), s["gu1"], wl["up1"], wl["dn1"], dep)
        ddn1 = _ffn_bwd_w(agu, 2, 1, dyb, agu, f"ffn_bwd_w_down_l{l}_ffn1").reshape(N_DEV, -1, d)
        if l > 0:
            dup1 = _ffn_bwd_w(agu, 0, 2, s["n1"], ddn1, f"ffn_bwd_w_up_l{l}_ffn1").reshape(N_DEV, -1, d)
            dep = scatter([("w_ffn1_up", l), ("w_ffn1_down", l)], [dup1, ddn1], f"l{l}_ffn1")
        else:
            dep = scatter([("w_ffn1_down", l)], [ddn1], f"l{l}_ffn1_down")
            dup1 = _ffn_bwd_w(agu, 0, 2, s["n1"], dep, f"ffn_bwd_w_up_l{l}_ffn1").reshape(N_DEV, -1, d)
            dep = scatter([("w_ffn1_up", l)], [dup1], f"l{l}_ffn1_up")
    grad_x = dh[None]

    big = {"w_ffn2_up": (w_ffn2_up, m_w_ffn2_up, v_w_ffn2_up, True), "w_ffn2_down": (w_ffn2_down, m_w_ffn2_down, v_w_ffn2_down, False),
           "w_in": (w_in, m_w_in, v_w_in, True), "w_mem_kv": (w_mem_kv, m_w_mem_kv, v_w_mem_kv, False),
           "w_out": (w_out, m_w_out, v_w_out, False), "w_ffn1_up": (w_ffn1_up, m_w_ffn1_up, v_w_ffn1_up, True),
           "w_ffn1_down": (w_ffn1_down, m_w_ffn1_down, v_w_ffn1_down, False)}
    me_index = jnp.reshape(me, (1,)).astype(jnp.int32)
    sharded, landed, begun = {}, {}, {}
    by_layer = {name for name, _ in started[-1][0]}

    def finish(groups, after):
        for names, state in groups:
            owns, lands = _scatter_wait(state, after)
            for key, own, land in zip(names, owns, lands):
                landed[key] = (own, land)
            after = lands[0]
            for name, l in names:
                w, m, v, transposed = big[name]
                fix = tr if transposed else (lambda a: a)
                if name in by_layer:
                    res = _adamw_sharded(me_index, [landed[name, l]], fix(w), fix(m), fix(v), after, l, begun.get(name))
                    done = name in begun
                    begun[name] = res
                elif all((name, k) in landed for k in range(depth)):
                    res = _adamw_sharded(me_index, [landed[name, k] for k in range(depth)], fix(w), fix(m), fix(v), after)
                    done = True
                else:
                    continue
                if done:
                    sharded[name] = tuple(fix(r) for r in res)
                after = res[0]
        return after

    loss, small_out, dep = reduce_small(finish(started[:-1], dep))
    finish(started[-1:], dep)

    order = ["g_ffn1", "w_ffn1_up", "w_ffn1_down", "g_mix", "w_in", "conv_w", "sinks", "g_mem", "w_mem_kv", "g_grp",
             "w_out", "g_ffn2", "w_ffn2_up", "w_ffn2_down", "g_final"]
    results = {**sharded, **small_out}
    outs = [loss, grad_x]
    for part in range(4):
        outs += [results[n][part] for n in order]
    return tuple(outs)
```

```python
import numpy as np
import jax
import jax.numpy as jnp
from jax import lax
from jax.experimental import pallas as pl
from jax.experimental.pallas import tpu as pltpu
from jax.experimental.pallas import tpu_sc as plsc

F32 = jnp.float32
BF16 = jnp.bfloat16

N_DEV = 8
EPS = 1e-6
N_SWA_HEADS = 8
N_SWA_KV = 2
SWA_GROUP = N_SWA_HEADS // N_SWA_KV
HEAD_DIM = 64
N_MEM_HEADS = 4
D_CONV = 256
BLOCK = 128
D_SWA = N_SWA_HEADS * HEAD_DIM
D_KV = N_SWA_KV * HEAD_DIM
D_MEMQ = N_MEM_HEADS * HEAD_DIM
D_MIX = D_CONV + D_SWA + D_MEMQ
D_IN = 3 * D_CONV + D_SWA + 2 * D_KV + D_MEMQ
COL_BG, COL_CG, COL_U = 0, D_CONV, 2 * D_CONV
COL_Q = 3 * D_CONV
COL_K = COL_Q + D_SWA
COL_V = COL_K + D_KV
COL_QM = COL_V + D_KV
MIX_GROUPS = ((0, D_CONV), (D_CONV, D_CONV + D_SWA), (D_CONV + D_SWA, D_MIX))
SLOPES = tuple(2.0 ** (-8.0 * (i + 1) / N_SWA_HEADS) for i in range(N_SWA_HEADS))
SCALE = HEAD_DIM ** -0.5
NEG = -1e30

ADAM_LR = 0.001
ADAM_B1 = 0.9
ADAM_B2 = 0.999
ADAM_EPS = 1e-08
ADAM_WD = 0.01
ADAM_STEP = 10

V7X_VMEM_BYTES = 64 * 1024 * 1024
VMEM_LIMIT = (V7X_VMEM_BYTES * 3) // 4
VMEM_LIMIT_WIDE = (V7X_VMEM_BYTES * 15) // 16
MESH = pl.DeviceIdType.MESH


def _pcall(body, **kw):
    return pl.pallas_call(body, **kw)


def _params(sem=None, vmem=VMEM_LIMIT):
    return pltpu.CompilerParams(dimension_semantics=sem, vmem_limit_bytes=vmem)


def _dot(a, b):
    return lax.dot_general(a, b, (((1,), (0,)), ((), ())), preferred_element_type=F32)


def _dot_nt(a, b):
    return lax.dot_general(a, b, (((1,), (1,)), ((), ())), preferred_element_type=F32)


def _dot_tn(a, b):
    return lax.dot_general(a, b, (((0,), (0,)), ((), ())), preferred_element_type=F32)


def _rstd(x):
    return lax.rsqrt(jnp.mean(x * x, axis=-1, keepdims=True) + EPS)


def _sigmoid(x):
    return 1.0 / (1.0 + jnp.exp(-x))


def _sum8(x):
    r, w = x.shape
    return jnp.sum(x.reshape(r // 8, 8, w), axis=0)


def _tok_block(t, rows=512):
    return min(rows, t)


def _feat_block(f, parts=N_DEV // 2):
    return f // parts


def _ffn_fwd(h, g, wup_t, wdn):
    t, d = h.shape
    f = wdn.shape[0]
    tm, tf = _tok_block(t), _feat_block(f, 2)
    ni, nj = t // tm, f // tf

    def body(h_ref, g_ref, wup_ref, wdn_ref, ho_ref, gu_ref, n_ref, nt_ref, acc_ref):
        j = pl.program_id(1)

        @pl.when(j == 0)
        def _():
            hh = h_ref[...]
            n = hh * _rstd(hh) * g_ref[...]
            n_ref[...] = n.astype(BF16)
            nt_ref[...] = n.T.astype(BF16)
            acc_ref[...] = jnp.zeros_like(acc_ref)

        nt = nt_ref[...]
        gate = _dot(wup_ref[0], nt)
        up = _dot(wup_ref[1], nt)
        gu_ref[0] = gate.astype(BF16)
        gu_ref[1] = up.astype(BF16)
        a = gate * _sigmoid(gate) * up
        acc_ref[...] += _dot_tn(a.astype(BF16), wdn_ref[...])

        @pl.when(j == nj - 1)
        def _():
            ho_ref[...] = h_ref[...] + 0.5 * acc_ref[...]

    return _pcall(
        body, name="ffn_fwd", grid=(ni, nj),
        in_specs=[pl.BlockSpec((tm, d), lambda i, j: (i, 0)),
                  pl.BlockSpec((1, d), lambda i, j: (0, 0)),
                  pl.BlockSpec((2, tf, d), lambda i, j: (0, j, 0)),
                  pl.BlockSpec((tf, d), lambda i, j: (j, 0))],
        out_specs=[pl.BlockSpec((tm, d), lambda i, j: (i, 0)),
                   pl.BlockSpec((2, tf, tm), lambda i, j: (0, j, i)),
                   pl.BlockSpec((tm, d), lambda i, j: (i, 0))],
        out_shape=[jax.ShapeDtypeStruct((t, d), F32),
                   jax.ShapeDtypeStruct((2, f, t), BF16),
                   jax.ShapeDtypeStruct((t, d), BF16)],
        scratch_shapes=[pltpu.VMEM((d, tm), BF16), pltpu.VMEM((tm, d), F32)],
        compiler_params=_params(("parallel", "arbitrary")),
    )(h, g, wup_t, wdn)


def _ffn_bwd_act(dho, h, g, gu, wup_t, wdn, dep):
    t, d = h.shape
    f = wdn.shape[0]
    tm, tf = _tok_block(t), _feat_block(f, 2)
    ni, nj = t // tm, f // tf

    def body(dho_ref, h_ref, g_ref, gu_ref, wup_ref, wdn_ref, dep_ref, dh_ref, agu_ref, dyb_ref, dg_ref, dyt_ref, acc_ref):
        i = pl.program_id(0)
        j = pl.program_id(1)

        @pl.when(j == 0)
        def _():
            dy0 = 0.5 * dho_ref[...]
            dyb_ref[...] = dy0.astype(BF16)
            dyt_ref[...] = dy0.T.astype(BF16)
            acc_ref[...] = jnp.zeros_like(acc_ref)

        da = _dot(wdn_ref[...], dyt_ref[...]).astype(BF16)
        gate = gu_ref[0]
        up = gu_ref[1]
        sg = _sigmoid(gate)
        silu = gate * sg
        dgate = da * up * (sg * (1.0 + gate * (1.0 - sg)))
        dup = da * silu
        agu_ref[0] = dgate
        agu_ref[1] = dup
        agu_ref[2] = silu * up
        acc_ref[...] += _dot_tn(dgate, wup_ref[0])
        acc_ref[...] += _dot_tn(dup, wup_ref[1])

        @pl.when(j == nj - 1)
        def _():
            hh = h_ref[...]
            r = _rstd(hh)
            xhat = hh * r
            dnf = acc_ref[...]
            dxh = dnf * g_ref[...]
            dh_ref[...] = dho_ref[...] + r * (dxh - xhat * jnp.mean(dxh * xhat, axis=-1, keepdims=True))
            part = _sum8(dnf * xhat)

            @pl.when(i == 0)
            def _():
                dg_ref[...] = part

            @pl.when(i > 0)
            def _():
                dg_ref[...] += part

    return _pcall(
        body, name="ffn_bwd_act", grid=(ni, nj),
        in_specs=[pl.BlockSpec((tm, d), lambda i, j: (i, 0)),
                  pl.BlockSpec((tm, d), lambda i, j: (i, 0)),
                  pl.BlockSpec((1, d), lambda i, j: (0, 0)),
                  pl.BlockSpec((2, tf, tm), lambda i, j: (0, j, i)),
                  pl.BlockSpec((2, tf, d), lambda i, j: (0, j, 0)),
                  pl.BlockSpec((tf, d), lambda i, j: (j, 0)),
                  pl.BlockSpec(memory_space=pl.ANY)],
        out_specs=[pl.BlockSpec((tm, d), lambda i, j: (i, 0)),
                   pl.BlockSpec((3, tf, tm), lambda i, j: (0, j, i)),
                   pl.BlockSpec((tm, d), lambda i, j: (i, 0)),
                   pl.BlockSpec((8, d), lambda i, j: (0, 0))],
        out_shape=[jax.ShapeDtypeStruct((t, d), F32),
                   jax.ShapeDtypeStruct((3, f, t), BF16),
                   jax.ShapeDtypeStruct((t, d), BF16),
                   jax.ShapeDtypeStruct((8, d), F32)],
        scratch_shapes=[pltpu.VMEM((d, tm), BF16), pltpu.VMEM((tm, d), F32)],
        compiler_params=_params(("arbitrary", "arbitrary"), VMEM_LIMIT_WIDE),
    )(dho, h, g, gu, wup_t, wdn, dep)


def _ffn_bwd_w(agu, first, count, rhs, dep, name):
    _, f, t = agu.shape
    d = rhs.shape[1]
    tm = _tok_block(t, 2048)
    tf = _feat_block(f, 2) if count == 1 else _feat_block(f)
    ni, nj = t // tm, f // tf

    def body(lhs_ref, rhs_ref, dep_ref, dw_ref, acc_ref):
        i = pl.program_id(1)
        @pl.when(i == 0)
        def _():
            acc_ref[...] = jnp.zeros_like(acc_ref)

        rb = rhs_ref[...]
        for k in range(count):
            acc_ref[k] += _dot(lhs_ref[k], rb)

        @pl.when(i == ni - 1)
        def _():
            dw_ref[...] = acc_ref[...].astype(BF16)

    return _pcall(
        body, name=name, grid=(nj, ni),
        in_specs=[pl.BlockSpec((count, tf, tm), lambda j, i: (first // count, j, i)),
                  pl.BlockSpec((tm, d), lambda j, i: (i, 0)),
                  pl.BlockSpec(memory_space=pl.ANY)],
        out_specs=pl.BlockSpec((count, tf, d), lambda j, i: (0, j, 0)),
        out_shape=jax.ShapeDtypeStruct((count, f, d), BF16),
        scratch_shapes=[pltpu.VMEM((count, tf, d), F32)],
        compiler_params=_params(("parallel", "arbitrary")),
    )(agu, rhs, dep)


N_HEADS = N_SWA_HEADS + N_MEM_HEADS


def _q_col(hd):
    return COL_Q + HEAD_DIM * hd if hd < N_SWA_HEADS else COL_QM + HEAD_DIM * (hd - N_SWA_HEADS)


def _mix_proj_fwd(h, g, win_t):
    t, d = h.shape
    tm = _tok_block(t)

    def body(h_ref, g_ref, win_ref, p_ref, n_ref, qh_ref):
        hh = h_ref[...]
        n = (hh * _rstd(hh) * g_ref[...]).astype(BF16)
        n_ref[...] = n
        proj = _dot_nt(n, win_ref[...])
        p_ref[...] = proj.astype(BF16)
        for hd in range(N_HEADS):
            c0 = _q_col(hd)
            qh_ref[hd] = (proj[:, c0:c0 + HEAD_DIM] * SCALE).astype(BF16)

    return _pcall(
        body, name="mix_proj_fwd", grid=(t // tm,),
        in_specs=[pl.BlockSpec((tm, d), lambda i: (i, 0)),
                  pl.BlockSpec((1, d), lambda i: (0, 0)),
                  pl.BlockSpec((D_IN, d), lambda i: (0, 0))],
        out_specs=[pl.BlockSpec((tm, D_IN), lambda i: (i, 0)),
                   pl.BlockSpec((tm, d), lambda i: (i, 0)),
                   pl.BlockSpec((N_HEADS, tm, HEAD_DIM), lambda i: (0, i, 0))],
        out_shape=[jax.ShapeDtypeStruct((t, D_IN), BF16), jax.ShapeDtypeStruct((t, d), BF16),
                   jax.ShapeDtypeStruct((N_HEADS, t, HEAD_DIM), BF16)],
        compiler_params=_params(("parallel",)),
    )(h, g, win_t)


def _memkv_fwd(mem, g, wkv, dep):
    m, d = mem.shape

    def body(mem_ref, g_ref, w_ref, dep_ref, mkv_ref, nt_ref):
        mm = mem_ref[...]
        n = mm * _rstd(mm) * g_ref[...]
        nt_ref[...] = n.T.astype(BF16)
        mkv_ref[...] = _dot(n.astype(BF16), w_ref[...]).astype(BF16)

    return _pcall(
        body, name="memkv_fwd", grid=(1,),
        in_specs=[pl.BlockSpec((m, d), lambda i: (0, 0)),
                  pl.BlockSpec((1, d), lambda i: (0, 0)),
                  pl.BlockSpec((d, 2 * D_MEMQ), lambda i: (0, 0)),
                  pl.BlockSpec(memory_space=pl.ANY)],
        out_specs=[pl.BlockSpec((m, 2 * D_MEMQ), lambda i: (0, 0)),
                   pl.BlockSpec((d, m), lambda i: (0, 0))],
        out_shape=[jax.ShapeDtypeStruct((m, 2 * D_MEMQ), BF16), jax.ShapeDtypeStruct((d, m), BF16)],
        compiler_params=_params(("arbitrary",)),
    )(mem, g, wkv, dep)


def _memkv_bwd(dmkv, mem, g, wkv, nt):
    m, d = mem.shape

    def body(dmkv_ref, mem_ref, g_ref, w_ref, nt_ref, dw_ref, dg_ref):
        db = dmkv_ref[...].astype(BF16)
        dw_ref[...] = _dot(nt_ref[...], db).astype(BF16)
        dn = _dot_nt(db, w_ref[...])
        mm = mem_ref[...]
        dg_ref[...] = _sum8(dn * (mm * _rstd(mm)))

    return _pcall(
        body, name="memkv_bwd", grid=(1,),
        in_specs=[pl.BlockSpec((m, 2 * D_MEMQ), lambda i: (0, 0)),
                  pl.BlockSpec((m, d), lambda i: (0, 0)),
                  pl.BlockSpec((1, d), lambda i: (0, 0)),
                  pl.BlockSpec((d, 2 * D_MEMQ), lambda i: (0, 0)),
                  pl.BlockSpec((d, m), lambda i: (0, 0))],
        out_specs=[pl.BlockSpec((d, 2 * D_MEMQ), lambda i: (0, 0)),
                   pl.BlockSpec((8, d), lambda i: (0, 0))],
        out_shape=[jax.ShapeDtypeStruct((d, 2 * D_MEMQ), BF16), jax.ShapeDtypeStruct((8, d), F32)],
        compiler_params=_params(("arbitrary",)),
    )(dmkv, mem, g, wkv, nt)


def _shift_rows(v, k, edge_rows, row):
    out = pltpu.roll(v, k, 0)
    for r in range(k):
        out = jnp.where(row == r, edge_rows[r], out)
    return out


def _shift_rows_up(v, k, edge_rows, row):
    n = v.shape[0]
    out = pltpu.roll(v, n - k, 0)
    for r in range(k):
        out = jnp.where(row == n - k + r, edge_rows[r], out)
    return out


GROUP_ROWS = SWA_GROUP * BLOCK
BIAS_CUR, BIAS_PREV, BIAS_NONE = 0, 1, 2


def _bias_table():
    tq = np.arange(BLOCK)[:, None]
    sk = np.arange(BLOCK)[None, :]
    slopes = np.asarray(SLOPES, np.float32)[:, None, None]
    cur = np.where(tq >= sk, -slopes * (tq - sk).astype(np.float32), NEG)
    prev = np.where(sk > tq, -slopes * (tq + BLOCK - sk).astype(np.float32), NEG)
    none = np.full_like(cur, NEG)
    tok = np.stack([cur, prev, none]).astype(np.float32).reshape(3, N_SWA_KV, GROUP_ROWS, BLOCK)
    return jnp.asarray(np.ascontiguousarray(tok.transpose(0, 1, 3, 2)))


def _head_cols(hd):
    return D_CONV + HEAD_DIM * hd


def _mix_core_fwd(p, qh, mkv, convw, sinks, bias_key):
    t = p.shape[0]
    m = mkv.shape[0]
    nb = t // BLOCK
    per_step = next(n for n in (8, 4, 2, 1) if nb % n == 0)
    rows = per_step * BLOCK

    def body(sk_ref, pc_ref, pkv_ref, ppc_ref, ppu_ref, qh_ref, mkv_ref, cw_ref, bc_ref, bf_ref, bp_ref, y_ref, l_ref):
        i = pl.program_id(0)
        prevf = (i > 0).astype(F32)
        row = lax.broadcasted_iota(jnp.int32, (BLOCK, D_CONV), 0)
        head_row = lax.broadcasted_iota(jnp.int32, (128, BLOCK), 0)
        w = cw_ref[...]

        for b in range(per_step):
            tok = slice(b * BLOCK, (b + 1) * BLOCK)
            before = slice((b - 1) * BLOCK, b * BLOCK)
            tail = slice(b * BLOCK - 16, b * BLOCK)
            bg = pc_ref[tok, COL_BG:COL_BG + D_CONV].astype(F32)
            cg = pc_ref[tok, COL_CG:COL_CG + D_CONV].astype(F32)
            u = pc_ref[tok, COL_U:COL_U + D_CONV].astype(F32)
            vv = cg * u
            if b == 0:
                pvv = ppc_ref[...].astype(F32) * ppu_ref[...].astype(F32) * prevf
            else:
                pvv = (pc_ref[tail, COL_CG:COL_CG + D_CONV].astype(F32)
                       * pc_ref[tail, COL_U:COL_U + D_CONV].astype(F32))
            vv1 = _shift_rows(vv, 1, [pvv[15:16]], row)
            vv2 = _shift_rows(vv, 2, [pvv[14:15], pvv[15:16]], row)
            y_ref[tok, 0:D_CONV] = bg * (w[0:1] * vv2 + w[1:2] * vv1 + w[2:3] * vv)

            lse_t = jnp.zeros((128, BLOCK), F32)
            for kv in range(N_SWA_KV):
                heads = range(kv * SWA_GROUP, (kv + 1) * SWA_GROUP)
                kc = pc_ref[tok, COL_K + HEAD_DIM * kv:COL_K + HEAD_DIM * (kv + 1)]
                vc = pc_ref[tok, COL_V + HEAD_DIM * kv:COL_V + HEAD_DIM * (kv + 1)]
                if b == 0:
                    kp = pkv_ref[:, HEAD_DIM * kv:HEAD_DIM * (kv + 1)]
                    vp = pkv_ref[:, D_KV + HEAD_DIM * kv:D_KV + HEAD_DIM * (kv + 1)]
                    bias_prev = bf_ref[0, kv]
                else:
                    kp = pc_ref[before, COL_K + HEAD_DIM * kv:COL_K + HEAD_DIM * (kv + 1)]
                    vp = pc_ref[before, COL_V + HEAD_DIM * kv:COL_V + HEAD_DIM * (kv + 1)]
                    bias_prev = bp_ref[0, kv]
                qg = qh_ref[kv * SWA_GROUP:(kv + 1) * SWA_GROUP, tok].reshape(GROUP_ROWS, HEAD_DIM)
                sc = _dot_nt(kc, qg) + bc_ref[0, kv]
                sp = _dot_nt(kp, qg) + bias_prev
                sink = jnp.concatenate([jnp.full((1, BLOCK), sk_ref[0, hd], F32) for hd in heads], axis=1)
                mx = jnp.maximum(jnp.max(jnp.maximum(sc, sp), axis=0, keepdims=True), sink)
                ec = jnp.exp(sc - mx)
                ep = jnp.exp(sp - mx)
                den = jnp.sum(ec + ep, axis=0, keepdims=True) + jnp.exp(sink - mx)
                ot = (_dot_tn(vc, ec.astype(BF16)) + _dot_tn(vp, ep.astype(BF16))) / den
                lse = mx + jnp.log(den)
                for gi, hd in enumerate(heads):
                    span = slice(gi * BLOCK, (gi + 1) * BLOCK)
                    y_ref[tok, _head_cols(hd):_head_cols(hd) + HEAD_DIM] = ot[:, span].T
                    lse_t = jnp.where(head_row == hd, lse[:, span], lse_t)

            for hm in range(N_MEM_HEADS):
                hd = N_SWA_HEADS + hm
                mk = mkv_ref[:, HEAD_DIM * hm:HEAD_DIM * (hm + 1)]
                mv = mkv_ref[:, D_MEMQ + HEAD_DIM * hm:D_MEMQ + HEAD_DIM * (hm + 1)]
                s = _dot_nt(mk, qh_ref[hd, tok])
                mx = jnp.max(s, axis=0, keepdims=True)
                e = jnp.exp(s - mx)
                den = jnp.sum(e, axis=0, keepdims=True)
                y_ref[tok, _head_cols(hd):_head_cols(hd) + HEAD_DIM] = (_dot_tn(mv, e.astype(BF16)) / den).T
                lse_t = jnp.where(head_row == hd, mx + jnp.log(den), lse_t)
            l_ref[tok, :] = lse_t.T

    kv_col = COL_K // (2 * D_KV)
    bias_block = (1, N_SWA_KV, BLOCK, GROUP_ROWS)
    return _pcall(
        body, name="mix_core_fwd", grid=(nb // per_step,),
        in_specs=[pl.BlockSpec(memory_space=pltpu.SMEM),
                  pl.BlockSpec((rows, D_IN), lambda i: (i, 0)),
                  pl.BlockSpec((BLOCK, 2 * D_KV), lambda i: (jnp.maximum(i * per_step - 1, 0), kv_col)),
                  pl.BlockSpec((16, D_CONV), lambda i: (jnp.maximum(i * (rows // 16) - 1, 0), COL_CG // D_CONV)),
                  pl.BlockSpec((16, D_CONV), lambda i: (jnp.maximum(i * (rows // 16) - 1, 0), COL_U // D_CONV)),
                  pl.BlockSpec((N_HEADS, rows, HEAD_DIM), lambda i: (0, i, 0)),
                  pl.BlockSpec((m, 2 * D_MEMQ), lambda i: (0, 0)),
                  pl.BlockSpec((3, D_CONV), lambda i: (0, 0)),
                  pl.BlockSpec(bias_block, lambda i: (BIAS_CUR, 0, 0, 0)),
                  pl.BlockSpec(bias_block, lambda i: (jnp.where(i == 0, BIAS_NONE, BIAS_PREV), 0, 0, 0)),
                  pl.BlockSpec(bias_block, lambda i: (BIAS_PREV, 0, 0, 0))],
        out_specs=[pl.BlockSpec((rows, D_MIX), lambda i: (i, 0)),
                   pl.BlockSpec((rows, 128), lambda i: (i, 0))],
        out_shape=[jax.ShapeDtypeStruct((t, D_MIX), F32), jax.ShapeDtypeStruct((t, 128), F32)],
        compiler_params=_params(("parallel",)),
    )(sinks, p, p, p, p, qh, mkv, convw, bias_key, bias_key, bias_key)


def _mix_core_bwd(p, qh, dyconv, doh, delta, lse, mkv, convw, sinks, bias_key):
    t = p.shape[0]
    m = mkv.shape[0]
    nb = t // BLOCK

    def body(sk_ref, pc_ref, pkv_ref, ppc_ref, ppu_ref, pnb_ref, dyc_ref, dyn_ref, qc_ref, qn_ref, doc_ref, don_ref,
             dlc_ref, dln_ref, lc_ref, ln_ref, mkv_ref, cw_ref, bp_ref, bct_ref, bnt_ref,
             dp_ref, dmkv_ref, dcw_ref, dsk_ref):
        i = pl.program_id(0)
        prevf = (i > 0).astype(F32)
        nextf = (i < nb - 1).astype(F32)
        row = lax.broadcasted_iota(jnp.int32, (BLOCK, D_CONV), 0)

        @pl.when(i == 0)
        def _():
            dmkv_ref[...] = jnp.zeros_like(dmkv_ref)
            dcw_ref[...] = jnp.zeros_like(dcw_ref)
            dsk_ref[...] = jnp.zeros_like(dsk_ref)

        bg = pc_ref[:, COL_BG:COL_BG + D_CONV].astype(F32)
        cg = pc_ref[:, COL_CG:COL_CG + D_CONV].astype(F32)
        u = pc_ref[:, COL_U:COL_U + D_CONV].astype(F32)
        vv = cg * u
        pvv = ppc_ref[...].astype(F32) * ppu_ref[...].astype(F32) * prevf
        vv1 = _shift_rows(vv, 1, [pvv[15:16]], row)
        vv2 = _shift_rows(vv, 2, [pvv[14:15], pvv[15:16]], row)
        w = cw_ref[...]
        yconv = w[0:1] * vv2 + w[1:2] * vv1 + w[2:3] * vv
        dyo = dyc_ref[...]
        dyc = dyo * bg
        nxt = dyn_ref[...] * pnb_ref[...].astype(F32) * nextf
        d1 = _shift_rows_up(dyc, 1, [nxt[0:1]], row)
        d2 = _shift_rows_up(dyc, 2, [nxt[0:1], nxt[1:2]], row)
        dvv = w[2:3] * dyc + w[1:2] * d1 + w[0:1] * d2
        dp_ref[:, COL_BG:COL_BG + D_CONV] = (dyo * yconv).astype(BF16)
        dp_ref[:, COL_CG:COL_CG + D_CONV] = (dvv * u).astype(BF16)
        dp_ref[:, COL_U:COL_U + D_CONV] = (dvv * cg).astype(BF16)
        dcw_ref[0:1, :] += jnp.sum(dyc * vv2, axis=0, keepdims=True)
        dcw_ref[1:2, :] += jnp.sum(dyc * vv1, axis=0, keepdims=True)
        dcw_ref[2:3, :] += jnp.sum(dyc * vv, axis=0, keepdims=True)

        lse_t, dl_t = lc_ref[...].T, dlc_ref[...].T
        lse_nt, dl_nt = ln_ref[...].T, dln_ref[...].T

        def stack_rows(tile_t, heads):
            return jnp.concatenate([tile_t[hd:hd + 1, :] for hd in heads], axis=1)

        lane8 = jnp.where(lax.broadcasted_iota(jnp.int32, (8, 128), 0) == 0,
                          lax.broadcasted_iota(jnp.int32, (8, 128), 1), -1)
        dsk = jnp.zeros((8, 128), F32)
        for kv in range(N_SWA_KV):
            heads = range(kv * SWA_GROUP, (kv + 1) * SWA_GROUP)
            kc = pc_ref[:, COL_K + HEAD_DIM * kv:COL_K + HEAD_DIM * (kv + 1)]
            vc = pc_ref[:, COL_V + HEAD_DIM * kv:COL_V + HEAD_DIM * (kv + 1)]
            kp = pkv_ref[:, HEAD_DIM * kv:HEAD_DIM * (kv + 1)]
            vp = pkv_ref[:, D_KV + HEAD_DIM * kv:D_KV + HEAD_DIM * (kv + 1)]
            qg = qc_ref[kv * SWA_GROUP:(kv + 1) * SWA_GROUP].reshape(GROUP_ROWS, HEAD_DIM)
            dog = doc_ref[kv * SWA_GROUP:(kv + 1) * SWA_GROUP].reshape(GROUP_ROWS, HEAD_DIM)
            qn = qn_ref[kv * SWA_GROUP:(kv + 1) * SWA_GROUP].reshape(GROUP_ROWS, HEAD_DIM)
            don = don_ref[kv * SWA_GROUP:(kv + 1) * SWA_GROUP].reshape(GROUP_ROWS, HEAD_DIM)
            lse_row, dl_row = stack_rows(lse_t, heads), stack_rows(dl_t, heads)
            ptp = jnp.exp(_dot_nt(kp, qg) + bp_ref[0, kv] - lse_row)
            dstp = (ptp * (_dot_nt(vp, dog) - dl_row)).astype(BF16)
            dq = _dot_tn(dstp, kp)
            pt = jnp.exp(_dot_nt(kc, qg) + bct_ref[0, kv] - lse_row)
            dst = (pt * (_dot_nt(vc, dog) - dl_row)).astype(BF16)
            dv = _dot(pt.astype(BF16), dog)
            dk = _dot(dst, qg)
            dq = dq + _dot_tn(dst, kc)
            ptn = jnp.exp(_dot_nt(kc, qn) + bnt_ref[0, kv] - stack_rows(lse_nt, heads))
            dstn = (ptn * (_dot_nt(vc, don) - stack_rows(dl_nt, heads))).astype(BF16)
            dv = dv + _dot(ptn.astype(BF16), don)
            dk = dk + _dot(dstn, qn)
            dp_ref[:, COL_K + HEAD_DIM * kv:COL_K + HEAD_DIM * (kv + 1)] = dk.astype(BF16)
            dp_ref[:, COL_V + HEAD_DIM * kv:COL_V + HEAD_DIM * (kv + 1)] = dv.astype(BF16)
            sink = jnp.concatenate([jnp.full((1, BLOCK), sk_ref[0, hd], F32) for hd in heads], axis=1)
            sink_term = jnp.exp(sink - lse_row) * dl_row
            for gi, hd in enumerate(heads):
                span = slice(gi * BLOCK, (gi + 1) * BLOCK)
                dp_ref[:, _q_col(hd):_q_col(hd) + HEAD_DIM] = (dq[span] * SCALE).astype(BF16)
                dsk = dsk + jnp.where(lane8 == hd, -jnp.sum(sink_term[:, span], axis=1, keepdims=True), 0.0)
        dsk_ref[...] += dsk

        for hm in range(N_MEM_HEADS):
            hd = N_SWA_HEADS + hm
            qm, dom = qc_ref[hd], doc_ref[hd]
            mk = mkv_ref[:, HEAD_DIM * hm:HEAD_DIM * (hm + 1)]
            mv = mkv_ref[:, D_MEMQ + HEAD_DIM * hm:D_MEMQ + HEAD_DIM * (hm + 1)]
            pt = jnp.exp(_dot_nt(mk, qm) - lse_t[hd:hd + 1, :])
            dst = (pt * (_dot_nt(mv, dom) - dl_t[hd:hd + 1, :])).astype(BF16)
            dp_ref[:, _q_col(hd):_q_col(hd) + HEAD_DIM] = (_dot_tn(dst, mk) * SCALE).astype(BF16)
            dmkv_ref[:, HEAD_DIM * hm:HEAD_DIM * (hm + 1)] += _dot(dst, qm)
            dmkv_ref[:, D_MEMQ + HEAD_DIM * hm:D_MEMQ + HEAD_DIM * (hm + 1)] += _dot(pt.astype(BF16), dom)

    cur = lambda i: (i, 0)
    const = lambda i: (0, 0)
    rows16 = BLOCK // 16
    last16 = t // 16 - 1
    before = lambda col: (lambda i: (jnp.maximum(i * rows16 - 1, 0), col))
    after = lambda i: (jnp.minimum((i + 1) * rows16, last16), 0)
    heads_cur = lambda i: (0, i, 0)
    heads_next = lambda i: (0, jnp.minimum(i + 1, nb - 1), 0)
    stat_next = lambda i: (jnp.minimum(i + 1, nb - 1), 0)
    key_block = (1, N_SWA_KV, BLOCK, GROUP_ROWS)
    head_block = (N_HEADS, BLOCK, HEAD_DIM)
    return _pcall(
        body, name="mix_core_bwd", grid=(nb,),
        in_specs=[pl.BlockSpec(memory_space=pltpu.SMEM),
                  pl.BlockSpec((BLOCK, D_IN), cur),
                  pl.BlockSpec((BLOCK, 2 * D_KV), lambda i: (jnp.maximum(i - 1, 0), COL_K // (2 * D_KV))),
                  pl.BlockSpec((16, D_CONV), before(COL_CG // D_CONV)),
                  pl.BlockSpec((16, D_CONV), before(COL_U // D_CONV)),
                  pl.BlockSpec((16, D_CONV), after),
                  pl.BlockSpec((BLOCK, D_CONV), cur),
                  pl.BlockSpec((16, D_CONV), after),
                  pl.BlockSpec(head_block, heads_cur), pl.BlockSpec(head_block, heads_next),
                  pl.BlockSpec(head_block, heads_cur), pl.BlockSpec(head_block, heads_next),
                  pl.BlockSpec((BLOCK, 128), cur), pl.BlockSpec((BLOCK, 128), stat_next),
                  pl.BlockSpec((BLOCK, 128), cur), pl.BlockSpec((BLOCK, 128), stat_next),
                  pl.BlockSpec((m, 2 * D_MEMQ), const),
                  pl.BlockSpec((3, D_CONV), const),
                  pl.BlockSpec(key_block, lambda i: (jnp.where(i == 0, BIAS_NONE, BIAS_PREV), 0, 0, 0)),
                  pl.BlockSpec(key_block, lambda i: (BIAS_CUR, 0, 0, 0)),
                  pl.BlockSpec(key_block, lambda i: (jnp.where(i == nb - 1, BIAS_NONE, BIAS_PREV), 0, 0, 0))],
        out_specs=[pl.BlockSpec((BLOCK, D_IN), cur),
                   pl.BlockSpec((m, 2 * D_MEMQ), const),
                   pl.BlockSpec((8, D_CONV), const),
                   pl.BlockSpec((8, 128), const)],
        out_shape=[jax.ShapeDtypeStruct((t, D_IN), BF16),
                   jax.ShapeDtypeStruct((m, 2 * D_MEMQ), F32),
                   jax.ShapeDtypeStruct((8, D_CONV), F32),
                   jax.ShapeDtypeStruct((8, 128), F32)],
        compiler_params=_params(("arbitrary",)),
    )(sinks, p, p, p, p, p, dyconv, dyconv, qh, qh, doh, doh, delta, delta, lse, lse, mkv, convw,
      bias_key, bias_key, bias_key)


def _group_norms(y):
    out = []
    for a, b in MIX_GROUPS:
        ys = y[:, a:b]
        r = _rstd(ys)
        out.append((ys * r, r))
    return out


def _mix_out_fwd(y, h, g, wout):
    t, d = h.shape
    tm = _tok_block(t)

    def body(y_ref, h_ref, g_ref, w_ref, ho_ref, mt_ref):
        yhat = jnp.concatenate([yh for yh, _ in _group_norms(y_ref[...])], axis=-1)
        mixed = yhat * g_ref[...]
        mt_ref[...] = mixed.T.astype(BF16)
        ho_ref[...] = h_ref[...] + _dot(mixed.astype(BF16), w_ref[...])

    return _pcall(
        body, name="mix_out_fwd", grid=(t // tm,),
        in_specs=[pl.BlockSpec((tm, D_MIX), lambda i: (i, 0)),
                  pl.BlockSpec((tm, d), lambda i: (i, 0)),
                  pl.BlockSpec((1, D_MIX), lambda i: (0, 0)),
                  pl.BlockSpec((D_MIX, d), lambda i: (0, 0))],
        out_specs=[pl.BlockSpec((tm, d), lambda i: (i, 0)),
                   pl.BlockSpec((D_MIX, tm), lambda i: (0, i))],
        out_shape=[jax.ShapeDtypeStruct((t, d), F32), jax.ShapeDtypeStruct((D_MIX, t), BF16)],
        compiler_params=_params(("parallel",)),
    )(y, h, g, wout)


def _head_indicator():
    ind = np.zeros((D_MIX, 128), np.float32)
    for hd in range(N_HEADS):
        ind[_head_cols(hd):_head_cols(hd) + HEAD_DIM, hd] = 1.0
    return jnp.asarray(ind, BF16)


def _mix_out_bwd(dho, y, g, wout, mt, dep):
    t, d = dho.shape
    tm = _tok_block(t)
    ni = t // tm

    def body(dho_ref, y_ref, g_ref, w_ref, mt_ref, ind_ref, dep_ref, dyc_ref, doh_ref, dl_ref, dw_ref, dg_ref, acc_ref):
        i = pl.program_id(0)
        dhb = dho_ref[...].astype(BF16)
        dm = _dot_nt(dhb, w_ref[...])
        pw = _dot(mt_ref[...], dhb)
        gg = g_ref[...]
        yy = y_ref[...]
        dys = []
        dgs = []
        for (a, b), (yhat, r) in zip(MIX_GROUPS, _group_norms(yy)):
            dmg = dm[:, a:b]
            dgs.append(_sum8(dmg * yhat))
            dyh = dmg * gg[:, a:b]
            dys.append(r * (dyh - yhat * jnp.mean(dyh * yhat, axis=-1, keepdims=True)))
        dy = jnp.concatenate(dys, axis=-1)
        dyc_ref[...] = dy[:, 0:D_CONV]
        for hd in range(N_HEADS):
            doh_ref[hd] = dy[:, _head_cols(hd):_head_cols(hd) + HEAD_DIM].astype(BF16)
        prod = dy * yy
        hi = prod.astype(BF16)
        lo = (prod - hi.astype(F32)).astype(BF16)
        dl_ref[...] = _dot(hi, ind_ref[...]) + _dot(lo, ind_ref[...])
        part = jnp.concatenate(dgs, axis=-1)

        @pl.when(i == 0)
        def _():
            acc_ref[...] = pw
            dg_ref[...] = part

        @pl.when(i > 0)
        def _():
            acc_ref[...] += pw
            dg_ref[...] += part

        @pl.when(i == ni - 1)
        def _():
            dw_ref[...] = acc_ref[...].astype(BF16)

    return _pcall(
        body, name="mix_out_bwd", grid=(ni,),
        in_specs=[pl.BlockSpec((tm, d), lambda i: (i, 0)),
                  pl.BlockSpec((tm, D_MIX), lambda i: (i, 0)),
                  pl.BlockSpec((1, D_MIX), lambda i: (0, 0)),
                  pl.BlockSpec((D_MIX, d), lambda i: (0, 0)),
                  pl.BlockSpec((D_MIX, tm), lambda i: (0, i)),
                  pl.BlockSpec((D_MIX, 128), lambda i: (0, 0)),
                  pl.BlockSpec(memory_space=pl.ANY)],
        out_specs=[pl.BlockSpec((tm, D_CONV), lambda i: (i, 0)),
                   pl.BlockSpec((N_HEADS, tm, HEAD_DIM), lambda i: (0, i, 0)),
                   pl.BlockSpec((tm, 128), lambda i: (i, 0)),
                   pl.BlockSpec((D_MIX, d), lambda i: (0, 0)),
                   pl.BlockSpec((8, D_MIX), lambda i: (0, 0))],
        out_shape=[jax.ShapeDtypeStruct((t, D_CONV), F32),
                   jax.ShapeDtypeStruct((N_HEADS, t, HEAD_DIM), BF16),
                   jax.ShapeDtypeStruct((t, 128), F32),
                   jax.ShapeDtypeStruct((D_MIX, d), BF16),
                   jax.ShapeDtypeStruct((8, D_MIX), F32)],
        scratch_shapes=[pltpu.VMEM((D_MIX, d), F32)],
        compiler_params=_params(("arbitrary",)),
    )(dho, y, g, wout, mt, _head_indicator(), dep)


def _mix_proj_bwd(dp, dho, h, g, win_t, n):
    t, d = h.shape
    tm = _tok_block(t)
    ni = t // tm

    def body(dp_ref, dho_ref, h_ref, g_ref, w_ref, n_ref, dh_ref, dw_ref, dg_ref, acc_ref):
        i = pl.program_id(0)
        dpb = dp_ref[...]
        dn = _dot(dpb, w_ref[...])

        @pl.when(i == 0)
        def _():
            acc_ref[...] = jnp.zeros_like(acc_ref)

        acc_ref[...] += _dot_tn(dpb, n_ref[...])
        hh = h_ref[...]
        r = _rstd(hh)
        xhat = hh * r
        dxh = dn * g_ref[...]
        dh_ref[...] = dho_ref[...] + r * (dxh - xhat * jnp.mean(dxh * xhat, axis=-1, keepdims=True))
        part = _sum8(dn * xhat)

        @pl.when(i == 0)
        def _():
            dg_ref[...] = part

        @pl.when(i > 0)
        def _():
            dg_ref[...] += part

        @pl.when(i == ni - 1)
        def _():
            dw_ref[...] = acc_ref[...].astype(BF16)

    return _pcall(
        body, name="mix_proj_bwd", grid=(ni,),
        in_specs=[pl.BlockSpec((tm, D_IN), lambda i: (i, 0)),
                  pl.BlockSpec((tm, d), lambda i: (i, 0)),
                  pl.BlockSpec((tm, d), lambda i: (i, 0)),
                  pl.BlockSpec((1, d), lambda i: (0, 0)),
                  pl.BlockSpec((D_IN, d), lambda i: (0, 0)),
                  pl.BlockSpec((tm, d), lambda i: (i, 0))],
        out_specs=[pl.BlockSpec((tm, d), lambda i: (i, 0)),
                   pl.BlockSpec((D_IN, d), lambda i: (0, 0)),
                   pl.BlockSpec((8, d), lambda i: (0, 0))],
        out_shape=[jax.ShapeDtypeStruct((t, d), F32),
                   jax.ShapeDtypeStruct((D_IN, d), BF16),
                   jax.ShapeDtypeStruct((8, d), F32)],
        scratch_shapes=[pltpu.VMEM((D_IN, d), F32)],
        compiler_params=_params(("arbitrary",)),
    )(dp, dho, h, g, win_t, n)


def _final_loss(h, g, tgt):
    t, d = h.shape
    tm = _tok_block(t)

    def body(h_ref, g_ref, t_ref, dh_ref, ls_ref, dg_ref):
        i = pl.program_id(0)
        hh = h_ref[...]
        r = _rstd(hh)
        xhat = hh * r
        gg = g_ref[...]
        err = xhat * gg - t_ref[...]
        dy = err * (1.0 / d)
        dxh = dy * gg
        dh_ref[...] = r * (dxh - xhat * jnp.mean(dxh * xhat, axis=-1, keepdims=True))
        lpart = _sum8(err * err)
        gpart = _sum8(dy * xhat)

        @pl.when(i == 0)
        def _():
            ls_ref[...] = lpart
            dg_ref[...] = gpart

        @pl.when(i > 0)
        def _():
            ls_ref[...] += lpart
            dg_ref[...] += gpart

    return _pcall(
        body, name="final_loss", grid=(t // tm,),
        in_specs=[pl.BlockSpec((tm, d), lambda i: (i, 0)),
                  pl.BlockSpec((1, d), lambda i: (0, 0)),
                  pl.BlockSpec((tm, d), lambda i: (i, 0))],
        out_specs=[pl.BlockSpec((tm, d), lambda i: (i, 0)),
                   pl.BlockSpec((8, d), lambda i: (0, 0)),
                   pl.BlockSpec((8, d), lambda i: (0, 0))],
        out_shape=[jax.ShapeDtypeStruct((t, d), F32),
                   jax.ShapeDtypeStruct((8, d), F32),
                   jax.ShapeDtypeStruct((8, d), F32)],
        compiler_params=_params(("arbitrary",)),
    )(h, g, tgt)


def _position():
    return lax.axis_index("x"), lax.axis_index("y"), lax.axis_index("c")


def _flip(v, bit):
    return 1 - v if bit else v


def _peer(k):
    x, y, c = _position()
    return _flip(x, k & 4), _flip(y, k & 2), _flip(c, k & 1)


def _slot(px, py, pc):
    return 4 * px + 2 * py + pc


def _handshake(peers):
    barrier = pltpu.get_barrier_semaphore()
    for peer in peers:
        pl.semaphore_signal(barrier, inc=1, device_id=peer, device_id_type=MESH)
    pl.semaphore_wait(barrier, len(peers))


def _sequencer_call(body, name, collective_id, out_type, scratch_types, operands):
    return pl.kernel(
        body, out_type=out_type, mesh=plsc.ScalarSubcoreMesh(axis_name="sequencer", num_cores=1), name=name,
        scratch_types=scratch_types, compiler_params=pltpu.CompilerParams(collective_id=collective_id),
    )(*operands)


def _all_gather(shards, name, collective_id):
    nt = len(shards)

    def body(*refs):
        xs = refs[:nt]
        outs = refs[nt:2 * nt]
        send_sems, recv_sems, local_sems = refs[2 * nt:]
        x, y, c = _position()
        me, sibling = (x, y, c), (x, y, 1 - c)
        xn, yn, dg = (1 - x, y), (x, 1 - y), (1 - x, 1 - y)
        pick = lambda a, b: (jnp.where(c == 0, a[0], b[0]), jnp.where(c == 0, a[1], b[1]))
        relay_from, relay_to = pick(yn, xn), pick(xn, yn)
        _handshake([sibling, (*xn, c), (*yn, c)])

        def copy(t, k, block, to, src=None):
            dst = outs[t].at[_slot(*block)]
            return pltpu.make_async_remote_copy(
                src_ref=dst if src is None else src, dst_ref=dst,
                send_sem=send_sems.at[t, k], recv_sem=recv_sems.at[t, k],
                device_id=to, device_id_type=MESH)

        mine = [pltpu.make_async_copy(xs[t], outs[t].at[_slot(*me)], local_sems.at[t]) for t in range(nt)]
        for cp in mine:
            cp.start()
        sent = []
        for t in range(nt):
            sent += [copy(t, 0, me, sibling, src=xs[t]), copy(t, 1, me, (*xn, c), src=xs[t]),
                     copy(t, 2, me, (*yn, c), src=xs[t])]
        for cp in sent:
            cp.start()
        for t in range(nt):
            copy(t, 1, (*xn, c), me).wait_recv()
            copy(t, 2, (*yn, c), me).wait_recv()
            passed = [copy(t, 3, (*relay_from, c), (*relay_to, c)),
                      copy(t, 4, (*xn, c), sibling), copy(t, 5, (*yn, c), sibling)]
            for cp in passed:
                cp.start()
            sent += passed
        for t in range(nt):
            copy(t, 3, (*dg, c), me).wait_recv()
            fwd = copy(t, 6, (*dg, c), sibling)
            fwd.start()
            sent.append(fwd)
        for t in range(nt):
            copy(t, 0, sibling, me).wait_recv()
            for k, chip in ((4, xn), (5, yn), (6, dg)):
                copy(t, k, (*chip, 1 - c), me).wait_recv()
        for cp in sent:
            cp.wait_send()
        for cp in mine:
            cp.wait()

    return _sequencer_call(
        body, name, collective_id,
        out_type=[jax.ShapeDtypeStruct((N_DEV,) + s.shape, s.dtype) for s in shards],
        scratch_types=[pltpu.SemaphoreType.DMA((nt, 7)), pltpu.SemaphoreType.DMA((nt, 7)),
                       pltpu.SemaphoreType.DMA((nt,))],
        operands=shards)


def _scatter_copy(srcs, lands, send_sems, recv_sems, t, k):
    peer = _peer(k)
    return pltpu.make_async_remote_copy(
        src_ref=srcs[t].at[_slot(*peer)], dst_ref=lands[t].at[k],
        send_sem=send_sems.at[t * (N_DEV - 1) + k - 1], recv_sem=recv_sems.at[t * (N_DEV - 1) + k - 1],
        device_id=peer, device_id_type=MESH)


def _scatter_start(partials, name):
    nt = len(partials)

    def body(*refs):
        srcs, lands = refs[:nt], refs[nt:2 * nt]
        send_sems, recv_sems = refs[2 * nt], refs[2 * nt + 1]
        token = refs[-1]
        for k in range(1, N_DEV):
            for t in range(nt):
                _scatter_copy(srcs, lands, send_sems, recv_sems, t, k).start()
        token[...] = jnp.zeros_like(token)

    hbm = pl.BlockSpec(memory_space=pltpu.HBM)
    sem = pl.BlockSpec(memory_space=pltpu.SEMAPHORE)
    shapes = [pltpu.HBM(p.shape, p.dtype) for p in partials]
    lands = [pltpu.with_memory_space_constraint(lax.empty(p.shape, p.dtype), pltpu.HBM) for p in partials]
    srcs = [pltpu.with_memory_space_constraint(p, pltpu.HBM) for p in partials]
    out = _pcall(
        body, name=name,
        out_shape=[pltpu.SemaphoreType.DMA((nt * (N_DEV - 1),))] * 2 + shapes + shapes
        + [jax.ShapeDtypeStruct((8, 128), F32)],
        in_specs=[hbm] * (2 * nt),
        out_specs=[sem, sem] + [hbm] * (2 * nt) + [pl.BlockSpec(memory_space=pltpu.VMEM)],
        input_output_aliases={i: 2 + i for i in range(2 * nt)},
        compiler_params=pltpu.CompilerParams(has_side_effects=pltpu.SideEffectType.DATAFLOW_SIDE_EFFECTING),
    )(*srcs, *lands)
    return (nt, name, out[:-1]), out[-1]


def _scatter_wait(state, after):
    nt, name, (send_sems, recv_sems, *thru) = state

    def body(*refs):
        srcs, lands = refs[:nt], refs[nt:2 * nt]
        send_sems, recv_sems = refs[2 * nt], refs[2 * nt + 1]
        for k in range(1, N_DEV):
            for t in range(nt):
                copy = _scatter_copy(srcs, lands, send_sems, recv_sems, t, k)
                copy.wait_send()
                copy.wait_recv()

    hbm = pl.BlockSpec(memory_space=pltpu.HBM)
    sem = pl.BlockSpec(memory_space=pltpu.SEMAPHORE)
    out = _pcall(
        body, name=name + "_wait",
        out_shape=[pltpu.HBM(a.shape, a.dtype) for a in thru],
        in_specs=[hbm] * (2 * nt) + [sem, sem, pl.BlockSpec(memory_space=pl.ANY)],
        out_specs=[hbm] * (2 * nt),
        input_output_aliases={i: i for i in range(2 * nt)},
        compiler_params=pltpu.CompilerParams(has_side_effects=pltpu.SideEffectType.DATAFLOW_SIDE_EFFECTING),
    )(*thru, send_sems, recv_sems, after)
    return out[:nt], out[nt:]


def _all_reduce_rows(v, dep):
    nv, _, w = v.shape

    def body(v_ref, dep_ref, out_ref, mine_ref, gath_ref, send_sems, recv_sems):
        x, y, c = _position()
        me = _slot(x, y, c)
        mine_ref[...] = jnp.sum(v_ref[...], axis=1)

        def copy(k):
            return pltpu.make_async_remote_copy(
                src_ref=mine_ref, dst_ref=gath_ref.at[me],
                send_sem=send_sems.at[k - 1], recv_sem=recv_sems.at[k - 1],
                device_id=_peer(k), device_id_type=MESH)

        def arrival(k):
            return pltpu.make_async_remote_copy(
                src_ref=mine_ref, dst_ref=gath_ref.at[_slot(*_peer(k))],
                send_sem=send_sems.at[k - 1], recv_sem=recv_sems.at[k - 1],
                device_id=_peer(k), device_id_type=MESH)

        sent = [copy(k) for k in range(1, N_DEV)]
        for cp in sent:
            cp.start()
        gath_ref[me] = mine_ref[...]
        for k in range(1, N_DEV):
            arrival(k).wait_recv()
        for cp in sent:
            cp.wait_send()
        total = gath_ref[0]
        for s in range(1, N_DEV):
            total = total + gath_ref[s]
        out_ref[...] = total

    vmem = pl.BlockSpec(memory_space=pltpu.VMEM)
    return _pcall(
        body, name="all_reduce_rows",
        in_specs=[vmem, pl.BlockSpec(memory_space=pl.ANY)], out_specs=vmem,
        out_shape=jax.ShapeDtypeStruct((nv, w), F32),
        scratch_shapes=[pltpu.VMEM((nv, w), F32), pltpu.VMEM((N_DEV, nv, w), F32),
                        pltpu.SemaphoreType.DMA((7,)), pltpu.SemaphoreType.DMA((7,))],
    )(v, dep)


def _adamw_math(w, g, m, v):
    m2 = ADAM_B1 * m + (1.0 - ADAM_B1) * g
    v2 = ADAM_B2 * v + (1.0 - ADAM_B2) * (g * g)
    m_hat = m2 / (1.0 - ADAM_B1 ** ADAM_STEP)
    v_hat = v2 / (1.0 - ADAM_B2 ** ADAM_STEP)
    delta = -ADAM_LR * (m_hat / (jnp.sqrt(v_hat) + ADAM_EPS) + ADAM_WD * w)
    return delta, m2, v2


def _row_block(r):
    for cand in (256, 176, 128):
        if r % cand == 0:
            return cand
    return r


def _adamw_sharded(me, grads, w, m, v, dep, first_layer=0, prev=None):
    nl = len(grads)
    _, r, c = grads[0][1].shape
    tr = _row_block(r)
    nr = r // tr
    prev = list(prev or ())

    def body(me_ref, *refs):
        grad_refs = refs[:2 * nl]
        w_ref, m_ref, v_ref = refs[2 * nl:2 * nl + 3]
        g_ref, d_ref, m2_ref, v2_ref = refs[-4:]
        layer = pl.program_id(0)

        def total(own_ref, land_ref):
            acc = own_ref[0].astype(F32)
            for k in range(1, N_DEV):
                acc = acc + land_ref[k].astype(F32)
            return acc

        g = total(grad_refs[0], grad_refs[1])
        for k in range(1, nl):
            g = jnp.where(layer == k, total(grad_refs[2 * k], grad_refs[2 * k + 1]), g)
        delta, m2, v2 = _adamw_math(w_ref[0], g, m_ref[0], v_ref[0])
        g_ref[0] = g
        d_ref[0] = delta
        m2_ref[0] = m2
        v2_ref[0] = v2

    def grad_pair_specs(k):
        def rows(l, i):
            return jnp.where(l == k, i, jnp.where(l < k, 0, nr - 1))
        return [pl.BlockSpec((1, tr, c), lambda l, i, me_ref: (me_ref[0], rows(l, i), 0)),
                pl.BlockSpec((N_DEV, tr, c), lambda l, i, me_ref: (0, rows(l, i), 0))]

    grad_specs = [spec for k in range(nl) for spec in grad_pair_specs(k)]
    shard = pl.BlockSpec((1, tr, c), lambda l, i, me_ref: (first_layer + l, i, 0))
    untouched = pl.BlockSpec(memory_space=pl.ANY)
    out = jax.ShapeDtypeStruct(w.shape, F32)
    first_prev = 1 + 2 * nl + 4
    return _pcall(
        body, name="adamw_sharded",
        grid_spec=pltpu.PrefetchScalarGridSpec(
            num_scalar_prefetch=1, grid=(nl, nr),
            in_specs=grad_specs + [shard, shard, shard] + [untouched] * (1 + len(prev)),
            out_specs=[shard, shard, shard, shard]),
        out_shape=[out, out, out, out],
        input_output_aliases={first_prev + k: k for k in range(len(prev))},
        compiler_params=_params(("arbitrary", "arbitrary")),
    )(me, *[a for pair in grads for a in pair], w, m, v, dep, *prev)


def _adamw_small(w, g, m, v):
    def body(w_ref, g_ref, m_ref, v_ref, d_ref, m2_ref, v2_ref):
        delta, m2, v2 = _adamw_math(w_ref[...], g_ref[...], m_ref[...], v_ref[...])
        d_ref[...] = delta
        m2_ref[...] = m2
        v2_ref[...] = v2

    spec = pl.BlockSpec(w.shape, lambda i: (0, 0))
    out = jax.ShapeDtypeStruct(w.shape, F32)
    return _pcall(
        body, name="adamw_small", grid=(1,),
        in_specs=[spec] * 4, out_specs=[spec] * 3, out_shape=[out] * 3,
        compiler_params=_params(("arbitrary",)),
    )(w, g, m, v)


def _pack(arrs):
    flat = jnp.concatenate([a.reshape(-1) for a in arrs])
    n = flat.shape[0]
    rows = -(-n // 1024) * 8
    return jnp.pad(flat, (0, rows * 128 - n)).reshape(rows, 128)


def _unpack(packed, like):
    flat = packed.reshape(-1)
    out, off = [], 0
    for a in like:
        out.append(flat[off:off + a.size].reshape(a.shape))
        off += a.size
    return out


def kernel(x, mem, g_ffn1, w_ffn1_up, w_ffn1_down, g_mix, w_in, conv_w, sinks, g_mem, w_mem_kv, g_grp, w_out, g_ffn2, w_ffn2_up, w_ffn2_down, g_final, loss_target, m_g_ffn1, m_w_ffn1_up, m_w_ffn1_down, m_g_mix, m_w_in, m_conv_w, m_sinks, m_g_mem, m_w_mem_kv, m_g_grp, m_w_out, m_g_ffn2, m_w_ffn2_up, m_w_ffn2_down, m_g_final, v_g_ffn1, v_w_ffn1_up, v_w_ffn1_down, v_g_mix, v_w_in, v_conv_w, v_sinks, v_g_mem, v_w_mem_kv, v_g_grp, v_w_out, v_g_ffn2, v_w_ffn2_up, v_w_ffn2_down, v_g_final):
    depth = g_ffn1.shape[0]
    t, d = x.shape[1], x.shape[2]
    width = max(d, D_MIX)
    me = _slot(*_position())
    conv_shard = conv_w.shape[2]

    xin, memin, tgt = x[0], mem[0], loss_target[0]

    conv_tile = jnp.zeros((depth * 8, 128), F32).at[:, :conv_shard].set(
        jnp.pad(conv_w, ((0, 0), (0, 8 - conv_w.shape[1]), (0, 0))).reshape(depth * 8, conv_shard))
    tr = lambda a: jnp.swapaxes(a, -1, -2)
    bf = lambda a: a.astype(BF16)
    weights = []
    collective_id = 0
    for l in range(depth):
        groups = [[bf(tr(w_ffn1_up[l])), bf(w_ffn1_down[l])] + ([conv_tile] if l == 0 else []),
                  [bf(tr(w_in[l])), bf(w_mem_kv[l]), bf(w_out[l])],
                  [bf(tr(w_ffn2_up[l])), bf(w_ffn2_down[l])]]
        full = []
        for gi, shards in enumerate(groups):
            full.append(_all_gather(shards, f"all_gather_l{l}_g{gi}", collective_id))
            collective_id += 1
        if l == 0:
            conv_full = full[0][2].reshape(N_DEV, depth, 8, 128)[:, :, :3, :conv_shard]
            conv_full = conv_full.transpose(1, 2, 0, 3).reshape(depth, 3, N_DEV * conv_shard)
        weights.append(dict(
            up1=full[0][0].reshape(2, -1, d), dn1=full[0][1].reshape(-1, d),
            win=full[1][0].reshape(D_IN, d), wkv=full[1][1].reshape(d, 2 * D_MEMQ), wout=full[1][2].reshape(D_MIX, d),
            up2=full[2][0].reshape(2, -1, d), dn2=full[2][1].reshape(-1, d)))

    row = lambda a: a.reshape(1, -1)
    bias_key = _bias_table()

    h = xin
    saved = []
    for l in range(depth):
        wl = weights[l]
        s = dict(h0=h)
        h, s["gu1"], s["n1"] = _ffn_fwd(h, row(g_ffn1[l]), wl["up1"], wl["dn1"])
        s["h1"] = h
        s["p"], s["n_mix"], s["qh"] = _mix_proj_fwd(h, row(g_mix[l]), wl["win"])
        s["mkv"], s["nt_mem"] = _memkv_fwd(memin, row(g_mem[l]), wl["wkv"], s["p"])
        s["y"], s["lse"] = _mix_core_fwd(s["p"], s["qh"], s["mkv"], conv_full[l], row(sinks[l]), bias_key)
        h, s["mt"] = _mix_out_fwd(s["y"], h, row(g_grp[l]), wl["wout"])
        s["h2"] = h
        h, s["gu2"], s["n2"] = _ffn_fwd(h, row(g_ffn2[l]), wl["up2"], wl["dn2"])
        saved.append(s)

    dh, loss_part, dg_final = _final_loss(h, row(g_final), tgt)

    small = {}
    dep = loss_part

    def reduce_small(after):
        def lanes(a):
            return jnp.pad(a, ((0, 0), (0, width - a.shape[1])))

        def first_row(a):
            return lanes(jnp.pad(a, ((0, 8 - a.shape[0]), (0, 0))))

        vec_names = ["g_ffn1", "g_mix", "g_mem", "g_grp", "g_ffn2", "sinks"]
        tiles = [lanes(small[n, l]) for n in vec_names for l in range(depth)]
        tiles += [first_row(small["conv_w", l][k:k + 1]) for l in range(depth) for k in range(3)]
        tiles.append(lanes(dg_final))
        n_real = len(tiles)
        tiles.append(lanes(loss_part))
        tiles += [jnp.zeros((8, width), F32)] * (-len(tiles) % 8)
        summed = _all_reduce_rows(jnp.stack(tiles), after)
        loss_all = 0.5 * jnp.sum(summed[n_real]) / d

        def vec(n, wd):
            return jnp.stack([summed[vec_names.index(n) * depth + l, :wd] for l in range(depth)])

        conv_base = len(vec_names) * depth
        conv_grad = jnp.stack([jnp.stack([summed[conv_base + 3 * l + k, :D_CONV] for k in range(3)])
                               for l in range(depth)])
        grads_small = {
            "g_ffn1": vec("g_ffn1", d), "g_mix": vec("g_mix", d), "g_mem": vec("g_mem", d),
            "g_grp": vec("g_grp", D_MIX), "g_ffn2": vec("g_ffn2", d), "sinks": vec("sinks", N_SWA_HEADS),
            "conv_w": lax.dynamic_slice_in_dim(conv_grad, me * conv_shard, conv_shard, axis=2),
            "g_final": summed[n_real - 1, :d],
        }
        small_w = [("g_ffn1", g_ffn1, m_g_ffn1, v_g_ffn1), ("g_mix", g_mix, m_g_mix, v_g_mix),
                   ("conv_w", conv_w, m_conv_w, v_conv_w), ("sinks", sinks, m_sinks, v_sinks),
                   ("g_mem", g_mem, m_g_mem, v_g_mem), ("g_grp", g_grp, m_g_grp, v_g_grp),
                   ("g_ffn2", g_ffn2, m_g_ffn2, v_g_ffn2), ("g_final", g_final, m_g_final, v_g_final)]
        like = [w for _, w, _, _ in small_w]
        packed = _adamw_small(_pack(like), _pack([grads_small[n] for n, _, _, _ in small_w]),
                              _pack([m for _, _, m, _ in small_w]), _pack([v for _, _, _, v in small_w]))
        updated = {n: (grads_small[n], dl, m2, v2)
                   for (n, _, _, _), dl, m2, v2 in zip(small_w, *[_unpack(pk, like) for pk in packed])}
        return loss_all, updated, packed[0]

    started = []

    def scatter(names, partials, label):
        state, token = _scatter_start(partials, f"scatter_grads_{label}")
        started.append((names, state))
        return token

    for l in reversed(range(depth)):
        wl, s = weights[l], saved[l]
        dh, agu, dyb, small["g_ffn2", l] = _ffn_bwd_act(dh, s["h2"], row(g_ffn2[l]), s["gu2"], wl["up2"], wl["dn2"], dep)
        ddn2 = _ffn_bwd_w(agu, 2, 1, dyb, agu, f"ffn_bwd_w_down_l{l}_ffn2").reshape(N_DEV, -1, d)
        dup2 = _ffn_bwd_w(agu, 0, 2, s["n2"], ddn2, f"ffn_bwd_w_up_l{l}_ffn2").reshape(N_DEV, -1, d)
        dep = scatter([("w_ffn2_up", l), ("w_ffn2_down", l)], [dup2, ddn2], f"l{l}_ffn2")
        dyconv, doh, delta, dwout, small["g_grp", l] = _mix_out_bwd(dh, s["y"], row(g_grp[l]), wl["wout"], s["mt"], dep)
        dp, dmkv, small["conv_w", l], small["sinks", l] = _mix_core_bwd(
            s["p"], s["qh"], dyconv, doh, delta, s["lse"], s["mkv"], conv_full[l], row(sinks[l]), bias_key)
        dwkv, small["g_mem", l] = _memkv_bwd(dmkv, memin, row(g_mem[l]), wl["wkv"], s["nt_mem"])
        dh, dwin, small["g_mix", l] = _mix_proj_bwd(dp, dh, s["h1"], row(g_mix[l]), wl["win"], s["n_mix"])
        dep = scatter([("w_in", l), ("w_mem_kv", l), ("w_out", l)],
                      [dwin.reshape(N_DEV, -1, d), dwkv.reshape(N_DEV, -1, 2 * D_MEMQ), dwout.reshape(N_DEV, -1, d)],
                      f"l{l}_mix")
        dh, agu, dyb, small["g_ffn1", l] = _ffn_bwd_act(dh, s["h0"], row(g_ffn1[l]), s["gu1"], wl["up1"], wl["dn1"], dep)
        ddn1 = _ffn_bwd_w(agu, 2, 1, dyb, agu, f"ffn_bwd_w_down_l{l}_ffn1").reshape(N_DEV, -1, d)
        if l > 0:
            dup1 = _ffn_bwd_w(agu, 0, 2, s["n1"], ddn1, f"ffn_bwd_w_up_l{l}_ffn1").reshape(N_DEV, -1, d)
            dep = scatter([("w_ffn1_up", l), ("w_ffn1_down", l)], [dup1, ddn1], f"l{l}_ffn1")
        else:
            dep = scatter([("w_ffn1_down", l)], [ddn1], f"l{l}_ffn1_down")
            dup1 = _ffn_bwd_w(agu, 0, 2, s["n1"], dep, f"ffn_bwd_w_up_l{l}_ffn1").reshape(N_DEV, -1, d)
            dep = scatter([("w_ffn1_up", l)], [dup1], f"l{l}_ffn1_up")
    grad_x = dh[None]

    big = {"w_ffn2_up": (w_ffn2_up, m_w_ffn2_up, v_w_ffn2_up, True), "w_ffn2_down": (w_ffn2_down, m_w_ffn2_down, v_w_ffn2_down, False),
           "w_in": (w_in, m_w_in, v_w_in, True), "w_mem_kv": (w_mem_kv, m_w_mem_kv, v_w_mem_kv, False),
           "w_out": (w_out, m_w_out, v_w_out, False), "w_ffn1_up": (w_ffn1_up, m_w_ffn1_up, v_w_ffn1_up, True),
           "w_ffn1_down": (w_ffn1_down, m_w_ffn1_down, v_w_ffn1_down, False)}
    me_index = jnp.reshape(me, (1,)).astype(jnp.int32)
    sharded, landed, begun = {}, {}, {}
    by_layer = {name for name, _ in started[-1][0]}

    def finish(groups, after):
        for names, state in groups:
            owns, lands = _scatter_wait(state, after)
            for key, own, land in zip(names, owns, lands):
                landed[key] = (own, land)
            after = lands[0]
            for name, l in names:
                w, m, v, transposed = big[name]
                fix = tr if transposed else (lambda a: a)
                if name in by_layer:
                    res = _adamw_sharded(me_index, [landed[name, l]], fix(w), fix(m), fix(v), after, l, begun.get(name))
                    done = name in begun
                    begun[name] = res
                elif all((name, k) in landed for k in range(depth)):
                    res = _adamw_sharded(me_index, [landed[name, k] for k in range(depth)], fix(w), fix(m), fix(v), after)
                    done = True
                else:
                    continue
                if done:
                    sharded[name] = tuple(fix(r) for r in res)
                after = res[0]
        return after

    loss, small_out, dep = reduce_small(finish(started[:-1], dep))
    finish(started[-1:], dep)

    order = ["g_ffn1", "w_ffn1_up", "w_ffn1_down", "g_mix", "w_in", "conv_w", "sinks", "g_mem", "w_mem_kv", "g_grp",
             "w_out", "g_ffn2", "w_ffn2_up", "w_ffn2_down", "g_final"]
    results = {**sharded, **small_out}
    outs = [loss, grad_x]
    for part in range(4):
        outs += [results[n][part] for n in order]
    return tuple(outs)
```
